```python
import math
import jax, jax.numpy as jnp
from jax import lax
import numpy as np

D_MODEL = 1024
BATCH = 8
SEQ = 4096
DEPTH = 2

N_MOD = 6
EPS = 1e-6
SSM_EXPAND = 2
SSM_D_INNER = SSM_EXPAND * D_MODEL
SSM_HEAD_DIM = 64
SSM_HEADS = SSM_D_INNER // SSM_HEAD_DIM
SSM_GROUPS = 4
SSM_STATE = 128
SSM_CONV = 4
SSM_CHUNK = 128
SSM_CONV_DIM = SSM_D_INNER + 2 * SSM_GROUPS * SSM_STATE
ATT_HEAD_DIM = 64
ATT_KV_HEADS = D_MODEL // ATT_HEAD_DIM
ATT_PATTERNS = ((128, 1), (512, 4), (2048, 16))
ATT_GROUPS = len(ATT_PATTERNS)
ATT_Q_HEADS = ATT_GROUPS * ATT_KV_HEADS
ATT_WIDTH = ATT_KV_HEADS * ATT_HEAD_DIM
REL_BUCKETS = 32
REL_MAX_DIST = 2048
HY_SIZES = (SSM_D_INNER, SSM_CONV_DIM, SSM_HEADS, ATT_Q_HEADS * ATT_HEAD_DIM, ATT_WIDTH, ATT_WIDTH)
HY_SPLITS = tuple(int(v) for v in np.cumsum(HY_SIZES)[:-1])
HY_IN_DIM = sum(HY_SIZES)
HY_OUT_DIM = SSM_D_INNER + ATT_WIDTH
CONV_WIDTH = 31
FFN_HIDDEN = -(-8 * D_MODEL // (3 * 256)) * 256
N_EVEN = (DEPTH + 1) // 2
N_ODD = DEPTH // 2

kernel_name = "hybrid_ssd_dilated_conformer_block"


def rms_norm(x, g):
    xf = x.astype(jnp.float32)
    y = xf * lax.rsqrt(jnp.mean(xf * xf, -1, keepdims=True) + EPS)
    return (y * g.astype(jnp.float32)).astype(x.dtype)


def layer_norm(x, g, b):
    xf = x.astype(jnp.float32)
    mu = jnp.mean(xf, -1, keepdims=True)
    var = jnp.mean(jnp.square(xf - mu), -1, keepdims=True)
    y = (xf - mu) * lax.rsqrt(var + EPS)
    return (y * g.astype(jnp.float32) + b.astype(jnp.float32)).astype(x.dtype)


def causal_depthwise_conv(x, w, b):
    k = w.shape[0]
    out = lax.conv_general_dilated(
        x, w[:, None, :].astype(x.dtype), window_strides=(1,), padding=[(k - 1, 0)],
        dimension_numbers=('NWC', 'WIO', 'NWC'), feature_group_count=x.shape[-1])
    return out + b


def t5_bucket(dist):
    max_exact = REL_BUCKETS // 2
    n = jnp.maximum(dist, 1).astype(jnp.float32)
    large = max_exact + jnp.log(n / max_exact) / math.log(REL_MAX_DIST / max_exact) * (REL_BUCKETS - max_exact)
    large = jnp.minimum(large.astype(jnp.int32), REL_BUCKETS - 1)
    return jnp.where(dist < max_exact, dist, large)


def ssd_chunked(x, dt, A, Bm, Cm):
    f32 = jnp.float32
    b_, s, h, p = x.shape
    g, n = Bm.shape[2], Bm.shape[3]
    r, q = h // g, SSM_CHUNK
    nc = s // q
    x = x.astype(f32).reshape(b_, nc, q, g, r, p)
    dt = dt.astype(f32).reshape(b_, nc, q, g, r)
    Bm = Bm.astype(f32).reshape(b_, nc, q, g, n)
    Cm = Cm.astype(f32).reshape(b_, nc, q, g, n)
    a_cs = jnp.cumsum(dt * A.astype(f32).reshape(g, r), axis=2)
    xdt = x * dt[..., None]
    seg = a_cs[:, :, :, None] - a_cs[:, :, None, :]
    causal = jnp.tril(jnp.ones((q, q), bool))[:, :, None, None]
    decay = jnp.exp(jnp.where(causal, seg, -jnp.inf))
    cb = jnp.einsum('bclgn,bcsgn->bclsg', Cm, Bm)
    y_diag = jnp.einsum('bclsgr,bcsgrp->bclgrp', cb[..., None] * decay, xdt)
    decay_end = jnp.exp(a_cs[:, :, -1:] - a_cs)
    states = jnp.einsum('bcsgn,bcsgrp->bcgrpn', Bm, xdt * decay_end[..., None])
    chunk_decay = jnp.exp(a_cs[:, :, -1])

    def step(hs, inp):
        st, dec = inp
        return dec[..., None, None] * hs + st, hs

    h0 = jnp.zeros((b_, g, r, p, n), f32)
    _, prev = lax.scan(step, h0, (jnp.moveaxis(states, 1, 0), jnp.moveaxis(chunk_decay, 1, 0)))
    prev = jnp.moveaxis(prev, 0, 1)
    y_off = jnp.einsum('bclgn,bcgrpn->bclgrp', Cm, prev) * jnp.exp(a_cs)[..., None]
    return (y_diag + y_off).reshape(b_, s, h, p)


def dilated_branch(q, k, v, bias_tab, window, dil):
    b_, s, h, dh = q.shape
    blk = window // dil
    L = s // dil
    nb = -(-L // blk)
    lp = nb * blk

    def blocks(t):
        t = t.reshape(b_, L, dil, h, dh)
        t = jnp.pad(t, ((0, 0), (0, lp - L), (0, 0), (0, 0), (0, 0)))
        return t.reshape(b_, nb, blk, dil, h, dh)

    def band_keys(t):
        prev = jnp.pad(t, ((0, 0), (1, 0), (0, 0), (0, 0), (0, 0), (0, 0)))[:, :-1]
        return jnp.concatenate([prev, t], axis=2)

    qb = blocks(q)
    kw, vw = band_keys(blocks(k)), band_keys(blocks(v))
    i = jnp.arange(blk)[:, None]
    j = jnp.arange(2 * blk)[None, :]
    delta = blk + i - j
    band = (delta >= 0) & (delta <= blk)
    kpos = jnp.arange(nb)[:, None] * blk + jnp.arange(2 * blk)[None, :] - blk
    mask = band[None] & (kpos >= 0)[:, None, :]
    bias = jnp.transpose(bias_tab[t5_bucket(jnp.maximum(delta, 0) * dil)], (2, 0, 1)).astype(jnp.float32)
    sc = jnp.einsum('bnirhd,bnjrhd->bnrhij', qb, kw).astype(jnp.float32) * (dh ** -0.5) + bias
    sc = jnp.where(mask[None, :, None, None], sc, -jnp.inf)
    m = jnp.max(sc, -1, keepdims=True)
    pr = jnp.exp(sc - m)
    l = jnp.sum(pr, -1, keepdims=True)
    o = jnp.einsum('bnrhij,bnjrhd->bnirhd', (pr / l).astype(v.dtype), vw)
    lse = jnp.transpose((m + jnp.log(l))[..., 0], (0, 1, 4, 2, 3))
    lse = lse.reshape(b_, lp, dil, h)[:, :L].reshape(b_, s, h)
    o = o.reshape(b_, lp, dil, h, dh)[:, :L].reshape(b_, s, h, dh)
    return o, lse


def dilated_attention(q, k, v, rel_table):
    b_, s, _, h, dh = q.shape
    outs, lses = [], []
    for gi, (w, d) in enumerate(ATT_PATTERNS):
        o, lse = dilated_branch(q[:, :, gi], k, v, rel_table[:, gi * h:(gi + 1) * h], w, d)
        outs.append(o)
        lses.append(lse)
    wgt = jax.nn.softmax(jnp.stack(lses, 0), axis=0)
    o = jnp.einsum('gbsh,gbshd->bshd', wgt.astype(outs[0].dtype), jnp.stack(outs, 0))
    return o.reshape(b_, s, h * dh)


def hybrid_mixer(h, w_in, conv_w, conv_b, dt_bias, a_log, d_skip, ssm_norm_g, w_out, rel_table):
    b_, s, _ = h.shape
    z, xbc, dt_raw, q, k, v = jnp.split(h @ w_in, HY_SPLITS, axis=-1)
    xbc = jax.nn.silu(causal_depthwise_conv(xbc, conv_w, conv_b))
    xs, bm, cm = jnp.split(xbc, (SSM_D_INNER, SSM_D_INNER + SSM_GROUPS * SSM_STATE), axis=-1)
    xs = xs.reshape(b_, s, SSM_HEADS, SSM_HEAD_DIM)
    dt = jax.nn.softplus((dt_raw + dt_bias).astype(jnp.float32))
    A = -jnp.exp(a_log.astype(jnp.float32))
    y = ssd_chunked(xs, dt, A, bm.reshape(b_, s, SSM_GROUPS, SSM_STATE), cm.reshape(b_, s, SSM_GROUPS, SSM_STATE))
    y = y + d_skip.astype(jnp.float32)[:, None] * xs.astype(jnp.float32)
    y = y.reshape(b_, s, SSM_D_INNER).astype(h.dtype)
    y = rms_norm(y * jax.nn.silu(z), ssm_norm_g)
    att = dilated_attention(q.reshape(b_, s, ATT_GROUPS, ATT_KV_HEADS, ATT_HEAD_DIM),
                            k.reshape(b_, s, ATT_KV_HEADS, ATT_HEAD_DIM),
                            v.reshape(b_, s, ATT_KV_HEADS, ATT_HEAD_DIM), rel_table)
    return jnp.concatenate([y, att.astype(y.dtype)], axis=-1) @ w_out


def conformer_conv(h, w1, b1, w_dw, b_dw, ln_g, ln_b, w2, b2):
    a, gt = jnp.split(h @ w1 + b1, 2, axis=-1)
    u = a * jax.nn.sigmoid(gt)
    u = causal_depthwise_conv(u, w_dw, b_dw)
    u = jax.nn.silu(layer_norm(u, ln_g, ln_b))
    return u @ w2 + b2


def swiglu(h, wg, wu, wd):
    return (jax.nn.silu(h @ wg) * (h @ wu)) @ wd


def _fwd_setup_inputs(seed: int = 0) -> dict:
    key = jax.random.key(seed)
    ks = iter(jax.random.split(key, 40))
    f32 = jnp.float32

    def nrm(shape, scale=1.0):
        return jax.random.normal(next(ks), shape, f32) * scale

    D = D_MODEL
    dt0 = jnp.exp(jax.random.uniform(next(ks), (N_EVEN, SSM_HEADS), f32, math.log(1e-3), math.log(1e-1)))
    return {
        "x": nrm((BATCH, SEQ, D)),
        "c": nrm((BATCH, D)),
        "ada_w": nrm((DEPTH, D, N_MOD * D), 0.5 * D ** -0.5),
        "ada_b": nrm((DEPTH, N_MOD * D), 0.02),
        "norm_mix_g": 1.0 + nrm((DEPTH, D), 0.05),
        "norm_ffn_g": 1.0 + nrm((DEPTH, D), 0.05),
        "hy_w_in": nrm((N_EVEN, D, HY_IN_DIM), D ** -0.5),
        "hy_conv_w": nrm((N_EVEN, SSM_CONV, SSM_CONV_DIM), SSM_CONV ** -0.5),
        "hy_conv_b": nrm((N_EVEN, SSM_CONV_DIM), 0.02),
        "hy_dt_bias": dt0 + jnp.log(-jnp.expm1(-dt0)),
        "hy_a_log": jnp.log(jax.random.uniform(next(ks), (N_EVEN, SSM_HEADS), f32, 1.0, 16.0)),
        "hy_d_skip": 1.0 + nrm((N_EVEN, SSM_HEADS), 0.1),
        "hy_ssm_norm_g": 1.0 + nrm((N_EVEN, SSM_D_INNER), 0.05),
        "hy_w_out": nrm((N_EVEN, HY_OUT_DIM, D), HY_OUT_DIM ** -0.5),
        "rel_table": nrm((REL_BUCKETS, ATT_Q_HEADS), 0.2),
        "cv_w_pw1": nrm((N_ODD, D, 2 * D), D ** -0.5),
        "cv_b_pw1": nrm((N_ODD, 2 * D), 0.02),
        "cv_w_dw": nrm((N_ODD, CONV_WIDTH, D), CONV_WIDTH ** -0.5),
        "cv_b_dw": nrm((N_ODD, D), 0.02),
        "cv_ln_g": 1.0 + nrm((N_ODD, D), 0.05),
        "cv_ln_b": nrm((N_ODD, D), 0.02),
        "cv_w_pw2": nrm((N_ODD, D, D), D ** -0.5),
        "cv_b_pw2": nrm((N_ODD, D), 0.02),
        "ffn_w_gate": nrm((DEPTH, D, FFN_HIDDEN), D ** -0.5),
        "ffn_w_up": nrm((DEPTH, D, FFN_HIDDEN), D ** -0.5),
        "ffn_w_down": nrm((DEPTH, FFN_HIDDEN, D), FFN_HIDDEN ** -0.5),
        "final_norm_g": 1.0 + nrm((D,), 0.05),
    }


def _fwd_reference(x, c, ada_w, ada_b, norm_mix_g, norm_ffn_g, hy_w_in, hy_conv_w, hy_conv_b, hy_dt_bias,
              hy_a_log, hy_d_skip, hy_ssm_norm_g, hy_w_out, rel_table, cv_w_pw1, cv_b_pw1, cv_w_dw,
              cv_b_dw, cv_ln_g, cv_ln_b, cv_w_pw2, cv_b_pw2, ffn_w_gate, ffn_w_up, ffn_w_down,
              final_norm_g):
    cs = jax.nn.silu(c)
    for i in range(DEPTH):
        mod = cs @ ada_w[i] + ada_b[i]
        sh1, sc1, g1, sh2, sc2, g2 = [m[:, None, :] for m in jnp.split(mod, N_MOD, axis=-1)]
        h = rms_norm(x, norm_mix_g[i]) * (1 + sc1) + sh1
        j = i // 2
        if i % 2 == 0:
            mix = hybrid_mixer(h, hy_w_in[j], hy_conv_w[j], hy_conv_b[j], hy_dt_bias[j], hy_a_log[j],
                               hy_d_skip[j], hy_ssm_norm_g[j], hy_w_out[j], rel_table)
        else:
            mix = conformer_conv(h, cv_w_pw1[j], cv_b_pw1[j], cv_w_dw[j], cv_b_dw[j], cv_ln_g[j],
                                 cv_ln_b[j], cv_w_pw2[j], cv_b_pw2[j])
        x = x + g1 * mix
        h = rms_norm(x, norm_ffn_g[i]) * (1 + sc2) + sh2
        x = x + g2 * swiglu(h, ffn_w_gate[i], ffn_w_up[i], ffn_w_down[i])
    return rms_norm(x, final_norm_g)


import jax as _jax
import jax.numpy as _jnp

TWIN_FORMAT = 'train_step'
FWD_PARAMS = ['x', 'c', 'ada_w', 'ada_b', 'norm_mix_g', 'norm_ffn_g', 'hy_w_in', 'hy_conv_w', 'hy_conv_b', 'hy_dt_bias', 'hy_a_log', 'hy_d_skip', 'hy_ssm_norm_g', 'hy_w_out', 'rel_table', 'cv_w_pw1', 'cv_b_pw1', 'cv_w_dw', 'cv_b_dw', 'cv_ln_g', 'cv_ln_b', 'cv_w_pw2', 'cv_b_pw2', 'ffn_w_gate', 'ffn_w_up', 'ffn_w_down', 'final_norm_g']
TWIN_WEIGHTS = ['ada_w', 'ada_b', 'norm_mix_g', 'norm_ffn_g', 'hy_w_in', 'hy_conv_w', 'hy_conv_b', 'hy_dt_bias', 'hy_a_log', 'hy_d_skip', 'hy_ssm_norm_g', 'hy_w_out', 'rel_table', 'cv_w_pw1', 'cv_b_pw1', 'cv_w_dw', 'cv_b_dw', 'cv_ln_g', 'cv_ln_b', 'cv_w_pw2', 'cv_b_pw2', 'ffn_w_gate', 'ffn_w_up', 'ffn_w_down', 'final_norm_g']
TWIN_DIFF_INPUT = 'x'
TWIN_INPUTS = ['x', 'c', 'ada_w', 'ada_b', 'norm_mix_g', 'norm_ffn_g', 'hy_w_in', 'hy_conv_w', 'hy_conv_b', 'hy_dt_bias', 'hy_a_log', 'hy_d_skip', 'hy_ssm_norm_g', 'hy_w_out', 'rel_table', 'cv_w_pw1', 'cv_b_pw1', 'cv_w_dw', 'cv_b_dw', 'cv_ln_g', 'cv_ln_b', 'cv_w_pw2', 'cv_b_pw2', 'ffn_w_gate', 'ffn_w_up', 'ffn_w_down', 'final_norm_g', 'loss_target', 'm_ada_w', 'm_ada_b', 'm_norm_mix_g', 'm_norm_ffn_g', 'm_hy_w_in', 'm_hy_conv_w', 'm_hy_conv_b', 'm_hy_dt_bias', 'm_hy_a_log', 'm_hy_d_skip', 'm_hy_ssm_norm_g', 'm_hy_w_out', 'm_rel_table', 'm_cv_w_pw1', 'm_cv_b_pw1', 'm_cv_w_dw', 'm_cv_b_dw', 'm_cv_ln_g', 'm_cv_ln_b', 'm_cv_w_pw2', 'm_cv_b_pw2', 'm_ffn_w_gate', 'm_ffn_w_up', 'm_ffn_w_down', 'm_final_norm_g', 'v_ada_w', 'v_ada_b', 'v_norm_mix_g', 'v_norm_ffn_g', 'v_hy_w_in', 'v_hy_conv_w', 'v_hy_conv_b', 'v_hy_dt_bias', 'v_hy_a_log', 'v_hy_d_skip', 'v_hy_ssm_norm_g', 'v_hy_w_out', 'v_rel_table', 'v_cv_w_pw1', 'v_cv_b_pw1', 'v_cv_w_dw', 'v_cv_b_dw', 'v_cv_ln_g', 'v_cv_ln_b', 'v_cv_w_pw2', 'v_cv_b_pw2', 'v_ffn_w_gate', 'v_ffn_w_up', 'v_ffn_w_down', 'v_final_norm_g']
TWIN_OUTPUTS = ['loss', 'grad_x', 'grad_ada_w', 'grad_ada_b', 'grad_norm_mix_g', 'grad_norm_ffn_g', 'grad_hy_w_in', 'grad_hy_conv_w', 'grad_hy_conv_b', 'grad_hy_dt_bias', 'grad_hy_a_log', 'grad_hy_d_skip', 'grad_hy_ssm_norm_g', 'grad_hy_w_out', 'grad_rel_table', 'grad_cv_w_pw1', 'grad_cv_b_pw1', 'grad_cv_w_dw', 'grad_cv_b_dw', 'grad_cv_ln_g', 'grad_cv_ln_b', 'grad_cv_w_pw2', 'grad_cv_b_pw2', 'grad_ffn_w_gate', 'grad_ffn_w_up', 'grad_ffn_w_down', 'grad_final_norm_g', 'delta_ada_w', 'delta_ada_b', 'delta_norm_mix_g', 'delta_norm_ffn_g', 'delta_hy_w_in', 'delta_hy_conv_w', 'delta_hy_conv_b', 'delta_hy_dt_bias', 'delta_hy_a_log', 'delta_hy_d_skip', 'delta_hy_ssm_norm_g', 'delta_hy_w_out', 'delta_rel_table', 'delta_cv_w_pw1', 'delta_cv_b_pw1', 'delta_cv_w_dw', 'delta_cv_b_dw', 'delta_cv_ln_g', 'delta_cv_ln_b', 'delta_cv_w_pw2', 'delta_cv_b_pw2', 'delta_ffn_w_gate', 'delta_ffn_w_up', 'delta_ffn_w_down', 'delta_final_norm_g', 'new_m_ada_w', 'new_m_ada_b', 'new_m_norm_mix_g', 'new_m_norm_ffn_g', 'new_m_hy_w_in', 'new_m_hy_conv_w', 'new_m_hy_conv_b', 'new_m_hy_dt_bias', 'new_m_hy_a_log', 'new_m_hy_d_skip', 'new_m_hy_ssm_norm_g', 'new_m_hy_w_out', 'new_m_rel_table', 'new_m_cv_w_pw1', 'new_m_cv_b_pw1', 'new_m_cv_w_dw', 'new_m_cv_b_dw', 'new_m_cv_ln_g', 'new_m_cv_ln_b', 'new_m_cv_w_pw2', 'new_m_cv_b_pw2', 'new_m_ffn_w_gate', 'new_m_ffn_w_up', 'new_m_ffn_w_down', 'new_m_final_norm_g', 'new_v_ada_w', 'new_v_ada_b', 'new_v_norm_mix_g', 'new_v_norm_ffn_g', 'new_v_hy_w_in', 'new_v_hy_conv_w', 'new_v_hy_conv_b', 'new_v_hy_dt_bias', 'new_v_hy_a_log', 'new_v_hy_d_skip', 'new_v_hy_ssm_norm_g', 'new_v_hy_w_out', 'new_v_rel_table', 'new_v_cv_w_pw1', 'new_v_cv_b_pw1', 'new_v_cv_w_dw', 'new_v_cv_b_dw', 'new_v_cv_ln_g', 'new_v_cv_ln_b', 'new_v_cv_w_pw2', 'new_v_cv_b_pw2', 'new_v_ffn_w_gate', 'new_v_ffn_w_up', 'new_v_ffn_w_down', 'new_v_final_norm_g']
TWIN_LEAF_KINDS = {'loss': 'loss', 'grad_x': 'grad_x', 'grad_ada_w': 'grad_w', 'grad_ada_b': 'grad_w', 'grad_norm_mix_g': 'grad_w', 'grad_norm_ffn_g': 'grad_w', 'grad_hy_w_in': 'grad_w', 'grad_hy_conv_w': 'grad_w', 'grad_hy_conv_b': 'grad_w', 'grad_hy_dt_bias': 'grad_w', 'grad_hy_a_log': 'grad_w', 'grad_hy_d_skip': 'grad_w', 'grad_hy_ssm_norm_g': 'grad_w', 'grad_hy_w_out': 'grad_w', 'grad_rel_table': 'grad_w', 'grad_cv_w_pw1': 'grad_w', 'grad_cv_b_pw1': 'grad_w', 'grad_cv_w_dw': 'grad_w', 'grad_cv_b_dw': 'grad_w', 'grad_cv_ln_g': 'grad_w', 'grad_cv_ln_b': 'grad_w', 'grad_cv_w_pw2': 'grad_w', 'grad_cv_b_pw2': 'grad_w', 'grad_ffn_w_gate': 'grad_w', 'grad_ffn_w_up': 'grad_w', 'grad_ffn_w_down': 'grad_w', 'grad_final_norm_g': 'grad_w', 'delta_ada_w': 'delta_w', 'delta_ada_b': 'delta_w', 'delta_norm_mix_g': 'delta_w', 'delta_norm_ffn_g': 'delta_w', 'delta_hy_w_in': 'delta_w', 'delta_hy_conv_w': 'delta_w', 'delta_hy_conv_b': 'delta_w', 'delta_hy_dt_bias': 'delta_w', 'delta_hy_a_log': 'delta_w', 'delta_hy_d_skip': 'delta_w', 'delta_hy_ssm_norm_g': 'delta_w', 'delta_hy_w_out': 'delta_w', 'delta_rel_table': 'delta_w', 'delta_cv_w_pw1': 'delta_w', 'delta_cv_b_pw1': 'delta_w', 'delta_cv_w_dw': 'delta_w', 'delta_cv_b_dw': 'delta_w', 'delta_cv_ln_g': 'delta_w', 'delta_cv_ln_b': 'delta_w', 'delta_cv_w_pw2': 'delta_w', 'delta_cv_b_pw2': 'delta_w', 'delta_ffn_w_gate': 'delta_w', 'delta_ffn_w_up': 'delta_w', 'delta_ffn_w_down': 'delta_w', 'delta_final_norm_g': 'delta_w', 'new_m_ada_w': 'new_m', 'new_m_ada_b': 'new_m', 'new_m_norm_mix_g': 'new_m', 'new_m_norm_ffn_g': 'new_m', 'new_m_hy_w_in': 'new_m', 'new_m_hy_conv_w': 'new_m', 'new_m_hy_conv_b': 'new_m', 'new_m_hy_dt_bias': 'new_m', 'new_m_hy_a_log': 'new_m', 'new_m_hy_d_skip': 'new_m', 'new_m_hy_ssm_norm_g': 'new_m', 'new_m_hy_w_out': 'new_m', 'new_m_rel_table': 'new_m', 'new_m_cv_w_pw1': 'new_m', 'new_m_cv_b_pw1': 'new_m', 'new_m_cv_w_dw': 'new_m', 'new_m_cv_b_dw': 'new_m', 'new_m_cv_ln_g': 'new_m', 'new_m_cv_ln_b': 'new_m', 'new_m_cv_w_pw2': 'new_m', 'new_m_cv_b_pw2': 'new_m', 'new_m_ffn_w_gate': 'new_m', 'new_m_ffn_w_up': 'new_m', 'new_m_ffn_w_down': 'new_m', 'new_m_final_norm_g': 'new_m', 'new_v_ada_w': 'new_v', 'new_v_ada_b': 'new_v', 'new_v_norm_mix_g': 'new_v', 'new_v_norm_ffn_g': 'new_v', 'new_v_hy_w_in': 'new_v', 'new_v_hy_conv_w': 'new_v', 'new_v_hy_conv_b': 'new_v', 'new_v_hy_dt_bias': 'new_v', 'new_v_hy_a_log': 'new_v', 'new_v_hy_d_skip': 'new_v', 'new_v_hy_ssm_norm_g': 'new_v', 'new_v_hy_w_out': 'new_v', 'new_v_rel_table': 'new_v', 'new_v_cv_w_pw1': 'new_v', 'new_v_cv_b_pw1': 'new_v', 'new_v_cv_w_dw': 'new_v', 'new_v_cv_b_dw': 'new_v', 'new_v_cv_ln_g': 'new_v', 'new_v_cv_ln_b': 'new_v', 'new_v_cv_w_pw2': 'new_v', 'new_v_cv_b_pw2': 'new_v', 'new_v_ffn_w_gate': 'new_v', 'new_v_ffn_w_up': 'new_v', 'new_v_ffn_w_down': 'new_v', 'new_v_final_norm_g': 'new_v'}


def _forward(args):
    return _fwd_reference(*[args[k] for k in FWD_PARAMS])


def _output_shape():
    def fwd():
        inp = _fwd_setup_inputs(0)
        return _fwd_reference(*[inp[k] for k in FWD_PARAMS])
    out = _jax.eval_shape(fwd)
    return out.shape, out.dtype

N_MICROBATCH = 1
ADAM_LR = 0.001
ADAM_B1 = 0.9
ADAM_B2 = 0.999
ADAM_EPS = 1e-08
ADAM_WD = 0.01
ADAM_STEP = 10
PER_EXAMPLE_BATCH_AXIS = {'x': 0, 'c': 0, 'loss_target': 0}
SHARED_INPUTS = []
_WEIGHT_DTYPES = {'ada_w': _jnp.float32, 'ada_b': _jnp.float32, 'norm_mix_g': _jnp.float32, 'norm_ffn_g': _jnp.float32, 'hy_w_in': _jnp.float32, 'hy_conv_w': _jnp.float32, 'hy_conv_b': _jnp.float32, 'hy_dt_bias': _jnp.float32, 'hy_a_log': _jnp.float32, 'hy_d_skip': _jnp.float32, 'hy_ssm_norm_g': _jnp.float32, 'hy_w_out': _jnp.float32, 'rel_table': _jnp.float32, 'cv_w_pw1': _jnp.float32, 'cv_b_pw1': _jnp.float32, 'cv_w_dw': _jnp.float32, 'cv_b_dw': _jnp.float32, 'cv_ln_g': _jnp.float32, 'cv_ln_b': _jnp.float32, 'cv_w_pw2': _jnp.float32, 'cv_b_pw2': _jnp.float32, 'ffn_w_gate': _jnp.float32, 'ffn_w_up': _jnp.float32, 'ffn_w_down': _jnp.float32, 'final_norm_g': _jnp.float32}
MOMENT_SCALE = {'ada_w': 6.296958e-02, 'ada_b': 1.236839e-01, 'norm_mix_g': 5.207803e-02, 'norm_ffn_g': 5.173501e-02, 'hy_w_in': 2.185085e-02, 'hy_conv_w': 2.842607e-02, 'hy_conv_b': 3.810006e-02, 'hy_dt_bias': 8.459290e-02, 'hy_a_log': 1.889323e-01, 'hy_d_skip': 1.533557e-01, 'hy_ssm_norm_g': 3.279742e-02, 'hy_w_out': 4.657274e-02, 'rel_table': 4.407665e-03, 'cv_w_pw1': 2.378937e-02, 'cv_b_pw1': 2.723230e-02, 'cv_w_dw': 3.138937e-02, 'cv_b_dw': 6.088156e-02, 'cv_ln_g': 3.883107e-02, 'cv_ln_b': 3.545952e-02, 'cv_w_pw2': 3.114945e-02, 'cv_b_pw2': 6.099127e-02, 'ffn_w_gate': 2.236312e-02, 'ffn_w_up': 2.170751e-02, 'ffn_w_down': 3.605072e-02, 'final_norm_g': 3.207659e+01}


def _to_microbatches(a, axis):
    t = _jnp.moveaxis(a, axis, 0)
    t = t.reshape((N_MICROBATCH, t.shape[0] // N_MICROBATCH) + t.shape[1:])
    return _jnp.moveaxis(t, 1, axis + 1)


def setup_inputs(seed: int = 0) -> dict:
    inp = _fwd_setup_inputs(seed)
    key = _jax.random.fold_in(_jax.random.key(seed), 7919)
    shape, _ = _output_shape()
    out = dict(inp)
    out["loss_target"] = _jax.random.normal(_jax.random.fold_in(key, 0), shape, _jnp.float32)
    for i, name in enumerate(TWIN_WEIGHTS):
        w = inp[name].astype(_jnp.float32)
        if MOMENT_SCALE is None:
            s = _jnp.sqrt(_jnp.mean(_jnp.square(w)) + 1e-30)
        else:
            s = MOMENT_SCALE[name]
        km, kv = _jax.random.split(_jax.random.fold_in(key, i + 1))
        out[name] = w
        out["m_" + name] = s * _jax.random.normal(km, w.shape, _jnp.float32)
        out["v_" + name] = (s * s) * _jax.random.uniform(kv, w.shape, _jnp.float32, 0.5, 1.5)
    if N_MICROBATCH > 1:
        for name, axis in PER_EXAMPLE_BATCH_AXIS.items():
            out[name] = _to_microbatches(out[name], axis)
    return {'x': out['x'], 'c': out['c'], 'ada_w': out['ada_w'], 'ada_b': out['ada_b'], 'norm_mix_g': out['norm_mix_g'], 'norm_ffn_g': out['norm_ffn_g'], 'hy_w_in': out['hy_w_in'], 'hy_conv_w': out['hy_conv_w'], 'hy_conv_b': out['hy_conv_b'], 'hy_dt_bias': out['hy_dt_bias'], 'hy_a_log': out['hy_a_log'], 'hy_d_skip': out['hy_d_skip'], 'hy_ssm_norm_g': out['hy_ssm_norm_g'], 'hy_w_out': out['hy_w_out'], 'rel_table': out['rel_table'], 'cv_w_pw1': out['cv_w_pw1'], 'cv_b_pw1': out['cv_b_pw1'], 'cv_w_dw': out['cv_w_dw'], 'cv_b_dw': out['cv_b_dw'], 'cv_ln_g': out['cv_ln_g'], 'cv_ln_b': out['cv_ln_b'], 'cv_w_pw2': out['cv_w_pw2'], 'cv_b_pw2': out['cv_b_pw2'], 'ffn_w_gate': out['ffn_w_gate'], 'ffn_w_up': out['ffn_w_up'], 'ffn_w_down': out['ffn_w_down'], 'final_norm_g': out['final_norm_g'], 'loss_target': out['loss_target'], 'm_ada_w': out['m_ada_w'], 'm_ada_b': out['m_ada_b'], 'm_norm_mix_g': out['m_norm_mix_g'], 'm_norm_ffn_g': out['m_norm_ffn_g'], 'm_hy_w_in': out['m_hy_w_in'], 'm_hy_conv_w': out['m_hy_conv_w'], 'm_hy_conv_b': out['m_hy_conv_b'], 'm_hy_dt_bias': out['m_hy_dt_bias'], 'm_hy_a_log': out['m_hy_a_log'], 'm_hy_d_skip': out['m_hy_d_skip'], 'm_hy_ssm_norm_g': out['m_hy_ssm_norm_g'], 'm_hy_w_out': out['m_hy_w_out'], 'm_rel_table': out['m_rel_table'], 'm_cv_w_pw1': out['m_cv_w_pw1'], 'm_cv_b_pw1': out['m_cv_b_pw1'], 'm_cv_w_dw': out['m_cv_w_dw'], 'm_cv_b_dw': out['m_cv_b_dw'], 'm_cv_ln_g': out['m_cv_ln_g'], 'm_cv_ln_b': out['m_cv_ln_b'], 'm_cv_w_pw2': out['m_cv_w_pw2'], 'm_cv_b_pw2': out['m_cv_b_pw2'], 'm_ffn_w_gate': out['m_ffn_w_gate'], 'm_ffn_w_up': out['m_ffn_w_up'], 'm_ffn_w_down': out['m_ffn_w_down'], 'm_final_norm_g': out['m_final_norm_g'], 'v_ada_w': out['v_ada_w'], 'v_ada_b': out['v_ada_b'], 'v_norm_mix_g': out['v_norm_mix_g'], 'v_norm_ffn_g': out['v_norm_ffn_g'], 'v_hy_w_in': out['v_hy_w_in'], 'v_hy_conv_w': out['v_hy_conv_w'], 'v_hy_conv_b': out['v_hy_conv_b'], 'v_hy_dt_bias': out['v_hy_dt_bias'], 'v_hy_a_log': out['v_hy_a_log'], 'v_hy_d_skip': out['v_hy_d_skip'], 'v_hy_ssm_norm_g': out['v_hy_ssm_norm_g'], 'v_hy_w_out': out['v_hy_w_out'], 'v_rel_table': out['v_rel_table'], 'v_cv_w_pw1': out['v_cv_w_pw1'], 'v_cv_b_pw1': out['v_cv_b_pw1'], 'v_cv_w_dw': out['v_cv_w_dw'], 'v_cv_b_dw': out['v_cv_b_dw'], 'v_cv_ln_g': out['v_cv_ln_g'], 'v_cv_ln_b': out['v_cv_ln_b'], 'v_cv_w_pw2': out['v_cv_w_pw2'], 'v_cv_b_pw2': out['v_cv_b_pw2'], 'v_ffn_w_gate': out['v_ffn_w_gate'], 'v_ffn_w_up': out['v_ffn_w_up'], 'v_ffn_w_down': out['v_ffn_w_down'], 'v_final_norm_g': out['v_final_norm_g']}


def _loss(weights, diff, rest, loss_target):
    with _jax.named_scope("forward"):
        args = {**rest, TWIN_DIFF_INPUT: diff, **{k: w.astype(_WEIGHT_DTYPES[k]) for k, w in weights.items()}}
        y = _forward(args)
    with _jax.named_scope("loss_head"):
        err = _jnp.square(y.astype(_jnp.float32) - loss_target)
        return 0.5 * _jnp.sum(_jnp.mean(err, axis=-1)) if err.ndim else 0.5 * err


def _adamw(w, g, m, v):
    m = ADAM_B1 * m + (1.0 - ADAM_B1) * g
    v = ADAM_B2 * v + (1.0 - ADAM_B2) * _jnp.square(g)
    m_hat = m / (1.0 - ADAM_B1 ** ADAM_STEP)
    v_hat = v / (1.0 - ADAM_B2 ** ADAM_STEP)
    delta = -ADAM_LR * (m_hat / (_jnp.sqrt(v_hat) + ADAM_EPS) + ADAM_WD * w)
    return delta, m, v


def reference(x, c, ada_w, ada_b, norm_mix_g, norm_ffn_g, hy_w_in, hy_conv_w, hy_conv_b, hy_dt_bias, hy_a_log, hy_d_skip, hy_ssm_norm_g, hy_w_out, rel_table, cv_w_pw1, cv_b_pw1, cv_w_dw, cv_b_dw, cv_ln_g, cv_ln_b, cv_w_pw2, cv_b_pw2, ffn_w_gate, ffn_w_up, ffn_w_down, final_norm_g, loss_target, m_ada_w, m_ada_b, m_norm_mix_g, m_norm_ffn_g, m_hy_w_in, m_hy_conv_w, m_hy_conv_b, m_hy_dt_bias, m_hy_a_log, m_hy_d_skip, m_hy_ssm_norm_g, m_hy_w_out, m_rel_table, m_cv_w_pw1, m_cv_b_pw1, m_cv_w_dw, m_cv_b_dw, m_cv_ln_g, m_cv_ln_b, m_cv_w_pw2, m_cv_b_pw2, m_ffn_w_gate, m_ffn_w_up, m_ffn_w_down, m_final_norm_g, v_ada_w, v_ada_b, v_norm_mix_g, v_norm_ffn_g, v_hy_w_in, v_hy_conv_w, v_hy_conv_b, v_hy_dt_bias, v_hy_a_log, v_hy_d_skip, v_hy_ssm_norm_g, v_hy_w_out, v_rel_table, v_cv_w_pw1, v_cv_b_pw1, v_cv_w_dw, v_cv_b_dw, v_cv_ln_g, v_cv_ln_b, v_cv_w_pw2, v_cv_b_pw2, v_ffn_w_gate, v_ffn_w_up, v_ffn_w_down, v_final_norm_g):
    given = dict(x=x, c=c, ada_w=ada_w, ada_b=ada_b, norm_mix_g=norm_mix_g, norm_ffn_g=norm_ffn_g, hy_w_in=hy_w_in, hy_conv_w=hy_conv_w, hy_conv_b=hy_conv_b, hy_dt_bias=hy_dt_bias, hy_a_log=hy_a_log, hy_d_skip=hy_d_skip, hy_ssm_norm_g=hy_ssm_norm_g, hy_w_out=hy_w_out, rel_table=rel_table, cv_w_pw1=cv_w_pw1, cv_b_pw1=cv_b_pw1, cv_w_dw=cv_w_dw, cv_b_dw=cv_b_dw, cv_ln_g=cv_ln_g, cv_ln_b=cv_ln_b, cv_w_pw2=cv_w_pw2, cv_b_pw2=cv_b_pw2, ffn_w_gate=ffn_w_gate, ffn_w_up=ffn_w_up, ffn_w_down=ffn_w_down, final_norm_g=final_norm_g, loss_target=loss_target, m_ada_w=m_ada_w, m_ada_b=m_ada_b, m_norm_mix_g=m_norm_mix_g, m_norm_ffn_g=m_norm_ffn_g, m_hy_w_in=m_hy_w_in, m_hy_conv_w=m_hy_conv_w, m_hy_conv_b=m_hy_conv_b, m_hy_dt_bias=m_hy_dt_bias, m_hy_a_log=m_hy_a_log, m_hy_d_skip=m_hy_d_skip, m_hy_ssm_norm_g=m_hy_ssm_norm_g, m_hy_w_out=m_hy_w_out, m_rel_table=m_rel_table, m_cv_w_pw1=m_cv_w_pw1, m_cv_b_pw1=m_cv_b_pw1, m_cv_w_dw=m_cv_w_dw, m_cv_b_dw=m_cv_b_dw, m_cv_ln_g=m_cv_ln_g, m_cv_ln_b=m_cv_ln_b, m_cv_w_pw2=m_cv_w_pw2, m_cv_b_pw2=m_cv_b_pw2, m_ffn_w_gate=m_ffn_w_gate, m_ffn_w_up=m_ffn_w_up, m_ffn_w_down=m_ffn_w_down, m_final_norm_g=m_final_norm_g, v_ada_w=v_ada_w, v_ada_b=v_ada_b, v_norm_mix_g=v_norm_mix_g, v_norm_ffn_g=v_norm_ffn_g, v_hy_w_in=v_hy_w_in, v_hy_conv_w=v_hy_conv_w, v_hy_conv_b=v_hy_conv_b, v_hy_dt_bias=v_hy_dt_bias, v_hy_a_log=v_hy_a_log, v_hy_d_skip=v_hy_d_skip, v_hy_ssm_norm_g=v_hy_ssm_norm_g, v_hy_w_out=v_hy_w_out, v_rel_table=v_rel_table, v_cv_w_pw1=v_cv_w_pw1, v_cv_b_pw1=v_cv_b_pw1, v_cv_w_dw=v_cv_w_dw, v_cv_b_dw=v_cv_b_dw, v_cv_ln_g=v_cv_ln_g, v_cv_ln_b=v_cv_ln_b, v_cv_w_pw2=v_cv_w_pw2, v_cv_b_pw2=v_cv_b_pw2, v_ffn_w_gate=v_ffn_w_gate, v_ffn_w_up=v_ffn_w_up, v_ffn_w_down=v_ffn_w_down, v_final_norm_g=v_final_norm_g)
    weights = {n: given[n] for n in TWIN_WEIGHTS}
    shared = {n: given[n] for n in SHARED_INPUTS}
    per_example = {n: given[n] for n in ['x', 'c']}
    grad_fn = _jax.value_and_grad(_loss, argnums=(0, 1))

    def one_microbatch(ex, loss_target):
        ex = dict(ex)
        diff = ex.pop(TWIN_DIFF_INPUT)
        return grad_fn(weights, diff, {**shared, **ex}, loss_target)

    if N_MICROBATCH == 1:
        loss, (grad_w, grad_x) = one_microbatch(per_example, given["loss_target"])
    else:
        def body(carry, xs):
            loss_sum, grad_sum = carry
            l_k, (gw_k, gx_k) = one_microbatch(xs[0], xs[1])
            with _jax.named_scope("update"):
                return (loss_sum + l_k, _jax.tree.map(_jnp.add, grad_sum, gw_k)), gx_k

        init = (_jnp.zeros((), _jnp.float32), _jax.tree.map(_jnp.zeros_like, weights))
        (loss, grad_w), grad_x = _jax.lax.scan(body, init, (per_example, given["loss_target"]))
    with _jax.named_scope("update"):
        delta_w, new_m, new_v = {}, {}, {}
        for n in TWIN_WEIGHTS:
            delta_w[n], new_m[n], new_v[n] = _adamw(weights[n], grad_w[n], given["m_" + n], given["v_" + n])
    return (loss, grad_x, *[grad_w[n] for n in TWIN_WEIGHTS], *[delta_w[n] for n in TWIN_WEIGHTS],
            *[new_m[n] for n in TWIN_WEIGHTS], *[new_v[n] for n in TWIN_WEIGHTS])
```

```python
import functools
import math

import numpy as np
import jax
import jax.numpy as jnp
from jax import lax
from jax.experimental import pallas as pl
from jax.experimental.pallas import tpu as pltpu

f32 = jnp.float32
bf16 = jnp.bfloat16
EPS = 1e-6
N_DEV = 8
LANES = 128
SSM_STATE = 128
SSM_CHUNK = 128
SSM_GROUPS = 4
HEAD_DIM = 64
ATT_BLK = 128
ATT_DILATIONS = (1, 4, 16)
REL_BUCKETS = 32
REL_MAX_DIST = 2048
ADAM_LR, ADAM_B1, ADAM_B2, ADAM_EPS, ADAM_WD, ADAM_STEP = 0.001, 0.9, 0.999, 1e-08, 0.01, 10
PACK_COLS = 1024
PACK_ROW_TILE = 256
MESH = pl.DeviceIdType.MESH
VMEM_LIMIT = 48 * 1024 * 1024


def _sds(shape, dtype=f32):
    return jax.ShapeDtypeStruct(tuple(shape), dtype)


def _tile(n, cap, mult):
    best = None
    t = mult
    while t <= min(n, cap):
        if n % t == 0:
            best = t
        t += mult
    return best if best is not None else n


def _params(sem):
    return pltpu.CompilerParams(dimension_semantics=sem, vmem_limit_bytes=VMEM_LIMIT)


def _mm(a, b, *, name, ta=False, tb=False, bias=None, add=None, tm_cap=512, tn_cap=512, tk_cap=1024):
    if ta:
        K, M = a.shape
    else:
        M, K = a.shape
    if tb:
        N, K2 = b.shape
    else:
        K2, N = b.shape
    assert K == K2, (a.shape, b.shape, ta, tb)
    tm = _tile(M, tm_cap, LANES)
    tn = _tile(N, tn_cap, LANES)
    tk = _tile(K, tk_cap, LANES)
    nk = K // tk
    has_bias, has_add = bias is not None, add is not None
    dn = (((0 if ta else 1,), (1 if tb else 0,)), ((), ()))

    def body(*refs):
        a_ref, b_ref = refs[0], refs[1]
        pos = 2
        bias_ref = add_ref = None
        if has_bias:
            bias_ref = refs[pos]
            pos += 1
        if has_add:
            add_ref = refs[pos]
            pos += 1
        o_ref, acc_ref = refs[pos], refs[pos + 1]
        k = pl.program_id(2)

        @pl.when(k == 0)
        def _():
            acc_ref[...] = jnp.zeros_like(acc_ref)

        acc_ref[...] += lax.dot_general(a_ref[...].astype(bf16), b_ref[...].astype(bf16), dn,
                                        preferred_element_type=f32)

        @pl.when(k == nk - 1)
        def _():
            r = acc_ref[...]
            if has_bias:
                r = r + bias_ref[...]
            if has_add:
                r = r + add_ref[...]
            o_ref[...] = r

    in_specs = [
        pl.BlockSpec((tk, tm), lambda i, j, k: (k, i)) if ta else pl.BlockSpec((tm, tk), lambda i, j, k: (i, k)),
        pl.BlockSpec((tn, tk), lambda i, j, k: (j, k)) if tb else pl.BlockSpec((tk, tn), lambda i, j, k: (k, j)),
    ]
    args = [a, b]
    if has_bias:
        in_specs.append(pl.BlockSpec((1, tn), lambda i, j, k: (0, j)))
        args.append(bias)
    if has_add:
        in_specs.append(pl.BlockSpec((tm, tn), lambda i, j, k: (i, j)))
        args.append(add)
    return pl.pallas_call(
        body, name=name, grid=(M // tm, N // tn, nk), in_specs=in_specs,
        out_specs=pl.BlockSpec((tm, tn), lambda i, j, k: (i, j)), out_shape=_sds((M, N)),
        scratch_shapes=[pltpu.VMEM((tm, tn), f32)],
        compiler_params=_params(("parallel", "parallel", "arbitrary")),
    )(*args)


def _rowwise(name, fn, rows, vecs, out_rows, out_accs, *, tr_cap=256, sub=8):
    rows = [r if isinstance(r, tuple) else (r, 0, r.shape[1]) for r in rows]
    R = rows[0][0].shape[0]
    tr = _tile(R, tr_cap, 8)
    sub = sub if tr % sub == 0 else tr
    n_r, n_v, n_or, n_oa = len(rows), len(vecs), len(out_rows), len(out_accs)

    def body(*refs):
        row_refs = refs[:n_r]
        vec_refs = refs[n_r:n_r + n_v]
        orow_refs = refs[n_r + n_v:n_r + n_v + n_or]
        oacc_refs = refs[n_r + n_v + n_or:]
        vv = [r[...] for r in vec_refs]

        def step(s, accs):
            sl = pl.ds(pl.multiple_of(s * sub, sub), sub)
            ro, ao = fn([r[sl, :] for r in row_refs], vv)
            for o_ref, o in zip(orow_refs, ro):
                o_ref[sl, :] = o.astype(o_ref.dtype)
            return tuple(x + y for x, y in zip(accs, ao))

        accs = lax.fori_loop(0, tr // sub, step, tuple(jnp.zeros((1, w), f32) for w in out_accs))
        if n_oa:
            @pl.when(pl.program_id(0) == 0)
            def _():
                for ref in oacc_refs:
                    ref[...] = jnp.zeros_like(ref)

            for ref, x in zip(oacc_refs, accs):
                ref[...] += x

    in_specs = [pl.BlockSpec((tr, w), functools.partial(lambda i, cb: (i, cb), cb=cb)) for (_, cb, w) in rows]
    in_specs += [pl.BlockSpec((1, v.shape[1]), lambda i: (0, 0)) for v in vecs]
    out_specs = [pl.BlockSpec((tr, w), lambda i: (i, 0)) for (w, _) in out_rows]
    out_specs += [pl.BlockSpec((1, w), lambda i: (0, 0)) for w in out_accs]
    out_shape = [_sds((R, w), dt) for (w, dt) in out_rows] + [_sds((1, w)) for w in out_accs]
    res = pl.pallas_call(
        body, name=name, grid=(R // tr,), in_specs=in_specs, out_specs=out_specs, out_shape=out_shape,
        compiler_params=_params(("arbitrary",)),
    )(*[r[0] for r in rows], *vecs)
    return res[:n_or], res[n_or:]


def _silu(x):
    return x * jax.nn.sigmoid(x)


def _rms(x, g):
    return x * lax.rsqrt(jnp.mean(x * x, -1, keepdims=True) + EPS) * g


def _adaln_f(x, g, sc, sh):
    return _rms(x, g) * (1.0 + sc) + sh


def _gate_f(y, z, g):
    return _rms(y * _silu(z), g)


def _lnsilu_f(u, g, b):
    mu = jnp.mean(u, -1, keepdims=True)
    var = jnp.mean(jnp.square(u - mu), -1, keepdims=True)
    return _silu((u - mu) * lax.rsqrt(var + EPS) * g + b)


def _adaln_fwd(x, g, sc, sh, name):
    (h,), _ = _rowwise(name, lambda rv, vv: ([_adaln_f(rv[0], *vv)], []), [x], [g, sc, sh], [(x.shape[1], f32)], [])
    return h


def _adaln_bwd(x, g, sc, sh, dh, dres, name):
    def fn(rv, vv):
        xv, dhv, drv = rv
        _, vjp = jax.vjp(_adaln_f, xv, *vv)
        dx, dg, dsc, dsh = vjp(dhv)
        return [dx + drv], [dg, dsc, dsh]
    w = x.shape[1]
    (dx,), accs = _rowwise(name, fn, [x, dh, dres], [g, sc, sh], [(w, f32)], [w, w, w])
    return dx, accs


def _resid_fwd(x, gate, mix, name):
    (y,), _ = _rowwise(name, lambda rv, vv: ([rv[0] + vv[0] * rv[1]], []), [x, mix], [gate], [(x.shape[1], f32)], [])
    return y


def _resid_bwd(dx, mix, gate, name):
    def fn(rv, vv):
        dxv, mv = rv
        dm = vv[0] * dxv
        return [dm], [jnp.sum(dxv * mv, 0, keepdims=True), jnp.sum(dm, 0, keepdims=True)]
    w = dx.shape[1]
    (dmix,), accs = _rowwise(name, fn, [dx, mix], [gate], [(w, f32)], [w, w])
    return dmix, accs


def _add3(a, b, c, name):
    (y,), _ = _rowwise(name, lambda rv, vv: ([rv[0] + rv[1] + rv[2]], []), [a, b, c], [], [(a.shape[1], f32)], [])
    return y


CONV_HALO = 32


def _conv_fwd(x, w, b, *, silu, name, tr=512):
    S, C = x.shape
    K = w.shape[0]
    H = CONV_HALO
    assert K - 1 <= H and S % tr == 0 and tr % H == 0 and C % LANES == 0
    nh = tr // H

    def body(xp_ref, xc_ref, w_ref, b_ref, *rest):
        outs, scr = rest[:-1], rest[-1]
        i = pl.program_id(1)
        scr[pl.ds(0, H), :] = jnp.where(i > 0, xp_ref[...], 0.0)
        scr[pl.ds(H, tr), :] = xc_ref[...]
        acc = jnp.zeros((tr, LANES), f32) + b_ref[...]
        for k in range(K):
            acc = acc + scr[pl.ds(H - (K - 1) + k, tr), :] * w_ref[pl.ds(k, 1), :]
        outs[0][...] = acc
        if silu:
            outs[1][...] = _silu(acc)

    n_out = 2 if silu else 1
    return pl.pallas_call(
        body, name=name, grid=(C // LANES, S // tr),
        in_specs=[pl.BlockSpec((H, LANES), lambda j, i: (jnp.maximum(i * nh - 1, 0), j)),
                  pl.BlockSpec((tr, LANES), lambda j, i: (i, j)),
                  pl.BlockSpec((K, LANES), lambda j, i: (0, j)),
                  pl.BlockSpec((1, LANES), lambda j, i: (0, j))],
        out_specs=[pl.BlockSpec((tr, LANES), lambda j, i: (i, j))] * n_out,
        out_shape=[_sds((S, C))] * n_out,
        scratch_shapes=[pltpu.VMEM((tr + H, LANES), f32)],
        compiler_params=_params(("parallel", "arbitrary")),
    )(x, x, w, b)


def _conv_bwd(x, w, dact, pre, *, silu, name, tr=512):
    S, C = x.shape
    K = w.shape[0]
    H = CONV_HALO
    nh = tr // H
    n_i = S // tr
    kp = -(-K // 8) * 8

    def dsilu(p):
        s = jax.nn.sigmoid(p)
        return s * (1.0 + p * (1.0 - s))

    def body(*refs):
        if silu:
            xp_ref, xc_ref, w_ref, dc_ref, dn_ref, pc_ref, pn_ref, dx_ref, dw_ref, db_ref, xs, ds = refs
        else:
            xp_ref, xc_ref, w_ref, dc_ref, dn_ref, dx_ref, dw_ref, db_ref, xs, ds = refs
        i = pl.program_id(1)
        xs[pl.ds(0, H), :] = jnp.where(i > 0, xp_ref[...], 0.0)
        xs[pl.ds(H, tr), :] = xc_ref[...]
        dcur = dc_ref[...]
        dnext = dn_ref[...]
        if silu:
            dcur = dcur * dsilu(pc_ref[...])
            dnext = dnext * dsilu(pn_ref[...])
        ds[pl.ds(0, tr), :] = dcur
        ds[pl.ds(tr, H), :] = jnp.where(i < n_i - 1, dnext, 0.0)
        acc = jnp.zeros((tr, LANES), f32)
        for k in range(K):
            acc = acc + ds[pl.ds(K - 1 - k, tr), :] * w_ref[pl.ds(k, 1), :]
        dx_ref[...] = acc

        @pl.when(i == 0)
        def _():
            dw_ref[...] = jnp.zeros_like(dw_ref)
            db_ref[...] = jnp.zeros_like(db_ref)

        for k in range(K):
            dw_ref[pl.ds(k, 1), :] += jnp.sum(dcur * xs[pl.ds(H - (K - 1) + k, tr), :], 0, keepdims=True)
        db_ref[...] += jnp.sum(dcur, 0, keepdims=True)

    prev = pl.BlockSpec((H, LANES), lambda j, i: (jnp.maximum(i * nh - 1, 0), j))
    cur = pl.BlockSpec((tr, LANES), lambda j, i: (i, j))
    nxt = pl.BlockSpec((H, LANES), lambda j, i: (jnp.minimum((i + 1) * nh, n_i * nh - 1), j))
    in_specs = [prev, cur, pl.BlockSpec((K, LANES), lambda j, i: (0, j)), cur, nxt]
    args = [x, x, w, dact, dact]
    if silu:
        in_specs += [cur, nxt]
        args += [pre, pre]
    dx, dw, db = pl.pallas_call(
        body, name=name, grid=(C // LANES, n_i), in_specs=in_specs,
        out_specs=[cur, pl.BlockSpec((kp, LANES), lambda j, i: (0, j)), pl.BlockSpec((1, LANES), lambda j, i: (0, j))],
        out_shape=[_sds((S, C)), _sds((kp, C)), _sds((1, C))],
        scratch_shapes=[pltpu.VMEM((tr + H, LANES), f32), pltpu.VMEM((tr + H, LANES), f32)],
        compiler_params=_params(("parallel", "arbitrary")),
    )(*args)
    return dx, dw[:K], db


def _dot(a, b):
    return jnp.dot(a.astype(bf16), b.astype(bf16), preferred_element_type=f32)


def _dot_nt(a, b):
    return lax.dot_general(a.astype(bf16), b.astype(bf16), (((1,), (1,)), ((), ())), preferred_element_type=f32)


def _dot_tn(a, b):
    return lax.dot_general(a.astype(bf16), b.astype(bf16), (((0,), (0,)), ((), ())), preferred_element_type=f32)


def _softplus(x):
    return jnp.maximum(x, 0.0) + jnp.log(1.0 + jnp.exp(-jnp.abs(x)))


def _tri(q):
    i = lax.broadcasted_iota(jnp.int32, (q, q), 0)
    j = lax.broadcasted_iota(jnp.int32, (q, q), 1)
    return i >= j


def _ssd_prep(dtraw, dt_bias, a_log):
    q = dtraw.shape[0]
    dt = _softplus(dtraw + dt_bias)
    A = -jnp.exp(a_log)
    tri = _tri(q)
    cs = jnp.dot(tri.astype(f32), dt * A, preferred_element_type=f32, precision=lax.Precision.HIGHEST)
    return dt, A, cs, cs.T, tri


def _expand(cols, h0, n, width):
    q = cols.shape[0]
    return jnp.concatenate([jnp.broadcast_to(cols[:, h0 + r:h0 + r + 1], (q, width)) for r in range(n)], axis=1)


def _ssd_fwd(xbc, dtraw, dt_bias, a_log, d_skip, di, name):
    S, CD = xbc.shape
    Q, N, G = SSM_CHUNK, SSM_STATE, SSM_GROUPS
    nc = S // Q
    nh = di // HEAD_DIM
    R = nh // G
    gw = R * HEAD_DIM

    def body(xbc_ref, dt_ref, bias_ref, alog_ref, dsk_ref, y_ref, hin_ref, state):
        c = pl.program_id(0)

        @pl.when(c == 0)
        def _():
            state[...] = jnp.zeros_like(state)

        hin_ref[...] = state[...]
        dt, A, cs, csT, tri = _ssd_prep(dt_ref[...], bias_ref[...], alog_ref[...])
        dsk = dsk_ref[...]
        ecs = jnp.exp(cs)
        dend = jnp.exp(cs[Q - 1:Q, :] - cs)
        elast = jnp.exp(cs[Q - 1:Q, :])
        for g in range(G):
            h0 = g * R
            Bg = xbc_ref[:, pl.ds(di + g * N, N)]
            Cg = xbc_ref[:, pl.ds(di + G * N + g * N, N)]
            xg = xbc_ref[:, pl.ds(g * gw, gw)]
            Hg = state[pl.ds(g * gw, gw), :]
            Gm = _dot_nt(Cg, Bg)
            xdt = xg * _expand(dt, h0, R, HEAD_DIM)
            yoff = _dot_nt(Cg, Hg) * _expand(ecs, h0, R, HEAD_DIM)
            ys = []
            for r in range(R):
                h = h0 + r
                L = jnp.exp(jnp.where(tri, cs[:, h:h + 1] - csT[h:h + 1, :], -jnp.inf))
                ys.append(_dot(Gm * L, xdt[:, r * HEAD_DIM:(r + 1) * HEAD_DIM]))
            y = jnp.concatenate(ys, axis=1) + yoff + xg * _expand(dsk, h0, R, HEAD_DIM)
            y_ref[:, pl.ds(g * gw, gw)] = y
            hnew = _dot_tn(xdt * _expand(dend, h0, R, HEAD_DIM), Bg)
            escale = jnp.concatenate([jnp.broadcast_to(elast[:, h0 + r:h0 + r + 1], (HEAD_DIM, N)) for r in range(R)], axis=0)
            state[pl.ds(g * gw, gw), :] = escale * Hg + hnew

    vec = pl.BlockSpec((1, LANES), lambda c: (0, 0))
    return pl.pallas_call(
        body, name=name, grid=(nc,),
        in_specs=[pl.BlockSpec((Q, CD), lambda c: (c, 0)), pl.BlockSpec((Q, LANES), lambda c: (c, 0)), vec, vec, vec],
        out_specs=[pl.BlockSpec((Q, di), lambda c: (c, 0)), pl.BlockSpec((None, di, N), lambda c: (c, 0, 0))],
        out_shape=[_sds((S, di)), _sds((nc, di, N))],
        scratch_shapes=[pltpu.VMEM((di, N), f32)],
        compiler_params=_params(("arbitrary",)),
    )(xbc, dtraw, dt_bias, a_log, d_skip)


def _ssd_bwd(xbc, dtraw, dt_bias, a_log, d_skip, hin, dy, di, name):
    S, CD = xbc.shape
    Q, N, G = SSM_CHUNK, SSM_STATE, SSM_GROUPS
    nc = S // Q
    nh = di // HEAD_DIM
    R = nh // G
    gw = R * HEAD_DIM
    P = HEAD_DIM

    def body(xbc_ref, dt_ref, bias_ref, alog_ref, dsk_ref, hin_ref, dy_ref, dxbc_ref, ddt_ref, dA_ref, ddsk_ref, dtb_ref, dstate):
        c = pl.program_id(0)

        @pl.when(c == 0)
        def _():
            dstate[...] = jnp.zeros_like(dstate)
            dA_ref[...] = jnp.zeros_like(dA_ref)
            ddsk_ref[...] = jnp.zeros_like(ddsk_ref)
            dtb_ref[...] = jnp.zeros_like(dtb_ref)

        dtraw_v = dt_ref[...]
        dt, A, cs, csT, tri = _ssd_prep(dtraw_v, bias_ref[...], alog_ref[...])
        dsk = dsk_ref[...]
        ecs = jnp.exp(cs)
        dend = jnp.exp(cs[Q - 1:Q, :] - cs)
        elast = jnp.exp(cs[Q - 1:Q, :])
        lane = lax.broadcasted_iota(jnp.int32, (1, LANES), 1)
        row = lax.broadcasted_iota(jnp.int32, (Q, 1), 0)
        dcs = jnp.zeros((Q, LANES), f32)
        rsx = jnp.zeros((Q, LANES), f32)
        ddsk = jnp.zeros((1, LANES), f32)
        for g in range(G):
            h0 = g * R
            Bg = xbc_ref[:, pl.ds(di + g * N, N)]
            Cg = xbc_ref[:, pl.ds(di + G * N + g * N, N)]
            xg = xbc_ref[:, pl.ds(g * gw, gw)]
            dyg = dy_ref[:, pl.ds(g * gw, gw)]
            Hg = hin_ref[pl.ds(g * gw, gw), :]
            dHg = dstate[pl.ds(g * gw, gw), :]
            dt_e = _expand(dt, h0, R, P)
            ecs_e = _expand(ecs, h0, R, P)
            dend_e = _expand(dend, h0, R, P)
            Gm = _dot_nt(Cg, Bg)
            xdt = xg * dt_e
            yoff_raw = _dot_nt(Cg, Hg)
            dye = dyg * ecs_e
            bdh = _dot_nt(Bg, dHg)
            dC = _dot(dye, Hg)
            dB = _dot(xdt * dend_e, dHg)
            dHin = _dot_tn(dye, Cg)
            dxdt_state = dend_e * bdh
            t_off = dyg * yoff_raw * ecs_e
            t_end = bdh * xdt * dend_e
            dG = jnp.zeros((Q, Q), f32)
            dxs = []
            for r in range(R):
                h = h0 + r
                sl = slice(r * P, (r + 1) * P)
                L = jnp.exp(jnp.where(tri, cs[:, h:h + 1] - csT[h:h + 1, :], -jnp.inf))
                M = Gm * L
                dyh = dyg[:, sl]
                dM = _dot_nt(dyh, xdt[:, sl])
                dxdt = _dot_tn(M, dyh) + dxdt_state[:, sl]
                dG = dG + dM * L
                E = dM * M
                w_end = jnp.sum(t_end[:, sl], 1, keepdims=True)
                hh = jnp.sum(dHg[sl, :] * Hg[sl, :], keepdims=True) * elast[:, h:h + 1]
                d = (jnp.sum(E, 1, keepdims=True) - jnp.sum(E.T, 1, keepdims=True)
                     + jnp.sum(t_off[:, sl], 1, keepdims=True) - w_end
                     + jnp.where(row == Q - 1, jnp.sum(w_end, keepdims=True) + hh, 0.0))
                onehot = (lane == h).astype(f32)
                dcs = dcs + d * onehot
                rsx = rsx + jnp.sum(dxdt * xg[:, sl], 1, keepdims=True) * onehot
                ddsk = ddsk + jnp.sum(dyh * xg[:, sl], keepdims=True) * onehot
                dxs.append(dxdt * dt_e[:, sl] + dyh * dsk[:, h:h + 1])
            dxbc_ref[:, pl.ds(g * gw, gw)] = jnp.concatenate(dxs, axis=1)
            dxbc_ref[:, pl.ds(di + g * N, N)] = dB + _dot_tn(dG, Cg)
            dxbc_ref[:, pl.ds(di + G * N + g * N, N)] = dC + _dot(dG, Bg)
            escale = jnp.concatenate([jnp.broadcast_to(elast[:, h0 + r:h0 + r + 1], (P, N)) for r in range(R)], axis=0)
            dstate[pl.ds(g * gw, gw), :] = escale * dHg + dHin
        da = lax.dot_general(tri.astype(f32), dcs, (((0,), (0,)), ((), ())), preferred_element_type=f32,
                             precision=lax.Precision.HIGHEST)
        ddt = da * A + rsx
        ddtraw = ddt * jax.nn.sigmoid(dtraw_v + bias_ref[...])
        ddt_ref[...] = ddtraw
        dA_ref[...] += jnp.sum(da * dt, 0, keepdims=True) * A
        ddsk_ref[...] += ddsk
        dtb_ref[...] += jnp.sum(ddtraw, 0, keepdims=True)

    vec = pl.BlockSpec((1, LANES), lambda c: (0, 0))
    rev = lambda c: (nc - 1 - c, 0)
    return pl.pallas_call(
        body, name=name, grid=(nc,),
        in_specs=[pl.BlockSpec((Q, CD), rev), pl.BlockSpec((Q, LANES), rev), vec, vec, vec,
                  pl.BlockSpec((None, di, N), lambda c: (nc - 1 - c, 0, 0)), pl.BlockSpec((Q, di), rev)],
        out_specs=[pl.BlockSpec((Q, CD), rev), pl.BlockSpec((Q, LANES), rev), vec, vec, vec],
        out_shape=[_sds((S, CD)), _sds((S, LANES)), _sds((1, LANES)), _sds((1, LANES)), _sds((1, LANES))],
        scratch_shapes=[pltpu.VMEM((di, N), f32)],
        compiler_params=_params(("arbitrary",)),
    )(xbc, dtraw, dt_bias, a_log, d_skip, hin, dy)


def _t5_bucket_np(dist):
    max_exact = REL_BUCKETS // 2
    n = np.maximum(dist, 1).astype(np.float32)
    large = np.float32(max_exact) + np.log(n / np.float32(max_exact)) / np.float32(math.log(REL_MAX_DIST / max_exact)) * np.float32(REL_BUCKETS - max_exact)
    large = np.minimum(large.astype(np.int32), REL_BUCKETS - 1)
    return np.where(dist < max_exact, dist, large)


def _bucket_onehot():
    i = np.arange(ATT_BLK)[:, None]
    j = np.arange(2 * ATT_BLK)[None, :]
    delta = np.maximum(ATT_BLK + i - j, 0)
    out = np.zeros((len(ATT_DILATIONS), REL_BUCKETS, ATT_BLK * 2 * ATT_BLK), np.float32)
    for gi, d in enumerate(ATT_DILATIONS):
        b = _t5_bucket_np(delta * d).reshape(-1)
        out[gi, b, np.arange(b.size)] = 1.0
    return out


def _exact_mm(a, b, *, name, tb=False):
    M, K = a.shape
    N = b.shape[0] if tb else b.shape[1]
    tn = _tile(N, 4096, LANES)
    dn = (((1,), (1 if tb else 0,)), ((), ()))

    def body(a_ref, b_ref, o_ref):
        o_ref[...] = lax.dot_general(a_ref[...], b_ref[...], dn, preferred_element_type=f32,
                                     precision=lax.Precision.HIGHEST)

    return pl.pallas_call(
        body, name=name, grid=(N // tn,),
        in_specs=[pl.BlockSpec((M, K), lambda j: (0, 0)),
                  pl.BlockSpec((tn, K), lambda j: (j, 0)) if tb else pl.BlockSpec((K, tn), lambda j: (0, j))],
        out_specs=pl.BlockSpec((M, tn), lambda j: (0, j)), out_shape=_sds((M, N)),
        compiler_params=_params(("parallel",)),
    )(a, b)


def _attn_mask(n):
    i = lax.broadcasted_iota(jnp.int32, (ATT_BLK, 2 * ATT_BLK), 0)
    j = lax.broadcasted_iota(jnp.int32, (ATT_BLK, 2 * ATT_BLK), 1)
    delta = ATT_BLK + i - j
    return (delta >= 0) & (delta <= ATT_BLK) & ((j >= ATT_BLK) | (n > 0))


def _attn_fwd(q, k, v, bias, d, name):
    S, D = q.shape
    L = S // d
    nb = L // ATT_BLK
    HP = D // LANES
    scale = HEAD_DIM ** -0.5
    view = lambda t: t.reshape(L, d * D)

    def body(q_ref, kp_ref, kc_ref, vp_ref, vc_ref, b_ref, o_ref, lse_ref):
        mask = _attn_mask(pl.program_id(2))
        qv = q_ref[...]
        kcat = jnp.concatenate([kp_ref[...], kc_ref[...]], axis=0)
        vcat = jnp.concatenate([vp_ref[...], vc_ref[...]], axis=0)
        outs, lses = [], []
        for hh in range(2):
            sl = slice(hh * HEAD_DIM, (hh + 1) * HEAD_DIM)
            s = _dot_nt(qv[:, sl], kcat[:, sl]) * scale + b_ref[hh]
            s = jnp.where(mask, s, -jnp.inf)
            m = jnp.max(s, -1, keepdims=True)
            p = jnp.exp(s - m)
            l = jnp.sum(p, -1, keepdims=True)
            outs.append(_dot(p / l, vcat[:, sl]))
            lses.append(jnp.broadcast_to(m + jnp.log(l), (ATT_BLK, HEAD_DIM)))
        o_ref[...] = jnp.concatenate(outs, axis=1)
        lse_ref[...] = jnp.concatenate(lses, axis=1)

    cur = pl.BlockSpec((ATT_BLK, LANES), lambda hp, r, n: (n, r * HP + hp))
    prev = pl.BlockSpec((ATT_BLK, LANES), lambda hp, r, n: (jnp.maximum(n - 1, 0), r * HP + hp))
    o, lse = pl.pallas_call(
        body, name=name, grid=(HP, d, nb),
        in_specs=[cur, prev, cur, prev, cur, pl.BlockSpec((2, ATT_BLK, 2 * ATT_BLK), lambda hp, r, n: (hp, 0, 0))],
        out_specs=[cur, cur], out_shape=[_sds((L, d * D)), _sds((L, d * D))],
        compiler_params=_params(("parallel", "parallel", "arbitrary")),
    )(view(q), view(k), view(k), view(v), view(v), bias)
    return o.reshape(S, D), lse.reshape(S, D)


def _attn_bwd(q, k, v, bias, att, datt, lse_tot, d, name):
    S, D = q.shape
    L = S // d
    nb = L // ATT_BLK
    HP = D // LANES
    H = D // HEAD_DIM
    scale = HEAD_DIM ** -0.5
    view = lambda t: t.reshape(L, d * D)

    def body(q_ref, kp_ref, kc_ref, vp_ref, vc_ref, b_ref, o_ref, do_ref, lse_ref,
             dq_ref, dk_ref, dv_ref, db_ref, carry_k, carry_v, part_k, part_v):
        r = pl.program_id(1)
        n = pl.program_id(2)

        @pl.when(n == 0)
        def _():
            carry_k[...] = jnp.zeros_like(carry_k)
            carry_v[...] = jnp.zeros_like(carry_v)

        @pl.when((n == 0) & (r == 0))
        def _():
            db_ref[...] = jnp.zeros_like(db_ref)

        @pl.when(n < nb)
        def _():
            mask = _attn_mask(n)
            qv = q_ref[...]
            kcat = jnp.concatenate([kp_ref[...], kc_ref[...]], axis=0)
            vcat = jnp.concatenate([vp_ref[...], vc_ref[...]], axis=0)
            ov, dov, lsev = o_ref[...], do_ref[...], lse_ref[...]
            dqs, dks, dvs = [], [], []
            for hh in range(2):
                sl = slice(hh * HEAD_DIM, (hh + 1) * HEAD_DIM)
                s = _dot_nt(qv[:, sl], kcat[:, sl]) * scale + b_ref[hh]
                p = jnp.exp(jnp.where(mask, s, -jnp.inf) - lsev[:, hh * HEAD_DIM:hh * HEAD_DIM + 1])
                dp = _dot_nt(dov[:, sl], vcat[:, sl])
                dsum = jnp.sum(dov[:, sl] * ov[:, sl], 1, keepdims=True)
                ds = p * (dp - dsum)
                db_ref[hh] += ds
                dqs.append(_dot(ds, kcat[:, sl]) * scale)
                dks.append(_dot_tn(ds, qv[:, sl]) * scale)
                dvs.append(_dot_tn(p, dov[:, sl]))
            dq_ref[...] = jnp.concatenate(dqs, axis=1)
            part_k[...] = jnp.concatenate(dks, axis=1)
            part_v[...] = jnp.concatenate(dvs, axis=1)

        @pl.when(n == nb)
        def _():
            part_k[...] = jnp.zeros_like(part_k)
            part_v[...] = jnp.zeros_like(part_v)

        dk_ref[...] = carry_k[...] + part_k[pl.ds(0, ATT_BLK), :]
        dv_ref[...] = carry_v[...] + part_v[pl.ds(0, ATT_BLK), :]
        carry_k[...] = part_k[pl.ds(ATT_BLK, ATT_BLK), :]
        carry_v[...] = part_v[pl.ds(ATT_BLK, ATT_BLK), :]

    nq = lambda n: jnp.minimum(n, nb - 1)
    cur = pl.BlockSpec((ATT_BLK, LANES), lambda hp, r, n: (nq(n), r * HP + hp))
    prev = pl.BlockSpec((ATT_BLK, LANES), lambda hp, r, n: (jnp.maximum(nq(n) - 1, 0), r * HP + hp))
    done = pl.BlockSpec((ATT_BLK, LANES), lambda hp, r, n: (jnp.maximum(n - 1, 0), r * HP + hp))
    bspec = pl.BlockSpec((2, ATT_BLK, 2 * ATT_BLK), lambda hp, r, n: (hp, 0, 0))
    dq, dk, dv, db = pl.pallas_call(
        body, name=name, grid=(HP, d, nb + 1),
        in_specs=[cur, prev, cur, prev, cur, bspec, cur, cur, cur],
        out_specs=[cur, done, done, bspec],
        out_shape=[_sds((L, d * D)), _sds((L, d * D)), _sds((L, d * D)), _sds((H, ATT_BLK, 2 * ATT_BLK))],
        scratch_shapes=[pltpu.VMEM((ATT_BLK, LANES), f32), pltpu.VMEM((ATT_BLK, LANES), f32),
                        pltpu.VMEM((2 * ATT_BLK, LANES), f32), pltpu.VMEM((2 * ATT_BLK, LANES), f32)],
        compiler_params=_params(("arbitrary", "arbitrary", "arbitrary")),
    )(view(q), view(k), view(k), view(v), view(v), bias, view(att), view(datt), view(lse_tot))
    return dq.reshape(S, D), dk.reshape(S, D), dv.reshape(S, D), db


def _attn_combine(os_, lses, name):
    def fn(rv, vv):
        o0, o1, o2, l0, l1, l2 = rv
        m = jnp.maximum(jnp.maximum(l0, l1), l2)
        e0, e1, e2 = jnp.exp(l0 - m), jnp.exp(l1 - m), jnp.exp(l2 - m)
        tot = e0 + e1 + e2
        return [(e0 * o0 + e1 * o1 + e2 * o2) / tot, m + jnp.log(tot)], []
    w = os_[0].shape[1]
    (att, lse), _ = _rowwise(name, fn, list(os_) + list(lses), [], [(w, f32), (w, f32)], [])
    return att, lse


ANY = pl.BlockSpec(memory_space=pl.ANY)


def _all_gather(v, name):
    def body(x_ref, out_ref, send_sems, recv_sems, local_sem):
        x, y, c = lax.axis_index("x"), lax.axis_index("y"), lax.axis_index("c")
        me, sibling = (x, y, c), (x, y, 1 - c)
        chips = [(1 - x, y), (x, 1 - y), (1 - x, 1 - y)]

        def slot(px, py, pc):
            return out_ref.at[4 * px + 2 * py + pc]

        def copy(k, block, to, src=None):
            return pltpu.make_async_remote_copy(
                src_ref=slot(*block) if src is None else src, dst_ref=slot(*block),
                send_sem=send_sems.at[k], recv_sem=recv_sems.at[k], device_id=to, device_id_type=MESH)

        mine = pltpu.make_async_copy(x_ref, slot(*me), local_sem)
        mine.start()
        first = [copy(0, me, sibling, src=x_ref)]
        first += [copy(1 + j, me, (*chip, c), src=x_ref) for j, chip in enumerate(chips)]
        for cp in first:
            cp.start()
        passed = [copy(4 + j, (*chip, c), sibling) for j, chip in enumerate(chips)]
        for j, chip in enumerate(chips):
            copy(1 + j, (*chip, c), me).wait_recv()
            passed[j].start()
        copy(0, sibling, me).wait_recv()
        for j, chip in enumerate(chips):
            copy(4 + j, (*chip, 1 - c), me).wait_recv()
        for cp in first + passed:
            cp.wait_send()
        mine.wait()

    return pl.pallas_call(
        body, name=name, out_shape=_sds((N_DEV,) + v.shape, v.dtype), in_specs=[ANY], out_specs=ANY,
        scratch_shapes=[pltpu.SemaphoreType.DMA((7,)), pltpu.SemaphoreType.DMA((7,)), pltpu.SemaphoreType.DMA],
    )(v)


def _rs_sibling(part, name):
    def body(p_ref, out_ref, send_sems, recv_sems):
        x, y, c = lax.axis_index("x"), lax.axis_index("y"), lax.axis_index("c")
        cps = [pltpu.make_async_remote_copy(
            src_ref=p_ref.at[k, 1 - c], dst_ref=out_ref.at[k], send_sem=send_sems.at[k], recv_sem=recv_sems.at[k],
            device_id=(x, y, 1 - c), device_id_type=MESH) for k in range(4)]
        for cp in cps:
            cp.start()
        for cp in cps:
            cp.wait()

    return pl.pallas_call(
        body, name=name, out_shape=_sds((4,) + part.shape[2:], part.dtype), in_specs=[ANY], out_specs=ANY,
        scratch_shapes=[pltpu.SemaphoreType.DMA((4,)), pltpu.SemaphoreType.DMA((4,))],
    )(part)


def _rs_chips(t, name):
    def body(t_ref, out_ref, send_sems, recv_sems, local_sem):
        x, y, c = lax.axis_index("x"), lax.axis_index("y"), lax.axis_index("c")
        mine = 2 * x + y
        local = pltpu.make_async_copy(t_ref.at[mine], out_ref.at[mine], local_sem)
        local.start()
        chips = [(1 - x, y), (x, 1 - y), (1 - x, 1 - y)]
        cps = [pltpu.make_async_remote_copy(
            src_ref=t_ref.at[2 * px + py], dst_ref=out_ref.at[mine], send_sem=send_sems.at[j], recv_sem=recv_sems.at[j],
            device_id=(px, py, c), device_id_type=MESH) for j, (px, py) in enumerate(chips)]
        for cp in cps:
            cp.start()
        for cp in cps:
            cp.wait()
        local.wait()

    return pl.pallas_call(
        body, name=name, out_shape=_sds(t.shape, t.dtype), in_specs=[ANY], out_specs=ANY,
        scratch_shapes=[pltpu.SemaphoreType.DMA((3,)), pltpu.SemaphoreType.DMA((3,)), pltpu.SemaphoreType.DMA],
    )(t)


def _pair_add(part, recv, c_arr, name):
    _, _, R, C = part.shape
    tr = _tile(R, PACK_ROW_TILE, 8)

    def body(c_ref, p_ref, r_ref, o_ref):
        o_ref[...] = p_ref[...] + r_ref[...]

    return pl.pallas_call(
        body, name=name,
        grid_spec=pltpu.PrefetchScalarGridSpec(
            num_scalar_prefetch=1, grid=(4, R // tr),
            in_specs=[pl.BlockSpec((None, None, tr, C), lambda k, i, c_ref: (k, c_ref[0], i, 0)),
                      pl.BlockSpec((None, tr, C), lambda k, i, c_ref: (k, i, 0))],
            out_specs=pl.BlockSpec((None, tr, C), lambda k, i, c_ref: (k, i, 0))),
        out_shape=_sds((4, R, C)),
        compiler_params=_params(("parallel", "parallel")),
    )(c_arr, part, recv)


def _sum_slots(t, name):
    n, R, C = t.shape
    tr = _tile(R, PACK_ROW_TILE, 8)

    def body(t_ref, o_ref):
        acc = t_ref[0]
        for k in range(1, n):
            acc = acc + t_ref[k]
        o_ref[...] = acc

    return pl.pallas_call(
        body, name=name, grid=(R // tr,),
        in_specs=[pl.BlockSpec((n, tr, C), lambda i: (0, i, 0))],
        out_specs=pl.BlockSpec((tr, C), lambda i: (i, 0)), out_shape=_sds((R, C)),
        compiler_params=_params(("parallel",)),
    )(t)


def _reduce_scatter(part, c_arr, name):
    _, R, C = part.shape
    part4 = part.reshape(4, 2, R, C)
    recv = _rs_sibling(part4, name + "_sibling")
    t = _pair_add(part4, recv, c_arr, name + "_pair")
    got = _rs_chips(t, name + "_chips")
    return _sum_slots(got, name + "_sum")


def _adamw(w, g, m, v, name):
    def fn(rv, vv):
        wv, gv, mv, vvv = rv
        m2 = ADAM_B1 * mv + (1.0 - ADAM_B1) * gv
        v2 = ADAM_B2 * vvv + (1.0 - ADAM_B2) * jnp.square(gv)
        m_hat = m2 / (1.0 - ADAM_B1 ** ADAM_STEP)
        v_hat = v2 / (1.0 - ADAM_B2 ** ADAM_STEP)
        delta = -ADAM_LR * (m_hat / (jnp.sqrt(v_hat) + ADAM_EPS) + ADAM_WD * wv)
        return [delta, m2, v2], []
    c = w.shape[1]
    (delta, m2, v2), _ = _rowwise(name, fn, [w, g, m, v], [], [(c, f32)] * 3, [])
    return delta, m2, v2


BIG = (("hy_w_in", "col"), ("hy_w_out", "row"), ("cv_w_pw1", "col"), ("cv_w_pw2", "row"),
       ("ffn_w_gate", "col"), ("ffn_w_up", "col"), ("ffn_w_down", "row"))


def _pack_rows(n_elems):
    rows = -(-n_elems // PACK_COLS)
    return -(-rows // PACK_ROW_TILE) * PACK_ROW_TILE


def _pack_shards(shards, dtype):
    flat = jnp.concatenate([s.astype(dtype).reshape(-1) for s in shards])
    rows = _pack_rows(flat.shape[0])
    flat = jnp.pad(flat, (0, rows * PACK_COLS - flat.shape[0]))
    return flat.reshape(rows, PACK_COLS)


def _unpack_full(gathered, shard_shapes):
    flat = gathered.reshape(N_DEV, -1)
    out, off = [], 0
    for (_, kind), shp in zip(BIG, shard_shapes):
        n = shp[0] * shp[1] * shp[2]
        seg = flat[:, off:off + n].reshape((N_DEV,) + tuple(shp))
        off += n
        if kind == "col":
            out.append(jnp.transpose(seg, (1, 2, 0, 3)).reshape(shp[0], shp[1], N_DEV * shp[2]))
        else:
            out.append(jnp.transpose(seg, (1, 0, 2, 3)).reshape(shp[0], N_DEV * shp[1], shp[2]))
    return out


def _pack_partials(fulls, shard_shapes):
    segs = []
    for (_, kind), shp, g in zip(BIG, shard_shapes, fulls):
        if kind == "col":
            t = jnp.transpose(g.reshape(shp[0], shp[1], N_DEV, shp[2]), (2, 0, 1, 3))
        else:
            t = jnp.transpose(g.reshape(shp[0], N_DEV, shp[1], shp[2]), (1, 0, 2, 3))
        segs.append(t.reshape(N_DEV, -1))
    flat = jnp.concatenate(segs, axis=1)
    rows = _pack_rows(flat.shape[1])
    flat = jnp.pad(flat, ((0, 0), (0, rows * PACK_COLS - flat.shape[1])))
    return flat.reshape(N_DEV, rows, PACK_COLS)


def _unpack_shards(packed, shard_shapes):
    flat = packed.reshape(-1)
    out, off = [], 0
    for shp in shard_shapes:
        n = shp[0] * shp[1] * shp[2]
        out.append(flat[off:off + n].reshape(shp))
        off += n
    return out


class _VecPack:
    def __init__(self, shapes):
        self.shapes = [tuple(s) for s in shapes]
        self.sizes = [int(np.prod(s)) for s in self.shapes]
        total = sum(self.sizes)
        self.rows = -(-(-(-total // LANES)) // 8) * 8
        self.total = total

    def pack(self, arrays):
        flat = jnp.concatenate([a.astype(f32).reshape(-1) for a in arrays])
        flat = jnp.pad(flat, (0, self.rows * LANES - self.total))
        return flat.reshape(self.rows, LANES)

    def unpack(self, packed):
        flat = packed.reshape(-1)
        out, off = [], 0
        for shp, n in zip(self.shapes, self.sizes):
            out.append(flat[off:off + n].reshape(shp))
            off += n
        return out


def _row(v):
    return v.reshape(1, -1)


def _pad_lanes(v):
    v = v.reshape(1, -1)
    return jnp.pad(v, ((0, 0), (0, LANES - v.shape[1])))


def _ffn_fwd(h, w_gu, w_down, tag):
    F = w_down.shape[0]
    au = _mm(h, w_gu, name=f"ffn_gu_{tag}")
    (f,), _ = _rowwise(f"swiglu_{tag}", lambda rv, vv: ([_silu(rv[0]) * rv[1]], []),
                       [(au, 0, F), (au, 1, F)], [], [(F, f32)], [])
    out = _mm(f, w_down, name=f"ffn_down_{tag}")
    return out, (au, f)


def _ffn_bwd(h, w_gu, w_down, saved, dout, tag):
    au, f = saved
    F = w_down.shape[0]
    df = _mm(dout, w_down, tb=True, name=f"ffn_down_dx_{tag}")
    dw_down = _mm(f, dout, ta=True, name=f"ffn_down_dw_{tag}")

    def fn(rv, vv):
        a, u, d = rv
        _, vjp = jax.vjp(lambda a_, u_: _silu(a_) * u_, a, u)
        da, du = vjp(d)
        return [da, du], []

    (da, du), _ = _rowwise(f"swiglu_bwd_{tag}", fn, [(au, 0, F), (au, 1, F), df], [], [(F, f32), (F, f32)], [])
    dau = jnp.concatenate([da, du], axis=1)
    dh = _mm(dau, w_gu, tb=True, name=f"ffn_gu_dx_{tag}")
    dw_gu = _mm(h, dau, ta=True, name=f"ffn_gu_dw_{tag}")
    return dh, dw_gu, dw_down


def _local_step(x, target, mod, W, small):
    S, D = x.shape
    w_in, w_out, w_pw1, w_pw2, w_gate, w_up, w_down = W
    di = small["hy_ssm_norm_g"].shape[-1]
    nh = small["hy_dt_bias"].shape[-1]
    cd = small["hy_conv_b"].shape[-1]
    F = w_down.shape[1]
    m = [[_row(mod[i, j]) for j in range(6)] for i in range(2)]

    o = 0
    seg = {}
    for nm, wd in (("z", di), ("xbc", cd), ("dt", nh), ("q0", D), ("q1", D), ("q2", D), ("k", D), ("v", D)):
        seg[nm] = (o, wd)
        o += wd
    wseg = {nm: w_in[0][:, a:a + wd] for nm, (a, wd) in seg.items()}
    wseg["dt"] = jnp.pad(wseg["dt"], ((0, 0), (0, LANES - nh)))
    w_out_y, w_out_a = w_out[0][:di], w_out[0][di:]
    w_gu = [jnp.concatenate([w_gate[i], w_up[i]], axis=1) for i in range(2)]

    g_mix = [_row(small["norm_mix_g"][i]) for i in range(2)]
    g_ffn = [_row(small["norm_ffn_g"][i]) for i in range(2)]
    conv_w, conv_b = small["hy_conv_w_full"], _row(small["hy_conv_b"][0])
    dt_bias, a_log, d_skip = (_pad_lanes(small[k][0]) for k in ("hy_dt_bias", "hy_a_log", "hy_d_skip"))
    g_ssm = _row(small["hy_ssm_norm_g"][0])
    onehot = jnp.asarray(_bucket_onehot())
    rel_t = small["rel_table"].T
    H = D // HEAD_DIM
    bias = [_exact_mm(rel_t[gi * H:(gi + 1) * H], onehot[gi], name=f"rel_bias_{gi}")
            .reshape(H, ATT_BLK, 2 * ATT_BLK) for gi in range(3)]

    h1 = _adaln_fwd(x, g_mix[0], m[0][1], m[0][0], "adaln_mix0")
    proj = {nm: _mm(h1, wseg[nm], name=f"in_{nm}") for nm in seg}
    xbc_pre, xbc = _conv_fwd(proj["xbc"], conv_w, conv_b, silu=True, name="ssm_conv")
    y, hin = _ssd_fwd(xbc, proj["dt"], dt_bias, a_log, d_skip, di, "ssd_fwd")
    (yg,), _ = _rowwise("ssm_gate", lambda rv, vv: ([_gate_f(rv[0], rv[1], vv[0])], []),
                        [y, proj["z"]], [g_ssm], [(di, f32)], [])
    og = [_attn_fwd(proj[f"q{gi}"], proj["k"], proj["v"], bias[gi], d, f"attn_fwd_{gi}")
          for gi, d in enumerate(ATT_DILATIONS)]
    att, lse_tot = _attn_combine([a for a, _ in og], [b for _, b in og], "attn_combine")
    mix0 = _mm(att, w_out_a, add=_mm(yg, w_out_y, name="out_y"), name="out_a")
    x1 = _resid_fwd(x, m[0][2], mix0, "resid_mix0")
    h2 = _adaln_fwd(x1, g_ffn[0], m[0][4], m[0][3], "adaln_ffn0")
    f0, ffn0_saved = _ffn_fwd(h2, w_gu[0], w_down[0], "0")
    x2 = _resid_fwd(x1, m[0][5], f0, "resid_ffn0")

    h3 = _adaln_fwd(x2, g_mix[1], m[1][1], m[1][0], "adaln_mix1")
    pw1 = _mm(h3, w_pw1[0], bias=_row(small["cv_b_pw1_full"]), name="cv_pw1")
    (u,), _ = _rowwise("cv_glu", lambda rv, vv: ([rv[0] * jax.nn.sigmoid(rv[1])], []),
                       [(pw1, 0, D), (pw1, 1, D)], [], [(D, f32)], [])
    (u2,) = _conv_fwd(u, small["cv_w_dw_full"], _row(small["cv_b_dw_full"]), silu=False, name="cv_dw")
    ln_g, ln_b = _row(small["cv_ln_g_full"]), _row(small["cv_ln_b_full"])
    (u3,), _ = _rowwise("cv_lnsilu", lambda rv, vv: ([_lnsilu_f(rv[0], vv[0], vv[1])], []),
                        [u2], [ln_g, ln_b], [(D, f32)], [])
    mix1 = _mm(u3, w_pw2[0], bias=_row(small["cv_b_pw2_full"]), name="cv_pw2")
    x3 = _resid_fwd(x2, m[1][2], mix1, "resid_mix1")
    h4 = _adaln_fwd(x3, g_ffn[1], m[1][4], m[1][3], "adaln_ffn1")
    f1, ffn1_saved = _ffn_fwd(h4, w_gu[1], w_down[1], "1")
    x4 = _resid_fwd(x3, m[1][5], f1, "resid_ffn1")

    g_fin = _row(small["final_norm_g"])

    def final_fn(rv, vv):
        xv, tv = rv
        yv, vjp = jax.vjp(_rms, xv, vv[0])
        err = yv - tv
        dx, dg = vjp(err / D)
        part = 0.5 * jnp.sum(jnp.mean(err * err, -1, keepdims=True), 0, keepdims=True)
        return [dx], [dg, jnp.broadcast_to(part, (1, LANES))]

    (dx4,), (d_fin, loss) = _rowwise("loss_head", final_fn, [x4, target], [g_fin], [(D, f32)], [D, LANES])

    dmod = [[None] * 6 for _ in range(2)]
    d_norm_mix, d_norm_ffn = [None, None], [None, None]
    d_gu, d_down = [None, None], [None, None]

    df1, (dmod[1][5], _) = _resid_bwd(dx4, f1, m[1][5], "resid_ffn1_bwd")
    dh4, d_gu[1], d_down[1] = _ffn_bwd(h4, w_gu[1], w_down[1], ffn1_saved, df1, "1")
    dx3, (d_norm_ffn[1], dmod[1][4], dmod[1][3]) = _adaln_bwd(x3, g_ffn[1], m[1][4], m[1][3], dh4, dx4, "adaln_ffn1_bwd")
    dmix1, (dmod[1][2], d_b_pw2) = _resid_bwd(dx3, mix1, m[1][2], "resid_mix1_bwd")
    du3 = _mm(dmix1, w_pw2[0], tb=True, name="cv_pw2_dx")
    d_pw2 = _mm(u3, dmix1, ta=True, name="cv_pw2_dw")

    def lnsilu_bwd(rv, vv):
        _, vjp = jax.vjp(_lnsilu_f, rv[0], vv[0], vv[1])
        du, dg, db = vjp(rv[1])
        return [du], [dg, db]

    (du2,), (d_ln_g, d_ln_b) = _rowwise("cv_lnsilu_bwd", lnsilu_bwd, [u2, du3], [ln_g, ln_b], [(D, f32)], [D, D])
    du, d_w_dw, d_b_dw = _conv_bwd(u, small["cv_w_dw_full"], du2, None, silu=False, name="cv_dw_bwd")

    def glu_bwd(rv, vv):
        a, gt, d = rv
        _, vjp = jax.vjp(lambda a_, g_: a_ * jax.nn.sigmoid(g_), a, gt)
        da, dg = vjp(d)
        return [da, dg], [jnp.sum(da, 0, keepdims=True), jnp.sum(dg, 0, keepdims=True)]

    (dpa, dpg), (d_b1a, d_b1g) = _rowwise("cv_glu_bwd", glu_bwd, [(pw1, 0, D), (pw1, 1, D), du], [],
                                           [(D, f32), (D, f32)], [D, D])
    dpw1 = jnp.concatenate([dpa, dpg], axis=1)
    d_b_pw1 = jnp.concatenate([d_b1a, d_b1g], axis=1)
    dh3 = _mm(dpw1, w_pw1[0], tb=True, name="cv_pw1_dx")
    d_pw1 = _mm(h3, dpw1, ta=True, name="cv_pw1_dw")
    dx2, (d_norm_mix[1], dmod[1][1], dmod[1][0]) = _adaln_bwd(x2, g_mix[1], m[1][1], m[1][0], dh3, dx3, "adaln_mix1_bwd")

    df0, (dmod[0][5], _) = _resid_bwd(dx2, f0, m[0][5], "resid_ffn0_bwd")
    dh2, d_gu[0], d_down[0] = _ffn_bwd(h2, w_gu[0], w_down[0], ffn0_saved, df0, "0")
    dx1, (d_norm_ffn[0], dmod[0][4], dmod[0][3]) = _adaln_bwd(x1, g_ffn[0], m[0][4], m[0][3], dh2, dx2, "adaln_ffn0_bwd")
    dmix0, (dmod[0][2], _) = _resid_bwd(dx1, mix0, m[0][2], "resid_mix0_bwd")
    dyg = _mm(dmix0, w_out_y, tb=True, name="out_y_dx")
    datt = _mm(dmix0, w_out_a, tb=True, name="out_a_dx")
    d_out = jnp.concatenate([_mm(yg, dmix0, ta=True, name="out_y_dw"), _mm(att, dmix0, ta=True, name="out_a_dw")], axis=0)

    dq, dks, dvs, dbs = [], [], [], []
    for gi, d in enumerate(ATT_DILATIONS):
        a, b, c_, e = _attn_bwd(proj[f"q{gi}"], proj["k"], proj["v"], bias[gi], att, datt, lse_tot, d, f"attn_bwd_{gi}")
        dq.append(a)
        dks.append(b)
        dvs.append(c_)
        dbs.append(e)
    dk = _add3(*dks, "attn_dk")
    dv = _add3(*dvs, "attn_dv")
    d_rel = jnp.concatenate(
        [_exact_mm(dbs[gi].reshape(H, -1), onehot[gi], tb=True, name=f"rel_grad_{gi}") for gi in range(3)], axis=0).T

    def gate_bwd(rv, vv):
        _, vjp = jax.vjp(_gate_f, rv[0], rv[1], vv[0])
        dy_, dz_, dg_ = vjp(rv[2])
        return [dy_, dz_], [dg_]

    (dy, dz), (d_g_ssm,) = _rowwise("ssm_gate_bwd", gate_bwd, [y, proj["z"], dyg], [g_ssm], [(di, f32), (di, f32)], [di])
    dxbc, ddtraw, d_a_log, d_dskip, d_dt_bias = _ssd_bwd(xbc, proj["dt"], dt_bias, a_log, d_skip, hin, dy, di, "ssd_bwd")
    dxbc_pre, d_conv_w, d_conv_b = _conv_bwd(proj["xbc"], conv_w, dxbc, xbc_pre, silu=True, name="ssm_conv_bwd")

    dseg = {"z": dz, "xbc": dxbc_pre, "dt": ddtraw, "q0": dq[0], "q1": dq[1], "q2": dq[2], "k": dk, "v": dv}
    dh1 = None
    d_in_parts = []
    for nm in seg:
        dh1 = _mm(dseg[nm], wseg[nm], tb=True, add=dh1, name=f"in_{nm}_dx")
        dwp = _mm(h1, dseg[nm], ta=True, name=f"in_{nm}_dw")
        d_in_parts.append(dwp[:, :nh] if nm == "dt" else dwp)
    d_in = jnp.concatenate(d_in_parts, axis=1)
    dx0, (d_norm_mix[0], dmod[0][1], dmod[0][0]) = _adaln_bwd(x, g_mix[0], m[0][1], m[0][0], dh1, dx1, "adaln_mix0_bwd")

    big = [d_in[None], d_out[None], d_pw1[None], d_pw2[None],
           jnp.stack([d_gu[0][:, :F], d_gu[1][:, :F]]), jnp.stack([d_gu[0][:, F:], d_gu[1][:, F:]]),
           jnp.stack(d_down)]
    smallg = dict(
        loss=loss, dmod=jnp.stack([jnp.concatenate(dmod[i], axis=1)[0] for i in range(2)]),
        norm_mix_g=jnp.concatenate(d_norm_mix, axis=0), norm_ffn_g=jnp.concatenate(d_norm_ffn, axis=0),
        hy_conv_w=d_conv_w, hy_conv_b=d_conv_b, hy_dt_bias=d_dt_bias[:, :nh], hy_a_log=d_a_log[:, :nh],
        hy_d_skip=d_dskip[:, :nh], hy_ssm_norm_g=d_g_ssm, rel_table=d_rel,
        cv_b_pw1=d_b_pw1, cv_w_dw=d_w_dw, cv_b_dw=d_b_dw, cv_ln_g=d_ln_g, cv_ln_b=d_ln_b, cv_b_pw2=d_b_pw2,
        final_norm_g=d_fin)
    return dx0, big, smallg


SMALL_GRAD_ORDER = ("loss", "dmod", "norm_mix_g", "norm_ffn_g", "hy_conv_w", "hy_conv_b", "hy_dt_bias", "hy_a_log",
                    "hy_d_skip", "hy_ssm_norm_g", "rel_table", "cv_b_pw1", "cv_w_dw", "cv_b_dw", "cv_ln_g", "cv_ln_b",
                    "cv_b_pw2", "final_norm_g")


def kernel(x, c, ada_w, ada_b, norm_mix_g, norm_ffn_g, hy_w_in, hy_conv_w, hy_conv_b, hy_dt_bias, hy_a_log, hy_d_skip, hy_ssm_norm_g, hy_w_out, rel_table, cv_w_pw1, cv_b_pw1, cv_w_dw, cv_b_dw, cv_ln_g, cv_ln_b, cv_w_pw2, cv_b_pw2, ffn_w_gate, ffn_w_up, ffn_w_down, final_norm_g, loss_target, m_ada_w, m_ada_b, m_norm_mix_g, m_norm_ffn_g, m_hy_w_in, m_hy_conv_w, m_hy_conv_b, m_hy_dt_bias, m_hy_a_log, m_hy_d_skip, m_hy_ssm_norm_g, m_hy_w_out, m_rel_table, m_cv_w_pw1, m_cv_b_pw1, m_cv_w_dw, m_cv_b_dw, m_cv_ln_g, m_cv_ln_b, m_cv_w_pw2, m_cv_b_pw2, m_ffn_w_gate, m_ffn_w_up, m_ffn_w_down, m_final_norm_g, v_ada_w, v_ada_b, v_norm_mix_g, v_norm_ffn_g, v_hy_w_in, v_hy_conv_w, v_hy_conv_b, v_hy_dt_bias, v_hy_a_log, v_hy_d_skip, v_hy_ssm_norm_g, v_hy_w_out, v_rel_table, v_cv_w_pw1, v_cv_b_pw1, v_cv_w_dw, v_cv_b_dw, v_cv_ln_g, v_cv_ln_b, v_cv_w_pw2, v_cv_b_pw2, v_ffn_w_gate, v_ffn_w_up, v_ffn_w_down, v_final_norm_g):
    names = ("ada_w", "ada_b", "norm_mix_g", "norm_ffn_g", "hy_w_in", "hy_conv_w", "hy_conv_b", "hy_dt_bias", "hy_a_log",
             "hy_d_skip", "hy_ssm_norm_g", "hy_w_out", "rel_table", "cv_w_pw1", "cv_b_pw1", "cv_w_dw", "cv_b_dw", "cv_ln_g",
             "cv_ln_b", "cv_w_pw2", "cv_b_pw2", "ffn_w_gate", "ffn_w_up", "ffn_w_down", "final_norm_g")
    w = dict(zip(names, (ada_w, ada_b, norm_mix_g, norm_ffn_g, hy_w_in, hy_conv_w, hy_conv_b, hy_dt_bias, hy_a_log, hy_d_skip,
                         hy_ssm_norm_g, hy_w_out, rel_table, cv_w_pw1, cv_b_pw1, cv_w_dw, cv_b_dw, cv_ln_g, cv_ln_b, cv_w_pw2,
                         cv_b_pw2, ffn_w_gate, ffn_w_up, ffn_w_down, final_norm_g)))
    mom = dict(zip(names, (m_ada_w, m_ada_b, m_norm_mix_g, m_norm_ffn_g, m_hy_w_in, m_hy_conv_w, m_hy_conv_b, m_hy_dt_bias,
                           m_hy_a_log, m_hy_d_skip, m_hy_ssm_norm_g, m_hy_w_out, m_rel_table, m_cv_w_pw1, m_cv_b_pw1, m_cv_w_dw,
                           m_cv_b_dw, m_cv_ln_g, m_cv_ln_b, m_cv_w_pw2, m_cv_b_pw2, m_ffn_w_gate, m_ffn_w_up, m_ffn_w_down,
                           m_final_norm_g)))
    vel = dict(zip(names, (v_ada_w, v_ada_b, v_norm_mix_g, v_norm_ffn_g, v_hy_w_in, v_hy_conv_w, v_hy_conv_b, v_hy_dt_bias,
                           v_hy_a_log, v_hy_d_skip, v_hy_ssm_norm_g, v_hy_w_out, v_rel_table, v_cv_w_pw1, v_cv_b_pw1, v_cv_w_dw,
                           v_cv_b_dw, v_cv_ln_g, v_cv_ln_b, v_cv_w_pw2, v_cv_b_pw2, v_ffn_w_gate, v_ffn_w_up, v_ffn_w_down,
                           v_final_norm_g)))
    S, D = x.shape[1], x.shape[2]
    ax, ay, ac = lax.axis_index("x"), lax.axis_index("y"), lax.axis_index("c")
    me = 4 * ax + 2 * ay + ac
    c_arr = jnp.reshape(ac, (1,)).astype(jnp.int32)
    nmod = ada_w.shape[2]

    shard_shapes = [w[nm].shape for nm, _ in BIG]
    gathered = _all_gather(_pack_shards([w[nm] for nm, _ in BIG], bf16), "gather_weights")
    W = _unpack_full(gathered, shard_shapes)

    sharded_small = ("hy_conv_w", "cv_b_pw1", "cv_w_dw", "cv_b_dw", "cv_ln_g", "cv_ln_b", "cv_b_pw2")
    vp = _VecPack([c.shape] + [w[nm].shape for nm in sharded_small])
    sg = _all_gather(vp.pack([c] + [w[nm] for nm in sharded_small]), "gather_vectors")
    parts = [vp.unpack(sg[j]) for j in range(N_DEV)]
    c_all = jnp.concatenate([p[0] for p in parts], axis=0)
    small = {k: w[k] for k in ("norm_mix_g", "norm_ffn_g", "hy_conv_b", "hy_dt_bias", "hy_a_log", "hy_d_skip",
                               "hy_ssm_norm_g", "rel_table", "final_norm_g")}
    for i, nm in enumerate(sharded_small):
        small[nm + "_full"] = jnp.concatenate([p[1 + i][0] for p in parts], axis=-1)

    (cs_all,), _ = _rowwise("ada_silu", lambda rv, vv: ([_silu(rv[0])], []), [c_all], [], [(D, f32)], [])
    b_mine = lax.dynamic_slice_in_dim(ada_b, me * nmod, nmod, axis=1)
    mod_part = jnp.stack([_mm(cs_all, ada_w[i], bias=b_mine[i:i + 1], name=f"ada_mod_{i}") for i in range(2)])
    mod_all = _all_gather(mod_part.reshape(2 * N_DEV, nmod), "gather_mod").reshape(N_DEV, 2, N_DEV, nmod)
    mod_mine = lax.dynamic_index_in_dim(mod_all, me, axis=2, keepdims=False)
    mod = jnp.transpose(mod_mine, (1, 0, 2)).reshape(2, 6, D)

    dx0, big, sgrad = _local_step(x[0], loss_target[0], mod, W, small)

    gp = _VecPack([sgrad[k].shape for k in SMALL_GRAD_ORDER])
    g_all = _all_gather(gp.pack([sgrad[k] for k in SMALL_GRAD_ORDER]), "gather_small_grads")
    tot = dict(zip(SMALL_GRAD_ORDER, gp.unpack(_sum_slots(g_all, "sum_small_grads"))))
    dmod_all = jnp.stack([gp.unpack(g_all[j])[1] for j in range(N_DEV)])
    loss = tot["loss"][0, 0]

    grads = {}
    dmod_mine = lax.dynamic_slice_in_dim(dmod_all, me * nmod, nmod, axis=2)
    grads["ada_w"] = jnp.stack([_mm(cs_all, dmod_mine[:, i], ta=True, name=f"ada_w_grad_{i}") for i in range(2)])
    grads["ada_b"] = tot["dmod"]
    grads["norm_mix_g"], grads["norm_ffn_g"] = tot["norm_mix_g"], tot["norm_ffn_g"]
    grads["hy_conv_b"] = tot["hy_conv_b"]
    grads["hy_dt_bias"] = tot["hy_dt_bias"]
    grads["hy_a_log"] = tot["hy_a_log"]
    grads["hy_d_skip"] = tot["hy_d_skip"]
    grads["hy_ssm_norm_g"] = tot["hy_ssm_norm_g"]
    grads["rel_table"] = tot["rel_table"]
    grads["final_norm_g"] = tot["final_norm_g"][0]
    for nm in sharded_small:
        n = w[nm].shape[-1]
        grads[nm] = lax.dynamic_slice_in_dim(tot[nm], me * n, n, axis=1).reshape(w[nm].shape)

    g_big = _unpack_shards(_reduce_scatter(_pack_partials(big, shard_shapes), c_arr, "rs"), shard_shapes)
    for (nm, _), g in zip(BIG, g_big):
        grads[nm] = g

    delta, new_m, new_v = {}, {}, {}
    for nm in ("ada_w",) + tuple(n for n, _ in BIG):
        shp = w[nm].shape
        two = lambda t: t.reshape(-1, shp[-1])
        d_, m_, v_ = _adamw(two(w[nm]), two(grads[nm]), two(mom[nm]), two(vel[nm]), f"adamw_{nm}")
        delta[nm], new_m[nm], new_v[nm] = d_.reshape(shp), m_.reshape(shp), v_.reshape(shp)
    rest = [nm for nm in names if nm not in delta]
    sp = _VecPack([w[nm].shape for nm in rest])
    packs = [sp.pack([t[nm] for nm in rest]) for t in (w, grads, mom, vel)]
    d_, m_, v_ = _adamw(*packs, "adamw_small")
    for nm, a, b, e in zip(rest, sp.unpack(d_), sp.unpack(m_), sp.unpack(v_)):
        delta[nm], new_m[nm], new_v[nm] = a, b, e

    return (loss, dx0[None], *[grads[n] for n in names], *[delta[n] for n in names],
            *[new_m[n] for n in names], *[new_v[n] for n in names])
```

```python
import functools
import math

import numpy as np
import jax
import jax.numpy as jnp
from jax import lax
from jax.experimental import pallas as pl
from jax.experimental.pallas import tpu as pltpu

f32 = jnp.float32
bf16 = jnp.bfloat16
EPS = 1e-6
N_DEV = 8
LANES = 128
SSM_STATE = 128
SSM_CHUNK = 128
SSM_GROUPS = 4
HEAD_DIM = 64
ATT_BLK = 128
ATT_DILATIONS = (1, 4, 16)
REL_BUCKETS = 32
REL_MAX_DIST = 2048
ADAM_LR, ADAM_B1, ADAM_B2, ADAM_EPS, ADAM_WD, ADAM_STEP = 0.001, 0.9, 0.999, 1e-08, 0.01, 10
PACK_COLS = 1024
PACK_ROW_TILE = 256
MESH = pl.DeviceIdType.MESH
VMEM_LIMIT = 48 * 1024 * 1024


def _sds(shape, dtype=f32):
    return jax.ShapeDtypeStruct(tuple(shape), dtype)


def _tile(n, cap, mult):
    best = None
    t = mult
    while t <= min(n, cap):
        if n % t == 0:
            best = t
        t += mult
    return best if best is not None else n


def _params(sem):
    return pltpu.CompilerParams(dimension_semantics=sem, vmem_limit_bytes=VMEM_LIMIT)


def _mm(a, b, *, name, ta=False, tb=False, bias=None, add=None, tm_cap=512, tn_cap=512, tk_cap=1024):
    if ta:
        K, M = a.shape
    else:
        M, K = a.shape
    if tb:
        N, K2 = b.shape
    else:
        K2, N = b.shape
    assert K == K2, (a.shape, b.shape, ta, tb)
    tm = _tile(M, tm_cap, LANES)
    tn = _tile(N, tn_cap, LANES)
    tk = _tile(K, tk_cap, LANES)
    nk = K // tk
    has_bias, has_add = bias is not None, add is not None
    dn = (((0 if ta else 1,), (1 if tb else 0,)), ((), ()))

    def body(*refs):
        a_ref, b_ref = refs[0], refs[1]
        pos = 2
        bias_ref = add_ref = None
        if has_bias:
            bias_ref = refs[pos]
            pos += 1
        if has_add:
            add_ref = refs[pos]
            pos += 1
        o_ref = refs[pos]
        k = pl.program_id(2)
        part = lax.dot_general(a_ref[...].astype(bf16), b_ref[...].astype(bf16), dn, preferred_element_type=f32)

        def finish(r):
            if has_bias:
                r = r + bias_ref[...]
            if has_add:
                r = r + add_ref[...]
            o_ref[...] = r

        if nk == 1:
            finish(part)
        else:
            acc_ref = refs[pos + 1]

            @pl.when(k == 0)
            def _():
                acc_ref[...] = part

            @pl.when((k > 0) & (k < nk - 1))
            def _():
                acc_ref[...] += part

            @pl.when(k == nk - 1)
            def _():
                finish(acc_ref[...] + part)

    in_specs = [
        pl.BlockSpec((tk, tm), lambda i, j, k: (k, i)) if ta else pl.BlockSpec((tm, tk), lambda i, j, k: (i, k)),
        pl.BlockSpec((tn, tk), lambda i, j, k: (j, k)) if tb else pl.BlockSpec((tk, tn), lambda i, j, k: (k, j)),
    ]
    args = [a, b]
    if has_bias:
        in_specs.append(pl.BlockSpec((1, tn), lambda i, j, k: (0, j)))
        args.append(bias)
    if has_add:
        in_specs.append(pl.BlockSpec((tm, tn), lambda i, j, k: (i, j)))
        args.append(add)
    return pl.pallas_call(
        body, name=name, grid=(M // tm, N // tn, nk), in_specs=in_specs,
        out_specs=pl.BlockSpec((tm, tn), lambda i, j, k: (i, j)), out_shape=_sds((M, N)),
        scratch_shapes=[pltpu.VMEM((tm, tn), f32)] if nk > 1 else [],
        compiler_params=_params(("parallel", "parallel", "arbitrary")),
    )(*args)


def _rowwise(name, fn, rows, vecs, out_rows, out_accs, *, tr_cap=256, sub=8):
    rows = [r if isinstance(r, tuple) else (r, 0, r.shape[1]) for r in rows]
    R = rows[0][0].shape[0]
    tr = _tile(R, tr_cap, 8)
    sub = sub if tr % sub == 0 else tr
    n_r, n_v, n_or, n_oa = len(rows), len(vecs), len(out_rows), len(out_accs)

    def body(*refs):
        row_refs = refs[:n_r]
        vec_refs = refs[n_r:n_r + n_v]
        orow_refs = refs[n_r + n_v:n_r + n_v + n_or]
        oacc_refs = refs[n_r + n_v + n_or:]
        vv = [r[...] for r in vec_refs]

        def step(s, accs):
            sl = pl.ds(pl.multiple_of(s * sub, sub), sub)
            ro, ao = fn([r[sl, :] for r in row_refs], vv)
            for o_ref, o in zip(orow_refs, ro):
                o_ref[sl, :] = o.astype(o_ref.dtype)
            return tuple(x + y for x, y in zip(accs, ao))

        accs = lax.fori_loop(0, tr // sub, step, tuple(jnp.zeros((1, w), f32) for w in out_accs))
        if n_oa:
            @pl.when(pl.program_id(0) == 0)
            def _():
                for ref in oacc_refs:
                    ref[...] = jnp.zeros_like(ref)

            for ref, x in zip(oacc_refs, accs):
                ref[...] += x

    in_specs = [pl.BlockSpec((tr, w), functools.partial(lambda i, cb: (i, cb), cb=cb)) for (_, cb, w) in rows]
    in_specs += [pl.BlockSpec((1, v.shape[1]), lambda i: (0, 0)) for v in vecs]
    out_specs = [pl.BlockSpec((tr, w), lambda i: (i, 0)) for (w, _) in out_rows]
    out_specs += [pl.BlockSpec((1, w), lambda i: (0, 0)) for w in out_accs]
    out_shape = [_sds((R, w), dt) for (w, dt) in out_rows] + [_sds((1, w)) for w in out_accs]
    res = pl.pallas_call(
        body, name=name, grid=(R // tr,), in_specs=in_specs, out_specs=out_specs, out_shape=out_shape,
        compiler_params=_params(("arbitrary",)),
    )(*[r[0] for r in rows], *vecs)
    return res[:n_or], res[n_or:]


def _silu(x):
    return x * jax.nn.sigmoid(x)


def _rms(x, g):
    return x * lax.rsqrt(jnp.mean(x * x, -1, keepdims=True) + EPS) * g


def _adaln_f(x, g, sc, sh):
    return _rms(x, g) * (1.0 + sc) + sh


def _gate_f(y, z, g):
    return _rms(y * _silu(z), g)


def _lnsilu_f(u, g, b):
    mu = jnp.mean(u, -1, keepdims=True)
    var = jnp.mean(jnp.square(u - mu), -1, keepdims=True)
    return _silu((u - mu) * lax.rsqrt(var + EPS) * g + b)


def _adaln_fwd(x, g, sc, sh, name):
    (h,), _ = _rowwise(name, lambda rv, vv: ([_adaln_f(rv[0], *vv)], []), [x], [g, sc, sh], [(x.shape[1], bf16)], [],
                       sub=16)
    return h


def _adaln_bwd(x, g, sc, sh, dh, dres, name):
    def fn(rv, vv):
        xv, dhv, drv = rv
        _, vjp = jax.vjp(_adaln_f, xv, *vv)
        dx, dg, dsc, dsh = vjp(dhv)
        return [dx + drv], [dg, dsc, dsh]
    w = x.shape[1]
    (dx,), accs = _rowwise(name, fn, [x, dh, dres], [g, sc, sh], [(w, f32)], [w, w, w])
    return dx, accs


def _resid_fwd(x, gate, mix, name):
    (y,), _ = _rowwise(name, lambda rv, vv: ([rv[0] + vv[0] * rv[1]], []), [x, mix], [gate], [(x.shape[1], f32)], [])
    return y


def _resid_bwd(dx, mix, gate, name):
    def fn(rv, vv):
        dxv, mv = rv
        dm = vv[0] * dxv
        return [dm], [jnp.sum(dxv * mv, 0, keepdims=True), jnp.sum(dm, 0, keepdims=True)]
    w = dx.shape[1]
    (dmix,), accs = _rowwise(name, fn, [dx, mix], [gate], [(w, bf16)], [w, w], sub=16)
    return dmix, accs


def _add3(a, b, c, name):
    (y,), _ = _rowwise(name, lambda rv, vv: ([rv[0] + rv[1] + rv[2]], []), [a, b, c], [], [(a.shape[1], bf16)], [],
                       sub=16)
    return y


CONV_HALO = 32


def _conv_fwd(x, w, b, *, silu, name, tr=512):
    S, C = x.shape
    K = w.shape[0]
    H = CONV_HALO
    assert K - 1 <= H and S % tr == 0 and tr % H == 0 and C % LANES == 0
    nh = tr // H

    def body(xp_ref, xc_ref, w_ref, b_ref, *rest):
        outs, scr = rest[:-1], rest[-1]
        i = pl.program_id(1)
        scr[pl.ds(0, H), :] = jnp.where(i > 0, xp_ref[...], 0.0)
        scr[pl.ds(H, tr), :] = xc_ref[...]
        acc = jnp.zeros((tr, LANES), f32) + b_ref[...]
        for k in range(K):
            acc = acc + scr[pl.ds(H - (K - 1) + k, tr), :] * w_ref[pl.ds(k, 1), :]
        outs[0][...] = acc
        if silu:
            outs[1][...] = _silu(acc)

    n_out = 2 if silu else 1
    return pl.pallas_call(
        body, name=name, grid=(C // LANES, S // tr),
        in_specs=[pl.BlockSpec((H, LANES), lambda j, i: (jnp.maximum(i * nh - 1, 0), j)),
                  pl.BlockSpec((tr, LANES), lambda j, i: (i, j)),
                  pl.BlockSpec((K, LANES), lambda j, i: (0, j)),
                  pl.BlockSpec((1, LANES), lambda j, i: (0, j))],
        out_specs=[pl.BlockSpec((tr, LANES), lambda j, i: (i, j))] * n_out,
        out_shape=[_sds((S, C))] * n_out,
        scratch_shapes=[pltpu.VMEM((tr + H, LANES), f32)],
        compiler_params=_params(("parallel", "arbitrary")),
    )(x, x, w, b)


def _conv_bwd(x, w, dact, pre, *, silu, name, dx_dtype=f32, tr=512):
    S, C = x.shape
    K = w.shape[0]
    H = CONV_HALO
    nh = tr // H
    n_i = S // tr
    kp = -(-K // 8) * 8

    def dsilu(p):
        s = jax.nn.sigmoid(p)
        return s * (1.0 + p * (1.0 - s))

    def body(*refs):
        if silu:
            xp_ref, xc_ref, w_ref, dc_ref, dn_ref, pc_ref, pn_ref, dx_ref, dw_ref, db_ref, xs, ds = refs
        else:
            xp_ref, xc_ref, w_ref, dc_ref, dn_ref, dx_ref, dw_ref, db_ref, xs, ds = refs
        i = pl.program_id(1)
        xs[pl.ds(0, H), :] = jnp.where(i > 0, xp_ref[...], 0.0)
        xs[pl.ds(H, tr), :] = xc_ref[...]
        dcur = dc_ref[...]
        dnext = dn_ref[...]
        if silu:
            dcur = dcur * dsilu(pc_ref[...])
            dnext = dnext * dsilu(pn_ref[...])
        ds[pl.ds(0, tr), :] = dcur
        ds[pl.ds(tr, H), :] = jnp.where(i < n_i - 1, dnext, 0.0)
        acc = jnp.zeros((tr, LANES), f32)
        for k in range(K):
            acc = acc + ds[pl.ds(K - 1 - k, tr), :] * w_ref[pl.ds(k, 1), :]
        dx_ref[...] = acc.astype(dx_ref.dtype)

        @pl.when(i == 0)
        def _():
            dw_ref[...] = jnp.zeros_like(dw_ref)
            db_ref[...] = jnp.zeros_like(db_ref)

        for k in range(K):
            dw_ref[pl.ds(k, 1), :] += jnp.sum(dcur * xs[pl.ds(H - (K - 1) + k, tr), :], 0, keepdims=True)
        db_ref[...] += jnp.sum(dcur, 0, keepdims=True)

    prev = pl.BlockSpec((H, LANES), lambda j, i: (jnp.maximum(i * nh - 1, 0), j))
    cur = pl.BlockSpec((tr, LANES), lambda j, i: (i, j))
    nxt = pl.BlockSpec((H, LANES), lambda j, i: (jnp.minimum((i + 1) * nh, n_i * nh - 1), j))
    in_specs = [prev, cur, pl.BlockSpec((K, LANES), lambda j, i: (0, j)), cur, nxt]
    args = [x, x, w, dact, dact]
    if silu:
        in_specs += [cur, nxt]
        args += [pre, pre]
    dx, dw, db = pl.pallas_call(
        body, name=name, grid=(C // LANES, n_i), in_specs=in_specs,
        out_specs=[cur, pl.BlockSpec((kp, LANES), lambda j, i: (0, j)), pl.BlockSpec((1, LANES), lambda j, i: (0, j))],
        out_shape=[_sds((S, C), dx_dtype), _sds((kp, C)), _sds((1, C))],
        scratch_shapes=[pltpu.VMEM((tr + H, LANES), f32), pltpu.VMEM((tr + H, LANES), f32)],
        compiler_params=_params(("parallel", "arbitrary")),
    )(*args)
    return dx, dw[:K], db


def _dot(a, b):
    return jnp.dot(a.astype(bf16), b.astype(bf16), preferred_element_type=f32)


def _dot_nt(a, b):
    return lax.dot_general(a.astype(bf16), b.astype(bf16), (((1,), (1,)), ((), ())), preferred_element_type=f32)


def _dot_tn(a, b):
    return lax.dot_general(a.astype(bf16), b.astype(bf16), (((0,), (0,)), ((), ())), preferred_element_type=f32)


def _softplus(x):
    return jnp.maximum(x, 0.0) + jnp.log(1.0 + jnp.exp(-jnp.abs(x)))


def _tri(q):
    i = lax.broadcasted_iota(jnp.int32, (q, q), 0)
    j = lax.broadcasted_iota(jnp.int32, (q, q), 1)
    return i >= j


def _ssd_prep(dtraw, dt_bias, a_log):
    q = dtraw.shape[0]
    dt = _softplus(dtraw + dt_bias)
    A = -jnp.exp(a_log)
    tri = _tri(q)
    cs = jnp.dot(tri.astype(f32), dt * A, preferred_element_type=f32, precision=lax.Precision.HIGHEST)
    return dt, A, cs, cs.T, tri


def _expand(cols, h0, n, width):
    q = cols.shape[0]
    return jnp.concatenate([jnp.broadcast_to(cols[:, h0 + r:h0 + r + 1], (q, width)) for r in range(n)], axis=1)


def _ssd_fwd(xbc, dtraw, dt_bias, a_log, d_skip, di, name):
    S, CD = xbc.shape
    Q, N, G = SSM_CHUNK, SSM_STATE, SSM_GROUPS
    nc = S // Q
    nh = di // HEAD_DIM
    R = nh // G
    gw = R * HEAD_DIM

    def body(xbc_ref, dt_ref, bias_ref, alog_ref, dsk_ref, y_ref, hin_ref, state):
        c = pl.program_id(0)

        @pl.when(c == 0)
        def _():
            state[...] = jnp.zeros_like(state)

        hin_ref[...] = state[...]
        dt, A, cs, csT, tri = _ssd_prep(dt_ref[...], bias_ref[...], alog_ref[...])
        dsk = dsk_ref[...]
        ecs = jnp.exp(cs)
        dend = jnp.exp(cs[Q - 1:Q, :] - cs)
        elast = jnp.exp(cs[Q - 1:Q, :])
        for g in range(G):
            h0 = g * R
            Bg = xbc_ref[:, pl.ds(di + g * N, N)]
            Cg = xbc_ref[:, pl.ds(di + G * N + g * N, N)]
            xg = xbc_ref[:, pl.ds(g * gw, gw)]
            Hg = state[pl.ds(g * gw, gw), :]
            Gm = _dot_nt(Cg, Bg)
            xdt = xg * _expand(dt, h0, R, HEAD_DIM)
            yoff = _dot_nt(Cg, Hg) * _expand(ecs, h0, R, HEAD_DIM)
            ys = []
            for r in range(R):
                h = h0 + r
                L = jnp.exp(jnp.where(tri, cs[:, h:h + 1] - csT[h:h + 1, :], -jnp.inf))
                ys.append(_dot(Gm * L, xdt[:, r * HEAD_DIM:(r + 1) * HEAD_DIM]))
            y = jnp.concatenate(ys, axis=1) + yoff + xg * _expand(dsk, h0, R, HEAD_DIM)
            y_ref[:, pl.ds(g * gw, gw)] = y
            hnew = _dot_tn(xdt * _expand(dend, h0, R, HEAD_DIM), Bg)
            escale = jnp.concatenate([jnp.broadcast_to(elast[:, h0 + r:h0 + r + 1], (HEAD_DIM, N)) for r in range(R)], axis=0)
            state[pl.ds(g * gw, gw), :] = escale * Hg + hnew

    vec = pl.BlockSpec((1, LANES), lambda c: (0, 0))
    return pl.pallas_call(
        body, name=name, grid=(nc,),
        in_specs=[pl.BlockSpec((Q, CD), lambda c: (c, 0)), pl.BlockSpec((Q, LANES), lambda c: (c, 0)), vec, vec, vec],
        out_specs=[pl.BlockSpec((Q, di), lambda c: (c, 0)), pl.BlockSpec((None, di, N), lambda c: (c, 0, 0))],
        out_shape=[_sds((S, di)), _sds((nc, di, N))],
        scratch_shapes=[pltpu.VMEM((di, N), f32)],
        compiler_params=_params(("arbitrary",)),
    )(xbc, dtraw, dt_bias, a_log, d_skip)


def _ssd_bwd(xbc, dtraw, dt_bias, a_log, d_skip, hin, dy, di, name):
    S, CD = xbc.shape
    Q, N, G = SSM_CHUNK, SSM_STATE, SSM_GROUPS
    nc = S // Q
    nh = di // HEAD_DIM
    R = nh // G
    gw = R * HEAD_DIM
    P = HEAD_DIM

    def body(xbc_ref, dt_ref, bias_ref, alog_ref, dsk_ref, hin_ref, dy_ref, dxbc_ref, ddt_ref, dA_ref, ddsk_ref, dtb_ref, dstate):
        c = pl.program_id(0)

        @pl.when(c == 0)
        def _():
            dstate[...] = jnp.zeros_like(dstate)
            dA_ref[...] = jnp.zeros_like(dA_ref)
            ddsk_ref[...] = jnp.zeros_like(ddsk_ref)
            dtb_ref[...] = jnp.zeros_like(dtb_ref)

        dtraw_v = dt_ref[...]
        dt, A, cs, csT, tri = _ssd_prep(dtraw_v, bias_ref[...], alog_ref[...])
        dsk = dsk_ref[...]
        ecs = jnp.exp(cs)
        dend = jnp.exp(cs[Q - 1:Q, :] - cs)
        elast = jnp.exp(cs[Q - 1:Q, :])
        lane = lax.broadcasted_iota(jnp.int32, (1, LANES), 1)
        row = lax.broadcasted_iota(jnp.int32, (Q, 1), 0)
        dcs = jnp.zeros((Q, LANES), f32)
        rsx = jnp.zeros((Q, LANES), f32)
        ddsk = jnp.zeros((1, LANES), f32)
        for g in range(G):
            h0 = g * R
            Bg = xbc_ref[:, pl.ds(di + g * N, N)]
            Cg = xbc_ref[:, pl.ds(di + G * N + g * N, N)]
            xg = xbc_ref[:, pl.ds(g * gw, gw)]
            dyg = dy_ref[:, pl.ds(g * gw, gw)]
            Hg = hin_ref[pl.ds(g * gw, gw), :]
            dHg = dstate[pl.ds(g * gw, gw), :]
            dt_e = _expand(dt, h0, R, P)
            ecs_e = _expand(ecs, h0, R, P)
            dend_e = _expand(dend, h0, R, P)
            Gm = _dot_nt(Cg, Bg)
            xdt = xg * dt_e
            yoff_raw = _dot_nt(Cg, Hg)
            dye = dyg * ecs_e
            bdh = _dot_nt(Bg, dHg)
            dC = _dot(dye, Hg)
            dB = _dot(xdt * dend_e, dHg)
            dHin = _dot_tn(dye, Cg)
            dxdt_state = dend_e * bdh
            t_off = dyg * yoff_raw * ecs_e
            t_end = bdh * xdt * dend_e
            dG = jnp.zeros((Q, Q), f32)
            dxs = []
            for r in range(R):
                h = h0 + r
                sl = slice(r * P, (r + 1) * P)
                L = jnp.exp(jnp.where(tri, cs[:, h:h + 1] - csT[h:h + 1, :], -jnp.inf))
                M = Gm * L
                dyh = dyg[:, sl]
                dM = _dot_nt(dyh, xdt[:, sl])
                dxdt = _dot_tn(M, dyh) + dxdt_state[:, sl]
                dG = dG + dM * L
                E = dM * M
                w_end = jnp.sum(t_end[:, sl], 1, keepdims=True)
                hh = jnp.sum(dHg[sl, :] * Hg[sl, :], keepdims=True) * elast[:, h:h + 1]
                d = (jnp.sum(E, 1, keepdims=True) - jnp.sum(E.T, 1, keepdims=True)
                     + jnp.sum(t_off[:, sl], 1, keepdims=True) - w_end
                     + jnp.where(row == Q - 1, jnp.sum(w_end, keepdims=True) + hh, 0.0))
                onehot = (lane == h).astype(f32)
                dcs = dcs + d * onehot
                rsx = rsx + jnp.sum(dxdt * xg[:, sl], 1, keepdims=True) * onehot
                ddsk = ddsk + jnp.sum(dyh * xg[:, sl], keepdims=True) * onehot
                dxs.append(dxdt * dt_e[:, sl] + dyh * dsk[:, h:h + 1])
            dxbc_ref[:, pl.ds(g * gw, gw)] = jnp.concatenate(dxs, axis=1)
            dxbc_ref[:, pl.ds(di + g * N, N)] = dB + _dot_tn(dG, Cg)
            dxbc_ref[:, pl.ds(di + G * N + g * N, N)] = dC + _dot(dG, Bg)
            escale = jnp.concatenate([jnp.broadcast_to(elast[:, h0 + r:h0 + r + 1], (P, N)) for r in range(R)], axis=0)
            dstate[pl.ds(g * gw, gw), :] = escale * dHg + dHin
        da = lax.dot_general(tri.astype(f32), dcs, (((0,), (0,)), ((), ())), preferred_element_type=f32,
                             precision=lax.Precision.HIGHEST)
        ddt = da * A + rsx
        ddtraw = ddt * jax.nn.sigmoid(dtraw_v + bias_ref[...])
        ddt_ref[...] = ddtraw.astype(ddt_ref.dtype)
        dA_ref[...] += jnp.sum(da * dt, 0, keepdims=True) * A
        ddsk_ref[...] += ddsk
        dtb_ref[...] += jnp.sum(ddtraw, 0, keepdims=True)

    vec = pl.BlockSpec((1, LANES), lambda c: (0, 0))
    rev = lambda c: (nc - 1 - c, 0)
    return pl.pallas_call(
        body, name=name, grid=(nc,),
        in_specs=[pl.BlockSpec((Q, CD), rev), pl.BlockSpec((Q, LANES), rev), vec, vec, vec,
                  pl.BlockSpec((None, di, N), lambda c: (nc - 1 - c, 0, 0)), pl.BlockSpec((Q, di), rev)],
        out_specs=[pl.BlockSpec((Q, CD), rev), pl.BlockSpec((Q, LANES), rev), vec, vec, vec],
        out_shape=[_sds((S, CD)), _sds((S, LANES), bf16), _sds((1, LANES)), _sds((1, LANES)), _sds((1, LANES))],
        scratch_shapes=[pltpu.VMEM((di, N), f32)],
        compiler_params=_params(("arbitrary",)),
    )(xbc, dtraw, dt_bias, a_log, d_skip, hin, dy)


def _t5_bucket_np(dist):
    max_exact = REL_BUCKETS // 2
    n = np.maximum(dist, 1).astype(np.float32)
    large = np.float32(max_exact) + np.log(n / np.float32(max_exact)) / np.float32(math.log(REL_MAX_DIST / max_exact)) * np.float32(REL_BUCKETS - max_exact)
    large = np.minimum(large.astype(np.int32), REL_BUCKETS - 1)
    return np.where(dist < max_exact, dist, large)


def _bucket_onehot():
    i = np.arange(ATT_BLK)[:, None]
    j = np.arange(2 * ATT_BLK)[None, :]
    delta = np.maximum(ATT_BLK + i - j, 0)
    out = np.zeros((len(ATT_DILATIONS), REL_BUCKETS, ATT_BLK * 2 * ATT_BLK), np.float32)
    for gi, d in enumerate(ATT_DILATIONS):
        b = _t5_bucket_np(delta * d).reshape(-1)
        out[gi, b, np.arange(b.size)] = 1.0
    return out


def _exact_mm(a, b, *, name, tb=False):
    M, K = a.shape
    N = b.shape[0] if tb else b.shape[1]
    tn = _tile(N, 4096, LANES)
    dn = (((1,), (1 if tb else 0,)), ((), ()))

    def body(a_ref, b_ref, o_ref):
        o_ref[...] = lax.dot_general(a_ref[...], b_ref[...], dn, preferred_element_type=f32,
                                     precision=lax.Precision.HIGHEST)

    return pl.pallas_call(
        body, name=name, grid=(N // tn,),
        in_specs=[pl.BlockSpec((M, K), lambda j: (0, 0)),
                  pl.BlockSpec((tn, K), lambda j: (j, 0)) if tb else pl.BlockSpec((K, tn), lambda j: (0, j))],
        out_specs=pl.BlockSpec((M, tn), lambda j: (0, j)), out_shape=_sds((M, N)),
        compiler_params=_params(("parallel",)),
    )(a, b)


def _attn_mask(n):
    i = lax.broadcasted_iota(jnp.int32, (ATT_BLK, 2 * ATT_BLK), 0)
    j = lax.broadcasted_iota(jnp.int32, (ATT_BLK, 2 * ATT_BLK), 1)
    delta = ATT_BLK + i - j
    return (delta >= 0) & (delta <= ATT_BLK) & ((j >= ATT_BLK) | (n > 0))


def _attn_fwd(q, k, v, bias, d, name):
    S, D = q.shape
    L = S // d
    nb = L // ATT_BLK
    HP = D // LANES
    scale = HEAD_DIM ** -0.5
    view = lambda t: t.reshape(L, d * D)

    def body(q_ref, kp_ref, kc_ref, vp_ref, vc_ref, b_ref, o_ref, lse_ref):
        mask = _attn_mask(pl.program_id(2))
        qv = q_ref[...]
        kcat = jnp.concatenate([kp_ref[...], kc_ref[...]], axis=0)
        vcat = jnp.concatenate([vp_ref[...], vc_ref[...]], axis=0)
        outs, lses = [], []
        for hh in range(2):
            sl = slice(hh * HEAD_DIM, (hh + 1) * HEAD_DIM)
            s = _dot_nt(qv[:, sl], kcat[:, sl]) * scale + b_ref[hh]
            s = jnp.where(mask, s, -jnp.inf)
            m = jnp.max(s, -1, keepdims=True)
            p = jnp.exp(s - m)
            l = jnp.sum(p, -1, keepdims=True)
            outs.append(_dot(p / l, vcat[:, sl]))
            lses.append(jnp.broadcast_to(m + jnp.log(l), (ATT_BLK, HEAD_DIM)))
        o_ref[...] = jnp.concatenate(outs, axis=1)
        lse_ref[...] = jnp.concatenate(lses, axis=1)

    cur = pl.BlockSpec((ATT_BLK, LANES), lambda hp, r, n: (n, r * HP + hp))
    prev = pl.BlockSpec((ATT_BLK, LANES), lambda hp, r, n: (jnp.maximum(n - 1, 0), r * HP + hp))
    o, lse = pl.pallas_call(
        body, name=name, grid=(HP, d, nb),
        in_specs=[cur, prev, cur, prev, cur, pl.BlockSpec((2, ATT_BLK, 2 * ATT_BLK), lambda hp, r, n: (hp, 0, 0))],
        out_specs=[cur, cur], out_shape=[_sds((L, d * D)), _sds((L, d * D))],
        compiler_params=_params(("parallel", "parallel", "arbitrary")),
    )(view(q), view(k), view(k), view(v), view(v), bias)
    return o.reshape(S, D), lse.reshape(S, D)


def _attn_bwd(q, k, v, bias, att, datt, lse_tot, d, name):
    S, D = q.shape
    L = S // d
    nb = L // ATT_BLK
    HP = D // LANES
    H = D // HEAD_DIM
    scale = HEAD_DIM ** -0.5
    view = lambda t: t.reshape(L, d * D)

    def body(q_ref, kp_ref, kc_ref, vp_ref, vc_ref, b_ref, o_ref, do_ref, lse_ref,
             dq_ref, dk_ref, dv_ref, db_ref, carry_k, carry_v, part_k, part_v):
        r = pl.program_id(1)
        n = pl.program_id(2)

        @pl.when(n == 0)
        def _():
            carry_k[...] = jnp.zeros_like(carry_k)
            carry_v[...] = jnp.zeros_like(carry_v)

        @pl.when((n == 0) & (r == 0))
        def _():
            db_ref[...] = jnp.zeros_like(db_ref)

        @pl.when(n < nb)
        def _():
            mask = _attn_mask(n)
            qv = q_ref[...]
            kcat = jnp.concatenate([kp_ref[...], kc_ref[...]], axis=0)
            vcat = jnp.concatenate([vp_ref[...], vc_ref[...]], axis=0)
            ov, dov, lsev = o_ref[...], do_ref[...], lse_ref[...]
            dqs, dks, dvs = [], [], []
            for hh in range(2):
                sl = slice(hh * HEAD_DIM, (hh + 1) * HEAD_DIM)
                s = _dot_nt(qv[:, sl], kcat[:, sl]) * scale + b_ref[hh]
                p = jnp.exp(jnp.where(mask, s, -jnp.inf) - lsev[:, hh * HEAD_DIM:hh * HEAD_DIM + 1])
                dp = _dot_nt(dov[:, sl], vcat[:, sl])
                dsum = jnp.sum(dov[:, sl] * ov[:, sl], 1, keepdims=True)
                ds = p * (dp - dsum)
                db_ref[hh] += ds
                dqs.append(_dot(ds, kcat[:, sl]) * scale)
                dks.append(_dot_tn(ds, qv[:, sl]) * scale)
                dvs.append(_dot_tn(p, dov[:, sl]))
            dq_ref[...] = jnp.concatenate(dqs, axis=1).astype(dq_ref.dtype)
            part_k[...] = jnp.concatenate(dks, axis=1)
            part_v[...] = jnp.concatenate(dvs, axis=1)

        @pl.when(n == nb)
        def _():
            part_k[...] = jnp.zeros_like(part_k)
            part_v[...] = jnp.zeros_like(part_v)

        dk_ref[...] = carry_k[...] + part_k[pl.ds(0, ATT_BLK), :]
        dv_ref[...] = carry_v[...] + part_v[pl.ds(0, ATT_BLK), :]
        carry_k[...] = part_k[pl.ds(ATT_BLK, ATT_BLK), :]
        carry_v[...] = part_v[pl.ds(ATT_BLK, ATT_BLK), :]

    nq = lambda n: jnp.minimum(n, nb - 1)
    cur = pl.BlockSpec((ATT_BLK, LANES), lambda hp, r, n: (nq(n), r * HP + hp))
    prev = pl.BlockSpec((ATT_BLK, LANES), lambda hp, r, n: (jnp.maximum(nq(n) - 1, 0), r * HP + hp))
    done = pl.BlockSpec((ATT_BLK, LANES), lambda hp, r, n: (jnp.maximum(n - 1, 0), r * HP + hp))
    bspec = pl.BlockSpec((2, ATT_BLK, 2 * ATT_BLK), lambda hp, r, n: (hp, 0, 0))
    dq, dk, dv, db = pl.pallas_call(
        body, name=name, grid=(HP, d, nb + 1),
        in_specs=[cur, prev, cur, prev, cur, bspec, cur, cur, cur],
        out_specs=[cur, done, done, bspec],
        out_shape=[_sds((L, d * D), bf16), _sds((L, d * D)), _sds((L, d * D)), _sds((H, ATT_BLK, 2 * ATT_BLK))],
        scratch_shapes=[pltpu.VMEM((ATT_BLK, LANES), f32), pltpu.VMEM((ATT_BLK, LANES), f32),
                        pltpu.VMEM((2 * ATT_BLK, LANES), f32), pltpu.VMEM((2 * ATT_BLK, LANES), f32)],
        compiler_params=_params(("arbitrary", "arbitrary", "arbitrary")),
    )(view(q), view(k), view(k), view(v), view(v), bias, view(att), view(datt), view(lse_tot))
    return dq.reshape(S, D), dk.reshape(S, D), dv.reshape(S, D), db


def _attn_combine(os_, lses, name):
    def fn(rv, vv):
        o0, o1, o2, l0, l1, l2 = rv
        m = jnp.maximum(jnp.maximum(l0, l1), l2)
        e0, e1, e2 = jnp.exp(l0 - m), jnp.exp(l1 - m), jnp.exp(l2 - m)
        tot = e0 + e1 + e2
        return [(e0 * o0 + e1 * o1 + e2 * o2) / tot, m + jnp.log(tot)], []
    w = os_[0].shape[1]
    (att, lse), _ = _rowwise(name, fn, list(os_) + list(lses), [], [(w, f32), (w, f32)], [])
    return att, lse


ANY = pl.BlockSpec(memory_space=pl.ANY)


def _all_gather(vs, name):
    n = len(vs)

    def body(*refs):
        x_refs, out_refs = refs[:n], refs[n:2 * n]
        send_sems, recv_sems, local_sems = refs[2 * n:]
        x, y, c = lax.axis_index("x"), lax.axis_index("y"), lax.axis_index("c")
        me, sibling = (x, y, c), (x, y, 1 - c)
        chips = [(1 - x, y), (x, 1 - y), (1 - x, 1 - y)]

        def slot(i, px, py, pc):
            return out_refs[i].at[4 * px + 2 * py + pc]

        def copy(i, k, block, to, src=None):
            return pltpu.make_async_remote_copy(
                src_ref=slot(i, *block) if src is None else src, dst_ref=slot(i, *block),
                send_sem=send_sems.at[i, k], recv_sem=recv_sems.at[i, k], device_id=to, device_id_type=MESH)

        mine = [pltpu.make_async_copy(x_refs[i], slot(i, *me), local_sems.at[i]) for i in range(n)]
        for cp in mine:
            cp.start()
        first = []
        for i in range(n):
            first.append(copy(i, 0, me, sibling, src=x_refs[i]))
            first += [copy(i, 1 + j, me, (*chip, c), src=x_refs[i]) for j, chip in enumerate(chips)]
        for cp in first:
            cp.start()
        passed = []
        for i in range(n):
            for j, chip in enumerate(chips):
                copy(i, 1 + j, (*chip, c), me).wait_recv()
                cp = copy(i, 4 + j, (*chip, c), sibling)
                cp.start()
                passed.append(cp)
        for i in range(n):
            copy(i, 0, sibling, me).wait_recv()
            for j, chip in enumerate(chips):
                copy(i, 4 + j, (*chip, 1 - c), me).wait_recv()
        for cp in first + passed:
            cp.wait_send()
        for cp in mine:
            cp.wait()

    return pl.pallas_call(
        body, name=name, out_shape=[_sds((N_DEV,) + v.shape, v.dtype) for v in vs], in_specs=[ANY] * n,
        out_specs=[ANY] * n,
        scratch_shapes=[pltpu.SemaphoreType.DMA((n, 7)), pltpu.SemaphoreType.DMA((n, 7)), pltpu.SemaphoreType.DMA((n,))],
    )(*vs)


def _rs_sibling(parts, name):
    n = len(parts)

    def body(*refs):
        p_refs, out_refs = refs[:n], refs[n:2 * n]
        send_sems, recv_sems = refs[2 * n:]
        x, y, c = lax.axis_index("x"), lax.axis_index("y"), lax.axis_index("c")
        cps = [pltpu.make_async_remote_copy(
            src_ref=p_refs[i].at[k, 1 - c], dst_ref=out_refs[i].at[k], send_sem=send_sems.at[i, k],
            recv_sem=recv_sems.at[i, k], device_id=(x, y, 1 - c), device_id_type=MESH)
            for i in range(n) for k in range(4)]
        for cp in cps:
            cp.start()
        for cp in cps:
            cp.wait()

    return pl.pallas_call(
        body, name=name, out_shape=[_sds((4,) + p.shape[2:], p.dtype) for p in parts], in_specs=[ANY] * n,
        out_specs=[ANY] * n,
        scratch_shapes=[pltpu.SemaphoreType.DMA((n, 4)), pltpu.SemaphoreType.DMA((n, 4))],
    )(*parts)


def _rs_chips(ts, name):
    n = len(ts)

    def body(*refs):
        t_refs, out_refs = refs[:n], refs[n:2 * n]
        send_sems, recv_sems, local_sems = refs[2 * n:]
        x, y, c = lax.axis_index("x"), lax.axis_index("y"), lax.axis_index("c")
        mine = 2 * x + y
        local = [pltpu.make_async_copy(t_refs[i].at[mine], out_refs[i].at[mine], local_sems.at[i]) for i in range(n)]
        for cp in local:
            cp.start()
        chips = [(1 - x, y), (x, 1 - y), (1 - x, 1 - y)]
        cps = [pltpu.make_async_remote_copy(
            src_ref=t_refs[i].at[2 * px + py], dst_ref=out_refs[i].at[mine], send_sem=send_sems.at[i, j],
            recv_sem=recv_sems.at[i, j], device_id=(px, py, c), device_id_type=MESH)
            for i in range(n) for j, (px, py) in enumerate(chips)]
        for cp in cps:
            cp.start()
        for cp in cps:
            cp.wait()
        for cp in local:
            cp.wait()

    return pl.pallas_call(
        body, name=name, out_shape=[_sds(t.shape, t.dtype) for t in ts], in_specs=[ANY] * n, out_specs=[ANY] * n,
        scratch_shapes=[pltpu.SemaphoreType.DMA((n, 3)), pltpu.SemaphoreType.DMA((n, 3)), pltpu.SemaphoreType.DMA((n,))],
    )(*ts)


def _pair_add(part, recv, c_arr, name):
    _, _, R, C = part.shape
    tr = _tile(R, PACK_ROW_TILE, 8)

    def body(c_ref, p_ref, r_ref, o_ref):
        o_ref[...] = p_ref[...] + r_ref[...]

    return pl.pallas_call(
        body, name=name,
        grid_spec=pltpu.PrefetchScalarGridSpec(
            num_scalar_prefetch=1, grid=(4, R // tr),
            in_specs=[pl.BlockSpec((None, None, tr, C), lambda k, i, c_ref: (k, c_ref[0], i, 0)),
                      pl.BlockSpec((None, tr, C), lambda k, i, c_ref: (k, i, 0))],
            out_specs=pl.BlockSpec((None, tr, C), lambda k, i, c_ref: (k, i, 0))),
        out_shape=_sds((4, R, C)),
        compiler_params=_params(("parallel", "parallel")),
    )(c_arr, part, recv)


def _sum_slots(t, name):
    n, R, C = t.shape
    tr = _tile(R, PACK_ROW_TILE, 8)

    def body(t_ref, o_ref):
        acc = t_ref[0]
        for k in range(1, n):
            acc = acc + t_ref[k]
        o_ref[...] = acc

    return pl.pallas_call(
        body, name=name, grid=(R // tr,),
        in_specs=[pl.BlockSpec((n, tr, C), lambda i: (0, i, 0))],
        out_specs=pl.BlockSpec((tr, C), lambda i: (i, 0)), out_shape=_sds((R, C)),
        compiler_params=_params(("parallel",)),
    )(t)


def _reduce_scatter(parts, c_arr, name):
    parts4 = [p.reshape((4, 2) + p.shape[1:]) for p in parts]
    recv = _rs_sibling(parts4, name + "_sibling")
    ts = [_pair_add(p, r, c_arr, f"{name}_pair_{i}") for i, (p, r) in enumerate(zip(parts4, recv))]
    got = _rs_chips(ts, name + "_chips")
    return [_sum_slots(g, f"{name}_sum_{i}") for i, g in enumerate(got)]


def _adamw(w, g, m, v, name):
    def fn(rv, vv):
        wv, gv, mv, vvv = rv
        m2 = ADAM_B1 * mv + (1.0 - ADAM_B1) * gv
        v2 = ADAM_B2 * vvv + (1.0 - ADAM_B2) * jnp.square(gv)
        m_hat = m2 / (1.0 - ADAM_B1 ** ADAM_STEP)
        v_hat = v2 / (1.0 - ADAM_B2 ** ADAM_STEP)
        delta = -ADAM_LR * (m_hat / (jnp.sqrt(v_hat) + ADAM_EPS) + ADAM_WD * wv)
        return [delta, m2, v2], []
    c = w.shape[1]
    (delta, m2, v2), _ = _rowwise(name, fn, [w, g, m, v], [], [(c, f32)] * 3, [])
    return delta, m2, v2


BIG = (("hy_w_in", "col"), ("hy_w_out", "row"), ("cv_w_pw1", "col"), ("cv_w_pw2", "row"),
       ("ffn_w_gate", "col"), ("ffn_w_up", "col"), ("ffn_w_down", "row"))


def _full_from_blocks(g, kind, shp):
    g = g.reshape((N_DEV,) + tuple(shp))
    if kind == "col":
        return jnp.transpose(g, (1, 2, 0, 3)).reshape(shp[0], shp[1], N_DEV * shp[2])
    return jnp.transpose(g, (1, 0, 2, 3)).reshape(shp[0], N_DEV * shp[1], shp[2])


def _blocks_from_full(g, kind, shp):
    if kind == "col":
        t = jnp.transpose(g.reshape(shp[0], shp[1], N_DEV, shp[2]), (2, 0, 1, 3))
    else:
        t = jnp.transpose(g.reshape(shp[0], N_DEV, shp[1], shp[2]), (1, 0, 2, 3))
    return t.reshape(N_DEV, shp[0] * shp[1], shp[2])


class _VecPack:
    def __init__(self, shapes):
        self.shapes = [tuple(s) for s in shapes]
        self.sizes = [int(np.prod(s)) for s in self.shapes]
        total = sum(self.sizes)
        self.rows = -(-(-(-total // LANES)) // 8) * 8
        self.total = total

    def pack(self, arrays):
        flat = jnp.concatenate([a.astype(f32).reshape(-1) for a in arrays])
        flat = jnp.pad(flat, (0, self.rows * LANES - self.total))
        return flat.reshape(self.rows, LANES)

    def unpack(self, packed):
        flat = packed.reshape(-1)
        out, off = [], 0
        for shp, n in zip(self.shapes, self.sizes):
            out.append(flat[off:off + n].reshape(shp))
            off += n
        return out


def _row(v):
    return v.reshape(1, -1)


def _pad_lanes(v):
    v = v.reshape(1, -1)
    return jnp.pad(v, ((0, 0), (0, LANES - v.shape[1])))


def _ffn_fwd(h, w_gu, w_down, tag):
    F = w_down.shape[0]
    au = _mm(h, w_gu, name=f"ffn_gu_{tag}")
    (f,), _ = _rowwise(f"swiglu_{tag}", lambda rv, vv: ([_silu(rv[0]) * rv[1]], []),
                       [(au, 0, F), (au, 1, F)], [], [(F, bf16)], [], sub=16)
    out = _mm(f, w_down, name=f"ffn_down_{tag}")
    return out, (au, f)


def _ffn_bwd(h, w_gu, w_down, saved, dout, tag):
    au, f = saved
    F = w_down.shape[0]
    df = _mm(dout, w_down, tb=True, name=f"ffn_down_dx_{tag}")
    dw_down = _mm(f, dout, ta=True, name=f"ffn_down_dw_{tag}")

    def fn(rv, vv):
        a, u, d = rv
        _, vjp = jax.vjp(lambda a_, u_: _silu(a_) * u_, a, u)
        da, du = vjp(d)
        return [da, du], []

    (da, du), _ = _rowwise(f"swiglu_bwd_{tag}", fn, [(au, 0, F), (au, 1, F), df], [], [(F, bf16), (F, bf16)], [],
                           sub=16)
    dau = jnp.concatenate([da, du], axis=1)
    dh = _mm(dau, w_gu, tb=True, name=f"ffn_gu_dx_{tag}")
    dw_gu = _mm(h, dau, ta=True, name=f"ffn_gu_dw_{tag}")
    return dh, dw_gu, dw_down


def _local_step(x, target, mod, W, small):
    S, D = x.shape
    w_in, w_out, w_pw1, w_pw2, w_gate, w_up, w_down = W
    di = small["hy_ssm_norm_g"].shape[-1]
    nh = small["hy_dt_bias"].shape[-1]
    cd = small["hy_conv_b"].shape[-1]
    F = w_down.shape[1]
    m = [[_row(mod[i, j]) for j in range(6)] for i in range(2)]

    o = 0
    seg = {}
    for nm, wd in (("z", di), ("xbc", cd), ("dt", nh), ("q0", D), ("q1", D), ("q2", D), ("k", D), ("v", D)):
        seg[nm] = (o, wd)
        o += wd
    wseg = {nm: w_in[0][:, a:a + wd] for nm, (a, wd) in seg.items()}
    wseg["dt"] = jnp.pad(wseg["dt"], ((0, 0), (0, LANES - nh)))
    w_out_y, w_out_a = w_out[0][:di], w_out[0][di:]
    w_gu = [jnp.concatenate([w_gate[i], w_up[i]], axis=1) for i in range(2)]

    g_mix = [_row(small["norm_mix_g"][i]) for i in range(2)]
    g_ffn = [_row(small["norm_ffn_g"][i]) for i in range(2)]
    conv_w, conv_b = small["hy_conv_w_full"], _row(small["hy_conv_b"][0])
    dt_bias, a_log, d_skip = (_pad_lanes(small[k][0]) for k in ("hy_dt_bias", "hy_a_log", "hy_d_skip"))
    g_ssm = _row(small["hy_ssm_norm_g"][0])
    onehot = jnp.asarray(_bucket_onehot())
    rel_t = small["rel_table"].T
    H = D // HEAD_DIM
    bias = [_exact_mm(rel_t[gi * H:(gi + 1) * H], onehot[gi], name=f"rel_bias_{gi}")
            .reshape(H, ATT_BLK, 2 * ATT_BLK) for gi in range(3)]

    h1 = _adaln_fwd(x, g_mix[0], m[0][1], m[0][0], "adaln_mix0")
    proj = {nm: _mm(h1, wseg[nm], name=f"in_{nm}") for nm in seg}
    xbc_pre, xbc = _conv_fwd(proj["xbc"], conv_w, conv_b, silu=True, name="ssm_conv")
    y, hin = _ssd_fwd(xbc, proj["dt"], dt_bias, a_log, d_skip, di, "ssd_fwd")
    (yg,), _ = _rowwise("ssm_gate", lambda rv, vv: ([_gate_f(rv[0], rv[1], vv[0])], []),
                        [y, proj["z"]], [g_ssm], [(di, bf16)], [], sub=16)
    og = [_attn_fwd(proj[f"q{gi}"], proj["k"], proj["v"], bias[gi], d, f"attn_fwd_{gi}")
          for gi, d in enumerate(ATT_DILATIONS)]
    att, lse_tot = _attn_combine([a for a, _ in og], [b for _, b in og], "attn_combine")
    mix0 = _mm(att, w_out_a, add=_mm(yg, w_out_y, name="out_y"), name="out_a")
    x1 = _resid_fwd(x, m[0][2], mix0, "resid_mix0")
    h2 = _adaln_fwd(x1, g_ffn[0], m[0][4], m[0][3], "adaln_ffn0")
    f0, ffn0_saved = _ffn_fwd(h2, w_gu[0], w_down[0], "0")
    x2 = _resid_fwd(x1, m[0][5], f0, "resid_ffn0")

    h3 = _adaln_fwd(x2, g_mix[1], m[1][1], m[1][0], "adaln_mix1")
    pw1 = _mm(h3, w_pw1[0], bias=_row(small["cv_b_pw1_full"]), name="cv_pw1")
    (u,), _ = _rowwise("cv_glu", lambda rv, vv: ([rv[0] * jax.nn.sigmoid(rv[1])], []),
                       [(pw1, 0, D), (pw1, 1, D)], [], [(D, f32)], [])
    (u2,) = _conv_fwd(u, small["cv_w_dw_full"], _row(small["cv_b_dw_full"]), silu=False, name="cv_dw")
    ln_g, ln_b = _row(small["cv_ln_g_full"]), _row(small["cv_ln_b_full"])
    (u3,), _ = _rowwise("cv_lnsilu", lambda rv, vv: ([_lnsilu_f(rv[0], vv[0], vv[1])], []),
                        [u2], [ln_g, ln_b], [(D, bf16)], [], sub=16)
    mix1 = _mm(u3, w_pw2[0], bias=_row(small["cv_b_pw2_full"]), name="cv_pw2")
    x3 = _resid_fwd(x2, m[1][2], mix1, "resid_mix1")
    h4 = _adaln_fwd(x3, g_ffn[1], m[1][4], m[1][3], "adaln_ffn1")
    f1, ffn1_saved = _ffn_fwd(h4, w_gu[1], w_down[1], "1")
    x4 = _resid_fwd(x3, m[1][5], f1, "resid_ffn1")

    g_fin = _row(small["final_norm_g"])

    def final_fn(rv, vv):
        xv, tv = rv
        yv, vjp = jax.vjp(_rms, xv, vv[0])
        err = yv - tv
        dx, dg = vjp(err / D)
        part = 0.5 * jnp.sum(jnp.mean(err * err, -1, keepdims=True), 0, keepdims=True)
        return [dx], [dg, jnp.broadcast_to(part, (1, LANES))]

    (dx4,), (d_fin, loss) = _rowwise("loss_head", final_fn, [x4, target], [g_fin], [(D, f32)], [D, LANES])

    dmod = [[None] * 6 for _ in range(2)]
    d_norm_mix, d_norm_ffn = [None, None], [None, None]
    d_gu, d_down = [None, None], [None, None]

    df1, (dmod[1][5], _) = _resid_bwd(dx4, f1, m[1][5], "resid_ffn1_bwd")
    dh4, d_gu[1], d_down[1] = _ffn_bwd(h4, w_gu[1], w_down[1], ffn1_saved, df1, "1")
    dx3, (d_norm_ffn[1], dmod[1][4], dmod[1][3]) = _adaln_bwd(x3, g_ffn[1], m[1][4], m[1][3], dh4, dx4, "adaln_ffn1_bwd")
    dmix1, (dmod[1][2], d_b_pw2) = _resid_bwd(dx3, mix1, m[1][2], "resid_mix1_bwd")
    du3 = _mm(dmix1, w_pw2[0], tb=True, name="cv_pw2_dx")
    d_pw2 = _mm(u3, dmix1, ta=True, name="cv_pw2_dw")

    def lnsilu_bwd(rv, vv):
        _, vjp = jax.vjp(_lnsilu_f, rv[0], vv[0], vv[1])
        du, dg, db = vjp(rv[1])
        return [du], [dg, db]

    (du2,), (d_ln_g, d_ln_b) = _rowwise("cv_lnsilu_bwd", lnsilu_bwd, [u2, du3], [ln_g, ln_b], [(D, f32)], [D, D])
    du, d_w_dw, d_b_dw = _conv_bwd(u, small["cv_w_dw_full"], du2, None, silu=False, name="cv_dw_bwd")

    def glu_bwd(rv, vv):
        a, gt, d = rv
        _, vjp = jax.vjp(lambda a_, g_: a_ * jax.nn.sigmoid(g_), a, gt)
        da, dg = vjp(d)
        return [da, dg], [jnp.sum(da, 0, keepdims=True), jnp.sum(dg, 0, keepdims=True)]

    (dpa, dpg), (d_b1a, d_b1g) = _rowwise("cv_glu_bwd", glu_bwd, [(pw1, 0, D), (pw1, 1, D), du], [],
                                           [(D, bf16), (D, bf16)], [D, D], sub=16)
    dpw1 = jnp.concatenate([dpa, dpg], axis=1)
    d_b_pw1 = jnp.concatenate([d_b1a, d_b1g], axis=1)
    dh3 = _mm(dpw1, w_pw1[0], tb=True, name="cv_pw1_dx")
    d_pw1 = _mm(h3, dpw1, ta=True, name="cv_pw1_dw")
    dx2, (d_norm_mix[1], dmod[1][1], dmod[1][0]) = _adaln_bwd(x2, g_mix[1], m[1][1], m[1][0], dh3, dx3, "adaln_mix1_bwd")

    df0, (dmod[0][5], _) = _resid_bwd(dx2, f0, m[0][5], "resid_ffn0_bwd")
    dh2, d_gu[0], d_down[0] = _ffn_bwd(h2, w_gu[0], w_down[0], ffn0_saved, df0, "0")
    dx1, (d_norm_ffn[0], dmod[0][4], dmod[0][3]) = _adaln_bwd(x1, g_ffn[0], m[0][4], m[0][3], dh2, dx2, "adaln_ffn0_bwd")
    dmix0, (dmod[0][2], _) = _resid_bwd(dx1, mix0, m[0][2], "resid_mix0_bwd")
    dyg = _mm(dmix0, w_out_y, tb=True, name="out_y_dx")
    datt = _mm(dmix0, w_out_a, tb=True, name="out_a_dx")
    d_out = jnp.concatenate([_mm(yg, dmix0, ta=True, name="out_y_dw"), _mm(att, dmix0, ta=True, name="out_a_dw")], axis=0)

    dq, dks, dvs, dbs = [], [], [], []
    for gi, d in enumerate(ATT_DILATIONS):
        a, b, c_, e = _attn_bwd(proj[f"q{gi}"], proj["k"], proj["v"], bias[gi], att, datt, lse_tot, d, f"attn_bwd_{gi}")
        dq.append(a)
        dks.append(b)
        dvs.append(c_)
        dbs.append(e)
    dk = _add3(*dks, "attn_dk")
    dv = _add3(*dvs, "attn_dv")
    d_rel = jnp.concatenate(
        [_exact_mm(dbs[gi].reshape(H, -1), onehot[gi], tb=True, name=f"rel_grad_{gi}") for gi in range(3)], axis=0).T

    def gate_bwd(rv, vv):
        _, vjp = jax.vjp(_gate_f, rv[0], rv[1], vv[0])
        dy_, dz_, dg_ = vjp(rv[2])
        return [dy_, dz_], [dg_]

    (dy, dz), (d_g_ssm,) = _rowwise("ssm_gate_bwd", gate_bwd, [y, proj["z"], dyg], [g_ssm], [(di, f32), (di, bf16)], [di],
                                    sub=16)
    dxbc, ddtraw, d_a_log, d_dskip, d_dt_bias = _ssd_bwd(xbc, proj["dt"], dt_bias, a_log, d_skip, hin, dy, di, "ssd_bwd")
    dxbc_pre, d_conv_w, d_conv_b = _conv_bwd(proj["xbc"], conv_w, dxbc, xbc_pre, silu=True, name="ssm_conv_bwd",
                                             dx_dtype=bf16)

    dseg = {"z": dz, "xbc": dxbc_pre, "dt": ddtraw, "q0": dq[0], "q1": dq[1], "q2": dq[2], "k": dk, "v": dv}
    dh1 = None
    d_in_parts = []
    for nm in seg:
        dh1 = _mm(dseg[nm], wseg[nm], tb=True, add=dh1, name=f"in_{nm}_dx")
        dwp = _mm(h1, dseg[nm], ta=True, name=f"in_{nm}_dw")
        d_in_parts.append(dwp[:, :nh] if nm == "dt" else dwp)
    d_in = jnp.concatenate(d_in_parts, axis=1)
    dx0, (d_norm_mix[0], dmod[0][1], dmod[0][0]) = _adaln_bwd(x, g_mix[0], m[0][1], m[0][0], dh1, dx1, "adaln_mix0_bwd")

    big = [d_in[None], d_out[None], d_pw1[None], d_pw2[None],
           jnp.stack([d_gu[0][:, :F], d_gu[1][:, :F]]), jnp.stack([d_gu[0][:, F:], d_gu[1][:, F:]]),
           jnp.stack(d_down)]
    smallg = dict(
        loss=loss, dmod=jnp.stack([jnp.concatenate(dmod[i], axis=1)[0] for i in range(2)]),
        norm_mix_g=jnp.concatenate(d_norm_mix, axis=0), norm_ffn_g=jnp.concatenate(d_norm_ffn, axis=0),
        hy_conv_w=d_conv_w, hy_conv_b=d_conv_b, hy_dt_bias=d_dt_bias[:, :nh], hy_a_log=d_a_log[:, :nh],
        hy_d_skip=d_dskip[:, :nh], hy_ssm_norm_g=d_g_ssm, rel_table=d_rel,
        cv_b_pw1=d_b_pw1, cv_w_dw=d_w_dw, cv_b_dw=d_b_dw, cv_ln_g=d_ln_g, cv_ln_b=d_ln_b, cv_b_pw2=d_b_pw2,
        final_norm_g=d_fin)
    return dx0, big, smallg


SMALL_GRAD_ORDER = ("loss", "dmod", "norm_mix_g", "norm_ffn_g", "hy_conv_w", "hy_conv_b", "hy_dt_bias", "hy_a_log",
                    "hy_d_skip", "hy_ssm_norm_g", "rel_table", "cv_b_pw1", "cv_w_dw", "cv_b_dw", "cv_ln_g", "cv_ln_b",
                    "cv_b_pw2", "final_norm_g")


def kernel(x, c, ada_w, ada_b, norm_mix_g, norm_ffn_g, hy_w_in, hy_conv_w, hy_conv_b, hy_dt_bias, hy_a_log, hy_d_skip, hy_ssm_norm_g, hy_w_out, rel_table, cv_w_pw1, cv_b_pw1, cv_w_dw, cv_b_dw, cv_ln_g, cv_ln_b, cv_w_pw2, cv_b_pw2, ffn_w_gate, ffn_w_up, ffn_w_down, final_norm_g, loss_target, m_ada_w, m_ada_b, m_norm_mix_g, m_norm_ffn_g, m_hy_w_in, m_hy_conv_w, m_hy_conv_b, m_hy_dt_bias, m_hy_a_log, m_hy_d_skip, m_hy_ssm_norm_g, m_hy_w_out, m_rel_table, m_cv_w_pw1, m_cv_b_pw1, m_cv_w_dw, m_cv_b_dw, m_cv_ln_g, m_cv_ln_b, m_cv_w_pw2, m_cv_b_pw2, m_ffn_w_gate, m_ffn_w_up, m_ffn_w_down, m_final_norm_g, v_ada_w, v_ada_b, v_norm_mix_g, v_norm_ffn_g, v_hy_w_in, v_hy_conv_w, v_hy_conv_b, v_hy_dt_bias, v_hy_a_log, v_hy_d_skip, v_hy_ssm_norm_g, v_hy_w_out, v_rel_table, v_cv_w_pw1, v_cv_b_pw1, v_cv_w_dw, v_cv_b_dw, v_cv_ln_g, v_cv_ln_b, v_cv_w_pw2, v_cv_b_pw2, v_ffn_w_gate, v_ffn_w_up, v_ffn_w_down, v_final_norm_g):
    names = ("ada_w", "ada_b", "norm_mix_g", "norm_ffn_g", "hy_w_in", "hy_conv_w", "hy_conv_b", "hy_dt_bias", "hy_a_log",
             "hy_d_skip", "hy_ssm_norm_g", "hy_w_out", "rel_table", "cv_w_pw1", "cv_b_pw1", "cv_w_dw", "cv_b_dw", "cv_ln_g",
             "cv_ln_b", "cv_w_pw2", "cv_b_pw2", "ffn_w_gate", "ffn_w_up", "ffn_w_down", "final_norm_g")
    w = dict(zip(names, (ada_w, ada_b, norm_mix_g, norm_ffn_g, hy_w_in, hy_conv_w, hy_conv_b, hy_dt_bias, hy_a_log, hy_d_skip,
                         hy_ssm_norm_g, hy_w_out, rel_table, cv_w_pw1, cv_b_pw1, cv_w_dw, cv_b_dw, cv_ln_g, cv_ln_b, cv_w_pw2,
                         cv_b_pw2, ffn_w_gate, ffn_w_up, ffn_w_down, final_norm_g)))
    mom = dict(zip(names, (m_ada_w, m_ada_b, m_norm_mix_g, m_norm_ffn_g, m_hy_w_in, m_hy_conv_w, m_hy_conv_b, m_hy_dt_bias,
                           m_hy_a_log, m_hy_d_skip, m_hy_ssm_norm_g, m_hy_w_out, m_rel_table, m_cv_w_pw1, m_cv_b_pw1, m_cv_w_dw,
                           m_cv_b_dw, m_cv_ln_g, m_cv_ln_b, m_cv_w_pw2, m_cv_b_pw2, m_ffn_w_gate, m_ffn_w_up, m_ffn_w_down,
                           m_final_norm_g)))
    vel = dict(zip(names, (v_ada_w, v_ada_b, v_norm_mix_g, v_norm_ffn_g, v_hy_w_in, v_hy_conv_w, v_hy_conv_b, v_hy_dt_bias,
                           v_hy_a_log, v_hy_d_skip, v_hy_ssm_norm_g, v_hy_w_out, v_rel_table, v_cv_w_pw1, v_cv_b_pw1, v_cv_w_dw,
                           v_cv_b_dw, v_cv_ln_g, v_cv_ln_b, v_cv_w_pw2, v_cv_b_pw2, v_ffn_w_gate, v_ffn_w_up, v_ffn_w_down,
                           v_final_norm_g)))
    S, D = x.shape[1], x.shape[2]
    ax, ay, ac = lax.axis_index("x"), lax.axis_index("y"), lax.axis_index("c")
    me = 4 * ax + 2 * ay + ac
    c_arr = jnp.reshape(ac, (1,)).astype(jnp.int32)
    nmod = ada_w.shape[2]

    shard_shapes = [w[nm].shape for nm, _ in BIG]
    gathered = _all_gather([w[nm].astype(bf16).reshape(-1, w[nm].shape[-1]) for nm, _ in BIG], "gather_weights")
    W = [_full_from_blocks(g, kind, shp) for g, (_, kind), shp in zip(gathered, BIG, shard_shapes)]

    sharded_small = ("hy_conv_w", "cv_b_pw1", "cv_w_dw", "cv_b_dw", "cv_ln_g", "cv_ln_b", "cv_b_pw2")
    vp = _VecPack([c.shape] + [w[nm].shape for nm in sharded_small])
    (sg,) = _all_gather([vp.pack([c] + [w[nm] for nm in sharded_small])], "gather_vectors")
    parts = [vp.unpack(sg[j]) for j in range(N_DEV)]
    c_all = jnp.concatenate([p[0] for p in parts], axis=0)
    small = {k: w[k] for k in ("norm_mix_g", "norm_ffn_g", "hy_conv_b", "hy_dt_bias", "hy_a_log", "hy_d_skip",
                               "hy_ssm_norm_g", "rel_table", "final_norm_g")}
    for i, nm in enumerate(sharded_small):
        small[nm + "_full"] = jnp.concatenate([p[1 + i][0] for p in parts], axis=-1)

    (cs_all,), _ = _rowwise("ada_silu", lambda rv, vv: ([_silu(rv[0])], []), [c_all], [], [(D, f32)], [])
    b_mine = lax.dynamic_slice_in_dim(ada_b, me * nmod, nmod, axis=1)
    mod_part = jnp.stack([_mm(cs_all, ada_w[i], bias=b_mine[i:i + 1], name=f"ada_mod_{i}") for i in range(2)])
    (mod_all,) = _all_gather([mod_part.reshape(2 * N_DEV, nmod)], "gather_mod")
    mod_all = mod_all.reshape(N_DEV, 2, N_DEV, nmod)
    mod_mine = lax.dynamic_index_in_dim(mod_all, me, axis=2, keepdims=False)
    mod = jnp.transpose(mod_mine, (1, 0, 2)).reshape(2, 6, D)

    dx0, big, sgrad = _local_step(x[0], loss_target[0], mod, W, small)

    gp = _VecPack([sgrad[k].shape for k in SMALL_GRAD_ORDER])
    (g_all,) = _all_gather([gp.pack([sgrad[k] for k in SMALL_GRAD_ORDER])], "gather_small_grads")
    tot = dict(zip(SMALL_GRAD_ORDER, gp.unpack(_sum_slots(g_all, "sum_small_grads"))))
    dmod_all = jnp.stack([gp.unpack(g_all[j])[1] for j in range(N_DEV)])
    loss = tot["loss"][0, 0]

    grads = {}
    dmod_mine = lax.dynamic_slice_in_dim(dmod_all, me * nmod, nmod, axis=2)
    grads["ada_w"] = jnp.stack([_mm(cs_all, dmod_mine[:, i], ta=True, name=f"ada_w_grad_{i}") for i in range(2)])
    grads["ada_b"] = tot["dmod"]
    grads["norm_mix_g"], grads["norm_ffn_g"] = tot["norm_mix_g"], tot["norm_ffn_g"]
    grads["hy_conv_b"] = tot["hy_conv_b"]
    grads["hy_dt_bias"] = tot["hy_dt_bias"]
    grads["hy_a_log"] = tot["hy_a_log"]
    grads["hy_d_skip"] = tot["hy_d_skip"]
    grads["hy_ssm_norm_g"] = tot["hy_ssm_norm_g"]
    grads["rel_table"] = tot["rel_table"]
    grads["final_norm_g"] = tot["final_norm_g"][0]
    for nm in sharded_small:
        n = w[nm].shape[-1]
        grads[nm] = lax.dynamic_slice_in_dim(tot[nm], me * n, n, axis=1).reshape(w[nm].shape)

    g_big = _reduce_scatter([_blocks_from_full(g, kind, shp) for g, (_, kind), shp in zip(big, BIG, shard_shapes)],
                            c_arr, "rs")
    for (nm, _), g, shp in zip(BIG, g_big, shard_shapes):
        grads[nm] = g.reshape(shp)

    delta, new_m, new_v = {}, {}, {}
    for nm in ("ada_w",) + tuple(n for n, _ in BIG):
        shp = w[nm].shape
        two = lambda t: t.reshape(-1, shp[-1])
        d_, m_, v_ = _adamw(two(w[nm]), two(grads[nm]), two(mom[nm]), two(vel[nm]), f"adamw_{nm}")
        delta[nm], new_m[nm], new_v[nm] = d_.reshape(shp), m_.reshape(shp), v_.reshape(shp)
    rest = [nm for nm in names if nm not in delta]
    sp = _VecPack([w[nm].shape for nm in rest])
    packs = [sp.pack([t[nm] for nm in rest]) for t in (w, grads, mom, vel)]
    d_, m_, v_ = _adamw(*packs, "adamw_small")
    for nm, a, b, e in zip(rest, sp.unpack(d_), sp.unpack(m_), sp.unpack(v_)):
        delta[nm], new_m[nm], new_v[nm] = a, b, e

    return (loss, dx0[None], *[grads[n] for n in names], *[delta[n] for n in names],
            *[new_m[n] for n in names], *[new_v[n] for n in names])
```

```python
import functools
import math

import numpy as np
import jax
import jax.numpy as jnp
from jax import lax
from jax.experimental import pallas as pl
from jax.experimental.pallas import tpu as pltpu

f32 = jnp.float32
bf16 = jnp.bfloat16
EPS = 1e-6
N_DEV = 8
LANES = 128
SSM_STATE = 128
SSM_CHUNK = 128
SSM_GROUPS = 4
HEAD_DIM = 64
ATT_BLK = 128
ATT_DILATIONS = (1, 4, 16)
REL_BUCKETS = 32
REL_MAX_DIST = 2048
ADAM_LR, ADAM_B1, ADAM_B2, ADAM_EPS, ADAM_WD, ADAM_STEP = 0.001, 0.9, 0.999, 1e-08, 0.01, 10
PACK_COLS = 1024
PACK_ROW_TILE = 256
MESH = pl.DeviceIdType.MESH
VMEM_LIMIT = 48 * 1024 * 1024


def _sds(shape, dtype=f32):
    return jax.ShapeDtypeStruct(tuple(shape), dtype)


def _tile(n, cap, mult):
    best = None
    t = mult
    while t <= min(n, cap):
        if n % t == 0:
            best = t
        t += mult
    return best if best is not None else n


def _params(sem):
    return pltpu.CompilerParams(dimension_semantics=sem, vmem_limit_bytes=VMEM_LIMIT)


def _mm(a, b, *, name, ta=False, tb=False, bias=None, add=None, out_dtype=f32, tm_cap=512, tn_cap=512, tk_cap=8192):
    if ta:
        K, M = a.shape
    else:
        M, K = a.shape
    if tb:
        N, K2 = b.shape
    else:
        K2, N = b.shape
    assert K == K2, (a.shape, b.shape, ta, tb)
    tm = _tile(M, tm_cap, LANES)
    tn = _tile(N, tn_cap, LANES)
    tk = _tile(K, tk_cap, LANES)
    nk = K // tk
    has_bias, has_add = bias is not None, add is not None
    dn = (((0 if ta else 1,), (1 if tb else 0,)), ((), ()))

    def body(*refs):
        a_ref, b_ref = refs[0], refs[1]
        pos = 2
        bias_ref = add_ref = None
        if has_bias:
            bias_ref = refs[pos]
            pos += 1
        if has_add:
            add_ref = refs[pos]
            pos += 1
        o_ref = refs[pos]
        k = pl.program_id(2)
        part = lax.dot_general(a_ref[...].astype(bf16), b_ref[...].astype(bf16), dn, preferred_element_type=f32)

        def finish(r):
            if has_bias:
                r = r + bias_ref[...]
            if has_add:
                r = r + add_ref[...]
            o_ref[...] = r.astype(o_ref.dtype)

        if nk == 1:
            finish(part)
        else:
            acc_ref = refs[pos + 1]

            @pl.when(k == 0)
            def _():
                acc_ref[...] = part

            @pl.when((k > 0) & (k < nk - 1))
            def _():
                acc_ref[...] += part

            @pl.when(k == nk - 1)
            def _():
                finish(acc_ref[...] + part)

    in_specs = [
        pl.BlockSpec((tk, tm), lambda i, j, k: (k, i)) if ta else pl.BlockSpec((tm, tk), lambda i, j, k: (i, k)),
        pl.BlockSpec((tn, tk), lambda i, j, k: (j, k)) if tb else pl.BlockSpec((tk, tn), lambda i, j, k: (k, j)),
    ]
    args = [a, b]
    if has_bias:
        in_specs.append(pl.BlockSpec((1, tn), lambda i, j, k: (0, j)))
        args.append(bias)
    if has_add:
        in_specs.append(pl.BlockSpec((tm, tn), lambda i, j, k: (i, j)))
        args.append(add)
    return pl.pallas_call(
        body, name=name, grid=(M // tm, N // tn, nk), in_specs=in_specs,
        out_specs=pl.BlockSpec((tm, tn), lambda i, j, k: (i, j)), out_shape=_sds((M, N), out_dtype),
        scratch_shapes=[pltpu.VMEM((tm, tn), f32)] if nk > 1 else [],
        compiler_params=_params(("parallel", "parallel", "arbitrary")),
    )(*args)


def _rowwise(name, fn, rows, vecs, out_rows, out_accs, *, tr_cap=256, sub=8):
    rows = [r if isinstance(r, tuple) else (r, 0, r.shape[1]) for r in rows]
    R = rows[0][0].shape[0]
    tr = _tile(R, tr_cap, 8)
    sub = sub if tr % sub == 0 else tr
    n_r, n_v, n_or, n_oa = len(rows), len(vecs), len(out_rows), len(out_accs)

    def body(*refs):
        row_refs = refs[:n_r]
        vec_refs = refs[n_r:n_r + n_v]
        orow_refs = refs[n_r + n_v:n_r + n_v + n_or]
        oacc_refs = refs[n_r + n_v + n_or:]
        vv = [r[...] for r in vec_refs]

        def step(s, accs):
            sl = pl.ds(pl.multiple_of(s * sub, sub), sub)
            ro, ao = fn([r[sl, :] for r in row_refs], vv)
            for o_ref, o in zip(orow_refs, ro):
                o_ref[sl, :] = o.astype(o_ref.dtype)
            return tuple(x + y for x, y in zip(accs, ao))

        accs = lax.fori_loop(0, tr // sub, step, tuple(jnp.zeros((1, w), f32) for w in out_accs))
        if n_oa:
            @pl.when(pl.program_id(0) == 0)
            def _():
                for ref in oacc_refs:
                    ref[...] = jnp.zeros_like(ref)

            for ref, x in zip(oacc_refs, accs):
                ref[...] += x

    in_specs = [pl.BlockSpec((tr, w), functools.partial(lambda i, cb: (i, cb), cb=cb)) for (_, cb, w) in rows]
    in_specs += [pl.BlockSpec((1, v.shape[1]), lambda i: (0, 0)) for v in vecs]
    out_specs = [pl.BlockSpec((tr, w), lambda i: (i, 0)) for (w, _) in out_rows]
    out_specs += [pl.BlockSpec((1, w), lambda i: (0, 0)) for w in out_accs]
    out_shape = [_sds((R, w), dt) for (w, dt) in out_rows] + [_sds((1, w)) for w in out_accs]
    res = pl.pallas_call(
        body, name=name, grid=(R // tr,), in_specs=in_specs, out_specs=out_specs, out_shape=out_shape,
        compiler_params=_params(("arbitrary",)),
    )(*[r[0] for r in rows], *vecs)
    return res[:n_or], res[n_or:]


def _silu(x):
    return x * jax.nn.sigmoid(x)


def _rms(x, g):
    return x * lax.rsqrt(jnp.mean(x * x, -1, keepdims=True) + EPS) * g


def _adaln_f(x, g, sc, sh):
    return _rms(x, g) * (1.0 + sc) + sh


def _gate_f(y, z, g):
    return _rms(y * _silu(z), g)


def _lnsilu_f(u, g, b):
    mu = jnp.mean(u, -1, keepdims=True)
    var = jnp.mean(jnp.square(u - mu), -1, keepdims=True)
    return _silu((u - mu) * lax.rsqrt(var + EPS) * g + b)


def _adaln_fwd(x, g, sc, sh, name):
    (h,), _ = _rowwise(name, lambda rv, vv: ([_adaln_f(rv[0], *vv)], []), [x], [g, sc, sh], [(x.shape[1], bf16)], [],
                       sub=16)
    return h


def _adaln_bwd(x, g, sc, sh, dh, dres, name):
    def fn(rv, vv):
        xv, dhv, drv = rv
        _, vjp = jax.vjp(_adaln_f, xv, *vv)
        dx, dg, dsc, dsh = vjp(dhv)
        return [dx + drv], [dg, dsc, dsh]
    w = x.shape[1]
    (dx,), accs = _rowwise(name, fn, [x, dh, dres], [g, sc, sh], [(w, f32)], [w, w, w])
    return dx, accs


def _resid_fwd(x, gate, mix, name):
    (y,), _ = _rowwise(name, lambda rv, vv: ([rv[0] + vv[0] * rv[1]], []), [x, mix], [gate], [(x.shape[1], f32)], [])
    return y


def _resid_bwd(dx, mix, gate, name):
    def fn(rv, vv):
        dxv, mv = rv
        dm = vv[0] * dxv
        return [dm], [jnp.sum(dxv * mv, 0, keepdims=True), jnp.sum(dm, 0, keepdims=True)]
    w = dx.shape[1]
    (dmix,), accs = _rowwise(name, fn, [dx, mix], [gate], [(w, bf16)], [w, w], sub=16)
    return dmix, accs


def _add3(a, b, c, name):
    (y,), _ = _rowwise(name, lambda rv, vv: ([rv[0] + rv[1] + rv[2]], []), [a, b, c], [], [(a.shape[1], bf16)], [],
                       sub=16)
    return y


CONV_HALO = 32


def _conv_fwd(x, w, b, *, silu, name, tr=512):
    S, C = x.shape
    K = w.shape[0]
    H = CONV_HALO
    assert K - 1 <= H and S % tr == 0 and tr % H == 0 and C % LANES == 0
    nh = tr // H

    def body(xp_ref, xc_ref, w_ref, b_ref, *rest):
        outs, scr = rest[:-1], rest[-1]
        i = pl.program_id(1)
        scr[pl.ds(0, H), :] = jnp.where(i > 0, xp_ref[...], 0.0)
        scr[pl.ds(H, tr), :] = xc_ref[...]
        acc = jnp.zeros((tr, LANES), f32) + b_ref[...]
        for k in range(K):
            acc = acc + scr[pl.ds(H - (K - 1) + k, tr), :] * w_ref[pl.ds(k, 1), :]
        outs[0][...] = acc
        if silu:
            outs[1][...] = _silu(acc)

    n_out = 2 if silu else 1
    return pl.pallas_call(
        body, name=name, grid=(C // LANES, S // tr),
        in_specs=[pl.BlockSpec((H, LANES), lambda j, i: (jnp.maximum(i * nh - 1, 0), j)),
                  pl.BlockSpec((tr, LANES), lambda j, i: (i, j)),
                  pl.BlockSpec((K, LANES), lambda j, i: (0, j)),
                  pl.BlockSpec((1, LANES), lambda j, i: (0, j))],
        out_specs=[pl.BlockSpec((tr, LANES), lambda j, i: (i, j))] * n_out,
        out_shape=[_sds((S, C))] * n_out,
        scratch_shapes=[pltpu.VMEM((tr + H, LANES), f32)],
        compiler_params=_params(("parallel", "arbitrary")),
    )(x, x, w, b)


def _conv_bwd(x, w, dact, pre, *, silu, name, dx_dtype=f32, tr=512):
    S, C = x.shape
    K = w.shape[0]
    H = CONV_HALO
    nh = tr // H
    n_i = S // tr
    kp = -(-K // 8) * 8

    def dsilu(p):
        s = jax.nn.sigmoid(p)
        return s * (1.0 + p * (1.0 - s))

    def body(*refs):
        if silu:
            xp_ref, xc_ref, w_ref, dc_ref, dn_ref, pc_ref, pn_ref, dx_ref, dw_ref, db_ref, xs, ds = refs
        else:
            xp_ref, xc_ref, w_ref, dc_ref, dn_ref, dx_ref, dw_ref, db_ref, xs, ds = refs
        i = pl.program_id(1)
        xs[pl.ds(0, H), :] = jnp.where(i > 0, xp_ref[...], 0.0)
        xs[pl.ds(H, tr), :] = xc_ref[...]
        dcur = dc_ref[...]
        dnext = dn_ref[...]
        if silu:
            dcur = dcur * dsilu(pc_ref[...])
            dnext = dnext * dsilu(pn_ref[...])
        ds[pl.ds(0, tr), :] = dcur
        ds[pl.ds(tr, H), :] = jnp.where(i < n_i - 1, dnext, 0.0)
        acc = jnp.zeros((tr, LANES), f32)
        for k in range(K):
            acc = acc + ds[pl.ds(K - 1 - k, tr), :] * w_ref[pl.ds(k, 1), :]
        dx_ref[...] = acc.astype(dx_ref.dtype)

        @pl.when(i == 0)
        def _():
            dw_ref[...] = jnp.zeros_like(dw_ref)
            db_ref[...] = jnp.zeros_like(db_ref)

        for k in range(K):
            dw_ref[pl.ds(k, 1), :] += jnp.sum(dcur * xs[pl.ds(H - (K - 1) + k, tr), :], 0, keepdims=True)
        db_ref[...] += jnp.sum(dcur, 0, keepdims=True)

    prev = pl.BlockSpec((H, LANES), lambda j, i: (jnp.maximum(i * nh - 1, 0), j))
    cur = pl.BlockSpec((tr, LANES), lambda j, i: (i, j))
    nxt = pl.BlockSpec((H, LANES), lambda j, i: (jnp.minimum((i + 1) * nh, n_i * nh - 1), j))
    in_specs = [prev, cur, pl.BlockSpec((K, LANES), lambda j, i: (0, j)), cur, nxt]
    args = [x, x, w, dact, dact]
    if silu:
        in_specs += [cur, nxt]
        args += [pre, pre]
    dx, dw, db = pl.pallas_call(
        body, name=name, grid=(C // LANES, n_i), in_specs=in_specs,
        out_specs=[cur, pl.BlockSpec((kp, LANES), lambda j, i: (0, j)), pl.BlockSpec((1, LANES), lambda j, i: (0, j))],
        out_shape=[_sds((S, C), dx_dtype), _sds((kp, C)), _sds((1, C))],
        scratch_shapes=[pltpu.VMEM((tr + H, LANES), f32), pltpu.VMEM((tr + H, LANES), f32)],
        compiler_params=_params(("parallel", "arbitrary")),
    )(*args)
    return dx, dw[:K], db


def _dot(a, b):
    return jnp.dot(a.astype(bf16), b.astype(bf16), preferred_element_type=f32)


def _dot_nt(a, b):
    return lax.dot_general(a.astype(bf16), b.astype(bf16), (((1,), (1,)), ((), ())), preferred_element_type=f32)


def _dot_tn(a, b):
    return lax.dot_general(a.astype(bf16), b.astype(bf16), (((0,), (0,)), ((), ())), preferred_element_type=f32)


def _softplus(x):
    return jnp.maximum(x, 0.0) + jnp.log(1.0 + jnp.exp(-jnp.abs(x)))


def _tri(q):
    i = lax.broadcasted_iota(jnp.int32, (q, q), 0)
    j = lax.broadcasted_iota(jnp.int32, (q, q), 1)
    return i >= j


def _ssd_prep(dtraw, dt_bias, a_log):
    q = dtraw.shape[0]
    dt = _softplus(dtraw + dt_bias)
    A = -jnp.exp(a_log)
    tri = _tri(q)
    cs = jnp.dot(tri.astype(f32), dt * A, preferred_element_type=f32, precision=lax.Precision.HIGHEST)
    return dt, A, cs, cs.T, tri


def _expand(cols, h0, n, width):
    q = cols.shape[0]
    return jnp.concatenate([jnp.broadcast_to(cols[:, h0 + r:h0 + r + 1], (q, width)) for r in range(n)], axis=1)


def _ssd_fwd(xbc, dtraw, dt_bias, a_log, d_skip, di, name):
    S, CD = xbc.shape
    Q, N, G = SSM_CHUNK, SSM_STATE, SSM_GROUPS
    nc = S // Q
    nh = di // HEAD_DIM
    R = nh // G
    gw = R * HEAD_DIM

    def body(xbc_ref, dt_ref, bias_ref, alog_ref, dsk_ref, y_ref, hin_ref, state):
        c = pl.program_id(0)

        @pl.when(c == 0)
        def _():
            state[...] = jnp.zeros_like(state)

        hin_ref[...] = state[...]
        dt, A, cs, csT, tri = _ssd_prep(dt_ref[...], bias_ref[...], alog_ref[...])
        dsk = dsk_ref[...]
        ecs = jnp.exp(cs)
        dend = jnp.exp(cs[Q - 1:Q, :] - cs)
        elast = jnp.exp(cs[Q - 1:Q, :])
        for g in range(G):
            h0 = g * R
            Bg = xbc_ref[:, pl.ds(di + g * N, N)]
            Cg = xbc_ref[:, pl.ds(di + G * N + g * N, N)]
            xg = xbc_ref[:, pl.ds(g * gw, gw)]
            Hg = state[pl.ds(g * gw, gw), :]
            Gm = _dot_nt(Cg, Bg)
            xdt = xg * _expand(dt, h0, R, HEAD_DIM)
            yoff = _dot_nt(Cg, Hg) * _expand(ecs, h0, R, HEAD_DIM)
            ys = []
            for r in range(R):
                h = h0 + r
                L = jnp.exp(jnp.where(tri, cs[:, h:h + 1] - csT[h:h + 1, :], -jnp.inf))
                ys.append(_dot(Gm * L, xdt[:, r * HEAD_DIM:(r + 1) * HEAD_DIM]))
            y = jnp.concatenate(ys, axis=1) + yoff + xg * _expand(dsk, h0, R, HEAD_DIM)
            y_ref[:, pl.ds(g * gw, gw)] = y
            hnew = _dot_tn(xdt * _expand(dend, h0, R, HEAD_DIM), Bg)
            escale = jnp.concatenate([jnp.broadcast_to(elast[:, h0 + r:h0 + r + 1], (HEAD_DIM, N)) for r in range(R)], axis=0)
            state[pl.ds(g * gw, gw), :] = escale * Hg + hnew

    vec = pl.BlockSpec((1, LANES), lambda c: (0, 0))
    return pl.pallas_call(
        body, name=name, grid=(nc,),
        in_specs=[pl.BlockSpec((Q, CD), lambda c: (c, 0)), pl.BlockSpec((Q, LANES), lambda c: (c, 0)), vec, vec, vec],
        out_specs=[pl.BlockSpec((Q, di), lambda c: (c, 0)), pl.BlockSpec((None, di, N), lambda c: (c, 0, 0))],
        out_shape=[_sds((S, di)), _sds((nc, di, N))],
        scratch_shapes=[pltpu.VMEM((di, N), f32)],
        compiler_params=_params(("arbitrary",)),
    )(xbc, dtraw, dt_bias, a_log, d_skip)


def _ssd_bwd(xbc, dtraw, dt_bias, a_log, d_skip, hin, dy, di, name):
    S, CD = xbc.shape
    Q, N, G = SSM_CHUNK, SSM_STATE, SSM_GROUPS
    nc = S // Q
    nh = di // HEAD_DIM
    R = nh // G
    gw = R * HEAD_DIM
    P = HEAD_DIM

    def body(xbc_ref, dt_ref, bias_ref, alog_ref, dsk_ref, hin_ref, dy_ref, dxbc_ref, ddt_ref, dA_ref, ddsk_ref, dtb_ref, dstate):
        c = pl.program_id(0)

        @pl.when(c == 0)
        def _():
            dstate[...] = jnp.zeros_like(dstate)
            dA_ref[...] = jnp.zeros_like(dA_ref)
            ddsk_ref[...] = jnp.zeros_like(ddsk_ref)
            dtb_ref[...] = jnp.zeros_like(dtb_ref)

        dtraw_v = dt_ref[...]
        dt, A, cs, csT, tri = _ssd_prep(dtraw_v, bias_ref[...], alog_ref[...])
        dsk = dsk_ref[...]
        ecs = jnp.exp(cs)
        dend = jnp.exp(cs[Q - 1:Q, :] - cs)
        elast = jnp.exp(cs[Q - 1:Q, :])
        lane = lax.broadcasted_iota(jnp.int32, (1, LANES), 1)
        row = lax.broadcasted_iota(jnp.int32, (Q, 1), 0)
        dcs = jnp.zeros((Q, LANES), f32)
        rsx = jnp.zeros((Q, LANES), f32)
        ddsk = jnp.zeros((1, LANES), f32)
        for g in range(G):
            h0 = g * R
            Bg = xbc_ref[:, pl.ds(di + g * N, N)]
            Cg = xbc_ref[:, pl.ds(di + G * N + g * N, N)]
            xg = xbc_ref[:, pl.ds(g * gw, gw)]
            dyg = dy_ref[:, pl.ds(g * gw, gw)]
            Hg = hin_ref[pl.ds(g * gw, gw), :]
            dHg = dstate[pl.ds(g * gw, gw), :]
            dt_e = _expand(dt, h0, R, P)
            ecs_e = _expand(ecs, h0, R, P)
            dend_e = _expand(dend, h0, R, P)
            Gm = _dot_nt(Cg, Bg)
            xdt = xg * dt_e
            yoff_raw = _dot_nt(Cg, Hg)
            dye = dyg * ecs_e
            bdh = _dot_nt(Bg, dHg)
            dC = _dot(dye, Hg)
            dB = _dot(xdt * dend_e, dHg)
            dHin = _dot_tn(dye, Cg)
            dxdt_state = dend_e * bdh
            t_off = dyg * yoff_raw * ecs_e
            t_end = bdh * xdt * dend_e
            dG = jnp.zeros((Q, Q), f32)
            dxs = []
            for r in range(R):
                h = h0 + r
                sl = slice(r * P, (r + 1) * P)
                L = jnp.exp(jnp.where(tri, cs[:, h:h + 1] - csT[h:h + 1, :], -jnp.inf))
                M = Gm * L
                dyh = dyg[:, sl]
                dM = _dot_nt(dyh, xdt[:, sl])
                dxdt = _dot_tn(M, dyh) + dxdt_state[:, sl]
                dG = dG + dM * L
                E = dM * M
                w_end = jnp.sum(t_end[:, sl], 1, keepdims=True)
                hh = jnp.sum(dHg[sl, :] * Hg[sl, :], keepdims=True) * elast[:, h:h + 1]
                d = (jnp.sum(E, 1, keepdims=True) - jnp.sum(E.T, 1, keepdims=True)
                     + jnp.sum(t_off[:, sl], 1, keepdims=True) - w_end
                     + jnp.where(row == Q - 1, jnp.sum(w_end, keepdims=True) + hh, 0.0))
                onehot = (lane == h).astype(f32)
                dcs = dcs + d * onehot
                rsx = rsx + jnp.sum(dxdt * xg[:, sl], 1, keepdims=True) * onehot
                ddsk = ddsk + jnp.sum(dyh * xg[:, sl], keepdims=True) * onehot
                dxs.append(dxdt * dt_e[:, sl] + dyh * dsk[:, h:h + 1])
            dxbc_ref[:, pl.ds(g * gw, gw)] = jnp.concatenate(dxs, axis=1)
            dxbc_ref[:, pl.ds(di + g * N, N)] = dB + _dot_tn(dG, Cg)
            dxbc_ref[:, pl.ds(di + G * N + g * N, N)] = dC + _dot(dG, Bg)
            escale = jnp.concatenate([jnp.broadcast_to(elast[:, h0 + r:h0 + r + 1], (P, N)) for r in range(R)], axis=0)
            dstate[pl.ds(g * gw, gw), :] = escale * dHg + dHin
        da = lax.dot_general(tri.astype(f32), dcs, (((0,), (0,)), ((), ())), preferred_element_type=f32,
                             precision=lax.Precision.HIGHEST)
        ddt = da * A + rsx
        ddtraw = ddt * jax.nn.sigmoid(dtraw_v + bias_ref[...])
        ddt_ref[...] = ddtraw.astype(ddt_ref.dtype)
        dA_ref[...] += jnp.sum(da * dt, 0, keepdims=True) * A
        ddsk_ref[...] += ddsk
        dtb_ref[...] += jnp.sum(ddtraw, 0, keepdims=True)

    vec = pl.BlockSpec((1, LANES), lambda c: (0, 0))
    rev = lambda c: (nc - 1 - c, 0)
    return pl.pallas_call(
        body, name=name, grid=(nc,),
        in_specs=[pl.BlockSpec((Q, CD), rev), pl.BlockSpec((Q, LANES), rev), vec, vec, vec,
                  pl.BlockSpec((None, di, N), lambda c: (nc - 1 - c, 0, 0)), pl.BlockSpec((Q, di), rev)],
        out_specs=[pl.BlockSpec((Q, CD), rev), pl.BlockSpec((Q, LANES), rev), vec, vec, vec],
        out_shape=[_sds((S, CD)), _sds((S, LANES), bf16), _sds((1, LANES)), _sds((1, LANES)), _sds((1, LANES))],
        scratch_shapes=[pltpu.VMEM((di, N), f32)],
        compiler_params=_params(("arbitrary",)),
    )(xbc, dtraw, dt_bias, a_log, d_skip, hin, dy)


def _t5_bucket_np(dist):
    max_exact = REL_BUCKETS // 2
    n = np.maximum(dist, 1).astype(np.float32)
    large = np.float32(max_exact) + np.log(n / np.float32(max_exact)) / np.float32(math.log(REL_MAX_DIST / max_exact)) * np.float32(REL_BUCKETS - max_exact)
    large = np.minimum(large.astype(np.int32), REL_BUCKETS - 1)
    return np.where(dist < max_exact, dist, large)


def _bucket_onehot():
    i = np.arange(ATT_BLK)[:, None]
    j = np.arange(2 * ATT_BLK)[None, :]
    delta = np.maximum(ATT_BLK + i - j, 0)
    out = np.zeros((len(ATT_DILATIONS), REL_BUCKETS, ATT_BLK * 2 * ATT_BLK), np.float32)
    for gi, d in enumerate(ATT_DILATIONS):
        b = _t5_bucket_np(delta * d).reshape(-1)
        out[gi, b, np.arange(b.size)] = 1.0
    return out


def _exact_mm(a, b, *, name, tb=False):
    M, K = a.shape
    N = b.shape[0] if tb else b.shape[1]
    tn = _tile(N, 4096, LANES)
    dn = (((1,), (1 if tb else 0,)), ((), ()))

    def body(a_ref, b_ref, o_ref):
        o_ref[...] = lax.dot_general(a_ref[...], b_ref[...], dn, preferred_element_type=f32,
                                     precision=lax.Precision.HIGHEST)

    return pl.pallas_call(
        body, name=name, grid=(N // tn,),
        in_specs=[pl.BlockSpec((M, K), lambda j: (0, 0)),
                  pl.BlockSpec((tn, K), lambda j: (j, 0)) if tb else pl.BlockSpec((K, tn), lambda j: (0, j))],
        out_specs=pl.BlockSpec((M, tn), lambda j: (0, j)), out_shape=_sds((M, N)),
        compiler_params=_params(("parallel",)),
    )(a, b)


def _band_penalty():
    i = np.arange(ATT_BLK)[:, None]
    j = np.arange(2 * ATT_BLK)[None, :]
    delta = ATT_BLK + i - j
    return np.where((delta >= 0) & (delta <= ATT_BLK), 0.0, -np.inf).astype(np.float32)


def _first_block_keep(n):
    col = lax.broadcasted_iota(jnp.int32, (ATT_BLK, 2 * ATT_BLK), 1)
    return (col >= ATT_BLK) | (n > 0)


ATT_SCALE = HEAD_DIM ** -0.5


def _rows(ref, r, d):
    return ref[...] if d == 1 else ref[pl.ds(r, ATT_BLK, stride=d), :]


def _set_rows(ref, r, d, val):
    if d == 1:
        ref[...] = val
    else:
        ref[pl.ds(r, ATT_BLK, stride=d), :] = val


def _attn_width(d, D):
    return D if d == 1 else LANES


def _over_residues(d, one, unroll=1):
    if d == 1:
        one(0)
    else:
        lax.fori_loop(0, d, lambda r, c: (one(r), c)[1], 0, unroll=unroll)


def _attn_fwd(q, k, v, bias, d, name):
    S, D = q.shape
    nb = S // (d * ATT_BLK)
    W = _attn_width(d, D)
    HB = W // HEAD_DIM

    def body(q_ref, kp_ref, kc_ref, vp_ref, vc_ref, b_ref, o_ref, lse_ref):
        keep = _first_block_keep(pl.program_id(1))

        def one(r):
            qs = (_rows(q_ref, r, d) * ATT_SCALE).astype(bf16)
            kcat = jnp.concatenate([_rows(kp_ref, r, d), _rows(kc_ref, r, d)], axis=0).astype(bf16)
            vcat = jnp.concatenate([_rows(vp_ref, r, d), _rows(vc_ref, r, d)], axis=0).astype(bf16)
            outs, lses = [], []
            for h in range(HB):
                sl = slice(h * HEAD_DIM, (h + 1) * HEAD_DIM)
                s = jnp.where(keep, _dot_nt(qs[:, sl], kcat[:, sl]) + b_ref[h], -jnp.inf)
                m = jnp.max(s, -1, keepdims=True)
                p = jnp.exp(s - m)
                l = jnp.sum(p, -1, keepdims=True)
                outs.append(_dot(p, vcat[:, sl]) / l)
                lses.append(jnp.broadcast_to(m + jnp.log(l), (ATT_BLK, HEAD_DIM)))
            _set_rows(o_ref, r, d, jnp.concatenate(outs, axis=1))
            _set_rows(lse_ref, r, d, jnp.concatenate(lses, axis=1))

        _over_residues(d, one, unroll=4)

    cur = pl.BlockSpec((ATT_BLK * d, W), lambda j, n: (n, j))
    prev = pl.BlockSpec((ATT_BLK * d, W), lambda j, n: (jnp.maximum(n - 1, 0), j))
    return pl.pallas_call(
        body, name=name, grid=(D // W, nb),
        in_specs=[cur, prev, cur, prev, cur, pl.BlockSpec((HB, ATT_BLK, 2 * ATT_BLK), lambda j, n: (j, 0, 0))],
        out_specs=[cur, cur], out_shape=[_sds((S, D)), _sds((S, D))],
        compiler_params=_params(("parallel", "arbitrary")),
    )(q, k, k, v, v, bias)


def _attn_bwd(q, k, v, bias, att, datt, lse_tot, d, name):
    S, D = q.shape
    nb = S // (d * ATT_BLK)
    H = D // HEAD_DIM
    W = _attn_width(d, D)
    HB = W // HEAD_DIM

    def body(q_ref, kp_ref, kc_ref, vp_ref, vc_ref, b_ref, o_ref, do_ref, lse_ref,
             dq_ref, dk_ref, dv_ref, db_ref, carry_k, carry_v):
        n = pl.program_id(1)

        @pl.when(n == 0)
        def _():
            carry_k[...] = jnp.zeros_like(carry_k)
            carry_v[...] = jnp.zeros_like(carry_v)
            db_ref[...] = jnp.zeros_like(db_ref)

        @pl.when(n < nb)
        def _():
            keep = _first_block_keep(n)

            def one(r):
                qs = (_rows(q_ref, r, d) * ATT_SCALE).astype(bf16)
                kcat = jnp.concatenate([_rows(kp_ref, r, d), _rows(kc_ref, r, d)], axis=0).astype(bf16)
                vcat = jnp.concatenate([_rows(vp_ref, r, d), _rows(vc_ref, r, d)], axis=0).astype(bf16)
                dov, lsev = _rows(do_ref, r, d), _rows(lse_ref, r, d)
                dsum_all = dov * _rows(o_ref, r, d)
                dob = dov.astype(bf16)
                dqs, dks, dvs = [], [], []
                for h in range(HB):
                    sl = slice(h * HEAD_DIM, (h + 1) * HEAD_DIM)
                    s = jnp.where(keep, _dot_nt(qs[:, sl], kcat[:, sl]) + b_ref[h], -jnp.inf)
                    p = jnp.exp(s - lsev[:, h * HEAD_DIM:h * HEAD_DIM + 1])
                    dp = _dot_nt(dob[:, sl], vcat[:, sl])
                    ds = p * (dp - jnp.sum(dsum_all[:, sl], 1, keepdims=True))
                    db_ref[h] += ds
                    dqs.append(_dot(ds, kcat[:, sl]) * ATT_SCALE)
                    dks.append(_dot_tn(ds, qs[:, sl]))
                    dvs.append(_dot_tn(p, dob[:, sl]))
                _set_rows(dq_ref, r, d, jnp.concatenate(dqs, axis=1))
                dk = jnp.concatenate(dks, axis=1)
                dv = jnp.concatenate(dvs, axis=1)
                _set_rows(dk_ref, r, d, carry_k[r] + dk[:ATT_BLK])
                _set_rows(dv_ref, r, d, carry_v[r] + dv[:ATT_BLK])
                carry_k[r] = dk[ATT_BLK:]
                carry_v[r] = dv[ATT_BLK:]

            _over_residues(d, one, unroll=2)

        @pl.when(n == nb)
        def _():
            def last(r):
                _set_rows(dk_ref, r, d, carry_k[r])
                _set_rows(dv_ref, r, d, carry_v[r])

            _over_residues(d, last)

    nq = lambda n: jnp.minimum(n, nb - 1)
    cur = pl.BlockSpec((ATT_BLK * d, W), lambda j, n: (nq(n), j))
    prev = pl.BlockSpec((ATT_BLK * d, W), lambda j, n: (jnp.maximum(nq(n) - 1, 0), j))
    done = pl.BlockSpec((ATT_BLK * d, W), lambda j, n: (jnp.maximum(n - 1, 0), j))
    bspec = pl.BlockSpec((HB, ATT_BLK, 2 * ATT_BLK), lambda j, n: (j, 0, 0))
    return pl.pallas_call(
        body, name=name, grid=(D // W, nb + 1),
        in_specs=[cur, prev, cur, prev, cur, bspec, cur, cur, cur],
        out_specs=[cur, done, done, bspec],
        out_shape=[_sds((S, D)), _sds((S, D)), _sds((S, D)), _sds((H, ATT_BLK, 2 * ATT_BLK))],
        scratch_shapes=[pltpu.VMEM((d, ATT_BLK, W), f32), pltpu.VMEM((d, ATT_BLK, W), f32)],
        compiler_params=_params(("arbitrary", "arbitrary")),
    )(q, k, k, v, v, bias, att, datt, lse_tot)


def _attn_combine(os_, lses, name):
    def fn(rv, vv):
        o0, o1, o2, l0, l1, l2 = rv
        m = jnp.maximum(jnp.maximum(l0, l1), l2)
        e0, e1, e2 = jnp.exp(l0 - m), jnp.exp(l1 - m), jnp.exp(l2 - m)
        tot = e0 + e1 + e2
        return [(e0 * o0 + e1 * o1 + e2 * o2) / tot, m + jnp.log(tot)], []
    w = os_[0].shape[1]
    (att, lse), _ = _rowwise(name, fn, list(os_) + list(lses), [], [(w, f32), (w, f32)], [])
    return att, lse


ANY = pl.BlockSpec(memory_space=pl.ANY)


def _all_gather(vs, name):
    n = len(vs)

    def body(*refs):
        x_refs, out_refs = refs[:n], refs[n:2 * n]
        send_sems, recv_sems, local_sems = refs[2 * n:]
        x, y, c = lax.axis_index("x"), lax.axis_index("y"), lax.axis_index("c")
        me, sibling = (x, y, c), (x, y, 1 - c)
        chips = [(1 - x, y), (x, 1 - y), (1 - x, 1 - y)]

        def slot(i, px, py, pc):
            return out_refs[i].at[4 * px + 2 * py + pc]

        def copy(i, k, block, to, src=None):
            return pltpu.make_async_remote_copy(
                src_ref=slot(i, *block) if src is None else src, dst_ref=slot(i, *block),
                send_sem=send_sems.at[i, k], recv_sem=recv_sems.at[i, k], device_id=to, device_id_type=MESH)

        mine = [pltpu.make_async_copy(x_refs[i], slot(i, *me), local_sems.at[i]) for i in range(n)]
        for cp in mine:
            cp.start()
        first = []
        for i in range(n):
            first.append(copy(i, 0, me, sibling, src=x_refs[i]))
            first += [copy(i, 1 + j, me, (*chip, c), src=x_refs[i]) for j, chip in enumerate(chips)]
        for cp in first:
            cp.start()
        passed = []
        for i in range(n):
            for j, chip in enumerate(chips):
                copy(i, 1 + j, (*chip, c), me).wait_recv()
                cp = copy(i, 4 + j, (*chip, c), sibling)
                cp.start()
                passed.append(cp)
        for i in range(n):
            copy(i, 0, sibling, me).wait_recv()
            for j, chip in enumerate(chips):
                copy(i, 4 + j, (*chip, 1 - c), me).wait_recv()
        for cp in first + passed:
            cp.wait_send()
        for cp in mine:
            cp.wait()

    return pl.pallas_call(
        body, name=name, out_shape=[_sds((N_DEV,) + v.shape, v.dtype) for v in vs], in_specs=[ANY] * n,
        out_specs=[ANY] * n,
        scratch_shapes=[pltpu.SemaphoreType.DMA((n, 7)), pltpu.SemaphoreType.DMA((n, 7)), pltpu.SemaphoreType.DMA((n,))],
    )(*vs)


def _rs_sibling(parts, name):
    n = len(parts)

    def body(*refs):
        p_refs, out_refs = refs[:n], refs[n:2 * n]
        send_sems, recv_sems = refs[2 * n:]
        x, y, c = lax.axis_index("x"), lax.axis_index("y"), lax.axis_index("c")
        cps = [pltpu.make_async_remote_copy(
            src_ref=p_refs[i].at[k, 1 - c], dst_ref=out_refs[i].at[k], send_sem=send_sems.at[i, k],
            recv_sem=recv_sems.at[i, k], device_id=(x, y, 1 - c), device_id_type=MESH)
            for i in range(n) for k in range(4)]
        for cp in cps:
            cp.start()
        for cp in cps:
            cp.wait()

    return pl.pallas_call(
        body, name=name, out_shape=[_sds((4,) + p.shape[2:], p.dtype) for p in parts], in_specs=[ANY] * n,
        out_specs=[ANY] * n,
        scratch_shapes=[pltpu.SemaphoreType.DMA((n, 4)), pltpu.SemaphoreType.DMA((n, 4))],
    )(*parts)


def _rs_chips(ts, name):
    n = len(ts)

    def body(*refs):
        t_refs, out_refs = refs[:n], refs[n:2 * n]
        send_sems, recv_sems, local_sems = refs[2 * n:]
        x, y, c = lax.axis_index("x"), lax.axis_index("y"), lax.axis_index("c")
        mine = 2 * x + y
        local = [pltpu.make_async_copy(t_refs[i].at[mine], out_refs[i].at[mine], local_sems.at[i]) for i in range(n)]
        for cp in local:
            cp.start()
        chips = [(1 - x, y), (x, 1 - y), (1 - x, 1 - y)]
        cps = [pltpu.make_async_remote_copy(
            src_ref=t_refs[i].at[2 * px + py], dst_ref=out_refs[i].at[mine], send_sem=send_sems.at[i, j],
            recv_sem=recv_sems.at[i, j], device_id=(px, py, c), device_id_type=MESH)
            for i in range(n) for j, (px, py) in enumerate(chips)]
        for cp in cps:
            cp.start()
        for cp in cps:
            cp.wait()
        for cp in local:
            cp.wait()

    return pl.pallas_call(
        body, name=name, out_shape=[_sds(t.shape, t.dtype) for t in ts], in_specs=[ANY] * n, out_specs=[ANY] * n,
        scratch_shapes=[pltpu.SemaphoreType.DMA((n, 3)), pltpu.SemaphoreType.DMA((n, 3)), pltpu.SemaphoreType.DMA((n,))],
    )(*ts)


def _pair_add(part, recv, c_arr, name):
    _, _, R, C = part.shape
    tr = _tile(R, PACK_ROW_TILE, 16)

    def body(c_ref, p_ref, r_ref, o_ref):
        o_ref[...] = (p_ref[...] + r_ref[...]).astype(o_ref.dtype)

    return pl.pallas_call(
        body, name=name,
        grid_spec=pltpu.PrefetchScalarGridSpec(
            num_scalar_prefetch=1, grid=(4, R // tr),
            in_specs=[pl.BlockSpec((None, None, tr, C), lambda k, i, c_ref: (k, c_ref[0], i, 0)),
                      pl.BlockSpec((None, tr, C), lambda k, i, c_ref: (k, i, 0))],
            out_specs=pl.BlockSpec((None, tr, C), lambda k, i, c_ref: (k, i, 0))),
        out_shape=_sds((4, R, C), bf16),
        compiler_params=_params(("parallel", "parallel")),
    )(c_arr, part, recv)


def _sum_slots(t, name):
    n, R, C = t.shape
    tr = _tile(R, PACK_ROW_TILE, 16)

    def body(t_ref, o_ref):
        acc = t_ref[0].astype(f32)
        for k in range(1, n):
            acc = acc + t_ref[k].astype(f32)
        o_ref[...] = acc

    return pl.pallas_call(
        body, name=name, grid=(R // tr,),
        in_specs=[pl.BlockSpec((n, tr, C), lambda i: (0, i, 0))],
        out_specs=pl.BlockSpec((tr, C), lambda i: (i, 0)), out_shape=_sds((R, C)),
        compiler_params=_params(("parallel",)),
    )(t)


def _reduce_scatter(parts, c_arr, name):
    parts4 = [p.reshape((4, 2) + p.shape[1:]) for p in parts]
    recv = _rs_sibling(parts4, name + "_sibling")
    ts = [_pair_add(p, r, c_arr, f"{name}_pair_{i}") for i, (p, r) in enumerate(zip(parts4, recv))]
    got = _rs_chips(ts, name + "_chips")
    return [_sum_slots(g, f"{name}_sum_{i}") for i, g in enumerate(got)]


def _adamw(w, g, m, v, name):
    def fn(rv, vv):
        wv, gv, mv, vvv = rv
        m2 = ADAM_B1 * mv + (1.0 - ADAM_B1) * gv
        v2 = ADAM_B2 * vvv + (1.0 - ADAM_B2) * jnp.square(gv)
        m_hat = m2 / (1.0 - ADAM_B1 ** ADAM_STEP)
        v_hat = v2 / (1.0 - ADAM_B2 ** ADAM_STEP)
        delta = -ADAM_LR * (m_hat / (jnp.sqrt(v_hat) + ADAM_EPS) + ADAM_WD * wv)
        return [delta, m2, v2], []
    c = w.shape[1]
    (delta, m2, v2), _ = _rowwise(name, fn, [w, g, m, v], [], [(c, f32)] * 3, [])
    return delta, m2, v2


BIG = (("hy_w_in", "col"), ("hy_w_out", "row"), ("cv_w_pw1", "col"), ("cv_w_pw2", "row"),
       ("ffn_w_gate", "col"), ("ffn_w_up", "col"), ("ffn_w_down", "row"))


def _full_from_blocks(g, kind, shp):
    g = g.reshape((N_DEV,) + tuple(shp))
    if kind == "col":
        return jnp.transpose(g, (1, 2, 0, 3)).reshape(shp[0], shp[1], N_DEV * shp[2])
    return jnp.transpose(g, (1, 0, 2, 3)).reshape(shp[0], N_DEV * shp[1], shp[2])


def _blocks_from_full(g, kind, shp):
    if kind == "col":
        t = jnp.transpose(g.reshape(shp[0], shp[1], N_DEV, shp[2]), (2, 0, 1, 3))
    else:
        t = jnp.transpose(g.reshape(shp[0], N_DEV, shp[1], shp[2]), (1, 0, 2, 3))
    return t.reshape(N_DEV, shp[0] * shp[1], shp[2])


class _VecPack:
    def __init__(self, shapes):
        self.shapes = [tuple(s) for s in shapes]
        self.sizes = [int(np.prod(s)) for s in self.shapes]
        total = sum(self.sizes)
        self.rows = -(-(-(-total // LANES)) // 8) * 8
        self.total = total

    def pack(self, arrays):
        flat = jnp.concatenate([a.astype(f32).reshape(-1) for a in arrays])
        flat = jnp.pad(flat, (0, self.rows * LANES - self.total))
        return flat.reshape(self.rows, LANES)

    def unpack(self, packed):
        flat = packed.reshape(-1)
        out, off = [], 0
        for shp, n in zip(self.shapes, self.sizes):
            out.append(flat[off:off + n].reshape(shp))
            off += n
        return out


def _row(v):
    return v.reshape(1, -1)


def _pad_lanes(v):
    v = v.reshape(1, -1)
    return jnp.pad(v, ((0, 0), (0, LANES - v.shape[1])))


def _ffn_fwd(h, w_gu, w_down, tag):
    F = w_down.shape[0]
    au = _mm(h, w_gu, name=f"ffn_gu_{tag}")
    (f,), _ = _rowwise(f"swiglu_{tag}", lambda rv, vv: ([_silu(rv[0]) * rv[1]], []),
                       [(au, 0, F), (au, 1, F)], [], [(F, bf16)], [], sub=16)
    out = _mm(f, w_down, name=f"ffn_down_{tag}")
    return out, (au, f)


def _ffn_bwd(h, w_gu, w_down, saved, dout, tag):
    au, f = saved
    F = w_down.shape[0]
    df = _mm(dout, w_down, tb=True, name=f"ffn_down_dx_{tag}")
    dw_down = _mm(f, dout, ta=True, name=f"ffn_down_dw_{tag}")

    def fn(rv, vv):
        a, u, d = rv
        _, vjp = jax.vjp(lambda a_, u_: _silu(a_) * u_, a, u)
        da, du = vjp(d)
        return [da, du], []

    (da, du), _ = _rowwise(f"swiglu_bwd_{tag}", fn, [(au, 0, F), (au, 1, F), df], [], [(F, bf16), (F, bf16)], [],
                           sub=16)
    dau = jnp.concatenate([da, du], axis=1)
    dh = _mm(dau, w_gu, tb=True, name=f"ffn_gu_dx_{tag}")
    dw_gu = _mm(h, dau, ta=True, name=f"ffn_gu_dw_{tag}")
    return dh, dw_gu, dw_down


def _local_step(x, target, mod, W, small):
    S, D = x.shape
    w_in, w_out, w_pw1, w_pw2, w_gate, w_up, w_down = W
    di = small["hy_ssm_norm_g"].shape[-1]
    nh = small["hy_dt_bias"].shape[-1]
    cd = small["hy_conv_b"].shape[-1]
    F = w_down.shape[1]
    m = [[_row(mod[i, j]) for j in range(6)] for i in range(2)]

    o = 0
    seg = {}
    for nm, wd in (("z", di), ("xbc", cd), ("dt", nh), ("q0", D), ("q1", D), ("q2", D), ("k", D), ("v", D)):
        seg[nm] = (o, wd)
        o += wd
    wseg = {nm: w_in[0][:, a:a + wd] for nm, (a, wd) in seg.items()}
    wseg["dt"] = jnp.pad(wseg["dt"], ((0, 0), (0, LANES - nh)))
    w_out_y, w_out_a = w_out[0][:di], w_out[0][di:]
    w_gu = [jnp.concatenate([w_gate[i], w_up[i]], axis=1) for i in range(2)]

    g_mix = [_row(small["norm_mix_g"][i]) for i in range(2)]
    g_ffn = [_row(small["norm_ffn_g"][i]) for i in range(2)]
    conv_w, conv_b = small["hy_conv_w_full"], _row(small["hy_conv_b"][0])
    dt_bias, a_log, d_skip = (_pad_lanes(small[k][0]) for k in ("hy_dt_bias", "hy_a_log", "hy_d_skip"))
    g_ssm = _row(small["hy_ssm_norm_g"][0])
    onehot = jnp.asarray(_bucket_onehot())
    rel_t = small["rel_table"].T
    H = D // HEAD_DIM
    bias = [_exact_mm(rel_t[gi * H:(gi + 1) * H], onehot[gi], name=f"rel_bias_{gi}")
            .reshape(H, ATT_BLK, 2 * ATT_BLK) + _band_penalty() for gi in range(3)]

    h1 = _adaln_fwd(x, g_mix[0], m[0][1], m[0][0], "adaln_mix0")
    proj = {nm: _mm(h1, wseg[nm], name=f"in_{nm}") for nm in seg}
    xbc_pre, xbc = _conv_fwd(proj["xbc"], conv_w, conv_b, silu=True, name="ssm_conv")
    y, hin = _ssd_fwd(xbc, proj["dt"], dt_bias, a_log, d_skip, di, "ssd_fwd")
    (yg,), _ = _rowwise("ssm_gate", lambda rv, vv: ([_gate_f(rv[0], rv[1], vv[0])], []),
                        [y, proj["z"]], [g_ssm], [(di, bf16)], [], sub=16)
    og = [_attn_fwd(proj[f"q{gi}"], proj["k"], proj["v"], bias[gi], d, f"attn_fwd_{gi}")
          for gi, d in enumerate(ATT_DILATIONS)]
    att, lse_tot = _attn_combine([a for a, _ in og], [b for _, b in og], "attn_combine")
    mix0 = _mm(att, w_out_a, add=_mm(yg, w_out_y, name="out_y"), name="out_a")
    x1 = _resid_fwd(x, m[0][2], mix0, "resid_mix0")
    h2 = _adaln_fwd(x1, g_ffn[0], m[0][4], m[0][3], "adaln_ffn0")
    f0, ffn0_saved = _ffn_fwd(h2, w_gu[0], w_down[0], "0")
    x2 = _resid_fwd(x1, m[0][5], f0, "resid_ffn0")

    h3 = _adaln_fwd(x2, g_mix[1], m[1][1], m[1][0], "adaln_mix1")
    pw1 = _mm(h3, w_pw1[0], bias=_row(small["cv_b_pw1_full"]), name="cv_pw1")
    (u,), _ = _rowwise("cv_glu", lambda rv, vv: ([rv[0] * jax.nn.sigmoid(rv[1])], []),
                       [(pw1, 0, D), (pw1, 1, D)], [], [(D, f32)], [])
    (u2,) = _conv_fwd(u, small["cv_w_dw_full"], _row(small["cv_b_dw_full"]), silu=False, name="cv_dw")
    ln_g, ln_b = _row(small["cv_ln_g_full"]), _row(small["cv_ln_b_full"])
    (u3,), _ = _rowwise("cv_lnsilu", lambda rv, vv: ([_lnsilu_f(rv[0], vv[0], vv[1])], []),
                        [u2], [ln_g, ln_b], [(D, bf16)], [], sub=16)
    mix1 = _mm(u3, w_pw2[0], bias=_row(small["cv_b_pw2_full"]), name="cv_pw2")
    x3 = _resid_fwd(x2, m[1][2], mix1, "resid_mix1")
    h4 = _adaln_fwd(x3, g_ffn[1], m[1][4], m[1][3], "adaln_ffn1")
    f1, ffn1_saved = _ffn_fwd(h4, w_gu[1], w_down[1], "1")
    x4 = _resid_fwd(x3, m[1][5], f1, "resid_ffn1")

    g_fin = _row(small["final_norm_g"])

    def final_fn(rv, vv):
        xv, tv = rv
        yv, vjp = jax.vjp(_rms, xv, vv[0])
        err = yv - tv
        dx, dg = vjp(err / D)
        part = 0.5 * jnp.sum(jnp.mean(err * err, -1, keepdims=True), 0, keepdims=True)
        return [dx], [dg, jnp.broadcast_to(part, (1, LANES))]

    (dx4,), (d_fin, loss) = _rowwise("loss_head", final_fn, [x4, target], [g_fin], [(D, f32)], [D, LANES])

    dmod = [[None] * 6 for _ in range(2)]
    d_norm_mix, d_norm_ffn = [None, None], [None, None]
    d_gu, d_down = [None, None], [None, None]

    df1, (dmod[1][5], _) = _resid_bwd(dx4, f1, m[1][5], "resid_ffn1_bwd")
    dh4, d_gu[1], d_down[1] = _ffn_bwd(h4, w_gu[1], w_down[1], ffn1_saved, df1, "1")
    dx3, (d_norm_ffn[1], dmod[1][4], dmod[1][3]) = _adaln_bwd(x3, g_ffn[1], m[1][4], m[1][3], dh4, dx4, "adaln_ffn1_bwd")
    dmix1, (dmod[1][2], d_b_pw2) = _resid_bwd(dx3, mix1, m[1][2], "resid_mix1_bwd")
    du3 = _mm(dmix1, w_pw2[0], tb=True, name="cv_pw2_dx")
    d_pw2 = _mm(u3, dmix1, ta=True, name="cv_pw2_dw")

    def lnsilu_bwd(rv, vv):
        _, vjp = jax.vjp(_lnsilu_f, rv[0], vv[0], vv[1])
        du, dg, db = vjp(rv[1])
        return [du], [dg, db]

    (du2,), (d_ln_g, d_ln_b) = _rowwise("cv_lnsilu_bwd", lnsilu_bwd, [u2, du3], [ln_g, ln_b], [(D, f32)], [D, D])
    du, d_w_dw, d_b_dw = _conv_bwd(u, small["cv_w_dw_full"], du2, None, silu=False, name="cv_dw_bwd")

    def glu_bwd(rv, vv):
        a, gt, d = rv
        _, vjp = jax.vjp(lambda a_, g_: a_ * jax.nn.sigmoid(g_), a, gt)
        da, dg = vjp(d)
        return [da, dg], [jnp.sum(da, 0, keepdims=True), jnp.sum(dg, 0, keepdims=True)]

    (dpa, dpg), (d_b1a, d_b1g) = _rowwise("cv_glu_bwd", glu_bwd, [(pw1, 0, D), (pw1, 1, D), du], [],
                                           [(D, bf16), (D, bf16)], [D, D], sub=16)
    dpw1 = jnp.concatenate([dpa, dpg], axis=1)
    d_b_pw1 = jnp.concatenate([d_b1a, d_b1g], axis=1)
    dh3 = _mm(dpw1, w_pw1[0], tb=True, name="cv_pw1_dx")
    d_pw1 = _mm(h3, dpw1, ta=True, name="cv_pw1_dw")
    dx2, (d_norm_mix[1], dmod[1][1], dmod[1][0]) = _adaln_bwd(x2, g_mix[1], m[1][1], m[1][0], dh3, dx3, "adaln_mix1_bwd")

    df0, (dmod[0][5], _) = _resid_bwd(dx2, f0, m[0][5], "resid_ffn0_bwd")
    dh2, d_gu[0], d_down[0] = _ffn_bwd(h2, w_gu[0], w_down[0], ffn0_saved, df0, "0")
    dx1, (d_norm_ffn[0], dmod[0][4], dmod[0][3]) = _adaln_bwd(x1, g_ffn[0], m[0][4], m[0][3], dh2, dx2, "adaln_ffn0_bwd")
    dmix0, (dmod[0][2], _) = _resid_bwd(dx1, mix0, m[0][2], "resid_mix0_bwd")
    dyg = _mm(dmix0, w_out_y, tb=True, name="out_y_dx")
    datt = _mm(dmix0, w_out_a, tb=True, name="out_a_dx")
    d_out = jnp.concatenate([_mm(yg, dmix0, ta=True, name="out_y_dw"), _mm(att, dmix0, ta=True, name="out_a_dw")], axis=0)

    dq, dks, dvs, dbs = [], [], [], []
    for gi, d in enumerate(ATT_DILATIONS):
        a, b, c_, e = _attn_bwd(proj[f"q{gi}"], proj["k"], proj["v"], bias[gi], att, datt, lse_tot, d, f"attn_bwd_{gi}")
        dq.append(a)
        dks.append(b)
        dvs.append(c_)
        dbs.append(e)
    dk = _add3(*dks, "attn_dk")
    dv = _add3(*dvs, "attn_dv")
    d_rel = jnp.concatenate(
        [_exact_mm(dbs[gi].reshape(H, -1), onehot[gi], tb=True, name=f"rel_grad_{gi}") for gi in range(3)], axis=0).T

    def gate_bwd(rv, vv):
        _, vjp = jax.vjp(_gate_f, rv[0], rv[1], vv[0])
        dy_, dz_, dg_ = vjp(rv[2])
        return [dy_, dz_], [dg_]

    (dy, dz), (d_g_ssm,) = _rowwise("ssm_gate_bwd", gate_bwd, [y, proj["z"], dyg], [g_ssm], [(di, f32), (di, bf16)], [di],
                                    sub=16)
    dxbc, ddtraw, d_a_log, d_dskip, d_dt_bias = _ssd_bwd(xbc, proj["dt"], dt_bias, a_log, d_skip, hin, dy, di, "ssd_bwd")
    dxbc_pre, d_conv_w, d_conv_b = _conv_bwd(proj["xbc"], conv_w, dxbc, xbc_pre, silu=True, name="ssm_conv_bwd",
                                             dx_dtype=bf16)

    dseg = {"z": dz, "xbc": dxbc_pre, "dt": ddtraw, "q0": dq[0], "q1": dq[1], "q2": dq[2], "k": dk, "v": dv}
    dh1 = None
    d_in_parts = []
    for nm in seg:
        dh1 = _mm(dseg[nm], wseg[nm], tb=True, add=dh1, name=f"in_{nm}_dx")
        dwp = _mm(h1, dseg[nm], ta=True, name=f"in_{nm}_dw")
        d_in_parts.append(dwp[:, :nh] if nm == "dt" else dwp)
    d_in = jnp.concatenate(d_in_parts, axis=1)
    dx0, (d_norm_mix[0], dmod[0][1], dmod[0][0]) = _adaln_bwd(x, g_mix[0], m[0][1], m[0][0], dh1, dx1, "adaln_mix0_bwd")

    big = [d_in[None], d_out[None], d_pw1[None], d_pw2[None],
           jnp.stack([d_gu[0][:, :F], d_gu[1][:, :F]]), jnp.stack([d_gu[0][:, F:], d_gu[1][:, F:]]),
           jnp.stack(d_down)]
    smallg = dict(
        loss=loss, dmod=jnp.stack([jnp.concatenate(dmod[i], axis=1)[0] for i in range(2)]),
        norm_mix_g=jnp.concatenate(d_norm_mix, axis=0), norm_ffn_g=jnp.concatenate(d_norm_ffn, axis=0),
        hy_conv_w=d_conv_w, hy_conv_b=d_conv_b, hy_dt_bias=d_dt_bias[:, :nh], hy_a_log=d_a_log[:, :nh],
        hy_d_skip=d_dskip[:, :nh], hy_ssm_norm_g=d_g_ssm, rel_table=d_rel,
        cv_b_pw1=d_b_pw1, cv_w_dw=d_w_dw, cv_b_dw=d_b_dw, cv_ln_g=d_ln_g, cv_ln_b=d_ln_b, cv_b_pw2=d_b_pw2,
        final_norm_g=d_fin)
    return dx0, big, smallg


SMALL_GRAD_ORDER = ("loss", "dmod", "norm_mix_g", "norm_ffn_g", "hy_conv_w", "hy_conv_b", "hy_dt_bias", "hy_a_log",
                    "hy_d_skip", "hy_ssm_norm_g", "rel_table", "cv_b_pw1", "cv_w_dw", "cv_b_dw", "cv_ln_g", "cv_ln_b",
                    "cv_b_pw2", "final_norm_g")


def kernel(x, c, ada_w, ada_b, norm_mix_g, norm_ffn_g, hy_w_in, hy_conv_w, hy_conv_b, hy_dt_bias, hy_a_log, hy_d_skip, hy_ssm_norm_g, hy_w_out, rel_table, cv_w_pw1, cv_b_pw1, cv_w_dw, cv_b_dw, cv_ln_g, cv_ln_b, cv_w_pw2, cv_b_pw2, ffn_w_gate, ffn_w_up, ffn_w_down, final_norm_g, loss_target, m_ada_w, m_ada_b, m_norm_mix_g, m_norm_ffn_g, m_hy_w_in, m_hy_conv_w, m_hy_conv_b, m_hy_dt_bias, m_hy_a_log, m_hy_d_skip, m_hy_ssm_norm_g, m_hy_w_out, m_rel_table, m_cv_w_pw1, m_cv_b_pw1, m_cv_w_dw, m_cv_b_dw, m_cv_ln_g, m_cv_ln_b, m_cv_w_pw2, m_cv_b_pw2, m_ffn_w_gate, m_ffn_w_up, m_ffn_w_down, m_final_norm_g, v_ada_w, v_ada_b, v_norm_mix_g, v_norm_ffn_g, v_hy_w_in, v_hy_conv_w, v_hy_conv_b, v_hy_dt_bias, v_hy_a_log, v_hy_d_skip, v_hy_ssm_norm_g, v_hy_w_out, v_rel_table, v_cv_w_pw1, v_cv_b_pw1, v_cv_w_dw, v_cv_b_dw, v_cv_ln_g, v_cv_ln_b, v_cv_w_pw2, v_cv_b_pw2, v_ffn_w_gate, v_ffn_w_up, v_ffn_w_down, v_final_norm_g):
    names = ("ada_w", "ada_b", "norm_mix_g", "norm_ffn_g", "hy_w_in", "hy_conv_w", "hy_conv_b", "hy_dt_bias", "hy_a_log",
             "hy_d_skip", "hy_ssm_norm_g", "hy_w_out", "rel_table", "cv_w_pw1", "cv_b_pw1", "cv_w_dw", "cv_b_dw", "cv_ln_g",
             "cv_ln_b", "cv_w_pw2", "cv_b_pw2", "ffn_w_gate", "ffn_w_up", "ffn_w_down", "final_norm_g")
    w = dict(zip(names, (ada_w, ada_b, norm_mix_g, norm_ffn_g, hy_w_in, hy_conv_w, hy_conv_b, hy_dt_bias, hy_a_log, hy_d_skip,
                         hy_ssm_norm_g, hy_w_out, rel_table, cv_w_pw1, cv_b_pw1, cv_w_dw, cv_b_dw, cv_ln_g, cv_ln_b, cv_w_pw2,
                         cv_b_pw2, ffn_w_gate, ffn_w_up, ffn_w_down, final_norm_g)))
    mom = dict(zip(names, (m_ada_w, m_ada_b, m_norm_mix_g, m_norm_ffn_g, m_hy_w_in, m_hy_conv_w, m_hy_conv_b, m_hy_dt_bias,
                           m_hy_a_log, m_hy_d_skip, m_hy_ssm_norm_g, m_hy_w_out, m_rel_table, m_cv_w_pw1, m_cv_b_pw1, m_cv_w_dw,
                           m_cv_b_dw, m_cv_ln_g, m_cv_ln_b, m_cv_w_pw2, m_cv_b_pw2, m_ffn_w_gate, m_ffn_w_up, m_ffn_w_down,
                           m_final_norm_g)))
    vel = dict(zip(names, (v_ada_w, v_ada_b, v_norm_mix_g, v_norm_ffn_g, v_hy_w_in, v_hy_conv_w, v_hy_conv_b, v_hy_dt_bias,
                           v_hy_a_log, v_hy_d_skip, v_hy_ssm_norm_g, v_hy_w_out, v_rel_table, v_cv_w_pw1, v_cv_b_pw1, v_cv_w_dw,
                           v_cv_b_dw, v_cv_ln_g, v_cv_ln_b, v_cv_w_pw2, v_cv_b_pw2, v_ffn_w_gate, v_ffn_w_up, v_ffn_w_down,
                           v_final_norm_g)))
    S, D = x.shape[1], x.shape[2]
    ax, ay, ac = lax.axis_index("x"), lax.axis_index("y"), lax.axis_index("c")
    me = 4 * ax + 2 * ay + ac
    c_arr = jnp.reshape(ac, (1,)).astype(jnp.int32)
    nmod = ada_w.shape[2]

    shard_shapes = [w[nm].shape for nm, _ in BIG]
    gathered = _all_gather([w[nm].astype(bf16).reshape(-1, w[nm].shape[-1]) for nm, _ in BIG], "gather_weights")
    W = [_full_from_blocks(g, kind, shp) for g, (_, kind), shp in zip(gathered, BIG, shard_shapes)]

    sharded_small = ("hy_conv_w", "cv_b_pw1", "cv_w_dw", "cv_b_dw", "cv_ln_g", "cv_ln_b", "cv_b_pw2")
    vp = _VecPack([c.shape] + [w[nm].shape for nm in sharded_small])
    (sg,) = _all_gather([vp.pack([c] + [w[nm] for nm in sharded_small])], "gather_vectors")
    parts = [vp.unpack(sg[j]) for j in range(N_DEV)]
    c_all = jnp.concatenate([p[0] for p in parts], axis=0)
    small = {k: w[k] for k in ("norm_mix_g", "norm_ffn_g", "hy_conv_b", "hy_dt_bias", "hy_a_log", "hy_d_skip",
                               "hy_ssm_norm_g", "rel_table", "final_norm_g")}
    for i, nm in enumerate(sharded_small):
        small[nm + "_full"] = jnp.concatenate([p[1 + i][0] for p in parts], axis=-1)

    (cs_all,), _ = _rowwise("ada_silu", lambda rv, vv: ([_silu(rv[0])], []), [c_all], [], [(D, f32)], [])
    b_mine = lax.dynamic_slice_in_dim(ada_b, me * nmod, nmod, axis=1)
    mod_part = jnp.stack([_mm(cs_all, ada_w[i], bias=b_mine[i:i + 1], name=f"ada_mod_{i}") for i in range(2)])
    (mod_all,) = _all_gather([mod_part.reshape(2 * N_DEV, nmod)], "gather_mod")
    mod_all = mod_all.reshape(N_DEV, 2, N_DEV, nmod)
    mod_mine = lax.dynamic_index_in_dim(mod_all, me, axis=2, keepdims=False)
    mod = jnp.transpose(mod_mine, (1, 0, 2)).reshape(2, 6, D)

    dx0, big, sgrad = _local_step(x[0], loss_target[0], mod, W, small)

    gp = _VecPack([sgrad[k].shape for k in SMALL_GRAD_ORDER])
    (g_all,) = _all_gather([gp.pack([sgrad[k] for k in SMALL_GRAD_ORDER])], "gather_small_grads")
    tot = dict(zip(SMALL_GRAD_ORDER, gp.unpack(_sum_slots(g_all, "sum_small_grads"))))
    dmod_all = jnp.stack([gp.unpack(g_all[j])[1] for j in range(N_DEV)])
    loss = tot["loss"][0, 0]

    grads = {}
    dmod_mine = lax.dynamic_slice_in_dim(dmod_all, me * nmod, nmod, axis=2)
    grads["ada_w"] = jnp.stack([_mm(cs_all, dmod_mine[:, i], ta=True, name=f"ada_w_grad_{i}") for i in range(2)])
    grads["ada_b"] = tot["dmod"]
    grads["norm_mix_g"], grads["norm_ffn_g"] = tot["norm_mix_g"], tot["norm_ffn_g"]
    grads["hy_conv_b"] = tot["hy_conv_b"]
    grads["hy_dt_bias"] = tot["hy_dt_bias"]
    grads["hy_a_log"] = tot["hy_a_log"]
    grads["hy_d_skip"] = tot["hy_d_skip"]
    grads["hy_ssm_norm_g"] = tot["hy_ssm_norm_g"]
    grads["rel_table"] = tot["rel_table"]
    grads["final_norm_g"] = tot["final_norm_g"][0]
    for nm in sharded_small:
        n = w[nm].shape[-1]
        grads[nm] = lax.dynamic_slice_in_dim(tot[nm], me * n, n, axis=1).reshape(w[nm].shape)

    g_big = _reduce_scatter([_blocks_from_full(g, kind, shp) for g, (_, kind), shp in zip(big, BIG, shard_shapes)],
                            c_arr, "rs")
    for (nm, _), g, shp in zip(BIG, g_big, shard_shapes):
        grads[nm] = g.reshape(shp)

    delta, new_m, new_v = {}, {}, {}
    for nm in ("ada_w",) + tuple(n for n, _ in BIG):
        shp = w[nm].shape
        two = lambda t: t.reshape(-1, shp[-1])
        d_, m_, v_ = _adamw(two(w[nm]), two(grads[nm]), two(mom[nm]), two(vel[nm]), f"adamw_{nm}")
        delta[nm], new_m[nm], new_v[nm] = d_.reshape(shp), m_.reshape(shp), v_.reshape(shp)
    rest = [nm for nm in names if nm not in delta]
    sp = _VecPack([w[nm].shape for nm in rest])
    packs = [sp.pack([t[nm] for nm in rest]) for t in (w, grads, mom, vel)]
    d_, m_, v_ = _adamw(*packs, "adamw_small")
    for nm, a, b, e in zip(rest, sp.unpack(d_), sp.unpack(m_), sp.unpack(v_)):
        delta[nm], new_m[nm], new_v[nm] = a, b, e

    return (loss, dx0[None], *[grads[n] for n in names], *[delta[n] for n in names],
            *[new_m[n] for n in names], *[new_v[n] for n in names])
```

```python
import functools
import math

import numpy as np
import jax
import jax.numpy as jnp
from jax import lax
from jax.experimental import pallas as pl
from jax.experimental.pallas import tpu as pltpu

f32 = jnp.float32
bf16 = jnp.bfloat16
EPS = 1e-6
N_DEV = 8
LANES = 128
SSM_STATE = 128
SSM_CHUNK = 128
SSM_GROUPS = 4
HEAD_DIM = 64
ATT_BLK = 128
ATT_DILATIONS = (1, 4, 16)
REL_BUCKETS = 32
REL_MAX_DIST = 2048
ADAM_LR, ADAM_B1, ADAM_B2, ADAM_EPS, ADAM_WD, ADAM_STEP = 0.001, 0.9, 0.999, 1e-08, 0.01, 10
PACK_COLS = 1024
PACK_ROW_TILE = 256
MESH = pl.DeviceIdType.MESH
VMEM_LIMIT = 48 * 1024 * 1024


def _sds(shape, dtype=f32):
    return jax.ShapeDtypeStruct(tuple(shape), dtype)


def _tile(n, cap, mult):
    best = None
    t = mult
    while t <= min(n, cap):
        if n % t == 0:
            best = t
        t += mult
    return best if best is not None else n


def _params(sem):
    return pltpu.CompilerParams(dimension_semantics=sem, vmem_limit_bytes=VMEM_LIMIT)


def _mm(a, b, *, name, ta=False, tb=False, b_rows=None, bias=None, add=None, out_dtype=f32,
        tm_cap=512, tn_cap=512, tk_cap=8192):
    if ta:
        K, M = a.shape
    else:
        M, K = a.shape
    off, cnt = b_rows if b_rows is not None else (0, b.shape[0])
    if tb:
        N, K2 = cnt, b.shape[1]
    else:
        K2, N = cnt, b.shape[1]
    assert K == K2, (a.shape, b.shape, ta, tb, b_rows)
    tm = _tile(M, tm_cap, LANES)
    tn = _tile(math.gcd(off, N) if tb else N, tn_cap, LANES)
    tk = _tile(K if tb else math.gcd(off, K), tk_cap, LANES)
    assert N % tn == 0 and K % tk == 0 and off % (tn if tb else tk) == 0, (name, off, N, K, tn, tk)
    nk = K // tk
    jo, ko = (off // tn, 0) if tb else (0, off // tk)
    has_bias, has_add = bias is not None, add is not None
    dn = (((0 if ta else 1,), (1 if tb else 0,)), ((), ()))

    def body(*refs):
        a_ref, b_ref = refs[0], refs[1]
        pos = 2
        bias_ref = add_ref = None
        if has_bias:
            bias_ref = refs[pos]
            pos += 1
        if has_add:
            add_ref = refs[pos]
            pos += 1
        o_ref = refs[pos]
        k = pl.program_id(2)
        part = lax.dot_general(a_ref[...].astype(bf16), b_ref[...].astype(bf16), dn, preferred_element_type=f32)

        def finish(r):
            if has_bias:
                r = r + bias_ref[...]
            if has_add:
                r = r + add_ref[...]
            o_ref[...] = r.astype(o_ref.dtype)

        if nk == 1:
            finish(part)
        else:
            acc_ref = refs[pos + 1]

            @pl.when(k == 0)
            def _():
                acc_ref[...] = part

            @pl.when((k > 0) & (k < nk - 1))
            def _():
                acc_ref[...] += part

            @pl.when(k == nk - 1)
            def _():
                finish(acc_ref[...] + part)

    in_specs = [
        pl.BlockSpec((tk, tm), lambda i, j, k: (k, i)) if ta else pl.BlockSpec((tm, tk), lambda i, j, k: (i, k)),
        pl.BlockSpec((tn, tk), lambda i, j, k: (j + jo, k)) if tb else pl.BlockSpec((tk, tn), lambda i, j, k: (k + ko, j)),
    ]
    args = [a, b]
    if has_bias:
        in_specs.append(pl.BlockSpec((1, tn), lambda i, j, k: (0, j)))
        args.append(bias)
    if has_add:
        in_specs.append(pl.BlockSpec((tm, tn), lambda i, j, k: (i, j)))
        args.append(add)
    return pl.pallas_call(
        body, name=name, grid=(M // tm, N // tn, nk), in_specs=in_specs,
        out_specs=pl.BlockSpec((tm, tn), lambda i, j, k: (i, j)), out_shape=_sds((M, N), out_dtype),
        scratch_shapes=[pltpu.VMEM((tm, tn), f32)] if nk > 1 else [],
        compiler_params=_params(("parallel", "parallel", "arbitrary")),
    )(*args)


def _rowwise(name, fn, rows, vecs, out_rows, out_accs, *, tr_cap=256, sub=8):
    rows = [r if isinstance(r, tuple) else (r, 0, r.shape[1]) for r in rows]
    R = rows[0][0].shape[0]
    tr = _tile(R, tr_cap, 8)
    sub = sub if tr % sub == 0 else tr
    n_r, n_v, n_or, n_oa = len(rows), len(vecs), len(out_rows), len(out_accs)

    def body(*refs):
        row_refs = refs[:n_r]
        vec_refs = refs[n_r:n_r + n_v]
        orow_refs = refs[n_r + n_v:n_r + n_v + n_or]
        oacc_refs = refs[n_r + n_v + n_or:]
        vv = [r[...] for r in vec_refs]

        def step(s, accs):
            sl = pl.ds(pl.multiple_of(s * sub, sub), sub)
            ro, ao = fn([r[sl, :] for r in row_refs], vv)
            for o_ref, o in zip(orow_refs, ro):
                o_ref[sl, :] = o.astype(o_ref.dtype)
            return tuple(x + y for x, y in zip(accs, ao))

        accs = lax.fori_loop(0, tr // sub, step, tuple(jnp.zeros((1, w), f32) for w in out_accs))
        if n_oa:
            @pl.when(pl.program_id(0) == 0)
            def _():
                for ref in oacc_refs:
                    ref[...] = jnp.zeros_like(ref)

            for ref, x in zip(oacc_refs, accs):
                ref[...] += x

    in_specs = [pl.BlockSpec((tr, w), functools.partial(lambda i, cb: (i, cb), cb=cb)) for (_, cb, w) in rows]
    in_specs += [pl.BlockSpec((1, v.shape[1]), lambda i: (0, 0)) for v in vecs]
    out_specs = [pl.BlockSpec((tr, w), lambda i: (i, 0)) for (w, _) in out_rows]
    out_specs += [pl.BlockSpec((1, w), lambda i: (0, 0)) for w in out_accs]
    out_shape = [_sds((R, w), dt) for (w, dt) in out_rows] + [_sds((1, w)) for w in out_accs]
    res = pl.pallas_call(
        body, name=name, grid=(R // tr,), in_specs=in_specs, out_specs=out_specs, out_shape=out_shape,
        compiler_params=_params(("arbitrary",)),
    )(*[r[0] for r in rows], *vecs)
    return res[:n_or], res[n_or:]


def _silu(x):
    return x * jax.nn.sigmoid(x)


def _rms(x, g):
    return x * lax.rsqrt(jnp.mean(x * x, -1, keepdims=True) + EPS) * g


def _adaln_f(x, g, sc, sh):
    return _rms(x, g) * (1.0 + sc) + sh


def _gate_f(y, z, g):
    return _rms(y * _silu(z), g)


def _lnsilu_f(u, g, b):
    mu = jnp.mean(u, -1, keepdims=True)
    var = jnp.mean(jnp.square(u - mu), -1, keepdims=True)
    return _silu((u - mu) * lax.rsqrt(var + EPS) * g + b)


def _adaln_fwd(x, g, sc, sh, name):
    (h,), _ = _rowwise(name, lambda rv, vv: ([_adaln_f(rv[0], *vv)], []), [x], [g, sc, sh], [(x.shape[1], bf16)], [],
                       sub=16)
    return h


def _adaln_bwd(x, g, sc, sh, dh, dres, name):
    def fn(rv, vv):
        xv, dhv, drv = rv
        _, vjp = jax.vjp(_adaln_f, xv, *vv)
        dx, dg, dsc, dsh = vjp(dhv)
        return [dx + drv], [dg, dsc, dsh]
    w = x.shape[1]
    (dx,), accs = _rowwise(name, fn, [x, dh, dres], [g, sc, sh], [(w, f32)], [w, w, w])
    return dx, accs


def _resid_fwd(x, gate, mix, name):
    (y,), _ = _rowwise(name, lambda rv, vv: ([rv[0] + vv[0] * rv[1]], []), [x, mix], [gate], [(x.shape[1], f32)], [])
    return y


def _resid_bwd(dx, mix, gate, name):
    def fn(rv, vv):
        dxv, mv = rv
        dm = vv[0] * dxv
        return [dm], [jnp.sum(dxv * mv, 0, keepdims=True), jnp.sum(dm, 0, keepdims=True)]
    w = dx.shape[1]
    (dmix,), accs = _rowwise(name, fn, [dx, mix], [gate], [(w, bf16)], [w, w], sub=16)
    return dmix, accs


def _add3(a, b, c, name):
    (y,), _ = _rowwise(name, lambda rv, vv: ([rv[0] + rv[1] + rv[2]], []), [a, b, c], [], [(a.shape[1], bf16)], [],
                       sub=16)
    return y


CONV_HALO = 32


def _conv_fwd(x, w, b, *, silu, name, tr=512):
    S, C = x.shape
    K = w.shape[0]
    H = CONV_HALO
    assert K - 1 <= H and S % tr == 0 and tr % H == 0 and C % LANES == 0
    nh = tr // H

    def body(xp_ref, xc_ref, w_ref, b_ref, *rest):
        outs, scr = rest[:-1], rest[-1]
        i = pl.program_id(1)
        scr[pl.ds(0, H), :] = jnp.where(i > 0, xp_ref[...], 0.0)
        scr[pl.ds(H, tr), :] = xc_ref[...]
        acc = jnp.zeros((tr, LANES), f32) + b_ref[...]
        for k in range(K):
            acc = acc + scr[pl.ds(H - (K - 1) + k, tr), :] * w_ref[pl.ds(k, 1), :]
        outs[0][...] = acc
        if silu:
            outs[1][...] = _silu(acc)

    n_out = 2 if silu else 1
    return pl.pallas_call(
        body, name=name, grid=(C // LANES, S // tr),
        in_specs=[pl.BlockSpec((H, LANES), lambda j, i: (jnp.maximum(i * nh - 1, 0), j)),
                  pl.BlockSpec((tr, LANES), lambda j, i: (i, j)),
                  pl.BlockSpec((K, LANES), lambda j, i: (0, j)),
                  pl.BlockSpec((1, LANES), lambda j, i: (0, j))],
        out_specs=[pl.BlockSpec((tr, LANES), lambda j, i: (i, j))] * n_out,
        out_shape=[_sds((S, C))] * n_out,
        scratch_shapes=[pltpu.VMEM((tr + H, LANES), f32)],
        compiler_params=_params(("parallel", "arbitrary")),
    )(x, x, w, b)


def _conv_bwd(x, w, dact, pre, *, silu, name, dx_dtype=f32, tr=512):
    S, C = x.shape
    K = w.shape[0]
    H = CONV_HALO
    nh = tr // H
    n_i = S // tr
    kp = -(-K // 8) * 8

    def dsilu(p):
        s = jax.nn.sigmoid(p)
        return s * (1.0 + p * (1.0 - s))

    def body(*refs):
        if silu:
            xp_ref, xc_ref, w_ref, dc_ref, dn_ref, pc_ref, pn_ref, dx_ref, dw_ref, db_ref, xs, ds = refs
        else:
            xp_ref, xc_ref, w_ref, dc_ref, dn_ref, dx_ref, dw_ref, db_ref, xs, ds = refs
        i = pl.program_id(1)
        xs[pl.ds(0, H), :] = jnp.where(i > 0, xp_ref[...], 0.0)
        xs[pl.ds(H, tr), :] = xc_ref[...]
        dcur = dc_ref[...]
        dnext = dn_ref[...]
        if silu:
            dcur = dcur * dsilu(pc_ref[...])
            dnext = dnext * dsilu(pn_ref[...])
        ds[pl.ds(0, tr), :] = dcur
        ds[pl.ds(tr, H), :] = jnp.where(i < n_i - 1, dnext, 0.0)
        acc = jnp.zeros((tr, LANES), f32)
        for k in range(K):
            acc = acc + ds[pl.ds(K - 1 - k, tr), :] * w_ref[pl.ds(k, 1), :]
        dx_ref[...] = acc.astype(dx_ref.dtype)

        @pl.when(i == 0)
        def _():
            dw_ref[...] = jnp.zeros_like(dw_ref)
            db_ref[...] = jnp.zeros_like(db_ref)

        for k in range(K):
            dw_ref[pl.ds(k, 1), :] += jnp.sum(dcur * xs[pl.ds(H - (K - 1) + k, tr), :], 0, keepdims=True)
        db_ref[...] += jnp.sum(dcur, 0, keepdims=True)

    prev = pl.BlockSpec((H, LANES), lambda j, i: (jnp.maximum(i * nh - 1, 0), j))
    cur = pl.BlockSpec((tr, LANES), lambda j, i: (i, j))
    nxt = pl.BlockSpec((H, LANES), lambda j, i: (jnp.minimum((i + 1) * nh, n_i * nh - 1), j))
    in_specs = [prev, cur, pl.BlockSpec((K, LANES), lambda j, i: (0, j)), cur, nxt]
    args = [x, x, w, dact, dact]
    if silu:
        in_specs += [cur, nxt]
        args += [pre, pre]
    dx, dw, db = pl.pallas_call(
        body, name=name, grid=(C // LANES, n_i), in_specs=in_specs,
        out_specs=[cur, pl.BlockSpec((kp, LANES), lambda j, i: (0, j)), pl.BlockSpec((1, LANES), lambda j, i: (0, j))],
        out_shape=[_sds((S, C), dx_dtype), _sds((kp, C)), _sds((1, C))],
        scratch_shapes=[pltpu.VMEM((tr + H, LANES), f32), pltpu.VMEM((tr + H, LANES), f32)],
        compiler_params=_params(("parallel", "arbitrary")),
    )(*args)
    return dx, dw[:K], db


def _dot(a, b):
    return jnp.dot(a.astype(bf16), b.astype(bf16), preferred_element_type=f32)


def _dot_nt(a, b):
    return lax.dot_general(a.astype(bf16), b.astype(bf16), (((1,), (1,)), ((), ())), preferred_element_type=f32)


def _dot_tn(a, b):
    return lax.dot_general(a.astype(bf16), b.astype(bf16), (((0,), (0,)), ((), ())), preferred_element_type=f32)


def _softplus(x):
    return jnp.maximum(x, 0.0) + jnp.log(1.0 + jnp.exp(-jnp.abs(x)))


def _tri(q):
    i = lax.broadcasted_iota(jnp.int32, (q, q), 0)
    j = lax.broadcasted_iota(jnp.int32, (q, q), 1)
    return i >= j


def _ssd_prep(dtraw, dt_bias, a_log):
    q = dtraw.shape[0]
    dt = _softplus(dtraw + dt_bias)
    A = -jnp.exp(a_log)
    tri = _tri(q)
    cs = jnp.dot(tri.astype(f32), dt * A, preferred_element_type=f32, precision=lax.Precision.HIGHEST)
    return dt, A, cs, cs.T, tri


def _expand(cols, h0, n, width):
    q = cols.shape[0]
    return jnp.concatenate([jnp.broadcast_to(cols[:, h0 + r:h0 + r + 1], (q, width)) for r in range(n)], axis=1)


def _ssd_fwd(xbc, dtraw, dt_bias, a_log, d_skip, di, name):
    S, CD = xbc.shape
    Q, N, G = SSM_CHUNK, SSM_STATE, SSM_GROUPS
    nc = S // Q
    nh = di // HEAD_DIM
    R = nh // G
    gw = R * HEAD_DIM

    def body(xbc_ref, dt_ref, bias_ref, alog_ref, dsk_ref, y_ref, hin_ref, state):
        c = pl.program_id(0)

        @pl.when(c == 0)
        def _():
            state[...] = jnp.zeros_like(state)

        hin_ref[...] = state[...]
        dt, A, cs, csT, tri = _ssd_prep(dt_ref[...], bias_ref[...], alog_ref[...])
        dsk = dsk_ref[...]
        ecs = jnp.exp(cs)
        dend = jnp.exp(cs[Q - 1:Q, :] - cs)
        elast = jnp.exp(cs[Q - 1:Q, :])
        for g in range(G):
            h0 = g * R
            Bg = xbc_ref[:, pl.ds(di + g * N, N)]
            Cg = xbc_ref[:, pl.ds(di + G * N + g * N, N)]
            xg = xbc_ref[:, pl.ds(g * gw, gw)]
            Hg = state[pl.ds(g * gw, gw), :]
            Gm = _dot_nt(Cg, Bg)
            xdt = xg * _expand(dt, h0, R, HEAD_DIM)
            yoff = _dot_nt(Cg, Hg) * _expand(ecs, h0, R, HEAD_DIM)
            ys = []
            for r in range(R):
                h = h0 + r
                L = jnp.exp(jnp.where(tri, cs[:, h:h + 1] - csT[h:h + 1, :], -jnp.inf))
                ys.append(_dot(Gm * L, xdt[:, r * HEAD_DIM:(r + 1) * HEAD_DIM]))
            y = jnp.concatenate(ys, axis=1) + yoff + xg * _expand(dsk, h0, R, HEAD_DIM)
            y_ref[:, pl.ds(g * gw, gw)] = y
            hnew = _dot_tn(xdt * _expand(dend, h0, R, HEAD_DIM), Bg)
            escale = jnp.concatenate([jnp.broadcast_to(elast[:, h0 + r:h0 + r + 1], (HEAD_DIM, N)) for r in range(R)], axis=0)
            state[pl.ds(g * gw, gw), :] = escale * Hg + hnew

    vec = pl.BlockSpec((1, LANES), lambda c: (0, 0))
    return pl.pallas_call(
        body, name=name, grid=(nc,),
        in_specs=[pl.BlockSpec((Q, CD), lambda c: (c, 0)), pl.BlockSpec((Q, LANES), lambda c: (c, 0)), vec, vec, vec],
        out_specs=[pl.BlockSpec((Q, di), lambda c: (c, 0)), pl.BlockSpec((None, di, N), lambda c: (c, 0, 0))],
        out_shape=[_sds((S, di)), _sds((nc, di, N))],
        scratch_shapes=[pltpu.VMEM((di, N), f32)],
        compiler_params=_params(("arbitrary",)),
    )(xbc, dtraw, dt_bias, a_log, d_skip)


def _ssd_bwd(xbc, dtraw, dt_bias, a_log, d_skip, hin, dy, di, name):
    S, CD = xbc.shape
    Q, N, G = SSM_CHUNK, SSM_STATE, SSM_GROUPS
    nc = S // Q
    nh = di // HEAD_DIM
    R = nh // G
    gw = R * HEAD_DIM
    P = HEAD_DIM

    def body(xbc_ref, dt_ref, bias_ref, alog_ref, dsk_ref, hin_ref, dy_ref, dxbc_ref, ddt_ref, dA_ref, ddsk_ref, dtb_ref, dstate):
        c = pl.program_id(0)

        @pl.when(c == 0)
        def _():
            dstate[...] = jnp.zeros_like(dstate)
            dA_ref[...] = jnp.zeros_like(dA_ref)
            ddsk_ref[...] = jnp.zeros_like(ddsk_ref)
            dtb_ref[...] = jnp.zeros_like(dtb_ref)

        dtraw_v = dt_ref[...]
        dt, A, cs, csT, tri = _ssd_prep(dtraw_v, bias_ref[...], alog_ref[...])
        dsk = dsk_ref[...]
        ecs = jnp.exp(cs)
        dend = jnp.exp(cs[Q - 1:Q, :] - cs)
        elast = jnp.exp(cs[Q - 1:Q, :])
        lane = lax.broadcasted_iota(jnp.int32, (1, LANES), 1)
        row = lax.broadcasted_iota(jnp.int32, (Q, 1), 0)
        dcs = jnp.zeros((Q, LANES), f32)
        rsx = jnp.zeros((Q, LANES), f32)
        ddsk = jnp.zeros((1, LANES), f32)
        for g in range(G):
            h0 = g * R
            Bg = xbc_ref[:, pl.ds(di + g * N, N)]
            Cg = xbc_ref[:, pl.ds(di + G * N + g * N, N)]
            xg = xbc_ref[:, pl.ds(g * gw, gw)]
            dyg = dy_ref[:, pl.ds(g * gw, gw)]
            Hg = hin_ref[pl.ds(g * gw, gw), :]
            dHg = dstate[pl.ds(g * gw, gw), :]
            dt_e = _expand(dt, h0, R, P)
            ecs_e = _expand(ecs, h0, R, P)
            dend_e = _expand(dend, h0, R, P)
            Gm = _dot_nt(Cg, Bg)
            xdt = xg * dt_e
            yoff_raw = _dot_nt(Cg, Hg)
            dye = dyg * ecs_e
            bdh = _dot_nt(Bg, dHg)
            dC = _dot(dye, Hg)
            dB = _dot(xdt * dend_e, dHg)
            dHin = _dot_tn(dye, Cg)
            dxdt_state = dend_e * bdh
            t_off = dyg * yoff_raw * ecs_e
            t_end = bdh * xdt * dend_e
            dG = jnp.zeros((Q, Q), f32)
            dxs = []
            for r in range(R):
                h = h0 + r
                sl = slice(r * P, (r + 1) * P)
                L = jnp.exp(jnp.where(tri, cs[:, h:h + 1] - csT[h:h + 1, :], -jnp.inf))
                M = Gm * L
                dyh = dyg[:, sl]
                dM = _dot_nt(dyh, xdt[:, sl])
                dxdt = _dot_tn(M, dyh) + dxdt_state[:, sl]
                dG = dG + dM * L
                E = dM * M
                w_end = jnp.sum(t_end[:, sl], 1, keepdims=True)
                hh = jnp.sum(dHg[sl, :] * Hg[sl, :], keepdims=True) * elast[:, h:h + 1]
                d = (jnp.sum(E, 1, keepdims=True) - jnp.sum(E.T, 1, keepdims=True)
                     + jnp.sum(t_off[:, sl], 1, keepdims=True) - w_end
                     + jnp.where(row == Q - 1, jnp.sum(w_end, keepdims=True) + hh, 0.0))
                onehot = (lane == h).astype(f32)
                dcs = dcs + d * onehot
                rsx = rsx + jnp.sum(dxdt * xg[:, sl], 1, keepdims=True) * onehot
                ddsk = ddsk + jnp.sum(dyh * xg[:, sl], keepdims=True) * onehot
                dxs.append(dxdt * dt_e[:, sl] + dyh * dsk[:, h:h + 1])
            dxbc_ref[:, pl.ds(g * gw, gw)] = jnp.concatenate(dxs, axis=1)
            dxbc_ref[:, pl.ds(di + g * N, N)] = dB + _dot_tn(dG, Cg)
            dxbc_ref[:, pl.ds(di + G * N + g * N, N)] = dC + _dot(dG, Bg)
            escale = jnp.concatenate([jnp.broadcast_to(elast[:, h0 + r:h0 + r + 1], (P, N)) for r in range(R)], axis=0)
            dstate[pl.ds(g * gw, gw), :] = escale * dHg + dHin
        da = lax.dot_general(tri.astype(f32), dcs, (((0,), (0,)), ((), ())), preferred_element_type=f32,
                             precision=lax.Precision.HIGHEST)
        ddt = da * A + rsx
        ddtraw = ddt * jax.nn.sigmoid(dtraw_v + bias_ref[...])
        ddt_ref[...] = ddtraw.astype(ddt_ref.dtype)
        dA_ref[...] += jnp.sum(da * dt, 0, keepdims=True) * A
        ddsk_ref[...] += ddsk
        dtb_ref[...] += jnp.sum(ddtraw, 0, keepdims=True)

    vec = pl.BlockSpec((1, LANES), lambda c: (0, 0))
    rev = lambda c: (nc - 1 - c, 0)
    return pl.pallas_call(
        body, name=name, grid=(nc,),
        in_specs=[pl.BlockSpec((Q, CD), rev), pl.BlockSpec((Q, LANES), rev), vec, vec, vec,
                  pl.BlockSpec((None, di, N), lambda c: (nc - 1 - c, 0, 0)), pl.BlockSpec((Q, di), rev)],
        out_specs=[pl.BlockSpec((Q, CD), rev), pl.BlockSpec((Q, LANES), rev), vec, vec, vec],
        out_shape=[_sds((S, CD)), _sds((S, LANES), bf16), _sds((1, LANES)), _sds((1, LANES)), _sds((1, LANES))],
        scratch_shapes=[pltpu.VMEM((di, N), f32)],
        compiler_params=_params(("arbitrary",)),
    )(xbc, dtraw, dt_bias, a_log, d_skip, hin, dy)


def _t5_bucket_np(dist):
    max_exact = REL_BUCKETS // 2
    n = np.maximum(dist, 1).astype(np.float32)
    large = np.float32(max_exact) + np.log(n / np.float32(max_exact)) / np.float32(math.log(REL_MAX_DIST / max_exact)) * np.float32(REL_BUCKETS - max_exact)
    large = np.minimum(large.astype(np.int32), REL_BUCKETS - 1)
    return np.where(dist < max_exact, dist, large)


def _bucket_onehot():
    i = np.arange(ATT_BLK)[:, None]
    j = np.arange(2 * ATT_BLK)[None, :]
    delta = np.maximum(ATT_BLK + i - j, 0)
    out = np.zeros((len(ATT_DILATIONS), REL_BUCKETS, ATT_BLK * 2 * ATT_BLK), np.float32)
    for gi, d in enumerate(ATT_DILATIONS):
        b = _t5_bucket_np(delta * d).reshape(-1)
        out[gi, b, np.arange(b.size)] = 1.0
    return out


def _exact_mm(a, b, *, name, tb=False):
    M, K = a.shape
    N = b.shape[0] if tb else b.shape[1]
    tn = _tile(N, 4096, LANES)
    dn = (((1,), (1 if tb else 0,)), ((), ()))

    def body(a_ref, b_ref, o_ref):
        o_ref[...] = lax.dot_general(a_ref[...], b_ref[...], dn, preferred_element_type=f32,
                                     precision=lax.Precision.HIGHEST)

    return pl.pallas_call(
        body, name=name, grid=(N // tn,),
        in_specs=[pl.BlockSpec((M, K), lambda j: (0, 0)),
                  pl.BlockSpec((tn, K), lambda j: (j, 0)) if tb else pl.BlockSpec((K, tn), lambda j: (0, j))],
        out_specs=pl.BlockSpec((M, tn), lambda j: (0, j)), out_shape=_sds((M, N)),
        compiler_params=_params(("parallel",)),
    )(a, b)


def _band_penalty():
    i = np.arange(ATT_BLK)[:, None]
    j = np.arange(2 * ATT_BLK)[None, :]
    delta = ATT_BLK + i - j
    return np.where((delta >= 0) & (delta <= ATT_BLK), 0.0, -np.inf).astype(np.float32)


def _first_block_keep(n):
    col = lax.broadcasted_iota(jnp.int32, (ATT_BLK, 2 * ATT_BLK), 1)
    return (col >= ATT_BLK) | (n > 0)


ATT_SCALE = HEAD_DIM ** -0.5


def _rows(ref, r, d):
    return ref[...] if d == 1 else ref[pl.ds(r, ATT_BLK, stride=d), :]


def _set_rows(ref, r, d, val):
    if d == 1:
        ref[...] = val
    else:
        ref[pl.ds(r, ATT_BLK, stride=d), :] = val


def _attn_width(d, D):
    return D if d == 1 else LANES


def _over_residues(d, one, unroll=1):
    if d == 1:
        one(0)
    else:
        lax.fori_loop(0, d, lambda r, c: (one(r), c)[1], 0, unroll=unroll)


def _attn_fwd(q, k, v, bias, d, name):
    S, D = q.shape
    nb = S // (d * ATT_BLK)
    W = _attn_width(d, D)
    HB = W // HEAD_DIM

    def body(q_ref, kp_ref, kc_ref, vp_ref, vc_ref, b_ref, o_ref, lse_ref):
        keep = _first_block_keep(pl.program_id(1))

        def one(r):
            qs = (_rows(q_ref, r, d) * ATT_SCALE).astype(bf16)
            kcat = jnp.concatenate([_rows(kp_ref, r, d), _rows(kc_ref, r, d)], axis=0).astype(bf16)
            vcat = jnp.concatenate([_rows(vp_ref, r, d), _rows(vc_ref, r, d)], axis=0).astype(bf16)
            outs, lses = [], []
            for h in range(HB):
                sl = slice(h * HEAD_DIM, (h + 1) * HEAD_DIM)
                s = jnp.where(keep, _dot_nt(qs[:, sl], kcat[:, sl]) + b_ref[h], -jnp.inf)
                m = jnp.max(s, -1, keepdims=True)
                p = jnp.exp(s - m)
                l = jnp.sum(p, -1, keepdims=True)
                outs.append(_dot(p, vcat[:, sl]) / l)
                lses.append(jnp.broadcast_to(m + jnp.log(l), (ATT_BLK, HEAD_DIM)))
            _set_rows(o_ref, r, d, jnp.concatenate(outs, axis=1))
            _set_rows(lse_ref, r, d, jnp.concatenate(lses, axis=1))

        _over_residues(d, one, unroll=4)

    cur = pl.BlockSpec((ATT_BLK * d, W), lambda j, n: (n, j))
    prev = pl.BlockSpec((ATT_BLK * d, W), lambda j, n: (jnp.maximum(n - 1, 0), j))
    return pl.pallas_call(
        body, name=name, grid=(D // W, nb),
        in_specs=[cur, prev, cur, prev, cur, pl.BlockSpec((HB, ATT_BLK, 2 * ATT_BLK), lambda j, n: (j, 0, 0))],
        out_specs=[cur, cur], out_shape=[_sds((S, D)), _sds((S, D))],
        compiler_params=_params(("parallel", "arbitrary")),
    )(q, k, k, v, v, bias)


def _attn_bwd(q, k, v, bias, att, datt, lse_tot, d, name):
    S, D = q.shape
    nb = S // (d * ATT_BLK)
    H = D // HEAD_DIM
    W = _attn_width(d, D)
    HB = W // HEAD_DIM

    def body(q_ref, kp_ref, kc_ref, vp_ref, vc_ref, b_ref, o_ref, do_ref, lse_ref,
             dq_ref, dk_ref, dv_ref, db_ref, carry_k, carry_v):
        n = pl.program_id(1)

        @pl.when(n == 0)
        def _():
            carry_k[...] = jnp.zeros_like(carry_k)
            carry_v[...] = jnp.zeros_like(carry_v)
            db_ref[...] = jnp.zeros_like(db_ref)

        @pl.when(n < nb)
        def _():
            keep = _first_block_keep(n)

            def one(r):
                qs = (_rows(q_ref, r, d) * ATT_SCALE).astype(bf16)
                kcat = jnp.concatenate([_rows(kp_ref, r, d), _rows(kc_ref, r, d)], axis=0).astype(bf16)
                vcat = jnp.concatenate([_rows(vp_ref, r, d), _rows(vc_ref, r, d)], axis=0).astype(bf16)
                dov, lsev = _rows(do_ref, r, d), _rows(lse_ref, r, d)
                dsum_all = dov * _rows(o_ref, r, d)
                dob = dov.astype(bf16)
                dqs, dks, dvs = [], [], []
                for h in range(HB):
                    sl = slice(h * HEAD_DIM, (h + 1) * HEAD_DIM)
                    s = jnp.where(keep, _dot_nt(qs[:, sl], kcat[:, sl]) + b_ref[h], -jnp.inf)
                    p = jnp.exp(s - lsev[:, h * HEAD_DIM:h * HEAD_DIM + 1])
                    dp = _dot_nt(dob[:, sl], vcat[:, sl])
                    ds = p * (dp - jnp.sum(dsum_all[:, sl], 1, keepdims=True))
                    db_ref[h] += ds
                    dqs.append(_dot(ds, kcat[:, sl]) * ATT_SCALE)
                    dks.append(_dot_tn(ds, qs[:, sl]))
                    dvs.append(_dot_tn(p, dob[:, sl]))
                _set_rows(dq_ref, r, d, jnp.concatenate(dqs, axis=1))
                dk = jnp.concatenate(dks, axis=1)
                dv = jnp.concatenate(dvs, axis=1)
                _set_rows(dk_ref, r, d, carry_k[r] + dk[:ATT_BLK])
                _set_rows(dv_ref, r, d, carry_v[r] + dv[:ATT_BLK])
                carry_k[r] = dk[ATT_BLK:]
                carry_v[r] = dv[ATT_BLK:]

            _over_residues(d, one, unroll=2)

        @pl.when(n == nb)
        def _():
            def last(r):
                _set_rows(dk_ref, r, d, carry_k[r])
                _set_rows(dv_ref, r, d, carry_v[r])

            _over_residues(d, last)

    nq = lambda n: jnp.minimum(n, nb - 1)
    cur = pl.BlockSpec((ATT_BLK * d, W), lambda j, n: (nq(n), j))
    prev = pl.BlockSpec((ATT_BLK * d, W), lambda j, n: (jnp.maximum(nq(n) - 1, 0), j))
    done = pl.BlockSpec((ATT_BLK * d, W), lambda j, n: (jnp.maximum(n - 1, 0), j))
    bspec = pl.BlockSpec((HB, ATT_BLK, 2 * ATT_BLK), lambda j, n: (j, 0, 0))
    return pl.pallas_call(
        body, name=name, grid=(D // W, nb + 1),
        in_specs=[cur, prev, cur, prev, cur, bspec, cur, cur, cur],
        out_specs=[cur, done, done, bspec],
        out_shape=[_sds((S, D)), _sds((S, D)), _sds((S, D)), _sds((H, ATT_BLK, 2 * ATT_BLK))],
        scratch_shapes=[pltpu.VMEM((d, ATT_BLK, W), f32), pltpu.VMEM((d, ATT_BLK, W), f32)],
        compiler_params=_params(("arbitrary", "arbitrary")),
    )(q, k, k, v, v, bias, att, datt, lse_tot)


def _attn_combine(os_, lses, name):
    def fn(rv, vv):
        o0, o1, o2, l0, l1, l2 = rv
        m = jnp.maximum(jnp.maximum(l0, l1), l2)
        e0, e1, e2 = jnp.exp(l0 - m), jnp.exp(l1 - m), jnp.exp(l2 - m)
        tot = e0 + e1 + e2
        return [(e0 * o0 + e1 * o1 + e2 * o2) / tot, m + jnp.log(tot)], []
    w = os_[0].shape[1]
    (att, lse), _ = _rowwise(name, fn, list(os_) + list(lses), [], [(w, f32), (w, f32)], [])
    return att, lse


ANY = pl.BlockSpec(memory_space=pl.ANY)


def _all_gather(vs, name):
    n = len(vs)

    def body(*refs):
        x_refs, out_refs = refs[:n], refs[n:2 * n]
        send_sems, recv_sems, local_sems = refs[2 * n:]
        x, y, c = lax.axis_index("x"), lax.axis_index("y"), lax.axis_index("c")
        me, sibling = (x, y, c), (x, y, 1 - c)
        chips = [(1 - x, y), (x, 1 - y), (1 - x, 1 - y)]

        def slot(i, px, py, pc):
            return out_refs[i].at[4 * px + 2 * py + pc]

        def copy(i, k, block, to, src=None):
            return pltpu.make_async_remote_copy(
                src_ref=slot(i, *block) if src is None else src, dst_ref=slot(i, *block),
                send_sem=send_sems.at[i, k], recv_sem=recv_sems.at[i, k], device_id=to, device_id_type=MESH)

        mine = [pltpu.make_async_copy(x_refs[i], slot(i, *me), local_sems.at[i]) for i in range(n)]
        for cp in mine:
            cp.start()
        first = []
        for i in range(n):
            first.append(copy(i, 0, me, sibling, src=x_refs[i]))
            first += [copy(i, 1 + j, me, (*chip, c), src=x_refs[i]) for j, chip in enumerate(chips)]
        for cp in first:
            cp.start()
        passed = []
        for i in range(n):
            for j, chip in enumerate(chips):
                copy(i, 1 + j, (*chip, c), me).wait_recv()
                cp = copy(i, 4 + j, (*chip, c), sibling)
                cp.start()
                passed.append(cp)
        for i in range(n):
            copy(i, 0, sibling, me).wait_recv()
            for j, chip in enumerate(chips):
                copy(i, 4 + j, (*chip, 1 - c), me).wait_recv()
        for cp in first + passed:
            cp.wait_send()
        for cp in mine:
            cp.wait()

    return pl.pallas_call(
        body, name=name, out_shape=[_sds((N_DEV,) + v.shape, v.dtype) for v in vs], in_specs=[ANY] * n,
        out_specs=[ANY] * n,
        scratch_shapes=[pltpu.SemaphoreType.DMA((n, 7)), pltpu.SemaphoreType.DMA((n, 7)), pltpu.SemaphoreType.DMA((n,))],
    )(*vs)


def _rs_sibling(parts, name):
    n = len(parts)

    def body(*refs):
        p_refs, out_refs = refs[:n], refs[n:2 * n]
        send_sems, recv_sems = refs[2 * n:]
        x, y, c = lax.axis_index("x"), lax.axis_index("y"), lax.axis_index("c")
        cps = [pltpu.make_async_remote_copy(
            src_ref=p_refs[i].at[k, 1 - c], dst_ref=out_refs[i].at[k], send_sem=send_sems.at[i, k],
            recv_sem=recv_sems.at[i, k], device_id=(x, y, 1 - c), device_id_type=MESH)
            for i in range(n) for k in range(4)]
        for cp in cps:
            cp.start()
        for cp in cps:
            cp.wait()

    return pl.pallas_call(
        body, name=name, out_shape=[_sds((4,) + p.shape[2:], p.dtype) for p in parts], in_specs=[ANY] * n,
        out_specs=[ANY] * n,
        scratch_shapes=[pltpu.SemaphoreType.DMA((n, 4)), pltpu.SemaphoreType.DMA((n, 4))],
    )(*parts)


def _rs_chips(ts, name):
    n = len(ts)

    def body(*refs):
        t_refs, out_refs = refs[:n], refs[n:2 * n]
        send_sems, recv_sems, local_sems = refs[2 * n:]
        x, y, c = lax.axis_index("x"), lax.axis_index("y"), lax.axis_index("c")
        mine = 2 * x + y
        local = [pltpu.make_async_copy(t_refs[i].at[mine], out_refs[i].at[mine], local_sems.at[i]) for i in range(n)]
        for cp in local:
            cp.start()
        chips = [(1 - x, y), (x, 1 - y), (1 - x, 1 - y)]
        cps = [pltpu.make_async_remote_copy(
            src_ref=t_refs[i].at[2 * px + py], dst_ref=out_refs[i].at[mine], send_sem=send_sems.at[i, j],
            recv_sem=recv_sems.at[i, j], device_id=(px, py, c), device_id_type=MESH)
            for i in range(n) for j, (px, py) in enumerate(chips)]
        for cp in cps:
            cp.start()
        for cp in cps:
            cp.wait()
        for cp in local:
            cp.wait()

    return pl.pallas_call(
        body, name=name, out_shape=[_sds(t.shape, t.dtype) for t in ts], in_specs=[ANY] * n, out_specs=[ANY] * n,
        scratch_shapes=[pltpu.SemaphoreType.DMA((n, 3)), pltpu.SemaphoreType.DMA((n, 3)), pltpu.SemaphoreType.DMA((n,))],
    )(*ts)


def _pair_add(part, recv, c_arr, name):
    _, _, R, C = part.shape
    tr = _tile(R, PACK_ROW_TILE, 16)

    def body(c_ref, p_ref, r_ref, o_ref):
        o_ref[...] = (p_ref[...] + r_ref[...]).astype(o_ref.dtype)

    return pl.pallas_call(
        body, name=name,
        grid_spec=pltpu.PrefetchScalarGridSpec(
            num_scalar_prefetch=1, grid=(4, R // tr),
            in_specs=[pl.BlockSpec((None, None, tr, C), lambda k, i, c_ref: (k, c_ref[0], i, 0)),
                      pl.BlockSpec((None, tr, C), lambda k, i, c_ref: (k, i, 0))],
            out_specs=pl.BlockSpec((None, tr, C), lambda k, i, c_ref: (k, i, 0))),
        out_shape=_sds((4, R, C), bf16),
        compiler_params=_params(("parallel", "parallel")),
    )(c_arr, part, recv)


def _sum_slots(t, name):
    n, R, C = t.shape
    tr = _tile(R, PACK_ROW_TILE, 16)

    def body(t_ref, o_ref):
        acc = t_ref[0].astype(f32)
        for k in range(1, n):
            acc = acc + t_ref[k].astype(f32)
        o_ref[...] = acc

    return pl.pallas_call(
        body, name=name, grid=(R // tr,),
        in_specs=[pl.BlockSpec((n, tr, C), lambda i: (0, i, 0))],
        out_specs=pl.BlockSpec((tr, C), lambda i: (i, 0)), out_shape=_sds((R, C)),
        compiler_params=_params(("parallel",)),
    )(t)


def _reduce_scatter(parts, c_arr, name):
    parts4 = [p.reshape((4, 2) + p.shape[1:]) for p in parts]
    recv = _rs_sibling(parts4, name + "_sibling")
    ts = [_pair_add(p, r, c_arr, f"{name}_pair_{i}") for i, (p, r) in enumerate(zip(parts4, recv))]
    got = _rs_chips(ts, name + "_chips")
    return [_sum_slots(g, f"{name}_sum_{i}") for i, g in enumerate(got)]


ADAM_ROWS = 32


def _adamw(w, g, m, v, name):
    R, C = w.shape
    cb = LANES if C % LANES == 0 else C

    def body(w_ref, g_ref, m_ref, v_ref, d_ref, m2_ref, v2_ref):
        def update(sl):
            gv = g_ref[sl, :]
            m2 = ADAM_B1 * m_ref[sl, :] + (1.0 - ADAM_B1) * gv
            v2 = ADAM_B2 * v_ref[sl, :] + (1.0 - ADAM_B2) * jnp.square(gv)
            m_hat = m2 / (1.0 - ADAM_B1 ** ADAM_STEP)
            v_hat = v2 / (1.0 - ADAM_B2 ** ADAM_STEP)
            d_ref[sl, :] = -ADAM_LR * (m_hat / (jnp.sqrt(v_hat) + ADAM_EPS) + ADAM_WD * w_ref[sl, :])
            m2_ref[sl, :] = m2
            v2_ref[sl, :] = v2

        main = R // ADAM_ROWS
        if main:
            lax.fori_loop(0, main, lambda i, c: (update(pl.ds(pl.multiple_of(i * ADAM_ROWS, ADAM_ROWS), ADAM_ROWS)), c)[1], 0)
        if R % ADAM_ROWS:
            update(pl.ds(main * ADAM_ROWS, R % ADAM_ROWS))

    spec = pl.BlockSpec((R, cb), lambda j: (0, j))
    return pl.pallas_call(
        body, name=name, grid=(C // cb,), in_specs=[spec] * 4, out_specs=[spec] * 3, out_shape=[_sds((R, C))] * 3,
        compiler_params=_params(("parallel",)),
    )(w, g, m, v)


def _shards_2d(w):
    t = lambda a: jnp.transpose(a)
    return dict(in_t=t(w["hy_w_in"][0]), out=w["hy_w_out"][0], pw1=w["cv_w_pw1"][0], pw2=w["cv_w_pw2"][0],
                gate_t0=t(w["ffn_w_gate"][0]), gate_t1=t(w["ffn_w_gate"][1]), up_t0=t(w["ffn_w_up"][0]),
                up_t1=t(w["ffn_w_up"][1]), down0=w["ffn_w_down"][0], down1=w["ffn_w_down"][1])


def _unshard_2d(s):
    t = lambda a: jnp.transpose(a)
    return dict(hy_w_in=t(s["in_t"])[None], hy_w_out=s["out"][None], cv_w_pw1=s["pw1"][None], cv_w_pw2=s["pw2"][None],
                ffn_w_gate=jnp.stack([t(s["gate_t0"]), t(s["gate_t1"])]),
                ffn_w_up=jnp.stack([t(s["up_t0"]), t(s["up_t1"])]), ffn_w_down=jnp.stack([s["down0"], s["down1"]]))


def _full_from_blocks(nm, g):
    if nm == "pw1":
        return jnp.transpose(g, (1, 0, 2)).reshape(g.shape[1], N_DEV * g.shape[2])
    return g.reshape(N_DEV * g.shape[1], g.shape[2])


def _blocks_from_full(nm, g):
    if nm == "pw1":
        return jnp.transpose(g.reshape(g.shape[0], N_DEV, g.shape[1] // N_DEV), (1, 0, 2))
    return g.reshape(N_DEV, g.shape[0] // N_DEV, g.shape[1])


class _VecPack:
    def __init__(self, shapes):
        self.shapes = [tuple(s) for s in shapes]
        self.sizes = [int(np.prod(s)) for s in self.shapes]
        total = sum(self.sizes)
        self.rows = -(-(-(-total // LANES)) // 8) * 8
        self.total = total

    def pack(self, arrays):
        flat = jnp.concatenate([a.astype(f32).reshape(-1) for a in arrays])
        flat = jnp.pad(flat, (0, self.rows * LANES - self.total))
        return flat.reshape(self.rows, LANES)

    def unpack(self, packed):
        flat = packed.reshape(-1)
        out, off = [], 0
        for shp, n in zip(self.shapes, self.sizes):
            out.append(flat[off:off + n].reshape(shp))
            off += n
        return out

    def unpack_stacked(self, stacked, only=None):
        flat = stacked.reshape(stacked.shape[0], -1)
        offs = np.concatenate([[0], np.cumsum(self.sizes)])
        get = lambda i: flat[:, offs[i]:offs[i + 1]].reshape((stacked.shape[0],) + self.shapes[i])
        return get(only) if only is not None else [get(i) for i in range(len(self.shapes))]


def _row(v):
    return v.reshape(1, -1)


def _pad_lanes(v):
    v = v.reshape(1, -1)
    return jnp.pad(v, ((0, 0), (0, LANES - v.shape[1])))


def _ffn_fwd(h, w_gate_t, w_up_t, w_down, tag):
    F = w_down.shape[0]
    a = _mm(h, w_gate_t, tb=True, name=f"ffn_gate_{tag}")
    u = _mm(h, w_up_t, tb=True, name=f"ffn_up_{tag}")
    (f,), _ = _rowwise(f"swiglu_{tag}", lambda rv, vv: ([_silu(rv[0]) * rv[1]], []), [a, u], [], [(F, bf16)], [], sub=16)
    out = _mm(f, w_down, name=f"ffn_down_{tag}")
    return out, (a, u, f)


def _ffn_bwd(h, w_gate_t, w_up_t, w_down, saved, dout, tag):
    a, u, f = saved
    F = w_down.shape[0]
    df = _mm(dout, w_down, tb=True, name=f"ffn_down_dx_{tag}")
    dw_down = _mm(f, dout, ta=True, name=f"ffn_down_dw_{tag}")

    def fn(rv, vv):
        _, vjp = jax.vjp(lambda a_, u_: _silu(a_) * u_, rv[0], rv[1])
        da, du = vjp(rv[2])
        return [da, du], []

    (da, du), _ = _rowwise(f"swiglu_bwd_{tag}", fn, [a, u, df], [], [(F, bf16), (F, bf16)], [], sub=16)
    dh = _mm(du, w_up_t, add=_mm(da, w_gate_t, name=f"ffn_gate_dx_{tag}"), name=f"ffn_up_dx_{tag}")
    dw_gate_t = _mm(da, h, ta=True, name=f"ffn_gate_dw_{tag}")
    dw_up_t = _mm(du, h, ta=True, name=f"ffn_up_dw_{tag}")
    return dh, dw_gate_t, dw_up_t, dw_down


def _local_step(x, target, mod, W, small):
    S, D = x.shape
    di = small["hy_ssm_norm_g"].shape[-1]
    nh = small["hy_dt_bias"].shape[-1]
    cd = small["hy_conv_b"].shape[-1]
    m = [[_row(mod[i, j]) for j in range(6)] for i in range(2)]

    w_in_t = W["in_t"]
    off_q = di + cd + nh
    w_qkv_t = w_in_t[off_q:]
    seg = dict(z=(w_in_t, 0, di), xbc=(w_in_t, di, cd), dt=(w_in_t, di + cd, LANES))
    for i, nm in enumerate(("q0", "q1", "q2", "k", "v")):
        seg[nm] = (w_qkv_t, i * D, D)
    w_out_y, w_out_a = W["out"][:di], W["out"][di:]

    g_mix = [_row(small["norm_mix_g"][i]) for i in range(2)]
    g_ffn = [_row(small["norm_ffn_g"][i]) for i in range(2)]
    conv_w, conv_b = small["hy_conv_w_full"], _row(small["hy_conv_b"][0])
    dt_bias, a_log, d_skip = (_pad_lanes(small[k][0]) for k in ("hy_dt_bias", "hy_a_log", "hy_d_skip"))
    g_ssm = _row(small["hy_ssm_norm_g"][0])
    onehot = jnp.asarray(_bucket_onehot())
    rel_t = small["rel_table"].T
    H = D // HEAD_DIM
    bias = [_exact_mm(rel_t[gi * H:(gi + 1) * H], onehot[gi], name=f"rel_bias_{gi}")
            .reshape(H, ATT_BLK, 2 * ATT_BLK) + _band_penalty() for gi in range(3)]

    h1 = _adaln_fwd(x, g_mix[0], m[0][1], m[0][0], "adaln_mix0")
    proj = {nm: _mm(h1, mat, tb=True, b_rows=(off, cnt), name=f"in_{nm}") for nm, (mat, off, cnt) in seg.items()}
    xbc_pre, xbc = _conv_fwd(proj["xbc"], conv_w, conv_b, silu=True, name="ssm_conv")
    y, hin = _ssd_fwd(xbc, proj["dt"], dt_bias, a_log, d_skip, di, "ssd_fwd")
    (yg,), _ = _rowwise("ssm_gate", lambda rv, vv: ([_gate_f(rv[0], rv[1], vv[0])], []),
                        [y, proj["z"]], [g_ssm], [(di, bf16)], [], sub=16)
    og = [_attn_fwd(proj[f"q{gi}"], proj["k"], proj["v"], bias[gi], d, f"attn_fwd_{gi}")
          for gi, d in enumerate(ATT_DILATIONS)]
    att, lse_tot = _attn_combine([a for a, _ in og], [b for _, b in og], "attn_combine")
    mix0 = _mm(att, w_out_a, add=_mm(yg, w_out_y, name="out_y"), name="out_a")
    x1 = _resid_fwd(x, m[0][2], mix0, "resid_mix0")
    h2 = _adaln_fwd(x1, g_ffn[0], m[0][4], m[0][3], "adaln_ffn0")
    f0, ffn0_saved = _ffn_fwd(h2, W["gate_t0"], W["up_t0"], W["down0"], "0")
    x2 = _resid_fwd(x1, m[0][5], f0, "resid_ffn0")

    h3 = _adaln_fwd(x2, g_mix[1], m[1][1], m[1][0], "adaln_mix1")
    pw1 = _mm(h3, W["pw1"], bias=_row(small["cv_b_pw1_full"]), name="cv_pw1")
    (u,), _ = _rowwise("cv_glu", lambda rv, vv: ([rv[0] * jax.nn.sigmoid(rv[1])], []),
                       [(pw1, 0, D), (pw1, 1, D)], [], [(D, f32)], [])
    (u2,) = _conv_fwd(u, small["cv_w_dw_full"], _row(small["cv_b_dw_full"]), silu=False, name="cv_dw")
    ln_g, ln_b = _row(small["cv_ln_g_full"]), _row(small["cv_ln_b_full"])
    (u3,), _ = _rowwise("cv_lnsilu", lambda rv, vv: ([_lnsilu_f(rv[0], vv[0], vv[1])], []),
                        [u2], [ln_g, ln_b], [(D, bf16)], [], sub=16)
    mix1 = _mm(u3, W["pw2"], bias=_row(small["cv_b_pw2_full"]), name="cv_pw2")
    x3 = _resid_fwd(x2, m[1][2], mix1, "resid_mix1")
    h4 = _adaln_fwd(x3, g_ffn[1], m[1][4], m[1][3], "adaln_ffn1")
    f1, ffn1_saved = _ffn_fwd(h4, W["gate_t1"], W["up_t1"], W["down1"], "1")
    x4 = _resid_fwd(x3, m[1][5], f1, "resid_ffn1")

    g_fin = _row(small["final_norm_g"])

    def final_fn(rv, vv):
        xv, tv = rv
        yv, vjp = jax.vjp(_rms, xv, vv[0])
        err = yv - tv
        dx, dg = vjp(err / D)
        part = 0.5 * jnp.sum(jnp.mean(err * err, -1, keepdims=True), 0, keepdims=True)
        return [dx], [dg, jnp.broadcast_to(part, (1, LANES))]

    (dx4,), (d_fin, loss) = _rowwise("loss_head", final_fn, [x4, target], [g_fin], [(D, f32)], [D, LANES])

    dmod = [[None] * 6 for _ in range(2)]
    d_norm_mix, d_norm_ffn = [None, None], [None, None]
    big = {}

    df1, (dmod[1][5], _) = _resid_bwd(dx4, f1, m[1][5], "resid_ffn1_bwd")
    dh4, big["gate_t1"], big["up_t1"], big["down1"] = _ffn_bwd(h4, W["gate_t1"], W["up_t1"], W["down1"], ffn1_saved, df1, "1")
    dx3, (d_norm_ffn[1], dmod[1][4], dmod[1][3]) = _adaln_bwd(x3, g_ffn[1], m[1][4], m[1][3], dh4, dx4, "adaln_ffn1_bwd")
    dmix1, (dmod[1][2], d_b_pw2) = _resid_bwd(dx3, mix1, m[1][2], "resid_mix1_bwd")
    du3 = _mm(dmix1, W["pw2"], tb=True, name="cv_pw2_dx")
    big["pw2"] = _mm(u3, dmix1, ta=True, name="cv_pw2_dw")

    def lnsilu_bwd(rv, vv):
        _, vjp = jax.vjp(_lnsilu_f, rv[0], vv[0], vv[1])
        du, dg, db = vjp(rv[1])
        return [du], [dg, db]

    (du2,), (d_ln_g, d_ln_b) = _rowwise("cv_lnsilu_bwd", lnsilu_bwd, [u2, du3], [ln_g, ln_b], [(D, f32)], [D, D])
    du, d_w_dw, d_b_dw = _conv_bwd(u, small["cv_w_dw_full"], du2, None, silu=False, name="cv_dw_bwd")

    def glu_bwd(rv, vv):
        a, gt, d = rv
        _, vjp = jax.vjp(lambda a_, g_: a_ * jax.nn.sigmoid(g_), a, gt)
        da, dg = vjp(d)
        return [da, dg], [jnp.sum(da, 0, keepdims=True), jnp.sum(dg, 0, keepdims=True)]

    (dpa, dpg), (d_b1a, d_b1g) = _rowwise("cv_glu_bwd", glu_bwd, [(pw1, 0, D), (pw1, 1, D), du], [],
                                           [(D, bf16), (D, bf16)], [D, D], sub=16)
    dpw1 = jnp.concatenate([dpa, dpg], axis=1)
    d_b_pw1 = jnp.concatenate([d_b1a, d_b1g], axis=1)
    dh3 = _mm(dpw1, W["pw1"], tb=True, name="cv_pw1_dx")
    big["pw1"] = _mm(h3, dpw1, ta=True, name="cv_pw1_dw")
    dx2, (d_norm_mix[1], dmod[1][1], dmod[1][0]) = _adaln_bwd(x2, g_mix[1], m[1][1], m[1][0], dh3, dx3, "adaln_mix1_bwd")

    df0, (dmod[0][5], _) = _resid_bwd(dx2, f0, m[0][5], "resid_ffn0_bwd")
    dh2, big["gate_t0"], big["up_t0"], big["down0"] = _ffn_bwd(h2, W["gate_t0"], W["up_t0"], W["down0"], ffn0_saved, df0, "0")
    dx1, (d_norm_ffn[0], dmod[0][4], dmod[0][3]) = _adaln_bwd(x1, g_ffn[0], m[0][4], m[0][3], dh2, dx2, "adaln_ffn0_bwd")
    dmix0, (dmod[0][2], _) = _resid_bwd(dx1, mix0, m[0][2], "resid_mix0_bwd")
    dyg = _mm(dmix0, w_out_y, tb=True, name="out_y_dx")
    datt = _mm(dmix0, w_out_a, tb=True, name="out_a_dx")
    big["out"] = jnp.concatenate([_mm(yg, dmix0, ta=True, name="out_y_dw"), _mm(att, dmix0, ta=True, name="out_a_dw")], axis=0)

    dq, dks, dvs, dbs = [], [], [], []
    for gi, d in enumerate(ATT_DILATIONS):
        a, b, c_, e = _attn_bwd(proj[f"q{gi}"], proj["k"], proj["v"], bias[gi], att, datt, lse_tot, d, f"attn_bwd_{gi}")
        dq.append(a)
        dks.append(b)
        dvs.append(c_)
        dbs.append(e)
    dk = _add3(*dks, "attn_dk")
    dv = _add3(*dvs, "attn_dv")
    d_rel = jnp.concatenate(
        [_exact_mm(dbs[gi].reshape(H, -1), onehot[gi], tb=True, name=f"rel_grad_{gi}") for gi in range(3)], axis=0).T

    def gate_bwd(rv, vv):
        _, vjp = jax.vjp(_gate_f, rv[0], rv[1], vv[0])
        dy_, dz_, dg_ = vjp(rv[2])
        return [dy_, dz_], [dg_]

    (dy, dz), (d_g_ssm,) = _rowwise("ssm_gate_bwd", gate_bwd, [y, proj["z"], dyg], [g_ssm], [(di, f32), (di, bf16)], [di],
                                    sub=16)
    dxbc, ddtraw, d_a_log, d_dskip, d_dt_bias = _ssd_bwd(xbc, proj["dt"], dt_bias, a_log, d_skip, hin, dy, di, "ssd_bwd")
    dxbc_pre, d_conv_w, d_conv_b = _conv_bwd(proj["xbc"], conv_w, dxbc, xbc_pre, silu=True, name="ssm_conv_bwd",
                                             dx_dtype=bf16)

    dseg = {"z": dz, "xbc": dxbc_pre, "dt": ddtraw, "q0": dq[0], "q1": dq[1], "q2": dq[2], "k": dk, "v": dv}
    dh1 = None
    d_in_parts = []
    for nm, (mat, off, cnt) in seg.items():
        dh1 = _mm(dseg[nm], mat, b_rows=(off, cnt), add=dh1, name=f"in_{nm}_dx")
        dwp = _mm(dseg[nm], h1, ta=True, name=f"in_{nm}_dw")
        d_in_parts.append(dwp[:nh] if nm == "dt" else dwp)
    big["in_t"] = jnp.concatenate(d_in_parts, axis=0)
    dx0, (d_norm_mix[0], dmod[0][1], dmod[0][0]) = _adaln_bwd(x, g_mix[0], m[0][1], m[0][0], dh1, dx1, "adaln_mix0_bwd")

    smallg = dict(
        loss=loss, dmod=jnp.stack([jnp.concatenate(dmod[i], axis=1)[0] for i in range(2)]),
        norm_mix_g=jnp.concatenate(d_norm_mix, axis=0), norm_ffn_g=jnp.concatenate(d_norm_ffn, axis=0),
        hy_conv_w=d_conv_w, hy_conv_b=d_conv_b, hy_dt_bias=d_dt_bias[:, :nh], hy_a_log=d_a_log[:, :nh],
        hy_d_skip=d_dskip[:, :nh], hy_ssm_norm_g=d_g_ssm, rel_table=d_rel,
        cv_b_pw1=d_b_pw1, cv_w_dw=d_w_dw, cv_b_dw=d_b_dw, cv_ln_g=d_ln_g, cv_ln_b=d_ln_b, cv_b_pw2=d_b_pw2,
        final_norm_g=d_fin)
    return dx0, big, smallg


SMALL_GRAD_ORDER = ("loss", "dmod", "norm_mix_g", "norm_ffn_g", "hy_conv_w", "hy_conv_b", "hy_dt_bias", "hy_a_log",
                    "hy_d_skip", "hy_ssm_norm_g", "rel_table", "cv_b_pw1", "cv_w_dw", "cv_b_dw", "cv_ln_g", "cv_ln_b",
                    "cv_b_pw2", "final_norm_g")


def kernel(x, c, ada_w, ada_b, norm_mix_g, norm_ffn_g, hy_w_in, hy_conv_w, hy_conv_b, hy_dt_bias, hy_a_log, hy_d_skip, hy_ssm_norm_g, hy_w_out, rel_table, cv_w_pw1, cv_b_pw1, cv_w_dw, cv_b_dw, cv_ln_g, cv_ln_b, cv_w_pw2, cv_b_pw2, ffn_w_gate, ffn_w_up, ffn_w_down, final_norm_g, loss_target, m_ada_w, m_ada_b, m_norm_mix_g, m_norm_ffn_g, m_hy_w_in, m_hy_conv_w, m_hy_conv_b, m_hy_dt_bias, m_hy_a_log, m_hy_d_skip, m_hy_ssm_norm_g, m_hy_w_out, m_rel_table, m_cv_w_pw1, m_cv_b_pw1, m_cv_w_dw, m_cv_b_dw, m_cv_ln_g, m_cv_ln_b, m_cv_w_pw2, m_cv_b_pw2, m_ffn_w_gate, m_ffn_w_up, m_ffn_w_down, m_final_norm_g, v_ada_w, v_ada_b, v_norm_mix_g, v_norm_ffn_g, v_hy_w_in, v_hy_conv_w, v_hy_conv_b, v_hy_dt_bias, v_hy_a_log, v_hy_d_skip, v_hy_ssm_norm_g, v_hy_w_out, v_rel_table, v_cv_w_pw1, v_cv_b_pw1, v_cv_w_dw, v_cv_b_dw, v_cv_ln_g, v_cv_ln_b, v_cv_w_pw2, v_cv_b_pw2, v_ffn_w_gate, v_ffn_w_up, v_ffn_w_down, v_final_norm_g):
    names = ("ada_w", "ada_b", "norm_mix_g", "norm_ffn_g", "hy_w_in", "hy_conv_w", "hy_conv_b", "hy_dt_bias", "hy_a_log",
             "hy_d_skip", "hy_ssm_norm_g", "hy_w_out", "rel_table", "cv_w_pw1", "cv_b_pw1", "cv_w_dw", "cv_b_dw", "cv_ln_g",
             "cv_ln_b", "cv_w_pw2", "cv_b_pw2", "ffn_w_gate", "ffn_w_up", "ffn_w_down", "final_norm_g")
    w = dict(zip(names, (ada_w, ada_b, norm_mix_g, norm_ffn_g, hy_w_in, hy_conv_w, hy_conv_b, hy_dt_bias, hy_a_log, hy_d_skip,
                         hy_ssm_norm_g, hy_w_out, rel_table, cv_w_pw1, cv_b_pw1, cv_w_dw, cv_b_dw, cv_ln_g, cv_ln_b, cv_w_pw2,
                         cv_b_pw2, ffn_w_gate, ffn_w_up, ffn_w_down, final_norm_g)))
    mom = dict(zip(names, (m_ada_w, m_ada_b, m_norm_mix_g, m_norm_ffn_g, m_hy_w_in, m_hy_conv_w, m_hy_conv_b, m_hy_dt_bias,
                           m_hy_a_log, m_hy_d_skip, m_hy_ssm_norm_g, m_hy_w_out, m_rel_table, m_cv_w_pw1, m_cv_b_pw1, m_cv_w_dw,
                           m_cv_b_dw, m_cv_ln_g, m_cv_ln_b, m_cv_w_pw2, m_cv_b_pw2, m_ffn_w_gate, m_ffn_w_up, m_ffn_w_down,
                           m_final_norm_g)))
    vel = dict(zip(names, (v_ada_w, v_ada_b, v_norm_mix_g, v_norm_ffn_g, v_hy_w_in, v_hy_conv_w, v_hy_conv_b, v_hy_dt_bias,
                           v_hy_a_log, v_hy_d_skip, v_hy_ssm_norm_g, v_hy_w_out, v_rel_table, v_cv_w_pw1, v_cv_b_pw1, v_cv_w_dw,
                           v_cv_b_dw, v_cv_ln_g, v_cv_ln_b, v_cv_w_pw2, v_cv_b_pw2, v_ffn_w_gate, v_ffn_w_up, v_ffn_w_down,
                           v_final_norm_g)))
    S, D = x.shape[1], x.shape[2]
    ax, ay, ac = lax.axis_index("x"), lax.axis_index("y"), lax.axis_index("c")
    me = 4 * ax + 2 * ay + ac
    c_arr = jnp.reshape(ac, (1,)).astype(jnp.int32)
    nmod = ada_w.shape[2]

    w2 = _shards_2d(w)
    big_names = list(w2)
    gathered = _all_gather([w2[nm].astype(bf16) for nm in big_names], "gather_weights")
    W = {nm: _full_from_blocks(nm, g) for nm, g in zip(big_names, gathered)}

    sharded_small = ("hy_conv_w", "cv_b_pw1", "cv_w_dw", "cv_b_dw", "cv_ln_g", "cv_ln_b", "cv_b_pw2")
    vp = _VecPack([c.shape] + [w[nm].shape for nm in sharded_small])
    (sg,) = _all_gather([vp.pack([c] + [w[nm] for nm in sharded_small])], "gather_vectors")
    parts = vp.unpack_stacked(sg)
    c_all = parts[0][:, 0]
    small = {k: w[k] for k in ("norm_mix_g", "norm_ffn_g", "hy_conv_b", "hy_dt_bias", "hy_a_log", "hy_d_skip",
                               "hy_ssm_norm_g", "rel_table", "final_norm_g")}
    for p, nm in zip(parts[1:], sharded_small):
        p = p[:, 0]
        p = jnp.moveaxis(p, 0, -2)
        small[nm + "_full"] = p.reshape(p.shape[:-2] + (N_DEV * p.shape[-1],))

    (cs_all,), _ = _rowwise("ada_silu", lambda rv, vv: ([_silu(rv[0])], []), [c_all], [], [(D, f32)], [])
    b_mine = lax.dynamic_slice_in_dim(ada_b, me * nmod, nmod, axis=1)
    mod_part = jnp.stack([_mm(cs_all, ada_w[i], bias=b_mine[i:i + 1], name=f"ada_mod_{i}") for i in range(2)])
    (mod_all,) = _all_gather([mod_part.reshape(2 * N_DEV, nmod)], "gather_mod")
    mod_all = mod_all.reshape(N_DEV, 2, N_DEV, nmod)
    mod_mine = lax.dynamic_index_in_dim(mod_all, me, axis=2, keepdims=False)
    mod = jnp.transpose(mod_mine, (1, 0, 2)).reshape(2, 6, D)

    dx0, big, sgrad = _local_step(x[0], loss_target[0], mod, W, small)

    gp = _VecPack([sgrad[k].shape for k in SMALL_GRAD_ORDER])
    (g_all,) = _all_gather([gp.pack([sgrad[k] for k in SMALL_GRAD_ORDER])], "gather_small_grads")
    tot = dict(zip(SMALL_GRAD_ORDER, gp.unpack(_sum_slots(g_all, "sum_small_grads"))))
    dmod_all = gp.unpack_stacked(g_all, only=SMALL_GRAD_ORDER.index("dmod"))
    loss = tot["loss"][0, 0]

    grads = {}
    dmod_mine = lax.dynamic_slice_in_dim(dmod_all, me * nmod, nmod, axis=2)
    grads["ada_w"] = jnp.stack([_mm(cs_all, dmod_mine[:, i], ta=True, name=f"ada_w_grad_{i}") for i in range(2)])
    grads["ada_b"] = tot["dmod"]
    grads["norm_mix_g"], grads["norm_ffn_g"] = tot["norm_mix_g"], tot["norm_ffn_g"]
    grads["hy_conv_b"] = tot["hy_conv_b"]
    grads["hy_dt_bias"] = tot["hy_dt_bias"]
    grads["hy_a_log"] = tot["hy_a_log"]
    grads["hy_d_skip"] = tot["hy_d_skip"]
    grads["hy_ssm_norm_g"] = tot["hy_ssm_norm_g"]
    grads["rel_table"] = tot["rel_table"]
    grads["final_norm_g"] = tot["final_norm_g"][0]
    for nm in sharded_small:
        n = w[nm].shape[-1]
        grads[nm] = lax.dynamic_slice_in_dim(tot[nm], me * n, n, axis=1).reshape(w[nm].shape)

    g2 = dict(zip(big_names, _reduce_scatter([_blocks_from_full(nm, big[nm]) for nm in big_names], c_arr, "rs")))
    grads.update(_unshard_2d(g2))

    delta, new_m, new_v = {}, {}, {}
    shp = ada_w.shape
    two = lambda t: t.reshape(-1, shp[-1])
    d_, m_, v_ = _adamw(two(ada_w), two(grads["ada_w"]), two(m_ada_w), two(v_ada_w), "adamw_ada_w")
    delta["ada_w"], new_m["ada_w"], new_v["ada_w"] = d_.reshape(shp), m_.reshape(shp), v_.reshape(shp)
    m2, v2 = _shards_2d(mom), _shards_2d(vel)
    d2, nm2, nv2 = {}, {}, {}
    for nm in big_names:
        d2[nm], nm2[nm], nv2[nm] = _adamw(w2[nm], g2[nm], m2[nm], v2[nm], f"adamw_{nm}")
    delta.update(_unshard_2d(d2))
    new_m.update(_unshard_2d(nm2))
    new_v.update(_unshard_2d(nv2))
    rest = [nm for nm in names if nm not in delta]
    sp = _VecPack([w[nm].shape for nm in rest])
    packs = [sp.pack([t[nm] for nm in rest]) for t in (w, grads, mom, vel)]
    d_, m_, v_ = _adamw(*packs, "adamw_small")
    for nm, a, b, e in zip(rest, sp.unpack(d_), sp.unpack(m_), sp.unpack(v_)):
        delta[nm], new_m[nm], new_v[nm] = a, b, e

    return (loss, dx0[None], *[grads[n] for n in names], *[delta[n] for n in names],
            *[new_m[n] for n in names], *[new_v[n] for n in names])
```

```python
import functools
import math

import numpy as np
import jax
import jax.numpy as jnp
from jax import lax
from jax.experimental import pallas as pl
from jax.experimental.pallas import tpu as pltpu

f32 = jnp.float32
bf16 = jnp.bfloat16
EPS = 1e-6
N_DEV = 8
LANES = 128
SSM_STATE = 128
SSM_CHUNK = 128
SSM_GROUPS = 4
HEAD_DIM = 64
ATT_BLK = 128
ATT_DILATIONS = (1, 4, 16)
REL_BUCKETS = 32
REL_MAX_DIST = 2048
ADAM_LR, ADAM_B1, ADAM_B2, ADAM_EPS, ADAM_WD, ADAM_STEP = 0.001, 0.9, 0.999, 1e-08, 0.01, 10
PACK_COLS = 1024
PACK_ROW_TILE = 256
MESH = pl.DeviceIdType.MESH
VMEM_LIMIT = 48 * 1024 * 1024


def _sds(shape, dtype=f32):
    return jax.ShapeDtypeStruct(tuple(shape), dtype)


def _tile(n, cap, mult):
    best = None
    t = mult
    while t <= min(n, cap):
        if n % t == 0:
            best = t
        t += mult
    return best if best is not None else n


def _params(sem):
    return pltpu.CompilerParams(dimension_semantics=sem, vmem_limit_bytes=VMEM_LIMIT)


def _mm(a, b, *, name, ta=False, tb=False, b_rows=None, bias=None, add=None, out_dtype=f32,
        tm_cap=512, tn_cap=1536, tk_cap=8192):
    if ta:
        K, M = a.shape
    else:
        M, K = a.shape
    off, cnt = b_rows if b_rows is not None else (0, b.shape[0])
    if tb:
        N, K2 = cnt, b.shape[1]
    else:
        K2, N = cnt, b.shape[1]
    assert K == K2, (a.shape, b.shape, ta, tb, b_rows)
    if ta and a.dtype == f32:
        tm_cap = min(tm_cap, 256)
    tm = _tile(M, tm_cap, LANES)
    tn = _tile(math.gcd(off, N) if tb else N, tn_cap, LANES)
    tk = _tile(K if tb else math.gcd(off, K), tk_cap, LANES)
    assert N % tn == 0 and K % tk == 0 and off % (tn if tb else tk) == 0, (name, off, N, K, tn, tk)
    nk = K // tk
    jo, ko = (off // tn, 0) if tb else (0, off // tk)
    has_bias, has_add = bias is not None, add is not None
    dn = (((0 if ta else 1,), (1 if tb else 0,)), ((), ()))

    def body(*refs):
        a_ref, b_ref = refs[0], refs[1]
        pos = 2
        bias_ref = add_ref = None
        if has_bias:
            bias_ref = refs[pos]
            pos += 1
        if has_add:
            add_ref = refs[pos]
            pos += 1
        o_ref = refs[pos]
        k = pl.program_id(2)
        part = lax.dot_general(a_ref[...].astype(bf16), b_ref[...].astype(bf16), dn, preferred_element_type=f32)

        def finish(r):
            if has_bias:
                r = r + bias_ref[...]
            if has_add:
                r = r + add_ref[...]
            o_ref[...] = r.astype(o_ref.dtype)

        if nk == 1:
            finish(part)
        else:
            acc_ref = refs[pos + 1]

            @pl.when(k == 0)
            def _():
                acc_ref[...] = part

            @pl.when((k > 0) & (k < nk - 1))
            def _():
                acc_ref[...] += part

            @pl.when(k == nk - 1)
            def _():
                finish(acc_ref[...] + part)

    in_specs = [
        pl.BlockSpec((tk, tm), lambda i, j, k: (k, i)) if ta else pl.BlockSpec((tm, tk), lambda i, j, k: (i, k)),
        pl.BlockSpec((tn, tk), lambda i, j, k: (j + jo, k)) if tb else pl.BlockSpec((tk, tn), lambda i, j, k: (k + ko, j)),
    ]
    args = [a, b]
    if has_bias:
        in_specs.append(pl.BlockSpec((1, tn), lambda i, j, k: (0, j)))
        args.append(bias)
    if has_add:
        in_specs.append(pl.BlockSpec((tm, tn), lambda i, j, k: (i, j)))
        args.append(add)
    return pl.pallas_call(
        body, name=name, grid=(M // tm, N // tn, nk), in_specs=in_specs,
        out_specs=pl.BlockSpec((tm, tn), lambda i, j, k: (i, j)), out_shape=_sds((M, N), out_dtype),
        scratch_shapes=[pltpu.VMEM((tm, tn), f32)] if nk > 1 else [],
        compiler_params=_params(("parallel", "parallel", "arbitrary")),
    )(*args)


def _rowwise(name, fn, rows, vecs, out_rows, out_accs, *, tr_cap=256, sub=8):
    rows = [r if isinstance(r, tuple) else (r, 0, r.shape[1]) for r in rows]
    R = rows[0][0].shape[0]
    tr = _tile(R, tr_cap, 8)
    sub = sub if tr % sub == 0 else tr
    n_r, n_v, n_or, n_oa = len(rows), len(vecs), len(out_rows), len(out_accs)

    def body(*refs):
        row_refs = refs[:n_r]
        vec_refs = refs[n_r:n_r + n_v]
        orow_refs = refs[n_r + n_v:n_r + n_v + n_or]
        oacc_refs = refs[n_r + n_v + n_or:]
        vv = [r[...] for r in vec_refs]

        def step(s, accs):
            sl = pl.ds(pl.multiple_of(s * sub, sub), sub)
            ro, ao = fn([r[sl, :] for r in row_refs], vv)
            for o_ref, o in zip(orow_refs, ro):
                o_ref[sl, :] = o.astype(o_ref.dtype)
            return tuple(x + y for x, y in zip(accs, ao))

        accs = lax.fori_loop(0, tr // sub, step, tuple(jnp.zeros((1, w), f32) for w in out_accs))
        if n_oa:
            @pl.when(pl.program_id(0) == 0)
            def _():
                for ref in oacc_refs:
                    ref[...] = jnp.zeros_like(ref)

            for ref, x in zip(oacc_refs, accs):
                ref[...] += x

    in_specs = [pl.BlockSpec((tr, w), functools.partial(lambda i, cb: (i, cb), cb=cb)) for (_, cb, w) in rows]
    in_specs += [pl.BlockSpec((1, v.shape[1]), lambda i: (0, 0)) for v in vecs]
    out_specs = [pl.BlockSpec((tr, w), lambda i: (i, 0)) for (w, _) in out_rows]
    out_specs += [pl.BlockSpec((1, w), lambda i: (0, 0)) for w in out_accs]
    out_shape = [_sds((R, w), dt) for (w, dt) in out_rows] + [_sds((1, w)) for w in out_accs]
    res = pl.pallas_call(
        body, name=name, grid=(R // tr,), in_specs=in_specs, out_specs=out_specs, out_shape=out_shape,
        compiler_params=_params(("arbitrary",)),
    )(*[r[0] for r in rows], *vecs)
    return res[:n_or], res[n_or:]


def _silu(x):
    return x * jax.nn.sigmoid(x)


def _rms(x, g):
    return x * lax.rsqrt(jnp.mean(x * x, -1, keepdims=True) + EPS) * g


def _adaln_f(x, g, sc, sh):
    return _rms(x, g) * (1.0 + sc) + sh


def _gate_f(y, z, g):
    return _rms(y * _silu(z), g)


def _lnsilu_f(u, g, b):
    mu = jnp.mean(u, -1, keepdims=True)
    var = jnp.mean(jnp.square(u - mu), -1, keepdims=True)
    return _silu((u - mu) * lax.rsqrt(var + EPS) * g + b)


def _adaln_fwd(x, g, sc, sh, name):
    (h,), _ = _rowwise(name, lambda rv, vv: ([_adaln_f(rv[0], *vv)], []), [x], [g, sc, sh], [(x.shape[1], bf16)], [],
                       sub=16)
    return h


def _adaln_bwd(x, g, sc, sh, dh, dres, name):
    def fn(rv, vv):
        xv, dhv, drv = rv
        _, vjp = jax.vjp(_adaln_f, xv, *vv)
        dx, dg, dsc, dsh = vjp(dhv)
        return [dx + drv], [dg, dsc, dsh]
    w = x.shape[1]
    (dx,), accs = _rowwise(name, fn, [x, dh, dres], [g, sc, sh], [(w, f32)], [w, w, w])
    return dx, accs


def _resid_fwd(x, gate, mix, name):
    (y,), _ = _rowwise(name, lambda rv, vv: ([rv[0] + vv[0] * rv[1]], []), [x, mix], [gate], [(x.shape[1], f32)], [])
    return y


def _resid_bwd(dx, mix, gate, name):
    def fn(rv, vv):
        dxv, mv = rv
        dm = vv[0] * dxv
        return [dm], [jnp.sum(dxv * mv, 0, keepdims=True), jnp.sum(dm, 0, keepdims=True)]
    w = dx.shape[1]
    (dmix,), accs = _rowwise(name, fn, [dx, mix], [gate], [(w, bf16)], [w, w], sub=16)
    return dmix, accs


def _add3(a, b, c, name):
    (y,), _ = _rowwise(name, lambda rv, vv: ([rv[0] + rv[1] + rv[2]], []), [a, b, c], [], [(a.shape[1], bf16)], [],
                       sub=16)
    return y


CONV_HALO = 32


def _conv_fwd(x, w, b, *, silu, name, tr=512):
    S, C = x.shape
    K = w.shape[0]
    H = CONV_HALO
    assert K - 1 <= H and S % tr == 0 and tr % H == 0 and C % LANES == 0
    nh = tr // H

    def body(xp_ref, xc_ref, w_ref, b_ref, *rest):
        outs, scr = rest[:-1], rest[-1]
        i = pl.program_id(1)
        scr[pl.ds(0, H), :] = jnp.where(i > 0, xp_ref[...], 0.0)
        scr[pl.ds(H, tr), :] = xc_ref[...]
        acc = jnp.zeros((tr, LANES), f32) + b_ref[...]
        for k in range(K):
            acc = acc + scr[pl.ds(H - (K - 1) + k, tr), :] * w_ref[pl.ds(k, 1), :]
        outs[0][...] = acc
        if silu:
            outs[1][...] = _silu(acc)

    n_out = 2 if silu else 1
    return pl.pallas_call(
        body, name=name, grid=(C // LANES, S // tr),
        in_specs=[pl.BlockSpec((H, LANES), lambda j, i: (jnp.maximum(i * nh - 1, 0), j)),
                  pl.BlockSpec((tr, LANES), lambda j, i: (i, j)),
                  pl.BlockSpec((K, LANES), lambda j, i: (0, j)),
                  pl.BlockSpec((1, LANES), lambda j, i: (0, j))],
        out_specs=[pl.BlockSpec((tr, LANES), lambda j, i: (i, j))] * n_out,
        out_shape=[_sds((S, C))] * n_out,
        scratch_shapes=[pltpu.VMEM((tr + H, LANES), f32)],
        compiler_params=_params(("parallel", "arbitrary")),
    )(x, x, w, b)


def _conv_bwd(x, w, dact, pre, *, silu, name, dx_dtype=f32, tr=512):
    S, C = x.shape
    K = w.shape[0]
    H = CONV_HALO
    nh = tr // H
    n_i = S // tr
    kp = -(-K // 8) * 8

    def dsilu(p):
        s = jax.nn.sigmoid(p)
        return s * (1.0 + p * (1.0 - s))

    def body(*refs):
        if silu:
            xp_ref, xc_ref, w_ref, dc_ref, dn_ref, pc_ref, pn_ref, dx_ref, dw_ref, db_ref, xs, ds = refs
        else:
            xp_ref, xc_ref, w_ref, dc_ref, dn_ref, dx_ref, dw_ref, db_ref, xs, ds = refs
        i = pl.program_id(1)
        xs[pl.ds(0, H), :] = jnp.where(i > 0, xp_ref[...], 0.0)
        xs[pl.ds(H, tr), :] = xc_ref[...]
        dcur = dc_ref[...]
        dnext = dn_ref[...]
        if silu:
            dcur = dcur * dsilu(pc_ref[...])
            dnext = dnext * dsilu(pn_ref[...])
        ds[pl.ds(0, tr), :] = dcur
        ds[pl.ds(tr, H), :] = jnp.where(i < n_i - 1, dnext, 0.0)
        acc = jnp.zeros((tr, LANES), f32)
        for k in range(K):
            acc = acc + ds[pl.ds(K - 1 - k, tr), :] * w_ref[pl.ds(k, 1), :]
        dx_ref[...] = acc.astype(dx_ref.dtype)

        @pl.when(i == 0)
        def _():
            dw_ref[...] = jnp.zeros_like(dw_ref)
            db_ref[...] = jnp.zeros_like(db_ref)

        for k in range(K):
            dw_ref[pl.ds(k, 1), :] += jnp.sum(dcur * xs[pl.ds(H - (K - 1) + k, tr), :], 0, keepdims=True)
        db_ref[...] += jnp.sum(dcur, 0, keepdims=True)

    prev = pl.BlockSpec((H, LANES), lambda j, i: (jnp.maximum(i * nh - 1, 0), j))
    cur = pl.BlockSpec((tr, LANES), lambda j, i: (i, j))
    nxt = pl.BlockSpec((H, LANES), lambda j, i: (jnp.minimum((i + 1) * nh, n_i * nh - 1), j))
    in_specs = [prev, cur, pl.BlockSpec((K, LANES), lambda j, i: (0, j)), cur, nxt]
    args = [x, x, w, dact, dact]
    if silu:
        in_specs += [cur, nxt]
        args += [pre, pre]
    dx, dw, db = pl.pallas_call(
        body, name=name, grid=(C // LANES, n_i), in_specs=in_specs,
        out_specs=[cur, pl.BlockSpec((kp, LANES), lambda j, i: (0, j)), pl.BlockSpec((1, LANES), lambda j, i: (0, j))],
        out_shape=[_sds((S, C), dx_dtype), _sds((kp, C)), _sds((1, C))],
        scratch_shapes=[pltpu.VMEM((tr + H, LANES), f32), pltpu.VMEM((tr + H, LANES), f32)],
        compiler_params=_params(("parallel", "arbitrary")),
    )(*args)
    return dx, dw[:K], db


def _dot(a, b):
    return jnp.dot(a.astype(bf16), b.astype(bf16), preferred_element_type=f32)


def _dot_nt(a, b):
    return lax.dot_general(a.astype(bf16), b.astype(bf16), (((1,), (1,)), ((), ())), preferred_element_type=f32)


def _dot_tn(a, b):
    return lax.dot_general(a.astype(bf16), b.astype(bf16), (((0,), (0,)), ((), ())), preferred_element_type=f32)


def _softplus(x):
    return jnp.maximum(x, 0.0) + jnp.log(1.0 + jnp.exp(-jnp.abs(x)))


def _tri(q):
    i = lax.broadcasted_iota(jnp.int32, (q, q), 0)
    j = lax.broadcasted_iota(jnp.int32, (q, q), 1)
    return i >= j


def _ssd_prep(dtraw, dt_bias, a_log):
    q = dtraw.shape[0]
    dt = _softplus(dtraw + dt_bias)
    A = -jnp.exp(a_log)
    tri = _tri(q)
    cs = jnp.dot(tri.astype(f32), dt * A, preferred_element_type=f32, precision=lax.Precision.HIGHEST)
    return dt, A, cs, cs.T, tri


def _expand(cols, h0, n, width):
    q = cols.shape[0]
    return jnp.concatenate([jnp.broadcast_to(cols[:, h0 + r:h0 + r + 1], (q, width)) for r in range(n)], axis=1)


def _ssd_fwd(xbc, dtraw, dt_bias, a_log, d_skip, di, name):
    S, CD = xbc.shape
    Q, N, G = SSM_CHUNK, SSM_STATE, SSM_GROUPS
    nc = S // Q
    nh = di // HEAD_DIM
    R = nh // G
    gw = R * HEAD_DIM

    def body(xbc_ref, dt_ref, bias_ref, alog_ref, dsk_ref, y_ref, hin_ref, state):
        c = pl.program_id(0)

        @pl.when(c == 0)
        def _():
            state[...] = jnp.zeros_like(state)

        hin_ref[...] = state[...]
        dt, A, cs, csT, tri = _ssd_prep(dt_ref[...], bias_ref[...], alog_ref[...])
        dsk = dsk_ref[...]
        ecs = jnp.exp(cs)
        dend = jnp.exp(cs[Q - 1:Q, :] - cs)
        elast = jnp.exp(cs[Q - 1:Q, :])
        for g in range(G):
            h0 = g * R
            Bg = xbc_ref[:, pl.ds(di + g * N, N)]
            Cg = xbc_ref[:, pl.ds(di + G * N + g * N, N)]
            xg = xbc_ref[:, pl.ds(g * gw, gw)]
            Hg = state[pl.ds(g * gw, gw), :]
            Gm = _dot_nt(Cg, Bg)
            xdt = xg * _expand(dt, h0, R, HEAD_DIM)
            yoff = _dot_nt(Cg, Hg) * _expand(ecs, h0, R, HEAD_DIM)
            ys = []
            for r in range(R):
                h = h0 + r
                L = jnp.exp(jnp.where(tri, cs[:, h:h + 1] - csT[h:h + 1, :], -jnp.inf))
                ys.append(_dot(Gm * L, xdt[:, r * HEAD_DIM:(r + 1) * HEAD_DIM]))
            y = jnp.concatenate(ys, axis=1) + yoff + xg * _expand(dsk, h0, R, HEAD_DIM)
            y_ref[:, pl.ds(g * gw, gw)] = y
            hnew = _dot_tn(xdt * _expand(dend, h0, R, HEAD_DIM), Bg)
            escale = jnp.concatenate([jnp.broadcast_to(elast[:, h0 + r:h0 + r + 1], (HEAD_DIM, N)) for r in range(R)], axis=0)
            state[pl.ds(g * gw, gw), :] = escale * Hg + hnew

    vec = pl.BlockSpec((1, LANES), lambda c: (0, 0))
    return pl.pallas_call(
        body, name=name, grid=(nc,),
        in_specs=[pl.BlockSpec((Q, CD), lambda c: (c, 0)), pl.BlockSpec((Q, LANES), lambda c: (c, 0)), vec, vec, vec],
        out_specs=[pl.BlockSpec((Q, di), lambda c: (c, 0)), pl.BlockSpec((None, di, N), lambda c: (c, 0, 0))],
        out_shape=[_sds((S, di)), _sds((nc, di, N))],
        scratch_shapes=[pltpu.VMEM((di, N), f32)],
        compiler_params=_params(("arbitrary",)),
    )(xbc, dtraw, dt_bias, a_log, d_skip)


def _dot_exact(a, b):
    return jnp.dot(a, b, preferred_element_type=f32, precision=lax.Precision.HIGHEST)


def _ssd_bwd(xbc, dtraw, dt_bias, a_log, d_skip, hin, y, dy, di, name):
    S, CD = xbc.shape
    Q, N, G = SSM_CHUNK, SSM_STATE, SSM_GROUPS
    nc = S // Q
    nh = di // HEAD_DIM
    R = nh // G
    gw = R * HEAD_DIM
    P = HEAD_DIM
    head_of_col = jnp.asarray((np.arange(di)[:, None] // P == np.arange(LANES)[None, :]).astype(np.float32))
    dsk_wide = jnp.repeat(d_skip[0, :nh], P)[None]

    def body(xbc_ref, dt_ref, bias_ref, alog_ref, dskw_ref, hoc_ref, hin_ref, y_ref, dy_ref,
             dxbc_ref, ddt_ref, dA_ref, ddsk_ref, dtb_ref, dstate, dxdt_all, tend_all, yoff_all, colterm_all):
        c = pl.program_id(0)

        @pl.when(c == 0)
        def _():
            dstate[...] = jnp.zeros_like(dstate)
            dA_ref[...] = jnp.zeros_like(dA_ref)
            ddsk_ref[...] = jnp.zeros_like(ddsk_ref)
            dtb_ref[...] = jnp.zeros_like(dtb_ref)

        dtraw_v = dt_ref[...]
        dt, A, cs, csT, tri = _ssd_prep(dtraw_v, bias_ref[...], alog_ref[...])
        tri_t = jnp.logical_not(tri) | (lax.broadcasted_iota(jnp.int32, (Q, Q), 0) == lax.broadcasted_iota(jnp.int32, (Q, Q), 1))
        ecs = jnp.exp(cs)
        dend = jnp.exp(cs[Q - 1:Q, :] - cs)
        elast = jnp.exp(cs[Q - 1:Q, :])
        hoc = hoc_ref[...]
        state_dot = jnp.sum(_dot_exact(dstate[...] * hin_ref[...], jnp.ones((N, LANES), f32)) * hoc, 0, keepdims=True) * elast
        for g in range(G):
            h0 = g * R
            Bg = xbc_ref[:, pl.ds(di + g * N, N)]
            Cg = xbc_ref[:, pl.ds(di + G * N + g * N, N)]
            xg = xbc_ref[:, pl.ds(g * gw, gw)]
            dyg = dy_ref[:, pl.ds(g * gw, gw)]
            Hg = hin_ref[pl.ds(g * gw, gw), :]
            dHg = dstate[pl.ds(g * gw, gw), :]
            dt_e = _expand(dt, h0, R, P)
            ecs_e = _expand(ecs, h0, R, P)
            dend_e = _expand(dend, h0, R, P)
            cols = pl.ds(g * gw, gw)
            Gm = _dot_nt(Cg, Bg)
            Gm_t = _dot_nt(Bg, Cg)
            xdt = xg * dt_e
            dye = dyg * ecs_e
            bdh = _dot_nt(Bg, dHg)
            dC = _dot(dye, Hg)
            dB = _dot(xdt * dend_e, dHg)
            dHin = _dot_tn(dye, Cg)
            dxdt_state = dend_e * bdh
            end_term = xdt * dxdt_state
            tend_all[:, cols] = end_term
            yoff_all[:, cols] = _dot_nt(Cg, Hg) * ecs_e
            dG = jnp.zeros((Q, Q), f32)
            dxd = []
            for r in range(R):
                h = h0 + r
                sl = slice(r * P, (r + 1) * P)
                seg = cs[:, h:h + 1] - csT[h:h + 1, :]
                L = jnp.exp(jnp.where(tri, seg, -jnp.inf))
                L_t = jnp.exp(jnp.where(tri_t, -seg, -jnp.inf))
                dyh = dyg[:, sl]
                dG = dG + _dot_nt(dyh, xdt[:, sl]) * L
                dxd.append(_dot(Gm_t * L_t, dyh))
            dxdt_diag = jnp.concatenate(dxd, axis=1)
            dxdt = dxdt_diag + dxdt_state
            dxdt_all[:, cols] = dxdt
            colterm_all[:, cols] = xdt.astype(bf16).astype(f32) * dxdt_diag + end_term
            dxbc_ref[:, cols] = dxdt * dt_e + dyg * dskw_ref[:, cols]
            dxbc_ref[:, pl.ds(di + g * N, N)] = dB + _dot_tn(dG, Cg)
            dxbc_ref[:, pl.ds(di + G * N + g * N, N)] = dC + _dot(dG, Bg)
            escale = jnp.concatenate([jnp.broadcast_to(elast[:, h0 + r:h0 + r + 1], (P, N)) for r in range(R)], axis=0)
            dstate[pl.ds(g * gw, gw), :] = escale * dHg + dHin
        xs = xbc_ref[:, pl.ds(0, di)]
        dyv = dy_ref[...]
        yoff = yoff_all[...]
        y_diag = y_ref[...] - dskw_ref[...] * xs - yoff
        rs_y = _dot_exact(dyv.astype(bf16).astype(f32) * y_diag + dyv * yoff, hoc)
        rs_c = _dot_exact(colterm_all[...], hoc)
        rs_x = _dot_exact(dxdt_all[...] * xs, hoc)
        end_dot = _dot_exact(jnp.broadcast_to(jnp.sum(tend_all[...], 0, keepdims=True), (8, di)), hoc)[0:1]
        last = lax.broadcasted_iota(jnp.int32, (Q, 1), 0) == Q - 1
        dcs = rs_y - rs_c + jnp.where(last, end_dot + state_dot, 0.0)
        da = lax.dot_general(tri.astype(f32), dcs, (((0,), (0,)), ((), ())), preferred_element_type=f32,
                             precision=lax.Precision.HIGHEST)
        ddt = da * A + rs_x
        ddtraw = ddt * jax.nn.sigmoid(dtraw_v + bias_ref[...])
        ddt_ref[...] = ddtraw.astype(ddt_ref.dtype)
        dA_ref[...] += jnp.sum(da * dt, 0, keepdims=True) * A
        ddsk_ref[...] += jnp.sum(_dot_exact(dyv * xs, hoc), 0, keepdims=True)
        dtb_ref[...] += jnp.sum(ddtraw, 0, keepdims=True)

    vec = pl.BlockSpec((1, LANES), lambda c: (0, 0))
    rev = lambda c: (nc - 1 - c, 0)
    return pl.pallas_call(
        body, name=name, grid=(nc,),
        in_specs=[pl.BlockSpec((Q, CD), rev), pl.BlockSpec((Q, LANES), rev), vec, vec,
                  pl.BlockSpec((1, di), lambda c: (0, 0)), pl.BlockSpec((di, LANES), lambda c: (0, 0)),
                  pl.BlockSpec((None, di, N), lambda c: (nc - 1 - c, 0, 0)), pl.BlockSpec((Q, di), rev),
                  pl.BlockSpec((Q, di), rev)],
        out_specs=[pl.BlockSpec((Q, CD), rev), pl.BlockSpec((Q, LANES), rev), vec, vec, vec],
        out_shape=[_sds((S, CD)), _sds((S, LANES), bf16), _sds((1, LANES)), _sds((1, LANES)), _sds((1, LANES))],
        scratch_shapes=[pltpu.VMEM((di, N), f32)] + [pltpu.VMEM((Q, di), f32)] * 4,
        compiler_params=_params(("arbitrary",)),
    )(xbc, dtraw, dt_bias, a_log, dsk_wide, head_of_col, hin, y, dy)


def _t5_bucket_np(dist):
    max_exact = REL_BUCKETS // 2
    n = np.maximum(dist, 1).astype(np.float32)
    large = np.float32(max_exact) + np.log(n / np.float32(max_exact)) / np.float32(math.log(REL_MAX_DIST / max_exact)) * np.float32(REL_BUCKETS - max_exact)
    large = np.minimum(large.astype(np.int32), REL_BUCKETS - 1)
    return np.where(dist < max_exact, dist, large)


def _bucket_onehot():
    i = np.arange(ATT_BLK)[:, None]
    j = np.arange(2 * ATT_BLK)[None, :]
    delta = np.maximum(ATT_BLK + i - j, 0)
    out = np.zeros((len(ATT_DILATIONS), REL_BUCKETS, ATT_BLK * 2 * ATT_BLK), np.float32)
    for gi, d in enumerate(ATT_DILATIONS):
        b = _t5_bucket_np(delta * d).reshape(-1)
        out[gi, b, np.arange(b.size)] = 1.0
    return out


def _exact_mm(a, b, *, name, tb=False):
    M, K = a.shape
    N = b.shape[0] if tb else b.shape[1]
    tn = _tile(N, 4096, LANES)
    dn = (((1,), (1 if tb else 0,)), ((), ()))

    def body(a_ref, b_ref, o_ref):
        o_ref[...] = lax.dot_general(a_ref[...], b_ref[...], dn, preferred_element_type=f32,
                                     precision=lax.Precision.HIGHEST)

    return pl.pallas_call(
        body, name=name, grid=(N // tn,),
        in_specs=[pl.BlockSpec((M, K), lambda j: (0, 0)),
                  pl.BlockSpec((tn, K), lambda j: (j, 0)) if tb else pl.BlockSpec((K, tn), lambda j: (0, j))],
        out_specs=pl.BlockSpec((M, tn), lambda j: (0, j)), out_shape=_sds((M, N)),
        compiler_params=_params(("parallel",)),
    )(a, b)


def _band_penalty():
    i = np.arange(ATT_BLK)[:, None]
    j = np.arange(2 * ATT_BLK)[None, :]
    delta = ATT_BLK + i - j
    return np.where((delta >= 0) & (delta <= ATT_BLK), 0.0, -np.inf).astype(np.float32)


def _first_block_keep(n):
    col = lax.broadcasted_iota(jnp.int32, (ATT_BLK, 2 * ATT_BLK), 1)
    return (col >= ATT_BLK) | (n > 0)


ATT_SCALE = HEAD_DIM ** -0.5


def _rows(ref, r, d):
    return ref[...] if d == 1 else ref[pl.ds(r, ATT_BLK, stride=d), :]


def _set_rows(ref, r, d, val):
    if d == 1:
        ref[...] = val
    else:
        ref[pl.ds(r, ATT_BLK, stride=d), :] = val


def _attn_width(d, D):
    return D if d == 1 else LANES


def _over_residues(d, one, unroll=1):
    if d == 1:
        one(0)
    else:
        lax.fori_loop(0, d, lambda r, c: (one(r), c)[1], 0, unroll=unroll)


def _attn_fwd(q, k, v, bias, d, name):
    S, D = q.shape
    nb = S // (d * ATT_BLK)
    W = _attn_width(d, D)
    HB = W // HEAD_DIM

    def body(q_ref, kp_ref, kc_ref, vp_ref, vc_ref, b_ref, o_ref, lse_ref):
        keep = _first_block_keep(pl.program_id(1))

        def one(r):
            qs = (_rows(q_ref, r, d) * ATT_SCALE).astype(bf16)
            kcat = jnp.concatenate([_rows(kp_ref, r, d), _rows(kc_ref, r, d)], axis=0).astype(bf16)
            vcat = jnp.concatenate([_rows(vp_ref, r, d), _rows(vc_ref, r, d)], axis=0).astype(bf16)
            outs, lses = [], []
            for h in range(HB):
                sl = slice(h * HEAD_DIM, (h + 1) * HEAD_DIM)
                s = jnp.where(keep, _dot_nt(qs[:, sl], kcat[:, sl]) + b_ref[h], -jnp.inf)
                m = jnp.max(s, -1, keepdims=True)
                p = jnp.exp(s - m)
                l = jnp.sum(p, -1, keepdims=True)
                outs.append(_dot(p, vcat[:, sl]) / l)
                lses.append(jnp.broadcast_to(m + jnp.log(l), (ATT_BLK, HEAD_DIM)))
            _set_rows(o_ref, r, d, jnp.concatenate(outs, axis=1))
            _set_rows(lse_ref, r, d, jnp.concatenate(lses, axis=1))

        _over_residues(d, one, unroll=4)

    cur = pl.BlockSpec((ATT_BLK * d, W), lambda j, n: (n, j))
    prev = pl.BlockSpec((ATT_BLK * d, W), lambda j, n: (jnp.maximum(n - 1, 0), j))
    return pl.pallas_call(
        body, name=name, grid=(D // W, nb),
        in_specs=[cur, prev, cur, prev, cur, pl.BlockSpec((HB, ATT_BLK, 2 * ATT_BLK), lambda j, n: (j, 0, 0))],
        out_specs=[cur, cur], out_shape=[_sds((S, D)), _sds((S, D))],
        compiler_params=_params(("parallel", "arbitrary")),
    )(q, k, k, v, v, bias)


def _attn_bwd(q, k, v, bias, att, datt, lse_tot, d, name):
    S, D = q.shape
    nb = S // (d * ATT_BLK)
    H = D // HEAD_DIM
    W = _attn_width(d, D)
    HB = W // HEAD_DIM

    def body(q_ref, kp_ref, kc_ref, vp_ref, vc_ref, b_ref, o_ref, do_ref, lse_ref,
             dq_ref, dk_ref, dv_ref, db_ref, carry_k, carry_v):
        n = pl.program_id(1)

        @pl.when(n == 0)
        def _():
            carry_k[...] = jnp.zeros_like(carry_k)
            carry_v[...] = jnp.zeros_like(carry_v)
            db_ref[...] = jnp.zeros_like(db_ref)

        @pl.when(n < nb)
        def _():
            keep = _first_block_keep(n)

            def one(r):
                qs = (_rows(q_ref, r, d) * ATT_SCALE).astype(bf16)
                kcat = jnp.concatenate([_rows(kp_ref, r, d), _rows(kc_ref, r, d)], axis=0).astype(bf16)
                vcat = jnp.concatenate([_rows(vp_ref, r, d), _rows(vc_ref, r, d)], axis=0).astype(bf16)
                dov, lsev = _rows(do_ref, r, d), _rows(lse_ref, r, d)
                dsum_all = dov * _rows(o_ref, r, d)
                dob = dov.astype(bf16)
                dqs, dks, dvs = [], [], []
                for h in range(HB):
                    sl = slice(h * HEAD_DIM, (h + 1) * HEAD_DIM)
                    s = jnp.where(keep, _dot_nt(qs[:, sl], kcat[:, sl]) + b_ref[h], -jnp.inf)
                    p = jnp.exp(s - lsev[:, h * HEAD_DIM:h * HEAD_DIM + 1])
                    dp = _dot_nt(dob[:, sl], vcat[:, sl])
                    ds = p * (dp - jnp.sum(dsum_all[:, sl], 1, keepdims=True))
                    db_ref[h] += ds
                    dqs.append(_dot(ds, kcat[:, sl]) * ATT_SCALE)
                    dks.append(_dot_tn(ds, qs[:, sl]))
                    dvs.append(_dot_tn(p, dob[:, sl]))
                _set_rows(dq_ref, r, d, jnp.concatenate(dqs, axis=1))
                dk = jnp.concatenate(dks, axis=1)
                dv = jnp.concatenate(dvs, axis=1)
                _set_rows(dk_ref, r, d, carry_k[r] + dk[:ATT_BLK])
                _set_rows(dv_ref, r, d, carry_v[r] + dv[:ATT_BLK])
                carry_k[r] = dk[ATT_BLK:]
                carry_v[r] = dv[ATT_BLK:]

            _over_residues(d, one, unroll=2)

        @pl.when(n == nb)
        def _():
            def last(r):
                _set_rows(dk_ref, r, d, carry_k[r])
                _set_rows(dv_ref, r, d, carry_v[r])

            _over_residues(d, last)

    nq = lambda n: jnp.minimum(n, nb - 1)
    cur = pl.BlockSpec((ATT_BLK * d, W), lambda j, n: (nq(n), j))
    prev = pl.BlockSpec((ATT_BLK * d, W), lambda j, n: (jnp.maximum(nq(n) - 1, 0), j))
    done = pl.BlockSpec((ATT_BLK * d, W), lambda j, n: (jnp.maximum(n - 1, 0), j))
    bspec = pl.BlockSpec((HB, ATT_BLK, 2 * ATT_BLK), lambda j, n: (j, 0, 0))
    return pl.pallas_call(
        body, name=name, grid=(D // W, nb + 1),
        in_specs=[cur, prev, cur, prev, cur, bspec, cur, cur, cur],
        out_specs=[cur, done, done, bspec],
        out_shape=[_sds((S, D)), _sds((S, D)), _sds((S, D)), _sds((H, ATT_BLK, 2 * ATT_BLK))],
        scratch_shapes=[pltpu.VMEM((d, ATT_BLK, W), f32), pltpu.VMEM((d, ATT_BLK, W), f32)],
        compiler_params=_params(("arbitrary", "arbitrary")),
    )(q, k, k, v, v, bias, att, datt, lse_tot)


def _attn_combine(os_, lses, name):
    def fn(rv, vv):
        o0, o1, o2, l0, l1, l2 = rv
        m = jnp.maximum(jnp.maximum(l0, l1), l2)
        e0, e1, e2 = jnp.exp(l0 - m), jnp.exp(l1 - m), jnp.exp(l2 - m)
        tot = e0 + e1 + e2
        att = (e0 * o0 + e1 * o1 + e2 * o2) / tot
        return [att, att, m + jnp.log(tot)], []
    w = os_[0].shape[1]
    (att, att_b, lse), _ = _rowwise(name, fn, list(os_) + list(lses), [], [(w, f32), (w, bf16), (w, f32)], [], sub=16)
    return att, att_b, lse


ANY = pl.BlockSpec(memory_space=pl.ANY)


def _all_gather(vs, name):
    n = len(vs)

    def body(*refs):
        x_refs, out_refs = refs[:n], refs[n:2 * n]
        send_sems, recv_sems, local_sems = refs[2 * n:]
        x, y, c = lax.axis_index("x"), lax.axis_index("y"), lax.axis_index("c")
        me, sibling = (x, y, c), (x, y, 1 - c)
        chips = [(1 - x, y), (x, 1 - y), (1 - x, 1 - y)]

        def slot(i, px, py, pc):
            return out_refs[i].at[4 * px + 2 * py + pc]

        def copy(i, k, block, to, src=None):
            return pltpu.make_async_remote_copy(
                src_ref=slot(i, *block) if src is None else src, dst_ref=slot(i, *block),
                send_sem=send_sems.at[i, k], recv_sem=recv_sems.at[i, k], device_id=to, device_id_type=MESH)

        mine = [pltpu.make_async_copy(x_refs[i], slot(i, *me), local_sems.at[i]) for i in range(n)]
        for cp in mine:
            cp.start()
        first = []
        for i in range(n):
            first.append(copy(i, 0, me, sibling, src=x_refs[i]))
            first += [copy(i, 1 + j, me, (*chip, c), src=x_refs[i]) for j, chip in enumerate(chips)]
        for cp in first:
            cp.start()
        passed = []
        for i in range(n):
            for j, chip in enumerate(chips):
                copy(i, 1 + j, (*chip, c), me).wait_recv()
                cp = copy(i, 4 + j, (*chip, c), sibling)
                cp.start()
                passed.append(cp)
        for i in range(n):
            copy(i, 0, sibling, me).wait_recv()
            for j, chip in enumerate(chips):
                copy(i, 4 + j, (*chip, 1 - c), me).wait_recv()
        for cp in first + passed:
            cp.wait_send()
        for cp in mine:
            cp.wait()

    return pl.pallas_call(
        body, name=name, out_shape=[_sds((N_DEV,) + v.shape, v.dtype) for v in vs], in_specs=[ANY] * n,
        out_specs=[ANY] * n,
        scratch_shapes=[pltpu.SemaphoreType.DMA((n, 7)), pltpu.SemaphoreType.DMA((n, 7)), pltpu.SemaphoreType.DMA((n,))],
    )(*vs)


def _rs_sibling(parts, name):
    n = len(parts)

    def body(*refs):
        p_refs, out_refs = refs[:n], refs[n:2 * n]
        send_sems, recv_sems = refs[2 * n:]
        x, y, c = lax.axis_index("x"), lax.axis_index("y"), lax.axis_index("c")
        cps = [pltpu.make_async_remote_copy(
            src_ref=p_refs[i].at[k, 1 - c], dst_ref=out_refs[i].at[k], send_sem=send_sems.at[i, k],
            recv_sem=recv_sems.at[i, k], device_id=(x, y, 1 - c), device_id_type=MESH)
            for i in range(n) for k in range(4)]
        for cp in cps:
            cp.start()
        for cp in cps:
            cp.wait()

    return pl.pallas_call(
        body, name=name, out_shape=[_sds((4,) + p.shape[2:], p.dtype) for p in parts], in_specs=[ANY] * n,
        out_specs=[ANY] * n,
        scratch_shapes=[pltpu.SemaphoreType.DMA((n, 4)), pltpu.SemaphoreType.DMA((n, 4))],
    )(*parts)


def _rs_chips(ts, name):
    n = len(ts)

    def body(*refs):
        t_refs, out_refs = refs[:n], refs[n:2 * n]
        send_sems, recv_sems, local_sems = refs[2 * n:]
        x, y, c = lax.axis_index("x"), lax.axis_index("y"), lax.axis_index("c")
        mine = 2 * x + y
        local = [pltpu.make_async_copy(t_refs[i].at[mine], out_refs[i].at[mine], local_sems.at[i]) for i in range(n)]
        for cp in local:
            cp.start()
        chips = [(1 - x, y), (x, 1 - y), (1 - x, 1 - y)]
        cps = [pltpu.make_async_remote_copy(
            src_ref=t_refs[i].at[2 * px + py], dst_ref=out_refs[i].at[mine], send_sem=send_sems.at[i, j],
            recv_sem=recv_sems.at[i, j], device_id=(px, py, c), device_id_type=MESH)
            for i in range(n) for j, (px, py) in enumerate(chips)]
        for cp in cps:
            cp.start()
        for cp in cps:
            cp.wait()
        for cp in local:
            cp.wait()

    return pl.pallas_call(
        body, name=name, out_shape=[_sds(t.shape, t.dtype) for t in ts], in_specs=[ANY] * n, out_specs=[ANY] * n,
        scratch_shapes=[pltpu.SemaphoreType.DMA((n, 3)), pltpu.SemaphoreType.DMA((n, 3)), pltpu.SemaphoreType.DMA((n,))],
    )(*ts)


def _pair_add(part, recv, c_arr, name):
    _, _, R, C = part.shape
    tr = _tile(R, PACK_ROW_TILE, 16)

    def body(c_ref, p_ref, r_ref, o_ref):
        o_ref[...] = (p_ref[...] + r_ref[...]).astype(o_ref.dtype)

    return pl.pallas_call(
        body, name=name,
        grid_spec=pltpu.PrefetchScalarGridSpec(
            num_scalar_prefetch=1, grid=(4, R // tr),
            in_specs=[pl.BlockSpec((None, None, tr, C), lambda k, i, c_ref: (k, c_ref[0], i, 0)),
                      pl.BlockSpec((None, tr, C), lambda k, i, c_ref: (k, i, 0))],
            out_specs=pl.BlockSpec((None, tr, C), lambda k, i, c_ref: (k, i, 0))),
        out_shape=_sds((4, R, C), bf16),
        compiler_params=_params(("parallel", "parallel")),
    )(c_arr, part, recv)


def _sum_slots(t, name):
    n, R, C = t.shape
    tr = _tile(R, PACK_ROW_TILE, 16)

    def body(t_ref, o_ref):
        acc = t_ref[0].astype(f32)
        for k in range(1, n):
            acc = acc + t_ref[k].astype(f32)
        o_ref[...] = acc

    return pl.pallas_call(
        body, name=name, grid=(R // tr,),
        in_specs=[pl.BlockSpec((n, tr, C), lambda i: (0, i, 0))],
        out_specs=pl.BlockSpec((tr, C), lambda i: (i, 0)), out_shape=_sds((R, C)),
        compiler_params=_params(("parallel",)),
    )(t)


def _reduce_scatter(parts, c_arr, name):
    parts4 = [p.reshape((4, 2) + p.shape[1:]) for p in parts]
    recv = _rs_sibling(parts4, name + "_sibling")
    ts = [_pair_add(p, r, c_arr, f"{name}_pair_{i}") for i, (p, r) in enumerate(zip(parts4, recv))]
    got = _rs_chips(ts, name + "_chips")
    return [_sum_slots(g, f"{name}_sum_{i}") for i, g in enumerate(got)]


ADAM_ROWS = 32


def _adamw(w, g, m, v, name):
    R, C = w.shape
    cb = LANES if C % LANES == 0 else C

    def body(w_ref, g_ref, m_ref, v_ref, d_ref, m2_ref, v2_ref):
        def update(sl):
            gv = g_ref[sl, :]
            m2 = ADAM_B1 * m_ref[sl, :] + (1.0 - ADAM_B1) * gv
            v2 = ADAM_B2 * v_ref[sl, :] + (1.0 - ADAM_B2) * jnp.square(gv)
            m_hat = m2 / (1.0 - ADAM_B1 ** ADAM_STEP)
            v_hat = v2 / (1.0 - ADAM_B2 ** ADAM_STEP)
            d_ref[sl, :] = -ADAM_LR * (m_hat / (jnp.sqrt(v_hat) + ADAM_EPS) + ADAM_WD * w_ref[sl, :])
            m2_ref[sl, :] = m2
            v2_ref[sl, :] = v2

        main = R // ADAM_ROWS
        if main:
            lax.fori_loop(0, main, lambda i, c: (update(pl.ds(pl.multiple_of(i * ADAM_ROWS, ADAM_ROWS), ADAM_ROWS)), c)[1], 0)
        if R % ADAM_ROWS:
            update(pl.ds(main * ADAM_ROWS, R % ADAM_ROWS))

    spec = pl.BlockSpec((R, cb), lambda j: (0, j))
    return pl.pallas_call(
        body, name=name, grid=(C // cb,), in_specs=[spec] * 4, out_specs=[spec] * 3, out_shape=[_sds((R, C))] * 3,
        compiler_params=_params(("parallel",)),
    )(w, g, m, v)


def _shards_2d(w):
    t = lambda a: jnp.transpose(a)
    return dict(in_t=t(w["hy_w_in"][0]), out=w["hy_w_out"][0], pw1=w["cv_w_pw1"][0], pw2=w["cv_w_pw2"][0],
                gate_t0=t(w["ffn_w_gate"][0]), gate_t1=t(w["ffn_w_gate"][1]), up_t0=t(w["ffn_w_up"][0]),
                up_t1=t(w["ffn_w_up"][1]), down0=w["ffn_w_down"][0], down1=w["ffn_w_down"][1])


def _unshard_2d(s):
    t = lambda a: jnp.transpose(a)
    return dict(hy_w_in=t(s["in_t"])[None], hy_w_out=s["out"][None], cv_w_pw1=s["pw1"][None], cv_w_pw2=s["pw2"][None],
                ffn_w_gate=jnp.stack([t(s["gate_t0"]), t(s["gate_t1"])]),
                ffn_w_up=jnp.stack([t(s["up_t0"]), t(s["up_t1"])]), ffn_w_down=jnp.stack([s["down0"], s["down1"]]))


def _full_from_blocks(nm, g):
    if nm == "pw1":
        return jnp.transpose(g, (1, 0, 2)).reshape(g.shape[1], N_DEV * g.shape[2])
    return g.reshape(N_DEV * g.shape[1], g.shape[2])


def _blocks_from_full(nm, g):
    if nm == "pw1":
        return jnp.transpose(g.reshape(g.shape[0], N_DEV, g.shape[1] // N_DEV), (1, 0, 2))
    return g.reshape(N_DEV, g.shape[0] // N_DEV, g.shape[1])


class _VecPack:
    def __init__(self, shapes):
        self.shapes = [tuple(s) for s in shapes]
        self.sizes = [int(np.prod(s)) for s in self.shapes]
        total = sum(self.sizes)
        self.rows = -(-(-(-total // LANES)) // 8) * 8
        self.total = total

    def pack(self, arrays):
        flat = jnp.concatenate([a.astype(f32).reshape(-1) for a in arrays])
        flat = jnp.pad(flat, (0, self.rows * LANES - self.total))
        return flat.reshape(self.rows, LANES)

    def unpack(self, packed):
        flat = packed.reshape(-1)
        out, off = [], 0
        for shp, n in zip(self.shapes, self.sizes):
            out.append(flat[off:off + n].reshape(shp))
            off += n
        return out

    def unpack_stacked(self, stacked, only=None):
        flat = stacked.reshape(stacked.shape[0], -1)
        offs = np.concatenate([[0], np.cumsum(self.sizes)])
        get = lambda i: flat[:, offs[i]:offs[i + 1]].reshape((stacked.shape[0],) + self.shapes[i])
        return get(only) if only is not None else [get(i) for i in range(len(self.shapes))]


def _row(v):
    return v.reshape(1, -1)


def _pad_lanes(v):
    v = v.reshape(1, -1)
    return jnp.pad(v, ((0, 0), (0, LANES - v.shape[1])))


def _ffn_fwd(h, w_gate_t, w_up_t, w_down, tag):
    F = w_down.shape[0]
    a = _mm(h, w_gate_t, tb=True, name=f"ffn_gate_{tag}")
    u = _mm(h, w_up_t, tb=True, name=f"ffn_up_{tag}")
    (f,), _ = _rowwise(f"swiglu_{tag}", lambda rv, vv: ([_silu(rv[0]) * rv[1]], []), [a, u], [], [(F, bf16)], [], sub=16)
    out = _mm(f, w_down, name=f"ffn_down_{tag}")
    return out, (a, u, f)


def _ffn_bwd(h, w_gate_t, w_up_t, w_down, saved, dout, tag):
    a, u, f = saved
    F = w_down.shape[0]
    df = _mm(dout, w_down, tb=True, name=f"ffn_down_dx_{tag}")
    dw_down = _mm(f, dout, ta=True, name=f"ffn_down_dw_{tag}")

    def fn(rv, vv):
        _, vjp = jax.vjp(lambda a_, u_: _silu(a_) * u_, rv[0], rv[1])
        da, du = vjp(rv[2])
        return [da, du], []

    (da, du), _ = _rowwise(f"swiglu_bwd_{tag}", fn, [a, u, df], [], [(F, bf16), (F, bf16)], [], sub=16)
    dh = _mm(du, w_up_t, add=_mm(da, w_gate_t, name=f"ffn_gate_dx_{tag}"), name=f"ffn_up_dx_{tag}")
    dw_gate_t = _mm(da, h, ta=True, name=f"ffn_gate_dw_{tag}")
    dw_up_t = _mm(du, h, ta=True, name=f"ffn_up_dw_{tag}")
    return dh, dw_gate_t, dw_up_t, dw_down


def _local_step(x, target, mod, W, small):
    S, D = x.shape
    di = small["hy_ssm_norm_g"].shape[-1]
    nh = small["hy_dt_bias"].shape[-1]
    cd = small["hy_conv_b"].shape[-1]
    m = [[_row(mod[i, j]) for j in range(6)] for i in range(2)]

    w_in_t = W["in_t"]
    off_q = di + cd + nh
    w_qkv_t = w_in_t[off_q:]
    seg = dict(z=(w_in_t, 0, di), xbc=(w_in_t, di, cd), dt=(w_in_t, di + cd, LANES))
    for i, nm in enumerate(("q0", "q1", "q2", "k", "v")):
        seg[nm] = (w_qkv_t, i * D, D)
    w_out_y, w_out_a = W["out"][:di], W["out"][di:]

    g_mix = [_row(small["norm_mix_g"][i]) for i in range(2)]
    g_ffn = [_row(small["norm_ffn_g"][i]) for i in range(2)]
    conv_w, conv_b = small["hy_conv_w_full"], _row(small["hy_conv_b"][0])
    dt_bias, a_log, d_skip = (_pad_lanes(small[k][0]) for k in ("hy_dt_bias", "hy_a_log", "hy_d_skip"))
    g_ssm = _row(small["hy_ssm_norm_g"][0])
    onehot = jnp.asarray(_bucket_onehot())
    rel_t = small["rel_table"].T
    H = D // HEAD_DIM
    bias = [_exact_mm(rel_t[gi * H:(gi + 1) * H], onehot[gi], name=f"rel_bias_{gi}")
            .reshape(H, ATT_BLK, 2 * ATT_BLK) + _band_penalty() for gi in range(3)]

    h1 = _adaln_fwd(x, g_mix[0], m[0][1], m[0][0], "adaln_mix0")
    proj = {nm: _mm(h1, mat, tb=True, b_rows=(off, cnt), name=f"in_{nm}") for nm, (mat, off, cnt) in seg.items()}
    xbc_pre, xbc = _conv_fwd(proj["xbc"], conv_w, conv_b, silu=True, name="ssm_conv")
    y, hin = _ssd_fwd(xbc, proj["dt"], dt_bias, a_log, d_skip, di, "ssd_fwd")
    (yg,), _ = _rowwise("ssm_gate", lambda rv, vv: ([_gate_f(rv[0], rv[1], vv[0])], []),
                        [y, proj["z"]], [g_ssm], [(di, bf16)], [], sub=16)
    og = [_attn_fwd(proj[f"q{gi}"], proj["k"], proj["v"], bias[gi], d, f"attn_fwd_{gi}")
          for gi, d in enumerate(ATT_DILATIONS)]
    att, att_b, lse_tot = _attn_combine([a for a, _ in og], [b for _, b in og], "attn_combine")
    mix0 = _mm(att_b, w_out_a, add=_mm(yg, w_out_y, name="out_y"), name="out_a")
    x1 = _resid_fwd(x, m[0][2], mix0, "resid_mix0")
    h2 = _adaln_fwd(x1, g_ffn[0], m[0][4], m[0][3], "adaln_ffn0")
    f0, ffn0_saved = _ffn_fwd(h2, W["gate_t0"], W["up_t0"], W["down0"], "0")
    x2 = _resid_fwd(x1, m[0][5], f0, "resid_ffn0")

    h3 = _adaln_fwd(x2, g_mix[1], m[1][1], m[1][0], "adaln_mix1")
    pw1 = _mm(h3, W["pw1"], bias=_row(small["cv_b_pw1_full"]), name="cv_pw1")
    (u,), _ = _rowwise("cv_glu", lambda rv, vv: ([rv[0] * jax.nn.sigmoid(rv[1])], []),
                       [(pw1, 0, D), (pw1, 1, D)], [], [(D, f32)], [])
    (u2,) = _conv_fwd(u, small["cv_w_dw_full"], _row(small["cv_b_dw_full"]), silu=False, name="cv_dw")
    ln_g, ln_b = _row(small["cv_ln_g_full"]), _row(small["cv_ln_b_full"])
    (u3,), _ = _rowwise("cv_lnsilu", lambda rv, vv: ([_lnsilu_f(rv[0], vv[0], vv[1])], []),
                        [u2], [ln_g, ln_b], [(D, bf16)], [], sub=16)
    mix1 = _mm(u3, W["pw2"], bias=_row(small["cv_b_pw2_full"]), name="cv_pw2")
    x3 = _resid_fwd(x2, m[1][2], mix1, "resid_mix1")
    h4 = _adaln_fwd(x3, g_ffn[1], m[1][4], m[1][3], "adaln_ffn1")
    f1, ffn1_saved = _ffn_fwd(h4, W["gate_t1"], W["up_t1"], W["down1"], "1")
    x4 = _resid_fwd(x3, m[1][5], f1, "resid_ffn1")

    g_fin = _row(small["final_norm_g"])

    def final_fn(rv, vv):
        xv, tv = rv
        yv, vjp = jax.vjp(_rms, xv, vv[0])
        err = yv - tv
        dx, dg = vjp(err / D)
        part = 0.5 * jnp.sum(jnp.mean(err * err, -1, keepdims=True), 0, keepdims=True)
        return [dx], [dg, jnp.broadcast_to(part, (1, LANES))]

    (dx4,), (d_fin, loss) = _rowwise("loss_head", final_fn, [x4, target], [g_fin], [(D, f32)], [D, LANES])

    dmod = [[None] * 6 for _ in range(2)]
    d_norm_mix, d_norm_ffn = [None, None], [None, None]
    big = {}

    df1, (dmod[1][5], _) = _resid_bwd(dx4, f1, m[1][5], "resid_ffn1_bwd")
    dh4, big["gate_t1"], big["up_t1"], big["down1"] = _ffn_bwd(h4, W["gate_t1"], W["up_t1"], W["down1"], ffn1_saved, df1, "1")
    dx3, (d_norm_ffn[1], dmod[1][4], dmod[1][3]) = _adaln_bwd(x3, g_ffn[1], m[1][4], m[1][3], dh4, dx4, "adaln_ffn1_bwd")
    dmix1, (dmod[1][2], d_b_pw2) = _resid_bwd(dx3, mix1, m[1][2], "resid_mix1_bwd")
    du3 = _mm(dmix1, W["pw2"], tb=True, name="cv_pw2_dx")
    big["pw2"] = _mm(u3, dmix1, ta=True, name="cv_pw2_dw")

    def lnsilu_bwd(rv, vv):
        _, vjp = jax.vjp(_lnsilu_f, rv[0], vv[0], vv[1])
        du, dg, db = vjp(rv[1])
        return [du], [dg, db]

    (du2,), (d_ln_g, d_ln_b) = _rowwise("cv_lnsilu_bwd", lnsilu_bwd, [u2, du3], [ln_g, ln_b], [(D, f32)], [D, D])
    du, d_w_dw, d_b_dw = _conv_bwd(u, small["cv_w_dw_full"], du2, None, silu=False, name="cv_dw_bwd")

    def glu_bwd(rv, vv):
        a, gt, d = rv
        _, vjp = jax.vjp(lambda a_, g_: a_ * jax.nn.sigmoid(g_), a, gt)
        da, dg = vjp(d)
        return [da, dg], [jnp.sum(da, 0, keepdims=True), jnp.sum(dg, 0, keepdims=True)]

    (dpa, dpg), (d_b1a, d_b1g) = _rowwise("cv_glu_bwd", glu_bwd, [(pw1, 0, D), (pw1, 1, D), du], [],
                                           [(D, bf16), (D, bf16)], [D, D], sub=16)
    dpw1 = jnp.concatenate([dpa, dpg], axis=1)
    d_b_pw1 = jnp.concatenate([d_b1a, d_b1g], axis=1)
    dh3 = _mm(dpw1, W["pw1"], tb=True, name="cv_pw1_dx")
    big["pw1"] = _mm(h3, dpw1, ta=True, name="cv_pw1_dw")
    dx2, (d_norm_mix[1], dmod[1][1], dmod[1][0]) = _adaln_bwd(x2, g_mix[1], m[1][1], m[1][0], dh3, dx3, "adaln_mix1_bwd")

    df0, (dmod[0][5], _) = _resid_bwd(dx2, f0, m[0][5], "resid_ffn0_bwd")
    dh2, big["gate_t0"], big["up_t0"], big["down0"] = _ffn_bwd(h2, W["gate_t0"], W["up_t0"], W["down0"], ffn0_saved, df0, "0")
    dx1, (d_norm_ffn[0], dmod[0][4], dmod[0][3]) = _adaln_bwd(x1, g_ffn[0], m[0][4], m[0][3], dh2, dx2, "adaln_ffn0_bwd")
    dmix0, (dmod[0][2], _) = _resid_bwd(dx1, mix0, m[0][2], "resid_mix0_bwd")
    dyg = _mm(dmix0, w_out_y, tb=True, name="out_y_dx")
    datt = _mm(dmix0, w_out_a, tb=True, name="out_a_dx")
    big["out"] = jnp.concatenate([_mm(yg, dmix0, ta=True, name="out_y_dw"), _mm(att_b, dmix0, ta=True, name="out_a_dw")], axis=0)

    dq, dks, dvs, dbs = [], [], [], []
    for gi, d in enumerate(ATT_DILATIONS):
        a, b, c_, e = _attn_bwd(proj[f"q{gi}"], proj["k"], proj["v"], bias[gi], att, datt, lse_tot, d, f"attn_bwd_{gi}")
        dq.append(a)
        dks.append(b)
        dvs.append(c_)
        dbs.append(e)
    dk = _add3(*dks, "attn_dk")
    dv = _add3(*dvs, "attn_dv")
    d_rel = jnp.concatenate(
        [_exact_mm(dbs[gi].reshape(H, -1), onehot[gi], tb=True, name=f"rel_grad_{gi}") for gi in range(3)], axis=0).T

    def gate_bwd(rv, vv):
        _, vjp = jax.vjp(_gate_f, rv[0], rv[1], vv[0])
        dy_, dz_, dg_ = vjp(rv[2])
        return [dy_, dz_], [dg_]

    (dy, dz), (d_g_ssm,) = _rowwise("ssm_gate_bwd", gate_bwd, [y, proj["z"], dyg], [g_ssm], [(di, f32), (di, bf16)], [di],
                                    sub=16)
    dxbc, ddtraw, d_a_log, d_dskip, d_dt_bias = _ssd_bwd(xbc, proj["dt"], dt_bias, a_log, d_skip, hin, y, dy, di, "ssd_bwd")
    dxbc_pre, d_conv_w, d_conv_b = _conv_bwd(proj["xbc"], conv_w, dxbc, xbc_pre, silu=True, name="ssm_conv_bwd",
                                             dx_dtype=bf16)

    dseg = {"z": dz, "xbc": dxbc_pre, "dt": ddtraw, "q0": dq[0], "q1": dq[1], "q2": dq[2], "k": dk, "v": dv}
    dh1 = None
    d_in_parts = []
    for nm, (mat, off, cnt) in seg.items():
        dh1 = _mm(dseg[nm], mat, b_rows=(off, cnt), add=dh1, name=f"in_{nm}_dx")
        dwp = _mm(dseg[nm], h1, ta=True, name=f"in_{nm}_dw")
        d_in_parts.append(dwp[:nh] if nm == "dt" else dwp)
    big["in_t"] = jnp.concatenate(d_in_parts, axis=0)
    dx0, (d_norm_mix[0], dmod[0][1], dmod[0][0]) = _adaln_bwd(x, g_mix[0], m[0][1], m[0][0], dh1, dx1, "adaln_mix0_bwd")

    smallg = dict(
        loss=loss, dmod=jnp.stack([jnp.concatenate(dmod[i], axis=1)[0] for i in range(2)]),
        norm_mix_g=jnp.concatenate(d_norm_mix, axis=0), norm_ffn_g=jnp.concatenate(d_norm_ffn, axis=0),
        hy_conv_w=d_conv_w, hy_conv_b=d_conv_b, hy_dt_bias=d_dt_bias[:, :nh], hy_a_log=d_a_log[:, :nh],
        hy_d_skip=d_dskip[:, :nh], hy_ssm_norm_g=d_g_ssm, rel_table=d_rel,
        cv_b_pw1=d_b_pw1, cv_w_dw=d_w_dw, cv_b_dw=d_b_dw, cv_ln_g=d_ln_g, cv_ln_b=d_ln_b, cv_b_pw2=d_b_pw2,
        final_norm_g=d_fin)
    return dx0, big, smallg


SMALL_GRAD_ORDER = ("loss", "dmod", "norm_mix_g", "norm_ffn_g", "hy_conv_w", "hy_conv_b", "hy_dt_bias", "hy_a_log",
                    "hy_d_skip", "hy_ssm_norm_g", "rel_table", "cv_b_pw1", "cv_w_dw", "cv_b_dw", "cv_ln_g", "cv_ln_b",
                    "cv_b_pw2", "final_norm_g")


def kernel(x, c, ada_w, ada_b, norm_mix_g, norm_ffn_g, hy_w_in, hy_conv_w, hy_conv_b, hy_dt_bias, hy_a_log, hy_d_skip, hy_ssm_norm_g, hy_w_out, rel_table, cv_w_pw1, cv_b_pw1, cv_w_dw, cv_b_dw, cv_ln_g, cv_ln_b, cv_w_pw2, cv_b_pw2, ffn_w_gate, ffn_w_up, ffn_w_down, final_norm_g, loss_target, m_ada_w, m_ada_b, m_norm_mix_g, m_norm_ffn_g, m_hy_w_in, m_hy_conv_w, m_hy_conv_b, m_hy_dt_bias, m_hy_a_log, m_hy_d_skip, m_hy_ssm_norm_g, m_hy_w_out, m_rel_table, m_cv_w_pw1, m_cv_b_pw1, m_cv_w_dw, m_cv_b_dw, m_cv_ln_g, m_cv_ln_b, m_cv_w_pw2, m_cv_b_pw2, m_ffn_w_gate, m_ffn_w_up, m_ffn_w_down, m_final_norm_g, v_ada_w, v_ada_b, v_norm_mix_g, v_norm_ffn_g, v_hy_w_in, v_hy_conv_w, v_hy_conv_b, v_hy_dt_bias, v_hy_a_log, v_hy_d_skip, v_hy_ssm_norm_g, v_hy_w_out, v_rel_table, v_cv_w_pw1, v_cv_b_pw1, v_cv_w_dw, v_cv_b_dw, v_cv_ln_g, v_cv_ln_b, v_cv_w_pw2, v_cv_b_pw2, v_ffn_w_gate, v_ffn_w_up, v_ffn_w_down, v_final_norm_g):
    names = ("ada_w", "ada_b", "norm_mix_g", "norm_ffn_g", "hy_w_in", "hy_conv_w", "hy_conv_b", "hy_dt_bias", "hy_a_log",
             "hy_d_skip", "hy_ssm_norm_g", "hy_w_out", "rel_table", "cv_w_pw1", "cv_b_pw1", "cv_w_dw", "cv_b_dw", "cv_ln_g",
             "cv_ln_b", "cv_w_pw2", "cv_b_pw2", "ffn_w_gate", "ffn_w_up", "ffn_w_down", "final_norm_g")
    w = dict(zip(names, (ada_w, ada_b, norm_mix_g, norm_ffn_g, hy_w_in, hy_conv_w, hy_conv_b, hy_dt_bias, hy_a_log, hy_d_skip,
                         hy_ssm_norm_g, hy_w_out, rel_table, cv_w_pw1, cv_b_pw1, cv_w_dw, cv_b_dw, cv_ln_g, cv_ln_b, cv_w_pw2,
                         cv_b_pw2, ffn_w_gate, ffn_w_up, ffn_w_down, final_norm_g)))
    mom = dict(zip(names, (m_ada_w, m_ada_b, m_norm_mix_g, m_norm_ffn_g, m_hy_w_in, m_hy_conv_w, m_hy_conv_b, m_hy_dt_bias,
                           m_hy_a_log, m_hy_d_skip, m_hy_ssm_norm_g, m_hy_w_out, m_rel_table, m_cv_w_pw1, m_cv_b_pw1, m_cv_w_dw,
                           m_cv_b_dw, m_cv_ln_g, m_cv_ln_b, m_cv_w_pw2, m_cv_b_pw2, m_ffn_w_gate, m_ffn_w_up, m_ffn_w_down,
                           m_final_norm_g)))
    vel = dict(zip(names, (v_ada_w, v_ada_b, v_norm_mix_g, v_norm_ffn_g, v_hy_w_in, v_hy_conv_w, v_hy_conv_b, v_hy_dt_bias,
                           v_hy_a_log, v_hy_d_skip, v_hy_ssm_norm_g, v_hy_w_out, v_rel_table, v_cv_w_pw1, v_cv_b_pw1, v_cv_w_dw,
                           v_cv_b_dw, v_cv_ln_g, v_cv_ln_b, v_cv_w_pw2, v_cv_b_pw2, v_ffn_w_gate, v_ffn_w_up, v_ffn_w_down,
                           v_final_norm_g)))
    S, D = x.shape[1], x.shape[2]
    ax, ay, ac = lax.axis_index("x"), lax.axis_index("y"), lax.axis_index("c")
    me = 4 * ax + 2 * ay + ac
    c_arr = jnp.reshape(ac, (1,)).astype(jnp.int32)
    nmod = ada_w.shape[2]

    w2 = _shards_2d(w)
    big_names = list(w2)
    gathered = _all_gather([w2[nm].astype(bf16) for nm in big_names], "gather_weights")
    W = {nm: _full_from_blocks(nm, g) for nm, g in zip(big_names, gathered)}

    sharded_small = ("hy_conv_w", "cv_b_pw1", "cv_w_dw", "cv_b_dw", "cv_ln_g", "cv_ln_b", "cv_b_pw2")
    vp = _VecPack([c.shape] + [w[nm].shape for nm in sharded_small])
    (sg,) = _all_gather([vp.pack([c] + [w[nm] for nm in sharded_small])], "gather_vectors")
    parts = vp.unpack_stacked(sg)
    c_all = parts[0][:, 0]
    small = {k: w[k] for k in ("norm_mix_g", "norm_ffn_g", "hy_conv_b", "hy_dt_bias", "hy_a_log", "hy_d_skip",
                               "hy_ssm_norm_g", "rel_table", "final_norm_g")}
    for p, nm in zip(parts[1:], sharded_small):
        p = p[:, 0]
        p = jnp.moveaxis(p, 0, -2)
        small[nm + "_full"] = p.reshape(p.shape[:-2] + (N_DEV * p.shape[-1],))

    (cs_all,), _ = _rowwise("ada_silu", lambda rv, vv: ([_silu(rv[0])], []), [c_all], [], [(D, f32)], [])
    b_mine = lax.dynamic_slice_in_dim(ada_b, me * nmod, nmod, axis=1)
    mod_part = jnp.stack([_mm(cs_all, ada_w[i], bias=b_mine[i:i + 1], name=f"ada_mod_{i}") for i in range(2)])
    (mod_all,) = _all_gather([mod_part.reshape(2 * N_DEV, nmod)], "gather_mod")
    mod_all = mod_all.reshape(N_DEV, 2, N_DEV, nmod)
    mod_mine = lax.dynamic_index_in_dim(mod_all, me, axis=2, keepdims=False)
    mod = jnp.transpose(mod_mine, (1, 0, 2)).reshape(2, 6, D)

    dx0, big, sgrad = _local_step(x[0], loss_target[0], mod, W, small)

    gp = _VecPack([sgrad[k].shape for k in SMALL_GRAD_ORDER])
    (g_all,) = _all_gather([gp.pack([sgrad[k] for k in SMALL_GRAD_ORDER])], "gather_small_grads")
    tot = dict(zip(SMALL_GRAD_ORDER, gp.unpack(_sum_slots(g_all, "sum_small_grads"))))
    dmod_all = gp.unpack_stacked(g_all, only=SMALL_GRAD_ORDER.index("dmod"))
    loss = tot["loss"][0, 0]

    grads = {}
    dmod_mine = lax.dynamic_slice_in_dim(dmod_all, me * nmod, nmod, axis=2)
    grads["ada_w"] = jnp.stack([_mm(cs_all, dmod_mine[:, i], ta=True, name=f"ada_w_grad_{i}") for i in range(2)])
    grads["ada_b"] = tot["dmod"]
    grads["norm_mix_g"], grads["norm_ffn_g"] = tot["norm_mix_g"], tot["norm_ffn_g"]
    grads["hy_conv_b"] = tot["hy_conv_b"]
    grads["hy_dt_bias"] = tot["hy_dt_bias"]
    grads["hy_a_log"] = tot["hy_a_log"]
    grads["hy_d_skip"] = tot["hy_d_skip"]
    grads["hy_ssm_norm_g"] = tot["hy_ssm_norm_g"]
    grads["rel_table"] = tot["rel_table"]
    grads["final_norm_g"] = tot["final_norm_g"][0]
    for nm in sharded_small:
        n = w[nm].shape[-1]
        grads[nm] = lax.dynamic_slice_in_dim(tot[nm], me * n, n, axis=1).reshape(w[nm].shape)

    g2 = dict(zip(big_names, _reduce_scatter([_blocks_from_full(nm, big[nm]) for nm in big_names], c_arr, "rs")))
    grads.update(_unshard_2d(g2))

    delta, new_m, new_v = {}, {}, {}
    shp = ada_w.shape
    two = lambda t: t.reshape(-1, shp[-1])
    d_, m_, v_ = _adamw(two(ada_w), two(grads["ada_w"]), two(m_ada_w), two(v_ada_w), "adamw_ada_w")
    delta["ada_w"], new_m["ada_w"], new_v["ada_w"] = d_.reshape(shp), m_.reshape(shp), v_.reshape(shp)
    m2, v2 = _shards_2d(mom), _shards_2d(vel)
    d2, nm2, nv2 = {}, {}, {}
    for nm in big_names:
        d2[nm], nm2[nm], nv2[nm] = _adamw(w2[nm], g2[nm], m2[nm], v2[nm], f"adamw_{nm}")
    delta.update(_unshard_2d(d2))
    new_m.update(_unshard_2d(nm2))
    new_v.update(_unshard_2d(nv2))
    rest = [nm for nm in names if nm not in delta]
    sp = _VecPack([w[nm].shape for nm in rest])
    packs = [sp.pack([t[nm] for nm in rest]) for t in (w, grads, mom, vel)]
    d_, m_, v_ = _adamw(*packs, "adamw_small")
    for nm, a, b, e in zip(rest, sp.unpack(d_), sp.unpack(m_), sp.unpack(v_)):
        delta[nm], new_m[nm], new_v[nm] = a, b, e

    return (loss, dx0[None], *[grads[n] for n in names], *[delta[n] for n in names],
            *[new_m[n] for n in names], *[new_v[n] for n in names])
```

```python
import functools
import math

import numpy as np
import jax
import jax.numpy as jnp
from jax import lax
from jax.experimental import pallas as pl
from jax.experimental.pallas import tpu as pltpu

f32 = jnp.float32
bf16 = jnp.bfloat16
EPS = 1e-6
N_DEV = 8
LANES = 128
SSM_STATE = 128
SSM_CHUNK = 128
SSM_GROUPS = 4
HEAD_DIM = 64
ATT_BLK = 128
ATT_DILATIONS = (1, 4, 16)
REL_BUCKETS = 32
REL_MAX_DIST = 2048
ADAM_LR, ADAM_B1, ADAM_B2, ADAM_EPS, ADAM_WD, ADAM_STEP = 0.001, 0.9, 0.999, 1e-08, 0.01, 10
PACK_COLS = 1024
PACK_ROW_TILE = 256
MESH = pl.DeviceIdType.MESH
VMEM_LIMIT = 48 * 1024 * 1024


def _sds(shape, dtype=f32):
    return jax.ShapeDtypeStruct(tuple(shape), dtype)


def _tile(n, cap, mult):
    best = None
    t = mult
    while t <= min(n, cap):
        if n % t == 0:
            best = t
        t += mult
    return best if best is not None else n


def _params(sem):
    return pltpu.CompilerParams(dimension_semantics=sem, vmem_limit_bytes=VMEM_LIMIT)


def _mm(a, b, *, name, ta=False, tb=False, b_rows=None, bias=None, add=None, out_dtype=f32,
        tm_cap=512, tn_cap=1536, tk_cap=8192):
    if ta:
        K, M = a.shape
    else:
        M, K = a.shape
    off, cnt = b_rows if b_rows is not None else (0, b.shape[0])
    if tb:
        N, K2 = cnt, b.shape[1]
    else:
        K2, N = cnt, b.shape[1]
    assert K == K2, (a.shape, b.shape, ta, tb, b_rows)
    if ta and a.dtype == f32:
        tm_cap = min(tm_cap, 256)
    tm = _tile(M, tm_cap, LANES)
    tn = _tile(math.gcd(off, N) if tb else N, tn_cap, LANES)
    tk = _tile(K if tb else math.gcd(off, K), tk_cap, LANES)
    assert N % tn == 0 and K % tk == 0 and off % (tn if tb else tk) == 0, (name, off, N, K, tn, tk)
    nk = K // tk
    jo, ko = (off // tn, 0) if tb else (0, off // tk)
    has_bias, has_add = bias is not None, add is not None
    dn = (((0 if ta else 1,), (1 if tb else 0,)), ((), ()))

    def body(*refs):
        a_ref, b_ref = refs[0], refs[1]
        pos = 2
        bias_ref = add_ref = None
        if has_bias:
            bias_ref = refs[pos]
            pos += 1
        if has_add:
            add_ref = refs[pos]
            pos += 1
        o_ref = refs[pos]
        k = pl.program_id(2)
        part = lax.dot_general(a_ref[...].astype(bf16), b_ref[...].astype(bf16), dn, preferred_element_type=f32)

        def finish(r):
            if has_bias:
                r = r + bias_ref[...]
            if has_add:
                r = r + add_ref[...]
            o_ref[...] = r.astype(o_ref.dtype)

        if nk == 1:
            finish(part)
        else:
            acc_ref = refs[pos + 1]

            @pl.when(k == 0)
            def _():
                acc_ref[...] = part

            @pl.when((k > 0) & (k < nk - 1))
            def _():
                acc_ref[...] += part

            @pl.when(k == nk - 1)
            def _():
                finish(acc_ref[...] + part)

    in_specs = [
        pl.BlockSpec((tk, tm), lambda i, j, k: (k, i)) if ta else pl.BlockSpec((tm, tk), lambda i, j, k: (i, k)),
        pl.BlockSpec((tn, tk), lambda i, j, k: (j + jo, k)) if tb else pl.BlockSpec((tk, tn), lambda i, j, k: (k + ko, j)),
    ]
    args = [a, b]
    if has_bias:
        in_specs.append(pl.BlockSpec((1, tn), lambda i, j, k: (0, j)))
        args.append(bias)
    if has_add:
        in_specs.append(pl.BlockSpec((tm, tn), lambda i, j, k: (i, j)))
        args.append(add)
    return pl.pallas_call(
        body, name=name, grid=(M // tm, N // tn, nk), in_specs=in_specs,
        out_specs=pl.BlockSpec((tm, tn), lambda i, j, k: (i, j)), out_shape=_sds((M, N), out_dtype),
        scratch_shapes=[pltpu.VMEM((tm, tn), f32)] if nk > 1 else [],
        compiler_params=_params(("parallel", "parallel", "arbitrary")),
    )(*args)


def _rowwise(name, fn, rows, vecs, out_rows, out_accs, *, tr_cap=256, sub=8):
    rows = [r if isinstance(r, tuple) else (r, 0, r.shape[1]) for r in rows]
    R = rows[0][0].shape[0]
    tr = _tile(R, tr_cap, 8)
    sub = sub if tr % sub == 0 else tr
    n_r, n_v, n_or, n_oa = len(rows), len(vecs), len(out_rows), len(out_accs)

    def body(*refs):
        row_refs = refs[:n_r]
        vec_refs = refs[n_r:n_r + n_v]
        orow_refs = refs[n_r + n_v:n_r + n_v + n_or]
        oacc_refs = refs[n_r + n_v + n_or:]
        vv = [r[...] for r in vec_refs]

        def step(s, accs):
            sl = pl.ds(pl.multiple_of(s * sub, sub), sub)
            ro, ao = fn([r[sl, :] for r in row_refs], vv)
            for o_ref, o in zip(orow_refs, ro):
                o_ref[sl, :] = o.astype(o_ref.dtype)
            return tuple(x + y for x, y in zip(accs, ao))

        n_sub = tr // sub
        accs = lax.fori_loop(0, n_sub, step, tuple(jnp.zeros((1, w), f32) for w in out_accs),
                             unroll=4 if n_sub % 4 == 0 else 1)
        if n_oa:
            @pl.when(pl.program_id(0) == 0)
            def _():
                for ref in oacc_refs:
                    ref[...] = jnp.zeros_like(ref)

            for ref, x in zip(oacc_refs, accs):
                ref[...] += x

    in_specs = [pl.BlockSpec((tr, w), functools.partial(lambda i, cb: (i, cb), cb=cb)) for (_, cb, w) in rows]
    in_specs += [pl.BlockSpec((1, v.shape[1]), lambda i: (0, 0)) for v in vecs]
    out_specs = [pl.BlockSpec((tr, w), lambda i: (i, 0)) for (w, _) in out_rows]
    out_specs += [pl.BlockSpec((1, w), lambda i: (0, 0)) for w in out_accs]
    out_shape = [_sds((R, w), dt) for (w, dt) in out_rows] + [_sds((1, w)) for w in out_accs]
    res = pl.pallas_call(
        body, name=name, grid=(R // tr,), in_specs=in_specs, out_specs=out_specs, out_shape=out_shape,
        compiler_params=_params(("arbitrary",)),
    )(*[r[0] for r in rows], *vecs)
    return res[:n_or], res[n_or:]


def _silu(x):
    return x * jax.nn.sigmoid(x)


def _rms(x, g):
    return x * lax.rsqrt(jnp.mean(x * x, -1, keepdims=True) + EPS) * g


def _adaln_f(x, g, sc, sh):
    return _rms(x, g) * (1.0 + sc) + sh


def _gate_f(y, z, g):
    return _rms(y * _silu(z), g)


def _lnsilu_f(u, g, b):
    mu = jnp.mean(u, -1, keepdims=True)
    var = jnp.mean(jnp.square(u - mu), -1, keepdims=True)
    return _silu((u - mu) * lax.rsqrt(var + EPS) * g + b)


def _adaln_fwd(x, g, sc, sh, name):
    (h,), _ = _rowwise(name, lambda rv, vv: ([_adaln_f(rv[0], *vv)], []), [x], [g, sc, sh], [(x.shape[1], bf16)], [],
                       sub=16)
    return h


def _adaln_bwd(x, g, sc, sh, dh, dres, name):
    def fn(rv, vv):
        xv, dhv, drv = rv
        _, vjp = jax.vjp(_adaln_f, xv, *vv)
        dx, dg, dsc, dsh = vjp(dhv)
        return [dx + drv], [dg, dsc, dsh]
    w = x.shape[1]
    (dx,), accs = _rowwise(name, fn, [x, dh, dres], [g, sc, sh], [(w, f32)], [w, w, w])
    return dx, accs


def _resid_fwd(x, gate, mix, name):
    (y,), _ = _rowwise(name, lambda rv, vv: ([rv[0] + vv[0] * rv[1]], []), [x, mix], [gate], [(x.shape[1], f32)], [])
    return y


def _resid_bwd(dx, mix, gate, name):
    def fn(rv, vv):
        dxv, mv = rv
        dm = vv[0] * dxv
        return [dm], [jnp.sum(dxv * mv, 0, keepdims=True), jnp.sum(dm, 0, keepdims=True)]
    w = dx.shape[1]
    (dmix,), accs = _rowwise(name, fn, [dx, mix], [gate], [(w, bf16)], [w, w], sub=16)
    return dmix, accs


def _add3(a, b, c, name):
    (y,), _ = _rowwise(name, lambda rv, vv: ([rv[0] + rv[1] + rv[2]], []), [a, b, c], [], [(a.shape[1], bf16)], [],
                       sub=16)
    return y


CONV_HALO = 32
CONV_ROWS = 64


def _conv_fwd(x, w, b, *, silu, name, tr=512):
    S, C = x.shape
    K = w.shape[0]
    H = CONV_HALO
    assert K - 1 <= H and S % tr == 0 and tr % H == 0 and C % LANES == 0
    nh = tr // H

    def body(xp_ref, xc_ref, w_ref, b_ref, *rest):
        outs, scr = rest[:-1], rest[-1]
        i = pl.program_id(1)
        scr[pl.ds(0, H), :] = jnp.where(i > 0, xp_ref[...], 0.0)
        scr[pl.ds(H, tr), :] = xc_ref[...]
        taps = [w_ref[pl.ds(k, 1), :] for k in range(K)]
        for c0 in range(0, tr, CONV_ROWS):
            acc = jnp.zeros((CONV_ROWS, LANES), f32) + b_ref[...]
            for k in range(K):
                acc = acc + scr[pl.ds(c0 + H - (K - 1) + k, CONV_ROWS), :] * taps[k]
            outs[0][pl.ds(c0, CONV_ROWS), :] = acc
            if silu:
                outs[1][pl.ds(c0, CONV_ROWS), :] = _silu(acc)

    n_out = 2 if silu else 1
    return pl.pallas_call(
        body, name=name, grid=(C // LANES, S // tr),
        in_specs=[pl.BlockSpec((H, LANES), lambda j, i: (jnp.maximum(i * nh - 1, 0), j)),
                  pl.BlockSpec((tr, LANES), lambda j, i: (i, j)),
                  pl.BlockSpec((K, LANES), lambda j, i: (0, j)),
                  pl.BlockSpec((1, LANES), lambda j, i: (0, j))],
        out_specs=[pl.BlockSpec((tr, LANES), lambda j, i: (i, j))] * n_out,
        out_shape=[_sds((S, C))] * n_out,
        scratch_shapes=[pltpu.VMEM((tr + H, LANES), f32)],
        compiler_params=_params(("parallel", "arbitrary")),
    )(x, x, w, b)


def _conv_bwd(x, w, dact, pre, *, silu, name, dx_dtype=f32, tr=512):
    S, C = x.shape
    K = w.shape[0]
    H = CONV_HALO
    nh = tr // H
    n_i = S // tr
    kp = -(-K // 8) * 8

    def dsilu(p):
        s = jax.nn.sigmoid(p)
        return s * (1.0 + p * (1.0 - s))

    def body(*refs):
        if silu:
            xp_ref, xc_ref, w_ref, dc_ref, dn_ref, pc_ref, pn_ref, dx_ref, dw_ref, db_ref, xs, ds = refs
        else:
            xp_ref, xc_ref, w_ref, dc_ref, dn_ref, dx_ref, dw_ref, db_ref, xs, ds = refs
        i = pl.program_id(1)
        xs[pl.ds(0, H), :] = jnp.where(i > 0, xp_ref[...], 0.0)
        xs[pl.ds(H, tr), :] = xc_ref[...]
        dcur = dc_ref[...]
        dnext = dn_ref[...]
        if silu:
            dcur = dcur * dsilu(pc_ref[...])
            dnext = dnext * dsilu(pn_ref[...])
        ds[pl.ds(0, tr), :] = dcur
        ds[pl.ds(tr, H), :] = jnp.where(i < n_i - 1, dnext, 0.0)
        taps = [w_ref[pl.ds(k, 1), :] for k in range(K)]
        fold = lambda t: jnp.sum(t.reshape(CONV_ROWS // 8, 8, LANES), axis=0)
        dw_parts = [jnp.zeros((8, LANES), f32) for _ in range(K)]
        db_part = jnp.zeros((8, LANES), f32)
        for c0 in range(0, tr, CONV_ROWS):
            acc = jnp.zeros((CONV_ROWS, LANES), f32)
            d_c = ds[pl.ds(c0, CONV_ROWS), :]
            for k in range(K):
                acc = acc + ds[pl.ds(c0 + K - 1 - k, CONV_ROWS), :] * taps[k]
                dw_parts[k] = dw_parts[k] + fold(d_c * xs[pl.ds(c0 + H - (K - 1) + k, CONV_ROWS), :])
            db_part = db_part + fold(d_c)
            dx_ref[pl.ds(c0, CONV_ROWS), :] = acc.astype(dx_ref.dtype)

        @pl.when(i == 0)
        def _():
            dw_ref[...] = jnp.zeros_like(dw_ref)
            db_ref[...] = jnp.zeros_like(db_ref)

        for k in range(K):
            dw_ref[pl.ds(k, 1), :] += jnp.sum(dw_parts[k], 0, keepdims=True)
        db_ref[...] += jnp.sum(db_part, 0, keepdims=True)

    prev = pl.BlockSpec((H, LANES), lambda j, i: (jnp.maximum(i * nh - 1, 0), j))
    cur = pl.BlockSpec((tr, LANES), lambda j, i: (i, j))
    nxt = pl.BlockSpec((H, LANES), lambda j, i: (jnp.minimum((i + 1) * nh, n_i * nh - 1), j))
    in_specs = [prev, cur, pl.BlockSpec((K, LANES), lambda j, i: (0, j)), cur, nxt]
    args = [x, x, w, dact, dact]
    if silu:
        in_specs += [cur, nxt]
        args += [pre, pre]
    dx, dw, db = pl.pallas_call(
        body, name=name, grid=(C // LANES, n_i), in_specs=in_specs,
        out_specs=[cur, pl.BlockSpec((kp, LANES), lambda j, i: (0, j)), pl.BlockSpec((1, LANES), lambda j, i: (0, j))],
        out_shape=[_sds((S, C), dx_dtype), _sds((kp, C)), _sds((1, C))],
        scratch_shapes=[pltpu.VMEM((tr + H, LANES), f32), pltpu.VMEM((tr + H, LANES), f32)],
        compiler_params=_params(("parallel", "arbitrary")),
    )(*args)
    return dx, dw[:K], db


def _dot(a, b):
    return jnp.dot(a.astype(bf16), b.astype(bf16), preferred_element_type=f32)


def _dot_nt(a, b):
    return lax.dot_general(a.astype(bf16), b.astype(bf16), (((1,), (1,)), ((), ())), preferred_element_type=f32)


def _dot_tn(a, b):
    return lax.dot_general(a.astype(bf16), b.astype(bf16), (((0,), (0,)), ((), ())), preferred_element_type=f32)


def _softplus(x):
    return jnp.maximum(x, 0.0) + jnp.log(1.0 + jnp.exp(-jnp.abs(x)))


def _tri(q):
    i = lax.broadcasted_iota(jnp.int32, (q, q), 0)
    j = lax.broadcasted_iota(jnp.int32, (q, q), 1)
    return i >= j


def _ssd_prep(dtraw, dt_bias, a_log):
    q = dtraw.shape[0]
    dt = _softplus(dtraw + dt_bias)
    A = -jnp.exp(a_log)
    tri = _tri(q)
    cs = jnp.dot(tri.astype(f32), dt * A, preferred_element_type=f32, precision=lax.Precision.HIGHEST)
    return dt, A, cs, cs.T, tri


def _expand(cols, h0, n, width):
    q = cols.shape[0]
    return jnp.concatenate([jnp.broadcast_to(cols[:, h0 + r:h0 + r + 1], (q, width)) for r in range(n)], axis=1)


def _ssd_fwd(xbc, dtraw, dt_bias, a_log, d_skip, di, name):
    S, CD = xbc.shape
    Q, N, G = SSM_CHUNK, SSM_STATE, SSM_GROUPS
    nc = S // Q
    nh = di // HEAD_DIM
    R = nh // G
    gw = R * HEAD_DIM

    def body(xbc_ref, dt_ref, bias_ref, alog_ref, dsk_ref, y_ref, hin_ref, state):
        c = pl.program_id(0)

        @pl.when(c == 0)
        def _():
            state[...] = jnp.zeros_like(state)

        hin_ref[...] = state[...]
        dt, A, cs, csT, tri = _ssd_prep(dt_ref[...], bias_ref[...], alog_ref[...])
        dsk = dsk_ref[...]
        ecs = jnp.exp(cs)
        dend = jnp.exp(cs[Q - 1:Q, :] - cs)
        elast = jnp.exp(cs[Q - 1:Q, :])
        for g in range(G):
            h0 = g * R
            Bg = xbc_ref[:, pl.ds(di + g * N, N)]
            Cg = xbc_ref[:, pl.ds(di + G * N + g * N, N)]
            xg = xbc_ref[:, pl.ds(g * gw, gw)]
            Hg = state[pl.ds(g * gw, gw), :]
            Gm = _dot_nt(Cg, Bg)
            xdt = xg * _expand(dt, h0, R, HEAD_DIM)
            yoff = _dot_nt(Cg, Hg) * _expand(ecs, h0, R, HEAD_DIM)
            ys = []
            for r in range(R):
                h = h0 + r
                L = jnp.exp(jnp.where(tri, cs[:, h:h + 1] - csT[h:h + 1, :], -jnp.inf))
                ys.append(_dot(Gm * L, xdt[:, r * HEAD_DIM:(r + 1) * HEAD_DIM]))
            y = jnp.concatenate(ys, axis=1) + yoff + xg * _expand(dsk, h0, R, HEAD_DIM)
            y_ref[:, pl.ds(g * gw, gw)] = y
            hnew = _dot_tn(xdt * _expand(dend, h0, R, HEAD_DIM), Bg)
            escale = jnp.concatenate([jnp.broadcast_to(elast[:, h0 + r:h0 + r + 1], (HEAD_DIM, N)) for r in range(R)], axis=0)
            state[pl.ds(g * gw, gw), :] = escale * Hg + hnew

    vec = pl.BlockSpec((1, LANES), lambda c: (0, 0))
    return pl.pallas_call(
        body, name=name, grid=(nc,),
        in_specs=[pl.BlockSpec((Q, CD), lambda c: (c, 0)), pl.BlockSpec((Q, LANES), lambda c: (c, 0)), vec, vec, vec],
        out_specs=[pl.BlockSpec((Q, di), lambda c: (c, 0)), pl.BlockSpec((None, di, N), lambda c: (c, 0, 0))],
        out_shape=[_sds((S, di)), _sds((nc, di, N))],
        scratch_shapes=[pltpu.VMEM((di, N), f32)],
        compiler_params=_params(("arbitrary",)),
    )(xbc, dtraw, dt_bias, a_log, d_skip)


def _dot_exact(a, b):
    return jnp.dot(a, b, preferred_element_type=f32, precision=lax.Precision.HIGHEST)


def _ssd_bwd(xbc, dtraw, dt_bias, a_log, d_skip, hin, y, dy, di, name):
    S, CD = xbc.shape
    Q, N, G = SSM_CHUNK, SSM_STATE, SSM_GROUPS
    nc = S // Q
    nh = di // HEAD_DIM
    R = nh // G
    gw = R * HEAD_DIM
    P = HEAD_DIM
    head_of_col = jnp.asarray((np.arange(di)[:, None] // P == np.arange(LANES)[None, :]).astype(np.float32))
    dsk_wide = jnp.repeat(d_skip[0, :nh], P)[None]

    def body(xbc_ref, dt_ref, bias_ref, alog_ref, dskw_ref, hoc_ref, hin_ref, y_ref, dy_ref,
             dxbc_ref, ddt_ref, dA_ref, ddsk_ref, dtb_ref, dstate, dxdt_all, tend_all, yoff_all, colterm_all):
        c = pl.program_id(0)

        @pl.when(c == 0)
        def _():
            dstate[...] = jnp.zeros_like(dstate)
            dA_ref[...] = jnp.zeros_like(dA_ref)
            ddsk_ref[...] = jnp.zeros_like(ddsk_ref)
            dtb_ref[...] = jnp.zeros_like(dtb_ref)

        dtraw_v = dt_ref[...]
        dt, A, cs, csT, tri = _ssd_prep(dtraw_v, bias_ref[...], alog_ref[...])
        tri_t = jnp.logical_not(tri) | (lax.broadcasted_iota(jnp.int32, (Q, Q), 0) == lax.broadcasted_iota(jnp.int32, (Q, Q), 1))
        ecs = jnp.exp(cs)
        dend = jnp.exp(cs[Q - 1:Q, :] - cs)
        elast = jnp.exp(cs[Q - 1:Q, :])
        hoc = hoc_ref[...]
        state_dot = jnp.sum(_dot_exact(dstate[...] * hin_ref[...], jnp.ones((N, LANES), f32)) * hoc, 0, keepdims=True) * elast
        for g in range(G):
            h0 = g * R
            Bg = xbc_ref[:, pl.ds(di + g * N, N)]
            Cg = xbc_ref[:, pl.ds(di + G * N + g * N, N)]
            xg = xbc_ref[:, pl.ds(g * gw, gw)]
            dyg = dy_ref[:, pl.ds(g * gw, gw)]
            Hg = hin_ref[pl.ds(g * gw, gw), :]
            dHg = dstate[pl.ds(g * gw, gw), :]
            dt_e = _expand(dt, h0, R, P)
            ecs_e = _expand(ecs, h0, R, P)
            dend_e = _expand(dend, h0, R, P)
            cols = pl.ds(g * gw, gw)
            Gm = _dot_nt(Cg, Bg)
            Gm_t = _dot_nt(Bg, Cg)
            xdt = xg * dt_e
            dye = dyg * ecs_e
            bdh = _dot_nt(Bg, dHg)
            dC = _dot(dye, Hg)
            dB = _dot(xdt * dend_e, dHg)
            dHin = _dot_tn(dye, Cg)
            dxdt_state = dend_e * bdh
            end_term = xdt * dxdt_state
            tend_all[:, cols] = end_term
            yoff_all[:, cols] = _dot_nt(Cg, Hg) * ecs_e
            dG = jnp.zeros((Q, Q), f32)
            dxd = []
            for r in range(R):
                h = h0 + r
                sl = slice(r * P, (r + 1) * P)
                seg = cs[:, h:h + 1] - csT[h:h + 1, :]
                L = jnp.exp(jnp.where(tri, seg, -jnp.inf))
                L_t = jnp.exp(jnp.where(tri_t, -seg, -jnp.inf))
                dyh = dyg[:, sl]
                dG = dG + _dot_nt(dyh, xdt[:, sl]) * L
                dxd.append(_dot(Gm_t * L_t, dyh))
            dxdt_diag = jnp.concatenate(dxd, axis=1)
            dxdt = dxdt_diag + dxdt_state
            dxdt_all[:, cols] = dxdt
            colterm_all[:, cols] = xdt.astype(bf16).astype(f32) * dxdt_diag + end_term
            dxbc_ref[:, cols] = dxdt * dt_e + dyg * dskw_ref[:, cols]
            dxbc_ref[:, pl.ds(di + g * N, N)] = dB + _dot_tn(dG, Cg)
            dxbc_ref[:, pl.ds(di + G * N + g * N, N)] = dC + _dot(dG, Bg)
            escale = jnp.concatenate([jnp.broadcast_to(elast[:, h0 + r:h0 + r + 1], (P, N)) for r in range(R)], axis=0)
            dstate[pl.ds(g * gw, gw), :] = escale * dHg + dHin
        xs = xbc_ref[:, pl.ds(0, di)]
        dyv = dy_ref[...]
        yoff = yoff_all[...]
        y_diag = y_ref[...] - dskw_ref[...] * xs - yoff
        rs_y = _dot_exact(dyv.astype(bf16).astype(f32) * y_diag + dyv * yoff, hoc)
        rs_c = _dot_exact(colterm_all[...], hoc)
        rs_x = _dot_exact(dxdt_all[...] * xs, hoc)
        end_dot = _dot_exact(jnp.broadcast_to(jnp.sum(tend_all[...], 0, keepdims=True), (8, di)), hoc)[0:1]
        last = lax.broadcasted_iota(jnp.int32, (Q, 1), 0) == Q - 1
        dcs = rs_y - rs_c + jnp.where(last, end_dot + state_dot, 0.0)
        da = lax.dot_general(tri.astype(f32), dcs, (((0,), (0,)), ((), ())), preferred_element_type=f32,
                             precision=lax.Precision.HIGHEST)
        ddt = da * A + rs_x
        ddtraw = ddt * jax.nn.sigmoid(dtraw_v + bias_ref[...])
        ddt_ref[...] = ddtraw.astype(ddt_ref.dtype)
        dA_ref[...] += jnp.sum(da * dt, 0, keepdims=True) * A
        ddsk_ref[...] += jnp.sum(_dot_exact(dyv * xs, hoc), 0, keepdims=True)
        dtb_ref[...] += jnp.sum(ddtraw, 0, keepdims=True)

    vec = pl.BlockSpec((1, LANES), lambda c: (0, 0))
    rev = lambda c: (nc - 1 - c, 0)
    return pl.pallas_call(
        body, name=name, grid=(nc,),
        in_specs=[pl.BlockSpec((Q, CD), rev), pl.BlockSpec((Q, LANES), rev), vec, vec,
                  pl.BlockSpec((1, di), lambda c: (0, 0)), pl.BlockSpec((di, LANES), lambda c: (0, 0)),
                  pl.BlockSpec((None, di, N), lambda c: (nc - 1 - c, 0, 0)), pl.BlockSpec((Q, di), rev),
                  pl.BlockSpec((Q, di), rev)],
        out_specs=[pl.BlockSpec((Q, CD), rev), pl.BlockSpec((Q, LANES), rev), vec, vec, vec],
        out_shape=[_sds((S, CD)), _sds((S, LANES), bf16), _sds((1, LANES)), _sds((1, LANES)), _sds((1, LANES))],
        scratch_shapes=[pltpu.VMEM((di, N), f32)] + [pltpu.VMEM((Q, di), f32)] * 4,
        compiler_params=_params(("arbitrary",)),
    )(xbc, dtraw, dt_bias, a_log, dsk_wide, head_of_col, hin, y, dy)


def _t5_bucket_np(dist):
    max_exact = REL_BUCKETS // 2
    n = np.maximum(dist, 1).astype(np.float32)
    large = np.float32(max_exact) + np.log(n / np.float32(max_exact)) / np.float32(math.log(REL_MAX_DIST / max_exact)) * np.float32(REL_BUCKETS - max_exact)
    large = np.minimum(large.astype(np.int32), REL_BUCKETS - 1)
    return np.where(dist < max_exact, dist, large)


def _bucket_onehot():
    i = np.arange(ATT_BLK)[:, None]
    j = np.arange(2 * ATT_BLK)[None, :]
    delta = np.maximum(ATT_BLK + i - j, 0)
    out = np.zeros((len(ATT_DILATIONS), REL_BUCKETS, ATT_BLK * 2 * ATT_BLK), np.float32)
    for gi, d in enumerate(ATT_DILATIONS):
        b = _t5_bucket_np(delta * d).reshape(-1)
        out[gi, b, np.arange(b.size)] = 1.0
    return out


def _exact_mm(a, b, *, name, tb=False):
    M, K = a.shape
    N = b.shape[0] if tb else b.shape[1]
    tn = _tile(N, 4096, LANES)
    dn = (((1,), (1 if tb else 0,)), ((), ()))

    def body(a_ref, b_ref, o_ref):
        o_ref[...] = lax.dot_general(a_ref[...], b_ref[...], dn, preferred_element_type=f32,
                                     precision=lax.Precision.HIGHEST)

    return pl.pallas_call(
        body, name=name, grid=(N // tn,),
        in_specs=[pl.BlockSpec((M, K), lambda j: (0, 0)),
                  pl.BlockSpec((tn, K), lambda j: (j, 0)) if tb else pl.BlockSpec((K, tn), lambda j: (0, j))],
        out_specs=pl.BlockSpec((M, tn), lambda j: (0, j)), out_shape=_sds((M, N)),
        compiler_params=_params(("parallel",)),
    )(a, b)


def _band_penalty():
    i = np.arange(ATT_BLK)[:, None]
    j = np.arange(2 * ATT_BLK)[None, :]
    delta = ATT_BLK + i - j
    return np.where((delta >= 0) & (delta <= ATT_BLK), 0.0, -np.inf).astype(np.float32)


def _first_block_keep(n):
    col = lax.broadcasted_iota(jnp.int32, (ATT_BLK, 2 * ATT_BLK), 1)
    return (col >= ATT_BLK) | (n > 0)


ATT_SCALE = HEAD_DIM ** -0.5


def _rows(ref, r, d):
    return ref[...] if d == 1 else ref[pl.ds(r, ATT_BLK, stride=d), :]


def _set_rows(ref, r, d, val):
    if d == 1:
        ref[...] = val
    else:
        ref[pl.ds(r, ATT_BLK, stride=d), :] = val


def _attn_width(d, D):
    return D if d == 1 else LANES


def _over_residues(d, one, unroll=1):
    if d == 1:
        one(0)
    else:
        lax.fori_loop(0, d, lambda r, c: (one(r), c)[1], 0, unroll=unroll)


def _attn_fwd(q, k, v, bias, d, name):
    S, D = q.shape
    nb = S // (d * ATT_BLK)
    W = _attn_width(d, D)
    HB = W // HEAD_DIM

    def body(q_ref, kp_ref, kc_ref, vp_ref, vc_ref, b_ref, o_ref, lse_ref):
        keep = _first_block_keep(pl.program_id(1))

        def one(r):
            qs = (_rows(q_ref, r, d) * ATT_SCALE).astype(bf16)
            kcat = jnp.concatenate([_rows(kp_ref, r, d), _rows(kc_ref, r, d)], axis=0).astype(bf16)
            vcat = jnp.concatenate([_rows(vp_ref, r, d), _rows(vc_ref, r, d)], axis=0).astype(bf16)
            outs, lses = [], []
            for h in range(HB):
                sl = slice(h * HEAD_DIM, (h + 1) * HEAD_DIM)
                s = jnp.where(keep, _dot_nt(qs[:, sl], kcat[:, sl]) + b_ref[h], -jnp.inf)
                m = jnp.max(s, -1, keepdims=True)
                p = jnp.exp(s - m)
                l = jnp.sum(p, -1, keepdims=True)
                outs.append(_dot(p, vcat[:, sl]) / l)
                lses.append(jnp.broadcast_to(m + jnp.log(l), (ATT_BLK, HEAD_DIM)))
            _set_rows(o_ref, r, d, jnp.concatenate(outs, axis=1))
            _set_rows(lse_ref, r, d, jnp.concatenate(lses, axis=1))

        _over_residues(d, one, unroll=4)

    cur = pl.BlockSpec((ATT_BLK * d, W), lambda j, n: (n, j))
    prev = pl.BlockSpec((ATT_BLK * d, W), lambda j, n: (jnp.maximum(n - 1, 0), j))
    return pl.pallas_call(
        body, name=name, grid=(D // W, nb),
        in_specs=[cur, prev, cur, prev, cur, pl.BlockSpec((HB, ATT_BLK, 2 * ATT_BLK), lambda j, n: (j, 0, 0))],
        out_specs=[cur, cur], out_shape=[_sds((S, D)), _sds((S, D))],
        compiler_params=_params(("parallel", "arbitrary")),
    )(q, k, k, v, v, bias)


def _attn_bwd(q, k, v, bias, att, datt, lse_tot, d, name):
    S, D = q.shape
    nb = S // (d * ATT_BLK)
    H = D // HEAD_DIM
    W = _attn_width(d, D)
    HB = W // HEAD_DIM

    def body(q_ref, kp_ref, kc_ref, vp_ref, vc_ref, b_ref, o_ref, do_ref, lse_ref,
             dq_ref, dk_ref, dv_ref, db_ref, carry_k, carry_v):
        n = pl.program_id(1)

        @pl.when(n == 0)
        def _():
            carry_k[...] = jnp.zeros_like(carry_k)
            carry_v[...] = jnp.zeros_like(carry_v)
            db_ref[...] = jnp.zeros_like(db_ref)

        @pl.when(n < nb)
        def _():
            keep = _first_block_keep(n)

            def one(r):
                qs = (_rows(q_ref, r, d) * ATT_SCALE).astype(bf16)
                kcat = jnp.concatenate([_rows(kp_ref, r, d), _rows(kc_ref, r, d)], axis=0).astype(bf16)
                vcat = jnp.concatenate([_rows(vp_ref, r, d), _rows(vc_ref, r, d)], axis=0).astype(bf16)
                dov, lsev = _rows(do_ref, r, d), _rows(lse_ref, r, d)
                dsum_all = dov * _rows(o_ref, r, d)
                dob = dov.astype(bf16)
                dqs, dks, dvs = [], [], []
                for h in range(HB):
                    sl = slice(h * HEAD_DIM, (h + 1) * HEAD_DIM)
                    s = jnp.where(keep, _dot_nt(qs[:, sl], kcat[:, sl]) + b_ref[h], -jnp.inf)
                    p = jnp.exp(s - lsev[:, h * HEAD_DIM:h * HEAD_DIM + 1])
                    dp = _dot_nt(dob[:, sl], vcat[:, sl])
                    ds = p * (dp - jnp.sum(dsum_all[:, sl], 1, keepdims=True))
                    db_ref[h] += ds
                    dqs.append(_dot(ds, kcat[:, sl]) * ATT_SCALE)
                    dks.append(_dot_tn(ds, qs[:, sl]))
                    dvs.append(_dot_tn(p, dob[:, sl]))
                _set_rows(dq_ref, r, d, jnp.concatenate(dqs, axis=1))
                dk = jnp.concatenate(dks, axis=1)
                dv = jnp.concatenate(dvs, axis=1)
                _set_rows(dk_ref, r, d, carry_k[r] + dk[:ATT_BLK])
                _set_rows(dv_ref, r, d, carry_v[r] + dv[:ATT_BLK])
                carry_k[r] = dk[ATT_BLK:]
                carry_v[r] = dv[ATT_BLK:]

            _over_residues(d, one, unroll=2)

        @pl.when(n == nb)
        def _():
            def last(r):
                _set_rows(dk_ref, r, d, carry_k[r])
                _set_rows(dv_ref, r, d, carry_v[r])

            _over_residues(d, last)

    nq = lambda n: jnp.minimum(n, nb - 1)
    cur = pl.BlockSpec((ATT_BLK * d, W), lambda j, n: (nq(n), j))
    prev = pl.BlockSpec((ATT_BLK * d, W), lambda j, n: (jnp.maximum(nq(n) - 1, 0), j))
    done = pl.BlockSpec((ATT_BLK * d, W), lambda j, n: (jnp.maximum(n - 1, 0), j))
    bspec = pl.BlockSpec((HB, ATT_BLK, 2 * ATT_BLK), lambda j, n: (j, 0, 0))
    return pl.pallas_call(
        body, name=name, grid=(D // W, nb + 1),
        in_specs=[cur, prev, cur, prev, cur, bspec, cur, cur, cur],
        out_specs=[cur, done, done, bspec],
        out_shape=[_sds((S, D)), _sds((S, D)), _sds((S, D)), _sds((H, ATT_BLK, 2 * ATT_BLK))],
        scratch_shapes=[pltpu.VMEM((d, ATT_BLK, W), f32), pltpu.VMEM((d, ATT_BLK, W), f32)],
        compiler_params=_params(("arbitrary", "arbitrary")),
    )(q, k, k, v, v, bias, att, datt, lse_tot)


def _attn_combine(os_, lses, name):
    def fn(rv, vv):
        o0, o1, o2, l0, l1, l2 = rv
        m = jnp.maximum(jnp.maximum(l0, l1), l2)
        e0, e1, e2 = jnp.exp(l0 - m), jnp.exp(l1 - m), jnp.exp(l2 - m)
        tot = e0 + e1 + e2
        att = (e0 * o0 + e1 * o1 + e2 * o2) / tot
        return [att, att, m + jnp.log(tot)], []
    w = os_[0].shape[1]
    (att, att_b, lse), _ = _rowwise(name, fn, list(os_) + list(lses), [], [(w, f32), (w, bf16), (w, f32)], [], sub=16)
    return att, att_b, lse


ANY = pl.BlockSpec(memory_space=pl.ANY)


def _all_gather(vs, name):
    n = len(vs)

    def body(*refs):
        x_refs, out_refs = refs[:n], refs[n:2 * n]
        send_sems, recv_sems, local_sems = refs[2 * n:]
        x, y, c = lax.axis_index("x"), lax.axis_index("y"), lax.axis_index("c")
        me, sibling = (x, y, c), (x, y, 1 - c)
        chips = [(1 - x, y), (x, 1 - y), (1 - x, 1 - y)]

        def slot(i, px, py, pc):
            return out_refs[i].at[4 * px + 2 * py + pc]

        def copy(i, k, block, to, src=None):
            return pltpu.make_async_remote_copy(
                src_ref=slot(i, *block) if src is None else src, dst_ref=slot(i, *block),
                send_sem=send_sems.at[i, k], recv_sem=recv_sems.at[i, k], device_id=to, device_id_type=MESH)

        mine = [pltpu.make_async_copy(x_refs[i], slot(i, *me), local_sems.at[i]) for i in range(n)]
        for cp in mine:
            cp.start()
        first = []
        for i in range(n):
            first.append(copy(i, 0, me, sibling, src=x_refs[i]))
            first += [copy(i, 1 + j, me, (*chip, c), src=x_refs[i]) for j, chip in enumerate(chips)]
        for cp in first:
            cp.start()
        passed = []
        for i in range(n):
            for j, chip in enumerate(chips):
                copy(i, 1 + j, (*chip, c), me).wait_recv()
                cp = copy(i, 4 + j, (*chip, c), sibling)
                cp.start()
                passed.append(cp)
        for i in range(n):
            copy(i, 0, sibling, me).wait_recv()
            for j, chip in enumerate(chips):
                copy(i, 4 + j, (*chip, 1 - c), me).wait_recv()
        for cp in first + passed:
            cp.wait_send()
        for cp in mine:
            cp.wait()

    return pl.pallas_call(
        body, name=name, out_shape=[_sds((N_DEV,) + v.shape, v.dtype) for v in vs], in_specs=[ANY] * n,
        out_specs=[ANY] * n,
        scratch_shapes=[pltpu.SemaphoreType.DMA((n, 7)), pltpu.SemaphoreType.DMA((n, 7)), pltpu.SemaphoreType.DMA((n,))],
    )(*vs)


def _rs_sibling(parts, name):
    n = len(parts)

    def body(*refs):
        p_refs, out_refs = refs[:n], refs[n:2 * n]
        send_sems, recv_sems = refs[2 * n:]
        x, y, c = lax.axis_index("x"), lax.axis_index("y"), lax.axis_index("c")
        cps = [pltpu.make_async_remote_copy(
            src_ref=p_refs[i].at[k, 1 - c], dst_ref=out_refs[i].at[k], send_sem=send_sems.at[i, k],
            recv_sem=recv_sems.at[i, k], device_id=(x, y, 1 - c), device_id_type=MESH)
            for i in range(n) for k in range(4)]
        for cp in cps:
            cp.start()
        for cp in cps:
            cp.wait()

    return pl.pallas_call(
        body, name=name, out_shape=[_sds((4,) + p.shape[2:], p.dtype) for p in parts], in_specs=[ANY] * n,
        out_specs=[ANY] * n,
        scratch_shapes=[pltpu.SemaphoreType.DMA((n, 4)), pltpu.SemaphoreType.DMA((n, 4))],
    )(*parts)


def _rs_chips(ts, name):
    n = len(ts)

    def body(*refs):
        t_refs, out_refs = refs[:n], refs[n:2 * n]
        send_sems, recv_sems, local_sems = refs[2 * n:]
        x, y, c = lax.axis_index("x"), lax.axis_index("y"), lax.axis_index("c")
        mine = 2 * x + y
        local = [pltpu.make_async_copy(t_refs[i].at[mine], out_refs[i].at[mine], local_sems.at[i]) for i in range(n)]
        for cp in local:
            cp.start()
        chips = [(1 - x, y), (x, 1 - y), (1 - x, 1 - y)]
        cps = [pltpu.make_async_remote_copy(
            src_ref=t_refs[i].at[2 * px + py], dst_ref=out_refs[i].at[mine], send_sem=send_sems.at[i, j],
            recv_sem=recv_sems.at[i, j], device_id=(px, py, c), device_id_type=MESH)
            for i in range(n) for j, (px, py) in enumerate(chips)]
        for cp in cps:
            cp.start()
        for cp in cps:
            cp.wait()
        for cp in local:
            cp.wait()

    return pl.pallas_call(
        body, name=name, out_shape=[_sds(t.shape, t.dtype) for t in ts], in_specs=[ANY] * n, out_specs=[ANY] * n,
        scratch_shapes=[pltpu.SemaphoreType.DMA((n, 3)), pltpu.SemaphoreType.DMA((n, 3)), pltpu.SemaphoreType.DMA((n,))],
    )(*ts)


def _pair_add(part, recv, c_arr, name):
    _, _, R, C = part.shape
    tr = _tile(R, PACK_ROW_TILE, 16)

    def body(c_ref, p_ref, r_ref, o_ref):
        o_ref[...] = (p_ref[...] + r_ref[...]).astype(o_ref.dtype)

    return pl.pallas_call(
        body, name=name,
        grid_spec=pltpu.PrefetchScalarGridSpec(
            num_scalar_prefetch=1, grid=(4, R // tr),
            in_specs=[pl.BlockSpec((None, None, tr, C), lambda k, i, c_ref: (k, c_ref[0], i, 0)),
                      pl.BlockSpec((None, tr, C), lambda k, i, c_ref: (k, i, 0))],
            out_specs=pl.BlockSpec((None, tr, C), lambda k, i, c_ref: (k, i, 0))),
        out_shape=_sds((4, R, C), bf16),
        compiler_params=_params(("parallel", "parallel")),
    )(c_arr, part, recv)


def _sum_slots(t, name):
    n, R, C = t.shape
    tr = _tile(R, PACK_ROW_TILE, 16)

    def body(t_ref, o_ref):
        acc = t_ref[0].astype(f32)
        for k in range(1, n):
            acc = acc + t_ref[k].astype(f32)
        o_ref[...] = acc

    return pl.pallas_call(
        body, name=name, grid=(R // tr,),
        in_specs=[pl.BlockSpec((n, tr, C), lambda i: (0, i, 0))],
        out_specs=pl.BlockSpec((tr, C), lambda i: (i, 0)), out_shape=_sds((R, C)),
        compiler_params=_params(("parallel",)),
    )(t)


def _reduce_scatter(parts, c_arr, name):
    parts4 = [p.reshape((4, 2) + p.shape[1:]) for p in parts]
    recv = _rs_sibling(parts4, name + "_sibling")
    ts = [_pair_add(p, r, c_arr, f"{name}_pair_{i}") for i, (p, r) in enumerate(zip(parts4, recv))]
    got = _rs_chips(ts, name + "_chips")
    return [_sum_slots(g, f"{name}_sum_{i}") for i, g in enumerate(got)]


HBM_SPEC = pl.BlockSpec(memory_space=pltpu.HBM)
SEM_SPEC = pl.BlockSpec(memory_space=pltpu.SEMAPHORE)
EFFECT = pltpu.SideEffectType.DATAFLOW_SIDE_EFFECTING


def _mesh_pos(p):
    return (p // 4, (p // 2) % 2, p % 2)


def _exchange_copy(src_refs, land_refs, send_sems, recv_sems, whole, i, k, receiving):
    me = 4 * lax.axis_index("x") + 2 * lax.axis_index("y") + lax.axis_index("c")
    to = (me + k) % N_DEV
    frm = (me + N_DEV - k) % N_DEV
    src = src_refs[i] if whole else src_refs[i].at[to]
    s = i * (N_DEV - 1) + k - 1
    send = pltpu.make_async_remote_copy(src_ref=src, dst_ref=land_refs[i].at[me], send_sem=send_sems.at[s],
                                        recv_sem=recv_sems.at[s], device_id=_mesh_pos(to), device_id_type=MESH)
    if not receiving:
        return send
    return send, pltpu.make_async_remote_copy(src_ref=src, dst_ref=land_refs[i].at[frm], send_sem=send_sems.at[s],
                                              recv_sem=recv_sems.at[s], device_id=_mesh_pos(to), device_id_type=MESH)


def _exchange_start(srcs, whole, name):
    n = len(srcs)
    lands = [lax.empty((N_DEV,) + s.shape[-2:], s.dtype) for s in srcs]

    def body(*refs):
        src_refs, land_refs = refs[:n], refs[n:2 * n]
        send_sems, recv_sems, token = refs[2 * n], refs[2 * n + 1], refs[-1]
        for i in range(n):
            for k in range(1, N_DEV):
                _exchange_copy(src_refs, land_refs, send_sems, recv_sems, whole, i, k, False).start()
        token[...] = jnp.zeros_like(token)

    sems = pltpu.SemaphoreType.DMA((n * (N_DEV - 1),))
    outs = pl.pallas_call(
        body, name=name,
        out_shape=(sems, sems, *[pltpu.HBM(a.shape, a.dtype) for a in srcs + lands], _sds((8, LANES))),
        in_specs=[HBM_SPEC] * (2 * n),
        out_specs=(SEM_SPEC, SEM_SPEC, *[HBM_SPEC] * (2 * n), pl.BlockSpec(memory_space=pltpu.VMEM)),
        input_output_aliases={i: 2 + i for i in range(2 * n)},
        compiler_params=pltpu.CompilerParams(has_side_effects=EFFECT),
    )(*[pltpu.with_memory_space_constraint(a, pltpu.HBM) for a in srcs + lands])
    return (outs[0], outs[1], list(outs[2:2 + n]), list(outs[2 + n:2 + 2 * n]), whole), outs[-1]


def _exchange_wait(handle, after, name):
    send_sems, recv_sems, srcs, lands, whole = handle
    n = len(srcs)

    def body(*refs):
        src_refs, land_refs = refs[:n], refs[n:2 * n]
        send_sems, recv_sems = refs[2 * n], refs[2 * n + 1]
        for i in range(n):
            for k in range(1, N_DEV):
                send, recv = _exchange_copy(src_refs, land_refs, send_sems, recv_sems, whole, i, k, True)
                send.wait_send()
                recv.wait_recv()

    outs = pl.pallas_call(
        body, name=name, out_shape=tuple(pltpu.HBM(a.shape, a.dtype) for a in srcs + lands),
        in_specs=[HBM_SPEC] * (2 * n) + [SEM_SPEC, SEM_SPEC, pl.BlockSpec(memory_space=pl.ANY)],
        out_specs=[HBM_SPEC] * (2 * n), input_output_aliases={i: i for i in range(2 * n)},
        compiler_params=pltpu.CompilerParams(has_side_effects=EFFECT),
    )(*srcs, *lands, send_sems, recv_sems, after)
    return list(outs[n:])


def _tie(v, token):
    return v + token[0:1, 0:1].astype(v.dtype).reshape((1,) * v.ndim)


def _with_own(land, own, me):
    return lax.dynamic_update_slice_in_dim(land, own[None].astype(land.dtype), me, 0)


class _Overlap:
    def __init__(self, shards, me):
        self.me = me
        self.names = list(shards)
        self.handle, self.token = _exchange_start([shards[nm] for nm in self.names], True, "weights_start")
        self.sent = {}

    def weights(self, after):
        lands = _exchange_wait(self.handle, after, "weights_wait")
        own = self.handle[2]
        return {nm: _full_from_blocks(nm, _with_own(land, o, self.me)) for nm, land, o in zip(self.names, lands, own)}

    def send(self, tag, grads):
        names = list(grads)
        handle, token = _exchange_start([_blocks_from_full(nm, grads[nm]) for nm in names], False, f"grads_start_{tag}")
        self.sent[tag] = (names, handle)
        return token

    def received(self, tag, after):
        names, handle = self.sent[tag]
        lands = _exchange_wait(handle, after, f"grads_wait_{tag}")
        own = [lax.dynamic_index_in_dim(b, self.me, 0, keepdims=False) for b in handle[2]]
        return {nm: _with_own(land, o, self.me) for nm, land, o in zip(names, lands, own)}


ADAM_ROWS = 32


def _adamw(w, g, m, v, name):
    R, C = w.shape
    cb = LANES if C % LANES == 0 else C

    def body(w_ref, g_ref, m_ref, v_ref, d_ref, m2_ref, v2_ref):
        def update(sl):
            gv = g_ref[sl, :]
            m2 = ADAM_B1 * m_ref[sl, :] + (1.0 - ADAM_B1) * gv
            v2 = ADAM_B2 * v_ref[sl, :] + (1.0 - ADAM_B2) * jnp.square(gv)
            m_hat = m2 / (1.0 - ADAM_B1 ** ADAM_STEP)
            v_hat = v2 / (1.0 - ADAM_B2 ** ADAM_STEP)
            d_ref[sl, :] = -ADAM_LR * (m_hat / (jnp.sqrt(v_hat) + ADAM_EPS) + ADAM_WD * w_ref[sl, :])
            m2_ref[sl, :] = m2
            v2_ref[sl, :] = v2

        main = R // ADAM_ROWS
        if main:
            lax.fori_loop(0, main, lambda i, c: (update(pl.ds(pl.multiple_of(i * ADAM_ROWS, ADAM_ROWS), ADAM_ROWS)), c)[1], 0)
        if R % ADAM_ROWS:
            update(pl.ds(main * ADAM_ROWS, R % ADAM_ROWS))

    spec = pl.BlockSpec((R, cb), lambda j: (0, j))
    return pl.pallas_call(
        body, name=name, grid=(C // cb,), in_specs=[spec] * 4, out_specs=[spec] * 3, out_shape=[_sds((R, C))] * 3,
        compiler_params=_params(("parallel",)),
    )(w, g, m, v)


BIG_PARAMS = ("hy_w_in", "hy_w_out", "cv_w_pw1", "cv_w_pw2", "ffn_w_gate", "ffn_w_up", "ffn_w_down")


def _shards_2d(w):
    t = lambda a: jnp.transpose(a)
    return dict(in_t=t(w["hy_w_in"][0]), out=w["hy_w_out"][0], pw1=w["cv_w_pw1"][0], pw2=w["cv_w_pw2"][0],
                gate_t0=t(w["ffn_w_gate"][0]), gate_t1=t(w["ffn_w_gate"][1]), up_t0=t(w["ffn_w_up"][0]),
                up_t1=t(w["ffn_w_up"][1]), down0=w["ffn_w_down"][0], down1=w["ffn_w_down"][1])


def _unshard_2d(s):
    t = lambda a: jnp.transpose(a)
    return dict(hy_w_in=t(s["in_t"])[None], hy_w_out=s["out"][None], cv_w_pw1=s["pw1"][None], cv_w_pw2=s["pw2"][None],
                ffn_w_gate=jnp.stack([t(s["gate_t0"]), t(s["gate_t1"])]),
                ffn_w_up=jnp.stack([t(s["up_t0"]), t(s["up_t1"])]), ffn_w_down=jnp.stack([s["down0"], s["down1"]]))


def _full_from_blocks(nm, g):
    if nm == "pw1":
        return jnp.transpose(g, (1, 0, 2)).reshape(g.shape[1], N_DEV * g.shape[2])
    return g.reshape(N_DEV * g.shape[1], g.shape[2])


def _blocks_from_full(nm, g):
    if nm == "pw1":
        return jnp.transpose(g.reshape(g.shape[0], N_DEV, g.shape[1] // N_DEV), (1, 0, 2))
    return g.reshape(N_DEV, g.shape[0] // N_DEV, g.shape[1])


class _VecPack:
    def __init__(self, shapes):
        self.shapes = [tuple(s) for s in shapes]
        self.sizes = [int(np.prod(s)) for s in self.shapes]
        total = sum(self.sizes)
        self.rows = -(-(-(-total // LANES)) // 8) * 8
        self.total = total

    def pack(self, arrays):
        flat = jnp.concatenate([a.astype(f32).reshape(-1) for a in arrays])
        flat = jnp.pad(flat, (0, self.rows * LANES - self.total))
        return flat.reshape(self.rows, LANES)

    def unpack(self, packed):
        flat = packed.reshape(-1)
        out, off = [], 0
        for shp, n in zip(self.shapes, self.sizes):
            out.append(flat[off:off + n].reshape(shp))
            off += n
        return out

    def unpack_stacked(self, stacked, only=None):
        flat = stacked.reshape(stacked.shape[0], -1)
        offs = np.concatenate([[0], np.cumsum(self.sizes)])
        get = lambda i: flat[:, offs[i]:offs[i + 1]].reshape((stacked.shape[0],) + self.shapes[i])
        return get(only) if only is not None else [get(i) for i in range(len(self.shapes))]


def _row(v):
    return v.reshape(1, -1)


def _pad_lanes(v):
    v = v.reshape(1, -1)
    return jnp.pad(v, ((0, 0), (0, LANES - v.shape[1])))


def _ffn_fwd(h, w_gate_t, w_up_t, w_down, tag):
    F = w_down.shape[0]
    a = _mm(h, w_gate_t, tb=True, name=f"ffn_gate_{tag}")
    u = _mm(h, w_up_t, tb=True, name=f"ffn_up_{tag}")
    (f,), _ = _rowwise(f"swiglu_{tag}", lambda rv, vv: ([_silu(rv[0]) * rv[1]], []), [a, u], [], [(F, bf16)], [], sub=16)
    out = _mm(f, w_down, name=f"ffn_down_{tag}")
    return out, (a, u, f)


def _ffn_bwd(h, w_gate_t, w_up_t, w_down, saved, dout, tag):
    a, u, f = saved
    F = w_down.shape[0]
    df = _mm(dout, w_down, tb=True, name=f"ffn_down_dx_{tag}")
    dw_down = _mm(f, dout, ta=True, out_dtype=bf16, name=f"ffn_down_dw_{tag}")

    def fn(rv, vv):
        _, vjp = jax.vjp(lambda a_, u_: _silu(a_) * u_, rv[0], rv[1])
        da, du = vjp(rv[2])
        return [da, du], []

    (da, du), _ = _rowwise(f"swiglu_bwd_{tag}", fn, [a, u, df], [], [(F, bf16), (F, bf16)], [], sub=16)
    dh = _mm(du, w_up_t, add=_mm(da, w_gate_t, name=f"ffn_gate_dx_{tag}"), name=f"ffn_up_dx_{tag}")
    dw_gate_t = _mm(da, h, ta=True, out_dtype=bf16, name=f"ffn_gate_dw_{tag}")
    dw_up_t = _mm(du, h, ta=True, out_dtype=bf16, name=f"ffn_up_dw_{tag}")
    return dh, dw_gate_t, dw_up_t, dw_down


def _local_step(x, target, mod, w_in_t, comm, small):
    S, D = x.shape
    di = small["hy_ssm_norm_g"].shape[-1]
    nh = small["hy_dt_bias"].shape[-1]
    cd = small["hy_conv_b"].shape[-1]
    m = [[_row(mod[i, j]) for j in range(6)] for i in range(2)]

    off_q = di + cd + nh
    w_qkv_t = w_in_t[off_q:]
    seg = dict(z=(w_in_t, 0, di), xbc=(w_in_t, di, cd), dt=(w_in_t, di + cd, LANES))
    for i, nm in enumerate(("q0", "q1", "q2", "k", "v")):
        seg[nm] = (w_qkv_t, i * D, D)

    g_mix = [_row(small["norm_mix_g"][i]) for i in range(2)]
    g_ffn = [_row(small["norm_ffn_g"][i]) for i in range(2)]
    conv_w, conv_b = small["hy_conv_w_full"], _row(small["hy_conv_b"][0])
    dt_bias, a_log, d_skip = (_pad_lanes(small[k][0]) for k in ("hy_dt_bias", "hy_a_log", "hy_d_skip"))
    g_ssm = _row(small["hy_ssm_norm_g"][0])
    onehot = jnp.asarray(_bucket_onehot())
    rel_t = small["rel_table"].T
    H = D // HEAD_DIM
    bias = [_exact_mm(rel_t[gi * H:(gi + 1) * H], onehot[gi], name=f"rel_bias_{gi}")
            .reshape(H, ATT_BLK, 2 * ATT_BLK) + _band_penalty() for gi in range(3)]

    h1 = _adaln_fwd(x, g_mix[0], m[0][1], m[0][0], "adaln_mix0")
    proj = {nm: _mm(h1, mat, tb=True, b_rows=(off, cnt), name=f"in_{nm}") for nm, (mat, off, cnt) in seg.items()}
    xbc_pre, xbc = _conv_fwd(proj["xbc"], conv_w, conv_b, silu=True, name="ssm_conv", tr=1024)
    y, hin = _ssd_fwd(xbc, proj["dt"], dt_bias, a_log, d_skip, di, "ssd_fwd")
    (yg,), _ = _rowwise("ssm_gate", lambda rv, vv: ([_gate_f(rv[0], rv[1], vv[0])], []),
                        [y, proj["z"]], [g_ssm], [(di, bf16)], [], sub=16)
    og = [_attn_fwd(proj[f"q{gi}"], proj["k"], proj["v"], bias[gi], d, f"attn_fwd_{gi}")
          for gi, d in enumerate(ATT_DILATIONS)]
    att, att_b, lse_tot = _attn_combine([a for a, _ in og], [b for _, b in og], "attn_combine")
    W = comm.weights(after=att_b)
    w_out_y, w_out_a = W["out"][:di], W["out"][di:]
    mix0 = _mm(att_b, w_out_a, add=_mm(yg, w_out_y, name="out_y"), name="out_a")
    x1 = _resid_fwd(x, m[0][2], mix0, "resid_mix0")
    h2 = _adaln_fwd(x1, g_ffn[0], m[0][4], m[0][3], "adaln_ffn0")
    f0, ffn0_saved = _ffn_fwd(h2, W["gate_t0"], W["up_t0"], W["down0"], "0")
    x2 = _resid_fwd(x1, m[0][5], f0, "resid_ffn0")

    h3 = _adaln_fwd(x2, g_mix[1], m[1][1], m[1][0], "adaln_mix1")
    pw1 = _mm(h3, W["pw1"], bias=_row(small["cv_b_pw1_full"]), name="cv_pw1")
    (u,), _ = _rowwise("cv_glu", lambda rv, vv: ([rv[0] * jax.nn.sigmoid(rv[1])], []),
                       [(pw1, 0, D), (pw1, 1, D)], [], [(D, f32)], [])
    (u2,) = _conv_fwd(u, small["cv_w_dw_full"], _row(small["cv_b_dw_full"]), silu=False, name="cv_dw")
    ln_g, ln_b = _row(small["cv_ln_g_full"]), _row(small["cv_ln_b_full"])
    (u3,), _ = _rowwise("cv_lnsilu", lambda rv, vv: ([_lnsilu_f(rv[0], vv[0], vv[1])], []),
                        [u2], [ln_g, ln_b], [(D, bf16)], [], sub=16)
    mix1 = _mm(u3, W["pw2"], bias=_row(small["cv_b_pw2_full"]), name="cv_pw2")
    x3 = _resid_fwd(x2, m[1][2], mix1, "resid_mix1")
    h4 = _adaln_fwd(x3, g_ffn[1], m[1][4], m[1][3], "adaln_ffn1")
    f1, ffn1_saved = _ffn_fwd(h4, W["gate_t1"], W["up_t1"], W["down1"], "1")
    x4 = _resid_fwd(x3, m[1][5], f1, "resid_ffn1")

    g_fin = _row(small["final_norm_g"])

    def final_fn(rv, vv):
        xv, tv = rv
        yv, vjp = jax.vjp(_rms, xv, vv[0])
        err = yv - tv
        dx, dg = vjp(err / D)
        part = 0.5 * jnp.sum(jnp.mean(err * err, -1, keepdims=True), 0, keepdims=True)
        return [dx], [dg, jnp.broadcast_to(part, (1, LANES))]

    (dx4,), (d_fin, loss) = _rowwise("loss_head", final_fn, [x4, target], [g_fin], [(D, f32)], [D, LANES])

    dmod = [[None] * 6 for _ in range(2)]
    d_norm_mix, d_norm_ffn = [None, None], [None, None]
    big = {}

    df1, (dmod[1][5], _) = _resid_bwd(dx4, f1, m[1][5], "resid_ffn1_bwd")
    dh4, big["gate_t1"], big["up_t1"], big["down1"] = _ffn_bwd(h4, W["gate_t1"], W["up_t1"], W["down1"], ffn1_saved, df1, "1")
    dx3, (d_norm_ffn[1], dmod[1][4], dmod[1][3]) = _adaln_bwd(x3, g_ffn[1], m[1][4], m[1][3], dh4, dx4, "adaln_ffn1_bwd")
    dmix1, (dmod[1][2], d_b_pw2) = _resid_bwd(dx3, mix1, m[1][2], "resid_mix1_bwd")
    du3 = _mm(dmix1, W["pw2"], tb=True, name="cv_pw2_dx")
    big["pw2"] = _mm(u3, dmix1, ta=True, out_dtype=bf16, name="cv_pw2_dw")

    def lnsilu_bwd(rv, vv):
        _, vjp = jax.vjp(_lnsilu_f, rv[0], vv[0], vv[1])
        du, dg, db = vjp(rv[1])
        return [du], [dg, db]

    (du2,), (d_ln_g, d_ln_b) = _rowwise("cv_lnsilu_bwd", lnsilu_bwd, [u2, du3], [ln_g, ln_b], [(D, f32)], [D, D])
    du, d_w_dw, d_b_dw = _conv_bwd(u, small["cv_w_dw_full"], du2, None, silu=False, name="cv_dw_bwd")

    def glu_bwd(rv, vv):
        a, gt, d = rv
        _, vjp = jax.vjp(lambda a_, g_: a_ * jax.nn.sigmoid(g_), a, gt)
        da, dg = vjp(d)
        return [da, dg], [jnp.sum(da, 0, keepdims=True), jnp.sum(dg, 0, keepdims=True)]

    (dpa, dpg), (d_b1a, d_b1g) = _rowwise("cv_glu_bwd", glu_bwd, [(pw1, 0, D), (pw1, 1, D), du], [],
                                           [(D, bf16), (D, bf16)], [D, D], sub=16)
    dpw1 = jnp.concatenate([dpa, dpg], axis=1)
    d_b_pw1 = jnp.concatenate([d_b1a, d_b1g], axis=1)
    dh3 = _mm(dpw1, W["pw1"], tb=True, name="cv_pw1_dx")
    big["pw1"] = _mm(h3, dpw1, ta=True, out_dtype=bf16, name="cv_pw1_dw")
    token = comm.send("layer1", {nm: big[nm] for nm in ("gate_t1", "up_t1", "down1", "pw2", "pw1")})
    dx2, (d_norm_mix[1], dmod[1][1], dmod[1][0]) = _adaln_bwd(x2, g_mix[1], m[1][1], _tie(m[1][0], token), dh3, dx3,
                                                              "adaln_mix1_bwd")

    df0, (dmod[0][5], _) = _resid_bwd(dx2, f0, m[0][5], "resid_ffn0_bwd")
    dh2, big["gate_t0"], big["up_t0"], big["down0"] = _ffn_bwd(h2, W["gate_t0"], W["up_t0"], W["down0"], ffn0_saved, df0, "0")
    dx1, (d_norm_ffn[0], dmod[0][4], dmod[0][3]) = _adaln_bwd(x1, g_ffn[0], m[0][4], m[0][3], dh2, dx2, "adaln_ffn0_bwd")
    dmix0, (dmod[0][2], _) = _resid_bwd(dx1, mix0, m[0][2], "resid_mix0_bwd")
    dyg = _mm(dmix0, w_out_y, tb=True, name="out_y_dx")
    datt = _mm(dmix0, w_out_a, tb=True, name="out_a_dx")
    big["out"] = jnp.concatenate([_mm(yg, dmix0, ta=True, out_dtype=bf16, name="out_y_dw"),
                                  _mm(att_b, dmix0, ta=True, out_dtype=bf16, name="out_a_dw")], axis=0)
    token = comm.send("layer0", {nm: big[nm] for nm in ("gate_t0", "up_t0", "down0", "out")})
    bias = [_tie(b, token) for b in bias]
    g_ssm = _tie(g_ssm, token)

    dq, dks, dvs, dbs = [], [], [], []
    for gi, d in enumerate(ATT_DILATIONS):
        a, b, c_, e = _attn_bwd(proj[f"q{gi}"], proj["k"], proj["v"], bias[gi], att, datt, lse_tot, d, f"attn_bwd_{gi}")
        dq.append(a)
        dks.append(b)
        dvs.append(c_)
        dbs.append(e)
    dk = _add3(*dks, "attn_dk")
    dv = _add3(*dvs, "attn_dv")
    d_rel = jnp.concatenate(
        [_exact_mm(dbs[gi].reshape(H, -1), onehot[gi], tb=True, name=f"rel_grad_{gi}") for gi in range(3)], axis=0).T

    def gate_bwd(rv, vv):
        _, vjp = jax.vjp(_gate_f, rv[0], rv[1], vv[0])
        dy_, dz_, dg_ = vjp(rv[2])
        return [dy_, dz_], [dg_]

    (dy, dz), (d_g_ssm,) = _rowwise("ssm_gate_bwd", gate_bwd, [y, proj["z"], dyg], [g_ssm], [(di, f32), (di, bf16)], [di],
                                    sub=16)
    dxbc, ddtraw, d_a_log, d_dskip, d_dt_bias = _ssd_bwd(xbc, proj["dt"], dt_bias, a_log, d_skip, hin, y, dy, di, "ssd_bwd")
    dxbc_pre, d_conv_w, d_conv_b = _conv_bwd(proj["xbc"], conv_w, dxbc, xbc_pre, silu=True, name="ssm_conv_bwd",
                                             dx_dtype=bf16, tr=1024)

    dseg = {"z": dz, "xbc": dxbc_pre, "dt": ddtraw, "q0": dq[0], "q1": dq[1], "q2": dq[2], "k": dk, "v": dv}
    dh1 = None
    d_in_parts = []
    for nm, (mat, off, cnt) in seg.items():
        dh1 = _mm(dseg[nm], mat, b_rows=(off, cnt), add=dh1, name=f"in_{nm}_dx")
        dwp = _mm(dseg[nm], h1, ta=True, out_dtype=bf16, name=f"in_{nm}_dw")
        d_in_parts.append(dwp[:nh] if nm == "dt" else dwp)
    big["in_t"] = jnp.concatenate(d_in_parts, axis=0)
    dx0, (d_norm_mix[0], dmod[0][1], dmod[0][0]) = _adaln_bwd(x, g_mix[0], m[0][1], m[0][0], dh1, dx1, "adaln_mix0_bwd")

    smallg = dict(
        loss=loss, dmod=jnp.stack([jnp.concatenate(dmod[i], axis=1)[0] for i in range(2)]),
        norm_mix_g=jnp.concatenate(d_norm_mix, axis=0), norm_ffn_g=jnp.concatenate(d_norm_ffn, axis=0),
        hy_conv_w=d_conv_w, hy_conv_b=d_conv_b, hy_dt_bias=d_dt_bias[:, :nh], hy_a_log=d_a_log[:, :nh],
        hy_d_skip=d_dskip[:, :nh], hy_ssm_norm_g=d_g_ssm, rel_table=d_rel,
        cv_b_pw1=d_b_pw1, cv_w_dw=d_w_dw, cv_b_dw=d_b_dw, cv_ln_g=d_ln_g, cv_ln_b=d_ln_b, cv_b_pw2=d_b_pw2,
        final_norm_g=d_fin)
    return dx0, big["in_t"], smallg


SMALL_GRAD_ORDER = ("loss", "dmod", "norm_mix_g", "norm_ffn_g", "hy_conv_w", "hy_conv_b", "hy_dt_bias", "hy_a_log",
                    "hy_d_skip", "hy_ssm_norm_g", "rel_table", "cv_b_pw1", "cv_w_dw", "cv_b_dw", "cv_ln_g", "cv_ln_b",
                    "cv_b_pw2", "final_norm_g")


def kernel(x, c, ada_w, ada_b, norm_mix_g, norm_ffn_g, hy_w_in, hy_conv_w, hy_conv_b, hy_dt_bias, hy_a_log, hy_d_skip, hy_ssm_norm_g, hy_w_out, rel_table, cv_w_pw1, cv_b_pw1, cv_w_dw, cv_b_dw, cv_ln_g, cv_ln_b, cv_w_pw2, cv_b_pw2, ffn_w_gate, ffn_w_up, ffn_w_down, final_norm_g, loss_target, m_ada_w, m_ada_b, m_norm_mix_g, m_norm_ffn_g, m_hy_w_in, m_hy_conv_w, m_hy_conv_b, m_hy_dt_bias, m_hy_a_log, m_hy_d_skip, m_hy_ssm_norm_g, m_hy_w_out, m_rel_table, m_cv_w_pw1, m_cv_b_pw1, m_cv_w_dw, m_cv_b_dw, m_cv_ln_g, m_cv_ln_b, m_cv_w_pw2, m_cv_b_pw2, m_ffn_w_gate, m_ffn_w_up, m_ffn_w_down, m_final_norm_g, v_ada_w, v_ada_b, v_norm_mix_g, v_norm_ffn_g, v_hy_w_in, v_hy_conv_w, v_hy_conv_b, v_hy_dt_bias, v_hy_a_log, v_hy_d_skip, v_hy_ssm_norm_g, v_hy_w_out, v_rel_table, v_cv_w_pw1, v_cv_b_pw1, v_cv_w_dw, v_cv_b_dw, v_cv_ln_g, v_cv_ln_b, v_cv_w_pw2, v_cv_b_pw2, v_ffn_w_gate, v_ffn_w_up, v_ffn_w_down, v_final_norm_g):
    names = ("ada_w", "ada_b", "norm_mix_g", "norm_ffn_g", "hy_w_in", "hy_conv_w", "hy_conv_b", "hy_dt_bias", "hy_a_log",
             "hy_d_skip", "hy_ssm_norm_g", "hy_w_out", "rel_table", "cv_w_pw1", "cv_b_pw1", "cv_w_dw", "cv_b_dw", "cv_ln_g",
             "cv_ln_b", "cv_w_pw2", "cv_b_pw2", "ffn_w_gate", "ffn_w_up", "ffn_w_down", "final_norm_g")
    w = dict(zip(names, (ada_w, ada_b, norm_mix_g, norm_ffn_g, hy_w_in, hy_conv_w, hy_conv_b, hy_dt_bias, hy_a_log, hy_d_skip,
                         hy_ssm_norm_g, hy_w_out, rel_table, cv_w_pw1, cv_b_pw1, cv_w_dw, cv_b_dw, cv_ln_g, cv_ln_b, cv_w_pw2,
                         cv_b_pw2, ffn_w_gate, ffn_w_up, ffn_w_down, final_norm_g)))
    mom = dict(zip(names, (m_ada_w, m_ada_b, m_norm_mix_g, m_norm_ffn_g, m_hy_w_in, m_hy_conv_w, m_hy_conv_b, m_hy_dt_bias,
                           m_hy_a_log, m_hy_d_skip, m_hy_ssm_norm_g, m_hy_w_out, m_rel_table, m_cv_w_pw1, m_cv_b_pw1, m_cv_w_dw,
                           m_cv_b_dw, m_cv_ln_g, m_cv_ln_b, m_cv_w_pw2, m_cv_b_pw2, m_ffn_w_gate, m_ffn_w_up, m_ffn_w_down,
                           m_final_norm_g)))
    vel = dict(zip(names, (v_ada_w, v_ada_b, v_norm_mix_g, v_norm_ffn_g, v_hy_w_in, v_hy_conv_w, v_hy_conv_b, v_hy_dt_bias,
                           v_hy_a_log, v_hy_d_skip, v_hy_ssm_norm_g, v_hy_w_out, v_rel_table, v_cv_w_pw1, v_cv_b_pw1, v_cv_w_dw,
                           v_cv_b_dw, v_cv_ln_g, v_cv_ln_b, v_cv_w_pw2, v_cv_b_pw2, v_ffn_w_gate, v_ffn_w_up, v_ffn_w_down,
                           v_final_norm_g)))
    S, D = x.shape[1], x.shape[2]
    ax, ay, ac = lax.axis_index("x"), lax.axis_index("y"), lax.axis_index("c")
    me = 4 * ax + 2 * ay + ac
    c_arr = jnp.reshape(ac, (1,)).astype(jnp.int32)
    nmod = ada_w.shape[2]

    w2 = _shards_2d(w)
    big_names = list(w2)
    (g_in,) = _all_gather([w2["in_t"].astype(bf16)], "gather_w_in")
    w_in_t = _full_from_blocks("in_t", g_in)
    comm = _Overlap({nm: w2[nm].astype(bf16) for nm in big_names if nm != "in_t"}, me)

    sharded_small = ("hy_conv_w", "cv_b_pw1", "cv_w_dw", "cv_b_dw", "cv_ln_g", "cv_ln_b", "cv_b_pw2")
    vp = _VecPack([c.shape] + [w[nm].shape for nm in sharded_small])
    (sg,) = _all_gather([vp.pack([_tie(c, comm.token)] + [w[nm] for nm in sharded_small])], "gather_vectors")
    parts = vp.unpack_stacked(sg)
    c_all = parts[0][:, 0]
    small = {k: w[k] for k in ("norm_mix_g", "norm_ffn_g", "hy_conv_b", "hy_dt_bias", "hy_a_log", "hy_d_skip",
                               "hy_ssm_norm_g", "rel_table", "final_norm_g")}
    for p, nm in zip(parts[1:], sharded_small):
        p = p[:, 0]
        p = jnp.moveaxis(p, 0, -2)
        small[nm + "_full"] = p.reshape(p.shape[:-2] + (N_DEV * p.shape[-1],))

    (cs_all,), _ = _rowwise("ada_silu", lambda rv, vv: ([_silu(rv[0])], []), [c_all], [], [(D, f32)], [])
    b_mine = lax.dynamic_slice_in_dim(ada_b, me * nmod, nmod, axis=1)
    mod_part = jnp.stack([_mm(cs_all, ada_w[i], bias=b_mine[i:i + 1], name=f"ada_mod_{i}") for i in range(2)])
    (mod_all,) = _all_gather([mod_part.reshape(2 * N_DEV, nmod)], "gather_mod")
    mod_all = mod_all.reshape(N_DEV, 2, N_DEV, nmod)
    mod_mine = lax.dynamic_index_in_dim(mod_all, me, axis=2, keepdims=False)
    mod = jnp.transpose(mod_mine, (1, 0, 2)).reshape(2, 6, D)

    dx0, d_in_t, sgrad = _local_step(x[0], loss_target[0], mod, w_in_t, comm, small)
    comm.send("in", {"in_t": d_in_t})

    gp = _VecPack([sgrad[k].shape for k in SMALL_GRAD_ORDER])
    (g_all,) = _all_gather([gp.pack([sgrad[k] for k in SMALL_GRAD_ORDER])], "gather_small_grads")
    tot = dict(zip(SMALL_GRAD_ORDER, gp.unpack(_sum_slots(g_all, "sum_small_grads"))))
    dmod_all = gp.unpack_stacked(g_all, only=SMALL_GRAD_ORDER.index("dmod"))
    loss = tot["loss"][0, 0]

    grads = {}
    dmod_mine = lax.dynamic_slice_in_dim(dmod_all, me * nmod, nmod, axis=2)
    grads["ada_w"] = jnp.stack([_mm(cs_all, dmod_mine[:, i], ta=True, name=f"ada_w_grad_{i}") for i in range(2)])
    grads["ada_b"] = tot["dmod"]
    grads["norm_mix_g"], grads["norm_ffn_g"] = tot["norm_mix_g"], tot["norm_ffn_g"]
    grads["hy_conv_b"] = tot["hy_conv_b"]
    grads["hy_dt_bias"] = tot["hy_dt_bias"]
    grads["hy_a_log"] = tot["hy_a_log"]
    grads["hy_d_skip"] = tot["hy_d_skip"]
    grads["hy_ssm_norm_g"] = tot["hy_ssm_norm_g"]
    grads["rel_table"] = tot["rel_table"]
    grads["final_norm_g"] = tot["final_norm_g"][0]
    for nm in sharded_small:
        n = w[nm].shape[-1]
        grads[nm] = lax.dynamic_slice_in_dim(tot[nm], me * n, n, axis=1).reshape(w[nm].shape)

    delta, new_m, new_v = {}, {}, {}
    shp = ada_w.shape
    two = lambda t: t.reshape(-1, shp[-1])
    d_, m_, v_ = _adamw(two(ada_w), two(grads["ada_w"]), two(m_ada_w), two(v_ada_w), "adamw_ada_w")
    delta["ada_w"], new_m["ada_w"], new_v["ada_w"] = d_.reshape(shp), m_.reshape(shp), v_.reshape(shp)
    rest = [nm for nm in names if nm not in BIG_PARAMS and nm != "ada_w"]
    sp = _VecPack([w[nm].shape for nm in rest])
    packs = [sp.pack([t[nm] for nm in rest]) for t in (w, grads, mom, vel)]
    ds_, ms_, vs_ = _adamw(*packs, "adamw_small")
    for nm, a, b, e in zip(rest, sp.unpack(ds_), sp.unpack(ms_), sp.unpack(vs_)):
        delta[nm], new_m[nm], new_v[nm] = a, b, e

    g2 = {}
    after = d_
    for tag in ("layer1", "layer0", "in"):
        for nm, slots in comm.received(tag, after).items():
            g2[nm] = _sum_slots(slots, f"sum_{nm}")
            after = g2[nm]
    grads.update(_unshard_2d(g2))
    m2, v2 = _shards_2d(mom), _shards_2d(vel)
    d2, nm2, nv2 = {}, {}, {}
    for nm in big_names:
        d2[nm], nm2[nm], nv2[nm] = _adamw(w2[nm], g2[nm], m2[nm], v2[nm], f"adamw_{nm}")
    delta.update(_unshard_2d(d2))
    new_m.update(_unshard_2d(nm2))
    new_v.update(_unshard_2d(nv2))

    return (loss, dx0[None], *[grads[n] for n in names], *[delta[n] for n in names],
            *[new_m[n] for n in names], *[new_v[n] for n in names])
```

```python
import functools
import math

import numpy as np
import jax
import jax.numpy as jnp
from jax import lax
from jax.experimental import pallas as pl
from jax.experimental.pallas import tpu as pltpu

f32 = jnp.float32
bf16 = jnp.bfloat16
EPS = 1e-6
N_DEV = 8
LANES = 128
SSM_STATE = 128
SSM_CHUNK = 128
SSM_GROUPS = 4
HEAD_DIM = 64
ATT_BLK = 128
ATT_DILATIONS = (1, 4, 16)
REL_BUCKETS = 32
REL_MAX_DIST = 2048
ADAM_LR, ADAM_B1, ADAM_B2, ADAM_EPS, ADAM_WD, ADAM_STEP = 0.001, 0.9, 0.999, 1e-08, 0.01, 10
PACK_COLS = 1024
PACK_ROW_TILE = 256
MESH = pl.DeviceIdType.MESH
VMEM_LIMIT = 48 * 1024 * 1024


def _sds(shape, dtype=f32):
    return jax.ShapeDtypeStruct(tuple(shape), dtype)


def _tile(n, cap, mult):
    best = None
    t = mult
    while t <= min(n, cap):
        if n % t == 0:
            best = t
        t += mult
    return best if best is not None else n


def _params(sem):
    return pltpu.CompilerParams(dimension_semantics=sem, vmem_limit_bytes=VMEM_LIMIT)


def _mm(a, b, *, name, ta=False, tb=False, b_rows=None, bias=None, add=None, out_dtype=f32,
        tm_cap=512, tn_cap=1536, tk_cap=8192):
    if ta:
        K, M = a.shape
    else:
        M, K = a.shape
    off, cnt = b_rows if b_rows is not None else (0, b.shape[0])
    if tb:
        N, K2 = cnt, b.shape[1]
    else:
        K2, N = cnt, b.shape[1]
    assert K == K2, (a.shape, b.shape, ta, tb, b_rows)
    if ta and a.dtype == f32:
        tm_cap = min(tm_cap, 256)
    tm = _tile(M, tm_cap, LANES)
    tn = _tile(math.gcd(off, N) if tb else N, tn_cap, LANES)
    tk = _tile(K if tb else math.gcd(off, K), tk_cap, LANES)
    assert N % tn == 0 and K % tk == 0 and off % (tn if tb else tk) == 0, (name, off, N, K, tn, tk)
    nk = K // tk
    jo, ko = (off // tn, 0) if tb else (0, off // tk)
    has_bias, has_add = bias is not None, add is not None
    dn = (((0 if ta else 1,), (1 if tb else 0,)), ((), ()))

    def body(*refs):
        a_ref, b_ref = refs[0], refs[1]
        pos = 2
        bias_ref = add_ref = None
        if has_bias:
            bias_ref = refs[pos]
            pos += 1
        if has_add:
            add_ref = refs[pos]
            pos += 1
        o_ref = refs[pos]
        k = pl.program_id(2)
        part = lax.dot_general(a_ref[...].astype(bf16), b_ref[...].astype(bf16), dn, preferred_element_type=f32)

        def finish(r):
            if has_bias:
                r = r + bias_ref[...]
            if has_add:
                r = r + add_ref[...]
            o_ref[...] = r.astype(o_ref.dtype)

        if nk == 1:
            finish(part)
        else:
            acc_ref = refs[pos + 1]

            @pl.when(k == 0)
            def _():
                acc_ref[...] = part

            @pl.when((k > 0) & (k < nk - 1))
            def _():
                acc_ref[...] += part

            @pl.when(k == nk - 1)
            def _():
                finish(acc_ref[...] + part)

    in_specs = [
        pl.BlockSpec((tk, tm), lambda i, j, k: (k, i)) if ta else pl.BlockSpec((tm, tk), lambda i, j, k: (i, k)),
        pl.BlockSpec((tn, tk), lambda i, j, k: (j + jo, k)) if tb else pl.BlockSpec((tk, tn), lambda i, j, k: (k + ko, j)),
    ]
    args = [a, b]
    if has_bias:
        in_specs.append(pl.BlockSpec((1, tn), lambda i, j, k: (0, j)))
        args.append(bias)
    if has_add:
        in_specs.append(pl.BlockSpec((tm, tn), lambda i, j, k: (i, j)))
        args.append(add)
    return pl.pallas_call(
        body, name=name, grid=(M // tm, N // tn, nk), in_specs=in_specs,
        out_specs=pl.BlockSpec((tm, tn), lambda i, j, k: (i, j)), out_shape=_sds((M, N), out_dtype),
        scratch_shapes=[pltpu.VMEM((tm, tn), f32)] if nk > 1 else [],
        compiler_params=_params(("parallel", "parallel", "arbitrary")),
    )(*args)


def _rowwise(name, fn, rows, vecs, out_rows, out_accs, *, tr_cap=256, sub=8):
    rows = [r if isinstance(r, tuple) else (r, 0, r.shape[1]) for r in rows]
    R = rows[0][0].shape[0]
    tr = _tile(R, tr_cap, 8)
    sub = sub if tr % sub == 0 else tr
    n_r, n_v, n_or, n_oa = len(rows), len(vecs), len(out_rows), len(out_accs)

    def body(*refs):
        row_refs = refs[:n_r]
        vec_refs = refs[n_r:n_r + n_v]
        orow_refs = refs[n_r + n_v:n_r + n_v + n_or]
        oacc_refs = refs[n_r + n_v + n_or:]
        vv = [r[...] for r in vec_refs]

        n_sub = tr // sub
        together = 4 if n_sub % 4 == 0 else 1

        def step(s, accs):
            for t in range(together):
                sl = pl.ds(pl.multiple_of((s * together + t) * sub, sub), sub)
                ro, ao = fn([r[sl, :] for r in row_refs], vv)
                for o_ref, o in zip(orow_refs, ro):
                    o_ref[sl, :] = o.astype(o_ref.dtype)
                accs = tuple(x + y for x, y in zip(accs, ao))
            return accs

        accs = lax.fori_loop(0, n_sub // together, step, tuple(jnp.zeros((1, w), f32) for w in out_accs))
        if n_oa:
            @pl.when(pl.program_id(0) == 0)
            def _():
                for ref in oacc_refs:
                    ref[...] = jnp.zeros_like(ref)

            for ref, x in zip(oacc_refs, accs):
                ref[...] += x

    in_specs = [pl.BlockSpec((tr, w), functools.partial(lambda i, cb: (i, cb), cb=cb)) for (_, cb, w) in rows]
    in_specs += [pl.BlockSpec((1, v.shape[1]), lambda i: (0, 0)) for v in vecs]
    out_specs = [pl.BlockSpec((tr, w), lambda i: (i, 0)) for (w, _) in out_rows]
    out_specs += [pl.BlockSpec((1, w), lambda i: (0, 0)) for w in out_accs]
    out_shape = [_sds((R, w), dt) for (w, dt) in out_rows] + [_sds((1, w)) for w in out_accs]
    res = pl.pallas_call(
        body, name=name, grid=(R // tr,), in_specs=in_specs, out_specs=out_specs, out_shape=out_shape,
        compiler_params=_params(("arbitrary",)),
    )(*[r[0] for r in rows], *vecs)
    return res[:n_or], res[n_or:]


def _silu(x):
    return x * jax.nn.sigmoid(x)


def _rms(x, g):
    return x * lax.rsqrt(jnp.mean(x * x, -1, keepdims=True) + EPS) * g


def _adaln_f(x, g, sc, sh):
    return _rms(x, g) * (1.0 + sc) + sh


def _gate_f(y, z, g):
    return _rms(y * _silu(z), g)


def _lnsilu_f(u, g, b):
    mu = jnp.mean(u, -1, keepdims=True)
    var = jnp.mean(jnp.square(u - mu), -1, keepdims=True)
    return _silu((u - mu) * lax.rsqrt(var + EPS) * g + b)


def _adaln_fwd(x, g, sc, sh, name):
    (h,), _ = _rowwise(name, lambda rv, vv: ([_adaln_f(rv[0], *vv)], []), [x], [g, sc, sh], [(x.shape[1], bf16)], [],
                       sub=16)
    return h


def _adaln_bwd(x, g, sc, sh, dh, dres, name):
    def fn(rv, vv):
        xv, dhv, drv = rv
        _, vjp = jax.vjp(_adaln_f, xv, *vv)
        dx, dg, dsc, dsh = vjp(dhv)
        return [dx + drv], [dg, dsc, dsh]
    w = x.shape[1]
    (dx,), accs = _rowwise(name, fn, [x, dh, dres], [g, sc, sh], [(w, f32)], [w, w, w])
    return dx, accs


def _resid_fwd(x, gate, mix, name):
    (y,), _ = _rowwise(name, lambda rv, vv: ([rv[0] + vv[0] * rv[1]], []), [x, mix], [gate], [(x.shape[1], f32)], [])
    return y


def _resid_bwd(dx, mix, gate, name):
    def fn(rv, vv):
        dxv, mv = rv
        dm = vv[0] * dxv
        return [dm], [jnp.sum(dxv * mv, 0, keepdims=True), jnp.sum(dm, 0, keepdims=True)]
    w = dx.shape[1]
    (dmix,), accs = _rowwise(name, fn, [dx, mix], [gate], [(w, bf16)], [w, w], sub=16)
    return dmix, accs


def _add3(a, b, c, name):
    (y,), _ = _rowwise(name, lambda rv, vv: ([rv[0] + rv[1] + rv[2]], []), [a, b, c], [], [(a.shape[1], bf16)], [],
                       sub=16)
    return y


CONV_HALO = 32
CONV_ROWS = 64


def _conv_fwd(x, w, b, *, silu, name, tr=512):
    S, C = x.shape
    K = w.shape[0]
    H = CONV_HALO
    assert K - 1 <= H and S % tr == 0 and tr % H == 0 and C % LANES == 0
    nh = tr // H

    def body(xp_ref, xc_ref, w_ref, b_ref, *rest):
        outs, scr = rest[:-1], rest[-1]
        i = pl.program_id(1)
        scr[pl.ds(0, H), :] = jnp.where(i > 0, xp_ref[...], 0.0)
        scr[pl.ds(H, tr), :] = xc_ref[...]
        taps = [w_ref[pl.ds(k, 1), :] for k in range(K)]
        for c0 in range(0, tr, CONV_ROWS):
            acc = jnp.zeros((CONV_ROWS, LANES), f32) + b_ref[...]
            for k in range(K):
                acc = acc + scr[pl.ds(c0 + H - (K - 1) + k, CONV_ROWS), :] * taps[k]
            outs[0][pl.ds(c0, CONV_ROWS), :] = acc
            if silu:
                outs[1][pl.ds(c0, CONV_ROWS), :] = _silu(acc)

    n_out = 2 if silu else 1
    return pl.pallas_call(
        body, name=name, grid=(C // LANES, S // tr),
        in_specs=[pl.BlockSpec((H, LANES), lambda j, i: (jnp.maximum(i * nh - 1, 0), j)),
                  pl.BlockSpec((tr, LANES), lambda j, i: (i, j)),
                  pl.BlockSpec((K, LANES), lambda j, i: (0, j)),
                  pl.BlockSpec((1, LANES), lambda j, i: (0, j))],
        out_specs=[pl.BlockSpec((tr, LANES), lambda j, i: (i, j))] * n_out,
        out_shape=[_sds((S, C))] * n_out,
        scratch_shapes=[pltpu.VMEM((tr + H, LANES), f32)],
        compiler_params=_params(("parallel", "arbitrary")),
    )(x, x, w, b)


def _conv_bwd(x, w, dact, pre, *, silu, name, dx_dtype=f32, tr=512):
    S, C = x.shape
    K = w.shape[0]
    H = CONV_HALO
    nh = tr // H
    n_i = S // tr
    kp = -(-K // 8) * 8

    def dsilu(p):
        s = jax.nn.sigmoid(p)
        return s * (1.0 + p * (1.0 - s))

    def body(*refs):
        if silu:
            xp_ref, xc_ref, w_ref, dc_ref, dn_ref, pc_ref, pn_ref, dx_ref, dw_ref, db_ref, xs, ds = refs
        else:
            xp_ref, xc_ref, w_ref, dc_ref, dn_ref, dx_ref, dw_ref, db_ref, xs, ds = refs
        i = pl.program_id(1)
        xs[pl.ds(0, H), :] = jnp.where(i > 0, xp_ref[...], 0.0)
        xs[pl.ds(H, tr), :] = xc_ref[...]
        dcur = dc_ref[...]
        dnext = dn_ref[...]
        if silu:
            dcur = dcur * dsilu(pc_ref[...])
            dnext = dnext * dsilu(pn_ref[...])
        ds[pl.ds(0, tr), :] = dcur
        ds[pl.ds(tr, H), :] = jnp.where(i < n_i - 1, dnext, 0.0)
        taps = [w_ref[pl.ds(k, 1), :] for k in range(K)]
        fold = lambda t: jnp.sum(t.reshape(CONV_ROWS // 8, 8, LANES), axis=0)
        dw_parts = [jnp.zeros((8, LANES), f32) for _ in range(K)]
        db_part = jnp.zeros((8, LANES), f32)
        for c0 in range(0, tr, CONV_ROWS):
            acc = jnp.zeros((CONV_ROWS, LANES), f32)
            d_c = ds[pl.ds(c0, CONV_ROWS), :]
            for k in range(K):
                acc = acc + ds[pl.ds(c0 + K - 1 - k, CONV_ROWS), :] * taps[k]
                dw_parts[k] = dw_parts[k] + fold(d_c * xs[pl.ds(c0 + H - (K - 1) + k, CONV_ROWS), :])
            db_part = db_part + fold(d_c)
            dx_ref[pl.ds(c0, CONV_ROWS), :] = acc.astype(dx_ref.dtype)

        @pl.when(i == 0)
        def _():
            dw_ref[...] = jnp.zeros_like(dw_ref)
            db_ref[...] = jnp.zeros_like(db_ref)

        for k in range(K):
            dw_ref[pl.ds(k, 1), :] += jnp.sum(dw_parts[k], 0, keepdims=True)
        db_ref[...] += jnp.sum(db_part, 0, keepdims=True)

    prev = pl.BlockSpec((H, LANES), lambda j, i: (jnp.maximum(i * nh - 1, 0), j))
    cur = pl.BlockSpec((tr, LANES), lambda j, i: (i, j))
    nxt = pl.BlockSpec((H, LANES), lambda j, i: (jnp.minimum((i + 1) * nh, n_i * nh - 1), j))
    in_specs = [prev, cur, pl.BlockSpec((K, LANES), lambda j, i: (0, j)), cur, nxt]
    args = [x, x, w, dact, dact]
    if silu:
        in_specs += [cur, nxt]
        args += [pre, pre]
    dx, dw, db = pl.pallas_call(
        body, name=name, grid=(C // LANES, n_i), in_specs=in_specs,
        out_specs=[cur, pl.BlockSpec((kp, LANES), lambda j, i: (0, j)), pl.BlockSpec((1, LANES), lambda j, i: (0, j))],
        out_shape=[_sds((S, C), dx_dtype), _sds((kp, C)), _sds((1, C))],
        scratch_shapes=[pltpu.VMEM((tr + H, LANES), f32), pltpu.VMEM((tr + H, LANES), f32)],
        compiler_params=_params(("parallel", "arbitrary")),
    )(*args)
    return dx, dw[:K], db


def _dot(a, b):
    return jnp.dot(a.astype(bf16), b.astype(bf16), preferred_element_type=f32)


def _dot_nt(a, b):
    return lax.dot_general(a.astype(bf16), b.astype(bf16), (((1,), (1,)), ((), ())), preferred_element_type=f32)


def _dot_tn(a, b):
    return lax.dot_general(a.astype(bf16), b.astype(bf16), (((0,), (0,)), ((), ())), preferred_element_type=f32)


def _softplus(x):
    return jnp.maximum(x, 0.0) + jnp.log(1.0 + jnp.exp(-jnp.abs(x)))


def _tri(q):
    i = lax.broadcasted_iota(jnp.int32, (q, q), 0)
    j = lax.broadcasted_iota(jnp.int32, (q, q), 1)
    return i >= j


def _ssd_prep(dtraw, dt_bias, a_log):
    q = dtraw.shape[0]
    dt = _softplus(dtraw + dt_bias)
    A = -jnp.exp(a_log)
    tri = _tri(q)
    cs = jnp.dot(tri.astype(f32), dt * A, preferred_element_type=f32, precision=lax.Precision.HIGHEST)
    return dt, A, cs, cs.T, tri


def _expand(cols, h0, n, width):
    q = cols.shape[0]
    return jnp.concatenate([jnp.broadcast_to(cols[:, h0 + r:h0 + r + 1], (q, width)) for r in range(n)], axis=1)


def _ssd_fwd(xbc, dtraw, dt_bias, a_log, d_skip, di, name):
    S, CD = xbc.shape
    Q, N, G = SSM_CHUNK, SSM_STATE, SSM_GROUPS
    nc = S // Q
    nh = di // HEAD_DIM
    R = nh // G
    gw = R * HEAD_DIM

    def body(xbc_ref, dt_ref, bias_ref, alog_ref, dsk_ref, y_ref, hin_ref, state):
        c = pl.program_id(0)

        @pl.when(c == 0)
        def _():
            state[...] = jnp.zeros_like(state)

        hin_ref[...] = state[...]
        dt, A, cs, csT, tri = _ssd_prep(dt_ref[...], bias_ref[...], alog_ref[...])
        dsk = dsk_ref[...]
        ecs = jnp.exp(cs)
        dend = jnp.exp(cs[Q - 1:Q, :] - cs)
        elast = jnp.exp(cs[Q - 1:Q, :])
        for g in range(G):
            h0 = g * R
            Bg = xbc_ref[:, pl.ds(di + g * N, N)]
            Cg = xbc_ref[:, pl.ds(di + G * N + g * N, N)]
            xg = xbc_ref[:, pl.ds(g * gw, gw)]
            Hg = state[pl.ds(g * gw, gw), :]
            Gm = _dot_nt(Cg, Bg)
            xdt = xg * _expand(dt, h0, R, HEAD_DIM)
            yoff = _dot_nt(Cg, Hg) * _expand(ecs, h0, R, HEAD_DIM)
            ys = []
            for r in range(R):
                h = h0 + r
                L = jnp.exp(jnp.where(tri, cs[:, h:h + 1] - csT[h:h + 1, :], -jnp.inf))
                ys.append(_dot(Gm * L, xdt[:, r * HEAD_DIM:(r + 1) * HEAD_DIM]))
            y = jnp.concatenate(ys, axis=1) + yoff + xg * _expand(dsk, h0, R, HEAD_DIM)
            y_ref[:, pl.ds(g * gw, gw)] = y
            hnew = _dot_tn(xdt * _expand(dend, h0, R, HEAD_DIM), Bg)
            escale = jnp.concatenate([jnp.broadcast_to(elast[:, h0 + r:h0 + r + 1], (HEAD_DIM, N)) for r in range(R)], axis=0)
            state[pl.ds(g * gw, gw), :] = escale * Hg + hnew

    vec = pl.BlockSpec((1, LANES), lambda c: (0, 0))
    return pl.pallas_call(
        body, name=name, grid=(nc,),
        in_specs=[pl.BlockSpec((Q, CD), lambda c: (c, 0)), pl.BlockSpec((Q, LANES), lambda c: (c, 0)), vec, vec, vec],
        out_specs=[pl.BlockSpec((Q, di), lambda c: (c, 0)), pl.BlockSpec((None, di, N), lambda c: (c, 0, 0))],
        out_shape=[_sds((S, di)), _sds((nc, di, N))],
        scratch_shapes=[pltpu.VMEM((di, N), f32)],
        compiler_params=_params(("arbitrary",)),
    )(xbc, dtraw, dt_bias, a_log, d_skip)


def _dot_exact(a, b):
    return jnp.dot(a, b, preferred_element_type=f32, precision=lax.Precision.HIGHEST)


def _ssd_bwd(xbc, dtraw, dt_bias, a_log, d_skip, hin, y, dy, di, name):
    S, CD = xbc.shape
    Q, N, G = SSM_CHUNK, SSM_STATE, SSM_GROUPS
    nc = S // Q
    nh = di // HEAD_DIM
    R = nh // G
    gw = R * HEAD_DIM
    P = HEAD_DIM
    head_of_col = jnp.asarray((np.arange(di)[:, None] // P == np.arange(LANES)[None, :]).astype(np.float32))
    dsk_wide = jnp.repeat(d_skip[0, :nh], P)[None]

    def body(xbc_ref, dt_ref, bias_ref, alog_ref, dskw_ref, hoc_ref, hin_ref, y_ref, dy_ref,
             dxbc_ref, ddt_ref, dA_ref, ddsk_ref, dtb_ref, dstate, dxdt_all, tend_all, yoff_all, colterm_all):
        c = pl.program_id(0)

        @pl.when(c == 0)
        def _():
            dstate[...] = jnp.zeros_like(dstate)
            dA_ref[...] = jnp.zeros_like(dA_ref)
            ddsk_ref[...] = jnp.zeros_like(ddsk_ref)
            dtb_ref[...] = jnp.zeros_like(dtb_ref)

        dtraw_v = dt_ref[...]
        dt, A, cs, csT, tri = _ssd_prep(dtraw_v, bias_ref[...], alog_ref[...])
        tri_t = jnp.logical_not(tri) | (lax.broadcasted_iota(jnp.int32, (Q, Q), 0) == lax.broadcasted_iota(jnp.int32, (Q, Q), 1))
        ecs = jnp.exp(cs)
        dend = jnp.exp(cs[Q - 1:Q, :] - cs)
        elast = jnp.exp(cs[Q - 1:Q, :])
        hoc = hoc_ref[...]
        state_dot = jnp.sum(_dot_exact(dstate[...] * hin_ref[...], jnp.ones((N, LANES), f32)) * hoc, 0, keepdims=True) * elast
        for g in range(G):
            h0 = g * R
            Bg = xbc_ref[:, pl.ds(di + g * N, N)]
            Cg = xbc_ref[:, pl.ds(di + G * N + g * N, N)]
            xg = xbc_ref[:, pl.ds(g * gw, gw)]
            dyg = dy_ref[:, pl.ds(g * gw, gw)]
            Hg = hin_ref[pl.ds(g * gw, gw), :]
            dHg = dstate[pl.ds(g * gw, gw), :]
            dt_e = _expand(dt, h0, R, P)
            ecs_e = _expand(ecs, h0, R, P)
            dend_e = _expand(dend, h0, R, P)
            cols = pl.ds(g * gw, gw)
            Gm = _dot_nt(Cg, Bg)
            Gm_t = _dot_nt(Bg, Cg)
            xdt = xg * dt_e
            dye = dyg * ecs_e
            bdh = _dot_nt(Bg, dHg)
            dC = _dot(dye, Hg)
            dB = _dot(xdt * dend_e, dHg)
            dHin = _dot_tn(dye, Cg)
            dxdt_state = dend_e * bdh
            end_term = xdt * dxdt_state
            tend_all[:, cols] = end_term
            yoff_all[:, cols] = _dot_nt(Cg, Hg) * ecs_e
            dG = jnp.zeros((Q, Q), f32)
            dxd = []
            for r in range(R):
                h = h0 + r
                sl = slice(r * P, (r + 1) * P)
                seg = cs[:, h:h + 1] - csT[h:h + 1, :]
                L = jnp.exp(jnp.where(tri, seg, -jnp.inf))
                L_t = jnp.exp(jnp.where(tri_t, -seg, -jnp.inf))
                dyh = dyg[:, sl]
                dG = dG + _dot_nt(dyh, xdt[:, sl]) * L
                dxd.append(_dot(Gm_t * L_t, dyh))
            dxdt_diag = jnp.concatenate(dxd, axis=1)
            dxdt = dxdt_diag + dxdt_state
            dxdt_all[:, cols] = dxdt
            colterm_all[:, cols] = xdt.astype(bf16).astype(f32) * dxdt_diag + end_term
            dxbc_ref[:, cols] = dxdt * dt_e + dyg * dskw_ref[:, cols]
            dxbc_ref[:, pl.ds(di + g * N, N)] = dB + _dot_tn(dG, Cg)
            dxbc_ref[:, pl.ds(di + G * N + g * N, N)] = dC + _dot(dG, Bg)
            escale = jnp.concatenate([jnp.broadcast_to(elast[:, h0 + r:h0 + r + 1], (P, N)) for r in range(R)], axis=0)
            dstate[pl.ds(g * gw, gw), :] = escale * dHg + dHin
        xs = xbc_ref[:, pl.ds(0, di)]
        dyv = dy_ref[...]
        yoff = yoff_all[...]
        y_diag = y_ref[...] - dskw_ref[...] * xs - yoff
        rs_y = _dot_exact(dyv.astype(bf16).astype(f32) * y_diag + dyv * yoff, hoc)
        rs_c = _dot_exact(colterm_all[...], hoc)
        rs_x = _dot_exact(dxdt_all[...] * xs, hoc)
        end_dot = _dot_exact(jnp.broadcast_to(jnp.sum(tend_all[...], 0, keepdims=True), (8, di)), hoc)[0:1]
        last = lax.broadcasted_iota(jnp.int32, (Q, 1), 0) == Q - 1
        dcs = rs_y - rs_c + jnp.where(last, end_dot + state_dot, 0.0)
        da = lax.dot_general(tri.astype(f32), dcs, (((0,), (0,)), ((), ())), preferred_element_type=f32,
                             precision=lax.Precision.HIGHEST)
        ddt = da * A + rs_x
        ddtraw = ddt * jax.nn.sigmoid(dtraw_v + bias_ref[...])
        ddt_ref[...] = ddtraw.astype(ddt_ref.dtype)
        dA_ref[...] += jnp.sum(da * dt, 0, keepdims=True) * A
        ddsk_ref[...] += jnp.sum(_dot_exact(dyv * xs, hoc), 0, keepdims=True)
        dtb_ref[...] += jnp.sum(ddtraw, 0, keepdims=True)

    vec = pl.BlockSpec((1, LANES), lambda c: (0, 0))
    rev = lambda c: (nc - 1 - c, 0)
    return pl.pallas_call(
        body, name=name, grid=(nc,),
        in_specs=[pl.BlockSpec((Q, CD), rev), pl.BlockSpec((Q, LANES), rev), vec, vec,
                  pl.BlockSpec((1, di), lambda c: (0, 0)), pl.BlockSpec((di, LANES), lambda c: (0, 0)),
                  pl.BlockSpec((None, di, N), lambda c: (nc - 1 - c, 0, 0)), pl.BlockSpec((Q, di), rev),
                  pl.BlockSpec((Q, di), rev)],
        out_specs=[pl.BlockSpec((Q, CD), rev), pl.BlockSpec((Q, LANES), rev), vec, vec, vec],
        out_shape=[_sds((S, CD)), _sds((S, LANES), bf16), _sds((1, LANES)), _sds((1, LANES)), _sds((1, LANES))],
        scratch_shapes=[pltpu.VMEM((di, N), f32)] + [pltpu.VMEM((Q, di), f32)] * 4,
        compiler_params=_params(("arbitrary",)),
    )(xbc, dtraw, dt_bias, a_log, dsk_wide, head_of_col, hin, y, dy)


def _t5_bucket_np(dist):
    max_exact = REL_BUCKETS // 2
    n = np.maximum(dist, 1).astype(np.float32)
    large = np.float32(max_exact) + np.log(n / np.float32(max_exact)) / np.float32(math.log(REL_MAX_DIST / max_exact)) * np.float32(REL_BUCKETS - max_exact)
    large = np.minimum(large.astype(np.int32), REL_BUCKETS - 1)
    return np.where(dist < max_exact, dist, large)


def _bucket_onehot():
    i = np.arange(ATT_BLK)[:, None]
    j = np.arange(2 * ATT_BLK)[None, :]
    delta = np.maximum(ATT_BLK + i - j, 0)
    out = np.zeros((len(ATT_DILATIONS), REL_BUCKETS, ATT_BLK * 2 * ATT_BLK), np.float32)
    for gi, d in enumerate(ATT_DILATIONS):
        b = _t5_bucket_np(delta * d).reshape(-1)
        out[gi, b, np.arange(b.size)] = 1.0
    return out


def _exact_mm(a, b, *, name, tb=False):
    M, K = a.shape
    N = b.shape[0] if tb else b.shape[1]
    tn = _tile(N, 4096, LANES)
    dn = (((1,), (1 if tb else 0,)), ((), ()))

    def body(a_ref, b_ref, o_ref):
        o_ref[...] = lax.dot_general(a_ref[...], b_ref[...], dn, preferred_element_type=f32,
                                     precision=lax.Precision.HIGHEST)

    return pl.pallas_call(
        body, name=name, grid=(N // tn,),
        in_specs=[pl.BlockSpec((M, K), lambda j: (0, 0)),
                  pl.BlockSpec((tn, K), lambda j: (j, 0)) if tb else pl.BlockSpec((K, tn), lambda j: (0, j))],
        out_specs=pl.BlockSpec((M, tn), lambda j: (0, j)), out_shape=_sds((M, N)),
        compiler_params=_params(("parallel",)),
    )(a, b)


def _band_penalty():
    i = np.arange(ATT_BLK)[:, None]
    j = np.arange(2 * ATT_BLK)[None, :]
    delta = ATT_BLK + i - j
    return np.where((delta >= 0) & (delta <= ATT_BLK), 0.0, -np.inf).astype(np.float32)


def _first_block_keep(n):
    col = lax.broadcasted_iota(jnp.int32, (ATT_BLK, 2 * ATT_BLK), 1)
    return (col >= ATT_BLK) | (n > 0)


ATT_SCALE = HEAD_DIM ** -0.5


def _rows(ref, r, d):
    return ref[...] if d == 1 else ref[pl.ds(r, ATT_BLK, stride=d), :]


def _set_rows(ref, r, d, val):
    if d == 1:
        ref[...] = val
    else:
        ref[pl.ds(r, ATT_BLK, stride=d), :] = val


def _attn_width(d, D):
    return D if d == 1 else LANES


def _over_residues(d, one, unroll=1):
    if d == 1:
        one(0)
    else:
        lax.fori_loop(0, d, lambda r, c: (one(r), c)[1], 0, unroll=unroll)


def _attn_fwd(q, k, v, bias, d, name):
    S, D = q.shape
    nb = S // (d * ATT_BLK)
    W = _attn_width(d, D)
    HB = W // HEAD_DIM

    def body(q_ref, kp_ref, kc_ref, vp_ref, vc_ref, b_ref, o_ref, lse_ref):
        keep = _first_block_keep(pl.program_id(1))

        def one(r):
            qs = (_rows(q_ref, r, d) * ATT_SCALE).astype(bf16)
            kcat = jnp.concatenate([_rows(kp_ref, r, d), _rows(kc_ref, r, d)], axis=0).astype(bf16)
            vcat = jnp.concatenate([_rows(vp_ref, r, d), _rows(vc_ref, r, d)], axis=0).astype(bf16)
            outs, lses = [], []
            for h in range(HB):
                sl = slice(h * HEAD_DIM, (h + 1) * HEAD_DIM)
                s = jnp.where(keep, _dot_nt(qs[:, sl], kcat[:, sl]) + b_ref[h], -jnp.inf)
                m = jnp.max(s, -1, keepdims=True)
                p = jnp.exp(s - m)
                l = jnp.sum(p, -1, keepdims=True)
                outs.append(_dot(p, vcat[:, sl]) / l)
                lses.append(jnp.broadcast_to(m + jnp.log(l), (ATT_BLK, HEAD_DIM)))
            _set_rows(o_ref, r, d, jnp.concatenate(outs, axis=1))
            _set_rows(lse_ref, r, d, jnp.concatenate(lses, axis=1))

        _over_residues(d, one, unroll=4)

    cur = pl.BlockSpec((ATT_BLK * d, W), lambda j, n: (n, j))
    prev = pl.BlockSpec((ATT_BLK * d, W), lambda j, n: (jnp.maximum(n - 1, 0), j))
    return pl.pallas_call(
        body, name=name, grid=(D // W, nb),
        in_specs=[cur, prev, cur, prev, cur, pl.BlockSpec((HB, ATT_BLK, 2 * ATT_BLK), lambda j, n: (j, 0, 0))],
        out_specs=[cur, cur], out_shape=[_sds((S, D)), _sds((S, D))],
        compiler_params=_params(("parallel", "arbitrary")),
    )(q, k, k, v, v, bias)


def _attn_bwd(q, k, v, bias, att, datt, lse_tot, d, name):
    S, D = q.shape
    nb = S // (d * ATT_BLK)
    H = D // HEAD_DIM
    W = _attn_width(d, D)
    HB = W // HEAD_DIM

    def body(q_ref, kp_ref, kc_ref, vp_ref, vc_ref, b_ref, o_ref, do_ref, lse_ref,
             dq_ref, dk_ref, dv_ref, db_ref, carry_k, carry_v):
        n = pl.program_id(1)

        @pl.when(n == 0)
        def _():
            carry_k[...] = jnp.zeros_like(carry_k)
            carry_v[...] = jnp.zeros_like(carry_v)
            db_ref[...] = jnp.zeros_like(db_ref)

        @pl.when(n < nb)
        def _():
            keep = _first_block_keep(n)

            def one(r):
                qs = (_rows(q_ref, r, d) * ATT_SCALE).astype(bf16)
                kcat = jnp.concatenate([_rows(kp_ref, r, d), _rows(kc_ref, r, d)], axis=0).astype(bf16)
                vcat = jnp.concatenate([_rows(vp_ref, r, d), _rows(vc_ref, r, d)], axis=0).astype(bf16)
                dov, lsev = _rows(do_ref, r, d), _rows(lse_ref, r, d)
                dsum_all = dov * _rows(o_ref, r, d)
                dob = dov.astype(bf16)
                dqs, dks, dvs = [], [], []
                for h in range(HB):
                    sl = slice(h * HEAD_DIM, (h + 1) * HEAD_DIM)
                    s = jnp.where(keep, _dot_nt(qs[:, sl], kcat[:, sl]) + b_ref[h], -jnp.inf)
                    p = jnp.exp(s - lsev[:, h * HEAD_DIM:h * HEAD_DIM + 1])
                    dp = _dot_nt(dob[:, sl], vcat[:, sl])
                    ds = p * (dp - jnp.sum(dsum_all[:, sl], 1, keepdims=True))
                    db_ref[h] += ds
                    dqs.append(_dot(ds, kcat[:, sl]) * ATT_SCALE)
                    dks.append(_dot_tn(ds, qs[:, sl]))
                    dvs.append(_dot_tn(p, dob[:, sl]))
                _set_rows(dq_ref, r, d, jnp.concatenate(dqs, axis=1))
                dk = jnp.concatenate(dks, axis=1)
                dv = jnp.concatenate(dvs, axis=1)
                _set_rows(dk_ref, r, d, carry_k[r] + dk[:ATT_BLK])
                _set_rows(dv_ref, r, d, carry_v[r] + dv[:ATT_BLK])
                carry_k[r] = dk[ATT_BLK:]
                carry_v[r] = dv[ATT_BLK:]

            _over_residues(d, one, unroll=2)

        @pl.when(n == nb)
        def _():
            def last(r):
                _set_rows(dk_ref, r, d, carry_k[r])
                _set_rows(dv_ref, r, d, carry_v[r])

            _over_residues(d, last)

    nq = lambda n: jnp.minimum(n, nb - 1)
    cur = pl.BlockSpec((ATT_BLK * d, W), lambda j, n: (nq(n), j))
    prev = pl.BlockSpec((ATT_BLK * d, W), lambda j, n: (jnp.maximum(nq(n) - 1, 0), j))
    done = pl.BlockSpec((ATT_BLK * d, W), lambda j, n: (jnp.maximum(n - 1, 0), j))
    bspec = pl.BlockSpec((HB, ATT_BLK, 2 * ATT_BLK), lambda j, n: (j, 0, 0))
    return pl.pallas_call(
        body, name=name, grid=(D // W, nb + 1),
        in_specs=[cur, prev, cur, prev, cur, bspec, cur, cur, cur],
        out_specs=[cur, done, done, bspec],
        out_shape=[_sds((S, D)), _sds((S, D)), _sds((S, D)), _sds((H, ATT_BLK, 2 * ATT_BLK))],
        scratch_shapes=[pltpu.VMEM((d, ATT_BLK, W), f32), pltpu.VMEM((d, ATT_BLK, W), f32)],
        compiler_params=_params(("arbitrary", "arbitrary")),
    )(q, k, k, v, v, bias, att, datt, lse_tot)


def _attn_combine(os_, lses, name):
    def fn(rv, vv):
        o0, o1, o2, l0, l1, l2 = rv
        m = jnp.maximum(jnp.maximum(l0, l1), l2)
        e0, e1, e2 = jnp.exp(l0 - m), jnp.exp(l1 - m), jnp.exp(l2 - m)
        tot = e0 + e1 + e2
        att = (e0 * o0 + e1 * o1 + e2 * o2) / tot
        return [att, att, m + jnp.log(tot)], []
    w = os_[0].shape[1]
    (att, att_b, lse), _ = _rowwise(name, fn, list(os_) + list(lses), [], [(w, f32), (w, bf16), (w, f32)], [], sub=16)
    return att, att_b, lse


ANY = pl.BlockSpec(memory_space=pl.ANY)


def _all_gather(vs, name):
    n = len(vs)

    def body(*refs):
        x_refs, out_refs = refs[:n], refs[n:2 * n]
        send_sems, recv_sems, local_sems = refs[2 * n:]
        x, y, c = lax.axis_index("x"), lax.axis_index("y"), lax.axis_index("c")
        me, sibling = (x, y, c), (x, y, 1 - c)
        chips = [(1 - x, y), (x, 1 - y), (1 - x, 1 - y)]

        def slot(i, px, py, pc):
            return out_refs[i].at[4 * px + 2 * py + pc]

        def copy(i, k, block, to, src=None):
            return pltpu.make_async_remote_copy(
                src_ref=slot(i, *block) if src is None else src, dst_ref=slot(i, *block),
                send_sem=send_sems.at[i, k], recv_sem=recv_sems.at[i, k], device_id=to, device_id_type=MESH)

        mine = [pltpu.make_async_copy(x_refs[i], slot(i, *me), local_sems.at[i]) for i in range(n)]
        for cp in mine:
            cp.start()
        first = []
        for i in range(n):
            first.append(copy(i, 0, me, sibling, src=x_refs[i]))
            first += [copy(i, 1 + j, me, (*chip, c), src=x_refs[i]) for j, chip in enumerate(chips)]
        for cp in first:
            cp.start()
        passed = []
        for i in range(n):
            for j, chip in enumerate(chips):
                copy(i, 1 + j, (*chip, c), me).wait_recv()
                cp = copy(i, 4 + j, (*chip, c), sibling)
                cp.start()
                passed.append(cp)
        for i in range(n):
            copy(i, 0, sibling, me).wait_recv()
            for j, chip in enumerate(chips):
                copy(i, 4 + j, (*chip, 1 - c), me).wait_recv()
        for cp in first + passed:
            cp.wait_send()
        for cp in mine:
            cp.wait()

    return pl.pallas_call(
        body, name=name, out_shape=[_sds((N_DEV,) + v.shape, v.dtype) for v in vs], in_specs=[ANY] * n,
        out_specs=[ANY] * n,
        scratch_shapes=[pltpu.SemaphoreType.DMA((n, 7)), pltpu.SemaphoreType.DMA((n, 7)), pltpu.SemaphoreType.DMA((n,))],
    )(*vs)


def _rs_sibling(parts, name):
    n = len(parts)

    def body(*refs):
        p_refs, out_refs = refs[:n], refs[n:2 * n]
        send_sems, recv_sems = refs[2 * n:]
        x, y, c = lax.axis_index("x"), lax.axis_index("y"), lax.axis_index("c")
        cps = [pltpu.make_async_remote_copy(
            src_ref=p_refs[i].at[k, 1 - c], dst_ref=out_refs[i].at[k], send_sem=send_sems.at[i, k],
            recv_sem=recv_sems.at[i, k], device_id=(x, y, 1 - c), device_id_type=MESH)
            for i in range(n) for k in range(4)]
        for cp in cps:
            cp.start()
        for cp in cps:
            cp.wait()

    return pl.pallas_call(
        body, name=name, out_shape=[_sds((4,) + p.shape[2:], p.dtype) for p in parts], in_specs=[ANY] * n,
        out_specs=[ANY] * n,
        scratch_shapes=[pltpu.SemaphoreType.DMA((n, 4)), pltpu.SemaphoreType.DMA((n, 4))],
    )(*parts)


def _rs_chips(ts, name):
    n = len(ts)

    def body(*refs):
        t_refs, out_refs = refs[:n], refs[n:2 * n]
        send_sems, recv_sems, local_sems = refs[2 * n:]
        x, y, c = lax.axis_index("x"), lax.axis_index("y"), lax.axis_index("c")
        mine = 2 * x + y
        local = [pltpu.make_async_copy(t_refs[i].at[mine], out_refs[i].at[mine], local_sems.at[i]) for i in range(n)]
        for cp in local:
            cp.start()
        chips = [(1 - x, y), (x, 1 - y), (1 - x, 1 - y)]
        cps = [pltpu.make_async_remote_copy(
            src_ref=t_refs[i].at[2 * px + py], dst_ref=out_refs[i].at[mine], send_sem=send_sems.at[i, j],
            recv_sem=recv_sems.at[i, j], device_id=(px, py, c), device_id_type=MESH)
            for i in range(n) for j, (px, py) in enumerate(chips)]
        for cp in cps:
            cp.start()
        for cp in cps:
            cp.wait()
        for cp in local:
            cp.wait()

    return pl.pallas_call(
        body, name=name, out_shape=[_sds(t.shape, t.dtype) for t in ts], in_specs=[ANY] * n, out_specs=[ANY] * n,
        scratch_shapes=[pltpu.SemaphoreType.DMA((n, 3)), pltpu.SemaphoreType.DMA((n, 3)), pltpu.SemaphoreType.DMA((n,))],
    )(*ts)


def _pair_add(part, recv, c_arr, name):
    _, _, R, C = part.shape
    tr = _tile(R, PACK_ROW_TILE, 16)

    def body(c_ref, p_ref, r_ref, o_ref):
        o_ref[...] = (p_ref[...] + r_ref[...]).astype(o_ref.dtype)

    return pl.pallas_call(
        body, name=name,
        grid_spec=pltpu.PrefetchScalarGridSpec(
            num_scalar_prefetch=1, grid=(4, R // tr),
            in_specs=[pl.BlockSpec((None, None, tr, C), lambda k, i, c_ref: (k, c_ref[0], i, 0)),
                      pl.BlockSpec((None, tr, C), lambda k, i, c_ref: (k, i, 0))],
            out_specs=pl.BlockSpec((None, tr, C), lambda k, i, c_ref: (k, i, 0))),
        out_shape=_sds((4, R, C), bf16),
        compiler_params=_params(("parallel", "parallel")),
    )(c_arr, part, recv)


def _sum_slots(t, name):
    n, R, C = t.shape
    tr = _tile(R, PACK_ROW_TILE, 16)

    def body(t_ref, o_ref):
        acc = t_ref[0].astype(f32)
        for k in range(1, n):
            acc = acc + t_ref[k].astype(f32)
        o_ref[...] = acc

    return pl.pallas_call(
        body, name=name, grid=(R // tr,),
        in_specs=[pl.BlockSpec((n, tr, C), lambda i: (0, i, 0))],
        out_specs=pl.BlockSpec((tr, C), lambda i: (i, 0)), out_shape=_sds((R, C)),
        compiler_params=_params(("parallel",)),
    )(t)


def _reduce_scatter(parts, c_arr, name):
    parts4 = [p.reshape((4, 2) + p.shape[1:]) for p in parts]
    recv = _rs_sibling(parts4, name + "_sibling")
    ts = [_pair_add(p, r, c_arr, f"{name}_pair_{i}") for i, (p, r) in enumerate(zip(parts4, recv))]
    got = _rs_chips(ts, name + "_chips")
    return [_sum_slots(g, f"{name}_sum_{i}") for i, g in enumerate(got)]


HBM_SPEC = pl.BlockSpec(memory_space=pltpu.HBM)
SEM_SPEC = pl.BlockSpec(memory_space=pltpu.SEMAPHORE)
EFFECT = pltpu.SideEffectType.DATAFLOW_SIDE_EFFECTING


def _mesh_pos(p):
    return (p // 4, (p // 2) % 2, p % 2)


def _exchange_copy(src_refs, land_refs, send_sems, recv_sems, whole, i, k, receiving):
    me = 4 * lax.axis_index("x") + 2 * lax.axis_index("y") + lax.axis_index("c")
    to = (me + k) % N_DEV
    frm = (me + N_DEV - k) % N_DEV
    src = src_refs[i] if whole else src_refs[i].at[to]
    s = i * (N_DEV - 1) + k - 1
    send = pltpu.make_async_remote_copy(src_ref=src, dst_ref=land_refs[i].at[me], send_sem=send_sems.at[s],
                                        recv_sem=recv_sems.at[s], device_id=_mesh_pos(to), device_id_type=MESH)
    if not receiving:
        return send
    return send, pltpu.make_async_remote_copy(src_ref=src, dst_ref=land_refs[i].at[frm], send_sem=send_sems.at[s],
                                              recv_sem=recv_sems.at[s], device_id=_mesh_pos(to), device_id_type=MESH)


def _exchange_start(srcs, whole, name, after=None):
    n = len(srcs)
    lands = [lax.empty((N_DEV,) + s.shape[-2:], s.dtype) for s in srcs]
    after = list(after or [])
    n_in = 2 * n + len(after)

    def body(*refs):
        src_refs, land_refs = refs[:n], refs[n:2 * n]
        send_sems, recv_sems, token = refs[n_in], refs[n_in + 1], refs[-1]
        for i in range(n):
            for k in range(1, N_DEV):
                _exchange_copy(src_refs, land_refs, send_sems, recv_sems, whole, i, k, False).start()
        token[...] = jnp.zeros_like(token)

    sems = pltpu.SemaphoreType.DMA((n * (N_DEV - 1),))
    outs = pl.pallas_call(
        body, name=name,
        out_shape=(sems, sems, *[pltpu.HBM(a.shape, a.dtype) for a in srcs + lands], _sds((8, LANES))),
        in_specs=[HBM_SPEC] * (2 * n) + [pl.BlockSpec(memory_space=pl.ANY)] * len(after),
        out_specs=(SEM_SPEC, SEM_SPEC, *[HBM_SPEC] * (2 * n), pl.BlockSpec(memory_space=pltpu.VMEM)),
        input_output_aliases={i: 2 + i for i in range(2 * n)},
        compiler_params=pltpu.CompilerParams(has_side_effects=EFFECT),
    )(*[pltpu.with_memory_space_constraint(a, pltpu.HBM) for a in srcs + lands], *after)
    return (outs[0], outs[1], list(outs[2:2 + n]), list(outs[2 + n:2 + 2 * n]), whole), outs[-1]


def _exchange_wait(handle, after, name):
    send_sems, recv_sems, srcs, lands, whole = handle
    n = len(srcs)

    def body(*refs):
        src_refs, land_refs = refs[:n], refs[n:2 * n]
        send_sems, recv_sems = refs[2 * n], refs[2 * n + 1]
        for i in range(n):
            for k in range(1, N_DEV):
                send, recv = _exchange_copy(src_refs, land_refs, send_sems, recv_sems, whole, i, k, True)
                send.wait_send()
                recv.wait_recv()

    outs = pl.pallas_call(
        body, name=name, out_shape=tuple(pltpu.HBM(a.shape, a.dtype) for a in srcs + lands),
        in_specs=[HBM_SPEC] * (2 * n) + [SEM_SPEC, SEM_SPEC, pl.BlockSpec(memory_space=pl.ANY)],
        out_specs=[HBM_SPEC] * (2 * n), input_output_aliases={i: i for i in range(2 * n)},
        compiler_params=pltpu.CompilerParams(has_side_effects=EFFECT),
    )(*srcs, *lands, send_sems, recv_sems, after)
    return list(outs[n:])


def _tie(v, token):
    return v + token[0:1, 0:1].astype(v.dtype).reshape((1,) * v.ndim)


def _with_own(land, own, me):
    return lax.dynamic_update_slice_in_dim(land, own[None].astype(land.dtype), me, 0)


class _Overlap:
    def __init__(self, shards, me, after):
        self.me = me
        self.names = list(shards)
        self.handle, self.token = _exchange_start([shards[nm] for nm in self.names], True, "weights_start", after)
        self.sent = {}

    def weights(self, after):
        lands = _exchange_wait(self.handle, after, "weights_wait")
        own = self.handle[2]
        return {nm: _full_from_blocks(nm, _with_own(land, o, self.me)) for nm, land, o in zip(self.names, lands, own)}

    def send(self, tag, grads, after=None):
        names = list(grads)
        handle, token = _exchange_start([_blocks_from_full(nm, grads[nm]) for nm in names], False, f"grads_start_{tag}",
                                        after)
        self.sent[tag] = (names, handle)
        return token

    def received(self, tag, after):
        names, handle = self.sent[tag]
        lands = _exchange_wait(handle, after, f"grads_wait_{tag}")
        own = [lax.dynamic_index_in_dim(b, self.me, 0, keepdims=False) for b in handle[2]]
        return {nm: _with_own(land, o, self.me) for nm, land, o in zip(names, lands, own)}


ADAM_ROWS = 32


def _adamw(w, g, m, v, name):
    R, C = w.shape
    cb = LANES if C % LANES == 0 else C

    def body(w_ref, g_ref, m_ref, v_ref, d_ref, m2_ref, v2_ref):
        def update(sl):
            gv = g_ref[sl, :]
            m2 = ADAM_B1 * m_ref[sl, :] + (1.0 - ADAM_B1) * gv
            v2 = ADAM_B2 * v_ref[sl, :] + (1.0 - ADAM_B2) * jnp.square(gv)
            m_hat = m2 / (1.0 - ADAM_B1 ** ADAM_STEP)
            v_hat = v2 / (1.0 - ADAM_B2 ** ADAM_STEP)
            d_ref[sl, :] = -ADAM_LR * (m_hat / (jnp.sqrt(v_hat) + ADAM_EPS) + ADAM_WD * w_ref[sl, :])
            m2_ref[sl, :] = m2
            v2_ref[sl, :] = v2

        main = R // ADAM_ROWS
        if main:
            lax.fori_loop(0, main, lambda i, c: (update(pl.ds(pl.multiple_of(i * ADAM_ROWS, ADAM_ROWS), ADAM_ROWS)), c)[1], 0)
        if R % ADAM_ROWS:
            update(pl.ds(main * ADAM_ROWS, R % ADAM_ROWS))

    spec = pl.BlockSpec((R, cb), lambda j: (0, j))
    return pl.pallas_call(
        body, name=name, grid=(C // cb,), in_specs=[spec] * 4, out_specs=[spec] * 3, out_shape=[_sds((R, C))] * 3,
        compiler_params=_params(("parallel",)),
    )(w, g, m, v)


BIG_PARAMS = ("hy_w_in", "hy_w_out", "cv_w_pw1", "cv_w_pw2", "ffn_w_gate", "ffn_w_up", "ffn_w_down")


def _shards_2d(w):
    t = lambda a: jnp.transpose(a)
    return dict(in_t=t(w["hy_w_in"][0]), out=w["hy_w_out"][0], pw1=w["cv_w_pw1"][0], pw2=w["cv_w_pw2"][0],
                gate_t0=t(w["ffn_w_gate"][0]), gate_t1=t(w["ffn_w_gate"][1]), up_t0=t(w["ffn_w_up"][0]),
                up_t1=t(w["ffn_w_up"][1]), down0=w["ffn_w_down"][0], down1=w["ffn_w_down"][1])


def _unshard_2d(s):
    t = lambda a: jnp.transpose(a)
    return dict(hy_w_in=t(s["in_t"])[None], hy_w_out=s["out"][None], cv_w_pw1=s["pw1"][None], cv_w_pw2=s["pw2"][None],
                ffn_w_gate=jnp.stack([t(s["gate_t0"]), t(s["gate_t1"])]),
                ffn_w_up=jnp.stack([t(s["up_t0"]), t(s["up_t1"])]), ffn_w_down=jnp.stack([s["down0"], s["down1"]]))


def _full_from_blocks(nm, g):
    if nm == "pw1":
        return jnp.transpose(g, (1, 0, 2)).reshape(g.shape[1], N_DEV * g.shape[2])
    return g.reshape(N_DEV * g.shape[1], g.shape[2])


def _blocks_from_full(nm, g):
    if nm == "pw1":
        return jnp.transpose(g.reshape(g.shape[0], N_DEV, g.shape[1] // N_DEV), (1, 0, 2))
    return g.reshape(N_DEV, g.shape[0] // N_DEV, g.shape[1])


class _VecPack:
    def __init__(self, shapes):
        self.shapes = [tuple(s) for s in shapes]
        self.sizes = [int(np.prod(s)) for s in self.shapes]
        total = sum(self.sizes)
        self.rows = -(-(-(-total // LANES)) // 8) * 8
        self.total = total

    def pack(self, arrays):
        flat = jnp.concatenate([a.astype(f32).reshape(-1) for a in arrays])
        flat = jnp.pad(flat, (0, self.rows * LANES - self.total))
        return flat.reshape(self.rows, LANES)

    def unpack(self, packed):
        flat = packed.reshape(-1)
        out, off = [], 0
        for shp, n in zip(self.shapes, self.sizes):
            out.append(flat[off:off + n].reshape(shp))
            off += n
        return out

    def unpack_stacked(self, stacked, only=None):
        flat = stacked.reshape(stacked.shape[0], -1)
        offs = np.concatenate([[0], np.cumsum(self.sizes)])
        get = lambda i: flat[:, offs[i]:offs[i + 1]].reshape((stacked.shape[0],) + self.shapes[i])
        return get(only) if only is not None else [get(i) for i in range(len(self.shapes))]


def _row(v):
    return v.reshape(1, -1)


def _pad_lanes(v):
    v = v.reshape(1, -1)
    return jnp.pad(v, ((0, 0), (0, LANES - v.shape[1])))


def _ffn_fwd(h, w_gate_t, w_up_t, w_down, tag):
    F = w_down.shape[0]
    a = _mm(h, w_gate_t, tb=True, name=f"ffn_gate_{tag}")
    u = _mm(h, w_up_t, tb=True, name=f"ffn_up_{tag}")
    (f,), _ = _rowwise(f"swiglu_{tag}", lambda rv, vv: ([_silu(rv[0]) * rv[1]], []), [a, u], [], [(F, bf16)], [], sub=16)
    out = _mm(f, w_down, name=f"ffn_down_{tag}")
    return out, (a, u, f)


def _ffn_bwd(h, w_gate_t, w_up_t, w_down, saved, dout, tag):
    a, u, f = saved
    F = w_down.shape[0]
    df = _mm(dout, w_down, tb=True, name=f"ffn_down_dx_{tag}")
    dw_down = _mm(f, dout, ta=True, out_dtype=bf16, name=f"ffn_down_dw_{tag}")

    def fn(rv, vv):
        _, vjp = jax.vjp(lambda a_, u_: _silu(a_) * u_, rv[0], rv[1])
        da, du = vjp(rv[2])
        return [da, du], []

    (da, du), _ = _rowwise(f"swiglu_bwd_{tag}", fn, [a, u, df], [], [(F, bf16), (F, bf16)], [], sub=16)
    dh = _mm(du, w_up_t, add=_mm(da, w_gate_t, name=f"ffn_gate_dx_{tag}"), name=f"ffn_up_dx_{tag}")
    dw_gate_t = _mm(da, h, ta=True, out_dtype=bf16, name=f"ffn_gate_dw_{tag}")
    dw_up_t = _mm(du, h, ta=True, out_dtype=bf16, name=f"ffn_up_dw_{tag}")
    return dh, dw_gate_t, dw_up_t, dw_down


def _local_step(x, target, mod, w_in_t, comm, small):
    S, D = x.shape
    di = small["hy_ssm_norm_g"].shape[-1]
    nh = small["hy_dt_bias"].shape[-1]
    cd = small["hy_conv_b"].shape[-1]
    m = [[_row(mod[i, j]) for j in range(6)] for i in range(2)]

    off_q = di + cd + nh
    w_qkv_t = w_in_t[off_q:]
    seg = dict(z=(w_in_t, 0, di), xbc=(w_in_t, di, cd), dt=(w_in_t, di + cd, LANES))
    for i, nm in enumerate(("q0", "q1", "q2", "k", "v")):
        seg[nm] = (w_qkv_t, i * D, D)

    g_mix = [_row(small["norm_mix_g"][i]) for i in range(2)]
    g_ffn = [_row(small["norm_ffn_g"][i]) for i in range(2)]
    conv_w, conv_b = small["hy_conv_w_full"], _row(small["hy_conv_b"][0])
    dt_bias, a_log, d_skip = (_pad_lanes(small[k][0]) for k in ("hy_dt_bias", "hy_a_log", "hy_d_skip"))
    g_ssm = _row(small["hy_ssm_norm_g"][0])
    onehot = jnp.asarray(_bucket_onehot())
    rel_t = small["rel_table"].T
    H = D // HEAD_DIM
    bias = [_exact_mm(rel_t[gi * H:(gi + 1) * H], onehot[gi], name=f"rel_bias_{gi}")
            .reshape(H, ATT_BLK, 2 * ATT_BLK) + _band_penalty() for gi in range(3)]

    h1 = _adaln_fwd(x, g_mix[0], m[0][1], m[0][0], "adaln_mix0")
    proj = {nm: _mm(h1, mat, tb=True, b_rows=(off, cnt), name=f"in_{nm}") for nm, (mat, off, cnt) in seg.items()}
    xbc_pre, xbc = _conv_fwd(proj["xbc"], conv_w, conv_b, silu=True, name="ssm_conv", tr=1024)
    y, hin = _ssd_fwd(xbc, proj["dt"], dt_bias, a_log, d_skip, di, "ssd_fwd")
    (yg,), _ = _rowwise("ssm_gate", lambda rv, vv: ([_gate_f(rv[0], rv[1], vv[0])], []),
                        [y, proj["z"]], [g_ssm], [(di, bf16)], [], sub=16)
    og = [_attn_fwd(proj[f"q{gi}"], proj["k"], proj["v"], bias[gi], d, f"attn_fwd_{gi}")
          for gi, d in enumerate(ATT_DILATIONS)]
    att, att_b, lse_tot = _attn_combine([a for a, _ in og], [b for _, b in og], "attn_combine")
    W = comm.weights(after=att_b)
    w_out_y, w_out_a = W["out"][:di], W["out"][di:]
    mix0 = _mm(att_b, w_out_a, add=_mm(yg, w_out_y, name="out_y"), name="out_a")
    x1 = _resid_fwd(x, m[0][2], mix0, "resid_mix0")
    h2 = _adaln_fwd(x1, g_ffn[0], m[0][4], m[0][3], "adaln_ffn0")
    f0, ffn0_saved = _ffn_fwd(h2, W["gate_t0"], W["up_t0"], W["down0"], "0")
    x2 = _resid_fwd(x1, m[0][5], f0, "resid_ffn0")

    h3 = _adaln_fwd(x2, g_mix[1], m[1][1], m[1][0], "adaln_mix1")
    pw1 = _mm(h3, W["pw1"], bias=_row(small["cv_b_pw1_full"]), name="cv_pw1")
    (u,), _ = _rowwise("cv_glu", lambda rv, vv: ([rv[0] * jax.nn.sigmoid(rv[1])], []),
                       [(pw1, 0, D), (pw1, 1, D)], [], [(D, f32)], [])
    (u2,) = _conv_fwd(u, small["cv_w_dw_full"], _row(small["cv_b_dw_full"]), silu=False, name="cv_dw")
    ln_g, ln_b = _row(small["cv_ln_g_full"]), _row(small["cv_ln_b_full"])
    (u3,), _ = _rowwise("cv_lnsilu", lambda rv, vv: ([_lnsilu_f(rv[0], vv[0], vv[1])], []),
                        [u2], [ln_g, ln_b], [(D, bf16)], [], sub=16)
    mix1 = _mm(u3, W["pw2"], bias=_row(small["cv_b_pw2_full"]), name="cv_pw2")
    x3 = _resid_fwd(x2, m[1][2], mix1, "resid_mix1")
    h4 = _adaln_fwd(x3, g_ffn[1], m[1][4], m[1][3], "adaln_ffn1")
    f1, ffn1_saved = _ffn_fwd(h4, W["gate_t1"], W["up_t1"], W["down1"], "1")
    x4 = _resid_fwd(x3, m[1][5], f1, "resid_ffn1")

    g_fin = _row(small["final_norm_g"])

    def final_fn(rv, vv):
        xv, tv = rv
        yv, vjp = jax.vjp(_rms, xv, vv[0])
        err = yv - tv
        dx, dg = vjp(err / D)
        part = 0.5 * jnp.sum(jnp.mean(err * err, -1, keepdims=True), 0, keepdims=True)
        return [dx], [dg, jnp.broadcast_to(part, (1, LANES))]

    (dx4,), (d_fin, loss) = _rowwise("loss_head", final_fn, [x4, target], [g_fin], [(D, f32)], [D, LANES])

    dmod = [[None] * 6 for _ in range(2)]
    d_norm_mix, d_norm_ffn = [None, None], [None, None]
    big = {}

    df1, (dmod[1][5], _) = _resid_bwd(dx4, f1, m[1][5], "resid_ffn1_bwd")
    dh4, big["gate_t1"], big["up_t1"], big["down1"] = _ffn_bwd(h4, W["gate_t1"], W["up_t1"], W["down1"], ffn1_saved, df1, "1")
    dx3, (d_norm_ffn[1], dmod[1][4], dmod[1][3]) = _adaln_bwd(x3, g_ffn[1], m[1][4], m[1][3], dh4, dx4, "adaln_ffn1_bwd")
    dmix1, (dmod[1][2], d_b_pw2) = _resid_bwd(dx3, mix1, m[1][2], "resid_mix1_bwd")
    du3 = _mm(dmix1, W["pw2"], tb=True, name="cv_pw2_dx")
    big["pw2"] = _mm(u3, dmix1, ta=True, out_dtype=bf16, name="cv_pw2_dw")

    def lnsilu_bwd(rv, vv):
        _, vjp = jax.vjp(_lnsilu_f, rv[0], vv[0], vv[1])
        du, dg, db = vjp(rv[1])
        return [du], [dg, db]

    (du2,), (d_ln_g, d_ln_b) = _rowwise("cv_lnsilu_bwd", lnsilu_bwd, [u2, du3], [ln_g, ln_b], [(D, f32)], [D, D])
    du, d_w_dw, d_b_dw = _conv_bwd(u, small["cv_w_dw_full"], du2, None, silu=False, name="cv_dw_bwd")

    def glu_bwd(rv, vv):
        a, gt, d = rv
        _, vjp = jax.vjp(lambda a_, g_: a_ * jax.nn.sigmoid(g_), a, gt)
        da, dg = vjp(d)
        return [da, dg], [jnp.sum(da, 0, keepdims=True), jnp.sum(dg, 0, keepdims=True)]

    (dpa, dpg), (d_b1a, d_b1g) = _rowwise("cv_glu_bwd", glu_bwd, [(pw1, 0, D), (pw1, 1, D), du], [],
                                           [(D, bf16), (D, bf16)], [D, D], sub=16)
    dpw1 = jnp.concatenate([dpa, dpg], axis=1)
    d_b_pw1 = jnp.concatenate([d_b1a, d_b1g], axis=1)
    dh3 = _mm(dpw1, W["pw1"], tb=True, name="cv_pw1_dx")
    big["pw1"] = _mm(h3, dpw1, ta=True, out_dtype=bf16, name="cv_pw1_dw")
    token = comm.send("layer1", {nm: big[nm] for nm in ("gate_t1", "up_t1", "down1", "pw2", "pw1")})
    dx2, (d_norm_mix[1], dmod[1][1], dmod[1][0]) = _adaln_bwd(x2, g_mix[1], m[1][1], _tie(m[1][0], token), dh3, dx3,
                                                              "adaln_mix1_bwd")

    df0, (dmod[0][5], _) = _resid_bwd(dx2, f0, m[0][5], "resid_ffn0_bwd")
    dh2, big["gate_t0"], big["up_t0"], big["down0"] = _ffn_bwd(h2, W["gate_t0"], W["up_t0"], W["down0"], ffn0_saved, df0, "0")
    dx1, (d_norm_ffn[0], dmod[0][4], dmod[0][3]) = _adaln_bwd(x1, g_ffn[0], m[0][4], m[0][3], dh2, dx2, "adaln_ffn0_bwd")
    dmix0, (dmod[0][2], _) = _resid_bwd(dx1, mix0, m[0][2], "resid_mix0_bwd")
    dyg = _mm(dmix0, w_out_y, tb=True, name="out_y_dx")
    datt = _mm(dmix0, w_out_a, tb=True, name="out_a_dx")
    big["out"] = jnp.concatenate([_mm(yg, dmix0, ta=True, out_dtype=bf16, name="out_y_dw"),
                                  _mm(att_b, dmix0, ta=True, out_dtype=bf16, name="out_a_dw")], axis=0)
    token = comm.send("layer0", {nm: big[nm] for nm in ("gate_t0", "up_t0", "down0", "out")})
    bias = [_tie(b, token) for b in bias]
    g_ssm = _tie(g_ssm, token)

    dq, dks, dvs, dbs = [], [], [], []
    for gi, d in enumerate(ATT_DILATIONS):
        a, b, c_, e = _attn_bwd(proj[f"q{gi}"], proj["k"], proj["v"], bias[gi], att, datt, lse_tot, d, f"attn_bwd_{gi}")
        dq.append(a)
        dks.append(b)
        dvs.append(c_)
        dbs.append(e)
    dk = _add3(*dks, "attn_dk")
    dv = _add3(*dvs, "attn_dv")
    d_rel = jnp.concatenate(
        [_exact_mm(dbs[gi].reshape(H, -1), onehot[gi], tb=True, name=f"rel_grad_{gi}") for gi in range(3)], axis=0).T

    def gate_bwd(rv, vv):
        _, vjp = jax.vjp(_gate_f, rv[0], rv[1], vv[0])
        dy_, dz_, dg_ = vjp(rv[2])
        return [dy_, dz_], [dg_]

    (dy, dz), (d_g_ssm,) = _rowwise("ssm_gate_bwd", gate_bwd, [y, proj["z"], dyg], [g_ssm], [(di, f32), (di, bf16)], [di],
                                    sub=16)
    dxbc, ddtraw, d_a_log, d_dskip, d_dt_bias = _ssd_bwd(xbc, proj["dt"], dt_bias, a_log, d_skip, hin, y, dy, di, "ssd_bwd")
    dxbc_pre, d_conv_w, d_conv_b = _conv_bwd(proj["xbc"], conv_w, dxbc, xbc_pre, silu=True, name="ssm_conv_bwd",
                                             dx_dtype=bf16, tr=1024)

    dseg = {"z": dz, "xbc": dxbc_pre, "dt": ddtraw, "q0": dq[0], "q1": dq[1], "q2": dq[2], "k": dk, "v": dv}
    dh1 = None
    d_in_parts = []
    for nm, (mat, off, cnt) in seg.items():
        dh1 = _mm(dseg[nm], mat, b_rows=(off, cnt), add=dh1, name=f"in_{nm}_dx")
        dwp = _mm(dseg[nm], h1, ta=True, out_dtype=bf16, name=f"in_{nm}_dw")
        d_in_parts.append(dwp[:nh] if nm == "dt" else dwp)
    big["in_t"] = jnp.concatenate(d_in_parts, axis=0)
    dx0, (d_norm_mix[0], dmod[0][1], dmod[0][0]) = _adaln_bwd(x, g_mix[0], m[0][1], m[0][0], dh1, dx1, "adaln_mix0_bwd")

    smallg = dict(
        loss=loss, dmod=jnp.stack([jnp.concatenate(dmod[i], axis=1)[0] for i in range(2)]),
        norm_mix_g=jnp.concatenate(d_norm_mix, axis=0), norm_ffn_g=jnp.concatenate(d_norm_ffn, axis=0),
        hy_conv_w=d_conv_w, hy_conv_b=d_conv_b, hy_dt_bias=d_dt_bias[:, :nh], hy_a_log=d_a_log[:, :nh],
        hy_d_skip=d_dskip[:, :nh], hy_ssm_norm_g=d_g_ssm, rel_table=d_rel,
        cv_b_pw1=d_b_pw1, cv_w_dw=d_w_dw, cv_b_dw=d_b_dw, cv_ln_g=d_ln_g, cv_ln_b=d_ln_b, cv_b_pw2=d_b_pw2,
        final_norm_g=d_fin)
    return dx0, big["in_t"], smallg


SMALL_GRAD_ORDER = ("loss", "dmod", "norm_mix_g", "norm_ffn_g", "hy_conv_w", "hy_conv_b", "hy_dt_bias", "hy_a_log",
                    "hy_d_skip", "hy_ssm_norm_g", "rel_table", "cv_b_pw1", "cv_w_dw", "cv_b_dw", "cv_ln_g", "cv_ln_b",
                    "cv_b_pw2", "final_norm_g")


def kernel(x, c, ada_w, ada_b, norm_mix_g, norm_ffn_g, hy_w_in, hy_conv_w, hy_conv_b, hy_dt_bias, hy_a_log, hy_d_skip, hy_ssm_norm_g, hy_w_out, rel_table, cv_w_pw1, cv_b_pw1, cv_w_dw, cv_b_dw, cv_ln_g, cv_ln_b, cv_w_pw2, cv_b_pw2, ffn_w_gate, ffn_w_up, ffn_w_down, final_norm_g, loss_target, m_ada_w, m_ada_b, m_norm_mix_g, m_norm_ffn_g, m_hy_w_in, m_hy_conv_w, m_hy_conv_b, m_hy_dt_bias, m_hy_a_log, m_hy_d_skip, m_hy_ssm_norm_g, m_hy_w_out, m_rel_table, m_cv_w_pw1, m_cv_b_pw1, m_cv_w_dw, m_cv_b_dw, m_cv_ln_g, m_cv_ln_b, m_cv_w_pw2, m_cv_b_pw2, m_ffn_w_gate, m_ffn_w_up, m_ffn_w_down, m_final_norm_g, v_ada_w, v_ada_b, v_norm_mix_g, v_norm_ffn_g, v_hy_w_in, v_hy_conv_w, v_hy_conv_b, v_hy_dt_bias, v_hy_a_log, v_hy_d_skip, v_hy_ssm_norm_g, v_hy_w_out, v_rel_table, v_cv_w_pw1, v_cv_b_pw1, v_cv_w_dw, v_cv_b_dw, v_cv_ln_g, v_cv_ln_b, v_cv_w_pw2, v_cv_b_pw2, v_ffn_w_gate, v_ffn_w_up, v_ffn_w_down, v_final_norm_g):
    names = ("ada_w", "ada_b", "norm_mix_g", "norm_ffn_g", "hy_w_in", "hy_conv_w", "hy_conv_b", "hy_dt_bias", "hy_a_log",
             "hy_d_skip", "hy_ssm_norm_g", "hy_w_out", "rel_table", "cv_w_pw1", "cv_b_pw1", "cv_w_dw", "cv_b_dw", "cv_ln_g",
             "cv_ln_b", "cv_w_pw2", "cv_b_pw2", "ffn_w_gate", "ffn_w_up", "ffn_w_down", "final_norm_g")
    w = dict(zip(names, (ada_w, ada_b, norm_mix_g, norm_ffn_g, hy_w_in, hy_conv_w, hy_conv_b, hy_dt_bias, hy_a_log, hy_d_skip,
                         hy_ssm_norm_g, hy_w_out, rel_table, cv_w_pw1, cv_b_pw1, cv_w_dw, cv_b_dw, cv_ln_g, cv_ln_b, cv_w_pw2,
                         cv_b_pw2, ffn_w_gate, ffn_w_up, ffn_w_down, final_norm_g)))
    mom = dict(zip(names, (m_ada_w, m_ada_b, m_norm_mix_g, m_norm_ffn_g, m_hy_w_in, m_hy_conv_w, m_hy_conv_b, m_hy_dt_bias,
                           m_hy_a_log, m_hy_d_skip, m_hy_ssm_norm_g, m_hy_w_out, m_rel_table, m_cv_w_pw1, m_cv_b_pw1, m_cv_w_dw,
                           m_cv_b_dw, m_cv_ln_g, m_cv_ln_b, m_cv_w_pw2, m_cv_b_pw2, m_ffn_w_gate, m_ffn_w_up, m_ffn_w_down,
                           m_final_norm_g)))
    vel = dict(zip(names, (v_ada_w, v_ada_b, v_norm_mix_g, v_norm_ffn_g, v_hy_w_in, v_hy_conv_w, v_hy_conv_b, v_hy_dt_bias,
                           v_hy_a_log, v_hy_d_skip, v_hy_ssm_norm_g, v_hy_w_out, v_rel_table, v_cv_w_pw1, v_cv_b_pw1, v_cv_w_dw,
                           v_cv_b_dw, v_cv_ln_g, v_cv_ln_b, v_cv_w_pw2, v_cv_b_pw2, v_ffn_w_gate, v_ffn_w_up, v_ffn_w_down,
                           v_final_norm_g)))
    S, D = x.shape[1], x.shape[2]
    ax, ay, ac = lax.axis_index("x"), lax.axis_index("y"), lax.axis_index("c")
    me = 4 * ax + 2 * ay + ac
    c_arr = jnp.reshape(ac, (1,)).astype(jnp.int32)
    nmod = ada_w.shape[2]

    w2 = _shards_2d(w)
    big_names = list(w2)
    (g_in,) = _all_gather([w2["in_t"].astype(bf16)], "gather_w_in")
    w_in_t = _full_from_blocks("in_t", g_in)

    sharded_small = ("hy_conv_w", "cv_b_pw1", "cv_w_dw", "cv_b_dw", "cv_ln_g", "cv_ln_b", "cv_b_pw2")
    vp = _VecPack([c.shape] + [w[nm].shape for nm in sharded_small])
    (sg,) = _all_gather([vp.pack([c] + [w[nm] for nm in sharded_small])], "gather_vectors")
    parts = vp.unpack_stacked(sg)
    c_all = parts[0][:, 0]
    small = {k: w[k] for k in ("norm_mix_g", "norm_ffn_g", "hy_conv_b", "hy_dt_bias", "hy_a_log", "hy_d_skip",
                               "hy_ssm_norm_g", "rel_table", "final_norm_g")}
    for p, nm in zip(parts[1:], sharded_small):
        p = p[:, 0]
        p = jnp.moveaxis(p, 0, -2)
        small[nm + "_full"] = p.reshape(p.shape[:-2] + (N_DEV * p.shape[-1],))

    (cs_all,), _ = _rowwise("ada_silu", lambda rv, vv: ([_silu(rv[0])], []), [c_all], [], [(D, f32)], [])
    b_mine = lax.dynamic_slice_in_dim(ada_b, me * nmod, nmod, axis=1)
    mod_part = jnp.stack([_mm(cs_all, ada_w[i], bias=b_mine[i:i + 1], name=f"ada_mod_{i}") for i in range(2)])
    (mod_all,) = _all_gather([mod_part.reshape(2 * N_DEV, nmod)], "gather_mod")
    mod_all = mod_all.reshape(N_DEV, 2, N_DEV, nmod)
    mod_mine = lax.dynamic_index_in_dim(mod_all, me, axis=2, keepdims=False)
    mod = jnp.transpose(mod_mine, (1, 0, 2)).reshape(2, 6, D)
    comm = _Overlap({nm: w2[nm].astype(bf16) for nm in big_names if nm != "in_t"}, me, after=[mod, w_in_t])
    mod = _tie(mod, comm.token)

    dx0, d_in_t, sgrad = _local_step(x[0], loss_target[0], mod, w_in_t, comm, small)
    comm.send("in", {"in_t": d_in_t})

    gp = _VecPack([sgrad[k].shape for k in SMALL_GRAD_ORDER])
    (g_all,) = _all_gather([gp.pack([sgrad[k] for k in SMALL_GRAD_ORDER])], "gather_small_grads")
    tot = dict(zip(SMALL_GRAD_ORDER, gp.unpack(_sum_slots(g_all, "sum_small_grads"))))
    dmod_all = gp.unpack_stacked(g_all, only=SMALL_GRAD_ORDER.index("dmod"))
    loss = tot["loss"][0, 0]

    grads = {}
    dmod_mine = lax.dynamic_slice_in_dim(dmod_all, me * nmod, nmod, axis=2)
    grads["ada_w"] = jnp.stack([_mm(cs_all, dmod_mine[:, i], ta=True, name=f"ada_w_grad_{i}") for i in range(2)])
    grads["ada_b"] = tot["dmod"]
    grads["norm_mix_g"], grads["norm_ffn_g"] = tot["norm_mix_g"], tot["norm_ffn_g"]
    grads["hy_conv_b"] = tot["hy_conv_b"]
    grads["hy_dt_bias"] = tot["hy_dt_bias"]
    grads["hy_a_log"] = tot["hy_a_log"]
    grads["hy_d_skip"] = tot["hy_d_skip"]
    grads["hy_ssm_norm_g"] = tot["hy_ssm_norm_g"]
    grads["rel_table"] = tot["rel_table"]
    grads["final_norm_g"] = tot["final_norm_g"][0]
    for nm in sharded_small:
        n = w[nm].shape[-1]
        grads[nm] = lax.dynamic_slice_in_dim(tot[nm], me * n, n, axis=1).reshape(w[nm].shape)

    delta, new_m, new_v = {}, {}, {}
    shp = ada_w.shape
    two = lambda t: t.reshape(-1, shp[-1])
    d_, m_, v_ = _adamw(two(ada_w), two(grads["ada_w"]), two(m_ada_w), two(v_ada_w), "adamw_ada_w")
    delta["ada_w"], new_m["ada_w"], new_v["ada_w"] = d_.reshape(shp), m_.reshape(shp), v_.reshape(shp)
    rest = [nm for nm in names if nm not in BIG_PARAMS and nm != "ada_w"]
    sp = _VecPack([w[nm].shape for nm in rest])
    packs = [sp.pack([t[nm] for nm in rest]) for t in (w, grads, mom, vel)]
    ds_, ms_, vs_ = _adamw(*packs, "adamw_small")
    for nm, a, b, e in zip(rest, sp.unpack(ds_), sp.unpack(ms_), sp.unpack(vs_)):
        delta[nm], new_m[nm], new_v[nm] = a, b, e

    g2 = {}
    after = d_
    for tag in ("layer1", "layer0", "in"):
        for nm, slots in comm.received(tag, after).items():
            g2[nm] = _sum_slots(slots, f"sum_{nm}")
            after = g2[nm]
    grads.update(_unshard_2d(g2))
    m2, v2 = _shards_2d(mom), _shards_2d(vel)
    d2, nm2, nv2 = {}, {}, {}
    for nm in big_names:
        d2[nm], nm2[nm], nv2[nm] = _adamw(w2[nm], g2[nm], m2[nm], v2[nm], f"adamw_{nm}")
    delta.update(_unshard_2d(d2))
    new_m.update(_unshard_2d(nm2))
    new_v.update(_unshard_2d(nv2))

    return (loss, dx0[None], *[grads[n] for n in names], *[delta[n] for n in names],
            *[new_m[n] for n in names], *[new_v[n] for n in names])
```

```python
import functools
import math

import numpy as np
import jax
import jax.numpy as jnp
from jax import lax
from jax.experimental import pallas as pl
from jax.experimental.pallas import tpu as pltpu

f32 = jnp.float32
bf16 = jnp.bfloat16
EPS = 1e-6
N_DEV = 8
LANES = 128
SSM_STATE = 128
SSM_CHUNK = 128
SSM_GROUPS = 4
HEAD_DIM = 64
ATT_BLK = 128
ATT_DILATIONS = (1, 4, 16)
REL_BUCKETS = 32
REL_MAX_DIST = 2048
ADAM_LR, ADAM_B1, ADAM_B2, ADAM_EPS, ADAM_WD, ADAM_STEP = 0.001, 0.9, 0.999, 1e-08, 0.01, 10
PACK_COLS = 1024
PACK_ROW_TILE = 256
MESH = pl.DeviceIdType.MESH
VMEM_LIMIT = 48 * 1024 * 1024


def _sds(shape, dtype=f32):
    return jax.ShapeDtypeStruct(tuple(shape), dtype)


def _tile(n, cap, mult):
    best = None
    t = mult
    while t <= min(n, cap):
        if n % t == 0:
            best = t
        t += mult
    return best if best is not None else n


def _params(sem):
    return pltpu.CompilerParams(dimension_semantics=sem, vmem_limit_bytes=VMEM_LIMIT)


def _mm(a, b, *, name, ta=False, tb=False, b_rows=None, bias=None, add=None, out_dtype=f32,
        tm_cap=512, tn_cap=1536, tk_cap=8192):
    if ta:
        K, M = a.shape
    else:
        M, K = a.shape
    off, cnt = b_rows if b_rows is not None else (0, b.shape[0])
    if tb:
        N, K2 = cnt, b.shape[1]
    else:
        K2, N = cnt, b.shape[1]
    assert K == K2, (a.shape, b.shape, ta, tb, b_rows)
    if ta and a.dtype == f32:
        tm_cap = min(tm_cap, 256)
    tm = _tile(M, tm_cap, LANES)
    tn = _tile(math.gcd(off, N) if tb else N, tn_cap, LANES)
    tk = _tile(K if tb else math.gcd(off, K), tk_cap, LANES)
    assert N % tn == 0 and K % tk == 0 and off % (tn if tb else tk) == 0, (name, off, N, K, tn, tk)
    nk = K // tk
    jo, ko = (off // tn, 0) if tb else (0, off // tk)
    has_bias, has_add = bias is not None, add is not None
    dn = (((0 if ta else 1,), (1 if tb else 0,)), ((), ()))

    def body(*refs):
        a_ref, b_ref = refs[0], refs[1]
        pos = 2
        bias_ref = add_ref = None
        if has_bias:
            bias_ref = refs[pos]
            pos += 1
        if has_add:
            add_ref = refs[pos]
            pos += 1
        o_ref = refs[pos]
        k = pl.program_id(2)
        part = lax.dot_general(a_ref[...].astype(bf16), b_ref[...].astype(bf16), dn, preferred_element_type=f32)

        def finish(r):
            if has_bias:
                r = r + bias_ref[...]
            if has_add:
                r = r + add_ref[...]
            o_ref[...] = r.astype(o_ref.dtype)

        if nk == 1:
            finish(part)
        else:
            acc_ref = refs[pos + 1]

            @pl.when(k == 0)
            def _():
                acc_ref[...] = part

            @pl.when((k > 0) & (k < nk - 1))
            def _():
                acc_ref[...] += part

            @pl.when(k == nk - 1)
            def _():
                finish(acc_ref[...] + part)

    in_specs = [
        pl.BlockSpec((tk, tm), lambda i, j, k: (k, i)) if ta else pl.BlockSpec((tm, tk), lambda i, j, k: (i, k)),
        pl.BlockSpec((tn, tk), lambda i, j, k: (j + jo, k)) if tb else pl.BlockSpec((tk, tn), lambda i, j, k: (k + ko, j)),
    ]
    args = [a, b]
    if has_bias:
        in_specs.append(pl.BlockSpec((1, tn), lambda i, j, k: (0, j)))
        args.append(bias)
    if has_add:
        in_specs.append(pl.BlockSpec((tm, tn), lambda i, j, k: (i, j)))
        args.append(add)
    return pl.pallas_call(
        body, name=name, grid=(M // tm, N // tn, nk), in_specs=in_specs,
        out_specs=pl.BlockSpec((tm, tn), lambda i, j, k: (i, j)), out_shape=_sds((M, N), out_dtype),
        scratch_shapes=[pltpu.VMEM((tm, tn), f32)] if nk > 1 else [],
        compiler_params=_params(("parallel", "parallel", "arbitrary")),
    )(*args)


def _rowwise(name, fn, rows, vecs, out_rows, out_accs, *, tr_cap=256, sub=8):
    rows = [r if isinstance(r, tuple) else (r, 0, r.shape[1]) for r in rows]
    R = rows[0][0].shape[0]
    tr = _tile(R, tr_cap, 8)
    sub = sub if tr % sub == 0 else tr
    n_r, n_v, n_or, n_oa = len(rows), len(vecs), len(out_rows), len(out_accs)

    def body(*refs):
        row_refs = refs[:n_r]
        vec_refs = refs[n_r:n_r + n_v]
        orow_refs = refs[n_r + n_v:n_r + n_v + n_or]
        oacc_refs = refs[n_r + n_v + n_or:]
        vv = [r[...] for r in vec_refs]

        n_sub = tr // sub
        together = 4 if n_sub % 4 == 0 else 1

        def step(s, accs):
            for t in range(together):
                sl = pl.ds(pl.multiple_of((s * together + t) * sub, sub), sub)
                ro, ao = fn([r[sl, :] for r in row_refs], vv)
                for o_ref, o in zip(orow_refs, ro):
                    o_ref[sl, :] = o.astype(o_ref.dtype)
                accs = tuple(x + y for x, y in zip(accs, ao))
            return accs

        accs = lax.fori_loop(0, n_sub // together, step, tuple(jnp.zeros((1, w), f32) for w in out_accs))
        if n_oa:
            @pl.when(pl.program_id(0) == 0)
            def _():
                for ref in oacc_refs:
                    ref[...] = jnp.zeros_like(ref)

            for ref, x in zip(oacc_refs, accs):
                ref[...] += x

    in_specs = [pl.BlockSpec((tr, w), functools.partial(lambda i, cb: (i, cb), cb=cb)) for (_, cb, w) in rows]
    in_specs += [pl.BlockSpec((1, v.shape[1]), lambda i: (0, 0)) for v in vecs]
    out_specs = [pl.BlockSpec((tr, w), lambda i: (i, 0)) for (w, _) in out_rows]
    out_specs += [pl.BlockSpec((1, w), lambda i: (0, 0)) for w in out_accs]
    out_shape = [_sds((R, w), dt) for (w, dt) in out_rows] + [_sds((1, w)) for w in out_accs]
    res = pl.pallas_call(
        body, name=name, grid=(R // tr,), in_specs=in_specs, out_specs=out_specs, out_shape=out_shape,
        compiler_params=_params(("arbitrary",)),
    )(*[r[0] for r in rows], *vecs)
    return res[:n_or], res[n_or:]


def _silu(x):
    return x * jax.nn.sigmoid(x)


def _rms(x, g):
    return x * lax.rsqrt(jnp.mean(x * x, -1, keepdims=True) + EPS) * g


def _adaln_f(x, g, sc, sh):
    return _rms(x, g) * (1.0 + sc) + sh


def _gate_f(y, z, g):
    return _rms(y * _silu(z), g)


def _lnsilu_f(u, g, b):
    mu = jnp.mean(u, -1, keepdims=True)
    var = jnp.mean(jnp.square(u - mu), -1, keepdims=True)
    return _silu((u - mu) * lax.rsqrt(var + EPS) * g + b)


def _adaln_fwd(x, g, sc, sh, name):
    (h,), _ = _rowwise(name, lambda rv, vv: ([_adaln_f(rv[0], *vv)], []), [x], [g, sc, sh], [(x.shape[1], bf16)], [],
                       sub=16)
    return h


def _adaln_bwd(x, g, sc, sh, dh, dres, name):
    def fn(rv, vv):
        xv, dhv, drv = rv
        _, vjp = jax.vjp(_adaln_f, xv, *vv)
        dx, dg, dsc, dsh = vjp(dhv)
        return [dx + drv], [dg, dsc, dsh]
    w = x.shape[1]
    (dx,), accs = _rowwise(name, fn, [x, dh, dres], [g, sc, sh], [(w, f32)], [w, w, w])
    return dx, accs


def _resid_fwd(x, gate, mix, name):
    (y,), _ = _rowwise(name, lambda rv, vv: ([rv[0] + vv[0] * rv[1]], []), [x, mix], [gate], [(x.shape[1], f32)], [])
    return y


def _resid_bwd(dx, mix, gate, name):
    def fn(rv, vv):
        dxv, mv = rv
        dm = vv[0] * dxv
        return [dm], [jnp.sum(dxv * mv, 0, keepdims=True), jnp.sum(dm, 0, keepdims=True)]
    w = dx.shape[1]
    (dmix,), accs = _rowwise(name, fn, [dx, mix], [gate], [(w, bf16)], [w, w], sub=16)
    return dmix, accs


def _add3(a, b, c, name):
    (y,), _ = _rowwise(name, lambda rv, vv: ([rv[0] + rv[1] + rv[2]], []), [a, b, c], [], [(a.shape[1], bf16)], [],
                       sub=16)
    return y


CONV_HALO = 32
CONV_ROWS = 64


def _conv_fwd(x, w, b, *, silu, name, tr=512):
    S, C = x.shape
    K = w.shape[0]
    H = CONV_HALO
    assert K - 1 <= H and S % tr == 0 and tr % H == 0 and C % LANES == 0
    nh = tr // H

    def body(xp_ref, xc_ref, w_ref, b_ref, *rest):
        outs, scr = rest[:-1], rest[-1]
        i = pl.program_id(1)
        scr[pl.ds(0, H), :] = jnp.where(i > 0, xp_ref[...], 0.0)
        scr[pl.ds(H, tr), :] = xc_ref[...]
        taps = [w_ref[pl.ds(k, 1), :] for k in range(K)]
        for c0 in range(0, tr, CONV_ROWS):
            acc = jnp.zeros((CONV_ROWS, LANES), f32) + b_ref[...]
            for k in range(K):
                acc = acc + scr[pl.ds(c0 + H - (K - 1) + k, CONV_ROWS), :] * taps[k]
            outs[0][pl.ds(c0, CONV_ROWS), :] = acc
            if silu:
                outs[1][pl.ds(c0, CONV_ROWS), :] = _silu(acc)

    n_out = 2 if silu else 1
    return pl.pallas_call(
        body, name=name, grid=(C // LANES, S // tr),
        in_specs=[pl.BlockSpec((H, LANES), lambda j, i: (jnp.maximum(i * nh - 1, 0), j)),
                  pl.BlockSpec((tr, LANES), lambda j, i: (i, j)),
                  pl.BlockSpec((K, LANES), lambda j, i: (0, j)),
                  pl.BlockSpec((1, LANES), lambda j, i: (0, j))],
        out_specs=[pl.BlockSpec((tr, LANES), lambda j, i: (i, j))] * n_out,
        out_shape=[_sds((S, C))] * n_out,
        scratch_shapes=[pltpu.VMEM((tr + H, LANES), f32)],
        compiler_params=_params(("parallel", "arbitrary")),
    )(x, x, w, b)


def _conv_bwd(x, w, dact, pre, *, silu, name, dx_dtype=f32, tr=512):
    S, C = x.shape
    K = w.shape[0]
    H = CONV_HALO
    nh = tr // H
    n_i = S // tr
    kp = -(-K // 8) * 8

    def dsilu(p):
        s = jax.nn.sigmoid(p)
        return s * (1.0 + p * (1.0 - s))

    def body(*refs):
        if silu:
            xp_ref, xc_ref, w_ref, dc_ref, dn_ref, pc_ref, pn_ref, dx_ref, dw_ref, db_ref, xs, ds = refs
        else:
            xp_ref, xc_ref, w_ref, dc_ref, dn_ref, dx_ref, dw_ref, db_ref, xs, ds = refs
        i = pl.program_id(1)
        xs[pl.ds(0, H), :] = jnp.where(i > 0, xp_ref[...], 0.0)
        xs[pl.ds(H, tr), :] = xc_ref[...]
        dcur = dc_ref[...]
        dnext = dn_ref[...]
        if silu:
            dcur = dcur * dsilu(pc_ref[...])
            dnext = dnext * dsilu(pn_ref[...])
        ds[pl.ds(0, tr), :] = dcur
        ds[pl.ds(tr, H), :] = jnp.where(i < n_i - 1, dnext, 0.0)
        taps = [w_ref[pl.ds(k, 1), :] for k in range(K)]
        fold = lambda t: jnp.sum(t.reshape(CONV_ROWS // 8, 8, LANES), axis=0)
        dw_parts = [jnp.zeros((8, LANES), f32) for _ in range(K)]
        db_part = jnp.zeros((8, LANES), f32)
        for c0 in range(0, tr, CONV_ROWS):
            acc = jnp.zeros((CONV_ROWS, LANES), f32)
            d_c = ds[pl.ds(c0, CONV_ROWS), :]
            for k in range(K):
                acc = acc + ds[pl.ds(c0 + K - 1 - k, CONV_ROWS), :] * taps[k]
                dw_parts[k] = dw_parts[k] + fold(d_c * xs[pl.ds(c0 + H - (K - 1) + k, CONV_ROWS), :])
            db_part = db_part + fold(d_c)
            dx_ref[pl.ds(c0, CONV_ROWS), :] = acc.astype(dx_ref.dtype)

        @pl.when(i == 0)
        def _():
            dw_ref[...] = jnp.zeros_like(dw_ref)
            db_ref[...] = jnp.zeros_like(db_ref)

        for k in range(K):
            dw_ref[pl.ds(k, 1), :] += jnp.sum(dw_parts[k], 0, keepdims=True)
        db_ref[...] += jnp.sum(db_part, 0, keepdims=True)

    prev = pl.BlockSpec((H, LANES), lambda j, i: (jnp.maximum(i * nh - 1, 0), j))
    cur = pl.BlockSpec((tr, LANES), lambda j, i: (i, j))
    nxt = pl.BlockSpec((H, LANES), lambda j, i: (jnp.minimum((i + 1) * nh, n_i * nh - 1), j))
    in_specs = [prev, cur, pl.BlockSpec((K, LANES), lambda j, i: (0, j)), cur, nxt]
    args = [x, x, w, dact, dact]
    if silu:
        in_specs += [cur, nxt]
        args += [pre, pre]
    dx, dw, db = pl.pallas_call(
        body, name=name, grid=(C // LANES, n_i), in_specs=in_specs,
        out_specs=[cur, pl.BlockSpec((kp, LANES), lambda j, i: (0, j)), pl.BlockSpec((1, LANES), lambda j, i: (0, j))],
        out_shape=[_sds((S, C), dx_dtype), _sds((kp, C)), _sds((1, C))],
        scratch_shapes=[pltpu.VMEM((tr + H, LANES), f32), pltpu.VMEM((tr + H, LANES), f32)],
        compiler_params=_params(("parallel", "arbitrary")),
    )(*args)
    return dx, dw[:K], db


def _dot(a, b):
    return jnp.dot(a.astype(bf16), b.astype(bf16), preferred_element_type=f32)


def _dot_nt(a, b):
    return lax.dot_general(a.astype(bf16), b.astype(bf16), (((1,), (1,)), ((), ())), preferred_element_type=f32)


def _dot_tn(a, b):
    return lax.dot_general(a.astype(bf16), b.astype(bf16), (((0,), (0,)), ((), ())), preferred_element_type=f32)


def _softplus(x):
    return jnp.maximum(x, 0.0) + jnp.log(1.0 + jnp.exp(-jnp.abs(x)))


def _tri(q):
    i = lax.broadcasted_iota(jnp.int32, (q, q), 0)
    j = lax.broadcasted_iota(jnp.int32, (q, q), 1)
    return i >= j


def _ssd_prep(dtraw, dt_bias, a_log):
    q = dtraw.shape[0]
    dt = _softplus(dtraw + dt_bias)
    A = -jnp.exp(a_log)
    tri = _tri(q)
    cs = jnp.dot(tri.astype(f32), dt * A, preferred_element_type=f32, precision=lax.Precision.HIGHEST)
    return dt, A, cs, cs.T, tri


def _expand(cols, h0, n, width):
    q = cols.shape[0]
    return jnp.concatenate([jnp.broadcast_to(cols[:, h0 + r:h0 + r + 1], (q, width)) for r in range(n)], axis=1)


def _ssd_fwd(xbc, dtraw, dt_bias, a_log, d_skip, di, name):
    S, CD = xbc.shape
    Q, N, G = SSM_CHUNK, SSM_STATE, SSM_GROUPS
    nc = S // Q
    nh = di // HEAD_DIM
    R = nh // G
    gw = R * HEAD_DIM

    def body(xbc_ref, dt_ref, bias_ref, alog_ref, dsk_ref, y_ref, hin_ref, state):
        c = pl.program_id(0)

        @pl.when(c == 0)
        def _():
            state[...] = jnp.zeros_like(state)

        hin_ref[...] = state[...]
        dt, A, cs, csT, tri = _ssd_prep(dt_ref[...], bias_ref[...], alog_ref[...])
        dsk = dsk_ref[...]
        ecs = jnp.exp(cs)
        dend = jnp.exp(cs[Q - 1:Q, :] - cs)
        elast = jnp.exp(cs[Q - 1:Q, :])
        for g in range(G):
            h0 = g * R
            Bg = xbc_ref[:, pl.ds(di + g * N, N)]
            Cg = xbc_ref[:, pl.ds(di + G * N + g * N, N)]
            xg = xbc_ref[:, pl.ds(g * gw, gw)]
            Hg = state[pl.ds(g * gw, gw), :]
            Gm = _dot_nt(Cg, Bg)
            xdt = xg * _expand(dt, h0, R, HEAD_DIM)
            yoff = _dot_nt(Cg, Hg) * _expand(ecs, h0, R, HEAD_DIM)
            ys = []
            for r in range(R):
                h = h0 + r
                L = jnp.exp(jnp.where(tri, cs[:, h:h + 1] - csT[h:h + 1, :], -jnp.inf))
                ys.append(_dot(Gm * L, xdt[:, r * HEAD_DIM:(r + 1) * HEAD_DIM]))
            y = jnp.concatenate(ys, axis=1) + yoff + xg * _expand(dsk, h0, R, HEAD_DIM)
            y_ref[:, pl.ds(g * gw, gw)] = y
            hnew = _dot_tn(xdt * _expand(dend, h0, R, HEAD_DIM), Bg)
            escale = jnp.concatenate([jnp.broadcast_to(elast[:, h0 + r:h0 + r + 1], (HEAD_DIM, N)) for r in range(R)], axis=0)
            state[pl.ds(g * gw, gw), :] = escale * Hg + hnew

    vec = pl.BlockSpec((1, LANES), lambda c: (0, 0))
    return pl.pallas_call(
        body, name=name, grid=(nc,),
        in_specs=[pl.BlockSpec((Q, CD), lambda c: (c, 0)), pl.BlockSpec((Q, LANES), lambda c: (c, 0)), vec, vec, vec],
        out_specs=[pl.BlockSpec((Q, di), lambda c: (c, 0)), pl.BlockSpec((None, di, N), lambda c: (c, 0, 0))],
        out_shape=[_sds((S, di)), _sds((nc, di, N))],
        scratch_shapes=[pltpu.VMEM((di, N), f32)],
        compiler_params=_params(("arbitrary",)),
    )(xbc, dtraw, dt_bias, a_log, d_skip)


def _dot_exact(a, b):
    return jnp.dot(a, b, preferred_element_type=f32, precision=lax.Precision.HIGHEST)


def _ssd_bwd(xbc, dtraw, dt_bias, a_log, d_skip, hin, y, dy, di, name):
    S, CD = xbc.shape
    Q, N, G = SSM_CHUNK, SSM_STATE, SSM_GROUPS
    nc = S // Q
    nh = di // HEAD_DIM
    R = nh // G
    gw = R * HEAD_DIM
    P = HEAD_DIM
    head_of_col = jnp.asarray((np.arange(di)[:, None] // P == np.arange(LANES)[None, :]).astype(np.float32))
    dsk_wide = jnp.repeat(d_skip[0, :nh], P)[None]

    def body(xbc_ref, dt_ref, bias_ref, alog_ref, dskw_ref, hoc_ref, hin_ref, y_ref, dy_ref,
             dxbc_ref, ddt_ref, dA_ref, ddsk_ref, dtb_ref, dstate, dxdt_all, tend_all, yoff_all, colterm_all):
        c = pl.program_id(0)

        @pl.when(c == 0)
        def _():
            dstate[...] = jnp.zeros_like(dstate)
            dA_ref[...] = jnp.zeros_like(dA_ref)
            ddsk_ref[...] = jnp.zeros_like(ddsk_ref)
            dtb_ref[...] = jnp.zeros_like(dtb_ref)

        dtraw_v = dt_ref[...]
        dt, A, cs, csT, tri = _ssd_prep(dtraw_v, bias_ref[...], alog_ref[...])
        tri_t = jnp.logical_not(tri) | (lax.broadcasted_iota(jnp.int32, (Q, Q), 0) == lax.broadcasted_iota(jnp.int32, (Q, Q), 1))
        ecs = jnp.exp(cs)
        dend = jnp.exp(cs[Q - 1:Q, :] - cs)
        elast = jnp.exp(cs[Q - 1:Q, :])
        hoc = hoc_ref[...]
        state_dot = jnp.sum(_dot_exact(dstate[...] * hin_ref[...], jnp.ones((N, LANES), f32)) * hoc, 0, keepdims=True) * elast
        for g in range(G):
            h0 = g * R
            Bg = xbc_ref[:, pl.ds(di + g * N, N)]
            Cg = xbc_ref[:, pl.ds(di + G * N + g * N, N)]
            xg = xbc_ref[:, pl.ds(g * gw, gw)]
            dyg = dy_ref[:, pl.ds(g * gw, gw)]
            Hg = hin_ref[pl.ds(g * gw, gw), :]
            dHg = dstate[pl.ds(g * gw, gw), :]
            dt_e = _expand(dt, h0, R, P)
            ecs_e = _expand(ecs, h0, R, P)
            dend_e = _expand(dend, h0, R, P)
            cols = pl.ds(g * gw, gw)
            Gm = _dot_nt(Cg, Bg)
            Gm_t = _dot_nt(Bg, Cg)
            xdt = xg * dt_e
            dye = dyg * ecs_e
            bdh = _dot_nt(Bg, dHg)
            dC = _dot(dye, Hg)
            dB = _dot(xdt * dend_e, dHg)
            dHin = _dot_tn(dye, Cg)
            dxdt_state = dend_e * bdh
            end_term = xdt * dxdt_state
            tend_all[:, cols] = end_term
            yoff_all[:, cols] = _dot_nt(Cg, Hg) * ecs_e
            dG = jnp.zeros((Q, Q), f32)
            dxd = []
            for r in range(R):
                h = h0 + r
                sl = slice(r * P, (r + 1) * P)
                seg = cs[:, h:h + 1] - csT[h:h + 1, :]
                L = jnp.exp(jnp.where(tri, seg, -jnp.inf))
                L_t = jnp.exp(jnp.where(tri_t, -seg, -jnp.inf))
                dyh = dyg[:, sl]
                dG = dG + _dot_nt(dyh, xdt[:, sl]) * L
                dxd.append(_dot(Gm_t * L_t, dyh))
            dxdt_diag = jnp.concatenate(dxd, axis=1)
            dxdt = dxdt_diag + dxdt_state
            dxdt_all[:, cols] = dxdt
            colterm_all[:, cols] = xdt.astype(bf16).astype(f32) * dxdt_diag + end_term
            dxbc_ref[:, cols] = dxdt * dt_e + dyg * dskw_ref[:, cols]
            dxbc_ref[:, pl.ds(di + g * N, N)] = dB + _dot_tn(dG, Cg)
            dxbc_ref[:, pl.ds(di + G * N + g * N, N)] = dC + _dot(dG, Bg)
            escale = jnp.concatenate([jnp.broadcast_to(elast[:, h0 + r:h0 + r + 1], (P, N)) for r in range(R)], axis=0)
            dstate[pl.ds(g * gw, gw), :] = escale * dHg + dHin
        xs = xbc_ref[:, pl.ds(0, di)]
        dyv = dy_ref[...]
        yoff = yoff_all[...]
        y_diag = y_ref[...] - dskw_ref[...] * xs - yoff
        rs_y = _dot_exact(dyv.astype(bf16).astype(f32) * y_diag + dyv * yoff, hoc)
        rs_c = _dot_exact(colterm_all[...], hoc)
        rs_x = _dot_exact(dxdt_all[...] * xs, hoc)
        end_dot = _dot_exact(jnp.broadcast_to(jnp.sum(tend_all[...], 0, keepdims=True), (8, di)), hoc)[0:1]
        last = lax.broadcasted_iota(jnp.int32, (Q, 1), 0) == Q - 1
        dcs = rs_y - rs_c + jnp.where(last, end_dot + state_dot, 0.0)
        da = lax.dot_general(tri.astype(f32), dcs, (((0,), (0,)), ((), ())), preferred_element_type=f32,
                             precision=lax.Precision.HIGHEST)
        ddt = da * A + rs_x
        ddtraw = ddt * jax.nn.sigmoid(dtraw_v + bias_ref[...])
        ddt_ref[...] = ddtraw.astype(ddt_ref.dtype)
        dA_ref[...] += jnp.sum(da * dt, 0, keepdims=True) * A
        ddsk_ref[...] += jnp.sum(_dot_exact(dyv * xs, hoc), 0, keepdims=True)
        dtb_ref[...] += jnp.sum(ddtraw, 0, keepdims=True)

    vec = pl.BlockSpec((1, LANES), lambda c: (0, 0))
    rev = lambda c: (nc - 1 - c, 0)
    return pl.pallas_call(
        body, name=name, grid=(nc,),
        in_specs=[pl.BlockSpec((Q, CD), rev), pl.BlockSpec((Q, LANES), rev), vec, vec,
                  pl.BlockSpec((1, di), lambda c: (0, 0)), pl.BlockSpec((di, LANES), lambda c: (0, 0)),
                  pl.BlockSpec((None, di, N), lambda c: (nc - 1 - c, 0, 0)), pl.BlockSpec((Q, di), rev),
                  pl.BlockSpec((Q, di), rev)],
        out_specs=[pl.BlockSpec((Q, CD), rev), pl.BlockSpec((Q, LANES), rev), vec, vec, vec],
        out_shape=[_sds((S, CD)), _sds((S, LANES), bf16), _sds((1, LANES)), _sds((1, LANES)), _sds((1, LANES))],
        scratch_shapes=[pltpu.VMEM((di, N), f32)] + [pltpu.VMEM((Q, di), f32)] * 4,
        compiler_params=_params(("arbitrary",)),
    )(xbc, dtraw, dt_bias, a_log, dsk_wide, head_of_col, hin, y, dy)


def _t5_bucket_np(dist):
    max_exact = REL_BUCKETS // 2
    n = np.maximum(dist, 1).astype(np.float32)
    large = np.float32(max_exact) + np.log(n / np.float32(max_exact)) / np.float32(math.log(REL_MAX_DIST / max_exact)) * np.float32(REL_BUCKETS - max_exact)
    large = np.minimum(large.astype(np.int32), REL_BUCKETS - 1)
    return np.where(dist < max_exact, dist, large)


def _bucket_onehot():
    i = np.arange(ATT_BLK)[:, None]
    j = np.arange(2 * ATT_BLK)[None, :]
    delta = np.maximum(ATT_BLK + i - j, 0)
    out = np.zeros((len(ATT_DILATIONS), REL_BUCKETS, ATT_BLK * 2 * ATT_BLK), np.float32)
    for gi, d in enumerate(ATT_DILATIONS):
        b = _t5_bucket_np(delta * d).reshape(-1)
        out[gi, b, np.arange(b.size)] = 1.0
    return out


def _exact_mm(a, b, *, name, tb=False):
    M, K = a.shape
    N = b.shape[0] if tb else b.shape[1]
    tn = _tile(N, 4096, LANES)
    dn = (((1,), (1 if tb else 0,)), ((), ()))

    def body(a_ref, b_ref, o_ref):
        o_ref[...] = lax.dot_general(a_ref[...], b_ref[...], dn, preferred_element_type=f32,
                                     precision=lax.Precision.HIGHEST)

    return pl.pallas_call(
        body, name=name, grid=(N // tn,),
        in_specs=[pl.BlockSpec((M, K), lambda j: (0, 0)),
                  pl.BlockSpec((tn, K), lambda j: (j, 0)) if tb else pl.BlockSpec((K, tn), lambda j: (0, j))],
        out_specs=pl.BlockSpec((M, tn), lambda j: (0, j)), out_shape=_sds((M, N)),
        compiler_params=_params(("parallel",)),
    )(a, b)


def _band_penalty():
    i = np.arange(ATT_BLK)[:, None]
    j = np.arange(2 * ATT_BLK)[None, :]
    delta = ATT_BLK + i - j
    return np.where((delta >= 0) & (delta <= ATT_BLK), 0.0, -np.inf).astype(np.float32)


def _first_block_keep(n):
    col = lax.broadcasted_iota(jnp.int32, (ATT_BLK, 2 * ATT_BLK), 1)
    return (col >= ATT_BLK) | (n > 0)


ATT_SCALE = HEAD_DIM ** -0.5


def _rows(ref, r, d):
    return ref[...] if d == 1 else ref[pl.ds(r, ATT_BLK, stride=d), :]


def _set_rows(ref, r, d, val):
    if d == 1:
        ref[...] = val
    else:
        ref[pl.ds(r, ATT_BLK, stride=d), :] = val


def _attn_width(d, D):
    return D if d == 1 else LANES


def _over_residues(d, one, unroll=1):
    if d == 1:
        one(0)
    else:
        lax.fori_loop(0, d, lambda r, c: (one(r), c)[1], 0, unroll=unroll)


def _attn_fwd(q, k, v, bias, d, name):
    S, D = q.shape
    nb = S // (d * ATT_BLK)
    W = _attn_width(d, D)
    HB = W // HEAD_DIM

    def body(q_ref, kp_ref, kc_ref, vp_ref, vc_ref, b_ref, o_ref, lse_ref):
        keep = _first_block_keep(pl.program_id(1))

        def one(r):
            qs = (_rows(q_ref, r, d) * ATT_SCALE).astype(bf16)
            kcat = jnp.concatenate([_rows(kp_ref, r, d), _rows(kc_ref, r, d)], axis=0).astype(bf16)
            vcat = jnp.concatenate([_rows(vp_ref, r, d), _rows(vc_ref, r, d)], axis=0).astype(bf16)
            outs, lses = [], []
            for h in range(HB):
                sl = slice(h * HEAD_DIM, (h + 1) * HEAD_DIM)
                s = jnp.where(keep, _dot_nt(qs[:, sl], kcat[:, sl]) + b_ref[h], -jnp.inf)
                m = jnp.max(s, -1, keepdims=True)
                p = jnp.exp(s - m)
                l = jnp.sum(p, -1, keepdims=True)
                outs.append(_dot(p, vcat[:, sl]) / l)
                lses.append(jnp.broadcast_to(m + jnp.log(l), (ATT_BLK, HEAD_DIM)))
            _set_rows(o_ref, r, d, jnp.concatenate(outs, axis=1))
            _set_rows(lse_ref, r, d, jnp.concatenate(lses, axis=1))

        _over_residues(d, one, unroll=4)

    cur = pl.BlockSpec((ATT_BLK * d, W), lambda j, n: (n, j))
    prev = pl.BlockSpec((ATT_BLK * d, W), lambda j, n: (jnp.maximum(n - 1, 0), j))
    return pl.pallas_call(
        body, name=name, grid=(D // W, nb),
        in_specs=[cur, prev, cur, prev, cur, pl.BlockSpec((HB, ATT_BLK, 2 * ATT_BLK), lambda j, n: (j, 0, 0))],
        out_specs=[cur, cur], out_shape=[_sds((S, D)), _sds((S, D))],
        compiler_params=_params(("parallel", "arbitrary")),
    )(q, k, k, v, v, bias)


def _by_block(a, d):
    S, H = a.shape
    t = jnp.transpose(a.reshape(S // (d * ATT_BLK), ATT_BLK, d, H), (0, 2, 3, 1))
    return t[:, :, :, None, :]


def _head_sums(a, b, name):
    S, D = a.shape
    tr = _tile(S, 512, 8)
    hoc = jnp.asarray((np.arange(D)[:, None] // HEAD_DIM == np.arange(LANES)[None, :]).astype(np.float32))

    def body(a_ref, b_ref, h_ref, o_ref):
        o_ref[...] = _dot_exact(a_ref[...] * b_ref[...], h_ref[...])

    return pl.pallas_call(
        body, name=name, grid=(S // tr,),
        in_specs=[pl.BlockSpec((tr, D), lambda i: (i, 0)), pl.BlockSpec((tr, D), lambda i: (i, 0)),
                  pl.BlockSpec((D, LANES), lambda i: (0, 0))],
        out_specs=pl.BlockSpec((tr, LANES), lambda i: (i, 0)), out_shape=_sds((S, LANES)),
        compiler_params=_params(("parallel",)),
    )(a, b, hoc)


def _attn_bwd(q, k, v, bias_t, datt, lse_rows, dsum_rows, d, name):
    S, D = q.shape
    nb = S // (d * ATT_BLK)
    H = D // HEAD_DIM
    W = _attn_width(d, D)
    HB = W // HEAD_DIM

    def body(q_ref, kp_ref, kc_ref, vp_ref, vc_ref, b_ref, do_ref, lse_ref, dsum_ref,
             dq_ref, dk_ref, dv_ref, db_ref, carry_k, carry_v):
        j = pl.program_id(0)
        n = pl.program_id(1)

        @pl.when(n == 0)
        def _():
            carry_k[...] = jnp.zeros_like(carry_k)
            carry_v[...] = jnp.zeros_like(carry_v)
            db_ref[...] = jnp.zeros_like(db_ref)

        @pl.when(n < nb)
        def _():
            key = lax.broadcasted_iota(jnp.int32, (2 * ATT_BLK, ATT_BLK), 0)
            keep = (key >= ATT_BLK) | (n > 0)
            first = lax.broadcasted_iota(jnp.int32, (1, LANES), 1) < HEAD_DIM

            def one(r):
                qs = (_rows(q_ref, r, d) * ATT_SCALE).astype(bf16)
                kcat = jnp.concatenate([_rows(kp_ref, r, d), _rows(kc_ref, r, d)], axis=0).astype(bf16)
                vcat = jnp.concatenate([_rows(vp_ref, r, d), _rows(vc_ref, r, d)], axis=0).astype(bf16)
                dob = _rows(do_ref, r, d).astype(bf16)
                dqs, dks, dvs = [], [], []
                for pair in range(W // LANES):
                    ps = slice(pair * LANES, (pair + 1) * LANES)
                    q2, k2, v2, do2 = qs[:, ps], kcat[:, ps], vcat[:, ps], dob[:, ps]
                    dq2 = jnp.zeros((ATT_BLK, LANES), f32)
                    dk2 = jnp.zeros((2 * ATT_BLK, LANES), f32)
                    dv2 = jnp.zeros((2 * ATT_BLK, LANES), f32)
                    for e in range(2):
                        h = 2 * pair + e
                        mine = first if e == 0 else jnp.logical_not(first)
                        zero = jnp.zeros((), bf16)
                        qm, dom, km = jnp.where(mine, q2, zero), jnp.where(mine, do2, zero), jnp.where(mine, k2, zero)
                        st = jnp.where(keep, _dot_nt(k2, qm) + b_ref[h], -jnp.inf)
                        pt = jnp.exp(st - lse_ref[r, j * HB + h])
                        dst = pt * (_dot_nt(v2, dom) - dsum_ref[r, j * HB + h])
                        db_ref[h] += dst
                        dv2 = dv2 + _dot(pt, dom)
                        dk2 = dk2 + _dot(dst, qm)
                        dq2 = dq2 + _dot_tn(dst, km)
                    dqs.append(dq2 * ATT_SCALE)
                    dks.append(dk2)
                    dvs.append(dv2)
                _set_rows(dq_ref, r, d, jnp.concatenate(dqs, axis=1))
                dk = jnp.concatenate(dks, axis=1)
                dv = jnp.concatenate(dvs, axis=1)
                _set_rows(dk_ref, r, d, carry_k[r] + dk[:ATT_BLK])
                _set_rows(dv_ref, r, d, carry_v[r] + dv[:ATT_BLK])
                carry_k[r] = dk[ATT_BLK:]
                carry_v[r] = dv[ATT_BLK:]

            _over_residues(d, one, unroll=2)

        @pl.when(n == nb)
        def _():
            def last(r):
                _set_rows(dk_ref, r, d, carry_k[r])
                _set_rows(dv_ref, r, d, carry_v[r])

            _over_residues(d, last)

    nq = lambda n: jnp.minimum(n, nb - 1)
    cur = pl.BlockSpec((ATT_BLK * d, W), lambda j, n: (nq(n), j))
    prev = pl.BlockSpec((ATT_BLK * d, W), lambda j, n: (jnp.maximum(nq(n) - 1, 0), j))
    done = pl.BlockSpec((ATT_BLK * d, W), lambda j, n: (jnp.maximum(n - 1, 0), j))
    bspec = pl.BlockSpec((HB, 2 * ATT_BLK, ATT_BLK), lambda j, n: (j, 0, 0))
    rows = pl.BlockSpec((None, d, H, 1, LANES), lambda j, n: (nq(n), 0, 0, 0, 0))
    return pl.pallas_call(
        body, name=name, grid=(D // W, nb + 1),
        in_specs=[cur, prev, cur, prev, cur, bspec, cur, rows, rows],
        out_specs=[cur, done, done, bspec],
        out_shape=[_sds((S, D)), _sds((S, D)), _sds((S, D)), _sds((H, 2 * ATT_BLK, ATT_BLK))],
        scratch_shapes=[pltpu.VMEM((d, ATT_BLK, W), f32), pltpu.VMEM((d, ATT_BLK, W), f32)],
        compiler_params=_params(("arbitrary", "arbitrary")),
    )(q, k, k, v, v, bias_t, datt, lse_rows, dsum_rows)


def _attn_combine(os_, lses, name):
    def fn(rv, vv):
        o0, o1, o2, l0, l1, l2 = rv
        m = jnp.maximum(jnp.maximum(l0, l1), l2)
        e0, e1, e2 = jnp.exp(l0 - m), jnp.exp(l1 - m), jnp.exp(l2 - m)
        tot = e0 + e1 + e2
        att = (e0 * o0 + e1 * o1 + e2 * o2) / tot
        return [att, att, m + jnp.log(tot)], []
    w = os_[0].shape[1]
    (att, att_b, lse), _ = _rowwise(name, fn, list(os_) + list(lses), [], [(w, f32), (w, bf16), (w, f32)], [], sub=16)
    return att, att_b, lse


ANY = pl.BlockSpec(memory_space=pl.ANY)


def _all_gather(vs, name):
    n = len(vs)

    def body(*refs):
        x_refs, out_refs = refs[:n], refs[n:2 * n]
        send_sems, recv_sems, local_sems = refs[2 * n:]
        x, y, c = lax.axis_index("x"), lax.axis_index("y"), lax.axis_index("c")
        me, sibling = (x, y, c), (x, y, 1 - c)
        chips = [(1 - x, y), (x, 1 - y), (1 - x, 1 - y)]

        def slot(i, px, py, pc):
            return out_refs[i].at[4 * px + 2 * py + pc]

        def copy(i, k, block, to, src=None):
            return pltpu.make_async_remote_copy(
                src_ref=slot(i, *block) if src is None else src, dst_ref=slot(i, *block),
                send_sem=send_sems.at[i, k], recv_sem=recv_sems.at[i, k], device_id=to, device_id_type=MESH)

        mine = [pltpu.make_async_copy(x_refs[i], slot(i, *me), local_sems.at[i]) for i in range(n)]
        for cp in mine:
            cp.start()
        first = []
        for i in range(n):
            first.append(copy(i, 0, me, sibling, src=x_refs[i]))
            first += [copy(i, 1 + j, me, (*chip, c), src=x_refs[i]) for j, chip in enumerate(chips)]
        for cp in first:
            cp.start()
        passed = []
        for i in range(n):
            for j, chip in enumerate(chips):
                copy(i, 1 + j, (*chip, c), me).wait_recv()
                cp = copy(i, 4 + j, (*chip, c), sibling)
                cp.start()
                passed.append(cp)
        for i in range(n):
            copy(i, 0, sibling, me).wait_recv()
            for j, chip in enumerate(chips):
                copy(i, 4 + j, (*chip, 1 - c), me).wait_recv()
        for cp in first + passed:
            cp.wait_send()
        for cp in mine:
            cp.wait()

    return pl.pallas_call(
        body, name=name, out_shape=[_sds((N_DEV,) + v.shape, v.dtype) for v in vs], in_specs=[ANY] * n,
        out_specs=[ANY] * n,
        scratch_shapes=[pltpu.SemaphoreType.DMA((n, 7)), pltpu.SemaphoreType.DMA((n, 7)), pltpu.SemaphoreType.DMA((n,))],
    )(*vs)


def _rs_sibling(parts, name):
    n = len(parts)

    def body(*refs):
        p_refs, out_refs = refs[:n], refs[n:2 * n]
        send_sems, recv_sems = refs[2 * n:]
        x, y, c = lax.axis_index("x"), lax.axis_index("y"), lax.axis_index("c")
        cps = [pltpu.make_async_remote_copy(
            src_ref=p_refs[i].at[k, 1 - c], dst_ref=out_refs[i].at[k], send_sem=send_sems.at[i, k],
            recv_sem=recv_sems.at[i, k], device_id=(x, y, 1 - c), device_id_type=MESH)
            for i in range(n) for k in range(4)]
        for cp in cps:
            cp.start()
        for cp in cps:
            cp.wait()

    return pl.pallas_call(
        body, name=name, out_shape=[_sds((4,) + p.shape[2:], p.dtype) for p in parts], in_specs=[ANY] * n,
        out_specs=[ANY] * n,
        scratch_shapes=[pltpu.SemaphoreType.DMA((n, 4)), pltpu.SemaphoreType.DMA((n, 4))],
    )(*parts)


def _rs_chips(ts, name):
    n = len(ts)

    def body(*refs):
        t_refs, out_refs = refs[:n], refs[n:2 * n]
        send_sems, recv_sems, local_sems = refs[2 * n:]
        x, y, c = lax.axis_index("x"), lax.axis_index("y"), lax.axis_index("c")
        mine = 2 * x + y
        local = [pltpu.make_async_copy(t_refs[i].at[mine], out_refs[i].at[mine], local_sems.at[i]) for i in range(n)]
        for cp in local:
            cp.start()
        chips = [(1 - x, y), (x, 1 - y), (1 - x, 1 - y)]
        cps = [pltpu.make_async_remote_copy(
            src_ref=t_refs[i].at[2 * px + py], dst_ref=out_refs[i].at[mine], send_sem=send_sems.at[i, j],
            recv_sem=recv_sems.at[i, j], device_id=(px, py, c), device_id_type=MESH)
            for i in range(n) for j, (px, py) in enumerate(chips)]
        for cp in cps:
            cp.start()
        for cp in cps:
            cp.wait()
        for cp in local:
            cp.wait()

    return pl.pallas_call(
        body, name=name, out_shape=[_sds(t.shape, t.dtype) for t in ts], in_specs=[ANY] * n, out_specs=[ANY] * n,
        scratch_shapes=[pltpu.SemaphoreType.DMA((n, 3)), pltpu.SemaphoreType.DMA((n, 3)), pltpu.SemaphoreType.DMA((n,))],
    )(*ts)


def _pair_add(part, recv, c_arr, name):
    _, _, R, C = part.shape
    tr = _tile(R, PACK_ROW_TILE, 16)

    def body(c_ref, p_ref, r_ref, o_ref):
        o_ref[...] = (p_ref[...] + r_ref[...]).astype(o_ref.dtype)

    return pl.pallas_call(
        body, name=name,
        grid_spec=pltpu.PrefetchScalarGridSpec(
            num_scalar_prefetch=1, grid=(4, R // tr),
            in_specs=[pl.BlockSpec((None, None, tr, C), lambda k, i, c_ref: (k, c_ref[0], i, 0)),
                      pl.BlockSpec((None, tr, C), lambda k, i, c_ref: (k, i, 0))],
            out_specs=pl.BlockSpec((None, tr, C), lambda k, i, c_ref: (k, i, 0))),
        out_shape=_sds((4, R, C), bf16),
        compiler_params=_params(("parallel", "parallel")),
    )(c_arr, part, recv)


def _sum_slots(t, name):
    n, R, C = t.shape
    tr = _tile(R, PACK_ROW_TILE, 16)

    def body(t_ref, o_ref):
        acc = t_ref[0].astype(f32)
        for k in range(1, n):
            acc = acc + t_ref[k].astype(f32)
        o_ref[...] = acc

    return pl.pallas_call(
        body, name=name, grid=(R // tr,),
        in_specs=[pl.BlockSpec((n, tr, C), lambda i: (0, i, 0))],
        out_specs=pl.BlockSpec((tr, C), lambda i: (i, 0)), out_shape=_sds((R, C)),
        compiler_params=_params(("parallel",)),
    )(t)


def _reduce_scatter(parts, c_arr, name):
    parts4 = [p.reshape((4, 2) + p.shape[1:]) for p in parts]
    recv = _rs_sibling(parts4, name + "_sibling")
    ts = [_pair_add(p, r, c_arr, f"{name}_pair_{i}") for i, (p, r) in enumerate(zip(parts4, recv))]
    got = _rs_chips(ts, name + "_chips")
    return [_sum_slots(g, f"{name}_sum_{i}") for i, g in enumerate(got)]


HBM_SPEC = pl.BlockSpec(memory_space=pltpu.HBM)
SEM_SPEC = pl.BlockSpec(memory_space=pltpu.SEMAPHORE)
EFFECT = pltpu.SideEffectType.DATAFLOW_SIDE_EFFECTING


def _mesh_pos(p):
    return (p // 4, (p // 2) % 2, p % 2)


def _exchange_copy(src_refs, land_refs, send_sems, recv_sems, whole, i, k, receiving):
    me = 4 * lax.axis_index("x") + 2 * lax.axis_index("y") + lax.axis_index("c")
    to = (me + k) % N_DEV
    frm = (me + N_DEV - k) % N_DEV
    src = src_refs[i] if whole else src_refs[i].at[to]
    s = i * (N_DEV - 1) + k - 1
    send = pltpu.make_async_remote_copy(src_ref=src, dst_ref=land_refs[i].at[me], send_sem=send_sems.at[s],
                                        recv_sem=recv_sems.at[s], device_id=_mesh_pos(to), device_id_type=MESH)
    if not receiving:
        return send
    return send, pltpu.make_async_remote_copy(src_ref=src, dst_ref=land_refs[i].at[frm], send_sem=send_sems.at[s],
                                              recv_sem=recv_sems.at[s], device_id=_mesh_pos(to), device_id_type=MESH)


def _exchange_start(srcs, whole, name, after=None):
    n = len(srcs)
    lands = [lax.empty((N_DEV,) + s.shape[-2:], s.dtype) for s in srcs]
    after = list(after or [])
    n_in = 2 * n + len(after)

    def body(*refs):
        src_refs, land_refs = refs[:n], refs[n:2 * n]
        send_sems, recv_sems, token = refs[n_in], refs[n_in + 1], refs[-1]
        for i in range(n):
            for k in range(1, N_DEV):
                _exchange_copy(src_refs, land_refs, send_sems, recv_sems, whole, i, k, False).start()
        token[...] = jnp.zeros_like(token)

    sems = pltpu.SemaphoreType.DMA((n * (N_DEV - 1),))
    outs = pl.pallas_call(
        body, name=name,
        out_shape=(sems, sems, *[pltpu.HBM(a.shape, a.dtype) for a in srcs + lands], _sds((8, LANES))),
        in_specs=[HBM_SPEC] * (2 * n) + [pl.BlockSpec(memory_space=pl.ANY)] * len(after),
        out_specs=(SEM_SPEC, SEM_SPEC, *[HBM_SPEC] * (2 * n), pl.BlockSpec(memory_space=pltpu.VMEM)),
        input_output_aliases={i: 2 + i for i in range(2 * n)},
        compiler_params=pltpu.CompilerParams(has_side_effects=EFFECT),
    )(*[pltpu.with_memory_space_constraint(a, pltpu.HBM) for a in srcs + lands], *after)
    return (outs[0], outs[1], list(outs[2:2 + n]), list(outs[2 + n:2 + 2 * n]), whole), outs[-1]


def _exchange_wait(handle, after, name):
    send_sems, recv_sems, srcs, lands, whole = handle
    n = len(srcs)

    def body(*refs):
        src_refs, land_refs = refs[:n], refs[n:2 * n]
        send_sems, recv_sems = refs[2 * n], refs[2 * n + 1]
        for i in range(n):
            for k in range(1, N_DEV):
                send, recv = _exchange_copy(src_refs, land_refs, send_sems, recv_sems, whole, i, k, True)
                send.wait_send()
                recv.wait_recv()

    outs = pl.pallas_call(
        body, name=name, out_shape=tuple(pltpu.HBM(a.shape, a.dtype) for a in srcs + lands),
        in_specs=[HBM_SPEC] * (2 * n) + [SEM_SPEC, SEM_SPEC, pl.BlockSpec(memory_space=pl.ANY)],
        out_specs=[HBM_SPEC] * (2 * n), input_output_aliases={i: i for i in range(2 * n)},
        compiler_params=pltpu.CompilerParams(has_side_effects=EFFECT),
    )(*srcs, *lands, send_sems, recv_sems, after)
    return list(outs[n:])


def _tie(v, token):
    return v + token[0:1, 0:1].astype(v.dtype).reshape((1,) * v.ndim)


def _with_own(land, own, me):
    return lax.dynamic_update_slice_in_dim(land, own[None].astype(land.dtype), me, 0)


class _Overlap:
    def __init__(self, shards, me, after):
        self.me = me
        self.names = list(shards)
        self.handle, self.token = _exchange_start([shards[nm] for nm in self.names], True, "weights_start", after)
        self.sent = {}

    def weights(self, after):
        lands = _exchange_wait(self.handle, after, "weights_wait")
        own = self.handle[2]
        return {nm: _full_from_blocks(nm, _with_own(land, o, self.me)) for nm, land, o in zip(self.names, lands, own)}

    def send(self, tag, grads, after=None):
        names = list(grads)
        handle, token = _exchange_start([_blocks_from_full(nm, grads[nm]) for nm in names], False, f"grads_start_{tag}",
                                        after)
        self.sent[tag] = (names, handle)
        return token

    def received(self, tag, after):
        names, handle = self.sent[tag]
        lands = _exchange_wait(handle, after, f"grads_wait_{tag}")
        own = [lax.dynamic_index_in_dim(b, self.me, 0, keepdims=False) for b in handle[2]]
        return {nm: _with_own(land, o, self.me) for nm, land, o in zip(names, lands, own)}


ADAM_ROWS = 32


def _adamw(w, g, m, v, name):
    R, C = w.shape
    cb = LANES if C % LANES == 0 else C

    def body(w_ref, g_ref, m_ref, v_ref, d_ref, m2_ref, v2_ref):
        def update(sl):
            gv = g_ref[sl, :]
            m2 = ADAM_B1 * m_ref[sl, :] + (1.0 - ADAM_B1) * gv
            v2 = ADAM_B2 * v_ref[sl, :] + (1.0 - ADAM_B2) * jnp.square(gv)
            m_hat = m2 / (1.0 - ADAM_B1 ** ADAM_STEP)
            v_hat = v2 / (1.0 - ADAM_B2 ** ADAM_STEP)
            d_ref[sl, :] = -ADAM_LR * (m_hat / (jnp.sqrt(v_hat) + ADAM_EPS) + ADAM_WD * w_ref[sl, :])
            m2_ref[sl, :] = m2
            v2_ref[sl, :] = v2

        main = R // ADAM_ROWS
        if main:
            lax.fori_loop(0, main, lambda i, c: (update(pl.ds(pl.multiple_of(i * ADAM_ROWS, ADAM_ROWS), ADAM_ROWS)), c)[1], 0)
        if R % ADAM_ROWS:
            update(pl.ds(main * ADAM_ROWS, R % ADAM_ROWS))

    spec = pl.BlockSpec((R, cb), lambda j: (0, j))
    return pl.pallas_call(
        body, name=name, grid=(C // cb,), in_specs=[spec] * 4, out_specs=[spec] * 3, out_shape=[_sds((R, C))] * 3,
        compiler_params=_params(("parallel",)),
    )(w, g, m, v)


BIG_PARAMS = ("hy_w_in", "hy_w_out", "cv_w_pw1", "cv_w_pw2", "ffn_w_gate", "ffn_w_up", "ffn_w_down")


def _shards_2d(w):
    t = lambda a: jnp.transpose(a)
    return dict(in_t=t(w["hy_w_in"][0]), out=w["hy_w_out"][0], pw1=w["cv_w_pw1"][0], pw2=w["cv_w_pw2"][0],
                gate_t0=t(w["ffn_w_gate"][0]), gate_t1=t(w["ffn_w_gate"][1]), up_t0=t(w["ffn_w_up"][0]),
                up_t1=t(w["ffn_w_up"][1]), down0=w["ffn_w_down"][0], down1=w["ffn_w_down"][1])


def _unshard_2d(s):
    t = lambda a: jnp.transpose(a)
    return dict(hy_w_in=t(s["in_t"])[None], hy_w_out=s["out"][None], cv_w_pw1=s["pw1"][None], cv_w_pw2=s["pw2"][None],
                ffn_w_gate=jnp.stack([t(s["gate_t0"]), t(s["gate_t1"])]),
                ffn_w_up=jnp.stack([t(s["up_t0"]), t(s["up_t1"])]), ffn_w_down=jnp.stack([s["down0"], s["down1"]]))


def _full_from_blocks(nm, g):
    if nm == "pw1":
        return jnp.transpose(g, (1, 0, 2)).reshape(g.shape[1], N_DEV * g.shape[2])
    return g.reshape(N_DEV * g.shape[1], g.shape[2])


def _blocks_from_full(nm, g):
    if nm == "pw1":
        return jnp.transpose(g.reshape(g.shape[0], N_DEV, g.shape[1] // N_DEV), (1, 0, 2))
    return g.reshape(N_DEV, g.shape[0] // N_DEV, g.shape[1])


class _VecPack:
    def __init__(self, shapes):
        self.shapes = [tuple(s) for s in shapes]
        self.sizes = [int(np.prod(s)) for s in self.shapes]
        total = sum(self.sizes)
        self.rows = -(-(-(-total // LANES)) // 8) * 8
        self.total = total

    def pack(self, arrays):
        flat = jnp.concatenate([a.astype(f32).reshape(-1) for a in arrays])
        flat = jnp.pad(flat, (0, self.rows * LANES - self.total))
        return flat.reshape(self.rows, LANES)

    def unpack(self, packed):
        flat = packed.reshape(-1)
        out, off = [], 0
        for shp, n in zip(self.shapes, self.sizes):
            out.append(flat[off:off + n].reshape(shp))
            off += n
        return out

    def unpack_stacked(self, stacked, only=None):
        flat = stacked.reshape(stacked.shape[0], -1)
        offs = np.concatenate([[0], np.cumsum(self.sizes)])
        get = lambda i: flat[:, offs[i]:offs[i + 1]].reshape((stacked.shape[0],) + self.shapes[i])
        return get(only) if only is not None else [get(i) for i in range(len(self.shapes))]


def _row(v):
    return v.reshape(1, -1)


def _pad_lanes(v):
    v = v.reshape(1, -1)
    return jnp.pad(v, ((0, 0), (0, LANES - v.shape[1])))


def _ffn_fwd(h, w_gate_t, w_up_t, w_down, tag):
    F = w_down.shape[0]
    a = _mm(h, w_gate_t, tb=True, name=f"ffn_gate_{tag}")
    u = _mm(h, w_up_t, tb=True, name=f"ffn_up_{tag}")
    (f,), _ = _rowwise(f"swiglu_{tag}", lambda rv, vv: ([_silu(rv[0]) * rv[1]], []), [a, u], [], [(F, bf16)], [], sub=16)
    out = _mm(f, w_down, name=f"ffn_down_{tag}")
    return out, (a, u, f)


def _ffn_bwd(h, w_gate_t, w_up_t, w_down, saved, dout, tag):
    a, u, f = saved
    F = w_down.shape[0]
    df = _mm(dout, w_down, tb=True, name=f"ffn_down_dx_{tag}")
    dw_down = _mm(f, dout, ta=True, out_dtype=bf16, name=f"ffn_down_dw_{tag}")

    def fn(rv, vv):
        _, vjp = jax.vjp(lambda a_, u_: _silu(a_) * u_, rv[0], rv[1])
        da, du = vjp(rv[2])
        return [da, du], []

    (da, du), _ = _rowwise(f"swiglu_bwd_{tag}", fn, [a, u, df], [], [(F, bf16), (F, bf16)], [], sub=16)
    dh = _mm(du, w_up_t, add=_mm(da, w_gate_t, name=f"ffn_gate_dx_{tag}"), name=f"ffn_up_dx_{tag}")
    dw_gate_t = _mm(da, h, ta=True, out_dtype=bf16, name=f"ffn_gate_dw_{tag}")
    dw_up_t = _mm(du, h, ta=True, out_dtype=bf16, name=f"ffn_up_dw_{tag}")
    return dh, dw_gate_t, dw_up_t, dw_down


def _local_step(x, target, mod, w_in_t, comm, small):
    S, D = x.shape
    di = small["hy_ssm_norm_g"].shape[-1]
    nh = small["hy_dt_bias"].shape[-1]
    cd = small["hy_conv_b"].shape[-1]
    m = [[_row(mod[i, j]) for j in range(6)] for i in range(2)]

    off_q = di + cd + nh
    w_qkv_t = w_in_t[off_q:]
    seg = dict(z=(w_in_t, 0, di), xbc=(w_in_t, di, cd), dt=(w_in_t, di + cd, LANES))
    for i, nm in enumerate(("q0", "q1", "q2", "k", "v")):
        seg[nm] = (w_qkv_t, i * D, D)

    g_mix = [_row(small["norm_mix_g"][i]) for i in range(2)]
    g_ffn = [_row(small["norm_ffn_g"][i]) for i in range(2)]
    conv_w, conv_b = small["hy_conv_w_full"], _row(small["hy_conv_b"][0])
    dt_bias, a_log, d_skip = (_pad_lanes(small[k][0]) for k in ("hy_dt_bias", "hy_a_log", "hy_d_skip"))
    g_ssm = _row(small["hy_ssm_norm_g"][0])
    onehot = jnp.asarray(_bucket_onehot())
    rel_t = small["rel_table"].T
    H = D // HEAD_DIM
    bias = [_exact_mm(rel_t[gi * H:(gi + 1) * H], onehot[gi], name=f"rel_bias_{gi}")
            .reshape(H, ATT_BLK, 2 * ATT_BLK) + _band_penalty() for gi in range(3)]

    h1 = _adaln_fwd(x, g_mix[0], m[0][1], m[0][0], "adaln_mix0")
    proj = {nm: _mm(h1, mat, tb=True, b_rows=(off, cnt), name=f"in_{nm}") for nm, (mat, off, cnt) in seg.items()}
    xbc_pre, xbc = _conv_fwd(proj["xbc"], conv_w, conv_b, silu=True, name="ssm_conv", tr=1024)
    y, hin = _ssd_fwd(xbc, proj["dt"], dt_bias, a_log, d_skip, di, "ssd_fwd")
    (yg,), _ = _rowwise("ssm_gate", lambda rv, vv: ([_gate_f(rv[0], rv[1], vv[0])], []),
                        [y, proj["z"]], [g_ssm], [(di, bf16)], [], sub=16)
    og = [_attn_fwd(proj[f"q{gi}"], proj["k"], proj["v"], bias[gi], d, f"attn_fwd_{gi}")
          for gi, d in enumerate(ATT_DILATIONS)]
    att, att_b, lse_tot = _attn_combine([a for a, _ in og], [b for _, b in og], "attn_combine")
    W = comm.weights(after=att_b)
    w_out_y, w_out_a = W["out"][:di], W["out"][di:]
    mix0 = _mm(att_b, w_out_a, add=_mm(yg, w_out_y, name="out_y"), name="out_a")
    x1 = _resid_fwd(x, m[0][2], mix0, "resid_mix0")
    h2 = _adaln_fwd(x1, g_ffn[0], m[0][4], m[0][3], "adaln_ffn0")
    f0, ffn0_saved = _ffn_fwd(h2, W["gate_t0"], W["up_t0"], W["down0"], "0")
    x2 = _resid_fwd(x1, m[0][5], f0, "resid_ffn0")

    h3 = _adaln_fwd(x2, g_mix[1], m[1][1], m[1][0], "adaln_mix1")
    pw1 = _mm(h3, W["pw1"], bias=_row(small["cv_b_pw1_full"]), name="cv_pw1")
    (u,), _ = _rowwise("cv_glu", lambda rv, vv: ([rv[0] * jax.nn.sigmoid(rv[1])], []),
                       [(pw1, 0, D), (pw1, 1, D)], [], [(D, f32)], [])
    (u2,) = _conv_fwd(u, small["cv_w_dw_full"], _row(small["cv_b_dw_full"]), silu=False, name="cv_dw")
    ln_g, ln_b = _row(small["cv_ln_g_full"]), _row(small["cv_ln_b_full"])
    (u3,), _ = _rowwise("cv_lnsilu", lambda rv, vv: ([_lnsilu_f(rv[0], vv[0], vv[1])], []),
                        [u2], [ln_g, ln_b], [(D, bf16)], [], sub=16)
    mix1 = _mm(u3, W["pw2"], bias=_row(small["cv_b_pw2_full"]), name="cv_pw2")
    x3 = _resid_fwd(x2, m[1][2], mix1, "resid_mix1")
    h4 = _adaln_fwd(x3, g_ffn[1], m[1][4], m[1][3], "adaln_ffn1")
    f1, ffn1_saved = _ffn_fwd(h4, W["gate_t1"], W["up_t1"], W["down1"], "1")
    x4 = _resid_fwd(x3, m[1][5], f1, "resid_ffn1")

    g_fin = _row(small["final_norm_g"])

    def final_fn(rv, vv):
        xv, tv = rv
        yv, vjp = jax.vjp(_rms, xv, vv[0])
        err = yv - tv
        dx, dg = vjp(err / D)
        part = 0.5 * jnp.sum(jnp.mean(err * err, -1, keepdims=True), 0, keepdims=True)
        return [dx], [dg, jnp.broadcast_to(part, (1, LANES))]

    (dx4,), (d_fin, loss) = _rowwise("loss_head", final_fn, [x4, target], [g_fin], [(D, f32)], [D, LANES])

    dmod = [[None] * 6 for _ in range(2)]
    d_norm_mix, d_norm_ffn = [None, None], [None, None]
    big = {}

    df1, (dmod[1][5], _) = _resid_bwd(dx4, f1, m[1][5], "resid_ffn1_bwd")
    dh4, big["gate_t1"], big["up_t1"], big["down1"] = _ffn_bwd(h4, W["gate_t1"], W["up_t1"], W["down1"], ffn1_saved, df1, "1")
    dx3, (d_norm_ffn[1], dmod[1][4], dmod[1][3]) = _adaln_bwd(x3, g_ffn[1], m[1][4], m[1][3], dh4, dx4, "adaln_ffn1_bwd")
    dmix1, (dmod[1][2], d_b_pw2) = _resid_bwd(dx3, mix1, m[1][2], "resid_mix1_bwd")
    du3 = _mm(dmix1, W["pw2"], tb=True, name="cv_pw2_dx")
    big["pw2"] = _mm(u3, dmix1, ta=True, out_dtype=bf16, name="cv_pw2_dw")

    def lnsilu_bwd(rv, vv):
        _, vjp = jax.vjp(_lnsilu_f, rv[0], vv[0], vv[1])
        du, dg, db = vjp(rv[1])
        return [du], [dg, db]

    (du2,), (d_ln_g, d_ln_b) = _rowwise("cv_lnsilu_bwd", lnsilu_bwd, [u2, du3], [ln_g, ln_b], [(D, f32)], [D, D])
    du, d_w_dw, d_b_dw = _conv_bwd(u, small["cv_w_dw_full"], du2, None, silu=False, name="cv_dw_bwd")

    def glu_bwd(rv, vv):
        a, gt, d = rv
        _, vjp = jax.vjp(lambda a_, g_: a_ * jax.nn.sigmoid(g_), a, gt)
        da, dg = vjp(d)
        return [da, dg], [jnp.sum(da, 0, keepdims=True), jnp.sum(dg, 0, keepdims=True)]

    (dpa, dpg), (d_b1a, d_b1g) = _rowwise("cv_glu_bwd", glu_bwd, [(pw1, 0, D), (pw1, 1, D), du], [],
                                           [(D, bf16), (D, bf16)], [D, D], sub=16)
    dpw1 = jnp.concatenate([dpa, dpg], axis=1)
    d_b_pw1 = jnp.concatenate([d_b1a, d_b1g], axis=1)
    dh3 = _mm(dpw1, W["pw1"], tb=True, name="cv_pw1_dx")
    big["pw1"] = _mm(h3, dpw1, ta=True, out_dtype=bf16, name="cv_pw1_dw")
    token = comm.send("layer1", {nm: big[nm] for nm in ("gate_t1", "up_t1", "down1", "pw2", "pw1")})
    dx2, (d_norm_mix[1], dmod[1][1], dmod[1][0]) = _adaln_bwd(x2, g_mix[1], m[1][1], _tie(m[1][0], token), dh3, dx3,
                                                              "adaln_mix1_bwd")

    df0, (dmod[0][5], _) = _resid_bwd(dx2, f0, m[0][5], "resid_ffn0_bwd")
    dh2, big["gate_t0"], big["up_t0"], big["down0"] = _ffn_bwd(h2, W["gate_t0"], W["up_t0"], W["down0"], ffn0_saved, df0, "0")
    dx1, (d_norm_ffn[0], dmod[0][4], dmod[0][3]) = _adaln_bwd(x1, g_ffn[0], m[0][4], m[0][3], dh2, dx2, "adaln_ffn0_bwd")
    dmix0, (dmod[0][2], _) = _resid_bwd(dx1, mix0, m[0][2], "resid_mix0_bwd")
    dyg = _mm(dmix0, w_out_y, tb=True, name="out_y_dx")
    datt = _mm(dmix0, w_out_a, tb=True, name="out_a_dx")
    big["out"] = jnp.concatenate([_mm(yg, dmix0, ta=True, out_dtype=bf16, name="out_y_dw"),
                                  _mm(att_b, dmix0, ta=True, out_dtype=bf16, name="out_a_dw")], axis=0)
    token = comm.send("layer0", {nm: big[nm] for nm in ("gate_t0", "up_t0", "down0", "out")})
    bias = [_tie(b, token) for b in bias]
    g_ssm = _tie(g_ssm, token)

    dq, dks, dvs, dbs = [], [], [], []
    lse_heads = lse_tot[:, ::HEAD_DIM]
    dsum_heads = _head_sums(att, datt, "attn_dsum")[:, :H]
    for gi, d in enumerate(ATT_DILATIONS):
        a, b, c_, e = _attn_bwd(proj[f"q{gi}"], proj["k"], proj["v"], jnp.transpose(bias[gi], (0, 2, 1)), datt,
                                _by_block(lse_heads, d), _by_block(dsum_heads, d), d, f"attn_bwd_{gi}")
        dq.append(a)
        dks.append(b)
        dvs.append(c_)
        dbs.append(jnp.transpose(e, (0, 2, 1)))
    dk = _add3(*dks, "attn_dk")
    dv = _add3(*dvs, "attn_dv")
    d_rel = jnp.concatenate(
        [_exact_mm(dbs[gi].reshape(H, -1), onehot[gi], tb=True, name=f"rel_grad_{gi}") for gi in range(3)], axis=0).T

    def gate_bwd(rv, vv):
        _, vjp = jax.vjp(_gate_f, rv[0], rv[1], vv[0])
        dy_, dz_, dg_ = vjp(rv[2])
        return [dy_, dz_], [dg_]

    (dy, dz), (d_g_ssm,) = _rowwise("ssm_gate_bwd", gate_bwd, [y, proj["z"], dyg], [g_ssm], [(di, f32), (di, bf16)], [di],
                                    sub=16)
    dxbc, ddtraw, d_a_log, d_dskip, d_dt_bias = _ssd_bwd(xbc, proj["dt"], dt_bias, a_log, d_skip, hin, y, dy, di, "ssd_bwd")
    dxbc_pre, d_conv_w, d_conv_b = _conv_bwd(proj["xbc"], conv_w, dxbc, xbc_pre, silu=True, name="ssm_conv_bwd",
                                             dx_dtype=bf16, tr=1024)

    dseg = {"z": dz, "xbc": dxbc_pre, "dt": ddtraw, "q0": dq[0], "q1": dq[1], "q2": dq[2], "k": dk, "v": dv}
    dh1 = None
    d_in_parts = []
    for nm, (mat, off, cnt) in seg.items():
        dh1 = _mm(dseg[nm], mat, b_rows=(off, cnt), add=dh1, name=f"in_{nm}_dx")
        dwp = _mm(dseg[nm], h1, ta=True, out_dtype=bf16, name=f"in_{nm}_dw")
        d_in_parts.append(dwp[:nh] if nm == "dt" else dwp)
    big["in_t"] = jnp.concatenate(d_in_parts, axis=0)
    dx0, (d_norm_mix[0], dmod[0][1], dmod[0][0]) = _adaln_bwd(x, g_mix[0], m[0][1], m[0][0], dh1, dx1, "adaln_mix0_bwd")

    smallg = dict(
        loss=loss, dmod=jnp.stack([jnp.concatenate(dmod[i], axis=1)[0] for i in range(2)]),
        norm_mix_g=jnp.concatenate(d_norm_mix, axis=0), norm_ffn_g=jnp.concatenate(d_norm_ffn, axis=0),
        hy_conv_w=d_conv_w, hy_conv_b=d_conv_b, hy_dt_bias=d_dt_bias[:, :nh], hy_a_log=d_a_log[:, :nh],
        hy_d_skip=d_dskip[:, :nh], hy_ssm_norm_g=d_g_ssm, rel_table=d_rel,
        cv_b_pw1=d_b_pw1, cv_w_dw=d_w_dw, cv_b_dw=d_b_dw, cv_ln_g=d_ln_g, cv_ln_b=d_ln_b, cv_b_pw2=d_b_pw2,
        final_norm_g=d_fin)
    return dx0, big["in_t"], smallg


SMALL_GRAD_ORDER = ("loss", "dmod", "norm_mix_g", "norm_ffn_g", "hy_conv_w", "hy_conv_b", "hy_dt_bias", "hy_a_log",
                    "hy_d_skip", "hy_ssm_norm_g", "rel_table", "cv_b_pw1", "cv_w_dw", "cv_b_dw", "cv_ln_g", "cv_ln_b",
                    "cv_b_pw2", "final_norm_g")


def kernel(x, c, ada_w, ada_b, norm_mix_g, norm_ffn_g, hy_w_in, hy_conv_w, hy_conv_b, hy_dt_bias, hy_a_log, hy_d_skip, hy_ssm_norm_g, hy_w_out, rel_table, cv_w_pw1, cv_b_pw1, cv_w_dw, cv_b_dw, cv_ln_g, cv_ln_b, cv_w_pw2, cv_b_pw2, ffn_w_gate, ffn_w_up, ffn_w_down, final_norm_g, loss_target, m_ada_w, m_ada_b, m_norm_mix_g, m_norm_ffn_g, m_hy_w_in, m_hy_conv_w, m_hy_conv_b, m_hy_dt_bias, m_hy_a_log, m_hy_d_skip, m_hy_ssm_norm_g, m_hy_w_out, m_rel_table, m_cv_w_pw1, m_cv_b_pw1, m_cv_w_dw, m_cv_b_dw, m_cv_ln_g, m_cv_ln_b, m_cv_w_pw2, m_cv_b_pw2, m_ffn_w_gate, m_ffn_w_up, m_ffn_w_down, m_final_norm_g, v_ada_w, v_ada_b, v_norm_mix_g, v_norm_ffn_g, v_hy_w_in, v_hy_conv_w, v_hy_conv_b, v_hy_dt_bias, v_hy_a_log, v_hy_d_skip, v_hy_ssm_norm_g, v_hy_w_out, v_rel_table, v_cv_w_pw1, v_cv_b_pw1, v_cv_w_dw, v_cv_b_dw, v_cv_ln_g, v_cv_ln_b, v_cv_w_pw2, v_cv_b_pw2, v_ffn_w_gate, v_ffn_w_up, v_ffn_w_down, v_final_norm_g):
    names = ("ada_w", "ada_b", "norm_mix_g", "norm_ffn_g", "hy_w_in", "hy_conv_w", "hy_conv_b", "hy_dt_bias", "hy_a_log",
             "hy_d_skip", "hy_ssm_norm_g", "hy_w_out", "rel_table", "cv_w_pw1", "cv_b_pw1", "cv_w_dw", "cv_b_dw", "cv_ln_g",
             "cv_ln_b", "cv_w_pw2", "cv_b_pw2", "ffn_w_gate", "ffn_w_up", "ffn_w_down", "final_norm_g")
    w = dict(zip(names, (ada_w, ada_b, norm_mix_g, norm_ffn_g, hy_w_in, hy_conv_w, hy_conv_b, hy_dt_bias, hy_a_log, hy_d_skip,
                         hy_ssm_norm_g, hy_w_out, rel_table, cv_w_pw1, cv_b_pw1, cv_w_dw, cv_b_dw, cv_ln_g, cv_ln_b, cv_w_pw2,
                         cv_b_pw2, ffn_w_gate, ffn_w_up, ffn_w_down, final_norm_g)))
    mom = dict(zip(names, (m_ada_w, m_ada_b, m_norm_mix_g, m_norm_ffn_g, m_hy_w_in, m_hy_conv_w, m_hy_conv_b, m_hy_dt_bias,
                           m_hy_a_log, m_hy_d_skip, m_hy_ssm_norm_g, m_hy_w_out, m_rel_table, m_cv_w_pw1, m_cv_b_pw1, m_cv_w_dw,
                           m_cv_b_dw, m_cv_ln_g, m_cv_ln_b, m_cv_w_pw2, m_cv_b_pw2, m_ffn_w_gate, m_ffn_w_up, m_ffn_w_down,
                           m_final_norm_g)))
    vel = dict(zip(names, (v_ada_w, v_ada_b, v_norm_mix_g, v_norm_ffn_g, v_hy_w_in, v_hy_conv_w, v_hy_conv_b, v_hy_dt_bias,
                           v_hy_a_log, v_hy_d_skip, v_hy_ssm_norm_g, v_hy_w_out, v_rel_table, v_cv_w_pw1, v_cv_b_pw1, v_cv_w_dw,
                           v_cv_b_dw, v_cv_ln_g, v_cv_ln_b, v_cv_w_pw2, v_cv_b_pw2, v_ffn_w_gate, v_ffn_w_up, v_ffn_w_down,
                           v_final_norm_g)))
    S, D = x.shape[1], x.shape[2]
    ax, ay, ac = lax.axis_index("x"), lax.axis_index("y"), lax.axis_index("c")
    me = 4 * ax + 2 * ay + ac
    c_arr = jnp.reshape(ac, (1,)).astype(jnp.int32)
    nmod = ada_w.shape[2]

    w2 = _shards_2d(w)
    big_names = list(w2)
    (g_in,) = _all_gather([w2["in_t"].astype(bf16)], "gather_w_in")
    w_in_t = _full_from_blocks("in_t", g_in)

    sharded_small = ("hy_conv_w", "cv_b_pw1", "cv_w_dw", "cv_b_dw", "cv_ln_g", "cv_ln_b", "cv_b_pw2")
    vp = _VecPack([c.shape] + [w[nm].shape for nm in sharded_small])
    (sg,) = _all_gather([vp.pack([c] + [w[nm] for nm in sharded_small])], "gather_vectors")
    parts = vp.unpack_stacked(sg)
    c_all = parts[0][:, 0]
    small = {k: w[k] for k in ("norm_mix_g", "norm_ffn_g", "hy_conv_b", "hy_dt_bias", "hy_a_log", "hy_d_skip",
                               "hy_ssm_norm_g", "rel_table", "final_norm_g")}
    for p, nm in zip(parts[1:], sharded_small):
        p = p[:, 0]
        p = jnp.moveaxis(p, 0, -2)
        small[nm + "_full"] = p.reshape(p.shape[:-2] + (N_DEV * p.shape[-1],))

    (cs_all,), _ = _rowwise("ada_silu", lambda rv, vv: ([_silu(rv[0])], []), [c_all], [], [(D, f32)], [])
    b_mine = lax.dynamic_slice_in_dim(ada_b, me * nmod, nmod, axis=1)
    mod_part = jnp.stack([_mm(cs_all, ada_w[i], bias=b_mine[i:i + 1], name=f"ada_mod_{i}") for i in range(2)])
    (mod_all,) = _all_gather([mod_part.reshape(2 * N_DEV, nmod)], "gather_mod")
    mod_all = mod_all.reshape(N_DEV, 2, N_DEV, nmod)
    mod_mine = lax.dynamic_index_in_dim(mod_all, me, axis=2, keepdims=False)
    mod = jnp.transpose(mod_mine, (1, 0, 2)).reshape(2, 6, D)
    comm = _Overlap({nm: w2[nm].astype(bf16) for nm in big_names if nm != "in_t"}, me, after=[mod, w_in_t])
    mod = _tie(mod, comm.token)

    dx0, d_in_t, sgrad = _local_step(x[0], loss_target[0], mod, w_in_t, comm, small)
    comm.send("in", {"in_t": d_in_t})

    gp = _VecPack([sgrad[k].shape for k in SMALL_GRAD_ORDER])
    (g_all,) = _all_gather([gp.pack([sgrad[k] for k in SMALL_GRAD_ORDER])], "gather_small_grads")
    tot = dict(zip(SMALL_GRAD_ORDER, gp.unpack(_sum_slots(g_all, "sum_small_grads"))))
    dmod_all = gp.unpack_stacked(g_all, only=SMALL_GRAD_ORDER.index("dmod"))
    loss = tot["loss"][0, 0]

    grads = {}
    dmod_mine = lax.dynamic_slice_in_dim(dmod_all, me * nmod, nmod, axis=2)
    grads["ada_w"] = jnp.stack([_mm(cs_all, dmod_mine[:, i], ta=True, name=f"ada_w_grad_{i}") for i in range(2)])
    grads["ada_b"] = tot["dmod"]
    grads["norm_mix_g"], grads["norm_ffn_g"] = tot["norm_mix_g"], tot["norm_ffn_g"]
    grads["hy_conv_b"] = tot["hy_conv_b"]
    grads["hy_dt_bias"] = tot["hy_dt_bias"]
    grads["hy_a_log"] = tot["hy_a_log"]
    grads["hy_d_skip"] = tot["hy_d_skip"]
    grads["hy_ssm_norm_g"] = tot["hy_ssm_norm_g"]
    grads["rel_table"] = tot["rel_table"]
    grads["final_norm_g"] = tot["final_norm_g"][0]
    for nm in sharded_small:
        n = w[nm].shape[-1]
        grads[nm] = lax.dynamic_slice_in_dim(tot[nm], me * n, n, axis=1).reshape(w[nm].shape)

    delta, new_m, new_v = {}, {}, {}
    shp = ada_w.shape
    two = lambda t: t.reshape(-1, shp[-1])
    d_, m_, v_ = _adamw(two(ada_w), two(grads["ada_w"]), two(m_ada_w), two(v_ada_w), "adamw_ada_w")
    delta["ada_w"], new_m["ada_w"], new_v["ada_w"] = d_.reshape(shp), m_.reshape(shp), v_.reshape(shp)
    rest = [nm for nm in names if nm not in BIG_PARAMS and nm != "ada_w"]
    sp = _VecPack([w[nm].shape for nm in rest])
    packs = [sp.pack([t[nm] for nm in rest]) for t in (w, grads, mom, vel)]
    ds_, ms_, vs_ = _adamw(*packs, "adamw_small")
    for nm, a, b, e in zip(rest, sp.unpack(ds_), sp.unpack(ms_), sp.unpack(vs_)):
        delta[nm], new_m[nm], new_v[nm] = a, b, e

    g2 = {}
    after = d_
    for tag in ("layer1", "layer0", "in"):
        for nm, slots in comm.received(tag, after).items():
            g2[nm] = _sum_slots(slots, f"sum_{nm}")
            after = g2[nm]
    grads.update(_unshard_2d(g2))
    m2, v2 = _shards_2d(mom), _shards_2d(vel)
    d2, nm2, nv2 = {}, {}, {}
    for nm in big_names:
        d2[nm], nm2[nm], nv2[nm] = _adamw(w2[nm], g2[nm], m2[nm], v2[nm], f"adamw_{nm}")
    delta.update(_unshard_2d(d2))
    new_m.update(_unshard_2d(nm2))
    new_v.update(_unshard_2d(nv2))

    return (loss, dx0[None], *[grads[n] for n in names], *[delta[n] for n in names],
            *[new_m[n] for n in names], *[new_v[n] for n in names])
```

```python
import functools
import math

import numpy as np
import jax
import jax.numpy as jnp
from jax import lax
from jax.experimental import pallas as pl
from jax.experimental.pallas import tpu as pltpu

f32 = jnp.float32
bf16 = jnp.bfloat16
EPS = 1e-6
N_DEV = 8
LANES = 128
SSM_STATE = 128
SSM_CHUNK = 128
SSM_GROUPS = 4
HEAD_DIM = 64
ATT_BLK = 128
ATT_DILATIONS = (1, 4, 16)
REL_BUCKETS = 32
REL_MAX_DIST = 2048
ADAM_LR, ADAM_B1, ADAM_B2, ADAM_EPS, ADAM_WD, ADAM_STEP = 0.001, 0.9, 0.999, 1e-08, 0.01, 10
PACK_COLS = 1024
PACK_ROW_TILE = 256
MESH = pl.DeviceIdType.MESH
VMEM_LIMIT = 48 * 1024 * 1024


def _sds(shape, dtype=f32):
    return jax.ShapeDtypeStruct(tuple(shape), dtype)


def _tile(n, cap, mult):
    best = None
    t = mult
    while t <= min(n, cap):
        if n % t == 0:
            best = t
        t += mult
    return best if best is not None else n


def _params(sem):
    return pltpu.CompilerParams(dimension_semantics=sem, vmem_limit_bytes=VMEM_LIMIT)


def _mm(a, b, *, name, ta=False, tb=False, b_rows=None, bias=None, add=None, out_dtype=f32,
        tm_cap=512, tn_cap=1536, tk_cap=8192):
    if ta:
        K, M = a.shape
    else:
        M, K = a.shape
    off, cnt = b_rows if b_rows is not None else (0, b.shape[0])
    if tb:
        N, K2 = cnt, b.shape[1]
    else:
        K2, N = cnt, b.shape[1]
    assert K == K2, (a.shape, b.shape, ta, tb, b_rows)
    if ta and a.dtype == f32:
        tm_cap = min(tm_cap, 256)
    tm = _tile(M, tm_cap, LANES)
    tn = _tile(math.gcd(off, N) if tb else N, tn_cap, LANES)
    tk = _tile(K if tb else math.gcd(off, K), tk_cap, LANES)
    assert N % tn == 0 and K % tk == 0 and off % (tn if tb else tk) == 0, (name, off, N, K, tn, tk)
    nk = K // tk
    jo, ko = (off // tn, 0) if tb else (0, off // tk)
    has_bias, has_add = bias is not None, add is not None
    dn = (((0 if ta else 1,), (1 if tb else 0,)), ((), ()))

    def body(*refs):
        a_ref, b_ref = refs[0], refs[1]
        pos = 2
        bias_ref = add_ref = None
        if has_bias:
            bias_ref = refs[pos]
            pos += 1
        if has_add:
            add_ref = refs[pos]
            pos += 1
        o_ref = refs[pos]
        k = pl.program_id(2)
        part = lax.dot_general(a_ref[...].astype(bf16), b_ref[...].astype(bf16), dn, preferred_element_type=f32)

        def finish(r):
            if has_bias:
                r = r + bias_ref[...]
            if has_add:
                r = r + add_ref[...]
            o_ref[...] = r.astype(o_ref.dtype)

        if nk == 1:
            finish(part)
        else:
            acc_ref = refs[pos + 1]

            @pl.when(k == 0)
            def _():
                acc_ref[...] = part

            @pl.when((k > 0) & (k < nk - 1))
            def _():
                acc_ref[...] += part

            @pl.when(k == nk - 1)
            def _():
                finish(acc_ref[...] + part)

    in_specs = [
        pl.BlockSpec((tk, tm), lambda i, j, k: (k, i)) if ta else pl.BlockSpec((tm, tk), lambda i, j, k: (i, k)),
        pl.BlockSpec((tn, tk), lambda i, j, k: (j + jo, k)) if tb else pl.BlockSpec((tk, tn), lambda i, j, k: (k + ko, j)),
    ]
    args = [a, b]
    if has_bias:
        in_specs.append(pl.BlockSpec((1, tn), lambda i, j, k: (0, j)))
        args.append(bias)
    if has_add:
        in_specs.append(pl.BlockSpec((tm, tn), lambda i, j, k: (i, j)))
        args.append(add)
    return pl.pallas_call(
        body, name=name, grid=(M // tm, N // tn, nk), in_specs=in_specs,
        out_specs=pl.BlockSpec((tm, tn), lambda i, j, k: (i, j)), out_shape=_sds((M, N), out_dtype),
        scratch_shapes=[pltpu.VMEM((tm, tn), f32)] if nk > 1 else [],
        compiler_params=_params(("parallel", "parallel", "arbitrary")),
    )(*args)


def _rowwise(name, fn, rows, vecs, out_rows, out_accs, *, tr_cap=256, sub=8):
    rows = [r if isinstance(r, tuple) else (r, 0, r.shape[1]) for r in rows]
    R = rows[0][0].shape[0]
    tr = _tile(R, tr_cap, 8)
    sub = sub if tr % sub == 0 else tr
    n_r, n_v, n_or, n_oa = len(rows), len(vecs), len(out_rows), len(out_accs)

    def body(*refs):
        row_refs = refs[:n_r]
        vec_refs = refs[n_r:n_r + n_v]
        orow_refs = refs[n_r + n_v:n_r + n_v + n_or]
        oacc_refs = refs[n_r + n_v + n_or:]
        vv = [r[...] for r in vec_refs]

        n_sub = tr // sub
        together = 4 if n_sub % 4 == 0 else 1

        def step(s, accs):
            for t in range(together):
                sl = pl.ds(pl.multiple_of((s * together + t) * sub, sub), sub)
                ro, ao = fn([r[sl, :] for r in row_refs], vv)
                for o_ref, o in zip(orow_refs, ro):
                    o_ref[sl, :] = o.astype(o_ref.dtype)
                accs = tuple(x + y for x, y in zip(accs, ao))
            return accs

        accs = lax.fori_loop(0, n_sub // together, step, tuple(jnp.zeros((1, w), f32) for w in out_accs))
        if n_oa:
            @pl.when(pl.program_id(0) == 0)
            def _():
                for ref in oacc_refs:
                    ref[...] = jnp.zeros_like(ref)

            for ref, x in zip(oacc_refs, accs):
                ref[...] += x

    in_specs = [pl.BlockSpec((tr, w), functools.partial(lambda i, cb: (i, cb), cb=cb)) for (_, cb, w) in rows]
    in_specs += [pl.BlockSpec((1, v.shape[1]), lambda i: (0, 0)) for v in vecs]
    out_specs = [pl.BlockSpec((tr, w), lambda i: (i, 0)) for (w, _) in out_rows]
    out_specs += [pl.BlockSpec((1, w), lambda i: (0, 0)) for w in out_accs]
    out_shape = [_sds((R, w), dt) for (w, dt) in out_rows] + [_sds((1, w)) for w in out_accs]
    res = pl.pallas_call(
        body, name=name, grid=(R // tr,), in_specs=in_specs, out_specs=out_specs, out_shape=out_shape,
        compiler_params=_params(("arbitrary",)),
    )(*[r[0] for r in rows], *vecs)
    return res[:n_or], res[n_or:]


def _silu(x):
    return x * jax.nn.sigmoid(x)


def _rms(x, g):
    return x * lax.rsqrt(jnp.mean(x * x, -1, keepdims=True) + EPS) * g


def _adaln_f(x, g, sc, sh):
    return _rms(x, g) * (1.0 + sc) + sh


def _gate_f(y, z, g):
    return _rms(y * _silu(z), g)


def _lnsilu_f(u, g, b):
    mu = jnp.mean(u, -1, keepdims=True)
    var = jnp.mean(jnp.square(u - mu), -1, keepdims=True)
    return _silu((u - mu) * lax.rsqrt(var + EPS) * g + b)


def _adaln_fwd(x, g, sc, sh, name):
    (h,), _ = _rowwise(name, lambda rv, vv: ([_adaln_f(rv[0], *vv)], []), [x], [g, sc, sh], [(x.shape[1], bf16)], [],
                       sub=16)
    return h


def _adaln_bwd(x, g, sc, sh, dh, dres, name):
    def fn(rv, vv):
        xv, dhv, drv = rv
        _, vjp = jax.vjp(_adaln_f, xv, *vv)
        dx, dg, dsc, dsh = vjp(dhv)
        return [dx + drv], [dg, dsc, dsh]
    w = x.shape[1]
    (dx,), accs = _rowwise(name, fn, [x, dh, dres], [g, sc, sh], [(w, f32)], [w, w, w])
    return dx, accs


def _resid_fwd(x, gate, mix, name):
    (y,), _ = _rowwise(name, lambda rv, vv: ([rv[0] + vv[0] * rv[1]], []), [x, mix], [gate], [(x.shape[1], f32)], [])
    return y


def _resid_bwd(dx, mix, gate, name):
    def fn(rv, vv):
        dxv, mv = rv
        dm = vv[0] * dxv
        return [dm], [jnp.sum(dxv * mv, 0, keepdims=True), jnp.sum(dm, 0, keepdims=True)]
    w = dx.shape[1]
    (dmix,), accs = _rowwise(name, fn, [dx, mix], [gate], [(w, bf16)], [w, w], sub=16)
    return dmix, accs


def _add3(a, b, c, name):
    (y,), _ = _rowwise(name, lambda rv, vv: ([rv[0] + rv[1] + rv[2]], []), [a, b, c], [], [(a.shape[1], bf16)], [],
                       sub=16)
    return y


CONV_HALO = 32
CONV_ROWS = 64


def _conv_fwd(x, w, b, *, silu, name, tr=512):
    S, C = x.shape
    K = w.shape[0]
    H = CONV_HALO
    assert K - 1 <= H and S % tr == 0 and tr % H == 0 and C % LANES == 0
    nh = tr // H

    def body(xp_ref, xc_ref, w_ref, b_ref, *rest):
        outs, scr = rest[:-1], rest[-1]
        i = pl.program_id(1)
        scr[pl.ds(0, H), :] = jnp.where(i > 0, xp_ref[...], 0.0)
        scr[pl.ds(H, tr), :] = xc_ref[...]
        taps = [w_ref[pl.ds(k, 1), :] for k in range(K)]
        for c0 in range(0, tr, CONV_ROWS):
            acc = jnp.zeros((CONV_ROWS, LANES), f32) + b_ref[...]
            for k in range(K):
                acc = acc + scr[pl.ds(c0 + H - (K - 1) + k, CONV_ROWS), :] * taps[k]
            outs[0][pl.ds(c0, CONV_ROWS), :] = acc
            if silu:
                outs[1][pl.ds(c0, CONV_ROWS), :] = _silu(acc)

    n_out = 2 if silu else 1
    return pl.pallas_call(
        body, name=name, grid=(C // LANES, S // tr),
        in_specs=[pl.BlockSpec((H, LANES), lambda j, i: (jnp.maximum(i * nh - 1, 0), j)),
                  pl.BlockSpec((tr, LANES), lambda j, i: (i, j)),
                  pl.BlockSpec((K, LANES), lambda j, i: (0, j)),
                  pl.BlockSpec((1, LANES), lambda j, i: (0, j))],
        out_specs=[pl.BlockSpec((tr, LANES), lambda j, i: (i, j))] * n_out,
        out_shape=[_sds((S, C))] * n_out,
        scratch_shapes=[pltpu.VMEM((tr + H, LANES), f32)],
        compiler_params=_params(("parallel", "arbitrary")),
    )(x, x, w, b)


def _conv_bwd(x, w, dact, pre, *, silu, name, dx_dtype=f32, tr=512):
    S, C = x.shape
    K = w.shape[0]
    H = CONV_HALO
    nh = tr // H
    n_i = S // tr
    kp = -(-K // 8) * 8

    def dsilu(p):
        s = jax.nn.sigmoid(p)
        return s * (1.0 + p * (1.0 - s))

    def body(*refs):
        if silu:
            xp_ref, xc_ref, w_ref, dc_ref, dn_ref, pc_ref, pn_ref, dx_ref, dw_ref, db_ref, xs, ds = refs
        else:
            xp_ref, xc_ref, w_ref, dc_ref, dn_ref, dx_ref, dw_ref, db_ref, xs, ds = refs
        i = pl.program_id(1)
        xs[pl.ds(0, H), :] = jnp.where(i > 0, xp_ref[...], 0.0)
        xs[pl.ds(H, tr), :] = xc_ref[...]
        dcur = dc_ref[...]
        dnext = dn_ref[...]
        if silu:
            dcur = dcur * dsilu(pc_ref[...])
            dnext = dnext * dsilu(pn_ref[...])
        ds[pl.ds(0, tr), :] = dcur
        ds[pl.ds(tr, H), :] = jnp.where(i < n_i - 1, dnext, 0.0)
        taps = [w_ref[pl.ds(k, 1), :] for k in range(K)]
        fold = lambda t: jnp.sum(t.reshape(CONV_ROWS // 8, 8, LANES), axis=0)
        dw_parts = [jnp.zeros((8, LANES), f32) for _ in range(K)]
        db_part = jnp.zeros((8, LANES), f32)
        for c0 in range(0, tr, CONV_ROWS):
            acc = jnp.zeros((CONV_ROWS, LANES), f32)
            d_c = ds[pl.ds(c0, CONV_ROWS), :]
            for k in range(K):
                acc = acc + ds[pl.ds(c0 + K - 1 - k, CONV_ROWS), :] * taps[k]
                dw_parts[k] = dw_parts[k] + fold(d_c * xs[pl.ds(c0 + H - (K - 1) + k, CONV_ROWS), :])
            db_part = db_part + fold(d_c)
            dx_ref[pl.ds(c0, CONV_ROWS), :] = acc.astype(dx_ref.dtype)

        @pl.when(i == 0)
        def _():
            dw_ref[...] = jnp.zeros_like(dw_ref)
            db_ref[...] = jnp.zeros_like(db_ref)

        for k in range(K):
            dw_ref[pl.ds(k, 1), :] += jnp.sum(dw_parts[k], 0, keepdims=True)
        db_ref[...] += jnp.sum(db_part, 0, keepdims=True)

    prev = pl.BlockSpec((H, LANES), lambda j, i: (jnp.maximum(i * nh - 1, 0), j))
    cur = pl.BlockSpec((tr, LANES), lambda j, i: (i, j))
    nxt = pl.BlockSpec((H, LANES), lambda j, i: (jnp.minimum((i + 1) * nh, n_i * nh - 1), j))
    in_specs = [prev, cur, pl.BlockSpec((K, LANES), lambda j, i: (0, j)), cur, nxt]
    args = [x, x, w, dact, dact]
    if silu:
        in_specs += [cur, nxt]
        args += [pre, pre]
    dx, dw, db = pl.pallas_call(
        body, name=name, grid=(C // LANES, n_i), in_specs=in_specs,
        out_specs=[cur, pl.BlockSpec((kp, LANES), lambda j, i: (0, j)), pl.BlockSpec((1, LANES), lambda j, i: (0, j))],
        out_shape=[_sds((S, C), dx_dtype), _sds((kp, C)), _sds((1, C))],
        scratch_shapes=[pltpu.VMEM((tr + H, LANES), f32), pltpu.VMEM((tr + H, LANES), f32)],
        compiler_params=_params(("parallel", "arbitrary")),
    )(*args)
    return dx, dw[:K], db


def _dot(a, b):
    return jnp.dot(a.astype(bf16), b.astype(bf16), preferred_element_type=f32)


def _dot_nt(a, b):
    return lax.dot_general(a.astype(bf16), b.astype(bf16), (((1,), (1,)), ((), ())), preferred_element_type=f32)


def _dot_tn(a, b):
    return lax.dot_general(a.astype(bf16), b.astype(bf16), (((0,), (0,)), ((), ())), preferred_element_type=f32)


def _softplus(x):
    return jnp.maximum(x, 0.0) + jnp.log(1.0 + jnp.exp(-jnp.abs(x)))


def _tri(q):
    i = lax.broadcasted_iota(jnp.int32, (q, q), 0)
    j = lax.broadcasted_iota(jnp.int32, (q, q), 1)
    return i >= j


def _ssd_prep(dtraw, dt_bias, a_log):
    q = dtraw.shape[0]
    dt = _softplus(dtraw + dt_bias)
    A = -jnp.exp(a_log)
    tri = _tri(q)
    cs = jnp.dot(tri.astype(f32), dt * A, preferred_element_type=f32, precision=lax.Precision.HIGHEST)
    return dt, A, cs, cs.T, tri


def _expand(cols, h0, n, width):
    q = cols.shape[0]
    return jnp.concatenate([jnp.broadcast_to(cols[:, h0 + r:h0 + r + 1], (q, width)) for r in range(n)], axis=1)


def _ssd_fwd(xbc, dtraw, dt_bias, a_log, d_skip, di, name):
    S, CD = xbc.shape
    Q, N, G = SSM_CHUNK, SSM_STATE, SSM_GROUPS
    nc = S // Q
    nh = di // HEAD_DIM
    R = nh // G
    gw = R * HEAD_DIM

    def body(xbc_ref, dt_ref, bias_ref, alog_ref, dsk_ref, y_ref, hin_ref, state):
        c = pl.program_id(0)

        @pl.when(c == 0)
        def _():
            state[...] = jnp.zeros_like(state)

        hin_ref[...] = state[...]
        dt, A, cs, csT, tri = _ssd_prep(dt_ref[...], bias_ref[...], alog_ref[...])
        dsk = dsk_ref[...]
        ecs = jnp.exp(cs)
        dend = jnp.exp(cs[Q - 1:Q, :] - cs)
        elast = jnp.exp(cs[Q - 1:Q, :])
        for g in range(G):
            h0 = g * R
            Bg = xbc_ref[:, pl.ds(di + g * N, N)]
            Cg = xbc_ref[:, pl.ds(di + G * N + g * N, N)]
            xg = xbc_ref[:, pl.ds(g * gw, gw)]
            Hg = state[pl.ds(g * gw, gw), :]
            Gm = _dot_nt(Cg, Bg)
            xdt = xg * _expand(dt, h0, R, HEAD_DIM)
            yoff = _dot_nt(Cg, Hg) * _expand(ecs, h0, R, HEAD_DIM)
            ys = []
            for r in range(R):
                h = h0 + r
                L = jnp.exp(jnp.where(tri, cs[:, h:h + 1] - csT[h:h + 1, :], -jnp.inf))
                ys.append(_dot(Gm * L, xdt[:, r * HEAD_DIM:(r + 1) * HEAD_DIM]))
            y = jnp.concatenate(ys, axis=1) + yoff + xg * _expand(dsk, h0, R, HEAD_DIM)
            y_ref[:, pl.ds(g * gw, gw)] = y
            hnew = _dot_tn(xdt * _expand(dend, h0, R, HEAD_DIM), Bg)
            escale = jnp.concatenate([jnp.broadcast_to(elast[:, h0 + r:h0 + r + 1], (HEAD_DIM, N)) for r in range(R)], axis=0)
            state[pl.ds(g * gw, gw), :] = escale * Hg + hnew

    vec = pl.BlockSpec((1, LANES), lambda c: (0, 0))
    return pl.pallas_call(
        body, name=name, grid=(nc,),
        in_specs=[pl.BlockSpec((Q, CD), lambda c: (c, 0)), pl.BlockSpec((Q, LANES), lambda c: (c, 0)), vec, vec, vec],
        out_specs=[pl.BlockSpec((Q, di), lambda c: (c, 0)), pl.BlockSpec((None, di, N), lambda c: (c, 0, 0))],
        out_shape=[_sds((S, di)), _sds((nc, di, N))],
        scratch_shapes=[pltpu.VMEM((di, N), f32)],
        compiler_params=_params(("arbitrary",)),
    )(xbc, dtraw, dt_bias, a_log, d_skip)


def _dot_exact(a, b):
    return jnp.dot(a, b, preferred_element_type=f32, precision=lax.Precision.HIGHEST)


def _ssd_bwd(xbc, dtraw, dt_bias, a_log, d_skip, hin, y, dy, di, name):
    S, CD = xbc.shape
    Q, N, G = SSM_CHUNK, SSM_STATE, SSM_GROUPS
    nc = S // Q
    nh = di // HEAD_DIM
    R = nh // G
    gw = R * HEAD_DIM
    P = HEAD_DIM
    head_of_col = jnp.asarray((np.arange(di)[:, None] // P == np.arange(LANES)[None, :]).astype(np.float32))
    dsk_wide = jnp.repeat(d_skip[0, :nh], P)[None]

    def body(xbc_ref, dt_ref, bias_ref, alog_ref, dskw_ref, hoc_ref, hin_ref, y_ref, dy_ref,
             dxbc_ref, ddt_ref, dA_ref, ddsk_ref, dtb_ref, dstate, dxdt_all, tend_all, yoff_all, colterm_all):
        c = pl.program_id(0)

        @pl.when(c == 0)
        def _():
            dstate[...] = jnp.zeros_like(dstate)
            dA_ref[...] = jnp.zeros_like(dA_ref)
            ddsk_ref[...] = jnp.zeros_like(ddsk_ref)
            dtb_ref[...] = jnp.zeros_like(dtb_ref)

        dtraw_v = dt_ref[...]
        dt, A, cs, csT, tri = _ssd_prep(dtraw_v, bias_ref[...], alog_ref[...])
        tri_t = jnp.logical_not(tri) | (lax.broadcasted_iota(jnp.int32, (Q, Q), 0) == lax.broadcasted_iota(jnp.int32, (Q, Q), 1))
        ecs = jnp.exp(cs)
        dend = jnp.exp(cs[Q - 1:Q, :] - cs)
        elast = jnp.exp(cs[Q - 1:Q, :])
        hoc = hoc_ref[...]
        state_dot = jnp.sum(_dot_exact(dstate[...] * hin_ref[...], jnp.ones((N, LANES), f32)) * hoc, 0, keepdims=True) * elast
        for g in range(G):
            h0 = g * R
            Bg = xbc_ref[:, pl.ds(di + g * N, N)]
            Cg = xbc_ref[:, pl.ds(di + G * N + g * N, N)]
            xg = xbc_ref[:, pl.ds(g * gw, gw)]
            dyg = dy_ref[:, pl.ds(g * gw, gw)]
            Hg = hin_ref[pl.ds(g * gw, gw), :]
            dHg = dstate[pl.ds(g * gw, gw), :]
            dt_e = _expand(dt, h0, R, P)
            ecs_e = _expand(ecs, h0, R, P)
            dend_e = _expand(dend, h0, R, P)
            cols = pl.ds(g * gw, gw)
            Gm = _dot_nt(Cg, Bg)
            Gm_t = _dot_nt(Bg, Cg)
            xdt = xg * dt_e
            dye = dyg * ecs_e
            bdh = _dot_nt(Bg, dHg)
            dC = _dot(dye, Hg)
            dB = _dot(xdt * dend_e, dHg)
            dHin = _dot_tn(dye, Cg)
            dxdt_state = dend_e * bdh
            end_term = xdt * dxdt_state
            tend_all[:, cols] = end_term
            yoff_all[:, cols] = _dot_nt(Cg, Hg) * ecs_e
            dG = jnp.zeros((Q, Q), f32)
            dxd = []
            for r in range(R):
                h = h0 + r
                sl = slice(r * P, (r + 1) * P)
                seg = cs[:, h:h + 1] - csT[h:h + 1, :]
                L = jnp.exp(jnp.where(tri, seg, -jnp.inf))
                L_t = jnp.exp(jnp.where(tri_t, -seg, -jnp.inf))
                dyh = dyg[:, sl]
                dG = dG + _dot_nt(dyh, xdt[:, sl]) * L
                dxd.append(_dot(Gm_t * L_t, dyh))
            dxdt_diag = jnp.concatenate(dxd, axis=1)
            dxdt = dxdt_diag + dxdt_state
            dxdt_all[:, cols] = dxdt
            colterm_all[:, cols] = xdt.astype(bf16).astype(f32) * dxdt_diag + end_term
            dxbc_ref[:, cols] = dxdt * dt_e + dyg * dskw_ref[:, cols]
            dxbc_ref[:, pl.ds(di + g * N, N)] = dB + _dot_tn(dG, Cg)
            dxbc_ref[:, pl.ds(di + G * N + g * N, N)] = dC + _dot(dG, Bg)
            escale = jnp.concatenate([jnp.broadcast_to(elast[:, h0 + r:h0 + r + 1], (P, N)) for r in range(R)], axis=0)
            dstate[pl.ds(g * gw, gw), :] = escale * dHg + dHin
        xs = xbc_ref[:, pl.ds(0, di)]
        dyv = dy_ref[...]
        yoff = yoff_all[...]
        y_diag = y_ref[...] - dskw_ref[...] * xs - yoff
        rs_y = _dot_exact(dyv.astype(bf16).astype(f32) * y_diag + dyv * yoff, hoc)
        rs_c = _dot_exact(colterm_all[...], hoc)
        rs_x = _dot_exact(dxdt_all[...] * xs, hoc)
        end_dot = _dot_exact(jnp.broadcast_to(jnp.sum(tend_all[...], 0, keepdims=True), (8, di)), hoc)[0:1]
        last = lax.broadcasted_iota(jnp.int32, (Q, 1), 0) == Q - 1
        dcs = rs_y - rs_c + jnp.where(last, end_dot + state_dot, 0.0)
        da = lax.dot_general(tri.astype(f32), dcs, (((0,), (0,)), ((), ())), preferred_element_type=f32,
                             precision=lax.Precision.HIGHEST)
        ddt = da * A + rs_x
        ddtraw = ddt * jax.nn.sigmoid(dtraw_v + bias_ref[...])
        ddt_ref[...] = ddtraw.astype(ddt_ref.dtype)
        dA_ref[...] += jnp.sum(da * dt, 0, keepdims=True) * A
        ddsk_ref[...] += jnp.sum(_dot_exact(dyv * xs, hoc), 0, keepdims=True)
        dtb_ref[...] += jnp.sum(ddtraw, 0, keepdims=True)

    vec = pl.BlockSpec((1, LANES), lambda c: (0, 0))
    rev = lambda c: (nc - 1 - c, 0)
    return pl.pallas_call(
        body, name=name, grid=(nc,),
        in_specs=[pl.BlockSpec((Q, CD), rev), pl.BlockSpec((Q, LANES), rev), vec, vec,
                  pl.BlockSpec((1, di), lambda c: (0, 0)), pl.BlockSpec((di, LANES), lambda c: (0, 0)),
                  pl.BlockSpec((None, di, N), lambda c: (nc - 1 - c, 0, 0)), pl.BlockSpec((Q, di), rev),
                  pl.BlockSpec((Q, di), rev)],
        out_specs=[pl.BlockSpec((Q, CD), rev), pl.BlockSpec((Q, LANES), rev), vec, vec, vec],
        out_shape=[_sds((S, CD)), _sds((S, LANES), bf16), _sds((1, LANES)), _sds((1, LANES)), _sds((1, LANES))],
        scratch_shapes=[pltpu.VMEM((di, N), f32)] + [pltpu.VMEM((Q, di), f32)] * 4,
        compiler_params=_params(("arbitrary",)),
    )(xbc, dtraw, dt_bias, a_log, dsk_wide, head_of_col, hin, y, dy)


def _t5_bucket_np(dist):
    max_exact = REL_BUCKETS // 2
    n = np.maximum(dist, 1).astype(np.float32)
    large = np.float32(max_exact) + np.log(n / np.float32(max_exact)) / np.float32(math.log(REL_MAX_DIST / max_exact)) * np.float32(REL_BUCKETS - max_exact)
    large = np.minimum(large.astype(np.int32), REL_BUCKETS - 1)
    return np.where(dist < max_exact, dist, large)


def _bucket_onehot():
    i = np.arange(ATT_BLK)[None, :]
    j = np.arange(2 * ATT_BLK)[:, None]
    delta = np.maximum(ATT_BLK + i - j, 0)
    out = np.zeros((len(ATT_DILATIONS), REL_BUCKETS, ATT_BLK * 2 * ATT_BLK), np.float32)
    for gi, d in enumerate(ATT_DILATIONS):
        b = _t5_bucket_np(delta * d).reshape(-1)
        out[gi, b, np.arange(b.size)] = 1.0
    return out


def _exact_mm(a, b, *, name, tb=False):
    M, K = a.shape
    N = b.shape[0] if tb else b.shape[1]
    tn = _tile(N, 4096, LANES)
    dn = (((1,), (1 if tb else 0,)), ((), ()))

    def body(a_ref, b_ref, o_ref):
        o_ref[...] = lax.dot_general(a_ref[...], b_ref[...], dn, preferred_element_type=f32,
                                     precision=lax.Precision.HIGHEST)

    return pl.pallas_call(
        body, name=name, grid=(N // tn,),
        in_specs=[pl.BlockSpec((M, K), lambda j: (0, 0)),
                  pl.BlockSpec((tn, K), lambda j: (j, 0)) if tb else pl.BlockSpec((K, tn), lambda j: (0, j))],
        out_specs=pl.BlockSpec((M, tn), lambda j: (0, j)), out_shape=_sds((M, N)),
        compiler_params=_params(("parallel",)),
    )(a, b)


def _band_penalty():
    i = np.arange(ATT_BLK)[None, :]
    j = np.arange(2 * ATT_BLK)[:, None]
    delta = ATT_BLK + i - j
    return np.where((delta >= 0) & (delta <= ATT_BLK), 0.0, -np.inf).astype(np.float32)


def _first_block_keep(n):
    key = lax.broadcasted_iota(jnp.int32, (2 * ATT_BLK, ATT_BLK), 0)
    return (key >= ATT_BLK) | (n > 0)


ATT_SCALE = HEAD_DIM ** -0.5


def _rows(ref, r, d):
    return ref[...] if d == 1 else ref[pl.ds(r, ATT_BLK, stride=d), :]


def _set_rows(ref, r, d, val):
    if d == 1:
        ref[...] = val
    else:
        ref[pl.ds(r, ATT_BLK, stride=d), :] = val


def _attn_width(d, D):
    return D if d == 1 else LANES


def _over_residues(d, one, unroll=1):
    if d == 1:
        one(0)
    else:
        lax.fori_loop(0, d, lambda r, c: (one(r), c)[1], 0, unroll=unroll)


def _attn_fwd(q, k, v, bias_t, d, name):
    S, D = q.shape
    nb = S // (d * ATT_BLK)
    H = D // HEAD_DIM
    W = _attn_width(d, D)
    HB = W // HEAD_DIM

    def body(q_ref, kp_ref, kc_ref, vp_ref, vc_ref, b_ref, o_ref, lse_ref):
        keep = _first_block_keep(pl.program_id(1))
        first = lax.broadcasted_iota(jnp.int32, (1, LANES), 1) < HEAD_DIM

        def one(r):
            qs = (_rows(q_ref, r, d) * ATT_SCALE).astype(bf16)
            kcat = jnp.concatenate([_rows(kp_ref, r, d), _rows(kc_ref, r, d)], axis=0).astype(bf16)
            vcat = jnp.concatenate([_rows(vp_ref, r, d), _rows(vc_ref, r, d)], axis=0).astype(bf16)
            outs = []
            for pair in range(W // LANES):
                ps = slice(pair * LANES, (pair + 1) * LANES)
                q2, k2, v2 = qs[:, ps], kcat[:, ps], vcat[:, ps]
                o2 = jnp.zeros((ATT_BLK, LANES), f32)
                for e in range(2):
                    h = 2 * pair + e
                    mine = first if e == 0 else jnp.logical_not(first)
                    zero = jnp.zeros((), bf16)
                    st = jnp.where(keep, _dot_nt(k2, jnp.where(mine, q2, zero)) + b_ref[h], -jnp.inf)
                    m = jnp.max(st, 0, keepdims=True)
                    pt = jnp.exp(st - m)
                    l = jnp.sum(pt, 0, keepdims=True)
                    o2 = o2 + _dot_tn(pt * (1.0 / l), jnp.where(mine, v2, zero))
                    lse_ref[r, h] = m + jnp.log(l)
                outs.append(o2)
            _set_rows(o_ref, r, d, jnp.concatenate(outs, axis=1))

        _over_residues(d, one, unroll=4)

    cur = pl.BlockSpec((ATT_BLK * d, W), lambda j, n: (n, j))
    prev = pl.BlockSpec((ATT_BLK * d, W), lambda j, n: (jnp.maximum(n - 1, 0), j))
    return pl.pallas_call(
        body, name=name, grid=(D // W, nb),
        in_specs=[cur, prev, cur, prev, cur, pl.BlockSpec((HB, 2 * ATT_BLK, ATT_BLK), lambda j, n: (j, 0, 0))],
        out_specs=[cur, pl.BlockSpec((None, d, HB, 1, LANES), lambda j, n: (n, 0, j, 0, 0))],
        out_shape=[_sds((S, D)), _sds((nb, d, H, 1, LANES))],
        compiler_params=_params(("parallel", "arbitrary")),
    )(q, k, k, v, v, bias_t)


def _from_blocks(rows, lanes=None):
    nb, d, H = rows.shape[:3]
    a = jnp.transpose(rows[:, :, :, 0, :], (0, 3, 1, 2)).reshape(nb * ATT_BLK * d, H)
    return a if lanes is None else jnp.pad(a, ((0, 0), (0, lanes - H)))


def _by_block(a, d):
    S, H = a.shape
    t = jnp.transpose(a.reshape(S // (d * ATT_BLK), ATT_BLK, d, H), (0, 2, 3, 1))
    return t[:, :, :, None, :]


def _head_sums(a, b, name):
    S, D = a.shape
    tr = _tile(S, 512, 8)
    hoc = jnp.asarray((np.arange(D)[:, None] // HEAD_DIM == np.arange(LANES)[None, :]).astype(np.float32))

    def body(a_ref, b_ref, h_ref, o_ref):
        o_ref[...] = _dot_exact(a_ref[...] * b_ref[...], h_ref[...])

    return pl.pallas_call(
        body, name=name, grid=(S // tr,),
        in_specs=[pl.BlockSpec((tr, D), lambda i: (i, 0)), pl.BlockSpec((tr, D), lambda i: (i, 0)),
                  pl.BlockSpec((D, LANES), lambda i: (0, 0))],
        out_specs=pl.BlockSpec((tr, LANES), lambda i: (i, 0)), out_shape=_sds((S, LANES)),
        compiler_params=_params(("parallel",)),
    )(a, b, hoc)


def _attn_bwd(q, k, v, bias_t, datt, lse_rows, dsum_rows, d, name):
    S, D = q.shape
    nb = S // (d * ATT_BLK)
    H = D // HEAD_DIM
    W = _attn_width(d, D)
    HB = W // HEAD_DIM

    def body(q_ref, kp_ref, kc_ref, vp_ref, vc_ref, b_ref, do_ref, lse_ref, dsum_ref,
             dq_ref, dk_ref, dv_ref, db_ref, carry_k, carry_v):
        j = pl.program_id(0)
        n = pl.program_id(1)

        @pl.when(n == 0)
        def _():
            carry_k[...] = jnp.zeros_like(carry_k)
            carry_v[...] = jnp.zeros_like(carry_v)
            db_ref[...] = jnp.zeros_like(db_ref)

        @pl.when(n < nb)
        def _():
            key = lax.broadcasted_iota(jnp.int32, (2 * ATT_BLK, ATT_BLK), 0)
            keep = (key >= ATT_BLK) | (n > 0)
            first = lax.broadcasted_iota(jnp.int32, (1, LANES), 1) < HEAD_DIM

            def one(r):
                qs = (_rows(q_ref, r, d) * ATT_SCALE).astype(bf16)
                kcat = jnp.concatenate([_rows(kp_ref, r, d), _rows(kc_ref, r, d)], axis=0).astype(bf16)
                vcat = jnp.concatenate([_rows(vp_ref, r, d), _rows(vc_ref, r, d)], axis=0).astype(bf16)
                dob = _rows(do_ref, r, d).astype(bf16)
                dqs, dks, dvs = [], [], []
                for pair in range(W // LANES):
                    ps = slice(pair * LANES, (pair + 1) * LANES)
                    q2, k2, v2, do2 = qs[:, ps], kcat[:, ps], vcat[:, ps], dob[:, ps]
                    dq2 = jnp.zeros((ATT_BLK, LANES), f32)
                    dk2 = jnp.zeros((2 * ATT_BLK, LANES), f32)
                    dv2 = jnp.zeros((2 * ATT_BLK, LANES), f32)
                    for e in range(2):
                        h = 2 * pair + e
                        mine = first if e == 0 else jnp.logical_not(first)
                        zero = jnp.zeros((), bf16)
                        qm, dom, km = jnp.where(mine, q2, zero), jnp.where(mine, do2, zero), jnp.where(mine, k2, zero)
                        st = jnp.where(keep, _dot_nt(k2, qm) + b_ref[h], -jnp.inf)
                        pt = jnp.exp(st - lse_ref[r, j * HB + h])
                        dst = pt * (_dot_nt(v2, dom) - dsum_ref[r, j * HB + h])
                        db_ref[h] += dst
                        dv2 = dv2 + _dot(pt, dom)
                        dk2 = dk2 + _dot(dst, qm)
                        dq2 = dq2 + _dot_tn(dst, km)
                    dqs.append(dq2 * ATT_SCALE)
                    dks.append(dk2)
                    dvs.append(dv2)
                _set_rows(dq_ref, r, d, jnp.concatenate(dqs, axis=1))
                dk = jnp.concatenate(dks, axis=1)
                dv = jnp.concatenate(dvs, axis=1)
                _set_rows(dk_ref, r, d, carry_k[r] + dk[:ATT_BLK])
                _set_rows(dv_ref, r, d, carry_v[r] + dv[:ATT_BLK])
                carry_k[r] = dk[ATT_BLK:]
                carry_v[r] = dv[ATT_BLK:]

            _over_residues(d, one, unroll=2)

        @pl.when(n == nb)
        def _():
            def last(r):
                _set_rows(dk_ref, r, d, carry_k[r])
                _set_rows(dv_ref, r, d, carry_v[r])

            _over_residues(d, last)

    nq = lambda n: jnp.minimum(n, nb - 1)
    cur = pl.BlockSpec((ATT_BLK * d, W), lambda j, n: (nq(n), j))
    prev = pl.BlockSpec((ATT_BLK * d, W), lambda j, n: (jnp.maximum(nq(n) - 1, 0), j))
    done = pl.BlockSpec((ATT_BLK * d, W), lambda j, n: (jnp.maximum(n - 1, 0), j))
    bspec = pl.BlockSpec((HB, 2 * ATT_BLK, ATT_BLK), lambda j, n: (j, 0, 0))
    rows = pl.BlockSpec((None, d, H, 1, LANES), lambda j, n: (nq(n), 0, 0, 0, 0))
    return pl.pallas_call(
        body, name=name, grid=(D // W, nb + 1),
        in_specs=[cur, prev, cur, prev, cur, bspec, cur, rows, rows],
        out_specs=[cur, done, done, bspec],
        out_shape=[_sds((S, D)), _sds((S, D)), _sds((S, D)), _sds((H, 2 * ATT_BLK, ATT_BLK))],
        scratch_shapes=[pltpu.VMEM((d, ATT_BLK, W), f32), pltpu.VMEM((d, ATT_BLK, W), f32)],
        compiler_params=_params(("arbitrary", "arbitrary")),
    )(q, k, k, v, v, bias_t, datt, lse_rows, dsum_rows)


def _attn_combine(os_, lses, name):
    S, D = os_[0].shape
    tr = _tile(S, 128, 16)
    head_cols = jnp.asarray((np.arange(LANES)[:, None] == np.arange(D)[None, :] // HEAD_DIM).astype(np.float32))

    def body(o0, o1, o2, l0, l1, l2, hc_ref, att_ref, attb_ref, lse_ref):
        a, b, c = l0[...], l1[...], l2[...]
        m = jnp.maximum(jnp.maximum(a, b), c)
        e0, e1, e2 = jnp.exp(a - m), jnp.exp(b - m), jnp.exp(c - m)
        tot = e0 + e1 + e2
        wide = lambda w: _dot_exact(w / tot, hc_ref[...])
        att = wide(e0) * o0[...] + wide(e1) * o1[...] + wide(e2) * o2[...]
        att_ref[...] = att
        attb_ref[...] = att.astype(bf16)
        lse_ref[...] = m + jnp.log(tot)

    wide_spec = pl.BlockSpec((tr, D), lambda i: (i, 0))
    lane_spec = pl.BlockSpec((tr, LANES), lambda i: (i, 0))
    return pl.pallas_call(
        body, name=name, grid=(S // tr,),
        in_specs=[wide_spec] * 3 + [lane_spec] * 3 + [pl.BlockSpec((LANES, D), lambda i: (0, 0))],
        out_specs=[wide_spec, wide_spec, lane_spec], out_shape=[_sds((S, D)), _sds((S, D), bf16), _sds((S, LANES))],
        compiler_params=_params(("parallel",)),
    )(*os_, *lses, head_cols)


ANY = pl.BlockSpec(memory_space=pl.ANY)


def _all_gather(vs, name):
    n = len(vs)

    def body(*refs):
        x_refs, out_refs = refs[:n], refs[n:2 * n]
        send_sems, recv_sems, local_sems = refs[2 * n:]
        x, y, c = lax.axis_index("x"), lax.axis_index("y"), lax.axis_index("c")
        me, sibling = (x, y, c), (x, y, 1 - c)
        chips = [(1 - x, y), (x, 1 - y), (1 - x, 1 - y)]

        def slot(i, px, py, pc):
            return out_refs[i].at[4 * px + 2 * py + pc]

        def copy(i, k, block, to, src=None):
            return pltpu.make_async_remote_copy(
                src_ref=slot(i, *block) if src is None else src, dst_ref=slot(i, *block),
                send_sem=send_sems.at[i, k], recv_sem=recv_sems.at[i, k], device_id=to, device_id_type=MESH)

        mine = [pltpu.make_async_copy(x_refs[i], slot(i, *me), local_sems.at[i]) for i in range(n)]
        for cp in mine:
            cp.start()
        first = []
        for i in range(n):
            first.append(copy(i, 0, me, sibling, src=x_refs[i]))
            first += [copy(i, 1 + j, me, (*chip, c), src=x_refs[i]) for j, chip in enumerate(chips)]
        for cp in first:
            cp.start()
        passed = []
        for i in range(n):
            for j, chip in enumerate(chips):
                copy(i, 1 + j, (*chip, c), me).wait_recv()
                cp = copy(i, 4 + j, (*chip, c), sibling)
                cp.start()
                passed.append(cp)
        for i in range(n):
            copy(i, 0, sibling, me).wait_recv()
            for j, chip in enumerate(chips):
                copy(i, 4 + j, (*chip, 1 - c), me).wait_recv()
        for cp in first + passed:
            cp.wait_send()
        for cp in mine:
            cp.wait()

    return pl.pallas_call(
        body, name=name, out_shape=[_sds((N_DEV,) + v.shape, v.dtype) for v in vs], in_specs=[ANY] * n,
        out_specs=[ANY] * n,
        scratch_shapes=[pltpu.SemaphoreType.DMA((n, 7)), pltpu.SemaphoreType.DMA((n, 7)), pltpu.SemaphoreType.DMA((n,))],
    )(*vs)


def _rs_sibling(parts, name):
    n = len(parts)

    def body(*refs):
        p_refs, out_refs = refs[:n], refs[n:2 * n]
        send_sems, recv_sems = refs[2 * n:]
        x, y, c = lax.axis_index("x"), lax.axis_index("y"), lax.axis_index("c")
        cps = [pltpu.make_async_remote_copy(
            src_ref=p_refs[i].at[k, 1 - c], dst_ref=out_refs[i].at[k], send_sem=send_sems.at[i, k],
            recv_sem=recv_sems.at[i, k], device_id=(x, y, 1 - c), device_id_type=MESH)
            for i in range(n) for k in range(4)]
        for cp in cps:
            cp.start()
        for cp in cps:
            cp.wait()

    return pl.pallas_call(
        body, name=name, out_shape=[_sds((4,) + p.shape[2:], p.dtype) for p in parts], in_specs=[ANY] * n,
        out_specs=[ANY] * n,
        scratch_shapes=[pltpu.SemaphoreType.DMA((n, 4)), pltpu.SemaphoreType.DMA((n, 4))],
    )(*parts)


def _rs_chips(ts, name):
    n = len(ts)

    def body(*refs):
        t_refs, out_refs = refs[:n], refs[n:2 * n]
        send_sems, recv_sems, local_sems = refs[2 * n:]
        x, y, c = lax.axis_index("x"), lax.axis_index("y"), lax.axis_index("c")
        mine = 2 * x + y
        local = [pltpu.make_async_copy(t_refs[i].at[mine], out_refs[i].at[mine], local_sems.at[i]) for i in range(n)]
        for cp in local:
            cp.start()
        chips = [(1 - x, y), (x, 1 - y), (1 - x, 1 - y)]
        cps = [pltpu.make_async_remote_copy(
            src_ref=t_refs[i].at[2 * px + py], dst_ref=out_refs[i].at[mine], send_sem=send_sems.at[i, j],
            recv_sem=recv_sems.at[i, j], device_id=(px, py, c), device_id_type=MESH)
            for i in range(n) for j, (px, py) in enumerate(chips)]
        for cp in cps:
            cp.start()
        for cp in cps:
            cp.wait()
        for cp in local:
            cp.wait()

    return pl.pallas_call(
        body, name=name, out_shape=[_sds(t.shape, t.dtype) for t in ts], in_specs=[ANY] * n, out_specs=[ANY] * n,
        scratch_shapes=[pltpu.SemaphoreType.DMA((n, 3)), pltpu.SemaphoreType.DMA((n, 3)), pltpu.SemaphoreType.DMA((n,))],
    )(*ts)


def _pair_add(part, recv, c_arr, name):
    _, _, R, C = part.shape
    tr = _tile(R, PACK_ROW_TILE, 16)

    def body(c_ref, p_ref, r_ref, o_ref):
        o_ref[...] = (p_ref[...] + r_ref[...]).astype(o_ref.dtype)

    return pl.pallas_call(
        body, name=name,
        grid_spec=pltpu.PrefetchScalarGridSpec(
            num_scalar_prefetch=1, grid=(4, R // tr),
            in_specs=[pl.BlockSpec((None, None, tr, C), lambda k, i, c_ref: (k, c_ref[0], i, 0)),
                      pl.BlockSpec((None, tr, C), lambda k, i, c_ref: (k, i, 0))],
            out_specs=pl.BlockSpec((None, tr, C), lambda k, i, c_ref: (k, i, 0))),
        out_shape=_sds((4, R, C), bf16),
        compiler_params=_params(("parallel", "parallel")),
    )(c_arr, part, recv)


def _sum_slots(t, name):
    n, R, C = t.shape
    tr = _tile(R, PACK_ROW_TILE, 16)

    def body(t_ref, o_ref):
        acc = t_ref[0].astype(f32)
        for k in range(1, n):
            acc = acc + t_ref[k].astype(f32)
        o_ref[...] = acc

    return pl.pallas_call(
        body, name=name, grid=(R // tr,),
        in_specs=[pl.BlockSpec((n, tr, C), lambda i: (0, i, 0))],
        out_specs=pl.BlockSpec((tr, C), lambda i: (i, 0)), out_shape=_sds((R, C)),
        compiler_params=_params(("parallel",)),
    )(t)


def _reduce_scatter(parts, c_arr, name):
    parts4 = [p.reshape((4, 2) + p.shape[1:]) for p in parts]
    recv = _rs_sibling(parts4, name + "_sibling")
    ts = [_pair_add(p, r, c_arr, f"{name}_pair_{i}") for i, (p, r) in enumerate(zip(parts4, recv))]
    got = _rs_chips(ts, name + "_chips")
    return [_sum_slots(g, f"{name}_sum_{i}") for i, g in enumerate(got)]


HBM_SPEC = pl.BlockSpec(memory_space=pltpu.HBM)
SEM_SPEC = pl.BlockSpec(memory_space=pltpu.SEMAPHORE)
EFFECT = pltpu.SideEffectType.DATAFLOW_SIDE_EFFECTING


def _mesh_pos(p):
    return (p // 4, (p // 2) % 2, p % 2)


def _exchange_copy(src_refs, land_refs, send_sems, recv_sems, whole, i, k, receiving):
    me = 4 * lax.axis_index("x") + 2 * lax.axis_index("y") + lax.axis_index("c")
    to = (me + k) % N_DEV
    frm = (me + N_DEV - k) % N_DEV
    src = src_refs[i] if whole else src_refs[i].at[to]
    s = i * (N_DEV - 1) + k - 1
    send = pltpu.make_async_remote_copy(src_ref=src, dst_ref=land_refs[i].at[me], send_sem=send_sems.at[s],
                                        recv_sem=recv_sems.at[s], device_id=_mesh_pos(to), device_id_type=MESH)
    if not receiving:
        return send
    return send, pltpu.make_async_remote_copy(src_ref=src, dst_ref=land_refs[i].at[frm], send_sem=send_sems.at[s],
                                              recv_sem=recv_sems.at[s], device_id=_mesh_pos(to), device_id_type=MESH)


def _exchange_start(srcs, whole, name, after=None):
    n = len(srcs)
    lands = [lax.empty((N_DEV,) + s.shape[-2:], s.dtype) for s in srcs]
    after = list(after or [])
    n_in = 2 * n + len(after)

    def body(*refs):
        src_refs, land_refs = refs[:n], refs[n:2 * n]
        send_sems, recv_sems, token = refs[n_in], refs[n_in + 1], refs[-1]
        for i in range(n):
            for k in range(1, N_DEV):
                _exchange_copy(src_refs, land_refs, send_sems, recv_sems, whole, i, k, False).start()
        token[...] = jnp.zeros_like(token)

    sems = pltpu.SemaphoreType.DMA((n * (N_DEV - 1),))
    outs = pl.pallas_call(
        body, name=name,
        out_shape=(sems, sems, *[pltpu.HBM(a.shape, a.dtype) for a in srcs + lands], _sds((8, LANES))),
        in_specs=[HBM_SPEC] * (2 * n) + [pl.BlockSpec(memory_space=pl.ANY)] * len(after),
        out_specs=(SEM_SPEC, SEM_SPEC, *[HBM_SPEC] * (2 * n), pl.BlockSpec(memory_space=pltpu.VMEM)),
        input_output_aliases={i: 2 + i for i in range(2 * n)},
        compiler_params=pltpu.CompilerParams(has_side_effects=EFFECT),
    )(*[pltpu.with_memory_space_constraint(a, pltpu.HBM) for a in srcs + lands], *after)
    return (outs[0], outs[1], list(outs[2:2 + n]), list(outs[2 + n:2 + 2 * n]), whole), outs[-1]


def _exchange_wait(handle, after, name):
    send_sems, recv_sems, srcs, lands, whole = handle
    n = len(srcs)

    def body(*refs):
        src_refs, land_refs = refs[:n], refs[n:2 * n]
        send_sems, recv_sems = refs[2 * n], refs[2 * n + 1]
        for i in range(n):
            for k in range(1, N_DEV):
                send, recv = _exchange_copy(src_refs, land_refs, send_sems, recv_sems, whole, i, k, True)
                send.wait_send()
                recv.wait_recv()

    outs = pl.pallas_call(
        body, name=name, out_shape=tuple(pltpu.HBM(a.shape, a.dtype) for a in srcs + lands),
        in_specs=[HBM_SPEC] * (2 * n) + [SEM_SPEC, SEM_SPEC, pl.BlockSpec(memory_space=pl.ANY)],
        out_specs=[HBM_SPEC] * (2 * n), input_output_aliases={i: i for i in range(2 * n)},
        compiler_params=pltpu.CompilerParams(has_side_effects=EFFECT),
    )(*srcs, *lands, send_sems, recv_sems, after)
    return list(outs[n:])


def _tie(v, token):
    return v + token[0:1, 0:1].astype(v.dtype).reshape((1,) * v.ndim)


def _with_own(land, own, me):
    return lax.dynamic_update_slice_in_dim(land, own[None].astype(land.dtype), me, 0)


class _Overlap:
    def __init__(self, shards, me, after):
        self.me = me
        self.names = list(shards)
        self.handle, self.token = _exchange_start([shards[nm] for nm in self.names], True, "weights_start", after)
        self.sent = {}

    def weights(self, after):
        lands = _exchange_wait(self.handle, after, "weights_wait")
        own = self.handle[2]
        return {nm: _full_from_blocks(nm, _with_own(land, o, self.me)) for nm, land, o in zip(self.names, lands, own)}

    def send(self, tag, grads, after=None):
        names = list(grads)
        handle, token = _exchange_start([_blocks_from_full(nm, grads[nm]) for nm in names], False, f"grads_start_{tag}",
                                        after)
        self.sent[tag] = (names, handle)
        return token

    def received(self, tag, after):
        names, handle = self.sent[tag]
        lands = _exchange_wait(handle, after, f"grads_wait_{tag}")
        own = [lax.dynamic_index_in_dim(b, self.me, 0, keepdims=False) for b in handle[2]]
        return {nm: _with_own(land, o, self.me) for nm, land, o in zip(names, lands, own)}


ADAM_ROWS = 32


def _adamw(w, g, m, v, name):
    R, C = w.shape
    cb = LANES if C % LANES == 0 else C

    def body(w_ref, g_ref, m_ref, v_ref, d_ref, m2_ref, v2_ref):
        def update(sl):
            gv = g_ref[sl, :]
            m2 = ADAM_B1 * m_ref[sl, :] + (1.0 - ADAM_B1) * gv
            v2 = ADAM_B2 * v_ref[sl, :] + (1.0 - ADAM_B2) * jnp.square(gv)
            m_hat = m2 / (1.0 - ADAM_B1 ** ADAM_STEP)
            v_hat = v2 / (1.0 - ADAM_B2 ** ADAM_STEP)
            d_ref[sl, :] = -ADAM_LR * (m_hat / (jnp.sqrt(v_hat) + ADAM_EPS) + ADAM_WD * w_ref[sl, :])
            m2_ref[sl, :] = m2
            v2_ref[sl, :] = v2

        main = R // ADAM_ROWS
        if main:
            lax.fori_loop(0, main, lambda i, c: (update(pl.ds(pl.multiple_of(i * ADAM_ROWS, ADAM_ROWS), ADAM_ROWS)), c)[1], 0)
        if R % ADAM_ROWS:
            update(pl.ds(main * ADAM_ROWS, R % ADAM_ROWS))

    spec = pl.BlockSpec((R, cb), lambda j: (0, j))
    return pl.pallas_call(
        body, name=name, grid=(C // cb,), in_specs=[spec] * 4, out_specs=[spec] * 3, out_shape=[_sds((R, C))] * 3,
        compiler_params=_params(("parallel",)),
    )(w, g, m, v)


BIG_PARAMS = ("hy_w_in", "hy_w_out", "cv_w_pw1", "cv_w_pw2", "ffn_w_gate", "ffn_w_up", "ffn_w_down")


def _shards_2d(w):
    t = lambda a: jnp.transpose(a)
    return dict(in_t=t(w["hy_w_in"][0]), out=w["hy_w_out"][0], pw1=w["cv_w_pw1"][0], pw2=w["cv_w_pw2"][0],
                gate_t0=t(w["ffn_w_gate"][0]), gate_t1=t(w["ffn_w_gate"][1]), up_t0=t(w["ffn_w_up"][0]),
                up_t1=t(w["ffn_w_up"][1]), down0=w["ffn_w_down"][0], down1=w["ffn_w_down"][1])


def _unshard_2d(s):
    t = lambda a: jnp.transpose(a)
    return dict(hy_w_in=t(s["in_t"])[None], hy_w_out=s["out"][None], cv_w_pw1=s["pw1"][None], cv_w_pw2=s["pw2"][None],
                ffn_w_gate=jnp.stack([t(s["gate_t0"]), t(s["gate_t1"])]),
                ffn_w_up=jnp.stack([t(s["up_t0"]), t(s["up_t1"])]), ffn_w_down=jnp.stack([s["down0"], s["down1"]]))


def _full_from_blocks(nm, g):
    if nm == "pw1":
        return jnp.transpose(g, (1, 0, 2)).reshape(g.shape[1], N_DEV * g.shape[2])
    return g.reshape(N_DEV * g.shape[1], g.shape[2])


def _blocks_from_full(nm, g):
    if nm == "pw1":
        return jnp.transpose(g.reshape(g.shape[0], N_DEV, g.shape[1] // N_DEV), (1, 0, 2))
    return g.reshape(N_DEV, g.shape[0] // N_DEV, g.shape[1])


class _VecPack:
    def __init__(self, shapes):
        self.shapes = [tuple(s) for s in shapes]
        self.sizes = [int(np.prod(s)) for s in self.shapes]
        total = sum(self.sizes)
        self.rows = -(-(-(-total // LANES)) // 8) * 8
        self.total = total

    def pack(self, arrays):
        flat = jnp.concatenate([a.astype(f32).reshape(-1) for a in arrays])
        flat = jnp.pad(flat, (0, self.rows * LANES - self.total))
        return flat.reshape(self.rows, LANES)

    def unpack(self, packed):
        flat = packed.reshape(-1)
        out, off = [], 0
        for shp, n in zip(self.shapes, self.sizes):
            out.append(flat[off:off + n].reshape(shp))
            off += n
        return out

    def unpack_stacked(self, stacked, only=None):
        flat = stacked.reshape(stacked.shape[0], -1)
        offs = np.concatenate([[0], np.cumsum(self.sizes)])
        get = lambda i: flat[:, offs[i]:offs[i + 1]].reshape((stacked.shape[0],) + self.shapes[i])
        return get(only) if only is not None else [get(i) for i in range(len(self.shapes))]


def _row(v):
    return v.reshape(1, -1)


def _pad_lanes(v):
    v = v.reshape(1, -1)
    return jnp.pad(v, ((0, 0), (0, LANES - v.shape[1])))


def _ffn_fwd(h, w_gate_t, w_up_t, w_down, tag):
    F = w_down.shape[0]
    a = _mm(h, w_gate_t, tb=True, name=f"ffn_gate_{tag}")
    u = _mm(h, w_up_t, tb=True, name=f"ffn_up_{tag}")
    (f,), _ = _rowwise(f"swiglu_{tag}", lambda rv, vv: ([_silu(rv[0]) * rv[1]], []), [a, u], [], [(F, bf16)], [], sub=16)
    out = _mm(f, w_down, name=f"ffn_down_{tag}")
    return out, (a, u, f)


def _ffn_bwd(h, w_gate_t, w_up_t, w_down, saved, dout, tag):
    a, u, f = saved
    F = w_down.shape[0]
    df = _mm(dout, w_down, tb=True, name=f"ffn_down_dx_{tag}")
    dw_down = _mm(f, dout, ta=True, out_dtype=bf16, name=f"ffn_down_dw_{tag}")

    def fn(rv, vv):
        _, vjp = jax.vjp(lambda a_, u_: _silu(a_) * u_, rv[0], rv[1])
        da, du = vjp(rv[2])
        return [da, du], []

    (da, du), _ = _rowwise(f"swiglu_bwd_{tag}", fn, [a, u, df], [], [(F, bf16), (F, bf16)], [], sub=16)
    dh = _mm(du, w_up_t, add=_mm(da, w_gate_t, name=f"ffn_gate_dx_{tag}"), name=f"ffn_up_dx_{tag}")
    dw_gate_t = _mm(da, h, ta=True, out_dtype=bf16, name=f"ffn_gate_dw_{tag}")
    dw_up_t = _mm(du, h, ta=True, out_dtype=bf16, name=f"ffn_up_dw_{tag}")
    return dh, dw_gate_t, dw_up_t, dw_down


def _local_step(x, target, mod, w_in_t, comm, small):
    S, D = x.shape
    di = small["hy_ssm_norm_g"].shape[-1]
    nh = small["hy_dt_bias"].shape[-1]
    cd = small["hy_conv_b"].shape[-1]
    m = [[_row(mod[i, j]) for j in range(6)] for i in range(2)]

    off_q = di + cd + nh
    w_qkv_t = w_in_t[off_q:]
    seg = dict(z=(w_in_t, 0, di), xbc=(w_in_t, di, cd), dt=(w_in_t, di + cd, LANES))
    for i, nm in enumerate(("q0", "q1", "q2", "k", "v")):
        seg[nm] = (w_qkv_t, i * D, D)

    g_mix = [_row(small["norm_mix_g"][i]) for i in range(2)]
    g_ffn = [_row(small["norm_ffn_g"][i]) for i in range(2)]
    conv_w, conv_b = small["hy_conv_w_full"], _row(small["hy_conv_b"][0])
    dt_bias, a_log, d_skip = (_pad_lanes(small[k][0]) for k in ("hy_dt_bias", "hy_a_log", "hy_d_skip"))
    g_ssm = _row(small["hy_ssm_norm_g"][0])
    onehot = jnp.asarray(_bucket_onehot())
    rel_t = small["rel_table"].T
    H = D // HEAD_DIM
    bias = [_exact_mm(rel_t[gi * H:(gi + 1) * H], onehot[gi], name=f"rel_bias_{gi}")
            .reshape(H, 2 * ATT_BLK, ATT_BLK) + _band_penalty() for gi in range(3)]

    h1 = _adaln_fwd(x, g_mix[0], m[0][1], m[0][0], "adaln_mix0")
    proj = {nm: _mm(h1, mat, tb=True, b_rows=(off, cnt), name=f"in_{nm}") for nm, (mat, off, cnt) in seg.items()}
    xbc_pre, xbc = _conv_fwd(proj["xbc"], conv_w, conv_b, silu=True, name="ssm_conv", tr=1024)
    y, hin = _ssd_fwd(xbc, proj["dt"], dt_bias, a_log, d_skip, di, "ssd_fwd")
    (yg,), _ = _rowwise("ssm_gate", lambda rv, vv: ([_gate_f(rv[0], rv[1], vv[0])], []),
                        [y, proj["z"]], [g_ssm], [(di, bf16)], [], sub=16)
    og = [_attn_fwd(proj[f"q{gi}"], proj["k"], proj["v"], bias[gi], d, f"attn_fwd_{gi}")
          for gi, d in enumerate(ATT_DILATIONS)]
    att, att_b, lse_tot = _attn_combine([a for a, _ in og], [_from_blocks(b, LANES) for _, b in og], "attn_combine")
    W = comm.weights(after=att_b)
    w_out_y, w_out_a = W["out"][:di], W["out"][di:]
    mix0 = _mm(att_b, w_out_a, add=_mm(yg, w_out_y, name="out_y"), name="out_a")
    x1 = _resid_fwd(x, m[0][2], mix0, "resid_mix0")
    h2 = _adaln_fwd(x1, g_ffn[0], m[0][4], m[0][3], "adaln_ffn0")
    f0, ffn0_saved = _ffn_fwd(h2, W["gate_t0"], W["up_t0"], W["down0"], "0")
    x2 = _resid_fwd(x1, m[0][5], f0, "resid_ffn0")

    h3 = _adaln_fwd(x2, g_mix[1], m[1][1], m[1][0], "adaln_mix1")
    pw1 = _mm(h3, W["pw1"], bias=_row(small["cv_b_pw1_full"]), name="cv_pw1")
    (u,), _ = _rowwise("cv_glu", lambda rv, vv: ([rv[0] * jax.nn.sigmoid(rv[1])], []),
                       [(pw1, 0, D), (pw1, 1, D)], [], [(D, f32)], [])
    (u2,) = _conv_fwd(u, small["cv_w_dw_full"], _row(small["cv_b_dw_full"]), silu=False, name="cv_dw")
    ln_g, ln_b = _row(small["cv_ln_g_full"]), _row(small["cv_ln_b_full"])
    (u3,), _ = _rowwise("cv_lnsilu", lambda rv, vv: ([_lnsilu_f(rv[0], vv[0], vv[1])], []),
                        [u2], [ln_g, ln_b], [(D, bf16)], [], sub=16)
    mix1 = _mm(u3, W["pw2"], bias=_row(small["cv_b_pw2_full"]), name="cv_pw2")
    x3 = _resid_fwd(x2, m[1][2], mix1, "resid_mix1")
    h4 = _adaln_fwd(x3, g_ffn[1], m[1][4], m[1][3], "adaln_ffn1")
    f1, ffn1_saved = _ffn_fwd(h4, W["gate_t1"], W["up_t1"], W["down1"], "1")
    x4 = _resid_fwd(x3, m[1][5], f1, "resid_ffn1")

    g_fin = _row(small["final_norm_g"])

    def final_fn(rv, vv):
        xv, tv = rv
        yv, vjp = jax.vjp(_rms, xv, vv[0])
        err = yv - tv
        dx, dg = vjp(err / D)
        part = 0.5 * jnp.sum(jnp.mean(err * err, -1, keepdims=True), 0, keepdims=True)
        return [dx], [dg, jnp.broadcast_to(part, (1, LANES))]

    (dx4,), (d_fin, loss) = _rowwise("loss_head", final_fn, [x4, target], [g_fin], [(D, f32)], [D, LANES])

    dmod = [[None] * 6 for _ in range(2)]
    d_norm_mix, d_norm_ffn = [None, None], [None, None]
    big = {}

    df1, (dmod[1][5], _) = _resid_bwd(dx4, f1, m[1][5], "resid_ffn1_bwd")
    dh4, big["gate_t1"], big["up_t1"], big["down1"] = _ffn_bwd(h4, W["gate_t1"], W["up_t1"], W["down1"], ffn1_saved, df1, "1")
    dx3, (d_norm_ffn[1], dmod[1][4], dmod[1][3]) = _adaln_bwd(x3, g_ffn[1], m[1][4], m[1][3], dh4, dx4, "adaln_ffn1_bwd")
    dmix1, (dmod[1][2], d_b_pw2) = _resid_bwd(dx3, mix1, m[1][2], "resid_mix1_bwd")
    du3 = _mm(dmix1, W["pw2"], tb=True, name="cv_pw2_dx")
    big["pw2"] = _mm(u3, dmix1, ta=True, out_dtype=bf16, name="cv_pw2_dw")

    def lnsilu_bwd(rv, vv):
        _, vjp = jax.vjp(_lnsilu_f, rv[0], vv[0], vv[1])
        du, dg, db = vjp(rv[1])
        return [du], [dg, db]

    (du2,), (d_ln_g, d_ln_b) = _rowwise("cv_lnsilu_bwd", lnsilu_bwd, [u2, du3], [ln_g, ln_b], [(D, f32)], [D, D])
    du, d_w_dw, d_b_dw = _conv_bwd(u, small["cv_w_dw_full"], du2, None, silu=False, name="cv_dw_bwd")

    def glu_bwd(rv, vv):
        a, gt, d = rv
        _, vjp = jax.vjp(lambda a_, g_: a_ * jax.nn.sigmoid(g_), a, gt)
        da, dg = vjp(d)
        return [da, dg], [jnp.sum(da, 0, keepdims=True), jnp.sum(dg, 0, keepdims=True)]

    (dpa, dpg), (d_b1a, d_b1g) = _rowwise("cv_glu_bwd", glu_bwd, [(pw1, 0, D), (pw1, 1, D), du], [],
                                           [(D, bf16), (D, bf16)], [D, D], sub=16)
    dpw1 = jnp.concatenate([dpa, dpg], axis=1)
    d_b_pw1 = jnp.concatenate([d_b1a, d_b1g], axis=1)
    dh3 = _mm(dpw1, W["pw1"], tb=True, name="cv_pw1_dx")
    big["pw1"] = _mm(h3, dpw1, ta=True, out_dtype=bf16, name="cv_pw1_dw")
    token = comm.send("layer1", {nm: big[nm] for nm in ("gate_t1", "up_t1", "down1", "pw2", "pw1")})
    dx2, (d_norm_mix[1], dmod[1][1], dmod[1][0]) = _adaln_bwd(x2, g_mix[1], m[1][1], _tie(m[1][0], token), dh3, dx3,
                                                              "adaln_mix1_bwd")

    df0, (dmod[0][5], _) = _resid_bwd(dx2, f0, m[0][5], "resid_ffn0_bwd")
    dh2, big["gate_t0"], big["up_t0"], big["down0"] = _ffn_bwd(h2, W["gate_t0"], W["up_t0"], W["down0"], ffn0_saved, df0, "0")
    dx1, (d_norm_ffn[0], dmod[0][4], dmod[0][3]) = _adaln_bwd(x1, g_ffn[0], m[0][4], m[0][3], dh2, dx2, "adaln_ffn0_bwd")
    dmix0, (dmod[0][2], _) = _resid_bwd(dx1, mix0, m[0][2], "resid_mix0_bwd")
    dyg = _mm(dmix0, w_out_y, tb=True, name="out_y_dx")
    datt = _mm(dmix0, w_out_a, tb=True, name="out_a_dx")
    big["out"] = jnp.concatenate([_mm(yg, dmix0, ta=True, out_dtype=bf16, name="out_y_dw"),
                                  _mm(att_b, dmix0, ta=True, out_dtype=bf16, name="out_a_dw")], axis=0)
    token = comm.send("layer0", {nm: big[nm] for nm in ("gate_t0", "up_t0", "down0", "out")})
    bias = [_tie(b, token) for b in bias]
    g_ssm = _tie(g_ssm, token)

    dq, dks, dvs, dbs = [], [], [], []
    lse_heads = lse_tot[:, :H]
    dsum_heads = _head_sums(att, datt, "attn_dsum")[:, :H]
    for gi, d in enumerate(ATT_DILATIONS):
        a, b, c_, e = _attn_bwd(proj[f"q{gi}"], proj["k"], proj["v"], bias[gi], datt,
                                _by_block(lse_heads, d), _by_block(dsum_heads, d), d, f"attn_bwd_{gi}")
        dq.append(a)
        dks.append(b)
        dvs.append(c_)
        dbs.append(e)
    dk = _add3(*dks, "attn_dk")
    dv = _add3(*dvs, "attn_dv")
    d_rel = jnp.concatenate(
        [_exact_mm(dbs[gi].reshape(H, -1), onehot[gi], tb=True, name=f"rel_grad_{gi}") for gi in range(3)], axis=0).T

    def gate_bwd(rv, vv):
        _, vjp = jax.vjp(_gate_f, rv[0], rv[1], vv[0])
        dy_, dz_, dg_ = vjp(rv[2])
        return [dy_, dz_], [dg_]

    (dy, dz), (d_g_ssm,) = _rowwise("ssm_gate_bwd", gate_bwd, [y, proj["z"], dyg], [g_ssm], [(di, f32), (di, bf16)], [di],
                                    sub=16)
    dxbc, ddtraw, d_a_log, d_dskip, d_dt_bias = _ssd_bwd(xbc, proj["dt"], dt_bias, a_log, d_skip, hin, y, dy, di, "ssd_bwd")
    dxbc_pre, d_conv_w, d_conv_b = _conv_bwd(proj["xbc"], conv_w, dxbc, xbc_pre, silu=True, name="ssm_conv_bwd",
                                             dx_dtype=bf16, tr=1024)

    dseg = {"z": dz, "xbc": dxbc_pre, "dt": ddtraw, "q0": dq[0], "q1": dq[1], "q2": dq[2], "k": dk, "v": dv}
    dh1 = None
    d_in_parts = []
    for nm, (mat, off, cnt) in seg.items():
        dh1 = _mm(dseg[nm], mat, b_rows=(off, cnt), add=dh1, name=f"in_{nm}_dx")
        dwp = _mm(dseg[nm], h1, ta=True, out_dtype=bf16, name=f"in_{nm}_dw")
        d_in_parts.append(dwp[:nh] if nm == "dt" else dwp)
    big["in_t"] = jnp.concatenate(d_in_parts, axis=0)
    dx0, (d_norm_mix[0], dmod[0][1], dmod[0][0]) = _adaln_bwd(x, g_mix[0], m[0][1], m[0][0], dh1, dx1, "adaln_mix0_bwd")

    smallg = dict(
        loss=loss, dmod=jnp.stack([jnp.concatenate(dmod[i], axis=1)[0] for i in range(2)]),
        norm_mix_g=jnp.concatenate(d_norm_mix, axis=0), norm_ffn_g=jnp.concatenate(d_norm_ffn, axis=0),
        hy_conv_w=d_conv_w, hy_conv_b=d_conv_b, hy_dt_bias=d_dt_bias[:, :nh], hy_a_log=d_a_log[:, :nh],
        hy_d_skip=d_dskip[:, :nh], hy_ssm_norm_g=d_g_ssm, rel_table=d_rel,
        cv_b_pw1=d_b_pw1, cv_w_dw=d_w_dw, cv_b_dw=d_b_dw, cv_ln_g=d_ln_g, cv_ln_b=d_ln_b, cv_b_pw2=d_b_pw2,
        final_norm_g=d_fin)
    return dx0, big["in_t"], smallg


SMALL_GRAD_ORDER = ("loss", "dmod", "norm_mix_g", "norm_ffn_g", "hy_conv_w", "hy_conv_b", "hy_dt_bias", "hy_a_log",
                    "hy_d_skip", "hy_ssm_norm_g", "rel_table", "cv_b_pw1", "cv_w_dw", "cv_b_dw", "cv_ln_g", "cv_ln_b",
                    "cv_b_pw2", "final_norm_g")


def kernel(x, c, ada_w, ada_b, norm_mix_g, norm_ffn_g, hy_w_in, hy_conv_w, hy_conv_b, hy_dt_bias, hy_a_log, hy_d_skip, hy_ssm_norm_g, hy_w_out, rel_table, cv_w_pw1, cv_b_pw1, cv_w_dw, cv_b_dw, cv_ln_g, cv_ln_b, cv_w_pw2, cv_b_pw2, ffn_w_gate, ffn_w_up, ffn_w_down, final_norm_g, loss_target, m_ada_w, m_ada_b, m_norm_mix_g, m_norm_ffn_g, m_hy_w_in, m_hy_conv_w, m_hy_conv_b, m_hy_dt_bias, m_hy_a_log, m_hy_d_skip, m_hy_ssm_norm_g, m_hy_w_out, m_rel_table, m_cv_w_pw1, m_cv_b_pw1, m_cv_w_dw, m_cv_b_dw, m_cv_ln_g, m_cv_ln_b, m_cv_w_pw2, m_cv_b_pw2, m_ffn_w_gate, m_ffn_w_up, m_ffn_w_down, m_final_norm_g, v_ada_w, v_ada_b, v_norm_mix_g, v_norm_ffn_g, v_hy_w_in, v_hy_conv_w, v_hy_conv_b, v_hy_dt_bias, v_hy_a_log, v_hy_d_skip, v_hy_ssm_norm_g, v_hy_w_out, v_rel_table, v_cv_w_pw1, v_cv_b_pw1, v_cv_w_dw, v_cv_b_dw, v_cv_ln_g, v_cv_ln_b, v_cv_w_pw2, v_cv_b_pw2, v_ffn_w_gate, v_ffn_w_up, v_ffn_w_down, v_final_norm_g):
    names = ("ada_w", "ada_b", "norm_mix_g", "norm_ffn_g", "hy_w_in", "hy_conv_w", "hy_conv_b", "hy_dt_bias", "hy_a_log",
             "hy_d_skip", "hy_ssm_norm_g", "hy_w_out", "rel_table", "cv_w_pw1", "cv_b_pw1", "cv_w_dw", "cv_b_dw", "cv_ln_g",
             "cv_ln_b", "cv_w_pw2", "cv_b_pw2", "ffn_w_gate", "ffn_w_up", "ffn_w_down", "final_norm_g")
    w = dict(zip(names, (ada_w, ada_b, norm_mix_g, norm_ffn_g, hy_w_in, hy_conv_w, hy_conv_b, hy_dt_bias, hy_a_log, hy_d_skip,
                         hy_ssm_norm_g, hy_w_out, rel_table, cv_w_pw1, cv_b_pw1, cv_w_dw, cv_b_dw, cv_ln_g, cv_ln_b, cv_w_pw2,
                         cv_b_pw2, ffn_w_gate, ffn_w_up, ffn_w_down, final_norm_g)))
    mom = dict(zip(names, (m_ada_w, m_ada_b, m_norm_mix_g, m_norm_ffn_g, m_hy_w_in, m_hy_conv_w, m_hy_conv_b, m_hy_dt_bias,
                           m_hy_a_log, m_hy_d_skip, m_hy_ssm_norm_g, m_hy_w_out, m_rel_table, m_cv_w_pw1, m_cv_b_pw1, m_cv_w_dw,
                           m_cv_b_dw, m_cv_ln_g, m_cv_ln_b, m_cv_w_pw2, m_cv_b_pw2, m_ffn_w_gate, m_ffn_w_up, m_ffn_w_down,
                           m_final_norm_g)))
    vel = dict(zip(names, (v_ada_w, v_ada_b, v_norm_mix_g, v_norm_ffn_g, v_hy_w_in, v_hy_conv_w, v_hy_conv_b, v_hy_dt_bias,
                           v_hy_a_log, v_hy_d_skip, v_hy_ssm_norm_g, v_hy_w_out, v_rel_table, v_cv_w_pw1, v_cv_b_pw1, v_cv_w_dw,
                           v_cv_b_dw, v_cv_ln_g, v_cv_ln_b, v_cv_w_pw2, v_cv_b_pw2, v_ffn_w_gate, v_ffn_w_up, v_ffn_w_down,
                           v_final_norm_g)))
    S, D = x.shape[1], x.shape[2]
    ax, ay, ac = lax.axis_index("x"), lax.axis_index("y"), lax.axis_index("c")
    me = 4 * ax + 2 * ay + ac
    c_arr = jnp.reshape(ac, (1,)).astype(jnp.int32)
    nmod = ada_w.shape[2]

    w2 = _shards_2d(w)
    big_names = list(w2)
    (g_in,) = _all_gather([w2["in_t"].astype(bf16)], "gather_w_in")
    w_in_t = _full_from_blocks("in_t", g_in)

    sharded_small = ("hy_conv_w", "cv_b_pw1", "cv_w_dw", "cv_b_dw", "cv_ln_g", "cv_ln_b", "cv_b_pw2")
    vp = _VecPack([c.shape] + [w[nm].shape for nm in sharded_small])
    (sg,) = _all_gather([vp.pack([c] + [w[nm] for nm in sharded_small])], "gather_vectors")
    parts = vp.unpack_stacked(sg)
    c_all = parts[0][:, 0]
    small = {k: w[k] for k in ("norm_mix_g", "norm_ffn_g", "hy_conv_b", "hy_dt_bias", "hy_a_log", "hy_d_skip",
                               "hy_ssm_norm_g", "rel_table", "final_norm_g")}
    for p, nm in zip(parts[1:], sharded_small):
        p = p[:, 0]
        p = jnp.moveaxis(p, 0, -2)
        small[nm + "_full"] = p.reshape(p.shape[:-2] + (N_DEV * p.shape[-1],))

    (cs_all,), _ = _rowwise("ada_silu", lambda rv, vv: ([_silu(rv[0])], []), [c_all], [], [(D, f32)], [])
    b_mine = lax.dynamic_slice_in_dim(ada_b, me * nmod, nmod, axis=1)
    mod_part = jnp.stack([_mm(cs_all, ada_w[i], bias=b_mine[i:i + 1], name=f"ada_mod_{i}") for i in range(2)])
    (mod_all,) = _all_gather([mod_part.reshape(2 * N_DEV, nmod)], "gather_mod")
    mod_all = mod_all.reshape(N_DEV, 2, N_DEV, nmod)
    mod_mine = lax.dynamic_index_in_dim(mod_all, me, axis=2, keepdims=False)
    mod = jnp.transpose(mod_mine, (1, 0, 2)).reshape(2, 6, D)
    comm = _Overlap({nm: w2[nm].astype(bf16) for nm in big_names if nm != "in_t"}, me, after=[mod, w_in_t])
    mod = _tie(mod, comm.token)

    dx0, d_in_t, sgrad = _local_step(x[0], loss_target[0], mod, w_in_t, comm, small)
    comm.send("in", {"in_t": d_in_t})

    gp = _VecPack([sgrad[k].shape for k in SMALL_GRAD_ORDER])
    (g_all,) = _all_gather([gp.pack([sgrad[k] for k in SMALL_GRAD_ORDER])], "gather_small_grads")
    tot = dict(zip(SMALL_GRAD_ORDER, gp.unpack(_sum_slots(g_all, "sum_small_grads"))))
    dmod_all = gp.unpack_stacked(g_all, only=SMALL_GRAD_ORDER.index("dmod"))
    loss = tot["loss"][0, 0]

    grads = {}
    dmod_mine = lax.dynamic_slice_in_dim(dmod_all, me * nmod, nmod, axis=2)
    grads["ada_w"] = jnp.stack([_mm(cs_all, dmod_mine[:, i], ta=True, name=f"ada_w_grad_{i}") for i in range(2)])
    grads["ada_b"] = tot["dmod"]
    grads["norm_mix_g"], grads["norm_ffn_g"] = tot["norm_mix_g"], tot["norm_ffn_g"]
    grads["hy_conv_b"] = tot["hy_conv_b"]
    grads["hy_dt_bias"] = tot["hy_dt_bias"]
    grads["hy_a_log"] = tot["hy_a_log"]
    grads["hy_d_skip"] = tot["hy_d_skip"]
    grads["hy_ssm_norm_g"] = tot["hy_ssm_norm_g"]
    grads["rel_table"] = tot["rel_table"]
    grads["final_norm_g"] = tot["final_norm_g"][0]
    for nm in sharded_small:
        n = w[nm].shape[-1]
        grads[nm] = lax.dynamic_slice_in_dim(tot[nm], me * n, n, axis=1).reshape(w[nm].shape)

    delta, new_m, new_v = {}, {}, {}
    shp = ada_w.shape
    two = lambda t: t.reshape(-1, shp[-1])
    d_, m_, v_ = _adamw(two(ada_w), two(grads["ada_w"]), two(m_ada_w), two(v_ada_w), "adamw_ada_w")
    delta["ada_w"], new_m["ada_w"], new_v["ada_w"] = d_.reshape(shp), m_.reshape(shp), v_.reshape(shp)
    rest = [nm for nm in names if nm not in BIG_PARAMS and nm != "ada_w"]
    sp = _VecPack([w[nm].shape for nm in rest])
    packs = [sp.pack([t[nm] for nm in rest]) for t in (w, grads, mom, vel)]
    ds_, ms_, vs_ = _adamw(*packs, "adamw_small")
    for nm, a, b, e in zip(rest, sp.unpack(ds_), sp.unpack(ms_), sp.unpack(vs_)):
        delta[nm], new_m[nm], new_v[nm] = a, b, e

    g2 = {}
    after = d_
    for tag in ("layer1", "layer0", "in"):
        for nm, slots in comm.received(tag, after).items():
            g2[nm] = _sum_slots(slots, f"sum_{nm}")
            after = g2[nm]
    grads.update(_unshard_2d(g2))
    m2, v2 = _shards_2d(mom), _shards_2d(vel)
    d2, nm2, nv2 = {}, {}, {}
    for nm in big_names:
        d2[nm], nm2[nm], nv2[nm] = _adamw(w2[nm], g2[nm], m2[nm], v2[nm], f"adamw_{nm}")
    delta.update(_unshard_2d(d2))
    new_m.update(_unshard_2d(nm2))
    new_v.update(_unshard_2d(nv2))

    return (loss, dx0[None], *[grads[n] for n in names], *[delta[n] for n in names],
            *[new_m[n] for n in names], *[new_v[n] for n in names])
```

```python
import functools
import math

import numpy as np
import jax
import jax.numpy as jnp
from jax import lax
from jax.experimental import pallas as pl
from jax.experimental.pallas import tpu as pltpu

f32 = jnp.float32
bf16 = jnp.bfloat16
EPS = 1e-6
N_DEV = 8
LANES = 128
SSM_STATE = 128
SSM_CHUNK = 128
SSM_GROUPS = 4
HEAD_DIM = 64
ATT_BLK = 128
ATT_DILATIONS = (1, 4, 16)
REL_BUCKETS = 32
REL_MAX_DIST = 2048
ADAM_LR, ADAM_B1, ADAM_B2, ADAM_EPS, ADAM_WD, ADAM_STEP = 0.001, 0.9, 0.999, 1e-08, 0.01, 10
PACK_COLS = 1024
PACK_ROW_TILE = 256
MESH = pl.DeviceIdType.MESH
VMEM_LIMIT = 48 * 1024 * 1024


def _sds(shape, dtype=f32):
    return jax.ShapeDtypeStruct(tuple(shape), dtype)


def _tile(n, cap, mult):
    best = None
    t = mult
    while t <= min(n, cap):
        if n % t == 0:
            best = t
        t += mult
    return best if best is not None else n


def _params(sem):
    return pltpu.CompilerParams(dimension_semantics=sem, vmem_limit_bytes=VMEM_LIMIT)


def _mm(a, b, *, name, ta=False, tb=False, b_rows=None, bias=None, add=None, out_dtype=f32,
        tm_cap=512, tn_cap=1536, tk_cap=8192):
    if ta:
        K, M = a.shape
    else:
        M, K = a.shape
    off, cnt = b_rows if b_rows is not None else (0, b.shape[0])
    if tb:
        N, K2 = cnt, b.shape[1]
    else:
        K2, N = cnt, b.shape[1]
    assert K == K2, (a.shape, b.shape, ta, tb, b_rows)
    if ta and a.dtype == f32:
        tm_cap = min(tm_cap, 256)
    tm = _tile(M, tm_cap, LANES)
    tn = _tile(math.gcd(off, N) if tb else N, tn_cap, LANES)
    tk = _tile(K if tb else math.gcd(off, K), tk_cap, LANES)
    assert N % tn == 0 and K % tk == 0 and off % (tn if tb else tk) == 0, (name, off, N, K, tn, tk)
    nk = K // tk
    jo, ko = (off // tn, 0) if tb else (0, off // tk)
    has_bias, has_add = bias is not None, add is not None
    dn = (((0 if ta else 1,), (1 if tb else 0,)), ((), ()))

    def body(*refs):
        a_ref, b_ref = refs[0], refs[1]
        pos = 2
        bias_ref = add_ref = None
        if has_bias:
            bias_ref = refs[pos]
            pos += 1
        if has_add:
            add_ref = refs[pos]
            pos += 1
        o_ref = refs[pos]
        k = pl.program_id(2)
        part = lax.dot_general(a_ref[...].astype(bf16), b_ref[...].astype(bf16), dn, preferred_element_type=f32)

        def finish(r):
            if has_bias:
                r = r + bias_ref[...]
            if has_add:
                r = r + add_ref[...]
            o_ref[...] = r.astype(o_ref.dtype)

        if nk == 1:
            finish(part)
        else:
            acc_ref = refs[pos + 1]

            @pl.when(k == 0)
            def _():
                acc_ref[...] = part

            @pl.when((k > 0) & (k < nk - 1))
            def _():
                acc_ref[...] += part

            @pl.when(k == nk - 1)
            def _():
                finish(acc_ref[...] + part)

    in_specs = [
        pl.BlockSpec((tk, tm), lambda i, j, k: (k, i)) if ta else pl.BlockSpec((tm, tk), lambda i, j, k: (i, k)),
        pl.BlockSpec((tn, tk), lambda i, j, k: (j + jo, k)) if tb else pl.BlockSpec((tk, tn), lambda i, j, k: (k + ko, j)),
    ]
    args = [a, b]
    if has_bias:
        in_specs.append(pl.BlockSpec((1, tn), lambda i, j, k: (0, j)))
        args.append(bias)
    if has_add:
        in_specs.append(pl.BlockSpec((tm, tn), lambda i, j, k: (i, j)))
        args.append(add)
    return pl.pallas_call(
        body, name=name, grid=(M // tm, N // tn, nk), in_specs=in_specs,
        out_specs=pl.BlockSpec((tm, tn), lambda i, j, k: (i, j)), out_shape=_sds((M, N), out_dtype),
        scratch_shapes=[pltpu.VMEM((tm, tn), f32)] if nk > 1 else [],
        compiler_params=_params(("parallel", "parallel", "arbitrary")),
    )(*args)


def _rowwise(name, fn, rows, vecs, out_rows, out_accs, *, tr_cap=256, sub=8, col_chunk=None):
    rows = [r if isinstance(r, tuple) else (r, 0, r.shape[1]) for r in rows]
    R = rows[0][0].shape[0]
    tr = _tile(R, tr_cap, 8)
    sub = sub if tr % sub == 0 else tr
    n_r, n_v, n_or, n_oa = len(rows), len(vecs), len(out_rows), len(out_accs)

    def body(*refs):
        row_refs = refs[:n_r]
        vec_refs = refs[n_r:n_r + n_v]
        orow_refs = refs[n_r + n_v:n_r + n_v + n_or]
        oacc_refs = refs[n_r + n_v + n_or:]
        vv = [r[...] for r in vec_refs]

        n_sub = tr // sub
        together = 4 if n_sub % 4 == 0 else 1

        def step(s, accs):
            for t in range(together):
                sl = pl.ds(pl.multiple_of((s * together + t) * sub, sub), sub)
                if col_chunk is None:
                    ro, ao = fn([r[sl, :] for r in row_refs], vv)
                    for o_ref, o in zip(orow_refs, ro):
                        o_ref[sl, :] = o.astype(o_ref.dtype)
                    accs = tuple(x + y for x, y in zip(accs, ao))
                else:
                    for c0 in range(0, rows[0][2], col_chunk):
                        cs_ = pl.ds(c0, col_chunk)
                        ro, _ = fn([r[sl, cs_] for r in row_refs], vv)
                        for o_ref, o in zip(orow_refs, ro):
                            o_ref[sl, cs_] = o.astype(o_ref.dtype)
            return accs

        accs = lax.fori_loop(0, n_sub // together, step, tuple(jnp.zeros((1, w), f32) for w in out_accs))
        if n_oa:
            @pl.when(pl.program_id(0) == 0)
            def _():
                for ref in oacc_refs:
                    ref[...] = jnp.zeros_like(ref)

            for ref, x in zip(oacc_refs, accs):
                ref[...] += x

    in_specs = [pl.BlockSpec((tr, w), functools.partial(lambda i, cb: (i, cb), cb=cb)) for (_, cb, w) in rows]
    in_specs += [pl.BlockSpec((1, v.shape[1]), lambda i: (0, 0)) for v in vecs]
    out_specs = [pl.BlockSpec((tr, w), lambda i: (i, 0)) for (w, _) in out_rows]
    out_specs += [pl.BlockSpec((1, w), lambda i: (0, 0)) for w in out_accs]
    out_shape = [_sds((R, w), dt) for (w, dt) in out_rows] + [_sds((1, w)) for w in out_accs]
    res = pl.pallas_call(
        body, name=name, grid=(R // tr,), in_specs=in_specs, out_specs=out_specs, out_shape=out_shape,
        compiler_params=_params(("arbitrary",)),
    )(*[r[0] for r in rows], *vecs)
    return res[:n_or], res[n_or:]


def _silu(x):
    return x * jax.nn.sigmoid(x)


def _rms(x, g):
    return x * lax.rsqrt(jnp.mean(x * x, -1, keepdims=True) + EPS) * g


def _adaln_f(x, g, sc, sh):
    return _rms(x, g) * (1.0 + sc) + sh


def _gate_f(y, z, g):
    return _rms(y * _silu(z), g)


def _lnsilu_f(u, g, b):
    mu = jnp.mean(u, -1, keepdims=True)
    var = jnp.mean(jnp.square(u - mu), -1, keepdims=True)
    return _silu((u - mu) * lax.rsqrt(var + EPS) * g + b)


def _adaln_fwd(x, g, sc, sh, name):
    (h,), _ = _rowwise(name, lambda rv, vv: ([_adaln_f(rv[0], *vv)], []), [x], [g, sc, sh], [(x.shape[1], bf16)], [],
                       sub=16)
    return h


def _adaln_bwd(x, g, sc, sh, dh, dres, name):
    def fn(rv, vv):
        xv, dhv, drv = rv
        _, vjp = jax.vjp(_adaln_f, xv, *vv)
        dx, dg, dsc, dsh = vjp(dhv)
        return [dx + drv], [dg, dsc, dsh]
    w = x.shape[1]
    (dx,), accs = _rowwise(name, fn, [x, dh, dres], [g, sc, sh], [(w, f32)], [w, w, w])
    return dx, accs


def _resid_fwd(x, gate, mix, name):
    (y,), _ = _rowwise(name, lambda rv, vv: ([rv[0] + vv[0] * rv[1]], []), [x, mix], [gate], [(x.shape[1], f32)], [])
    return y


def _resid_bwd(dx, mix, gate, name):
    def fn(rv, vv):
        dxv, mv = rv
        dm = vv[0] * dxv
        return [dm], [jnp.sum(dxv * mv, 0, keepdims=True), jnp.sum(dm, 0, keepdims=True)]
    w = dx.shape[1]
    (dmix,), accs = _rowwise(name, fn, [dx, mix], [gate], [(w, bf16)], [w, w], sub=16)
    return dmix, accs


def _add3(a, b, c, name):
    (y,), _ = _rowwise(name, lambda rv, vv: ([rv[0] + rv[1] + rv[2]], []), [a, b, c], [], [(a.shape[1], bf16)], [],
                       sub=16)
    return y


CONV_HALO = 32
CONV_ROWS = 64


def _conv_fwd(x, w, b, *, silu, name, tr=512):
    S, C = x.shape
    K = w.shape[0]
    H = CONV_HALO
    assert K - 1 <= H and S % tr == 0 and tr % H == 0 and C % LANES == 0
    nh = tr // H

    def body(xp_ref, xc_ref, w_ref, b_ref, *rest):
        outs, scr = rest[:-1], rest[-1]
        i = pl.program_id(1)
        scr[pl.ds(0, H), :] = jnp.where(i > 0, xp_ref[...], 0.0)
        scr[pl.ds(H, tr), :] = xc_ref[...]
        taps = [w_ref[pl.ds(k, 1), :] for k in range(K)]
        for c0 in range(0, tr, CONV_ROWS):
            acc = jnp.zeros((CONV_ROWS, LANES), f32) + b_ref[...]
            for k in range(K):
                acc = acc + scr[pl.ds(c0 + H - (K - 1) + k, CONV_ROWS), :] * taps[k]
            outs[0][pl.ds(c0, CONV_ROWS), :] = acc
            if silu:
                outs[1][pl.ds(c0, CONV_ROWS), :] = _silu(acc)

    n_out = 2 if silu else 1
    return pl.pallas_call(
        body, name=name, grid=(C // LANES, S // tr),
        in_specs=[pl.BlockSpec((H, LANES), lambda j, i: (jnp.maximum(i * nh - 1, 0), j)),
                  pl.BlockSpec((tr, LANES), lambda j, i: (i, j)),
                  pl.BlockSpec((K, LANES), lambda j, i: (0, j)),
                  pl.BlockSpec((1, LANES), lambda j, i: (0, j))],
        out_specs=[pl.BlockSpec((tr, LANES), lambda j, i: (i, j))] * n_out,
        out_shape=[_sds((S, C))] * n_out,
        scratch_shapes=[pltpu.VMEM((tr + H, LANES), f32)],
        compiler_params=_params(("parallel", "arbitrary")),
    )(x, x, w, b)


def _conv_bwd(x, w, dact, pre, *, silu, name, dx_dtype=f32, tr=512):
    S, C = x.shape
    K = w.shape[0]
    H = CONV_HALO
    nh = tr // H
    n_i = S // tr
    kp = -(-K // 8) * 8

    def dsilu(p):
        s = jax.nn.sigmoid(p)
        return s * (1.0 + p * (1.0 - s))

    def body(*refs):
        if silu:
            xp_ref, xc_ref, w_ref, dc_ref, dn_ref, pc_ref, pn_ref, dx_ref, dw_ref, db_ref, xs, ds = refs
        else:
            xp_ref, xc_ref, w_ref, dc_ref, dn_ref, dx_ref, dw_ref, db_ref, xs, ds = refs
        i = pl.program_id(1)
        xs[pl.ds(0, H), :] = jnp.where(i > 0, xp_ref[...], 0.0)
        xs[pl.ds(H, tr), :] = xc_ref[...]
        dcur = dc_ref[...]
        dnext = dn_ref[...]
        if silu:
            dcur = dcur * dsilu(pc_ref[...])
            dnext = dnext * dsilu(pn_ref[...])
        ds[pl.ds(0, tr), :] = dcur
        ds[pl.ds(tr, H), :] = jnp.where(i < n_i - 1, dnext, 0.0)
        taps = [w_ref[pl.ds(k, 1), :] for k in range(K)]
        fold = lambda t: jnp.sum(t.reshape(CONV_ROWS // 8, 8, LANES), axis=0)
        dw_parts = [jnp.zeros((8, LANES), f32) for _ in range(K)]
        db_part = jnp.zeros((8, LANES), f32)
        for c0 in range(0, tr, CONV_ROWS):
            acc = jnp.zeros((CONV_ROWS, LANES), f32)
            d_c = ds[pl.ds(c0, CONV_ROWS), :]
            for k in range(K):
                acc = acc + ds[pl.ds(c0 + K - 1 - k, CONV_ROWS), :] * taps[k]
                dw_parts[k] = dw_parts[k] + fold(d_c * xs[pl.ds(c0 + H - (K - 1) + k, CONV_ROWS), :])
            db_part = db_part + fold(d_c)
            dx_ref[pl.ds(c0, CONV_ROWS), :] = acc.astype(dx_ref.dtype)

        @pl.when(i == 0)
        def _():
            dw_ref[...] = jnp.zeros_like(dw_ref)
            db_ref[...] = jnp.zeros_like(db_ref)

        for k in range(K):
            dw_ref[pl.ds(k, 1), :] += jnp.sum(dw_parts[k], 0, keepdims=True)
        db_ref[...] += jnp.sum(db_part, 0, keepdims=True)

    prev = pl.BlockSpec((H, LANES), lambda j, i: (jnp.maximum(i * nh - 1, 0), j))
    cur = pl.BlockSpec((tr, LANES), lambda j, i: (i, j))
    nxt = pl.BlockSpec((H, LANES), lambda j, i: (jnp.minimum((i + 1) * nh, n_i * nh - 1), j))
    in_specs = [prev, cur, pl.BlockSpec((K, LANES), lambda j, i: (0, j)), cur, nxt]
    args = [x, x, w, dact, dact]
    if silu:
        in_specs += [cur, nxt]
        args += [pre, pre]
    dx, dw, db = pl.pallas_call(
        body, name=name, grid=(C // LANES, n_i), in_specs=in_specs,
        out_specs=[cur, pl.BlockSpec((kp, LANES), lambda j, i: (0, j)), pl.BlockSpec((1, LANES), lambda j, i: (0, j))],
        out_shape=[_sds((S, C), dx_dtype), _sds((kp, C)), _sds((1, C))],
        scratch_shapes=[pltpu.VMEM((tr + H, LANES), f32), pltpu.VMEM((tr + H, LANES), f32)],
        compiler_params=_params(("parallel", "arbitrary")),
    )(*args)
    return dx, dw[:K], db


def _dot(a, b):
    return jnp.dot(a.astype(bf16), b.astype(bf16), preferred_element_type=f32)


def _dot_nt(a, b):
    return lax.dot_general(a.astype(bf16), b.astype(bf16), (((1,), (1,)), ((), ())), preferred_element_type=f32)


def _dot_tn(a, b):
    return lax.dot_general(a.astype(bf16), b.astype(bf16), (((0,), (0,)), ((), ())), preferred_element_type=f32)


def _softplus(x):
    return jnp.maximum(x, 0.0) + jnp.log(1.0 + jnp.exp(-jnp.abs(x)))


def _tri(q):
    i = lax.broadcasted_iota(jnp.int32, (q, q), 0)
    j = lax.broadcasted_iota(jnp.int32, (q, q), 1)
    return i >= j


def _ssd_prep(dtraw, dt_bias, a_log):
    q = dtraw.shape[0]
    dt = _softplus(dtraw + dt_bias)
    A = -jnp.exp(a_log)
    tri = _tri(q)
    cs = jnp.dot(tri.astype(f32), dt * A, preferred_element_type=f32, precision=lax.Precision.HIGHEST)
    return dt, A, cs, cs.T, tri


def _expand(cols, h0, n, width):
    q = cols.shape[0]
    return jnp.concatenate([jnp.broadcast_to(cols[:, h0 + r:h0 + r + 1], (q, width)) for r in range(n)], axis=1)


def _ssd_fwd(xbc, dtraw, dt_bias, a_log, d_skip, di, name):
    S, CD = xbc.shape
    Q, N, G = SSM_CHUNK, SSM_STATE, SSM_GROUPS
    nc = S // Q
    nh = di // HEAD_DIM
    R = nh // G
    gw = R * HEAD_DIM

    def body(xbc_ref, dt_ref, bias_ref, alog_ref, dsk_ref, y_ref, hin_ref, state):
        c = pl.program_id(0)

        @pl.when(c == 0)
        def _():
            state[...] = jnp.zeros_like(state)

        hin_ref[...] = state[...]
        dt, A, cs, csT, tri = _ssd_prep(dt_ref[...], bias_ref[...], alog_ref[...])
        dsk = dsk_ref[...]
        ecs = jnp.exp(cs)
        dend = jnp.exp(cs[Q - 1:Q, :] - cs)
        elast = jnp.exp(cs[Q - 1:Q, :])
        for g in range(G):
            h0 = g * R
            Bg = xbc_ref[:, pl.ds(di + g * N, N)]
            Cg = xbc_ref[:, pl.ds(di + G * N + g * N, N)]
            xg = xbc_ref[:, pl.ds(g * gw, gw)]
            Hg = state[pl.ds(g * gw, gw), :]
            Gm = _dot_nt(Cg, Bg)
            xdt = xg * _expand(dt, h0, R, HEAD_DIM)
            yoff = _dot_nt(Cg, Hg) * _expand(ecs, h0, R, HEAD_DIM)
            ys = []
            for r in range(R):
                h = h0 + r
                L = jnp.exp(jnp.where(tri, cs[:, h:h + 1] - csT[h:h + 1, :], -jnp.inf))
                ys.append(_dot(Gm * L, xdt[:, r * HEAD_DIM:(r + 1) * HEAD_DIM]))
            y = jnp.concatenate(ys, axis=1) + yoff + xg * _expand(dsk, h0, R, HEAD_DIM)
            y_ref[:, pl.ds(g * gw, gw)] = y
            hnew = _dot_tn(xdt * _expand(dend, h0, R, HEAD_DIM), Bg)
            escale = jnp.concatenate([jnp.broadcast_to(elast[:, h0 + r:h0 + r + 1], (HEAD_DIM, N)) for r in range(R)], axis=0)
            state[pl.ds(g * gw, gw), :] = escale * Hg + hnew

    vec = pl.BlockSpec((1, LANES), lambda c: (0, 0))
    return pl.pallas_call(
        body, name=name, grid=(nc,),
        in_specs=[pl.BlockSpec((Q, CD), lambda c: (c, 0)), pl.BlockSpec((Q, LANES), lambda c: (c, 0)), vec, vec, vec],
        out_specs=[pl.BlockSpec((Q, di), lambda c: (c, 0)), pl.BlockSpec((None, di, N), lambda c: (c, 0, 0))],
        out_shape=[_sds((S, di)), _sds((nc, di, N))],
        scratch_shapes=[pltpu.VMEM((di, N), f32)],
        compiler_params=_params(("arbitrary",)),
    )(xbc, dtraw, dt_bias, a_log, d_skip)


def _dot_exact(a, b):
    bb = b.astype(bf16)
    hi = a.astype(bf16)
    rest = a - hi.astype(f32)
    mid = rest.astype(bf16)
    low = (rest - mid.astype(f32)).astype(bf16)
    one_pass = lambda t: jnp.dot(t, bb, preferred_element_type=f32)
    return one_pass(hi) + one_pass(mid) + one_pass(low)


def _ssd_bwd(xbc, dtraw, dt_bias, a_log, d_skip, hin, y, dy, di, name):
    S, CD = xbc.shape
    Q, N, G = SSM_CHUNK, SSM_STATE, SSM_GROUPS
    nc = S // Q
    nh = di // HEAD_DIM
    R = nh // G
    gw = R * HEAD_DIM
    P = HEAD_DIM
    head_of_col = jnp.asarray((np.arange(di)[:, None] // P == np.arange(LANES)[None, :]).astype(np.float32))
    dsk_wide = jnp.repeat(d_skip[0, :nh], P)[None]

    def body(xbc_ref, dt_ref, bias_ref, alog_ref, dskw_ref, hoc_ref, hin_ref, y_ref, dy_ref,
             dxbc_ref, ddt_ref, dA_ref, ddsk_ref, dtb_ref, dstate, dxdt_all, tend_all, yoff_all, colterm_all):
        c = pl.program_id(0)

        @pl.when(c == 0)
        def _():
            dstate[...] = jnp.zeros_like(dstate)
            dA_ref[...] = jnp.zeros_like(dA_ref)
            ddsk_ref[...] = jnp.zeros_like(ddsk_ref)
            dtb_ref[...] = jnp.zeros_like(dtb_ref)

        dtraw_v = dt_ref[...]
        dt, A, cs, csT, tri = _ssd_prep(dtraw_v, bias_ref[...], alog_ref[...])
        tri_t = jnp.logical_not(tri) | (lax.broadcasted_iota(jnp.int32, (Q, Q), 0) == lax.broadcasted_iota(jnp.int32, (Q, Q), 1))
        ecs = jnp.exp(cs)
        dend = jnp.exp(cs[Q - 1:Q, :] - cs)
        elast = jnp.exp(cs[Q - 1:Q, :])
        hoc = hoc_ref[...]
        state_dot = jnp.sum(_dot_exact(dstate[...] * hin_ref[...], jnp.ones((N, LANES), f32)) * hoc, 0, keepdims=True) * elast
        for g in range(G):
            h0 = g * R
            Bg = xbc_ref[:, pl.ds(di + g * N, N)]
            Cg = xbc_ref[:, pl.ds(di + G * N + g * N, N)]
            xg = xbc_ref[:, pl.ds(g * gw, gw)]
            dyg = dy_ref[:, pl.ds(g * gw, gw)]
            Hg = hin_ref[pl.ds(g * gw, gw), :]
            dHg = dstate[pl.ds(g * gw, gw), :]
            dt_e = _expand(dt, h0, R, P)
            ecs_e = _expand(ecs, h0, R, P)
            dend_e = _expand(dend, h0, R, P)
            cols = pl.ds(g * gw, gw)
            Gm = _dot_nt(Cg, Bg)
            Gm_t = _dot_nt(Bg, Cg)
            xdt = xg * dt_e
            dye = dyg * ecs_e
            bdh = _dot_nt(Bg, dHg)
            dC = _dot(dye, Hg)
            dB = _dot(xdt * dend_e, dHg)
            dHin = _dot_tn(dye, Cg)
            dxdt_state = dend_e * bdh
            end_term = xdt * dxdt_state
            tend_all[:, cols] = end_term
            yoff_all[:, cols] = _dot_nt(Cg, Hg) * ecs_e
            dG = jnp.zeros((Q, Q), f32)
            dxd = []
            for r in range(R):
                h = h0 + r
                sl = slice(r * P, (r + 1) * P)
                seg = cs[:, h:h + 1] - csT[h:h + 1, :]
                L = jnp.exp(jnp.where(tri, seg, -jnp.inf))
                L_t = jnp.exp(jnp.where(tri_t, -seg, -jnp.inf))
                dyh = dyg[:, sl]
                dG = dG + _dot_nt(dyh, xdt[:, sl]) * L
                dxd.append(_dot(Gm_t * L_t, dyh))
            dxdt_diag = jnp.concatenate(dxd, axis=1)
            dxdt = dxdt_diag + dxdt_state
            dxdt_all[:, cols] = dxdt
            colterm_all[:, cols] = xdt.astype(bf16).astype(f32) * dxdt_diag + end_term
            dxbc_ref[:, cols] = dxdt * dt_e + dyg * dskw_ref[:, cols]
            dxbc_ref[:, pl.ds(di + g * N, N)] = dB + _dot_tn(dG, Cg)
            dxbc_ref[:, pl.ds(di + G * N + g * N, N)] = dC + _dot(dG, Bg)
            escale = jnp.concatenate([jnp.broadcast_to(elast[:, h0 + r:h0 + r + 1], (P, N)) for r in range(R)], axis=0)
            dstate[pl.ds(g * gw, gw), :] = escale * dHg + dHin
        xs = xbc_ref[:, pl.ds(0, di)]
        dyv = dy_ref[...]
        yoff = yoff_all[...]
        y_diag = y_ref[...] - dskw_ref[...] * xs - yoff
        rs_y = _dot_exact(dyv.astype(bf16).astype(f32) * y_diag + dyv * yoff, hoc)
        rs_c = _dot_exact(colterm_all[...], hoc)
        rs_x = _dot_exact(dxdt_all[...] * xs, hoc)
        end_dot = _dot_exact(jnp.broadcast_to(jnp.sum(tend_all[...], 0, keepdims=True), (8, di)), hoc)[0:1]
        last = lax.broadcasted_iota(jnp.int32, (Q, 1), 0) == Q - 1
        dcs = rs_y - rs_c + jnp.where(last, end_dot + state_dot, 0.0)
        da = lax.dot_general(tri.astype(f32), dcs, (((0,), (0,)), ((), ())), preferred_element_type=f32,
                             precision=lax.Precision.HIGHEST)
        ddt = da * A + rs_x
        ddtraw = ddt * jax.nn.sigmoid(dtraw_v + bias_ref[...])
        ddt_ref[...] = ddtraw.astype(ddt_ref.dtype)
        dA_ref[...] += jnp.sum(da * dt, 0, keepdims=True) * A
        ddsk_ref[...] += jnp.sum(_dot_exact(dyv * xs, hoc), 0, keepdims=True)
        dtb_ref[...] += jnp.sum(ddtraw, 0, keepdims=True)

    vec = pl.BlockSpec((1, LANES), lambda c: (0, 0))
    rev = lambda c: (nc - 1 - c, 0)
    return pl.pallas_call(
        body, name=name, grid=(nc,),
        in_specs=[pl.BlockSpec((Q, CD), rev), pl.BlockSpec((Q, LANES), rev), vec, vec,
                  pl.BlockSpec((1, di), lambda c: (0, 0)), pl.BlockSpec((di, LANES), lambda c: (0, 0)),
                  pl.BlockSpec((None, di, N), lambda c: (nc - 1 - c, 0, 0)), pl.BlockSpec((Q, di), rev),
                  pl.BlockSpec((Q, di), rev)],
        out_specs=[pl.BlockSpec((Q, CD), rev), pl.BlockSpec((Q, LANES), rev), vec, vec, vec],
        out_shape=[_sds((S, CD)), _sds((S, LANES), bf16), _sds((1, LANES)), _sds((1, LANES)), _sds((1, LANES))],
        scratch_shapes=[pltpu.VMEM((di, N), f32)] + [pltpu.VMEM((Q, di), f32)] * 4,
        compiler_params=_params(("arbitrary",)),
    )(xbc, dtraw, dt_bias, a_log, dsk_wide, head_of_col, hin, y, dy)


def _t5_bucket_np(dist):
    max_exact = REL_BUCKETS // 2
    n = np.maximum(dist, 1).astype(np.float32)
    large = np.float32(max_exact) + np.log(n / np.float32(max_exact)) / np.float32(math.log(REL_MAX_DIST / max_exact)) * np.float32(REL_BUCKETS - max_exact)
    large = np.minimum(large.astype(np.int32), REL_BUCKETS - 1)
    return np.where(dist < max_exact, dist, large)


def _bucket_onehot():
    i = np.arange(ATT_BLK)[None, :]
    j = np.arange(2 * ATT_BLK)[:, None]
    delta = np.maximum(ATT_BLK + i - j, 0)
    out = np.zeros((len(ATT_DILATIONS), REL_BUCKETS, ATT_BLK * 2 * ATT_BLK), np.float32)
    for gi, d in enumerate(ATT_DILATIONS):
        b = _t5_bucket_np(delta * d).reshape(-1)
        out[gi, b, np.arange(b.size)] = 1.0
    return out


def _exact_mm(a, b, *, name, tb=False):
    M, K = a.shape
    N = b.shape[0] if tb else b.shape[1]
    tn = _tile(N, 4096, LANES)
    dn = (((1,), (1 if tb else 0,)), ((), ()))

    def body(a_ref, b_ref, o_ref):
        o_ref[...] = lax.dot_general(a_ref[...], b_ref[...], dn, preferred_element_type=f32,
                                     precision=lax.Precision.HIGHEST)

    return pl.pallas_call(
        body, name=name, grid=(N // tn,),
        in_specs=[pl.BlockSpec((M, K), lambda j: (0, 0)),
                  pl.BlockSpec((tn, K), lambda j: (j, 0)) if tb else pl.BlockSpec((K, tn), lambda j: (0, j))],
        out_specs=pl.BlockSpec((M, tn), lambda j: (0, j)), out_shape=_sds((M, N)),
        compiler_params=_params(("parallel",)),
    )(a, b)


def _band_penalty():
    i = np.arange(ATT_BLK)[None, :]
    j = np.arange(2 * ATT_BLK)[:, None]
    delta = ATT_BLK + i - j
    return np.where((delta >= 0) & (delta <= ATT_BLK), 0.0, -np.inf).astype(np.float32)


def _first_block_keep(n):
    key = lax.broadcasted_iota(jnp.int32, (2 * ATT_BLK, ATT_BLK), 0)
    return (key >= ATT_BLK) | (n > 0)


ATT_SCALE = HEAD_DIM ** -0.5


def _rows(ref, r, d):
    return ref[...] if d == 1 else ref[pl.ds(r, ATT_BLK, stride=d), :]


def _set_rows(ref, r, d, val):
    if d == 1:
        ref[...] = val
    else:
        ref[pl.ds(r, ATT_BLK, stride=d), :] = val


def _attn_width(d, D):
    return D if d == 1 else LANES


def _over_residues(d, one, unroll=1):
    if d == 1:
        one(0)
    else:
        lax.fori_loop(0, d, lambda r, c: (one(r), c)[1], 0, unroll=unroll)


def _attn_fwd(q, k, v, bias_t, d, name):
    S, D = q.shape
    nb = S // (d * ATT_BLK)
    H = D // HEAD_DIM
    W = _attn_width(d, D)
    HB = W // HEAD_DIM

    def body(q_ref, kp_ref, kc_ref, vp_ref, vc_ref, b_ref, o_ref, lse_ref):
        keep = _first_block_keep(pl.program_id(1))
        first = lax.broadcasted_iota(jnp.int32, (1, LANES), 1) < HEAD_DIM

        def one(r):
            qs = (_rows(q_ref, r, d) * ATT_SCALE).astype(bf16)
            kcat = jnp.concatenate([_rows(kp_ref, r, d), _rows(kc_ref, r, d)], axis=0).astype(bf16)
            vcat = jnp.concatenate([_rows(vp_ref, r, d), _rows(vc_ref, r, d)], axis=0).astype(bf16)
            outs = []
            for pair in range(W // LANES):
                ps = slice(pair * LANES, (pair + 1) * LANES)
                q2, k2, v2 = qs[:, ps], kcat[:, ps], vcat[:, ps]
                o2 = jnp.zeros((ATT_BLK, LANES), f32)
                for e in range(2):
                    h = 2 * pair + e
                    mine = first if e == 0 else jnp.logical_not(first)
                    zero = jnp.zeros((), bf16)
                    st = jnp.where(keep, _dot_nt(k2, jnp.where(mine, q2, zero)) + b_ref[h], -jnp.inf)
                    m = jnp.max(st, 0, keepdims=True)
                    pt = jnp.exp(st - m)
                    l = jnp.sum(pt, 0, keepdims=True)
                    o2 = o2 + _dot_tn(pt * (1.0 / l), jnp.where(mine, v2, zero))
                    lse_ref[r, h] = m + jnp.log(l)
                outs.append(o2)
            _set_rows(o_ref, r, d, jnp.concatenate(outs, axis=1))

        _over_residues(d, one, unroll=4)

    cur = pl.BlockSpec((ATT_BLK * d, W), lambda j, n: (n, j))
    prev = pl.BlockSpec((ATT_BLK * d, W), lambda j, n: (jnp.maximum(n - 1, 0), j))
    return pl.pallas_call(
        body, name=name, grid=(D // W, nb),
        in_specs=[cur, prev, cur, prev, cur, pl.BlockSpec((HB, 2 * ATT_BLK, ATT_BLK), lambda j, n: (j, 0, 0))],
        out_specs=[cur, pl.BlockSpec((None, d, HB, 1, LANES), lambda j, n: (n, 0, j, 0, 0))],
        out_shape=[_sds((S, D)), _sds((nb, d, H, 1, LANES))],
        compiler_params=_params(("parallel", "arbitrary")),
    )(q, k, k, v, v, bias_t)


def _from_blocks(rows, lanes=None):
    nb, d, H = rows.shape[:3]
    a = jnp.transpose(rows[:, :, :, 0, :], (0, 3, 1, 2)).reshape(nb * ATT_BLK * d, H)
    return a if lanes is None else jnp.pad(a, ((0, 0), (0, lanes - H)))


def _by_block(a, d):
    S, H = a.shape
    t = jnp.transpose(a.reshape(S // (d * ATT_BLK), ATT_BLK, d, H), (0, 2, 3, 1))
    return t[:, :, :, None, :]


def _head_sums(a, b, name):
    S, D = a.shape
    tr = _tile(S, 512, 8)
    hoc = jnp.asarray((np.arange(D)[:, None] // HEAD_DIM == np.arange(LANES)[None, :]).astype(np.float32))

    def body(a_ref, b_ref, h_ref, o_ref):
        o_ref[...] = _dot_exact(a_ref[...] * b_ref[...], h_ref[...])

    return pl.pallas_call(
        body, name=name, grid=(S // tr,),
        in_specs=[pl.BlockSpec((tr, D), lambda i: (i, 0)), pl.BlockSpec((tr, D), lambda i: (i, 0)),
                  pl.BlockSpec((D, LANES), lambda i: (0, 0))],
        out_specs=pl.BlockSpec((tr, LANES), lambda i: (i, 0)), out_shape=_sds((S, LANES)),
        compiler_params=_params(("parallel",)),
    )(a, b, hoc)


def _attn_bwd(q, k, v, bias_t, datt, lse_rows, dsum_rows, d, name):
    S, D = q.shape
    nb = S // (d * ATT_BLK)
    H = D // HEAD_DIM
    W = _attn_width(d, D)
    HB = W // HEAD_DIM

    def body(q_ref, kp_ref, kc_ref, vp_ref, vc_ref, b_ref, do_ref, lse_ref, dsum_ref,
             dq_ref, dk_ref, dv_ref, db_ref, carry_k, carry_v):
        j = pl.program_id(0)
        n = pl.program_id(1)

        @pl.when(n == 0)
        def _():
            carry_k[...] = jnp.zeros_like(carry_k)
            carry_v[...] = jnp.zeros_like(carry_v)
            db_ref[...] = jnp.zeros_like(db_ref)

        @pl.when(n < nb)
        def _():
            key = lax.broadcasted_iota(jnp.int32, (2 * ATT_BLK, ATT_BLK), 0)
            keep = (key >= ATT_BLK) | (n > 0)
            first = lax.broadcasted_iota(jnp.int32, (1, LANES), 1) < HEAD_DIM

            def one(r):
                qs = (_rows(q_ref, r, d) * ATT_SCALE).astype(bf16)
                kcat = jnp.concatenate([_rows(kp_ref, r, d), _rows(kc_ref, r, d)], axis=0).astype(bf16)
                vcat = jnp.concatenate([_rows(vp_ref, r, d), _rows(vc_ref, r, d)], axis=0).astype(bf16)
                dob = _rows(do_ref, r, d).astype(bf16)
                dqs, dks, dvs = [], [], []
                for pair in range(W // LANES):
                    ps = slice(pair * LANES, (pair + 1) * LANES)
                    q2, k2, v2, do2 = qs[:, ps], kcat[:, ps], vcat[:, ps], dob[:, ps]
                    dq2 = jnp.zeros((ATT_BLK, LANES), f32)
                    dk2 = jnp.zeros((2 * ATT_BLK, LANES), f32)
                    dv2 = jnp.zeros((2 * ATT_BLK, LANES), f32)
                    for e in range(2):
                        h = 2 * pair + e
                        mine = first if e == 0 else jnp.logical_not(first)
                        zero = jnp.zeros((), bf16)
                        qm, dom, km = jnp.where(mine, q2, zero), jnp.where(mine, do2, zero), jnp.where(mine, k2, zero)
                        st = jnp.where(keep, _dot_nt(k2, qm) + b_ref[h], -jnp.inf)
                        pt = jnp.exp(st - lse_ref[r, j * HB + h])
                        dst = pt * (_dot_nt(v2, dom) - dsum_ref[r, j * HB + h])
                        db_ref[h] += dst
                        dv2 = dv2 + _dot(pt, dom)
                        dk2 = dk2 + _dot(dst, qm)
                        dq2 = dq2 + _dot_tn(dst, km)
                    dqs.append(dq2 * ATT_SCALE)
                    dks.append(dk2)
                    dvs.append(dv2)
                _set_rows(dq_ref, r, d, jnp.concatenate(dqs, axis=1))
                dk = jnp.concatenate(dks, axis=1)
                dv = jnp.concatenate(dvs, axis=1)
                _set_rows(dk_ref, r, d, carry_k[r] + dk[:ATT_BLK])
                _set_rows(dv_ref, r, d, carry_v[r] + dv[:ATT_BLK])
                carry_k[r] = dk[ATT_BLK:]
                carry_v[r] = dv[ATT_BLK:]

            _over_residues(d, one, unroll=2)

        @pl.when(n == nb)
        def _():
            def last(r):
                _set_rows(dk_ref, r, d, carry_k[r])
                _set_rows(dv_ref, r, d, carry_v[r])

            _over_residues(d, last)

    nq = lambda n: jnp.minimum(n, nb - 1)
    cur = pl.BlockSpec((ATT_BLK * d, W), lambda j, n: (nq(n), j))
    prev = pl.BlockSpec((ATT_BLK * d, W), lambda j, n: (jnp.maximum(nq(n) - 1, 0), j))
    done = pl.BlockSpec((ATT_BLK * d, W), lambda j, n: (jnp.maximum(n - 1, 0), j))
    bspec = pl.BlockSpec((HB, 2 * ATT_BLK, ATT_BLK), lambda j, n: (j, 0, 0))
    rows = pl.BlockSpec((None, d, H, 1, LANES), lambda j, n: (nq(n), 0, 0, 0, 0))
    return pl.pallas_call(
        body, name=name, grid=(D // W, nb + 1),
        in_specs=[cur, prev, cur, prev, cur, bspec, cur, rows, rows],
        out_specs=[cur, done, done, bspec],
        out_shape=[_sds((S, D)), _sds((S, D)), _sds((S, D)), _sds((H, 2 * ATT_BLK, ATT_BLK))],
        scratch_shapes=[pltpu.VMEM((d, ATT_BLK, W), f32), pltpu.VMEM((d, ATT_BLK, W), f32)],
        compiler_params=_params(("arbitrary", "arbitrary")),
    )(q, k, k, v, v, bias_t, datt, lse_rows, dsum_rows)


def _attn_combine(os_, lses, name):
    S, D = os_[0].shape
    tr = _tile(S, 128, 16)
    head_cols = jnp.asarray((np.arange(LANES)[:, None] == np.arange(D)[None, :] // HEAD_DIM).astype(np.float32))

    def body(o0, o1, o2, l0, l1, l2, hc_ref, att_ref, attb_ref, lse_ref):
        a, b, c = l0[...], l1[...], l2[...]
        m = jnp.maximum(jnp.maximum(a, b), c)
        e0, e1, e2 = jnp.exp(a - m), jnp.exp(b - m), jnp.exp(c - m)
        tot = e0 + e1 + e2
        wide = lambda w: _dot_exact(w / tot, hc_ref[...])
        att = wide(e0) * o0[...] + wide(e1) * o1[...] + wide(e2) * o2[...]
        att_ref[...] = att
        attb_ref[...] = att.astype(bf16)
        lse_ref[...] = m + jnp.log(tot)

    wide_spec = pl.BlockSpec((tr, D), lambda i: (i, 0))
    lane_spec = pl.BlockSpec((tr, LANES), lambda i: (i, 0))
    return pl.pallas_call(
        body, name=name, grid=(S // tr,),
        in_specs=[wide_spec] * 3 + [lane_spec] * 3 + [pl.BlockSpec((LANES, D), lambda i: (0, 0))],
        out_specs=[wide_spec, wide_spec, lane_spec], out_shape=[_sds((S, D)), _sds((S, D), bf16), _sds((S, LANES))],
        compiler_params=_params(("parallel",)),
    )(*os_, *lses, head_cols)


ANY = pl.BlockSpec(memory_space=pl.ANY)


def _all_gather(vs, name):
    n = len(vs)

    def body(*refs):
        x_refs, out_refs = refs[:n], refs[n:2 * n]
        send_sems, recv_sems, local_sems = refs[2 * n:]
        x, y, c = lax.axis_index("x"), lax.axis_index("y"), lax.axis_index("c")
        me, sibling = (x, y, c), (x, y, 1 - c)
        chips = [(1 - x, y), (x, 1 - y), (1 - x, 1 - y)]

        def slot(i, px, py, pc):
            return out_refs[i].at[4 * px + 2 * py + pc]

        def copy(i, k, block, to, src=None):
            return pltpu.make_async_remote_copy(
                src_ref=slot(i, *block) if src is None else src, dst_ref=slot(i, *block),
                send_sem=send_sems.at[i, k], recv_sem=recv_sems.at[i, k], device_id=to, device_id_type=MESH)

        mine = [pltpu.make_async_copy(x_refs[i], slot(i, *me), local_sems.at[i]) for i in range(n)]
        for cp in mine:
            cp.start()
        first = []
        for i in range(n):
            first.append(copy(i, 0, me, sibling, src=x_refs[i]))
            first += [copy(i, 1 + j, me, (*chip, c), src=x_refs[i]) for j, chip in enumerate(chips)]
        for cp in first:
            cp.start()
        passed = []
        for i in range(n):
            for j, chip in enumerate(chips):
                copy(i, 1 + j, (*chip, c), me).wait_recv()
                cp = copy(i, 4 + j, (*chip, c), sibling)
                cp.start()
                passed.append(cp)
        for i in range(n):
            copy(i, 0, sibling, me).wait_recv()
            for j, chip in enumerate(chips):
                copy(i, 4 + j, (*chip, 1 - c), me).wait_recv()
        for cp in first + passed:
            cp.wait_send()
        for cp in mine:
            cp.wait()

    return pl.pallas_call(
        body, name=name, out_shape=[_sds((N_DEV,) + v.shape, v.dtype) for v in vs], in_specs=[ANY] * n,
        out_specs=[ANY] * n,
        scratch_shapes=[pltpu.SemaphoreType.DMA((n, 7)), pltpu.SemaphoreType.DMA((n, 7)), pltpu.SemaphoreType.DMA((n,))],
    )(*vs)


def _rs_sibling(parts, name):
    n = len(parts)

    def body(*refs):
        p_refs, out_refs = refs[:n], refs[n:2 * n]
        send_sems, recv_sems = refs[2 * n:]
        x, y, c = lax.axis_index("x"), lax.axis_index("y"), lax.axis_index("c")
        cps = [pltpu.make_async_remote_copy(
            src_ref=p_refs[i].at[k, 1 - c], dst_ref=out_refs[i].at[k], send_sem=send_sems.at[i, k],
            recv_sem=recv_sems.at[i, k], device_id=(x, y, 1 - c), device_id_type=MESH)
            for i in range(n) for k in range(4)]
        for cp in cps:
            cp.start()
        for cp in cps:
            cp.wait()

    return pl.pallas_call(
        body, name=name, out_shape=[_sds((4,) + p.shape[2:], p.dtype) for p in parts], in_specs=[ANY] * n,
        out_specs=[ANY] * n,
        scratch_shapes=[pltpu.SemaphoreType.DMA((n, 4)), pltpu.SemaphoreType.DMA((n, 4))],
    )(*parts)


def _rs_chips(ts, name):
    n = len(ts)

    def body(*refs):
        t_refs, out_refs = refs[:n], refs[n:2 * n]
        send_sems, recv_sems, local_sems = refs[2 * n:]
        x, y, c = lax.axis_index("x"), lax.axis_index("y"), lax.axis_index("c")
        mine = 2 * x + y
        local = [pltpu.make_async_copy(t_refs[i].at[mine], out_refs[i].at[mine], local_sems.at[i]) for i in range(n)]
        for cp in local:
            cp.start()
        chips = [(1 - x, y), (x, 1 - y), (1 - x, 1 - y)]
        cps = [pltpu.make_async_remote_copy(
            src_ref=t_refs[i].at[2 * px + py], dst_ref=out_refs[i].at[mine], send_sem=send_sems.at[i, j],
            recv_sem=recv_sems.at[i, j], device_id=(px, py, c), device_id_type=MESH)
            for i in range(n) for j, (px, py) in enumerate(chips)]
        for cp in cps:
            cp.start()
        for cp in cps:
            cp.wait()
        for cp in local:
            cp.wait()

    return pl.pallas_call(
        body, name=name, out_shape=[_sds(t.shape, t.dtype) for t in ts], in_specs=[ANY] * n, out_specs=[ANY] * n,
        scratch_shapes=[pltpu.SemaphoreType.DMA((n, 3)), pltpu.SemaphoreType.DMA((n, 3)), pltpu.SemaphoreType.DMA((n,))],
    )(*ts)


def _pair_add(part, recv, c_arr, name):
    _, _, R, C = part.shape
    tr = _tile(R, PACK_ROW_TILE, 16)

    def body(c_ref, p_ref, r_ref, o_ref):
        o_ref[...] = (p_ref[...] + r_ref[...]).astype(o_ref.dtype)

    return pl.pallas_call(
        body, name=name,
        grid_spec=pltpu.PrefetchScalarGridSpec(
            num_scalar_prefetch=1, grid=(4, R // tr),
            in_specs=[pl.BlockSpec((None, None, tr, C), lambda k, i, c_ref: (k, c_ref[0], i, 0)),
                      pl.BlockSpec((None, tr, C), lambda k, i, c_ref: (k, i, 0))],
            out_specs=pl.BlockSpec((None, tr, C), lambda k, i, c_ref: (k, i, 0))),
        out_shape=_sds((4, R, C), bf16),
        compiler_params=_params(("parallel", "parallel")),
    )(c_arr, part, recv)


def _sum_slots(t, name):
    n, R, C = t.shape
    tr = _tile(R, PACK_ROW_TILE, 16)

    def body(t_ref, o_ref):
        acc = t_ref[0].astype(f32)
        for k in range(1, n):
            acc = acc + t_ref[k].astype(f32)
        o_ref[...] = acc

    return pl.pallas_call(
        body, name=name, grid=(R // tr,),
        in_specs=[pl.BlockSpec((n, tr, C), lambda i: (0, i, 0))],
        out_specs=pl.BlockSpec((tr, C), lambda i: (i, 0)), out_shape=_sds((R, C)),
        compiler_params=_params(("parallel",)),
    )(t)


def _reduce_scatter(parts, c_arr, name):
    parts4 = [p.reshape((4, 2) + p.shape[1:]) for p in parts]
    recv = _rs_sibling(parts4, name + "_sibling")
    ts = [_pair_add(p, r, c_arr, f"{name}_pair_{i}") for i, (p, r) in enumerate(zip(parts4, recv))]
    got = _rs_chips(ts, name + "_chips")
    return [_sum_slots(g, f"{name}_sum_{i}") for i, g in enumerate(got)]


HBM_SPEC = pl.BlockSpec(memory_space=pltpu.HBM)
SEM_SPEC = pl.BlockSpec(memory_space=pltpu.SEMAPHORE)
EFFECT = pltpu.SideEffectType.DATAFLOW_SIDE_EFFECTING


def _mesh_pos(p):
    return (p // 4, (p // 2) % 2, p % 2)


def _exchange_copy(src_refs, land_refs, send_sems, recv_sems, whole, i, k, receiving):
    me = 4 * lax.axis_index("x") + 2 * lax.axis_index("y") + lax.axis_index("c")
    to = (me + k) % N_DEV
    frm = (me + N_DEV - k) % N_DEV
    src = src_refs[i] if whole else src_refs[i].at[to]
    s = i * (N_DEV - 1) + k - 1
    send = pltpu.make_async_remote_copy(src_ref=src, dst_ref=land_refs[i].at[me], send_sem=send_sems.at[s],
                                        recv_sem=recv_sems.at[s], device_id=_mesh_pos(to), device_id_type=MESH)
    if not receiving:
        return send
    return send, pltpu.make_async_remote_copy(src_ref=src, dst_ref=land_refs[i].at[frm], send_sem=send_sems.at[s],
                                              recv_sem=recv_sems.at[s], device_id=_mesh_pos(to), device_id_type=MESH)


def _exchange_start(srcs, whole, name, after=None):
    n = len(srcs)
    lands = [lax.empty((N_DEV,) + s.shape[-2:], s.dtype) for s in srcs]
    after = list(after or [])
    n_in = 2 * n + len(after)

    def body(*refs):
        src_refs, land_refs = refs[:n], refs[n:2 * n]
        send_sems, recv_sems, token = refs[n_in], refs[n_in + 1], refs[-1]
        for i in range(n):
            for k in range(1, N_DEV):
                _exchange_copy(src_refs, land_refs, send_sems, recv_sems, whole, i, k, False).start()
        token[...] = jnp.zeros_like(token)

    sems = pltpu.SemaphoreType.DMA((n * (N_DEV - 1),))
    outs = pl.pallas_call(
        body, name=name,
        out_shape=(sems, sems, *[pltpu.HBM(a.shape, a.dtype) for a in srcs + lands], _sds((8, LANES))),
        in_specs=[HBM_SPEC] * (2 * n) + [pl.BlockSpec(memory_space=pl.ANY)] * len(after),
        out_specs=(SEM_SPEC, SEM_SPEC, *[HBM_SPEC] * (2 * n), pl.BlockSpec(memory_space=pltpu.VMEM)),
        input_output_aliases={i: 2 + i for i in range(2 * n)},
        compiler_params=pltpu.CompilerParams(has_side_effects=EFFECT),
    )(*[pltpu.with_memory_space_constraint(a, pltpu.HBM) for a in srcs + lands], *after)
    return (outs[0], outs[1], list(outs[2:2 + n]), list(outs[2 + n:2 + 2 * n]), whole), outs[-1]


def _exchange_wait(handle, after, name):
    send_sems, recv_sems, srcs, lands, whole = handle
    n = len(srcs)

    def body(*refs):
        src_refs, land_refs = refs[:n], refs[n:2 * n]
        send_sems, recv_sems = refs[2 * n], refs[2 * n + 1]
        for i in range(n):
            for k in range(1, N_DEV):
                send, recv = _exchange_copy(src_refs, land_refs, send_sems, recv_sems, whole, i, k, True)
                send.wait_send()
                recv.wait_recv()

    outs = pl.pallas_call(
        body, name=name, out_shape=tuple(pltpu.HBM(a.shape, a.dtype) for a in srcs + lands),
        in_specs=[HBM_SPEC] * (2 * n) + [SEM_SPEC, SEM_SPEC, pl.BlockSpec(memory_space=pl.ANY)],
        out_specs=[HBM_SPEC] * (2 * n), input_output_aliases={i: i for i in range(2 * n)},
        compiler_params=pltpu.CompilerParams(has_side_effects=EFFECT),
    )(*srcs, *lands, send_sems, recv_sems, after)
    return list(outs[n:])


def _tie(v, token):
    return v + token[0:1, 0:1].astype(v.dtype).reshape((1,) * v.ndim)


def _with_own(land, own, me):
    return lax.dynamic_update_slice_in_dim(land, own[None].astype(land.dtype), me, 0)


class _Overlap:
    def __init__(self, shards, me, after):
        self.me = me
        self.names = list(shards)
        self.handle, self.token = _exchange_start([shards[nm] for nm in self.names], True, "weights_start", after)
        self.sent = {}

    def weights(self, after):
        lands = _exchange_wait(self.handle, after, "weights_wait")
        own = self.handle[2]
        return {nm: _full_from_blocks(nm, _with_own(land, o, self.me)) for nm, land, o in zip(self.names, lands, own)}

    def send(self, tag, grads, after=None):
        names = list(grads)
        handle, token = _exchange_start([_blocks_from_full(nm, grads[nm]) for nm in names], False, f"grads_start_{tag}",
                                        after)
        self.sent[tag] = (names, handle)
        return token

    def received(self, tag, after):
        names, handle = self.sent[tag]
        lands = _exchange_wait(handle, after, f"grads_wait_{tag}")
        own = [lax.dynamic_index_in_dim(b, self.me, 0, keepdims=False) for b in handle[2]]
        return {nm: _with_own(land, o, self.me) for nm, land, o in zip(names, lands, own)}


ADAM_ROWS = 32


def _adamw(w, g, m, v, name):
    R, C = w.shape
    cb = LANES if C % LANES == 0 else C

    def body(w_ref, g_ref, m_ref, v_ref, d_ref, m2_ref, v2_ref):
        def update(sl):
            gv = g_ref[sl, :]
            m2 = ADAM_B1 * m_ref[sl, :] + (1.0 - ADAM_B1) * gv
            v2 = ADAM_B2 * v_ref[sl, :] + (1.0 - ADAM_B2) * jnp.square(gv)
            m_hat = m2 / (1.0 - ADAM_B1 ** ADAM_STEP)
            v_hat = v2 / (1.0 - ADAM_B2 ** ADAM_STEP)
            d_ref[sl, :] = -ADAM_LR * (m_hat / (jnp.sqrt(v_hat) + ADAM_EPS) + ADAM_WD * w_ref[sl, :])
            m2_ref[sl, :] = m2
            v2_ref[sl, :] = v2

        main = R // ADAM_ROWS
        if main:
            lax.fori_loop(0, main, lambda i, c: (update(pl.ds(pl.multiple_of(i * ADAM_ROWS, ADAM_ROWS), ADAM_ROWS)), c)[1], 0)
        if R % ADAM_ROWS:
            update(pl.ds(main * ADAM_ROWS, R % ADAM_ROWS))

    spec = pl.BlockSpec((R, cb), lambda j: (0, j))
    return pl.pallas_call(
        body, name=name, grid=(C // cb,), in_specs=[spec] * 4, out_specs=[spec] * 3, out_shape=[_sds((R, C))] * 3,
        compiler_params=_params(("parallel",)),
    )(w, g, m, v)


BIG_PARAMS = ("hy_w_in", "hy_w_out", "cv_w_pw1", "cv_w_pw2", "ffn_w_gate", "ffn_w_up", "ffn_w_down")


def _shards_2d(w):
    t = lambda a: jnp.transpose(a)
    return dict(in_t=t(w["hy_w_in"][0]), out=w["hy_w_out"][0], pw1=w["cv_w_pw1"][0], pw2=w["cv_w_pw2"][0],
                gate_t0=t(w["ffn_w_gate"][0]), gate_t1=t(w["ffn_w_gate"][1]), up_t0=t(w["ffn_w_up"][0]),
                up_t1=t(w["ffn_w_up"][1]), down0=w["ffn_w_down"][0], down1=w["ffn_w_down"][1])


def _unshard_2d(s):
    t = lambda a: jnp.transpose(a)
    return dict(hy_w_in=t(s["in_t"])[None], hy_w_out=s["out"][None], cv_w_pw1=s["pw1"][None], cv_w_pw2=s["pw2"][None],
                ffn_w_gate=jnp.stack([t(s["gate_t0"]), t(s["gate_t1"])]),
                ffn_w_up=jnp.stack([t(s["up_t0"]), t(s["up_t1"])]), ffn_w_down=jnp.stack([s["down0"], s["down1"]]))


def _full_from_blocks(nm, g):
    if nm == "pw1":
        return jnp.transpose(g, (1, 0, 2)).reshape(g.shape[1], N_DEV * g.shape[2])
    return g.reshape(N_DEV * g.shape[1], g.shape[2])


def _blocks_from_full(nm, g):
    if nm == "pw1":
        return jnp.transpose(g.reshape(g.shape[0], N_DEV, g.shape[1] // N_DEV), (1, 0, 2))
    return g.reshape(N_DEV, g.shape[0] // N_DEV, g.shape[1])


class _VecPack:
    def __init__(self, shapes):
        self.shapes = [tuple(s) for s in shapes]
        self.sizes = [int(np.prod(s)) for s in self.shapes]
        total = sum(self.sizes)
        self.rows = -(-(-(-total // LANES)) // 8) * 8
        self.total = total

    def pack(self, arrays):
        flat = jnp.concatenate([a.astype(f32).reshape(-1) for a in arrays])
        flat = jnp.pad(flat, (0, self.rows * LANES - self.total))
        return flat.reshape(self.rows, LANES)

    def unpack(self, packed):
        flat = packed.reshape(-1)
        out, off = [], 0
        for shp, n in zip(self.shapes, self.sizes):
            out.append(flat[off:off + n].reshape(shp))
            off += n
        return out

    def unpack_stacked(self, stacked, only=None):
        flat = stacked.reshape(stacked.shape[0], -1)
        offs = np.concatenate([[0], np.cumsum(self.sizes)])
        get = lambda i: flat[:, offs[i]:offs[i + 1]].reshape((stacked.shape[0],) + self.shapes[i])
        return get(only) if only is not None else [get(i) for i in range(len(self.shapes))]


def _row(v):
    return v.reshape(1, -1)


def _pad_lanes(v):
    v = v.reshape(1, -1)
    return jnp.pad(v, ((0, 0), (0, LANES - v.shape[1])))


def _ffn_fwd(h, w_gate_t, w_up_t, w_down, tag):
    F = w_down.shape[0]
    a = _mm(h, w_gate_t, tb=True, name=f"ffn_gate_{tag}")
    u = _mm(h, w_up_t, tb=True, name=f"ffn_up_{tag}")
    (f,), _ = _rowwise(f"swiglu_{tag}", lambda rv, vv: ([_silu(rv[0]) * rv[1]], []), [a, u], [], [(F, bf16)], [], sub=16,
                       col_chunk=_tile(F, 512, LANES))
    out = _mm(f, w_down, name=f"ffn_down_{tag}")
    return out, (a, u, f)


def _ffn_bwd(h, w_gate_t, w_up_t, w_down, saved, dout, tag):
    a, u, f = saved
    F = w_down.shape[0]
    df = _mm(dout, w_down, tb=True, name=f"ffn_down_dx_{tag}")
    dw_down = _mm(f, dout, ta=True, out_dtype=bf16, name=f"ffn_down_dw_{tag}")

    def fn(rv, vv):
        _, vjp = jax.vjp(lambda a_, u_: _silu(a_) * u_, rv[0], rv[1])
        da, du = vjp(rv[2])
        return [da, du], []

    (da, du), _ = _rowwise(f"swiglu_bwd_{tag}", fn, [a, u, df], [], [(F, bf16), (F, bf16)], [], sub=16,
                           col_chunk=_tile(F, 512, LANES))
    dh = _mm(du, w_up_t, add=_mm(da, w_gate_t, name=f"ffn_gate_dx_{tag}"), name=f"ffn_up_dx_{tag}")
    dw_gate_t = _mm(da, h, ta=True, out_dtype=bf16, name=f"ffn_gate_dw_{tag}")
    dw_up_t = _mm(du, h, ta=True, out_dtype=bf16, name=f"ffn_up_dw_{tag}")
    return dh, dw_gate_t, dw_up_t, dw_down


def _local_step(x, target, mod, w_in_t, comm, small):
    S, D = x.shape
    di = small["hy_ssm_norm_g"].shape[-1]
    nh = small["hy_dt_bias"].shape[-1]
    cd = small["hy_conv_b"].shape[-1]
    m = [[_row(mod[i, j]) for j in range(6)] for i in range(2)]

    off_q = di + cd + nh
    w_qkv_t = w_in_t[off_q:]
    seg = dict(z=(w_in_t, 0, di), xbc=(w_in_t, di, cd), dt=(w_in_t, di + cd, LANES))
    for i, nm in enumerate(("q0", "q1", "q2", "k", "v")):
        seg[nm] = (w_qkv_t, i * D, D)

    g_mix = [_row(small["norm_mix_g"][i]) for i in range(2)]
    g_ffn = [_row(small["norm_ffn_g"][i]) for i in range(2)]
    conv_w, conv_b = small["hy_conv_w_full"], _row(small["hy_conv_b"][0])
    dt_bias, a_log, d_skip = (_pad_lanes(small[k][0]) for k in ("hy_dt_bias", "hy_a_log", "hy_d_skip"))
    g_ssm = _row(small["hy_ssm_norm_g"][0])
    onehot = jnp.asarray(_bucket_onehot())
    rel_t = small["rel_table"].T
    H = D // HEAD_DIM
    bias = [_exact_mm(rel_t[gi * H:(gi + 1) * H], onehot[gi], name=f"rel_bias_{gi}")
            .reshape(H, 2 * ATT_BLK, ATT_BLK) + _band_penalty() for gi in range(3)]

    h1 = _adaln_fwd(x, g_mix[0], m[0][1], m[0][0], "adaln_mix0")
    proj = {nm: _mm(h1, mat, tb=True, b_rows=(off, cnt), name=f"in_{nm}") for nm, (mat, off, cnt) in seg.items()}
    xbc_pre, xbc = _conv_fwd(proj["xbc"], conv_w, conv_b, silu=True, name="ssm_conv", tr=1024)
    y, hin = _ssd_fwd(xbc, proj["dt"], dt_bias, a_log, d_skip, di, "ssd_fwd")
    (yg,), _ = _rowwise("ssm_gate", lambda rv, vv: ([_gate_f(rv[0], rv[1], vv[0])], []),
                        [y, proj["z"]], [g_ssm], [(di, bf16)], [], sub=16)
    og = [_attn_fwd(proj[f"q{gi}"], proj["k"], proj["v"], bias[gi], d, f"attn_fwd_{gi}")
          for gi, d in enumerate(ATT_DILATIONS)]
    att, att_b, lse_tot = _attn_combine([a for a, _ in og], [_from_blocks(b, LANES) for _, b in og], "attn_combine")
    W = comm.weights(after=att_b)
    w_out_y, w_out_a = W["out"][:di], W["out"][di:]
    mix0 = _mm(att_b, w_out_a, add=_mm(yg, w_out_y, name="out_y"), name="out_a")
    x1 = _resid_fwd(x, m[0][2], mix0, "resid_mix0")
    h2 = _adaln_fwd(x1, g_ffn[0], m[0][4], m[0][3], "adaln_ffn0")
    f0, ffn0_saved = _ffn_fwd(h2, W["gate_t0"], W["up_t0"], W["down0"], "0")
    x2 = _resid_fwd(x1, m[0][5], f0, "resid_ffn0")

    h3 = _adaln_fwd(x2, g_mix[1], m[1][1], m[1][0], "adaln_mix1")
    pw1 = _mm(h3, W["pw1"], bias=_row(small["cv_b_pw1_full"]), name="cv_pw1")
    (u,), _ = _rowwise("cv_glu", lambda rv, vv: ([rv[0] * jax.nn.sigmoid(rv[1])], []),
                       [(pw1, 0, D), (pw1, 1, D)], [], [(D, f32)], [])
    (u2,) = _conv_fwd(u, small["cv_w_dw_full"], _row(small["cv_b_dw_full"]), silu=False, name="cv_dw")
    ln_g, ln_b = _row(small["cv_ln_g_full"]), _row(small["cv_ln_b_full"])
    (u3,), _ = _rowwise("cv_lnsilu", lambda rv, vv: ([_lnsilu_f(rv[0], vv[0], vv[1])], []),
                        [u2], [ln_g, ln_b], [(D, bf16)], [], sub=16)
    mix1 = _mm(u3, W["pw2"], bias=_row(small["cv_b_pw2_full"]), name="cv_pw2")
    x3 = _resid_fwd(x2, m[1][2], mix1, "resid_mix1")
    h4 = _adaln_fwd(x3, g_ffn[1], m[1][4], m[1][3], "adaln_ffn1")
    f1, ffn1_saved = _ffn_fwd(h4, W["gate_t1"], W["up_t1"], W["down1"], "1")
    x4 = _resid_fwd(x3, m[1][5], f1, "resid_ffn1")

    g_fin = _row(small["final_norm_g"])

    def final_fn(rv, vv):
        xv, tv = rv
        yv, vjp = jax.vjp(_rms, xv, vv[0])
        err = yv - tv
        dx, dg = vjp(err / D)
        part = 0.5 * jnp.sum(jnp.mean(err * err, -1, keepdims=True), 0, keepdims=True)
        return [dx], [dg, jnp.broadcast_to(part, (1, LANES))]

    (dx4,), (d_fin, loss) = _rowwise("loss_head", final_fn, [x4, target], [g_fin], [(D, f32)], [D, LANES])

    dmod = [[None] * 6 for _ in range(2)]
    d_norm_mix, d_norm_ffn = [None, None], [None, None]
    big = {}

    df1, (dmod[1][5], _) = _resid_bwd(dx4, f1, m[1][5], "resid_ffn1_bwd")
    dh4, big["gate_t1"], big["up_t1"], big["down1"] = _ffn_bwd(h4, W["gate_t1"], W["up_t1"], W["down1"], ffn1_saved, df1, "1")
    dx3, (d_norm_ffn[1], dmod[1][4], dmod[1][3]) = _adaln_bwd(x3, g_ffn[1], m[1][4], m[1][3], dh4, dx4, "adaln_ffn1_bwd")
    dmix1, (dmod[1][2], d_b_pw2) = _resid_bwd(dx3, mix1, m[1][2], "resid_mix1_bwd")
    du3 = _mm(dmix1, W["pw2"], tb=True, name="cv_pw2_dx")
    big["pw2"] = _mm(u3, dmix1, ta=True, out_dtype=bf16, name="cv_pw2_dw")

    def lnsilu_bwd(rv, vv):
        _, vjp = jax.vjp(_lnsilu_f, rv[0], vv[0], vv[1])
        du, dg, db = vjp(rv[1])
        return [du], [dg, db]

    (du2,), (d_ln_g, d_ln_b) = _rowwise("cv_lnsilu_bwd", lnsilu_bwd, [u2, du3], [ln_g, ln_b], [(D, f32)], [D, D])
    du, d_w_dw, d_b_dw = _conv_bwd(u, small["cv_w_dw_full"], du2, None, silu=False, name="cv_dw_bwd")

    def glu_bwd(rv, vv):
        a, gt, d = rv
        _, vjp = jax.vjp(lambda a_, g_: a_ * jax.nn.sigmoid(g_), a, gt)
        da, dg = vjp(d)
        return [da, dg], [jnp.sum(da, 0, keepdims=True), jnp.sum(dg, 0, keepdims=True)]

    (dpa, dpg), (d_b1a, d_b1g) = _rowwise("cv_glu_bwd", glu_bwd, [(pw1, 0, D), (pw1, 1, D), du], [],
                                           [(D, bf16), (D, bf16)], [D, D], sub=16)
    dpw1 = jnp.concatenate([dpa, dpg], axis=1)
    d_b_pw1 = jnp.concatenate([d_b1a, d_b1g], axis=1)
    dh3 = _mm(dpw1, W["pw1"], tb=True, name="cv_pw1_dx")
    big["pw1"] = _mm(h3, dpw1, ta=True, out_dtype=bf16, name="cv_pw1_dw")
    token = comm.send("layer1", {nm: big[nm] for nm in ("gate_t1", "up_t1", "down1", "pw2", "pw1")})
    dx2, (d_norm_mix[1], dmod[1][1], dmod[1][0]) = _adaln_bwd(x2, g_mix[1], m[1][1], _tie(m[1][0], token), dh3, dx3,
                                                              "adaln_mix1_bwd")

    df0, (dmod[0][5], _) = _resid_bwd(dx2, f0, m[0][5], "resid_ffn0_bwd")
    dh2, big["gate_t0"], big["up_t0"], big["down0"] = _ffn_bwd(h2, W["gate_t0"], W["up_t0"], W["down0"], ffn0_saved, df0, "0")
    dx1, (d_norm_ffn[0], dmod[0][4], dmod[0][3]) = _adaln_bwd(x1, g_ffn[0], m[0][4], m[0][3], dh2, dx2, "adaln_ffn0_bwd")
    dmix0, (dmod[0][2], _) = _resid_bwd(dx1, mix0, m[0][2], "resid_mix0_bwd")
    dyg = _mm(dmix0, w_out_y, tb=True, name="out_y_dx")
    datt = _mm(dmix0, w_out_a, tb=True, name="out_a_dx")
    big["out"] = jnp.concatenate([_mm(yg, dmix0, ta=True, out_dtype=bf16, name="out_y_dw"),
                                  _mm(att_b, dmix0, ta=True, out_dtype=bf16, name="out_a_dw")], axis=0)
    token = comm.send("layer0", {nm: big[nm] for nm in ("gate_t0", "up_t0", "down0", "out")})
    bias = [_tie(b, token) for b in bias]
    g_ssm = _tie(g_ssm, token)

    dq, dks, dvs, dbs = [], [], [], []
    lse_heads = lse_tot[:, :H]
    dsum_heads = _head_sums(att, datt, "attn_dsum")[:, :H]
    for gi, d in enumerate(ATT_DILATIONS):
        a, b, c_, e = _attn_bwd(proj[f"q{gi}"], proj["k"], proj["v"], bias[gi], datt,
                                _by_block(lse_heads, d), _by_block(dsum_heads, d), d, f"attn_bwd_{gi}")
        dq.append(a)
        dks.append(b)
        dvs.append(c_)
        dbs.append(e)
    dk = _add3(*dks, "attn_dk")
    dv = _add3(*dvs, "attn_dv")
    d_rel = jnp.concatenate(
        [_exact_mm(dbs[gi].reshape(H, -1), onehot[gi], tb=True, name=f"rel_grad_{gi}") for gi in range(3)], axis=0).T

    def gate_bwd(rv, vv):
        _, vjp = jax.vjp(_gate_f, rv[0], rv[1], vv[0])
        dy_, dz_, dg_ = vjp(rv[2])
        return [dy_, dz_], [dg_]

    (dy, dz), (d_g_ssm,) = _rowwise("ssm_gate_bwd", gate_bwd, [y, proj["z"], dyg], [g_ssm], [(di, f32), (di, bf16)], [di],
                                    sub=16)
    dxbc, ddtraw, d_a_log, d_dskip, d_dt_bias = _ssd_bwd(xbc, proj["dt"], dt_bias, a_log, d_skip, hin, y, dy, di, "ssd_bwd")
    dxbc_pre, d_conv_w, d_conv_b = _conv_bwd(proj["xbc"], conv_w, dxbc, xbc_pre, silu=True, name="ssm_conv_bwd",
                                             dx_dtype=bf16, tr=1024)

    dseg = {"z": dz, "xbc": dxbc_pre, "dt": ddtraw, "q0": dq[0], "q1": dq[1], "q2": dq[2], "k": dk, "v": dv}
    dh1 = None
    d_in_parts = []
    for nm, (mat, off, cnt) in seg.items():
        dh1 = _mm(dseg[nm], mat, b_rows=(off, cnt), add=dh1, name=f"in_{nm}_dx")
        dwp = _mm(dseg[nm], h1, ta=True, out_dtype=bf16, name=f"in_{nm}_dw")
        d_in_parts.append(dwp[:nh] if nm == "dt" else dwp)
    big["in_t"] = jnp.concatenate(d_in_parts, axis=0)
    dx0, (d_norm_mix[0], dmod[0][1], dmod[0][0]) = _adaln_bwd(x, g_mix[0], m[0][1], m[0][0], dh1, dx1, "adaln_mix0_bwd")

    smallg = dict(
        loss=loss, dmod=jnp.stack([jnp.concatenate(dmod[i], axis=1)[0] for i in range(2)]),
        norm_mix_g=jnp.concatenate(d_norm_mix, axis=0), norm_ffn_g=jnp.concatenate(d_norm_ffn, axis=0),
        hy_conv_w=d_conv_w, hy_conv_b=d_conv_b, hy_dt_bias=d_dt_bias[:, :nh], hy_a_log=d_a_log[:, :nh],
        hy_d_skip=d_dskip[:, :nh], hy_ssm_norm_g=d_g_ssm, rel_table=d_rel,
        cv_b_pw1=d_b_pw1, cv_w_dw=d_w_dw, cv_b_dw=d_b_dw, cv_ln_g=d_ln_g, cv_ln_b=d_ln_b, cv_b_pw2=d_b_pw2,
        final_norm_g=d_fin)
    return dx0, big["in_t"], smallg


SMALL_GRAD_ORDER = ("loss", "dmod", "norm_mix_g", "norm_ffn_g", "hy_conv_w", "hy_conv_b", "hy_dt_bias", "hy_a_log",
                    "hy_d_skip", "hy_ssm_norm_g", "rel_table", "cv_b_pw1", "cv_w_dw", "cv_b_dw", "cv_ln_g", "cv_ln_b",
                    "cv_b_pw2", "final_norm_g")


def kernel(x, c, ada_w, ada_b, norm_mix_g, norm_ffn_g, hy_w_in, hy_conv_w, hy_conv_b, hy_dt_bias, hy_a_log, hy_d_skip, hy_ssm_norm_g, hy_w_out, rel_table, cv_w_pw1, cv_b_pw1, cv_w_dw, cv_b_dw, cv_ln_g, cv_ln_b, cv_w_pw2, cv_b_pw2, ffn_w_gate, ffn_w_up, ffn_w_down, final_norm_g, loss_target, m_ada_w, m_ada_b, m_norm_mix_g, m_norm_ffn_g, m_hy_w_in, m_hy_conv_w, m_hy_conv_b, m_hy_dt_bias, m_hy_a_log, m_hy_d_skip, m_hy_ssm_norm_g, m_hy_w_out, m_rel_table, m_cv_w_pw1, m_cv_b_pw1, m_cv_w_dw, m_cv_b_dw, m_cv_ln_g, m_cv_ln_b, m_cv_w_pw2, m_cv_b_pw2, m_ffn_w_gate, m_ffn_w_up, m_ffn_w_down, m_final_norm_g, v_ada_w, v_ada_b, v_norm_mix_g, v_norm_ffn_g, v_hy_w_in, v_hy_conv_w, v_hy_conv_b, v_hy_dt_bias, v_hy_a_log, v_hy_d_skip, v_hy_ssm_norm_g, v_hy_w_out, v_rel_table, v_cv_w_pw1, v_cv_b_pw1, v_cv_w_dw, v_cv_b_dw, v_cv_ln_g, v_cv_ln_b, v_cv_w_pw2, v_cv_b_pw2, v_ffn_w_gate, v_ffn_w_up, v_ffn_w_down, v_final_norm_g):
    names = ("ada_w", "ada_b", "norm_mix_g", "norm_ffn_g", "hy_w_in", "hy_conv_w", "hy_conv_b", "hy_dt_bias", "hy_a_log",
             "hy_d_skip", "hy_ssm_norm_g", "hy_w_out", "rel_table", "cv_w_pw1", "cv_b_pw1", "cv_w_dw", "cv_b_dw", "cv_ln_g",
             "cv_ln_b", "cv_w_pw2", "cv_b_pw2", "ffn_w_gate", "ffn_w_up", "ffn_w_down", "final_norm_g")
    w = dict(zip(names, (ada_w, ada_b, norm_mix_g, norm_ffn_g, hy_w_in, hy_conv_w, hy_conv_b, hy_dt_bias, hy_a_log, hy_d_skip,
                         hy_ssm_norm_g, hy_w_out, rel_table, cv_w_pw1, cv_b_pw1, cv_w_dw, cv_b_dw, cv_ln_g, cv_ln_b, cv_w_pw2,
                         cv_b_pw2, ffn_w_gate, ffn_w_up, ffn_w_down, final_norm_g)))
    mom = dict(zip(names, (m_ada_w, m_ada_b, m_norm_mix_g, m_norm_ffn_g, m_hy_w_in, m_hy_conv_w, m_hy_conv_b, m_hy_dt_bias,
                           m_hy_a_log, m_hy_d_skip, m_hy_ssm_norm_g, m_hy_w_out, m_rel_table, m_cv_w_pw1, m_cv_b_pw1, m_cv_w_dw,
                           m_cv_b_dw, m_cv_ln_g, m_cv_ln_b, m_cv_w_pw2, m_cv_b_pw2, m_ffn_w_gate, m_ffn_w_up, m_ffn_w_down,
                           m_final_norm_g)))
    vel = dict(zip(names, (v_ada_w, v_ada_b, v_norm_mix_g, v_norm_ffn_g, v_hy_w_in, v_hy_conv_w, v_hy_conv_b, v_hy_dt_bias,
                           v_hy_a_log, v_hy_d_skip, v_hy_ssm_norm_g, v_hy_w_out, v_rel_table, v_cv_w_pw1, v_cv_b_pw1, v_cv_w_dw,
                           v_cv_b_dw, v_cv_ln_g, v_cv_ln_b, v_cv_w_pw2, v_cv_b_pw2, v_ffn_w_gate, v_ffn_w_up, v_ffn_w_down,
                           v_final_norm_g)))
    S, D = x.shape[1], x.shape[2]
    ax, ay, ac = lax.axis_index("x"), lax.axis_index("y"), lax.axis_index("c")
    me = 4 * ax + 2 * ay + ac
    c_arr = jnp.reshape(ac, (1,)).astype(jnp.int32)
    nmod = ada_w.shape[2]

    w2 = _shards_2d(w)
    big_names = list(w2)
    (g_in,) = _all_gather([w2["in_t"].astype(bf16)], "gather_w_in")
    w_in_t = _full_from_blocks("in_t", g_in)

    sharded_small = ("hy_conv_w", "cv_b_pw1", "cv_w_dw", "cv_b_dw", "cv_ln_g", "cv_ln_b", "cv_b_pw2")
    vp = _VecPack([c.shape] + [w[nm].shape for nm in sharded_small])
    (sg,) = _all_gather([vp.pack([c] + [w[nm] for nm in sharded_small])], "gather_vectors")
    parts = vp.unpack_stacked(sg)
    c_all = parts[0][:, 0]
    small = {k: w[k] for k in ("norm_mix_g", "norm_ffn_g", "hy_conv_b", "hy_dt_bias", "hy_a_log", "hy_d_skip",
                               "hy_ssm_norm_g", "rel_table", "final_norm_g")}
    for p, nm in zip(parts[1:], sharded_small):
        p = p[:, 0]
        p = jnp.moveaxis(p, 0, -2)
        small[nm + "_full"] = p.reshape(p.shape[:-2] + (N_DEV * p.shape[-1],))

    (cs_all,), _ = _rowwise("ada_silu", lambda rv, vv: ([_silu(rv[0])], []), [c_all], [], [(D, f32)], [])
    b_mine = lax.dynamic_slice_in_dim(ada_b, me * nmod, nmod, axis=1)
    mod_part = jnp.stack([_mm(cs_all, ada_w[i], bias=b_mine[i:i + 1], name=f"ada_mod_{i}") for i in range(2)])
    (mod_all,) = _all_gather([mod_part.reshape(2 * N_DEV, nmod)], "gather_mod")
    mod_all = mod_all.reshape(N_DEV, 2, N_DEV, nmod)
    mod_mine = lax.dynamic_index_in_dim(mod_all, me, axis=2, keepdims=False)
    mod = jnp.transpose(mod_mine, (1, 0, 2)).reshape(2, 6, D)
    comm = _Overlap({nm: w2[nm].astype(bf16) for nm in big_names if nm != "in_t"}, me, after=[mod, w_in_t])
    mod = _tie(mod, comm.token)

    dx0, d_in_t, sgrad = _local_step(x[0], loss_target[0], mod, w_in_t, comm, small)
    comm.send("in", {"in_t": d_in_t})

    gp = _VecPack([sgrad[k].shape for k in SMALL_GRAD_ORDER])
    (g_all,) = _all_gather([gp.pack([sgrad[k] for k in SMALL_GRAD_ORDER])], "gather_small_grads")
    tot = dict(zip(SMALL_GRAD_ORDER, gp.unpack(_sum_slots(g_all, "sum_small_grads"))))
    dmod_all = gp.unpack_stacked(g_all, only=SMALL_GRAD_ORDER.index("dmod"))
    loss = tot["loss"][0, 0]

    grads = {}
    dmod_mine = lax.dynamic_slice_in_dim(dmod_all, me * nmod, nmod, axis=2)
    grads["ada_w"] = jnp.stack([_mm(cs_all, dmod_mine[:, i], ta=True, name=f"ada_w_grad_{i}") for i in range(2)])
    grads["ada_b"] = tot["dmod"]
    grads["norm_mix_g"], grads["norm_ffn_g"] = tot["norm_mix_g"], tot["norm_ffn_g"]
    grads["hy_conv_b"] = tot["hy_conv_b"]
    grads["hy_dt_bias"] = tot["hy_dt_bias"]
    grads["hy_a_log"] = tot["hy_a_log"]
    grads["hy_d_skip"] = tot["hy_d_skip"]
    grads["hy_ssm_norm_g"] = tot["hy_ssm_norm_g"]
    grads["rel_table"] = tot["rel_table"]
    grads["final_norm_g"] = tot["final_norm_g"][0]
    for nm in sharded_small:
        n = w[nm].shape[-1]
        grads[nm] = lax.dynamic_slice_in_dim(tot[nm], me * n, n, axis=1).reshape(w[nm].shape)

    delta, new_m, new_v = {}, {}, {}
    shp = ada_w.shape
    two = lambda t: t.reshape(-1, shp[-1])
    d_, m_, v_ = _adamw(two(ada_w), two(grads["ada_w"]), two(m_ada_w), two(v_ada_w), "adamw_ada_w")
    delta["ada_w"], new_m["ada_w"], new_v["ada_w"] = d_.reshape(shp), m_.reshape(shp), v_.reshape(shp)
    rest = [nm for nm in names if nm not in BIG_PARAMS and nm != "ada_w"]
    sp = _VecPack([w[nm].shape for nm in rest])
    packs = [sp.pack([t[nm] for nm in rest]) for t in (w, grads, mom, vel)]
    ds_, ms_, vs_ = _adamw(*packs, "adamw_small")
    for nm, a, b, e in zip(rest, sp.unpack(ds_), sp.unpack(ms_), sp.unpack(vs_)):
        delta[nm], new_m[nm], new_v[nm] = a, b, e

    g2 = {}
    after = d_
    for tag in ("layer1", "layer0", "in"):
        for nm, slots in comm.received(tag, after).items():
            g2[nm] = _sum_slots(slots, f"sum_{nm}")
            after = g2[nm]
    grads.update(_unshard_2d(g2))
    m2, v2 = _shards_2d(mom), _shards_2d(vel)
    d2, nm2, nv2 = {}, {}, {}
    for nm in big_names:
        d2[nm], nm2[nm], nv2[nm] = _adamw(w2[nm], g2[nm], m2[nm], v2[nm], f"adamw_{nm}")
    delta.update(_unshard_2d(d2))
    new_m.update(_unshard_2d(nm2))
    new_v.update(_unshard_2d(nv2))

    return (loss, dx0[None], *[grads[n] for n in names], *[delta[n] for n in names],
            *[new_m[n] for n in names], *[new_v[n] for n in names])
```

```python
import functools
import math

import numpy as np
import jax
import jax.numpy as jnp
from jax import lax
from jax.experimental import pallas as pl
from jax.experimental.pallas import tpu as pltpu

f32 = jnp.float32
bf16 = jnp.bfloat16
EPS = 1e-6
N_DEV = 8
LANES = 128
SSM_STATE = 128
SSM_CHUNK = 128
SSM_GROUPS = 4
HEAD_DIM = 64
ATT_BLK = 128
ATT_DILATIONS = (1, 4, 16)
REL_BUCKETS = 32
REL_MAX_DIST = 2048
ADAM_LR, ADAM_B1, ADAM_B2, ADAM_EPS, ADAM_WD, ADAM_STEP = 0.001, 0.9, 0.999, 1e-08, 0.01, 10
PACK_COLS = 1024
PACK_ROW_TILE = 256
MESH = pl.DeviceIdType.MESH
VMEM_LIMIT = 48 * 1024 * 1024


def _sds(shape, dtype=f32):
    return jax.ShapeDtypeStruct(tuple(shape), dtype)


def _tile(n, cap, mult):
    best = None
    t = mult
    while t <= min(n, cap):
        if n % t == 0:
            best = t
        t += mult
    return best if best is not None else n


def _params(sem):
    return pltpu.CompilerParams(dimension_semantics=sem, vmem_limit_bytes=VMEM_LIMIT)


def _mm(a, b, *, name, ta=False, tb=False, b_rows=None, bias=None, add=None, out_dtype=f32,
        tm_cap=512, tn_cap=1536, tk_cap=8192):
    if ta:
        K, M = a.shape
    else:
        M, K = a.shape
    off, cnt = b_rows if b_rows is not None else (0, b.shape[0])
    if tb:
        N, K2 = cnt, b.shape[1]
    else:
        K2, N = cnt, b.shape[1]
    assert K == K2, (a.shape, b.shape, ta, tb, b_rows)
    if ta and a.dtype == f32:
        tm_cap = min(tm_cap, 256)
    tm = _tile(M, tm_cap, LANES)
    tn = _tile(math.gcd(off, N) if tb else N, tn_cap, LANES)
    tk = _tile(K if tb else math.gcd(off, K), tk_cap, LANES)
    assert N % tn == 0 and K % tk == 0 and off % (tn if tb else tk) == 0, (name, off, N, K, tn, tk)
    nk = K // tk
    jo, ko = (off // tn, 0) if tb else (0, off // tk)
    has_bias, has_add = bias is not None, add is not None
    dn = (((0 if ta else 1,), (1 if tb else 0,)), ((), ()))

    def body(*refs):
        a_ref, b_ref = refs[0], refs[1]
        pos = 2
        bias_ref = add_ref = None
        if has_bias:
            bias_ref = refs[pos]
            pos += 1
        if has_add:
            add_ref = refs[pos]
            pos += 1
        o_ref = refs[pos]
        k = pl.program_id(2)
        part = lax.dot_general(a_ref[...].astype(bf16), b_ref[...].astype(bf16), dn, preferred_element_type=f32)

        def finish(r):
            if has_bias:
                r = r + bias_ref[...]
            if has_add:
                r = r + add_ref[...]
            o_ref[...] = r.astype(o_ref.dtype)

        if nk == 1:
            finish(part)
        else:
            acc_ref = refs[pos + 1]

            @pl.when(k == 0)
            def _():
                acc_ref[...] = part

            @pl.when((k > 0) & (k < nk - 1))
            def _():
                acc_ref[...] += part

            @pl.when(k == nk - 1)
            def _():
                finish(acc_ref[...] + part)

    in_specs = [
        pl.BlockSpec((tk, tm), lambda i, j, k: (k, i)) if ta else pl.BlockSpec((tm, tk), lambda i, j, k: (i, k)),
        pl.BlockSpec((tn, tk), lambda i, j, k: (j + jo, k)) if tb else pl.BlockSpec((tk, tn), lambda i, j, k: (k + ko, j)),
    ]
    args = [a, b]
    if has_bias:
        in_specs.append(pl.BlockSpec((1, tn), lambda i, j, k: (0, j)))
        args.append(bias)
    if has_add:
        in_specs.append(pl.BlockSpec((tm, tn), lambda i, j, k: (i, j)))
        args.append(add)
    return pl.pallas_call(
        body, name=name, grid=(M // tm, N // tn, nk), in_specs=in_specs,
        out_specs=pl.BlockSpec((tm, tn), lambda i, j, k: (i, j)), out_shape=_sds((M, N), out_dtype),
        scratch_shapes=[pltpu.VMEM((tm, tn), f32)] if nk > 1 else [],
        compiler_params=_params(("parallel", "parallel", "arbitrary")),
    )(*args)


def _rowwise(name, fn, rows, vecs, out_rows, out_accs, *, tr_cap=256, sub=8, col_chunk=None):
    rows = [r if isinstance(r, tuple) else (r, 0, r.shape[1]) for r in rows]
    R = rows[0][0].shape[0]
    tr = _tile(R, tr_cap, 8)
    sub = sub if tr % sub == 0 else tr
    n_r, n_v, n_or, n_oa = len(rows), len(vecs), len(out_rows), len(out_accs)

    def body(*refs):
        row_refs = refs[:n_r]
        vec_refs = refs[n_r:n_r + n_v]
        orow_refs = refs[n_r + n_v:n_r + n_v + n_or]
        oacc_refs = refs[n_r + n_v + n_or:]
        vv = [r[...] for r in vec_refs]

        n_sub = tr // sub
        together = 4 if n_sub % 4 == 0 else 1

        def step(s, accs):
            for t in range(together):
                sl = pl.ds(pl.multiple_of((s * together + t) * sub, sub), sub)
                if col_chunk is None:
                    ro, ao = fn([r[sl, :] for r in row_refs], vv)
                    for o_ref, o in zip(orow_refs, ro):
                        o_ref[sl, :] = o.astype(o_ref.dtype)
                    accs = tuple(x + y for x, y in zip(accs, ao))
                else:
                    for c0 in range(0, rows[0][2], col_chunk):
                        cs_ = pl.ds(c0, col_chunk)
                        ro, _ = fn([r[sl, cs_] for r in row_refs], vv)
                        for o_ref, o in zip(orow_refs, ro):
                            o_ref[sl, cs_] = o.astype(o_ref.dtype)
            return accs

        accs = lax.fori_loop(0, n_sub // together, step, tuple(jnp.zeros((1, w), f32) for w in out_accs))
        if n_oa:
            @pl.when(pl.program_id(0) == 0)
            def _():
                for ref in oacc_refs:
                    ref[...] = jnp.zeros_like(ref)

            for ref, x in zip(oacc_refs, accs):
                ref[...] += x

    in_specs = [pl.BlockSpec((tr, w), functools.partial(lambda i, cb: (i, cb), cb=cb)) for (_, cb, w) in rows]
    in_specs += [pl.BlockSpec((1, v.shape[1]), lambda i: (0, 0)) for v in vecs]
    out_specs = [pl.BlockSpec((tr, w), lambda i: (i, 0)) for (w, _) in out_rows]
    out_specs += [pl.BlockSpec((1, w), lambda i: (0, 0)) for w in out_accs]
    out_shape = [_sds((R, w), dt) for (w, dt) in out_rows] + [_sds((1, w)) for w in out_accs]
    res = pl.pallas_call(
        body, name=name, grid=(R // tr,), in_specs=in_specs, out_specs=out_specs, out_shape=out_shape,
        compiler_params=_params(("arbitrary",)),
    )(*[r[0] for r in rows], *vecs)
    return res[:n_or], res[n_or:]


def _silu(x):
    return x * jax.nn.sigmoid(x)


def _rms(x, g):
    return x * lax.rsqrt(jnp.mean(x * x, -1, keepdims=True) + EPS) * g


def _adaln_f(x, g, sc, sh):
    return _rms(x, g) * (1.0 + sc) + sh


def _gate_f(y, z, g):
    return _rms(y * _silu(z), g)


def _lnsilu_f(u, g, b):
    mu = jnp.mean(u, -1, keepdims=True)
    var = jnp.mean(jnp.square(u - mu), -1, keepdims=True)
    return _silu((u - mu) * lax.rsqrt(var + EPS) * g + b)


def _adaln_fwd(x, g, sc, sh, name):
    (h,), _ = _rowwise(name, lambda rv, vv: ([_adaln_f(rv[0], *vv)], []), [x], [g, sc, sh], [(x.shape[1], bf16)], [],
                       sub=16)
    return h


def _adaln_bwd(x, g, sc, sh, dh, dres, name):
    def fn(rv, vv):
        xv, dhv, drv = rv
        _, vjp = jax.vjp(_adaln_f, xv, *vv)
        dx, dg, dsc, dsh = vjp(dhv)
        return [dx + drv], [dg, dsc, dsh]
    w = x.shape[1]
    (dx,), accs = _rowwise(name, fn, [x, dh, dres], [g, sc, sh], [(w, f32)], [w, w, w])
    return dx, accs


def _resid_fwd(x, gate, mix, name):
    (y,), _ = _rowwise(name, lambda rv, vv: ([rv[0] + vv[0] * rv[1]], []), [x, mix], [gate], [(x.shape[1], f32)], [])
    return y


def _resid_bwd(dx, mix, gate, name):
    def fn(rv, vv):
        dxv, mv = rv
        dm = vv[0] * dxv
        return [dm], [jnp.sum(dxv * mv, 0, keepdims=True), jnp.sum(dm, 0, keepdims=True)]
    w = dx.shape[1]
    (dmix,), accs = _rowwise(name, fn, [dx, mix], [gate], [(w, bf16)], [w, w], sub=16)
    return dmix, accs


def _add3(a, b, c, name):
    (y,), _ = _rowwise(name, lambda rv, vv: ([rv[0] + rv[1] + rv[2]], []), [a, b, c], [], [(a.shape[1], bf16)], [],
                       sub=16)
    return y


CONV_HALO = 32
CONV_ROWS = 64


def _conv_fwd(x, w, b, *, silu, name, tr=512):
    S, C = x.shape
    K = w.shape[0]
    H = CONV_HALO
    assert K - 1 <= H and S % tr == 0 and tr % H == 0 and C % LANES == 0
    nh = tr // H

    def body(xp_ref, xc_ref, w_ref, b_ref, *rest):
        outs, scr = rest[:-1], rest[-1]
        i = pl.program_id(1)
        scr[pl.ds(0, H), :] = jnp.where(i > 0, xp_ref[...], 0.0)
        scr[pl.ds(H, tr), :] = xc_ref[...]
        taps = [w_ref[pl.ds(k, 1), :] for k in range(K)]
        for c0 in range(0, tr, CONV_ROWS):
            acc = jnp.zeros((CONV_ROWS, LANES), f32) + b_ref[...]
            for k in range(K):
                acc = acc + scr[pl.ds(c0 + H - (K - 1) + k, CONV_ROWS), :] * taps[k]
            outs[0][pl.ds(c0, CONV_ROWS), :] = acc
            if silu:
                outs[1][pl.ds(c0, CONV_ROWS), :] = _silu(acc)

    n_out = 2 if silu else 1
    return pl.pallas_call(
        body, name=name, grid=(C // LANES, S // tr),
        in_specs=[pl.BlockSpec((H, LANES), lambda j, i: (jnp.maximum(i * nh - 1, 0), j)),
                  pl.BlockSpec((tr, LANES), lambda j, i: (i, j)),
                  pl.BlockSpec((K, LANES), lambda j, i: (0, j)),
                  pl.BlockSpec((1, LANES), lambda j, i: (0, j))],
        out_specs=[pl.BlockSpec((tr, LANES), lambda j, i: (i, j))] * n_out,
        out_shape=[_sds((S, C))] * n_out,
        scratch_shapes=[pltpu.VMEM((tr + H, LANES), f32)],
        compiler_params=_params(("parallel", "arbitrary")),
    )(x, x, w, b)


def _conv_bwd(x, w, dact, pre, *, silu, name, dx_dtype=f32, tr=512):
    S, C = x.shape
    K = w.shape[0]
    H = CONV_HALO
    nh = tr // H
    n_i = S // tr
    kp = -(-K // 8) * 8

    def dsilu(p):
        s = jax.nn.sigmoid(p)
        return s * (1.0 + p * (1.0 - s))

    def body(*refs):
        if silu:
            xp_ref, xc_ref, w_ref, dc_ref, dn_ref, pc_ref, pn_ref, dx_ref, dw_ref, db_ref, xs, ds = refs
        else:
            xp_ref, xc_ref, w_ref, dc_ref, dn_ref, dx_ref, dw_ref, db_ref, xs, ds = refs
        i = pl.program_id(1)
        xs[pl.ds(0, H), :] = jnp.where(i > 0, xp_ref[...], 0.0)
        xs[pl.ds(H, tr), :] = xc_ref[...]
        dcur = dc_ref[...]
        dnext = dn_ref[...]
        if silu:
            dcur = dcur * dsilu(pc_ref[...])
            dnext = dnext * dsilu(pn_ref[...])
        ds[pl.ds(0, tr), :] = dcur
        ds[pl.ds(tr, H), :] = jnp.where(i < n_i - 1, dnext, 0.0)
        taps = [w_ref[pl.ds(k, 1), :] for k in range(K)]
        fold = lambda t: jnp.sum(t.reshape(CONV_ROWS // 8, 8, LANES), axis=0)
        dw_parts = [jnp.zeros((8, LANES), f32) for _ in range(K)]
        db_part = jnp.zeros((8, LANES), f32)
        for c0 in range(0, tr, CONV_ROWS):
            acc = jnp.zeros((CONV_ROWS, LANES), f32)
            d_c = ds[pl.ds(c0, CONV_ROWS), :]
            for k in range(K):
                acc = acc + ds[pl.ds(c0 + K - 1 - k, CONV_ROWS), :] * taps[k]
                dw_parts[k] = dw_parts[k] + fold(d_c * xs[pl.ds(c0 + H - (K - 1) + k, CONV_ROWS), :])
            db_part = db_part + fold(d_c)
            dx_ref[pl.ds(c0, CONV_ROWS), :] = acc.astype(dx_ref.dtype)

        @pl.when(i == 0)
        def _():
            dw_ref[...] = jnp.zeros_like(dw_ref)
            db_ref[...] = jnp.zeros_like(db_ref)

        for k in range(K):
            dw_ref[pl.ds(k, 1), :] += jnp.sum(dw_parts[k], 0, keepdims=True)
        db_ref[...] += jnp.sum(db_part, 0, keepdims=True)

    prev = pl.BlockSpec((H, LANES), lambda j, i: (jnp.maximum(i * nh - 1, 0), j))
    cur = pl.BlockSpec((tr, LANES), lambda j, i: (i, j))
    nxt = pl.BlockSpec((H, LANES), lambda j, i: (jnp.minimum((i + 1) * nh, n_i * nh - 1), j))
    in_specs = [prev, cur, pl.BlockSpec((K, LANES), lambda j, i: (0, j)), cur, nxt]
    args = [x, x, w, dact, dact]
    if silu:
        in_specs += [cur, nxt]
        args += [pre, pre]
    dx, dw, db = pl.pallas_call(
        body, name=name, grid=(C // LANES, n_i), in_specs=in_specs,
        out_specs=[cur, pl.BlockSpec((kp, LANES), lambda j, i: (0, j)), pl.BlockSpec((1, LANES), lambda j, i: (0, j))],
        out_shape=[_sds((S, C), dx_dtype), _sds((kp, C)), _sds((1, C))],
        scratch_shapes=[pltpu.VMEM((tr + H, LANES), f32), pltpu.VMEM((tr + H, LANES), f32)],
        compiler_params=_params(("parallel", "arbitrary")),
    )(*args)
    return dx, dw[:K], db


def _dot(a, b):
    return jnp.dot(a.astype(bf16), b.astype(bf16), preferred_element_type=f32)


def _dot_nt(a, b):
    return lax.dot_general(a.astype(bf16), b.astype(bf16), (((1,), (1,)), ((), ())), preferred_element_type=f32)


def _dot_tn(a, b):
    return lax.dot_general(a.astype(bf16), b.astype(bf16), (((0,), (0,)), ((), ())), preferred_element_type=f32)


def _softplus(x):
    return jnp.maximum(x, 0.0) + jnp.log(1.0 + jnp.exp(-jnp.abs(x)))


def _tri(q):
    i = lax.broadcasted_iota(jnp.int32, (q, q), 0)
    j = lax.broadcasted_iota(jnp.int32, (q, q), 1)
    return i >= j


def _ssd_prep(dtraw, dt_bias, a_log):
    q = dtraw.shape[0]
    dt = _softplus(dtraw + dt_bias)
    A = -jnp.exp(a_log)
    tri = _tri(q)
    cs = jnp.dot(tri.astype(f32), dt * A, preferred_element_type=f32, precision=lax.Precision.HIGHEST)
    return dt, A, cs, cs.T, tri


def _expand(cols, h0, n, width):
    q = cols.shape[0]
    return jnp.concatenate([jnp.broadcast_to(cols[:, h0 + r:h0 + r + 1], (q, width)) for r in range(n)], axis=1)


def _ssd_fwd(xbc, dtraw, dt_bias, a_log, d_skip, di, name):
    S, CD = xbc.shape
    Q, N, G = SSM_CHUNK, SSM_STATE, SSM_GROUPS
    nc = S // Q
    nh = di // HEAD_DIM
    R = nh // G
    gw = R * HEAD_DIM

    def body(xbc_ref, dt_ref, bias_ref, alog_ref, dsk_ref, y_ref, hin_ref, state):
        c = pl.program_id(0)

        @pl.when(c == 0)
        def _():
            state[...] = jnp.zeros_like(state)

        hin_ref[...] = state[...]
        dt, A, cs, csT, tri = _ssd_prep(dt_ref[...], bias_ref[...], alog_ref[...])
        dsk = dsk_ref[...]
        ecs = jnp.exp(cs)
        dend = jnp.exp(cs[Q - 1:Q, :] - cs)
        elast = jnp.exp(cs[Q - 1:Q, :])
        for g in range(G):
            h0 = g * R
            Bg = xbc_ref[:, pl.ds(di + g * N, N)]
            Cg = xbc_ref[:, pl.ds(di + G * N + g * N, N)]
            xg = xbc_ref[:, pl.ds(g * gw, gw)]
            Hg = state[pl.ds(g * gw, gw), :]
            Gm = _dot_nt(Cg, Bg)
            xdt = xg * _expand(dt, h0, R, HEAD_DIM)
            yoff = _dot_nt(Cg, Hg) * _expand(ecs, h0, R, HEAD_DIM)
            ys = []
            for r in range(R):
                h = h0 + r
                L = jnp.exp(jnp.where(tri, cs[:, h:h + 1] - csT[h:h + 1, :], -jnp.inf))
                ys.append(_dot(Gm * L, xdt[:, r * HEAD_DIM:(r + 1) * HEAD_DIM]))
            y = jnp.concatenate(ys, axis=1) + yoff + xg * _expand(dsk, h0, R, HEAD_DIM)
            y_ref[:, pl.ds(g * gw, gw)] = y
            hnew = _dot_tn(xdt * _expand(dend, h0, R, HEAD_DIM), Bg)
            escale = jnp.concatenate([jnp.broadcast_to(elast[:, h0 + r:h0 + r + 1], (HEAD_DIM, N)) for r in range(R)], axis=0)
            state[pl.ds(g * gw, gw), :] = escale * Hg + hnew

    vec = pl.BlockSpec((1, LANES), lambda c: (0, 0))
    return pl.pallas_call(
        body, name=name, grid=(nc,),
        in_specs=[pl.BlockSpec((Q, CD), lambda c: (c, 0)), pl.BlockSpec((Q, LANES), lambda c: (c, 0)), vec, vec, vec],
        out_specs=[pl.BlockSpec((Q, di), lambda c: (c, 0)), pl.BlockSpec((None, di, N), lambda c: (c, 0, 0))],
        out_shape=[_sds((S, di)), _sds((nc, di, N))],
        scratch_shapes=[pltpu.VMEM((di, N), f32)],
        compiler_params=_params(("arbitrary",)),
    )(xbc, dtraw, dt_bias, a_log, d_skip)


def _dot_exact(a, b):
    bb = b.astype(bf16)
    hi = a.astype(bf16)
    rest = a - hi.astype(f32)
    mid = rest.astype(bf16)
    low = (rest - mid.astype(f32)).astype(bf16)
    one_pass = lambda t: jnp.dot(t, bb, preferred_element_type=f32)
    return one_pass(hi) + one_pass(mid) + one_pass(low)


def _ssd_bwd(xbc, dtraw, dt_bias, a_log, d_skip, hin, y, dy, di, name):
    S, CD = xbc.shape
    Q, N, G = SSM_CHUNK, SSM_STATE, SSM_GROUPS
    nc = S // Q
    nh = di // HEAD_DIM
    R = nh // G
    gw = R * HEAD_DIM
    P = HEAD_DIM
    head_of_col = jnp.asarray((np.arange(di)[:, None] // P == np.arange(LANES)[None, :]).astype(np.float32))
    dsk_wide = jnp.repeat(d_skip[0, :nh], P)[None]

    def body(xbc_ref, dt_ref, bias_ref, alog_ref, dskw_ref, hoc_ref, hin_ref, y_ref, dy_ref,
             dxbc_ref, ddt_ref, dA_ref, ddsk_ref, dtb_ref, dstate, dxdt_all, tend_all, yoff_all, colterm_all):
        c = pl.program_id(0)

        @pl.when(c == 0)
        def _():
            dstate[...] = jnp.zeros_like(dstate)
            dA_ref[...] = jnp.zeros_like(dA_ref)
            ddsk_ref[...] = jnp.zeros_like(ddsk_ref)
            dtb_ref[...] = jnp.zeros_like(dtb_ref)

        dtraw_v = dt_ref[...]
        dt, A, cs, csT, tri = _ssd_prep(dtraw_v, bias_ref[...], alog_ref[...])
        tri_t = jnp.logical_not(tri) | (lax.broadcasted_iota(jnp.int32, (Q, Q), 0) == lax.broadcasted_iota(jnp.int32, (Q, Q), 1))
        ecs = jnp.exp(cs)
        dend = jnp.exp(cs[Q - 1:Q, :] - cs)
        elast = jnp.exp(cs[Q - 1:Q, :])
        hoc = hoc_ref[...]
        state_dot = jnp.sum(_dot_exact(dstate[...] * hin_ref[...], jnp.ones((N, LANES), f32)) * hoc, 0, keepdims=True) * elast
        for g in range(G):
            h0 = g * R
            Bg = xbc_ref[:, pl.ds(di + g * N, N)]
            Cg = xbc_ref[:, pl.ds(di + G * N + g * N, N)]
            xg = xbc_ref[:, pl.ds(g * gw, gw)]
            dyg = dy_ref[:, pl.ds(g * gw, gw)]
            Hg = hin_ref[pl.ds(g * gw, gw), :]
            dHg = dstate[pl.ds(g * gw, gw), :]
            dt_e = _expand(dt, h0, R, P)
            ecs_e = _expand(ecs, h0, R, P)
            dend_e = _expand(dend, h0, R, P)
            cols = pl.ds(g * gw, gw)
            Gm = _dot_nt(Cg, Bg)
            Gm_t = _dot_nt(Bg, Cg)
            xdt = xg * dt_e
            dye = dyg * ecs_e
            bdh = _dot_nt(Bg, dHg)
            dC = _dot(dye, Hg)
            dB = _dot(xdt * dend_e, dHg)
            dHin = _dot_tn(dye, Cg)
            dxdt_state = dend_e * bdh
            end_term = xdt * dxdt_state
            tend_all[:, cols] = end_term
            yoff_all[:, cols] = _dot_nt(Cg, Hg) * ecs_e
            dG = jnp.zeros((Q, Q), f32)
            dxd = []
            for r in range(R):
                h = h0 + r
                sl = slice(r * P, (r + 1) * P)
                seg = cs[:, h:h + 1] - csT[h:h + 1, :]
                L = jnp.exp(jnp.where(tri, seg, -jnp.inf))
                L_t = jnp.exp(jnp.where(tri_t, -seg, -jnp.inf))
                dyh = dyg[:, sl]
                dG = dG + _dot_nt(dyh, xdt[:, sl]) * L
                dxd.append(_dot(Gm_t * L_t, dyh))
            dxdt_diag = jnp.concatenate(dxd, axis=1)
            dxdt = dxdt_diag + dxdt_state
            dxdt_all[:, cols] = dxdt
            colterm_all[:, cols] = xdt.astype(bf16).astype(f32) * dxdt_diag + end_term
            dxbc_ref[:, cols] = dxdt * dt_e + dyg * dskw_ref[:, cols]
            dxbc_ref[:, pl.ds(di + g * N, N)] = dB + _dot_tn(dG, Cg)
            dxbc_ref[:, pl.ds(di + G * N + g * N, N)] = dC + _dot(dG, Bg)
            escale = jnp.concatenate([jnp.broadcast_to(elast[:, h0 + r:h0 + r + 1], (P, N)) for r in range(R)], axis=0)
            dstate[pl.ds(g * gw, gw), :] = escale * dHg + dHin
        xs = xbc_ref[:, pl.ds(0, di)]
        dyv = dy_ref[...]
        yoff = yoff_all[...]
        y_diag = y_ref[...] - dskw_ref[...] * xs - yoff
        rs_y = _dot_exact(dyv.astype(bf16).astype(f32) * y_diag + dyv * yoff, hoc)
        rs_c = _dot_exact(colterm_all[...], hoc)
        rs_x = _dot_exact(dxdt_all[...] * xs, hoc)
        end_dot = _dot_exact(jnp.broadcast_to(jnp.sum(tend_all[...], 0, keepdims=True), (8, di)), hoc)[0:1]
        last = lax.broadcasted_iota(jnp.int32, (Q, 1), 0) == Q - 1
        dcs = rs_y - rs_c + jnp.where(last, end_dot + state_dot, 0.0)
        da = lax.dot_general(tri.astype(f32), dcs, (((0,), (0,)), ((), ())), preferred_element_type=f32,
                             precision=lax.Precision.HIGHEST)
        ddt = da * A + rs_x
        ddtraw = ddt * jax.nn.sigmoid(dtraw_v + bias_ref[...])
        ddt_ref[...] = ddtraw.astype(ddt_ref.dtype)
        dA_ref[...] += jnp.sum(da * dt, 0, keepdims=True) * A
        ddsk_ref[...] += jnp.sum(_dot_exact(dyv * xs, hoc), 0, keepdims=True)
        dtb_ref[...] += jnp.sum(ddtraw, 0, keepdims=True)

    vec = pl.BlockSpec((1, LANES), lambda c: (0, 0))
    rev = lambda c: (nc - 1 - c, 0)
    return pl.pallas_call(
        body, name=name, grid=(nc,),
        in_specs=[pl.BlockSpec((Q, CD), rev), pl.BlockSpec((Q, LANES), rev), vec, vec,
                  pl.BlockSpec((1, di), lambda c: (0, 0)), pl.BlockSpec((di, LANES), lambda c: (0, 0)),
                  pl.BlockSpec((None, di, N), lambda c: (nc - 1 - c, 0, 0)), pl.BlockSpec((Q, di), rev),
                  pl.BlockSpec((Q, di), rev)],
        out_specs=[pl.BlockSpec((Q, CD), rev), pl.BlockSpec((Q, LANES), rev), vec, vec, vec],
        out_shape=[_sds((S, CD)), _sds((S, LANES), bf16), _sds((1, LANES)), _sds((1, LANES)), _sds((1, LANES))],
        scratch_shapes=[pltpu.VMEM((di, N), f32)] + [pltpu.VMEM((Q, di), f32)] * 4,
        compiler_params=_params(("arbitrary",)),
    )(xbc, dtraw, dt_bias, a_log, dsk_wide, head_of_col, hin, y, dy)


def _t5_bucket_np(dist):
    max_exact = REL_BUCKETS // 2
    n = np.maximum(dist, 1).astype(np.float32)
    large = np.float32(max_exact) + np.log(n / np.float32(max_exact)) / np.float32(math.log(REL_MAX_DIST / max_exact)) * np.float32(REL_BUCKETS - max_exact)
    large = np.minimum(large.astype(np.int32), REL_BUCKETS - 1)
    return np.where(dist < max_exact, dist, large)


def _bucket_onehot():
    i = np.arange(ATT_BLK)[None, :]
    j = np.arange(2 * ATT_BLK)[:, None]
    delta = np.maximum(ATT_BLK + i - j, 0)
    out = np.zeros((len(ATT_DILATIONS), REL_BUCKETS, ATT_BLK * 2 * ATT_BLK), np.float32)
    for gi, d in enumerate(ATT_DILATIONS):
        b = _t5_bucket_np(delta * d).reshape(-1)
        out[gi, b, np.arange(b.size)] = 1.0
    return out


def _exact_mm(a, b, *, name, tb=False):
    M, K = a.shape
    N = b.shape[0] if tb else b.shape[1]
    tn = _tile(N, 4096, LANES)
    dn = (((1,), (1 if tb else 0,)), ((), ()))

    def body(a_ref, b_ref, o_ref):
        o_ref[...] = lax.dot_general(a_ref[...], b_ref[...], dn, preferred_element_type=f32,
                                     precision=lax.Precision.HIGHEST)

    return pl.pallas_call(
        body, name=name, grid=(N // tn,),
        in_specs=[pl.BlockSpec((M, K), lambda j: (0, 0)),
                  pl.BlockSpec((tn, K), lambda j: (j, 0)) if tb else pl.BlockSpec((K, tn), lambda j: (0, j))],
        out_specs=pl.BlockSpec((M, tn), lambda j: (0, j)), out_shape=_sds((M, N)),
        compiler_params=_params(("parallel",)),
    )(a, b)


def _band_penalty():
    i = np.arange(ATT_BLK)[None, :]
    j = np.arange(2 * ATT_BLK)[:, None]
    delta = ATT_BLK + i - j
    return np.where((delta >= 0) & (delta <= ATT_BLK), 0.0, -np.inf).astype(np.float32)


def _first_block_keep(n):
    key = lax.broadcasted_iota(jnp.int32, (2 * ATT_BLK, ATT_BLK), 0)
    return (key >= ATT_BLK) | (n > 0)


ATT_SCALE = HEAD_DIM ** -0.5


def _rows(ref, r, d):
    return ref[...] if d == 1 else ref[pl.ds(r, ATT_BLK, stride=d), :]


def _set_rows(ref, r, d, val):
    if d == 1:
        ref[...] = val
    else:
        ref[pl.ds(r, ATT_BLK, stride=d), :] = val


def _attn_width(d, D):
    return D if d == 1 else LANES


def _over_residues(d, one, unroll=1):
    if d == 1:
        one(0)
    else:
        lax.fori_loop(0, d, lambda r, c: (one(r), c)[1], 0, unroll=unroll)


def _attn_fwd(q, k, v, bias_t, d, name):
    S, D = q.shape
    nb = S // (d * ATT_BLK)
    H = D // HEAD_DIM
    W = _attn_width(d, D)
    HB = W // HEAD_DIM

    def body(q_ref, kp_ref, kc_ref, vp_ref, vc_ref, b_ref, o_ref, lse_ref):
        keep = _first_block_keep(pl.program_id(1))
        first = lax.broadcasted_iota(jnp.int32, (1, LANES), 1) < HEAD_DIM

        def one(r):
            qs = (_rows(q_ref, r, d) * ATT_SCALE).astype(bf16)
            kcat = jnp.concatenate([_rows(kp_ref, r, d), _rows(kc_ref, r, d)], axis=0).astype(bf16)
            vcat = jnp.concatenate([_rows(vp_ref, r, d), _rows(vc_ref, r, d)], axis=0).astype(bf16)
            outs = []
            for pair in range(W // LANES):
                ps = slice(pair * LANES, (pair + 1) * LANES)
                q2, k2, v2 = qs[:, ps], kcat[:, ps], vcat[:, ps]
                o2 = jnp.zeros((ATT_BLK, LANES), f32)
                for e in range(2):
                    h = 2 * pair + e
                    mine = first if e == 0 else jnp.logical_not(first)
                    zero = jnp.zeros((), bf16)
                    st = jnp.where(keep, _dot_nt(k2, jnp.where(mine, q2, zero)) + b_ref[h], -jnp.inf)
                    m = jnp.max(st, 0, keepdims=True)
                    pt = jnp.exp(st - m)
                    l = jnp.sum(pt, 0, keepdims=True)
                    o2 = o2 + _dot_tn(pt * (1.0 / l), jnp.where(mine, v2, zero))
                    lse_ref[r, h] = m + jnp.log(l)
                outs.append(o2)
            _set_rows(o_ref, r, d, jnp.concatenate(outs, axis=1))

        _over_residues(d, one, unroll=4)

    cur = pl.BlockSpec((ATT_BLK * d, W), lambda j, n: (n, j))
    prev = pl.BlockSpec((ATT_BLK * d, W), lambda j, n: (jnp.maximum(n - 1, 0), j))
    return pl.pallas_call(
        body, name=name, grid=(D // W, nb),
        in_specs=[cur, prev, cur, prev, cur, pl.BlockSpec((HB, 2 * ATT_BLK, ATT_BLK), lambda j, n: (j, 0, 0))],
        out_specs=[cur, pl.BlockSpec((None, d, HB, 1, LANES), lambda j, n: (n, 0, j, 0, 0))],
        out_shape=[_sds((S, D)), _sds((nb, d, H, 1, LANES))],
        compiler_params=_params(("parallel", "arbitrary")),
    )(q, k, k, v, v, bias_t)


def _from_blocks(rows, lanes=None):
    nb, d, H = rows.shape[:3]
    a = jnp.transpose(rows[:, :, :, 0, :], (0, 3, 1, 2)).reshape(nb * ATT_BLK * d, H)
    return a if lanes is None else jnp.pad(a, ((0, 0), (0, lanes - H)))


def _by_block(a, d):
    S, H = a.shape
    t = jnp.transpose(a.reshape(S // (d * ATT_BLK), ATT_BLK, d, H), (0, 2, 3, 1))
    return t[:, :, :, None, :]


def _head_sums(a, b, name):
    S, D = a.shape
    tr = _tile(S, 512, 8)
    hoc = jnp.asarray((np.arange(D)[:, None] // HEAD_DIM == np.arange(LANES)[None, :]).astype(np.float32))

    def body(a_ref, b_ref, h_ref, o_ref):
        o_ref[...] = _dot_exact(a_ref[...] * b_ref[...], h_ref[...])

    return pl.pallas_call(
        body, name=name, grid=(S // tr,),
        in_specs=[pl.BlockSpec((tr, D), lambda i: (i, 0)), pl.BlockSpec((tr, D), lambda i: (i, 0)),
                  pl.BlockSpec((D, LANES), lambda i: (0, 0))],
        out_specs=pl.BlockSpec((tr, LANES), lambda i: (i, 0)), out_shape=_sds((S, LANES)),
        compiler_params=_params(("parallel",)),
    )(a, b, hoc)


def _attn_bwd(q, k, v, bias_t, datt, lse_rows, dsum_rows, d, name):
    S, D = q.shape
    nb = S // (d * ATT_BLK)
    H = D // HEAD_DIM
    W = _attn_width(d, D)
    HB = W // HEAD_DIM

    def body(q_ref, kp_ref, kc_ref, vp_ref, vc_ref, b_ref, do_ref, lse_ref, dsum_ref,
             dq_ref, dk_ref, dv_ref, db_ref, carry_k, carry_v):
        j = pl.program_id(0)
        n = pl.program_id(1)

        @pl.when(n == 0)
        def _():
            carry_k[...] = jnp.zeros_like(carry_k)
            carry_v[...] = jnp.zeros_like(carry_v)
            db_ref[...] = jnp.zeros_like(db_ref)

        @pl.when(n < nb)
        def _():
            key = lax.broadcasted_iota(jnp.int32, (2 * ATT_BLK, ATT_BLK), 0)
            keep = (key >= ATT_BLK) | (n > 0)
            first = lax.broadcasted_iota(jnp.int32, (1, LANES), 1) < HEAD_DIM

            def one(r):
                qs = (_rows(q_ref, r, d) * ATT_SCALE).astype(bf16)
                kcat = jnp.concatenate([_rows(kp_ref, r, d), _rows(kc_ref, r, d)], axis=0).astype(bf16)
                vcat = jnp.concatenate([_rows(vp_ref, r, d), _rows(vc_ref, r, d)], axis=0).astype(bf16)
                dob = _rows(do_ref, r, d).astype(bf16)
                dqs, dks, dvs = [], [], []
                for pair in range(W // LANES):
                    ps = slice(pair * LANES, (pair + 1) * LANES)
                    q2, k2, v2, do2 = qs[:, ps], kcat[:, ps], vcat[:, ps], dob[:, ps]
                    dq2 = jnp.zeros((ATT_BLK, LANES), f32)
                    dk2 = jnp.zeros((2 * ATT_BLK, LANES), f32)
                    dv2 = jnp.zeros((2 * ATT_BLK, LANES), f32)
                    for e in range(2):
                        h = 2 * pair + e
                        mine = first if e == 0 else jnp.logical_not(first)
                        zero = jnp.zeros((), bf16)
                        qm, dom, km = jnp.where(mine, q2, zero), jnp.where(mine, do2, zero), jnp.where(mine, k2, zero)
                        st = jnp.where(keep, _dot_nt(k2, qm) + b_ref[h], -jnp.inf)
                        pt = jnp.exp(st - lse_ref[r, j * HB + h])
                        dst = pt * (_dot_nt(v2, dom) - dsum_ref[r, j * HB + h])
                        db_ref[h] += dst
                        dv2 = dv2 + _dot(pt, dom)
                        dk2 = dk2 + _dot(dst, qm)
                        dq2 = dq2 + _dot_tn(dst, km)
                    dqs.append(dq2 * ATT_SCALE)
                    dks.append(dk2)
                    dvs.append(dv2)
                _set_rows(dq_ref, r, d, jnp.concatenate(dqs, axis=1))
                dk = jnp.concatenate(dks, axis=1)
                dv = jnp.concatenate(dvs, axis=1)
                _set_rows(dk_ref, r, d, carry_k[r] + dk[:ATT_BLK])
                _set_rows(dv_ref, r, d, carry_v[r] + dv[:ATT_BLK])
                carry_k[r] = dk[ATT_BLK:]
                carry_v[r] = dv[ATT_BLK:]

            _over_residues(d, one, unroll=2)

        @pl.when(n == nb)
        def _():
            def last(r):
                _set_rows(dk_ref, r, d, carry_k[r])
                _set_rows(dv_ref, r, d, carry_v[r])

            _over_residues(d, last)

    nq = lambda n: jnp.minimum(n, nb - 1)
    cur = pl.BlockSpec((ATT_BLK * d, W), lambda j, n: (nq(n), j))
    prev = pl.BlockSpec((ATT_BLK * d, W), lambda j, n: (jnp.maximum(nq(n) - 1, 0), j))
    done = pl.BlockSpec((ATT_BLK * d, W), lambda j, n: (jnp.maximum(n - 1, 0), j))
    bspec = pl.BlockSpec((HB, 2 * ATT_BLK, ATT_BLK), lambda j, n: (j, 0, 0))
    rows = pl.BlockSpec((None, d, H, 1, LANES), lambda j, n: (nq(n), 0, 0, 0, 0))
    return pl.pallas_call(
        body, name=name, grid=(D // W, nb + 1),
        in_specs=[cur, prev, cur, prev, cur, bspec, cur, rows, rows],
        out_specs=[cur, done, done, bspec],
        out_shape=[_sds((S, D)), _sds((S, D)), _sds((S, D)), _sds((H, 2 * ATT_BLK, ATT_BLK))],
        scratch_shapes=[pltpu.VMEM((d, ATT_BLK, W), f32), pltpu.VMEM((d, ATT_BLK, W), f32)],
        compiler_params=_params(("arbitrary", "arbitrary")),
    )(q, k, k, v, v, bias_t, datt, lse_rows, dsum_rows)


def _attn_combine(os_, lses, name):
    S, D = os_[0].shape
    tr = _tile(S, 128, 16)
    head_cols = jnp.asarray((np.arange(LANES)[:, None] == np.arange(D)[None, :] // HEAD_DIM).astype(np.float32))

    def body(o0, o1, o2, l0, l1, l2, hc_ref, att_ref, attb_ref, lse_ref):
        a, b, c = l0[...], l1[...], l2[...]
        m = jnp.maximum(jnp.maximum(a, b), c)
        e0, e1, e2 = jnp.exp(a - m), jnp.exp(b - m), jnp.exp(c - m)
        tot = e0 + e1 + e2
        wide = lambda w: _dot_exact(w / tot, hc_ref[...])
        att = wide(e0) * o0[...] + wide(e1) * o1[...] + wide(e2) * o2[...]
        att_ref[...] = att
        attb_ref[...] = att.astype(bf16)
        lse_ref[...] = m + jnp.log(tot)

    wide_spec = pl.BlockSpec((tr, D), lambda i: (i, 0))
    lane_spec = pl.BlockSpec((tr, LANES), lambda i: (i, 0))
    return pl.pallas_call(
        body, name=name, grid=(S // tr,),
        in_specs=[wide_spec] * 3 + [lane_spec] * 3 + [pl.BlockSpec((LANES, D), lambda i: (0, 0))],
        out_specs=[wide_spec, wide_spec, lane_spec], out_shape=[_sds((S, D)), _sds((S, D), bf16), _sds((S, LANES))],
        compiler_params=_params(("parallel",)),
    )(*os_, *lses, head_cols)


ANY = pl.BlockSpec(memory_space=pl.ANY)


def _all_gather(vs, name):
    n = len(vs)

    def body(*refs):
        x_refs, out_refs = refs[:n], refs[n:2 * n]
        send_sems, recv_sems, local_sems = refs[2 * n:]
        x, y, c = lax.axis_index("x"), lax.axis_index("y"), lax.axis_index("c")
        me, sibling = (x, y, c), (x, y, 1 - c)
        chips = [(1 - x, y), (x, 1 - y), (1 - x, 1 - y)]

        def slot(i, px, py, pc):
            return out_refs[i].at[4 * px + 2 * py + pc]

        def copy(i, k, block, to, src=None):
            return pltpu.make_async_remote_copy(
                src_ref=slot(i, *block) if src is None else src, dst_ref=slot(i, *block),
                send_sem=send_sems.at[i, k], recv_sem=recv_sems.at[i, k], device_id=to, device_id_type=MESH)

        mine = [pltpu.make_async_copy(x_refs[i], slot(i, *me), local_sems.at[i]) for i in range(n)]
        for cp in mine:
            cp.start()
        first = []
        for i in range(n):
            first.append(copy(i, 0, me, sibling, src=x_refs[i]))
            first += [copy(i, 1 + j, me, (*chip, c), src=x_refs[i]) for j, chip in enumerate(chips)]
        for cp in first:
            cp.start()
        passed = []
        for i in range(n):
            for j, chip in enumerate(chips):
                copy(i, 1 + j, (*chip, c), me).wait_recv()
                cp = copy(i, 4 + j, (*chip, c), sibling)
                cp.start()
                passed.append(cp)
        for i in range(n):
            copy(i, 0, sibling, me).wait_recv()
            for j, chip in enumerate(chips):
                copy(i, 4 + j, (*chip, 1 - c), me).wait_recv()
        for cp in first + passed:
            cp.wait_send()
        for cp in mine:
            cp.wait()

    return pl.pallas_call(
        body, name=name, out_shape=[_sds((N_DEV,) + v.shape, v.dtype) for v in vs], in_specs=[ANY] * n,
        out_specs=[ANY] * n,
        scratch_shapes=[pltpu.SemaphoreType.DMA((n, 7)), pltpu.SemaphoreType.DMA((n, 7)), pltpu.SemaphoreType.DMA((n,))],
    )(*vs)


def _rs_sibling(parts, name):
    n = len(parts)

    def body(*refs):
        p_refs, out_refs = refs[:n], refs[n:2 * n]
        send_sems, recv_sems = refs[2 * n:]
        x, y, c = lax.axis_index("x"), lax.axis_index("y"), lax.axis_index("c")
        cps = [pltpu.make_async_remote_copy(
            src_ref=p_refs[i].at[k, 1 - c], dst_ref=out_refs[i].at[k], send_sem=send_sems.at[i, k],
            recv_sem=recv_sems.at[i, k], device_id=(x, y, 1 - c), device_id_type=MESH)
            for i in range(n) for k in range(4)]
        for cp in cps:
            cp.start()
        for cp in cps:
            cp.wait()

    return pl.pallas_call(
        body, name=name, out_shape=[_sds((4,) + p.shape[2:], p.dtype) for p in parts], in_specs=[ANY] * n,
        out_specs=[ANY] * n,
        scratch_shapes=[pltpu.SemaphoreType.DMA((n, 4)), pltpu.SemaphoreType.DMA((n, 4))],
    )(*parts)


def _rs_chips(ts, name):
    n = len(ts)

    def body(*refs):
        t_refs, out_refs = refs[:n], refs[n:2 * n]
        send_sems, recv_sems, local_sems = refs[2 * n:]
        x, y, c = lax.axis_index("x"), lax.axis_index("y"), lax.axis_index("c")
        mine = 2 * x + y
        local = [pltpu.make_async_copy(t_refs[i].at[mine], out_refs[i].at[mine], local_sems.at[i]) for i in range(n)]
        for cp in local:
            cp.start()
        chips = [(1 - x, y), (x, 1 - y), (1 - x, 1 - y)]
        cps = [pltpu.make_async_remote_copy(
            src_ref=t_refs[i].at[2 * px + py], dst_ref=out_refs[i].at[mine], send_sem=send_sems.at[i, j],
            recv_sem=recv_sems.at[i, j], device_id=(px, py, c), device_id_type=MESH)
            for i in range(n) for j, (px, py) in enumerate(chips)]
        for cp in cps:
            cp.start()
        for cp in cps:
            cp.wait()
        for cp in local:
            cp.wait()

    return pl.pallas_call(
        body, name=name, out_shape=[_sds(t.shape, t.dtype) for t in ts], in_specs=[ANY] * n, out_specs=[ANY] * n,
        scratch_shapes=[pltpu.SemaphoreType.DMA((n, 3)), pltpu.SemaphoreType.DMA((n, 3)), pltpu.SemaphoreType.DMA((n,))],
    )(*ts)


def _pair_add(part, recv, c_arr, name):
    _, _, R, C = part.shape
    tr = _tile(R, PACK_ROW_TILE, 16)

    def body(c_ref, p_ref, r_ref, o_ref):
        o_ref[...] = (p_ref[...] + r_ref[...]).astype(o_ref.dtype)

    return pl.pallas_call(
        body, name=name,
        grid_spec=pltpu.PrefetchScalarGridSpec(
            num_scalar_prefetch=1, grid=(4, R // tr),
            in_specs=[pl.BlockSpec((None, None, tr, C), lambda k, i, c_ref: (k, c_ref[0], i, 0)),
                      pl.BlockSpec((None, tr, C), lambda k, i, c_ref: (k, i, 0))],
            out_specs=pl.BlockSpec((None, tr, C), lambda k, i, c_ref: (k, i, 0))),
        out_shape=_sds((4, R, C), bf16),
        compiler_params=_params(("parallel", "parallel")),
    )(c_arr, part, recv)


def _sum_slots(t, name):
    n, R, C = t.shape
    tr = _tile(R, PACK_ROW_TILE, 16)

    def body(t_ref, o_ref):
        acc = t_ref[0].astype(f32)
        for k in range(1, n):
            acc = acc + t_ref[k].astype(f32)
        o_ref[...] = acc

    return pl.pallas_call(
        body, name=name, grid=(R // tr,),
        in_specs=[pl.BlockSpec((n, tr, C), lambda i: (0, i, 0))],
        out_specs=pl.BlockSpec((tr, C), lambda i: (i, 0)), out_shape=_sds((R, C)),
        compiler_params=_params(("parallel",)),
    )(t)


def _reduce_scatter(parts, c_arr, name):
    parts4 = [p.reshape((4, 2) + p.shape[1:]) for p in parts]
    recv = _rs_sibling(parts4, name + "_sibling")
    ts = [_pair_add(p, r, c_arr, f"{name}_pair_{i}") for i, (p, r) in enumerate(zip(parts4, recv))]
    got = _rs_chips(ts, name + "_chips")
    return [_sum_slots(g, f"{name}_sum_{i}") for i, g in enumerate(got)]


HBM_SPEC = pl.BlockSpec(memory_space=pltpu.HBM)
SEM_SPEC = pl.BlockSpec(memory_space=pltpu.SEMAPHORE)
EFFECT = pltpu.SideEffectType.DATAFLOW_SIDE_EFFECTING


def _mesh_pos(p):
    return (p // 4, (p // 2) % 2, p % 2)


def _exchange_copy(src_refs, land_refs, send_sems, recv_sems, whole, dests, i, k):
    me = 4 * lax.axis_index("x") + 2 * lax.axis_index("y") + lax.axis_index("c")
    to = (me + k) % N_DEV
    frm = (me + N_DEV - k) % N_DEV
    lo, hi = dests
    src = src_refs[i] if whole else src_refs[i].at[jnp.minimum(jnp.maximum(to - lo, 0), hi - lo - 1)]
    s = i * (N_DEV - 1) + k - 1
    send = pltpu.make_async_remote_copy(src_ref=src, dst_ref=land_refs[i].at[me], send_sem=send_sems.at[s],
                                        recv_sem=recv_sems.at[s], device_id=_mesh_pos(to), device_id_type=MESH)
    recv = pltpu.make_async_remote_copy(src_ref=src, dst_ref=land_refs[i].at[frm], send_sem=send_sems.at[s],
                                        recv_sem=recv_sems.at[s], device_id=_mesh_pos(to), device_id_type=MESH)
    return send, recv, (to >= lo) & (to < hi), (me >= lo) & (me < hi)


def _exchange_start(srcs, whole, name, after=None, dests=(0, N_DEV)):
    n = len(srcs)
    lands = [lax.empty((N_DEV,) + s.shape[-2:], s.dtype) for s in srcs]
    after = list(after or [])
    n_in = 2 * n + len(after)
    everyone = dests == (0, N_DEV)

    def body(*refs):
        src_refs, land_refs = refs[:n], refs[n:2 * n]
        send_sems, recv_sems, token = refs[n_in], refs[n_in + 1], refs[-1]
        for i in range(n):
            for k in range(1, N_DEV):
                send, _, sends, _ = _exchange_copy(src_refs, land_refs, send_sems, recv_sems, whole, dests, i, k)
                if everyone:
                    send.start()
                else:
                    pl.when(sends)(send.start)
        token[...] = jnp.zeros_like(token)

    sems = pltpu.SemaphoreType.DMA((n * (N_DEV - 1),))
    outs = pl.pallas_call(
        body, name=name,
        out_shape=(sems, sems, *[pltpu.HBM(a.shape, a.dtype) for a in srcs + lands], _sds((8, LANES))),
        in_specs=[HBM_SPEC] * (2 * n) + [pl.BlockSpec(memory_space=pl.ANY)] * len(after),
        out_specs=(SEM_SPEC, SEM_SPEC, *[HBM_SPEC] * (2 * n), pl.BlockSpec(memory_space=pltpu.VMEM)),
        input_output_aliases={i: 2 + i for i in range(2 * n)},
        compiler_params=pltpu.CompilerParams(has_side_effects=EFFECT),
    )(*[pltpu.with_memory_space_constraint(a, pltpu.HBM) for a in srcs + lands], *after)
    return (outs[0], outs[1], list(outs[2:2 + n]), list(outs[2 + n:2 + 2 * n]), whole, dests), outs[-1]


def _exchange_wait(handle, after, name):
    send_sems, recv_sems, srcs, lands, whole, dests = handle
    n = len(srcs)
    everyone = dests == (0, N_DEV)

    def body(*refs):
        src_refs, land_refs = refs[:n], refs[n:2 * n]
        send_sems, recv_sems = refs[2 * n], refs[2 * n + 1]
        for i in range(n):
            for k in range(1, N_DEV):
                send, recv, sends, receives = _exchange_copy(src_refs, land_refs, send_sems, recv_sems, whole, dests, i, k)
                if everyone:
                    send.wait_send()
                    recv.wait_recv()
                else:
                    pl.when(sends)(send.wait_send)
                    pl.when(receives)(recv.wait_recv)

    outs = pl.pallas_call(
        body, name=name, out_shape=tuple(pltpu.HBM(a.shape, a.dtype) for a in srcs + lands),
        in_specs=[HBM_SPEC] * (2 * n) + [SEM_SPEC, SEM_SPEC, pl.BlockSpec(memory_space=pl.ANY)],
        out_specs=[HBM_SPEC] * (2 * n), input_output_aliases={i: i for i in range(2 * n)},
        compiler_params=pltpu.CompilerParams(has_side_effects=EFFECT),
    )(*srcs, *lands, send_sems, recv_sems, after)
    return list(outs[n:])


def _tie(v, token):
    return v + token[0:1, 0:1].astype(v.dtype).reshape((1,) * v.ndim)


def _with_own(land, own, me):
    return lax.dynamic_update_slice_in_dim(land, own[None].astype(land.dtype), me, 0)


class _Overlap:
    def __init__(self, shards, me, after):
        self.me = me
        self.names = list(shards)
        self.handle, self.token = _exchange_start([shards[nm] for nm in self.names], True, "weights_start", after)
        self.sent = {}

    def weights(self, after):
        lands = _exchange_wait(self.handle, after, "weights_wait")
        own = self.handle[2]
        return {nm: _full_from_blocks(nm, _with_own(land, o, self.me)) for nm, land, o in zip(self.names, lands, own)}

    def send(self, tag, grads, after=None):
        names = list(grads)
        handle, token = _exchange_start([_blocks_from_full(nm, grads[nm]) for nm in names], False, f"grads_start_{tag}",
                                        after)
        self.sent[tag] = (names, handle)
        return token

    def send_rows(self, tag, rows, dests):
        lo, hi = dests
        blocks = rows.reshape(hi - lo, rows.shape[0] // (hi - lo), rows.shape[1])
        handle, token = _exchange_start([blocks], False, f"grads_start_{tag}", None, dests)
        self.sent[tag] = ([tag], handle)
        return token

    def received(self, tag, after):
        names, handle = self.sent[tag]
        lands = _exchange_wait(handle, after, f"grads_wait_{tag}")
        lo = handle[5][0]
        own = [lax.dynamic_index_in_dim(b, self.me - lo, 0, keepdims=False) for b in handle[2]]
        return {nm: _with_own(land, o, self.me) for nm, land, o in zip(names, lands, own)}


ADAM_ROWS = 32


def _adamw(w, g, m, v, name):
    R, C = w.shape
    cb = LANES if C % LANES == 0 else C

    def body(w_ref, g_ref, m_ref, v_ref, d_ref, m2_ref, v2_ref):
        def update(sl):
            gv = g_ref[sl, :]
            m2 = ADAM_B1 * m_ref[sl, :] + (1.0 - ADAM_B1) * gv
            v2 = ADAM_B2 * v_ref[sl, :] + (1.0 - ADAM_B2) * jnp.square(gv)
            m_hat = m2 / (1.0 - ADAM_B1 ** ADAM_STEP)
            v_hat = v2 / (1.0 - ADAM_B2 ** ADAM_STEP)
            d_ref[sl, :] = -ADAM_LR * (m_hat / (jnp.sqrt(v_hat) + ADAM_EPS) + ADAM_WD * w_ref[sl, :])
            m2_ref[sl, :] = m2
            v2_ref[sl, :] = v2

        main = R // ADAM_ROWS
        if main:
            lax.fori_loop(0, main, lambda i, c: (update(pl.ds(pl.multiple_of(i * ADAM_ROWS, ADAM_ROWS), ADAM_ROWS)), c)[1], 0)
        if R % ADAM_ROWS:
            update(pl.ds(main * ADAM_ROWS, R % ADAM_ROWS))

    spec = pl.BlockSpec((R, cb), lambda j: (0, j))
    return pl.pallas_call(
        body, name=name, grid=(C // cb,), in_specs=[spec] * 4, out_specs=[spec] * 3, out_shape=[_sds((R, C))] * 3,
        compiler_params=_params(("parallel",)),
    )(w, g, m, v)


BIG_PARAMS = ("hy_w_in", "hy_w_out", "cv_w_pw1", "cv_w_pw2", "ffn_w_gate", "ffn_w_up", "ffn_w_down")


def _shards_2d(w):
    t = lambda a: jnp.transpose(a)
    return dict(in_t=t(w["hy_w_in"][0]), out=w["hy_w_out"][0], pw1=w["cv_w_pw1"][0], pw2=w["cv_w_pw2"][0],
                gate_t0=t(w["ffn_w_gate"][0]), gate_t1=t(w["ffn_w_gate"][1]), up_t0=t(w["ffn_w_up"][0]),
                up_t1=t(w["ffn_w_up"][1]), down0=w["ffn_w_down"][0], down1=w["ffn_w_down"][1])


def _unshard_2d(s):
    t = lambda a: jnp.transpose(a)
    return dict(hy_w_in=t(s["in_t"])[None], hy_w_out=s["out"][None], cv_w_pw1=s["pw1"][None], cv_w_pw2=s["pw2"][None],
                ffn_w_gate=jnp.stack([t(s["gate_t0"]), t(s["gate_t1"])]),
                ffn_w_up=jnp.stack([t(s["up_t0"]), t(s["up_t1"])]), ffn_w_down=jnp.stack([s["down0"], s["down1"]]))


def _full_from_blocks(nm, g):
    if nm == "pw1":
        return jnp.transpose(g, (1, 0, 2)).reshape(g.shape[1], N_DEV * g.shape[2])
    return g.reshape(N_DEV * g.shape[1], g.shape[2])


def _blocks_from_full(nm, g):
    if nm == "pw1":
        return jnp.transpose(g.reshape(g.shape[0], N_DEV, g.shape[1] // N_DEV), (1, 0, 2))
    return g.reshape(N_DEV, g.shape[0] // N_DEV, g.shape[1])


class _VecPack:
    def __init__(self, shapes):
        self.shapes = [tuple(s) for s in shapes]
        self.sizes = [int(np.prod(s)) for s in self.shapes]
        total = sum(self.sizes)
        self.rows = -(-(-(-total // LANES)) // 8) * 8
        self.total = total

    def pack(self, arrays):
        flat = jnp.concatenate([a.astype(f32).reshape(-1) for a in arrays])
        flat = jnp.pad(flat, (0, self.rows * LANES - self.total))
        return flat.reshape(self.rows, LANES)

    def unpack(self, packed):
        flat = packed.reshape(-1)
        out, off = [], 0
        for shp, n in zip(self.shapes, self.sizes):
            out.append(flat[off:off + n].reshape(shp))
            off += n
        return out

    def unpack_stacked(self, stacked, only=None):
        flat = stacked.reshape(stacked.shape[0], -1)
        offs = np.concatenate([[0], np.cumsum(self.sizes)])
        get = lambda i: flat[:, offs[i]:offs[i + 1]].reshape((stacked.shape[0],) + self.shapes[i])
        return get(only) if only is not None else [get(i) for i in range(len(self.shapes))]


def _row(v):
    return v.reshape(1, -1)


def _pad_lanes(v):
    v = v.reshape(1, -1)
    return jnp.pad(v, ((0, 0), (0, LANES - v.shape[1])))


def _ffn_fwd(h, w_gate_t, w_up_t, w_down, tag):
    F = w_down.shape[0]
    a = _mm(h, w_gate_t, tb=True, out_dtype=bf16, name=f"ffn_gate_{tag}")
    u = _mm(h, w_up_t, tb=True, out_dtype=bf16, name=f"ffn_up_{tag}")
    (f,), _ = _rowwise(f"swiglu_{tag}", lambda rv, vv: ([_silu(rv[0].astype(f32)) * rv[1].astype(f32)], []), [a, u], [],
                       [(F, bf16)], [], sub=16, col_chunk=_tile(F, 512, LANES))
    out = _mm(f, w_down, name=f"ffn_down_{tag}")
    return out, (a, u, f)


def _ffn_bwd(h, w_gate_t, w_up_t, w_down, saved, dout, tag):
    a, u, f = saved
    F = w_down.shape[0]
    df = _mm(dout, w_down, tb=True, out_dtype=bf16, name=f"ffn_down_dx_{tag}")
    dw_down = _mm(f, dout, ta=True, out_dtype=bf16, name=f"ffn_down_dw_{tag}")

    def fn(rv, vv):
        _, vjp = jax.vjp(lambda a_, u_: _silu(a_) * u_, rv[0].astype(f32), rv[1].astype(f32))
        da, du = vjp(rv[2].astype(f32))
        return [da, du], []

    (da, du), _ = _rowwise(f"swiglu_bwd_{tag}", fn, [a, u, df], [], [(F, bf16), (F, bf16)], [], sub=16,
                           col_chunk=_tile(F, 512, LANES))
    dh = _mm(du, w_up_t, add=_mm(da, w_gate_t, name=f"ffn_gate_dx_{tag}"), name=f"ffn_up_dx_{tag}")
    dw_gate_t = _mm(da, h, ta=True, out_dtype=bf16, name=f"ffn_gate_dw_{tag}")
    dw_up_t = _mm(du, h, ta=True, out_dtype=bf16, name=f"ffn_up_dw_{tag}")
    return dh, dw_gate_t, dw_up_t, dw_down


def _local_step(x, target, mod, w_in_t, comm, small):
    S, D = x.shape
    di = small["hy_ssm_norm_g"].shape[-1]
    nh = small["hy_dt_bias"].shape[-1]
    cd = small["hy_conv_b"].shape[-1]
    m = [[_row(mod[i, j]) for j in range(6)] for i in range(2)]

    off_q = di + cd + nh
    w_qkv_t = w_in_t[off_q:]
    seg = dict(z=(w_in_t, 0, di), xbc=(w_in_t, di, cd), dt=(w_in_t, di + cd, LANES))
    for i, nm in enumerate(("q0", "q1", "q2", "k", "v")):
        seg[nm] = (w_qkv_t, i * D, D)

    g_mix = [_row(small["norm_mix_g"][i]) for i in range(2)]
    g_ffn = [_row(small["norm_ffn_g"][i]) for i in range(2)]
    conv_w, conv_b = small["hy_conv_w_full"], _row(small["hy_conv_b"][0])
    dt_bias, a_log, d_skip = (_pad_lanes(small[k][0]) for k in ("hy_dt_bias", "hy_a_log", "hy_d_skip"))
    g_ssm = _row(small["hy_ssm_norm_g"][0])
    onehot = jnp.asarray(_bucket_onehot())
    rel_t = small["rel_table"].T
    H = D // HEAD_DIM
    bias = [_exact_mm(rel_t[gi * H:(gi + 1) * H], onehot[gi], name=f"rel_bias_{gi}")
            .reshape(H, 2 * ATT_BLK, ATT_BLK) + _band_penalty() for gi in range(3)]

    h1 = _adaln_fwd(x, g_mix[0], m[0][1], m[0][0], "adaln_mix0")
    proj = {nm: _mm(h1, mat, tb=True, b_rows=(off, cnt), name=f"in_{nm}") for nm, (mat, off, cnt) in seg.items()}
    xbc_pre, xbc = _conv_fwd(proj["xbc"], conv_w, conv_b, silu=True, name="ssm_conv", tr=1024)
    y, hin = _ssd_fwd(xbc, proj["dt"], dt_bias, a_log, d_skip, di, "ssd_fwd")
    (yg,), _ = _rowwise("ssm_gate", lambda rv, vv: ([_gate_f(rv[0], rv[1], vv[0])], []),
                        [y, proj["z"]], [g_ssm], [(di, bf16)], [], sub=16)
    og = [_attn_fwd(proj[f"q{gi}"], proj["k"], proj["v"], bias[gi], d, f"attn_fwd_{gi}")
          for gi, d in enumerate(ATT_DILATIONS)]
    att, att_b, lse_tot = _attn_combine([a for a, _ in og], [_from_blocks(b, LANES) for _, b in og], "attn_combine")
    W = comm.weights(after=att_b)
    w_out_y, w_out_a = W["out"][:di], W["out"][di:]
    mix0 = _mm(att_b, w_out_a, add=_mm(yg, w_out_y, name="out_y"), name="out_a")
    x1 = _resid_fwd(x, m[0][2], mix0, "resid_mix0")
    h2 = _adaln_fwd(x1, g_ffn[0], m[0][4], m[0][3], "adaln_ffn0")
    f0, ffn0_saved = _ffn_fwd(h2, W["gate_t0"], W["up_t0"], W["down0"], "0")
    x2 = _resid_fwd(x1, m[0][5], f0, "resid_ffn0")

    h3 = _adaln_fwd(x2, g_mix[1], m[1][1], m[1][0], "adaln_mix1")
    pw1 = _mm(h3, W["pw1"], bias=_row(small["cv_b_pw1_full"]), name="cv_pw1")
    (u,), _ = _rowwise("cv_glu", lambda rv, vv: ([rv[0] * jax.nn.sigmoid(rv[1])], []),
                       [(pw1, 0, D), (pw1, 1, D)], [], [(D, f32)], [])
    (u2,) = _conv_fwd(u, small["cv_w_dw_full"], _row(small["cv_b_dw_full"]), silu=False, name="cv_dw")
    ln_g, ln_b = _row(small["cv_ln_g_full"]), _row(small["cv_ln_b_full"])
    (u3,), _ = _rowwise("cv_lnsilu", lambda rv, vv: ([_lnsilu_f(rv[0], vv[0], vv[1])], []),
                        [u2], [ln_g, ln_b], [(D, bf16)], [], sub=16)
    mix1 = _mm(u3, W["pw2"], bias=_row(small["cv_b_pw2_full"]), name="cv_pw2")
    x3 = _resid_fwd(x2, m[1][2], mix1, "resid_mix1")
    h4 = _adaln_fwd(x3, g_ffn[1], m[1][4], m[1][3], "adaln_ffn1")
    f1, ffn1_saved = _ffn_fwd(h4, W["gate_t1"], W["up_t1"], W["down1"], "1")
    x4 = _resid_fwd(x3, m[1][5], f1, "resid_ffn1")

    g_fin = _row(small["final_norm_g"])

    def final_fn(rv, vv):
        xv, tv = rv
        yv, vjp = jax.vjp(_rms, xv, vv[0])
        err = yv - tv
        dx, dg = vjp(err / D)
        part = 0.5 * jnp.sum(jnp.mean(err * err, -1, keepdims=True), 0, keepdims=True)
        return [dx], [dg, jnp.broadcast_to(part, (1, LANES))]

    (dx4,), (d_fin, loss) = _rowwise("loss_head", final_fn, [x4, target], [g_fin], [(D, f32)], [D, LANES])

    dmod = [[None] * 6 for _ in range(2)]
    d_norm_mix, d_norm_ffn = [None, None], [None, None]
    big = {}

    df1, (dmod[1][5], _) = _resid_bwd(dx4, f1, m[1][5], "resid_ffn1_bwd")
    dh4, big["gate_t1"], big["up_t1"], big["down1"] = _ffn_bwd(h4, W["gate_t1"], W["up_t1"], W["down1"], ffn1_saved, df1, "1")
    dx3, (d_norm_ffn[1], dmod[1][4], dmod[1][3]) = _adaln_bwd(x3, g_ffn[1], m[1][4], m[1][3], dh4, dx4, "adaln_ffn1_bwd")
    dmix1, (dmod[1][2], d_b_pw2) = _resid_bwd(dx3, mix1, m[1][2], "resid_mix1_bwd")
    du3 = _mm(dmix1, W["pw2"], tb=True, name="cv_pw2_dx")
    big["pw2"] = _mm(u3, dmix1, ta=True, out_dtype=bf16, name="cv_pw2_dw")

    def lnsilu_bwd(rv, vv):
        _, vjp = jax.vjp(_lnsilu_f, rv[0], vv[0], vv[1])
        du, dg, db = vjp(rv[1])
        return [du], [dg, db]

    (du2,), (d_ln_g, d_ln_b) = _rowwise("cv_lnsilu_bwd", lnsilu_bwd, [u2, du3], [ln_g, ln_b], [(D, f32)], [D, D])
    du, d_w_dw, d_b_dw = _conv_bwd(u, small["cv_w_dw_full"], du2, None, silu=False, name="cv_dw_bwd")

    def glu_bwd(rv, vv):
        a, gt, d = rv
        _, vjp = jax.vjp(lambda a_, g_: a_ * jax.nn.sigmoid(g_), a, gt)
        da, dg = vjp(d)
        return [da, dg], [jnp.sum(da, 0, keepdims=True), jnp.sum(dg, 0, keepdims=True)]

    (dpa, dpg), (d_b1a, d_b1g) = _rowwise("cv_glu_bwd", glu_bwd, [(pw1, 0, D), (pw1, 1, D), du], [],
                                           [(D, bf16), (D, bf16)], [D, D], sub=16)
    dpw1 = jnp.concatenate([dpa, dpg], axis=1)
    d_b_pw1 = jnp.concatenate([d_b1a, d_b1g], axis=1)
    dh3 = _mm(dpw1, W["pw1"], tb=True, name="cv_pw1_dx")
    big["pw1"] = _mm(h3, dpw1, ta=True, out_dtype=bf16, name="cv_pw1_dw")
    token = comm.send("layer1", {nm: big[nm] for nm in ("gate_t1", "up_t1", "down1", "pw2", "pw1")})
    dx2, (d_norm_mix[1], dmod[1][1], dmod[1][0]) = _adaln_bwd(x2, g_mix[1], m[1][1], _tie(m[1][0], token), dh3, dx3,
                                                              "adaln_mix1_bwd")

    df0, (dmod[0][5], _) = _resid_bwd(dx2, f0, m[0][5], "resid_ffn0_bwd")
    dh2, big["gate_t0"], big["up_t0"], big["down0"] = _ffn_bwd(h2, W["gate_t0"], W["up_t0"], W["down0"], ffn0_saved, df0, "0")
    dx1, (d_norm_ffn[0], dmod[0][4], dmod[0][3]) = _adaln_bwd(x1, g_ffn[0], m[0][4], m[0][3], dh2, dx2, "adaln_ffn0_bwd")
    dmix0, (dmod[0][2], _) = _resid_bwd(dx1, mix0, m[0][2], "resid_mix0_bwd")
    dyg = _mm(dmix0, w_out_y, tb=True, name="out_y_dx")
    datt = _mm(dmix0, w_out_a, tb=True, name="out_a_dx")
    big["out"] = jnp.concatenate([_mm(yg, dmix0, ta=True, out_dtype=bf16, name="out_y_dw"),
                                  _mm(att_b, dmix0, ta=True, out_dtype=bf16, name="out_a_dw")], axis=0)
    token = comm.send("layer0", {nm: big[nm] for nm in ("gate_t0", "up_t0", "down0", "out")})
    g_ssm = _tie(g_ssm, token)

    def gate_bwd(rv, vv):
        _, vjp = jax.vjp(_gate_f, rv[0], rv[1], vv[0])
        dy_, dz_, dg_ = vjp(rv[2])
        return [dy_, dz_], [dg_]

    (dy, dz), (d_g_ssm,) = _rowwise("ssm_gate_bwd", gate_bwd, [y, proj["z"], dyg], [g_ssm], [(di, f32), (di, bf16)], [di],
                                    sub=16)
    dxbc, ddtraw, d_a_log, d_dskip, d_dt_bias = _ssd_bwd(xbc, proj["dt"], dt_bias, a_log, d_skip, hin, y, dy, di, "ssd_bwd")
    dxbc_pre, d_conv_w, d_conv_b = _conv_bwd(proj["xbc"], conv_w, dxbc, xbc_pre, silu=True, name="ssm_conv_bwd",
                                             dx_dtype=bf16, tr=1024)
    dh1 = None
    early = []
    for nm, dseg in (("z", dz), ("xbc", dxbc_pre), ("dt", ddtraw)):
        mat, off, cnt = seg[nm]
        dh1 = _mm(dseg, mat, b_rows=(off, cnt), add=dh1, name=f"in_{nm}_dx")
        dwp = _mm(dseg, h1, ta=True, out_dtype=bf16, name=f"in_{nm}_dw")
        early.append(dwp[:nh] if nm == "dt" else dwp)
    early = jnp.concatenate(early, axis=0)
    shard_rows = w_in_t.shape[0] // N_DEV
    n_early = off_q // shard_rows
    token = comm.send_rows("in_early", early[:n_early * shard_rows], (0, n_early))
    bias = [_tie(b, token) for b in bias]

    dq, dks, dvs, dbs = [], [], [], []
    lse_heads = lse_tot[:, :H]
    dsum_heads = _head_sums(att, datt, "attn_dsum")[:, :H]
    for gi, d in enumerate(ATT_DILATIONS):
        a, b, c_, e = _attn_bwd(proj[f"q{gi}"], proj["k"], proj["v"], bias[gi], datt,
                                _by_block(lse_heads, d), _by_block(dsum_heads, d), d, f"attn_bwd_{gi}")
        dq.append(a)
        dks.append(b)
        dvs.append(c_)
        dbs.append(e)
    dk = _add3(*dks, "attn_dk")
    dv = _add3(*dvs, "attn_dv")
    d_rel = jnp.concatenate(
        [_exact_mm(dbs[gi].reshape(H, -1), onehot[gi], tb=True, name=f"rel_grad_{gi}") for gi in range(3)], axis=0).T

    late = [early[n_early * shard_rows:]]
    for nm, dseg in (("q0", dq[0]), ("q1", dq[1]), ("q2", dq[2]), ("k", dk), ("v", dv)):
        mat, off, cnt = seg[nm]
        dh1 = _mm(dseg, mat, b_rows=(off, cnt), add=dh1, name=f"in_{nm}_dx")
        late.append(_mm(dseg, h1, ta=True, out_dtype=bf16, name=f"in_{nm}_dw"))
    late = jnp.concatenate(late, axis=0)
    dx0, (d_norm_mix[0], dmod[0][1], dmod[0][0]) = _adaln_bwd(x, g_mix[0], m[0][1], m[0][0], dh1, dx1, "adaln_mix0_bwd")

    smallg = dict(
        loss=loss, dmod=jnp.stack([jnp.concatenate(dmod[i], axis=1)[0] for i in range(2)]),
        norm_mix_g=jnp.concatenate(d_norm_mix, axis=0), norm_ffn_g=jnp.concatenate(d_norm_ffn, axis=0),
        hy_conv_w=d_conv_w, hy_conv_b=d_conv_b, hy_dt_bias=d_dt_bias[:, :nh], hy_a_log=d_a_log[:, :nh],
        hy_d_skip=d_dskip[:, :nh], hy_ssm_norm_g=d_g_ssm, rel_table=d_rel,
        cv_b_pw1=d_b_pw1, cv_w_dw=d_w_dw, cv_b_dw=d_b_dw, cv_ln_g=d_ln_g, cv_ln_b=d_ln_b, cv_b_pw2=d_b_pw2,
        final_norm_g=d_fin)
    return dx0, (late, n_early), smallg


SMALL_GRAD_ORDER = ("loss", "dmod", "norm_mix_g", "norm_ffn_g", "hy_conv_w", "hy_conv_b", "hy_dt_bias", "hy_a_log",
                    "hy_d_skip", "hy_ssm_norm_g", "rel_table", "cv_b_pw1", "cv_w_dw", "cv_b_dw", "cv_ln_g", "cv_ln_b",
                    "cv_b_pw2", "final_norm_g")


def kernel(x, c, ada_w, ada_b, norm_mix_g, norm_ffn_g, hy_w_in, hy_conv_w, hy_conv_b, hy_dt_bias, hy_a_log, hy_d_skip, hy_ssm_norm_g, hy_w_out, rel_table, cv_w_pw1, cv_b_pw1, cv_w_dw, cv_b_dw, cv_ln_g, cv_ln_b, cv_w_pw2, cv_b_pw2, ffn_w_gate, ffn_w_up, ffn_w_down, final_norm_g, loss_target, m_ada_w, m_ada_b, m_norm_mix_g, m_norm_ffn_g, m_hy_w_in, m_hy_conv_w, m_hy_conv_b, m_hy_dt_bias, m_hy_a_log, m_hy_d_skip, m_hy_ssm_norm_g, m_hy_w_out, m_rel_table, m_cv_w_pw1, m_cv_b_pw1, m_cv_w_dw, m_cv_b_dw, m_cv_ln_g, m_cv_ln_b, m_cv_w_pw2, m_cv_b_pw2, m_ffn_w_gate, m_ffn_w_up, m_ffn_w_down, m_final_norm_g, v_ada_w, v_ada_b, v_norm_mix_g, v_norm_ffn_g, v_hy_w_in, v_hy_conv_w, v_hy_conv_b, v_hy_dt_bias, v_hy_a_log, v_hy_d_skip, v_hy_ssm_norm_g, v_hy_w_out, v_rel_table, v_cv_w_pw1, v_cv_b_pw1, v_cv_w_dw, v_cv_b_dw, v_cv_ln_g, v_cv_ln_b, v_cv_w_pw2, v_cv_b_pw2, v_ffn_w_gate, v_ffn_w_up, v_ffn_w_down, v_final_norm_g):
    names = ("ada_w", "ada_b", "norm_mix_g", "norm_ffn_g", "hy_w_in", "hy_conv_w", "hy_conv_b", "hy_dt_bias", "hy_a_log",
             "hy_d_skip", "hy_ssm_norm_g", "hy_w_out", "rel_table", "cv_w_pw1", "cv_b_pw1", "cv_w_dw", "cv_b_dw", "cv_ln_g",
             "cv_ln_b", "cv_w_pw2", "cv_b_pw2", "ffn_w_gate", "ffn_w_up", "ffn_w_down", "final_norm_g")
    w = dict(zip(names, (ada_w, ada_b, norm_mix_g, norm_ffn_g, hy_w_in, hy_conv_w, hy_conv_b, hy_dt_bias, hy_a_log, hy_d_skip,
                         hy_ssm_norm_g, hy_w_out, rel_table, cv_w_pw1, cv_b_pw1, cv_w_dw, cv_b_dw, cv_ln_g, cv_ln_b, cv_w_pw2,
                         cv_b_pw2, ffn_w_gate, ffn_w_up, ffn_w_down, final_norm_g)))
    mom = dict(zip(names, (m_ada_w, m_ada_b, m_norm_mix_g, m_norm_ffn_g, m_hy_w_in, m_hy_conv_w, m_hy_conv_b, m_hy_dt_bias,
                           m_hy_a_log, m_hy_d_skip, m_hy_ssm_norm_g, m_hy_w_out, m_rel_table, m_cv_w_pw1, m_cv_b_pw1, m_cv_w_dw,
                           m_cv_b_dw, m_cv_ln_g, m_cv_ln_b, m_cv_w_pw2, m_cv_b_pw2, m_ffn_w_gate, m_ffn_w_up, m_ffn_w_down,
                           m_final_norm_g)))
    vel = dict(zip(names, (v_ada_w, v_ada_b, v_norm_mix_g, v_norm_ffn_g, v_hy_w_in, v_hy_conv_w, v_hy_conv_b, v_hy_dt_bias,
                           v_hy_a_log, v_hy_d_skip, v_hy_ssm_norm_g, v_hy_w_out, v_rel_table, v_cv_w_pw1, v_cv_b_pw1, v_cv_w_dw,
                           v_cv_b_dw, v_cv_ln_g, v_cv_ln_b, v_cv_w_pw2, v_cv_b_pw2, v_ffn_w_gate, v_ffn_w_up, v_ffn_w_down,
                           v_final_norm_g)))
    S, D = x.shape[1], x.shape[2]
    ax, ay, ac = lax.axis_index("x"), lax.axis_index("y"), lax.axis_index("c")
    me = 4 * ax + 2 * ay + ac
    c_arr = jnp.reshape(ac, (1,)).astype(jnp.int32)
    nmod = ada_w.shape[2]

    w2 = _shards_2d(w)
    big_names = list(w2)
    (g_in,) = _all_gather([w2["in_t"].astype(bf16)], "gather_w_in")
    w_in_t = _full_from_blocks("in_t", g_in)

    sharded_small = ("hy_conv_w", "cv_b_pw1", "cv_w_dw", "cv_b_dw", "cv_ln_g", "cv_ln_b", "cv_b_pw2")
    vp = _VecPack([c.shape] + [w[nm].shape for nm in sharded_small])
    (sg,) = _all_gather([vp.pack([c] + [w[nm] for nm in sharded_small])], "gather_vectors")
    parts = vp.unpack_stacked(sg)
    c_all = parts[0][:, 0]
    small = {k: w[k] for k in ("norm_mix_g", "norm_ffn_g", "hy_conv_b", "hy_dt_bias", "hy_a_log", "hy_d_skip",
                               "hy_ssm_norm_g", "rel_table", "final_norm_g")}
    for p, nm in zip(parts[1:], sharded_small):
        p = p[:, 0]
        p = jnp.moveaxis(p, 0, -2)
        small[nm + "_full"] = p.reshape(p.shape[:-2] + (N_DEV * p.shape[-1],))

    (cs_all,), _ = _rowwise("ada_silu", lambda rv, vv: ([_silu(rv[0])], []), [c_all], [], [(D, f32)], [])
    b_mine = lax.dynamic_slice_in_dim(ada_b, me * nmod, nmod, axis=1)
    mod_part = jnp.stack([_mm(cs_all, ada_w[i], bias=b_mine[i:i + 1], name=f"ada_mod_{i}") for i in range(2)])
    (mod_all,) = _all_gather([mod_part.reshape(2 * N_DEV, nmod)], "gather_mod")
    mod_all = mod_all.reshape(N_DEV, 2, N_DEV, nmod)
    mod_mine = lax.dynamic_index_in_dim(mod_all, me, axis=2, keepdims=False)
    mod = jnp.transpose(mod_mine, (1, 0, 2)).reshape(2, 6, D)
    comm = _Overlap({nm: w2[nm].astype(bf16) for nm in big_names if nm != "in_t"}, me, after=[mod, w_in_t])
    mod = _tie(mod, comm.token)

    dx0, (d_in_late, n_early), sgrad = _local_step(x[0], loss_target[0], mod, w_in_t, comm, small)
    comm.send_rows("in_late", d_in_late, (n_early, N_DEV))

    gp = _VecPack([sgrad[k].shape for k in SMALL_GRAD_ORDER])
    (g_all,) = _all_gather([gp.pack([sgrad[k] for k in SMALL_GRAD_ORDER])], "gather_small_grads")
    tot = dict(zip(SMALL_GRAD_ORDER, gp.unpack(_sum_slots(g_all, "sum_small_grads"))))
    dmod_all = gp.unpack_stacked(g_all, only=SMALL_GRAD_ORDER.index("dmod"))
    loss = tot["loss"][0, 0]

    grads = {}
    dmod_mine = lax.dynamic_slice_in_dim(dmod_all, me * nmod, nmod, axis=2)
    grads["ada_w"] = jnp.stack([_mm(cs_all, dmod_mine[:, i], ta=True, name=f"ada_w_grad_{i}") for i in range(2)])
    grads["ada_b"] = tot["dmod"]
    grads["norm_mix_g"], grads["norm_ffn_g"] = tot["norm_mix_g"], tot["norm_ffn_g"]
    grads["hy_conv_b"] = tot["hy_conv_b"]
    grads["hy_dt_bias"] = tot["hy_dt_bias"]
    grads["hy_a_log"] = tot["hy_a_log"]
    grads["hy_d_skip"] = tot["hy_d_skip"]
    grads["hy_ssm_norm_g"] = tot["hy_ssm_norm_g"]
    grads["rel_table"] = tot["rel_table"]
    grads["final_norm_g"] = tot["final_norm_g"][0]
    for nm in sharded_small:
        n = w[nm].shape[-1]
        grads[nm] = lax.dynamic_slice_in_dim(tot[nm], me * n, n, axis=1).reshape(w[nm].shape)

    delta, new_m, new_v = {}, {}, {}
    shp = ada_w.shape
    two = lambda t: t.reshape(-1, shp[-1])
    d_, m_, v_ = _adamw(two(ada_w), two(grads["ada_w"]), two(m_ada_w), two(v_ada_w), "adamw_ada_w")
    delta["ada_w"], new_m["ada_w"], new_v["ada_w"] = d_.reshape(shp), m_.reshape(shp), v_.reshape(shp)
    rest = [nm for nm in names if nm not in BIG_PARAMS and nm != "ada_w"]
    sp = _VecPack([w[nm].shape for nm in rest])
    packs = [sp.pack([t[nm] for nm in rest]) for t in (w, grads, mom, vel)]
    ds_, ms_, vs_ = _adamw(*packs, "adamw_small")
    for nm, a, b, e in zip(rest, sp.unpack(ds_), sp.unpack(ms_), sp.unpack(vs_)):
        delta[nm], new_m[nm], new_v[nm] = a, b, e

    g2 = {}
    after = d_
    for tag in ("layer1", "layer0", "in_early", "in_late"):
        for nm, slots in comm.received(tag, after).items():
            g2[nm] = _sum_slots(slots, f"sum_{nm}")
            after = g2[nm]
    g2["in_t"] = jnp.where(me < n_early, g2.pop("in_early"), g2.pop("in_late"))
    grads.update(_unshard_2d(g2))
    m2, v2 = _shards_2d(mom), _shards_2d(vel)
    d2, nm2, nv2 = {}, {}, {}
    for nm in big_names:
        d2[nm], nm2[nm], nv2[nm] = _adamw(w2[nm], g2[nm], m2[nm], v2[nm], f"adamw_{nm}")
    delta.update(_unshard_2d(d2))
    new_m.update(_unshard_2d(nm2))
    new_v.update(_unshard_2d(nv2))

    return (loss, dx0[None], *[grads[n] for n in names], *[delta[n] for n in names],
            *[new_m[n] for n in names], *[new_v[n] for n in names])
```

```python
import functools
import math

import numpy as np
import jax
import jax.numpy as jnp
from jax import lax
from jax.experimental import pallas as pl
from jax.experimental.pallas import tpu as pltpu

f32 = jnp.float32
bf16 = jnp.bfloat16
EPS = 1e-6
N_DEV = 8
LANES = 128
SSM_STATE = 128
SSM_CHUNK = 128
SSM_GROUPS = 4
HEAD_DIM = 64
ATT_BLK = 128
ATT_DILATIONS = (1, 4, 16)
REL_BUCKETS = 32
REL_MAX_DIST = 2048
ADAM_LR, ADAM_B1, ADAM_B2, ADAM_EPS, ADAM_WD, ADAM_STEP = 0.001, 0.9, 0.999, 1e-08, 0.01, 10
PACK_ROW_TILE = 256
MESH = pl.DeviceIdType.MESH
VMEM_LIMIT = 48 * 1024 * 1024


def _sds(shape, dtype=f32):
    return jax.ShapeDtypeStruct(tuple(shape), dtype)


def _tile(n, cap, mult):
    best = None
    t = mult
    while t <= min(n, cap):
        if n % t == 0:
            best = t
        t += mult
    return best if best is not None else n


def _params(sem):
    return pltpu.CompilerParams(dimension_semantics=sem, vmem_limit_bytes=VMEM_LIMIT)


def _mm(a, b, *, name, ta=False, tb=False, b_rows=None, bias=None, add=None, out_dtype=f32,
        tm_cap=512, tn_cap=1536, tk_cap=8192):
    if ta:
        K, M = a.shape
    else:
        M, K = a.shape
    off, cnt = b_rows if b_rows is not None else (0, b.shape[0])
    if tb:
        N, K2 = cnt, b.shape[1]
    else:
        K2, N = cnt, b.shape[1]
    assert K == K2, (a.shape, b.shape, ta, tb, b_rows)
    if ta and a.dtype == f32:
        tm_cap = min(tm_cap, 256)
    tm = _tile(M, tm_cap, LANES)
    tn = _tile(math.gcd(off, N) if tb else N, tn_cap, LANES)
    tk = _tile(K if tb else math.gcd(off, K), tk_cap, LANES)
    assert N % tn == 0 and K % tk == 0 and off % (tn if tb else tk) == 0, (name, off, N, K, tn, tk)
    nk = K // tk
    jo, ko = (off // tn, 0) if tb else (0, off // tk)
    has_bias, has_add = bias is not None, add is not None
    dn = (((0 if ta else 1,), (1 if tb else 0,)), ((), ()))

    def body(*refs):
        a_ref, b_ref = refs[0], refs[1]
        pos = 2
        bias_ref = add_ref = None
        if has_bias:
            bias_ref = refs[pos]
            pos += 1
        if has_add:
            add_ref = refs[pos]
            pos += 1
        o_ref = refs[pos]
        k = pl.program_id(2)
        part = lax.dot_general(a_ref[...].astype(bf16), b_ref[...].astype(bf16), dn, preferred_element_type=f32)

        def finish(r):
            if has_bias:
                r = r + bias_ref[...]
            if has_add:
                r = r + add_ref[...]
            o_ref[...] = r.astype(o_ref.dtype)

        if nk == 1:
            finish(part)
        else:
            acc_ref = refs[pos + 1]

            @pl.when(k == 0)
            def _():
                acc_ref[...] = part

            @pl.when((k > 0) & (k < nk - 1))
            def _():
                acc_ref[...] += part

            @pl.when(k == nk - 1)
            def _():
                finish(acc_ref[...] + part)

    in_specs = [
        pl.BlockSpec((tk, tm), lambda i, j, k: (k, i)) if ta else pl.BlockSpec((tm, tk), lambda i, j, k: (i, k)),
        pl.BlockSpec((tn, tk), lambda i, j, k: (j + jo, k)) if tb else pl.BlockSpec((tk, tn), lambda i, j, k: (k + ko, j)),
    ]
    args = [a, b]
    if has_bias:
        in_specs.append(pl.BlockSpec((1, tn), lambda i, j, k: (0, j)))
        args.append(bias)
    if has_add:
        in_specs.append(pl.BlockSpec((tm, tn), lambda i, j, k: (i, j)))
        args.append(add)
    return pl.pallas_call(
        body, name=name, grid=(M // tm, N // tn, nk), in_specs=in_specs,
        out_specs=pl.BlockSpec((tm, tn), lambda i, j, k: (i, j)), out_shape=_sds((M, N), out_dtype),
        scratch_shapes=[pltpu.VMEM((tm, tn), f32)] if nk > 1 else [],
        compiler_params=_params(("parallel", "parallel", "arbitrary")),
    )(*args)


def _rowwise(name, fn, rows, vecs, out_rows, out_accs, *, tr_cap=256, sub=8, col_chunk=None):
    rows = [r if isinstance(r, tuple) else (r, 0, r.shape[1]) for r in rows]
    R = rows[0][0].shape[0]
    tr = _tile(R, tr_cap, 8)
    sub = sub if tr % sub == 0 else tr
    n_r, n_v, n_or, n_oa = len(rows), len(vecs), len(out_rows), len(out_accs)

    def body(*refs):
        row_refs = refs[:n_r]
        vec_refs = refs[n_r:n_r + n_v]
        orow_refs = refs[n_r + n_v:n_r + n_v + n_or]
        oacc_refs = refs[n_r + n_v + n_or:]
        vv = [r[...] for r in vec_refs]

        n_sub = tr // sub
        together = 4 if n_sub % 4 == 0 else 1

        def step(s, accs):
            for t in range(together):
                sl = pl.ds(pl.multiple_of((s * together + t) * sub, sub), sub)
                if col_chunk is None:
                    ro, ao = fn([r[sl, :] for r in row_refs], vv)
                    for o_ref, o in zip(orow_refs, ro):
                        o_ref[sl, :] = o.astype(o_ref.dtype)
                    accs = tuple(x + y for x, y in zip(accs, ao))
                else:
                    for c0 in range(0, rows[0][2], col_chunk):
                        cs_ = pl.ds(c0, col_chunk)
                        ro, _ = fn([r[sl, cs_] for r in row_refs], vv)
                        for o_ref, o in zip(orow_refs, ro):
                            o_ref[sl, cs_] = o.astype(o_ref.dtype)
            return accs

        accs = lax.fori_loop(0, n_sub // together, step, tuple(jnp.zeros((1, w), f32) for w in out_accs))
        if n_oa:
            @pl.when(pl.program_id(0) == 0)
            def _():
                for ref in oacc_refs:
                    ref[...] = jnp.zeros_like(ref)

            for ref, x in zip(oacc_refs, accs):
                ref[...] += x

    in_specs = [pl.BlockSpec((tr, w), functools.partial(lambda i, cb: (i, cb), cb=cb)) for (_, cb, w) in rows]
    in_specs += [pl.BlockSpec((1, v.shape[1]), lambda i: (0, 0)) for v in vecs]
    out_specs = [pl.BlockSpec((tr, w), lambda i: (i, 0)) for (w, _) in out_rows]
    out_specs += [pl.BlockSpec((1, w), lambda i: (0, 0)) for w in out_accs]
    out_shape = [_sds((R, w), dt) for (w, dt) in out_rows] + [_sds((1, w)) for w in out_accs]
    res = pl.pallas_call(
        body, name=name, grid=(R // tr,), in_specs=in_specs, out_specs=out_specs, out_shape=out_shape,
        compiler_params=_params(("arbitrary",)),
    )(*[r[0] for r in rows], *vecs)
    return res[:n_or], res[n_or:]


def _silu(x):
    return x * jax.nn.sigmoid(x)


def _rms(x, g):
    return x * lax.rsqrt(jnp.mean(x * x, -1, keepdims=True) + EPS) * g


def _adaln_f(x, g, sc, sh):
    return _rms(x, g) * (1.0 + sc) + sh


def _gate_f(y, z, g):
    return _rms(y * _silu(z), g)


def _lnsilu_f(u, g, b):
    mu = jnp.mean(u, -1, keepdims=True)
    var = jnp.mean(jnp.square(u - mu), -1, keepdims=True)
    return _silu((u - mu) * lax.rsqrt(var + EPS) * g + b)


def _adaln_fwd(x, g, sc, sh, name):
    (h,), _ = _rowwise(name, lambda rv, vv: ([_adaln_f(rv[0], *vv)], []), [x], [g, sc, sh], [(x.shape[1], bf16)], [],
                       sub=16)
    return h


def _adaln_bwd(x, g, sc, sh, dh, dres, name):
    def fn(rv, vv):
        xv, dhv, drv = rv
        _, vjp = jax.vjp(_adaln_f, xv, *vv)
        dx, dg, dsc, dsh = vjp(dhv)
        return [dx + drv], [dg, dsc, dsh]
    w = x.shape[1]
    (dx,), accs = _rowwise(name, fn, [x, dh, dres], [g, sc, sh], [(w, f32)], [w, w, w])
    return dx, accs


def _resid_fwd(x, gate, mix, name):
    (y,), _ = _rowwise(name, lambda rv, vv: ([rv[0] + vv[0] * rv[1]], []), [x, mix], [gate], [(x.shape[1], f32)], [])
    return y


def _resid_bwd(dx, mix, gate, name):
    def fn(rv, vv):
        dxv, mv = rv
        dm = vv[0] * dxv
        return [dm], [jnp.sum(dxv * mv, 0, keepdims=True), jnp.sum(dm, 0, keepdims=True)]
    w = dx.shape[1]
    (dmix,), accs = _rowwise(name, fn, [dx, mix], [gate], [(w, bf16)], [w, w], sub=16)
    return dmix, accs


def _add3(a, b, c, name):
    (y,), _ = _rowwise(name, lambda rv, vv: ([rv[0] + rv[1] + rv[2]], []), [a, b, c], [], [(a.shape[1], bf16)], [],
                       sub=16)
    return y


CONV_HALO = 32
CONV_ROWS = 64


def _conv_fwd(x, w, b, *, silu, name, tr=512):
    S, C = x.shape
    K = w.shape[0]
    H = CONV_HALO
    assert K - 1 <= H and S % tr == 0 and tr % H == 0 and C % LANES == 0
    nh = tr // H

    def body(xp_ref, xc_ref, w_ref, b_ref, *rest):
        outs, scr = rest[:-1], rest[-1]
        i = pl.program_id(1)
        scr[pl.ds(0, H), :] = jnp.where(i > 0, xp_ref[...], 0.0)
        scr[pl.ds(H, tr), :] = xc_ref[...]
        taps = [w_ref[pl.ds(k, 1), :] for k in range(K)]
        for c0 in range(0, tr, CONV_ROWS):
            acc = jnp.zeros((CONV_ROWS, LANES), f32) + b_ref[...]
            for k in range(K):
                acc = acc + scr[pl.ds(c0 + H - (K - 1) + k, CONV_ROWS), :] * taps[k]
            outs[0][pl.ds(c0, CONV_ROWS), :] = acc
            if silu:
                outs[1][pl.ds(c0, CONV_ROWS), :] = _silu(acc)

    n_out = 2 if silu else 1
    return pl.pallas_call(
        body, name=name, grid=(C // LANES, S // tr),
        in_specs=[pl.BlockSpec((H, LANES), lambda j, i: (jnp.maximum(i * nh - 1, 0), j)),
                  pl.BlockSpec((tr, LANES), lambda j, i: (i, j)),
                  pl.BlockSpec((K, LANES), lambda j, i: (0, j)),
                  pl.BlockSpec((1, LANES), lambda j, i: (0, j))],
        out_specs=[pl.BlockSpec((tr, LANES), lambda j, i: (i, j))] * n_out,
        out_shape=[_sds((S, C))] * n_out,
        scratch_shapes=[pltpu.VMEM((tr + H, LANES), f32)],
        compiler_params=_params(("parallel", "arbitrary")),
    )(x, x, w, b)


def _conv_bwd(x, w, dact, pre, *, silu, name, dx_dtype=f32, tr=512):
    S, C = x.shape
    K = w.shape[0]
    H = CONV_HALO
    nh = tr // H
    n_i = S // tr
    kp = -(-K // 8) * 8

    def dsilu(p):
        s = jax.nn.sigmoid(p)
        return s * (1.0 + p * (1.0 - s))

    def body(*refs):
        if silu:
            xp_ref, xc_ref, w_ref, dc_ref, dn_ref, pc_ref, pn_ref, dx_ref, dw_ref, db_ref, xs, ds = refs
        else:
            xp_ref, xc_ref, w_ref, dc_ref, dn_ref, dx_ref, dw_ref, db_ref, xs, ds = refs
        i = pl.program_id(1)
        xs[pl.ds(0, H), :] = jnp.where(i > 0, xp_ref[...], 0.0)
        xs[pl.ds(H, tr), :] = xc_ref[...]
        dcur = dc_ref[...]
        dnext = dn_ref[...]
        if silu:
            dcur = dcur * dsilu(pc_ref[...])
            dnext = dnext * dsilu(pn_ref[...])
        ds[pl.ds(0, tr), :] = dcur
        ds[pl.ds(tr, H), :] = jnp.where(i < n_i - 1, dnext, 0.0)
        taps = [w_ref[pl.ds(k, 1), :] for k in range(K)]
        fold = lambda t: jnp.sum(t.reshape(CONV_ROWS // 8, 8, LANES), axis=0)
        dw_parts = [jnp.zeros((8, LANES), f32) for _ in range(K)]
        db_part = jnp.zeros((8, LANES), f32)
        for c0 in range(0, tr, CONV_ROWS):
            acc = jnp.zeros((CONV_ROWS, LANES), f32)
            d_c = ds[pl.ds(c0, CONV_ROWS), :]
            for k in range(K):
                acc = acc + ds[pl.ds(c0 + K - 1 - k, CONV_ROWS), :] * taps[k]
                dw_parts[k] = dw_parts[k] + fold(d_c * xs[pl.ds(c0 + H - (K - 1) + k, CONV_ROWS), :])
            db_part = db_part + fold(d_c)
            dx_ref[pl.ds(c0, CONV_ROWS), :] = acc.astype(dx_ref.dtype)

        @pl.when(i == 0)
        def _():
            dw_ref[...] = jnp.zeros_like(dw_ref)
            db_ref[...] = jnp.zeros_like(db_ref)

        for k in range(K):
            dw_ref[pl.ds(k, 1), :] += jnp.sum(dw_parts[k], 0, keepdims=True)
        db_ref[...] += jnp.sum(db_part, 0, keepdims=True)

    prev = pl.BlockSpec((H, LANES), lambda j, i: (jnp.maximum(i * nh - 1, 0), j))
    cur = pl.BlockSpec((tr, LANES), lambda j, i: (i, j))
    nxt = pl.BlockSpec((H, LANES), lambda j, i: (jnp.minimum((i + 1) * nh, n_i * nh - 1), j))
    in_specs = [prev, cur, pl.BlockSpec((K, LANES), lambda j, i: (0, j)), cur, nxt]
    args = [x, x, w, dact, dact]
    if silu:
        in_specs += [cur, nxt]
        args += [pre, pre]
    dx, dw, db = pl.pallas_call(
        body, name=name, grid=(C // LANES, n_i), in_specs=in_specs,
        out_specs=[cur, pl.BlockSpec((kp, LANES), lambda j, i: (0, j)), pl.BlockSpec((1, LANES), lambda j, i: (0, j))],
        out_shape=[_sds((S, C), dx_dtype), _sds((kp, C)), _sds((1, C))],
        scratch_shapes=[pltpu.VMEM((tr + H, LANES), f32), pltpu.VMEM((tr + H, LANES), f32)],
        compiler_params=_params(("parallel", "arbitrary")),
    )(*args)
    return dx, dw[:K], db


def _dot(a, b):
    return jnp.dot(a.astype(bf16), b.astype(bf16), preferred_element_type=f32)


def _dot_nt(a, b):
    return lax.dot_general(a.astype(bf16), b.astype(bf16), (((1,), (1,)), ((), ())), preferred_element_type=f32)


def _dot_tn(a, b):
    return lax.dot_general(a.astype(bf16), b.astype(bf16), (((0,), (0,)), ((), ())), preferred_element_type=f32)


def _softplus(x):
    return jnp.maximum(x, 0.0) + jnp.log(1.0 + jnp.exp(-jnp.abs(x)))


def _tri(q):
    i = lax.broadcasted_iota(jnp.int32, (q, q), 0)
    j = lax.broadcasted_iota(jnp.int32, (q, q), 1)
    return i >= j


def _ssd_prep(dtraw, dt_bias, a_log):
    q = dtraw.shape[0]
    dt = _softplus(dtraw + dt_bias)
    A = -jnp.exp(a_log)
    tri = _tri(q)
    cs = jnp.dot(tri.astype(f32), dt * A, preferred_element_type=f32, precision=lax.Precision.HIGHEST)
    return dt, A, cs, cs.T, tri


def _expand(cols, h0, n, width):
    q = cols.shape[0]
    return jnp.concatenate([jnp.broadcast_to(cols[:, h0 + r:h0 + r + 1], (q, width)) for r in range(n)], axis=1)


def _ssd_fwd(xbc, dtraw, dt_bias, a_log, d_skip, di, name):
    S, CD = xbc.shape
    Q, N, G = SSM_CHUNK, SSM_STATE, SSM_GROUPS
    nc = S // Q
    nh = di // HEAD_DIM
    R = nh // G
    gw = R * HEAD_DIM

    def body(xbc_ref, dt_ref, bias_ref, alog_ref, dsk_ref, y_ref, hin_ref, state):
        c = pl.program_id(0)

        @pl.when(c == 0)
        def _():
            state[...] = jnp.zeros_like(state)

        hin_ref[...] = state[...]
        dt, A, cs, csT, tri = _ssd_prep(dt_ref[...], bias_ref[...], alog_ref[...])
        dsk = dsk_ref[...]
        ecs = jnp.exp(cs)
        dend = jnp.exp(cs[Q - 1:Q, :] - cs)
        elast = jnp.exp(cs[Q - 1:Q, :])
        for g in range(G):
            h0 = g * R
            Bg = xbc_ref[:, pl.ds(di + g * N, N)]
            Cg = xbc_ref[:, pl.ds(di + G * N + g * N, N)]
            xg = xbc_ref[:, pl.ds(g * gw, gw)]
            Hg = state[pl.ds(g * gw, gw), :]
            Gm = _dot_nt(Cg, Bg)
            xdt = xg * _expand(dt, h0, R, HEAD_DIM)
            yoff = _dot_nt(Cg, Hg) * _expand(ecs, h0, R, HEAD_DIM)
            ys = []
            for r in range(R):
                h = h0 + r
                L = jnp.exp(jnp.where(tri, cs[:, h:h + 1] - csT[h:h + 1, :], -jnp.inf))
                ys.append(_dot(Gm * L, xdt[:, r * HEAD_DIM:(r + 1) * HEAD_DIM]))
            y = jnp.concatenate(ys, axis=1) + yoff + xg * _expand(dsk, h0, R, HEAD_DIM)
            y_ref[:, pl.ds(g * gw, gw)] = y
            hnew = _dot_tn(xdt * _expand(dend, h0, R, HEAD_DIM), Bg)
            escale = jnp.concatenate([jnp.broadcast_to(elast[:, h0 + r:h0 + r + 1], (HEAD_DIM, N)) for r in range(R)], axis=0)
            state[pl.ds(g * gw, gw), :] = escale * Hg + hnew

    vec = pl.BlockSpec((1, LANES), lambda c: (0, 0))
    return pl.pallas_call(
        body, name=name, grid=(nc,),
        in_specs=[pl.BlockSpec((Q, CD), lambda c: (c, 0)), pl.BlockSpec((Q, LANES), lambda c: (c, 0)), vec, vec, vec],
        out_specs=[pl.BlockSpec((Q, di), lambda c: (c, 0)), pl.BlockSpec((None, di, N), lambda c: (c, 0, 0))],
        out_shape=[_sds((S, di)), _sds((nc, di, N))],
        scratch_shapes=[pltpu.VMEM((di, N), f32)],
        compiler_params=_params(("arbitrary",)),
    )(xbc, dtraw, dt_bias, a_log, d_skip)


def _dot_exact(a, b):
    bb = b.astype(bf16)
    hi = a.astype(bf16)
    rest = a - hi.astype(f32)
    mid = rest.astype(bf16)
    low = (rest - mid.astype(f32)).astype(bf16)
    one_pass = lambda t: jnp.dot(t, bb, preferred_element_type=f32)
    return one_pass(hi) + one_pass(mid) + one_pass(low)


def _ssd_bwd(xbc, dtraw, dt_bias, a_log, d_skip, hin, y, dy, di, name):
    S, CD = xbc.shape
    Q, N, G = SSM_CHUNK, SSM_STATE, SSM_GROUPS
    nc = S // Q
    nh = di // HEAD_DIM
    R = nh // G
    gw = R * HEAD_DIM
    P = HEAD_DIM
    head_of_col = jnp.asarray((np.arange(di)[:, None] // P == np.arange(LANES)[None, :]).astype(np.float32))
    dsk_wide = jnp.repeat(d_skip[0, :nh], P)[None]

    def body(xbc_ref, dt_ref, bias_ref, alog_ref, dskw_ref, hoc_ref, hin_ref, y_ref, dy_ref,
             dxbc_ref, ddt_ref, dA_ref, ddsk_ref, dtb_ref, dstate, dxdt_all, tend_all, yoff_all, colterm_all):
        c = pl.program_id(0)

        @pl.when(c == 0)
        def _():
            dstate[...] = jnp.zeros_like(dstate)
            dA_ref[...] = jnp.zeros_like(dA_ref)
            ddsk_ref[...] = jnp.zeros_like(ddsk_ref)
            dtb_ref[...] = jnp.zeros_like(dtb_ref)

        dtraw_v = dt_ref[...]
        dt, A, cs, csT, tri = _ssd_prep(dtraw_v, bias_ref[...], alog_ref[...])
        tri_t = jnp.logical_not(tri) | (lax.broadcasted_iota(jnp.int32, (Q, Q), 0) == lax.broadcasted_iota(jnp.int32, (Q, Q), 1))
        ecs = jnp.exp(cs)
        dend = jnp.exp(cs[Q - 1:Q, :] - cs)
        elast = jnp.exp(cs[Q - 1:Q, :])
        hoc = hoc_ref[...]
        state_dot = jnp.sum(_dot_exact(dstate[...] * hin_ref[...], jnp.ones((N, LANES), f32)) * hoc, 0, keepdims=True) * elast
        for g in range(G):
            h0 = g * R
            Bg = xbc_ref[:, pl.ds(di + g * N, N)]
            Cg = xbc_ref[:, pl.ds(di + G * N + g * N, N)]
            xg = xbc_ref[:, pl.ds(g * gw, gw)]
            dyg = dy_ref[:, pl.ds(g * gw, gw)]
            Hg = hin_ref[pl.ds(g * gw, gw), :]
            dHg = dstate[pl.ds(g * gw, gw), :]
            dt_e = _expand(dt, h0, R, P)
            ecs_e = _expand(ecs, h0, R, P)
            dend_e = _expand(dend, h0, R, P)
            cols = pl.ds(g * gw, gw)
            Gm = _dot_nt(Cg, Bg)
            Gm_t = _dot_nt(Bg, Cg)
            xdt = xg * dt_e
            dye = dyg * ecs_e
            bdh = _dot_nt(Bg, dHg)
            dC = _dot(dye, Hg)
            dB = _dot(xdt * dend_e, dHg)
            dHin = _dot_tn(dye, Cg)
            dxdt_state = dend_e * bdh
            end_term = xdt * dxdt_state
            tend_all[:, cols] = end_term
            yoff_all[:, cols] = _dot_nt(Cg, Hg) * ecs_e
            dG = jnp.zeros((Q, Q), f32)
            dxd = []
            for r in range(R):
                h = h0 + r
                sl = slice(r * P, (r + 1) * P)
                seg = cs[:, h:h + 1] - csT[h:h + 1, :]
                L = jnp.exp(jnp.where(tri, seg, -jnp.inf))
                L_t = jnp.exp(jnp.where(tri_t, -seg, -jnp.inf))
                dyh = dyg[:, sl]
                dG = dG + _dot_nt(dyh, xdt[:, sl]) * L
                dxd.append(_dot(Gm_t * L_t, dyh))
            dxdt_diag = jnp.concatenate(dxd, axis=1)
            dxdt = dxdt_diag + dxdt_state
            dxdt_all[:, cols] = dxdt
            colterm_all[:, cols] = xdt.astype(bf16).astype(f32) * dxdt_diag + end_term
            dxbc_ref[:, cols] = dxdt * dt_e + dyg * dskw_ref[:, cols]
            dxbc_ref[:, pl.ds(di + g * N, N)] = dB + _dot_tn(dG, Cg)
            dxbc_ref[:, pl.ds(di + G * N + g * N, N)] = dC + _dot(dG, Bg)
            escale = jnp.concatenate([jnp.broadcast_to(elast[:, h0 + r:h0 + r + 1], (P, N)) for r in range(R)], axis=0)
            dstate[pl.ds(g * gw, gw), :] = escale * dHg + dHin
        xs = xbc_ref[:, pl.ds(0, di)]
        dyv = dy_ref[...]
        yoff = yoff_all[...]
        y_diag = y_ref[...] - dskw_ref[...] * xs - yoff
        rs_y = _dot_exact(dyv.astype(bf16).astype(f32) * y_diag + dyv * yoff, hoc)
        rs_c = _dot_exact(colterm_all[...], hoc)
        rs_x = _dot_exact(dxdt_all[...] * xs, hoc)
        end_dot = _dot_exact(jnp.broadcast_to(jnp.sum(tend_all[...], 0, keepdims=True), (8, di)), hoc)[0:1]
        last = lax.broadcasted_iota(jnp.int32, (Q, 1), 0) == Q - 1
        dcs = rs_y - rs_c + jnp.where(last, end_dot + state_dot, 0.0)
        da = lax.dot_general(tri.astype(f32), dcs, (((0,), (0,)), ((), ())), preferred_element_type=f32,
                             precision=lax.Precision.HIGHEST)
        ddt = da * A + rs_x
        ddtraw = ddt * jax.nn.sigmoid(dtraw_v + bias_ref[...])
        ddt_ref[...] = ddtraw.astype(ddt_ref.dtype)
        dA_ref[...] += jnp.sum(da * dt, 0, keepdims=True) * A
        ddsk_ref[...] += jnp.sum(_dot_exact(dyv * xs, hoc), 0, keepdims=True)
        dtb_ref[...] += jnp.sum(ddtraw, 0, keepdims=True)

    vec = pl.BlockSpec((1, LANES), lambda c: (0, 0))
    rev = lambda c: (nc - 1 - c, 0)
    return pl.pallas_call(
        body, name=name, grid=(nc,),
        in_specs=[pl.BlockSpec((Q, CD), rev), pl.BlockSpec((Q, LANES), rev), vec, vec,
                  pl.BlockSpec((1, di), lambda c: (0, 0)), pl.BlockSpec((di, LANES), lambda c: (0, 0)),
                  pl.BlockSpec((None, di, N), lambda c: (nc - 1 - c, 0, 0)), pl.BlockSpec((Q, di), rev),
                  pl.BlockSpec((Q, di), rev)],
        out_specs=[pl.BlockSpec((Q, CD), rev), pl.BlockSpec((Q, LANES), rev), vec, vec, vec],
        out_shape=[_sds((S, CD)), _sds((S, LANES), bf16), _sds((1, LANES)), _sds((1, LANES)), _sds((1, LANES))],
        scratch_shapes=[pltpu.VMEM((di, N), f32)] + [pltpu.VMEM((Q, di), f32)] * 4,
        compiler_params=_params(("arbitrary",)),
    )(xbc, dtraw, dt_bias, a_log, dsk_wide, head_of_col, hin, y, dy)


def _t5_bucket_np(dist):
    max_exact = REL_BUCKETS // 2
    n = np.maximum(dist, 1).astype(np.float32)
    large = np.float32(max_exact) + np.log(n / np.float32(max_exact)) / np.float32(math.log(REL_MAX_DIST / max_exact)) * np.float32(REL_BUCKETS - max_exact)
    large = np.minimum(large.astype(np.int32), REL_BUCKETS - 1)
    return np.where(dist < max_exact, dist, large)


def _bucket_onehot():
    i = np.arange(ATT_BLK)[None, :]
    j = np.arange(2 * ATT_BLK)[:, None]
    delta = np.maximum(ATT_BLK + i - j, 0)
    out = np.zeros((len(ATT_DILATIONS), REL_BUCKETS, ATT_BLK * 2 * ATT_BLK), np.float32)
    for gi, d in enumerate(ATT_DILATIONS):
        b = _t5_bucket_np(delta * d).reshape(-1)
        out[gi, b, np.arange(b.size)] = 1.0
    return out


def _exact_mm(a, b, *, name, tb=False):
    M, K = a.shape
    N = b.shape[0] if tb else b.shape[1]
    tn = _tile(N, 4096, LANES)
    dn = (((1,), (1 if tb else 0,)), ((), ()))

    def body(a_ref, b_ref, o_ref):
        o_ref[...] = lax.dot_general(a_ref[...], b_ref[...], dn, preferred_element_type=f32,
                                     precision=lax.Precision.HIGHEST)

    return pl.pallas_call(
        body, name=name, grid=(N // tn,),
        in_specs=[pl.BlockSpec((M, K), lambda j: (0, 0)),
                  pl.BlockSpec((tn, K), lambda j: (j, 0)) if tb else pl.BlockSpec((K, tn), lambda j: (0, j))],
        out_specs=pl.BlockSpec((M, tn), lambda j: (0, j)), out_shape=_sds((M, N)),
        compiler_params=_params(("parallel",)),
    )(a, b)


def _band_penalty():
    i = np.arange(ATT_BLK)[None, :]
    j = np.arange(2 * ATT_BLK)[:, None]
    delta = ATT_BLK + i - j
    return np.where((delta >= 0) & (delta <= ATT_BLK), 0.0, -np.inf).astype(np.float32)


def _first_block_keep(n):
    key = lax.broadcasted_iota(jnp.int32, (2 * ATT_BLK, ATT_BLK), 0)
    return (key >= ATT_BLK) | (n > 0)


ATT_SCALE = HEAD_DIM ** -0.5


def _rows(ref, r, d):
    return ref[...] if d == 1 else ref[pl.ds(r, ATT_BLK, stride=d), :]


def _set_rows(ref, r, d, val):
    if d == 1:
        ref[...] = val
    else:
        ref[pl.ds(r, ATT_BLK, stride=d), :] = val


def _attn_width(d, D):
    return D if d == 1 else LANES


def _over_residues(d, one, unroll=1):
    if d == 1:
        one(0)
    else:
        lax.fori_loop(0, d, lambda r, c: (one(r), c)[1], 0, unroll=unroll)


def _attn_fwd(q, k, v, bias_t, d, name):
    S, D = q.shape
    nb = S // (d * ATT_BLK)
    H = D // HEAD_DIM
    W = _attn_width(d, D)
    HB = W // HEAD_DIM

    def body(q_ref, kp_ref, kc_ref, vp_ref, vc_ref, b_ref, o_ref, lse_ref):
        keep = _first_block_keep(pl.program_id(1))
        first = lax.broadcasted_iota(jnp.int32, (1, LANES), 1) < HEAD_DIM

        def one(r):
            qs = (_rows(q_ref, r, d) * ATT_SCALE).astype(bf16)
            kcat = jnp.concatenate([_rows(kp_ref, r, d), _rows(kc_ref, r, d)], axis=0).astype(bf16)
            vcat = jnp.concatenate([_rows(vp_ref, r, d), _rows(vc_ref, r, d)], axis=0).astype(bf16)
            outs = []
            for pair in range(W // LANES):
                ps = slice(pair * LANES, (pair + 1) * LANES)
                q2, k2, v2 = qs[:, ps], kcat[:, ps], vcat[:, ps]
                o2 = jnp.zeros((ATT_BLK, LANES), f32)
                for e in range(2):
                    h = 2 * pair + e
                    mine = first if e == 0 else jnp.logical_not(first)
                    zero = jnp.zeros((), bf16)
                    st = jnp.where(keep, _dot_nt(k2, jnp.where(mine, q2, zero)) + b_ref[h], -jnp.inf)
                    m = jnp.max(st, 0, keepdims=True)
                    pt = jnp.exp(st - m)
                    l = jnp.sum(pt, 0, keepdims=True)
                    o2 = o2 + _dot_tn(pt * (1.0 / l), jnp.where(mine, v2, zero))
                    lse_ref[r, h] = m + jnp.log(l)
                outs.append(o2)
            _set_rows(o_ref, r, d, jnp.concatenate(outs, axis=1))

        _over_residues(d, one, unroll=4)

    cur = pl.BlockSpec((ATT_BLK * d, W), lambda j, n: (n, j))
    prev = pl.BlockSpec((ATT_BLK * d, W), lambda j, n: (jnp.maximum(n - 1, 0), j))
    return pl.pallas_call(
        body, name=name, grid=(D // W, nb),
        in_specs=[cur, prev, cur, prev, cur, pl.BlockSpec((HB, 2 * ATT_BLK, ATT_BLK), lambda j, n: (j, 0, 0))],
        out_specs=[cur, pl.BlockSpec((None, d, HB, 1, LANES), lambda j, n: (n, 0, j, 0, 0))],
        out_shape=[_sds((S, D)), _sds((nb, d, H, 1, LANES))],
        compiler_params=_params(("parallel", "arbitrary")),
    )(q, k, k, v, v, bias_t)


def _from_blocks(rows, lanes=None):
    nb, d, H = rows.shape[:3]
    a = jnp.transpose(rows[:, :, :, 0, :], (0, 3, 1, 2)).reshape(nb * ATT_BLK * d, H)
    return a if lanes is None else jnp.pad(a, ((0, 0), (0, lanes - H)))


def _by_block(a, d):
    S, H = a.shape
    t = jnp.transpose(a.reshape(S // (d * ATT_BLK), ATT_BLK, d, H), (0, 2, 3, 1))
    return t[:, :, :, None, :]


def _head_sums(a, b, name):
    S, D = a.shape
    tr = _tile(S, 512, 8)
    hoc = jnp.asarray((np.arange(D)[:, None] // HEAD_DIM == np.arange(LANES)[None, :]).astype(np.float32))

    def body(a_ref, b_ref, h_ref, o_ref):
        o_ref[...] = _dot_exact(a_ref[...] * b_ref[...], h_ref[...])

    return pl.pallas_call(
        body, name=name, grid=(S // tr,),
        in_specs=[pl.BlockSpec((tr, D), lambda i: (i, 0)), pl.BlockSpec((tr, D), lambda i: (i, 0)),
                  pl.BlockSpec((D, LANES), lambda i: (0, 0))],
        out_specs=pl.BlockSpec((tr, LANES), lambda i: (i, 0)), out_shape=_sds((S, LANES)),
        compiler_params=_params(("parallel",)),
    )(a, b, hoc)


def _attn_bwd(q, k, v, bias_t, datt, lse_rows, dsum_rows, d, name):
    S, D = q.shape
    nb = S // (d * ATT_BLK)
    H = D // HEAD_DIM
    W = _attn_width(d, D)
    HB = W // HEAD_DIM

    def body(q_ref, kp_ref, kc_ref, vp_ref, vc_ref, b_ref, do_ref, lse_ref, dsum_ref,
             dq_ref, dk_ref, dv_ref, db_ref, carry_k, carry_v):
        j = pl.program_id(0)
        n = pl.program_id(1)

        @pl.when(n == 0)
        def _():
            carry_k[...] = jnp.zeros_like(carry_k)
            carry_v[...] = jnp.zeros_like(carry_v)
            db_ref[...] = jnp.zeros_like(db_ref)

        @pl.when(n < nb)
        def _():
            key = lax.broadcasted_iota(jnp.int32, (2 * ATT_BLK, ATT_BLK), 0)
            keep = (key >= ATT_BLK) | (n > 0)
            first = lax.broadcasted_iota(jnp.int32, (1, LANES), 1) < HEAD_DIM

            def one(r):
                qs = (_rows(q_ref, r, d) * ATT_SCALE).astype(bf16)
                kcat = jnp.concatenate([_rows(kp_ref, r, d), _rows(kc_ref, r, d)], axis=0).astype(bf16)
                vcat = jnp.concatenate([_rows(vp_ref, r, d), _rows(vc_ref, r, d)], axis=0).astype(bf16)
                dob = _rows(do_ref, r, d).astype(bf16)
                dqs, dks, dvs = [], [], []
                for pair in range(W // LANES):
                    ps = slice(pair * LANES, (pair + 1) * LANES)
                    q2, k2, v2, do2 = qs[:, ps], kcat[:, ps], vcat[:, ps], dob[:, ps]
                    dq2 = jnp.zeros((ATT_BLK, LANES), f32)
                    dk2 = jnp.zeros((2 * ATT_BLK, LANES), f32)
                    dv2 = jnp.zeros((2 * ATT_BLK, LANES), f32)
                    for e in range(2):
                        h = 2 * pair + e
                        mine = first if e == 0 else jnp.logical_not(first)
                        zero = jnp.zeros((), bf16)
                        qm, dom, km = jnp.where(mine, q2, zero), jnp.where(mine, do2, zero), jnp.where(mine, k2, zero)
                        st = jnp.where(keep, _dot_nt(k2, qm) + b_ref[h], -jnp.inf)
                        pt = jnp.exp(st - lse_ref[r, j * HB + h])
                        dst = pt * (_dot_nt(v2, dom) - dsum_ref[r, j * HB + h])
                        db_ref[h] += dst
                        dv2 = dv2 + _dot(pt, dom)
                        dk2 = dk2 + _dot(dst, qm)
                        dq2 = dq2 + _dot_tn(dst, km)
                    dqs.append(dq2 * ATT_SCALE)
                    dks.append(dk2)
                    dvs.append(dv2)
                _set_rows(dq_ref, r, d, jnp.concatenate(dqs, axis=1))
                dk = jnp.concatenate(dks, axis=1)
                dv = jnp.concatenate(dvs, axis=1)
                _set_rows(dk_ref, r, d, carry_k[r] + dk[:ATT_BLK])
                _set_rows(dv_ref, r, d, carry_v[r] + dv[:ATT_BLK])
                carry_k[r] = dk[ATT_BLK:]
                carry_v[r] = dv[ATT_BLK:]

            _over_residues(d, one, unroll=2)

        @pl.when(n == nb)
        def _():
            def last(r):
                _set_rows(dk_ref, r, d, carry_k[r])
                _set_rows(dv_ref, r, d, carry_v[r])

            _over_residues(d, last)

    nq = lambda n: jnp.minimum(n, nb - 1)
    cur = pl.BlockSpec((ATT_BLK * d, W), lambda j, n: (nq(n), j))
    prev = pl.BlockSpec((ATT_BLK * d, W), lambda j, n: (jnp.maximum(nq(n) - 1, 0), j))
    done = pl.BlockSpec((ATT_BLK * d, W), lambda j, n: (jnp.maximum(n - 1, 0), j))
    bspec = pl.BlockSpec((HB, 2 * ATT_BLK, ATT_BLK), lambda j, n: (j, 0, 0))
    rows = pl.BlockSpec((None, d, H, 1, LANES), lambda j, n: (nq(n), 0, 0, 0, 0))
    return pl.pallas_call(
        body, name=name, grid=(D // W, nb + 1),
        in_specs=[cur, prev, cur, prev, cur, bspec, cur, rows, rows],
        out_specs=[cur, done, done, bspec],
        out_shape=[_sds((S, D)), _sds((S, D)), _sds((S, D)), _sds((H, 2 * ATT_BLK, ATT_BLK))],
        scratch_shapes=[pltpu.VMEM((d, ATT_BLK, W), f32), pltpu.VMEM((d, ATT_BLK, W), f32)],
        compiler_params=_params(("arbitrary", "arbitrary")),
    )(q, k, k, v, v, bias_t, datt, lse_rows, dsum_rows)


def _attn_combine(os_, lses, name):
    S, D = os_[0].shape
    tr = _tile(S, 128, 16)
    head_cols = jnp.asarray((np.arange(LANES)[:, None] == np.arange(D)[None, :] // HEAD_DIM).astype(np.float32))

    def body(o0, o1, o2, l0, l1, l2, hc_ref, att_ref, attb_ref, lse_ref):
        a, b, c = l0[...], l1[...], l2[...]
        m = jnp.maximum(jnp.maximum(a, b), c)
        e0, e1, e2 = jnp.exp(a - m), jnp.exp(b - m), jnp.exp(c - m)
        tot = e0 + e1 + e2
        wide = lambda w: _dot_exact(w / tot, hc_ref[...])
        att = wide(e0) * o0[...] + wide(e1) * o1[...] + wide(e2) * o2[...]
        att_ref[...] = att
        attb_ref[...] = att.astype(bf16)
        lse_ref[...] = m + jnp.log(tot)

    wide_spec = pl.BlockSpec((tr, D), lambda i: (i, 0))
    lane_spec = pl.BlockSpec((tr, LANES), lambda i: (i, 0))
    return pl.pallas_call(
        body, name=name, grid=(S // tr,),
        in_specs=[wide_spec] * 3 + [lane_spec] * 3 + [pl.BlockSpec((LANES, D), lambda i: (0, 0))],
        out_specs=[wide_spec, wide_spec, lane_spec], out_shape=[_sds((S, D)), _sds((S, D), bf16), _sds((S, LANES))],
        compiler_params=_params(("parallel",)),
    )(*os_, *lses, head_cols)


ANY = pl.BlockSpec(memory_space=pl.ANY)


def _all_gather(vs, name):
    n = len(vs)

    def body(*refs):
        x_refs, out_refs = refs[:n], refs[n:2 * n]
        send_sems, recv_sems, local_sems = refs[2 * n:]
        x, y, c = lax.axis_index("x"), lax.axis_index("y"), lax.axis_index("c")
        me, sibling = (x, y, c), (x, y, 1 - c)
        chips = [(1 - x, y), (x, 1 - y), (1 - x, 1 - y)]

        def slot(i, px, py, pc):
            return out_refs[i].at[4 * px + 2 * py + pc]

        def copy(i, k, block, to, src=None):
            return pltpu.make_async_remote_copy(
                src_ref=slot(i, *block) if src is None else src, dst_ref=slot(i, *block),
                send_sem=send_sems.at[i, k], recv_sem=recv_sems.at[i, k], device_id=to, device_id_type=MESH)

        mine = [pltpu.make_async_copy(x_refs[i], slot(i, *me), local_sems.at[i]) for i in range(n)]
        for cp in mine:
            cp.start()
        first = []
        for i in range(n):
            first.append(copy(i, 0, me, sibling, src=x_refs[i]))
            first += [copy(i, 1 + j, me, (*chip, c), src=x_refs[i]) for j, chip in enumerate(chips)]
        for cp in first:
            cp.start()
        passed = []
        for i in range(n):
            for j, chip in enumerate(chips):
                copy(i, 1 + j, (*chip, c), me).wait_recv()
                cp = copy(i, 4 + j, (*chip, c), sibling)
                cp.start()
                passed.append(cp)
        for i in range(n):
            copy(i, 0, sibling, me).wait_recv()
            for j, chip in enumerate(chips):
                copy(i, 4 + j, (*chip, 1 - c), me).wait_recv()
        for cp in first + passed:
            cp.wait_send()
        for cp in mine:
            cp.wait()

    return pl.pallas_call(
        body, name=name, out_shape=[_sds((N_DEV,) + v.shape, v.dtype) for v in vs], in_specs=[ANY] * n,
        out_specs=[ANY] * n,
        scratch_shapes=[pltpu.SemaphoreType.DMA((n, 7)), pltpu.SemaphoreType.DMA((n, 7)), pltpu.SemaphoreType.DMA((n,))],
    )(*vs)


def _sum_slots(t, name):
    n, R, C = t.shape
    tr = _tile(R, PACK_ROW_TILE, 16)

    def body(t_ref, o_ref):
        acc = t_ref[0].astype(f32)
        for k in range(1, n):
            acc = acc + t_ref[k].astype(f32)
        o_ref[...] = acc

    return pl.pallas_call(
        body, name=name, grid=(R // tr,),
        in_specs=[pl.BlockSpec((n, tr, C), lambda i: (0, i, 0))],
        out_specs=pl.BlockSpec((tr, C), lambda i: (i, 0)), out_shape=_sds((R, C)),
        compiler_params=_params(("parallel",)),
    )(t)


HBM_SPEC = pl.BlockSpec(memory_space=pltpu.HBM)
SEM_SPEC = pl.BlockSpec(memory_space=pltpu.SEMAPHORE)
EFFECT = pltpu.SideEffectType.DATAFLOW_SIDE_EFFECTING


def _mesh_pos(p):
    return (p // 4, (p // 2) % 2, p % 2)


def _exchange_copy(src_refs, land_refs, send_sems, recv_sems, whole, dests, i, k):
    me = 4 * lax.axis_index("x") + 2 * lax.axis_index("y") + lax.axis_index("c")
    to = (me + k) % N_DEV
    frm = (me + N_DEV - k) % N_DEV
    lo, hi = dests
    src = src_refs[i] if whole else src_refs[i].at[jnp.minimum(jnp.maximum(to - lo, 0), hi - lo - 1)]
    s = i * (N_DEV - 1) + k - 1
    send = pltpu.make_async_remote_copy(src_ref=src, dst_ref=land_refs[i].at[me], send_sem=send_sems.at[s],
                                        recv_sem=recv_sems.at[s], device_id=_mesh_pos(to), device_id_type=MESH)
    recv = pltpu.make_async_remote_copy(src_ref=src, dst_ref=land_refs[i].at[frm], send_sem=send_sems.at[s],
                                        recv_sem=recv_sems.at[s], device_id=_mesh_pos(to), device_id_type=MESH)
    return send, recv, (to >= lo) & (to < hi), (me >= lo) & (me < hi)


def _exchange_start(srcs, whole, name, after=None, dests=(0, N_DEV)):
    n = len(srcs)
    lands = [lax.empty((N_DEV,) + s.shape[-2:], s.dtype) for s in srcs]
    after = list(after or [])
    n_in = 2 * n + len(after)
    everyone = dests == (0, N_DEV)

    def body(*refs):
        src_refs, land_refs = refs[:n], refs[n:2 * n]
        send_sems, recv_sems, token = refs[n_in], refs[n_in + 1], refs[-1]
        for i in range(n):
            for k in range(1, N_DEV):
                send, _, sends, _ = _exchange_copy(src_refs, land_refs, send_sems, recv_sems, whole, dests, i, k)
                if everyone:
                    send.start()
                else:
                    pl.when(sends)(send.start)
        token[...] = jnp.zeros_like(token)

    sems = pltpu.SemaphoreType.DMA((n * (N_DEV - 1),))
    outs = pl.pallas_call(
        body, name=name,
        out_shape=(sems, sems, *[pltpu.HBM(a.shape, a.dtype) for a in srcs + lands], _sds((8, LANES))),
        in_specs=[HBM_SPEC] * (2 * n) + [pl.BlockSpec(memory_space=pl.ANY)] * len(after),
        out_specs=(SEM_SPEC, SEM_SPEC, *[HBM_SPEC] * (2 * n), pl.BlockSpec(memory_space=pltpu.VMEM)),
        input_output_aliases={i: 2 + i for i in range(2 * n)},
        compiler_params=pltpu.CompilerParams(has_side_effects=EFFECT),
    )(*[pltpu.with_memory_space_constraint(a, pltpu.HBM) for a in srcs + lands], *after)
    return (outs[0], outs[1], list(outs[2:2 + n]), list(outs[2 + n:2 + 2 * n]), whole, dests), outs[-1]


def _exchange_wait(handle, after, name):
    send_sems, recv_sems, srcs, lands, whole, dests = handle
    n = len(srcs)
    everyone = dests == (0, N_DEV)

    def body(*refs):
        src_refs, land_refs = refs[:n], refs[n:2 * n]
        send_sems, recv_sems = refs[2 * n], refs[2 * n + 1]
        for i in range(n):
            for k in range(1, N_DEV):
                send, recv, sends, receives = _exchange_copy(src_refs, land_refs, send_sems, recv_sems, whole, dests, i, k)
                if everyone:
                    send.wait_send()
                    recv.wait_recv()
                else:
                    pl.when(sends)(send.wait_send)
                    pl.when(receives)(recv.wait_recv)

    outs = pl.pallas_call(
        body, name=name, out_shape=tuple(pltpu.HBM(a.shape, a.dtype) for a in srcs + lands),
        in_specs=[HBM_SPEC] * (2 * n) + [SEM_SPEC, SEM_SPEC, pl.BlockSpec(memory_space=pl.ANY)],
        out_specs=[HBM_SPEC] * (2 * n), input_output_aliases={i: i for i in range(2 * n)},
        compiler_params=pltpu.CompilerParams(has_side_effects=EFFECT),
    )(*srcs, *lands, send_sems, recv_sems, after)
    return list(outs[n:])


def _tie(v, token):
    return v + token[0:1, 0:1].astype(v.dtype).reshape((1,) * v.ndim)


def _with_own(land, own, me):
    return lax.dynamic_update_slice_in_dim(land, own[None].astype(land.dtype), me, 0)


class _Overlap:
    def __init__(self, shards, me, after):
        self.me = me
        self.names = list(shards)
        self.handle, self.token = _exchange_start([shards[nm] for nm in self.names], True, "weights_start", after)
        self.sent = {}

    def weights(self, after):
        lands = _exchange_wait(self.handle, after, "weights_wait")
        own = self.handle[2]
        return {nm: _full_from_blocks(nm, _with_own(land, o, self.me)) for nm, land, o in zip(self.names, lands, own)}

    def send(self, tag, grads, after=None):
        names = list(grads)
        handle, token = _exchange_start([_blocks_from_full(nm, grads[nm]) for nm in names], False, f"grads_start_{tag}",
                                        after)
        self.sent[tag] = (names, handle)
        return token

    def send_rows(self, tag, rows, dests):
        lo, hi = dests
        blocks = rows.reshape(hi - lo, rows.shape[0] // (hi - lo), rows.shape[1])
        handle, token = _exchange_start([blocks], False, f"grads_start_{tag}", None, dests)
        self.sent[tag] = ([tag], handle)
        return token

    def received(self, tag, after):
        names, handle = self.sent[tag]
        lands = _exchange_wait(handle, after, f"grads_wait_{tag}")
        lo = handle[5][0]
        own = [lax.dynamic_index_in_dim(b, self.me - lo, 0, keepdims=False) for b in handle[2]]
        return {nm: _with_own(land, o, self.me) for nm, land, o in zip(names, lands, own)}


ADAM_ROWS = 32


def _adamw(w, g, m, v, name):
    deep = w.ndim == 3
    R, C = w.shape[0], w.shape[-1]
    cb = LANES if C % LANES == 0 else C
    n_parts = g.shape[0] if g.ndim == 3 else 0

    def body(w_ref, g_ref, m_ref, v_ref, d_ref, m2_ref, v2_ref, *g_out):
        at = (lambda ref, sl: ref.at[sl, 0, :]) if deep else (lambda ref, sl: ref.at[sl, :])

        def update(sl):
            if n_parts:
                gv = g_ref[0, sl, :].astype(f32)
                for k in range(1, n_parts):
                    gv = gv + g_ref[k, sl, :].astype(f32)
                at(g_out[0], sl)[...] = gv
            else:
                gv = g_ref[sl, :]
            m2 = ADAM_B1 * at(m_ref, sl)[...] + (1.0 - ADAM_B1) * gv
            v2 = ADAM_B2 * at(v_ref, sl)[...] + (1.0 - ADAM_B2) * jnp.square(gv)
            m_hat = m2 / (1.0 - ADAM_B1 ** ADAM_STEP)
            v_hat = v2 / (1.0 - ADAM_B2 ** ADAM_STEP)
            at(d_ref, sl)[...] = -ADAM_LR * (m_hat / (jnp.sqrt(v_hat) + ADAM_EPS) + ADAM_WD * at(w_ref, sl)[...])
            at(m2_ref, sl)[...] = m2
            at(v2_ref, sl)[...] = v2

        main = R // ADAM_ROWS
        if main:
            lax.fori_loop(0, main, lambda i, c: (update(pl.ds(pl.multiple_of(i * ADAM_ROWS, ADAM_ROWS), ADAM_ROWS)), c)[1], 0)
        if R % ADAM_ROWS:
            update(pl.ds(main * ADAM_ROWS, R % ADAM_ROWS))

    spec = pl.BlockSpec((R, 1, cb), lambda j: (0, 0, j)) if deep else pl.BlockSpec((R, cb), lambda j: (0, j))
    g_spec = pl.BlockSpec((n_parts, R, cb), lambda j: (0, 0, j)) if n_parts else pl.BlockSpec((R, cb), lambda j: (0, j))
    n_out = 4 if n_parts else 3
    return pl.pallas_call(
        body, name=name, grid=(C // cb,), in_specs=[spec, g_spec, spec, spec], out_specs=[spec] * n_out,
        out_shape=[_sds(w.shape)] * n_out, compiler_params=_params(("parallel",)),
    )(w, g, m, v)


BIG_PARAMS = ("hy_w_in", "hy_w_out", "cv_w_pw1", "cv_w_pw2", "ffn_w_gate", "ffn_w_up", "ffn_w_down")


def _shards_2d(w):
    t = lambda a: jnp.transpose(a)
    return dict(in_t=t(w["hy_w_in"][0]), out=w["hy_w_out"][0], pw1=w["cv_w_pw1"][0], pw2=w["cv_w_pw2"][0],
                gate_t0=t(w["ffn_w_gate"][0]), gate_t1=t(w["ffn_w_gate"][1]), up_t0=t(w["ffn_w_up"][0]),
                up_t1=t(w["ffn_w_up"][1]), down0=w["ffn_w_down"][0], down1=w["ffn_w_down"][1])


def _unshard_2d(s):
    t = lambda a: jnp.transpose(a)
    out = dict(hy_w_out=s["out"][None], cv_w_pw1=s["pw1"][None], cv_w_pw2=s["pw2"][None],
               ffn_w_gate=jnp.stack([t(s["gate_t0"]), t(s["gate_t1"])]),
               ffn_w_up=jnp.stack([t(s["up_t0"]), t(s["up_t1"])]), ffn_w_down=jnp.stack([s["down0"], s["down1"]]))
    if "in_t" in s:
        out["hy_w_in"] = t(s["in_t"])[None]
    return out


def _full_from_blocks(nm, g):
    if nm == "pw1":
        return jnp.transpose(g, (1, 0, 2)).reshape(g.shape[1], N_DEV * g.shape[2])
    return g.reshape(N_DEV * g.shape[1], g.shape[2])


def _blocks_from_full(nm, g):
    if nm == "pw1":
        return jnp.transpose(g.reshape(g.shape[0], N_DEV, g.shape[1] // N_DEV), (1, 0, 2))
    return g.reshape(N_DEV, g.shape[0] // N_DEV, g.shape[1])


class _VecPack:
    def __init__(self, shapes):
        self.shapes = [tuple(s) for s in shapes]
        self.sizes = [int(np.prod(s)) for s in self.shapes]
        total = sum(self.sizes)
        self.rows = -(-(-(-total // LANES)) // 8) * 8
        self.total = total

    def pack(self, arrays):
        flat = jnp.concatenate([a.astype(f32).reshape(-1) for a in arrays])
        flat = jnp.pad(flat, (0, self.rows * LANES - self.total))
        return flat.reshape(self.rows, LANES)

    def unpack(self, packed):
        flat = packed.reshape(-1)
        out, off = [], 0
        for shp, n in zip(self.shapes, self.sizes):
            out.append(flat[off:off + n].reshape(shp))
            off += n
        return out

    def unpack_stacked(self, stacked, only=None):
        flat = stacked.reshape(stacked.shape[0], -1)
        offs = np.concatenate([[0], np.cumsum(self.sizes)])
        get = lambda i: flat[:, offs[i]:offs[i + 1]].reshape((stacked.shape[0],) + self.shapes[i])
        return get(only) if only is not None else [get(i) for i in range(len(self.shapes))]


def _row(v):
    return v.reshape(1, -1)


def _pad_lanes(v):
    v = v.reshape(1, -1)
    return jnp.pad(v, ((0, 0), (0, LANES - v.shape[1])))


def _ffn_fwd(h, w_gate_t, w_up_t, w_down, tag):
    F = w_down.shape[0]
    a = _mm(h, w_gate_t, tb=True, out_dtype=bf16, name=f"ffn_gate_{tag}")
    u = _mm(h, w_up_t, tb=True, out_dtype=bf16, name=f"ffn_up_{tag}")
    (f,), _ = _rowwise(f"swiglu_{tag}", lambda rv, vv: ([_silu(rv[0].astype(f32)) * rv[1].astype(f32)], []), [a, u], [],
                       [(F, bf16)], [], sub=16, col_chunk=_tile(F, 512, LANES))
    out = _mm(f, w_down, name=f"ffn_down_{tag}")
    return out, (a, u, f)


def _ffn_bwd(h, w_gate_t, w_up_t, w_down, saved, dout, tag):
    a, u, f = saved
    F = w_down.shape[0]
    df = _mm(dout, w_down, tb=True, out_dtype=bf16, name=f"ffn_down_dx_{tag}")
    dw_down = _mm(f, dout, ta=True, out_dtype=bf16, name=f"ffn_down_dw_{tag}")

    def fn(rv, vv):
        _, vjp = jax.vjp(lambda a_, u_: _silu(a_) * u_, rv[0].astype(f32), rv[1].astype(f32))
        da, du = vjp(rv[2].astype(f32))
        return [da, du], []

    (da, du), _ = _rowwise(f"swiglu_bwd_{tag}", fn, [a, u, df], [], [(F, bf16), (F, bf16)], [], sub=16,
                           col_chunk=_tile(F, 512, LANES))
    dh = _mm(du, w_up_t, add=_mm(da, w_gate_t, name=f"ffn_gate_dx_{tag}"), name=f"ffn_up_dx_{tag}")
    dw_gate_t = _mm(da, h, ta=True, out_dtype=bf16, name=f"ffn_gate_dw_{tag}")
    dw_up_t = _mm(du, h, ta=True, out_dtype=bf16, name=f"ffn_up_dw_{tag}")
    return dh, dw_gate_t, dw_up_t, dw_down


def _local_step(x, target, mod, w_in_t, comm, small):
    S, D = x.shape
    di = small["hy_ssm_norm_g"].shape[-1]
    nh = small["hy_dt_bias"].shape[-1]
    cd = small["hy_conv_b"].shape[-1]
    m = [[_row(mod[i, j]) for j in range(6)] for i in range(2)]

    off_q = di + cd + nh
    w_qkv_t = w_in_t[off_q:]
    seg = dict(z=(w_in_t, 0, di), xbc=(w_in_t, di, cd), dt=(w_in_t, di + cd, LANES))
    for i, nm in enumerate(("q0", "q1", "q2", "k", "v")):
        seg[nm] = (w_qkv_t, i * D, D)

    g_mix = [_row(small["norm_mix_g"][i]) for i in range(2)]
    g_ffn = [_row(small["norm_ffn_g"][i]) for i in range(2)]
    conv_w, conv_b = small["hy_conv_w_full"], _row(small["hy_conv_b"][0])
    dt_bias, a_log, d_skip = (_pad_lanes(small[k][0]) for k in ("hy_dt_bias", "hy_a_log", "hy_d_skip"))
    g_ssm = _row(small["hy_ssm_norm_g"][0])
    onehot = jnp.asarray(_bucket_onehot())
    rel_t = small["rel_table"].T
    H = D // HEAD_DIM
    bias = [_exact_mm(rel_t[gi * H:(gi + 1) * H], onehot[gi], name=f"rel_bias_{gi}")
            .reshape(H, 2 * ATT_BLK, ATT_BLK) + _band_penalty() for gi in range(3)]

    h1 = _adaln_fwd(x, g_mix[0], m[0][1], m[0][0], "adaln_mix0")
    proj = {nm: _mm(h1, mat, tb=True, b_rows=(off, cnt), name=f"in_{nm}") for nm, (mat, off, cnt) in seg.items()}
    xbc_pre, xbc = _conv_fwd(proj["xbc"], conv_w, conv_b, silu=True, name="ssm_conv", tr=1024)
    y, hin = _ssd_fwd(xbc, proj["dt"], dt_bias, a_log, d_skip, di, "ssd_fwd")
    (yg,), _ = _rowwise("ssm_gate", lambda rv, vv: ([_gate_f(rv[0], rv[1], vv[0])], []),
                        [y, proj["z"]], [g_ssm], [(di, bf16)], [], sub=16)
    og = [_attn_fwd(proj[f"q{gi}"], proj["k"], proj["v"], bias[gi], d, f"attn_fwd_{gi}")
          for gi, d in enumerate(ATT_DILATIONS)]
    att, att_b, lse_tot = _attn_combine([a for a, _ in og], [_from_blocks(b, LANES) for _, b in og], "attn_combine")
    W = comm.weights(after=att_b)
    w_out_y, w_out_a = W["out"][:di], W["out"][di:]
    mix0 = _mm(att_b, w_out_a, add=_mm(yg, w_out_y, name="out_y"), name="out_a")
    x1 = _resid_fwd(x, m[0][2], mix0, "resid_mix0")
    h2 = _adaln_fwd(x1, g_ffn[0], m[0][4], m[0][3], "adaln_ffn0")
    f0, ffn0_saved = _ffn_fwd(h2, W["gate_t0"], W["up_t0"], W["down0"], "0")
    x2 = _resid_fwd(x1, m[0][5], f0, "resid_ffn0")

    h3 = _adaln_fwd(x2, g_mix[1], m[1][1], m[1][0], "adaln_mix1")
    pw1 = _mm(h3, W["pw1"], bias=_row(small["cv_b_pw1_full"]), name="cv_pw1")
    (u,), _ = _rowwise("cv_glu", lambda rv, vv: ([rv[0] * jax.nn.sigmoid(rv[1])], []),
                       [(pw1, 0, D), (pw1, 1, D)], [], [(D, f32)], [])
    (u2,) = _conv_fwd(u, small["cv_w_dw_full"], _row(small["cv_b_dw_full"]), silu=False, name="cv_dw")
    ln_g, ln_b = _row(small["cv_ln_g_full"]), _row(small["cv_ln_b_full"])
    (u3,), _ = _rowwise("cv_lnsilu", lambda rv, vv: ([_lnsilu_f(rv[0], vv[0], vv[1])], []),
                        [u2], [ln_g, ln_b], [(D, bf16)], [], sub=16)
    mix1 = _mm(u3, W["pw2"], bias=_row(small["cv_b_pw2_full"]), name="cv_pw2")
    x3 = _resid_fwd(x2, m[1][2], mix1, "resid_mix1")
    h4 = _adaln_fwd(x3, g_ffn[1], m[1][4], m[1][3], "adaln_ffn1")
    f1, ffn1_saved = _ffn_fwd(h4, W["gate_t1"], W["up_t1"], W["down1"], "1")
    x4 = _resid_fwd(x3, m[1][5], f1, "resid_ffn1")

    g_fin = _row(small["final_norm_g"])

    def final_fn(rv, vv):
        xv, tv = rv
        yv, vjp = jax.vjp(_rms, xv, vv[0])
        err = yv - tv
        dx, dg = vjp(err / D)
        part = 0.5 * jnp.sum(jnp.mean(err * err, -1, keepdims=True), 0, keepdims=True)
        return [dx], [dg, jnp.broadcast_to(part, (1, LANES))]

    (dx4,), (d_fin, loss) = _rowwise("loss_head", final_fn, [x4, target], [g_fin], [(D, f32)], [D, LANES])

    dmod = [[None] * 6 for _ in range(2)]
    d_norm_mix, d_norm_ffn = [None, None], [None, None]
    big = {}

    df1, (dmod[1][5], _) = _resid_bwd(dx4, f1, m[1][5], "resid_ffn1_bwd")
    dh4, big["gate_t1"], big["up_t1"], big["down1"] = _ffn_bwd(h4, W["gate_t1"], W["up_t1"], W["down1"], ffn1_saved, df1, "1")
    dx3, (d_norm_ffn[1], dmod[1][4], dmod[1][3]) = _adaln_bwd(x3, g_ffn[1], m[1][4], m[1][3], dh4, dx4, "adaln_ffn1_bwd")
    dmix1, (dmod[1][2], d_b_pw2) = _resid_bwd(dx3, mix1, m[1][2], "resid_mix1_bwd")
    du3 = _mm(dmix1, W["pw2"], tb=True, name="cv_pw2_dx")
    big["pw2"] = _mm(u3, dmix1, ta=True, out_dtype=bf16, name="cv_pw2_dw")

    def lnsilu_bwd(rv, vv):
        _, vjp = jax.vjp(_lnsilu_f, rv[0], vv[0], vv[1])
        du, dg, db = vjp(rv[1])
        return [du], [dg, db]

    (du2,), (d_ln_g, d_ln_b) = _rowwise("cv_lnsilu_bwd", lnsilu_bwd, [u2, du3], [ln_g, ln_b], [(D, f32)], [D, D])
    du, d_w_dw, d_b_dw = _conv_bwd(u, small["cv_w_dw_full"], du2, None, silu=False, name="cv_dw_bwd")

    def glu_bwd(rv, vv):
        a, gt, d = rv
        _, vjp = jax.vjp(lambda a_, g_: a_ * jax.nn.sigmoid(g_), a, gt)
        da, dg = vjp(d)
        return [da, dg], [jnp.sum(da, 0, keepdims=True), jnp.sum(dg, 0, keepdims=True)]

    (dpa, dpg), (d_b1a, d_b1g) = _rowwise("cv_glu_bwd", glu_bwd, [(pw1, 0, D), (pw1, 1, D), du], [],
                                           [(D, bf16), (D, bf16)], [D, D], sub=16)
    dpw1 = jnp.concatenate([dpa, dpg], axis=1)
    d_b_pw1 = jnp.concatenate([d_b1a, d_b1g], axis=1)
    dh3 = _mm(dpw1, W["pw1"], tb=True, name="cv_pw1_dx")
    big["pw1"] = _mm(h3, dpw1, ta=True, out_dtype=bf16, name="cv_pw1_dw")
    token = comm.send("layer1", {nm: big[nm] for nm in ("gate_t1", "up_t1", "down1", "pw2", "pw1")})
    dx2, (d_norm_mix[1], dmod[1][1], dmod[1][0]) = _adaln_bwd(x2, g_mix[1], m[1][1], _tie(m[1][0], token), dh3, dx3,
                                                              "adaln_mix1_bwd")

    df0, (dmod[0][5], _) = _resid_bwd(dx2, f0, m[0][5], "resid_ffn0_bwd")
    dh2, big["gate_t0"], big["up_t0"], big["down0"] = _ffn_bwd(h2, W["gate_t0"], W["up_t0"], W["down0"], ffn0_saved, df0, "0")
    dx1, (d_norm_ffn[0], dmod[0][4], dmod[0][3]) = _adaln_bwd(x1, g_ffn[0], m[0][4], m[0][3], dh2, dx2, "adaln_ffn0_bwd")
    dmix0, (dmod[0][2], _) = _resid_bwd(dx1, mix0, m[0][2], "resid_mix0_bwd")
    dyg = _mm(dmix0, w_out_y, tb=True, name="out_y_dx")
    datt = _mm(dmix0, w_out_a, tb=True, name="out_a_dx")
    big["out"] = jnp.concatenate([_mm(yg, dmix0, ta=True, out_dtype=bf16, name="out_y_dw"),
                                  _mm(att_b, dmix0, ta=True, out_dtype=bf16, name="out_a_dw")], axis=0)
    token = comm.send("layer0", {nm: big[nm] for nm in ("gate_t0", "up_t0", "down0", "out")})
    g_ssm = _tie(g_ssm, token)

    def gate_bwd(rv, vv):
        _, vjp = jax.vjp(_gate_f, rv[0], rv[1], vv[0])
        dy_, dz_, dg_ = vjp(rv[2])
        return [dy_, dz_], [dg_]

    (dy, dz), (d_g_ssm,) = _rowwise("ssm_gate_bwd", gate_bwd, [y, proj["z"], dyg], [g_ssm], [(di, f32), (di, bf16)], [di],
                                    sub=16)
    dxbc, ddtraw, d_a_log, d_dskip, d_dt_bias = _ssd_bwd(xbc, proj["dt"], dt_bias, a_log, d_skip, hin, y, dy, di, "ssd_bwd")
    dxbc_pre, d_conv_w, d_conv_b = _conv_bwd(proj["xbc"], conv_w, dxbc, xbc_pre, silu=True, name="ssm_conv_bwd",
                                             dx_dtype=bf16, tr=1024)
    dh1 = None
    early = []
    for nm, dseg in (("z", dz), ("xbc", dxbc_pre), ("dt", ddtraw)):
        mat, off, cnt = seg[nm]
        dh1 = _mm(dseg, mat, b_rows=(off, cnt), add=dh1, name=f"in_{nm}_dx")
        dwp = _mm(dseg, h1, ta=True, out_dtype=bf16, name=f"in_{nm}_dw")
        early.append(dwp[:nh] if nm == "dt" else dwp)
    early = jnp.concatenate(early, axis=0)
    shard_rows = w_in_t.shape[0] // N_DEV
    n_early = off_q // shard_rows
    token = comm.send_rows("in_early", early[:n_early * shard_rows], (0, n_early))
    bias = [_tie(b, token) for b in bias]

    dq, dks, dvs, dbs = [], [], [], []
    lse_heads = lse_tot[:, :H]
    dsum_heads = _head_sums(att, datt, "attn_dsum")[:, :H]
    for gi, d in enumerate(ATT_DILATIONS):
        a, b, c_, e = _attn_bwd(proj[f"q{gi}"], proj["k"], proj["v"], bias[gi], datt,
                                _by_block(lse_heads, d), _by_block(dsum_heads, d), d, f"attn_bwd_{gi}")
        dq.append(a)
        dks.append(b)
        dvs.append(c_)
        dbs.append(e)
    dk = _add3(*dks, "attn_dk")
    dv = _add3(*dvs, "attn_dv")
    d_rel = jnp.concatenate(
        [_exact_mm(dbs[gi].reshape(H, -1), onehot[gi], tb=True, name=f"rel_grad_{gi}") for gi in range(3)], axis=0).T

    late = [early[n_early * shard_rows:]]
    for nm, dseg in (("q0", dq[0]), ("q1", dq[1]), ("q2", dq[2]), ("k", dk), ("v", dv)):
        mat, off, cnt = seg[nm]
        dh1 = _mm(dseg, mat, b_rows=(off, cnt), add=dh1, name=f"in_{nm}_dx")
        late.append(_mm(dseg, h1, ta=True, out_dtype=bf16, name=f"in_{nm}_dw"))
    late = jnp.concatenate(late, axis=0)
    dx0, (d_norm_mix[0], dmod[0][1], dmod[0][0]) = _adaln_bwd(x, g_mix[0], m[0][1], m[0][0], dh1, dx1, "adaln_mix0_bwd")

    smallg = dict(
        loss=loss, dmod=jnp.stack([jnp.concatenate(dmod[i], axis=1)[0] for i in range(2)]),
        norm_mix_g=jnp.concatenate(d_norm_mix, axis=0), norm_ffn_g=jnp.concatenate(d_norm_ffn, axis=0),
        hy_conv_w=d_conv_w, hy_conv_b=d_conv_b, hy_dt_bias=d_dt_bias[:, :nh], hy_a_log=d_a_log[:, :nh],
        hy_d_skip=d_dskip[:, :nh], hy_ssm_norm_g=d_g_ssm, rel_table=d_rel,
        cv_b_pw1=d_b_pw1, cv_w_dw=d_w_dw, cv_b_dw=d_b_dw, cv_ln_g=d_ln_g, cv_ln_b=d_ln_b, cv_b_pw2=d_b_pw2,
        final_norm_g=d_fin)
    return dx0, (late, n_early), smallg


SMALL_GRAD_ORDER = ("loss", "dmod", "norm_mix_g", "norm_ffn_g", "hy_conv_w", "hy_conv_b", "hy_dt_bias", "hy_a_log",
                    "hy_d_skip", "hy_ssm_norm_g", "rel_table", "cv_b_pw1", "cv_w_dw", "cv_b_dw", "cv_ln_g", "cv_ln_b",
                    "cv_b_pw2", "final_norm_g")


def kernel(x, c, ada_w, ada_b, norm_mix_g, norm_ffn_g, hy_w_in, hy_conv_w, hy_conv_b, hy_dt_bias, hy_a_log, hy_d_skip, hy_ssm_norm_g, hy_w_out, rel_table, cv_w_pw1, cv_b_pw1, cv_w_dw, cv_b_dw, cv_ln_g, cv_ln_b, cv_w_pw2, cv_b_pw2, ffn_w_gate, ffn_w_up, ffn_w_down, final_norm_g, loss_target, m_ada_w, m_ada_b, m_norm_mix_g, m_norm_ffn_g, m_hy_w_in, m_hy_conv_w, m_hy_conv_b, m_hy_dt_bias, m_hy_a_log, m_hy_d_skip, m_hy_ssm_norm_g, m_hy_w_out, m_rel_table, m_cv_w_pw1, m_cv_b_pw1, m_cv_w_dw, m_cv_b_dw, m_cv_ln_g, m_cv_ln_b, m_cv_w_pw2, m_cv_b_pw2, m_ffn_w_gate, m_ffn_w_up, m_ffn_w_down, m_final_norm_g, v_ada_w, v_ada_b, v_norm_mix_g, v_norm_ffn_g, v_hy_w_in, v_hy_conv_w, v_hy_conv_b, v_hy_dt_bias, v_hy_a_log, v_hy_d_skip, v_hy_ssm_norm_g, v_hy_w_out, v_rel_table, v_cv_w_pw1, v_cv_b_pw1, v_cv_w_dw, v_cv_b_dw, v_cv_ln_g, v_cv_ln_b, v_cv_w_pw2, v_cv_b_pw2, v_ffn_w_gate, v_ffn_w_up, v_ffn_w_down, v_final_norm_g):
    names = ("ada_w", "ada_b", "norm_mix_g", "norm_ffn_g", "hy_w_in", "hy_conv_w", "hy_conv_b", "hy_dt_bias", "hy_a_log",
             "hy_d_skip", "hy_ssm_norm_g", "hy_w_out", "rel_table", "cv_w_pw1", "cv_b_pw1", "cv_w_dw", "cv_b_dw", "cv_ln_g",
             "cv_ln_b", "cv_w_pw2", "cv_b_pw2", "ffn_w_gate", "ffn_w_up", "ffn_w_down", "final_norm_g")
    w = dict(zip(names, (ada_w, ada_b, norm_mix_g, norm_ffn_g, hy_w_in, hy_conv_w, hy_conv_b, hy_dt_bias, hy_a_log, hy_d_skip,
                         hy_ssm_norm_g, hy_w_out, rel_table, cv_w_pw1, cv_b_pw1, cv_w_dw, cv_b_dw, cv_ln_g, cv_ln_b, cv_w_pw2,
                         cv_b_pw2, ffn_w_gate, ffn_w_up, ffn_w_down, final_norm_g)))
    mom = dict(zip(names, (m_ada_w, m_ada_b, m_norm_mix_g, m_norm_ffn_g, m_hy_w_in, m_hy_conv_w, m_hy_conv_b, m_hy_dt_bias,
                           m_hy_a_log, m_hy_d_skip, m_hy_ssm_norm_g, m_hy_w_out, m_rel_table, m_cv_w_pw1, m_cv_b_pw1, m_cv_w_dw,
                           m_cv_b_dw, m_cv_ln_g, m_cv_ln_b, m_cv_w_pw2, m_cv_b_pw2, m_ffn_w_gate, m_ffn_w_up, m_ffn_w_down,
                           m_final_norm_g)))
    vel = dict(zip(names, (v_ada_w, v_ada_b, v_norm_mix_g, v_norm_ffn_g, v_hy_w_in, v_hy_conv_w, v_hy_conv_b, v_hy_dt_bias,
                           v_hy_a_log, v_hy_d_skip, v_hy_ssm_norm_g, v_hy_w_out, v_rel_table, v_cv_w_pw1, v_cv_b_pw1, v_cv_w_dw,
                           v_cv_b_dw, v_cv_ln_g, v_cv_ln_b, v_cv_w_pw2, v_cv_b_pw2, v_ffn_w_gate, v_ffn_w_up, v_ffn_w_down,
                           v_final_norm_g)))
    S, D = x.shape[1], x.shape[2]
    ax, ay, ac = lax.axis_index("x"), lax.axis_index("y"), lax.axis_index("c")
    me = 4 * ax + 2 * ay + ac
    nmod = ada_w.shape[2]

    w2 = _shards_2d(w)
    big_names = list(w2)
    sharded_small = ("hy_conv_w", "cv_b_pw1", "cv_w_dw", "cv_b_dw", "cv_ln_g", "cv_ln_b", "cv_b_pw2")
    vp = _VecPack([c.shape] + [w[nm].shape for nm in sharded_small])
    g_in, sg = _all_gather([w2["in_t"].astype(bf16), vp.pack([c] + [w[nm] for nm in sharded_small])], "gather_w_in")
    w_in_t = _full_from_blocks("in_t", g_in)
    parts = vp.unpack_stacked(sg)
    c_all = parts[0][:, 0]
    small = {k: w[k] for k in ("norm_mix_g", "norm_ffn_g", "hy_conv_b", "hy_dt_bias", "hy_a_log", "hy_d_skip",
                               "hy_ssm_norm_g", "rel_table", "final_norm_g")}
    for p, nm in zip(parts[1:], sharded_small):
        p = p[:, 0]
        p = jnp.moveaxis(p, 0, -2)
        small[nm + "_full"] = p.reshape(p.shape[:-2] + (N_DEV * p.shape[-1],))

    (cs_all,), _ = _rowwise("ada_silu", lambda rv, vv: ([_silu(rv[0])], []), [c_all], [], [(D, f32)], [])
    b_mine = lax.dynamic_slice_in_dim(ada_b, me * nmod, nmod, axis=1)
    mod_part = jnp.stack([_mm(cs_all, ada_w[i], bias=b_mine[i:i + 1], name=f"ada_mod_{i}") for i in range(2)])
    (mod_all,) = _all_gather([mod_part.reshape(2 * N_DEV, nmod)], "gather_mod")
    mod_all = mod_all.reshape(N_DEV, 2, N_DEV, nmod)
    mod_mine = lax.dynamic_index_in_dim(mod_all, me, axis=2, keepdims=False)
    mod = jnp.transpose(mod_mine, (1, 0, 2)).reshape(2, 6, D)
    comm = _Overlap({nm: w2[nm].astype(bf16) for nm in big_names if nm != "in_t"}, me, after=[mod, w_in_t])
    mod = _tie(mod, comm.token)

    dx0, (d_in_late, n_early), sgrad = _local_step(x[0], loss_target[0], mod, w_in_t, comm, small)
    comm.send_rows("in_late", d_in_late, (n_early, N_DEV))

    gp = _VecPack([sgrad[k].shape for k in SMALL_GRAD_ORDER])
    (g_all,) = _all_gather([gp.pack([sgrad[k] for k in SMALL_GRAD_ORDER])], "gather_small_grads")
    tot = dict(zip(SMALL_GRAD_ORDER, gp.unpack(_sum_slots(g_all, "sum_small_grads"))))
    dmod_all = gp.unpack_stacked(g_all, only=SMALL_GRAD_ORDER.index("dmod"))
    loss = tot["loss"][0, 0]

    grads = {}
    dmod_mine = lax.dynamic_slice_in_dim(dmod_all, me * nmod, nmod, axis=2)
    grads["ada_w"] = jnp.stack([_mm(cs_all, dmod_mine[:, i], ta=True, name=f"ada_w_grad_{i}") for i in range(2)])
    grads["ada_b"] = tot["dmod"]
    grads["norm_mix_g"], grads["norm_ffn_g"] = tot["norm_mix_g"], tot["norm_ffn_g"]
    grads["hy_conv_b"] = tot["hy_conv_b"]
    grads["hy_dt_bias"] = tot["hy_dt_bias"]
    grads["hy_a_log"] = tot["hy_a_log"]
    grads["hy_d_skip"] = tot["hy_d_skip"]
    grads["hy_ssm_norm_g"] = tot["hy_ssm_norm_g"]
    grads["rel_table"] = tot["rel_table"]
    grads["final_norm_g"] = tot["final_norm_g"][0]
    for nm in sharded_small:
        n = w[nm].shape[-1]
        grads[nm] = lax.dynamic_slice_in_dim(tot[nm], me * n, n, axis=1).reshape(w[nm].shape)

    delta, new_m, new_v = {}, {}, {}
    shp = ada_w.shape
    two = lambda t: t.reshape(-1, shp[-1])
    d_, m_, v_ = _adamw(two(ada_w), two(grads["ada_w"]), two(m_ada_w), two(v_ada_w), "adamw_ada_w")
    delta["ada_w"], new_m["ada_w"], new_v["ada_w"] = d_.reshape(shp), m_.reshape(shp), v_.reshape(shp)
    rest = [nm for nm in names if nm not in BIG_PARAMS and nm != "ada_w"]
    sp = _VecPack([w[nm].shape for nm in rest])
    packs = [sp.pack([t[nm] for nm in rest]) for t in (w, grads, mom, vel)]
    ds_, ms_, vs_ = _adamw(*packs, "adamw_small")
    for nm, a, b, e in zip(rest, sp.unpack(ds_), sp.unpack(ms_), sp.unpack(vs_)):
        delta[nm], new_m[nm], new_v[nm] = a, b, e

    m2, v2 = _shards_2d(mom), _shards_2d(vel)
    g2, d2, nm2, nv2 = {}, {}, {}, {}
    after = d_
    slots = {}
    for tag in ("layer1", "layer0", "in_early", "in_late"):
        slots.update(comm.received(tag, after))
        if tag.startswith("in_"):
            continue
        for nm in comm.sent[tag][0]:
            d2[nm], nm2[nm], nv2[nm], g2[nm] = _adamw(w2[nm], slots[nm], m2[nm], v2[nm], f"adamw_{nm}")
            after = g2[nm]
    stored = lambda t: jnp.transpose(t, (2, 0, 1))
    in_slots = jnp.where(me < n_early, slots["in_early"], slots["in_late"])
    d_in, m_in, v_in, g_in = _adamw(stored(hy_w_in), in_slots, stored(m_hy_w_in), stored(v_hy_w_in), "adamw_in_t")
    for dst, part, t in ((grads, g2, g_in), (delta, d2, d_in), (new_m, nm2, m_in), (new_v, nv2, v_in)):
        dst.update(_unshard_2d(part))
        dst["hy_w_in"] = jnp.transpose(t, (1, 2, 0))

    return (loss, dx0[None], *[grads[n] for n in names], *[delta[n] for n in names],
            *[new_m[n] for n in names], *[new_v[n] for n in names])
```

```python
import functools
import math

import numpy as np
import jax
import jax.numpy as jnp
from jax import lax
from jax.experimental import pallas as pl
from jax.experimental.pallas import tpu as pltpu

f32 = jnp.float32
bf16 = jnp.bfloat16
EPS = 1e-6
N_DEV = 8
LANES = 128
SSM_STATE = 128
SSM_CHUNK = 128
SSM_GROUPS = 4
HEAD_DIM = 64
ATT_BLK = 128
ATT_DILATIONS = (1, 4, 16)
REL_BUCKETS = 32
REL_MAX_DIST = 2048
ADAM_LR, ADAM_B1, ADAM_B2, ADAM_EPS, ADAM_WD, ADAM_STEP = 0.001, 0.9, 0.999, 1e-08, 0.01, 10
PACK_ROW_TILE = 256
MESH = pl.DeviceIdType.MESH
VMEM_LIMIT = 48 * 1024 * 1024


def _sds(shape, dtype=f32):
    return jax.ShapeDtypeStruct(tuple(shape), dtype)


def _tile(n, cap, mult):
    best = None
    t = mult
    while t <= min(n, cap):
        if n % t == 0:
            best = t
        t += mult
    return best if best is not None else n


def _params(sem):
    return pltpu.CompilerParams(dimension_semantics=sem, vmem_limit_bytes=VMEM_LIMIT)


def _mm(a, b, *, name, ta=False, tb=False, b_rows=None, bias=None, add=None, out_dtype=f32,
        tm_cap=512, tn_cap=1536, tk_cap=8192):
    if ta:
        K, M = a.shape
    else:
        M, K = a.shape
    off, cnt = b_rows if b_rows is not None else (0, b.shape[0])
    if tb:
        N, K2 = cnt, b.shape[1]
    else:
        K2, N = cnt, b.shape[1]
    assert K == K2, (a.shape, b.shape, ta, tb, b_rows)
    if ta and a.dtype == f32:
        tm_cap = min(tm_cap, 256)
    tm = _tile(M, tm_cap, LANES)
    tn = _tile(math.gcd(off, N) if tb else N, tn_cap, LANES)
    tk = _tile(K if tb else math.gcd(off, K), tk_cap, LANES)
    assert N % tn == 0 and K % tk == 0 and off % (tn if tb else tk) == 0, (name, off, N, K, tn, tk)
    nk = K // tk
    jo, ko = (off // tn, 0) if tb else (0, off // tk)
    has_bias, has_add = bias is not None, add is not None
    dn = (((0 if ta else 1,), (1 if tb else 0,)), ((), ()))

    def body(*refs):
        a_ref, b_ref = refs[0], refs[1]
        pos = 2
        bias_ref = add_ref = None
        if has_bias:
            bias_ref = refs[pos]
            pos += 1
        if has_add:
            add_ref = refs[pos]
            pos += 1
        o_ref = refs[pos]
        k = pl.program_id(2)
        part = lax.dot_general(a_ref[...].astype(bf16), b_ref[...].astype(bf16), dn, preferred_element_type=f32)

        def finish(r):
            if has_bias:
                r = r + bias_ref[...]
            if has_add:
                r = r + add_ref[...]
            o_ref[...] = r.astype(o_ref.dtype)

        if nk == 1:
            finish(part)
        else:
            acc_ref = refs[pos + 1]

            @pl.when(k == 0)
            def _():
                acc_ref[...] = part

            @pl.when((k > 0) & (k < nk - 1))
            def _():
                acc_ref[...] += part

            @pl.when(k == nk - 1)
            def _():
                finish(acc_ref[...] + part)

    in_specs = [
        pl.BlockSpec((tk, tm), lambda i, j, k: (k, i)) if ta else pl.BlockSpec((tm, tk), lambda i, j, k: (i, k)),
        pl.BlockSpec((tn, tk), lambda i, j, k: (j + jo, k)) if tb else pl.BlockSpec((tk, tn), lambda i, j, k: (k + ko, j)),
    ]
    args = [a, b]
    if has_bias:
        in_specs.append(pl.BlockSpec((1, tn), lambda i, j, k: (0, j)))
        args.append(bias)
    if has_add:
        in_specs.append(pl.BlockSpec((tm, tn), lambda i, j, k: (i, j)))
        args.append(add)
    return pl.pallas_call(
        body, name=name, grid=(M // tm, N // tn, nk), in_specs=in_specs,
        out_specs=pl.BlockSpec((tm, tn), lambda i, j, k: (i, j)), out_shape=_sds((M, N), out_dtype),
        scratch_shapes=[pltpu.VMEM((tm, tn), f32)] if nk > 1 else [],
        compiler_params=_params(("parallel", "parallel", "arbitrary")),
    )(*args)


def _rowwise(name, fn, rows, vecs, out_rows, out_accs, *, tr_cap=256, sub=8, col_chunk=None):
    rows = [r if isinstance(r, tuple) else (r, 0, r.shape[1]) for r in rows]
    R = rows[0][0].shape[0]
    tr = _tile(R, tr_cap, 8)
    sub = sub if tr % sub == 0 else tr
    n_r, n_v, n_or, n_oa = len(rows), len(vecs), len(out_rows), len(out_accs)

    def body(*refs):
        row_refs = refs[:n_r]
        vec_refs = refs[n_r:n_r + n_v]
        orow_refs = refs[n_r + n_v:n_r + n_v + n_or]
        oacc_refs = refs[n_r + n_v + n_or:]
        vv = [r[...] for r in vec_refs]

        n_sub = tr // sub
        together = 4 if n_sub % 4 == 0 else 1

        def step(s, accs):
            for t in range(together):
                sl = pl.ds(pl.multiple_of((s * together + t) * sub, sub), sub)
                if col_chunk is None:
                    ro, ao = fn([r[sl, :] for r in row_refs], vv)
                    for o_ref, o in zip(orow_refs, ro):
                        o_ref[sl, :] = o.astype(o_ref.dtype)
                    accs = tuple(x + y for x, y in zip(accs, ao))
                else:
                    for c0 in range(0, rows[0][2], col_chunk):
                        cs_ = pl.ds(c0, col_chunk)
                        ro, _ = fn([r[sl, cs_] for r in row_refs], vv)
                        for o_ref, o in zip(orow_refs, ro):
                            o_ref[sl, cs_] = o.astype(o_ref.dtype)
            return accs

        accs = lax.fori_loop(0, n_sub // together, step, tuple(jnp.zeros((1, w), f32) for w in out_accs))
        if n_oa:
            @pl.when(pl.program_id(0) == 0)
            def _():
                for ref in oacc_refs:
                    ref[...] = jnp.zeros_like(ref)

            for ref, x in zip(oacc_refs, accs):
                ref[...] += x

    in_specs = [pl.BlockSpec((tr, w), functools.partial(lambda i, cb: (i, cb), cb=cb)) for (_, cb, w) in rows]
    in_specs += [pl.BlockSpec((1, v.shape[1]), lambda i: (0, 0)) for v in vecs]
    out_specs = [pl.BlockSpec((tr, w), lambda i: (i, 0)) for (w, _) in out_rows]
    out_specs += [pl.BlockSpec((1, w), lambda i: (0, 0)) for w in out_accs]
    out_shape = [_sds((R, w), dt) for (w, dt) in out_rows] + [_sds((1, w)) for w in out_accs]
    res = pl.pallas_call(
        body, name=name, grid=(R // tr,), in_specs=in_specs, out_specs=out_specs, out_shape=out_shape,
        compiler_params=_params(("arbitrary",)),
    )(*[r[0] for r in rows], *vecs)
    return res[:n_or], res[n_or:]


def _silu(x):
    return x * jax.nn.sigmoid(x)


def _rms(x, g):
    return x * lax.rsqrt(jnp.mean(x * x, -1, keepdims=True) + EPS) * g


def _adaln_f(x, g, sc, sh):
    return _rms(x, g) * (1.0 + sc) + sh


def _gate_f(y, z, g):
    return _rms(y * _silu(z), g)


def _lnsilu_f(u, g, b):
    mu = jnp.mean(u, -1, keepdims=True)
    var = jnp.mean(jnp.square(u - mu), -1, keepdims=True)
    return _silu((u - mu) * lax.rsqrt(var + EPS) * g + b)


def _adaln_fwd(x, g, sc, sh, name):
    (h,), _ = _rowwise(name, lambda rv, vv: ([_adaln_f(rv[0], *vv)], []), [x], [g, sc, sh], [(x.shape[1], bf16)], [],
                       sub=16)
    return h


def _adaln_bwd(x, g, sc, sh, dh, dres, name):
    def fn(rv, vv):
        xv, dhv, drv = rv
        _, vjp = jax.vjp(_adaln_f, xv, *vv)
        dx, dg, dsc, dsh = vjp(dhv)
        return [dx + drv], [dg, dsc, dsh]
    w = x.shape[1]
    (dx,), accs = _rowwise(name, fn, [x, dh, dres], [g, sc, sh], [(w, f32)], [w, w, w])
    return dx, accs


def _resid_adaln_fwd(x, gate, mix, g, sc, sh, name):
    def fn(rv, vv):
        xn = rv[0] + vv[0] * rv[1]
        return [xn, _adaln_f(xn, vv[1], vv[2], vv[3])], []
    w = x.shape[1]
    (xn, h), _ = _rowwise(name, fn, [x, mix], [gate, g, sc, sh], [(w, f32), (w, bf16)], [], sub=16)
    return xn, h


def _adaln_resid_bwd(x, g, sc, sh, dh, dres, mix, gate, name):
    def fn(rv, vv):
        xv, dhv, drv, mv = rv
        _, vjp = jax.vjp(_adaln_f, xv, vv[0], vv[1], vv[2])
        dx, dg, dsc, dsh = vjp(dhv)
        dx = dx + drv
        dm = vv[3] * dx
        return [dx, dm], [dg, dsc, dsh, jnp.sum(dx * mv, 0, keepdims=True), jnp.sum(dm, 0, keepdims=True)]
    w = x.shape[1]
    (dx, dmix), accs = _rowwise(name, fn, [x, dh, dres, mix], [g, sc, sh, gate], [(w, f32), (w, bf16)], [w] * 5, sub=16)
    return dx, dmix, accs


def _add3(a, b, c, name):
    (y,), _ = _rowwise(name, lambda rv, vv: ([rv[0] + rv[1] + rv[2]], []), [a, b, c], [], [(a.shape[1], bf16)], [],
                       sub=16)
    return y


CONV_HALO = 32
CONV_ROWS = 64


def _conv_fwd(x, w, b, *, silu, name, tr=512):
    S, C = x.shape
    K = w.shape[0]
    H = CONV_HALO
    assert K - 1 <= H and S % tr == 0 and tr % H == 0 and C % LANES == 0
    nh = tr // H

    def body(xp_ref, xc_ref, w_ref, b_ref, *rest):
        outs, scr = rest[:-1], rest[-1]
        i = pl.program_id(1)
        scr[pl.ds(0, H), :] = jnp.where(i > 0, xp_ref[...], 0.0)
        scr[pl.ds(H, tr), :] = xc_ref[...]
        taps = [w_ref[pl.ds(k, 1), :] for k in range(K)]
        for c0 in range(0, tr, CONV_ROWS):
            acc = jnp.zeros((CONV_ROWS, LANES), f32) + b_ref[...]
            for k in range(K):
                acc = acc + scr[pl.ds(c0 + H - (K - 1) + k, CONV_ROWS), :] * taps[k]
            outs[0][pl.ds(c0, CONV_ROWS), :] = acc.astype(outs[0].dtype)
            if silu:
                outs[1][pl.ds(c0, CONV_ROWS), :] = _silu(acc)

    n_out = 2 if silu else 1
    return pl.pallas_call(
        body, name=name, grid=(C // LANES, S // tr),
        in_specs=[pl.BlockSpec((H, LANES), lambda j, i: (jnp.maximum(i * nh - 1, 0), j)),
                  pl.BlockSpec((tr, LANES), lambda j, i: (i, j)),
                  pl.BlockSpec((K, LANES), lambda j, i: (0, j)),
                  pl.BlockSpec((1, LANES), lambda j, i: (0, j))],
        out_specs=[pl.BlockSpec((tr, LANES), lambda j, i: (i, j))] * n_out,
        out_shape=[_sds((S, C), bf16), _sds((S, C))] if silu else [_sds((S, C))],
        scratch_shapes=[pltpu.VMEM((tr + H, LANES), f32)],
        compiler_params=_params(("parallel", "arbitrary")),
    )(x, x, w, b)


def _conv_bwd(x, w, dact, pre, *, silu, name, dx_dtype=f32, tr=512):
    S, C = x.shape
    K = w.shape[0]
    H = CONV_HALO
    nh = tr // H
    n_i = S // tr
    kp = -(-K // 8) * 8

    def dsilu(p):
        s = jax.nn.sigmoid(p)
        return s * (1.0 + p * (1.0 - s))

    def body(*refs):
        if silu:
            xp_ref, xc_ref, w_ref, dc_ref, dn_ref, pc_ref, pn_ref, dx_ref, dw_ref, db_ref, xs, ds = refs
        else:
            xp_ref, xc_ref, w_ref, dc_ref, dn_ref, dx_ref, dw_ref, db_ref, xs, ds = refs
        i = pl.program_id(1)
        xs[pl.ds(0, H), :] = jnp.where(i > 0, xp_ref[...], 0.0)
        xs[pl.ds(H, tr), :] = xc_ref[...]
        dcur = dc_ref[...]
        dnext = dn_ref[...]
        if silu:
            dcur = dcur * dsilu(pc_ref[...].astype(f32))
            dnext = dnext * dsilu(pn_ref[...].astype(f32))
        ds[pl.ds(0, tr), :] = dcur
        ds[pl.ds(tr, H), :] = jnp.where(i < n_i - 1, dnext, 0.0)
        taps = [w_ref[pl.ds(k, 1), :] for k in range(K)]
        fold = lambda t: jnp.sum(t.reshape(CONV_ROWS // 8, 8, LANES), axis=0)
        dw_parts = [jnp.zeros((8, LANES), f32) for _ in range(K)]
        db_part = jnp.zeros((8, LANES), f32)
        for c0 in range(0, tr, CONV_ROWS):
            acc = jnp.zeros((CONV_ROWS, LANES), f32)
            d_c = ds[pl.ds(c0, CONV_ROWS), :]
            for k in range(K):
                acc = acc + ds[pl.ds(c0 + K - 1 - k, CONV_ROWS), :] * taps[k]
                dw_parts[k] = dw_parts[k] + fold(d_c * xs[pl.ds(c0 + H - (K - 1) + k, CONV_ROWS), :])
            db_part = db_part + fold(d_c)
            dx_ref[pl.ds(c0, CONV_ROWS), :] = acc.astype(dx_ref.dtype)

        @pl.when(i == 0)
        def _():
            dw_ref[...] = jnp.zeros_like(dw_ref)
            db_ref[...] = jnp.zeros_like(db_ref)

        for k in range(K):
            dw_ref[pl.ds(k, 1), :] += jnp.sum(dw_parts[k], 0, keepdims=True)
        db_ref[...] += jnp.sum(db_part, 0, keepdims=True)

    prev = pl.BlockSpec((H, LANES), lambda j, i: (jnp.maximum(i * nh - 1, 0), j))
    cur = pl.BlockSpec((tr, LANES), lambda j, i: (i, j))
    nxt = pl.BlockSpec((H, LANES), lambda j, i: (jnp.minimum((i + 1) * nh, n_i * nh - 1), j))
    in_specs = [prev, cur, pl.BlockSpec((K, LANES), lambda j, i: (0, j)), cur, nxt]
    args = [x, x, w, dact, dact]
    if silu:
        in_specs += [cur, nxt]
        args += [pre, pre]
    dx, dw, db = pl.pallas_call(
        body, name=name, grid=(C // LANES, n_i), in_specs=in_specs,
        out_specs=[cur, pl.BlockSpec((kp, LANES), lambda j, i: (0, j)), pl.BlockSpec((1, LANES), lambda j, i: (0, j))],
        out_shape=[_sds((S, C), dx_dtype), _sds((kp, C)), _sds((1, C))],
        scratch_shapes=[pltpu.VMEM((tr + H, LANES), f32), pltpu.VMEM((tr + H, LANES), f32)],
        compiler_params=_params(("parallel", "arbitrary")),
    )(*args)
    return dx, dw[:K], db


def _dot(a, b):
    return jnp.dot(a.astype(bf16), b.astype(bf16), preferred_element_type=f32)


def _dot_nt(a, b):
    return lax.dot_general(a.astype(bf16), b.astype(bf16), (((1,), (1,)), ((), ())), preferred_element_type=f32)


def _dot_tn(a, b):
    return lax.dot_general(a.astype(bf16), b.astype(bf16), (((0,), (0,)), ((), ())), preferred_element_type=f32)


def _softplus(x):
    return jnp.maximum(x, 0.0) + jnp.log(1.0 + jnp.exp(-jnp.abs(x)))


def _tri(q):
    i = lax.broadcasted_iota(jnp.int32, (q, q), 0)
    j = lax.broadcasted_iota(jnp.int32, (q, q), 1)
    return i >= j


def _ssd_prep(dtraw, dt_bias, a_log):
    q = dtraw.shape[0]
    dt = _softplus(dtraw + dt_bias)
    A = -jnp.exp(a_log)
    tri = _tri(q)
    cs = jnp.dot(tri.astype(f32), dt * A, preferred_element_type=f32, precision=lax.Precision.HIGHEST)
    return dt, A, cs, cs.T, tri


def _expand(cols, h0, n, width):
    q = cols.shape[0]
    return jnp.concatenate([jnp.broadcast_to(cols[:, h0 + r:h0 + r + 1], (q, width)) for r in range(n)], axis=1)


def _ssd_fwd(xbc, dtraw, dt_bias, a_log, d_skip, di, name):
    S, CD = xbc.shape
    Q, N, G = SSM_CHUNK, SSM_STATE, SSM_GROUPS
    nc = S // Q
    nh = di // HEAD_DIM
    R = nh // G
    gw = R * HEAD_DIM

    def body(xbc_ref, dt_ref, bias_ref, alog_ref, dsk_ref, y_ref, hin_ref, state):
        c = pl.program_id(0)

        @pl.when(c == 0)
        def _():
            state[...] = jnp.zeros_like(state)

        hin_ref[...] = state[...]
        dt, A, cs, csT, tri = _ssd_prep(dt_ref[...], bias_ref[...], alog_ref[...])
        dsk = dsk_ref[...]
        ecs = jnp.exp(cs)
        dend = jnp.exp(cs[Q - 1:Q, :] - cs)
        elast = jnp.exp(cs[Q - 1:Q, :])
        for g in range(G):
            h0 = g * R
            Bg = xbc_ref[:, pl.ds(di + g * N, N)]
            Cg = xbc_ref[:, pl.ds(di + G * N + g * N, N)]
            xg = xbc_ref[:, pl.ds(g * gw, gw)]
            Hg = state[pl.ds(g * gw, gw), :]
            Gm = _dot_nt(Cg, Bg)
            xdt = xg * _expand(dt, h0, R, HEAD_DIM)
            yoff = _dot_nt(Cg, Hg) * _expand(ecs, h0, R, HEAD_DIM)
            ys = []
            for r in range(R):
                h = h0 + r
                L = jnp.exp(jnp.where(tri, cs[:, h:h + 1] - csT[h:h + 1, :], -jnp.inf))
                ys.append(_dot(Gm * L, xdt[:, r * HEAD_DIM:(r + 1) * HEAD_DIM]))
            y = jnp.concatenate(ys, axis=1) + yoff + xg * _expand(dsk, h0, R, HEAD_DIM)
            y_ref[:, pl.ds(g * gw, gw)] = y
            hnew = _dot_tn(xdt * _expand(dend, h0, R, HEAD_DIM), Bg)
            escale = jnp.concatenate([jnp.broadcast_to(elast[:, h0 + r:h0 + r + 1], (HEAD_DIM, N)) for r in range(R)], axis=0)
            state[pl.ds(g * gw, gw), :] = escale * Hg + hnew

    vec = pl.BlockSpec((1, LANES), lambda c: (0, 0))
    return pl.pallas_call(
        body, name=name, grid=(nc,),
        in_specs=[pl.BlockSpec((Q, CD), lambda c: (c, 0)), pl.BlockSpec((Q, LANES), lambda c: (c, 0)), vec, vec, vec],
        out_specs=[pl.BlockSpec((Q, di), lambda c: (c, 0)), pl.BlockSpec((None, di, N), lambda c: (c, 0, 0))],
        out_shape=[_sds((S, di)), _sds((nc, di, N))],
        scratch_shapes=[pltpu.VMEM((di, N), f32)],
        compiler_params=_params(("arbitrary",)),
    )(xbc, dtraw, dt_bias, a_log, d_skip)


def _dot_exact(a, b):
    bb = b.astype(bf16)
    hi = a.astype(bf16)
    rest = a - hi.astype(f32)
    mid = rest.astype(bf16)
    low = (rest - mid.astype(f32)).astype(bf16)
    one_pass = lambda t: jnp.dot(t, bb, preferred_element_type=f32)
    return one_pass(hi) + one_pass(mid) + one_pass(low)


def _ssd_bwd(xbc, dtraw, dt_bias, a_log, d_skip, hin, y, dy, di, name):
    S, CD = xbc.shape
    Q, N, G = SSM_CHUNK, SSM_STATE, SSM_GROUPS
    nc = S // Q
    nh = di // HEAD_DIM
    R = nh // G
    gw = R * HEAD_DIM
    P = HEAD_DIM
    head_of_col = jnp.asarray((np.arange(di)[:, None] // P == np.arange(LANES)[None, :]).astype(np.float32))
    dsk_wide = jnp.repeat(d_skip[0, :nh], P)[None]

    def body(xbc_ref, dt_ref, bias_ref, alog_ref, dskw_ref, hoc_ref, hin_ref, y_ref, dy_ref,
             dxbc_ref, ddt_ref, dA_ref, ddsk_ref, dtb_ref, dstate, dxdt_all, tend_all, yoff_all, colterm_all):
        c = pl.program_id(0)

        @pl.when(c == 0)
        def _():
            dstate[...] = jnp.zeros_like(dstate)
            dA_ref[...] = jnp.zeros_like(dA_ref)
            ddsk_ref[...] = jnp.zeros_like(ddsk_ref)
            dtb_ref[...] = jnp.zeros_like(dtb_ref)

        dtraw_v = dt_ref[...]
        dt, A, cs, csT, tri = _ssd_prep(dtraw_v, bias_ref[...], alog_ref[...])
        tri_t = jnp.logical_not(tri) | (lax.broadcasted_iota(jnp.int32, (Q, Q), 0) == lax.broadcasted_iota(jnp.int32, (Q, Q), 1))
        ecs = jnp.exp(cs)
        dend = jnp.exp(cs[Q - 1:Q, :] - cs)
        elast = jnp.exp(cs[Q - 1:Q, :])
        hoc = hoc_ref[...]
        state_dot = jnp.sum(_dot_exact(dstate[...] * hin_ref[...], jnp.ones((N, LANES), f32)) * hoc, 0, keepdims=True) * elast
        for g in range(G):
            h0 = g * R
            Bg = xbc_ref[:, pl.ds(di + g * N, N)]
            Cg = xbc_ref[:, pl.ds(di + G * N + g * N, N)]
            xg = xbc_ref[:, pl.ds(g * gw, gw)]
            dyg = dy_ref[:, pl.ds(g * gw, gw)]
            Hg = hin_ref[pl.ds(g * gw, gw), :]
            dHg = dstate[pl.ds(g * gw, gw), :]
            dt_e = _expand(dt, h0, R, P)
            ecs_e = _expand(ecs, h0, R, P)
            dend_e = _expand(dend, h0, R, P)
            cols = pl.ds(g * gw, gw)
            Gm = _dot_nt(Cg, Bg)
            Gm_t = _dot_nt(Bg, Cg)
            xdt = xg * dt_e
            dye = dyg * ecs_e
            bdh = _dot_nt(Bg, dHg)
            dC = _dot(dye, Hg)
            dB = _dot(xdt * dend_e, dHg)
            dHin = _dot_tn(dye, Cg)
            dxdt_state = dend_e * bdh
            end_term = xdt * dxdt_state
            tend_all[:, cols] = end_term
            yoff_all[:, cols] = _dot_nt(Cg, Hg) * ecs_e
            dG = jnp.zeros((Q, Q), f32)
            dxd = []
            for r in range(R):
                h = h0 + r
                sl = slice(r * P, (r + 1) * P)
                seg = cs[:, h:h + 1] - csT[h:h + 1, :]
                L = jnp.exp(jnp.where(tri, seg, -jnp.inf))
                L_t = jnp.exp(jnp.where(tri_t, -seg, -jnp.inf))
                dyh = dyg[:, sl]
                dG = dG + _dot_nt(dyh, xdt[:, sl]) * L
                dxd.append(_dot(Gm_t * L_t, dyh))
            dxdt_diag = jnp.concatenate(dxd, axis=1)
            dxdt = dxdt_diag + dxdt_state
            dxdt_all[:, cols] = dxdt
            colterm_all[:, cols] = xdt.astype(bf16).astype(f32) * dxdt_diag + end_term
            dxbc_ref[:, cols] = dxdt * dt_e + dyg * dskw_ref[:, cols]
            dxbc_ref[:, pl.ds(di + g * N, N)] = dB + _dot_tn(dG, Cg)
            dxbc_ref[:, pl.ds(di + G * N + g * N, N)] = dC + _dot(dG, Bg)
            escale = jnp.concatenate([jnp.broadcast_to(elast[:, h0 + r:h0 + r + 1], (P, N)) for r in range(R)], axis=0)
            dstate[pl.ds(g * gw, gw), :] = escale * dHg + dHin
        xs = xbc_ref[:, pl.ds(0, di)]
        dyv = dy_ref[...]
        yoff = yoff_all[...]
        y_diag = y_ref[...] - dskw_ref[...] * xs - yoff
        rs_y = _dot_exact(dyv.astype(bf16).astype(f32) * y_diag + dyv * yoff, hoc)
        rs_c = _dot_exact(colterm_all[...], hoc)
        rs_x = _dot_exact(dxdt_all[...] * xs, hoc)
        end_dot = _dot_exact(jnp.broadcast_to(jnp.sum(tend_all[...], 0, keepdims=True), (8, di)), hoc)[0:1]
        last = lax.broadcasted_iota(jnp.int32, (Q, 1), 0) == Q - 1
        dcs = rs_y - rs_c + jnp.where(last, end_dot + state_dot, 0.0)
        da = lax.dot_general(tri.astype(f32), dcs, (((0,), (0,)), ((), ())), preferred_element_type=f32,
                             precision=lax.Precision.HIGHEST)
        ddt = da * A + rs_x
        ddtraw = ddt * jax.nn.sigmoid(dtraw_v + bias_ref[...])
        ddt_ref[...] = ddtraw.astype(ddt_ref.dtype)
        dA_ref[...] += jnp.sum(da * dt, 0, keepdims=True) * A
        ddsk_ref[...] += jnp.sum(_dot_exact(dyv * xs, hoc), 0, keepdims=True)
        dtb_ref[...] += jnp.sum(ddtraw, 0, keepdims=True)

    vec = pl.BlockSpec((1, LANES), lambda c: (0, 0))
    rev = lambda c: (nc - 1 - c, 0)
    return pl.pallas_call(
        body, name=name, grid=(nc,),
        in_specs=[pl.BlockSpec((Q, CD), rev), pl.BlockSpec((Q, LANES), rev), vec, vec,
                  pl.BlockSpec((1, di), lambda c: (0, 0)), pl.BlockSpec((di, LANES), lambda c: (0, 0)),
                  pl.BlockSpec((None, di, N), lambda c: (nc - 1 - c, 0, 0)), pl.BlockSpec((Q, di), rev),
                  pl.BlockSpec((Q, di), rev)],
        out_specs=[pl.BlockSpec((Q, CD), rev), pl.BlockSpec((Q, LANES), rev), vec, vec, vec],
        out_shape=[_sds((S, CD)), _sds((S, LANES), bf16), _sds((1, LANES)), _sds((1, LANES)), _sds((1, LANES))],
        scratch_shapes=[pltpu.VMEM((di, N), f32)] + [pltpu.VMEM((Q, di), f32)] * 4,
        compiler_params=_params(("arbitrary",)),
    )(xbc, dtraw, dt_bias, a_log, dsk_wide, head_of_col, hin, y, dy)


def _t5_bucket_np(dist):
    max_exact = REL_BUCKETS // 2
    n = np.maximum(dist, 1).astype(np.float32)
    large = np.float32(max_exact) + np.log(n / np.float32(max_exact)) / np.float32(math.log(REL_MAX_DIST / max_exact)) * np.float32(REL_BUCKETS - max_exact)
    large = np.minimum(large.astype(np.int32), REL_BUCKETS - 1)
    return np.where(dist < max_exact, dist, large)


def _bucket_onehot():
    i = np.arange(ATT_BLK)[None, :]
    j = np.arange(2 * ATT_BLK)[:, None]
    delta = np.maximum(ATT_BLK + i - j, 0)
    out = np.zeros((len(ATT_DILATIONS), REL_BUCKETS, ATT_BLK * 2 * ATT_BLK), np.float32)
    for gi, d in enumerate(ATT_DILATIONS):
        b = _t5_bucket_np(delta * d).reshape(-1)
        out[gi, b, np.arange(b.size)] = 1.0
    return out


def _exact_mm(a, b, *, name, tb=False):
    M, K = a.shape
    N = b.shape[0] if tb else b.shape[1]
    tn = _tile(N, 4096, LANES)
    dn = (((1,), (1 if tb else 0,)), ((), ()))

    def body(a_ref, b_ref, o_ref):
        o_ref[...] = lax.dot_general(a_ref[...], b_ref[...], dn, preferred_element_type=f32,
                                     precision=lax.Precision.HIGHEST)

    return pl.pallas_call(
        body, name=name, grid=(N // tn,),
        in_specs=[pl.BlockSpec((M, K), lambda j: (0, 0)),
                  pl.BlockSpec((tn, K), lambda j: (j, 0)) if tb else pl.BlockSpec((K, tn), lambda j: (0, j))],
        out_specs=pl.BlockSpec((M, tn), lambda j: (0, j)), out_shape=_sds((M, N)),
        compiler_params=_params(("parallel",)),
    )(a, b)


def _band_penalty():
    i = np.arange(ATT_BLK)[None, :]
    j = np.arange(2 * ATT_BLK)[:, None]
    delta = ATT_BLK + i - j
    return np.where((delta >= 0) & (delta <= ATT_BLK), 0.0, -np.inf).astype(np.float32)


def _first_block_keep(n):
    key = lax.broadcasted_iota(jnp.int32, (2 * ATT_BLK, ATT_BLK), 0)
    return (key >= ATT_BLK) | (n > 0)


ATT_SCALE = HEAD_DIM ** -0.5


def _rows(ref, r, d):
    return ref[...] if d == 1 else ref[pl.ds(r, ATT_BLK, stride=d), :]


def _set_rows(ref, r, d, val):
    if d == 1:
        ref[...] = val
    else:
        ref[pl.ds(r, ATT_BLK, stride=d), :] = val


def _attn_width(d, D):
    return D if d == 1 else LANES


def _over_residues(d, one, unroll=1):
    if d == 1:
        one(0)
    else:
        lax.fori_loop(0, d, lambda r, c: (one(r), c)[1], 0, unroll=unroll)


def _attn_fwd(q, k, v, bias_t, d, name):
    S, D = q.shape
    nb = S // (d * ATT_BLK)
    H = D // HEAD_DIM
    W = _attn_width(d, D)
    HB = W // HEAD_DIM

    def body(q_ref, kp_ref, kc_ref, vp_ref, vc_ref, b_ref, o_ref, lse_ref):
        keep = _first_block_keep(pl.program_id(1))
        first = lax.broadcasted_iota(jnp.int32, (1, LANES), 1) < HEAD_DIM

        def one(r):
            qs = (_rows(q_ref, r, d) * ATT_SCALE).astype(bf16)
            kcat = jnp.concatenate([_rows(kp_ref, r, d), _rows(kc_ref, r, d)], axis=0).astype(bf16)
            vcat = jnp.concatenate([_rows(vp_ref, r, d), _rows(vc_ref, r, d)], axis=0).astype(bf16)
            outs = []
            for pair in range(W // LANES):
                ps = slice(pair * LANES, (pair + 1) * LANES)
                q2, k2, v2 = qs[:, ps], kcat[:, ps], vcat[:, ps]
                o2 = jnp.zeros((ATT_BLK, LANES), f32)
                for e in range(2):
                    h = 2 * pair + e
                    mine = first if e == 0 else jnp.logical_not(first)
                    zero = jnp.zeros((), bf16)
                    st = jnp.where(keep, _dot_nt(k2, jnp.where(mine, q2, zero)) + b_ref[h], -jnp.inf)
                    m = jnp.max(st, 0, keepdims=True)
                    pt = jnp.exp(st - m)
                    l = jnp.sum(pt, 0, keepdims=True)
                    o2 = o2 + _dot_tn(pt * (1.0 / l), jnp.where(mine, v2, zero))
                    lse_ref[r, h] = m + jnp.log(l)
                outs.append(o2)
            _set_rows(o_ref, r, d, jnp.concatenate(outs, axis=1))

        _over_residues(d, one, unroll=4)

    cur = pl.BlockSpec((ATT_BLK * d, W), lambda j, n: (n, j))
    prev = pl.BlockSpec((ATT_BLK * d, W), lambda j, n: (jnp.maximum(n - 1, 0), j))
    return pl.pallas_call(
        body, name=name, grid=(D // W, nb),
        in_specs=[cur, prev, cur, prev, cur, pl.BlockSpec((HB, 2 * ATT_BLK, ATT_BLK), lambda j, n: (j, 0, 0))],
        out_specs=[cur, pl.BlockSpec((None, d, HB, 1, LANES), lambda j, n: (n, 0, j, 0, 0))],
        out_shape=[_sds((S, D)), _sds((nb, d, H, 1, LANES))],
        compiler_params=_params(("parallel", "arbitrary")),
    )(q, k, k, v, v, bias_t)


def _from_blocks(rows, lanes=None):
    nb, d, H = rows.shape[:3]
    a = jnp.transpose(rows[:, :, :, 0, :], (0, 3, 1, 2)).reshape(nb * ATT_BLK * d, H)
    return a if lanes is None else jnp.pad(a, ((0, 0), (0, lanes - H)))


def _by_block(a, d):
    S, H = a.shape
    t = jnp.transpose(a.reshape(S // (d * ATT_BLK), ATT_BLK, d, H), (0, 2, 3, 1))
    return t[:, :, :, None, :]


def _head_sums(a, b, name):
    S, D = a.shape
    tr = _tile(S, 512, 8)
    hoc = jnp.asarray((np.arange(D)[:, None] // HEAD_DIM == np.arange(LANES)[None, :]).astype(np.float32))

    def body(a_ref, b_ref, h_ref, o_ref):
        o_ref[...] = _dot_exact(a_ref[...] * b_ref[...], h_ref[...])

    return pl.pallas_call(
        body, name=name, grid=(S // tr,),
        in_specs=[pl.BlockSpec((tr, D), lambda i: (i, 0)), pl.BlockSpec((tr, D), lambda i: (i, 0)),
                  pl.BlockSpec((D, LANES), lambda i: (0, 0))],
        out_specs=pl.BlockSpec((tr, LANES), lambda i: (i, 0)), out_shape=_sds((S, LANES)),
        compiler_params=_params(("parallel",)),
    )(a, b, hoc)


def _attn_bwd(q, k, v, bias_t, datt, lse_rows, dsum_rows, d, name):
    S, D = q.shape
    nb = S // (d * ATT_BLK)
    H = D // HEAD_DIM
    W = _attn_width(d, D)
    HB = W // HEAD_DIM

    def body(q_ref, kp_ref, kc_ref, vp_ref, vc_ref, b_ref, do_ref, lse_ref, dsum_ref,
             dq_ref, dk_ref, dv_ref, db_ref, carry_k, carry_v):
        j = pl.program_id(0)
        n = pl.program_id(1)

        @pl.when(n == 0)
        def _():
            carry_k[...] = jnp.zeros_like(carry_k)
            carry_v[...] = jnp.zeros_like(carry_v)
            db_ref[...] = jnp.zeros_like(db_ref)

        @pl.when(n < nb)
        def _():
            key = lax.broadcasted_iota(jnp.int32, (2 * ATT_BLK, ATT_BLK), 0)
            keep = (key >= ATT_BLK) | (n > 0)
            first = lax.broadcasted_iota(jnp.int32, (1, LANES), 1) < HEAD_DIM

            def one(r):
                qs = (_rows(q_ref, r, d) * ATT_SCALE).astype(bf16)
                kcat = jnp.concatenate([_rows(kp_ref, r, d), _rows(kc_ref, r, d)], axis=0).astype(bf16)
                vcat = jnp.concatenate([_rows(vp_ref, r, d), _rows(vc_ref, r, d)], axis=0).astype(bf16)
                dob = _rows(do_ref, r, d).astype(bf16)
                dqs, dks, dvs = [], [], []
                for pair in range(W // LANES):
                    ps = slice(pair * LANES, (pair + 1) * LANES)
                    q2, k2, v2, do2 = qs[:, ps], kcat[:, ps], vcat[:, ps], dob[:, ps]
                    dq2 = jnp.zeros((ATT_BLK, LANES), f32)
                    dk2 = jnp.zeros((2 * ATT_BLK, LANES), f32)
                    dv2 = jnp.zeros((2 * ATT_BLK, LANES), f32)
                    for e in range(2):
                        h = 2 * pair + e
                        mine = first if e == 0 else jnp.logical_not(first)
                        zero = jnp.zeros((), bf16)
                        qm, dom, km = jnp.where(mine, q2, zero), jnp.where(mine, do2, zero), jnp.where(mine, k2, zero)
                        st = jnp.where(keep, _dot_nt(k2, qm) + b_ref[h], -jnp.inf)
                        pt = jnp.exp(st - lse_ref[r, j * HB + h])
                        dst = pt * (_dot_nt(v2, dom) - dsum_ref[r, j * HB + h])
                        db_ref[h] += dst
                        dv2 = dv2 + _dot(pt, dom)
                        dk2 = dk2 + _dot(dst, qm)
                        dq2 = dq2 + _dot_tn(dst, km)
                    dqs.append(dq2 * ATT_SCALE)
                    dks.append(dk2)
                    dvs.append(dv2)
                _set_rows(dq_ref, r, d, jnp.concatenate(dqs, axis=1))
                dk = jnp.concatenate(dks, axis=1)
                dv = jnp.concatenate(dvs, axis=1)
                _set_rows(dk_ref, r, d, carry_k[r] + dk[:ATT_BLK])
                _set_rows(dv_ref, r, d, carry_v[r] + dv[:ATT_BLK])
                carry_k[r] = dk[ATT_BLK:]
                carry_v[r] = dv[ATT_BLK:]

            _over_residues(d, one, unroll=2)

        @pl.when(n == nb)
        def _():
            def last(r):
                _set_rows(dk_ref, r, d, carry_k[r])
                _set_rows(dv_ref, r, d, carry_v[r])

            _over_residues(d, last)

    nq = lambda n: jnp.minimum(n, nb - 1)
    cur = pl.BlockSpec((ATT_BLK * d, W), lambda j, n: (nq(n), j))
    prev = pl.BlockSpec((ATT_BLK * d, W), lambda j, n: (jnp.maximum(nq(n) - 1, 0), j))
    done = pl.BlockSpec((ATT_BLK * d, W), lambda j, n: (jnp.maximum(n - 1, 0), j))
    bspec = pl.BlockSpec((HB, 2 * ATT_BLK, ATT_BLK), lambda j, n: (j, 0, 0))
    rows = pl.BlockSpec((None, d, H, 1, LANES), lambda j, n: (nq(n), 0, 0, 0, 0))
    return pl.pallas_call(
        body, name=name, grid=(D // W, nb + 1),
        in_specs=[cur, prev, cur, prev, cur, bspec, cur, rows, rows],
        out_specs=[cur, done, done, bspec],
        out_shape=[_sds((S, D)), _sds((S, D)), _sds((S, D)), _sds((H, 2 * ATT_BLK, ATT_BLK))],
        scratch_shapes=[pltpu.VMEM((d, ATT_BLK, W), f32), pltpu.VMEM((d, ATT_BLK, W), f32)],
        compiler_params=_params(("arbitrary", "arbitrary")),
    )(q, k, k, v, v, bias_t, datt, lse_rows, dsum_rows)


def _attn_combine(os_, lses, name):
    S, D = os_[0].shape
    tr = _tile(S, 128, 16)
    head_cols = jnp.asarray((np.arange(LANES)[:, None] == np.arange(D)[None, :] // HEAD_DIM).astype(np.float32))

    def body(o0, o1, o2, l0, l1, l2, hc_ref, att_ref, attb_ref, lse_ref):
        a, b, c = l0[...], l1[...], l2[...]
        m = jnp.maximum(jnp.maximum(a, b), c)
        e0, e1, e2 = jnp.exp(a - m), jnp.exp(b - m), jnp.exp(c - m)
        tot = e0 + e1 + e2
        wide = lambda w: _dot_exact(w / tot, hc_ref[...])
        att = wide(e0) * o0[...] + wide(e1) * o1[...] + wide(e2) * o2[...]
        att_ref[...] = att
        attb_ref[...] = att.astype(bf16)
        lse_ref[...] = m + jnp.log(tot)

    wide_spec = pl.BlockSpec((tr, D), lambda i: (i, 0))
    lane_spec = pl.BlockSpec((tr, LANES), lambda i: (i, 0))
    return pl.pallas_call(
        body, name=name, grid=(S // tr,),
        in_specs=[wide_spec] * 3 + [lane_spec] * 3 + [pl.BlockSpec((LANES, D), lambda i: (0, 0))],
        out_specs=[wide_spec, wide_spec, lane_spec], out_shape=[_sds((S, D)), _sds((S, D), bf16), _sds((S, LANES))],
        compiler_params=_params(("parallel",)),
    )(*os_, *lses, head_cols)


ANY = pl.BlockSpec(memory_space=pl.ANY)


def _all_gather(vs, name):
    n = len(vs)

    def body(*refs):
        x_refs, out_refs = refs[:n], refs[n:2 * n]
        send_sems, recv_sems, local_sems = refs[2 * n:]
        x, y, c = lax.axis_index("x"), lax.axis_index("y"), lax.axis_index("c")
        me, sibling = (x, y, c), (x, y, 1 - c)
        chips = [(1 - x, y), (x, 1 - y), (1 - x, 1 - y)]

        def slot(i, px, py, pc):
            return out_refs[i].at[4 * px + 2 * py + pc]

        def copy(i, k, block, to, src=None):
            return pltpu.make_async_remote_copy(
                src_ref=slot(i, *block) if src is None else src, dst_ref=slot(i, *block),
                send_sem=send_sems.at[i, k], recv_sem=recv_sems.at[i, k], device_id=to, device_id_type=MESH)

        mine = [pltpu.make_async_copy(x_refs[i], slot(i, *me), local_sems.at[i]) for i in range(n)]
        for cp in mine:
            cp.start()
        first = []
        for i in range(n):
            first.append(copy(i, 0, me, sibling, src=x_refs[i]))
            first += [copy(i, 1 + j, me, (*chip, c), src=x_refs[i]) for j, chip in enumerate(chips)]
        for cp in first:
            cp.start()
        passed = []
        for i in range(n):
            for j, chip in enumerate(chips):
                copy(i, 1 + j, (*chip, c), me).wait_recv()
                cp = copy(i, 4 + j, (*chip, c), sibling)
                cp.start()
                passed.append(cp)
        for i in range(n):
            copy(i, 0, sibling, me).wait_recv()
            for j, chip in enumerate(chips):
                copy(i, 4 + j, (*chip, 1 - c), me).wait_recv()
        for cp in first + passed:
            cp.wait_send()
        for cp in mine:
            cp.wait()

    return pl.pallas_call(
        body, name=name, out_shape=[_sds((N_DEV,) + v.shape, v.dtype) for v in vs], in_specs=[ANY] * n,
        out_specs=[ANY] * n,
        scratch_shapes=[pltpu.SemaphoreType.DMA((n, 7)), pltpu.SemaphoreType.DMA((n, 7)), pltpu.SemaphoreType.DMA((n,))],
    )(*vs)


def _sum_slots(t, name):
    n, R, C = t.shape
    tr = _tile(R, PACK_ROW_TILE, 16)

    def body(t_ref, o_ref):
        acc = t_ref[0].astype(f32)
        for k in range(1, n):
            acc = acc + t_ref[k].astype(f32)
        o_ref[...] = acc

    return pl.pallas_call(
        body, name=name, grid=(R // tr,),
        in_specs=[pl.BlockSpec((n, tr, C), lambda i: (0, i, 0))],
        out_specs=pl.BlockSpec((tr, C), lambda i: (i, 0)), out_shape=_sds((R, C)),
        compiler_params=_params(("parallel",)),
    )(t)


HBM_SPEC = pl.BlockSpec(memory_space=pltpu.HBM)
SEM_SPEC = pl.BlockSpec(memory_space=pltpu.SEMAPHORE)
EFFECT = pltpu.SideEffectType.DATAFLOW_SIDE_EFFECTING


def _mesh_pos(p):
    return (p // 4, (p // 2) % 2, p % 2)


def _exchange_copy(src_refs, land_refs, send_sems, recv_sems, whole, dests, i, k):
    me = 4 * lax.axis_index("x") + 2 * lax.axis_index("y") + lax.axis_index("c")
    to = (me + k) % N_DEV
    frm = (me + N_DEV - k) % N_DEV
    lo, hi = dests
    src = src_refs[i] if whole else src_refs[i].at[jnp.minimum(jnp.maximum(to - lo, 0), hi - lo - 1)]
    s = i * (N_DEV - 1) + k - 1
    send = pltpu.make_async_remote_copy(src_ref=src, dst_ref=land_refs[i].at[me], send_sem=send_sems.at[s],
                                        recv_sem=recv_sems.at[s], device_id=_mesh_pos(to), device_id_type=MESH)
    recv = pltpu.make_async_remote_copy(src_ref=src, dst_ref=land_refs[i].at[frm], send_sem=send_sems.at[s],
                                        recv_sem=recv_sems.at[s], device_id=_mesh_pos(to), device_id_type=MESH)
    return send, recv, (to >= lo) & (to < hi), (me >= lo) & (me < hi)


def _exchange_start(srcs, whole, name, after=None, dests=(0, N_DEV)):
    n = len(srcs)
    lands = [lax.empty((N_DEV,) + s.shape[-2:], s.dtype) for s in srcs]
    after = list(after or [])
    n_in = 2 * n + len(after)
    everyone = dests == (0, N_DEV)

    def body(*refs):
        src_refs, land_refs = refs[:n], refs[n:2 * n]
        send_sems, recv_sems, token = refs[n_in], refs[n_in + 1], refs[-1]
        for i in range(n):
            for k in range(1, N_DEV):
                send, _, sends, _ = _exchange_copy(src_refs, land_refs, send_sems, recv_sems, whole, dests, i, k)
                if everyone:
                    send.start()
                else:
                    pl.when(sends)(send.start)
        token[...] = jnp.zeros_like(token)

    sems = pltpu.SemaphoreType.DMA((n * (N_DEV - 1),))
    outs = pl.pallas_call(
        body, name=name,
        out_shape=(sems, sems, *[pltpu.HBM(a.shape, a.dtype) for a in srcs + lands], _sds((8, LANES))),
        in_specs=[HBM_SPEC] * (2 * n) + [pl.BlockSpec(memory_space=pl.ANY)] * len(after),
        out_specs=(SEM_SPEC, SEM_SPEC, *[HBM_SPEC] * (2 * n), pl.BlockSpec(memory_space=pltpu.VMEM)),
        input_output_aliases={i: 2 + i for i in range(2 * n)},
        compiler_params=pltpu.CompilerParams(has_side_effects=EFFECT),
    )(*[pltpu.with_memory_space_constraint(a, pltpu.HBM) for a in srcs + lands], *after)
    return (outs[0], outs[1], list(outs[2:2 + n]), list(outs[2 + n:2 + 2 * n]), whole, dests), outs[-1]


def _exchange_wait(handle, after, name):
    send_sems, recv_sems, srcs, lands, whole, dests = handle
    n = len(srcs)
    everyone = dests == (0, N_DEV)

    def body(*refs):
        src_refs, land_refs = refs[:n], refs[n:2 * n]
        send_sems, recv_sems = refs[2 * n], refs[2 * n + 1]
        for i in range(n):
            for k in range(1, N_DEV):
                send, recv, sends, receives = _exchange_copy(src_refs, land_refs, send_sems, recv_sems, whole, dests, i, k)
                if everyone:
                    send.wait_send()
                    recv.wait_recv()
                else:
                    pl.when(sends)(send.wait_send)
                    pl.when(receives)(recv.wait_recv)

    outs = pl.pallas_call(
        body, name=name, out_shape=tuple(pltpu.HBM(a.shape, a.dtype) for a in srcs + lands),
        in_specs=[HBM_SPEC] * (2 * n) + [SEM_SPEC, SEM_SPEC, pl.BlockSpec(memory_space=pl.ANY)],
        out_specs=[HBM_SPEC] * (2 * n), input_output_aliases={i: i for i in range(2 * n)},
        compiler_params=pltpu.CompilerParams(has_side_effects=EFFECT),
    )(*srcs, *lands, send_sems, recv_sems, after)
    return list(outs[n:])


def _tie(v, token):
    return v + token[0:1, 0:1].astype(v.dtype).reshape((1,) * v.ndim)


def _with_own(land, own, me):
    return lax.dynamic_update_slice_in_dim(land, own[None].astype(land.dtype), me, 0)


class _Overlap:
    def __init__(self, shards, me, after):
        self.me = me
        self.names = list(shards)
        self.handle, self.token = _exchange_start([shards[nm] for nm in self.names], True, "weights_start", after)
        self.sent = {}

    def weights(self, after):
        lands = _exchange_wait(self.handle, after, "weights_wait")
        own = self.handle[2]
        return {nm: _full_from_blocks(nm, _with_own(land, o, self.me)) for nm, land, o in zip(self.names, lands, own)}

    def send(self, tag, grads, after=None):
        names = list(grads)
        handle, token = _exchange_start([_blocks_from_full(nm, grads[nm]) for nm in names], False, f"grads_start_{tag}",
                                        after)
        self.sent[tag] = (names, handle)
        return token

    def send_rows(self, tag, rows, dests):
        lo, hi = dests
        blocks = rows.reshape(hi - lo, rows.shape[0] // (hi - lo), rows.shape[1])
        handle, token = _exchange_start([blocks], False, f"grads_start_{tag}", None, dests)
        self.sent[tag] = ([tag], handle)
        return token

    def received(self, tag, after):
        names, handle = self.sent[tag]
        lands = _exchange_wait(handle, after, f"grads_wait_{tag}")
        lo = handle[5][0]
        own = [lax.dynamic_index_in_dim(b, self.me - lo, 0, keepdims=False) for b in handle[2]]
        return {nm: _with_own(land, o, self.me) for nm, land, o in zip(names, lands, own)}


ADAM_ROWS = 32


def _adamw(w, g, m, v, name):
    deep = w.ndim == 3
    R, C = w.shape[0], w.shape[-1]
    cb = LANES if C % LANES == 0 else C
    n_parts = g.shape[0] if g.ndim == 3 else 0

    def body(w_ref, g_ref, m_ref, v_ref, d_ref, m2_ref, v2_ref, *g_out):
        at = (lambda ref, sl: ref.at[sl, 0, :]) if deep else (lambda ref, sl: ref.at[sl, :])

        def update(sl):
            if n_parts:
                gv = g_ref[0, sl, :].astype(f32)
                for k in range(1, n_parts):
                    gv = gv + g_ref[k, sl, :].astype(f32)
                at(g_out[0], sl)[...] = gv
            else:
                gv = g_ref[sl, :]
            m2 = ADAM_B1 * at(m_ref, sl)[...] + (1.0 - ADAM_B1) * gv
            v2 = ADAM_B2 * at(v_ref, sl)[...] + (1.0 - ADAM_B2) * jnp.square(gv)
            m_hat = m2 / (1.0 - ADAM_B1 ** ADAM_STEP)
            v_hat = v2 / (1.0 - ADAM_B2 ** ADAM_STEP)
            at(d_ref, sl)[...] = -ADAM_LR * (m_hat / (jnp.sqrt(v_hat) + ADAM_EPS) + ADAM_WD * at(w_ref, sl)[...])
            at(m2_ref, sl)[...] = m2
            at(v2_ref, sl)[...] = v2

        main = R // ADAM_ROWS
        if main:
            lax.fori_loop(0, main, lambda i, c: (update(pl.ds(pl.multiple_of(i * ADAM_ROWS, ADAM_ROWS), ADAM_ROWS)), c)[1], 0)
        if R % ADAM_ROWS:
            update(pl.ds(main * ADAM_ROWS, R % ADAM_ROWS))

    spec = pl.BlockSpec((R, 1, cb), lambda j: (0, 0, j)) if deep else pl.BlockSpec((R, cb), lambda j: (0, j))
    g_spec = pl.BlockSpec((n_parts, R, cb), lambda j: (0, 0, j)) if n_parts else pl.BlockSpec((R, cb), lambda j: (0, j))
    n_out = 4 if n_parts else 3
    return pl.pallas_call(
        body, name=name, grid=(C // cb,), in_specs=[spec, g_spec, spec, spec], out_specs=[spec] * n_out,
        out_shape=[_sds(w.shape)] * n_out, compiler_params=_params(("parallel",)),
    )(w, g, m, v)


BIG_PARAMS = ("hy_w_in", "hy_w_out", "cv_w_pw1", "cv_w_pw2", "ffn_w_gate", "ffn_w_up", "ffn_w_down")


def _shards_2d(w):
    t = lambda a: jnp.transpose(a)
    return dict(in_t=t(w["hy_w_in"][0]), out=w["hy_w_out"][0], pw1=w["cv_w_pw1"][0], pw2=w["cv_w_pw2"][0],
                gate_t0=t(w["ffn_w_gate"][0]), gate_t1=t(w["ffn_w_gate"][1]), up_t0=t(w["ffn_w_up"][0]),
                up_t1=t(w["ffn_w_up"][1]), down0=w["ffn_w_down"][0], down1=w["ffn_w_down"][1])


def _unshard_2d(s):
    t = lambda a: jnp.transpose(a)
    out = dict(hy_w_out=s["out"][None], cv_w_pw1=s["pw1"][None], cv_w_pw2=s["pw2"][None],
               ffn_w_gate=jnp.stack([t(s["gate_t0"]), t(s["gate_t1"])]),
               ffn_w_up=jnp.stack([t(s["up_t0"]), t(s["up_t1"])]), ffn_w_down=jnp.stack([s["down0"], s["down1"]]))
    if "in_t" in s:
        out["hy_w_in"] = t(s["in_t"])[None]
    return out


def _full_from_blocks(nm, g):
    if nm == "pw1":
        return jnp.transpose(g, (1, 0, 2)).reshape(g.shape[1], N_DEV * g.shape[2])
    return g.reshape(N_DEV * g.shape[1], g.shape[2])


def _blocks_from_full(nm, g):
    if nm == "pw1":
        return jnp.transpose(g.reshape(g.shape[0], N_DEV, g.shape[1] // N_DEV), (1, 0, 2))
    return g.reshape(N_DEV, g.shape[0] // N_DEV, g.shape[1])


class _VecPack:
    def __init__(self, shapes):
        self.shapes = [tuple(s) for s in shapes]
        self.sizes = [int(np.prod(s)) for s in self.shapes]
        total = sum(self.sizes)
        self.rows = -(-(-(-total // LANES)) // 8) * 8
        self.total = total

    def pack(self, arrays):
        flat = jnp.concatenate([a.astype(f32).reshape(-1) for a in arrays])
        flat = jnp.pad(flat, (0, self.rows * LANES - self.total))
        return flat.reshape(self.rows, LANES)

    def unpack(self, packed):
        flat = packed.reshape(-1)
        out, off = [], 0
        for shp, n in zip(self.shapes, self.sizes):
            out.append(flat[off:off + n].reshape(shp))
            off += n
        return out

    def unpack_stacked(self, stacked, only=None):
        flat = stacked.reshape(stacked.shape[0], -1)
        offs = np.concatenate([[0], np.cumsum(self.sizes)])
        get = lambda i: flat[:, offs[i]:offs[i + 1]].reshape((stacked.shape[0],) + self.shapes[i])
        return get(only) if only is not None else [get(i) for i in range(len(self.shapes))]


def _row(v):
    return v.reshape(1, -1)


def _pad_lanes(v):
    v = v.reshape(1, -1)
    return jnp.pad(v, ((0, 0), (0, LANES - v.shape[1])))


def _ffn_fwd(h, w_gate_t, w_up_t, w_down, tag):
    F = w_down.shape[0]
    a = _mm(h, w_gate_t, tb=True, out_dtype=bf16, name=f"ffn_gate_{tag}")
    u = _mm(h, w_up_t, tb=True, out_dtype=bf16, name=f"ffn_up_{tag}")
    (f,), _ = _rowwise(f"swiglu_{tag}", lambda rv, vv: ([_silu(rv[0].astype(f32)) * rv[1].astype(f32)], []), [a, u], [],
                       [(F, bf16)], [], sub=16, col_chunk=_tile(F, 512, LANES))
    out = _mm(f, w_down, name=f"ffn_down_{tag}")
    return out, (a, u, f)


def _ffn_bwd(h, w_gate_t, w_up_t, w_down, saved, dout, tag):
    a, u, f = saved
    F = w_down.shape[0]
    df = _mm(dout, w_down, tb=True, out_dtype=bf16, name=f"ffn_down_dx_{tag}")
    dw_down = _mm(f, dout, ta=True, out_dtype=bf16, name=f"ffn_down_dw_{tag}")

    def fn(rv, vv):
        _, vjp = jax.vjp(lambda a_, u_: _silu(a_) * u_, rv[0].astype(f32), rv[1].astype(f32))
        da, du = vjp(rv[2].astype(f32))
        return [da, du], []

    (da, du), _ = _rowwise(f"swiglu_bwd_{tag}", fn, [a, u, df], [], [(F, bf16), (F, bf16)], [], sub=16,
                           col_chunk=_tile(F, 512, LANES))
    dh = _mm(du, w_up_t, add=_mm(da, w_gate_t, name=f"ffn_gate_dx_{tag}"), name=f"ffn_up_dx_{tag}")
    dw_gate_t = _mm(da, h, ta=True, out_dtype=bf16, name=f"ffn_gate_dw_{tag}")
    dw_up_t = _mm(du, h, ta=True, out_dtype=bf16, name=f"ffn_up_dw_{tag}")
    return dh, dw_gate_t, dw_up_t, dw_down


def _local_step(x, target, mod, w_in_t, comm, small):
    S, D = x.shape
    di = small["hy_ssm_norm_g"].shape[-1]
    nh = small["hy_dt_bias"].shape[-1]
    cd = small["hy_conv_b"].shape[-1]
    m = [[_row(mod[i, j]) for j in range(6)] for i in range(2)]

    off_q = di + cd + nh
    w_qkv_t = w_in_t[off_q:]
    seg = dict(z=(w_in_t, 0, di), xbc=(w_in_t, di, cd), dt=(w_in_t, di + cd, LANES))
    for i, nm in enumerate(("q0", "q1", "q2", "k", "v")):
        seg[nm] = (w_qkv_t, i * D, D)

    g_mix = [_row(small["norm_mix_g"][i]) for i in range(2)]
    g_ffn = [_row(small["norm_ffn_g"][i]) for i in range(2)]
    conv_w, conv_b = small["hy_conv_w_full"], _row(small["hy_conv_b"][0])
    dt_bias, a_log, d_skip = (_pad_lanes(small[k][0]) for k in ("hy_dt_bias", "hy_a_log", "hy_d_skip"))
    g_ssm = _row(small["hy_ssm_norm_g"][0])
    onehot = jnp.asarray(_bucket_onehot())
    rel_t = small["rel_table"].T
    H = D // HEAD_DIM
    bias = [_exact_mm(rel_t[gi * H:(gi + 1) * H], onehot[gi], name=f"rel_bias_{gi}")
            .reshape(H, 2 * ATT_BLK, ATT_BLK) + _band_penalty() for gi in range(3)]

    h1 = _adaln_fwd(x, g_mix[0], m[0][1], m[0][0], "adaln_mix0")
    proj = {nm: _mm(h1, mat, tb=True, b_rows=(off, cnt), name=f"in_{nm}") for nm, (mat, off, cnt) in seg.items()}
    xbc_pre, xbc = _conv_fwd(proj["xbc"], conv_w, conv_b, silu=True, name="ssm_conv", tr=1024)
    y, hin = _ssd_fwd(xbc, proj["dt"], dt_bias, a_log, d_skip, di, "ssd_fwd")
    (yg,), _ = _rowwise("ssm_gate", lambda rv, vv: ([_gate_f(rv[0], rv[1], vv[0])], []),
                        [y, proj["z"]], [g_ssm], [(di, bf16)], [], sub=16)
    og = [_attn_fwd(proj[f"q{gi}"], proj["k"], proj["v"], bias[gi], d, f"attn_fwd_{gi}")
          for gi, d in enumerate(ATT_DILATIONS)]
    att, att_b, lse_tot = _attn_combine([a for a, _ in og], [_from_blocks(b, LANES) for _, b in og], "attn_combine")
    W = comm.weights(after=att_b)
    w_out_y, w_out_a = W["out"][:di], W["out"][di:]
    mix0 = _mm(att_b, w_out_a, add=_mm(yg, w_out_y, name="out_y"), name="out_a")
    x1, h2 = _resid_adaln_fwd(x, m[0][2], mix0, g_ffn[0], m[0][4], m[0][3], "resid_mix0_adaln_ffn0")
    f0, ffn0_saved = _ffn_fwd(h2, W["gate_t0"], W["up_t0"], W["down0"], "0")
    x2, h3 = _resid_adaln_fwd(x1, m[0][5], f0, g_mix[1], m[1][1], m[1][0], "resid_ffn0_adaln_mix1")
    pw1 = _mm(h3, W["pw1"], bias=_row(small["cv_b_pw1_full"]), name="cv_pw1")
    (u,), _ = _rowwise("cv_glu", lambda rv, vv: ([rv[0] * jax.nn.sigmoid(rv[1])], []),
                       [(pw1, 0, D), (pw1, 1, D)], [], [(D, f32)], [])
    (u2,) = _conv_fwd(u, small["cv_w_dw_full"], _row(small["cv_b_dw_full"]), silu=False, name="cv_dw")
    ln_g, ln_b = _row(small["cv_ln_g_full"]), _row(small["cv_ln_b_full"])
    (u3,), _ = _rowwise("cv_lnsilu", lambda rv, vv: ([_lnsilu_f(rv[0], vv[0], vv[1])], []),
                        [u2], [ln_g, ln_b], [(D, bf16)], [], sub=16)
    mix1 = _mm(u3, W["pw2"], bias=_row(small["cv_b_pw2_full"]), name="cv_pw2")
    x3, h4 = _resid_adaln_fwd(x2, m[1][2], mix1, g_ffn[1], m[1][4], m[1][3], "resid_mix1_adaln_ffn1")
    f1, ffn1_saved = _ffn_fwd(h4, W["gate_t1"], W["up_t1"], W["down1"], "1")

    g_fin = _row(small["final_norm_g"])
    dmod = [[None] * 6 for _ in range(2)]
    d_norm_mix, d_norm_ffn = [None, None], [None, None]
    big = {}

    def final_fn(rv, vv):
        xv, fv, tv = rv
        gate = vv[1]
        yv, vjp = jax.vjp(_rms, xv + gate * fv, vv[0])
        err = yv - tv
        dx, dg = vjp(err / D)
        part = 0.5 * jnp.sum(jnp.mean(err * err, -1, keepdims=True), 0, keepdims=True)
        return [dx, gate * dx], [dg, jnp.broadcast_to(part, (1, LANES)), jnp.sum(dx * fv, 0, keepdims=True)]

    (dx4, df1), (d_fin, loss, dmod[1][5]) = _rowwise("loss_head", final_fn, [x3, f1, target], [g_fin, m[1][5]],
                                                      [(D, f32), (D, bf16)], [D, LANES, D], sub=16)

    dh4, big["gate_t1"], big["up_t1"], big["down1"] = _ffn_bwd(h4, W["gate_t1"], W["up_t1"], W["down1"], ffn1_saved, df1, "1")
    dx3, dmix1, (d_norm_ffn[1], dmod[1][4], dmod[1][3], dmod[1][2], d_b_pw2) = _adaln_resid_bwd(
        x3, g_ffn[1], m[1][4], m[1][3], dh4, dx4, mix1, m[1][2], "adaln_ffn1_resid_mix1_bwd")
    du3 = _mm(dmix1, W["pw2"], tb=True, name="cv_pw2_dx")
    big["pw2"] = _mm(u3, dmix1, ta=True, out_dtype=bf16, name="cv_pw2_dw")

    def lnsilu_bwd(rv, vv):
        _, vjp = jax.vjp(_lnsilu_f, rv[0], vv[0], vv[1])
        du, dg, db = vjp(rv[1])
        return [du], [dg, db]

    (du2,), (d_ln_g, d_ln_b) = _rowwise("cv_lnsilu_bwd", lnsilu_bwd, [u2, du3], [ln_g, ln_b], [(D, f32)], [D, D])
    du, d_w_dw, d_b_dw = _conv_bwd(u, small["cv_w_dw_full"], du2, None, silu=False, name="cv_dw_bwd")

    def glu_bwd(rv, vv):
        a, gt, d = rv
        _, vjp = jax.vjp(lambda a_, g_: a_ * jax.nn.sigmoid(g_), a, gt)
        da, dg = vjp(d)
        return [da, dg], [jnp.sum(da, 0, keepdims=True), jnp.sum(dg, 0, keepdims=True)]

    (dpa, dpg), (d_b1a, d_b1g) = _rowwise("cv_glu_bwd", glu_bwd, [(pw1, 0, D), (pw1, 1, D), du], [],
                                           [(D, bf16), (D, bf16)], [D, D], sub=16)
    dpw1 = jnp.concatenate([dpa, dpg], axis=1)
    d_b_pw1 = jnp.concatenate([d_b1a, d_b1g], axis=1)
    dh3 = _mm(dpw1, W["pw1"], tb=True, name="cv_pw1_dx")
    big["pw1"] = _mm(h3, dpw1, ta=True, out_dtype=bf16, name="cv_pw1_dw")
    token = comm.send("layer1", {nm: big[nm] for nm in ("gate_t1", "up_t1", "down1", "pw2", "pw1")})
    dx2, df0, (d_norm_mix[1], dmod[1][1], dmod[1][0], dmod[0][5], _) = _adaln_resid_bwd(
        x2, g_mix[1], m[1][1], _tie(m[1][0], token), dh3, dx3, f0, m[0][5], "adaln_mix1_resid_ffn0_bwd")

    dh2, big["gate_t0"], big["up_t0"], big["down0"] = _ffn_bwd(h2, W["gate_t0"], W["up_t0"], W["down0"], ffn0_saved, df0, "0")
    dx1, dmix0, (d_norm_ffn[0], dmod[0][4], dmod[0][3], dmod[0][2], _) = _adaln_resid_bwd(
        x1, g_ffn[0], m[0][4], m[0][3], dh2, dx2, mix0, m[0][2], "adaln_ffn0_resid_mix0_bwd")
    dyg = _mm(dmix0, w_out_y, tb=True, name="out_y_dx")
    datt = _mm(dmix0, w_out_a, tb=True, name="out_a_dx")
    big["out"] = jnp.concatenate([_mm(yg, dmix0, ta=True, out_dtype=bf16, name="out_y_dw"),
                                  _mm(att_b, dmix0, ta=True, out_dtype=bf16, name="out_a_dw")], axis=0)
    token = comm.send("layer0", {nm: big[nm] for nm in ("gate_t0", "up_t0", "down0", "out")})
    g_ssm = _tie(g_ssm, token)

    def gate_bwd(rv, vv):
        _, vjp = jax.vjp(_gate_f, rv[0], rv[1], vv[0])
        dy_, dz_, dg_ = vjp(rv[2])
        return [dy_, dz_], [dg_]

    (dy, dz), (d_g_ssm,) = _rowwise("ssm_gate_bwd", gate_bwd, [y, proj["z"], dyg], [g_ssm], [(di, f32), (di, bf16)], [di],
                                    sub=16)
    dxbc, ddtraw, d_a_log, d_dskip, d_dt_bias = _ssd_bwd(xbc, proj["dt"], dt_bias, a_log, d_skip, hin, y, dy, di, "ssd_bwd")
    dxbc_pre, d_conv_w, d_conv_b = _conv_bwd(proj["xbc"], conv_w, dxbc, xbc_pre, silu=True, name="ssm_conv_bwd",
                                             dx_dtype=bf16, tr=1024)
    dh1 = None
    early = []
    for nm, dseg in (("z", dz), ("xbc", dxbc_pre), ("dt", ddtraw)):
        mat, off, cnt = seg[nm]
        dh1 = _mm(dseg, mat, b_rows=(off, cnt), add=dh1, name=f"in_{nm}_dx")
        dwp = _mm(dseg, h1, ta=True, out_dtype=bf16, name=f"in_{nm}_dw")
        early.append(dwp[:nh] if nm == "dt" else dwp)
    early = jnp.concatenate(early, axis=0)
    shard_rows = w_in_t.shape[0] // N_DEV
    n_early = off_q // shard_rows
    token = comm.send_rows("in_early", early[:n_early * shard_rows], (0, n_early))
    bias = [_tie(b, token) for b in bias]

    dq, dks, dvs, dbs = [], [], [], []
    lse_heads = lse_tot[:, :H]
    dsum_heads = _head_sums(att, datt, "attn_dsum")[:, :H]
    for gi, d in enumerate(ATT_DILATIONS):
        a, b, c_, e = _attn_bwd(proj[f"q{gi}"], proj["k"], proj["v"], bias[gi], datt,
                                _by_block(lse_heads, d), _by_block(dsum_heads, d), d, f"attn_bwd_{gi}")
        dq.append(a)
        dks.append(b)
        dvs.append(c_)
        dbs.append(e)
    dk = _add3(*dks, "attn_dk")
    dv = _add3(*dvs, "attn_dv")
    d_rel = jnp.concatenate(
        [_exact_mm(dbs[gi].reshape(H, -1), onehot[gi], tb=True, name=f"rel_grad_{gi}") for gi in range(3)], axis=0).T

    late = [early[n_early * shard_rows:]]
    for nm, dseg in (("q0", dq[0]), ("q1", dq[1]), ("q2", dq[2]), ("k", dk), ("v", dv)):
        mat, off, cnt = seg[nm]
        dh1 = _mm(dseg, mat, b_rows=(off, cnt), add=dh1, name=f"in_{nm}_dx")
        late.append(_mm(dseg, h1, ta=True, out_dtype=bf16, name=f"in_{nm}_dw"))
    late = jnp.concatenate(late, axis=0)
    dx0, (d_norm_mix[0], dmod[0][1], dmod[0][0]) = _adaln_bwd(x, g_mix[0], m[0][1], m[0][0], dh1, dx1, "adaln_mix0_bwd")

    smallg = dict(
        loss=loss, dmod=jnp.stack([jnp.concatenate(dmod[i], axis=1)[0] for i in range(2)]),
        norm_mix_g=jnp.concatenate(d_norm_mix, axis=0), norm_ffn_g=jnp.concatenate(d_norm_ffn, axis=0),
        hy_conv_w=d_conv_w, hy_conv_b=d_conv_b, hy_dt_bias=d_dt_bias[:, :nh], hy_a_log=d_a_log[:, :nh],
        hy_d_skip=d_dskip[:, :nh], hy_ssm_norm_g=d_g_ssm, rel_table=d_rel,
        cv_b_pw1=d_b_pw1, cv_w_dw=d_w_dw, cv_b_dw=d_b_dw, cv_ln_g=d_ln_g, cv_ln_b=d_ln_b, cv_b_pw2=d_b_pw2,
        final_norm_g=d_fin)
    return dx0, (late, n_early), smallg


SMALL_GRAD_ORDER = ("loss", "dmod", "norm_mix_g", "norm_ffn_g", "hy_conv_w", "hy_conv_b", "hy_dt_bias", "hy_a_log",
                    "hy_d_skip", "hy_ssm_norm_g", "rel_table", "cv_b_pw1", "cv_w_dw", "cv_b_dw", "cv_ln_g", "cv_ln_b",
                    "cv_b_pw2", "final_norm_g")


def kernel(x, c, ada_w, ada_b, norm_mix_g, norm_ffn_g, hy_w_in, hy_conv_w, hy_conv_b, hy_dt_bias, hy_a_log, hy_d_skip, hy_ssm_norm_g, hy_w_out, rel_table, cv_w_pw1, cv_b_pw1, cv_w_dw, cv_b_dw, cv_ln_g, cv_ln_b, cv_w_pw2, cv_b_pw2, ffn_w_gate, ffn_w_up, ffn_w_down, final_norm_g, loss_target, m_ada_w, m_ada_b, m_norm_mix_g, m_norm_ffn_g, m_hy_w_in, m_hy_conv_w, m_hy_conv_b, m_hy_dt_bias, m_hy_a_log, m_hy_d_skip, m_hy_ssm_norm_g, m_hy_w_out, m_rel_table, m_cv_w_pw1, m_cv_b_pw1, m_cv_w_dw, m_cv_b_dw, m_cv_ln_g, m_cv_ln_b, m_cv_w_pw2, m_cv_b_pw2, m_ffn_w_gate, m_ffn_w_up, m_ffn_w_down, m_final_norm_g, v_ada_w, v_ada_b, v_norm_mix_g, v_norm_ffn_g, v_hy_w_in, v_hy_conv_w, v_hy_conv_b, v_hy_dt_bias, v_hy_a_log, v_hy_d_skip, v_hy_ssm_norm_g, v_hy_w_out, v_rel_table, v_cv_w_pw1, v_cv_b_pw1, v_cv_w_dw, v_cv_b_dw, v_cv_ln_g, v_cv_ln_b, v_cv_w_pw2, v_cv_b_pw2, v_ffn_w_gate, v_ffn_w_up, v_ffn_w_down, v_final_norm_g):
    names = ("ada_w", "ada_b", "norm_mix_g", "norm_ffn_g", "hy_w_in", "hy_conv_w", "hy_conv_b", "hy_dt_bias", "hy_a_log",
             "hy_d_skip", "hy_ssm_norm_g", "hy_w_out", "rel_table", "cv_w_pw1", "cv_b_pw1", "cv_w_dw", "cv_b_dw", "cv_ln_g",
             "cv_ln_b", "cv_w_pw2", "cv_b_pw2", "ffn_w_gate", "ffn_w_up", "ffn_w_down", "final_norm_g")
    w = dict(zip(names, (ada_w, ada_b, norm_mix_g, norm_ffn_g, hy_w_in, hy_conv_w, hy_conv_b, hy_dt_bias, hy_a_log, hy_d_skip,
                         hy_ssm_norm_g, hy_w_out, rel_table, cv_w_pw1, cv_b_pw1, cv_w_dw, cv_b_dw, cv_ln_g, cv_ln_b, cv_w_pw2,
                         cv_b_pw2, ffn_w_gate, ffn_w_up, ffn_w_down, final_norm_g)))
    mom = dict(zip(names, (m_ada_w, m_ada_b, m_norm_mix_g, m_norm_ffn_g, m_hy_w_in, m_hy_conv_w, m_hy_conv_b, m_hy_dt_bias,
                           m_hy_a_log, m_hy_d_skip, m_hy_ssm_norm_g, m_hy_w_out, m_rel_table, m_cv_w_pw1, m_cv_b_pw1, m_cv_w_dw,
                           m_cv_b_dw, m_cv_ln_g, m_cv_ln_b, m_cv_w_pw2, m_cv_b_pw2, m_ffn_w_gate, m_ffn_w_up, m_ffn_w_down,
                           m_final_norm_g)))
    vel = dict(zip(names, (v_ada_w, v_ada_b, v_norm_mix_g, v_norm_ffn_g, v_hy_w_in, v_hy_conv_w, v_hy_conv_b, v_hy_dt_bias,
                           v_hy_a_log, v_hy_d_skip, v_hy_ssm_norm_g, v_hy_w_out, v_rel_table, v_cv_w_pw1, v_cv_b_pw1, v_cv_w_dw,
                           v_cv_b_dw, v_cv_ln_g, v_cv_ln_b, v_cv_w_pw2, v_cv_b_pw2, v_ffn_w_gate, v_ffn_w_up, v_ffn_w_down,
                           v_final_norm_g)))
    S, D = x.shape[1], x.shape[2]
    ax, ay, ac = lax.axis_index("x"), lax.axis_index("y"), lax.axis_index("c")
    me = 4 * ax + 2 * ay + ac
    nmod = ada_w.shape[2]

    w2 = _shards_2d(w)
    big_names = list(w2)
    sharded_small = ("hy_conv_w", "cv_b_pw1", "cv_w_dw", "cv_b_dw", "cv_ln_g", "cv_ln_b", "cv_b_pw2")
    vp = _VecPack([c.shape] + [w[nm].shape for nm in sharded_small])
    g_in, sg = _all_gather([w2["in_t"].astype(bf16), vp.pack([c] + [w[nm] for nm in sharded_small])], "gather_w_in")
    w_in_t = _full_from_blocks("in_t", g_in)
    parts = vp.unpack_stacked(sg)
    c_all = parts[0][:, 0]
    small = {k: w[k] for k in ("norm_mix_g", "norm_ffn_g", "hy_conv_b", "hy_dt_bias", "hy_a_log", "hy_d_skip",
                               "hy_ssm_norm_g", "rel_table", "final_norm_g")}
    for p, nm in zip(parts[1:], sharded_small):
        p = p[:, 0]
        p = jnp.moveaxis(p, 0, -2)
        small[nm + "_full"] = p.reshape(p.shape[:-2] + (N_DEV * p.shape[-1],))

    (cs_all,), _ = _rowwise("ada_silu", lambda rv, vv: ([_silu(rv[0])], []), [c_all], [], [(D, f32)], [])
    b_mine = lax.dynamic_slice_in_dim(ada_b, me * nmod, nmod, axis=1)
    mod_part = jnp.stack([_mm(cs_all, ada_w[i], bias=b_mine[i:i + 1], name=f"ada_mod_{i}") for i in range(2)])
    (mod_all,) = _all_gather([mod_part.reshape(2 * N_DEV, nmod)], "gather_mod")
    mod_all = mod_all.reshape(N_DEV, 2, N_DEV, nmod)
    mod_mine = lax.dynamic_index_in_dim(mod_all, me, axis=2, keepdims=False)
    mod = jnp.transpose(mod_mine, (1, 0, 2)).reshape(2, 6, D)
    comm = _Overlap({nm: w2[nm].astype(bf16) for nm in big_names if nm != "in_t"}, me, after=[mod, w_in_t])
    mod = _tie(mod, comm.token)

    dx0, (d_in_late, n_early), sgrad = _local_step(x[0], loss_target[0], mod, w_in_t, comm, small)
    comm.send_rows("in_late", d_in_late, (n_early, N_DEV))

    gp = _VecPack([sgrad[k].shape for k in SMALL_GRAD_ORDER])
    (g_all,) = _all_gather([gp.pack([sgrad[k] for k in SMALL_GRAD_ORDER])], "gather_small_grads")
    tot = dict(zip(SMALL_GRAD_ORDER, gp.unpack(_sum_slots(g_all, "sum_small_grads"))))
    dmod_all = gp.unpack_stacked(g_all, only=SMALL_GRAD_ORDER.index("dmod"))
    loss = tot["loss"][0, 0]

    grads = {}
    dmod_mine = lax.dynamic_slice_in_dim(dmod_all, me * nmod, nmod, axis=2)
    grads["ada_w"] = jnp.stack([_mm(cs_all, dmod_mine[:, i], ta=True, name=f"ada_w_grad_{i}") for i in range(2)])
    grads["ada_b"] = tot["dmod"]
    grads["norm_mix_g"], grads["norm_ffn_g"] = tot["norm_mix_g"], tot["norm_ffn_g"]
    grads["hy_conv_b"] = tot["hy_conv_b"]
    grads["hy_dt_bias"] = tot["hy_dt_bias"]
    grads["hy_a_log"] = tot["hy_a_log"]
    grads["hy_d_skip"] = tot["hy_d_skip"]
    grads["hy_ssm_norm_g"] = tot["hy_ssm_norm_g"]
    grads["rel_table"] = tot["rel_table"]
    grads["final_norm_g"] = tot["final_norm_g"][0]
    for nm in sharded_small:
        n = w[nm].shape[-1]
        grads[nm] = lax.dynamic_slice_in_dim(tot[nm], me * n, n, axis=1).reshape(w[nm].shape)

    delta, new_m, new_v = {}, {}, {}
    shp = ada_w.shape
    two = lambda t: t.reshape(-1, shp[-1])
    d_, m_, v_ = _adamw(two(ada_w), two(grads["ada_w"]), two(m_ada_w), two(v_ada_w), "adamw_ada_w")
    delta["ada_w"], new_m["ada_w"], new_v["ada_w"] = d_.reshape(shp), m_.reshape(shp), v_.reshape(shp)
    rest = [nm for nm in names if nm not in BIG_PARAMS and nm != "ada_w"]
    sp = _VecPack([w[nm].shape for nm in rest])
    packs = [sp.pack([t[nm] for nm in rest]) for t in (w, grads, mom, vel)]
    ds_, ms_, vs_ = _adamw(*packs, "adamw_small")
    for nm, a, b, e in zip(rest, sp.unpack(ds_), sp.unpack(ms_), sp.unpack(vs_)):
        delta[nm], new_m[nm], new_v[nm] = a, b, e

    m2, v2 = _shards_2d(mom), _shards_2d(vel)
    g2, d2, nm2, nv2 = {}, {}, {}, {}
    after = d_
    slots = {}
    for tag in ("layer1", "layer0", "in_early", "in_late"):
        slots.update(comm.received(tag, after))
        if tag.startswith("in_"):
            continue
        for nm in comm.sent[tag][0]:
            d2[nm], nm2[nm], nv2[nm], g2[nm] = _adamw(w2[nm], slots[nm], m2[nm], v2[nm], f"adamw_{nm}")
            after = g2[nm]
    stored = lambda t: jnp.transpose(t, (2, 0, 1))
    in_slots = jnp.where(me < n_early, slots["in_early"], slots["in_late"])
    d_in, m_in, v_in, g_in = _adamw(stored(hy_w_in), in_slots, stored(m_hy_w_in), stored(v_hy_w_in), "adamw_in_t")
    for dst, part, t in ((grads, g2, g_in), (delta, d2, d_in), (new_m, nm2, m_in), (new_v, nv2, v_in)):
        dst.update(_unshard_2d(part))
        dst["hy_w_in"] = jnp.transpose(t, (1, 2, 0))

    return (loss, dx0[None], *[grads[n] for n in names], *[delta[n] for n in names],
            *[new_m[n] for n in names], *[new_v[n] for n in names])
```

```python
import functools
import math

import numpy as np
import jax
import jax.numpy as jnp
from jax import lax
from jax.experimental import pallas as pl
from jax.experimental.pallas import tpu as pltpu

f32 = jnp.float32
bf16 = jnp.bfloat16
EPS = 1e-6
N_DEV = 8
LANES = 128
SSM_STATE = 128
SSM_CHUNK = 128
SSM_GROUPS = 4
HEAD_DIM = 64
ATT_BLK = 128
ATT_DILATIONS = (1, 4, 16)
REL_BUCKETS = 32
REL_MAX_DIST = 2048
ADAM_LR, ADAM_B1, ADAM_B2, ADAM_EPS, ADAM_WD, ADAM_STEP = 0.001, 0.9, 0.999, 1e-08, 0.01, 10
PACK_ROW_TILE = 256
MESH = pl.DeviceIdType.MESH
VMEM_LIMIT = 48 * 1024 * 1024


def _sds(shape, dtype=f32):
    return jax.ShapeDtypeStruct(tuple(shape), dtype)


def _tile(n, cap, mult):
    best = None
    t = mult
    while t <= min(n, cap):
        if n % t == 0:
            best = t
        t += mult
    return best if best is not None else n


def _params(sem):
    return pltpu.CompilerParams(dimension_semantics=sem, vmem_limit_bytes=VMEM_LIMIT)


def _mm(a, b, *, name, ta=False, tb=False, b_rows=None, bias=None, add=None, out_dtype=f32,
        tm_cap=512, tn_cap=1536, tk_cap=8192):
    if ta:
        K, M = a.shape
    else:
        M, K = a.shape
    off, cnt = b_rows if b_rows is not None else (0, b.shape[0])
    if tb:
        N, K2 = cnt, b.shape[1]
    else:
        K2, N = cnt, b.shape[1]
    assert K == K2, (a.shape, b.shape, ta, tb, b_rows)
    if ta and a.dtype == f32:
        tm_cap = min(tm_cap, 256)
    tm = _tile(M, tm_cap, LANES)
    tn = _tile(math.gcd(off, N) if tb else N, tn_cap, LANES)
    tk = _tile(K if tb else math.gcd(off, K), tk_cap, LANES)
    assert N % tn == 0 and K % tk == 0 and off % (tn if tb else tk) == 0, (name, off, N, K, tn, tk)
    nk = K // tk
    jo, ko = (off // tn, 0) if tb else (0, off // tk)
    has_bias, has_add = bias is not None, add is not None
    dn = (((0 if ta else 1,), (1 if tb else 0,)), ((), ()))

    def body(*refs):
        a_ref, b_ref = refs[0], refs[1]
        pos = 2
        bias_ref = add_ref = None
        if has_bias:
            bias_ref = refs[pos]
            pos += 1
        if has_add:
            add_ref = refs[pos]
            pos += 1
        o_ref = refs[pos]
        k = pl.program_id(2)
        part = lax.dot_general(a_ref[...].astype(bf16), b_ref[...].astype(bf16), dn, preferred_element_type=f32)

        def finish(r):
            if has_bias:
                r = r + bias_ref[...]
            if has_add:
                r = r + add_ref[...]
            o_ref[...] = r.astype(o_ref.dtype)

        if nk == 1:
            finish(part)
        else:
            acc_ref = refs[pos + 1]

            @pl.when(k == 0)
            def _():
                acc_ref[...] = part

            @pl.when((k > 0) & (k < nk - 1))
            def _():
                acc_ref[...] += part

            @pl.when(k == nk - 1)
            def _():
                finish(acc_ref[...] + part)

    in_specs = [
        pl.BlockSpec((tk, tm), lambda i, j, k: (k, i)) if ta else pl.BlockSpec((tm, tk), lambda i, j, k: (i, k)),
        pl.BlockSpec((tn, tk), lambda i, j, k: (j + jo, k)) if tb else pl.BlockSpec((tk, tn), lambda i, j, k: (k + ko, j)),
    ]
    args = [a, b]
    if has_bias:
        in_specs.append(pl.BlockSpec((1, tn), lambda i, j, k: (0, j)))
        args.append(bias)
    if has_add:
        in_specs.append(pl.BlockSpec((tm, tn), lambda i, j, k: (i, j)))
        args.append(add)
    return pl.pallas_call(
        body, name=name, grid=(M // tm, N // tn, nk), in_specs=in_specs,
        out_specs=pl.BlockSpec((tm, tn), lambda i, j, k: (i, j)), out_shape=_sds((M, N), out_dtype),
        scratch_shapes=[pltpu.VMEM((tm, tn), f32)] if nk > 1 else [],
        compiler_params=_params(("parallel", "parallel", "arbitrary")),
    )(*args)


def _rowwise(name, fn, rows, vecs, out_rows, out_accs, *, tr_cap=256, sub=8, col_chunk=None):
    rows = [r if isinstance(r, tuple) else (r, 0, r.shape[1]) for r in rows]
    R = rows[0][0].shape[0]
    tr = _tile(R, tr_cap, 8)
    sub = sub if tr % sub == 0 else tr
    n_r, n_v, n_or, n_oa = len(rows), len(vecs), len(out_rows), len(out_accs)

    def body(*refs):
        row_refs = refs[:n_r]
        vec_refs = refs[n_r:n_r + n_v]
        orow_refs = refs[n_r + n_v:n_r + n_v + n_or]
        oacc_refs = refs[n_r + n_v + n_or:]
        vv = [r[...] for r in vec_refs]

        n_sub = tr // sub
        together = 4 if n_sub % 4 == 0 else 1

        def step(s, accs):
            for t in range(together):
                sl = pl.ds(pl.multiple_of((s * together + t) * sub, sub), sub)
                if col_chunk is None:
                    ro, ao = fn([r[sl, :] for r in row_refs], vv)
                    for o_ref, o in zip(orow_refs, ro):
                        o_ref[sl, :] = o.astype(o_ref.dtype)
                    accs = tuple(x + y for x, y in zip(accs, ao))
                else:
                    for c0 in range(0, rows[0][2], col_chunk):
                        cs_ = pl.ds(c0, col_chunk)
                        ro, _ = fn([r[sl, cs_] for r in row_refs], vv)
                        for o_ref, o in zip(orow_refs, ro):
                            o_ref[sl, cs_] = o.astype(o_ref.dtype)
            return accs

        accs = lax.fori_loop(0, n_sub // together, step, tuple(jnp.zeros((1, w), f32) for w in out_accs))
        if n_oa:
            @pl.when(pl.program_id(0) == 0)
            def _():
                for ref in oacc_refs:
                    ref[...] = jnp.zeros_like(ref)

            for ref, x in zip(oacc_refs, accs):
                ref[...] += x

    in_specs = [pl.BlockSpec((tr, w), functools.partial(lambda i, cb: (i, cb), cb=cb)) for (_, cb, w) in rows]
    in_specs += [pl.BlockSpec((1, v.shape[1]), lambda i: (0, 0)) for v in vecs]
    out_specs = [pl.BlockSpec((tr, w), lambda i: (i, 0)) for (w, _) in out_rows]
    out_specs += [pl.BlockSpec((1, w), lambda i: (0, 0)) for w in out_accs]
    out_shape = [_sds((R, w), dt) for (w, dt) in out_rows] + [_sds((1, w)) for w in out_accs]
    res = pl.pallas_call(
        body, name=name, grid=(R // tr,), in_specs=in_specs, out_specs=out_specs, out_shape=out_shape,
        compiler_params=_params(("arbitrary",)),
    )(*[r[0] for r in rows], *vecs)
    return res[:n_or], res[n_or:]


def _silu(x):
    return x * jax.nn.sigmoid(x)


def _rms(x, g):
    return x * lax.rsqrt(jnp.mean(x * x, -1, keepdims=True) + EPS) * g


def _adaln_f(x, g, sc, sh):
    return _rms(x, g) * (1.0 + sc) + sh


def _gate_f(y, z, g):
    return _rms(y * _silu(z), g)


def _lnsilu_f(u, g, b):
    mu = jnp.mean(u, -1, keepdims=True)
    var = jnp.mean(jnp.square(u - mu), -1, keepdims=True)
    return _silu((u - mu) * lax.rsqrt(var + EPS) * g + b)


def _adaln_fwd(x, g, sc, sh, name):
    (h,), _ = _rowwise(name, lambda rv, vv: ([_adaln_f(rv[0], *vv)], []), [x], [g, sc, sh], [(x.shape[1], bf16)], [],
                       sub=16)
    return h


def _adaln_bwd(x, g, sc, sh, dh, dres, name):
    def fn(rv, vv):
        xv, dhv, drv = rv
        _, vjp = jax.vjp(_adaln_f, xv, *vv)
        dx, dg, dsc, dsh = vjp(dhv)
        return [dx + drv], [dg, dsc, dsh]
    w = x.shape[1]
    (dx,), accs = _rowwise(name, fn, [x, dh, dres], [g, sc, sh], [(w, f32)], [w, w, w])
    return dx, accs


def _resid_adaln_fwd(x, gate, mix, g, sc, sh, name):
    def fn(rv, vv):
        xn = rv[0] + vv[0] * rv[1]
        return [xn, _adaln_f(xn, vv[1], vv[2], vv[3])], []
    w = x.shape[1]
    (xn, h), _ = _rowwise(name, fn, [x, mix], [gate, g, sc, sh], [(w, f32), (w, bf16)], [], sub=16)
    return xn, h


def _adaln_resid_bwd(x, g, sc, sh, dh, dres, mix, gate, name):
    def fn(rv, vv):
        xv, dhv, drv, mv = rv
        _, vjp = jax.vjp(_adaln_f, xv, vv[0], vv[1], vv[2])
        dx, dg, dsc, dsh = vjp(dhv)
        dx = dx + drv
        dm = vv[3] * dx
        return [dx, dm], [dg, dsc, dsh, jnp.sum(dx * mv, 0, keepdims=True), jnp.sum(dm, 0, keepdims=True)]
    w = x.shape[1]
    (dx, dmix), accs = _rowwise(name, fn, [x, dh, dres, mix], [g, sc, sh, gate], [(w, f32), (w, bf16)], [w] * 5, sub=16)
    return dx, dmix, accs


def _add3(a, b, c, name):
    (y,), _ = _rowwise(name, lambda rv, vv: ([rv[0] + rv[1] + rv[2]], []), [a, b, c], [], [(a.shape[1], bf16)], [],
                       sub=16)
    return y


CONV_HALO = 32
CONV_ROWS = 64


def _conv_fwd(x, w, b, *, silu, name, tr=512):
    S, C = x.shape
    K = w.shape[0]
    H = CONV_HALO
    assert K - 1 <= H and S % tr == 0 and tr % H == 0 and C % LANES == 0
    nh = tr // H

    def body(xp_ref, xc_ref, w_ref, b_ref, *rest):
        outs, scr = rest[:-1], rest[-1]
        i = pl.program_id(1)
        scr[pl.ds(0, H), :] = jnp.where(i > 0, xp_ref[...], 0.0)
        scr[pl.ds(H, tr), :] = xc_ref[...]
        taps = [w_ref[pl.ds(k, 1), :] for k in range(K)]
        for c0 in range(0, tr, CONV_ROWS):
            acc = jnp.zeros((CONV_ROWS, LANES), f32) + b_ref[...]
            for k in range(K):
                acc = acc + scr[pl.ds(c0 + H - (K - 1) + k, CONV_ROWS), :] * taps[k]
            outs[0][pl.ds(c0, CONV_ROWS), :] = acc.astype(outs[0].dtype)
            if silu:
                outs[1][pl.ds(c0, CONV_ROWS), :] = _silu(acc)

    n_out = 2 if silu else 1
    return pl.pallas_call(
        body, name=name, grid=(C // LANES, S // tr),
        in_specs=[pl.BlockSpec((H, LANES), lambda j, i: (jnp.maximum(i * nh - 1, 0), j)),
                  pl.BlockSpec((tr, LANES), lambda j, i: (i, j)),
                  pl.BlockSpec((K, LANES), lambda j, i: (0, j)),
                  pl.BlockSpec((1, LANES), lambda j, i: (0, j))],
        out_specs=[pl.BlockSpec((tr, LANES), lambda j, i: (i, j))] * n_out,
        out_shape=[_sds((S, C), bf16), _sds((S, C))] if silu else [_sds((S, C))],
        scratch_shapes=[pltpu.VMEM((tr + H, LANES), f32)],
        compiler_params=_params(("parallel", "arbitrary")),
    )(x, x, w, b)


def _conv_bwd(x, w, dact, pre, *, silu, name, dx_dtype=f32, tr=512):
    S, C = x.shape
    K = w.shape[0]
    H = CONV_HALO
    nh = tr // H
    n_i = S // tr
    kp = -(-K // 8) * 8

    def dsilu(p):
        s = jax.nn.sigmoid(p)
        return s * (1.0 + p * (1.0 - s))

    def body(*refs):
        if silu:
            xp_ref, xc_ref, w_ref, dc_ref, dn_ref, pc_ref, pn_ref, dx_ref, dw_ref, db_ref, xs, ds = refs
        else:
            xp_ref, xc_ref, w_ref, dc_ref, dn_ref, dx_ref, dw_ref, db_ref, xs, ds = refs
        i = pl.program_id(1)
        xs[pl.ds(0, H), :] = jnp.where(i > 0, xp_ref[...], 0.0)
        xs[pl.ds(H, tr), :] = xc_ref[...]
        dcur = dc_ref[...]
        dnext = dn_ref[...]
        if silu:
            dcur = dcur * dsilu(pc_ref[...].astype(f32))
            dnext = dnext * dsilu(pn_ref[...].astype(f32))
        ds[pl.ds(0, tr), :] = dcur
        ds[pl.ds(tr, H), :] = jnp.where(i < n_i - 1, dnext, 0.0)
        taps = [w_ref[pl.ds(k, 1), :] for k in range(K)]
        fold = lambda t: jnp.sum(t.reshape(CONV_ROWS // 8, 8, LANES), axis=0)
        dw_parts = [jnp.zeros((8, LANES), f32) for _ in range(K)]
        db_part = jnp.zeros((8, LANES), f32)
        for c0 in range(0, tr, CONV_ROWS):
            acc = jnp.zeros((CONV_ROWS, LANES), f32)
            d_c = ds[pl.ds(c0, CONV_ROWS), :]
            for k in range(K):
                acc = acc + ds[pl.ds(c0 + K - 1 - k, CONV_ROWS), :] * taps[k]
                dw_parts[k] = dw_parts[k] + fold(d_c * xs[pl.ds(c0 + H - (K - 1) + k, CONV_ROWS), :])
            db_part = db_part + fold(d_c)
            dx_ref[pl.ds(c0, CONV_ROWS), :] = acc.astype(dx_ref.dtype)

        @pl.when(i == 0)
        def _():
            dw_ref[...] = jnp.zeros_like(dw_ref)
            db_ref[...] = jnp.zeros_like(db_ref)

        for k in range(K):
            dw_ref[pl.ds(k, 1), :] += jnp.sum(dw_parts[k], 0, keepdims=True)
        db_ref[...] += jnp.sum(db_part, 0, keepdims=True)

    prev = pl.BlockSpec((H, LANES), lambda j, i: (jnp.maximum(i * nh - 1, 0), j))
    cur = pl.BlockSpec((tr, LANES), lambda j, i: (i, j))
    nxt = pl.BlockSpec((H, LANES), lambda j, i: (jnp.minimum((i + 1) * nh, n_i * nh - 1), j))
    in_specs = [prev, cur, pl.BlockSpec((K, LANES), lambda j, i: (0, j)), cur, nxt]
    args = [x, x, w, dact, dact]
    if silu:
        in_specs += [cur, nxt]
        args += [pre, pre]
    dx, dw, db = pl.pallas_call(
        body, name=name, grid=(C // LANES, n_i), in_specs=in_specs,
        out_specs=[cur, pl.BlockSpec((kp, LANES), lambda j, i: (0, j)), pl.BlockSpec((1, LANES), lambda j, i: (0, j))],
        out_shape=[_sds((S, C), dx_dtype), _sds((kp, C)), _sds((1, C))],
        scratch_shapes=[pltpu.VMEM((tr + H, LANES), f32), pltpu.VMEM((tr + H, LANES), f32)],
        compiler_params=_params(("parallel", "arbitrary")),
    )(*args)
    return dx, dw[:K], db


def _dot(a, b):
    return jnp.dot(a.astype(bf16), b.astype(bf16), preferred_element_type=f32)


def _dot_nt(a, b):
    return lax.dot_general(a.astype(bf16), b.astype(bf16), (((1,), (1,)), ((), ())), preferred_element_type=f32)


def _dot_tn(a, b):
    return lax.dot_general(a.astype(bf16), b.astype(bf16), (((0,), (0,)), ((), ())), preferred_element_type=f32)


def _softplus(x):
    return jnp.maximum(x, 0.0) + jnp.log(1.0 + jnp.exp(-jnp.abs(x)))


def _tri(q):
    i = lax.broadcasted_iota(jnp.int32, (q, q), 0)
    j = lax.broadcasted_iota(jnp.int32, (q, q), 1)
    return i >= j


def _ssd_prep(dtraw, dt_bias, a_log):
    q = dtraw.shape[0]
    dt = _softplus(dtraw + dt_bias)
    A = -jnp.exp(a_log)
    tri = _tri(q)
    cs = jnp.dot(tri.astype(f32), dt * A, preferred_element_type=f32, precision=lax.Precision.HIGHEST)
    return dt, A, cs, cs.T, tri


def _expand(cols, h0, n, width):
    q = cols.shape[0]
    return jnp.concatenate([jnp.broadcast_to(cols[:, h0 + r:h0 + r + 1], (q, width)) for r in range(n)], axis=1)


def _ssd_fwd(xbc, dtraw, dt_bias, a_log, d_skip, di, name):
    S, CD = xbc.shape
    Q, N, G = SSM_CHUNK, SSM_STATE, SSM_GROUPS
    nc = S // Q
    nh = di // HEAD_DIM
    R = nh // G
    gw = R * HEAD_DIM

    def body(xbc_ref, dt_ref, bias_ref, alog_ref, dsk_ref, y_ref, hin_ref, state):
        c = pl.program_id(0)

        @pl.when(c == 0)
        def _():
            state[...] = jnp.zeros_like(state)

        hin_ref[...] = state[...]
        dt, A, cs, csT, tri = _ssd_prep(dt_ref[...], bias_ref[...], alog_ref[...])
        dsk = dsk_ref[...]
        ecs = jnp.exp(cs)
        dend = jnp.exp(cs[Q - 1:Q, :] - cs)
        elast = jnp.exp(cs[Q - 1:Q, :])
        for g in range(G):
            h0 = g * R
            Bg = xbc_ref[:, pl.ds(di + g * N, N)]
            Cg = xbc_ref[:, pl.ds(di + G * N + g * N, N)]
            xg = xbc_ref[:, pl.ds(g * gw, gw)]
            Hg = state[pl.ds(g * gw, gw), :]
            Gm = _dot_nt(Cg, Bg)
            xdt = xg * _expand(dt, h0, R, HEAD_DIM)
            yoff = _dot_nt(Cg, Hg) * _expand(ecs, h0, R, HEAD_DIM)
            ys = []
            for r in range(R):
                h = h0 + r
                L = jnp.exp(jnp.where(tri, cs[:, h:h + 1] - csT[h:h + 1, :], -jnp.inf))
                ys.append(_dot(Gm * L, xdt[:, r * HEAD_DIM:(r + 1) * HEAD_DIM]))
            y = jnp.concatenate(ys, axis=1) + yoff + xg * _expand(dsk, h0, R, HEAD_DIM)
            y_ref[:, pl.ds(g * gw, gw)] = y
            hnew = _dot_tn(xdt * _expand(dend, h0, R, HEAD_DIM), Bg)
            escale = jnp.concatenate([jnp.broadcast_to(elast[:, h0 + r:h0 + r + 1], (HEAD_DIM, N)) for r in range(R)], axis=0)
            state[pl.ds(g * gw, gw), :] = escale * Hg + hnew

    vec = pl.BlockSpec((1, LANES), lambda c: (0, 0))
    return pl.pallas_call(
        body, name=name, grid=(nc,),
        in_specs=[pl.BlockSpec((Q, CD), lambda c: (c, 0)), pl.BlockSpec((Q, LANES), lambda c: (c, 0)), vec, vec, vec],
        out_specs=[pl.BlockSpec((Q, di), lambda c: (c, 0)), pl.BlockSpec((None, di, N), lambda c: (c, 0, 0))],
        out_shape=[_sds((S, di)), _sds((nc, di, N))],
        scratch_shapes=[pltpu.VMEM((di, N), f32)],
        compiler_params=_params(("arbitrary",)),
    )(xbc, dtraw, dt_bias, a_log, d_skip)


def _dot_exact(a, b):
    bb = b.astype(bf16)
    hi = a.astype(bf16)
    rest = a - hi.astype(f32)
    mid = rest.astype(bf16)
    low = (rest - mid.astype(f32)).astype(bf16)
    one_pass = lambda t: jnp.dot(t, bb, preferred_element_type=f32)
    return one_pass(hi) + one_pass(mid) + one_pass(low)


def _ssd_bwd(xbc, dtraw, dt_bias, a_log, d_skip, hin, y, dy, di, name):
    S, CD = xbc.shape
    Q, N, G = SSM_CHUNK, SSM_STATE, SSM_GROUPS
    nc = S // Q
    nh = di // HEAD_DIM
    R = nh // G
    gw = R * HEAD_DIM
    P = HEAD_DIM
    head_of_col = jnp.asarray((np.arange(di)[:, None] // P == np.arange(LANES)[None, :]).astype(np.float32))
    dsk_wide = jnp.repeat(d_skip[0, :nh], P)[None]

    def body(xbc_ref, dt_ref, bias_ref, alog_ref, dskw_ref, hoc_ref, hin_ref, y_ref, dy_ref,
             dxbc_ref, ddt_ref, dA_ref, ddsk_ref, dtb_ref, dstate, dxdt_all, tend_all, yoff_all, colterm_all):
        c = pl.program_id(0)

        @pl.when(c == 0)
        def _():
            dstate[...] = jnp.zeros_like(dstate)
            dA_ref[...] = jnp.zeros_like(dA_ref)
            ddsk_ref[...] = jnp.zeros_like(ddsk_ref)
            dtb_ref[...] = jnp.zeros_like(dtb_ref)

        dtraw_v = dt_ref[...]
        dt, A, cs, csT, tri = _ssd_prep(dtraw_v, bias_ref[...], alog_ref[...])
        tri_t = jnp.logical_not(tri) | (lax.broadcasted_iota(jnp.int32, (Q, Q), 0) == lax.broadcasted_iota(jnp.int32, (Q, Q), 1))
        ecs = jnp.exp(cs)
        dend = jnp.exp(cs[Q - 1:Q, :] - cs)
        elast = jnp.exp(cs[Q - 1:Q, :])
        hoc = hoc_ref[...]
        state_dot = jnp.sum(_dot_exact(dstate[...] * hin_ref[...], jnp.ones((N, LANES), f32)) * hoc, 0, keepdims=True) * elast
        for g in range(G):
            h0 = g * R
            Bg = xbc_ref[:, pl.ds(di + g * N, N)]
            Cg = xbc_ref[:, pl.ds(di + G * N + g * N, N)]
            xg = xbc_ref[:, pl.ds(g * gw, gw)]
            dyg = dy_ref[:, pl.ds(g * gw, gw)]
            Hg = hin_ref[pl.ds(g * gw, gw), :]
            dHg = dstate[pl.ds(g * gw, gw), :]
            dt_e = _expand(dt, h0, R, P)
            ecs_e = _expand(ecs, h0, R, P)
            dend_e = _expand(dend, h0, R, P)
            cols = pl.ds(g * gw, gw)
            Gm = _dot_nt(Cg, Bg)
            Gm_t = _dot_nt(Bg, Cg)
            xdt = xg * dt_e
            dye = dyg * ecs_e
            bdh = _dot_nt(Bg, dHg)
            dC = _dot(dye, Hg)
            dB = _dot(xdt * dend_e, dHg)
            dHin = _dot_tn(dye, Cg)
            dxdt_state = dend_e * bdh
            end_term = xdt * dxdt_state
            tend_all[:, cols] = end_term
            yoff_all[:, cols] = _dot_nt(Cg, Hg) * ecs_e
            dG = jnp.zeros((Q, Q), f32)
            dxd = []
            for r in range(R):
                h = h0 + r
                sl = slice(r * P, (r + 1) * P)
                seg = cs[:, h:h + 1] - csT[h:h + 1, :]
                L = jnp.exp(jnp.where(tri, seg, -jnp.inf))
                L_t = jnp.exp(jnp.where(tri_t, -seg, -jnp.inf))
                dyh = dyg[:, sl]
                dG = dG + _dot_nt(dyh, xdt[:, sl]) * L
                dxd.append(_dot(Gm_t * L_t, dyh))
            dxdt_diag = jnp.concatenate(dxd, axis=1)
            dxdt = dxdt_diag + dxdt_state
            dxdt_all[:, cols] = dxdt
            colterm_all[:, cols] = xdt.astype(bf16).astype(f32) * dxdt_diag + end_term
            dxbc_ref[:, cols] = dxdt * dt_e + dyg * dskw_ref[:, cols]
            dxbc_ref[:, pl.ds(di + g * N, N)] = dB + _dot_tn(dG, Cg)
            dxbc_ref[:, pl.ds(di + G * N + g * N, N)] = dC + _dot(dG, Bg)
            escale = jnp.concatenate([jnp.broadcast_to(elast[:, h0 + r:h0 + r + 1], (P, N)) for r in range(R)], axis=0)
            dstate[pl.ds(g * gw, gw), :] = escale * dHg + dHin
        xs = xbc_ref[:, pl.ds(0, di)]
        dyv = dy_ref[...]
        yoff = yoff_all[...]
        y_diag = y_ref[...] - dskw_ref[...] * xs - yoff
        rs_y = _dot_exact(dyv.astype(bf16).astype(f32) * y_diag + dyv * yoff, hoc)
        rs_c = _dot_exact(colterm_all[...], hoc)
        rs_x = _dot_exact(dxdt_all[...] * xs, hoc)
        end_dot = _dot_exact(jnp.broadcast_to(jnp.sum(tend_all[...], 0, keepdims=True), (8, di)), hoc)[0:1]
        last = lax.broadcasted_iota(jnp.int32, (Q, 1), 0) == Q - 1
        dcs = rs_y - rs_c + jnp.where(last, end_dot + state_dot, 0.0)
        da = lax.dot_general(tri.astype(f32), dcs, (((0,), (0,)), ((), ())), preferred_element_type=f32,
                             precision=lax.Precision.HIGHEST)
        ddt = da * A + rs_x
        ddtraw = ddt * jax.nn.sigmoid(dtraw_v + bias_ref[...])
        ddt_ref[...] = ddtraw.astype(ddt_ref.dtype)
        dA_ref[...] += jnp.sum(da * dt, 0, keepdims=True) * A
        ddsk_ref[...] += jnp.sum(_dot_exact(dyv * xs, hoc), 0, keepdims=True)
        dtb_ref[...] += jnp.sum(ddtraw, 0, keepdims=True)

    vec = pl.BlockSpec((1, LANES), lambda c: (0, 0))
    rev = lambda c: (nc - 1 - c, 0)
    return pl.pallas_call(
        body, name=name, grid=(nc,),
        in_specs=[pl.BlockSpec((Q, CD), rev), pl.BlockSpec((Q, LANES), rev), vec, vec,
                  pl.BlockSpec((1, di), lambda c: (0, 0)), pl.BlockSpec((di, LANES), lambda c: (0, 0)),
                  pl.BlockSpec((None, di, N), lambda c: (nc - 1 - c, 0, 0)), pl.BlockSpec((Q, di), rev),
                  pl.BlockSpec((Q, di), rev)],
        out_specs=[pl.BlockSpec((Q, CD), rev), pl.BlockSpec((Q, LANES), rev), vec, vec, vec],
        out_shape=[_sds((S, CD)), _sds((S, LANES), bf16), _sds((1, LANES)), _sds((1, LANES)), _sds((1, LANES))],
        scratch_shapes=[pltpu.VMEM((di, N), f32)] + [pltpu.VMEM((Q, di), f32)] * 4,
        compiler_params=_params(("arbitrary",)),
    )(xbc, dtraw, dt_bias, a_log, dsk_wide, head_of_col, hin, y, dy)


def _t5_bucket_np(dist):
    max_exact = REL_BUCKETS // 2
    n = np.maximum(dist, 1).astype(np.float32)
    large = np.float32(max_exact) + np.log(n / np.float32(max_exact)) / np.float32(math.log(REL_MAX_DIST / max_exact)) * np.float32(REL_BUCKETS - max_exact)
    large = np.minimum(large.astype(np.int32), REL_BUCKETS - 1)
    return np.where(dist < max_exact, dist, large)


def _bucket_onehot():
    i = np.arange(ATT_BLK)[None, :]
    j = np.arange(2 * ATT_BLK)[:, None]
    delta = np.maximum(ATT_BLK + i - j, 0)
    out = np.zeros((len(ATT_DILATIONS), REL_BUCKETS, ATT_BLK * 2 * ATT_BLK), np.float32)
    for gi, d in enumerate(ATT_DILATIONS):
        b = _t5_bucket_np(delta * d).reshape(-1)
        out[gi, b, np.arange(b.size)] = 1.0
    return out


def _exact_mm(a, b, *, name, tb=False):
    M, K = a.shape
    N = b.shape[0] if tb else b.shape[1]
    tn = _tile(N, 4096, LANES)
    dn = (((1,), (1 if tb else 0,)), ((), ()))

    def body(a_ref, b_ref, o_ref):
        o_ref[...] = lax.dot_general(a_ref[...], b_ref[...], dn, preferred_element_type=f32,
                                     precision=lax.Precision.HIGHEST)

    return pl.pallas_call(
        body, name=name, grid=(N // tn,),
        in_specs=[pl.BlockSpec((M, K), lambda j: (0, 0)),
                  pl.BlockSpec((tn, K), lambda j: (j, 0)) if tb else pl.BlockSpec((K, tn), lambda j: (0, j))],
        out_specs=pl.BlockSpec((M, tn), lambda j: (0, j)), out_shape=_sds((M, N)),
        compiler_params=_params(("parallel",)),
    )(a, b)


def _band_penalty():
    i = np.arange(ATT_BLK)[None, :]
    j = np.arange(2 * ATT_BLK)[:, None]
    delta = ATT_BLK + i - j
    return np.where((delta >= 0) & (delta <= ATT_BLK), 0.0, -np.inf).astype(np.float32)


def _first_block_keep(n):
    key = lax.broadcasted_iota(jnp.int32, (2 * ATT_BLK, ATT_BLK), 0)
    return (key >= ATT_BLK) | (n > 0)


ATT_SCALE = HEAD_DIM ** -0.5


def _rows(ref, r, d):
    return ref[...] if d == 1 else ref[pl.ds(r, ATT_BLK, stride=d), :]


def _set_rows(ref, r, d, val):
    if d == 1:
        ref[...] = val
    else:
        ref[pl.ds(r, ATT_BLK, stride=d), :] = val


def _attn_width(d, D):
    return D if d == 1 else LANES


def _over_residues(d, one, unroll=1):
    if d == 1:
        one(0)
    else:
        lax.fori_loop(0, d, lambda r, c: (one(r), c)[1], 0, unroll=unroll)


def _attn_fwd(q, k, v, bias_t, d, name):
    S, D = q.shape
    nb = S // (d * ATT_BLK)
    H = D // HEAD_DIM
    W = _attn_width(d, D)
    HB = W // HEAD_DIM

    def body(q_ref, kp_ref, kc_ref, vp_ref, vc_ref, b_ref, o_ref, lse_ref):
        keep = _first_block_keep(pl.program_id(1))
        first = lax.broadcasted_iota(jnp.int32, (1, LANES), 1) < HEAD_DIM

        def one(r):
            qs = (_rows(q_ref, r, d) * ATT_SCALE).astype(bf16)
            kcat = jnp.concatenate([_rows(kp_ref, r, d), _rows(kc_ref, r, d)], axis=0).astype(bf16)
            vcat = jnp.concatenate([_rows(vp_ref, r, d), _rows(vc_ref, r, d)], axis=0).astype(bf16)
            outs = []
            for pair in range(W // LANES):
                ps = slice(pair * LANES, (pair + 1) * LANES)
                q2, k2, v2 = qs[:, ps], kcat[:, ps], vcat[:, ps]
                o2 = jnp.zeros((ATT_BLK, LANES), f32)
                for e in range(2):
                    h = 2 * pair + e
                    mine = first if e == 0 else jnp.logical_not(first)
                    zero = jnp.zeros((), bf16)
                    st = jnp.where(keep, _dot_nt(k2, jnp.where(mine, q2, zero)) + b_ref[h], -jnp.inf)
                    m = jnp.max(st, 0, keepdims=True)
                    pt = jnp.exp(st - m)
                    l = jnp.sum(pt, 0, keepdims=True)
                    o2 = o2 + _dot_tn(pt * (1.0 / l), jnp.where(mine, v2, zero))
                    lse_ref[r, h] = m + jnp.log(l)
                outs.append(o2)
            _set_rows(o_ref, r, d, jnp.concatenate(outs, axis=1))

        _over_residues(d, one, unroll=4)

    cur = pl.BlockSpec((ATT_BLK * d, W), lambda j, n: (n, j))
    prev = pl.BlockSpec((ATT_BLK * d, W), lambda j, n: (jnp.maximum(n - 1, 0), j))
    return pl.pallas_call(
        body, name=name, grid=(D // W, nb),
        in_specs=[cur, prev, cur, prev, cur, pl.BlockSpec((HB, 2 * ATT_BLK, ATT_BLK), lambda j, n: (j, 0, 0))],
        out_specs=[cur, pl.BlockSpec((None, d, HB, 1, LANES), lambda j, n: (n, 0, j, 0, 0))],
        out_shape=[_sds((S, D)), _sds((nb, d, H, 1, LANES))],
        compiler_params=_params(("parallel", "arbitrary")),
    )(q, k, k, v, v, bias_t)


def _from_blocks(rows, lanes=None):
    nb, d, H = rows.shape[:3]
    a = jnp.transpose(rows[:, :, :, 0, :], (0, 3, 1, 2)).reshape(nb * ATT_BLK * d, H)
    return a if lanes is None else jnp.pad(a, ((0, 0), (0, lanes - H)))


def _by_block(a, d):
    S, H = a.shape
    t = jnp.transpose(a.reshape(S // (d * ATT_BLK), ATT_BLK, d, H), (0, 2, 3, 1))
    return t[:, :, :, None, :]


def _head_sums(a, b, name):
    S, D = a.shape
    tr = _tile(S, 512, 8)
    hoc = jnp.asarray((np.arange(D)[:, None] // HEAD_DIM == np.arange(LANES)[None, :]).astype(np.float32))

    def body(a_ref, b_ref, h_ref, o_ref):
        o_ref[...] = _dot_exact(a_ref[...] * b_ref[...], h_ref[...])

    return pl.pallas_call(
        body, name=name, grid=(S // tr,),
        in_specs=[pl.BlockSpec((tr, D), lambda i: (i, 0)), pl.BlockSpec((tr, D), lambda i: (i, 0)),
                  pl.BlockSpec((D, LANES), lambda i: (0, 0))],
        out_specs=pl.BlockSpec((tr, LANES), lambda i: (i, 0)), out_shape=_sds((S, LANES)),
        compiler_params=_params(("parallel",)),
    )(a, b, hoc)


def _attn_bwd(q, k, v, bias_t, datt, lse_rows, dsum_rows, d, name):
    S, D = q.shape
    nb = S // (d * ATT_BLK)
    H = D // HEAD_DIM
    W = _attn_width(d, D)
    HB = W // HEAD_DIM

    def body(q_ref, kp_ref, kc_ref, vp_ref, vc_ref, b_ref, do_ref, lse_ref, dsum_ref,
             dq_ref, dk_ref, dv_ref, db_ref, carry_k, carry_v):
        j = pl.program_id(0)
        n = pl.program_id(1)

        @pl.when(n == 0)
        def _():
            carry_k[...] = jnp.zeros_like(carry_k)
            carry_v[...] = jnp.zeros_like(carry_v)
            db_ref[...] = jnp.zeros_like(db_ref)

        @pl.when(n < nb)
        def _():
            key = lax.broadcasted_iota(jnp.int32, (2 * ATT_BLK, ATT_BLK), 0)
            keep = (key >= ATT_BLK) | (n > 0)
            first = lax.broadcasted_iota(jnp.int32, (1, LANES), 1) < HEAD_DIM

            def one(r):
                qs = (_rows(q_ref, r, d) * ATT_SCALE).astype(bf16)
                kcat = jnp.concatenate([_rows(kp_ref, r, d), _rows(kc_ref, r, d)], axis=0).astype(bf16)
                vcat = jnp.concatenate([_rows(vp_ref, r, d), _rows(vc_ref, r, d)], axis=0).astype(bf16)
                dob = _rows(do_ref, r, d).astype(bf16)
                dqs, dks, dvs = [], [], []
                for pair in range(W // LANES):
                    ps = slice(pair * LANES, (pair + 1) * LANES)
                    q2, k2, v2, do2 = qs[:, ps], kcat[:, ps], vcat[:, ps], dob[:, ps]
                    dq2 = jnp.zeros((ATT_BLK, LANES), f32)
                    dk2 = jnp.zeros((2 * ATT_BLK, LANES), f32)
                    dv2 = jnp.zeros((2 * ATT_BLK, LANES), f32)
                    for e in range(2):
                        h = 2 * pair + e
                        mine = first if e == 0 else jnp.logical_not(first)
                        zero = jnp.zeros((), bf16)
                        qm, dom, km = jnp.where(mine, q2, zero), jnp.where(mine, do2, zero), jnp.where(mine, k2, zero)
                        st = jnp.where(keep, _dot_nt(k2, qm) + b_ref[h], -jnp.inf)
                        pt = jnp.exp(st - lse_ref[r, j * HB + h])
                        dst = pt * (_dot_nt(v2, dom) - dsum_ref[r, j * HB + h])
                        db_ref[h] += dst
                        dv2 = dv2 + _dot(pt, dom)
                        dk2 = dk2 + _dot(dst, qm)
                        dq2 = dq2 + _dot_tn(dst, km)
                    dqs.append(dq2 * ATT_SCALE)
                    dks.append(dk2)
                    dvs.append(dv2)
                _set_rows(dq_ref, r, d, jnp.concatenate(dqs, axis=1))
                dk = jnp.concatenate(dks, axis=1)
                dv = jnp.concatenate(dvs, axis=1)
                _set_rows(dk_ref, r, d, carry_k[r] + dk[:ATT_BLK])
                _set_rows(dv_ref, r, d, carry_v[r] + dv[:ATT_BLK])
                carry_k[r] = dk[ATT_BLK:]
                carry_v[r] = dv[ATT_BLK:]

            _over_residues(d, one, unroll=2)

        @pl.when(n == nb)
        def _():
            def last(r):
                _set_rows(dk_ref, r, d, carry_k[r])
                _set_rows(dv_ref, r, d, carry_v[r])

            _over_residues(d, last)

    nq = lambda n: jnp.minimum(n, nb - 1)
    cur = pl.BlockSpec((ATT_BLK * d, W), lambda j, n: (nq(n), j))
    prev = pl.BlockSpec((ATT_BLK * d, W), lambda j, n: (jnp.maximum(nq(n) - 1, 0), j))
    done = pl.BlockSpec((ATT_BLK * d, W), lambda j, n: (jnp.maximum(n - 1, 0), j))
    bspec = pl.BlockSpec((HB, 2 * ATT_BLK, ATT_BLK), lambda j, n: (j, 0, 0))
    rows = pl.BlockSpec((None, d, H, 1, LANES), lambda j, n: (nq(n), 0, 0, 0, 0))
    return pl.pallas_call(
        body, name=name, grid=(D // W, nb + 1),
        in_specs=[cur, prev, cur, prev, cur, bspec, cur, rows, rows],
        out_specs=[cur, done, done, bspec],
        out_shape=[_sds((S, D)), _sds((S, D)), _sds((S, D)), _sds((H, 2 * ATT_BLK, ATT_BLK))],
        scratch_shapes=[pltpu.VMEM((d, ATT_BLK, W), f32), pltpu.VMEM((d, ATT_BLK, W), f32)],
        compiler_params=_params(("arbitrary", "arbitrary")),
    )(q, k, k, v, v, bias_t, datt, lse_rows, dsum_rows)


def _attn_combine(os_, lses, name):
    S, D = os_[0].shape
    tr = _tile(S, 128, 16)
    head_cols = jnp.asarray((np.arange(LANES)[:, None] == np.arange(D)[None, :] // HEAD_DIM).astype(np.float32))

    def body(o0, o1, o2, l0, l1, l2, hc_ref, att_ref, attb_ref, lse_ref):
        a, b, c = l0[...], l1[...], l2[...]
        m = jnp.maximum(jnp.maximum(a, b), c)
        e0, e1, e2 = jnp.exp(a - m), jnp.exp(b - m), jnp.exp(c - m)
        tot = e0 + e1 + e2
        wide = lambda w: _dot_exact(w / tot, hc_ref[...])
        att = wide(e0) * o0[...] + wide(e1) * o1[...] + wide(e2) * o2[...]
        att_ref[...] = att
        attb_ref[...] = att.astype(bf16)
        lse_ref[...] = m + jnp.log(tot)

    wide_spec = pl.BlockSpec((tr, D), lambda i: (i, 0))
    lane_spec = pl.BlockSpec((tr, LANES), lambda i: (i, 0))
    return pl.pallas_call(
        body, name=name, grid=(S // tr,),
        in_specs=[wide_spec] * 3 + [lane_spec] * 3 + [pl.BlockSpec((LANES, D), lambda i: (0, 0))],
        out_specs=[wide_spec, wide_spec, lane_spec], out_shape=[_sds((S, D)), _sds((S, D), bf16), _sds((S, LANES))],
        compiler_params=_params(("parallel",)),
    )(*os_, *lses, head_cols)


ANY = pl.BlockSpec(memory_space=pl.ANY)


def _all_gather(vs, name):
    n = len(vs)

    def body(*refs):
        x_refs, out_refs = refs[:n], refs[n:2 * n]
        send_sems, recv_sems, local_sems = refs[2 * n:]
        x, y, c = lax.axis_index("x"), lax.axis_index("y"), lax.axis_index("c")
        me, sibling = (x, y, c), (x, y, 1 - c)
        chips = [(1 - x, y), (x, 1 - y), (1 - x, 1 - y)]

        def slot(i, px, py, pc):
            return out_refs[i].at[4 * px + 2 * py + pc]

        def copy(i, k, block, to, src=None):
            return pltpu.make_async_remote_copy(
                src_ref=slot(i, *block) if src is None else src, dst_ref=slot(i, *block),
                send_sem=send_sems.at[i, k], recv_sem=recv_sems.at[i, k], device_id=to, device_id_type=MESH)

        mine = [pltpu.make_async_copy(x_refs[i], slot(i, *me), local_sems.at[i]) for i in range(n)]
        for cp in mine:
            cp.start()
        first = []
        for i in range(n):
            first.append(copy(i, 0, me, sibling, src=x_refs[i]))
            first += [copy(i, 1 + j, me, (*chip, c), src=x_refs[i]) for j, chip in enumerate(chips)]
        for cp in first:
            cp.start()
        passed = []
        for i in range(n):
            for j, chip in enumerate(chips):
                copy(i, 1 + j, (*chip, c), me).wait_recv()
                cp = copy(i, 4 + j, (*chip, c), sibling)
                cp.start()
                passed.append(cp)
        for i in range(n):
            copy(i, 0, sibling, me).wait_recv()
            for j, chip in enumerate(chips):
                copy(i, 4 + j, (*chip, 1 - c), me).wait_recv()
        for cp in first + passed:
            cp.wait_send()
        for cp in mine:
            cp.wait()

    return pl.pallas_call(
        body, name=name, out_shape=[_sds((N_DEV,) + v.shape, v.dtype) for v in vs], in_specs=[ANY] * n,
        out_specs=[ANY] * n,
        scratch_shapes=[pltpu.SemaphoreType.DMA((n, 7)), pltpu.SemaphoreType.DMA((n, 7)), pltpu.SemaphoreType.DMA((n,))],
    )(*vs)


def _sum_slots(t, name):
    n, R, C = t.shape
    tr = _tile(R, PACK_ROW_TILE, 16)

    def body(t_ref, o_ref):
        acc = t_ref[0].astype(f32)
        for k in range(1, n):
            acc = acc + t_ref[k].astype(f32)
        o_ref[...] = acc

    return pl.pallas_call(
        body, name=name, grid=(R // tr,),
        in_specs=[pl.BlockSpec((n, tr, C), lambda i: (0, i, 0))],
        out_specs=pl.BlockSpec((tr, C), lambda i: (i, 0)), out_shape=_sds((R, C)),
        compiler_params=_params(("parallel",)),
    )(t)


HBM_SPEC = pl.BlockSpec(memory_space=pltpu.HBM)
SEM_SPEC = pl.BlockSpec(memory_space=pltpu.SEMAPHORE)
EFFECT = pltpu.SideEffectType.DATAFLOW_SIDE_EFFECTING


def _mesh_pos(p):
    return (p // 4, (p // 2) % 2, p % 2)


def _exchange_copy(src_refs, land_refs, send_sems, recv_sems, whole, dests, i, k):
    me = 4 * lax.axis_index("x") + 2 * lax.axis_index("y") + lax.axis_index("c")
    to = (me + k) % N_DEV
    frm = (me + N_DEV - k) % N_DEV
    lo, hi = dests
    src = src_refs[i] if whole else src_refs[i].at[jnp.minimum(jnp.maximum(to - lo, 0), hi - lo - 1)]
    s = i * (N_DEV - 1) + k - 1
    send = pltpu.make_async_remote_copy(src_ref=src, dst_ref=land_refs[i].at[me], send_sem=send_sems.at[s],
                                        recv_sem=recv_sems.at[s], device_id=_mesh_pos(to), device_id_type=MESH)
    recv = pltpu.make_async_remote_copy(src_ref=src, dst_ref=land_refs[i].at[frm], send_sem=send_sems.at[s],
                                        recv_sem=recv_sems.at[s], device_id=_mesh_pos(to), device_id_type=MESH)
    return send, recv, (to >= lo) & (to < hi), (me >= lo) & (me < hi)


def _exchange_start(srcs, whole, name, after=None, dests=(0, N_DEV)):
    n = len(srcs)
    lands = [lax.empty((N_DEV,) + s.shape[-2:], s.dtype) for s in srcs]
    after = list(after or [])
    n_in = 2 * n + len(after)
    everyone = dests == (0, N_DEV)

    def body(*refs):
        src_refs, land_refs = refs[:n], refs[n:2 * n]
        send_sems, recv_sems, token = refs[n_in], refs[n_in + 1], refs[-1]
        for i in range(n):
            for k in range(1, N_DEV):
                send, _, sends, _ = _exchange_copy(src_refs, land_refs, send_sems, recv_sems, whole, dests, i, k)
                if everyone:
                    send.start()
                else:
                    pl.when(sends)(send.start)
        token[...] = jnp.zeros_like(token)

    sems = pltpu.SemaphoreType.DMA((n * (N_DEV - 1),))
    outs = pl.pallas_call(
        body, name=name,
        out_shape=(sems, sems, *[pltpu.HBM(a.shape, a.dtype) for a in srcs + lands], _sds((8, LANES))),
        in_specs=[HBM_SPEC] * (2 * n) + [pl.BlockSpec(memory_space=pl.ANY)] * len(after),
        out_specs=(SEM_SPEC, SEM_SPEC, *[HBM_SPEC] * (2 * n), pl.BlockSpec(memory_space=pltpu.VMEM)),
        input_output_aliases={i: 2 + i for i in range(2 * n)},
        compiler_params=pltpu.CompilerParams(has_side_effects=EFFECT),
    )(*[pltpu.with_memory_space_constraint(a, pltpu.HBM) for a in srcs + lands], *after)
    return (outs[0], outs[1], list(outs[2:2 + n]), list(outs[2 + n:2 + 2 * n]), whole, dests), outs[-1]


def _exchange_wait(handle, after, name):
    send_sems, recv_sems, srcs, lands, whole, dests = handle
    n = len(srcs)
    everyone = dests == (0, N_DEV)

    def body(*refs):
        src_refs, land_refs = refs[:n], refs[n:2 * n]
        send_sems, recv_sems = refs[2 * n], refs[2 * n + 1]
        for i in range(n):
            for k in range(1, N_DEV):
                send, recv, sends, receives = _exchange_copy(src_refs, land_refs, send_sems, recv_sems, whole, dests, i, k)
                if everyone:
                    send.wait_send()
                    recv.wait_recv()
                else:
                    pl.when(sends)(send.wait_send)
                    pl.when(receives)(recv.wait_recv)

    outs = pl.pallas_call(
        body, name=name, out_shape=tuple(pltpu.HBM(a.shape, a.dtype) for a in srcs + lands),
        in_specs=[HBM_SPEC] * (2 * n) + [SEM_SPEC, SEM_SPEC, pl.BlockSpec(memory_space=pl.ANY)],
        out_specs=[HBM_SPEC] * (2 * n), input_output_aliases={i: i for i in range(2 * n)},
        compiler_params=pltpu.CompilerParams(has_side_effects=EFFECT),
    )(*srcs, *lands, send_sems, recv_sems, after)
    return list(outs[n:])


def _tie(v, token):
    return v + token[0:1, 0:1].astype(v.dtype).reshape((1,) * v.ndim)


def _with_own(land, own, me):
    return lax.dynamic_update_slice_in_dim(land, own[None].astype(land.dtype), me, 0)


class _Overlap:
    def __init__(self, shards, me, after):
        self.me = me
        self.names = list(shards)
        self.handle, self.token = _exchange_start([shards[nm] for nm in self.names], True, "weights_start", after)
        self.sent = {}

    def weights(self, after):
        lands = _exchange_wait(self.handle, after, "weights_wait")
        own = self.handle[2]
        return {nm: _full_from_blocks(nm, _with_own(land, o, self.me)) for nm, land, o in zip(self.names, lands, own)}

    def send(self, tag, grads, after=None):
        names = list(grads)
        handle, token = _exchange_start([_blocks_from_full(nm, grads[nm]) for nm in names], False, f"grads_start_{tag}",
                                        after)
        self.sent[tag] = (names, handle)
        return token

    def send_rows(self, tag, rows, dests):
        lo, hi = dests
        blocks = rows.reshape(hi - lo, rows.shape[0] // (hi - lo), rows.shape[1])
        handle, token = _exchange_start([blocks], False, f"grads_start_{tag}", None, dests)
        self.sent[tag] = ([tag], handle)
        return token

    def received(self, tag, after):
        names, handle = self.sent[tag]
        lands = _exchange_wait(handle, after, f"grads_wait_{tag}")
        lo = handle[5][0]
        own = [lax.dynamic_index_in_dim(b, self.me - lo, 0, keepdims=False) for b in handle[2]]
        return {nm: _with_own(land, o, self.me) for nm, land, o in zip(names, lands, own)}


ADAM_ROWS = 32


def _adamw(w, g, m, v, name):
    deep = w.ndim == 3
    R, C = w.shape[0], w.shape[-1]
    cb = LANES if C % LANES == 0 else C
    n_parts = g.shape[0] if g.ndim == 3 else 0

    def body(w_ref, g_ref, m_ref, v_ref, d_ref, m2_ref, v2_ref, *g_out):
        at = (lambda ref, sl: ref.at[sl, 0, :]) if deep else (lambda ref, sl: ref.at[sl, :])

        def update(sl):
            if n_parts:
                gv = g_ref[0, sl, :].astype(f32)
                for k in range(1, n_parts):
                    gv = gv + g_ref[k, sl, :].astype(f32)
                at(g_out[0], sl)[...] = gv
            else:
                gv = g_ref[sl, :]
            m2 = ADAM_B1 * at(m_ref, sl)[...] + (1.0 - ADAM_B1) * gv
            v2 = ADAM_B2 * at(v_ref, sl)[...] + (1.0 - ADAM_B2) * jnp.square(gv)
            m_hat = m2 / (1.0 - ADAM_B1 ** ADAM_STEP)
            v_hat = v2 / (1.0 - ADAM_B2 ** ADAM_STEP)
            at(d_ref, sl)[...] = -ADAM_LR * (m_hat / (jnp.sqrt(v_hat) + ADAM_EPS) + ADAM_WD * at(w_ref, sl)[...])
            at(m2_ref, sl)[...] = m2
            at(v2_ref, sl)[...] = v2

        main = R // ADAM_ROWS
        if main:
            lax.fori_loop(0, main, lambda i, c: (update(pl.ds(pl.multiple_of(i * ADAM_ROWS, ADAM_ROWS), ADAM_ROWS)), c)[1], 0)
        if R % ADAM_ROWS:
            update(pl.ds(main * ADAM_ROWS, R % ADAM_ROWS))

    spec = pl.BlockSpec((R, 1, cb), lambda j: (0, 0, j)) if deep else pl.BlockSpec((R, cb), lambda j: (0, j))
    g_spec = pl.BlockSpec((n_parts, R, cb), lambda j: (0, 0, j)) if n_parts else pl.BlockSpec((R, cb), lambda j: (0, j))
    n_out = 4 if n_parts else 3
    return pl.pallas_call(
        body, name=name, grid=(C // cb,), in_specs=[spec, g_spec, spec, spec], out_specs=[spec] * n_out,
        out_shape=[_sds(w.shape)] * n_out, compiler_params=_params(("parallel",)),
    )(w, g, m, v)


BIG_PARAMS = ("hy_w_in", "hy_w_out", "cv_w_pw1", "cv_w_pw2", "ffn_w_gate", "ffn_w_up", "ffn_w_down")


def _shards_2d(w):
    t = lambda a: jnp.transpose(a)
    return dict(in_t=t(w["hy_w_in"][0]), out=w["hy_w_out"][0], pw1=w["cv_w_pw1"][0], pw2=w["cv_w_pw2"][0],
                gate_t0=t(w["ffn_w_gate"][0]), gate_t1=t(w["ffn_w_gate"][1]), up_t0=t(w["ffn_w_up"][0]),
                up_t1=t(w["ffn_w_up"][1]), down0=w["ffn_w_down"][0], down1=w["ffn_w_down"][1])


def _unshard_2d(s):
    t = lambda a: jnp.transpose(a)
    out = dict(hy_w_out=s["out"][None], cv_w_pw1=s["pw1"][None], cv_w_pw2=s["pw2"][None],
               ffn_w_gate=jnp.stack([t(s["gate_t0"]), t(s["gate_t1"])]),
               ffn_w_up=jnp.stack([t(s["up_t0"]), t(s["up_t1"])]), ffn_w_down=jnp.stack([s["down0"], s["down1"]]))
    if "in_t" in s:
        out["hy_w_in"] = t(s["in_t"])[None]
    return out


def _full_from_blocks(nm, g):
    if nm == "pw1":
        return jnp.transpose(g, (1, 0, 2)).reshape(g.shape[1], N_DEV * g.shape[2])
    return g.reshape(N_DEV * g.shape[1], g.shape[2])


def _blocks_from_full(nm, g):
    if nm == "pw1":
        return jnp.transpose(g.reshape(g.shape[0], N_DEV, g.shape[1] // N_DEV), (1, 0, 2))
    return g.reshape(N_DEV, g.shape[0] // N_DEV, g.shape[1])


class _VecPack:
    def __init__(self, shapes):
        self.shapes = [tuple(s) for s in shapes]
        self.sizes = [int(np.prod(s)) for s in self.shapes]
        total = sum(self.sizes)
        self.rows = -(-(-(-total // LANES)) // 8) * 8
        self.total = total

    def pack(self, arrays):
        flat = jnp.concatenate([a.astype(f32).reshape(-1) for a in arrays])
        flat = jnp.pad(flat, (0, self.rows * LANES - self.total))
        return flat.reshape(self.rows, LANES)

    def unpack(self, packed):
        flat = packed.reshape(-1)
        out, off = [], 0
        for shp, n in zip(self.shapes, self.sizes):
            out.append(flat[off:off + n].reshape(shp))
            off += n
        return out

    def unpack_stacked(self, stacked, only=None):
        flat = stacked.reshape(stacked.shape[0], -1)
        offs = np.concatenate([[0], np.cumsum(self.sizes)])
        get = lambda i: flat[:, offs[i]:offs[i + 1]].reshape((stacked.shape[0],) + self.shapes[i])
        return get(only) if only is not None else [get(i) for i in range(len(self.shapes))]


def _row(v):
    return v.reshape(1, -1)


def _pad_lanes(v):
    v = v.reshape(1, -1)
    return jnp.pad(v, ((0, 0), (0, LANES - v.shape[1])))


def _ffn_fwd(h, w_gate_t, w_up_t, w_down, tag):
    F = w_down.shape[0]
    a = _mm(h, w_gate_t, tb=True, out_dtype=bf16, name=f"ffn_gate_{tag}")
    u = _mm(h, w_up_t, tb=True, out_dtype=bf16, name=f"ffn_up_{tag}")
    (f,), _ = _rowwise(f"swiglu_{tag}", lambda rv, vv: ([_silu(rv[0].astype(f32)) * rv[1].astype(f32)], []), [a, u], [],
                       [(F, bf16)], [], sub=16, col_chunk=_tile(F, 512, LANES))
    out = _mm(f, w_down, name=f"ffn_down_{tag}")
    return out, (a, u, f)


def _ffn_bwd(h, w_gate_t, w_up_t, w_down, saved, dout, tag):
    a, u, f = saved
    F = w_down.shape[0]
    df = _mm(dout, w_down, tb=True, out_dtype=bf16, name=f"ffn_down_dx_{tag}")
    dw_down = _mm(f, dout, ta=True, out_dtype=bf16, name=f"ffn_down_dw_{tag}")

    def fn(rv, vv):
        _, vjp = jax.vjp(lambda a_, u_: _silu(a_) * u_, rv[0].astype(f32), rv[1].astype(f32))
        da, du = vjp(rv[2].astype(f32))
        return [da, du], []

    (da, du), _ = _rowwise(f"swiglu_bwd_{tag}", fn, [a, u, df], [], [(F, bf16), (F, bf16)], [], sub=16,
                           col_chunk=_tile(F, 512, LANES))
    dh = _mm(du, w_up_t, add=_mm(da, w_gate_t, name=f"ffn_gate_dx_{tag}"), name=f"ffn_up_dx_{tag}")
    dw_gate_t = _mm(da, h, ta=True, out_dtype=bf16, name=f"ffn_gate_dw_{tag}")
    dw_up_t = _mm(du, h, ta=True, out_dtype=bf16, name=f"ffn_up_dw_{tag}")
    return dh, dw_gate_t, dw_up_t, dw_down


def _local_step(x, target, mod, w_in_t, comm, small):
    S, D = x.shape
    di = small["hy_ssm_norm_g"].shape[-1]
    nh = small["hy_dt_bias"].shape[-1]
    cd = small["hy_conv_b"].shape[-1]
    m = [[_row(mod[i, j]) for j in range(6)] for i in range(2)]

    off_q = di + cd + nh
    w_qkv_t = w_in_t[off_q:]
    seg = dict(z=(w_in_t, 0, di), xbc=(w_in_t, di, cd), dt=(w_in_t, di + cd, LANES))
    for i, nm in enumerate(("q0", "q1", "q2", "k", "v")):
        seg[nm] = (w_qkv_t, i * D, D)

    g_mix = [_row(small["norm_mix_g"][i]) for i in range(2)]
    g_ffn = [_row(small["norm_ffn_g"][i]) for i in range(2)]
    conv_w, conv_b = small["hy_conv_w_full"], _row(small["hy_conv_b"][0])
    dt_bias, a_log, d_skip = (_pad_lanes(small[k][0]) for k in ("hy_dt_bias", "hy_a_log", "hy_d_skip"))
    g_ssm = _row(small["hy_ssm_norm_g"][0])
    onehot = jnp.asarray(_bucket_onehot())
    rel_t = small["rel_table"].T
    H = D // HEAD_DIM
    bias = [_exact_mm(rel_t[gi * H:(gi + 1) * H], onehot[gi], name=f"rel_bias_{gi}")
            .reshape(H, 2 * ATT_BLK, ATT_BLK) + _band_penalty() for gi in range(3)]

    h1 = _adaln_fwd(x, g_mix[0], m[0][1], m[0][0], "adaln_mix0")
    proj = {nm: _mm(h1, mat, tb=True, b_rows=(off, cnt), name=f"in_{nm}") for nm, (mat, off, cnt) in seg.items()}
    xbc_pre, xbc = _conv_fwd(proj["xbc"], conv_w, conv_b, silu=True, name="ssm_conv", tr=1024)
    y, hin = _ssd_fwd(xbc, proj["dt"], dt_bias, a_log, d_skip, di, "ssd_fwd")
    (yg,), _ = _rowwise("ssm_gate", lambda rv, vv: ([_gate_f(rv[0], rv[1], vv[0])], []),
                        [y, proj["z"]], [g_ssm], [(di, bf16)], [], sub=16)
    og = [_attn_fwd(proj[f"q{gi}"], proj["k"], proj["v"], bias[gi], d, f"attn_fwd_{gi}")
          for gi, d in enumerate(ATT_DILATIONS)]
    att, att_b, lse_tot = _attn_combine([a for a, _ in og], [_from_blocks(b, LANES) for _, b in og], "attn_combine")
    W = comm.weights(after=att_b)
    w_out_y, w_out_a = W["out"][:di], W["out"][di:]
    mix0 = _mm(att_b, w_out_a, add=_mm(yg, w_out_y, name="out_y"), name="out_a")
    x1, h2 = _resid_adaln_fwd(x, m[0][2], mix0, g_ffn[0], m[0][4], m[0][3], "resid_mix0_adaln_ffn0")
    f0, ffn0_saved = _ffn_fwd(h2, W["gate_t0"], W["up_t0"], W["down0"], "0")
    x2, h3 = _resid_adaln_fwd(x1, m[0][5], f0, g_mix[1], m[1][1], m[1][0], "resid_ffn0_adaln_mix1")
    pw1 = _mm(h3, W["pw1"], bias=_row(small["cv_b_pw1_full"]), name="cv_pw1")
    (u,), _ = _rowwise("cv_glu", lambda rv, vv: ([rv[0] * jax.nn.sigmoid(rv[1])], []),
                       [(pw1, 0, D), (pw1, 1, D)], [], [(D, f32)], [])
    (u2,) = _conv_fwd(u, small["cv_w_dw_full"], _row(small["cv_b_dw_full"]), silu=False, name="cv_dw")
    ln_g, ln_b = _row(small["cv_ln_g_full"]), _row(small["cv_ln_b_full"])
    (u3,), _ = _rowwise("cv_lnsilu", lambda rv, vv: ([_lnsilu_f(rv[0], vv[0], vv[1])], []),
                        [u2], [ln_g, ln_b], [(D, bf16)], [], sub=16)
    mix1 = _mm(u3, W["pw2"], bias=_row(small["cv_b_pw2_full"]), name="cv_pw2")
    x3, h4 = _resid_adaln_fwd(x2, m[1][2], mix1, g_ffn[1], m[1][4], m[1][3], "resid_mix1_adaln_ffn1")
    f1, ffn1_saved = _ffn_fwd(h4, W["gate_t1"], W["up_t1"], W["down1"], "1")

    g_fin = _row(small["final_norm_g"])
    dmod = [[None] * 6 for _ in range(2)]
    d_norm_mix, d_norm_ffn = [None, None], [None, None]
    big = {}

    def final_fn(rv, vv):
        xv, fv, tv = rv
        gate = vv[1]
        yv, vjp = jax.vjp(_rms, xv + gate * fv, vv[0])
        err = yv - tv
        dx, dg = vjp(err / D)
        part = 0.5 * jnp.sum(jnp.mean(err * err, -1, keepdims=True), 0, keepdims=True)
        return [dx, gate * dx], [dg, jnp.broadcast_to(part, (1, LANES)), jnp.sum(dx * fv, 0, keepdims=True)]

    (dx4, df1), (d_fin, loss, dmod[1][5]) = _rowwise("loss_head", final_fn, [x3, f1, target], [g_fin, m[1][5]],
                                                      [(D, f32), (D, bf16)], [D, LANES, D], sub=16)

    dh4, big["gate_t1"], big["up_t1"], big["down1"] = _ffn_bwd(h4, W["gate_t1"], W["up_t1"], W["down1"], ffn1_saved, df1, "1")
    dx3, dmix1, (d_norm_ffn[1], dmod[1][4], dmod[1][3], dmod[1][2], d_b_pw2) = _adaln_resid_bwd(
        x3, g_ffn[1], m[1][4], m[1][3], dh4, dx4, mix1, m[1][2], "adaln_ffn1_resid_mix1_bwd")
    du3 = _mm(dmix1, W["pw2"], tb=True, name="cv_pw2_dx")
    big["pw2"] = _mm(u3, dmix1, ta=True, out_dtype=bf16, name="cv_pw2_dw")

    def lnsilu_bwd(rv, vv):
        _, vjp = jax.vjp(_lnsilu_f, rv[0], vv[0], vv[1])
        du, dg, db = vjp(rv[1])
        return [du], [dg, db]

    (du2,), (d_ln_g, d_ln_b) = _rowwise("cv_lnsilu_bwd", lnsilu_bwd, [u2, du3], [ln_g, ln_b], [(D, f32)], [D, D])
    du, d_w_dw, d_b_dw = _conv_bwd(u, small["cv_w_dw_full"], du2, None, silu=False, name="cv_dw_bwd")

    def glu_bwd(rv, vv):
        a, gt, d = rv
        _, vjp = jax.vjp(lambda a_, g_: a_ * jax.nn.sigmoid(g_), a, gt)
        da, dg = vjp(d)
        return [da, dg], [jnp.sum(da, 0, keepdims=True), jnp.sum(dg, 0, keepdims=True)]

    (dpa, dpg), (d_b1a, d_b1g) = _rowwise("cv_glu_bwd", glu_bwd, [(pw1, 0, D), (pw1, 1, D), du], [],
                                           [(D, bf16), (D, bf16)], [D, D], sub=16)
    dpw1 = jnp.concatenate([dpa, dpg], axis=1)
    d_b_pw1 = jnp.concatenate([d_b1a, d_b1g], axis=1)
    dh3 = _mm(dpw1, W["pw1"], tb=True, name="cv_pw1_dx")
    big["pw1"] = _mm(h3, dpw1, ta=True, out_dtype=bf16, name="cv_pw1_dw")
    token = comm.send("layer1", {nm: big[nm] for nm in ("gate_t1", "up_t1", "down1", "pw2", "pw1")})
    dx2, df0, (d_norm_mix[1], dmod[1][1], dmod[1][0], dmod[0][5], _) = _adaln_resid_bwd(
        x2, g_mix[1], m[1][1], _tie(m[1][0], token), dh3, dx3, f0, m[0][5], "adaln_mix1_resid_ffn0_bwd")

    dh2, big["gate_t0"], big["up_t0"], big["down0"] = _ffn_bwd(h2, W["gate_t0"], W["up_t0"], W["down0"], ffn0_saved, df0, "0")
    dx1, dmix0, (d_norm_ffn[0], dmod[0][4], dmod[0][3], dmod[0][2], _) = _adaln_resid_bwd(
        x1, g_ffn[0], m[0][4], m[0][3], dh2, dx2, mix0, m[0][2], "adaln_ffn0_resid_mix0_bwd")
    dyg = _mm(dmix0, w_out_y, tb=True, name="out_y_dx")
    datt = _mm(dmix0, w_out_a, tb=True, name="out_a_dx")
    big["out"] = jnp.concatenate([_mm(yg, dmix0, ta=True, out_dtype=bf16, name="out_y_dw"),
                                  _mm(att_b, dmix0, ta=True, out_dtype=bf16, name="out_a_dw")], axis=0)
    token = comm.send("layer0", {nm: big[nm] for nm in ("gate_t0", "up_t0", "down0", "out")})
    g_ssm = _tie(g_ssm, token)

    def gate_bwd(rv, vv):
        _, vjp = jax.vjp(_gate_f, rv[0], rv[1], vv[0])
        dy_, dz_, dg_ = vjp(rv[2])
        return [dy_, dz_], [dg_]

    (dy, dz), (d_g_ssm,) = _rowwise("ssm_gate_bwd", gate_bwd, [y, proj["z"], dyg], [g_ssm], [(di, f32), (di, bf16)], [di],
                                    sub=16)
    dxbc, ddtraw, d_a_log, d_dskip, d_dt_bias = _ssd_bwd(xbc, proj["dt"], dt_bias, a_log, d_skip, hin, y, dy, di, "ssd_bwd")
    dxbc_pre, d_conv_w, d_conv_b = _conv_bwd(proj["xbc"], conv_w, dxbc, xbc_pre, silu=True, name="ssm_conv_bwd",
                                             dx_dtype=bf16, tr=1024)
    dh1 = None
    early = []
    for nm, dseg in (("z", dz), ("xbc", dxbc_pre), ("dt", ddtraw)):
        mat, off, cnt = seg[nm]
        dh1 = _mm(dseg, mat, b_rows=(off, cnt), add=dh1, name=f"in_{nm}_dx")
        dwp = _mm(dseg, h1, ta=True, out_dtype=bf16, name=f"in_{nm}_dw")
        early.append(dwp[:nh] if nm == "dt" else dwp)
    early = jnp.concatenate(early, axis=0)
    shard_rows = w_in_t.shape[0] // N_DEV
    n_early = off_q // shard_rows
    token = comm.send_rows("in_early", early[:n_early * shard_rows], (0, n_early))
    bias = [_tie(b, token) for b in bias]

    dq, dks, dvs, dbs = [], [], [], []
    lse_heads = lse_tot[:, :H]
    dsum_heads = _head_sums(att, datt, "attn_dsum")[:, :H]
    for gi, d in enumerate(ATT_DILATIONS):
        a, b, c_, e = _attn_bwd(proj[f"q{gi}"], proj["k"], proj["v"], bias[gi], datt,
                                _by_block(lse_heads, d), _by_block(dsum_heads, d), d, f"attn_bwd_{gi}")
        dq.append(a)
        dks.append(b)
        dvs.append(c_)
        dbs.append(e)
    dk = _add3(*dks, "attn_dk")
    dv = _add3(*dvs, "attn_dv")
    d_rel = jnp.concatenate(
        [_exact_mm(dbs[gi].reshape(H, -1), onehot[gi], tb=True, name=f"rel_grad_{gi}") for gi in range(3)], axis=0).T

    dsegs = (("q0", dq[0]), ("q1", dq[1]), ("q2", dq[2]), ("k", dk), ("v", dv))
    late = jnp.concatenate([early[n_early * shard_rows:]] +
                           [_mm(dseg, h1, ta=True, out_dtype=bf16, name=f"in_{nm}_dw") for nm, dseg in dsegs], axis=0)
    token = comm.send_rows("in_late", late, (n_early, N_DEV))
    w_qkv_after = _tie(w_qkv_t, token)
    for nm, dseg in dsegs:
        _, off, cnt = seg[nm]
        dh1 = _mm(dseg, w_qkv_after, b_rows=(off, cnt), add=dh1, name=f"in_{nm}_dx")
    dx0, (d_norm_mix[0], dmod[0][1], dmod[0][0]) = _adaln_bwd(x, g_mix[0], m[0][1], m[0][0], dh1, dx1, "adaln_mix0_bwd")

    smallg = dict(
        loss=loss, dmod=jnp.stack([jnp.concatenate(dmod[i], axis=1)[0] for i in range(2)]),
        norm_mix_g=jnp.concatenate(d_norm_mix, axis=0), norm_ffn_g=jnp.concatenate(d_norm_ffn, axis=0),
        hy_conv_w=d_conv_w, hy_conv_b=d_conv_b, hy_dt_bias=d_dt_bias[:, :nh], hy_a_log=d_a_log[:, :nh],
        hy_d_skip=d_dskip[:, :nh], hy_ssm_norm_g=d_g_ssm, rel_table=d_rel,
        cv_b_pw1=d_b_pw1, cv_w_dw=d_w_dw, cv_b_dw=d_b_dw, cv_ln_g=d_ln_g, cv_ln_b=d_ln_b, cv_b_pw2=d_b_pw2,
        final_norm_g=d_fin)
    return dx0, n_early, smallg


SMALL_GRAD_ORDER = ("loss", "dmod", "norm_mix_g", "norm_ffn_g", "hy_conv_w", "hy_conv_b", "hy_dt_bias", "hy_a_log",
                    "hy_d_skip", "hy_ssm_norm_g", "rel_table", "cv_b_pw1", "cv_w_dw", "cv_b_dw", "cv_ln_g", "cv_ln_b",
                    "cv_b_pw2", "final_norm_g")


def kernel(x, c, ada_w, ada_b, norm_mix_g, norm_ffn_g, hy_w_in, hy_conv_w, hy_conv_b, hy_dt_bias, hy_a_log, hy_d_skip, hy_ssm_norm_g, hy_w_out, rel_table, cv_w_pw1, cv_b_pw1, cv_w_dw, cv_b_dw, cv_ln_g, cv_ln_b, cv_w_pw2, cv_b_pw2, ffn_w_gate, ffn_w_up, ffn_w_down, final_norm_g, loss_target, m_ada_w, m_ada_b, m_norm_mix_g, m_norm_ffn_g, m_hy_w_in, m_hy_conv_w, m_hy_conv_b, m_hy_dt_bias, m_hy_a_log, m_hy_d_skip, m_hy_ssm_norm_g, m_hy_w_out, m_rel_table, m_cv_w_pw1, m_cv_b_pw1, m_cv_w_dw, m_cv_b_dw, m_cv_ln_g, m_cv_ln_b, m_cv_w_pw2, m_cv_b_pw2, m_ffn_w_gate, m_ffn_w_up, m_ffn_w_down, m_final_norm_g, v_ada_w, v_ada_b, v_norm_mix_g, v_norm_ffn_g, v_hy_w_in, v_hy_conv_w, v_hy_conv_b, v_hy_dt_bias, v_hy_a_log, v_hy_d_skip, v_hy_ssm_norm_g, v_hy_w_out, v_rel_table, v_cv_w_pw1, v_cv_b_pw1, v_cv_w_dw, v_cv_b_dw, v_cv_ln_g, v_cv_ln_b, v_cv_w_pw2, v_cv_b_pw2, v_ffn_w_gate, v_ffn_w_up, v_ffn_w_down, v_final_norm_g):
    names = ("ada_w", "ada_b", "norm_mix_g", "norm_ffn_g", "hy_w_in", "hy_conv_w", "hy_conv_b", "hy_dt_bias", "hy_a_log",
             "hy_d_skip", "hy_ssm_norm_g", "hy_w_out", "rel_table", "cv_w_pw1", "cv_b_pw1", "cv_w_dw", "cv_b_dw", "cv_ln_g",
             "cv_ln_b", "cv_w_pw2", "cv_b_pw2", "ffn_w_gate", "ffn_w_up", "ffn_w_down", "final_norm_g")
    w = dict(zip(names, (ada_w, ada_b, norm_mix_g, norm_ffn_g, hy_w_in, hy_conv_w, hy_conv_b, hy_dt_bias, hy_a_log, hy_d_skip,
                         hy_ssm_norm_g, hy_w_out, rel_table, cv_w_pw1, cv_b_pw1, cv_w_dw, cv_b_dw, cv_ln_g, cv_ln_b, cv_w_pw2,
                         cv_b_pw2, ffn_w_gate, ffn_w_up, ffn_w_down, final_norm_g)))
    mom = dict(zip(names, (m_ada_w, m_ada_b, m_norm_mix_g, m_norm_ffn_g, m_hy_w_in, m_hy_conv_w, m_hy_conv_b, m_hy_dt_bias,
                           m_hy_a_log, m_hy_d_skip, m_hy_ssm_norm_g, m_hy_w_out, m_rel_table, m_cv_w_pw1, m_cv_b_pw1, m_cv_w_dw,
                           m_cv_b_dw, m_cv_ln_g, m_cv_ln_b, m_cv_w_pw2, m_cv_b_pw2, m_ffn_w_gate, m_ffn_w_up, m_ffn_w_down,
                           m_final_norm_g)))
    vel = dict(zip(names, (v_ada_w, v_ada_b, v_norm_mix_g, v_norm_ffn_g, v_hy_w_in, v_hy_conv_w, v_hy_conv_b, v_hy_dt_bias,
                           v_hy_a_log, v_hy_d_skip, v_hy_ssm_norm_g, v_hy_w_out, v_rel_table, v_cv_w_pw1, v_cv_b_pw1, v_cv_w_dw,
                           v_cv_b_dw, v_cv_ln_g, v_cv_ln_b, v_cv_w_pw2, v_cv_b_pw2, v_ffn_w_gate, v_ffn_w_up, v_ffn_w_down,
                           v_final_norm_g)))
    S, D = x.shape[1], x.shape[2]
    ax, ay, ac = lax.axis_index("x"), lax.axis_index("y"), lax.axis_index("c")
    me = 4 * ax + 2 * ay + ac
    nmod = ada_w.shape[2]

    w2 = _shards_2d(w)
    big_names = list(w2)
    sharded_small = ("hy_conv_w", "cv_b_pw1", "cv_w_dw", "cv_b_dw", "cv_ln_g", "cv_ln_b", "cv_b_pw2")
    vp = _VecPack([c.shape] + [w[nm].shape for nm in sharded_small])
    g_in, sg = _all_gather([w2["in_t"].astype(bf16), vp.pack([c] + [w[nm] for nm in sharded_small])], "gather_w_in")
    w_in_t = _full_from_blocks("in_t", g_in)
    parts = vp.unpack_stacked(sg)
    c_all = parts[0][:, 0]
    small = {k: w[k] for k in ("norm_mix_g", "norm_ffn_g", "hy_conv_b", "hy_dt_bias", "hy_a_log", "hy_d_skip",
                               "hy_ssm_norm_g", "rel_table", "final_norm_g")}
    for p, nm in zip(parts[1:], sharded_small):
        p = p[:, 0]
        p = jnp.moveaxis(p, 0, -2)
        small[nm + "_full"] = p.reshape(p.shape[:-2] + (N_DEV * p.shape[-1],))

    (cs_all,), _ = _rowwise("ada_silu", lambda rv, vv: ([_silu(rv[0])], []), [c_all], [], [(D, f32)], [])
    b_mine = lax.dynamic_slice_in_dim(ada_b, me * nmod, nmod, axis=1)
    mod_part = jnp.stack([_mm(cs_all, ada_w[i], bias=b_mine[i:i + 1], name=f"ada_mod_{i}") for i in range(2)])
    (mod_all,) = _all_gather([mod_part.reshape(2 * N_DEV, nmod)], "gather_mod")
    mod_all = mod_all.reshape(N_DEV, 2, N_DEV, nmod)
    mod_mine = lax.dynamic_index_in_dim(mod_all, me, axis=2, keepdims=False)
    mod = jnp.transpose(mod_mine, (1, 0, 2)).reshape(2, 6, D)
    comm = _Overlap({nm: w2[nm].astype(bf16) for nm in big_names if nm != "in_t"}, me, after=[mod, w_in_t])
    mod = _tie(mod, comm.token)

    dx0, n_early, sgrad = _local_step(x[0], loss_target[0], mod, w_in_t, comm, small)

    gp = _VecPack([sgrad[k].shape for k in SMALL_GRAD_ORDER])
    (g_all,) = _all_gather([gp.pack([sgrad[k] for k in SMALL_GRAD_ORDER])], "gather_small_grads")
    tot = dict(zip(SMALL_GRAD_ORDER, gp.unpack(_sum_slots(g_all, "sum_small_grads"))))
    dmod_all = gp.unpack_stacked(g_all, only=SMALL_GRAD_ORDER.index("dmod"))
    loss = tot["loss"][0, 0]

    grads = {}
    dmod_mine = lax.dynamic_slice_in_dim(dmod_all, me * nmod, nmod, axis=2)
    grads["ada_w"] = jnp.stack([_mm(cs_all, dmod_mine[:, i], ta=True, name=f"ada_w_grad_{i}") for i in range(2)])
    grads["ada_b"] = tot["dmod"]
    grads["norm_mix_g"], grads["norm_ffn_g"] = tot["norm_mix_g"], tot["norm_ffn_g"]
    grads["hy_conv_b"] = tot["hy_conv_b"]
    grads["hy_dt_bias"] = tot["hy_dt_bias"]
    grads["hy_a_log"] = tot["hy_a_log"]
    grads["hy_d_skip"] = tot["hy_d_skip"]
    grads["hy_ssm_norm_g"] = tot["hy_ssm_norm_g"]
    grads["rel_table"] = tot["rel_table"]
    grads["final_norm_g"] = tot["final_norm_g"][0]
    for nm in sharded_small:
        n = w[nm].shape[-1]
        grads[nm] = lax.dynamic_slice_in_dim(tot[nm], me * n, n, axis=1).reshape(w[nm].shape)

    delta, new_m, new_v = {}, {}, {}
    shp = ada_w.shape
    two = lambda t: t.reshape(-1, shp[-1])
    d_, m_, v_ = _adamw(two(ada_w), two(grads["ada_w"]), two(m_ada_w), two(v_ada_w), "adamw_ada_w")
    delta["ada_w"], new_m["ada_w"], new_v["ada_w"] = d_.reshape(shp), m_.reshape(shp), v_.reshape(shp)
    rest = [nm for nm in names if nm not in BIG_PARAMS and nm != "ada_w"]
    sp = _VecPack([w[nm].shape for nm in rest])
    packs = [sp.pack([t[nm] for nm in rest]) for t in (w, grads, mom, vel)]
    ds_, ms_, vs_ = _adamw(*packs, "adamw_small")
    for nm, a, b, e in zip(rest, sp.unpack(ds_), sp.unpack(ms_), sp.unpack(vs_)):
        delta[nm], new_m[nm], new_v[nm] = a, b, e

    m2, v2 = _shards_2d(mom), _shards_2d(vel)
    g2, d2, nm2, nv2 = {}, {}, {}, {}
    after = d_
    slots = {}
    for tag in ("layer1", "layer0", "in_early", "in_late"):
        slots.update(comm.received(tag, after))
        if tag.startswith("in_"):
            continue
        for nm in comm.sent[tag][0]:
            d2[nm], nm2[nm], nv2[nm], g2[nm] = _adamw(w2[nm], slots[nm], m2[nm], v2[nm], f"adamw_{nm}")
            after = g2[nm]
    stored = lambda t: jnp.transpose(t, (2, 0, 1))
    in_slots = jnp.where(me < n_early, slots["in_early"], slots["in_late"])
    d_in, m_in, v_in, g_in = _adamw(stored(hy_w_in), in_slots, stored(m_hy_w_in), stored(v_hy_w_in), "adamw_in_t")
    for dst, part, t in ((grads, g2, g_in), (delta, d2, d_in), (new_m, nm2, m_in), (new_v, nv2, v_in)):
        dst.update(_unshard_2d(part))
        dst["hy_w_in"] = jnp.transpose(t, (1, 2, 0))

    return (loss, dx0[None], *[grads[n] for n in names], *[delta[n] for n in names],
            *[new_m[n] for n in names], *[new_v[n] for n in names])
```

```python
import functools
import math

import numpy as np
import jax
import jax.numpy as jnp
from jax import lax
from jax.experimental import pallas as pl
from jax.experimental.pallas import tpu as pltpu

f32 = jnp.float32
bf16 = jnp.bfloat16
EPS = 1e-6
N_DEV = 8
LANES = 128
SSM_STATE = 128
SSM_CHUNK = 128
SSM_GROUPS = 4
HEAD_DIM = 64
ATT_BLK = 128
ATT_DILATIONS = (1, 4, 16)
REL_BUCKETS = 32
REL_MAX_DIST = 2048
ADAM_LR, ADAM_B1, ADAM_B2, ADAM_EPS, ADAM_WD, ADAM_STEP = 0.001, 0.9, 0.999, 1e-08, 0.01, 10
PACK_ROW_TILE = 256
MESH = pl.DeviceIdType.MESH
VMEM_LIMIT = 48 * 1024 * 1024


def _sds(shape, dtype=f32):
    return jax.ShapeDtypeStruct(tuple(shape), dtype)


def _tile(n, cap, mult):
    best = None
    t = mult
    while t <= min(n, cap):
        if n % t == 0:
            best = t
        t += mult
    return best if best is not None else n


def _params(sem):
    return pltpu.CompilerParams(dimension_semantics=sem, vmem_limit_bytes=VMEM_LIMIT)


def _mm(a, b, *, name, ta=False, tb=False, b_rows=None, bias=None, add=None, out_dtype=f32,
        tm_cap=512, tn_cap=1536, tk_cap=8192):
    if ta:
        K, M = a.shape
    else:
        M, K = a.shape
    off, cnt = b_rows if b_rows is not None else (0, b.shape[0])
    if tb:
        N, K2 = cnt, b.shape[1]
    else:
        K2, N = cnt, b.shape[1]
    assert K == K2, (a.shape, b.shape, ta, tb, b_rows)
    if ta and a.dtype == f32:
        tm_cap = min(tm_cap, 256)
    tm = _tile(M, tm_cap, LANES)
    tn = _tile(math.gcd(off, N) if tb else N, tn_cap, LANES)
    tk = _tile(K if tb else math.gcd(off, K), tk_cap, LANES)
    assert N % tn == 0 and K % tk == 0 and off % (tn if tb else tk) == 0, (name, off, N, K, tn, tk)
    nk = K // tk
    jo, ko = (off // tn, 0) if tb else (0, off // tk)
    has_bias, has_add = bias is not None, add is not None
    dn = (((0 if ta else 1,), (1 if tb else 0,)), ((), ()))

    def body(*refs):
        a_ref, b_ref = refs[0], refs[1]
        pos = 2
        bias_ref = add_ref = None
        if has_bias:
            bias_ref = refs[pos]
            pos += 1
        if has_add:
            add_ref = refs[pos]
            pos += 1
        o_ref = refs[pos]
        k = pl.program_id(2)
        part = lax.dot_general(a_ref[...].astype(bf16), b_ref[...].astype(bf16), dn, preferred_element_type=f32)

        def finish(r):
            if has_bias:
                r = r + bias_ref[...]
            if has_add:
                r = r + add_ref[...]
            o_ref[...] = r.astype(o_ref.dtype)

        if nk == 1:
            finish(part)
        else:
            acc_ref = refs[pos + 1]

            @pl.when(k == 0)
            def _():
                acc_ref[...] = part

            @pl.when((k > 0) & (k < nk - 1))
            def _():
                acc_ref[...] += part

            @pl.when(k == nk - 1)
            def _():
                finish(acc_ref[...] + part)

    in_specs = [
        pl.BlockSpec((tk, tm), lambda i, j, k: (k, i)) if ta else pl.BlockSpec((tm, tk), lambda i, j, k: (i, k)),
        pl.BlockSpec((tn, tk), lambda i, j, k: (j + jo, k)) if tb else pl.BlockSpec((tk, tn), lambda i, j, k: (k + ko, j)),
    ]
    args = [a, b]
    if has_bias:
        in_specs.append(pl.BlockSpec((1, tn), lambda i, j, k: (0, j)))
        args.append(bias)
    if has_add:
        in_specs.append(pl.BlockSpec((tm, tn), lambda i, j, k: (i, j)))
        args.append(add)
    return pl.pallas_call(
        body, name=name, grid=(M // tm, N // tn, nk), in_specs=in_specs,
        out_specs=pl.BlockSpec((tm, tn), lambda i, j, k: (i, j)), out_shape=_sds((M, N), out_dtype),
        scratch_shapes=[pltpu.VMEM((tm, tn), f32)] if nk > 1 else [],
        compiler_params=_params(("parallel", "parallel", "arbitrary")),
    )(*args)


def _rowwise(name, fn, rows, vecs, out_rows, out_accs, *, tr_cap=256, sub=8, col_chunk=None):
    rows = [r if isinstance(r, tuple) else (r, 0, r.shape[1]) for r in rows]
    R = rows[0][0].shape[0]
    tr = _tile(R, tr_cap, 8)
    sub = sub if tr % sub == 0 else tr
    n_r, n_v, n_or, n_oa = len(rows), len(vecs), len(out_rows), len(out_accs)

    def body(*refs):
        row_refs = refs[:n_r]
        vec_refs = refs[n_r:n_r + n_v]
        orow_refs = refs[n_r + n_v:n_r + n_v + n_or]
        oacc_refs = refs[n_r + n_v + n_or:]
        vv = [r[...] for r in vec_refs]

        n_sub = tr // sub
        together = 4 if n_sub % 4 == 0 else 1

        def step(s, accs):
            for t in range(together):
                sl = pl.ds(pl.multiple_of((s * together + t) * sub, sub), sub)
                if col_chunk is None:
                    ro, ao = fn([r[sl, :] for r in row_refs], vv)
                    for o_ref, o in zip(orow_refs, ro):
                        o_ref[sl, :] = o.astype(o_ref.dtype)
                    accs = tuple(x + y for x, y in zip(accs, ao))
                else:
                    for c0 in range(0, rows[0][2], col_chunk):
                        cs_ = pl.ds(c0, col_chunk)
                        ro, _ = fn([r[sl, cs_] for r in row_refs], vv)
                        for o_ref, o in zip(orow_refs, ro):
                            o_ref[sl, cs_] = o.astype(o_ref.dtype)
            return accs

        accs = lax.fori_loop(0, n_sub // together, step, tuple(jnp.zeros((1, w), f32) for w in out_accs))
        if n_oa:
            @pl.when(pl.program_id(0) == 0)
            def _():
                for ref in oacc_refs:
                    ref[...] = jnp.zeros_like(ref)

            for ref, x in zip(oacc_refs, accs):
                ref[...] += x

    in_specs = [pl.BlockSpec((tr, w), functools.partial(lambda i, cb: (i, cb), cb=cb)) for (_, cb, w) in rows]
    in_specs += [pl.BlockSpec((1, v.shape[1]), lambda i: (0, 0)) for v in vecs]
    out_specs = [pl.BlockSpec((tr, w), lambda i: (i, 0)) for (w, _) in out_rows]
    out_specs += [pl.BlockSpec((1, w), lambda i: (0, 0)) for w in out_accs]
    out_shape = [_sds((R, w), dt) for (w, dt) in out_rows] + [_sds((1, w)) for w in out_accs]
    res = pl.pallas_call(
        body, name=name, grid=(R // tr,), in_specs=in_specs, out_specs=out_specs, out_shape=out_shape,
        compiler_params=_params(("arbitrary",)),
    )(*[r[0] for r in rows], *vecs)
    return res[:n_or], res[n_or:]


def _silu(x):
    return x * jax.nn.sigmoid(x)


def _rms(x, g):
    return x * lax.rsqrt(jnp.mean(x * x, -1, keepdims=True) + EPS) * g


def _adaln_f(x, g, sc, sh):
    return _rms(x, g) * (1.0 + sc) + sh


def _gate_f(y, z, g):
    return _rms(y * _silu(z), g)


def _lnsilu_f(u, g, b):
    mu = jnp.mean(u, -1, keepdims=True)
    var = jnp.mean(jnp.square(u - mu), -1, keepdims=True)
    return _silu((u - mu) * lax.rsqrt(var + EPS) * g + b)


def _adaln_fwd(x, g, sc, sh, name):
    (h,), _ = _rowwise(name, lambda rv, vv: ([_adaln_f(rv[0], *vv)], []), [x], [g, sc, sh], [(x.shape[1], bf16)], [],
                       sub=16)
    return h


def _adaln_bwd(x, g, sc, sh, dh, dres, name):
    def fn(rv, vv):
        xv, dhv, drv = rv
        _, vjp = jax.vjp(_adaln_f, xv, *vv)
        dx, dg, dsc, dsh = vjp(dhv)
        return [dx + drv], [dg, dsc, dsh]
    w = x.shape[1]
    (dx,), accs = _rowwise(name, fn, [x, dh, dres], [g, sc, sh], [(w, f32)], [w, w, w])
    return dx, accs


def _resid_adaln_fwd(x, gate, mix, g, sc, sh, name):
    def fn(rv, vv):
        xn = rv[0] + vv[0] * rv[1]
        return [xn, _adaln_f(xn, vv[1], vv[2], vv[3])], []
    w = x.shape[1]
    (xn, h), _ = _rowwise(name, fn, [x, mix], [gate, g, sc, sh], [(w, f32), (w, bf16)], [], sub=16)
    return xn, h


def _adaln_resid_bwd(x, g, sc, sh, dh, dres, mix, gate, name):
    def fn(rv, vv):
        xv, dhv, drv, mv = rv
        _, vjp = jax.vjp(_adaln_f, xv, vv[0], vv[1], vv[2])
        dx, dg, dsc, dsh = vjp(dhv)
        dx = dx + drv
        dm = vv[3] * dx
        return [dx, dm], [dg, dsc, dsh, jnp.sum(dx * mv, 0, keepdims=True), jnp.sum(dm, 0, keepdims=True)]
    w = x.shape[1]
    (dx, dmix), accs = _rowwise(name, fn, [x, dh, dres, mix], [g, sc, sh, gate], [(w, f32), (w, bf16)], [w] * 5, sub=16)
    return dx, dmix, accs


def _add3(a, b, c, name):
    (y,), _ = _rowwise(name, lambda rv, vv: ([rv[0] + rv[1] + rv[2]], []), [a, b, c], [], [(a.shape[1], bf16)], [],
                       sub=16)
    return y


CONV_HALO = 32
CONV_ROWS = 64


def _conv_fwd(x, w, b, *, silu, name, tr=512):
    S, C = x.shape
    K = w.shape[0]
    H = CONV_HALO
    assert K - 1 <= H and S % tr == 0 and tr % H == 0 and C % LANES == 0
    nh = tr // H

    def body(xp_ref, xc_ref, w_ref, b_ref, *rest):
        outs, scr = rest[:-1], rest[-1]
        i = pl.program_id(1)
        scr[pl.ds(0, H), :] = jnp.where(i > 0, xp_ref[...], 0.0)
        scr[pl.ds(H, tr), :] = xc_ref[...]
        taps = [w_ref[pl.ds(k, 1), :] for k in range(K)]
        for c0 in range(0, tr, CONV_ROWS):
            acc = jnp.zeros((CONV_ROWS, LANES), f32) + b_ref[...]
            for k in range(K):
                acc = acc + scr[pl.ds(c0 + H - (K - 1) + k, CONV_ROWS), :] * taps[k]
            outs[0][pl.ds(c0, CONV_ROWS), :] = acc.astype(outs[0].dtype)
            if silu:
                outs[1][pl.ds(c0, CONV_ROWS), :] = _silu(acc)

    n_out = 2 if silu else 1
    return pl.pallas_call(
        body, name=name, grid=(C // LANES, S // tr),
        in_specs=[pl.BlockSpec((H, LANES), lambda j, i: (jnp.maximum(i * nh - 1, 0), j)),
                  pl.BlockSpec((tr, LANES), lambda j, i: (i, j)),
                  pl.BlockSpec((K, LANES), lambda j, i: (0, j)),
                  pl.BlockSpec((1, LANES), lambda j, i: (0, j))],
        out_specs=[pl.BlockSpec((tr, LANES), lambda j, i: (i, j))] * n_out,
        out_shape=[_sds((S, C), bf16), _sds((S, C))] if silu else [_sds((S, C))],
        scratch_shapes=[pltpu.VMEM((tr + H, LANES), f32)],
        compiler_params=_params(("parallel", "arbitrary")),
    )(x, x, w, b)


def _conv_bwd(x, w, dact, pre, *, silu, name, dx_dtype=f32, tr=512):
    S, C = x.shape
    K = w.shape[0]
    H = CONV_HALO
    nh = tr // H
    n_i = S // tr
    kp = -(-K // 8) * 8

    def dsilu(p):
        s = jax.nn.sigmoid(p)
        return s * (1.0 + p * (1.0 - s))

    def body(*refs):
        if silu:
            xp_ref, xc_ref, w_ref, dc_ref, dn_ref, pc_ref, pn_ref, dx_ref, dw_ref, db_ref, xs, ds = refs
        else:
            xp_ref, xc_ref, w_ref, dc_ref, dn_ref, dx_ref, dw_ref, db_ref, xs, ds = refs
        i = pl.program_id(1)
        xs[pl.ds(0, H), :] = jnp.where(i > 0, xp_ref[...], 0.0)
        xs[pl.ds(H, tr), :] = xc_ref[...]
        dcur = dc_ref[...]
        dnext = dn_ref[...]
        if silu:
            dcur = dcur * dsilu(pc_ref[...].astype(f32))
            dnext = dnext * dsilu(pn_ref[...].astype(f32))
        ds[pl.ds(0, tr), :] = dcur
        ds[pl.ds(tr, H), :] = jnp.where(i < n_i - 1, dnext, 0.0)
        taps = [w_ref[pl.ds(k, 1), :] for k in range(K)]
        fold = lambda t: jnp.sum(t.reshape(CONV_ROWS // 8, 8, LANES), axis=0)
        dw_parts = [jnp.zeros((8, LANES), f32) for _ in range(K)]
        db_part = jnp.zeros((8, LANES), f32)
        for c0 in range(0, tr, CONV_ROWS):
            acc = jnp.zeros((CONV_ROWS, LANES), f32)
            d_c = ds[pl.ds(c0, CONV_ROWS), :]
            for k in range(K):
                acc = acc + ds[pl.ds(c0 + K - 1 - k, CONV_ROWS), :] * taps[k]
                dw_parts[k] = dw_parts[k] + fold(d_c * xs[pl.ds(c0 + H - (K - 1) + k, CONV_ROWS), :])
            db_part = db_part + fold(d_c)
            dx_ref[pl.ds(c0, CONV_ROWS), :] = acc.astype(dx_ref.dtype)

        @pl.when(i == 0)
        def _():
            dw_ref[...] = jnp.zeros_like(dw_ref)
            db_ref[...] = jnp.zeros_like(db_ref)

        for k in range(K):
            dw_ref[pl.ds(k, 1), :] += jnp.sum(dw_parts[k], 0, keepdims=True)
        db_ref[...] += jnp.sum(db_part, 0, keepdims=True)

    prev = pl.BlockSpec((H, LANES), lambda j, i: (jnp.maximum(i * nh - 1, 0), j))
    cur = pl.BlockSpec((tr, LANES), lambda j, i: (i, j))
    nxt = pl.BlockSpec((H, LANES), lambda j, i: (jnp.minimum((i + 1) * nh, n_i * nh - 1), j))
    in_specs = [prev, cur, pl.BlockSpec((K, LANES), lambda j, i: (0, j)), cur, nxt]
    args = [x, x, w, dact, dact]
    if silu:
        in_specs += [cur, nxt]
        args += [pre, pre]
    dx, dw, db = pl.pallas_call(
        body, name=name, grid=(C // LANES, n_i), in_specs=in_specs,
        out_specs=[cur, pl.BlockSpec((kp, LANES), lambda j, i: (0, j)), pl.BlockSpec((1, LANES), lambda j, i: (0, j))],
        out_shape=[_sds((S, C), dx_dtype), _sds((kp, C)), _sds((1, C))],
        scratch_shapes=[pltpu.VMEM((tr + H, LANES), f32), pltpu.VMEM((tr + H, LANES), f32)],
        compiler_params=_params(("parallel", "arbitrary")),
    )(*args)
    return dx, dw[:K], db


def _dot(a, b):
    return jnp.dot(a.astype(bf16), b.astype(bf16), preferred_element_type=f32)


def _dot_nt(a, b):
    return lax.dot_general(a.astype(bf16), b.astype(bf16), (((1,), (1,)), ((), ())), preferred_element_type=f32)


def _dot_tn(a, b):
    return lax.dot_general(a.astype(bf16), b.astype(bf16), (((0,), (0,)), ((), ())), preferred_element_type=f32)


def _softplus(x):
    return jnp.maximum(x, 0.0) + jnp.log(1.0 + jnp.exp(-jnp.abs(x)))


def _tri(q):
    i = lax.broadcasted_iota(jnp.int32, (q, q), 0)
    j = lax.broadcasted_iota(jnp.int32, (q, q), 1)
    return i >= j


def _ssd_prep(dtraw, dt_bias, a_log):
    q = dtraw.shape[0]
    dt = _softplus(dtraw + dt_bias)
    A = -jnp.exp(a_log)
    tri = _tri(q)
    cs = jnp.dot(tri.astype(f32), dt * A, preferred_element_type=f32, precision=lax.Precision.HIGHEST)
    return dt, A, cs, cs.T, tri


def _expand(cols, h0, n, width):
    q = cols.shape[0]
    return jnp.concatenate([jnp.broadcast_to(cols[:, h0 + r:h0 + r + 1], (q, width)) for r in range(n)], axis=1)


def _ssd_fwd(xbc, dtraw, dt_bias, a_log, d_skip, di, name):
    S, CD = xbc.shape
    Q, N, G = SSM_CHUNK, SSM_STATE, SSM_GROUPS
    nc = S // Q
    nh = di // HEAD_DIM
    R = nh // G
    gw = R * HEAD_DIM
    col_of_head = jnp.asarray((np.arange(LANES)[:, None] == np.arange(di)[None, :] // HEAD_DIM).astype(np.float32))
    dsk_wide = jnp.repeat(d_skip[0, :nh], HEAD_DIM)[None]

    def body(xbc_ref, dt_ref, bias_ref, alog_ref, dskw_ref, coh_ref, y_ref, hin_ref, state):
        c = pl.program_id(0)

        @pl.when(c == 0)
        def _():
            state[...] = jnp.zeros_like(state)

        hin_ref[...] = state[...]
        dt, A, cs, csT, tri = _ssd_prep(dt_ref[...], bias_ref[...], alog_ref[...])
        elast = jnp.exp(cs[Q - 1:Q, :])
        coh = coh_ref[...]
        dt_w, ecs_w, dend_w = _dot_exact(dt, coh), _dot_exact(jnp.exp(cs), coh), _dot_exact(jnp.exp(cs[Q - 1:Q, :] - cs), coh)
        for g in range(G):
            h0 = g * R
            cols = pl.ds(g * gw, gw)
            lanes = slice(g * gw, (g + 1) * gw)
            Bg = xbc_ref[:, pl.ds(di + g * N, N)]
            Cg = xbc_ref[:, pl.ds(di + G * N + g * N, N)]
            xg = xbc_ref[:, cols]
            Hg = state[cols, :]
            Gm = _dot_nt(Cg, Bg)
            xdt = xg * dt_w[:, lanes]
            yoff = _dot_nt(Cg, Hg) * ecs_w[:, lanes]
            ys = []
            for r in range(R):
                h = h0 + r
                L = jnp.exp(jnp.where(tri, cs[:, h:h + 1] - csT[h:h + 1, :], -jnp.inf))
                ys.append(_dot(Gm * L, xdt[:, r * HEAD_DIM:(r + 1) * HEAD_DIM]))
            y_ref[:, cols] = jnp.concatenate(ys, axis=1) + yoff + xg * dskw_ref[:, cols]
            hnew = _dot_tn(xdt * dend_w[:, lanes], Bg)
            escale = jnp.concatenate([jnp.broadcast_to(elast[:, h0 + r:h0 + r + 1], (HEAD_DIM, N)) for r in range(R)], axis=0)
            state[cols, :] = escale * Hg + hnew

    vec = pl.BlockSpec((1, LANES), lambda c: (0, 0))
    return pl.pallas_call(
        body, name=name, grid=(nc,),
        in_specs=[pl.BlockSpec((Q, CD), lambda c: (c, 0)), pl.BlockSpec((Q, LANES), lambda c: (c, 0)), vec, vec,
                  pl.BlockSpec((1, di), lambda c: (0, 0)), pl.BlockSpec((LANES, di), lambda c: (0, 0))],
        out_specs=[pl.BlockSpec((Q, di), lambda c: (c, 0)), pl.BlockSpec((None, di, N), lambda c: (c, 0, 0))],
        out_shape=[_sds((S, di)), _sds((nc, di, N))],
        scratch_shapes=[pltpu.VMEM((di, N), f32)],
        compiler_params=_params(("arbitrary",)),
    )(xbc, dtraw, dt_bias, a_log, dsk_wide, col_of_head)


def _dot_exact(a, b):
    bb = b.astype(bf16)
    hi = a.astype(bf16)
    rest = a - hi.astype(f32)
    mid = rest.astype(bf16)
    low = (rest - mid.astype(f32)).astype(bf16)
    one_pass = lambda t: jnp.dot(t, bb, preferred_element_type=f32)
    return one_pass(hi) + one_pass(mid) + one_pass(low)


def _ssd_bwd(xbc, dtraw, dt_bias, a_log, d_skip, hin, y, dy, di, name):
    S, CD = xbc.shape
    Q, N, G = SSM_CHUNK, SSM_STATE, SSM_GROUPS
    nc = S // Q
    nh = di // HEAD_DIM
    R = nh // G
    gw = R * HEAD_DIM
    P = HEAD_DIM
    head_of_col = jnp.asarray((np.arange(di)[:, None] // P == np.arange(LANES)[None, :]).astype(np.float32))
    dsk_wide = jnp.repeat(d_skip[0, :nh], P)[None]

    def body(xbc_ref, dt_ref, bias_ref, alog_ref, dskw_ref, hoc_ref, hin_ref, y_ref, dy_ref,
             dxbc_ref, ddt_ref, dA_ref, ddsk_ref, dtb_ref, dstate, dxdt_all, tend_all, yoff_all, colterm_all):
        c = pl.program_id(0)

        @pl.when(c == 0)
        def _():
            dstate[...] = jnp.zeros_like(dstate)
            dA_ref[...] = jnp.zeros_like(dA_ref)
            ddsk_ref[...] = jnp.zeros_like(ddsk_ref)
            dtb_ref[...] = jnp.zeros_like(dtb_ref)

        dtraw_v = dt_ref[...]
        dt, A, cs, csT, tri = _ssd_prep(dtraw_v, bias_ref[...], alog_ref[...])
        tri_t = jnp.logical_not(tri) | (lax.broadcasted_iota(jnp.int32, (Q, Q), 0) == lax.broadcasted_iota(jnp.int32, (Q, Q), 1))
        ecs = jnp.exp(cs)
        dend = jnp.exp(cs[Q - 1:Q, :] - cs)
        elast = jnp.exp(cs[Q - 1:Q, :])
        hoc = hoc_ref[...]
        state_dot = jnp.sum(_dot_exact(dstate[...] * hin_ref[...], jnp.ones((N, LANES), f32)) * hoc, 0, keepdims=True) * elast
        for g in range(G):
            h0 = g * R
            Bg = xbc_ref[:, pl.ds(di + g * N, N)]
            Cg = xbc_ref[:, pl.ds(di + G * N + g * N, N)]
            xg = xbc_ref[:, pl.ds(g * gw, gw)]
            dyg = dy_ref[:, pl.ds(g * gw, gw)]
            Hg = hin_ref[pl.ds(g * gw, gw), :]
            dHg = dstate[pl.ds(g * gw, gw), :]
            dt_e = _expand(dt, h0, R, P)
            ecs_e = _expand(ecs, h0, R, P)
            dend_e = _expand(dend, h0, R, P)
            cols = pl.ds(g * gw, gw)
            Gm = _dot_nt(Cg, Bg)
            Gm_t = _dot_nt(Bg, Cg)
            xdt = xg * dt_e
            dye = dyg * ecs_e
            bdh = _dot_nt(Bg, dHg)
            dC = _dot(dye, Hg)
            dB = _dot(xdt * dend_e, dHg)
            dHin = _dot_tn(dye, Cg)
            dxdt_state = dend_e * bdh
            end_term = xdt * dxdt_state
            tend_all[:, cols] = end_term
            yoff_all[:, cols] = _dot_nt(Cg, Hg) * ecs_e
            dG = jnp.zeros((Q, Q), f32)
            dxd = []
            for r in range(R):
                h = h0 + r
                sl = slice(r * P, (r + 1) * P)
                seg = cs[:, h:h + 1] - csT[h:h + 1, :]
                L = jnp.exp(jnp.where(tri, seg, -jnp.inf))
                L_t = jnp.exp(jnp.where(tri_t, -seg, -jnp.inf))
                dyh = dyg[:, sl]
                dG = dG + _dot_nt(dyh, xdt[:, sl]) * L
                dxd.append(_dot(Gm_t * L_t, dyh))
            dxdt_diag = jnp.concatenate(dxd, axis=1)
            dxdt = dxdt_diag + dxdt_state
            dxdt_all[:, cols] = dxdt
            colterm_all[:, cols] = xdt.astype(bf16).astype(f32) * dxdt_diag + end_term
            dxbc_ref[:, cols] = dxdt * dt_e + dyg * dskw_ref[:, cols]
            dxbc_ref[:, pl.ds(di + g * N, N)] = dB + _dot_tn(dG, Cg)
            dxbc_ref[:, pl.ds(di + G * N + g * N, N)] = dC + _dot(dG, Bg)
            escale = jnp.concatenate([jnp.broadcast_to(elast[:, h0 + r:h0 + r + 1], (P, N)) for r in range(R)], axis=0)
            dstate[pl.ds(g * gw, gw), :] = escale * dHg + dHin
        xs = xbc_ref[:, pl.ds(0, di)]
        dyv = dy_ref[...]
        yoff = yoff_all[...]
        y_diag = y_ref[...] - dskw_ref[...] * xs - yoff
        rs_y = _dot_exact(dyv.astype(bf16).astype(f32) * y_diag + dyv * yoff, hoc)
        rs_c = _dot_exact(colterm_all[...], hoc)
        rs_x = _dot_exact(dxdt_all[...] * xs, hoc)
        end_dot = _dot_exact(jnp.broadcast_to(jnp.sum(tend_all[...], 0, keepdims=True), (8, di)), hoc)[0:1]
        last = lax.broadcasted_iota(jnp.int32, (Q, 1), 0) == Q - 1
        dcs = rs_y - rs_c + jnp.where(last, end_dot + state_dot, 0.0)
        da = lax.dot_general(tri.astype(f32), dcs, (((0,), (0,)), ((), ())), preferred_element_type=f32,
                             precision=lax.Precision.HIGHEST)
        ddt = da * A + rs_x
        ddtraw = ddt * jax.nn.sigmoid(dtraw_v + bias_ref[...])
        ddt_ref[...] = ddtraw.astype(ddt_ref.dtype)
        dA_ref[...] += jnp.sum(da * dt, 0, keepdims=True) * A
        ddsk_ref[...] += jnp.sum(_dot_exact(dyv * xs, hoc), 0, keepdims=True)
        dtb_ref[...] += jnp.sum(ddtraw, 0, keepdims=True)

    vec = pl.BlockSpec((1, LANES), lambda c: (0, 0))
    rev = lambda c: (nc - 1 - c, 0)
    return pl.pallas_call(
        body, name=name, grid=(nc,),
        in_specs=[pl.BlockSpec((Q, CD), rev), pl.BlockSpec((Q, LANES), rev), vec, vec,
                  pl.BlockSpec((1, di), lambda c: (0, 0)), pl.BlockSpec((di, LANES), lambda c: (0, 0)),
                  pl.BlockSpec((None, di, N), lambda c: (nc - 1 - c, 0, 0)), pl.BlockSpec((Q, di), rev),
                  pl.BlockSpec((Q, di), rev)],
        out_specs=[pl.BlockSpec((Q, CD), rev), pl.BlockSpec((Q, LANES), rev), vec, vec, vec],
        out_shape=[_sds((S, CD)), _sds((S, LANES), bf16), _sds((1, LANES)), _sds((1, LANES)), _sds((1, LANES))],
        scratch_shapes=[pltpu.VMEM((di, N), f32)] + [pltpu.VMEM((Q, di), f32)] * 4,
        compiler_params=_params(("arbitrary",)),
    )(xbc, dtraw, dt_bias, a_log, dsk_wide, head_of_col, hin, y, dy)


def _t5_bucket_np(dist):
    max_exact = REL_BUCKETS // 2
    n = np.maximum(dist, 1).astype(np.float32)
    large = np.float32(max_exact) + np.log(n / np.float32(max_exact)) / np.float32(math.log(REL_MAX_DIST / max_exact)) * np.float32(REL_BUCKETS - max_exact)
    large = np.minimum(large.astype(np.int32), REL_BUCKETS - 1)
    return np.where(dist < max_exact, dist, large)


def _bucket_onehot():
    i = np.arange(ATT_BLK)[None, :]
    j = np.arange(2 * ATT_BLK)[:, None]
    delta = np.maximum(ATT_BLK + i - j, 0)
    out = np.zeros((len(ATT_DILATIONS), REL_BUCKETS, ATT_BLK * 2 * ATT_BLK), np.float32)
    for gi, d in enumerate(ATT_DILATIONS):
        b = _t5_bucket_np(delta * d).reshape(-1)
        out[gi, b, np.arange(b.size)] = 1.0
    return out


def _exact_mm(a, b, *, name, tb=False):
    M, K = a.shape
    N = b.shape[0] if tb else b.shape[1]
    tn = _tile(N, 4096, LANES)
    dn = (((1,), (1 if tb else 0,)), ((), ()))

    def body(a_ref, b_ref, o_ref):
        o_ref[...] = lax.dot_general(a_ref[...], b_ref[...], dn, preferred_element_type=f32,
                                     precision=lax.Precision.HIGHEST)

    return pl.pallas_call(
        body, name=name, grid=(N // tn,),
        in_specs=[pl.BlockSpec((M, K), lambda j: (0, 0)),
                  pl.BlockSpec((tn, K), lambda j: (j, 0)) if tb else pl.BlockSpec((K, tn), lambda j: (0, j))],
        out_specs=pl.BlockSpec((M, tn), lambda j: (0, j)), out_shape=_sds((M, N)),
        compiler_params=_params(("parallel",)),
    )(a, b)


def _band_penalty():
    i = np.arange(ATT_BLK)[None, :]
    j = np.arange(2 * ATT_BLK)[:, None]
    delta = ATT_BLK + i - j
    return np.where((delta >= 0) & (delta <= ATT_BLK), 0.0, -np.inf).astype(np.float32)


def _first_block_keep(n):
    key = lax.broadcasted_iota(jnp.int32, (2 * ATT_BLK, ATT_BLK), 0)
    return (key >= ATT_BLK) | (n > 0)


ATT_SCALE = HEAD_DIM ** -0.5


def _rows(ref, r, d):
    return ref[...] if d == 1 else ref[pl.ds(r, ATT_BLK, stride=d), :]


def _set_rows(ref, r, d, val):
    if d == 1:
        ref[...] = val
    else:
        ref[pl.ds(r, ATT_BLK, stride=d), :] = val


def _attn_width(d, D):
    return D if d == 1 else LANES


def _over_residues(d, one, unroll=1):
    if d == 1:
        one(0)
    else:
        lax.fori_loop(0, d, lambda r, c: (one(r), c)[1], 0, unroll=unroll)


def _attn_fwd(q, k, v, bias_t, d, name):
    S, D = q.shape
    nb = S // (d * ATT_BLK)
    H = D // HEAD_DIM
    W = _attn_width(d, D)
    HB = W // HEAD_DIM

    def body(q_ref, kp_ref, kc_ref, vp_ref, vc_ref, b_ref, o_ref, lse_ref):
        keep = _first_block_keep(pl.program_id(1))
        first = lax.broadcasted_iota(jnp.int32, (1, LANES), 1) < HEAD_DIM

        def one(r):
            qs = (_rows(q_ref, r, d) * ATT_SCALE).astype(bf16)
            kcat = jnp.concatenate([_rows(kp_ref, r, d), _rows(kc_ref, r, d)], axis=0).astype(bf16)
            vcat = jnp.concatenate([_rows(vp_ref, r, d), _rows(vc_ref, r, d)], axis=0).astype(bf16)
            outs = []
            for pair in range(W // LANES):
                ps = slice(pair * LANES, (pair + 1) * LANES)
                q2, k2, v2 = qs[:, ps], kcat[:, ps], vcat[:, ps]
                o2 = jnp.zeros((ATT_BLK, LANES), f32)
                for e in range(2):
                    h = 2 * pair + e
                    mine = first if e == 0 else jnp.logical_not(first)
                    zero = jnp.zeros((), bf16)
                    st = jnp.where(keep, _dot_nt(k2, jnp.where(mine, q2, zero)) + b_ref[h], -jnp.inf)
                    m = jnp.max(st, 0, keepdims=True)
                    pt = jnp.exp(st - m)
                    l = jnp.sum(pt, 0, keepdims=True)
                    o2 = o2 + _dot_tn(pt * (1.0 / l), jnp.where(mine, v2, zero))
                    lse_ref[r, h] = m + jnp.log(l)
                outs.append(o2)
            _set_rows(o_ref, r, d, jnp.concatenate(outs, axis=1))

        _over_residues(d, one, unroll=4)

    cur = pl.BlockSpec((ATT_BLK * d, W), lambda j, n: (n, j))
    prev = pl.BlockSpec((ATT_BLK * d, W), lambda j, n: (jnp.maximum(n - 1, 0), j))
    return pl.pallas_call(
        body, name=name, grid=(D // W, nb),
        in_specs=[cur, prev, cur, prev, cur, pl.BlockSpec((HB, 2 * ATT_BLK, ATT_BLK), lambda j, n: (j, 0, 0))],
        out_specs=[cur, pl.BlockSpec((None, d, HB, 1, LANES), lambda j, n: (n, 0, j, 0, 0))],
        out_shape=[_sds((S, D)), _sds((nb, d, H, 1, LANES))],
        compiler_params=_params(("parallel", "arbitrary")),
    )(q, k, k, v, v, bias_t)


def _from_blocks(rows, lanes=None):
    nb, d, H = rows.shape[:3]
    a = jnp.transpose(rows[:, :, :, 0, :], (0, 3, 1, 2)).reshape(nb * ATT_BLK * d, H)
    return a if lanes is None else jnp.pad(a, ((0, 0), (0, lanes - H)))


def _by_block(a, d):
    S, H = a.shape
    t = jnp.transpose(a.reshape(S // (d * ATT_BLK), ATT_BLK, d, H), (0, 2, 3, 1))
    return t[:, :, :, None, :]


def _head_sums(a, b, name):
    S, D = a.shape
    tr = _tile(S, 512, 8)
    hoc = jnp.asarray((np.arange(D)[:, None] // HEAD_DIM == np.arange(LANES)[None, :]).astype(np.float32))

    def body(a_ref, b_ref, h_ref, o_ref):
        o_ref[...] = _dot_exact(a_ref[...] * b_ref[...], h_ref[...])

    return pl.pallas_call(
        body, name=name, grid=(S // tr,),
        in_specs=[pl.BlockSpec((tr, D), lambda i: (i, 0)), pl.BlockSpec((tr, D), lambda i: (i, 0)),
                  pl.BlockSpec((D, LANES), lambda i: (0, 0))],
        out_specs=pl.BlockSpec((tr, LANES), lambda i: (i, 0)), out_shape=_sds((S, LANES)),
        compiler_params=_params(("parallel",)),
    )(a, b, hoc)


def _attn_bwd(q, k, v, bias_t, datt, lse_rows, dsum_rows, d, name):
    S, D = q.shape
    nb = S // (d * ATT_BLK)
    H = D // HEAD_DIM
    W = _attn_width(d, D)
    HB = W // HEAD_DIM

    def body(q_ref, kp_ref, kc_ref, vp_ref, vc_ref, b_ref, do_ref, lse_ref, dsum_ref,
             dq_ref, dk_ref, dv_ref, db_ref, carry_k, carry_v):
        j = pl.program_id(0)
        n = pl.program_id(1)

        @pl.when(n == 0)
        def _():
            carry_k[...] = jnp.zeros_like(carry_k)
            carry_v[...] = jnp.zeros_like(carry_v)
            db_ref[...] = jnp.zeros_like(db_ref)

        @pl.when(n < nb)
        def _():
            key = lax.broadcasted_iota(jnp.int32, (2 * ATT_BLK, ATT_BLK), 0)
            keep = (key >= ATT_BLK) | (n > 0)
            first = lax.broadcasted_iota(jnp.int32, (1, LANES), 1) < HEAD_DIM

            def one(r):
                qs = (_rows(q_ref, r, d) * ATT_SCALE).astype(bf16)
                kcat = jnp.concatenate([_rows(kp_ref, r, d), _rows(kc_ref, r, d)], axis=0).astype(bf16)
                vcat = jnp.concatenate([_rows(vp_ref, r, d), _rows(vc_ref, r, d)], axis=0).astype(bf16)
                dob = _rows(do_ref, r, d).astype(bf16)
                dqs, dks, dvs = [], [], []
                for pair in range(W // LANES):
                    ps = slice(pair * LANES, (pair + 1) * LANES)
                    q2, k2, v2, do2 = qs[:, ps], kcat[:, ps], vcat[:, ps], dob[:, ps]
                    dq2 = jnp.zeros((ATT_BLK, LANES), f32)
                    dk2 = jnp.zeros((2 * ATT_BLK, LANES), f32)
                    dv2 = jnp.zeros((2 * ATT_BLK, LANES), f32)
                    for e in range(2):
                        h = 2 * pair + e
                        mine = first if e == 0 else jnp.logical_not(first)
                        zero = jnp.zeros((), bf16)
                        qm, dom, km = jnp.where(mine, q2, zero), jnp.where(mine, do2, zero), jnp.where(mine, k2, zero)
                        st = jnp.where(keep, _dot_nt(k2, qm) + b_ref[h], -jnp.inf)
                        pt = jnp.exp(st - lse_ref[r, j * HB + h])
                        dst = pt * (_dot_nt(v2, dom) - dsum_ref[r, j * HB + h])
                        db_ref[h] += dst
                        dv2 = dv2 + _dot(pt, dom)
                        dk2 = dk2 + _dot(dst, qm)
                        dq2 = dq2 + _dot_tn(dst, km)
                    dqs.append(dq2 * ATT_SCALE)
                    dks.append(dk2)
                    dvs.append(dv2)
                _set_rows(dq_ref, r, d, jnp.concatenate(dqs, axis=1))
                dk = jnp.concatenate(dks, axis=1)
                dv = jnp.concatenate(dvs, axis=1)
                _set_rows(dk_ref, r, d, carry_k[r] + dk[:ATT_BLK])
                _set_rows(dv_ref, r, d, carry_v[r] + dv[:ATT_BLK])
                carry_k[r] = dk[ATT_BLK:]
                carry_v[r] = dv[ATT_BLK:]

            _over_residues(d, one, unroll=2)

        @pl.when(n == nb)
        def _():
            def last(r):
                _set_rows(dk_ref, r, d, carry_k[r])
                _set_rows(dv_ref, r, d, carry_v[r])

            _over_residues(d, last)

    nq = lambda n: jnp.minimum(n, nb - 1)
    cur = pl.BlockSpec((ATT_BLK * d, W), lambda j, n: (nq(n), j))
    prev = pl.BlockSpec((ATT_BLK * d, W), lambda j, n: (jnp.maximum(nq(n) - 1, 0), j))
    done = pl.BlockSpec((ATT_BLK * d, W), lambda j, n: (jnp.maximum(n - 1, 0), j))
    bspec = pl.BlockSpec((HB, 2 * ATT_BLK, ATT_BLK), lambda j, n: (j, 0, 0))
    rows = pl.BlockSpec((None, d, H, 1, LANES), lambda j, n: (nq(n), 0, 0, 0, 0))
    return pl.pallas_call(
        body, name=name, grid=(D // W, nb + 1),
        in_specs=[cur, prev, cur, prev, cur, bspec, cur, rows, rows],
        out_specs=[cur, done, done, bspec],
        out_shape=[_sds((S, D)), _sds((S, D)), _sds((S, D)), _sds((H, 2 * ATT_BLK, ATT_BLK))],
        scratch_shapes=[pltpu.VMEM((d, ATT_BLK, W), f32), pltpu.VMEM((d, ATT_BLK, W), f32)],
        compiler_params=_params(("arbitrary", "arbitrary")),
    )(q, k, k, v, v, bias_t, datt, lse_rows, dsum_rows)


def _attn_combine(os_, lses, name):
    S, D = os_[0].shape
    tr = _tile(S, 128, 16)
    head_cols = jnp.asarray((np.arange(LANES)[:, None] == np.arange(D)[None, :] // HEAD_DIM).astype(np.float32))

    def body(o0, o1, o2, l0, l1, l2, hc_ref, att_ref, attb_ref, lse_ref):
        a, b, c = l0[...], l1[...], l2[...]
        m = jnp.maximum(jnp.maximum(a, b), c)
        e0, e1, e2 = jnp.exp(a - m), jnp.exp(b - m), jnp.exp(c - m)
        tot = e0 + e1 + e2
        wide = lambda w: _dot_exact(w / tot, hc_ref[...])
        att = wide(e0) * o0[...] + wide(e1) * o1[...] + wide(e2) * o2[...]
        att_ref[...] = att
        attb_ref[...] = att.astype(bf16)
        lse_ref[...] = m + jnp.log(tot)

    wide_spec = pl.BlockSpec((tr, D), lambda i: (i, 0))
    lane_spec = pl.BlockSpec((tr, LANES), lambda i: (i, 0))
    return pl.pallas_call(
        body, name=name, grid=(S // tr,),
        in_specs=[wide_spec] * 3 + [lane_spec] * 3 + [pl.BlockSpec((LANES, D), lambda i: (0, 0))],
        out_specs=[wide_spec, wide_spec, lane_spec], out_shape=[_sds((S, D)), _sds((S, D), bf16), _sds((S, LANES))],
        compiler_params=_params(("parallel",)),
    )(*os_, *lses, head_cols)


ANY = pl.BlockSpec(memory_space=pl.ANY)


def _all_gather(vs, name):
    n = len(vs)

    def body(*refs):
        x_refs, out_refs = refs[:n], refs[n:2 * n]
        send_sems, recv_sems, local_sems = refs[2 * n:]
        x, y, c = lax.axis_index("x"), lax.axis_index("y"), lax.axis_index("c")
        me, sibling = (x, y, c), (x, y, 1 - c)
        chips = [(1 - x, y), (x, 1 - y), (1 - x, 1 - y)]

        def slot(i, px, py, pc):
            return out_refs[i].at[4 * px + 2 * py + pc]

        def copy(i, k, block, to, src=None):
            return pltpu.make_async_remote_copy(
                src_ref=slot(i, *block) if src is None else src, dst_ref=slot(i, *block),
                send_sem=send_sems.at[i, k], recv_sem=recv_sems.at[i, k], device_id=to, device_id_type=MESH)

        mine = [pltpu.make_async_copy(x_refs[i], slot(i, *me), local_sems.at[i]) for i in range(n)]
        for cp in mine:
            cp.start()
        first = []
        for i in range(n):
            first.append(copy(i, 0, me, sibling, src=x_refs[i]))
            first += [copy(i, 1 + j, me, (*chip, c), src=x_refs[i]) for j, chip in enumerate(chips)]
        for cp in first:
            cp.start()
        passed = []
        for i in range(n):
            for j, chip in enumerate(chips):
                copy(i, 1 + j, (*chip, c), me).wait_recv()
                cp = copy(i, 4 + j, (*chip, c), sibling)
                cp.start()
                passed.append(cp)
        for i in range(n):
            copy(i, 0, sibling, me).wait_recv()
            for j, chip in enumerate(chips):
                copy(i, 4 + j, (*chip, 1 - c), me).wait_recv()
        for cp in first + passed:
            cp.wait_send()
        for cp in mine:
            cp.wait()

    return pl.pallas_call(
        body, name=name, out_shape=[_sds((N_DEV,) + v.shape, v.dtype) for v in vs], in_specs=[ANY] * n,
        out_specs=[ANY] * n,
        scratch_shapes=[pltpu.SemaphoreType.DMA((n, 7)), pltpu.SemaphoreType.DMA((n, 7)), pltpu.SemaphoreType.DMA((n,))],
    )(*vs)


def _sum_slots(t, name):
    n, R, C = t.shape
    tr = _tile(R, PACK_ROW_TILE, 16)

    def body(t_ref, o_ref):
        acc = t_ref[0].astype(f32)
        for k in range(1, n):
            acc = acc + t_ref[k].astype(f32)
        o_ref[...] = acc

    return pl.pallas_call(
        body, name=name, grid=(R // tr,),
        in_specs=[pl.BlockSpec((n, tr, C), lambda i: (0, i, 0))],
        out_specs=pl.BlockSpec((tr, C), lambda i: (i, 0)), out_shape=_sds((R, C)),
        compiler_params=_params(("parallel",)),
    )(t)


HBM_SPEC = pl.BlockSpec(memory_space=pltpu.HBM)
SEM_SPEC = pl.BlockSpec(memory_space=pltpu.SEMAPHORE)
EFFECT = pltpu.SideEffectType.DATAFLOW_SIDE_EFFECTING


def _mesh_pos(p):
    return (p // 4, (p // 2) % 2, p % 2)


def _exchange_copy(src_refs, land_refs, send_sems, recv_sems, whole, dests, i, k):
    me = 4 * lax.axis_index("x") + 2 * lax.axis_index("y") + lax.axis_index("c")
    to = (me + k) % N_DEV
    frm = (me + N_DEV - k) % N_DEV
    lo, hi = dests
    src = src_refs[i] if whole else src_refs[i].at[jnp.minimum(jnp.maximum(to - lo, 0), hi - lo - 1)]
    s = i * (N_DEV - 1) + k - 1
    send = pltpu.make_async_remote_copy(src_ref=src, dst_ref=land_refs[i].at[me], send_sem=send_sems.at[s],
                                        recv_sem=recv_sems.at[s], device_id=_mesh_pos(to), device_id_type=MESH)
    recv = pltpu.make_async_remote_copy(src_ref=src, dst_ref=land_refs[i].at[frm], send_sem=send_sems.at[s],
                                        recv_sem=recv_sems.at[s], device_id=_mesh_pos(to), device_id_type=MESH)
    return send, recv, (to >= lo) & (to < hi), (me >= lo) & (me < hi)


def _exchange_start(srcs, whole, name, after=None, dests=(0, N_DEV)):
    n = len(srcs)
    lands = [lax.empty((N_DEV,) + s.shape[-2:], s.dtype) for s in srcs]
    after = list(after or [])
    n_in = 2 * n + len(after)
    everyone = dests == (0, N_DEV)

    def body(*refs):
        src_refs, land_refs = refs[:n], refs[n:2 * n]
        send_sems, recv_sems, token = refs[n_in], refs[n_in + 1], refs[-1]
        for i in range(n):
            for k in range(1, N_DEV):
                send, _, sends, _ = _exchange_copy(src_refs, land_refs, send_sems, recv_sems, whole, dests, i, k)
                if everyone:
                    send.start()
                else:
                    pl.when(sends)(send.start)
        token[...] = jnp.zeros_like(token)

    sems = pltpu.SemaphoreType.DMA((n * (N_DEV - 1),))
    outs = pl.pallas_call(
        body, name=name,
        out_shape=(sems, sems, *[pltpu.HBM(a.shape, a.dtype) for a in srcs + lands], _sds((8, LANES))),
        in_specs=[HBM_SPEC] * (2 * n) + [pl.BlockSpec(memory_space=pl.ANY)] * len(after),
        out_specs=(SEM_SPEC, SEM_SPEC, *[HBM_SPEC] * (2 * n), pl.BlockSpec(memory_space=pltpu.VMEM)),
        input_output_aliases={i: 2 + i for i in range(2 * n)},
        compiler_params=pltpu.CompilerParams(has_side_effects=EFFECT),
    )(*[pltpu.with_memory_space_constraint(a, pltpu.HBM) for a in srcs + lands], *after)
    return (outs[0], outs[1], list(outs[2:2 + n]), list(outs[2 + n:2 + 2 * n]), whole, dests), outs[-1]


def _exchange_wait(handle, after, name):
    send_sems, recv_sems, srcs, lands, whole, dests = handle
    n = len(srcs)
    everyone = dests == (0, N_DEV)

    def body(*refs):
        src_refs, land_refs = refs[:n], refs[n:2 * n]
        send_sems, recv_sems = refs[2 * n], refs[2 * n + 1]
        for i in range(n):
            for k in range(1, N_DEV):
                send, recv, sends, receives = _exchange_copy(src_refs, land_refs, send_sems, recv_sems, whole, dests, i, k)
                if everyone:
                    send.wait_send()
                    recv.wait_recv()
                else:
                    pl.when(sends)(send.wait_send)
                    pl.when(receives)(recv.wait_recv)

    outs = pl.pallas_call(
        body, name=name, out_shape=tuple(pltpu.HBM(a.shape, a.dtype) for a in srcs + lands),
        in_specs=[HBM_SPEC] * (2 * n) + [SEM_SPEC, SEM_SPEC, pl.BlockSpec(memory_space=pl.ANY)],
        out_specs=[HBM_SPEC] * (2 * n), input_output_aliases={i: i for i in range(2 * n)},
        compiler_params=pltpu.CompilerParams(has_side_effects=EFFECT),
    )(*srcs, *lands, send_sems, recv_sems, after)
    return list(outs[n:])


def _tie(v, token):
    return v + token[0:1, 0:1].astype(v.dtype).reshape((1,) * v.ndim)


def _with_own(land, own, me):
    return lax.dynamic_update_slice_in_dim(land, own[None].astype(land.dtype), me, 0)


class _Overlap:
    def __init__(self, shards, me, after):
        self.me = me
        self.names = list(shards)
        self.handle, self.token = _exchange_start([shards[nm] for nm in self.names], True, "weights_start", after)
        self.sent = {}

    def weights(self, after):
        lands = _exchange_wait(self.handle, after, "weights_wait")
        own = self.handle[2]
        return {nm: _full_from_blocks(nm, _with_own(land, o, self.me)) for nm, land, o in zip(self.names, lands, own)}

    def send(self, tag, grads, after=None):
        names = list(grads)
        handle, token = _exchange_start([_blocks_from_full(nm, grads[nm]) for nm in names], False, f"grads_start_{tag}",
                                        after)
        self.sent[tag] = (names, handle)
        return token

    def send_rows(self, tag, rows, dests, after=None):
        lo, hi = dests
        blocks = rows.reshape(hi - lo, rows.shape[0] // (hi - lo), rows.shape[1])
        handle, token = _exchange_start([blocks], False, f"grads_start_{tag}", after, dests)
        self.sent[tag] = ([tag], handle)
        return token

    def received(self, tag, after):
        names, handle = self.sent[tag]
        lands = _exchange_wait(handle, after, f"grads_wait_{tag}")
        lo = handle[5][0]
        own = [lax.dynamic_index_in_dim(b, self.me - lo, 0, keepdims=False) for b in handle[2]]
        return {nm: _with_own(land, o, self.me) for nm, land, o in zip(names, lands, own)}


ADAM_ROWS = 32


def _adamw(w, g, m, v, name):
    deep = w.ndim == 3
    R, C = w.shape[0], w.shape[-1]
    cb = LANES if C % LANES == 0 else C
    n_parts = g.shape[0] if g.ndim == 3 else 0

    def body(w_ref, g_ref, m_ref, v_ref, d_ref, m2_ref, v2_ref, *g_out):
        at = (lambda ref, sl: ref.at[sl, 0, :]) if deep else (lambda ref, sl: ref.at[sl, :])

        def update(sl):
            if n_parts:
                gv = g_ref[0, sl, :].astype(f32)
                for k in range(1, n_parts):
                    gv = gv + g_ref[k, sl, :].astype(f32)
                at(g_out[0], sl)[...] = gv
            else:
                gv = g_ref[sl, :]
            m2 = ADAM_B1 * at(m_ref, sl)[...] + (1.0 - ADAM_B1) * gv
            v2 = ADAM_B2 * at(v_ref, sl)[...] + (1.0 - ADAM_B2) * jnp.square(gv)
            m_hat = m2 / (1.0 - ADAM_B1 ** ADAM_STEP)
            v_hat = v2 / (1.0 - ADAM_B2 ** ADAM_STEP)
            at(d_ref, sl)[...] = -ADAM_LR * (m_hat / (jnp.sqrt(v_hat) + ADAM_EPS) + ADAM_WD * at(w_ref, sl)[...])
            at(m2_ref, sl)[...] = m2
            at(v2_ref, sl)[...] = v2

        main = R // ADAM_ROWS
        if main:
            lax.fori_loop(0, main, lambda i, c: (update(pl.ds(pl.multiple_of(i * ADAM_ROWS, ADAM_ROWS), ADAM_ROWS)), c)[1], 0)
        if R % ADAM_ROWS:
            update(pl.ds(main * ADAM_ROWS, R % ADAM_ROWS))

    spec = pl.BlockSpec((R, 1, cb), lambda j: (0, 0, j)) if deep else pl.BlockSpec((R, cb), lambda j: (0, j))
    g_spec = pl.BlockSpec((n_parts, R, cb), lambda j: (0, 0, j)) if n_parts else pl.BlockSpec((R, cb), lambda j: (0, j))
    n_out = 4 if n_parts else 3
    return pl.pallas_call(
        body, name=name, grid=(C // cb,), in_specs=[spec, g_spec, spec, spec], out_specs=[spec] * n_out,
        out_shape=[_sds(w.shape)] * n_out, compiler_params=_params(("parallel",)),
    )(w, g, m, v)


BIG_PARAMS = ("hy_w_in", "hy_w_out", "cv_w_pw1", "cv_w_pw2", "ffn_w_gate", "ffn_w_up", "ffn_w_down")


def _shards_2d(w):
    t = lambda a: jnp.transpose(a)
    return dict(in_t=t(w["hy_w_in"][0]), out=w["hy_w_out"][0], pw1=w["cv_w_pw1"][0], pw2=w["cv_w_pw2"][0],
                gate_t0=t(w["ffn_w_gate"][0]), gate_t1=t(w["ffn_w_gate"][1]), up_t0=t(w["ffn_w_up"][0]),
                up_t1=t(w["ffn_w_up"][1]), down0=w["ffn_w_down"][0], down1=w["ffn_w_down"][1])


def _unshard_2d(s):
    t = lambda a: jnp.transpose(a)
    out = dict(hy_w_out=s["out"][None], cv_w_pw1=s["pw1"][None], cv_w_pw2=s["pw2"][None],
               ffn_w_gate=jnp.stack([t(s["gate_t0"]), t(s["gate_t1"])]),
               ffn_w_up=jnp.stack([t(s["up_t0"]), t(s["up_t1"])]), ffn_w_down=jnp.stack([s["down0"], s["down1"]]))
    if "in_t" in s:
        out["hy_w_in"] = t(s["in_t"])[None]
    return out


def _full_from_blocks(nm, g):
    if nm == "pw1":
        return jnp.transpose(g, (1, 0, 2)).reshape(g.shape[1], N_DEV * g.shape[2])
    return g.reshape(N_DEV * g.shape[1], g.shape[2])


def _blocks_from_full(nm, g):
    if nm == "pw1":
        return jnp.transpose(g.reshape(g.shape[0], N_DEV, g.shape[1] // N_DEV), (1, 0, 2))
    return g.reshape(N_DEV, g.shape[0] // N_DEV, g.shape[1])


class _VecPack:
    def __init__(self, shapes):
        self.shapes = [tuple(s) for s in shapes]
        self.sizes = [int(np.prod(s)) for s in self.shapes]
        total = sum(self.sizes)
        self.rows = -(-(-(-total // LANES)) // 8) * 8
        self.total = total

    def pack(self, arrays):
        flat = jnp.concatenate([a.astype(f32).reshape(-1) for a in arrays])
        flat = jnp.pad(flat, (0, self.rows * LANES - self.total))
        return flat.reshape(self.rows, LANES)

    def unpack(self, packed):
        flat = packed.reshape(-1)
        out, off = [], 0
        for shp, n in zip(self.shapes, self.sizes):
            out.append(flat[off:off + n].reshape(shp))
            off += n
        return out

    def unpack_stacked(self, stacked, only=None):
        flat = stacked.reshape(stacked.shape[0], -1)
        offs = np.concatenate([[0], np.cumsum(self.sizes)])
        get = lambda i: flat[:, offs[i]:offs[i + 1]].reshape((stacked.shape[0],) + self.shapes[i])
        return get(only) if only is not None else [get(i) for i in range(len(self.shapes))]


def _row(v):
    return v.reshape(1, -1)


def _pad_lanes(v):
    v = v.reshape(1, -1)
    return jnp.pad(v, ((0, 0), (0, LANES - v.shape[1])))


def _ffn_fwd(h, w_gate_t, w_up_t, w_down, tag):
    F = w_down.shape[0]
    a = _mm(h, w_gate_t, tb=True, out_dtype=bf16, name=f"ffn_gate_{tag}")
    u = _mm(h, w_up_t, tb=True, out_dtype=bf16, name=f"ffn_up_{tag}")
    (f,), _ = _rowwise(f"swiglu_{tag}", lambda rv, vv: ([_silu(rv[0].astype(f32)) * rv[1].astype(f32)], []), [a, u], [],
                       [(F, bf16)], [], sub=16, col_chunk=_tile(F, 512, LANES))
    out = _mm(f, w_down, name=f"ffn_down_{tag}")
    return out, (a, u, f)


def _ffn_bwd(h, w_gate_t, w_up_t, w_down, saved, dout, tag):
    a, u, f = saved
    F = w_down.shape[0]
    df = _mm(dout, w_down, tb=True, out_dtype=bf16, name=f"ffn_down_dx_{tag}")
    dw_down = _mm(f, dout, ta=True, out_dtype=bf16, name=f"ffn_down_dw_{tag}")

    def fn(rv, vv):
        _, vjp = jax.vjp(lambda a_, u_: _silu(a_) * u_, rv[0].astype(f32), rv[1].astype(f32))
        da, du = vjp(rv[2].astype(f32))
        return [da, du], []

    (da, du), _ = _rowwise(f"swiglu_bwd_{tag}", fn, [a, u, df], [], [(F, bf16), (F, bf16)], [], sub=16,
                           col_chunk=_tile(F, 512, LANES))
    dh = _mm(du, w_up_t, add=_mm(da, w_gate_t, name=f"ffn_gate_dx_{tag}"), name=f"ffn_up_dx_{tag}")
    dw_gate_t = _mm(da, h, ta=True, out_dtype=bf16, name=f"ffn_gate_dw_{tag}")
    dw_up_t = _mm(du, h, ta=True, out_dtype=bf16, name=f"ffn_up_dw_{tag}")
    return dh, dw_gate_t, dw_up_t, dw_down


def _local_step(x, target, mod, w_in_t, comm, small):
    S, D = x.shape
    di = small["hy_ssm_norm_g"].shape[-1]
    nh = small["hy_dt_bias"].shape[-1]
    cd = small["hy_conv_b"].shape[-1]
    m = [[_row(mod[i, j]) for j in range(6)] for i in range(2)]

    off_q = di + cd + nh
    w_qkv_t = w_in_t[off_q:]
    seg = dict(z=(w_in_t, 0, di), xbc=(w_in_t, di, cd), dt=(w_in_t, di + cd, LANES))
    for i, nm in enumerate(("q0", "q1", "q2", "k", "v")):
        seg[nm] = (w_qkv_t, i * D, D)

    g_mix = [_row(small["norm_mix_g"][i]) for i in range(2)]
    g_ffn = [_row(small["norm_ffn_g"][i]) for i in range(2)]
    conv_w, conv_b = small["hy_conv_w_full"], _row(small["hy_conv_b"][0])
    dt_bias, a_log, d_skip = (_pad_lanes(small[k][0]) for k in ("hy_dt_bias", "hy_a_log", "hy_d_skip"))
    g_ssm = _row(small["hy_ssm_norm_g"][0])
    onehot = jnp.asarray(_bucket_onehot())
    rel_t = small["rel_table"].T
    H = D // HEAD_DIM
    bias = [_exact_mm(rel_t[gi * H:(gi + 1) * H], onehot[gi], name=f"rel_bias_{gi}")
            .reshape(H, 2 * ATT_BLK, ATT_BLK) + _band_penalty() for gi in range(3)]

    h1 = _adaln_fwd(x, g_mix[0], m[0][1], m[0][0], "adaln_mix0")
    proj = {nm: _mm(h1, mat, tb=True, b_rows=(off, cnt), name=f"in_{nm}") for nm, (mat, off, cnt) in seg.items()}
    xbc_pre, xbc = _conv_fwd(proj["xbc"], conv_w, conv_b, silu=True, name="ssm_conv", tr=1024)
    y, hin = _ssd_fwd(xbc, proj["dt"], dt_bias, a_log, d_skip, di, "ssd_fwd")
    (yg,), _ = _rowwise("ssm_gate", lambda rv, vv: ([_gate_f(rv[0], rv[1], vv[0])], []),
                        [y, proj["z"]], [g_ssm], [(di, bf16)], [], sub=16)
    og = [_attn_fwd(proj[f"q{gi}"], proj["k"], proj["v"], bias[gi], d, f"attn_fwd_{gi}")
          for gi, d in enumerate(ATT_DILATIONS)]
    att, att_b, lse_tot = _attn_combine([a for a, _ in og], [_from_blocks(b, LANES) for _, b in og], "attn_combine")
    W = comm.weights(after=att_b)
    w_out_y, w_out_a = W["out"][:di], W["out"][di:]
    mix0 = _mm(att_b, w_out_a, add=_mm(yg, w_out_y, name="out_y"), name="out_a")
    x1, h2 = _resid_adaln_fwd(x, m[0][2], mix0, g_ffn[0], m[0][4], m[0][3], "resid_mix0_adaln_ffn0")
    f0, ffn0_saved = _ffn_fwd(h2, W["gate_t0"], W["up_t0"], W["down0"], "0")
    x2, h3 = _resid_adaln_fwd(x1, m[0][5], f0, g_mix[1], m[1][1], m[1][0], "resid_ffn0_adaln_mix1")
    pw1 = _mm(h3, W["pw1"], bias=_row(small["cv_b_pw1_full"]), name="cv_pw1")
    (u,), _ = _rowwise("cv_glu", lambda rv, vv: ([rv[0] * jax.nn.sigmoid(rv[1])], []),
                       [(pw1, 0, D), (pw1, 1, D)], [], [(D, f32)], [])
    (u2,) = _conv_fwd(u, small["cv_w_dw_full"], _row(small["cv_b_dw_full"]), silu=False, name="cv_dw")
    ln_g, ln_b = _row(small["cv_ln_g_full"]), _row(small["cv_ln_b_full"])
    (u3,), _ = _rowwise("cv_lnsilu", lambda rv, vv: ([_lnsilu_f(rv[0], vv[0], vv[1])], []),
                        [u2], [ln_g, ln_b], [(D, bf16)], [], sub=16)
    mix1 = _mm(u3, W["pw2"], bias=_row(small["cv_b_pw2_full"]), name="cv_pw2")
    x3, h4 = _resid_adaln_fwd(x2, m[1][2], mix1, g_ffn[1], m[1][4], m[1][3], "resid_mix1_adaln_ffn1")
    f1, ffn1_saved = _ffn_fwd(h4, W["gate_t1"], W["up_t1"], W["down1"], "1")

    g_fin = _row(small["final_norm_g"])
    dmod = [[None] * 6 for _ in range(2)]
    d_norm_mix, d_norm_ffn = [None, None], [None, None]
    big = {}

    def final_fn(rv, vv):
        xv, fv, tv = rv
        gate = vv[1]
        yv, vjp = jax.vjp(_rms, xv + gate * fv, vv[0])
        err = yv - tv
        dx, dg = vjp(err / D)
        part = 0.5 * jnp.sum(jnp.mean(err * err, -1, keepdims=True), 0, keepdims=True)
        return [dx, gate * dx], [dg, jnp.broadcast_to(part, (1, LANES)), jnp.sum(dx * fv, 0, keepdims=True)]

    (dx4, df1), (d_fin, loss, dmod[1][5]) = _rowwise("loss_head", final_fn, [x3, f1, target], [g_fin, m[1][5]],
                                                      [(D, f32), (D, bf16)], [D, LANES, D], sub=16)

    dh4, big["gate_t1"], big["up_t1"], big["down1"] = _ffn_bwd(h4, W["gate_t1"], W["up_t1"], W["down1"], ffn1_saved, df1, "1")
    dx3, dmix1, (d_norm_ffn[1], dmod[1][4], dmod[1][3], dmod[1][2], d_b_pw2) = _adaln_resid_bwd(
        x3, g_ffn[1], m[1][4], m[1][3], dh4, dx4, mix1, m[1][2], "adaln_ffn1_resid_mix1_bwd")
    du3 = _mm(dmix1, W["pw2"], tb=True, name="cv_pw2_dx")
    big["pw2"] = _mm(u3, dmix1, ta=True, out_dtype=bf16, name="cv_pw2_dw")

    def lnsilu_bwd(rv, vv):
        _, vjp = jax.vjp(_lnsilu_f, rv[0], vv[0], vv[1])
        du, dg, db = vjp(rv[1])
        return [du], [dg, db]

    (du2,), (d_ln_g, d_ln_b) = _rowwise("cv_lnsilu_bwd", lnsilu_bwd, [u2, du3], [ln_g, ln_b], [(D, f32)], [D, D])
    du, d_w_dw, d_b_dw = _conv_bwd(u, small["cv_w_dw_full"], du2, None, silu=False, name="cv_dw_bwd")

    def glu_bwd(rv, vv):
        a, gt, d = rv
        _, vjp = jax.vjp(lambda a_, g_: a_ * jax.nn.sigmoid(g_), a, gt)
        da, dg = vjp(d)
        return [da, dg], [jnp.sum(da, 0, keepdims=True), jnp.sum(dg, 0, keepdims=True)]

    (dpa, dpg), (d_b1a, d_b1g) = _rowwise("cv_glu_bwd", glu_bwd, [(pw1, 0, D), (pw1, 1, D), du], [],
                                           [(D, bf16), (D, bf16)], [D, D], sub=16)
    dpw1 = jnp.concatenate([dpa, dpg], axis=1)
    d_b_pw1 = jnp.concatenate([d_b1a, d_b1g], axis=1)
    dh3 = _mm(dpw1, W["pw1"], tb=True, name="cv_pw1_dx")
    big["pw1"] = _mm(h3, dpw1, ta=True, out_dtype=bf16, name="cv_pw1_dw")
    token = comm.send("layer1", {nm: big[nm] for nm in ("gate_t1", "up_t1", "down1", "pw2", "pw1")})
    dx2, df0, (d_norm_mix[1], dmod[1][1], dmod[1][0], dmod[0][5], _) = _adaln_resid_bwd(
        x2, g_mix[1], m[1][1], _tie(m[1][0], token), dh3, dx3, f0, m[0][5], "adaln_mix1_resid_ffn0_bwd")

    dh2, big["gate_t0"], big["up_t0"], big["down0"] = _ffn_bwd(h2, W["gate_t0"], W["up_t0"], W["down0"], ffn0_saved, df0, "0")
    dx1, dmix0, (d_norm_ffn[0], dmod[0][4], dmod[0][3], dmod[0][2], _) = _adaln_resid_bwd(
        x1, g_ffn[0], m[0][4], m[0][3], dh2, dx2, mix0, m[0][2], "adaln_ffn0_resid_mix0_bwd")
    dyg = _mm(dmix0, w_out_y, tb=True, name="out_y_dx")
    datt = _mm(dmix0, w_out_a, tb=True, name="out_a_dx")
    big["out"] = jnp.concatenate([_mm(yg, dmix0, ta=True, out_dtype=bf16, name="out_y_dw"),
                                  _mm(att_b, dmix0, ta=True, out_dtype=bf16, name="out_a_dw")], axis=0)
    token = comm.send("layer0", {nm: big[nm] for nm in ("gate_t0", "up_t0", "down0", "out")})
    g_ssm = _tie(g_ssm, token)

    def gate_bwd(rv, vv):
        _, vjp = jax.vjp(_gate_f, rv[0], rv[1], vv[0])
        dy_, dz_, dg_ = vjp(rv[2])
        return [dy_, dz_], [dg_]

    (dy, dz), (d_g_ssm,) = _rowwise("ssm_gate_bwd", gate_bwd, [y, proj["z"], dyg], [g_ssm], [(di, f32), (di, bf16)], [di],
                                    sub=16)
    dxbc, ddtraw, d_a_log, d_dskip, d_dt_bias = _ssd_bwd(xbc, proj["dt"], dt_bias, a_log, d_skip, hin, y, dy, di, "ssd_bwd")
    dxbc_pre, d_conv_w, d_conv_b = _conv_bwd(proj["xbc"], conv_w, dxbc, xbc_pre, silu=True, name="ssm_conv_bwd",
                                             dx_dtype=bf16, tr=1024)
    dh1 = None
    early = []
    for nm, dseg in (("z", dz), ("xbc", dxbc_pre), ("dt", ddtraw)):
        mat, off, cnt = seg[nm]
        dh1 = _mm(dseg, mat, b_rows=(off, cnt), add=dh1, name=f"in_{nm}_dx")
        dwp = _mm(dseg, h1, ta=True, out_dtype=bf16, name=f"in_{nm}_dw")
        early.append(dwp[:nh] if nm == "dt" else dwp)
    early = jnp.concatenate(early, axis=0)
    shard_rows = w_in_t.shape[0] // N_DEV
    n_early = off_q // shard_rows
    token = comm.send_rows("in_early", early[:n_early * shard_rows], (0, n_early))
    bias = [_tie(b, token) for b in bias]

    dq, dks, dvs, dbs = [], [], [], []
    lse_heads = lse_tot[:, :H]
    dsum_heads = _head_sums(att, datt, "attn_dsum")[:, :H]
    for gi, d in enumerate(ATT_DILATIONS):
        a, b, c_, e = _attn_bwd(proj[f"q{gi}"], proj["k"], proj["v"], bias[gi], datt,
                                _by_block(lse_heads, d), _by_block(dsum_heads, d), d, f"attn_bwd_{gi}")
        dq.append(a)
        dks.append(b)
        dvs.append(c_)
        dbs.append(e)
    dk = _add3(*dks, "attn_dk")
    dv = _add3(*dvs, "attn_dv")
    d_rel = jnp.concatenate(
        [_exact_mm(dbs[gi].reshape(H, -1), onehot[gi], tb=True, name=f"rel_grad_{gi}") for gi in range(3)], axis=0).T

    dsegs = (("q0", dq[0]), ("q1", dq[1]), ("q2", dq[2]), ("k", dk), ("v", dv))
    late = jnp.concatenate([early[n_early * shard_rows:]] +
                           [_mm(dseg, h1, ta=True, out_dtype=bf16, name=f"in_{nm}_dw") for nm, dseg in dsegs], axis=0)
    n_a = N_DEV - 1 - n_early
    token = comm.send_rows("in_late", late[:n_a * shard_rows], (n_early, N_DEV - 1))
    last_block = late[n_a * shard_rows:]
    w_qkv_after = _tie(w_qkv_t, token)
    for nm, dseg in dsegs:
        _, off, cnt = seg[nm]
        dh1 = _mm(dseg, w_qkv_after, b_rows=(off, cnt), add=dh1, name=f"in_{nm}_dx")
    dx0, (d_norm_mix[0], dmod[0][1], dmod[0][0]) = _adaln_bwd(x, g_mix[0], m[0][1], m[0][0], dh1, dx1, "adaln_mix0_bwd")

    smallg = dict(
        loss=loss, dmod=jnp.stack([jnp.concatenate(dmod[i], axis=1)[0] for i in range(2)]),
        norm_mix_g=jnp.concatenate(d_norm_mix, axis=0), norm_ffn_g=jnp.concatenate(d_norm_ffn, axis=0),
        hy_conv_w=d_conv_w, hy_conv_b=d_conv_b, hy_dt_bias=d_dt_bias[:, :nh], hy_a_log=d_a_log[:, :nh],
        hy_d_skip=d_dskip[:, :nh], hy_ssm_norm_g=d_g_ssm, rel_table=d_rel,
        cv_b_pw1=d_b_pw1, cv_w_dw=d_w_dw, cv_b_dw=d_b_dw, cv_ln_g=d_ln_g, cv_ln_b=d_ln_b, cv_b_pw2=d_b_pw2,
        final_norm_g=d_fin)
    return dx0, (last_block, n_early), smallg


SMALL_GRAD_ORDER = ("loss", "dmod", "norm_mix_g", "norm_ffn_g", "hy_conv_w", "hy_conv_b", "hy_dt_bias", "hy_a_log",
                    "hy_d_skip", "hy_ssm_norm_g", "rel_table", "cv_b_pw1", "cv_w_dw", "cv_b_dw", "cv_ln_g", "cv_ln_b",
                    "cv_b_pw2", "final_norm_g")


def kernel(x, c, ada_w, ada_b, norm_mix_g, norm_ffn_g, hy_w_in, hy_conv_w, hy_conv_b, hy_dt_bias, hy_a_log, hy_d_skip, hy_ssm_norm_g, hy_w_out, rel_table, cv_w_pw1, cv_b_pw1, cv_w_dw, cv_b_dw, cv_ln_g, cv_ln_b, cv_w_pw2, cv_b_pw2, ffn_w_gate, ffn_w_up, ffn_w_down, final_norm_g, loss_target, m_ada_w, m_ada_b, m_norm_mix_g, m_norm_ffn_g, m_hy_w_in, m_hy_conv_w, m_hy_conv_b, m_hy_dt_bias, m_hy_a_log, m_hy_d_skip, m_hy_ssm_norm_g, m_hy_w_out, m_rel_table, m_cv_w_pw1, m_cv_b_pw1, m_cv_w_dw, m_cv_b_dw, m_cv_ln_g, m_cv_ln_b, m_cv_w_pw2, m_cv_b_pw2, m_ffn_w_gate, m_ffn_w_up, m_ffn_w_down, m_final_norm_g, v_ada_w, v_ada_b, v_norm_mix_g, v_norm_ffn_g, v_hy_w_in, v_hy_conv_w, v_hy_conv_b, v_hy_dt_bias, v_hy_a_log, v_hy_d_skip, v_hy_ssm_norm_g, v_hy_w_out, v_rel_table, v_cv_w_pw1, v_cv_b_pw1, v_cv_w_dw, v_cv_b_dw, v_cv_ln_g, v_cv_ln_b, v_cv_w_pw2, v_cv_b_pw2, v_ffn_w_gate, v_ffn_w_up, v_ffn_w_down, v_final_norm_g):
    names = ("ada_w", "ada_b", "norm_mix_g", "norm_ffn_g", "hy_w_in", "hy_conv_w", "hy_conv_b", "hy_dt_bias", "hy_a_log",
             "hy_d_skip", "hy_ssm_norm_g", "hy_w_out", "rel_table", "cv_w_pw1", "cv_b_pw1", "cv_w_dw", "cv_b_dw", "cv_ln_g",
             "cv_ln_b", "cv_w_pw2", "cv_b_pw2", "ffn_w_gate", "ffn_w_up", "ffn_w_down", "final_norm_g")
    w = dict(zip(names, (ada_w, ada_b, norm_mix_g, norm_ffn_g, hy_w_in, hy_conv_w, hy_conv_b, hy_dt_bias, hy_a_log, hy_d_skip,
                         hy_ssm_norm_g, hy_w_out, rel_table, cv_w_pw1, cv_b_pw1, cv_w_dw, cv_b_dw, cv_ln_g, cv_ln_b, cv_w_pw2,
                         cv_b_pw2, ffn_w_gate, ffn_w_up, ffn_w_down, final_norm_g)))
    mom = dict(zip(names, (m_ada_w, m_ada_b, m_norm_mix_g, m_norm_ffn_g, m_hy_w_in, m_hy_conv_w, m_hy_conv_b, m_hy_dt_bias,
                           m_hy_a_log, m_hy_d_skip, m_hy_ssm_norm_g, m_hy_w_out, m_rel_table, m_cv_w_pw1, m_cv_b_pw1, m_cv_w_dw,
                           m_cv_b_dw, m_cv_ln_g, m_cv_ln_b, m_cv_w_pw2, m_cv_b_pw2, m_ffn_w_gate, m_ffn_w_up, m_ffn_w_down,
                           m_final_norm_g)))
    vel = dict(zip(names, (v_ada_w, v_ada_b, v_norm_mix_g, v_norm_ffn_g, v_hy_w_in, v_hy_conv_w, v_hy_conv_b, v_hy_dt_bias,
                           v_hy_a_log, v_hy_d_skip, v_hy_ssm_norm_g, v_hy_w_out, v_rel_table, v_cv_w_pw1, v_cv_b_pw1, v_cv_w_dw,
                           v_cv_b_dw, v_cv_ln_g, v_cv_ln_b, v_cv_w_pw2, v_cv_b_pw2, v_ffn_w_gate, v_ffn_w_up, v_ffn_w_down,
                           v_final_norm_g)))
    S, D = x.shape[1], x.shape[2]
    ax, ay, ac = lax.axis_index("x"), lax.axis_index("y"), lax.axis_index("c")
    me = 4 * ax + 2 * ay + ac
    nmod = ada_w.shape[2]

    w2 = _shards_2d(w)
    big_names = list(w2)
    sharded_small = ("hy_conv_w", "cv_b_pw1", "cv_w_dw", "cv_b_dw", "cv_ln_g", "cv_ln_b", "cv_b_pw2")
    vp = _VecPack([c.shape] + [w[nm].shape for nm in sharded_small])
    g_in, sg = _all_gather([w2["in_t"].astype(bf16), vp.pack([c] + [w[nm] for nm in sharded_small])], "gather_w_in")
    w_in_t = _full_from_blocks("in_t", g_in)
    parts = vp.unpack_stacked(sg)
    c_all = parts[0][:, 0]
    small = {k: w[k] for k in ("norm_mix_g", "norm_ffn_g", "hy_conv_b", "hy_dt_bias", "hy_a_log", "hy_d_skip",
                               "hy_ssm_norm_g", "rel_table", "final_norm_g")}
    for p, nm in zip(parts[1:], sharded_small):
        p = p[:, 0]
        p = jnp.moveaxis(p, 0, -2)
        small[nm + "_full"] = p.reshape(p.shape[:-2] + (N_DEV * p.shape[-1],))

    (cs_all,), _ = _rowwise("ada_silu", lambda rv, vv: ([_silu(rv[0])], []), [c_all], [], [(D, f32)], [])
    b_mine = lax.dynamic_slice_in_dim(ada_b, me * nmod, nmod, axis=1)
    mod_part = jnp.stack([_mm(cs_all, ada_w[i], bias=b_mine[i:i + 1], name=f"ada_mod_{i}") for i in range(2)])
    (mod_all,) = _all_gather([mod_part.reshape(2 * N_DEV, nmod)], "gather_mod")
    mod_all = mod_all.reshape(N_DEV, 2, N_DEV, nmod)
    mod_mine = lax.dynamic_index_in_dim(mod_all, me, axis=2, keepdims=False)
    mod = jnp.transpose(mod_mine, (1, 0, 2)).reshape(2, 6, D)
    comm = _Overlap({nm: w2[nm].astype(bf16) for nm in big_names if nm != "in_t"}, me, after=[mod, w_in_t])
    mod = _tie(mod, comm.token)

    dx0, (d_in_last, n_early), sgrad = _local_step(x[0], loss_target[0], mod, w_in_t, comm, small)

    gp = _VecPack([sgrad[k].shape for k in SMALL_GRAD_ORDER])
    (g_all,) = _all_gather([gp.pack([sgrad[k] for k in SMALL_GRAD_ORDER])], "gather_small_grads")
    comm.send_rows("in_last", d_in_last, (N_DEV - 1, N_DEV), after=[g_all])
    tot = dict(zip(SMALL_GRAD_ORDER, gp.unpack(_sum_slots(g_all, "sum_small_grads"))))
    dmod_all = gp.unpack_stacked(g_all, only=SMALL_GRAD_ORDER.index("dmod"))
    loss = tot["loss"][0, 0]

    grads = {}
    dmod_mine = lax.dynamic_slice_in_dim(dmod_all, me * nmod, nmod, axis=2)
    grads["ada_w"] = jnp.stack([_mm(cs_all, dmod_mine[:, i], ta=True, name=f"ada_w_grad_{i}") for i in range(2)])
    grads["ada_b"] = tot["dmod"]
    grads["norm_mix_g"], grads["norm_ffn_g"] = tot["norm_mix_g"], tot["norm_ffn_g"]
    grads["hy_conv_b"] = tot["hy_conv_b"]
    grads["hy_dt_bias"] = tot["hy_dt_bias"]
    grads["hy_a_log"] = tot["hy_a_log"]
    grads["hy_d_skip"] = tot["hy_d_skip"]
    grads["hy_ssm_norm_g"] = tot["hy_ssm_norm_g"]
    grads["rel_table"] = tot["rel_table"]
    grads["final_norm_g"] = tot["final_norm_g"][0]
    for nm in sharded_small:
        n = w[nm].shape[-1]
        grads[nm] = lax.dynamic_slice_in_dim(tot[nm], me * n, n, axis=1).reshape(w[nm].shape)

    delta, new_m, new_v = {}, {}, {}
    shp = ada_w.shape
    two = lambda t: t.reshape(-1, shp[-1])
    d_, m_, v_ = _adamw(two(ada_w), two(grads["ada_w"]), two(m_ada_w), two(v_ada_w), "adamw_ada_w")
    delta["ada_w"], new_m["ada_w"], new_v["ada_w"] = d_.reshape(shp), m_.reshape(shp), v_.reshape(shp)
    rest = [nm for nm in names if nm not in BIG_PARAMS and nm != "ada_w"]
    sp = _VecPack([w[nm].shape for nm in rest])
    packs = [sp.pack([t[nm] for nm in rest]) for t in (w, grads, mom, vel)]
    ds_, ms_, vs_ = _adamw(*packs, "adamw_small")
    for nm, a, b, e in zip(rest, sp.unpack(ds_), sp.unpack(ms_), sp.unpack(vs_)):
        delta[nm], new_m[nm], new_v[nm] = a, b, e

    m2, v2 = _shards_2d(mom), _shards_2d(vel)
    g2, d2, nm2, nv2 = {}, {}, {}, {}
    after = d_
    slots = {}
    for tag in ("layer1", "layer0", "in_early", "in_late", "in_last"):
        slots.update(comm.received(tag, after))
        if tag.startswith("in_"):
            continue
        for nm in comm.sent[tag][0]:
            d2[nm], nm2[nm], nv2[nm], g2[nm] = _adamw(w2[nm], slots[nm], m2[nm], v2[nm], f"adamw_{nm}")
            after = g2[nm]
    stored = lambda t: jnp.transpose(t, (2, 0, 1))
    in_slots = jnp.where(me < n_early, slots["in_early"], jnp.where(me < N_DEV - 1, slots["in_late"], slots["in_last"]))
    d_in, m_in, v_in, g_in = _adamw(stored(hy_w_in), in_slots, stored(m_hy_w_in), stored(v_hy_w_in), "adamw_in_t")
    for dst, part, t in ((grads, g2, g_in), (delta, d2, d_in), (new_m, nm2, m_in), (new_v, nv2, v_in)):
        dst.update(_unshard_2d(part))
        dst["hy_w_in"] = jnp.transpose(t, (1, 2, 0))

    return (loss, dx0[None], *[grads[n] for n in names], *[delta[n] for n in names],
            *[new_m[n] for n in names], *[new_v[n] for n in names])
```

```python
import functools
import math

import numpy as np
import jax
import jax.numpy as jnp
from jax import lax
from jax.experimental import pallas as pl
from jax.experimental.pallas import tpu as pltpu

f32 = jnp.float32
bf16 = jnp.bfloat16
EPS = 1e-6
N_DEV = 8
LANES = 128
SSM_STATE = 128
SSM_CHUNK = 128
SSM_GROUPS = 4
HEAD_DIM = 64
ATT_BLK = 128
ATT_DILATIONS = (1, 4, 16)
REL_BUCKETS = 32
REL_MAX_DIST = 2048
ADAM_LR, ADAM_B1, ADAM_B2, ADAM_EPS, ADAM_WD, ADAM_STEP = 0.001, 0.9, 0.999, 1e-08, 0.01, 10
PACK_ROW_TILE = 256
MESH = pl.DeviceIdType.MESH
VMEM_LIMIT = 48 * 1024 * 1024


def _sds(shape, dtype=f32):
    return jax.ShapeDtypeStruct(tuple(shape), dtype)


def _tile(n, cap, mult):
    best = None
    t = mult
    while t <= min(n, cap):
        if n % t == 0:
            best = t
        t += mult
    return best if best is not None else n


def _params(sem):
    return pltpu.CompilerParams(dimension_semantics=sem, vmem_limit_bytes=VMEM_LIMIT)


def _mm(a, b, *, name, ta=False, tb=False, b_rows=None, bias=None, add=None, out_dtype=f32,
        tm_cap=512, tn_cap=1536, tk_cap=8192):
    if ta:
        K, M = a.shape
    else:
        M, K = a.shape
    off, cnt = b_rows if b_rows is not None else (0, b.shape[0])
    if tb:
        N, K2 = cnt, b.shape[1]
    else:
        K2, N = cnt, b.shape[1]
    assert K == K2, (a.shape, b.shape, ta, tb, b_rows)
    if ta and a.dtype == f32:
        tm_cap = min(tm_cap, 256)
    tm = _tile(M, tm_cap, LANES)
    tn = _tile(math.gcd(off, N) if tb else N, tn_cap, LANES)
    tk = _tile(K if tb else math.gcd(off, K), tk_cap, LANES)
    assert N % tn == 0 and K % tk == 0 and off % (tn if tb else tk) == 0, (name, off, N, K, tn, tk)
    nk = K // tk
    jo, ko = (off // tn, 0) if tb else (0, off // tk)
    has_bias, has_add = bias is not None, add is not None
    dn = (((0 if ta else 1,), (1 if tb else 0,)), ((), ()))

    def body(*refs):
        a_ref, b_ref = refs[0], refs[1]
        pos = 2
        bias_ref = add_ref = None
        if has_bias:
            bias_ref = refs[pos]
            pos += 1
        if has_add:
            add_ref = refs[pos]
            pos += 1
        o_ref = refs[pos]
        k = pl.program_id(2)
        part = lax.dot_general(a_ref[...].astype(bf16), b_ref[...].astype(bf16), dn, preferred_element_type=f32)

        def finish(r):
            if has_bias:
                r = r + bias_ref[...]
            if has_add:
                r = r + add_ref[...]
            o_ref[...] = r.astype(o_ref.dtype)

        if nk == 1:
            finish(part)
        else:
            acc_ref = refs[pos + 1]

            @pl.when(k == 0)
            def _():
                acc_ref[...] = part

            @pl.when((k > 0) & (k < nk - 1))
            def _():
                acc_ref[...] += part

            @pl.when(k == nk - 1)
            def _():
                finish(acc_ref[...] + part)

    in_specs = [
        pl.BlockSpec((tk, tm), lambda i, j, k: (k, i)) if ta else pl.BlockSpec((tm, tk), lambda i, j, k: (i, k)),
        pl.BlockSpec((tn, tk), lambda i, j, k: (j + jo, k)) if tb else pl.BlockSpec((tk, tn), lambda i, j, k: (k + ko, j)),
    ]
    args = [a, b]
    if has_bias:
        in_specs.append(pl.BlockSpec((1, tn), lambda i, j, k: (0, j)))
        args.append(bias)
    if has_add:
        in_specs.append(pl.BlockSpec((tm, tn), lambda i, j, k: (i, j)))
        args.append(add)
    return pl.pallas_call(
        body, name=name, grid=(M // tm, N // tn, nk), in_specs=in_specs,
        out_specs=pl.BlockSpec((tm, tn), lambda i, j, k: (i, j)), out_shape=_sds((M, N), out_dtype),
        scratch_shapes=[pltpu.VMEM((tm, tn), f32)] if nk > 1 else [],
        compiler_params=_params(("parallel", "parallel", "arbitrary")),
    )(*args)


def _rowwise(name, fn, rows, vecs, out_rows, out_accs, *, tr_cap=256, sub=8, col_chunk=None):
    rows = [r if isinstance(r, tuple) else (r, 0, r.shape[1]) for r in rows]
    R = rows[0][0].shape[0]
    tr = _tile(R, tr_cap, 8)
    sub = sub if tr % sub == 0 else tr
    n_r, n_v, n_or, n_oa = len(rows), len(vecs), len(out_rows), len(out_accs)

    def body(*refs):
        row_refs = refs[:n_r]
        vec_refs = refs[n_r:n_r + n_v]
        orow_refs = refs[n_r + n_v:n_r + n_v + n_or]
        oacc_refs = refs[n_r + n_v + n_or:]
        vv = [r[...] for r in vec_refs]

        n_sub = tr // sub
        together = 4 if n_sub % 4 == 0 else 1

        def step(s, accs):
            for t in range(together):
                sl = pl.ds(pl.multiple_of((s * together + t) * sub, sub), sub)
                if col_chunk is None:
                    ro, ao = fn([r[sl, :] for r in row_refs], vv)
                    for o_ref, o in zip(orow_refs, ro):
                        o_ref[sl, :] = o.astype(o_ref.dtype)
                    accs = tuple(x + y for x, y in zip(accs, ao))
                else:
                    for c0 in range(0, rows[0][2], col_chunk):
                        cs_ = pl.ds(c0, col_chunk)
                        ro, _ = fn([r[sl, cs_] for r in row_refs], vv)
                        for o_ref, o in zip(orow_refs, ro):
                            o_ref[sl, cs_] = o.astype(o_ref.dtype)
            return accs

        accs = lax.fori_loop(0, n_sub // together, step, tuple(jnp.zeros((1, w), f32) for w in out_accs))
        if n_oa:
            @pl.when(pl.program_id(0) == 0)
            def _():
                for ref in oacc_refs:
                    ref[...] = jnp.zeros_like(ref)

            for ref, x in zip(oacc_refs, accs):
                ref[...] += x

    in_specs = [pl.BlockSpec((tr, w), functools.partial(lambda i, cb: (i, cb), cb=cb)) for (_, cb, w) in rows]
    in_specs += [pl.BlockSpec((1, v.shape[1]), lambda i: (0, 0)) for v in vecs]
    out_specs = [pl.BlockSpec((tr, w), lambda i: (i, 0)) for (w, _) in out_rows]
    out_specs += [pl.BlockSpec((1, w), lambda i: (0, 0)) for w in out_accs]
    out_shape = [_sds((R, w), dt) for (w, dt) in out_rows] + [_sds((1, w)) for w in out_accs]
    res = pl.pallas_call(
        body, name=name, grid=(R // tr,), in_specs=in_specs, out_specs=out_specs, out_shape=out_shape,
        compiler_params=_params(("arbitrary",)),
    )(*[r[0] for r in rows], *vecs)
    return res[:n_or], res[n_or:]


def _silu(x):
    return x * jax.nn.sigmoid(x)


def _rms(x, g):
    return x * lax.rsqrt(jnp.mean(x * x, -1, keepdims=True) + EPS) * g


def _adaln_f(x, g, sc, sh):
    return _rms(x, g) * (1.0 + sc) + sh


def _gate_f(y, z, g):
    return _rms(y * _silu(z), g)


def _lnsilu_f(u, g, b):
    mu = jnp.mean(u, -1, keepdims=True)
    var = jnp.mean(jnp.square(u - mu), -1, keepdims=True)
    return _silu((u - mu) * lax.rsqrt(var + EPS) * g + b)


def _adaln_fwd(x, g, sc, sh, name):
    (h,), _ = _rowwise(name, lambda rv, vv: ([_adaln_f(rv[0], *vv)], []), [x], [g, sc, sh], [(x.shape[1], bf16)], [],
                       sub=16)
    return h


def _adaln_bwd(x, g, sc, sh, dh, dres, name):
    def fn(rv, vv):
        xv, dhv, drv = rv
        _, vjp = jax.vjp(_adaln_f, xv, *vv)
        dx, dg, dsc, dsh = vjp(dhv)
        return [dx + drv], [dg, dsc, dsh]
    w = x.shape[1]
    (dx,), accs = _rowwise(name, fn, [x, dh, dres], [g, sc, sh], [(w, f32)], [w, w, w])
    return dx, accs


def _resid_adaln_fwd(x, gate, mix, g, sc, sh, name):
    def fn(rv, vv):
        xn = rv[0] + vv[0] * rv[1]
        return [xn, _adaln_f(xn, vv[1], vv[2], vv[3])], []
    w = x.shape[1]
    (xn, h), _ = _rowwise(name, fn, [x, mix], [gate, g, sc, sh], [(w, f32), (w, bf16)], [], sub=16)
    return xn, h


def _adaln_resid_bwd(x, g, sc, sh, dh, dres, mix, gate, name):
    def fn(rv, vv):
        xv, dhv, drv, mv = rv
        _, vjp = jax.vjp(_adaln_f, xv, vv[0], vv[1], vv[2])
        dx, dg, dsc, dsh = vjp(dhv)
        dx = dx + drv
        dm = vv[3] * dx
        return [dx, dm], [dg, dsc, dsh, jnp.sum(dx * mv, 0, keepdims=True), jnp.sum(dm, 0, keepdims=True)]
    w = x.shape[1]
    (dx, dmix), accs = _rowwise(name, fn, [x, dh, dres, mix], [g, sc, sh, gate], [(w, f32), (w, bf16)], [w] * 5, sub=16)
    return dx, dmix, accs


def _add3(a, b, c, name):
    (y,), _ = _rowwise(name, lambda rv, vv: ([rv[0] + rv[1] + rv[2]], []), [a, b, c], [], [(a.shape[1], bf16)], [],
                       sub=16)
    return y


CONV_HALO = 32
CONV_ROWS = 64


def _conv_fwd(x, w, b, *, silu, name, tr=512):
    S, C = x.shape
    K = w.shape[0]
    H = CONV_HALO
    assert K - 1 <= H and S % tr == 0 and tr % H == 0 and C % LANES == 0
    nh = tr // H

    def body(xp_ref, xc_ref, w_ref, b_ref, *rest):
        outs, scr = rest[:-1], rest[-1]
        i = pl.program_id(1)
        scr[pl.ds(0, H), :] = jnp.where(i > 0, xp_ref[...], 0.0)
        scr[pl.ds(H, tr), :] = xc_ref[...]
        taps = [w_ref[pl.ds(k, 1), :] for k in range(K)]
        for c0 in range(0, tr, CONV_ROWS):
            acc = jnp.zeros((CONV_ROWS, LANES), f32) + b_ref[...]
            for k in range(K):
                acc = acc + scr[pl.ds(c0 + H - (K - 1) + k, CONV_ROWS), :] * taps[k]
            outs[0][pl.ds(c0, CONV_ROWS), :] = acc.astype(outs[0].dtype)
            if silu:
                outs[1][pl.ds(c0, CONV_ROWS), :] = _silu(acc)

    n_out = 2 if silu else 1
    return pl.pallas_call(
        body, name=name, grid=(C // LANES, S // tr),
        in_specs=[pl.BlockSpec((H, LANES), lambda j, i: (jnp.maximum(i * nh - 1, 0), j)),
                  pl.BlockSpec((tr, LANES), lambda j, i: (i, j)),
                  pl.BlockSpec((K, LANES), lambda j, i: (0, j)),
                  pl.BlockSpec((1, LANES), lambda j, i: (0, j))],
        out_specs=[pl.BlockSpec((tr, LANES), lambda j, i: (i, j))] * n_out,
        out_shape=[_sds((S, C), bf16), _sds((S, C))] if silu else [_sds((S, C))],
        scratch_shapes=[pltpu.VMEM((tr + H, LANES), f32)],
        compiler_params=_params(("parallel", "arbitrary")),
    )(x, x, w, b)


def _conv_bwd(x, w, dact, pre, *, silu, name, dx_dtype=f32, tr=512):
    S, C = x.shape
    K = w.shape[0]
    H = CONV_HALO
    nh = tr // H
    n_i = S // tr
    kp = -(-K // 8) * 8

    def dsilu(p):
        s = jax.nn.sigmoid(p)
        return s * (1.0 + p * (1.0 - s))

    def body(*refs):
        if silu:
            xp_ref, xc_ref, w_ref, dc_ref, dn_ref, pc_ref, pn_ref, dx_ref, dw_ref, db_ref, xs, ds = refs
        else:
            xp_ref, xc_ref, w_ref, dc_ref, dn_ref, dx_ref, dw_ref, db_ref, xs, ds = refs
        i = pl.program_id(1)
        xs[pl.ds(0, H), :] = jnp.where(i > 0, xp_ref[...], 0.0)
        xs[pl.ds(H, tr), :] = xc_ref[...]
        dcur = dc_ref[...]
        dnext = dn_ref[...]
        if silu:
            dcur = dcur * dsilu(pc_ref[...].astype(f32))
            dnext = dnext * dsilu(pn_ref[...].astype(f32))
        ds[pl.ds(0, tr), :] = dcur
        ds[pl.ds(tr, H), :] = jnp.where(i < n_i - 1, dnext, 0.0)
        taps = [w_ref[pl.ds(k, 1), :] for k in range(K)]
        fold = lambda t: jnp.sum(t.reshape(CONV_ROWS // 8, 8, LANES), axis=0)
        dw_parts = [jnp.zeros((8, LANES), f32) for _ in range(K)]
        db_part = jnp.zeros((8, LANES), f32)
        for c0 in range(0, tr, CONV_ROWS):
            acc = jnp.zeros((CONV_ROWS, LANES), f32)
            d_c = ds[pl.ds(c0, CONV_ROWS), :]
            for k in range(K):
                acc = acc + ds[pl.ds(c0 + K - 1 - k, CONV_ROWS), :] * taps[k]
                dw_parts[k] = dw_parts[k] + fold(d_c * xs[pl.ds(c0 + H - (K - 1) + k, CONV_ROWS), :])
            db_part = db_part + fold(d_c)
            dx_ref[pl.ds(c0, CONV_ROWS), :] = acc.astype(dx_ref.dtype)

        @pl.when(i == 0)
        def _():
            dw_ref[...] = jnp.zeros_like(dw_ref)
            db_ref[...] = jnp.zeros_like(db_ref)

        for k in range(K):
            dw_ref[pl.ds(k, 1), :] += jnp.sum(dw_parts[k], 0, keepdims=True)
        db_ref[...] += jnp.sum(db_part, 0, keepdims=True)

    prev = pl.BlockSpec((H, LANES), lambda j, i: (jnp.maximum(i * nh - 1, 0), j))
    cur = pl.BlockSpec((tr, LANES), lambda j, i: (i, j))
    nxt = pl.BlockSpec((H, LANES), lambda j, i: (jnp.minimum((i + 1) * nh, n_i * nh - 1), j))
    in_specs = [prev, cur, pl.BlockSpec((K, LANES), lambda j, i: (0, j)), cur, nxt]
    args = [x, x, w, dact, dact]
    if silu:
        in_specs += [cur, nxt]
        args += [pre, pre]
    dx, dw, db = pl.pallas_call(
        body, name=name, grid=(C // LANES, n_i), in_specs=in_specs,
        out_specs=[cur, pl.BlockSpec((kp, LANES), lambda j, i: (0, j)), pl.BlockSpec((1, LANES), lambda j, i: (0, j))],
        out_shape=[_sds((S, C), dx_dtype), _sds((kp, C)), _sds((1, C))],
        scratch_shapes=[pltpu.VMEM((tr + H, LANES), f32), pltpu.VMEM((tr + H, LANES), f32)],
        compiler_params=_params(("parallel", "arbitrary")),
    )(*args)
    return dx, dw[:K], db


def _dot(a, b):
    return jnp.dot(a.astype(bf16), b.astype(bf16), preferred_element_type=f32)


def _dot_nt(a, b):
    return lax.dot_general(a.astype(bf16), b.astype(bf16), (((1,), (1,)), ((), ())), preferred_element_type=f32)


def _dot_tn(a, b):
    return lax.dot_general(a.astype(bf16), b.astype(bf16), (((0,), (0,)), ((), ())), preferred_element_type=f32)


def _softplus(x):
    return jnp.maximum(x, 0.0) + jnp.log(1.0 + jnp.exp(-jnp.abs(x)))


def _tri(q):
    i = lax.broadcasted_iota(jnp.int32, (q, q), 0)
    j = lax.broadcasted_iota(jnp.int32, (q, q), 1)
    return i >= j


def _ssd_prep(dtraw, dt_bias, a_log):
    q = dtraw.shape[0]
    dt = _softplus(dtraw + dt_bias)
    A = -jnp.exp(a_log)
    tri = _tri(q)
    cs = jnp.dot(tri.astype(f32), dt * A, preferred_element_type=f32, precision=lax.Precision.HIGHEST)
    return dt, A, cs, cs.T, tri


def _expand(cols, h0, n, width):
    q = cols.shape[0]
    return jnp.concatenate([jnp.broadcast_to(cols[:, h0 + r:h0 + r + 1], (q, width)) for r in range(n)], axis=1)


def _ssd_fwd(xbc, dtraw, dt_bias, a_log, d_skip, di, name):
    S, CD = xbc.shape
    Q, N, G = SSM_CHUNK, SSM_STATE, SSM_GROUPS
    nc = S // Q
    nh = di // HEAD_DIM
    R = nh // G
    gw = R * HEAD_DIM
    col_of_head = jnp.asarray((np.arange(LANES)[:, None] == np.arange(di)[None, :] // HEAD_DIM).astype(np.float32))
    dsk_wide = jnp.repeat(d_skip[0, :nh], HEAD_DIM)[None]

    def body(xbc_ref, dt_ref, bias_ref, alog_ref, dskw_ref, coh_ref, y_ref, hin_ref, state):
        c = pl.program_id(0)

        @pl.when(c == 0)
        def _():
            state[...] = jnp.zeros_like(state)

        hin_ref[...] = state[...]
        dt, A, cs, csT, tri = _ssd_prep(dt_ref[...], bias_ref[...], alog_ref[...])
        elast = jnp.exp(cs[Q - 1:Q, :])
        coh = coh_ref[...]
        dt_w, ecs_w, dend_w = _dot_exact(dt, coh), _dot_exact(jnp.exp(cs), coh), _dot_exact(jnp.exp(cs[Q - 1:Q, :] - cs), coh)
        for g in range(G):
            h0 = g * R
            cols = pl.ds(g * gw, gw)
            lanes = slice(g * gw, (g + 1) * gw)
            Bg = xbc_ref[:, pl.ds(di + g * N, N)]
            Cg = xbc_ref[:, pl.ds(di + G * N + g * N, N)]
            xg = xbc_ref[:, cols]
            Hg = state[cols, :]
            Gm = _dot_nt(Cg, Bg)
            xdt = xg * dt_w[:, lanes]
            yoff = _dot_nt(Cg, Hg) * ecs_w[:, lanes]
            ys = []
            for r in range(R):
                h = h0 + r
                L = jnp.exp(jnp.where(tri, cs[:, h:h + 1] - csT[h:h + 1, :], -jnp.inf))
                ys.append(_dot(Gm * L, xdt[:, r * HEAD_DIM:(r + 1) * HEAD_DIM]))
            y_ref[:, cols] = jnp.concatenate(ys, axis=1) + yoff + xg * dskw_ref[:, cols]
            hnew = _dot_tn(xdt * dend_w[:, lanes], Bg)
            escale = jnp.concatenate([jnp.broadcast_to(elast[:, h0 + r:h0 + r + 1], (HEAD_DIM, N)) for r in range(R)], axis=0)
            state[cols, :] = escale * Hg + hnew

    vec = pl.BlockSpec((1, LANES), lambda c: (0, 0))
    return pl.pallas_call(
        body, name=name, grid=(nc,),
        in_specs=[pl.BlockSpec((Q, CD), lambda c: (c, 0)), pl.BlockSpec((Q, LANES), lambda c: (c, 0)), vec, vec,
                  pl.BlockSpec((1, di), lambda c: (0, 0)), pl.BlockSpec((LANES, di), lambda c: (0, 0))],
        out_specs=[pl.BlockSpec((Q, di), lambda c: (c, 0)), pl.BlockSpec((None, di, N), lambda c: (c, 0, 0))],
        out_shape=[_sds((S, di)), _sds((nc, di, N))],
        scratch_shapes=[pltpu.VMEM((di, N), f32)],
        compiler_params=_params(("arbitrary",)),
    )(xbc, dtraw, dt_bias, a_log, dsk_wide, col_of_head)


def _dot_exact(a, b):
    bb = b.astype(bf16)
    hi = a.astype(bf16)
    rest = a - hi.astype(f32)
    mid = rest.astype(bf16)
    low = (rest - mid.astype(f32)).astype(bf16)
    one_pass = lambda t: jnp.dot(t, bb, preferred_element_type=f32)
    return one_pass(hi) + one_pass(mid) + one_pass(low)


def _ssd_bwd(xbc, dtraw, dt_bias, a_log, d_skip, hin, y, dy, di, name):
    S, CD = xbc.shape
    Q, N, G = SSM_CHUNK, SSM_STATE, SSM_GROUPS
    nc = S // Q
    nh = di // HEAD_DIM
    R = nh // G
    gw = R * HEAD_DIM
    P = HEAD_DIM
    head_of_col = jnp.asarray((np.arange(di)[:, None] // P == np.arange(LANES)[None, :]).astype(np.float32))
    dsk_wide = jnp.repeat(d_skip[0, :nh], P)[None]

    def body(xbc_ref, dt_ref, bias_ref, alog_ref, dskw_ref, hoc_ref, hin_ref, y_ref, dy_ref,
             dxbc_ref, ddt_ref, dA_ref, ddsk_ref, dtb_ref, dstate, dxdt_all, tend_all, yoff_all, colterm_all):
        c = pl.program_id(0)

        @pl.when(c == 0)
        def _():
            dstate[...] = jnp.zeros_like(dstate)
            dA_ref[...] = jnp.zeros_like(dA_ref)
            ddsk_ref[...] = jnp.zeros_like(ddsk_ref)
            dtb_ref[...] = jnp.zeros_like(dtb_ref)

        dtraw_v = dt_ref[...]
        dt, A, cs, csT, tri = _ssd_prep(dtraw_v, bias_ref[...], alog_ref[...])
        tri_t = jnp.logical_not(tri) | (lax.broadcasted_iota(jnp.int32, (Q, Q), 0) == lax.broadcasted_iota(jnp.int32, (Q, Q), 1))
        ecs = jnp.exp(cs)
        dend = jnp.exp(cs[Q - 1:Q, :] - cs)
        elast = jnp.exp(cs[Q - 1:Q, :])
        hoc = hoc_ref[...]
        state_dot = jnp.sum(_dot_exact(dstate[...] * hin_ref[...], jnp.ones((N, LANES), f32)) * hoc, 0, keepdims=True) * elast
        for g in range(G):
            h0 = g * R
            Bg = xbc_ref[:, pl.ds(di + g * N, N)]
            Cg = xbc_ref[:, pl.ds(di + G * N + g * N, N)]
            xg = xbc_ref[:, pl.ds(g * gw, gw)]
            dyg = dy_ref[:, pl.ds(g * gw, gw)]
            Hg = hin_ref[pl.ds(g * gw, gw), :]
            dHg = dstate[pl.ds(g * gw, gw), :]
            dt_e = _expand(dt, h0, R, P)
            ecs_e = _expand(ecs, h0, R, P)
            dend_e = _expand(dend, h0, R, P)
            cols = pl.ds(g * gw, gw)
            Gm = _dot_nt(Cg, Bg)
            Gm_t = _dot_nt(Bg, Cg)
            xdt = xg * dt_e
            dye = dyg * ecs_e
            bdh = _dot_nt(Bg, dHg)
            dC = _dot(dye, Hg)
            dB = _dot(xdt * dend_e, dHg)
            dHin = _dot_tn(dye, Cg)
            dxdt_state = dend_e * bdh
            end_term = xdt * dxdt_state
            tend_all[:, cols] = end_term
            yoff_all[:, cols] = _dot_nt(Cg, Hg) * ecs_e
            dG = jnp.zeros((Q, Q), f32)
            dxd = []
            for r in range(R):
                h = h0 + r
                sl = slice(r * P, (r + 1) * P)
                seg = cs[:, h:h + 1] - csT[h:h + 1, :]
                L = jnp.exp(jnp.where(tri, seg, -jnp.inf))
                L_t = jnp.exp(jnp.where(tri_t, -seg, -jnp.inf))
                dyh = dyg[:, sl]
                dG = dG + _dot_nt(dyh, xdt[:, sl]) * L
                dxd.append(_dot(Gm_t * L_t, dyh))
            dxdt_diag = jnp.concatenate(dxd, axis=1)
            dxdt = dxdt_diag + dxdt_state
            dxdt_all[:, cols] = dxdt
            colterm_all[:, cols] = xdt.astype(bf16).astype(f32) * dxdt_diag + end_term
            dxbc_ref[:, cols] = dxdt * dt_e + dyg * dskw_ref[:, cols]
            dxbc_ref[:, pl.ds(di + g * N, N)] = dB + _dot_tn(dG, Cg)
            dxbc_ref[:, pl.ds(di + G * N + g * N, N)] = dC + _dot(dG, Bg)
            escale = jnp.concatenate([jnp.broadcast_to(elast[:, h0 + r:h0 + r + 1], (P, N)) for r in range(R)], axis=0)
            dstate[pl.ds(g * gw, gw), :] = escale * dHg + dHin
        xs = xbc_ref[:, pl.ds(0, di)]
        dyv = dy_ref[...]
        yoff = yoff_all[...]
        y_diag = y_ref[...] - dskw_ref[...] * xs - yoff
        rs_y = _dot_exact(dyv.astype(bf16).astype(f32) * y_diag + dyv * yoff, hoc)
        rs_c = _dot_exact(colterm_all[...], hoc)
        rs_x = _dot_exact(dxdt_all[...] * xs, hoc)
        end_dot = _dot_exact(jnp.broadcast_to(jnp.sum(tend_all[...], 0, keepdims=True), (8, di)), hoc)[0:1]
        last = lax.broadcasted_iota(jnp.int32, (Q, 1), 0) == Q - 1
        dcs = rs_y - rs_c + jnp.where(last, end_dot + state_dot, 0.0)
        da = lax.dot_general(tri.astype(f32), dcs, (((0,), (0,)), ((), ())), preferred_element_type=f32,
                             precision=lax.Precision.HIGHEST)
        ddt = da * A + rs_x
        ddtraw = ddt * jax.nn.sigmoid(dtraw_v + bias_ref[...])
        ddt_ref[...] = ddtraw.astype(ddt_ref.dtype)
        dA_ref[...] += jnp.sum(da * dt, 0, keepdims=True) * A
        ddsk_ref[...] += jnp.sum(_dot_exact(dyv * xs, hoc), 0, keepdims=True)
        dtb_ref[...] += jnp.sum(ddtraw, 0, keepdims=True)

    vec = pl.BlockSpec((1, LANES), lambda c: (0, 0))
    rev = lambda c: (nc - 1 - c, 0)
    return pl.pallas_call(
        body, name=name, grid=(nc,),
        in_specs=[pl.BlockSpec((Q, CD), rev), pl.BlockSpec((Q, LANES), rev), vec, vec,
                  pl.BlockSpec((1, di), lambda c: (0, 0)), pl.BlockSpec((di, LANES), lambda c: (0, 0)),
                  pl.BlockSpec((None, di, N), lambda c: (nc - 1 - c, 0, 0)), pl.BlockSpec((Q, di), rev),
                  pl.BlockSpec((Q, di), rev)],
        out_specs=[pl.BlockSpec((Q, CD), rev), pl.BlockSpec((Q, LANES), rev), vec, vec, vec],
        out_shape=[_sds((S, CD)), _sds((S, LANES), bf16), _sds((1, LANES)), _sds((1, LANES)), _sds((1, LANES))],
        scratch_shapes=[pltpu.VMEM((di, N), f32)] + [pltpu.VMEM((Q, di), f32)] * 4,
        compiler_params=_params(("arbitrary",)),
    )(xbc, dtraw, dt_bias, a_log, dsk_wide, head_of_col, hin, y, dy)


def _t5_bucket_np(dist):
    max_exact = REL_BUCKETS // 2
    n = np.maximum(dist, 1).astype(np.float32)
    large = np.float32(max_exact) + np.log(n / np.float32(max_exact)) / np.float32(math.log(REL_MAX_DIST / max_exact)) * np.float32(REL_BUCKETS - max_exact)
    large = np.minimum(large.astype(np.int32), REL_BUCKETS - 1)
    return np.where(dist < max_exact, dist, large)


def _bucket_onehot():
    i = np.arange(ATT_BLK)[None, :]
    j = np.arange(2 * ATT_BLK)[:, None]
    delta = np.maximum(ATT_BLK + i - j, 0)
    out = np.zeros((len(ATT_DILATIONS), REL_BUCKETS, ATT_BLK * 2 * ATT_BLK), np.float32)
    for gi, d in enumerate(ATT_DILATIONS):
        b = _t5_bucket_np(delta * d).reshape(-1)
        out[gi, b, np.arange(b.size)] = 1.0
    return out


def _exact_mm(a, b, *, name, tb=False):
    M, K = a.shape
    N = b.shape[0] if tb else b.shape[1]
    tn = _tile(N, 4096, LANES)
    dn = (((1,), (1 if tb else 0,)), ((), ()))

    def body(a_ref, b_ref, o_ref):
        o_ref[...] = lax.dot_general(a_ref[...], b_ref[...], dn, preferred_element_type=f32,
                                     precision=lax.Precision.HIGHEST)

    return pl.pallas_call(
        body, name=name, grid=(N // tn,),
        in_specs=[pl.BlockSpec((M, K), lambda j: (0, 0)),
                  pl.BlockSpec((tn, K), lambda j: (j, 0)) if tb else pl.BlockSpec((K, tn), lambda j: (0, j))],
        out_specs=pl.BlockSpec((M, tn), lambda j: (0, j)), out_shape=_sds((M, N)),
        compiler_params=_params(("parallel",)),
    )(a, b)


def _band_penalty():
    i = np.arange(ATT_BLK)[None, :]
    j = np.arange(2 * ATT_BLK)[:, None]
    delta = ATT_BLK + i - j
    return np.where((delta >= 0) & (delta <= ATT_BLK), 0.0, -np.inf).astype(np.float32)


def _first_block_keep(n):
    key = lax.broadcasted_iota(jnp.int32, (2 * ATT_BLK, ATT_BLK), 0)
    return (key >= ATT_BLK) | (n > 0)


ATT_SCALE = HEAD_DIM ** -0.5


def _rows(ref, r, d):
    return ref[...] if d == 1 else ref[pl.ds(r, ATT_BLK, stride=d), :]


def _set_rows(ref, r, d, val):
    if d == 1:
        ref[...] = val
    else:
        ref[pl.ds(r, ATT_BLK, stride=d), :] = val


def _attn_width(d, D):
    return D if d == 1 else LANES


def _over_residues(d, one, unroll=1):
    if d == 1:
        one(0)
    else:
        lax.fori_loop(0, d, lambda r, c: (one(r), c)[1], 0, unroll=unroll)


def _attn_fwd(q, k, v, bias_t, d, name):
    S, D = q.shape
    nb = S // (d * ATT_BLK)
    H = D // HEAD_DIM
    W = _attn_width(d, D)
    HB = W // HEAD_DIM

    def body(q_ref, kp_ref, kc_ref, vp_ref, vc_ref, b_ref, o_ref, lse_ref):
        keep = _first_block_keep(pl.program_id(1))
        first = lax.broadcasted_iota(jnp.int32, (1, LANES), 1) < HEAD_DIM

        def one(r):
            qs = (_rows(q_ref, r, d) * ATT_SCALE).astype(bf16)
            kcat = jnp.concatenate([_rows(kp_ref, r, d), _rows(kc_ref, r, d)], axis=0).astype(bf16)
            vcat = jnp.concatenate([_rows(vp_ref, r, d), _rows(vc_ref, r, d)], axis=0).astype(bf16)
            outs = []
            for pair in range(W // LANES):
                ps = slice(pair * LANES, (pair + 1) * LANES)
                q2, k2, v2 = qs[:, ps], kcat[:, ps], vcat[:, ps]
                o2 = jnp.zeros((ATT_BLK, LANES), f32)
                for e in range(2):
                    h = 2 * pair + e
                    mine = first if e == 0 else jnp.logical_not(first)
                    zero = jnp.zeros((), bf16)
                    st = jnp.where(keep, _dot_nt(k2, jnp.where(mine, q2, zero)) + b_ref[h], -jnp.inf)
                    m = jnp.max(st, 0, keepdims=True)
                    pt = jnp.exp(st - m)
                    l = jnp.sum(pt, 0, keepdims=True)
                    o2 = o2 + _dot_tn(pt * (1.0 / l), jnp.where(mine, v2, zero))
                    lse_ref[r, h] = m + jnp.log(l)
                outs.append(o2)
            _set_rows(o_ref, r, d, jnp.concatenate(outs, axis=1))

        _over_residues(d, one, unroll=4)

    cur = pl.BlockSpec((ATT_BLK * d, W), lambda j, n: (n, j))
    prev = pl.BlockSpec((ATT_BLK * d, W), lambda j, n: (jnp.maximum(n - 1, 0), j))
    return pl.pallas_call(
        body, name=name, grid=(D // W, nb),
        in_specs=[cur, prev, cur, prev, cur, pl.BlockSpec((HB, 2 * ATT_BLK, ATT_BLK), lambda j, n: (j, 0, 0))],
        out_specs=[cur, pl.BlockSpec((None, d, HB, 1, LANES), lambda j, n: (n, 0, j, 0, 0))],
        out_shape=[_sds((S, D)), _sds((nb, d, H, 1, LANES))],
        compiler_params=_params(("parallel", "arbitrary")),
    )(q, k, k, v, v, bias_t)


def _from_blocks(rows, lanes=None):
    nb, d, H = rows.shape[:3]
    a = jnp.transpose(rows[:, :, :, 0, :], (0, 3, 1, 2)).reshape(nb * ATT_BLK * d, H)
    return a if lanes is None else jnp.pad(a, ((0, 0), (0, lanes - H)))


def _by_block(a, d):
    S, H = a.shape
    t = jnp.transpose(a.reshape(S // (d * ATT_BLK), ATT_BLK, d, H), (0, 2, 3, 1))
    return t[:, :, :, None, :]


def _head_sums(a, b, name):
    S, D = a.shape
    tr = _tile(S, 512, 8)
    hoc = jnp.asarray((np.arange(D)[:, None] // HEAD_DIM == np.arange(LANES)[None, :]).astype(np.float32))

    def body(a_ref, b_ref, h_ref, o_ref):
        o_ref[...] = _dot_exact(a_ref[...] * b_ref[...], h_ref[...])

    return pl.pallas_call(
        body, name=name, grid=(S // tr,),
        in_specs=[pl.BlockSpec((tr, D), lambda i: (i, 0)), pl.BlockSpec((tr, D), lambda i: (i, 0)),
                  pl.BlockSpec((D, LANES), lambda i: (0, 0))],
        out_specs=pl.BlockSpec((tr, LANES), lambda i: (i, 0)), out_shape=_sds((S, LANES)),
        compiler_params=_params(("parallel",)),
    )(a, b, hoc)


def _attn_bwd(q, k, v, bias_t, datt, lse_rows, dsum_rows, d, name):
    S, D = q.shape
    nb = S // (d * ATT_BLK)
    H = D // HEAD_DIM
    W = _attn_width(d, D)
    HB = W // HEAD_DIM

    def body(q_ref, kp_ref, kc_ref, vp_ref, vc_ref, b_ref, do_ref, lse_ref, dsum_ref,
             dq_ref, dk_ref, dv_ref, db_ref, carry_k, carry_v):
        j = pl.program_id(0)
        n = pl.program_id(1)

        @pl.when(n == 0)
        def _():
            carry_k[...] = jnp.zeros_like(carry_k)
            carry_v[...] = jnp.zeros_like(carry_v)
            db_ref[...] = jnp.zeros_like(db_ref)

        @pl.when(n < nb)
        def _():
            key = lax.broadcasted_iota(jnp.int32, (2 * ATT_BLK, ATT_BLK), 0)
            keep = (key >= ATT_BLK) | (n > 0)
            first = lax.broadcasted_iota(jnp.int32, (1, LANES), 1) < HEAD_DIM

            def one(r):
                qs = (_rows(q_ref, r, d) * ATT_SCALE).astype(bf16)
                kcat = jnp.concatenate([_rows(kp_ref, r, d), _rows(kc_ref, r, d)], axis=0).astype(bf16)
                vcat = jnp.concatenate([_rows(vp_ref, r, d), _rows(vc_ref, r, d)], axis=0).astype(bf16)
                dob = _rows(do_ref, r, d).astype(bf16)
                dqs, dks, dvs = [], [], []
                for pair in range(W // LANES):
                    ps = slice(pair * LANES, (pair + 1) * LANES)
                    q2, k2, v2, do2 = qs[:, ps], kcat[:, ps], vcat[:, ps], dob[:, ps]
                    dq2 = jnp.zeros((ATT_BLK, LANES), f32)
                    dk2 = jnp.zeros((2 * ATT_BLK, LANES), f32)
                    dv2 = jnp.zeros((2 * ATT_BLK, LANES), f32)
                    for e in range(2):
                        h = 2 * pair + e
                        mine = first if e == 0 else jnp.logical_not(first)
                        zero = jnp.zeros((), bf16)
                        qm, dom, km = jnp.where(mine, q2, zero), jnp.where(mine, do2, zero), jnp.where(mine, k2, zero)
                        st = jnp.where(keep, _dot_nt(k2, qm) + b_ref[h], -jnp.inf)
                        pt = jnp.exp(st - lse_ref[r, j * HB + h])
                        dst = pt * (_dot_nt(v2, dom) - dsum_ref[r, j * HB + h])
                        db_ref[h] += dst
                        dv2 = dv2 + _dot(pt, dom)
                        dk2 = dk2 + _dot(dst, qm)
                        dq2 = dq2 + _dot_tn(dst, km)
                    dqs.append(dq2 * ATT_SCALE)
                    dks.append(dk2)
                    dvs.append(dv2)
                _set_rows(dq_ref, r, d, jnp.concatenate(dqs, axis=1))
                dk = jnp.concatenate(dks, axis=1)
                dv = jnp.concatenate(dvs, axis=1)
                _set_rows(dk_ref, r, d, carry_k[r] + dk[:ATT_BLK])
                _set_rows(dv_ref, r, d, carry_v[r] + dv[:ATT_BLK])
                carry_k[r] = dk[ATT_BLK:]
                carry_v[r] = dv[ATT_BLK:]

            _over_residues(d, one, unroll=2)

        @pl.when(n == nb)
        def _():
            def last(r):
                _set_rows(dk_ref, r, d, carry_k[r])
                _set_rows(dv_ref, r, d, carry_v[r])

            _over_residues(d, last)

    nq = lambda n: jnp.minimum(n, nb - 1)
    cur = pl.BlockSpec((ATT_BLK * d, W), lambda j, n: (nq(n), j))
    prev = pl.BlockSpec((ATT_BLK * d, W), lambda j, n: (jnp.maximum(nq(n) - 1, 0), j))
    done = pl.BlockSpec((ATT_BLK * d, W), lambda j, n: (jnp.maximum(n - 1, 0), j))
    bspec = pl.BlockSpec((HB, 2 * ATT_BLK, ATT_BLK), lambda j, n: (j, 0, 0))
    rows = pl.BlockSpec((None, d, H, 1, LANES), lambda j, n: (nq(n), 0, 0, 0, 0))
    return pl.pallas_call(
        body, name=name, grid=(D // W, nb + 1),
        in_specs=[cur, prev, cur, prev, cur, bspec, cur, rows, rows],
        out_specs=[cur, done, done, bspec],
        out_shape=[_sds((S, D)), _sds((S, D)), _sds((S, D)), _sds((H, 2 * ATT_BLK, ATT_BLK))],
        scratch_shapes=[pltpu.VMEM((d, ATT_BLK, W), f32), pltpu.VMEM((d, ATT_BLK, W), f32)],
        compiler_params=_params(("arbitrary", "arbitrary")),
    )(q, k, k, v, v, bias_t, datt, lse_rows, dsum_rows)


def _attn_combine(os_, lses, name):
    S, D = os_[0].shape
    tr = _tile(S, 128, 16)
    head_cols = jnp.asarray((np.arange(LANES)[:, None] == np.arange(D)[None, :] // HEAD_DIM).astype(np.float32))

    def body(o0, o1, o2, l0, l1, l2, hc_ref, att_ref, attb_ref, lse_ref):
        a, b, c = l0[...], l1[...], l2[...]
        m = jnp.maximum(jnp.maximum(a, b), c)
        e0, e1, e2 = jnp.exp(a - m), jnp.exp(b - m), jnp.exp(c - m)
        tot = e0 + e1 + e2
        wide = lambda w: _dot_exact(w / tot, hc_ref[...])
        att = wide(e0) * o0[...] + wide(e1) * o1[...] + wide(e2) * o2[...]
        att_ref[...] = att
        attb_ref[...] = att.astype(bf16)
        lse_ref[...] = m + jnp.log(tot)

    wide_spec = pl.BlockSpec((tr, D), lambda i: (i, 0))
    lane_spec = pl.BlockSpec((tr, LANES), lambda i: (i, 0))
    return pl.pallas_call(
        body, name=name, grid=(S // tr,),
        in_specs=[wide_spec] * 3 + [lane_spec] * 3 + [pl.BlockSpec((LANES, D), lambda i: (0, 0))],
        out_specs=[wide_spec, wide_spec, lane_spec], out_shape=[_sds((S, D)), _sds((S, D), bf16), _sds((S, LANES))],
        compiler_params=_params(("parallel",)),
    )(*os_, *lses, head_cols)


ANY = pl.BlockSpec(memory_space=pl.ANY)


def _all_gather(vs, name):
    n = len(vs)

    def body(*refs):
        x_refs, out_refs = refs[:n], refs[n:2 * n]
        send_sems, recv_sems, local_sems = refs[2 * n:]
        x, y, c = lax.axis_index("x"), lax.axis_index("y"), lax.axis_index("c")
        me, sibling = (x, y, c), (x, y, 1 - c)
        chips = [(1 - x, y), (x, 1 - y), (1 - x, 1 - y)]

        def slot(i, px, py, pc):
            return out_refs[i].at[4 * px + 2 * py + pc]

        def copy(i, k, block, to, src=None):
            return pltpu.make_async_remote_copy(
                src_ref=slot(i, *block) if src is None else src, dst_ref=slot(i, *block),
                send_sem=send_sems.at[i, k], recv_sem=recv_sems.at[i, k], device_id=to, device_id_type=MESH)

        mine = [pltpu.make_async_copy(x_refs[i], slot(i, *me), local_sems.at[i]) for i in range(n)]
        for cp in mine:
            cp.start()
        first = []
        for i in range(n):
            first.append(copy(i, 0, me, sibling, src=x_refs[i]))
            first += [copy(i, 1 + j, me, (*chip, c), src=x_refs[i]) for j, chip in enumerate(chips)]
        for cp in first:
            cp.start()
        passed = []
        for i in range(n):
            for j, chip in enumerate(chips):
                copy(i, 1 + j, (*chip, c), me).wait_recv()
                cp = copy(i, 4 + j, (*chip, c), sibling)
                cp.start()
                passed.append(cp)
        for i in range(n):
            copy(i, 0, sibling, me).wait_recv()
            for j, chip in enumerate(chips):
                copy(i, 4 + j, (*chip, 1 - c), me).wait_recv()
        for cp in first + passed:
            cp.wait_send()
        for cp in mine:
            cp.wait()

    return pl.pallas_call(
        body, name=name, out_shape=[_sds((N_DEV,) + v.shape, v.dtype) for v in vs], in_specs=[ANY] * n,
        out_specs=[ANY] * n,
        scratch_shapes=[pltpu.SemaphoreType.DMA((n, 7)), pltpu.SemaphoreType.DMA((n, 7)), pltpu.SemaphoreType.DMA((n,))],
    )(*vs)


def _sum_slots(t, name):
    n, R, C = t.shape
    tr = _tile(R, PACK_ROW_TILE, 16)

    def body(t_ref, o_ref):
        acc = t_ref[0].astype(f32)
        for k in range(1, n):
            acc = acc + t_ref[k].astype(f32)
        o_ref[...] = acc

    return pl.pallas_call(
        body, name=name, grid=(R // tr,),
        in_specs=[pl.BlockSpec((n, tr, C), lambda i: (0, i, 0))],
        out_specs=pl.BlockSpec((tr, C), lambda i: (i, 0)), out_shape=_sds((R, C)),
        compiler_params=_params(("parallel",)),
    )(t)


HBM_SPEC = pl.BlockSpec(memory_space=pltpu.HBM)
SEM_SPEC = pl.BlockSpec(memory_space=pltpu.SEMAPHORE)
EFFECT = pltpu.SideEffectType.DATAFLOW_SIDE_EFFECTING


def _mesh_pos(p):
    return (p // 4, (p // 2) % 2, p % 2)


def _exchange_copy(src_refs, land_refs, send_sems, recv_sems, whole, dests, i, k):
    me = 4 * lax.axis_index("x") + 2 * lax.axis_index("y") + lax.axis_index("c")
    to = (me + k) % N_DEV
    frm = (me + N_DEV - k) % N_DEV
    lo, hi = dests
    src = src_refs[i] if whole else src_refs[i].at[jnp.minimum(jnp.maximum(to - lo, 0), hi - lo - 1)]
    s = i * (N_DEV - 1) + k - 1
    send = pltpu.make_async_remote_copy(src_ref=src, dst_ref=land_refs[i].at[me], send_sem=send_sems.at[s],
                                        recv_sem=recv_sems.at[s], device_id=_mesh_pos(to), device_id_type=MESH)
    recv = pltpu.make_async_remote_copy(src_ref=src, dst_ref=land_refs[i].at[frm], send_sem=send_sems.at[s],
                                        recv_sem=recv_sems.at[s], device_id=_mesh_pos(to), device_id_type=MESH)
    return send, recv, (to >= lo) & (to < hi), (me >= lo) & (me < hi)


def _exchange_start(srcs, whole, name, after=None, dests=(0, N_DEV)):
    n = len(srcs)
    lands = [lax.empty((N_DEV,) + s.shape[-2:], s.dtype) for s in srcs]
    after = list(after or [])
    n_in = 2 * n + len(after)
    everyone = dests == (0, N_DEV)

    def body(*refs):
        src_refs, land_refs = refs[:n], refs[n:2 * n]
        send_sems, recv_sems, token = refs[n_in], refs[n_in + 1], refs[-1]
        for i in range(n):
            for k in range(1, N_DEV):
                send, _, sends, _ = _exchange_copy(src_refs, land_refs, send_sems, recv_sems, whole, dests, i, k)
                if everyone:
                    send.start()
                else:
                    pl.when(sends)(send.start)
        token[...] = jnp.zeros_like(token)

    sems = pltpu.SemaphoreType.DMA((n * (N_DEV - 1),))
    outs = pl.pallas_call(
        body, name=name,
        out_shape=(sems, sems, *[pltpu.HBM(a.shape, a.dtype) for a in srcs + lands], _sds((8, LANES))),
        in_specs=[HBM_SPEC] * (2 * n) + [pl.BlockSpec(memory_space=pl.ANY)] * len(after),
        out_specs=(SEM_SPEC, SEM_SPEC, *[HBM_SPEC] * (2 * n), pl.BlockSpec(memory_space=pltpu.VMEM)),
        input_output_aliases={i: 2 + i for i in range(2 * n)},
        compiler_params=pltpu.CompilerParams(has_side_effects=EFFECT),
    )(*[pltpu.with_memory_space_constraint(a, pltpu.HBM) for a in srcs + lands], *after)
    return (outs[0], outs[1], list(outs[2:2 + n]), list(outs[2 + n:2 + 2 * n]), whole, dests), outs[-1]


def _exchange_wait(handle, after, name):
    send_sems, recv_sems, srcs, lands, whole, dests = handle
    n = len(srcs)
    everyone = dests == (0, N_DEV)

    def body(*refs):
        src_refs, land_refs = refs[:n], refs[n:2 * n]
        send_sems, recv_sems = refs[2 * n], refs[2 * n + 1]
        for i in range(n):
            for k in range(1, N_DEV):
                send, recv, sends, receives = _exchange_copy(src_refs, land_refs, send_sems, recv_sems, whole, dests, i, k)
                if everyone:
                    send.wait_send()
                    recv.wait_recv()
                else:
                    pl.when(sends)(send.wait_send)
                    pl.when(receives)(recv.wait_recv)

    outs = pl.pallas_call(
        body, name=name, out_shape=tuple(pltpu.HBM(a.shape, a.dtype) for a in srcs + lands),
        in_specs=[HBM_SPEC] * (2 * n) + [SEM_SPEC, SEM_SPEC, pl.BlockSpec(memory_space=pl.ANY)],
        out_specs=[HBM_SPEC] * (2 * n), input_output_aliases={i: i for i in range(2 * n)},
        compiler_params=pltpu.CompilerParams(has_side_effects=EFFECT),
    )(*srcs, *lands, send_sems, recv_sems, after)
    return list(outs[n:])


def _tie(v, token):
    return v + token[0:1, 0:1].astype(v.dtype).reshape((1,) * v.ndim)


def _with_own(land, own, me):
    return lax.dynamic_update_slice_in_dim(land, own[None].astype(land.dtype), me, 0)


class _Overlap:
    def __init__(self, shards, me, after):
        self.me = me
        self.names = list(shards)
        self.handle, self.token = _exchange_start([shards[nm] for nm in self.names], True, "weights_start", after)
        self.sent = {}

    def weights(self, after):
        lands = _exchange_wait(self.handle, after, "weights_wait")
        own = self.handle[2]
        return {nm: _full_from_blocks(nm, _with_own(land, o, self.me)) for nm, land, o in zip(self.names, lands, own)}

    def send(self, tag, grads):
        names = list(grads)
        handle, token = _exchange_start([_blocks_from_full(nm, grads[nm]) for nm in names], False, f"grads_start_{tag}")
        self.sent[tag] = (names, handle)
        return token

    def send_rows(self, tag, rows, dests):
        lo, hi = dests
        blocks = rows.reshape(hi - lo, rows.shape[0] // (hi - lo), rows.shape[1])
        handle, token = _exchange_start([blocks], False, f"grads_start_{tag}", None, dests)
        self.sent[tag] = ([tag], handle)
        return token

    def received(self, tag, after):
        names, handle = self.sent[tag]
        lands = _exchange_wait(handle, after, f"grads_wait_{tag}")
        lo = handle[5][0]
        own = [lax.dynamic_index_in_dim(b, self.me - lo, 0, keepdims=False) for b in handle[2]]
        return {nm: _with_own(land, o, self.me) for nm, land, o in zip(names, lands, own)}


ADAM_ROWS = 32


def _adamw(w, g, m, v, name):
    deep = w.ndim == 3
    R, C = w.shape[0], w.shape[-1]
    cb = LANES if C % LANES == 0 else C
    n_parts = g.shape[0] if g.ndim == 3 else 0

    def body(w_ref, g_ref, m_ref, v_ref, d_ref, m2_ref, v2_ref, *g_out):
        at = (lambda ref, sl: ref.at[sl, 0, :]) if deep else (lambda ref, sl: ref.at[sl, :])

        def update(sl):
            if n_parts:
                gv = g_ref[0, sl, :].astype(f32)
                for k in range(1, n_parts):
                    gv = gv + g_ref[k, sl, :].astype(f32)
                at(g_out[0], sl)[...] = gv
            else:
                gv = g_ref[sl, :]
            m2 = ADAM_B1 * at(m_ref, sl)[...] + (1.0 - ADAM_B1) * gv
            v2 = ADAM_B2 * at(v_ref, sl)[...] + (1.0 - ADAM_B2) * jnp.square(gv)
            m_hat = m2 / (1.0 - ADAM_B1 ** ADAM_STEP)
            v_hat = v2 / (1.0 - ADAM_B2 ** ADAM_STEP)
            at(d_ref, sl)[...] = -ADAM_LR * (m_hat / (jnp.sqrt(v_hat) + ADAM_EPS) + ADAM_WD * at(w_ref, sl)[...])
            at(m2_ref, sl)[...] = m2
            at(v2_ref, sl)[...] = v2

        main = R // ADAM_ROWS
        if main:
            lax.fori_loop(0, main, lambda i, c: (update(pl.ds(pl.multiple_of(i * ADAM_ROWS, ADAM_ROWS), ADAM_ROWS)), c)[1], 0)
        if R % ADAM_ROWS:
            update(pl.ds(main * ADAM_ROWS, R % ADAM_ROWS))

    spec = pl.BlockSpec((R, 1, cb), lambda j: (0, 0, j)) if deep else pl.BlockSpec((R, cb), lambda j: (0, j))
    g_spec = pl.BlockSpec((n_parts, R, cb), lambda j: (0, 0, j)) if n_parts else pl.BlockSpec((R, cb), lambda j: (0, j))
    n_out = 4 if n_parts else 3
    return pl.pallas_call(
        body, name=name, grid=(C // cb,), in_specs=[spec, g_spec, spec, spec], out_specs=[spec] * n_out,
        out_shape=[_sds(w.shape)] * n_out, compiler_params=_params(("parallel",)),
    )(w, g, m, v)


BIG_PARAMS = ("hy_w_in", "hy_w_out", "cv_w_pw1", "cv_w_pw2", "ffn_w_gate", "ffn_w_up", "ffn_w_down")


def _shards_2d(w):
    t = lambda a: jnp.transpose(a)
    return dict(in_t=t(w["hy_w_in"][0]), out=w["hy_w_out"][0], pw1=w["cv_w_pw1"][0], pw2=w["cv_w_pw2"][0],
                gate_t0=t(w["ffn_w_gate"][0]), gate_t1=t(w["ffn_w_gate"][1]), up_t0=t(w["ffn_w_up"][0]),
                up_t1=t(w["ffn_w_up"][1]), down0=w["ffn_w_down"][0], down1=w["ffn_w_down"][1])


def _unshard_2d(s):
    t = lambda a: jnp.transpose(a)
    out = dict(hy_w_out=s["out"][None], cv_w_pw1=s["pw1"][None], cv_w_pw2=s["pw2"][None],
               ffn_w_gate=jnp.stack([t(s["gate_t0"]), t(s["gate_t1"])]),
               ffn_w_up=jnp.stack([t(s["up_t0"]), t(s["up_t1"])]), ffn_w_down=jnp.stack([s["down0"], s["down1"]]))
    if "in_t" in s:
        out["hy_w_in"] = t(s["in_t"])[None]
    return out


def _full_from_blocks(nm, g):
    if nm == "pw1":
        return jnp.transpose(g, (1, 0, 2)).reshape(g.shape[1], N_DEV * g.shape[2])
    return g.reshape(N_DEV * g.shape[1], g.shape[2])


def _blocks_from_full(nm, g):
    if nm == "pw1":
        return jnp.transpose(g.reshape(g.shape[0], N_DEV, g.shape[1] // N_DEV), (1, 0, 2))
    return g.reshape(N_DEV, g.shape[0] // N_DEV, g.shape[1])


class _VecPack:
    def __init__(self, shapes):
        self.shapes = [tuple(s) for s in shapes]
        self.sizes = [int(np.prod(s)) for s in self.shapes]
        total = sum(self.sizes)
        self.rows = -(-(-(-total // LANES)) // 8) * 8
        self.total = total

    def pack(self, arrays):
        flat = jnp.concatenate([a.astype(f32).reshape(-1) for a in arrays])
        flat = jnp.pad(flat, (0, self.rows * LANES - self.total))
        return flat.reshape(self.rows, LANES)

    def unpack(self, packed):
        flat = packed.reshape(-1)
        out, off = [], 0
        for shp, n in zip(self.shapes, self.sizes):
            out.append(flat[off:off + n].reshape(shp))
            off += n
        return out

    def unpack_stacked(self, stacked, only=None):
        flat = stacked.reshape(stacked.shape[0], -1)
        offs = np.concatenate([[0], np.cumsum(self.sizes)])
        get = lambda i: flat[:, offs[i]:offs[i + 1]].reshape((stacked.shape[0],) + self.shapes[i])
        return get(only) if only is not None else [get(i) for i in range(len(self.shapes))]


def _row(v):
    return v.reshape(1, -1)


def _pad_lanes(v):
    v = v.reshape(1, -1)
    return jnp.pad(v, ((0, 0), (0, LANES - v.shape[1])))


def _ffn_fwd(h, w_gate_t, w_up_t, w_down, tag):
    F = w_down.shape[0]
    a = _mm(h, w_gate_t, tb=True, out_dtype=bf16, name=f"ffn_gate_{tag}")
    u = _mm(h, w_up_t, tb=True, out_dtype=bf16, name=f"ffn_up_{tag}")
    (f,), _ = _rowwise(f"swiglu_{tag}", lambda rv, vv: ([_silu(rv[0].astype(f32)) * rv[1].astype(f32)], []), [a, u], [],
                       [(F, bf16)], [], sub=16, col_chunk=_tile(F, 512, LANES))
    out = _mm(f, w_down, name=f"ffn_down_{tag}")
    return out, (a, u, f)


def _ffn_bwd(h, w_gate_t, w_up_t, w_down, saved, dout, tag):
    a, u, f = saved
    F = w_down.shape[0]
    df = _mm(dout, w_down, tb=True, out_dtype=bf16, name=f"ffn_down_dx_{tag}")
    dw_down = _mm(f, dout, ta=True, out_dtype=bf16, name=f"ffn_down_dw_{tag}")

    def fn(rv, vv):
        _, vjp = jax.vjp(lambda a_, u_: _silu(a_) * u_, rv[0].astype(f32), rv[1].astype(f32))
        da, du = vjp(rv[2].astype(f32))
        return [da, du], []

    (da, du), _ = _rowwise(f"swiglu_bwd_{tag}", fn, [a, u, df], [], [(F, bf16), (F, bf16)], [], sub=16,
                           col_chunk=_tile(F, 512, LANES))
    dh = _mm(du, w_up_t, add=_mm(da, w_gate_t, name=f"ffn_gate_dx_{tag}"), name=f"ffn_up_dx_{tag}")
    dw_gate_t = _mm(da, h, ta=True, out_dtype=bf16, name=f"ffn_gate_dw_{tag}")
    dw_up_t = _mm(du, h, ta=True, out_dtype=bf16, name=f"ffn_up_dw_{tag}")
    return dh, dw_gate_t, dw_up_t, dw_down


def _local_step(x, target, mod, w_in_t, comm, small):
    S, D = x.shape
    di = small["hy_ssm_norm_g"].shape[-1]
    nh = small["hy_dt_bias"].shape[-1]
    cd = small["hy_conv_b"].shape[-1]
    m = [[_row(mod[i, j]) for j in range(6)] for i in range(2)]

    off_q = di + cd + nh
    w_qkv_t = w_in_t[off_q:]
    seg = dict(z=(w_in_t, 0, di), xbc=(w_in_t, di, cd), dt=(w_in_t, di + cd, LANES))
    for i, nm in enumerate(("q0", "q1", "q2", "k", "v")):
        seg[nm] = (w_qkv_t, i * D, D)

    g_mix = [_row(small["norm_mix_g"][i]) for i in range(2)]
    g_ffn = [_row(small["norm_ffn_g"][i]) for i in range(2)]
    conv_w, conv_b = small["hy_conv_w_full"], _row(small["hy_conv_b"][0])
    dt_bias, a_log, d_skip = (_pad_lanes(small[k][0]) for k in ("hy_dt_bias", "hy_a_log", "hy_d_skip"))
    g_ssm = _row(small["hy_ssm_norm_g"][0])
    onehot = jnp.asarray(_bucket_onehot())
    rel_t = small["rel_table"].T
    H = D // HEAD_DIM
    bias = [_exact_mm(rel_t[gi * H:(gi + 1) * H], onehot[gi], name=f"rel_bias_{gi}")
            .reshape(H, 2 * ATT_BLK, ATT_BLK) + _band_penalty() for gi in range(3)]

    h1 = _adaln_fwd(x, g_mix[0], m[0][1], m[0][0], "adaln_mix0")
    proj = {nm: _mm(h1, mat, tb=True, b_rows=(off, cnt), name=f"in_{nm}") for nm, (mat, off, cnt) in seg.items()}
    xbc_pre, xbc = _conv_fwd(proj["xbc"], conv_w, conv_b, silu=True, name="ssm_conv", tr=1024)
    y, hin = _ssd_fwd(xbc, proj["dt"], dt_bias, a_log, d_skip, di, "ssd_fwd")
    (yg,), _ = _rowwise("ssm_gate", lambda rv, vv: ([_gate_f(rv[0], rv[1], vv[0])], []),
                        [y, proj["z"]], [g_ssm], [(di, bf16)], [], sub=16)
    og = [_attn_fwd(proj[f"q{gi}"], proj["k"], proj["v"], bias[gi], d, f"attn_fwd_{gi}")
          for gi, d in enumerate(ATT_DILATIONS)]
    att, att_b, lse_tot = _attn_combine([a for a, _ in og], [_from_blocks(b, LANES) for _, b in og], "attn_combine")
    W = comm.weights(after=att_b)
    w_out_y, w_out_a = W["out"][:di], W["out"][di:]
    mix0 = _mm(att_b, w_out_a, add=_mm(yg, w_out_y, name="out_y"), name="out_a")
    x1, h2 = _resid_adaln_fwd(x, m[0][2], mix0, g_ffn[0], m[0][4], m[0][3], "resid_mix0_adaln_ffn0")
    f0, ffn0_saved = _ffn_fwd(h2, W["gate_t0"], W["up_t0"], W["down0"], "0")
    x2, h3 = _resid_adaln_fwd(x1, m[0][5], f0, g_mix[1], m[1][1], m[1][0], "resid_ffn0_adaln_mix1")
    pw1 = _mm(h3, W["pw1"], bias=_row(small["cv_b_pw1_full"]), name="cv_pw1")
    (u,), _ = _rowwise("cv_glu", lambda rv, vv: ([rv[0] * jax.nn.sigmoid(rv[1])], []),
                       [(pw1, 0, D), (pw1, 1, D)], [], [(D, f32)], [])
    (u2,) = _conv_fwd(u, small["cv_w_dw_full"], _row(small["cv_b_dw_full"]), silu=False, name="cv_dw")
    ln_g, ln_b = _row(small["cv_ln_g_full"]), _row(small["cv_ln_b_full"])
    (u3,), _ = _rowwise("cv_lnsilu", lambda rv, vv: ([_lnsilu_f(rv[0], vv[0], vv[1])], []),
                        [u2], [ln_g, ln_b], [(D, bf16)], [], sub=16)
    mix1 = _mm(u3, W["pw2"], bias=_row(small["cv_b_pw2_full"]), name="cv_pw2")
    x3, h4 = _resid_adaln_fwd(x2, m[1][2], mix1, g_ffn[1], m[1][4], m[1][3], "resid_mix1_adaln_ffn1")
    f1, ffn1_saved = _ffn_fwd(h4, W["gate_t1"], W["up_t1"], W["down1"], "1")

    g_fin = _row(small["final_norm_g"])
    dmod = [[None] * 6 for _ in range(2)]
    d_norm_mix, d_norm_ffn = [None, None], [None, None]
    big = {}

    def final_fn(rv, vv):
        xv, fv, tv = rv
        gate = vv[1]
        yv, vjp = jax.vjp(_rms, xv + gate * fv, vv[0])
        err = yv - tv
        dx, dg = vjp(err / D)
        part = 0.5 * jnp.sum(jnp.mean(err * err, -1, keepdims=True), 0, keepdims=True)
        return [dx, gate * dx], [dg, jnp.broadcast_to(part, (1, LANES)), jnp.sum(dx * fv, 0, keepdims=True)]

    (dx4, df1), (d_fin, loss, dmod[1][5]) = _rowwise("loss_head", final_fn, [x3, f1, target], [g_fin, m[1][5]],
                                                      [(D, f32), (D, bf16)], [D, LANES, D], sub=16)

    dh4, big["gate_t1"], big["up_t1"], big["down1"] = _ffn_bwd(h4, W["gate_t1"], W["up_t1"], W["down1"], ffn1_saved, df1, "1")
    dx3, dmix1, (d_norm_ffn[1], dmod[1][4], dmod[1][3], dmod[1][2], d_b_pw2) = _adaln_resid_bwd(
        x3, g_ffn[1], m[1][4], m[1][3], dh4, dx4, mix1, m[1][2], "adaln_ffn1_resid_mix1_bwd")
    du3 = _mm(dmix1, W["pw2"], tb=True, name="cv_pw2_dx")
    big["pw2"] = _mm(u3, dmix1, ta=True, out_dtype=bf16, name="cv_pw2_dw")

    def lnsilu_bwd(rv, vv):
        _, vjp = jax.vjp(_lnsilu_f, rv[0], vv[0], vv[1])
        du, dg, db = vjp(rv[1])
        return [du], [dg, db]

    (du2,), (d_ln_g, d_ln_b) = _rowwise("cv_lnsilu_bwd", lnsilu_bwd, [u2, du3], [ln_g, ln_b], [(D, f32)], [D, D])
    du, d_w_dw, d_b_dw = _conv_bwd(u, small["cv_w_dw_full"], du2, None, silu=False, name="cv_dw_bwd")

    def glu_bwd(rv, vv):
        a, gt, d = rv
        _, vjp = jax.vjp(lambda a_, g_: a_ * jax.nn.sigmoid(g_), a, gt)
        da, dg = vjp(d)
        return [da, dg], [jnp.sum(da, 0, keepdims=True), jnp.sum(dg, 0, keepdims=True)]

    (dpa, dpg), (d_b1a, d_b1g) = _rowwise("cv_glu_bwd", glu_bwd, [(pw1, 0, D), (pw1, 1, D), du], [],
                                           [(D, bf16), (D, bf16)], [D, D], sub=16)
    dpw1 = jnp.concatenate([dpa, dpg], axis=1)
    d_b_pw1 = jnp.concatenate([d_b1a, d_b1g], axis=1)
    dh3 = _mm(dpw1, W["pw1"], tb=True, name="cv_pw1_dx")
    big["pw1"] = _mm(h3, dpw1, ta=True, out_dtype=bf16, name="cv_pw1_dw")
    token = comm.send("layer1", {nm: big[nm] for nm in ("gate_t1", "up_t1", "down1", "pw2", "pw1")})
    dx2, df0, (d_norm_mix[1], dmod[1][1], dmod[1][0], dmod[0][5], _) = _adaln_resid_bwd(
        x2, g_mix[1], m[1][1], _tie(m[1][0], token), dh3, dx3, f0, m[0][5], "adaln_mix1_resid_ffn0_bwd")

    dh2, big["gate_t0"], big["up_t0"], big["down0"] = _ffn_bwd(h2, W["gate_t0"], W["up_t0"], W["down0"], ffn0_saved, df0, "0")
    dx1, dmix0, (d_norm_ffn[0], dmod[0][4], dmod[0][3], dmod[0][2], _) = _adaln_resid_bwd(
        x1, g_ffn[0], m[0][4], m[0][3], dh2, dx2, mix0, m[0][2], "adaln_ffn0_resid_mix0_bwd")
    dyg = _mm(dmix0, w_out_y, tb=True, name="out_y_dx")
    datt = _mm(dmix0, w_out_a, tb=True, name="out_a_dx")
    big["out"] = jnp.concatenate([_mm(yg, dmix0, ta=True, out_dtype=bf16, name="out_y_dw"),
                                  _mm(att_b, dmix0, ta=True, out_dtype=bf16, name="out_a_dw")], axis=0)
    token = comm.send("layer0", {nm: big[nm] for nm in ("gate_t0", "up_t0", "down0", "out")})
    g_ssm = _tie(g_ssm, token)

    def gate_bwd(rv, vv):
        _, vjp = jax.vjp(_gate_f, rv[0], rv[1], vv[0])
        dy_, dz_, dg_ = vjp(rv[2])
        return [dy_, dz_], [dg_]

    (dy, dz), (d_g_ssm,) = _rowwise("ssm_gate_bwd", gate_bwd, [y, proj["z"], dyg], [g_ssm], [(di, f32), (di, bf16)], [di],
                                    sub=16)
    dxbc, ddtraw, d_a_log, d_dskip, d_dt_bias = _ssd_bwd(xbc, proj["dt"], dt_bias, a_log, d_skip, hin, y, dy, di, "ssd_bwd")
    dxbc_pre, d_conv_w, d_conv_b = _conv_bwd(proj["xbc"], conv_w, dxbc, xbc_pre, silu=True, name="ssm_conv_bwd",
                                             dx_dtype=bf16, tr=1024)
    dh1 = None
    early = []
    for nm, dseg in (("z", dz), ("xbc", dxbc_pre), ("dt", ddtraw)):
        mat, off, cnt = seg[nm]
        dh1 = _mm(dseg, mat, b_rows=(off, cnt), add=dh1, name=f"in_{nm}_dx")
        dwp = _mm(dseg, h1, ta=True, out_dtype=bf16, name=f"in_{nm}_dw")
        early.append(dwp[:nh] if nm == "dt" else dwp)
    early = jnp.concatenate(early, axis=0)
    shard_rows = w_in_t.shape[0] // N_DEV
    n_early = off_q // shard_rows
    token = comm.send_rows("in_early", early[:n_early * shard_rows], (0, n_early))
    bias = [_tie(b, token) for b in bias]

    dq, dks, dvs, dbs = [], [], [], []
    lse_heads = lse_tot[:, :H]
    dsum_heads = _head_sums(att, datt, "attn_dsum")[:, :H]
    for gi, d in enumerate(ATT_DILATIONS):
        a, b, c_, e = _attn_bwd(proj[f"q{gi}"], proj["k"], proj["v"], bias[gi], datt,
                                _by_block(lse_heads, d), _by_block(dsum_heads, d), d, f"attn_bwd_{gi}")
        dq.append(a)
        dks.append(b)
        dvs.append(c_)
        dbs.append(e)
    dk = _add3(*dks, "attn_dk")
    dv = _add3(*dvs, "attn_dv")
    d_rel = jnp.concatenate(
        [_exact_mm(dbs[gi].reshape(H, -1), onehot[gi], tb=True, name=f"rel_grad_{gi}") for gi in range(3)], axis=0).T

    dsegs = (("q0", dq[0]), ("q1", dq[1]), ("q2", dq[2]), ("k", dk), ("v", dv))
    late = jnp.concatenate([early[n_early * shard_rows:]] +
                           [_mm(dseg, h1, ta=True, out_dtype=bf16, name=f"in_{nm}_dw") for nm, dseg in dsegs], axis=0)
    token = comm.send_rows("in_late", late, (n_early, N_DEV))
    w_qkv_after = _tie(w_qkv_t, token)
    for nm, dseg in dsegs:
        _, off, cnt = seg[nm]
        dh1 = _mm(dseg, w_qkv_after, b_rows=(off, cnt), add=dh1, name=f"in_{nm}_dx")
    dx0, (d_norm_mix[0], dmod[0][1], dmod[0][0]) = _adaln_bwd(x, g_mix[0], m[0][1], m[0][0], dh1, dx1, "adaln_mix0_bwd")

    smallg = dict(
        loss=loss, dmod=jnp.stack([jnp.concatenate(dmod[i], axis=1)[0] for i in range(2)]),
        norm_mix_g=jnp.concatenate(d_norm_mix, axis=0), norm_ffn_g=jnp.concatenate(d_norm_ffn, axis=0),
        hy_conv_w=d_conv_w, hy_conv_b=d_conv_b, hy_dt_bias=d_dt_bias[:, :nh], hy_a_log=d_a_log[:, :nh],
        hy_d_skip=d_dskip[:, :nh], hy_ssm_norm_g=d_g_ssm, rel_table=d_rel,
        cv_b_pw1=d_b_pw1, cv_w_dw=d_w_dw, cv_b_dw=d_b_dw, cv_ln_g=d_ln_g, cv_ln_b=d_ln_b, cv_b_pw2=d_b_pw2,
        final_norm_g=d_fin)
    return dx0, n_early, smallg


SMALL_GRAD_ORDER = ("loss", "dmod", "norm_mix_g", "norm_ffn_g", "hy_conv_w", "hy_conv_b", "hy_dt_bias", "hy_a_log",
                    "hy_d_skip", "hy_ssm_norm_g", "rel_table", "cv_b_pw1", "cv_w_dw", "cv_b_dw", "cv_ln_g", "cv_ln_b",
                    "cv_b_pw2", "final_norm_g")


def kernel(x, c, ada_w, ada_b, norm_mix_g, norm_ffn_g, hy_w_in, hy_conv_w, hy_conv_b, hy_dt_bias, hy_a_log, hy_d_skip, hy_ssm_norm_g, hy_w_out, rel_table, cv_w_pw1, cv_b_pw1, cv_w_dw, cv_b_dw, cv_ln_g, cv_ln_b, cv_w_pw2, cv_b_pw2, ffn_w_gate, ffn_w_up, ffn_w_down, final_norm_g, loss_target, m_ada_w, m_ada_b, m_norm_mix_g, m_norm_ffn_g, m_hy_w_in, m_hy_conv_w, m_hy_conv_b, m_hy_dt_bias, m_hy_a_log, m_hy_d_skip, m_hy_ssm_norm_g, m_hy_w_out, m_rel_table, m_cv_w_pw1, m_cv_b_pw1, m_cv_w_dw, m_cv_b_dw, m_cv_ln_g, m_cv_ln_b, m_cv_w_pw2, m_cv_b_pw2, m_ffn_w_gate, m_ffn_w_up, m_ffn_w_down, m_final_norm_g, v_ada_w, v_ada_b, v_norm_mix_g, v_norm_ffn_g, v_hy_w_in, v_hy_conv_w, v_hy_conv_b, v_hy_dt_bias, v_hy_a_log, v_hy_d_skip, v_hy_ssm_norm_g, v_hy_w_out, v_rel_table, v_cv_w_pw1, v_cv_b_pw1, v_cv_w_dw, v_cv_b_dw, v_cv_ln_g, v_cv_ln_b, v_cv_w_pw2, v_cv_b_pw2, v_ffn_w_gate, v_ffn_w_up, v_ffn_w_down, v_final_norm_g):
    names = ("ada_w", "ada_b", "norm_mix_g", "norm_ffn_g", "hy_w_in", "hy_conv_w", "hy_conv_b", "hy_dt_bias", "hy_a_log",
             "hy_d_skip", "hy_ssm_norm_g", "hy_w_out", "rel_table", "cv_w_pw1", "cv_b_pw1", "cv_w_dw", "cv_b_dw", "cv_ln_g",
             "cv_ln_b", "cv_w_pw2", "cv_b_pw2", "ffn_w_gate", "ffn_w_up", "ffn_w_down", "final_norm_g")
    w = dict(zip(names, (ada_w, ada_b, norm_mix_g, norm_ffn_g, hy_w_in, hy_conv_w, hy_conv_b, hy_dt_bias, hy_a_log, hy_d_skip,
                         hy_ssm_norm_g, hy_w_out, rel_table, cv_w_pw1, cv_b_pw1, cv_w_dw, cv_b_dw, cv_ln_g, cv_ln_b, cv_w_pw2,
                         cv_b_pw2, ffn_w_gate, ffn_w_up, ffn_w_down, final_norm_g)))
    mom = dict(zip(names, (m_ada_w, m_ada_b, m_norm_mix_g, m_norm_ffn_g, m_hy_w_in, m_hy_conv_w, m_hy_conv_b, m_hy_dt_bias,
                           m_hy_a_log, m_hy_d_skip, m_hy_ssm_norm_g, m_hy_w_out, m_rel_table, m_cv_w_pw1, m_cv_b_pw1, m_cv_w_dw,
                           m_cv_b_dw, m_cv_ln_g, m_cv_ln_b, m_cv_w_pw2, m_cv_b_pw2, m_ffn_w_gate, m_ffn_w_up, m_ffn_w_down,
                           m_final_norm_g)))
    vel = dict(zip(names, (v_ada_w, v_ada_b, v_norm_mix_g, v_norm_ffn_g, v_hy_w_in, v_hy_conv_w, v_hy_conv_b, v_hy_dt_bias,
                           v_hy_a_log, v_hy_d_skip, v_hy_ssm_norm_g, v_hy_w_out, v_rel_table, v_cv_w_pw1, v_cv_b_pw1, v_cv_w_dw,
                           v_cv_b_dw, v_cv_ln_g, v_cv_ln_b, v_cv_w_pw2, v_cv_b_pw2, v_ffn_w_gate, v_ffn_w_up, v_ffn_w_down,
                           v_final_norm_g)))
    S, D = x.shape[1], x.shape[2]
    ax, ay, ac = lax.axis_index("x"), lax.axis_index("y"), lax.axis_index("c")
    me = 4 * ax + 2 * ay + ac
    nmod = ada_w.shape[2]

    w2 = _shards_2d(w)
    big_names = list(w2)
    sharded_small = ("hy_conv_w", "cv_b_pw1", "cv_w_dw", "cv_b_dw", "cv_ln_g", "cv_ln_b", "cv_b_pw2")
    vp = _VecPack([c.shape] + [w[nm].shape for nm in sharded_small])
    g_in, sg = _all_gather([w2["in_t"].astype(bf16), vp.pack([c] + [w[nm] for nm in sharded_small])], "gather_w_in")
    w_in_t = _full_from_blocks("in_t", g_in)
    parts = vp.unpack_stacked(sg)
    c_all = parts[0][:, 0]
    small = {k: w[k] for k in ("norm_mix_g", "norm_ffn_g", "hy_conv_b", "hy_dt_bias", "hy_a_log", "hy_d_skip",
                               "hy_ssm_norm_g", "rel_table", "final_norm_g")}
    for p, nm in zip(parts[1:], sharded_small):
        p = p[:, 0]
        p = jnp.moveaxis(p, 0, -2)
        small[nm + "_full"] = p.reshape(p.shape[:-2] + (N_DEV * p.shape[-1],))

    (cs_all,), _ = _rowwise("ada_silu", lambda rv, vv: ([_silu(rv[0])], []), [c_all], [], [(D, f32)], [])
    b_mine = lax.dynamic_slice_in_dim(ada_b, me * nmod, nmod, axis=1)
    mod_part = jnp.stack([_mm(cs_all, ada_w[i], bias=b_mine[i:i + 1], name=f"ada_mod_{i}") for i in range(2)])
    (mod_all,) = _all_gather([mod_part.reshape(2 * N_DEV, nmod)], "gather_mod")
    mod_all = mod_all.reshape(N_DEV, 2, N_DEV, nmod)
    mod_mine = lax.dynamic_index_in_dim(mod_all, me, axis=2, keepdims=False)
    mod = jnp.transpose(mod_mine, (1, 0, 2)).reshape(2, 6, D)
    comm = _Overlap({nm: w2[nm].astype(bf16) for nm in big_names if nm != "in_t"}, me, after=[mod, w_in_t])
    mod = _tie(mod, comm.token)

    dx0, n_early, sgrad = _local_step(x[0], loss_target[0], mod, w_in_t, comm, small)

    gp = _VecPack([sgrad[k].shape for k in SMALL_GRAD_ORDER])
    (g_all,) = _all_gather([gp.pack([sgrad[k] for k in SMALL_GRAD_ORDER])], "gather_small_grads")
    tot = dict(zip(SMALL_GRAD_ORDER, gp.unpack(_sum_slots(g_all, "sum_small_grads"))))
    dmod_all = gp.unpack_stacked(g_all, only=SMALL_GRAD_ORDER.index("dmod"))
    loss = tot["loss"][0, 0]

    grads = {}
    dmod_mine = lax.dynamic_slice_in_dim(dmod_all, me * nmod, nmod, axis=2)
    grads["ada_w"] = jnp.stack([_mm(cs_all, dmod_mine[:, i], ta=True, name=f"ada_w_grad_{i}") for i in range(2)])
    grads["ada_b"] = tot["dmod"]
    grads["norm_mix_g"], grads["norm_ffn_g"] = tot["norm_mix_g"], tot["norm_ffn_g"]
    grads["hy_conv_b"] = tot["hy_conv_b"]
    grads["hy_dt_bias"] = tot["hy_dt_bias"]
    grads["hy_a_log"] = tot["hy_a_log"]
    grads["hy_d_skip"] = tot["hy_d_skip"]
    grads["hy_ssm_norm_g"] = tot["hy_ssm_norm_g"]
    grads["rel_table"] = tot["rel_table"]
    grads["final_norm_g"] = tot["final_norm_g"][0]
    for nm in sharded_small:
        n = w[nm].shape[-1]
        grads[nm] = lax.dynamic_slice_in_dim(tot[nm], me * n, n, axis=1).reshape(w[nm].shape)

    delta, new_m, new_v = {}, {}, {}
    shp = ada_w.shape
    two = lambda t: t.reshape(-1, shp[-1])
    d_, m_, v_ = _adamw(two(ada_w), two(grads["ada_w"]), two(m_ada_w), two(v_ada_w), "adamw_ada_w")
    delta["ada_w"], new_m["ada_w"], new_v["ada_w"] = d_.reshape(shp), m_.reshape(shp), v_.reshape(shp)
    rest = [nm for nm in names if nm not in BIG_PARAMS and nm != "ada_w"]
    sp = _VecPack([w[nm].shape for nm in rest])
    packs = [sp.pack([t[nm] for nm in rest]) for t in (w, grads, mom, vel)]
    ds_, ms_, vs_ = _adamw(*packs, "adamw_small")
    for nm, a, b, e in zip(rest, sp.unpack(ds_), sp.unpack(ms_), sp.unpack(vs_)):
        delta[nm], new_m[nm], new_v[nm] = a, b, e

    m2, v2 = _shards_2d(mom), _shards_2d(vel)
    g2, d2, nm2, nv2 = {}, {}, {}, {}
    after = d_
    slots = {}
    for tag in ("layer1", "layer0", "in_early", "in_late"):
        slots.update(comm.received(tag, after))
        if tag.startswith("in_"):
            continue
        for nm in comm.sent[tag][0]:
            d2[nm], nm2[nm], nv2[nm], g2[nm] = _adamw(w2[nm], slots[nm], m2[nm], v2[nm], f"adamw_{nm}")
            after = g2[nm]
    stored = lambda t: jnp.transpose(t, (2, 0, 1))
    in_slots = jnp.where(me < n_early, slots["in_early"], slots["in_late"])
    d_in, m_in, v_in, g_in = _adamw(stored(hy_w_in), in_slots, stored(m_hy_w_in), stored(v_hy_w_in), "adamw_in_t")
    for dst, part, t in ((grads, g2, g_in), (delta, d2, d_in), (new_m, nm2, m_in), (new_v, nv2, v_in)):
        dst.update(_unshard_2d(part))
        dst["hy_w_in"] = jnp.transpose(t, (1, 2, 0))

    return (loss, dx0[None], *[grads[n] for n in names], *[delta[n] for n in names],
            *[new_m[n] for n in names], *[new_v[n] for n in names])
```

```python
import functools
import math

import numpy as np
import jax
import jax.numpy as jnp
from jax import lax
from jax.experimental import pallas as pl
from jax.experimental.pallas import tpu as pltpu

f32 = jnp.float32
bf16 = jnp.bfloat16
EPS = 1e-6
N_DEV = 8
LANES = 128
SSM_STATE = 128
SSM_CHUNK = 128
SSM_GROUPS = 4
HEAD_DIM = 64
ATT_BLK = 128
ATT_DILATIONS = (1, 4, 16)
REL_BUCKETS = 32
REL_MAX_DIST = 2048
ADAM_LR, ADAM_B1, ADAM_B2, ADAM_EPS, ADAM_WD, ADAM_STEP = 0.001, 0.9, 0.999, 1e-08, 0.01, 10
PACK_ROW_TILE = 256
MESH = pl.DeviceIdType.MESH
VMEM_LIMIT = 48 * 1024 * 1024


def _sds(shape, dtype=f32):
    return jax.ShapeDtypeStruct(tuple(shape), dtype)


def _tile(n, cap, mult):
    best = None
    t = mult
    while t <= min(n, cap):
        if n % t == 0:
            best = t
        t += mult
    return best if best is not None else n


def _params(sem):
    return pltpu.CompilerParams(dimension_semantics=sem, vmem_limit_bytes=VMEM_LIMIT)


def _mm(a, b, *, name, ta=False, tb=False, b_rows=None, bias=None, add=None, out_dtype=f32,
        tm_cap=512, tn_cap=1536, tk_cap=8192):
    if ta:
        K, M = a.shape
    else:
        M, K = a.shape
    off, cnt = b_rows if b_rows is not None else (0, b.shape[0])
    if tb:
        N, K2 = cnt, b.shape[1]
    else:
        K2, N = cnt, b.shape[1]
    assert K == K2, (a.shape, b.shape, ta, tb, b_rows)
    if ta and a.dtype == f32:
        tm_cap = min(tm_cap, 256)
    tm = _tile(M, tm_cap, LANES)
    tn = _tile(math.gcd(off, N) if tb else N, tn_cap, LANES)
    tk = _tile(K if tb else math.gcd(off, K), tk_cap, LANES)
    assert N % tn == 0 and K % tk == 0 and off % (tn if tb else tk) == 0, (name, off, N, K, tn, tk)
    nk = K // tk
    jo, ko = (off // tn, 0) if tb else (0, off // tk)
    has_bias, has_add = bias is not None, add is not None
    dn = (((0 if ta else 1,), (1 if tb else 0,)), ((), ()))

    def body(*refs):
        a_ref, b_ref = refs[0], refs[1]
        pos = 2
        bias_ref = add_ref = None
        if has_bias:
            bias_ref = refs[pos]
            pos += 1
        if has_add:
            add_ref = refs[pos]
            pos += 1
        o_ref = refs[pos]
        k = pl.program_id(2)
        part = lax.dot_general(a_ref[...].astype(bf16), b_ref[...].astype(bf16), dn, preferred_element_type=f32)

        def finish(r):
            if has_bias:
                r = r + bias_ref[...]
            if has_add:
                r = r + add_ref[...]
            o_ref[...] = r.astype(o_ref.dtype)

        if nk == 1:
            finish(part)
        else:
            acc_ref = refs[pos + 1]

            @pl.when(k == 0)
            def _():
                acc_ref[...] = part

            @pl.when((k > 0) & (k < nk - 1))
            def _():
                acc_ref[...] += part

            @pl.when(k == nk - 1)
            def _():
                finish(acc_ref[...] + part)

    in_specs = [
        pl.BlockSpec((tk, tm), lambda i, j, k: (k, i)) if ta else pl.BlockSpec((tm, tk), lambda i, j, k: (i, k)),
        pl.BlockSpec((tn, tk), lambda i, j, k: (j + jo, k)) if tb else pl.BlockSpec((tk, tn), lambda i, j, k: (k + ko, j)),
    ]
    args = [a, b]
    if has_bias:
        in_specs.append(pl.BlockSpec((1, tn), lambda i, j, k: (0, j)))
        args.append(bias)
    if has_add:
        in_specs.append(pl.BlockSpec((tm, tn), lambda i, j, k: (i, j)))
        args.append(add)
    return pl.pallas_call(
        body, name=name, grid=(M // tm, N // tn, nk), in_specs=in_specs,
        out_specs=pl.BlockSpec((tm, tn), lambda i, j, k: (i, j)), out_shape=_sds((M, N), out_dtype),
        scratch_shapes=[pltpu.VMEM((tm, tn), f32)] if nk > 1 else [],
        compiler_params=_params(("parallel", "parallel", "arbitrary")),
    )(*args)


def _rowwise(name, fn, rows, vecs, out_rows, out_accs, *, tr_cap=256, sub=8, col_chunk=None):
    rows = [r if isinstance(r, tuple) else (r, 0, r.shape[1]) for r in rows]
    R = rows[0][0].shape[0]
    tr = _tile(R, tr_cap, 8)
    sub = sub if tr % sub == 0 else tr
    n_r, n_v, n_or, n_oa = len(rows), len(vecs), len(out_rows), len(out_accs)

    def body(*refs):
        row_refs = refs[:n_r]
        vec_refs = refs[n_r:n_r + n_v]
        orow_refs = refs[n_r + n_v:n_r + n_v + n_or]
        oacc_refs = refs[n_r + n_v + n_or:]
        vv = [r[...] for r in vec_refs]

        n_sub = tr // sub
        together = 4 if n_sub % 4 == 0 else 1

        def step(s, accs):
            for t in range(together):
                sl = pl.ds(pl.multiple_of((s * together + t) * sub, sub), sub)
                if col_chunk is None:
                    ro, ao = fn([r[sl, :] for r in row_refs], vv)
                    for o_ref, o in zip(orow_refs, ro):
                        o_ref[sl, :] = o.astype(o_ref.dtype)
                    accs = tuple(x + y for x, y in zip(accs, ao))
                else:
                    for c0 in range(0, rows[0][2], col_chunk):
                        cs_ = pl.ds(c0, col_chunk)
                        ro, _ = fn([r[sl, cs_] for r in row_refs], vv)
                        for o_ref, o in zip(orow_refs, ro):
                            o_ref[sl, cs_] = o.astype(o_ref.dtype)
            return accs

        accs = lax.fori_loop(0, n_sub // together, step, tuple(jnp.zeros((1, w), f32) for w in out_accs))
        if n_oa:
            @pl.when(pl.program_id(0) == 0)
            def _():
                for ref in oacc_refs:
                    ref[...] = jnp.zeros_like(ref)

            for ref, x in zip(oacc_refs, accs):
                ref[...] += x

    in_specs = [pl.BlockSpec((tr, w), functools.partial(lambda i, cb: (i, cb), cb=cb)) for (_, cb, w) in rows]
    in_specs += [pl.BlockSpec((1, v.shape[1]), lambda i: (0, 0)) for v in vecs]
    out_specs = [pl.BlockSpec((tr, w), lambda i: (i, 0)) for (w, _) in out_rows]
    out_specs += [pl.BlockSpec((1, w), lambda i: (0, 0)) for w in out_accs]
    out_shape = [_sds((R, w), dt) for (w, dt) in out_rows] + [_sds((1, w)) for w in out_accs]
    res = pl.pallas_call(
        body, name=name, grid=(R // tr,), in_specs=in_specs, out_specs=out_specs, out_shape=out_shape,
        compiler_params=_params(("arbitrary",)),
    )(*[r[0] for r in rows], *vecs)
    return res[:n_or], res[n_or:]


def _silu(x):
    return x * jax.nn.sigmoid(x)


def _rms(x, g):
    return x * lax.rsqrt(jnp.mean(x * x, -1, keepdims=True) + EPS) * g


def _adaln_f(x, g, sc, sh):
    return _rms(x, g) * (1.0 + sc) + sh


def _gate_f(y, z, g):
    return _rms(y * _silu(z), g)


def _lnsilu_f(u, g, b):
    mu = jnp.mean(u, -1, keepdims=True)
    var = jnp.mean(jnp.square(u - mu), -1, keepdims=True)
    return _silu((u - mu) * lax.rsqrt(var + EPS) * g + b)


def _adaln_fwd(x, g, sc, sh, name):
    (h,), _ = _rowwise(name, lambda rv, vv: ([_adaln_f(rv[0], *vv)], []), [x], [g, sc, sh], [(x.shape[1], bf16)], [],
                       sub=16)
    return h


def _adaln_bwd(x, g, sc, sh, dh, dres, name):
    def fn(rv, vv):
        xv, dhv, drv = rv
        _, vjp = jax.vjp(_adaln_f, xv, *vv)
        dx, dg, dsc, dsh = vjp(dhv)
        return [dx + drv], [dg, dsc, dsh]
    w = x.shape[1]
    (dx,), accs = _rowwise(name, fn, [x, dh, dres], [g, sc, sh], [(w, f32)], [w, w, w])
    return dx, accs


def _resid_adaln_fwd(x, gate, mix, g, sc, sh, name):
    def fn(rv, vv):
        xn = rv[0] + vv[0] * rv[1]
        return [xn, _adaln_f(xn, vv[1], vv[2], vv[3])], []
    w = x.shape[1]
    (xn, h), _ = _rowwise(name, fn, [x, mix], [gate, g, sc, sh], [(w, f32), (w, bf16)], [], sub=16)
    return xn, h


def _adaln_resid_bwd(x, g, sc, sh, dh, dres, mix, gate, name):
    def fn(rv, vv):
        xv, dhv, drv, mv = rv
        _, vjp = jax.vjp(_adaln_f, xv, vv[0], vv[1], vv[2])
        dx, dg, dsc, dsh = vjp(dhv)
        dx = dx + drv
        dm = vv[3] * dx
        return [dx, dm], [dg, dsc, dsh, jnp.sum(dx * mv, 0, keepdims=True), jnp.sum(dm, 0, keepdims=True)]
    w = x.shape[1]
    (dx, dmix), accs = _rowwise(name, fn, [x, dh, dres, mix], [g, sc, sh, gate], [(w, f32), (w, bf16)], [w] * 5, sub=16)
    return dx, dmix, accs


def _add3(a, b, c, name):
    (y,), _ = _rowwise(name, lambda rv, vv: ([rv[0] + rv[1] + rv[2]], []), [a, b, c], [], [(a.shape[1], bf16)], [],
                       sub=16)
    return y


CONV_HALO = 32
CONV_ROWS = 64


def _conv_fwd(x, w, b, *, silu, name, tr=512):
    S, C = x.shape
    K = w.shape[0]
    H = CONV_HALO
    assert K - 1 <= H and S % tr == 0 and tr % H == 0 and C % LANES == 0
    nh = tr // H

    def body(xp_ref, xc_ref, w_ref, b_ref, *rest):
        outs, scr = rest[:-1], rest[-1]
        i = pl.program_id(1)
        scr[pl.ds(0, H), :] = jnp.where(i > 0, xp_ref[...], 0.0)
        scr[pl.ds(H, tr), :] = xc_ref[...]
        taps = [w_ref[pl.ds(k, 1), :] for k in range(K)]
        for c0 in range(0, tr, CONV_ROWS):
            acc = jnp.zeros((CONV_ROWS, LANES), f32) + b_ref[...]
            for k in range(K):
                acc = acc + scr[pl.ds(c0 + H - (K - 1) + k, CONV_ROWS), :] * taps[k]
            outs[0][pl.ds(c0, CONV_ROWS), :] = acc.astype(outs[0].dtype)
            if silu:
                outs[1][pl.ds(c0, CONV_ROWS), :] = _silu(acc)

    n_out = 2 if silu else 1
    return pl.pallas_call(
        body, name=name, grid=(C // LANES, S // tr),
        in_specs=[pl.BlockSpec((H, LANES), lambda j, i: (jnp.maximum(i * nh - 1, 0), j)),
                  pl.BlockSpec((tr, LANES), lambda j, i: (i, j)),
                  pl.BlockSpec((K, LANES), lambda j, i: (0, j)),
                  pl.BlockSpec((1, LANES), lambda j, i: (0, j))],
        out_specs=[pl.BlockSpec((tr, LANES), lambda j, i: (i, j))] * n_out,
        out_shape=[_sds((S, C), bf16), _sds((S, C))] if silu else [_sds((S, C))],
        scratch_shapes=[pltpu.VMEM((tr + H, LANES), f32)],
        compiler_params=_params(("parallel", "arbitrary")),
    )(x, x, w, b)


def _conv_bwd(x, w, dact, pre, *, silu, name, dx_dtype=f32, tr=512):
    S, C = x.shape
    K = w.shape[0]
    H = CONV_HALO
    nh = tr // H
    n_i = S // tr
    kp = -(-K // 8) * 8

    def dsilu(p):
        s = jax.nn.sigmoid(p)
        return s * (1.0 + p * (1.0 - s))

    def body(*refs):
        if silu:
            xp_ref, xc_ref, w_ref, dc_ref, dn_ref, pc_ref, pn_ref, dx_ref, dw_ref, db_ref, xs, ds = refs
        else:
            xp_ref, xc_ref, w_ref, dc_ref, dn_ref, dx_ref, dw_ref, db_ref, xs, ds = refs
        i = pl.program_id(1)
        xs[pl.ds(0, H), :] = jnp.where(i > 0, xp_ref[...], 0.0)
        xs[pl.ds(H, tr), :] = xc_ref[...]
        dcur = dc_ref[...]
        dnext = dn_ref[...]
        if silu:
            dcur = dcur * dsilu(pc_ref[...].astype(f32))
            dnext = dnext * dsilu(pn_ref[...].astype(f32))
        ds[pl.ds(0, tr), :] = dcur
        ds[pl.ds(tr, H), :] = jnp.where(i < n_i - 1, dnext, 0.0)
        taps = [w_ref[pl.ds(k, 1), :] for k in range(K)]
        fold = lambda t: jnp.sum(t.reshape(CONV_ROWS // 8, 8, LANES), axis=0)
        dw_parts = [jnp.zeros((8, LANES), f32) for _ in range(K)]
        db_part = jnp.zeros((8, LANES), f32)
        for c0 in range(0, tr, CONV_ROWS):
            acc = jnp.zeros((CONV_ROWS, LANES), f32)
            d_c = ds[pl.ds(c0, CONV_ROWS), :]
            for k in range(K):
                acc = acc + ds[pl.ds(c0 + K - 1 - k, CONV_ROWS), :] * taps[k]
                dw_parts[k] = dw_parts[k] + fold(d_c * xs[pl.ds(c0 + H - (K - 1) + k, CONV_ROWS), :])
            db_part = db_part + fold(d_c)
            dx_ref[pl.ds(c0, CONV_ROWS), :] = acc.astype(dx_ref.dtype)

        @pl.when(i == 0)
        def _():
            dw_ref[...] = jnp.zeros_like(dw_ref)
            db_ref[...] = jnp.zeros_like(db_ref)

        for k in range(K):
            dw_ref[pl.ds(k, 1), :] += jnp.sum(dw_parts[k], 0, keepdims=True)
        db_ref[...] += jnp.sum(db_part, 0, keepdims=True)

    prev = pl.BlockSpec((H, LANES), lambda j, i: (jnp.maximum(i * nh - 1, 0), j))
    cur = pl.BlockSpec((tr, LANES), lambda j, i: (i, j))
    nxt = pl.BlockSpec((H, LANES), lambda j, i: (jnp.minimum((i + 1) * nh, n_i * nh - 1), j))
    in_specs = [prev, cur, pl.BlockSpec((K, LANES), lambda j, i: (0, j)), cur, nxt]
    args = [x, x, w, dact, dact]
    if silu:
        in_specs += [cur, nxt]
        args += [pre, pre]
    dx, dw, db = pl.pallas_call(
        body, name=name, grid=(C // LANES, n_i), in_specs=in_specs,
        out_specs=[cur, pl.BlockSpec((kp, LANES), lambda j, i: (0, j)), pl.BlockSpec((1, LANES), lambda j, i: (0, j))],
        out_shape=[_sds((S, C), dx_dtype), _sds((kp, C)), _sds((1, C))],
        scratch_shapes=[pltpu.VMEM((tr + H, LANES), f32), pltpu.VMEM((tr + H, LANES), f32)],
        compiler_params=_params(("parallel", "arbitrary")),
    )(*args)
    return dx, dw[:K], db


def _dot(a, b):
    return jnp.dot(a.astype(bf16), b.astype(bf16), preferred_element_type=f32)


def _dot_nt(a, b):
    return lax.dot_general(a.astype(bf16), b.astype(bf16), (((1,), (1,)), ((), ())), preferred_element_type=f32)


def _dot_tn(a, b):
    return lax.dot_general(a.astype(bf16), b.astype(bf16), (((0,), (0,)), ((), ())), preferred_element_type=f32)


def _softplus(x):
    return jnp.maximum(x, 0.0) + jnp.log(1.0 + jnp.exp(-jnp.abs(x)))


def _tri(q):
    i = lax.broadcasted_iota(jnp.int32, (q, q), 0)
    j = lax.broadcasted_iota(jnp.int32, (q, q), 1)
    return i >= j


def _ssd_prep(dtraw, dt_bias, a_log):
    q = dtraw.shape[0]
    dt = _softplus(dtraw + dt_bias)
    A = -jnp.exp(a_log)
    tri = _tri(q)
    cs = jnp.dot(tri.astype(f32), dt * A, preferred_element_type=f32, precision=lax.Precision.HIGHEST)
    return dt, A, cs, cs.T, tri


def _expand(cols, h0, n, width):
    q = cols.shape[0]
    return jnp.concatenate([jnp.broadcast_to(cols[:, h0 + r:h0 + r + 1], (q, width)) for r in range(n)], axis=1)


def _ssd_fwd(xbc, dtraw, dt_bias, a_log, d_skip, di, name):
    S, CD = xbc.shape
    Q, N, G = SSM_CHUNK, SSM_STATE, SSM_GROUPS
    nc = S // Q
    nh = di // HEAD_DIM
    R = nh // G
    gw = R * HEAD_DIM
    col_of_head = jnp.asarray((np.arange(LANES)[:, None] == np.arange(di)[None, :] // HEAD_DIM).astype(np.float32))
    dsk_wide = jnp.repeat(d_skip[0, :nh], HEAD_DIM)[None]

    def body(xbc_ref, dt_ref, bias_ref, alog_ref, dskw_ref, coh_ref, y_ref, hin_ref, state):
        c = pl.program_id(0)

        @pl.when(c == 0)
        def _():
            state[...] = jnp.zeros_like(state)

        hin_ref[...] = state[...]
        dt, A, cs, csT, tri = _ssd_prep(dt_ref[...], bias_ref[...], alog_ref[...])
        elast = jnp.exp(cs[Q - 1:Q, :])
        coh = coh_ref[...]
        dt_w, ecs_w, dend_w = _dot_exact(dt, coh), _dot_exact(jnp.exp(cs), coh), _dot_exact(jnp.exp(cs[Q - 1:Q, :] - cs), coh)
        for g in range(G):
            h0 = g * R
            cols = pl.ds(g * gw, gw)
            lanes = slice(g * gw, (g + 1) * gw)
            Bg = xbc_ref[:, pl.ds(di + g * N, N)]
            Cg = xbc_ref[:, pl.ds(di + G * N + g * N, N)]
            xg = xbc_ref[:, cols]
            Hg = state[cols, :]
            Gm = _dot_nt(Cg, Bg)
            xdt = xg * dt_w[:, lanes]
            yoff = _dot_nt(Cg, Hg) * ecs_w[:, lanes]
            ys = []
            for r in range(R):
                h = h0 + r
                L = jnp.exp(jnp.where(tri, cs[:, h:h + 1] - csT[h:h + 1, :], -jnp.inf))
                ys.append(_dot(Gm * L, xdt[:, r * HEAD_DIM:(r + 1) * HEAD_DIM]))
            y_ref[:, cols] = jnp.concatenate(ys, axis=1) + yoff + xg * dskw_ref[:, cols]
            hnew = _dot_tn(xdt * dend_w[:, lanes], Bg)
            escale = jnp.concatenate([jnp.broadcast_to(elast[:, h0 + r:h0 + r + 1], (HEAD_DIM, N)) for r in range(R)], axis=0)
            state[cols, :] = escale * Hg + hnew

    vec = pl.BlockSpec((1, LANES), lambda c: (0, 0))
    return pl.pallas_call(
        body, name=name, grid=(nc,),
        in_specs=[pl.BlockSpec((Q, CD), lambda c: (c, 0)), pl.BlockSpec((Q, LANES), lambda c: (c, 0)), vec, vec,
                  pl.BlockSpec((1, di), lambda c: (0, 0)), pl.BlockSpec((LANES, di), lambda c: (0, 0))],
        out_specs=[pl.BlockSpec((Q, di), lambda c: (c, 0)), pl.BlockSpec((None, di, N), lambda c: (c, 0, 0))],
        out_shape=[_sds((S, di)), _sds((nc, di, N))],
        scratch_shapes=[pltpu.VMEM((di, N), f32)],
        compiler_params=_params(("arbitrary",)),
    )(xbc, dtraw, dt_bias, a_log, dsk_wide, col_of_head)


def _dot_exact(a, b):
    bb = b.astype(bf16)
    hi = a.astype(bf16)
    rest = a - hi.astype(f32)
    mid = rest.astype(bf16)
    low = (rest - mid.astype(f32)).astype(bf16)
    one_pass = lambda t: jnp.dot(t, bb, preferred_element_type=f32)
    return one_pass(hi) + one_pass(mid) + one_pass(low)


def _ssd_bwd(xbc, dtraw, dt_bias, a_log, d_skip, hin, y, dy, di, name):
    S, CD = xbc.shape
    Q, N, G = SSM_CHUNK, SSM_STATE, SSM_GROUPS
    nc = S // Q
    nh = di // HEAD_DIM
    R = nh // G
    gw = R * HEAD_DIM
    P = HEAD_DIM
    head_of_col = jnp.asarray((np.arange(di)[:, None] // P == np.arange(LANES)[None, :]).astype(np.float32))
    dsk_wide = jnp.repeat(d_skip[0, :nh], P)[None]

    def body(xbc_ref, dt_ref, bias_ref, alog_ref, dskw_ref, hoc_ref, hin_ref, y_ref, dy_ref,
             dxbc_ref, ddt_ref, dA_ref, ddsk_ref, dtb_ref, dstate, dxdt_all, tend_all, yoff_all, colterm_all):
        c = pl.program_id(0)

        @pl.when(c == 0)
        def _():
            dstate[...] = jnp.zeros_like(dstate)
            dA_ref[...] = jnp.zeros_like(dA_ref)
            ddsk_ref[...] = jnp.zeros_like(ddsk_ref)
            dtb_ref[...] = jnp.zeros_like(dtb_ref)

        dtraw_v = dt_ref[...]
        dt, A, cs, csT, tri = _ssd_prep(dtraw_v, bias_ref[...], alog_ref[...])
        tri_t = jnp.logical_not(tri) | (lax.broadcasted_iota(jnp.int32, (Q, Q), 0) == lax.broadcasted_iota(jnp.int32, (Q, Q), 1))
        ecs = jnp.exp(cs)
        dend = jnp.exp(cs[Q - 1:Q, :] - cs)
        elast = jnp.exp(cs[Q - 1:Q, :])
        hoc = hoc_ref[...]
        state_dot = jnp.sum(_dot_exact(dstate[...] * hin_ref[...], jnp.ones((N, LANES), f32)) * hoc, 0, keepdims=True) * elast
        for g in range(G):
            h0 = g * R
            Bg = xbc_ref[:, pl.ds(di + g * N, N)]
            Cg = xbc_ref[:, pl.ds(di + G * N + g * N, N)]
            xg = xbc_ref[:, pl.ds(g * gw, gw)]
            dyg = dy_ref[:, pl.ds(g * gw, gw)]
            Hg = hin_ref[pl.ds(g * gw, gw), :]
            dHg = dstate[pl.ds(g * gw, gw), :]
            dt_e = _expand(dt, h0, R, P)
            ecs_e = _expand(ecs, h0, R, P)
            dend_e = _expand(dend, h0, R, P)
            cols = pl.ds(g * gw, gw)
            Gm = _dot_nt(Cg, Bg)
            Gm_t = _dot_nt(Bg, Cg)
            xdt = xg * dt_e
            dye = dyg * ecs_e
            bdh = _dot_nt(Bg, dHg)
            dC = _dot(dye, Hg)
            dB = _dot(xdt * dend_e, dHg)
            dHin = _dot_tn(dye, Cg)
            dxdt_state = dend_e * bdh
            end_term = xdt * dxdt_state
            tend_all[:, cols] = end_term
            yoff_all[:, cols] = _dot_nt(Cg, Hg) * ecs_e
            dG = jnp.zeros((Q, Q), f32)
            dxd = []
            for r in range(R):
                h = h0 + r
                sl = slice(r * P, (r + 1) * P)
                seg = cs[:, h:h + 1] - csT[h:h + 1, :]
                L = jnp.exp(jnp.where(tri, seg, -jnp.inf))
                L_t = jnp.exp(jnp.where(tri_t, -seg, -jnp.inf))
                dyh = dyg[:, sl]
                dG = dG + _dot_nt(dyh, xdt[:, sl]) * L
                dxd.append(_dot(Gm_t * L_t, dyh))
            dxdt_diag = jnp.concatenate(dxd, axis=1)
            dxdt = dxdt_diag + dxdt_state
            dxdt_all[:, cols] = dxdt
            colterm_all[:, cols] = xdt.astype(bf16).astype(f32) * dxdt_diag + end_term
            dxbc_ref[:, cols] = dxdt * dt_e + dyg * dskw_ref[:, cols]
            dxbc_ref[:, pl.ds(di + g * N, N)] = dB + _dot_tn(dG, Cg)
            dxbc_ref[:, pl.ds(di + G * N + g * N, N)] = dC + _dot(dG, Bg)
            escale = jnp.concatenate([jnp.broadcast_to(elast[:, h0 + r:h0 + r + 1], (P, N)) for r in range(R)], axis=0)
            dstate[pl.ds(g * gw, gw), :] = escale * dHg + dHin
        xs = xbc_ref[:, pl.ds(0, di)]
        dyv = dy_ref[...]
        yoff = yoff_all[...]
        y_diag = y_ref[...] - dskw_ref[...] * xs - yoff
        rs_y = _dot_exact(dyv.astype(bf16).astype(f32) * y_diag + dyv * yoff, hoc)
        rs_c = _dot_exact(colterm_all[...], hoc)
        rs_x = _dot_exact(dxdt_all[...] * xs, hoc)
        end_dot = _dot_exact(jnp.broadcast_to(jnp.sum(tend_all[...], 0, keepdims=True), (8, di)), hoc)[0:1]
        last = lax.broadcasted_iota(jnp.int32, (Q, 1), 0) == Q - 1
        dcs = rs_y - rs_c + jnp.where(last, end_dot + state_dot, 0.0)
        da = lax.dot_general(tri.astype(f32), dcs, (((0,), (0,)), ((), ())), preferred_element_type=f32,
                             precision=lax.Precision.HIGHEST)
        ddt = da * A + rs_x
        ddtraw = ddt * jax.nn.sigmoid(dtraw_v + bias_ref[...])
        ddt_ref[...] = ddtraw.astype(ddt_ref.dtype)
        dA_ref[...] += jnp.sum(da * dt, 0, keepdims=True) * A
        ddsk_ref[...] += jnp.sum(_dot_exact(dyv * xs, hoc), 0, keepdims=True)
        dtb_ref[...] += jnp.sum(ddtraw, 0, keepdims=True)

    vec = pl.BlockSpec((1, LANES), lambda c: (0, 0))
    rev = lambda c: (nc - 1 - c, 0)
    return pl.pallas_call(
        body, name=name, grid=(nc,),
        in_specs=[pl.BlockSpec((Q, CD), rev), pl.BlockSpec((Q, LANES), rev), vec, vec,
                  pl.BlockSpec((1, di), lambda c: (0, 0)), pl.BlockSpec((di, LANES), lambda c: (0, 0)),
                  pl.BlockSpec((None, di, N), lambda c: (nc - 1 - c, 0, 0)), pl.BlockSpec((Q, di), rev),
                  pl.BlockSpec((Q, di), rev)],
        out_specs=[pl.BlockSpec((Q, CD), rev), pl.BlockSpec((Q, LANES), rev), vec, vec, vec],
        out_shape=[_sds((S, CD)), _sds((S, LANES), bf16), _sds((1, LANES)), _sds((1, LANES)), _sds((1, LANES))],
        scratch_shapes=[pltpu.VMEM((di, N), f32)] + [pltpu.VMEM((Q, di), f32)] * 4,
        compiler_params=_params(("arbitrary",)),
    )(xbc, dtraw, dt_bias, a_log, dsk_wide, head_of_col, hin, y, dy)


def _t5_bucket_np(dist):
    max_exact = REL_BUCKETS // 2
    n = np.maximum(dist, 1).astype(np.float32)
    large = np.float32(max_exact) + np.log(n / np.float32(max_exact)) / np.float32(math.log(REL_MAX_DIST / max_exact)) * np.float32(REL_BUCKETS - max_exact)
    large = np.minimum(large.astype(np.int32), REL_BUCKETS - 1)
    return np.where(dist < max_exact, dist, large)


def _bucket_onehot():
    i = np.arange(ATT_BLK)[None, :]
    j = np.arange(2 * ATT_BLK)[:, None]
    delta = np.maximum(ATT_BLK + i - j, 0)
    out = np.zeros((len(ATT_DILATIONS), REL_BUCKETS, ATT_BLK * 2 * ATT_BLK), np.float32)
    for gi, d in enumerate(ATT_DILATIONS):
        b = _t5_bucket_np(delta * d).reshape(-1)
        out[gi, b, np.arange(b.size)] = 1.0
    return out


def _exact_mm(a, b, *, name, tb=False):
    M, K = a.shape
    N = b.shape[0] if tb else b.shape[1]
    tn = _tile(N, 4096, LANES)
    dn = (((1,), (1 if tb else 0,)), ((), ()))

    def body(a_ref, b_ref, o_ref):
        o_ref[...] = lax.dot_general(a_ref[...], b_ref[...], dn, preferred_element_type=f32,
                                     precision=lax.Precision.HIGHEST)

    return pl.pallas_call(
        body, name=name, grid=(N // tn,),
        in_specs=[pl.BlockSpec((M, K), lambda j: (0, 0)),
                  pl.BlockSpec((tn, K), lambda j: (j, 0)) if tb else pl.BlockSpec((K, tn), lambda j: (0, j))],
        out_specs=pl.BlockSpec((M, tn), lambda j: (0, j)), out_shape=_sds((M, N)),
        compiler_params=_params(("parallel",)),
    )(a, b)


def _band_penalty():
    i = np.arange(ATT_BLK)[None, :]
    j = np.arange(2 * ATT_BLK)[:, None]
    delta = ATT_BLK + i - j
    return np.where((delta >= 0) & (delta <= ATT_BLK), 0.0, -np.inf).astype(np.float32)


def _first_block_keep(n):
    key = lax.broadcasted_iota(jnp.int32, (2 * ATT_BLK, ATT_BLK), 0)
    return (key >= ATT_BLK) | (n > 0)


ATT_SCALE = HEAD_DIM ** -0.5


def _rows(ref, r, d):
    return ref[...] if d == 1 else ref[pl.ds(r, ATT_BLK, stride=d), :]


def _set_rows(ref, r, d, val):
    if d == 1:
        ref[...] = val
    else:
        ref[pl.ds(r, ATT_BLK, stride=d), :] = val


def _attn_width(d, D):
    return D if d == 1 else LANES


def _over_residues(d, one, unroll=1):
    if d == 1:
        one(0)
    else:
        lax.fori_loop(0, d, lambda r, c: (one(r), c)[1], 0, unroll=unroll)


def _attn_fwd(q, k, v, bias_t, d, name):
    S, D = q.shape
    nb = S // (d * ATT_BLK)
    H = D // HEAD_DIM
    W = _attn_width(d, D)
    HB = W // HEAD_DIM

    def body(q_ref, kp_ref, kc_ref, vp_ref, vc_ref, b_ref, o_ref, lse_ref):
        keep = _first_block_keep(pl.program_id(1))
        first = lax.broadcasted_iota(jnp.int32, (1, LANES), 1) < HEAD_DIM

        def one(r):
            qs = (_rows(q_ref, r, d) * ATT_SCALE).astype(bf16)
            kcat = jnp.concatenate([_rows(kp_ref, r, d), _rows(kc_ref, r, d)], axis=0).astype(bf16)
            vcat = jnp.concatenate([_rows(vp_ref, r, d), _rows(vc_ref, r, d)], axis=0).astype(bf16)
            outs = []
            for pair in range(W // LANES):
                ps = slice(pair * LANES, (pair + 1) * LANES)
                q2, k2, v2 = qs[:, ps], kcat[:, ps], vcat[:, ps]
                o2 = jnp.zeros((ATT_BLK, LANES), f32)
                for e in range(2):
                    h = 2 * pair + e
                    mine = first if e == 0 else jnp.logical_not(first)
                    zero = jnp.zeros((), bf16)
                    st = jnp.where(keep, _dot_nt(k2, jnp.where(mine, q2, zero)) + b_ref[h], -jnp.inf)
                    m = jnp.max(st, 0, keepdims=True)
                    pt = jnp.exp(st - m)
                    l = jnp.sum(pt, 0, keepdims=True)
                    o2 = o2 + _dot_tn(pt * (1.0 / l), jnp.where(mine, v2, zero))
                    lse_ref[r, h] = m + jnp.log(l)
                outs.append(o2)
            _set_rows(o_ref, r, d, jnp.concatenate(outs, axis=1))

        _over_residues(d, one, unroll=4)

    cur = pl.BlockSpec((ATT_BLK * d, W), lambda j, n: (n, j))
    prev = pl.BlockSpec((ATT_BLK * d, W), lambda j, n: (jnp.maximum(n - 1, 0), j))
    return pl.pallas_call(
        body, name=name, grid=(D // W, nb),
        in_specs=[cur, prev, cur, prev, cur, pl.BlockSpec((HB, 2 * ATT_BLK, ATT_BLK), lambda j, n: (j, 0, 0))],
        out_specs=[cur, pl.BlockSpec((None, d, HB, 1, LANES), lambda j, n: (n, 0, j, 0, 0))],
        out_shape=[_sds((S, D)), _sds((nb, d, H, 1, LANES))],
        compiler_params=_params(("parallel", "arbitrary")),
    )(q, k, k, v, v, bias_t)


def _from_blocks(rows, lanes=None):
    nb, d, H = rows.shape[:3]
    a = jnp.transpose(rows[:, :, :, 0, :], (0, 3, 1, 2)).reshape(nb * ATT_BLK * d, H)
    return a if lanes is None else jnp.pad(a, ((0, 0), (0, lanes - H)))


def _by_block(a, d):
    S, H = a.shape
    t = jnp.transpose(a.reshape(S // (d * ATT_BLK), ATT_BLK, d, H), (0, 2, 3, 1))
    return t[:, :, :, None, :]


def _head_sums(a, b, name):
    S, D = a.shape
    tr = _tile(S, 512, 8)
    hoc = jnp.asarray((np.arange(D)[:, None] // HEAD_DIM == np.arange(LANES)[None, :]).astype(np.float32))

    def body(a_ref, b_ref, h_ref, o_ref):
        o_ref[...] = _dot_exact(a_ref[...] * b_ref[...], h_ref[...])

    return pl.pallas_call(
        body, name=name, grid=(S // tr,),
        in_specs=[pl.BlockSpec((tr, D), lambda i: (i, 0)), pl.BlockSpec((tr, D), lambda i: (i, 0)),
                  pl.BlockSpec((D, LANES), lambda i: (0, 0))],
        out_specs=pl.BlockSpec((tr, LANES), lambda i: (i, 0)), out_shape=_sds((S, LANES)),
        compiler_params=_params(("parallel",)),
    )(a, b, hoc)


def _attn_bwd(q, k, v, bias_t, datt, lse_rows, dsum_rows, d, name):
    S, D = q.shape
    nb = S // (d * ATT_BLK)
    H = D // HEAD_DIM
    W = _attn_width(d, D)
    HB = W // HEAD_DIM

    def body(q_ref, kp_ref, kc_ref, vp_ref, vc_ref, b_ref, do_ref, lse_ref, dsum_ref,
             dq_ref, dk_ref, dv_ref, db_ref, carry_k, carry_v):
        j = pl.program_id(0)
        n = pl.program_id(1)

        @pl.when(n == 0)
        def _():
            carry_k[...] = jnp.zeros_like(carry_k)
            carry_v[...] = jnp.zeros_like(carry_v)
            db_ref[...] = jnp.zeros_like(db_ref)

        @pl.when(n < nb)
        def _():
            key = lax.broadcasted_iota(jnp.int32, (2 * ATT_BLK, ATT_BLK), 0)
            keep = (key >= ATT_BLK) | (n > 0)
            first = lax.broadcasted_iota(jnp.int32, (1, LANES), 1) < HEAD_DIM

            def one(r):
                qs = (_rows(q_ref, r, d) * ATT_SCALE).astype(bf16)
                kcat = jnp.concatenate([_rows(kp_ref, r, d), _rows(kc_ref, r, d)], axis=0).astype(bf16)
                vcat = jnp.concatenate([_rows(vp_ref, r, d), _rows(vc_ref, r, d)], axis=0).astype(bf16)
                dob = _rows(do_ref, r, d).astype(bf16)
                dqs, dks, dvs = [], [], []
                for pair in range(W // LANES):
                    ps = slice(pair * LANES, (pair + 1) * LANES)
                    q2, k2, v2, do2 = qs[:, ps], kcat[:, ps], vcat[:, ps], dob[:, ps]
                    dq2 = jnp.zeros((ATT_BLK, LANES), f32)
                    dk2 = jnp.zeros((2 * ATT_BLK, LANES), f32)
                    dv2 = jnp.zeros((2 * ATT_BLK, LANES), f32)
                    for e in range(2):
                        h = 2 * pair + e
                        mine = first if e == 0 else jnp.logical_not(first)
                        zero = jnp.zeros((), bf16)
                        qm, dom, km = jnp.where(mine, q2, zero), jnp.where(mine, do2, zero), jnp.where(mine, k2, zero)
                        st = jnp.where(keep, _dot_nt(k2, qm) + b_ref[h], -jnp.inf)
                        pt = jnp.exp(st - lse_ref[r, j * HB + h])
                        dst = pt * (_dot_nt(v2, dom) - dsum_ref[r, j * HB + h])
                        db_ref[h] += dst
                        dv2 = dv2 + _dot(pt, dom)
                        dk2 = dk2 + _dot(dst, qm)
                        dq2 = dq2 + _dot_tn(dst, km)
                    dqs.append(dq2 * ATT_SCALE)
                    dks.append(dk2)
                    dvs.append(dv2)
                _set_rows(dq_ref, r, d, jnp.concatenate(dqs, axis=1))
                dk = jnp.concatenate(dks, axis=1)
                dv = jnp.concatenate(dvs, axis=1)
                _set_rows(dk_ref, r, d, carry_k[r] + dk[:ATT_BLK])
                _set_rows(dv_ref, r, d, carry_v[r] + dv[:ATT_BLK])
                carry_k[r] = dk[ATT_BLK:]
                carry_v[r] = dv[ATT_BLK:]

            _over_residues(d, one, unroll=2)

        @pl.when(n == nb)
        def _():
            def last(r):
                _set_rows(dk_ref, r, d, carry_k[r])
                _set_rows(dv_ref, r, d, carry_v[r])

            _over_residues(d, last)

    nq = lambda n: jnp.minimum(n, nb - 1)
    cur = pl.BlockSpec((ATT_BLK * d, W), lambda j, n: (nq(n), j))
    prev = pl.BlockSpec((ATT_BLK * d, W), lambda j, n: (jnp.maximum(nq(n) - 1, 0), j))
    done = pl.BlockSpec((ATT_BLK * d, W), lambda j, n: (jnp.maximum(n - 1, 0), j))
    bspec = pl.BlockSpec((HB, 2 * ATT_BLK, ATT_BLK), lambda j, n: (j, 0, 0))
    rows = pl.BlockSpec((None, d, H, 1, LANES), lambda j, n: (nq(n), 0, 0, 0, 0))
    return pl.pallas_call(
        body, name=name, grid=(D // W, nb + 1),
        in_specs=[cur, prev, cur, prev, cur, bspec, cur, rows, rows],
        out_specs=[cur, done, done, bspec],
        out_shape=[_sds((S, D)), _sds((S, D)), _sds((S, D)), _sds((H, 2 * ATT_BLK, ATT_BLK))],
        scratch_shapes=[pltpu.VMEM((d, ATT_BLK, W), f32), pltpu.VMEM((d, ATT_BLK, W), f32)],
        compiler_params=_params(("arbitrary", "arbitrary")),
    )(q, k, k, v, v, bias_t, datt, lse_rows, dsum_rows)


def _attn_combine(os_, lses, name):
    S, D = os_[0].shape
    tr = _tile(S, 128, 16)
    head_cols = jnp.asarray((np.arange(LANES)[:, None] == np.arange(D)[None, :] // HEAD_DIM).astype(np.float32))

    def body(o0, o1, o2, l0, l1, l2, hc_ref, att_ref, attb_ref, lse_ref):
        a, b, c = l0[...], l1[...], l2[...]
        m = jnp.maximum(jnp.maximum(a, b), c)
        e0, e1, e2 = jnp.exp(a - m), jnp.exp(b - m), jnp.exp(c - m)
        tot = e0 + e1 + e2
        wide = lambda w: _dot_exact(w / tot, hc_ref[...])
        att = wide(e0) * o0[...] + wide(e1) * o1[...] + wide(e2) * o2[...]
        att_ref[...] = att
        attb_ref[...] = att.astype(bf16)
        lse_ref[...] = m + jnp.log(tot)

    wide_spec = pl.BlockSpec((tr, D), lambda i: (i, 0))
    lane_spec = pl.BlockSpec((tr, LANES), lambda i: (i, 0))
    return pl.pallas_call(
        body, name=name, grid=(S // tr,),
        in_specs=[wide_spec] * 3 + [lane_spec] * 3 + [pl.BlockSpec((LANES, D), lambda i: (0, 0))],
        out_specs=[wide_spec, wide_spec, lane_spec], out_shape=[_sds((S, D)), _sds((S, D), bf16), _sds((S, LANES))],
        compiler_params=_params(("parallel",)),
    )(*os_, *lses, head_cols)


ANY = pl.BlockSpec(memory_space=pl.ANY)


def _all_gather(vs, name):
    n = len(vs)

    def body(*refs):
        x_refs, out_refs = refs[:n], refs[n:2 * n]
        send_sems, recv_sems, local_sems = refs[2 * n:]
        x, y, c = lax.axis_index("x"), lax.axis_index("y"), lax.axis_index("c")
        me, sibling = (x, y, c), (x, y, 1 - c)
        chips = [(1 - x, y), (x, 1 - y), (1 - x, 1 - y)]

        def slot(i, px, py, pc):
            return out_refs[i].at[4 * px + 2 * py + pc]

        def copy(i, k, block, to, src=None):
            return pltpu.make_async_remote_copy(
                src_ref=slot(i, *block) if src is None else src, dst_ref=slot(i, *block),
                send_sem=send_sems.at[i, k], recv_sem=recv_sems.at[i, k], device_id=to, device_id_type=MESH)

        mine = [pltpu.make_async_copy(x_refs[i], slot(i, *me), local_sems.at[i]) for i in range(n)]
        for cp in mine:
            cp.start()
        first = []
        for i in range(n):
            first.append(copy(i, 0, me, sibling, src=x_refs[i]))
            first += [copy(i, 1 + j, me, (*chip, c), src=x_refs[i]) for j, chip in enumerate(chips)]
        for cp in first:
            cp.start()
        passed = []
        for i in range(n):
            for j, chip in enumerate(chips):
                copy(i, 1 + j, (*chip, c), me).wait_recv()
                cp = copy(i, 4 + j, (*chip, c), sibling)
                cp.start()
                passed.append(cp)
        for i in range(n):
            copy(i, 0, sibling, me).wait_recv()
            for j, chip in enumerate(chips):
                copy(i, 4 + j, (*chip, 1 - c), me).wait_recv()
        for cp in first + passed:
            cp.wait_send()
        for cp in mine:
            cp.wait()

    return pl.pallas_call(
        body, name=name, out_shape=[_sds((N_DEV,) + v.shape, v.dtype) for v in vs], in_specs=[ANY] * n,
        out_specs=[ANY] * n,
        scratch_shapes=[pltpu.SemaphoreType.DMA((n, 7)), pltpu.SemaphoreType.DMA((n, 7)), pltpu.SemaphoreType.DMA((n,))],
    )(*vs)


def _sum_slots(t, name):
    n, R, C = t.shape
    tr = _tile(R, PACK_ROW_TILE, 16)

    def body(t_ref, o_ref):
        acc = t_ref[0].astype(f32)
        for k in range(1, n):
            acc = acc + t_ref[k].astype(f32)
        o_ref[...] = acc

    return pl.pallas_call(
        body, name=name, grid=(R // tr,),
        in_specs=[pl.BlockSpec((n, tr, C), lambda i: (0, i, 0))],
        out_specs=pl.BlockSpec((tr, C), lambda i: (i, 0)), out_shape=_sds((R, C)),
        compiler_params=_params(("parallel",)),
    )(t)


HBM_SPEC = pl.BlockSpec(memory_space=pltpu.HBM)
SEM_SPEC = pl.BlockSpec(memory_space=pltpu.SEMAPHORE)
EFFECT = pltpu.SideEffectType.DATAFLOW_SIDE_EFFECTING


def _mesh_pos(p):
    return (p // 4, (p // 2) % 2, p % 2)


def _exchange_copy(src_refs, land_refs, send_sems, recv_sems, whole, dests, i, k):
    me = 4 * lax.axis_index("x") + 2 * lax.axis_index("y") + lax.axis_index("c")
    to = (me + k) % N_DEV
    frm = (me + N_DEV - k) % N_DEV
    lo, hi = dests
    src = src_refs[i] if whole else src_refs[i].at[jnp.minimum(jnp.maximum(to - lo, 0), hi - lo - 1)]
    s = i * (N_DEV - 1) + k - 1
    send = pltpu.make_async_remote_copy(src_ref=src, dst_ref=land_refs[i].at[me], send_sem=send_sems.at[s],
                                        recv_sem=recv_sems.at[s], device_id=_mesh_pos(to), device_id_type=MESH)
    recv = pltpu.make_async_remote_copy(src_ref=src, dst_ref=land_refs[i].at[frm], send_sem=send_sems.at[s],
                                        recv_sem=recv_sems.at[s], device_id=_mesh_pos(to), device_id_type=MESH)
    return send, recv, (to >= lo) & (to < hi), (me >= lo) & (me < hi)


def _exchange_start(srcs, whole, name, after=None, dests=(0, N_DEV)):
    n = len(srcs)
    lands = [lax.empty((N_DEV,) + s.shape[-2:], s.dtype) for s in srcs]
    after = list(after or [])
    n_in = 2 * n + len(after)
    everyone = dests == (0, N_DEV)

    def body(*refs):
        src_refs, land_refs = refs[:n], refs[n:2 * n]
        send_sems, recv_sems, token = refs[n_in], refs[n_in + 1], refs[-1]
        for i in range(n):
            for k in range(1, N_DEV):
                send, _, sends, _ = _exchange_copy(src_refs, land_refs, send_sems, recv_sems, whole, dests, i, k)
                if everyone:
                    send.start()
                else:
                    pl.when(sends)(send.start)
        token[...] = jnp.zeros_like(token)

    sems = pltpu.SemaphoreType.DMA((n * (N_DEV - 1),))
    outs = pl.pallas_call(
        body, name=name,
        out_shape=(sems, sems, *[pltpu.HBM(a.shape, a.dtype) for a in srcs + lands], _sds((8, LANES))),
        in_specs=[HBM_SPEC] * (2 * n) + [pl.BlockSpec(memory_space=pl.ANY)] * len(after),
        out_specs=(SEM_SPEC, SEM_SPEC, *[HBM_SPEC] * (2 * n), pl.BlockSpec(memory_space=pltpu.VMEM)),
        input_output_aliases={i: 2 + i for i in range(2 * n)},
        compiler_params=pltpu.CompilerParams(has_side_effects=EFFECT),
    )(*[pltpu.with_memory_space_constraint(a, pltpu.HBM) for a in srcs + lands], *after)
    return (outs[0], outs[1], list(outs[2:2 + n]), list(outs[2 + n:2 + 2 * n]), whole, dests), outs[-1]


def _exchange_wait(handle, after, name):
    send_sems, recv_sems, srcs, lands, whole, dests = handle
    n = len(srcs)
    everyone = dests == (0, N_DEV)

    def body(*refs):
        src_refs, land_refs = refs[:n], refs[n:2 * n]
        send_sems, recv_sems = refs[2 * n], refs[2 * n + 1]
        for i in range(n):
            for k in range(1, N_DEV):
                send, recv, sends, receives = _exchange_copy(src_refs, land_refs, send_sems, recv_sems, whole, dests, i, k)
                if everyone:
                    send.wait_send()
                    recv.wait_recv()
                else:
                    pl.when(sends)(send.wait_send)
                    pl.when(receives)(recv.wait_recv)

    outs = pl.pallas_call(
        body, name=name, out_shape=tuple(pltpu.HBM(a.shape, a.dtype) for a in srcs + lands),
        in_specs=[HBM_SPEC] * (2 * n) + [SEM_SPEC, SEM_SPEC, pl.BlockSpec(memory_space=pl.ANY)],
        out_specs=[HBM_SPEC] * (2 * n), input_output_aliases={i: i for i in range(2 * n)},
        compiler_params=pltpu.CompilerParams(has_side_effects=EFFECT),
    )(*srcs, *lands, send_sems, recv_sems, after)
    return list(outs[n:])


def _tie(v, token):
    return v + token[0:1, 0:1].astype(v.dtype).reshape((1,) * v.ndim)


def _with_own(land, own, me):
    return lax.dynamic_update_slice_in_dim(land, own[None].astype(land.dtype), me, 0)


class _Overlap:
    def __init__(self, shards, me, after):
        self.me = me
        self.names = list(shards)
        self.handle, self.token = _exchange_start([shards[nm] for nm in self.names], True, "weights_start", after)
        self.sent = {}

    def weights(self, after):
        lands = _exchange_wait(self.handle, after, "weights_wait")
        own = self.handle[2]
        return {nm: _full_from_blocks(nm, _with_own(land, o, self.me)) for nm, land, o in zip(self.names, lands, own)}

    def send(self, tag, grads):
        names = list(grads)
        handle, token = _exchange_start([_blocks_from_full(nm, grads[nm]) for nm in names], False, f"grads_start_{tag}")
        self.sent[tag] = (names, handle)
        return token

    def send_rows(self, tag, rows, dests):
        lo, hi = dests
        blocks = rows.reshape(hi - lo, rows.shape[0] // (hi - lo), rows.shape[1])
        handle, token = _exchange_start([blocks], False, f"grads_start_{tag}", None, dests)
        self.sent[tag] = ([tag], handle)
        return token

    def received(self, tag, after):
        names, handle = self.sent[tag]
        lands = _exchange_wait(handle, after, f"grads_wait_{tag}")
        lo = handle[5][0]
        own = [lax.dynamic_index_in_dim(b, self.me - lo, 0, keepdims=False) for b in handle[2]]
        return {nm: _with_own(land, o, self.me) for nm, land, o in zip(names, lands, own)}


ADAM_ROWS = 32


def _adamw(w, g, m, v, name):
    deep = w.ndim == 3
    R, C = w.shape[0], w.shape[-1]
    cb = LANES if C % LANES == 0 else C
    n_parts = g.shape[0] if g.ndim == 3 else 0

    def body(w_ref, g_ref, m_ref, v_ref, d_ref, m2_ref, v2_ref, *g_out):
        at = (lambda ref, sl: ref.at[sl, 0, :]) if deep else (lambda ref, sl: ref.at[sl, :])

        def update(sl):
            if n_parts:
                gv = g_ref[0, sl, :].astype(f32)
                for k in range(1, n_parts):
                    gv = gv + g_ref[k, sl, :].astype(f32)
                at(g_out[0], sl)[...] = gv
            else:
                gv = g_ref[sl, :]
            m2 = ADAM_B1 * at(m_ref, sl)[...] + (1.0 - ADAM_B1) * gv
            v2 = ADAM_B2 * at(v_ref, sl)[...] + (1.0 - ADAM_B2) * jnp.square(gv)
            m_hat = m2 / (1.0 - ADAM_B1 ** ADAM_STEP)
            v_hat = v2 / (1.0 - ADAM_B2 ** ADAM_STEP)
            at(d_ref, sl)[...] = -ADAM_LR * (m_hat / (jnp.sqrt(v_hat) + ADAM_EPS) + ADAM_WD * at(w_ref, sl)[...])
            at(m2_ref, sl)[...] = m2
            at(v2_ref, sl)[...] = v2

        main = R // ADAM_ROWS
        if main:
            lax.fori_loop(0, main, lambda i, c: (update(pl.ds(pl.multiple_of(i * ADAM_ROWS, ADAM_ROWS), ADAM_ROWS)), c)[1], 0)
        if R % ADAM_ROWS:
            update(pl.ds(main * ADAM_ROWS, R % ADAM_ROWS))

    spec = pl.BlockSpec((R, 1, cb), lambda j: (0, 0, j)) if deep else pl.BlockSpec((R, cb), lambda j: (0, j))
    g_spec = pl.BlockSpec((n_parts, R, cb), lambda j: (0, 0, j)) if n_parts else pl.BlockSpec((R, cb), lambda j: (0, j))
    n_out = 4 if n_parts else 3
    return pl.pallas_call(
        body, name=name, grid=(C // cb,), in_specs=[spec, g_spec, spec, spec], out_specs=[spec] * n_out,
        out_shape=[_sds(w.shape)] * n_out, compiler_params=_params(("parallel",)),
    )(w, g, m, v)


BIG_PARAMS = ("hy_w_in", "hy_w_out", "cv_w_pw1", "cv_w_pw2", "ffn_w_gate", "ffn_w_up", "ffn_w_down")


def _shards_2d(w):
    t = lambda a: jnp.transpose(a)
    return dict(in_t=t(w["hy_w_in"][0]), out=w["hy_w_out"][0], pw1=w["cv_w_pw1"][0], pw2=w["cv_w_pw2"][0],
                gate_t0=t(w["ffn_w_gate"][0]), gate_t1=t(w["ffn_w_gate"][1]), up_t0=t(w["ffn_w_up"][0]),
                up_t1=t(w["ffn_w_up"][1]), down0=w["ffn_w_down"][0], down1=w["ffn_w_down"][1])


def _unshard_2d(s):
    t = lambda a: jnp.transpose(a)
    out = dict(hy_w_out=s["out"][None], cv_w_pw1=s["pw1"][None], cv_w_pw2=s["pw2"][None],
               ffn_w_gate=jnp.stack([t(s["gate_t0"]), t(s["gate_t1"])]),
               ffn_w_up=jnp.stack([t(s["up_t0"]), t(s["up_t1"])]), ffn_w_down=jnp.stack([s["down0"], s["down1"]]))
    if "in_t" in s:
        out["hy_w_in"] = t(s["in_t"])[None]
    return out


def _full_from_blocks(nm, g):
    if nm == "pw1":
        return jnp.transpose(g, (1, 0, 2)).reshape(g.shape[1], N_DEV * g.shape[2])
    return g.reshape(N_DEV * g.shape[1], g.shape[2])


def _blocks_from_full(nm, g):
    if nm == "pw1":
        return jnp.transpose(g.reshape(g.shape[0], N_DEV, g.shape[1] // N_DEV), (1, 0, 2))
    return g.reshape(N_DEV, g.shape[0] // N_DEV, g.shape[1])


class _VecPack:
    def __init__(self, shapes):
        self.shapes = [tuple(s) for s in shapes]
        self.sizes = [int(np.prod(s)) for s in self.shapes]
        total = sum(self.sizes)
        self.rows = -(-(-(-total // LANES)) // 8) * 8
        self.total = total

    def pack(self, arrays):
        flat = jnp.concatenate([a.astype(f32).reshape(-1) for a in arrays])
        flat = jnp.pad(flat, (0, self.rows * LANES - self.total))
        return flat.reshape(self.rows, LANES)

    def unpack(self, packed):
        flat = packed.reshape(-1)
        out, off = [], 0
        for shp, n in zip(self.shapes, self.sizes):
            out.append(flat[off:off + n].reshape(shp))
            off += n
        return out

    def unpack_stacked(self, stacked, only=None):
        flat = stacked.reshape(stacked.shape[0], -1)
        offs = np.concatenate([[0], np.cumsum(self.sizes)])
        get = lambda i: flat[:, offs[i]:offs[i + 1]].reshape((stacked.shape[0],) + self.shapes[i])
        return get(only) if only is not None else [get(i) for i in range(len(self.shapes))]


def _row(v):
    return v.reshape(1, -1)


def _pad_lanes(v):
    v = v.reshape(1, -1)
    return jnp.pad(v, ((0, 0), (0, LANES - v.shape[1])))


def _ffn_fwd(h, w_gate_t, w_up_t, w_down, tag):
    F = w_down.shape[0]
    a = _mm(h, w_gate_t, tb=True, out_dtype=bf16, name=f"ffn_gate_{tag}")
    u = _mm(h, w_up_t, tb=True, out_dtype=bf16, name=f"ffn_up_{tag}")
    (f,), _ = _rowwise(f"swiglu_{tag}", lambda rv, vv: ([_silu(rv[0].astype(f32)) * rv[1].astype(f32)], []), [a, u], [],
                       [(F, bf16)], [], sub=16, col_chunk=_tile(F, 512, LANES))
    out = _mm(f, w_down, name=f"ffn_down_{tag}")
    return out, (a, u, f)


def _ffn_bwd(h, w_gate_t, w_up_t, w_down, saved, dout, tag):
    a, u, f = saved
    F = w_down.shape[0]
    df = _mm(dout, w_down, tb=True, out_dtype=bf16, name=f"ffn_down_dx_{tag}")
    dw_down = _mm(f, dout, ta=True, out_dtype=bf16, name=f"ffn_down_dw_{tag}")

    def fn(rv, vv):
        _, vjp = jax.vjp(lambda a_, u_: _silu(a_) * u_, rv[0].astype(f32), rv[1].astype(f32))
        da, du = vjp(rv[2].astype(f32))
        return [da, du], []

    (da, du), _ = _rowwise(f"swiglu_bwd_{tag}", fn, [a, u, df], [], [(F, bf16), (F, bf16)], [], sub=16,
                           col_chunk=_tile(F, 512, LANES))
    dh = _mm(du, w_up_t, add=_mm(da, w_gate_t, name=f"ffn_gate_dx_{tag}"), name=f"ffn_up_dx_{tag}")
    dw_gate_t = _mm(da, h, ta=True, out_dtype=bf16, name=f"ffn_gate_dw_{tag}")
    dw_up_t = _mm(du, h, ta=True, out_dtype=bf16, name=f"ffn_up_dw_{tag}")
    return dh, dw_gate_t, dw_up_t, dw_down


def _local_step(x, target, mod, w_in_t, comm, small):
    S, D = x.shape
    di = small["hy_ssm_norm_g"].shape[-1]
    nh = small["hy_dt_bias"].shape[-1]
    cd = small["hy_conv_b"].shape[-1]
    m = [[_row(mod[i, j]) for j in range(6)] for i in range(2)]

    off_q = di + cd + nh
    w_qkv_t = w_in_t[off_q:]
    seg = dict(z=(w_in_t, 0, di), xbc=(w_in_t, di, cd), dt=(w_in_t, di + cd, LANES))
    for i, nm in enumerate(("q0", "q1", "q2", "k", "v")):
        seg[nm] = (w_qkv_t, i * D, D)

    g_mix = [_row(small["norm_mix_g"][i]) for i in range(2)]
    g_ffn = [_row(small["norm_ffn_g"][i]) for i in range(2)]
    conv_w, conv_b = small["hy_conv_w_full"], _row(small["hy_conv_b"][0])
    dt_bias, a_log, d_skip = (_pad_lanes(small[k][0]) for k in ("hy_dt_bias", "hy_a_log", "hy_d_skip"))
    g_ssm = _row(small["hy_ssm_norm_g"][0])
    onehot = jnp.asarray(_bucket_onehot())
    rel_t = small["rel_table"].T
    H = D // HEAD_DIM
    bias = [_exact_mm(rel_t[gi * H:(gi + 1) * H], onehot[gi], name=f"rel_bias_{gi}")
            .reshape(H, 2 * ATT_BLK, ATT_BLK) + _band_penalty() for gi in range(3)]

    h1 = _adaln_fwd(x, g_mix[0], m[0][1], m[0][0], "adaln_mix0")
    proj = {nm: _mm(h1, mat, tb=True, b_rows=(off, cnt), name=f"in_{nm}") for nm, (mat, off, cnt) in seg.items()}
    xbc_pre, xbc = _conv_fwd(proj["xbc"], conv_w, conv_b, silu=True, name="ssm_conv", tr=1024)
    y, hin = _ssd_fwd(xbc, proj["dt"], dt_bias, a_log, d_skip, di, "ssd_fwd")
    (yg,), _ = _rowwise("ssm_gate", lambda rv, vv: ([_gate_f(rv[0], rv[1], vv[0])], []),
                        [y, proj["z"]], [g_ssm], [(di, bf16)], [], sub=16)
    og = [_attn_fwd(proj[f"q{gi}"], proj["k"], proj["v"], bias[gi], d, f"attn_fwd_{gi}")
          for gi, d in enumerate(ATT_DILATIONS)]
    att, att_b, lse_tot = _attn_combine([a for a, _ in og], [_from_blocks(b, LANES) for _, b in og], "attn_combine")
    W = comm.weights(after=att_b)
    w_out_y, w_out_a = W["out"][:di], W["out"][di:]
    mix0 = _mm(att_b, w_out_a, add=_mm(yg, w_out_y, name="out_y"), name="out_a")
    x1, h2 = _resid_adaln_fwd(x, m[0][2], mix0, g_ffn[0], m[0][4], m[0][3], "resid_mix0_adaln_ffn0")
    f0, ffn0_saved = _ffn_fwd(h2, W["gate_t0"], W["up_t0"], W["down0"], "0")
    x2, h3 = _resid_adaln_fwd(x1, m[0][5], f0, g_mix[1], m[1][1], m[1][0], "resid_ffn0_adaln_mix1")
    pw1 = _mm(h3, W["pw1"], bias=_row(small["cv_b_pw1_full"]), name="cv_pw1")
    (u,), _ = _rowwise("cv_glu", lambda rv, vv: ([rv[0] * jax.nn.sigmoid(rv[1])], []),
                       [(pw1, 0, D), (pw1, 1, D)], [], [(D, f32)], [])
    (u2,) = _conv_fwd(u, small["cv_w_dw_full"], _row(small["cv_b_dw_full"]), silu=False, name="cv_dw")
    ln_g, ln_b = _row(small["cv_ln_g_full"]), _row(small["cv_ln_b_full"])
    (u3,), _ = _rowwise("cv_lnsilu", lambda rv, vv: ([_lnsilu_f(rv[0], vv[0], vv[1])], []),
                        [u2], [ln_g, ln_b], [(D, bf16)], [], sub=16)
    mix1 = _mm(u3, W["pw2"], bias=_row(small["cv_b_pw2_full"]), name="cv_pw2")
    x3, h4 = _resid_adaln_fwd(x2, m[1][2], mix1, g_ffn[1], m[1][4], m[1][3], "resid_mix1_adaln_ffn1")
    f1, ffn1_saved = _ffn_fwd(h4, W["gate_t1"], W["up_t1"], W["down1"], "1")

    g_fin = _row(small["final_norm_g"])
    dmod = [[None] * 6 for _ in range(2)]
    d_norm_mix, d_norm_ffn = [None, None], [None, None]
    big = {}

    def final_fn(rv, vv):
        xv, fv, tv = rv
        gate = vv[1]
        yv, vjp = jax.vjp(_rms, xv + gate * fv, vv[0])
        err = yv - tv
        dx, dg = vjp(err / D)
        part = 0.5 * jnp.sum(jnp.mean(err * err, -1, keepdims=True), 0, keepdims=True)
        return [dx, gate * dx], [dg, jnp.broadcast_to(part, (1, LANES)), jnp.sum(dx * fv, 0, keepdims=True)]

    (dx4, df1), (d_fin, loss, dmod[1][5]) = _rowwise("loss_head", final_fn, [x3, f1, target], [g_fin, m[1][5]],
                                                      [(D, f32), (D, bf16)], [D, LANES, D], sub=16)

    dh4, big["gate_t1"], big["up_t1"], big["down1"] = _ffn_bwd(h4, W["gate_t1"], W["up_t1"], W["down1"], ffn1_saved, df1, "1")
    dx3, dmix1, (d_norm_ffn[1], dmod[1][4], dmod[1][3], dmod[1][2], d_b_pw2) = _adaln_resid_bwd(
        x3, g_ffn[1], m[1][4], m[1][3], dh4, dx4, mix1, m[1][2], "adaln_ffn1_resid_mix1_bwd")
    du3 = _mm(dmix1, W["pw2"], tb=True, name="cv_pw2_dx")
    big["pw2"] = _mm(u3, dmix1, ta=True, out_dtype=bf16, name="cv_pw2_dw")

    def lnsilu_bwd(rv, vv):
        _, vjp = jax.vjp(_lnsilu_f, rv[0], vv[0], vv[1])
        du, dg, db = vjp(rv[1])
        return [du], [dg, db]

    (du2,), (d_ln_g, d_ln_b) = _rowwise("cv_lnsilu_bwd", lnsilu_bwd, [u2, du3], [ln_g, ln_b], [(D, f32)], [D, D])
    du, d_w_dw, d_b_dw = _conv_bwd(u, small["cv_w_dw_full"], du2, None, silu=False, name="cv_dw_bwd")

    def glu_bwd(rv, vv):
        a, gt, d = rv
        _, vjp = jax.vjp(lambda a_, g_: a_ * jax.nn.sigmoid(g_), a, gt)
        da, dg = vjp(d)
        return [da, dg], [jnp.sum(da, 0, keepdims=True), jnp.sum(dg, 0, keepdims=True)]

    (dpa, dpg), (d_b1a, d_b1g) = _rowwise("cv_glu_bwd", glu_bwd, [(pw1, 0, D), (pw1, 1, D), du], [],
                                           [(D, bf16), (D, bf16)], [D, D], sub=16)
    dpw1 = jnp.concatenate([dpa, dpg], axis=1)
    d_b_pw1 = jnp.concatenate([d_b1a, d_b1g], axis=1)
    dh3 = _mm(dpw1, W["pw1"], tb=True, name="cv_pw1_dx")
    big["pw1"] = _mm(h3, dpw1, ta=True, out_dtype=bf16, name="cv_pw1_dw")
    token = comm.send("layer1", {nm: big[nm] for nm in ("gate_t1", "up_t1", "down1", "pw2", "pw1")})
    dx2, df0, (d_norm_mix[1], dmod[1][1], dmod[1][0], dmod[0][5], _) = _adaln_resid_bwd(
        x2, g_mix[1], m[1][1], _tie(m[1][0], token), dh3, dx3, f0, m[0][5], "adaln_mix1_resid_ffn0_bwd")

    dh2, big["gate_t0"], big["up_t0"], big["down0"] = _ffn_bwd(h2, W["gate_t0"], W["up_t0"], W["down0"], ffn0_saved, df0, "0")
    dx1, dmix0, (d_norm_ffn[0], dmod[0][4], dmod[0][3], dmod[0][2], _) = _adaln_resid_bwd(
        x1, g_ffn[0], m[0][4], m[0][3], dh2, dx2, mix0, m[0][2], "adaln_ffn0_resid_mix0_bwd")
    dyg = _mm(dmix0, w_out_y, tb=True, name="out_y_dx")
    datt = _mm(dmix0, w_out_a, tb=True, name="out_a_dx")
    big["out"] = jnp.concatenate([_mm(yg, dmix0, ta=True, out_dtype=bf16, name="out_y_dw"),
                                  _mm(att_b, dmix0, ta=True, out_dtype=bf16, name="out_a_dw")], axis=0)
    token = comm.send("layer0", {nm: big[nm] for nm in ("gate_t0", "up_t0", "down0", "out")})
    g_ssm = _tie(g_ssm, token)

    def gate_bwd(rv, vv):
        _, vjp = jax.vjp(_gate_f, rv[0], rv[1], vv[0])
        dy_, dz_, dg_ = vjp(rv[2])
        return [dy_, dz_], [dg_]

    (dy, dz), (d_g_ssm,) = _rowwise("ssm_gate_bwd", gate_bwd, [y, proj["z"], dyg], [g_ssm], [(di, f32), (di, bf16)], [di],
                                    sub=16)
    dxbc, ddtraw, d_a_log, d_dskip, d_dt_bias = _ssd_bwd(xbc, proj["dt"], dt_bias, a_log, d_skip, hin, y, dy, di, "ssd_bwd")
    dxbc_pre, d_conv_w, d_conv_b = _conv_bwd(proj["xbc"], conv_w, dxbc, xbc_pre, silu=True, name="ssm_conv_bwd",
                                             dx_dtype=bf16, tr=1024)
    dh1 = None
    early = []
    for nm, dseg in (("z", dz), ("xbc", dxbc_pre), ("dt", ddtraw)):
        mat, off, cnt = seg[nm]
        dh1 = _mm(dseg, mat, b_rows=(off, cnt), add=dh1, name=f"in_{nm}_dx")
        dwp = _mm(dseg, h1, ta=True, out_dtype=bf16, name=f"in_{nm}_dw")
        early.append(dwp[:nh] if nm == "dt" else dwp)
    early = jnp.concatenate(early, axis=0)
    shard_rows = w_in_t.shape[0] // N_DEV
    n_early = off_q // shard_rows
    token = comm.send_rows("in_early", early[:n_early * shard_rows], (0, n_early))
    bias = [_tie(b, token) for b in bias]

    dq, dks, dvs, dbs = [], [], [], []
    lse_heads = lse_tot[:, :H]
    dsum_heads = _head_sums(att, datt, "attn_dsum")[:, :H]
    for gi, d in enumerate(ATT_DILATIONS):
        a, b, c_, e = _attn_bwd(proj[f"q{gi}"], proj["k"], proj["v"], bias[gi], datt,
                                _by_block(lse_heads, d), _by_block(dsum_heads, d), d, f"attn_bwd_{gi}")
        dq.append(a)
        dks.append(b)
        dvs.append(c_)
        dbs.append(e)
    dk = _add3(*dks, "attn_dk")
    dv = _add3(*dvs, "attn_dv")
    d_rel = jnp.concatenate(
        [_exact_mm(dbs[gi].reshape(H, -1), onehot[gi], tb=True, name=f"rel_grad_{gi}") for gi in range(3)], axis=0).T

    dsegs = (("q0", dq[0]), ("q1", dq[1]), ("q2", dq[2]), ("k", dk), ("v", dv))
    late = jnp.concatenate([early[n_early * shard_rows:]] +
                           [_mm(dseg, h1, ta=True, out_dtype=bf16, name=f"in_{nm}_dw") for nm, dseg in dsegs], axis=0)
    token = comm.send_rows("in_late", late, (n_early, N_DEV))
    w_qkv_after = _tie(w_qkv_t, token)
    for nm, dseg in dsegs:
        _, off, cnt = seg[nm]
        dh1 = _mm(dseg, w_qkv_after, b_rows=(off, cnt), add=dh1, name=f"in_{nm}_dx")
    dx0, (d_norm_mix[0], dmod[0][1], dmod[0][0]) = _adaln_bwd(x, g_mix[0], m[0][1], m[0][0], dh1, dx1, "adaln_mix0_bwd")

    smallg = dict(
        loss=loss, dmod=jnp.stack([jnp.concatenate(dmod[i], axis=1)[0] for i in range(2)]),
        norm_mix_g=jnp.concatenate(d_norm_mix, axis=0), norm_ffn_g=jnp.concatenate(d_norm_ffn, axis=0),
        hy_conv_w=d_conv_w, hy_conv_b=d_conv_b, hy_dt_bias=d_dt_bias[:, :nh], hy_a_log=d_a_log[:, :nh],
        hy_d_skip=d_dskip[:, :nh], hy_ssm_norm_g=d_g_ssm, rel_table=d_rel,
        cv_b_pw1=d_b_pw1, cv_w_dw=d_w_dw, cv_b_dw=d_b_dw, cv_ln_g=d_ln_g, cv_ln_b=d_ln_b, cv_b_pw2=d_b_pw2,
        final_norm_g=d_fin)
    return dx0, n_early, smallg


SMALL_GRAD_ORDER = ("loss", "dmod", "norm_mix_g", "norm_ffn_g", "hy_conv_w", "hy_conv_b", "hy_dt_bias", "hy_a_log",
                    "hy_d_skip", "hy_ssm_norm_g", "rel_table", "cv_b_pw1", "cv_w_dw", "cv_b_dw", "cv_ln_g", "cv_ln_b",
                    "cv_b_pw2", "final_norm_g")


def kernel(x, c, ada_w, ada_b, norm_mix_g, norm_ffn_g, hy_w_in, hy_conv_w, hy_conv_b, hy_dt_bias, hy_a_log, hy_d_skip, hy_ssm_norm_g, hy_w_out, rel_table, cv_w_pw1, cv_b_pw1, cv_w_dw, cv_b_dw, cv_ln_g, cv_ln_b, cv_w_pw2, cv_b_pw2, ffn_w_gate, ffn_w_up, ffn_w_down, final_norm_g, loss_target, m_ada_w, m_ada_b, m_norm_mix_g, m_norm_ffn_g, m_hy_w_in, m_hy_conv_w, m_hy_conv_b, m_hy_dt_bias, m_hy_a_log, m_hy_d_skip, m_hy_ssm_norm_g, m_hy_w_out, m_rel_table, m_cv_w_pw1, m_cv_b_pw1, m_cv_w_dw, m_cv_b_dw, m_cv_ln_g, m_cv_ln_b, m_cv_w_pw2, m_cv_b_pw2, m_ffn_w_gate, m_ffn_w_up, m_ffn_w_down, m_final_norm_g, v_ada_w, v_ada_b, v_norm_mix_g, v_norm_ffn_g, v_hy_w_in, v_hy_conv_w, v_hy_conv_b, v_hy_dt_bias, v_hy_a_log, v_hy_d_skip, v_hy_ssm_norm_g, v_hy_w_out, v_rel_table, v_cv_w_pw1, v_cv_b_pw1, v_cv_w_dw, v_cv_b_dw, v_cv_ln_g, v_cv_ln_b, v_cv_w_pw2, v_cv_b_pw2, v_ffn_w_gate, v_ffn_w_up, v_ffn_w_down, v_final_norm_g):
    names = ("ada_w", "ada_b", "norm_mix_g", "norm_ffn_g", "hy_w_in", "hy_conv_w", "hy_conv_b", "hy_dt_bias", "hy_a_log",
             "hy_d_skip", "hy_ssm_norm_g", "hy_w_out", "rel_table", "cv_w_pw1", "cv_b_pw1", "cv_w_dw", "cv_b_dw", "cv_ln_g",
             "cv_ln_b", "cv_w_pw2", "cv_b_pw2", "ffn_w_gate", "ffn_w_up", "ffn_w_down", "final_norm_g")
    w = dict(zip(names, (ada_w, ada_b, norm_mix_g, norm_ffn_g, hy_w_in, hy_conv_w, hy_conv_b, hy_dt_bias, hy_a_log, hy_d_skip,
                         hy_ssm_norm_g, hy_w_out, rel_table, cv_w_pw1, cv_b_pw1, cv_w_dw, cv_b_dw, cv_ln_g, cv_ln_b, cv_w_pw2,
                         cv_b_pw2, ffn_w_gate, ffn_w_up, ffn_w_down, final_norm_g)))
    mom = dict(zip(names, (m_ada_w, m_ada_b, m_norm_mix_g, m_norm_ffn_g, m_hy_w_in, m_hy_conv_w, m_hy_conv_b, m_hy_dt_bias,
                           m_hy_a_log, m_hy_d_skip, m_hy_ssm_norm_g, m_hy_w_out, m_rel_table, m_cv_w_pw1, m_cv_b_pw1, m_cv_w_dw,
                           m_cv_b_dw, m_cv_ln_g, m_cv_ln_b, m_cv_w_pw2, m_cv_b_pw2, m_ffn_w_gate, m_ffn_w_up, m_ffn_w_down,
                           m_final_norm_g)))
    vel = dict(zip(names, (v_ada_w, v_ada_b, v_norm_mix_g, v_norm_ffn_g, v_hy_w_in, v_hy_conv_w, v_hy_conv_b, v_hy_dt_bias,
                           v_hy_a_log, v_hy_d_skip, v_hy_ssm_norm_g, v_hy_w_out, v_rel_table, v_cv_w_pw1, v_cv_b_pw1, v_cv_w_dw,
                           v_cv_b_dw, v_cv_ln_g, v_cv_ln_b, v_cv_w_pw2, v_cv_b_pw2, v_ffn_w_gate, v_ffn_w_up, v_ffn_w_down,
                           v_final_norm_g)))
    S, D = x.shape[1], x.shape[2]
    ax, ay, ac = lax.axis_index("x"), lax.axis_index("y"), lax.axis_index("c")
    me = 4 * ax + 2 * ay + ac
    nmod = ada_w.shape[2]

    w2 = _shards_2d(w)
    big_names = list(w2)
    sharded_small = ("hy_conv_w", "cv_b_pw1", "cv_w_dw", "cv_b_dw", "cv_ln_g", "cv_ln_b", "cv_b_pw2")
    vp = _VecPack([c.shape] + [w[nm].shape for nm in sharded_small])
    g_in, sg = _all_gather([w2["in_t"].astype(bf16), vp.pack([c] + [w[nm] for nm in sharded_small])], "gather_w_in")
    w_in_t = _full_from_blocks("in_t", g_in)
    parts = vp.unpack_stacked(sg)
    c_all = parts[0][:, 0]
    small = {k: w[k] for k in ("norm_mix_g", "norm_ffn_g", "hy_conv_b", "hy_dt_bias", "hy_a_log", "hy_d_skip",
                               "hy_ssm_norm_g", "rel_table", "final_norm_g")}
    for p, nm in zip(parts[1:], sharded_small):
        p = p[:, 0]
        p = jnp.moveaxis(p, 0, -2)
        small[nm + "_full"] = p.reshape(p.shape[:-2] + (N_DEV * p.shape[-1],))

    (cs_all,), _ = _rowwise("ada_silu", lambda rv, vv: ([_silu(rv[0])], []), [c_all], [], [(D, f32)], [])
    b_mine = lax.dynamic_slice_in_dim(ada_b, me * nmod, nmod, axis=1)
    mod_part = jnp.stack([_mm(cs_all, ada_w[i], bias=b_mine[i:i + 1], name=f"ada_mod_{i}") for i in range(2)])
    (mod_all,) = _all_gather([mod_part.reshape(2 * N_DEV, nmod)], "gather_mod")
    mod_all = mod_all.reshape(N_DEV, 2, N_DEV, nmod)
    mod_mine = lax.dynamic_index_in_dim(mod_all, me, axis=2, keepdims=False)
    mod = jnp.transpose(mod_mine, (1, 0, 2)).reshape(2, 6, D)
    comm = _Overlap({nm: w2[nm].astype(bf16) for nm in big_names if nm != "in_t"}, me, after=[mod, w_in_t])
    mod = _tie(mod, comm.token)

    dx0, n_early, sgrad = _local_step(x[0], loss_target[0], mod, w_in_t, comm, small)

    gp = _VecPack([sgrad[k].shape for k in SMALL_GRAD_ORDER])
    small_handle, after = _exchange_start([gp.pack([sgrad[k] for k in SMALL_GRAD_ORDER])], True, "small_grads_start")
    m2, v2 = _shards_2d(mom), _shards_2d(vel)
    g2, d2, nm2, nv2 = {}, {}, {}, {}
    slots = {}
    for tag in ("layer1", "layer0"):
        slots.update(comm.received(tag, after))
        for nm in comm.sent[tag][0]:
            d2[nm], nm2[nm], nv2[nm], g2[nm] = _adamw(w2[nm], slots[nm], m2[nm], v2[nm], f"adamw_{nm}")
            after = g2[nm]
    (g_all,) = _exchange_wait(small_handle, after, "small_grads_wait")
    g_all = _with_own(g_all, small_handle[2][0], me)
    tot = dict(zip(SMALL_GRAD_ORDER, gp.unpack(_sum_slots(g_all, "sum_small_grads"))))
    dmod_all = gp.unpack_stacked(g_all, only=SMALL_GRAD_ORDER.index("dmod"))
    loss = tot["loss"][0, 0]

    grads = {}
    dmod_mine = lax.dynamic_slice_in_dim(dmod_all, me * nmod, nmod, axis=2)
    grads["ada_w"] = jnp.stack([_mm(cs_all, dmod_mine[:, i], ta=True, name=f"ada_w_grad_{i}") for i in range(2)])
    grads["ada_b"] = tot["dmod"]
    grads["norm_mix_g"], grads["norm_ffn_g"] = tot["norm_mix_g"], tot["norm_ffn_g"]
    grads["hy_conv_b"] = tot["hy_conv_b"]
    grads["hy_dt_bias"] = tot["hy_dt_bias"]
    grads["hy_a_log"] = tot["hy_a_log"]
    grads["hy_d_skip"] = tot["hy_d_skip"]
    grads["hy_ssm_norm_g"] = tot["hy_ssm_norm_g"]
    grads["rel_table"] = tot["rel_table"]
    grads["final_norm_g"] = tot["final_norm_g"][0]
    for nm in sharded_small:
        n = w[nm].shape[-1]
        grads[nm] = lax.dynamic_slice_in_dim(tot[nm], me * n, n, axis=1).reshape(w[nm].shape)

    delta, new_m, new_v = {}, {}, {}
    shp = ada_w.shape
    two = lambda t: t.reshape(-1, shp[-1])
    d_, m_, v_ = _adamw(two(ada_w), two(grads["ada_w"]), two(m_ada_w), two(v_ada_w), "adamw_ada_w")
    delta["ada_w"], new_m["ada_w"], new_v["ada_w"] = d_.reshape(shp), m_.reshape(shp), v_.reshape(shp)
    rest = [nm for nm in names if nm not in BIG_PARAMS and nm != "ada_w"]
    sp = _VecPack([w[nm].shape for nm in rest])
    packs = [sp.pack([t[nm] for nm in rest]) for t in (w, grads, mom, vel)]
    ds_, ms_, vs_ = _adamw(*packs, "adamw_small")
    for nm, a, b, e in zip(rest, sp.unpack(ds_), sp.unpack(ms_), sp.unpack(vs_)):
        delta[nm], new_m[nm], new_v[nm] = a, b, e

    slots.update(comm.received("in_early", ds_))
    slots.update(comm.received("in_late", slots["in_early"]))
    stored = lambda t: jnp.transpose(t, (2, 0, 1))
    in_slots = jnp.where(me < n_early, slots["in_early"], slots["in_late"])
    d_in, m_in, v_in, g_in = _adamw(stored(hy_w_in), in_slots, stored(m_hy_w_in), stored(v_hy_w_in), "adamw_in_t")
    for dst, part, t in ((grads, g2, g_in), (delta, d2, d_in), (new_m, nm2, m_in), (new_v, nv2, v_in)):
        dst.update(_unshard_2d(part))
        dst["hy_w_in"] = jnp.transpose(t, (1, 2, 0))

    return (loss, dx0[None], *[grads[n] for n in names], *[delta[n] for n in names],
            *[new_m[n] for n in names], *[new_v[n] for n in names])
```

```python
import functools
import math

import numpy as np
import jax
import jax.numpy as jnp
from jax import lax
from jax.experimental import pallas as pl
from jax.experimental.pallas import tpu as pltpu

f32 = jnp.float32
bf16 = jnp.bfloat16
EPS = 1e-6
N_DEV = 8
LANES = 128
SSM_STATE = 128
SSM_CHUNK = 128
SSM_GROUPS = 4
HEAD_DIM = 64
ATT_BLK = 128
ATT_DILATIONS = (1, 4, 16)
REL_BUCKETS = 32
REL_MAX_DIST = 2048
ADAM_LR, ADAM_B1, ADAM_B2, ADAM_EPS, ADAM_WD, ADAM_STEP = 0.001, 0.9, 0.999, 1e-08, 0.01, 10
PACK_ROW_TILE = 256
MESH = pl.DeviceIdType.MESH
VMEM_LIMIT = 48 * 1024 * 1024


def _sds(shape, dtype=f32):
    return jax.ShapeDtypeStruct(tuple(shape), dtype)


def _tile(n, cap, mult):
    best = None
    t = mult
    while t <= min(n, cap):
        if n % t == 0:
            best = t
        t += mult
    return best if best is not None else n


def _params(sem):
    return pltpu.CompilerParams(dimension_semantics=sem, vmem_limit_bytes=VMEM_LIMIT)


def _mm(a, b, *, name, ta=False, tb=False, b_rows=None, bias=None, add=None, out_dtype=f32,
        tm_cap=512, tn_cap=1536, tk_cap=8192):
    if ta:
        K, M = a.shape
    else:
        M, K = a.shape
    off, cnt = b_rows if b_rows is not None else (0, b.shape[0])
    if tb:
        N, K2 = cnt, b.shape[1]
    else:
        K2, N = cnt, b.shape[1]
    assert K == K2, (a.shape, b.shape, ta, tb, b_rows)
    if ta and a.dtype == f32:
        tm_cap = min(tm_cap, 256)
    tm = _tile(M, tm_cap, LANES)
    tn = _tile(math.gcd(off, N) if tb else N, tn_cap, LANES)
    tk = _tile(K if tb else math.gcd(off, K), tk_cap, LANES)
    assert N % tn == 0 and K % tk == 0 and off % (tn if tb else tk) == 0, (name, off, N, K, tn, tk)
    nk = K // tk
    jo, ko = (off // tn, 0) if tb else (0, off // tk)
    has_bias, has_add = bias is not None, add is not None
    dn = (((0 if ta else 1,), (1 if tb else 0,)), ((), ()))

    def body(*refs):
        a_ref, b_ref = refs[0], refs[1]
        pos = 2
        bias_ref = add_ref = None
        if has_bias:
            bias_ref = refs[pos]
            pos += 1
        if has_add:
            add_ref = refs[pos]
            pos += 1
        o_ref = refs[pos]
        k = pl.program_id(2)
        part = lax.dot_general(a_ref[...].astype(bf16), b_ref[...].astype(bf16), dn, preferred_element_type=f32)

        def finish(r):
            if has_bias:
                r = r + bias_ref[...]
            if has_add:
                r = r + add_ref[...]
            o_ref[...] = r.astype(o_ref.dtype)

        if nk == 1:
            finish(part)
        else:
            acc_ref = refs[pos + 1]

            @pl.when(k == 0)
            def _():
                acc_ref[...] = part

            @pl.when((k > 0) & (k < nk - 1))
            def _():
                acc_ref[...] += part

            @pl.when(k == nk - 1)
            def _():
                finish(acc_ref[...] + part)

    in_specs = [
        pl.BlockSpec((tk, tm), lambda i, j, k: (k, i)) if ta else pl.BlockSpec((tm, tk), lambda i, j, k: (i, k)),
        pl.BlockSpec((tn, tk), lambda i, j, k: (j + jo, k)) if tb else pl.BlockSpec((tk, tn), lambda i, j, k: (k + ko, j)),
    ]
    args = [a, b]
    if has_bias:
        in_specs.append(pl.BlockSpec((1, tn), lambda i, j, k: (0, j)))
        args.append(bias)
    if has_add:
        in_specs.append(pl.BlockSpec((tm, tn), lambda i, j, k: (i, j)))
        args.append(add)
    return pl.pallas_call(
        body, name=name, grid=(M // tm, N // tn, nk), in_specs=in_specs,
        out_specs=pl.BlockSpec((tm, tn), lambda i, j, k: (i, j)), out_shape=_sds((M, N), out_dtype),
        scratch_shapes=[pltpu.VMEM((tm, tn), f32)] if nk > 1 else [],
        compiler_params=_params(("parallel", "parallel", "arbitrary")),
    )(*args)


def _rowwise(name, fn, rows, vecs, out_rows, out_accs, *, tr_cap=256, sub=8, col_chunk=None):
    rows = [r if isinstance(r, tuple) else (r, 0, r.shape[1]) for r in rows]
    R = rows[0][0].shape[0]
    tr = _tile(R, tr_cap, 8)
    sub = sub if tr % sub == 0 else tr
    n_r, n_v, n_or, n_oa = len(rows), len(vecs), len(out_rows), len(out_accs)

    def body(*refs):
        row_refs = refs[:n_r]
        vec_refs = refs[n_r:n_r + n_v]
        orow_refs = refs[n_r + n_v:n_r + n_v + n_or]
        oacc_refs = refs[n_r + n_v + n_or:]
        vv = [r[...] for r in vec_refs]

        n_sub = tr // sub
        together = 4 if n_sub % 4 == 0 else 1

        def step(s, accs):
            for t in range(together):
                sl = pl.ds(pl.multiple_of((s * together + t) * sub, sub), sub)
                if col_chunk is None:
                    ro, ao = fn([r[sl, :] for r in row_refs], vv)
                    for o_ref, o in zip(orow_refs, ro):
                        o_ref[sl, :] = o.astype(o_ref.dtype)
                    accs = tuple(x + y for x, y in zip(accs, ao))
                else:
                    for c0 in range(0, rows[0][2], col_chunk):
                        cs_ = pl.ds(c0, col_chunk)
                        ro, _ = fn([r[sl, cs_] for r in row_refs], vv)
                        for o_ref, o in zip(orow_refs, ro):
                            o_ref[sl, cs_] = o.astype(o_ref.dtype)
            return accs

        accs = lax.fori_loop(0, n_sub // together, step, tuple(jnp.zeros((1, w), f32) for w in out_accs))
        if n_oa:
            @pl.when(pl.program_id(0) == 0)
            def _():
                for ref in oacc_refs:
                    ref[...] = jnp.zeros_like(ref)

            for ref, x in zip(oacc_refs, accs):
                ref[...] += x

    in_specs = [pl.BlockSpec((tr, w), functools.partial(lambda i, cb: (i, cb), cb=cb)) for (_, cb, w) in rows]
    in_specs += [pl.BlockSpec((1, v.shape[1]), lambda i: (0, 0)) for v in vecs]
    out_specs = [pl.BlockSpec((tr, w), lambda i: (i, 0)) for (w, _) in out_rows]
    out_specs += [pl.BlockSpec((1, w), lambda i: (0, 0)) for w in out_accs]
    out_shape = [_sds((R, w), dt) for (w, dt) in out_rows] + [_sds((1, w)) for w in out_accs]
    res = pl.pallas_call(
        body, name=name, grid=(R // tr,), in_specs=in_specs, out_specs=out_specs, out_shape=out_shape,
        compiler_params=_params(("arbitrary",)),
    )(*[r[0] for r in rows], *vecs)
    return res[:n_or], res[n_or:]


def _silu(x):
    return x * jax.nn.sigmoid(x)


def _rms(x, g):
    return x * lax.rsqrt(jnp.mean(x * x, -1, keepdims=True) + EPS) * g


def _adaln_f(x, g, sc, sh):
    return _rms(x, g) * (1.0 + sc) + sh


def _gate_f(y, z, g):
    return _rms(y * _silu(z), g)


def _lnsilu_f(u, g, b):
    mu = jnp.mean(u, -1, keepdims=True)
    var = jnp.mean(jnp.square(u - mu), -1, keepdims=True)
    return _silu((u - mu) * lax.rsqrt(var + EPS) * g + b)


def _adaln_fwd(x, g, sc, sh, name):
    (h,), _ = _rowwise(name, lambda rv, vv: ([_adaln_f(rv[0], *vv)], []), [x], [g, sc, sh], [(x.shape[1], bf16)], [],
                       sub=16)
    return h


def _adaln_bwd(x, g, sc, sh, dh, dres, name):
    def fn(rv, vv):
        xv, dhv, drv = rv
        _, vjp = jax.vjp(_adaln_f, xv, *vv)
        dx, dg, dsc, dsh = vjp(dhv)
        return [dx + drv], [dg, dsc, dsh]
    w = x.shape[1]
    (dx,), accs = _rowwise(name, fn, [x, dh, dres], [g, sc, sh], [(w, f32)], [w, w, w])
    return dx, accs


def _resid_adaln_fwd(x, gate, mix, g, sc, sh, name):
    def fn(rv, vv):
        xn = rv[0] + vv[0] * rv[1]
        return [xn, _adaln_f(xn, vv[1], vv[2], vv[3])], []
    w = x.shape[1]
    (xn, h), _ = _rowwise(name, fn, [x, mix], [gate, g, sc, sh], [(w, f32), (w, bf16)], [], sub=16)
    return xn, h


def _adaln_resid_bwd(x, g, sc, sh, dh, dres, mix, gate, name):
    def fn(rv, vv):
        xv, dhv, drv, mv = rv
        _, vjp = jax.vjp(_adaln_f, xv, vv[0], vv[1], vv[2])
        dx, dg, dsc, dsh = vjp(dhv)
        dx = dx + drv
        dm = vv[3] * dx
        return [dx, dm], [dg, dsc, dsh, jnp.sum(dx * mv, 0, keepdims=True), jnp.sum(dm, 0, keepdims=True)]
    w = x.shape[1]
    (dx, dmix), accs = _rowwise(name, fn, [x, dh, dres, mix], [g, sc, sh, gate], [(w, f32), (w, bf16)], [w] * 5, sub=16)
    return dx, dmix, accs


def _add3(a, b, c, name):
    (y,), _ = _rowwise(name, lambda rv, vv: ([rv[0] + rv[1] + rv[2]], []), [a, b, c], [], [(a.shape[1], bf16)], [],
                       sub=16)
    return y


CONV_HALO = 32
CONV_ROWS = 64


def _conv_fwd(x, w, b, *, silu, name, tr=512):
    S, C = x.shape
    K = w.shape[0]
    H = CONV_HALO
    assert K - 1 <= H and S % tr == 0 and tr % H == 0 and C % LANES == 0
    nh = tr // H

    def body(xp_ref, xc_ref, w_ref, b_ref, *rest):
        outs, scr = rest[:-1], rest[-1]
        i = pl.program_id(1)
        scr[pl.ds(0, H), :] = jnp.where(i > 0, xp_ref[...], 0.0)
        scr[pl.ds(H, tr), :] = xc_ref[...]
        taps = [w_ref[pl.ds(k, 1), :] for k in range(K)]
        for c0 in range(0, tr, CONV_ROWS):
            acc = jnp.zeros((CONV_ROWS, LANES), f32) + b_ref[...]
            for k in range(K):
                acc = acc + scr[pl.ds(c0 + H - (K - 1) + k, CONV_ROWS), :] * taps[k]
            outs[0][pl.ds(c0, CONV_ROWS), :] = acc.astype(outs[0].dtype)
            if silu:
                outs[1][pl.ds(c0, CONV_ROWS), :] = _silu(acc)

    n_out = 2 if silu else 1
    return pl.pallas_call(
        body, name=name, grid=(C // LANES, S // tr),
        in_specs=[pl.BlockSpec((H, LANES), lambda j, i: (jnp.maximum(i * nh - 1, 0), j)),
                  pl.BlockSpec((tr, LANES), lambda j, i: (i, j)),
                  pl.BlockSpec((K, LANES), lambda j, i: (0, j)),
                  pl.BlockSpec((1, LANES), lambda j, i: (0, j))],
        out_specs=[pl.BlockSpec((tr, LANES), lambda j, i: (i, j))] * n_out,
        out_shape=[_sds((S, C), bf16), _sds((S, C))] if silu else [_sds((S, C))],
        scratch_shapes=[pltpu.VMEM((tr + H, LANES), f32)],
        compiler_params=_params(("parallel", "arbitrary")),
    )(x, x, w, b)


def _conv_bwd(x, w, dact, pre, *, silu, name, dx_dtype=f32, tr=512):
    S, C = x.shape
    K = w.shape[0]
    H = CONV_HALO
    nh = tr // H
    n_i = S // tr
    kp = -(-K // 8) * 8

    def dsilu(p):
        s = jax.nn.sigmoid(p)
        return s * (1.0 + p * (1.0 - s))

    def body(*refs):
        if silu:
            xp_ref, xc_ref, w_ref, dc_ref, dn_ref, pc_ref, pn_ref, dx_ref, dw_ref, db_ref, xs, ds = refs
        else:
            xp_ref, xc_ref, w_ref, dc_ref, dn_ref, dx_ref, dw_ref, db_ref, xs, ds = refs
        i = pl.program_id(1)
        xs[pl.ds(0, H), :] = jnp.where(i > 0, xp_ref[...], 0.0)
        xs[pl.ds(H, tr), :] = xc_ref[...]
        dcur = dc_ref[...]
        dnext = dn_ref[...]
        if silu:
            dcur = dcur * dsilu(pc_ref[...].astype(f32))
            dnext = dnext * dsilu(pn_ref[...].astype(f32))
        ds[pl.ds(0, tr), :] = dcur
        ds[pl.ds(tr, H), :] = jnp.where(i < n_i - 1, dnext, 0.0)
        taps = [w_ref[pl.ds(k, 1), :] for k in range(K)]
        fold = lambda t: jnp.sum(t.reshape(CONV_ROWS // 8, 8, LANES), axis=0)
        dw_parts = [jnp.zeros((8, LANES), f32) for _ in range(K)]
        db_part = jnp.zeros((8, LANES), f32)
        for c0 in range(0, tr, CONV_ROWS):
            acc = jnp.zeros((CONV_ROWS, LANES), f32)
            d_c = ds[pl.ds(c0, CONV_ROWS), :]
            for k in range(K):
                acc = acc + ds[pl.ds(c0 + K - 1 - k, CONV_ROWS), :] * taps[k]
                dw_parts[k] = dw_parts[k] + fold(d_c * xs[pl.ds(c0 + H - (K - 1) + k, CONV_ROWS), :])
            db_part = db_part + fold(d_c)
            dx_ref[pl.ds(c0, CONV_ROWS), :] = acc.astype(dx_ref.dtype)

        @pl.when(i == 0)
        def _():
            dw_ref[...] = jnp.zeros_like(dw_ref)
            db_ref[...] = jnp.zeros_like(db_ref)

        for k in range(K):
            dw_ref[pl.ds(k, 1), :] += jnp.sum(dw_parts[k], 0, keepdims=True)
        db_ref[...] += jnp.sum(db_part, 0, keepdims=True)

    prev = pl.BlockSpec((H, LANES), lambda j, i: (jnp.maximum(i * nh - 1, 0), j))
    cur = pl.BlockSpec((tr, LANES), lambda j, i: (i, j))
    nxt = pl.BlockSpec((H, LANES), lambda j, i: (jnp.minimum((i + 1) * nh, n_i * nh - 1), j))
    in_specs = [prev, cur, pl.BlockSpec((K, LANES), lambda j, i: (0, j)), cur, nxt]
    args = [x, x, w, dact, dact]
    if silu:
        in_specs += [cur, nxt]
        args += [pre, pre]
    dx, dw, db = pl.pallas_call(
        body, name=name, grid=(C // LANES, n_i), in_specs=in_specs,
        out_specs=[cur, pl.BlockSpec((kp, LANES), lambda j, i: (0, j)), pl.BlockSpec((1, LANES), lambda j, i: (0, j))],
        out_shape=[_sds((S, C), dx_dtype), _sds((kp, C)), _sds((1, C))],
        scratch_shapes=[pltpu.VMEM((tr + H, LANES), f32), pltpu.VMEM((tr + H, LANES), f32)],
        compiler_params=_params(("parallel", "arbitrary")),
    )(*args)
    return dx, dw[:K], db


def _dot(a, b):
    return jnp.dot(a.astype(bf16), b.astype(bf16), preferred_element_type=f32)


def _dot_nt(a, b):
    return lax.dot_general(a.astype(bf16), b.astype(bf16), (((1,), (1,)), ((), ())), preferred_element_type=f32)


def _dot_tn(a, b):
    return lax.dot_general(a.astype(bf16), b.astype(bf16), (((0,), (0,)), ((), ())), preferred_element_type=f32)


def _softplus(x):
    return jnp.maximum(x, 0.0) + jnp.log(1.0 + jnp.exp(-jnp.abs(x)))


def _tri(q):
    i = lax.broadcasted_iota(jnp.int32, (q, q), 0)
    j = lax.broadcasted_iota(jnp.int32, (q, q), 1)
    return i >= j


def _ssd_prep(dtraw, dt_bias, a_log):
    q = dtraw.shape[0]
    dt = _softplus(dtraw + dt_bias)
    A = -jnp.exp(a_log)
    tri = _tri(q)
    cs = jnp.dot(tri.astype(f32), dt * A, preferred_element_type=f32, precision=lax.Precision.HIGHEST)
    return dt, A, cs, cs.T, tri


def _expand(cols, h0, n, width):
    q = cols.shape[0]
    return jnp.concatenate([jnp.broadcast_to(cols[:, h0 + r:h0 + r + 1], (q, width)) for r in range(n)], axis=1)


def _ssd_fwd(xbc, dtraw, dt_bias, a_log, d_skip, di, name):
    S, CD = xbc.shape
    Q, N, G = SSM_CHUNK, SSM_STATE, SSM_GROUPS
    nc = S // Q
    nh = di // HEAD_DIM
    R = nh // G
    gw = R * HEAD_DIM
    col_of_head = jnp.asarray((np.arange(LANES)[:, None] == np.arange(di)[None, :] // HEAD_DIM).astype(np.float32))
    dsk_wide = jnp.repeat(d_skip[0, :nh], HEAD_DIM)[None]

    def body(xbc_ref, dt_ref, bias_ref, alog_ref, dskw_ref, coh_ref, y_ref, hin_ref, state):
        c = pl.program_id(0)

        @pl.when(c == 0)
        def _():
            state[...] = jnp.zeros_like(state)

        hin_ref[...] = state[...]
        dt, A, cs, csT, tri = _ssd_prep(dt_ref[...], bias_ref[...], alog_ref[...])
        elast = jnp.exp(cs[Q - 1:Q, :])
        coh = coh_ref[...]
        dt_w, ecs_w, dend_w = _dot_exact(dt, coh), _dot_exact(jnp.exp(cs), coh), _dot_exact(jnp.exp(cs[Q - 1:Q, :] - cs), coh)
        for g in range(G):
            h0 = g * R
            cols = pl.ds(g * gw, gw)
            lanes = slice(g * gw, (g + 1) * gw)
            Bg = xbc_ref[:, pl.ds(di + g * N, N)]
            Cg = xbc_ref[:, pl.ds(di + G * N + g * N, N)]
            xg = xbc_ref[:, cols]
            Hg = state[cols, :]
            Gm = _dot_nt(Cg, Bg)
            xdt = xg * dt_w[:, lanes]
            yoff = _dot_nt(Cg, Hg) * ecs_w[:, lanes]
            ys = []
            for r in range(R):
                h = h0 + r
                L = jnp.exp(jnp.where(tri, cs[:, h:h + 1] - csT[h:h + 1, :], -jnp.inf))
                ys.append(_dot(Gm * L, xdt[:, r * HEAD_DIM:(r + 1) * HEAD_DIM]))
            y_ref[:, cols] = jnp.concatenate(ys, axis=1) + yoff + xg * dskw_ref[:, cols]
            hnew = _dot_tn(xdt * dend_w[:, lanes], Bg)
            escale = jnp.concatenate([jnp.broadcast_to(elast[:, h0 + r:h0 + r + 1], (HEAD_DIM, N)) for r in range(R)], axis=0)
            state[cols, :] = escale * Hg + hnew

    vec = pl.BlockSpec((1, LANES), lambda c: (0, 0))
    return pl.pallas_call(
        body, name=name, grid=(nc,),
        in_specs=[pl.BlockSpec((Q, CD), lambda c: (c, 0)), pl.BlockSpec((Q, LANES), lambda c: (c, 0)), vec, vec,
                  pl.BlockSpec((1, di), lambda c: (0, 0)), pl.BlockSpec((LANES, di), lambda c: (0, 0))],
        out_specs=[pl.BlockSpec((Q, di), lambda c: (c, 0)), pl.BlockSpec((None, di, N), lambda c: (c, 0, 0))],
        out_shape=[_sds((S, di)), _sds((nc, di, N))],
        scratch_shapes=[pltpu.VMEM((di, N), f32)],
        compiler_params=_params(("arbitrary",)),
    )(xbc, dtraw, dt_bias, a_log, dsk_wide, col_of_head)


def _dot_exact(a, b):
    bb = b.astype(bf16)
    hi = a.astype(bf16)
    rest = a - hi.astype(f32)
    mid = rest.astype(bf16)
    low = (rest - mid.astype(f32)).astype(bf16)
    one_pass = lambda t: jnp.dot(t, bb, preferred_element_type=f32)
    return one_pass(hi) + one_pass(mid) + one_pass(low)


def _ssd_bwd(xbc, dtraw, dt_bias, a_log, d_skip, hin, y, dy, di, name):
    S, CD = xbc.shape
    Q, N, G = SSM_CHUNK, SSM_STATE, SSM_GROUPS
    nc = S // Q
    nh = di // HEAD_DIM
    R = nh // G
    gw = R * HEAD_DIM
    P = HEAD_DIM
    head_of_col = jnp.asarray((np.arange(di)[:, None] // P == np.arange(LANES)[None, :]).astype(np.float32))
    dsk_wide = jnp.repeat(d_skip[0, :nh], P)[None]

    def body(xbc_ref, dt_ref, bias_ref, alog_ref, dskw_ref, hoc_ref, hin_ref, y_ref, dy_ref,
             dxbc_ref, ddt_ref, dA_ref, ddsk_ref, dtb_ref, dstate, dxdt_all, tend_all, yoff_all, colterm_all):
        c = pl.program_id(0)

        @pl.when(c == 0)
        def _():
            dstate[...] = jnp.zeros_like(dstate)
            dA_ref[...] = jnp.zeros_like(dA_ref)
            ddsk_ref[...] = jnp.zeros_like(ddsk_ref)
            dtb_ref[...] = jnp.zeros_like(dtb_ref)

        dtraw_v = dt_ref[...]
        dt, A, cs, csT, tri = _ssd_prep(dtraw_v, bias_ref[...], alog_ref[...])
        tri_t = jnp.logical_not(tri) | (lax.broadcasted_iota(jnp.int32, (Q, Q), 0) == lax.broadcasted_iota(jnp.int32, (Q, Q), 1))
        ecs = jnp.exp(cs)
        dend = jnp.exp(cs[Q - 1:Q, :] - cs)
        elast = jnp.exp(cs[Q - 1:Q, :])
        hoc = hoc_ref[...]
        state_dot = jnp.sum(_dot_exact(dstate[...] * hin_ref[...], jnp.ones((N, LANES), f32)) * hoc, 0, keepdims=True) * elast
        for g in range(G):
            h0 = g * R
            Bg = xbc_ref[:, pl.ds(di + g * N, N)]
            Cg = xbc_ref[:, pl.ds(di + G * N + g * N, N)]
            xg = xbc_ref[:, pl.ds(g * gw, gw)]
            dyg = dy_ref[:, pl.ds(g * gw, gw)]
            Hg = hin_ref[pl.ds(g * gw, gw), :]
            dHg = dstate[pl.ds(g * gw, gw), :]
            dt_e = _expand(dt, h0, R, P)
            ecs_e = _expand(ecs, h0, R, P)
            dend_e = _expand(dend, h0, R, P)
            cols = pl.ds(g * gw, gw)
            Gm = _dot_nt(Cg, Bg)
            Gm_t = _dot_nt(Bg, Cg)
            xdt = xg * dt_e
            dye = dyg * ecs_e
            bdh = _dot_nt(Bg, dHg)
            dC = _dot(dye, Hg)
            dB = _dot(xdt * dend_e, dHg)
            dHin = _dot_tn(dye, Cg)
            dxdt_state = dend_e * bdh
            end_term = xdt * dxdt_state
            tend_all[:, cols] = end_term
            yoff_all[:, cols] = _dot_nt(Cg, Hg) * ecs_e
            dG = jnp.zeros((Q, Q), f32)
            dxd = []
            for r in range(R):
                h = h0 + r
                sl = slice(r * P, (r + 1) * P)
                seg = cs[:, h:h + 1] - csT[h:h + 1, :]
                L = jnp.exp(jnp.where(tri, seg, -jnp.inf))
                L_t = jnp.exp(jnp.where(tri_t, -seg, -jnp.inf))
                dyh = dyg[:, sl]
                dG = dG + _dot_nt(dyh, xdt[:, sl]) * L
                dxd.append(_dot(Gm_t * L_t, dyh))
            dxdt_diag = jnp.concatenate(dxd, axis=1)
            dxdt = dxdt_diag + dxdt_state
            dxdt_all[:, cols] = dxdt
            colterm_all[:, cols] = xdt.astype(bf16).astype(f32) * dxdt_diag + end_term
            dxbc_ref[:, cols] = dxdt * dt_e + dyg * dskw_ref[:, cols]
            dxbc_ref[:, pl.ds(di + g * N, N)] = dB + _dot_tn(dG, Cg)
            dxbc_ref[:, pl.ds(di + G * N + g * N, N)] = dC + _dot(dG, Bg)
            escale = jnp.concatenate([jnp.broadcast_to(elast[:, h0 + r:h0 + r + 1], (P, N)) for r in range(R)], axis=0)
            dstate[pl.ds(g * gw, gw), :] = escale * dHg + dHin
        xs = xbc_ref[:, pl.ds(0, di)]
        dyv = dy_ref[...]
        yoff = yoff_all[...]
        y_diag = y_ref[...] - dskw_ref[...] * xs - yoff
        rs_y = _dot_exact(dyv.astype(bf16).astype(f32) * y_diag + dyv * yoff, hoc)
        rs_c = _dot_exact(colterm_all[...], hoc)
        rs_x = _dot_exact(dxdt_all[...] * xs, hoc)
        end_dot = _dot_exact(jnp.broadcast_to(jnp.sum(tend_all[...], 0, keepdims=True), (8, di)), hoc)[0:1]
        last = lax.broadcasted_iota(jnp.int32, (Q, 1), 0) == Q - 1
        dcs = rs_y - rs_c + jnp.where(last, end_dot + state_dot, 0.0)
        da = lax.dot_general(tri.astype(f32), dcs, (((0,), (0,)), ((), ())), preferred_element_type=f32,
                             precision=lax.Precision.HIGHEST)
        ddt = da * A + rs_x
        ddtraw = ddt * jax.nn.sigmoid(dtraw_v + bias_ref[...])
        ddt_ref[...] = ddtraw.astype(ddt_ref.dtype)
        dA_ref[...] += jnp.sum(da * dt, 0, keepdims=True) * A
        ddsk_ref[...] += jnp.sum(_dot_exact(dyv * xs, hoc), 0, keepdims=True)
        dtb_ref[...] += jnp.sum(ddtraw, 0, keepdims=True)

    vec = pl.BlockSpec((1, LANES), lambda c: (0, 0))
    rev = lambda c: (nc - 1 - c, 0)
    return pl.pallas_call(
        body, name=name, grid=(nc,),
        in_specs=[pl.BlockSpec((Q, CD), rev), pl.BlockSpec((Q, LANES), rev), vec, vec,
                  pl.BlockSpec((1, di), lambda c: (0, 0)), pl.BlockSpec((di, LANES), lambda c: (0, 0)),
                  pl.BlockSpec((None, di, N), lambda c: (nc - 1 - c, 0, 0)), pl.BlockSpec((Q, di), rev),
                  pl.BlockSpec((Q, di), rev)],
        out_specs=[pl.BlockSpec((Q, CD), rev), pl.BlockSpec((Q, LANES), rev), vec, vec, vec],
        out_shape=[_sds((S, CD)), _sds((S, LANES), bf16), _sds((1, LANES)), _sds((1, LANES)), _sds((1, LANES))],
        scratch_shapes=[pltpu.VMEM((di, N), f32)] + [pltpu.VMEM((Q, di), f32)] * 4,
        compiler_params=_params(("arbitrary",)),
    )(xbc, dtraw, dt_bias, a_log, dsk_wide, head_of_col, hin, y, dy)


def _t5_bucket_np(dist):
    max_exact = REL_BUCKETS // 2
    n = np.maximum(dist, 1).astype(np.float32)
    large = np.float32(max_exact) + np.log(n / np.float32(max_exact)) / np.float32(math.log(REL_MAX_DIST / max_exact)) * np.float32(REL_BUCKETS - max_exact)
    large = np.minimum(large.astype(np.int32), REL_BUCKETS - 1)
    return np.where(dist < max_exact, dist, large)


def _bucket_onehot():
    i = np.arange(ATT_BLK)[None, :]
    j = np.arange(2 * ATT_BLK)[:, None]
    delta = np.maximum(ATT_BLK + i - j, 0)
    out = np.zeros((len(ATT_DILATIONS), REL_BUCKETS, ATT_BLK * 2 * ATT_BLK), np.float32)
    for gi, d in enumerate(ATT_DILATIONS):
        b = _t5_bucket_np(delta * d).reshape(-1)
        out[gi, b, np.arange(b.size)] = 1.0
    return out


def _exact_mm(a, b, *, name, tb=False):
    M, K = a.shape
    N = b.shape[0] if tb else b.shape[1]
    tn = _tile(N, 4096, LANES)
    dn = (((1,), (1 if tb else 0,)), ((), ()))

    def body(a_ref, b_ref, o_ref):
        o_ref[...] = lax.dot_general(a_ref[...], b_ref[...], dn, preferred_element_type=f32,
                                     precision=lax.Precision.HIGHEST)

    return pl.pallas_call(
        body, name=name, grid=(N // tn,),
        in_specs=[pl.BlockSpec((M, K), lambda j: (0, 0)),
                  pl.BlockSpec((tn, K), lambda j: (j, 0)) if tb else pl.BlockSpec((K, tn), lambda j: (0, j))],
        out_specs=pl.BlockSpec((M, tn), lambda j: (0, j)), out_shape=_sds((M, N)),
        compiler_params=_params(("parallel",)),
    )(a, b)


def _band_penalty():
    i = np.arange(ATT_BLK)[None, :]
    j = np.arange(2 * ATT_BLK)[:, None]
    delta = ATT_BLK + i - j
    return np.where((delta >= 0) & (delta <= ATT_BLK), 0.0, -np.inf).astype(np.float32)


def _first_block_keep(n):
    key = lax.broadcasted_iota(jnp.int32, (2 * ATT_BLK, ATT_BLK), 0)
    return (key >= ATT_BLK) | (n > 0)


ATT_SCALE = HEAD_DIM ** -0.5


def _rows(ref, r, d):
    return ref[...] if d == 1 else ref[pl.ds(r, ATT_BLK, stride=d), :]


def _set_rows(ref, r, d, val):
    if d == 1:
        ref[...] = val
    else:
        ref[pl.ds(r, ATT_BLK, stride=d), :] = val


def _attn_width(d, D):
    return D if d == 1 else LANES


def _over_residues(d, one, unroll=1):
    if d == 1:
        one(0)
    else:
        lax.fori_loop(0, d, lambda r, c: (one(r), c)[1], 0, unroll=unroll)


def _attn_fwd(q, k, v, bias_t, d, name):
    S, D = q.shape
    nb = S // (d * ATT_BLK)
    H = D // HEAD_DIM
    W = _attn_width(d, D)
    HB = W // HEAD_DIM

    def body(q_ref, kp_ref, kc_ref, vp_ref, vc_ref, b_ref, o_ref, lse_ref):
        keep = _first_block_keep(pl.program_id(1))
        first = lax.broadcasted_iota(jnp.int32, (1, LANES), 1) < HEAD_DIM

        def one(r):
            qs = (_rows(q_ref, r, d) * ATT_SCALE).astype(bf16)
            kcat = jnp.concatenate([_rows(kp_ref, r, d), _rows(kc_ref, r, d)], axis=0).astype(bf16)
            vcat = jnp.concatenate([_rows(vp_ref, r, d), _rows(vc_ref, r, d)], axis=0).astype(bf16)
            outs = []
            for pair in range(W // LANES):
                ps = slice(pair * LANES, (pair + 1) * LANES)
                q2, k2, v2 = qs[:, ps], kcat[:, ps], vcat[:, ps]
                o2 = jnp.zeros((ATT_BLK, LANES), f32)
                for e in range(2):
                    h = 2 * pair + e
                    mine = first if e == 0 else jnp.logical_not(first)
                    zero = jnp.zeros((), bf16)
                    st = jnp.where(keep, _dot_nt(k2, jnp.where(mine, q2, zero)) + b_ref[h], -jnp.inf)
                    m = jnp.max(st, 0, keepdims=True)
                    pt = jnp.exp(st - m)
                    l = jnp.sum(pt, 0, keepdims=True)
                    o2 = o2 + _dot_tn(pt * (1.0 / l), jnp.where(mine, v2, zero))
                    lse_ref[r, h] = m + jnp.log(l)
                outs.append(o2)
            _set_rows(o_ref, r, d, jnp.concatenate(outs, axis=1))

        _over_residues(d, one, unroll=4)

    cur = pl.BlockSpec((ATT_BLK * d, W), lambda j, n: (n, j))
    prev = pl.BlockSpec((ATT_BLK * d, W), lambda j, n: (jnp.maximum(n - 1, 0), j))
    return pl.pallas_call(
        body, name=name, grid=(D // W, nb),
        in_specs=[cur, prev, cur, prev, cur, pl.BlockSpec((HB, 2 * ATT_BLK, ATT_BLK), lambda j, n: (j, 0, 0))],
        out_specs=[cur, pl.BlockSpec((None, d, HB, 1, LANES), lambda j, n: (n, 0, j, 0, 0))],
        out_shape=[_sds((S, D)), _sds((nb, d, H, 1, LANES))],
        compiler_params=_params(("parallel", "arbitrary")),
    )(q, k, k, v, v, bias_t)


def _from_blocks(rows, lanes=None):
    nb, d, H = rows.shape[:3]
    a = jnp.transpose(rows[:, :, :, 0, :], (0, 3, 1, 2)).reshape(nb * ATT_BLK * d, H)
    return a if lanes is None else jnp.pad(a, ((0, 0), (0, lanes - H)))


def _by_block(a, d):
    S, H = a.shape
    t = jnp.transpose(a.reshape(S // (d * ATT_BLK), ATT_BLK, d, H), (0, 2, 3, 1))
    return t[:, :, :, None, :]


def _head_sums(a, b, name):
    S, D = a.shape
    tr = _tile(S, 512, 8)
    hoc = jnp.asarray((np.arange(D)[:, None] // HEAD_DIM == np.arange(LANES)[None, :]).astype(np.float32))

    def body(a_ref, b_ref, h_ref, o_ref):
        o_ref[...] = _dot_exact(a_ref[...] * b_ref[...], h_ref[...])

    return pl.pallas_call(
        body, name=name, grid=(S // tr,),
        in_specs=[pl.BlockSpec((tr, D), lambda i: (i, 0)), pl.BlockSpec((tr, D), lambda i: (i, 0)),
                  pl.BlockSpec((D, LANES), lambda i: (0, 0))],
        out_specs=pl.BlockSpec((tr, LANES), lambda i: (i, 0)), out_shape=_sds((S, LANES)),
        compiler_params=_params(("parallel",)),
    )(a, b, hoc)


def _attn_bwd(q, k, v, bias_t, datt, lse_rows, dsum_rows, d, name):
    S, D = q.shape
    nb = S // (d * ATT_BLK)
    H = D // HEAD_DIM
    W = _attn_width(d, D)
    HB = W // HEAD_DIM

    def body(q_ref, kp_ref, kc_ref, vp_ref, vc_ref, b_ref, do_ref, lse_ref, dsum_ref,
             dq_ref, dk_ref, dv_ref, db_ref, carry_k, carry_v):
        j = pl.program_id(0)
        n = pl.program_id(1)

        @pl.when(n == 0)
        def _():
            carry_k[...] = jnp.zeros_like(carry_k)
            carry_v[...] = jnp.zeros_like(carry_v)
            db_ref[...] = jnp.zeros_like(db_ref)

        @pl.when(n < nb)
        def _():
            key = lax.broadcasted_iota(jnp.int32, (2 * ATT_BLK, ATT_BLK), 0)
            keep = (key >= ATT_BLK) | (n > 0)
            first = lax.broadcasted_iota(jnp.int32, (1, LANES), 1) < HEAD_DIM

            def one(r):
                qs = (_rows(q_ref, r, d) * ATT_SCALE).astype(bf16)
                kcat = jnp.concatenate([_rows(kp_ref, r, d), _rows(kc_ref, r, d)], axis=0).astype(bf16)
                vcat = jnp.concatenate([_rows(vp_ref, r, d), _rows(vc_ref, r, d)], axis=0).astype(bf16)
                dob = _rows(do_ref, r, d).astype(bf16)
                dqs, dks, dvs = [], [], []
                for pair in range(W // LANES):
                    ps = slice(pair * LANES, (pair + 1) * LANES)
                    q2, k2, v2, do2 = qs[:, ps], kcat[:, ps], vcat[:, ps], dob[:, ps]
                    dq2 = jnp.zeros((ATT_BLK, LANES), f32)
                    dk2 = jnp.zeros((2 * ATT_BLK, LANES), f32)
                    dv2 = jnp.zeros((2 * ATT_BLK, LANES), f32)
                    for e in range(2):
                        h = 2 * pair + e
                        mine = first if e == 0 else jnp.logical_not(first)
                        zero = jnp.zeros((), bf16)
                        qm, dom, km = jnp.where(mine, q2, zero), jnp.where(mine, do2, zero), jnp.where(mine, k2, zero)
                        st = jnp.where(keep, _dot_nt(k2, qm) + b_ref[h], -jnp.inf)
                        pt = jnp.exp(st - lse_ref[r, j * HB + h])
                        dst = pt * (_dot_nt(v2, dom) - dsum_ref[r, j * HB + h])
                        db_ref[h] += dst
                        dv2 = dv2 + _dot(pt, dom)
                        dk2 = dk2 + _dot(dst, qm)
                        dq2 = dq2 + _dot_tn(dst, km)
                    dqs.append(dq2 * ATT_SCALE)
                    dks.append(dk2)
                    dvs.append(dv2)
                _set_rows(dq_ref, r, d, jnp.concatenate(dqs, axis=1))
                dk = jnp.concatenate(dks, axis=1)
                dv = jnp.concatenate(dvs, axis=1)
                _set_rows(dk_ref, r, d, carry_k[r] + dk[:ATT_BLK])
                _set_rows(dv_ref, r, d, carry_v[r] + dv[:ATT_BLK])
                carry_k[r] = dk[ATT_BLK:]
                carry_v[r] = dv[ATT_BLK:]

            _over_residues(d, one, unroll=2)

        @pl.when(n == nb)
        def _():
            def last(r):
                _set_rows(dk_ref, r, d, carry_k[r])
                _set_rows(dv_ref, r, d, carry_v[r])

            _over_residues(d, last)

    nq = lambda n: jnp.minimum(n, nb - 1)
    cur = pl.BlockSpec((ATT_BLK * d, W), lambda j, n: (nq(n), j))
    prev = pl.BlockSpec((ATT_BLK * d, W), lambda j, n: (jnp.maximum(nq(n) - 1, 0), j))
    done = pl.BlockSpec((ATT_BLK * d, W), lambda j, n: (jnp.maximum(n - 1, 0), j))
    bspec = pl.BlockSpec((HB, 2 * ATT_BLK, ATT_BLK), lambda j, n: (j, 0, 0))
    rows = pl.BlockSpec((None, d, H, 1, LANES), lambda j, n: (nq(n), 0, 0, 0, 0))
    return pl.pallas_call(
        body, name=name, grid=(D // W, nb + 1),
        in_specs=[cur, prev, cur, prev, cur, bspec, cur, rows, rows],
        out_specs=[cur, done, done, bspec],
        out_shape=[_sds((S, D)), _sds((S, D)), _sds((S, D)), _sds((H, 2 * ATT_BLK, ATT_BLK))],
        scratch_shapes=[pltpu.VMEM((d, ATT_BLK, W), f32), pltpu.VMEM((d, ATT_BLK, W), f32)],
        compiler_params=_params(("arbitrary", "arbitrary")),
    )(q, k, k, v, v, bias_t, datt, lse_rows, dsum_rows)


def _attn_combine(os_, lses, name):
    S, D = os_[0].shape
    tr = _tile(S, 256, 16)
    head_cols = jnp.asarray((np.arange(LANES)[:, None] == np.arange(D)[None, :] // HEAD_DIM).astype(np.float32))

    def body(o0, o1, o2, l0, l1, l2, hc_ref, att_ref, attb_ref, lse_ref):
        a, b, c = l0[...], l1[...], l2[...]
        m = jnp.maximum(jnp.maximum(a, b), c)
        e0, e1, e2 = jnp.exp(a - m), jnp.exp(b - m), jnp.exp(c - m)
        tot = e0 + e1 + e2
        wide = lambda w: _dot_exact(w / tot, hc_ref[...])
        att = wide(e0) * o0[...] + wide(e1) * o1[...] + wide(e2) * o2[...]
        att_ref[...] = att
        attb_ref[...] = att.astype(bf16)
        lse_ref[...] = m + jnp.log(tot)

    wide_spec = pl.BlockSpec((tr, D), lambda i: (i, 0))
    lane_spec = pl.BlockSpec((tr, LANES), lambda i: (i, 0))
    return pl.pallas_call(
        body, name=name, grid=(S // tr,),
        in_specs=[wide_spec] * 3 + [lane_spec] * 3 + [pl.BlockSpec((LANES, D), lambda i: (0, 0))],
        out_specs=[wide_spec, wide_spec, lane_spec], out_shape=[_sds((S, D)), _sds((S, D), bf16), _sds((S, LANES))],
        compiler_params=_params(("parallel",)),
    )(*os_, *lses, head_cols)


ANY = pl.BlockSpec(memory_space=pl.ANY)


def _all_gather(vs, name):
    n = len(vs)

    def body(*refs):
        x_refs, out_refs = refs[:n], refs[n:2 * n]
        send_sems, recv_sems, local_sems = refs[2 * n:]
        x, y, c = lax.axis_index("x"), lax.axis_index("y"), lax.axis_index("c")
        me, sibling = (x, y, c), (x, y, 1 - c)
        chips = [(1 - x, y), (x, 1 - y), (1 - x, 1 - y)]

        def slot(i, px, py, pc):
            return out_refs[i].at[4 * px + 2 * py + pc]

        def copy(i, k, block, to, src=None):
            return pltpu.make_async_remote_copy(
                src_ref=slot(i, *block) if src is None else src, dst_ref=slot(i, *block),
                send_sem=send_sems.at[i, k], recv_sem=recv_sems.at[i, k], device_id=to, device_id_type=MESH)

        mine = [pltpu.make_async_copy(x_refs[i], slot(i, *me), local_sems.at[i]) for i in range(n)]
        for cp in mine:
            cp.start()
        first = []
        for i in range(n):
            first.append(copy(i, 0, me, sibling, src=x_refs[i]))
            first += [copy(i, 1 + j, me, (*chip, c), src=x_refs[i]) for j, chip in enumerate(chips)]
        for cp in first:
            cp.start()
        passed = []
        for i in range(n):
            for j, chip in enumerate(chips):
                copy(i, 1 + j, (*chip, c), me).wait_recv()
                cp = copy(i, 4 + j, (*chip, c), sibling)
                cp.start()
                passed.append(cp)
        for i in range(n):
            copy(i, 0, sibling, me).wait_recv()
            for j, chip in enumerate(chips):
                copy(i, 4 + j, (*chip, 1 - c), me).wait_recv()
        for cp in first + passed:
            cp.wait_send()
        for cp in mine:
            cp.wait()

    return pl.pallas_call(
        body, name=name, out_shape=[_sds((N_DEV,) + v.shape, v.dtype) for v in vs], in_specs=[ANY] * n,
        out_specs=[ANY] * n,
        scratch_shapes=[pltpu.SemaphoreType.DMA((n, 7)), pltpu.SemaphoreType.DMA((n, 7)), pltpu.SemaphoreType.DMA((n,))],
    )(*vs)


def _sum_slots(t, name):
    n, R, C = t.shape
    tr = _tile(R, PACK_ROW_TILE, 16)

    def body(t_ref, o_ref):
        acc = t_ref[0].astype(f32)
        for k in range(1, n):
            acc = acc + t_ref[k].astype(f32)
        o_ref[...] = acc

    return pl.pallas_call(
        body, name=name, grid=(R // tr,),
        in_specs=[pl.BlockSpec((n, tr, C), lambda i: (0, i, 0))],
        out_specs=pl.BlockSpec((tr, C), lambda i: (i, 0)), out_shape=_sds((R, C)),
        compiler_params=_params(("parallel",)),
    )(t)


HBM_SPEC = pl.BlockSpec(memory_space=pltpu.HBM)
SEM_SPEC = pl.BlockSpec(memory_space=pltpu.SEMAPHORE)
EFFECT = pltpu.SideEffectType.DATAFLOW_SIDE_EFFECTING


def _mesh_pos(p):
    return (p // 4, (p // 2) % 2, p % 2)


def _exchange_copy(src_refs, land_refs, send_sems, recv_sems, whole, dests, i, k):
    me = 4 * lax.axis_index("x") + 2 * lax.axis_index("y") + lax.axis_index("c")
    to = (me + k) % N_DEV
    frm = (me + N_DEV - k) % N_DEV
    lo, hi = dests
    src = src_refs[i] if whole else src_refs[i].at[jnp.minimum(jnp.maximum(to - lo, 0), hi - lo - 1)]
    s = i * (N_DEV - 1) + k - 1
    send = pltpu.make_async_remote_copy(src_ref=src, dst_ref=land_refs[i].at[me], send_sem=send_sems.at[s],
                                        recv_sem=recv_sems.at[s], device_id=_mesh_pos(to), device_id_type=MESH)
    recv = pltpu.make_async_remote_copy(src_ref=src, dst_ref=land_refs[i].at[frm], send_sem=send_sems.at[s],
                                        recv_sem=recv_sems.at[s], device_id=_mesh_pos(to), device_id_type=MESH)
    return send, recv, (to >= lo) & (to < hi), (me >= lo) & (me < hi)


def _exchange_start(srcs, whole, name, after=None, dests=(0, N_DEV)):
    n = len(srcs)
    lands = [lax.empty((N_DEV,) + s.shape[-2:], s.dtype) for s in srcs]
    after = list(after or [])
    n_in = 2 * n + len(after)
    everyone = dests == (0, N_DEV)

    def body(*refs):
        src_refs, land_refs = refs[:n], refs[n:2 * n]
        send_sems, recv_sems, token = refs[n_in], refs[n_in + 1], refs[-1]
        for i in range(n):
            for k in range(1, N_DEV):
                send, _, sends, _ = _exchange_copy(src_refs, land_refs, send_sems, recv_sems, whole, dests, i, k)
                if everyone:
                    send.start()
                else:
                    pl.when(sends)(send.start)
        token[...] = jnp.zeros_like(token)

    sems = pltpu.SemaphoreType.DMA((n * (N_DEV - 1),))
    outs = pl.pallas_call(
        body, name=name,
        out_shape=(sems, sems, *[pltpu.HBM(a.shape, a.dtype) for a in srcs + lands], _sds((8, LANES))),
        in_specs=[HBM_SPEC] * (2 * n) + [pl.BlockSpec(memory_space=pl.ANY)] * len(after),
        out_specs=(SEM_SPEC, SEM_SPEC, *[HBM_SPEC] * (2 * n), pl.BlockSpec(memory_space=pltpu.VMEM)),
        input_output_aliases={i: 2 + i for i in range(2 * n)},
        compiler_params=pltpu.CompilerParams(has_side_effects=EFFECT),
    )(*[pltpu.with_memory_space_constraint(a, pltpu.HBM) for a in srcs + lands], *after)
    return (outs[0], outs[1], list(outs[2:2 + n]), list(outs[2 + n:2 + 2 * n]), whole, dests), outs[-1]


def _exchange_wait(handle, after, name):
    send_sems, recv_sems, srcs, lands, whole, dests = handle
    n = len(srcs)
    everyone = dests == (0, N_DEV)

    def body(*refs):
        src_refs, land_refs = refs[:n], refs[n:2 * n]
        send_sems, recv_sems = refs[2 * n], refs[2 * n + 1]
        for i in range(n):
            for k in range(1, N_DEV):
                send, recv, sends, receives = _exchange_copy(src_refs, land_refs, send_sems, recv_sems, whole, dests, i, k)
                if everyone:
                    send.wait_send()
                    recv.wait_recv()
                else:
                    pl.when(sends)(send.wait_send)
                    pl.when(receives)(recv.wait_recv)

    outs = pl.pallas_call(
        body, name=name, out_shape=tuple(pltpu.HBM(a.shape, a.dtype) for a in srcs + lands),
        in_specs=[HBM_SPEC] * (2 * n) + [SEM_SPEC, SEM_SPEC, pl.BlockSpec(memory_space=pl.ANY)],
        out_specs=[HBM_SPEC] * (2 * n), input_output_aliases={i: i for i in range(2 * n)},
        compiler_params=pltpu.CompilerParams(has_side_effects=EFFECT),
    )(*srcs, *lands, send_sems, recv_sems, after)
    return list(outs[n:])


def _tie(v, token):
    return v + token[0:1, 0:1].astype(v.dtype).reshape((1,) * v.ndim)


def _with_own(land, own, me):
    return lax.dynamic_update_slice_in_dim(land, own[None].astype(land.dtype), me, 0)


class _Overlap:
    def __init__(self, shards, me, after):
        self.me = me
        self.names = list(shards)
        self.handle, self.token = _exchange_start([shards[nm] for nm in self.names], True, "weights_start", after)
        self.sent = {}

    def weights(self, after):
        lands = _exchange_wait(self.handle, after, "weights_wait")
        own = self.handle[2]
        return {nm: _full_from_blocks(nm, _with_own(land, o, self.me)) for nm, land, o in zip(self.names, lands, own)}

    def send(self, tag, grads):
        names = list(grads)
        handle, token = _exchange_start([_blocks_from_full(nm, grads[nm]) for nm in names], False, f"grads_start_{tag}")
        self.sent[tag] = (names, handle)
        return token

    def send_rows(self, tag, rows, dests):
        lo, hi = dests
        blocks = rows.reshape(hi - lo, rows.shape[0] // (hi - lo), rows.shape[1])
        handle, token = _exchange_start([blocks], False, f"grads_start_{tag}", None, dests)
        self.sent[tag] = ([tag], handle)
        return token

    def received(self, tag, after):
        names, handle = self.sent[tag]
        lands = _exchange_wait(handle, after, f"grads_wait_{tag}")
        lo = handle[5][0]
        own = [lax.dynamic_index_in_dim(b, self.me - lo, 0, keepdims=False) for b in handle[2]]
        return {nm: _with_own(land, o, self.me) for nm, land, o in zip(names, lands, own)}


ADAM_ROWS = 32


def _adamw(w, g, m, v, name):
    deep = w.ndim == 3
    R, C = w.shape[0], w.shape[-1]
    cb = LANES if C % LANES == 0 else C
    n_parts = g.shape[0] if g.ndim == 3 else 0

    def body(w_ref, g_ref, m_ref, v_ref, d_ref, m2_ref, v2_ref, *g_out):
        at = (lambda ref, sl: ref.at[sl, 0, :]) if deep else (lambda ref, sl: ref.at[sl, :])

        def update(sl):
            if n_parts:
                gv = g_ref[0, sl, :].astype(f32)
                for k in range(1, n_parts):
                    gv = gv + g_ref[k, sl, :].astype(f32)
                at(g_out[0], sl)[...] = gv
            else:
                gv = g_ref[sl, :]
            m2 = ADAM_B1 * at(m_ref, sl)[...] + (1.0 - ADAM_B1) * gv
            v2 = ADAM_B2 * at(v_ref, sl)[...] + (1.0 - ADAM_B2) * jnp.square(gv)
            m_hat = m2 / (1.0 - ADAM_B1 ** ADAM_STEP)
            v_hat = v2 / (1.0 - ADAM_B2 ** ADAM_STEP)
            at(d_ref, sl)[...] = -ADAM_LR * (m_hat / (jnp.sqrt(v_hat) + ADAM_EPS) + ADAM_WD * at(w_ref, sl)[...])
            at(m2_ref, sl)[...] = m2
            at(v2_ref, sl)[...] = v2

        main = R // ADAM_ROWS
        if main:
            lax.fori_loop(0, main, lambda i, c: (update(pl.ds(pl.multiple_of(i * ADAM_ROWS, ADAM_ROWS), ADAM_ROWS)), c)[1], 0)
        if R % ADAM_ROWS:
            update(pl.ds(main * ADAM_ROWS, R % ADAM_ROWS))

    spec = pl.BlockSpec((R, 1, cb), lambda j: (0, 0, j)) if deep else pl.BlockSpec((R, cb), lambda j: (0, j))
    g_spec = pl.BlockSpec((n_parts, R, cb), lambda j: (0, 0, j)) if n_parts else pl.BlockSpec((R, cb), lambda j: (0, j))
    n_out = 4 if n_parts else 3
    return pl.pallas_call(
        body, name=name, grid=(C // cb,), in_specs=[spec, g_spec, spec, spec], out_specs=[spec] * n_out,
        out_shape=[_sds(w.shape)] * n_out, compiler_params=_params(("parallel",)),
    )(w, g, m, v)


BIG_PARAMS = ("hy_w_in", "hy_w_out", "cv_w_pw1", "cv_w_pw2", "ffn_w_gate", "ffn_w_up", "ffn_w_down")


def _shards_2d(w):
    t = lambda a: jnp.transpose(a)
    return dict(in_t=t(w["hy_w_in"][0]), out=w["hy_w_out"][0], pw1=w["cv_w_pw1"][0], pw2=w["cv_w_pw2"][0],
                gate_t0=t(w["ffn_w_gate"][0]), gate_t1=t(w["ffn_w_gate"][1]), up_t0=t(w["ffn_w_up"][0]),
                up_t1=t(w["ffn_w_up"][1]), down0=w["ffn_w_down"][0], down1=w["ffn_w_down"][1])


def _unshard_2d(s):
    t = lambda a: jnp.transpose(a)
    out = dict(hy_w_out=s["out"][None], cv_w_pw1=s["pw1"][None], cv_w_pw2=s["pw2"][None],
               ffn_w_gate=jnp.stack([t(s["gate_t0"]), t(s["gate_t1"])]),
               ffn_w_up=jnp.stack([t(s["up_t0"]), t(s["up_t1"])]), ffn_w_down=jnp.stack([s["down0"], s["down1"]]))
    if "in_t" in s:
        out["hy_w_in"] = t(s["in_t"])[None]
    return out


def _full_from_blocks(nm, g):
    if nm == "pw1":
        return jnp.transpose(g, (1, 0, 2)).reshape(g.shape[1], N_DEV * g.shape[2])
    return g.reshape(N_DEV * g.shape[1], g.shape[2])


def _blocks_from_full(nm, g):
    if nm == "pw1":
        return jnp.transpose(g.reshape(g.shape[0], N_DEV, g.shape[1] // N_DEV), (1, 0, 2))
    return g.reshape(N_DEV, g.shape[0] // N_DEV, g.shape[1])


class _VecPack:
    def __init__(self, shapes):
        self.shapes = [tuple(s) for s in shapes]
        self.sizes = [int(np.prod(s)) for s in self.shapes]
        total = sum(self.sizes)
        self.rows = -(-(-(-total // LANES)) // 8) * 8
        self.total = total

    def pack(self, arrays):
        flat = jnp.concatenate([a.astype(f32).reshape(-1) for a in arrays])
        flat = jnp.pad(flat, (0, self.rows * LANES - self.total))
        return flat.reshape(self.rows, LANES)

    def unpack(self, packed):
        flat = packed.reshape(-1)
        out, off = [], 0
        for shp, n in zip(self.shapes, self.sizes):
            out.append(flat[off:off + n].reshape(shp))
            off += n
        return out

    def unpack_stacked(self, stacked, only=None):
        flat = stacked.reshape(stacked.shape[0], -1)
        offs = np.concatenate([[0], np.cumsum(self.sizes)])
        get = lambda i: flat[:, offs[i]:offs[i + 1]].reshape((stacked.shape[0],) + self.shapes[i])
        return get(only) if only is not None else [get(i) for i in range(len(self.shapes))]


def _row(v):
    return v.reshape(1, -1)


def _pad_lanes(v):
    v = v.reshape(1, -1)
    return jnp.pad(v, ((0, 0), (0, LANES - v.shape[1])))


def _ffn_fwd(h, w_gate_t, w_up_t, w_down, tag):
    F = w_down.shape[0]
    a = _mm(h, w_gate_t, tb=True, out_dtype=bf16, name=f"ffn_gate_{tag}")
    u = _mm(h, w_up_t, tb=True, out_dtype=bf16, name=f"ffn_up_{tag}")
    (f,), _ = _rowwise(f"swiglu_{tag}", lambda rv, vv: ([_silu(rv[0].astype(f32)) * rv[1].astype(f32)], []), [a, u], [],
                       [(F, bf16)], [], sub=16, col_chunk=_tile(F, 512, LANES))
    out = _mm(f, w_down, name=f"ffn_down_{tag}")
    return out, (a, u, f)


def _ffn_bwd(h, w_gate_t, w_up_t, w_down, saved, dout, tag):
    a, u, f = saved
    F = w_down.shape[0]
    df = _mm(dout, w_down, tb=True, out_dtype=bf16, name=f"ffn_down_dx_{tag}")
    dw_down = _mm(f, dout, ta=True, out_dtype=bf16, name=f"ffn_down_dw_{tag}")

    def fn(rv, vv):
        av, uv, dv = rv[0].astype(f32), rv[1].astype(f32), rv[2].astype(f32)
        sig = jax.nn.sigmoid(av)
        act = av * sig
        return [dv * uv * (sig + act * (1.0 - sig)), dv * act], []

    (da, du), _ = _rowwise(f"swiglu_bwd_{tag}", fn, [a, u, df], [], [(F, bf16), (F, bf16)], [], sub=16,
                           col_chunk=_tile(F, 512, LANES))
    dh = _mm(du, w_up_t, add=_mm(da, w_gate_t, name=f"ffn_gate_dx_{tag}"), name=f"ffn_up_dx_{tag}")
    dw_gate_t = _mm(da, h, ta=True, out_dtype=bf16, name=f"ffn_gate_dw_{tag}")
    dw_up_t = _mm(du, h, ta=True, out_dtype=bf16, name=f"ffn_up_dw_{tag}")
    return dh, dw_gate_t, dw_up_t, dw_down


def _local_step(x, target, mod, w_in_t, comm, small):
    S, D = x.shape
    di = small["hy_ssm_norm_g"].shape[-1]
    nh = small["hy_dt_bias"].shape[-1]
    cd = small["hy_conv_b"].shape[-1]
    m = [[_row(mod[i, j]) for j in range(6)] for i in range(2)]

    off_q = di + cd + nh
    w_qkv_t = w_in_t[off_q:]
    seg = dict(z=(w_in_t, 0, di), xbc=(w_in_t, di, cd), dt=(w_in_t, di + cd, LANES))
    for i, nm in enumerate(("q0", "q1", "q2", "k", "v")):
        seg[nm] = (w_qkv_t, i * D, D)

    g_mix = [_row(small["norm_mix_g"][i]) for i in range(2)]
    g_ffn = [_row(small["norm_ffn_g"][i]) for i in range(2)]
    conv_w, conv_b = small["hy_conv_w_full"], _row(small["hy_conv_b"][0])
    dt_bias, a_log, d_skip = (_pad_lanes(small[k][0]) for k in ("hy_dt_bias", "hy_a_log", "hy_d_skip"))
    g_ssm = _row(small["hy_ssm_norm_g"][0])
    onehot = jnp.asarray(_bucket_onehot())
    rel_t = small["rel_table"].T
    H = D // HEAD_DIM
    bias = [_exact_mm(rel_t[gi * H:(gi + 1) * H], onehot[gi], name=f"rel_bias_{gi}")
            .reshape(H, 2 * ATT_BLK, ATT_BLK) + _band_penalty() for gi in range(3)]

    h1 = _adaln_fwd(x, g_mix[0], m[0][1], m[0][0], "adaln_mix0")
    proj = {nm: _mm(h1, mat, tb=True, b_rows=(off, cnt), name=f"in_{nm}") for nm, (mat, off, cnt) in seg.items()}
    xbc_pre, xbc = _conv_fwd(proj["xbc"], conv_w, conv_b, silu=True, name="ssm_conv", tr=1024)
    y, hin = _ssd_fwd(xbc, proj["dt"], dt_bias, a_log, d_skip, di, "ssd_fwd")
    (yg,), _ = _rowwise("ssm_gate", lambda rv, vv: ([_gate_f(rv[0], rv[1], vv[0])], []),
                        [y, proj["z"]], [g_ssm], [(di, bf16)], [], sub=16)
    og = [_attn_fwd(proj[f"q{gi}"], proj["k"], proj["v"], bias[gi], d, f"attn_fwd_{gi}")
          for gi, d in enumerate(ATT_DILATIONS)]
    att, att_b, lse_tot = _attn_combine([a for a, _ in og], [_from_blocks(b, LANES) for _, b in og], "attn_combine")
    W = comm.weights(after=att_b)
    w_out_y, w_out_a = W["out"][:di], W["out"][di:]
    mix0 = _mm(att_b, w_out_a, add=_mm(yg, w_out_y, name="out_y"), name="out_a")
    x1, h2 = _resid_adaln_fwd(x, m[0][2], mix0, g_ffn[0], m[0][4], m[0][3], "resid_mix0_adaln_ffn0")
    f0, ffn0_saved = _ffn_fwd(h2, W["gate_t0"], W["up_t0"], W["down0"], "0")
    x2, h3 = _resid_adaln_fwd(x1, m[0][5], f0, g_mix[1], m[1][1], m[1][0], "resid_ffn0_adaln_mix1")
    pw1 = _mm(h3, W["pw1"], bias=_row(small["cv_b_pw1_full"]), name="cv_pw1")
    (u,), _ = _rowwise("cv_glu", lambda rv, vv: ([rv[0] * jax.nn.sigmoid(rv[1])], []),
                       [(pw1, 0, D), (pw1, 1, D)], [], [(D, f32)], [])
    (u2,) = _conv_fwd(u, small["cv_w_dw_full"], _row(small["cv_b_dw_full"]), silu=False, name="cv_dw")
    ln_g, ln_b = _row(small["cv_ln_g_full"]), _row(small["cv_ln_b_full"])
    (u3,), _ = _rowwise("cv_lnsilu", lambda rv, vv: ([_lnsilu_f(rv[0], vv[0], vv[1])], []),
                        [u2], [ln_g, ln_b], [(D, bf16)], [], sub=16)
    mix1 = _mm(u3, W["pw2"], bias=_row(small["cv_b_pw2_full"]), name="cv_pw2")
    x3, h4 = _resid_adaln_fwd(x2, m[1][2], mix1, g_ffn[1], m[1][4], m[1][3], "resid_mix1_adaln_ffn1")
    f1, ffn1_saved = _ffn_fwd(h4, W["gate_t1"], W["up_t1"], W["down1"], "1")

    g_fin = _row(small["final_norm_g"])
    dmod = [[None] * 6 for _ in range(2)]
    d_norm_mix, d_norm_ffn = [None, None], [None, None]
    big = {}

    def final_fn(rv, vv):
        xv, fv, tv = rv
        gate = vv[1]
        yv, vjp = jax.vjp(_rms, xv + gate * fv, vv[0])
        err = yv - tv
        dx, dg = vjp(err / D)
        part = 0.5 * jnp.sum(jnp.mean(err * err, -1, keepdims=True), 0, keepdims=True)
        return [dx, gate * dx], [dg, jnp.broadcast_to(part, (1, LANES)), jnp.sum(dx * fv, 0, keepdims=True)]

    (dx4, df1), (d_fin, loss, dmod[1][5]) = _rowwise("loss_head", final_fn, [x3, f1, target], [g_fin, m[1][5]],
                                                      [(D, f32), (D, bf16)], [D, LANES, D], sub=16)

    dh4, big["gate_t1"], big["up_t1"], big["down1"] = _ffn_bwd(h4, W["gate_t1"], W["up_t1"], W["down1"], ffn1_saved, df1, "1")
    dx3, dmix1, (d_norm_ffn[1], dmod[1][4], dmod[1][3], dmod[1][2], d_b_pw2) = _adaln_resid_bwd(
        x3, g_ffn[1], m[1][4], m[1][3], dh4, dx4, mix1, m[1][2], "adaln_ffn1_resid_mix1_bwd")
    du3 = _mm(dmix1, W["pw2"], tb=True, name="cv_pw2_dx")
    big["pw2"] = _mm(u3, dmix1, ta=True, out_dtype=bf16, name="cv_pw2_dw")

    def lnsilu_bwd(rv, vv):
        _, vjp = jax.vjp(_lnsilu_f, rv[0], vv[0], vv[1])
        du, dg, db = vjp(rv[1])
        return [du], [dg, db]

    (du2,), (d_ln_g, d_ln_b) = _rowwise("cv_lnsilu_bwd", lnsilu_bwd, [u2, du3], [ln_g, ln_b], [(D, f32)], [D, D])
    du, d_w_dw, d_b_dw = _conv_bwd(u, small["cv_w_dw_full"], du2, None, silu=False, name="cv_dw_bwd")

    def glu_bwd(rv, vv):
        a, gt, d = rv
        _, vjp = jax.vjp(lambda a_, g_: a_ * jax.nn.sigmoid(g_), a, gt)
        da, dg = vjp(d)
        return [da, dg], [jnp.sum(da, 0, keepdims=True), jnp.sum(dg, 0, keepdims=True)]

    (dpa, dpg), (d_b1a, d_b1g) = _rowwise("cv_glu_bwd", glu_bwd, [(pw1, 0, D), (pw1, 1, D), du], [],
                                           [(D, bf16), (D, bf16)], [D, D], sub=16)
    dpw1 = jnp.concatenate([dpa, dpg], axis=1)
    d_b_pw1 = jnp.concatenate([d_b1a, d_b1g], axis=1)
    dh3 = _mm(dpw1, W["pw1"], tb=True, name="cv_pw1_dx")
    big["pw1"] = _mm(h3, dpw1, ta=True, out_dtype=bf16, name="cv_pw1_dw")
    token = comm.send("layer1", {nm: big[nm] for nm in ("gate_t1", "up_t1", "down1", "pw2", "pw1")})
    dx2, df0, (d_norm_mix[1], dmod[1][1], dmod[1][0], dmod[0][5], _) = _adaln_resid_bwd(
        x2, g_mix[1], m[1][1], _tie(m[1][0], token), dh3, dx3, f0, m[0][5], "adaln_mix1_resid_ffn0_bwd")

    dh2, big["gate_t0"], big["up_t0"], big["down0"] = _ffn_bwd(h2, W["gate_t0"], W["up_t0"], W["down0"], ffn0_saved, df0, "0")
    dx1, dmix0, (d_norm_ffn[0], dmod[0][4], dmod[0][3], dmod[0][2], _) = _adaln_resid_bwd(
        x1, g_ffn[0], m[0][4], m[0][3], dh2, dx2, mix0, m[0][2], "adaln_ffn0_resid_mix0_bwd")
    dyg = _mm(dmix0, w_out_y, tb=True, name="out_y_dx")
    datt = _mm(dmix0, w_out_a, tb=True, name="out_a_dx")
    big["out"] = jnp.concatenate([_mm(yg, dmix0, ta=True, out_dtype=bf16, name="out_y_dw"),
                                  _mm(att_b, dmix0, ta=True, out_dtype=bf16, name="out_a_dw")], axis=0)
    token = comm.send("layer0", {nm: big[nm] for nm in ("gate_t0", "up_t0", "down0", "out")})
    g_ssm = _tie(g_ssm, token)

    def gate_bwd(rv, vv):
        _, vjp = jax.vjp(_gate_f, rv[0], rv[1], vv[0])
        dy_, dz_, dg_ = vjp(rv[2])
        return [dy_, dz_], [dg_]

    (dy, dz), (d_g_ssm,) = _rowwise("ssm_gate_bwd", gate_bwd, [y, proj["z"], dyg], [g_ssm], [(di, f32), (di, bf16)], [di],
                                    sub=16)
    dxbc, ddtraw, d_a_log, d_dskip, d_dt_bias = _ssd_bwd(xbc, proj["dt"], dt_bias, a_log, d_skip, hin, y, dy, di, "ssd_bwd")
    dxbc_pre, d_conv_w, d_conv_b = _conv_bwd(proj["xbc"], conv_w, dxbc, xbc_pre, silu=True, name="ssm_conv_bwd",
                                             dx_dtype=bf16, tr=1024)
    dh1 = None
    early = []
    for nm, dseg in (("z", dz), ("xbc", dxbc_pre), ("dt", ddtraw)):
        mat, off, cnt = seg[nm]
        dh1 = _mm(dseg, mat, b_rows=(off, cnt), add=dh1, name=f"in_{nm}_dx")
        dwp = _mm(dseg, h1, ta=True, out_dtype=bf16, name=f"in_{nm}_dw")
        early.append(dwp[:nh] if nm == "dt" else dwp)
    early = jnp.concatenate(early, axis=0)
    shard_rows = w_in_t.shape[0] // N_DEV
    n_early = off_q // shard_rows
    token = comm.send_rows("in_early", early[:n_early * shard_rows], (0, n_early))
    bias = [_tie(b, token) for b in bias]

    dq, dks, dvs, dbs = [], [], [], []
    lse_heads = lse_tot[:, :H]
    dsum_heads = _head_sums(att, datt, "attn_dsum")[:, :H]
    for gi, d in enumerate(ATT_DILATIONS):
        a, b, c_, e = _attn_bwd(proj[f"q{gi}"], proj["k"], proj["v"], bias[gi], datt,
                                _by_block(lse_heads, d), _by_block(dsum_heads, d), d, f"attn_bwd_{gi}")
        dq.append(a)
        dks.append(b)
        dvs.append(c_)
        dbs.append(e)
    dk = _add3(*dks, "attn_dk")
    dv = _add3(*dvs, "attn_dv")
    d_rel = jnp.concatenate(
        [_exact_mm(dbs[gi].reshape(H, -1), onehot[gi], tb=True, name=f"rel_grad_{gi}") for gi in range(3)], axis=0).T

    dsegs = (("q0", dq[0]), ("q1", dq[1]), ("q2", dq[2]), ("k", dk), ("v", dv))
    late = jnp.concatenate([early[n_early * shard_rows:]] +
                           [_mm(dseg, h1, ta=True, out_dtype=bf16, name=f"in_{nm}_dw") for nm, dseg in dsegs], axis=0)
    token = comm.send_rows("in_late", late, (n_early, N_DEV))
    w_qkv_after = _tie(w_qkv_t, token)
    for nm, dseg in dsegs:
        _, off, cnt = seg[nm]
        dh1 = _mm(dseg, w_qkv_after, b_rows=(off, cnt), add=dh1, name=f"in_{nm}_dx")
    dx0, (d_norm_mix[0], dmod[0][1], dmod[0][0]) = _adaln_bwd(x, g_mix[0], m[0][1], m[0][0], dh1, dx1, "adaln_mix0_bwd")

    smallg = dict(
        loss=loss, dmod=jnp.stack([jnp.concatenate(dmod[i], axis=1)[0] for i in range(2)]),
        norm_mix_g=jnp.concatenate(d_norm_mix, axis=0), norm_ffn_g=jnp.concatenate(d_norm_ffn, axis=0),
        hy_conv_w=d_conv_w, hy_conv_b=d_conv_b, hy_dt_bias=d_dt_bias[:, :nh], hy_a_log=d_a_log[:, :nh],
        hy_d_skip=d_dskip[:, :nh], hy_ssm_norm_g=d_g_ssm, rel_table=d_rel,
        cv_b_pw1=d_b_pw1, cv_w_dw=d_w_dw, cv_b_dw=d_b_dw, cv_ln_g=d_ln_g, cv_ln_b=d_ln_b, cv_b_pw2=d_b_pw2,
        final_norm_g=d_fin)
    return dx0, n_early, smallg


SMALL_GRAD_ORDER = ("loss", "dmod", "norm_mix_g", "norm_ffn_g", "hy_conv_w", "hy_conv_b", "hy_dt_bias", "hy_a_log",
                    "hy_d_skip", "hy_ssm_norm_g", "rel_table", "cv_b_pw1", "cv_w_dw", "cv_b_dw", "cv_ln_g", "cv_ln_b",
                    "cv_b_pw2", "final_norm_g")


def kernel(x, c, ada_w, ada_b, norm_mix_g, norm_ffn_g, hy_w_in, hy_conv_w, hy_conv_b, hy_dt_bias, hy_a_log, hy_d_skip, hy_ssm_norm_g, hy_w_out, rel_table, cv_w_pw1, cv_b_pw1, cv_w_dw, cv_b_dw, cv_ln_g, cv_ln_b, cv_w_pw2, cv_b_pw2, ffn_w_gate, ffn_w_up, ffn_w_down, final_norm_g, loss_target, m_ada_w, m_ada_b, m_norm_mix_g, m_norm_ffn_g, m_hy_w_in, m_hy_conv_w, m_hy_conv_b, m_hy_dt_bias, m_hy_a_log, m_hy_d_skip, m_hy_ssm_norm_g, m_hy_w_out, m_rel_table, m_cv_w_pw1, m_cv_b_pw1, m_cv_w_dw, m_cv_b_dw, m_cv_ln_g, m_cv_ln_b, m_cv_w_pw2, m_cv_b_pw2, m_ffn_w_gate, m_ffn_w_up, m_ffn_w_down, m_final_norm_g, v_ada_w, v_ada_b, v_norm_mix_g, v_norm_ffn_g, v_hy_w_in, v_hy_conv_w, v_hy_conv_b, v_hy_dt_bias, v_hy_a_log, v_hy_d_skip, v_hy_ssm_norm_g, v_hy_w_out, v_rel_table, v_cv_w_pw1, v_cv_b_pw1, v_cv_w_dw, v_cv_b_dw, v_cv_ln_g, v_cv_ln_b, v_cv_w_pw2, v_cv_b_pw2, v_ffn_w_gate, v_ffn_w_up, v_ffn_w_down, v_final_norm_g):
    names = ("ada_w", "ada_b", "norm_mix_g", "norm_ffn_g", "hy_w_in", "hy_conv_w", "hy_conv_b", "hy_dt_bias", "hy_a_log",
             "hy_d_skip", "hy_ssm_norm_g", "hy_w_out", "rel_table", "cv_w_pw1", "cv_b_pw1", "cv_w_dw", "cv_b_dw", "cv_ln_g",
             "cv_ln_b", "cv_w_pw2", "cv_b_pw2", "ffn_w_gate", "ffn_w_up", "ffn_w_down", "final_norm_g")
    w = dict(zip(names, (ada_w, ada_b, norm_mix_g, norm_ffn_g, hy_w_in, hy_conv_w, hy_conv_b, hy_dt_bias, hy_a_log, hy_d_skip,
                         hy_ssm_norm_g, hy_w_out, rel_table, cv_w_pw1, cv_b_pw1, cv_w_dw, cv_b_dw, cv_ln_g, cv_ln_b, cv_w_pw2,
                         cv_b_pw2, ffn_w_gate, ffn_w_up, ffn_w_down, final_norm_g)))
    mom = dict(zip(names, (m_ada_w, m_ada_b, m_norm_mix_g, m_norm_ffn_g, m_hy_w_in, m_hy_conv_w, m_hy_conv_b, m_hy_dt_bias,
                           m_hy_a_log, m_hy_d_skip, m_hy_ssm_norm_g, m_hy_w_out, m_rel_table, m_cv_w_pw1, m_cv_b_pw1, m_cv_w_dw,
                           m_cv_b_dw, m_cv_ln_g, m_cv_ln_b, m_cv_w_pw2, m_cv_b_pw2, m_ffn_w_gate, m_ffn_w_up, m_ffn_w_down,
                           m_final_norm_g)))
    vel = dict(zip(names, (v_ada_w, v_ada_b, v_norm_mix_g, v_norm_ffn_g, v_hy_w_in, v_hy_conv_w, v_hy_conv_b, v_hy_dt_bias,
                           v_hy_a_log, v_hy_d_skip, v_hy_ssm_norm_g, v_hy_w_out, v_rel_table, v_cv_w_pw1, v_cv_b_pw1, v_cv_w_dw,
                           v_cv_b_dw, v_cv_ln_g, v_cv_ln_b, v_cv_w_pw2, v_cv_b_pw2, v_ffn_w_gate, v_ffn_w_up, v_ffn_w_down,
                           v_final_norm_g)))
    S, D = x.shape[1], x.shape[2]
    ax, ay, ac = lax.axis_index("x"), lax.axis_index("y"), lax.axis_index("c")
    me = 4 * ax + 2 * ay + ac
    nmod = ada_w.shape[2]

    w2 = _shards_2d(w)
    big_names = list(w2)
    sharded_small = ("hy_conv_w", "cv_b_pw1", "cv_w_dw", "cv_b_dw", "cv_ln_g", "cv_ln_b", "cv_b_pw2")
    vp = _VecPack([c.shape] + [w[nm].shape for nm in sharded_small])
    g_in, sg = _all_gather([w2["in_t"].astype(bf16), vp.pack([c] + [w[nm] for nm in sharded_small])], "gather_w_in")
    w_in_t = _full_from_blocks("in_t", g_in)
    parts = vp.unpack_stacked(sg)
    c_all = parts[0][:, 0]
    small = {k: w[k] for k in ("norm_mix_g", "norm_ffn_g", "hy_conv_b", "hy_dt_bias", "hy_a_log", "hy_d_skip",
                               "hy_ssm_norm_g", "rel_table", "final_norm_g")}
    for p, nm in zip(parts[1:], sharded_small):
        p = p[:, 0]
        p = jnp.moveaxis(p, 0, -2)
        small[nm + "_full"] = p.reshape(p.shape[:-2] + (N_DEV * p.shape[-1],))

    (cs_all,), _ = _rowwise("ada_silu", lambda rv, vv: ([_silu(rv[0])], []), [c_all], [], [(D, f32)], [])
    b_mine = lax.dynamic_slice_in_dim(ada_b, me * nmod, nmod, axis=1)
    mod_part = jnp.stack([_mm(cs_all, ada_w[i], bias=b_mine[i:i + 1], name=f"ada_mod_{i}") for i in range(2)])
    (mod_all,) = _all_gather([mod_part.reshape(2 * N_DEV, nmod)], "gather_mod")
    mod_all = mod_all.reshape(N_DEV, 2, N_DEV, nmod)
    mod_mine = lax.dynamic_index_in_dim(mod_all, me, axis=2, keepdims=False)
    mod = jnp.transpose(mod_mine, (1, 0, 2)).reshape(2, 6, D)
    comm = _Overlap({nm: w2[nm].astype(bf16) for nm in big_names if nm != "in_t"}, me, after=[mod, w_in_t])
    mod = _tie(mod, comm.token)

    dx0, n_early, sgrad = _local_step(x[0], loss_target[0], mod, w_in_t, comm, small)

    gp = _VecPack([sgrad[k].shape for k in SMALL_GRAD_ORDER])
    small_handle, after = _exchange_start([gp.pack([sgrad[k] for k in SMALL_GRAD_ORDER])], True, "small_grads_start")
    m2, v2 = _shards_2d(mom), _shards_2d(vel)
    g2, d2, nm2, nv2 = {}, {}, {}, {}
    slots = {}
    for tag in ("layer1", "layer0"):
        slots.update(comm.received(tag, after))
        for nm in comm.sent[tag][0]:
            d2[nm], nm2[nm], nv2[nm], g2[nm] = _adamw(w2[nm], slots[nm], m2[nm], v2[nm], f"adamw_{nm}")
            after = g2[nm]
    (g_all,) = _exchange_wait(small_handle, after, "small_grads_wait")
    g_all = _with_own(g_all, small_handle[2][0], me)
    tot = dict(zip(SMALL_GRAD_ORDER, gp.unpack(_sum_slots(g_all, "sum_small_grads"))))
    dmod_all = gp.unpack_stacked(g_all, only=SMALL_GRAD_ORDER.index("dmod"))
    loss = tot["loss"][0, 0]

    grads = {}
    dmod_mine = lax.dynamic_slice_in_dim(dmod_all, me * nmod, nmod, axis=2)
    grads["ada_w"] = jnp.stack([_mm(cs_all, dmod_mine[:, i], ta=True, name=f"ada_w_grad_{i}") for i in range(2)])
    grads["ada_b"] = tot["dmod"]
    grads["norm_mix_g"], grads["norm_ffn_g"] = tot["norm_mix_g"], tot["norm_ffn_g"]
    grads["hy_conv_b"] = tot["hy_conv_b"]
    grads["hy_dt_bias"] = tot["hy_dt_bias"]
    grads["hy_a_log"] = tot["hy_a_log"]
    grads["hy_d_skip"] = tot["hy_d_skip"]
    grads["hy_ssm_norm_g"] = tot["hy_ssm_norm_g"]
    grads["rel_table"] = tot["rel_table"]
    grads["final_norm_g"] = tot["final_norm_g"][0]
    for nm in sharded_small:
        n = w[nm].shape[-1]
        grads[nm] = lax.dynamic_slice_in_dim(tot[nm], me * n, n, axis=1).reshape(w[nm].shape)

    delta, new_m, new_v = {}, {}, {}
    shp = ada_w.shape
    two = lambda t: t.reshape(-1, shp[-1])
    d_, m_, v_ = _adamw(two(ada_w), two(grads["ada_w"]), two(m_ada_w), two(v_ada_w), "adamw_ada_w")
    delta["ada_w"], new_m["ada_w"], new_v["ada_w"] = d_.reshape(shp), m_.reshape(shp), v_.reshape(shp)
    rest = [nm for nm in names if nm not in BIG_PARAMS and nm != "ada_w"]
    sp = _VecPack([w[nm].shape for nm in rest])
    packs = [sp.pack([t[nm] for nm in rest]) for t in (w, grads, mom, vel)]
    ds_, ms_, vs_ = _adamw(*packs, "adamw_small")
    for nm, a, b, e in zip(rest, sp.unpack(ds_), sp.unpack(ms_), sp.unpack(vs_)):
        delta[nm], new_m[nm], new_v[nm] = a, b, e

    slots.update(comm.received("in_early", ds_))
    slots.update(comm.received("in_late", slots["in_early"]))
    stored = lambda t: jnp.transpose(t, (2, 0, 1))
    in_slots = jnp.where(me < n_early, slots["in_early"], slots["in_late"])
    d_in, m_in, v_in, g_in = _adamw(stored(hy_w_in), in_slots, stored(m_hy_w_in), stored(v_hy_w_in), "adamw_in_t")
    for dst, part, t in ((grads, g2, g_in), (delta, d2, d_in), (new_m, nm2, m_in), (new_v, nv2, v_in)):
        dst.update(_unshard_2d(part))
        dst["hy_w_in"] = jnp.transpose(t, (1, 2, 0))

    return (loss, dx0[None], *[grads[n] for n in names], *[delta[n] for n in names],
            *[new_m[n] for n in names], *[new_v[n] for n in names])
```

```python
import functools
import math

import numpy as np
import jax
import jax.numpy as jnp
from jax import lax
from jax.experimental import pallas as pl
from jax.experimental.pallas import tpu as pltpu

f32 = jnp.float32
bf16 = jnp.bfloat16
EPS = 1e-6
N_DEV = 8
LANES = 128
SSM_STATE = 128
SSM_CHUNK = 128
SSM_GROUPS = 4
HEAD_DIM = 64
ATT_BLK = 128
ATT_DILATIONS = (1, 4, 16)
REL_BUCKETS = 32
REL_MAX_DIST = 2048
ADAM_LR, ADAM_B1, ADAM_B2, ADAM_EPS, ADAM_WD, ADAM_STEP = 0.001, 0.9, 0.999, 1e-08, 0.01, 10
PACK_ROW_TILE = 256
MESH = pl.DeviceIdType.MESH
VMEM_LIMIT = 48 * 1024 * 1024


def _sds(shape, dtype=f32):
    return jax.ShapeDtypeStruct(tuple(shape), dtype)


def _tile(n, cap, mult):
    best = None
    t = mult
    while t <= min(n, cap):
        if n % t == 0:
            best = t
        t += mult
    return best if best is not None else n


def _params(sem):
    return pltpu.CompilerParams(dimension_semantics=sem, vmem_limit_bytes=VMEM_LIMIT)


def _mm(a, b, *, name, ta=False, tb=False, b_rows=None, bias=None, add=None, out_dtype=f32,
        tm_cap=512, tn_cap=1536, tk_cap=8192):
    if ta:
        K, M = a.shape
    else:
        M, K = a.shape
    off, cnt = b_rows if b_rows is not None else (0, b.shape[0])
    if tb:
        N, K2 = cnt, b.shape[1]
    else:
        K2, N = cnt, b.shape[1]
    assert K == K2, (a.shape, b.shape, ta, tb, b_rows)
    if ta and a.dtype == f32:
        tm_cap = min(tm_cap, 256)
    if not ta:
        tm_cap = 2 * tm_cap
    tm = _tile(M, tm_cap, LANES)
    tn = _tile(math.gcd(off, N) if tb else N, tn_cap, LANES)
    tk = _tile(K if tb else math.gcd(off, K), tk_cap, LANES)
    assert N % tn == 0 and K % tk == 0 and off % (tn if tb else tk) == 0, (name, off, N, K, tn, tk)
    nk = K // tk
    jo, ko = (off // tn, 0) if tb else (0, off // tk)
    has_bias, has_add = bias is not None, add is not None
    dn = (((0 if ta else 1,), (1 if tb else 0,)), ((), ()))

    def body(*refs):
        a_ref, b_ref = refs[0], refs[1]
        pos = 2
        bias_ref = add_ref = None
        if has_bias:
            bias_ref = refs[pos]
            pos += 1
        if has_add:
            add_ref = refs[pos]
            pos += 1
        o_ref = refs[pos]
        k = pl.program_id(2)
        part = lax.dot_general(a_ref[...].astype(bf16), b_ref[...].astype(bf16), dn, preferred_element_type=f32)

        def finish(r):
            if has_bias:
                r = r + bias_ref[...]
            if has_add:
                r = r + add_ref[...]
            o_ref[...] = r.astype(o_ref.dtype)

        if nk == 1:
            finish(part)
        else:
            acc_ref = refs[pos + 1]

            @pl.when(k == 0)
            def _():
                acc_ref[...] = part

            @pl.when((k > 0) & (k < nk - 1))
            def _():
                acc_ref[...] += part

            @pl.when(k == nk - 1)
            def _():
                finish(acc_ref[...] + part)

    in_specs = [
        pl.BlockSpec((tk, tm), lambda i, j, k: (k, i)) if ta else pl.BlockSpec((tm, tk), lambda i, j, k: (i, k)),
        pl.BlockSpec((tn, tk), lambda i, j, k: (j + jo, k)) if tb else pl.BlockSpec((tk, tn), lambda i, j, k: (k + ko, j)),
    ]
    args = [a, b]
    if has_bias:
        in_specs.append(pl.BlockSpec((1, tn), lambda i, j, k: (0, j)))
        args.append(bias)
    if has_add:
        in_specs.append(pl.BlockSpec((tm, tn), lambda i, j, k: (i, j)))
        args.append(add)
    return pl.pallas_call(
        body, name=name, grid=(M // tm, N // tn, nk), in_specs=in_specs,
        out_specs=pl.BlockSpec((tm, tn), lambda i, j, k: (i, j)), out_shape=_sds((M, N), out_dtype),
        scratch_shapes=[pltpu.VMEM((tm, tn), f32)] if nk > 1 else [],
        compiler_params=_params(("parallel", "parallel", "arbitrary")),
    )(*args)


def _rowwise(name, fn, rows, vecs, out_rows, out_accs, *, tr_cap=256, sub=8, col_chunk=None):
    rows = [r if isinstance(r, tuple) else (r, 0, r.shape[1]) for r in rows]
    R = rows[0][0].shape[0]
    tr = _tile(R, tr_cap, 8)
    sub = sub if tr % sub == 0 else tr
    n_r, n_v, n_or, n_oa = len(rows), len(vecs), len(out_rows), len(out_accs)

    def body(*refs):
        row_refs = refs[:n_r]
        vec_refs = refs[n_r:n_r + n_v]
        orow_refs = refs[n_r + n_v:n_r + n_v + n_or]
        oacc_refs = refs[n_r + n_v + n_or:]
        vv = [r[...] for r in vec_refs]

        n_sub = tr // sub
        together = 4 if n_sub % 4 == 0 else 1

        def step(s, accs):
            for t in range(together):
                sl = pl.ds(pl.multiple_of((s * together + t) * sub, sub), sub)
                if col_chunk is None:
                    ro, ao = fn([r[sl, :] for r in row_refs], vv)
                    for o_ref, o in zip(orow_refs, ro):
                        o_ref[sl, :] = o.astype(o_ref.dtype)
                    accs = tuple(x + y for x, y in zip(accs, ao))
                else:
                    for c0 in range(0, rows[0][2], col_chunk):
                        cs_ = pl.ds(c0, col_chunk)
                        ro, _ = fn([r[sl, cs_] for r in row_refs], vv)
                        for o_ref, o in zip(orow_refs, ro):
                            o_ref[sl, cs_] = o.astype(o_ref.dtype)
            return accs

        accs = lax.fori_loop(0, n_sub // together, step, tuple(jnp.zeros((1, w), f32) for w in out_accs))
        if n_oa:
            @pl.when(pl.program_id(0) == 0)
            def _():
                for ref in oacc_refs:
                    ref[...] = jnp.zeros_like(ref)

            for ref, x in zip(oacc_refs, accs):
                ref[...] += x

    in_specs = [pl.BlockSpec((tr, w), functools.partial(lambda i, cb: (i, cb), cb=cb)) for (_, cb, w) in rows]
    in_specs += [pl.BlockSpec((1, v.shape[1]), lambda i: (0, 0)) for v in vecs]
    out_specs = [pl.BlockSpec((tr, w), lambda i: (i, 0)) for (w, _) in out_rows]
    out_specs += [pl.BlockSpec((1, w), lambda i: (0, 0)) for w in out_accs]
    out_shape = [_sds((R, w), dt) for (w, dt) in out_rows] + [_sds((1, w)) for w in out_accs]
    res = pl.pallas_call(
        body, name=name, grid=(R // tr,), in_specs=in_specs, out_specs=out_specs, out_shape=out_shape,
        compiler_params=_params(("arbitrary",)),
    )(*[r[0] for r in rows], *vecs)
    return res[:n_or], res[n_or:]


def _silu(x):
    return x * jax.nn.sigmoid(x)


def _rms(x, g):
    return x * lax.rsqrt(jnp.mean(x * x, -1, keepdims=True) + EPS) * g


def _adaln_f(x, g, sc, sh):
    return _rms(x, g) * (1.0 + sc) + sh


def _gate_f(y, z, g):
    return _rms(y * _silu(z), g)


def _lnsilu_f(u, g, b):
    mu = jnp.mean(u, -1, keepdims=True)
    var = jnp.mean(jnp.square(u - mu), -1, keepdims=True)
    return _silu((u - mu) * lax.rsqrt(var + EPS) * g + b)


def _adaln_fwd(x, g, sc, sh, name):
    (h,), _ = _rowwise(name, lambda rv, vv: ([_adaln_f(rv[0], *vv)], []), [x], [g, sc, sh], [(x.shape[1], bf16)], [],
                       sub=16)
    return h


def _adaln_bwd(x, g, sc, sh, dh, dres, name):
    def fn(rv, vv):
        xv, dhv, drv = rv
        _, vjp = jax.vjp(_adaln_f, xv, *vv)
        dx, dg, dsc, dsh = vjp(dhv)
        return [dx + drv], [dg, dsc, dsh]
    w = x.shape[1]
    (dx,), accs = _rowwise(name, fn, [x, dh, dres], [g, sc, sh], [(w, f32)], [w, w, w])
    return dx, accs


def _resid_adaln_fwd(x, gate, mix, g, sc, sh, name):
    def fn(rv, vv):
        xn = rv[0] + vv[0] * rv[1]
        return [xn, _adaln_f(xn, vv[1], vv[2], vv[3])], []
    w = x.shape[1]
    (xn, h), _ = _rowwise(name, fn, [x, mix], [gate, g, sc, sh], [(w, f32), (w, bf16)], [], sub=16)
    return xn, h


def _adaln_resid_bwd(x, g, sc, sh, dh, dres, mix, gate, name):
    def fn(rv, vv):
        xv, dhv, drv, mv = rv
        _, vjp = jax.vjp(_adaln_f, xv, vv[0], vv[1], vv[2])
        dx, dg, dsc, dsh = vjp(dhv)
        dx = dx + drv
        dm = vv[3] * dx
        return [dx, dm], [dg, dsc, dsh, jnp.sum(dx * mv, 0, keepdims=True), jnp.sum(dm, 0, keepdims=True)]
    w = x.shape[1]
    (dx, dmix), accs = _rowwise(name, fn, [x, dh, dres, mix], [g, sc, sh, gate], [(w, f32), (w, bf16)], [w] * 5, sub=16)
    return dx, dmix, accs


def _add3(a, b, c, name):
    (y,), _ = _rowwise(name, lambda rv, vv: ([rv[0] + rv[1] + rv[2]], []), [a, b, c], [], [(a.shape[1], bf16)], [],
                       sub=16)
    return y


CONV_HALO = 32
CONV_ROWS = 64


def _conv_fwd(x, w, b, *, silu, name, tr=512):
    S, C = x.shape
    K = w.shape[0]
    H = CONV_HALO
    assert K - 1 <= H and S % tr == 0 and tr % H == 0 and C % LANES == 0
    nh = tr // H

    def body(xp_ref, xc_ref, w_ref, b_ref, *rest):
        outs, scr = rest[:-1], rest[-1]
        i = pl.program_id(1)
        scr[pl.ds(0, H), :] = jnp.where(i > 0, xp_ref[...], 0.0)
        scr[pl.ds(H, tr), :] = xc_ref[...]
        taps = [w_ref[pl.ds(k, 1), :] for k in range(K)]
        for c0 in range(0, tr, CONV_ROWS):
            acc = jnp.zeros((CONV_ROWS, LANES), f32) + b_ref[...]
            for k in range(K):
                acc = acc + scr[pl.ds(c0 + H - (K - 1) + k, CONV_ROWS), :] * taps[k]
            outs[0][pl.ds(c0, CONV_ROWS), :] = acc.astype(outs[0].dtype)
            if silu:
                outs[1][pl.ds(c0, CONV_ROWS), :] = _silu(acc)

    n_out = 2 if silu else 1
    return pl.pallas_call(
        body, name=name, grid=(C // LANES, S // tr),
        in_specs=[pl.BlockSpec((H, LANES), lambda j, i: (jnp.maximum(i * nh - 1, 0), j)),
                  pl.BlockSpec((tr, LANES), lambda j, i: (i, j)),
                  pl.BlockSpec((K, LANES), lambda j, i: (0, j)),
                  pl.BlockSpec((1, LANES), lambda j, i: (0, j))],
        out_specs=[pl.BlockSpec((tr, LANES), lambda j, i: (i, j))] * n_out,
        out_shape=[_sds((S, C), bf16), _sds((S, C))] if silu else [_sds((S, C))],
        scratch_shapes=[pltpu.VMEM((tr + H, LANES), f32)],
        compiler_params=_params(("parallel", "arbitrary")),
    )(x, x, w, b)


def _conv_bwd(x, w, dact, pre, *, silu, name, dx_dtype=f32, tr=512):
    S, C = x.shape
    K = w.shape[0]
    H = CONV_HALO
    nh = tr // H
    n_i = S // tr
    kp = -(-K // 8) * 8

    def dsilu(p):
        s = jax.nn.sigmoid(p)
        return s * (1.0 + p * (1.0 - s))

    def body(*refs):
        if silu:
            xp_ref, xc_ref, w_ref, dc_ref, dn_ref, pc_ref, pn_ref, dx_ref, dw_ref, db_ref, xs, ds = refs
        else:
            xp_ref, xc_ref, w_ref, dc_ref, dn_ref, dx_ref, dw_ref, db_ref, xs, ds = refs
        i = pl.program_id(1)
        xs[pl.ds(0, H), :] = jnp.where(i > 0, xp_ref[...], 0.0)
        xs[pl.ds(H, tr), :] = xc_ref[...]
        dcur = dc_ref[...]
        dnext = dn_ref[...]
        if silu:
            dcur = dcur * dsilu(pc_ref[...].astype(f32))
            dnext = dnext * dsilu(pn_ref[...].astype(f32))
        ds[pl.ds(0, tr), :] = dcur
        ds[pl.ds(tr, H), :] = jnp.where(i < n_i - 1, dnext, 0.0)
        taps = [w_ref[pl.ds(k, 1), :] for k in range(K)]
        fold = lambda t: jnp.sum(t.reshape(CONV_ROWS // 8, 8, LANES), axis=0)
        dw_parts = [jnp.zeros((8, LANES), f32) for _ in range(K)]
        db_part = jnp.zeros((8, LANES), f32)
        for c0 in range(0, tr, CONV_ROWS):
            acc = jnp.zeros((CONV_ROWS, LANES), f32)
            d_c = ds[pl.ds(c0, CONV_ROWS), :]
            for k in range(K):
                acc = acc + ds[pl.ds(c0 + K - 1 - k, CONV_ROWS), :] * taps[k]
                dw_parts[k] = dw_parts[k] + fold(d_c * xs[pl.ds(c0 + H - (K - 1) + k, CONV_ROWS), :])
            db_part = db_part + fold(d_c)
            dx_ref[pl.ds(c0, CONV_ROWS), :] = acc.astype(dx_ref.dtype)

        @pl.when(i == 0)
        def _():
            dw_ref[...] = jnp.zeros_like(dw_ref)
            db_ref[...] = jnp.zeros_like(db_ref)

        for k in range(K):
            dw_ref[pl.ds(k, 1), :] += jnp.sum(dw_parts[k], 0, keepdims=True)
        db_ref[...] += jnp.sum(db_part, 0, keepdims=True)

    prev = pl.BlockSpec((H, LANES), lambda j, i: (jnp.maximum(i * nh - 1, 0), j))
    cur = pl.BlockSpec((tr, LANES), lambda j, i: (i, j))
    nxt = pl.BlockSpec((H, LANES), lambda j, i: (jnp.minimum((i + 1) * nh, n_i * nh - 1), j))
    in_specs = [prev, cur, pl.BlockSpec((K, LANES), lambda j, i: (0, j)), cur, nxt]
    args = [x, x, w, dact, dact]
    if silu:
        in_specs += [cur, nxt]
        args += [pre, pre]
    dx, dw, db = pl.pallas_call(
        body, name=name, grid=(C // LANES, n_i), in_specs=in_specs,
        out_specs=[cur, pl.BlockSpec((kp, LANES), lambda j, i: (0, j)), pl.BlockSpec((1, LANES), lambda j, i: (0, j))],
        out_shape=[_sds((S, C), dx_dtype), _sds((kp, C)), _sds((1, C))],
        scratch_shapes=[pltpu.VMEM((tr + H, LANES), f32), pltpu.VMEM((tr + H, LANES), f32)],
        compiler_params=_params(("parallel", "arbitrary")),
    )(*args)
    return dx, dw[:K], db


def _dot(a, b):
    return jnp.dot(a.astype(bf16), b.astype(bf16), preferred_element_type=f32)


def _dot_nt(a, b):
    return lax.dot_general(a.astype(bf16), b.astype(bf16), (((1,), (1,)), ((), ())), preferred_element_type=f32)


def _dot_tn(a, b):
    return lax.dot_general(a.astype(bf16), b.astype(bf16), (((0,), (0,)), ((), ())), preferred_element_type=f32)


def _softplus(x):
    return jnp.maximum(x, 0.0) + jnp.log(1.0 + jnp.exp(-jnp.abs(x)))


def _tri(q):
    i = lax.broadcasted_iota(jnp.int32, (q, q), 0)
    j = lax.broadcasted_iota(jnp.int32, (q, q), 1)
    return i >= j


def _ssd_prep(dtraw, dt_bias, a_log):
    q = dtraw.shape[0]
    dt = _softplus(dtraw + dt_bias)
    A = -jnp.exp(a_log)
    tri = _tri(q)
    cs = jnp.dot(tri.astype(f32), dt * A, preferred_element_type=f32, precision=lax.Precision.HIGHEST)
    return dt, A, cs, cs.T, tri


def _expand(cols, h0, n, width):
    q = cols.shape[0]
    return jnp.concatenate([jnp.broadcast_to(cols[:, h0 + r:h0 + r + 1], (q, width)) for r in range(n)], axis=1)


def _ssd_fwd(xbc, dtraw, dt_bias, a_log, d_skip, di, name):
    S, CD = xbc.shape
    Q, N, G = SSM_CHUNK, SSM_STATE, SSM_GROUPS
    nc = S // Q
    nh = di // HEAD_DIM
    R = nh // G
    gw = R * HEAD_DIM
    col_of_head = jnp.asarray((np.arange(LANES)[:, None] == np.arange(di)[None, :] // HEAD_DIM).astype(np.float32))
    dsk_wide = jnp.repeat(d_skip[0, :nh], HEAD_DIM)[None]

    def body(xbc_ref, dt_ref, bias_ref, alog_ref, dskw_ref, coh_ref, y_ref, hin_ref, state):
        c = pl.program_id(0)

        @pl.when(c == 0)
        def _():
            state[...] = jnp.zeros_like(state)

        hin_ref[...] = state[...]
        dt, A, cs, csT, tri = _ssd_prep(dt_ref[...], bias_ref[...], alog_ref[...])
        elast = jnp.exp(cs[Q - 1:Q, :])
        coh = coh_ref[...]
        dt_w, ecs_w, dend_w = _dot_exact(dt, coh), _dot_exact(jnp.exp(cs), coh), _dot_exact(jnp.exp(cs[Q - 1:Q, :] - cs), coh)
        for g in range(G):
            h0 = g * R
            cols = pl.ds(g * gw, gw)
            lanes = slice(g * gw, (g + 1) * gw)
            Bg = xbc_ref[:, pl.ds(di + g * N, N)]
            Cg = xbc_ref[:, pl.ds(di + G * N + g * N, N)]
            xg = xbc_ref[:, cols]
            Hg = state[cols, :]
            Gm = _dot_nt(Cg, Bg)
            xdt = xg * dt_w[:, lanes]
            yoff = _dot_nt(Cg, Hg) * ecs_w[:, lanes]
            ys = []
            for r in range(R):
                h = h0 + r
                L = jnp.exp(jnp.where(tri, cs[:, h:h + 1] - csT[h:h + 1, :], -jnp.inf))
                ys.append(_dot(Gm * L, xdt[:, r * HEAD_DIM:(r + 1) * HEAD_DIM]))
            y_ref[:, cols] = jnp.concatenate(ys, axis=1) + yoff + xg * dskw_ref[:, cols]
            hnew = _dot_tn(xdt * dend_w[:, lanes], Bg)
            escale = jnp.concatenate([jnp.broadcast_to(elast[:, h0 + r:h0 + r + 1], (HEAD_DIM, N)) for r in range(R)], axis=0)
            state[cols, :] = escale * Hg + hnew

    vec = pl.BlockSpec((1, LANES), lambda c: (0, 0))
    return pl.pallas_call(
        body, name=name, grid=(nc,),
        in_specs=[pl.BlockSpec((Q, CD), lambda c: (c, 0)), pl.BlockSpec((Q, LANES), lambda c: (c, 0)), vec, vec,
                  pl.BlockSpec((1, di), lambda c: (0, 0)), pl.BlockSpec((LANES, di), lambda c: (0, 0))],
        out_specs=[pl.BlockSpec((Q, di), lambda c: (c, 0)), pl.BlockSpec((None, di, N), lambda c: (c, 0, 0))],
        out_shape=[_sds((S, di)), _sds((nc, di, N))],
        scratch_shapes=[pltpu.VMEM((di, N), f32)],
        compiler_params=_params(("arbitrary",)),
    )(xbc, dtraw, dt_bias, a_log, dsk_wide, col_of_head)


def _dot_exact(a, b):
    bb = b.astype(bf16)
    hi = a.astype(bf16)
    rest = a - hi.astype(f32)
    mid = rest.astype(bf16)
    low = (rest - mid.astype(f32)).astype(bf16)
    one_pass = lambda t: jnp.dot(t, bb, preferred_element_type=f32)
    return one_pass(hi) + one_pass(mid) + one_pass(low)


def _ssd_bwd(xbc, dtraw, dt_bias, a_log, d_skip, hin, y, dy, di, name):
    S, CD = xbc.shape
    Q, N, G = SSM_CHUNK, SSM_STATE, SSM_GROUPS
    nc = S // Q
    nh = di // HEAD_DIM
    R = nh // G
    gw = R * HEAD_DIM
    P = HEAD_DIM
    head_of_col = jnp.asarray((np.arange(di)[:, None] // P == np.arange(LANES)[None, :]).astype(np.float32))
    dsk_wide = jnp.repeat(d_skip[0, :nh], P)[None]

    def body(xbc_ref, dt_ref, bias_ref, alog_ref, dskw_ref, hoc_ref, hin_ref, y_ref, dy_ref,
             dxbc_ref, ddt_ref, dA_ref, ddsk_ref, dtb_ref, dstate, dxdt_all, tend_all, yoff_all, colterm_all):
        c = pl.program_id(0)

        @pl.when(c == 0)
        def _():
            dstate[...] = jnp.zeros_like(dstate)
            dA_ref[...] = jnp.zeros_like(dA_ref)
            ddsk_ref[...] = jnp.zeros_like(ddsk_ref)
            dtb_ref[...] = jnp.zeros_like(dtb_ref)

        dtraw_v = dt_ref[...]
        dt, A, cs, csT, tri = _ssd_prep(dtraw_v, bias_ref[...], alog_ref[...])
        tri_t = jnp.logical_not(tri) | (lax.broadcasted_iota(jnp.int32, (Q, Q), 0) == lax.broadcasted_iota(jnp.int32, (Q, Q), 1))
        ecs = jnp.exp(cs)
        dend = jnp.exp(cs[Q - 1:Q, :] - cs)
        elast = jnp.exp(cs[Q - 1:Q, :])
        hoc = hoc_ref[...]
        state_dot = jnp.sum(_dot_exact(dstate[...] * hin_ref[...], jnp.ones((N, LANES), f32)) * hoc, 0, keepdims=True) * elast
        for g in range(G):
            h0 = g * R
            Bg = xbc_ref[:, pl.ds(di + g * N, N)]
            Cg = xbc_ref[:, pl.ds(di + G * N + g * N, N)]
            xg = xbc_ref[:, pl.ds(g * gw, gw)]
            dyg = dy_ref[:, pl.ds(g * gw, gw)]
            Hg = hin_ref[pl.ds(g * gw, gw), :]
            dHg = dstate[pl.ds(g * gw, gw), :]
            dt_e = _expand(dt, h0, R, P)
            ecs_e = _expand(ecs, h0, R, P)
            dend_e = _expand(dend, h0, R, P)
            cols = pl.ds(g * gw, gw)
            Gm = _dot_nt(Cg, Bg)
            Gm_t = _dot_nt(Bg, Cg)
            xdt = xg * dt_e
            dye = dyg * ecs_e
            bdh = _dot_nt(Bg, dHg)
            dC = _dot(dye, Hg)
            dB = _dot(xdt * dend_e, dHg)
            dHin = _dot_tn(dye, Cg)
            dxdt_state = dend_e * bdh
            end_term = xdt * dxdt_state
            tend_all[:, cols] = end_term
            yoff_all[:, cols] = _dot_nt(Cg, Hg) * ecs_e
            dG = jnp.zeros((Q, Q), f32)
            dxd = []
            for r in range(R):
                h = h0 + r
                sl = slice(r * P, (r + 1) * P)
                seg = cs[:, h:h + 1] - csT[h:h + 1, :]
                L = jnp.exp(jnp.where(tri, seg, -jnp.inf))
                L_t = jnp.exp(jnp.where(tri_t, -seg, -jnp.inf))
                dyh = dyg[:, sl]
                dG = dG + _dot_nt(dyh, xdt[:, sl]) * L
                dxd.append(_dot(Gm_t * L_t, dyh))
            dxdt_diag = jnp.concatenate(dxd, axis=1)
            dxdt = dxdt_diag + dxdt_state
            dxdt_all[:, cols] = dxdt
            colterm_all[:, cols] = xdt.astype(bf16).astype(f32) * dxdt_diag + end_term
            dxbc_ref[:, cols] = dxdt * dt_e + dyg * dskw_ref[:, cols]
            dxbc_ref[:, pl.ds(di + g * N, N)] = dB + _dot_tn(dG, Cg)
            dxbc_ref[:, pl.ds(di + G * N + g * N, N)] = dC + _dot(dG, Bg)
            escale = jnp.concatenate([jnp.broadcast_to(elast[:, h0 + r:h0 + r + 1], (P, N)) for r in range(R)], axis=0)
            dstate[pl.ds(g * gw, gw), :] = escale * dHg + dHin
        xs = xbc_ref[:, pl.ds(0, di)]
        dyv = dy_ref[...]
        yoff = yoff_all[...]
        y_diag = y_ref[...] - dskw_ref[...] * xs - yoff
        rs_y = _dot_exact(dyv.astype(bf16).astype(f32) * y_diag + dyv * yoff, hoc)
        rs_c = _dot_exact(colterm_all[...], hoc)
        rs_x = _dot_exact(dxdt_all[...] * xs, hoc)
        end_dot = _dot_exact(jnp.broadcast_to(jnp.sum(tend_all[...], 0, keepdims=True), (8, di)), hoc)[0:1]
        last = lax.broadcasted_iota(jnp.int32, (Q, 1), 0) == Q - 1
        dcs = rs_y - rs_c + jnp.where(last, end_dot + state_dot, 0.0)
        da = lax.dot_general(tri.astype(f32), dcs, (((0,), (0,)), ((), ())), preferred_element_type=f32,
                             precision=lax.Precision.HIGHEST)
        ddt = da * A + rs_x
        ddtraw = ddt * jax.nn.sigmoid(dtraw_v + bias_ref[...])
        ddt_ref[...] = ddtraw.astype(ddt_ref.dtype)
        dA_ref[...] += jnp.sum(da * dt, 0, keepdims=True) * A
        ddsk_ref[...] += jnp.sum(_dot_exact(dyv * xs, hoc), 0, keepdims=True)
        dtb_ref[...] += jnp.sum(ddtraw, 0, keepdims=True)

    vec = pl.BlockSpec((1, LANES), lambda c: (0, 0))
    rev = lambda c: (nc - 1 - c, 0)
    return pl.pallas_call(
        body, name=name, grid=(nc,),
        in_specs=[pl.BlockSpec((Q, CD), rev), pl.BlockSpec((Q, LANES), rev), vec, vec,
                  pl.BlockSpec((1, di), lambda c: (0, 0)), pl.BlockSpec((di, LANES), lambda c: (0, 0)),
                  pl.BlockSpec((None, di, N), lambda c: (nc - 1 - c, 0, 0)), pl.BlockSpec((Q, di), rev),
                  pl.BlockSpec((Q, di), rev)],
        out_specs=[pl.BlockSpec((Q, CD), rev), pl.BlockSpec((Q, LANES), rev), vec, vec, vec],
        out_shape=[_sds((S, CD)), _sds((S, LANES), bf16), _sds((1, LANES)), _sds((1, LANES)), _sds((1, LANES))],
        scratch_shapes=[pltpu.VMEM((di, N), f32)] + [pltpu.VMEM((Q, di), f32)] * 4,
        compiler_params=_params(("arbitrary",)),
    )(xbc, dtraw, dt_bias, a_log, dsk_wide, head_of_col, hin, y, dy)


def _t5_bucket_np(dist):
    max_exact = REL_BUCKETS // 2
    n = np.maximum(dist, 1).astype(np.float32)
    large = np.float32(max_exact) + np.log(n / np.float32(max_exact)) / np.float32(math.log(REL_MAX_DIST / max_exact)) * np.float32(REL_BUCKETS - max_exact)
    large = np.minimum(large.astype(np.int32), REL_BUCKETS - 1)
    return np.where(dist < max_exact, dist, large)


def _bucket_onehot():
    i = np.arange(ATT_BLK)[None, :]
    j = np.arange(2 * ATT_BLK)[:, None]
    delta = np.maximum(ATT_BLK + i - j, 0)
    out = np.zeros((len(ATT_DILATIONS), REL_BUCKETS, ATT_BLK * 2 * ATT_BLK), np.float32)
    for gi, d in enumerate(ATT_DILATIONS):
        b = _t5_bucket_np(delta * d).reshape(-1)
        out[gi, b, np.arange(b.size)] = 1.0
    return out


def _exact_mm(a, b, *, name, tb=False):
    M, K = a.shape
    N = b.shape[0] if tb else b.shape[1]
    tn = _tile(N, 4096, LANES)
    dn = (((1,), (1 if tb else 0,)), ((), ()))

    def body(a_ref, b_ref, o_ref):
        o_ref[...] = lax.dot_general(a_ref[...], b_ref[...], dn, preferred_element_type=f32,
                                     precision=lax.Precision.HIGHEST)

    return pl.pallas_call(
        body, name=name, grid=(N // tn,),
        in_specs=[pl.BlockSpec((M, K), lambda j: (0, 0)),
                  pl.BlockSpec((tn, K), lambda j: (j, 0)) if tb else pl.BlockSpec((K, tn), lambda j: (0, j))],
        out_specs=pl.BlockSpec((M, tn), lambda j: (0, j)), out_shape=_sds((M, N)),
        compiler_params=_params(("parallel",)),
    )(a, b)


def _band_penalty():
    i = np.arange(ATT_BLK)[None, :]
    j = np.arange(2 * ATT_BLK)[:, None]
    delta = ATT_BLK + i - j
    return np.where((delta >= 0) & (delta <= ATT_BLK), 0.0, -np.inf).astype(np.float32)


def _first_block_keep(n):
    key = lax.broadcasted_iota(jnp.int32, (2 * ATT_BLK, ATT_BLK), 0)
    return (key >= ATT_BLK) | (n > 0)


ATT_SCALE = HEAD_DIM ** -0.5


def _rows(ref, r, d):
    return ref[...] if d == 1 else ref[pl.ds(r, ATT_BLK, stride=d), :]


def _set_rows(ref, r, d, val):
    if d == 1:
        ref[...] = val
    else:
        ref[pl.ds(r, ATT_BLK, stride=d), :] = val


def _attn_width(d, D):
    return D if d == 1 else LANES


def _over_residues(d, one, unroll=1):
    if d == 1:
        one(0)
    else:
        lax.fori_loop(0, d, lambda r, c: (one(r), c)[1], 0, unroll=unroll)


def _attn_fwd(q, k, v, bias_t, d, name):
    S, D = q.shape
    nb = S // (d * ATT_BLK)
    H = D // HEAD_DIM
    W = _attn_width(d, D)
    HB = W // HEAD_DIM

    def body(q_ref, kp_ref, kc_ref, vp_ref, vc_ref, b_ref, o_ref, lse_ref):
        keep = _first_block_keep(pl.program_id(1))
        first = lax.broadcasted_iota(jnp.int32, (1, LANES), 1) < HEAD_DIM

        def one(r):
            qs = (_rows(q_ref, r, d) * ATT_SCALE).astype(bf16)
            kcat = jnp.concatenate([_rows(kp_ref, r, d), _rows(kc_ref, r, d)], axis=0).astype(bf16)
            vcat = jnp.concatenate([_rows(vp_ref, r, d), _rows(vc_ref, r, d)], axis=0).astype(bf16)
            outs = []
            for pair in range(W // LANES):
                ps = slice(pair * LANES, (pair + 1) * LANES)
                q2, k2, v2 = qs[:, ps], kcat[:, ps], vcat[:, ps]
                o2 = jnp.zeros((ATT_BLK, LANES), f32)
                for e in range(2):
                    h = 2 * pair + e
                    mine = first if e == 0 else jnp.logical_not(first)
                    zero = jnp.zeros((), bf16)
                    st = jnp.where(keep, _dot_nt(k2, jnp.where(mine, q2, zero)) + b_ref[h], -jnp.inf)
                    m = jnp.max(st, 0, keepdims=True)
                    pt = jnp.exp(st - m)
                    l = jnp.sum(pt, 0, keepdims=True)
                    o2 = o2 + _dot_tn(pt * (1.0 / l), jnp.where(mine, v2, zero))
                    lse_ref[r, h] = m + jnp.log(l)
                outs.append(o2)
            _set_rows(o_ref, r, d, jnp.concatenate(outs, axis=1))

        _over_residues(d, one, unroll=4)

    cur = pl.BlockSpec((ATT_BLK * d, W), lambda j, n: (n, j))
    prev = pl.BlockSpec((ATT_BLK * d, W), lambda j, n: (jnp.maximum(n - 1, 0), j))
    return pl.pallas_call(
        body, name=name, grid=(D // W, nb),
        in_specs=[cur, prev, cur, prev, cur, pl.BlockSpec((HB, 2 * ATT_BLK, ATT_BLK), lambda j, n: (j, 0, 0))],
        out_specs=[cur, pl.BlockSpec((None, d, HB, 1, LANES), lambda j, n: (n, 0, j, 0, 0))],
        out_shape=[_sds((S, D)), _sds((nb, d, H, 1, LANES))],
        compiler_params=_params(("parallel", "arbitrary")),
    )(q, k, k, v, v, bias_t)


def _from_blocks(rows, lanes=None):
    nb, d, H = rows.shape[:3]
    a = jnp.transpose(rows[:, :, :, 0, :], (0, 3, 1, 2)).reshape(nb * ATT_BLK * d, H)
    return a if lanes is None else jnp.pad(a, ((0, 0), (0, lanes - H)))


def _by_block(a, d):
    S, H = a.shape
    t = jnp.transpose(a.reshape(S // (d * ATT_BLK), ATT_BLK, d, H), (0, 2, 3, 1))
    return t[:, :, :, None, :]


def _head_sums(a, b, name):
    S, D = a.shape
    tr = _tile(S, 512, 8)
    hoc = jnp.asarray((np.arange(D)[:, None] // HEAD_DIM == np.arange(LANES)[None, :]).astype(np.float32))

    def body(a_ref, b_ref, h_ref, o_ref):
        o_ref[...] = _dot_exact(a_ref[...] * b_ref[...], h_ref[...])

    return pl.pallas_call(
        body, name=name, grid=(S // tr,),
        in_specs=[pl.BlockSpec((tr, D), lambda i: (i, 0)), pl.BlockSpec((tr, D), lambda i: (i, 0)),
                  pl.BlockSpec((D, LANES), lambda i: (0, 0))],
        out_specs=pl.BlockSpec((tr, LANES), lambda i: (i, 0)), out_shape=_sds((S, LANES)),
        compiler_params=_params(("parallel",)),
    )(a, b, hoc)


def _attn_bwd(q, k, v, bias_t, datt, lse_rows, dsum_rows, d, name):
    S, D = q.shape
    nb = S // (d * ATT_BLK)
    H = D // HEAD_DIM
    W = _attn_width(d, D)
    HB = W // HEAD_DIM

    def body(q_ref, kp_ref, kc_ref, vp_ref, vc_ref, b_ref, do_ref, lse_ref, dsum_ref,
             dq_ref, dk_ref, dv_ref, db_ref, carry_k, carry_v):
        j = pl.program_id(0)
        n = pl.program_id(1)

        @pl.when(n == 0)
        def _():
            carry_k[...] = jnp.zeros_like(carry_k)
            carry_v[...] = jnp.zeros_like(carry_v)
            db_ref[...] = jnp.zeros_like(db_ref)

        @pl.when(n < nb)
        def _():
            key = lax.broadcasted_iota(jnp.int32, (2 * ATT_BLK, ATT_BLK), 0)
            keep = (key >= ATT_BLK) | (n > 0)
            first = lax.broadcasted_iota(jnp.int32, (1, LANES), 1) < HEAD_DIM

            def one(r):
                qs = (_rows(q_ref, r, d) * ATT_SCALE).astype(bf16)
                kcat = jnp.concatenate([_rows(kp_ref, r, d), _rows(kc_ref, r, d)], axis=0).astype(bf16)
                vcat = jnp.concatenate([_rows(vp_ref, r, d), _rows(vc_ref, r, d)], axis=0).astype(bf16)
                dob = _rows(do_ref, r, d).astype(bf16)
                dqs, dks, dvs = [], [], []
                for pair in range(W // LANES):
                    ps = slice(pair * LANES, (pair + 1) * LANES)
                    q2, k2, v2, do2 = qs[:, ps], kcat[:, ps], vcat[:, ps], dob[:, ps]
                    dq2 = jnp.zeros((ATT_BLK, LANES), f32)
                    dk2 = jnp.zeros((2 * ATT_BLK, LANES), f32)
                    dv2 = jnp.zeros((2 * ATT_BLK, LANES), f32)
                    for e in range(2):
                        h = 2 * pair + e
                        mine = first if e == 0 else jnp.logical_not(first)
                        zero = jnp.zeros((), bf16)
                        qm, dom, km = jnp.where(mine, q2, zero), jnp.where(mine, do2, zero), jnp.where(mine, k2, zero)
                        st = jnp.where(keep, _dot_nt(k2, qm) + b_ref[h], -jnp.inf)
                        pt = jnp.exp(st - lse_ref[r, j * HB + h])
                        dst = pt * (_dot_nt(v2, dom) - dsum_ref[r, j * HB + h])
                        db_ref[h] += dst
                        dv2 = dv2 + _dot(pt, dom)
                        dk2 = dk2 + _dot(dst, qm)
                        dq2 = dq2 + _dot_tn(dst, km)
                    dqs.append(dq2 * ATT_SCALE)
                    dks.append(dk2)
                    dvs.append(dv2)
                _set_rows(dq_ref, r, d, jnp.concatenate(dqs, axis=1))
                dk = jnp.concatenate(dks, axis=1)
                dv = jnp.concatenate(dvs, axis=1)
                _set_rows(dk_ref, r, d, carry_k[r] + dk[:ATT_BLK])
                _set_rows(dv_ref, r, d, carry_v[r] + dv[:ATT_BLK])
                carry_k[r] = dk[ATT_BLK:]
                carry_v[r] = dv[ATT_BLK:]

            _over_residues(d, one, unroll=2)

        @pl.when(n == nb)
        def _():
            def last(r):
                _set_rows(dk_ref, r, d, carry_k[r])
                _set_rows(dv_ref, r, d, carry_v[r])

            _over_residues(d, last)

    nq = lambda n: jnp.minimum(n, nb - 1)
    cur = pl.BlockSpec((ATT_BLK * d, W), lambda j, n: (nq(n), j))
    prev = pl.BlockSpec((ATT_BLK * d, W), lambda j, n: (jnp.maximum(nq(n) - 1, 0), j))
    done = pl.BlockSpec((ATT_BLK * d, W), lambda j, n: (jnp.maximum(n - 1, 0), j))
    bspec = pl.BlockSpec((HB, 2 * ATT_BLK, ATT_BLK), lambda j, n: (j, 0, 0))
    rows = pl.BlockSpec((None, d, H, 1, LANES), lambda j, n: (nq(n), 0, 0, 0, 0))
    return pl.pallas_call(
        body, name=name, grid=(D // W, nb + 1),
        in_specs=[cur, prev, cur, prev, cur, bspec, cur, rows, rows],
        out_specs=[cur, done, done, bspec],
        out_shape=[_sds((S, D)), _sds((S, D)), _sds((S, D)), _sds((H, 2 * ATT_BLK, ATT_BLK))],
        scratch_shapes=[pltpu.VMEM((d, ATT_BLK, W), f32), pltpu.VMEM((d, ATT_BLK, W), f32)],
        compiler_params=_params(("arbitrary", "arbitrary")),
    )(q, k, k, v, v, bias_t, datt, lse_rows, dsum_rows)


def _attn_combine(os_, lses, name):
    S, D = os_[0].shape
    tr = _tile(S, 256, 16)
    head_cols = jnp.asarray((np.arange(LANES)[:, None] == np.arange(D)[None, :] // HEAD_DIM).astype(np.float32))

    def body(o0, o1, o2, l0, l1, l2, hc_ref, att_ref, attb_ref, lse_ref):
        a, b, c = l0[...], l1[...], l2[...]
        m = jnp.maximum(jnp.maximum(a, b), c)
        e0, e1, e2 = jnp.exp(a - m), jnp.exp(b - m), jnp.exp(c - m)
        tot = e0 + e1 + e2
        wide = lambda w: _dot_exact(w / tot, hc_ref[...])
        att = wide(e0) * o0[...] + wide(e1) * o1[...] + wide(e2) * o2[...]
        att_ref[...] = att
        attb_ref[...] = att.astype(bf16)
        lse_ref[...] = m + jnp.log(tot)

    wide_spec = pl.BlockSpec((tr, D), lambda i: (i, 0))
    lane_spec = pl.BlockSpec((tr, LANES), lambda i: (i, 0))
    return pl.pallas_call(
        body, name=name, grid=(S // tr,),
        in_specs=[wide_spec] * 3 + [lane_spec] * 3 + [pl.BlockSpec((LANES, D), lambda i: (0, 0))],
        out_specs=[wide_spec, wide_spec, lane_spec], out_shape=[_sds((S, D)), _sds((S, D), bf16), _sds((S, LANES))],
        compiler_params=_params(("parallel",)),
    )(*os_, *lses, head_cols)


ANY = pl.BlockSpec(memory_space=pl.ANY)


def _all_gather(vs, name):
    n = len(vs)

    def body(*refs):
        x_refs, out_refs = refs[:n], refs[n:2 * n]
        send_sems, recv_sems, local_sems = refs[2 * n:]
        x, y, c = lax.axis_index("x"), lax.axis_index("y"), lax.axis_index("c")
        me, sibling = (x, y, c), (x, y, 1 - c)
        chips = [(1 - x, y), (x, 1 - y), (1 - x, 1 - y)]

        def slot(i, px, py, pc):
            return out_refs[i].at[4 * px + 2 * py + pc]

        def copy(i, k, block, to, src=None):
            return pltpu.make_async_remote_copy(
                src_ref=slot(i, *block) if src is None else src, dst_ref=slot(i, *block),
                send_sem=send_sems.at[i, k], recv_sem=recv_sems.at[i, k], device_id=to, device_id_type=MESH)

        mine = [pltpu.make_async_copy(x_refs[i], slot(i, *me), local_sems.at[i]) for i in range(n)]
        for cp in mine:
            cp.start()
        first = []
        for i in range(n):
            first.append(copy(i, 0, me, sibling, src=x_refs[i]))
            first += [copy(i, 1 + j, me, (*chip, c), src=x_refs[i]) for j, chip in enumerate(chips)]
        for cp in first:
            cp.start()
        passed = []
        for i in range(n):
            for j, chip in enumerate(chips):
                copy(i, 1 + j, (*chip, c), me).wait_recv()
                cp = copy(i, 4 + j, (*chip, c), sibling)
                cp.start()
                passed.append(cp)
        for i in range(n):
            copy(i, 0, sibling, me).wait_recv()
            for j, chip in enumerate(chips):
                copy(i, 4 + j, (*chip, 1 - c), me).wait_recv()
        for cp in first + passed:
            cp.wait_send()
        for cp in mine:
            cp.wait()

    return pl.pallas_call(
        body, name=name, out_shape=[_sds((N_DEV,) + v.shape, v.dtype) for v in vs], in_specs=[ANY] * n,
        out_specs=[ANY] * n,
        scratch_shapes=[pltpu.SemaphoreType.DMA((n, 7)), pltpu.SemaphoreType.DMA((n, 7)), pltpu.SemaphoreType.DMA((n,))],
    )(*vs)


def _sum_slots(t, name):
    n, R, C = t.shape
    tr = _tile(R, PACK_ROW_TILE, 16)

    def body(t_ref, o_ref):
        acc = t_ref[0].astype(f32)
        for k in range(1, n):
            acc = acc + t_ref[k].astype(f32)
        o_ref[...] = acc

    return pl.pallas_call(
        body, name=name, grid=(R // tr,),
        in_specs=[pl.BlockSpec((n, tr, C), lambda i: (0, i, 0))],
        out_specs=pl.BlockSpec((tr, C), lambda i: (i, 0)), out_shape=_sds((R, C)),
        compiler_params=_params(("parallel",)),
    )(t)


HBM_SPEC = pl.BlockSpec(memory_space=pltpu.HBM)
SEM_SPEC = pl.BlockSpec(memory_space=pltpu.SEMAPHORE)
EFFECT = pltpu.SideEffectType.DATAFLOW_SIDE_EFFECTING


def _mesh_pos(p):
    return (p // 4, (p // 2) % 2, p % 2)


def _exchange_copy(src_refs, land_refs, send_sems, recv_sems, whole, dests, i, k):
    me = 4 * lax.axis_index("x") + 2 * lax.axis_index("y") + lax.axis_index("c")
    to = (me + k) % N_DEV
    frm = (me + N_DEV - k) % N_DEV
    lo, hi = dests
    src = src_refs[i] if whole else src_refs[i].at[jnp.minimum(jnp.maximum(to - lo, 0), hi - lo - 1)]
    s = i * (N_DEV - 1) + k - 1
    send = pltpu.make_async_remote_copy(src_ref=src, dst_ref=land_refs[i].at[me], send_sem=send_sems.at[s],
                                        recv_sem=recv_sems.at[s], device_id=_mesh_pos(to), device_id_type=MESH)
    recv = pltpu.make_async_remote_copy(src_ref=src, dst_ref=land_refs[i].at[frm], send_sem=send_sems.at[s],
                                        recv_sem=recv_sems.at[s], device_id=_mesh_pos(to), device_id_type=MESH)
    return send, recv, (to >= lo) & (to < hi), (me >= lo) & (me < hi)


def _exchange_start(srcs, whole, name, after=None, dests=(0, N_DEV)):
    n = len(srcs)
    lands = [lax.empty((N_DEV,) + s.shape[-2:], s.dtype) for s in srcs]
    after = list(after or [])
    n_in = 2 * n + len(after)
    everyone = dests == (0, N_DEV)

    def body(*refs):
        src_refs, land_refs = refs[:n], refs[n:2 * n]
        send_sems, recv_sems, token = refs[n_in], refs[n_in + 1], refs[-1]
        for i in range(n):
            for k in range(1, N_DEV):
                send, _, sends, _ = _exchange_copy(src_refs, land_refs, send_sems, recv_sems, whole, dests, i, k)
                if everyone:
                    send.start()
                else:
                    pl.when(sends)(send.start)
        token[...] = jnp.zeros_like(token)

    sems = pltpu.SemaphoreType.DMA((n * (N_DEV - 1),))
    outs = pl.pallas_call(
        body, name=name,
        out_shape=(sems, sems, *[pltpu.HBM(a.shape, a.dtype) for a in srcs + lands], _sds((8, LANES))),
        in_specs=[HBM_SPEC] * (2 * n) + [pl.BlockSpec(memory_space=pl.ANY)] * len(after),
        out_specs=(SEM_SPEC, SEM_SPEC, *[HBM_SPEC] * (2 * n), pl.BlockSpec(memory_space=pltpu.VMEM)),
        input_output_aliases={i: 2 + i for i in range(2 * n)},
        compiler_params=pltpu.CompilerParams(has_side_effects=EFFECT),
    )(*[pltpu.with_memory_space_constraint(a, pltpu.HBM) for a in srcs + lands], *after)
    return (outs[0], outs[1], list(outs[2:2 + n]), list(outs[2 + n:2 + 2 * n]), whole, dests), outs[-1]


def _exchange_wait(handle, after, name):
    send_sems, recv_sems, srcs, lands, whole, dests = handle
    n = len(srcs)
    everyone = dests == (0, N_DEV)

    def body(*refs):
        src_refs, land_refs = refs[:n], refs[n:2 * n]
        send_sems, recv_sems = refs[2 * n], refs[2 * n + 1]
        for i in range(n):
            for k in range(1, N_DEV):
                send, recv, sends, receives = _exchange_copy(src_refs, land_refs, send_sems, recv_sems, whole, dests, i, k)
                if everyone:
                    send.wait_send()
                    recv.wait_recv()
                else:
                    pl.when(sends)(send.wait_send)
                    pl.when(receives)(recv.wait_recv)

    outs = pl.pallas_call(
        body, name=name, out_shape=tuple(pltpu.HBM(a.shape, a.dtype) for a in srcs + lands),
        in_specs=[HBM_SPEC] * (2 * n) + [SEM_SPEC, SEM_SPEC, pl.BlockSpec(memory_space=pl.ANY)],
        out_specs=[HBM_SPEC] * (2 * n), input_output_aliases={i: i for i in range(2 * n)},
        compiler_params=pltpu.CompilerParams(has_side_effects=EFFECT),
    )(*srcs, *lands, send_sems, recv_sems, after)
    return list(outs[n:])


def _tie(v, token):
    return v + token[0:1, 0:1].astype(v.dtype).reshape((1,) * v.ndim)


def _with_own(land, own, me):
    return lax.dynamic_update_slice_in_dim(land, own[None].astype(land.dtype), me, 0)


class _Overlap:
    def __init__(self, shards, me, after):
        self.me = me
        self.names = list(shards)
        self.handle, self.token = _exchange_start([shards[nm] for nm in self.names], True, "weights_start", after)
        self.sent = {}

    def weights(self, after):
        lands = _exchange_wait(self.handle, after, "weights_wait")
        own = self.handle[2]
        return {nm: _full_from_blocks(nm, _with_own(land, o, self.me)) for nm, land, o in zip(self.names, lands, own)}

    def send(self, tag, grads):
        names = list(grads)
        handle, token = _exchange_start([_blocks_from_full(nm, grads[nm]) for nm in names], False, f"grads_start_{tag}")
        self.sent[tag] = (names, handle)
        return token

    def send_rows(self, tag, rows, dests):
        lo, hi = dests
        blocks = rows.reshape(hi - lo, rows.shape[0] // (hi - lo), rows.shape[1])
        handle, token = _exchange_start([blocks], False, f"grads_start_{tag}", None, dests)
        self.sent[tag] = ([tag], handle)
        return token

    def received(self, tag, after):
        names, handle = self.sent[tag]
        lands = _exchange_wait(handle, after, f"grads_wait_{tag}")
        lo = handle[5][0]
        own = [lax.dynamic_index_in_dim(b, self.me - lo, 0, keepdims=False) for b in handle[2]]
        return {nm: _with_own(land, o, self.me) for nm, land, o in zip(names, lands, own)}


ADAM_ROWS = 32


def _adamw(w, g, m, v, name):
    deep = w.ndim == 3
    R, C = w.shape[0], w.shape[-1]
    cb = LANES if C % LANES == 0 else C
    n_parts = g.shape[0] if g.ndim == 3 else 0

    def body(w_ref, g_ref, m_ref, v_ref, d_ref, m2_ref, v2_ref, *g_out):
        at = (lambda ref, sl: ref.at[sl, 0, :]) if deep else (lambda ref, sl: ref.at[sl, :])

        def update(sl):
            if n_parts:
                gv = g_ref[0, sl, :].astype(f32)
                for k in range(1, n_parts):
                    gv = gv + g_ref[k, sl, :].astype(f32)
                at(g_out[0], sl)[...] = gv
            else:
                gv = g_ref[sl, :]
            m2 = ADAM_B1 * at(m_ref, sl)[...] + (1.0 - ADAM_B1) * gv
            v2 = ADAM_B2 * at(v_ref, sl)[...] + (1.0 - ADAM_B2) * jnp.square(gv)
            m_hat = m2 / (1.0 - ADAM_B1 ** ADAM_STEP)
            v_hat = v2 / (1.0 - ADAM_B2 ** ADAM_STEP)
            at(d_ref, sl)[...] = -ADAM_LR * (m_hat / (jnp.sqrt(v_hat) + ADAM_EPS) + ADAM_WD * at(w_ref, sl)[...])
            at(m2_ref, sl)[...] = m2
            at(v2_ref, sl)[...] = v2

        main = R // ADAM_ROWS
        if main:
            lax.fori_loop(0, main, lambda i, c: (update(pl.ds(pl.multiple_of(i * ADAM_ROWS, ADAM_ROWS), ADAM_ROWS)), c)[1], 0)
        if R % ADAM_ROWS:
            update(pl.ds(main * ADAM_ROWS, R % ADAM_ROWS))

    spec = pl.BlockSpec((R, 1, cb), lambda j: (0, 0, j)) if deep else pl.BlockSpec((R, cb), lambda j: (0, j))
    g_spec = pl.BlockSpec((n_parts, R, cb), lambda j: (0, 0, j)) if n_parts else pl.BlockSpec((R, cb), lambda j: (0, j))
    n_out = 4 if n_parts else 3
    return pl.pallas_call(
        body, name=name, grid=(C // cb,), in_specs=[spec, g_spec, spec, spec], out_specs=[spec] * n_out,
        out_shape=[_sds(w.shape)] * n_out, compiler_params=_params(("parallel",)),
    )(w, g, m, v)


BIG_PARAMS = ("hy_w_in", "hy_w_out", "cv_w_pw1", "cv_w_pw2", "ffn_w_gate", "ffn_w_up", "ffn_w_down")


def _shards_2d(w):
    t = lambda a: jnp.transpose(a)
    return dict(in_t=t(w["hy_w_in"][0]), out=w["hy_w_out"][0], pw1=w["cv_w_pw1"][0], pw2=w["cv_w_pw2"][0],
                gate_t0=t(w["ffn_w_gate"][0]), gate_t1=t(w["ffn_w_gate"][1]), up_t0=t(w["ffn_w_up"][0]),
                up_t1=t(w["ffn_w_up"][1]), down0=w["ffn_w_down"][0], down1=w["ffn_w_down"][1])


def _unshard_2d(s):
    t = lambda a: jnp.transpose(a)
    out = dict(hy_w_out=s["out"][None], cv_w_pw1=s["pw1"][None], cv_w_pw2=s["pw2"][None],
               ffn_w_gate=jnp.stack([t(s["gate_t0"]), t(s["gate_t1"])]),
               ffn_w_up=jnp.stack([t(s["up_t0"]), t(s["up_t1"])]), ffn_w_down=jnp.stack([s["down0"], s["down1"]]))
    if "in_t" in s:
        out["hy_w_in"] = t(s["in_t"])[None]
    return out


def _full_from_blocks(nm, g):
    if nm == "pw1":
        return jnp.transpose(g, (1, 0, 2)).reshape(g.shape[1], N_DEV * g.shape[2])
    return g.reshape(N_DEV * g.shape[1], g.shape[2])


def _blocks_from_full(nm, g):
    if nm == "pw1":
        return jnp.transpose(g.reshape(g.shape[0], N_DEV, g.shape[1] // N_DEV), (1, 0, 2))
    return g.reshape(N_DEV, g.shape[0] // N_DEV, g.shape[1])


class _VecPack:
    def __init__(self, shapes):
        self.shapes = [tuple(s) for s in shapes]
        self.sizes = [int(np.prod(s)) for s in self.shapes]
        total = sum(self.sizes)
        self.rows = -(-(-(-total // LANES)) // 8) * 8
        self.total = total

    def pack(self, arrays):
        flat = jnp.concatenate([a.astype(f32).reshape(-1) for a in arrays])
        flat = jnp.pad(flat, (0, self.rows * LANES - self.total))
        return flat.reshape(self.rows, LANES)

    def unpack(self, packed):
        flat = packed.reshape(-1)
        out, off = [], 0
        for shp, n in zip(self.shapes, self.sizes):
            out.append(flat[off:off + n].reshape(shp))
            off += n
        return out

    def unpack_stacked(self, stacked, only=None):
        flat = stacked.reshape(stacked.shape[0], -1)
        offs = np.concatenate([[0], np.cumsum(self.sizes)])
        get = lambda i: flat[:, offs[i]:offs[i + 1]].reshape((stacked.shape[0],) + self.shapes[i])
        return get(only) if only is not None else [get(i) for i in range(len(self.shapes))]


def _row(v):
    return v.reshape(1, -1)


def _pad_lanes(v):
    v = v.reshape(1, -1)
    return jnp.pad(v, ((0, 0), (0, LANES - v.shape[1])))


def _ffn_fwd(h, w_gate_t, w_up_t, w_down, tag):
    F = w_down.shape[0]
    a = _mm(h, w_gate_t, tb=True, out_dtype=bf16, name=f"ffn_gate_{tag}")
    u = _mm(h, w_up_t, tb=True, out_dtype=bf16, name=f"ffn_up_{tag}")
    (f,), _ = _rowwise(f"swiglu_{tag}", lambda rv, vv: ([_silu(rv[0].astype(f32)) * rv[1].astype(f32)], []), [a, u], [],
                       [(F, bf16)], [], sub=16, col_chunk=_tile(F, 512, LANES))
    out = _mm(f, w_down, name=f"ffn_down_{tag}")
    return out, (a, u, f)


def _ffn_bwd(h, w_gate_t, w_up_t, w_down, saved, dout, tag):
    a, u, f = saved
    F = w_down.shape[0]
    df = _mm(dout, w_down, tb=True, out_dtype=bf16, name=f"ffn_down_dx_{tag}")
    dw_down = _mm(f, dout, ta=True, out_dtype=bf16, name=f"ffn_down_dw_{tag}")

    def fn(rv, vv):
        av, uv, dv = rv[0].astype(f32), rv[1].astype(f32), rv[2].astype(f32)
        sig = jax.nn.sigmoid(av)
        act = av * sig
        return [dv * uv * (sig + act * (1.0 - sig)), dv * act], []

    (da, du), _ = _rowwise(f"swiglu_bwd_{tag}", fn, [a, u, df], [], [(F, bf16), (F, bf16)], [], sub=16,
                           col_chunk=_tile(F, 512, LANES))
    dh = _mm(du, w_up_t, add=_mm(da, w_gate_t, name=f"ffn_gate_dx_{tag}"), name=f"ffn_up_dx_{tag}")
    dw_gate_t = _mm(da, h, ta=True, out_dtype=bf16, name=f"ffn_gate_dw_{tag}")
    dw_up_t = _mm(du, h, ta=True, out_dtype=bf16, name=f"ffn_up_dw_{tag}")
    return dh, dw_gate_t, dw_up_t, dw_down


def _local_step(x, target, mod, w_in_t, comm, small):
    S, D = x.shape
    di = small["hy_ssm_norm_g"].shape[-1]
    nh = small["hy_dt_bias"].shape[-1]
    cd = small["hy_conv_b"].shape[-1]
    m = [[_row(mod[i, j]) for j in range(6)] for i in range(2)]

    off_q = di + cd + nh
    w_qkv_t = w_in_t[off_q:]
    seg = dict(z=(w_in_t, 0, di), xbc=(w_in_t, di, cd), dt=(w_in_t, di + cd, LANES))
    for i, nm in enumerate(("q0", "q1", "q2", "k", "v")):
        seg[nm] = (w_qkv_t, i * D, D)

    g_mix = [_row(small["norm_mix_g"][i]) for i in range(2)]
    g_ffn = [_row(small["norm_ffn_g"][i]) for i in range(2)]
    conv_w, conv_b = small["hy_conv_w_full"], _row(small["hy_conv_b"][0])
    dt_bias, a_log, d_skip = (_pad_lanes(small[k][0]) for k in ("hy_dt_bias", "hy_a_log", "hy_d_skip"))
    g_ssm = _row(small["hy_ssm_norm_g"][0])
    onehot = jnp.asarray(_bucket_onehot())
    rel_t = small["rel_table"].T
    H = D // HEAD_DIM
    bias = [_exact_mm(rel_t[gi * H:(gi + 1) * H], onehot[gi], name=f"rel_bias_{gi}")
            .reshape(H, 2 * ATT_BLK, ATT_BLK) + _band_penalty() for gi in range(3)]

    h1 = _adaln_fwd(x, g_mix[0], m[0][1], m[0][0], "adaln_mix0")
    proj = {nm: _mm(h1, mat, tb=True, b_rows=(off, cnt), name=f"in_{nm}") for nm, (mat, off, cnt) in seg.items()}
    xbc_pre, xbc = _conv_fwd(proj["xbc"], conv_w, conv_b, silu=True, name="ssm_conv", tr=1024)
    y, hin = _ssd_fwd(xbc, proj["dt"], dt_bias, a_log, d_skip, di, "ssd_fwd")
    (yg,), _ = _rowwise("ssm_gate", lambda rv, vv: ([_gate_f(rv[0], rv[1], vv[0])], []),
                        [y, proj["z"]], [g_ssm], [(di, bf16)], [], sub=16)
    og = [_attn_fwd(proj[f"q{gi}"], proj["k"], proj["v"], bias[gi], d, f"attn_fwd_{gi}")
          for gi, d in enumerate(ATT_DILATIONS)]
    att, att_b, lse_tot = _attn_combine([a for a, _ in og], [_from_blocks(b, LANES) for _, b in og], "attn_combine")
    W = comm.weights(after=att_b)
    w_out_y, w_out_a = W["out"][:di], W["out"][di:]
    mix0 = _mm(att_b, w_out_a, add=_mm(yg, w_out_y, name="out_y"), name="out_a")
    x1, h2 = _resid_adaln_fwd(x, m[0][2], mix0, g_ffn[0], m[0][4], m[0][3], "resid_mix0_adaln_ffn0")
    f0, ffn0_saved = _ffn_fwd(h2, W["gate_t0"], W["up_t0"], W["down0"], "0")
    x2, h3 = _resid_adaln_fwd(x1, m[0][5], f0, g_mix[1], m[1][1], m[1][0], "resid_ffn0_adaln_mix1")
    pw1 = _mm(h3, W["pw1"], bias=_row(small["cv_b_pw1_full"]), name="cv_pw1")
    (u,), _ = _rowwise("cv_glu", lambda rv, vv: ([rv[0] * jax.nn.sigmoid(rv[1])], []),
                       [(pw1, 0, D), (pw1, 1, D)], [], [(D, f32)], [])
    (u2,) = _conv_fwd(u, small["cv_w_dw_full"], _row(small["cv_b_dw_full"]), silu=False, name="cv_dw")
    ln_g, ln_b = _row(small["cv_ln_g_full"]), _row(small["cv_ln_b_full"])
    (u3,), _ = _rowwise("cv_lnsilu", lambda rv, vv: ([_lnsilu_f(rv[0], vv[0], vv[1])], []),
                        [u2], [ln_g, ln_b], [(D, bf16)], [], sub=16)
    mix1 = _mm(u3, W["pw2"], bias=_row(small["cv_b_pw2_full"]), name="cv_pw2")
    x3, h4 = _resid_adaln_fwd(x2, m[1][2], mix1, g_ffn[1], m[1][4], m[1][3], "resid_mix1_adaln_ffn1")
    f1, ffn1_saved = _ffn_fwd(h4, W["gate_t1"], W["up_t1"], W["down1"], "1")

    g_fin = _row(small["final_norm_g"])
    dmod = [[None] * 6 for _ in range(2)]
    d_norm_mix, d_norm_ffn = [None, None], [None, None]
    big = {}

    def final_fn(rv, vv):
        xv, fv, tv = rv
        gate = vv[1]
        yv, vjp = jax.vjp(_rms, xv + gate * fv, vv[0])
        err = yv - tv
        dx, dg = vjp(err / D)
        part = 0.5 * jnp.sum(jnp.mean(err * err, -1, keepdims=True), 0, keepdims=True)
        return [dx, gate * dx], [dg, jnp.broadcast_to(part, (1, LANES)), jnp.sum(dx * fv, 0, keepdims=True)]

    (dx4, df1), (d_fin, loss, dmod[1][5]) = _rowwise("loss_head", final_fn, [x3, f1, target], [g_fin, m[1][5]],
                                                      [(D, f32), (D, bf16)], [D, LANES, D], sub=16)

    dh4, big["gate_t1"], big["up_t1"], big["down1"] = _ffn_bwd(h4, W["gate_t1"], W["up_t1"], W["down1"], ffn1_saved, df1, "1")
    dx3, dmix1, (d_norm_ffn[1], dmod[1][4], dmod[1][3], dmod[1][2], d_b_pw2) = _adaln_resid_bwd(
        x3, g_ffn[1], m[1][4], m[1][3], dh4, dx4, mix1, m[1][2], "adaln_ffn1_resid_mix1_bwd")
    du3 = _mm(dmix1, W["pw2"], tb=True, name="cv_pw2_dx")
    big["pw2"] = _mm(u3, dmix1, ta=True, out_dtype=bf16, name="cv_pw2_dw")

    def lnsilu_bwd(rv, vv):
        _, vjp = jax.vjp(_lnsilu_f, rv[0], vv[0], vv[1])
        du, dg, db = vjp(rv[1])
        return [du], [dg, db]

    (du2,), (d_ln_g, d_ln_b) = _rowwise("cv_lnsilu_bwd", lnsilu_bwd, [u2, du3], [ln_g, ln_b], [(D, f32)], [D, D])
    du, d_w_dw, d_b_dw = _conv_bwd(u, small["cv_w_dw_full"], du2, None, silu=False, name="cv_dw_bwd")

    def glu_bwd(rv, vv):
        a, gt, d = rv
        _, vjp = jax.vjp(lambda a_, g_: a_ * jax.nn.sigmoid(g_), a, gt)
        da, dg = vjp(d)
        return [da, dg], [jnp.sum(da, 0, keepdims=True), jnp.sum(dg, 0, keepdims=True)]

    (dpa, dpg), (d_b1a, d_b1g) = _rowwise("cv_glu_bwd", glu_bwd, [(pw1, 0, D), (pw1, 1, D), du], [],
                                           [(D, bf16), (D, bf16)], [D, D], sub=16)
    dpw1 = jnp.concatenate([dpa, dpg], axis=1)
    d_b_pw1 = jnp.concatenate([d_b1a, d_b1g], axis=1)
    dh3 = _mm(dpw1, W["pw1"], tb=True, name="cv_pw1_dx")
    big["pw1"] = _mm(h3, dpw1, ta=True, out_dtype=bf16, name="cv_pw1_dw")
    token = comm.send("layer1", {nm: big[nm] for nm in ("gate_t1", "up_t1", "down1", "pw2", "pw1")})
    dx2, df0, (d_norm_mix[1], dmod[1][1], dmod[1][0], dmod[0][5], _) = _adaln_resid_bwd(
        x2, g_mix[1], m[1][1], _tie(m[1][0], token), dh3, dx3, f0, m[0][5], "adaln_mix1_resid_ffn0_bwd")

    dh2, big["gate_t0"], big["up_t0"], big["down0"] = _ffn_bwd(h2, W["gate_t0"], W["up_t0"], W["down0"], ffn0_saved, df0, "0")
    dx1, dmix0, (d_norm_ffn[0], dmod[0][4], dmod[0][3], dmod[0][2], _) = _adaln_resid_bwd(
        x1, g_ffn[0], m[0][4], m[0][3], dh2, dx2, mix0, m[0][2], "adaln_ffn0_resid_mix0_bwd")
    dyg = _mm(dmix0, w_out_y, tb=True, name="out_y_dx")
    datt = _mm(dmix0, w_out_a, tb=True, name="out_a_dx")
    big["out"] = jnp.concatenate([_mm(yg, dmix0, ta=True, out_dtype=bf16, name="out_y_dw"),
                                  _mm(att_b, dmix0, ta=True, out_dtype=bf16, name="out_a_dw")], axis=0)
    token = comm.send("layer0", {nm: big[nm] for nm in ("gate_t0", "up_t0", "down0", "out")})
    g_ssm = _tie(g_ssm, token)

    def gate_bwd(rv, vv):
        _, vjp = jax.vjp(_gate_f, rv[0], rv[1], vv[0])
        dy_, dz_, dg_ = vjp(rv[2])
        return [dy_, dz_], [dg_]

    (dy, dz), (d_g_ssm,) = _rowwise("ssm_gate_bwd", gate_bwd, [y, proj["z"], dyg], [g_ssm], [(di, f32), (di, bf16)], [di],
                                    sub=16)
    dxbc, ddtraw, d_a_log, d_dskip, d_dt_bias = _ssd_bwd(xbc, proj["dt"], dt_bias, a_log, d_skip, hin, y, dy, di, "ssd_bwd")
    dxbc_pre, d_conv_w, d_conv_b = _conv_bwd(proj["xbc"], conv_w, dxbc, xbc_pre, silu=True, name="ssm_conv_bwd",
                                             dx_dtype=bf16, tr=1024)
    dh1 = None
    early = []
    for nm, dseg in (("z", dz), ("xbc", dxbc_pre), ("dt", ddtraw)):
        mat, off, cnt = seg[nm]
        dh1 = _mm(dseg, mat, b_rows=(off, cnt), add=dh1, name=f"in_{nm}_dx")
        dwp = _mm(dseg, h1, ta=True, out_dtype=bf16, name=f"in_{nm}_dw")
        early.append(dwp[:nh] if nm == "dt" else dwp)
    early = jnp.concatenate(early, axis=0)
    shard_rows = w_in_t.shape[0] // N_DEV
    n_early = off_q // shard_rows
    token = comm.send_rows("in_early", early[:n_early * shard_rows], (0, n_early))
    bias = [_tie(b, token) for b in bias]

    dq, dks, dvs, dbs = [], [], [], []
    lse_heads = lse_tot[:, :H]
    dsum_heads = _head_sums(att, datt, "attn_dsum")[:, :H]
    for gi, d in enumerate(ATT_DILATIONS):
        a, b, c_, e = _attn_bwd(proj[f"q{gi}"], proj["k"], proj["v"], bias[gi], datt,
                                _by_block(lse_heads, d), _by_block(dsum_heads, d), d, f"attn_bwd_{gi}")
        dq.append(a)
        dks.append(b)
        dvs.append(c_)
        dbs.append(e)
    dk = _add3(*dks, "attn_dk")
    dv = _add3(*dvs, "attn_dv")
    d_rel = jnp.concatenate(
        [_exact_mm(dbs[gi].reshape(H, -1), onehot[gi], tb=True, name=f"rel_grad_{gi}") for gi in range(3)], axis=0).T

    dsegs = (("q0", dq[0]), ("q1", dq[1]), ("q2", dq[2]), ("k", dk), ("v", dv))
    late = jnp.concatenate([early[n_early * shard_rows:]] +
                           [_mm(dseg, h1, ta=True, out_dtype=bf16, name=f"in_{nm}_dw") for nm, dseg in dsegs], axis=0)
    token = comm.send_rows("in_late", late, (n_early, N_DEV))
    w_qkv_after = _tie(w_qkv_t, token)
    for nm, dseg in dsegs:
        _, off, cnt = seg[nm]
        dh1 = _mm(dseg, w_qkv_after, b_rows=(off, cnt), add=dh1, name=f"in_{nm}_dx")
    dx0, (d_norm_mix[0], dmod[0][1], dmod[0][0]) = _adaln_bwd(x, g_mix[0], m[0][1], m[0][0], dh1, dx1, "adaln_mix0_bwd")

    smallg = dict(
        loss=loss, dmod=jnp.stack([jnp.concatenate(dmod[i], axis=1)[0] for i in range(2)]),
        norm_mix_g=jnp.concatenate(d_norm_mix, axis=0), norm_ffn_g=jnp.concatenate(d_norm_ffn, axis=0),
        hy_conv_w=d_conv_w, hy_conv_b=d_conv_b, hy_dt_bias=d_dt_bias[:, :nh], hy_a_log=d_a_log[:, :nh],
        hy_d_skip=d_dskip[:, :nh], hy_ssm_norm_g=d_g_ssm, rel_table=d_rel,
        cv_b_pw1=d_b_pw1, cv_w_dw=d_w_dw, cv_b_dw=d_b_dw, cv_ln_g=d_ln_g, cv_ln_b=d_ln_b, cv_b_pw2=d_b_pw2,
        final_norm_g=d_fin)
    return dx0, n_early, smallg


SMALL_GRAD_ORDER = ("loss", "dmod", "norm_mix_g", "norm_ffn_g", "hy_conv_w", "hy_conv_b", "hy_dt_bias", "hy_a_log",
                    "hy_d_skip", "hy_ssm_norm_g", "rel_table", "cv_b_pw1", "cv_w_dw", "cv_b_dw", "cv_ln_g", "cv_ln_b",
                    "cv_b_pw2", "final_norm_g")


def kernel(x, c, ada_w, ada_b, norm_mix_g, norm_ffn_g, hy_w_in, hy_conv_w, hy_conv_b, hy_dt_bias, hy_a_log, hy_d_skip, hy_ssm_norm_g, hy_w_out, rel_table, cv_w_pw1, cv_b_pw1, cv_w_dw, cv_b_dw, cv_ln_g, cv_ln_b, cv_w_pw2, cv_b_pw2, ffn_w_gate, ffn_w_up, ffn_w_down, final_norm_g, loss_target, m_ada_w, m_ada_b, m_norm_mix_g, m_norm_ffn_g, m_hy_w_in, m_hy_conv_w, m_hy_conv_b, m_hy_dt_bias, m_hy_a_log, m_hy_d_skip, m_hy_ssm_norm_g, m_hy_w_out, m_rel_table, m_cv_w_pw1, m_cv_b_pw1, m_cv_w_dw, m_cv_b_dw, m_cv_ln_g, m_cv_ln_b, m_cv_w_pw2, m_cv_b_pw2, m_ffn_w_gate, m_ffn_w_up, m_ffn_w_down, m_final_norm_g, v_ada_w, v_ada_b, v_norm_mix_g, v_norm_ffn_g, v_hy_w_in, v_hy_conv_w, v_hy_conv_b, v_hy_dt_bias, v_hy_a_log, v_hy_d_skip, v_hy_ssm_norm_g, v_hy_w_out, v_rel_table, v_cv_w_pw1, v_cv_b_pw1, v_cv_w_dw, v_cv_b_dw, v_cv_ln_g, v_cv_ln_b, v_cv_w_pw2, v_cv_b_pw2, v_ffn_w_gate, v_ffn_w_up, v_ffn_w_down, v_final_norm_g):
    names = ("ada_w", "ada_b", "norm_mix_g", "norm_ffn_g", "hy_w_in", "hy_conv_w", "hy_conv_b", "hy_dt_bias", "hy_a_log",
             "hy_d_skip", "hy_ssm_norm_g", "hy_w_out", "rel_table", "cv_w_pw1", "cv_b_pw1", "cv_w_dw", "cv_b_dw", "cv_ln_g",
             "cv_ln_b", "cv_w_pw2", "cv_b_pw2", "ffn_w_gate", "ffn_w_up", "ffn_w_down", "final_norm_g")
    w = dict(zip(names, (ada_w, ada_b, norm_mix_g, norm_ffn_g, hy_w_in, hy_conv_w, hy_conv_b, hy_dt_bias, hy_a_log, hy_d_skip,
                         hy_ssm_norm_g, hy_w_out, rel_table, cv_w_pw1, cv_b_pw1, cv_w_dw, cv_b_dw, cv_ln_g, cv_ln_b, cv_w_pw2,
                         cv_b_pw2, ffn_w_gate, ffn_w_up, ffn_w_down, final_norm_g)))
    mom = dict(zip(names, (m_ada_w, m_ada_b, m_norm_mix_g, m_norm_ffn_g, m_hy_w_in, m_hy_conv_w, m_hy_conv_b, m_hy_dt_bias,
                           m_hy_a_log, m_hy_d_skip, m_hy_ssm_norm_g, m_hy_w_out, m_rel_table, m_cv_w_pw1, m_cv_b_pw1, m_cv_w_dw,
                           m_cv_b_dw, m_cv_ln_g, m_cv_ln_b, m_cv_w_pw2, m_cv_b_pw2, m_ffn_w_gate, m_ffn_w_up, m_ffn_w_down,
                           m_final_norm_g)))
    vel = dict(zip(names, (v_ada_w, v_ada_b, v_norm_mix_g, v_norm_ffn_g, v_hy_w_in, v_hy_conv_w, v_hy_conv_b, v_hy_dt_bias,
                           v_hy_a_log, v_hy_d_skip, v_hy_ssm_norm_g, v_hy_w_out, v_rel_table, v_cv_w_pw1, v_cv_b_pw1, v_cv_w_dw,
                           v_cv_b_dw, v_cv_ln_g, v_cv_ln_b, v_cv_w_pw2, v_cv_b_pw2, v_ffn_w_gate, v_ffn_w_up, v_ffn_w_down,
                           v_final_norm_g)))
    S, D = x.shape[1], x.shape[2]
    ax, ay, ac = lax.axis_index("x"), lax.axis_index("y"), lax.axis_index("c")
    me = 4 * ax + 2 * ay + ac
    nmod = ada_w.shape[2]

    w2 = _shards_2d(w)
    big_names = list(w2)
    sharded_small = ("hy_conv_w", "cv_b_pw1", "cv_w_dw", "cv_b_dw", "cv_ln_g", "cv_ln_b", "cv_b_pw2")
    vp = _VecPack([c.shape] + [w[nm].shape for nm in sharded_small])
    g_in, sg = _all_gather([w2["in_t"].astype(bf16), vp.pack([c] + [w[nm] for nm in sharded_small])], "gather_w_in")
    w_in_t = _full_from_blocks("in_t", g_in)
    parts = vp.unpack_stacked(sg)
    c_all = parts[0][:, 0]
    small = {k: w[k] for k in ("norm_mix_g", "norm_ffn_g", "hy_conv_b", "hy_dt_bias", "hy_a_log", "hy_d_skip",
                               "hy_ssm_norm_g", "rel_table", "final_norm_g")}
    for p, nm in zip(parts[1:], sharded_small):
        p = p[:, 0]
        p = jnp.moveaxis(p, 0, -2)
        small[nm + "_full"] = p.reshape(p.shape[:-2] + (N_DEV * p.shape[-1],))

    (cs_all,), _ = _rowwise("ada_silu", lambda rv, vv: ([_silu(rv[0])], []), [c_all], [], [(D, f32)], [])
    b_mine = lax.dynamic_slice_in_dim(ada_b, me * nmod, nmod, axis=1)
    mod_part = jnp.stack([_mm(cs_all, ada_w[i], bias=b_mine[i:i + 1], name=f"ada_mod_{i}") for i in range(2)])
    (mod_all,) = _all_gather([mod_part.reshape(2 * N_DEV, nmod)], "gather_mod")
    mod_all = mod_all.reshape(N_DEV, 2, N_DEV, nmod)
    mod_mine = lax.dynamic_index_in_dim(mod_all, me, axis=2, keepdims=False)
    mod = jnp.transpose(mod_mine, (1, 0, 2)).reshape(2, 6, D)
    comm = _Overlap({nm: w2[nm].astype(bf16) for nm in big_names if nm != "in_t"}, me, after=[mod, w_in_t])
    mod = _tie(mod, comm.token)

    dx0, n_early, sgrad = _local_step(x[0], loss_target[0], mod, w_in_t, comm, small)

    gp = _VecPack([sgrad[k].shape for k in SMALL_GRAD_ORDER])
    small_handle, after = _exchange_start([gp.pack([sgrad[k] for k in SMALL_GRAD_ORDER])], True, "small_grads_start")
    m2, v2 = _shards_2d(mom), _shards_2d(vel)
    g2, d2, nm2, nv2 = {}, {}, {}, {}
    slots = {}
    for tag in ("layer1", "layer0"):
        slots.update(comm.received(tag, after))
        for nm in comm.sent[tag][0]:
            d2[nm], nm2[nm], nv2[nm], g2[nm] = _adamw(w2[nm], slots[nm], m2[nm], v2[nm], f"adamw_{nm}")
            after = g2[nm]
    (g_all,) = _exchange_wait(small_handle, after, "small_grads_wait")
    g_all = _with_own(g_all, small_handle[2][0], me)
    tot = dict(zip(SMALL_GRAD_ORDER, gp.unpack(_sum_slots(g_all, "sum_small_grads"))))
    dmod_all = gp.unpack_stacked(g_all, only=SMALL_GRAD_ORDER.index("dmod"))
    loss = tot["loss"][0, 0]

    grads = {}
    dmod_mine = lax.dynamic_slice_in_dim(dmod_all, me * nmod, nmod, axis=2)
    grads["ada_w"] = jnp.stack([_mm(cs_all, dmod_mine[:, i], ta=True, name=f"ada_w_grad_{i}") for i in range(2)])
    grads["ada_b"] = tot["dmod"]
    grads["norm_mix_g"], grads["norm_ffn_g"] = tot["norm_mix_g"], tot["norm_ffn_g"]
    grads["hy_conv_b"] = tot["hy_conv_b"]
    grads["hy_dt_bias"] = tot["hy_dt_bias"]
    grads["hy_a_log"] = tot["hy_a_log"]
    grads["hy_d_skip"] = tot["hy_d_skip"]
    grads["hy_ssm_norm_g"] = tot["hy_ssm_norm_g"]
    grads["rel_table"] = tot["rel_table"]
    grads["final_norm_g"] = tot["final_norm_g"][0]
    for nm in sharded_small:
        n = w[nm].shape[-1]
        grads[nm] = lax.dynamic_slice_in_dim(tot[nm], me * n, n, axis=1).reshape(w[nm].shape)

    delta, new_m, new_v = {}, {}, {}
    shp = ada_w.shape
    two = lambda t: t.reshape(-1, shp[-1])
    d_, m_, v_ = _adamw(two(ada_w), two(grads["ada_w"]), two(m_ada_w), two(v_ada_w), "adamw_ada_w")
    delta["ada_w"], new_m["ada_w"], new_v["ada_w"] = d_.reshape(shp), m_.reshape(shp), v_.reshape(shp)
    rest = [nm for nm in names if nm not in BIG_PARAMS and nm != "ada_w"]
    sp = _VecPack([w[nm].shape for nm in rest])
    packs = [sp.pack([t[nm] for nm in rest]) for t in (w, grads, mom, vel)]
    ds_, ms_, vs_ = _adamw(*packs, "adamw_small")
    for nm, a, b, e in zip(rest, sp.unpack(ds_), sp.unpack(ms_), sp.unpack(vs_)):
        delta[nm], new_m[nm], new_v[nm] = a, b, e

    slots.update(comm.received("in_early", ds_))
    slots.update(comm.received("in_late", slots["in_early"]))
    stored = lambda t: jnp.transpose(t, (2, 0, 1))
    in_slots = jnp.where(me < n_early, slots["in_early"], slots["in_late"])
    d_in, m_in, v_in, g_in = _adamw(stored(hy_w_in), in_slots, stored(m_hy_w_in), stored(v_hy_w_in), "adamw_in_t")
    for dst, part, t in ((grads, g2, g_in), (delta, d2, d_in), (new_m, nm2, m_in), (new_v, nv2, v_in)):
        dst.update(_unshard_2d(part))
        dst["hy_w_in"] = jnp.transpose(t, (1, 2, 0))

    return (loss, dx0[None], *[grads[n] for n in names], *[delta[n] for n in names],
            *[new_m[n] for n in names], *[new_v[n] for n in names])
```

```python
import functools
import math

import numpy as np
import jax
import jax.numpy as jnp
from jax import lax
from jax.experimental import pallas as pl
from jax.experimental.pallas import tpu as pltpu

f32 = jnp.float32
bf16 = jnp.bfloat16
EPS = 1e-6
N_DEV = 8
LANES = 128
SSM_STATE = 128
SSM_CHUNK = 128
SSM_GROUPS = 4
HEAD_DIM = 64
ATT_BLK = 128
ATT_DILATIONS = (1, 4, 16)
REL_BUCKETS = 32
REL_MAX_DIST = 2048
ADAM_LR, ADAM_B1, ADAM_B2, ADAM_EPS, ADAM_WD, ADAM_STEP = 0.001, 0.9, 0.999, 1e-08, 0.01, 10
PACK_ROW_TILE = 256
MESH = pl.DeviceIdType.MESH
VMEM_LIMIT = 48 * 1024 * 1024


def _sds(shape, dtype=f32):
    return jax.ShapeDtypeStruct(tuple(shape), dtype)


def _tile(n, cap, mult):
    best = None
    t = mult
    while t <= min(n, cap):
        if n % t == 0:
            best = t
        t += mult
    return best if best is not None else n


def _params(sem):
    return pltpu.CompilerParams(dimension_semantics=sem, vmem_limit_bytes=VMEM_LIMIT)


def _mm(a, b, *, name, ta=False, tb=False, b_rows=None, bias=None, add=None, out_dtype=f32,
        tm_cap=512, tn_cap=1536, tk_cap=8192):
    if ta:
        K, M = a.shape
    else:
        M, K = a.shape
    off, cnt = b_rows if b_rows is not None else (0, b.shape[0])
    if tb:
        N, K2 = cnt, b.shape[1]
    else:
        K2, N = cnt, b.shape[1]
    assert K == K2, (a.shape, b.shape, ta, tb, b_rows)
    if ta and a.dtype == f32:
        tm_cap = min(tm_cap, 256)
    if not ta:
        tm_cap = 2 * tm_cap
    tm = _tile(M, tm_cap, LANES)
    tn = _tile(math.gcd(off, N) if tb else N, tn_cap, LANES)
    tk = _tile(K if tb else math.gcd(off, K), tk_cap, LANES)
    assert N % tn == 0 and K % tk == 0 and off % (tn if tb else tk) == 0, (name, off, N, K, tn, tk)
    nk = K // tk
    jo, ko = (off // tn, 0) if tb else (0, off // tk)
    has_bias, has_add = bias is not None, add is not None
    dn = (((0 if ta else 1,), (1 if tb else 0,)), ((), ()))

    def body(*refs):
        a_ref, b_ref = refs[0], refs[1]
        pos = 2
        bias_ref = add_ref = None
        if has_bias:
            bias_ref = refs[pos]
            pos += 1
        if has_add:
            add_ref = refs[pos]
            pos += 1
        o_ref = refs[pos]
        k = pl.program_id(2)
        part = lax.dot_general(a_ref[...].astype(bf16), b_ref[...].astype(bf16), dn, preferred_element_type=f32)

        def finish(r):
            if has_bias:
                r = r + bias_ref[...]
            if has_add:
                r = r + add_ref[...]
            o_ref[...] = r.astype(o_ref.dtype)

        if nk == 1:
            finish(part)
        else:
            acc_ref = refs[pos + 1]

            @pl.when(k == 0)
            def _():
                acc_ref[...] = part

            @pl.when((k > 0) & (k < nk - 1))
            def _():
                acc_ref[...] += part

            @pl.when(k == nk - 1)
            def _():
                finish(acc_ref[...] + part)

    in_specs = [
        pl.BlockSpec((tk, tm), lambda i, j, k: (k, i)) if ta else pl.BlockSpec((tm, tk), lambda i, j, k: (i, k)),
        pl.BlockSpec((tn, tk), lambda i, j, k: (j + jo, k)) if tb else pl.BlockSpec((tk, tn), lambda i, j, k: (k + ko, j)),
    ]
    args = [a, b]
    if has_bias:
        in_specs.append(pl.BlockSpec((1, tn), lambda i, j, k: (0, j)))
        args.append(bias)
    if has_add:
        in_specs.append(pl.BlockSpec((tm, tn), lambda i, j, k: (i, j)))
        args.append(add)
    return pl.pallas_call(
        body, name=name, grid=(M // tm, N // tn, nk), in_specs=in_specs,
        out_specs=pl.BlockSpec((tm, tn), lambda i, j, k: (i, j)), out_shape=_sds((M, N), out_dtype),
        scratch_shapes=[pltpu.VMEM((tm, tn), f32)] if nk > 1 else [],
        compiler_params=_params(("parallel", "parallel", "arbitrary")),
    )(*args)


def _rowwise(name, fn, rows, vecs, out_rows, out_accs, *, tr_cap=512, sub=8, col_chunk=None):
    rows = [r if isinstance(r, tuple) else (r, 0, r.shape[1]) for r in rows]
    R = rows[0][0].shape[0]
    tr = _tile(R, tr_cap, 8)
    sub = sub if tr % sub == 0 else tr
    n_r, n_v, n_or, n_oa = len(rows), len(vecs), len(out_rows), len(out_accs)

    def body(*refs):
        row_refs = refs[:n_r]
        vec_refs = refs[n_r:n_r + n_v]
        orow_refs = refs[n_r + n_v:n_r + n_v + n_or]
        oacc_refs = refs[n_r + n_v + n_or:]
        vv = [r[...] for r in vec_refs]

        n_sub = tr // sub
        together = 4 if n_sub % 4 == 0 else 1

        def step(s, accs):
            for t in range(together):
                sl = pl.ds(pl.multiple_of((s * together + t) * sub, sub), sub)
                if col_chunk is None:
                    ro, ao = fn([r[sl, :] for r in row_refs], vv)
                    for o_ref, o in zip(orow_refs, ro):
                        o_ref[sl, :] = o.astype(o_ref.dtype)
                    accs = tuple(x + y for x, y in zip(accs, ao))
                else:
                    for c0 in range(0, rows[0][2], col_chunk):
                        cs_ = pl.ds(c0, col_chunk)
                        ro, _ = fn([r[sl, cs_] for r in row_refs], vv)
                        for o_ref, o in zip(orow_refs, ro):
                            o_ref[sl, cs_] = o.astype(o_ref.dtype)
            return accs

        accs = lax.fori_loop(0, n_sub // together, step, tuple(jnp.zeros((1, w), f32) for w in out_accs))
        if n_oa:
            @pl.when(pl.program_id(0) == 0)
            def _():
                for ref in oacc_refs:
                    ref[...] = jnp.zeros_like(ref)

            for ref, x in zip(oacc_refs, accs):
                ref[...] += x

    in_specs = [pl.BlockSpec((tr, w), functools.partial(lambda i, cb: (i, cb), cb=cb)) for (_, cb, w) in rows]
    in_specs += [pl.BlockSpec((1, v.shape[1]), lambda i: (0, 0)) for v in vecs]
    out_specs = [pl.BlockSpec((tr, w), lambda i: (i, 0)) for (w, _) in out_rows]
    out_specs += [pl.BlockSpec((1, w), lambda i: (0, 0)) for w in out_accs]
    out_shape = [_sds((R, w), dt) for (w, dt) in out_rows] + [_sds((1, w)) for w in out_accs]
    res = pl.pallas_call(
        body, name=name, grid=(R // tr,), in_specs=in_specs, out_specs=out_specs, out_shape=out_shape,
        compiler_params=_params(("arbitrary",)),
    )(*[r[0] for r in rows], *vecs)
    return res[:n_or], res[n_or:]


def _silu(x):
    return x * jax.nn.sigmoid(x)


def _rms(x, g):
    return x * lax.rsqrt(jnp.mean(x * x, -1, keepdims=True) + EPS) * g


def _adaln_f(x, g, sc, sh):
    return _rms(x, g) * (1.0 + sc) + sh


def _gate_f(y, z, g):
    return _rms(y * _silu(z), g)


def _lnsilu_f(u, g, b):
    mu = jnp.mean(u, -1, keepdims=True)
    var = jnp.mean(jnp.square(u - mu), -1, keepdims=True)
    return _silu((u - mu) * lax.rsqrt(var + EPS) * g + b)


def _adaln_fwd(x, g, sc, sh, name):
    (h,), _ = _rowwise(name, lambda rv, vv: ([_adaln_f(rv[0], *vv)], []), [x], [g, sc, sh], [(x.shape[1], bf16)], [],
                       sub=16)
    return h


def _adaln_bwd(x, g, sc, sh, dh, dres, name):
    def fn(rv, vv):
        xv, dhv, drv = rv
        _, vjp = jax.vjp(_adaln_f, xv, *vv)
        dx, dg, dsc, dsh = vjp(dhv)
        return [dx + drv], [dg, dsc, dsh]
    w = x.shape[1]
    (dx,), accs = _rowwise(name, fn, [x, dh, dres], [g, sc, sh], [(w, f32)], [w, w, w])
    return dx, accs


def _resid_adaln_fwd(x, gate, mix, g, sc, sh, name):
    def fn(rv, vv):
        xn = rv[0] + vv[0] * rv[1]
        return [xn, _adaln_f(xn, vv[1], vv[2], vv[3])], []
    w = x.shape[1]
    (xn, h), _ = _rowwise(name, fn, [x, mix], [gate, g, sc, sh], [(w, f32), (w, bf16)], [], sub=16)
    return xn, h


def _adaln_resid_bwd(x, g, sc, sh, dh, dres, mix, gate, name):
    def fn(rv, vv):
        xv, dhv, drv, mv = rv
        _, vjp = jax.vjp(_adaln_f, xv, vv[0], vv[1], vv[2])
        dx, dg, dsc, dsh = vjp(dhv)
        dx = dx + drv
        dm = vv[3] * dx
        return [dx, dm], [dg, dsc, dsh, jnp.sum(dx * mv, 0, keepdims=True), jnp.sum(dm, 0, keepdims=True)]
    w = x.shape[1]
    (dx, dmix), accs = _rowwise(name, fn, [x, dh, dres, mix], [g, sc, sh, gate], [(w, f32), (w, bf16)], [w] * 5, sub=16)
    return dx, dmix, accs


def _add3(a, b, c, name):
    (y,), _ = _rowwise(name, lambda rv, vv: ([rv[0] + rv[1] + rv[2]], []), [a, b, c], [], [(a.shape[1], bf16)], [],
                       sub=16)
    return y


CONV_HALO = 32
CONV_ROWS = 64


def _conv_fwd(x, w, b, *, silu, name, tr=512):
    S, C = x.shape
    K = w.shape[0]
    H = CONV_HALO
    assert K - 1 <= H and S % tr == 0 and tr % H == 0 and C % LANES == 0
    nh = tr // H

    def body(xp_ref, xc_ref, w_ref, b_ref, *rest):
        outs, scr = rest[:-1], rest[-1]
        i = pl.program_id(1)
        scr[pl.ds(0, H), :] = jnp.where(i > 0, xp_ref[...], 0.0)
        scr[pl.ds(H, tr), :] = xc_ref[...]
        taps = [w_ref[pl.ds(k, 1), :] for k in range(K)]
        for c0 in range(0, tr, CONV_ROWS):
            acc = jnp.zeros((CONV_ROWS, LANES), f32) + b_ref[...]
            for k in range(K):
                acc = acc + scr[pl.ds(c0 + H - (K - 1) + k, CONV_ROWS), :] * taps[k]
            outs[0][pl.ds(c0, CONV_ROWS), :] = acc.astype(outs[0].dtype)
            if silu:
                outs[1][pl.ds(c0, CONV_ROWS), :] = _silu(acc)

    n_out = 2 if silu else 1
    return pl.pallas_call(
        body, name=name, grid=(C // LANES, S // tr),
        in_specs=[pl.BlockSpec((H, LANES), lambda j, i: (jnp.maximum(i * nh - 1, 0), j)),
                  pl.BlockSpec((tr, LANES), lambda j, i: (i, j)),
                  pl.BlockSpec((K, LANES), lambda j, i: (0, j)),
                  pl.BlockSpec((1, LANES), lambda j, i: (0, j))],
        out_specs=[pl.BlockSpec((tr, LANES), lambda j, i: (i, j))] * n_out,
        out_shape=[_sds((S, C), bf16), _sds((S, C))] if silu else [_sds((S, C))],
        scratch_shapes=[pltpu.VMEM((tr + H, LANES), f32)],
        compiler_params=_params(("parallel", "arbitrary")),
    )(x, x, w, b)


def _conv_bwd(x, w, dact, pre, *, silu, name, dx_dtype=f32, tr=512):
    S, C = x.shape
    K = w.shape[0]
    H = CONV_HALO
    nh = tr // H
    n_i = S // tr
    kp = -(-K // 8) * 8

    def dsilu(p):
        s = jax.nn.sigmoid(p)
        return s * (1.0 + p * (1.0 - s))

    def body(*refs):
        if silu:
            xp_ref, xc_ref, w_ref, dc_ref, dn_ref, pc_ref, pn_ref, dx_ref, dw_ref, db_ref, xs, ds = refs
        else:
            xp_ref, xc_ref, w_ref, dc_ref, dn_ref, dx_ref, dw_ref, db_ref, xs, ds = refs
        i = pl.program_id(1)
        xs[pl.ds(0, H), :] = jnp.where(i > 0, xp_ref[...], 0.0)
        xs[pl.ds(H, tr), :] = xc_ref[...]
        dcur = dc_ref[...]
        dnext = dn_ref[...]
        if silu:
            dcur = dcur * dsilu(pc_ref[...].astype(f32))
            dnext = dnext * dsilu(pn_ref[...].astype(f32))
        ds[pl.ds(0, tr), :] = dcur
        ds[pl.ds(tr, H), :] = jnp.where(i < n_i - 1, dnext, 0.0)
        taps = [w_ref[pl.ds(k, 1), :] for k in range(K)]
        fold = lambda t: jnp.sum(t.reshape(CONV_ROWS // 8, 8, LANES), axis=0)
        dw_parts = [jnp.zeros((8, LANES), f32) for _ in range(K)]
        db_part = jnp.zeros((8, LANES), f32)
        for c0 in range(0, tr, CONV_ROWS):
            acc = jnp.zeros((CONV_ROWS, LANES), f32)
            d_c = ds[pl.ds(c0, CONV_ROWS), :]
            for k in range(K):
                acc = acc + ds[pl.ds(c0 + K - 1 - k, CONV_ROWS), :] * taps[k]
                dw_parts[k] = dw_parts[k] + fold(d_c * xs[pl.ds(c0 + H - (K - 1) + k, CONV_ROWS), :])
            db_part = db_part + fold(d_c)
            dx_ref[pl.ds(c0, CONV_ROWS), :] = acc.astype(dx_ref.dtype)

        @pl.when(i == 0)
        def _():
            dw_ref[...] = jnp.zeros_like(dw_ref)
            db_ref[...] = jnp.zeros_like(db_ref)

        for k in range(K):
            dw_ref[pl.ds(k, 1), :] += jnp.sum(dw_parts[k], 0, keepdims=True)
        db_ref[...] += jnp.sum(db_part, 0, keepdims=True)

    prev = pl.BlockSpec((H, LANES), lambda j, i: (jnp.maximum(i * nh - 1, 0), j))
    cur = pl.BlockSpec((tr, LANES), lambda j, i: (i, j))
    nxt = pl.BlockSpec((H, LANES), lambda j, i: (jnp.minimum((i + 1) * nh, n_i * nh - 1), j))
    in_specs = [prev, cur, pl.BlockSpec((K, LANES), lambda j, i: (0, j)), cur, nxt]
    args = [x, x, w, dact, dact]
    if silu:
        in_specs += [cur, nxt]
        args += [pre, pre]
    dx, dw, db = pl.pallas_call(
        body, name=name, grid=(C // LANES, n_i), in_specs=in_specs,
        out_specs=[cur, pl.BlockSpec((kp, LANES), lambda j, i: (0, j)), pl.BlockSpec((1, LANES), lambda j, i: (0, j))],
        out_shape=[_sds((S, C), dx_dtype), _sds((kp, C)), _sds((1, C))],
        scratch_shapes=[pltpu.VMEM((tr + H, LANES), f32), pltpu.VMEM((tr + H, LANES), f32)],
        compiler_params=_params(("parallel", "arbitrary")),
    )(*args)
    return dx, dw[:K], db


def _dot(a, b):
    return jnp.dot(a.astype(bf16), b.astype(bf16), preferred_element_type=f32)


def _dot_nt(a, b):
    return lax.dot_general(a.astype(bf16), b.astype(bf16), (((1,), (1,)), ((), ())), preferred_element_type=f32)


def _dot_tn(a, b):
    return lax.dot_general(a.astype(bf16), b.astype(bf16), (((0,), (0,)), ((), ())), preferred_element_type=f32)


def _softplus(x):
    return jnp.maximum(x, 0.0) + jnp.log(1.0 + jnp.exp(-jnp.abs(x)))


def _tri(q):
    i = lax.broadcasted_iota(jnp.int32, (q, q), 0)
    j = lax.broadcasted_iota(jnp.int32, (q, q), 1)
    return i >= j


def _ssd_prep(dtraw, dt_bias, a_log):
    q = dtraw.shape[0]
    dt = _softplus(dtraw + dt_bias)
    A = -jnp.exp(a_log)
    tri = _tri(q)
    cs = jnp.dot(tri.astype(f32), dt * A, preferred_element_type=f32, precision=lax.Precision.HIGHEST)
    return dt, A, cs, cs.T, tri


def _expand(cols, h0, n, width):
    q = cols.shape[0]
    return jnp.concatenate([jnp.broadcast_to(cols[:, h0 + r:h0 + r + 1], (q, width)) for r in range(n)], axis=1)


def _ssd_fwd(xbc, dtraw, dt_bias, a_log, d_skip, di, name):
    S, CD = xbc.shape
    Q, N, G = SSM_CHUNK, SSM_STATE, SSM_GROUPS
    nc = S // Q
    nh = di // HEAD_DIM
    R = nh // G
    gw = R * HEAD_DIM
    col_of_head = jnp.asarray((np.arange(LANES)[:, None] == np.arange(di)[None, :] // HEAD_DIM).astype(np.float32))
    dsk_wide = jnp.repeat(d_skip[0, :nh], HEAD_DIM)[None]

    def body(xbc_ref, dt_ref, bias_ref, alog_ref, dskw_ref, coh_ref, y_ref, hin_ref, state):
        c = pl.program_id(0)

        @pl.when(c == 0)
        def _():
            state[...] = jnp.zeros_like(state)

        hin_ref[...] = state[...]
        dt, A, cs, csT, tri = _ssd_prep(dt_ref[...], bias_ref[...], alog_ref[...])
        elast = jnp.exp(cs[Q - 1:Q, :])
        coh = coh_ref[...]
        dt_w, ecs_w, dend_w = _dot_exact(dt, coh), _dot_exact(jnp.exp(cs), coh), _dot_exact(jnp.exp(cs[Q - 1:Q, :] - cs), coh)
        for g in range(G):
            h0 = g * R
            cols = pl.ds(g * gw, gw)
            lanes = slice(g * gw, (g + 1) * gw)
            Bg = xbc_ref[:, pl.ds(di + g * N, N)]
            Cg = xbc_ref[:, pl.ds(di + G * N + g * N, N)]
            xg = xbc_ref[:, cols]
            Hg = state[cols, :]
            Gm = _dot_nt(Cg, Bg)
            xdt = xg * dt_w[:, lanes]
            yoff = _dot_nt(Cg, Hg) * ecs_w[:, lanes]
            ys = []
            for r in range(R):
                h = h0 + r
                L = jnp.exp(jnp.where(tri, cs[:, h:h + 1] - csT[h:h + 1, :], -jnp.inf))
                ys.append(_dot(Gm * L, xdt[:, r * HEAD_DIM:(r + 1) * HEAD_DIM]))
            y_ref[:, cols] = jnp.concatenate(ys, axis=1) + yoff + xg * dskw_ref[:, cols]
            hnew = _dot_tn(xdt * dend_w[:, lanes], Bg)
            escale = jnp.concatenate([jnp.broadcast_to(elast[:, h0 + r:h0 + r + 1], (HEAD_DIM, N)) for r in range(R)], axis=0)
            state[cols, :] = escale * Hg + hnew

    vec = pl.BlockSpec((1, LANES), lambda c: (0, 0))
    return pl.pallas_call(
        body, name=name, grid=(nc,),
        in_specs=[pl.BlockSpec((Q, CD), lambda c: (c, 0)), pl.BlockSpec((Q, LANES), lambda c: (c, 0)), vec, vec,
                  pl.BlockSpec((1, di), lambda c: (0, 0)), pl.BlockSpec((LANES, di), lambda c: (0, 0))],
        out_specs=[pl.BlockSpec((Q, di), lambda c: (c, 0)), pl.BlockSpec((None, di, N), lambda c: (c, 0, 0))],
        out_shape=[_sds((S, di)), _sds((nc, di, N))],
        scratch_shapes=[pltpu.VMEM((di, N), f32)],
        compiler_params=_params(("arbitrary",)),
    )(xbc, dtraw, dt_bias, a_log, dsk_wide, col_of_head)


def _dot_exact(a, b):
    bb = b.astype(bf16)
    hi = a.astype(bf16)
    rest = a - hi.astype(f32)
    mid = rest.astype(bf16)
    low = (rest - mid.astype(f32)).astype(bf16)
    one_pass = lambda t: jnp.dot(t, bb, preferred_element_type=f32)
    return one_pass(hi) + one_pass(mid) + one_pass(low)


def _ssd_bwd(xbc, dtraw, dt_bias, a_log, d_skip, hin, y, dy, di, name):
    S, CD = xbc.shape
    Q, N, G = SSM_CHUNK, SSM_STATE, SSM_GROUPS
    nc = S // Q
    nh = di // HEAD_DIM
    R = nh // G
    gw = R * HEAD_DIM
    P = HEAD_DIM
    head_of_col = jnp.asarray((np.arange(di)[:, None] // P == np.arange(LANES)[None, :]).astype(np.float32))
    dsk_wide = jnp.repeat(d_skip[0, :nh], P)[None]

    def body(xbc_ref, dt_ref, bias_ref, alog_ref, dskw_ref, hoc_ref, hin_ref, y_ref, dy_ref,
             dxbc_ref, ddt_ref, dA_ref, ddsk_ref, dtb_ref, dstate, dxdt_all, tend_all, yoff_all, colterm_all):
        c = pl.program_id(0)

        @pl.when(c == 0)
        def _():
            dstate[...] = jnp.zeros_like(dstate)
            dA_ref[...] = jnp.zeros_like(dA_ref)
            ddsk_ref[...] = jnp.zeros_like(ddsk_ref)
            dtb_ref[...] = jnp.zeros_like(dtb_ref)

        dtraw_v = dt_ref[...]
        dt, A, cs, csT, tri = _ssd_prep(dtraw_v, bias_ref[...], alog_ref[...])
        tri_t = jnp.logical_not(tri) | (lax.broadcasted_iota(jnp.int32, (Q, Q), 0) == lax.broadcasted_iota(jnp.int32, (Q, Q), 1))
        ecs = jnp.exp(cs)
        dend = jnp.exp(cs[Q - 1:Q, :] - cs)
        elast = jnp.exp(cs[Q - 1:Q, :])
        hoc = hoc_ref[...]
        state_dot = jnp.sum(_dot_exact(dstate[...] * hin_ref[...], jnp.ones((N, LANES), f32)) * hoc, 0, keepdims=True) * elast
        for g in range(G):
            h0 = g * R
            Bg = xbc_ref[:, pl.ds(di + g * N, N)]
            Cg = xbc_ref[:, pl.ds(di + G * N + g * N, N)]
            xg = xbc_ref[:, pl.ds(g * gw, gw)]
            dyg = dy_ref[:, pl.ds(g * gw, gw)]
            Hg = hin_ref[pl.ds(g * gw, gw), :]
            dHg = dstate[pl.ds(g * gw, gw), :]
            dt_e = _expand(dt, h0, R, P)
            ecs_e = _expand(ecs, h0, R, P)
            dend_e = _expand(dend, h0, R, P)
            cols = pl.ds(g * gw, gw)
            Gm = _dot_nt(Cg, Bg)
            Gm_t = _dot_nt(Bg, Cg)
            xdt = xg * dt_e
            dye = dyg * ecs_e
            bdh = _dot_nt(Bg, dHg)
            dC = _dot(dye, Hg)
            dB = _dot(xdt * dend_e, dHg)
            dHin = _dot_tn(dye, Cg)
            dxdt_state = dend_e * bdh
            end_term = xdt * dxdt_state
            tend_all[:, cols] = end_term
            yoff_all[:, cols] = _dot_nt(Cg, Hg) * ecs_e
            dG = jnp.zeros((Q, Q), f32)
            dxd = []
            for r in range(R):
                h = h0 + r
                sl = slice(r * P, (r + 1) * P)
                seg = cs[:, h:h + 1] - csT[h:h + 1, :]
                L = jnp.exp(jnp.where(tri, seg, -jnp.inf))
                L_t = jnp.exp(jnp.where(tri_t, -seg, -jnp.inf))
                dyh = dyg[:, sl]
                dG = dG + _dot_nt(dyh, xdt[:, sl]) * L
                dxd.append(_dot(Gm_t * L_t, dyh))
            dxdt_diag = jnp.concatenate(dxd, axis=1)
            dxdt = dxdt_diag + dxdt_state
            dxdt_all[:, cols] = dxdt
            colterm_all[:, cols] = xdt.astype(bf16).astype(f32) * dxdt_diag + end_term
            dxbc_ref[:, cols] = dxdt * dt_e + dyg * dskw_ref[:, cols]
            dxbc_ref[:, pl.ds(di + g * N, N)] = dB + _dot_tn(dG, Cg)
            dxbc_ref[:, pl.ds(di + G * N + g * N, N)] = dC + _dot(dG, Bg)
            escale = jnp.concatenate([jnp.broadcast_to(elast[:, h0 + r:h0 + r + 1], (P, N)) for r in range(R)], axis=0)
            dstate[pl.ds(g * gw, gw), :] = escale * dHg + dHin
        xs = xbc_ref[:, pl.ds(0, di)]
        dyv = dy_ref[...]
        yoff = yoff_all[...]
        y_diag = y_ref[...] - dskw_ref[...] * xs - yoff
        rs_y = _dot_exact(dyv.astype(bf16).astype(f32) * y_diag + dyv * yoff, hoc)
        rs_c = _dot_exact(colterm_all[...], hoc)
        rs_x = _dot_exact(dxdt_all[...] * xs, hoc)
        end_dot = _dot_exact(jnp.broadcast_to(jnp.sum(tend_all[...], 0, keepdims=True), (8, di)), hoc)[0:1]
        last = lax.broadcasted_iota(jnp.int32, (Q, 1), 0) == Q - 1
        dcs = rs_y - rs_c + jnp.where(last, end_dot + state_dot, 0.0)
        da = lax.dot_general(tri.astype(f32), dcs, (((0,), (0,)), ((), ())), preferred_element_type=f32,
                             precision=lax.Precision.HIGHEST)
        ddt = da * A + rs_x
        ddtraw = ddt * jax.nn.sigmoid(dtraw_v + bias_ref[...])
        ddt_ref[...] = ddtraw.astype(ddt_ref.dtype)
        dA_ref[...] += jnp.sum(da * dt, 0, keepdims=True) * A
        ddsk_ref[...] += jnp.sum(_dot_exact(dyv * xs, hoc), 0, keepdims=True)
        dtb_ref[...] += jnp.sum(ddtraw, 0, keepdims=True)

    vec = pl.BlockSpec((1, LANES), lambda c: (0, 0))
    rev = lambda c: (nc - 1 - c, 0)
    return pl.pallas_call(
        body, name=name, grid=(nc,),
        in_specs=[pl.BlockSpec((Q, CD), rev), pl.BlockSpec((Q, LANES), rev), vec, vec,
                  pl.BlockSpec((1, di), lambda c: (0, 0)), pl.BlockSpec((di, LANES), lambda c: (0, 0)),
                  pl.BlockSpec((None, di, N), lambda c: (nc - 1 - c, 0, 0)), pl.BlockSpec((Q, di), rev),
                  pl.BlockSpec((Q, di), rev)],
        out_specs=[pl.BlockSpec((Q, CD), rev), pl.BlockSpec((Q, LANES), rev), vec, vec, vec],
        out_shape=[_sds((S, CD)), _sds((S, LANES), bf16), _sds((1, LANES)), _sds((1, LANES)), _sds((1, LANES))],
        scratch_shapes=[pltpu.VMEM((di, N), f32)] + [pltpu.VMEM((Q, di), f32)] * 4,
        compiler_params=_params(("arbitrary",)),
    )(xbc, dtraw, dt_bias, a_log, dsk_wide, head_of_col, hin, y, dy)


def _t5_bucket_np(dist):
    max_exact = REL_BUCKETS // 2
    n = np.maximum(dist, 1).astype(np.float32)
    large = np.float32(max_exact) + np.log(n / np.float32(max_exact)) / np.float32(math.log(REL_MAX_DIST / max_exact)) * np.float32(REL_BUCKETS - max_exact)
    large = np.minimum(large.astype(np.int32), REL_BUCKETS - 1)
    return np.where(dist < max_exact, dist, large)


def _bucket_onehot():
    i = np.arange(ATT_BLK)[None, :]
    j = np.arange(2 * ATT_BLK)[:, None]
    delta = np.maximum(ATT_BLK + i - j, 0)
    out = np.zeros((len(ATT_DILATIONS), REL_BUCKETS, ATT_BLK * 2 * ATT_BLK), np.float32)
    for gi, d in enumerate(ATT_DILATIONS):
        b = _t5_bucket_np(delta * d).reshape(-1)
        out[gi, b, np.arange(b.size)] = 1.0
    return out


def _exact_mm(a, b, *, name, tb=False):
    M, K = a.shape
    N = b.shape[0] if tb else b.shape[1]
    tn = _tile(N, 4096, LANES)
    dn = (((1,), (1 if tb else 0,)), ((), ()))

    def body(a_ref, b_ref, o_ref):
        o_ref[...] = lax.dot_general(a_ref[...], b_ref[...], dn, preferred_element_type=f32,
                                     precision=lax.Precision.HIGHEST)

    return pl.pallas_call(
        body, name=name, grid=(N // tn,),
        in_specs=[pl.BlockSpec((M, K), lambda j: (0, 0)),
                  pl.BlockSpec((tn, K), lambda j: (j, 0)) if tb else pl.BlockSpec((K, tn), lambda j: (0, j))],
        out_specs=pl.BlockSpec((M, tn), lambda j: (0, j)), out_shape=_sds((M, N)),
        compiler_params=_params(("parallel",)),
    )(a, b)


def _band_penalty():
    i = np.arange(ATT_BLK)[None, :]
    j = np.arange(2 * ATT_BLK)[:, None]
    delta = ATT_BLK + i - j
    return np.where((delta >= 0) & (delta <= ATT_BLK), 0.0, -np.inf).astype(np.float32)


def _first_block_keep(n):
    key = lax.broadcasted_iota(jnp.int32, (2 * ATT_BLK, ATT_BLK), 0)
    return (key >= ATT_BLK) | (n > 0)


ATT_SCALE = HEAD_DIM ** -0.5


def _rows(ref, r, d):
    return ref[...] if d == 1 else ref[pl.ds(r, ATT_BLK, stride=d), :]


def _set_rows(ref, r, d, val):
    if d == 1:
        ref[...] = val
    else:
        ref[pl.ds(r, ATT_BLK, stride=d), :] = val


def _attn_width(d, D):
    return D if d == 1 else LANES


def _over_residues(d, one, unroll=1):
    if d == 1:
        one(0)
    else:
        lax.fori_loop(0, d, lambda r, c: (one(r), c)[1], 0, unroll=unroll)


def _attn_fwd(q, k, v, bias_t, d, name):
    S, D = q.shape
    nb = S // (d * ATT_BLK)
    H = D // HEAD_DIM
    W = _attn_width(d, D)
    HB = W // HEAD_DIM

    def body(q_ref, kp_ref, kc_ref, vp_ref, vc_ref, b_ref, o_ref, lse_ref):
        keep = _first_block_keep(pl.program_id(1))
        first = lax.broadcasted_iota(jnp.int32, (1, LANES), 1) < HEAD_DIM

        def one(r):
            qs = (_rows(q_ref, r, d) * ATT_SCALE).astype(bf16)
            kcat = jnp.concatenate([_rows(kp_ref, r, d), _rows(kc_ref, r, d)], axis=0).astype(bf16)
            vcat = jnp.concatenate([_rows(vp_ref, r, d), _rows(vc_ref, r, d)], axis=0).astype(bf16)
            outs = []
            for pair in range(W // LANES):
                ps = slice(pair * LANES, (pair + 1) * LANES)
                q2, k2, v2 = qs[:, ps], kcat[:, ps], vcat[:, ps]
                o2 = jnp.zeros((ATT_BLK, LANES), f32)
                for e in range(2):
                    h = 2 * pair + e
                    mine = first if e == 0 else jnp.logical_not(first)
                    zero = jnp.zeros((), bf16)
                    st = jnp.where(keep, _dot_nt(k2, jnp.where(mine, q2, zero)) + b_ref[h], -jnp.inf)
                    m = jnp.max(st, 0, keepdims=True)
                    pt = jnp.exp(st - m)
                    l = jnp.sum(pt, 0, keepdims=True)
                    o2 = o2 + _dot_tn(pt * (1.0 / l), jnp.where(mine, v2, zero))
                    lse_ref[r, h] = m + jnp.log(l)
                outs.append(o2)
            _set_rows(o_ref, r, d, jnp.concatenate(outs, axis=1))

        _over_residues(d, one, unroll=4)

    cur = pl.BlockSpec((ATT_BLK * d, W), lambda j, n: (n, j))
    prev = pl.BlockSpec((ATT_BLK * d, W), lambda j, n: (jnp.maximum(n - 1, 0), j))
    return pl.pallas_call(
        body, name=name, grid=(D // W, nb),
        in_specs=[cur, prev, cur, prev, cur, pl.BlockSpec((HB, 2 * ATT_BLK, ATT_BLK), lambda j, n: (j, 0, 0))],
        out_specs=[cur, pl.BlockSpec((None, d, HB, 1, LANES), lambda j, n: (n, 0, j, 0, 0))],
        out_shape=[_sds((S, D)), _sds((nb, d, H, 1, LANES))],
        compiler_params=_params(("parallel", "arbitrary")),
    )(q, k, k, v, v, bias_t)


def _from_blocks(rows, lanes=None):
    nb, d, H = rows.shape[:3]
    a = jnp.transpose(rows[:, :, :, 0, :], (0, 3, 1, 2)).reshape(nb * ATT_BLK * d, H)
    return a if lanes is None else jnp.pad(a, ((0, 0), (0, lanes - H)))


def _by_block(a, d):
    S, H = a.shape
    t = jnp.transpose(a.reshape(S // (d * ATT_BLK), ATT_BLK, d, H), (0, 2, 3, 1))
    return t[:, :, :, None, :]


def _head_sums(a, b, name):
    S, D = a.shape
    tr = _tile(S, 512, 8)
    hoc = jnp.asarray((np.arange(D)[:, None] // HEAD_DIM == np.arange(LANES)[None, :]).astype(np.float32))

    def body(a_ref, b_ref, h_ref, o_ref):
        o_ref[...] = _dot_exact(a_ref[...] * b_ref[...], h_ref[...])

    return pl.pallas_call(
        body, name=name, grid=(S // tr,),
        in_specs=[pl.BlockSpec((tr, D), lambda i: (i, 0)), pl.BlockSpec((tr, D), lambda i: (i, 0)),
                  pl.BlockSpec((D, LANES), lambda i: (0, 0))],
        out_specs=pl.BlockSpec((tr, LANES), lambda i: (i, 0)), out_shape=_sds((S, LANES)),
        compiler_params=_params(("parallel",)),
    )(a, b, hoc)


def _attn_bwd(q, k, v, bias_t, datt, lse_rows, dsum_rows, d, name):
    S, D = q.shape
    nb = S // (d * ATT_BLK)
    H = D // HEAD_DIM
    W = _attn_width(d, D)
    HB = W // HEAD_DIM

    def body(q_ref, kp_ref, kc_ref, vp_ref, vc_ref, b_ref, do_ref, lse_ref, dsum_ref,
             dq_ref, dk_ref, dv_ref, db_ref, carry_k, carry_v):
        j = pl.program_id(0)
        n = pl.program_id(1)

        @pl.when(n == 0)
        def _():
            carry_k[...] = jnp.zeros_like(carry_k)
            carry_v[...] = jnp.zeros_like(carry_v)
            db_ref[...] = jnp.zeros_like(db_ref)

        @pl.when(n < nb)
        def _():
            key = lax.broadcasted_iota(jnp.int32, (2 * ATT_BLK, ATT_BLK), 0)
            keep = (key >= ATT_BLK) | (n > 0)
            first = lax.broadcasted_iota(jnp.int32, (1, LANES), 1) < HEAD_DIM

            def one(r):
                qs = (_rows(q_ref, r, d) * ATT_SCALE).astype(bf16)
                kcat = jnp.concatenate([_rows(kp_ref, r, d), _rows(kc_ref, r, d)], axis=0).astype(bf16)
                vcat = jnp.concatenate([_rows(vp_ref, r, d), _rows(vc_ref, r, d)], axis=0).astype(bf16)
                dob = _rows(do_ref, r, d).astype(bf16)
                dqs, dks, dvs = [], [], []
                for pair in range(W // LANES):
                    ps = slice(pair * LANES, (pair + 1) * LANES)
                    q2, k2, v2, do2 = qs[:, ps], kcat[:, ps], vcat[:, ps], dob[:, ps]
                    dq2 = jnp.zeros((ATT_BLK, LANES), f32)
                    dk2 = jnp.zeros((2 * ATT_BLK, LANES), f32)
                    dv2 = jnp.zeros((2 * ATT_BLK, LANES), f32)
                    for e in range(2):
                        h = 2 * pair + e
                        mine = first if e == 0 else jnp.logical_not(first)
                        zero = jnp.zeros((), bf16)
                        qm, dom, km = jnp.where(mine, q2, zero), jnp.where(mine, do2, zero), jnp.where(mine, k2, zero)
                        st = jnp.where(keep, _dot_nt(k2, qm) + b_ref[h], -jnp.inf)
                        pt = jnp.exp(st - lse_ref[r, j * HB + h])
                        dst = pt * (_dot_nt(v2, dom) - dsum_ref[r, j * HB + h])
                        db_ref[h] += dst
                        dv2 = dv2 + _dot(pt, dom)
                        dk2 = dk2 + _dot(dst, qm)
                        dq2 = dq2 + _dot_tn(dst, km)
                    dqs.append(dq2 * ATT_SCALE)
                    dks.append(dk2)
                    dvs.append(dv2)
                _set_rows(dq_ref, r, d, jnp.concatenate(dqs, axis=1))
                dk = jnp.concatenate(dks, axis=1)
                dv = jnp.concatenate(dvs, axis=1)
                _set_rows(dk_ref, r, d, carry_k[r] + dk[:ATT_BLK])
                _set_rows(dv_ref, r, d, carry_v[r] + dv[:ATT_BLK])
                carry_k[r] = dk[ATT_BLK:]
                carry_v[r] = dv[ATT_BLK:]

            _over_residues(d, one, unroll=2)

        @pl.when(n == nb)
        def _():
            def last(r):
                _set_rows(dk_ref, r, d, carry_k[r])
                _set_rows(dv_ref, r, d, carry_v[r])

            _over_residues(d, last)

    nq = lambda n: jnp.minimum(n, nb - 1)
    cur = pl.BlockSpec((ATT_BLK * d, W), lambda j, n: (nq(n), j))
    prev = pl.BlockSpec((ATT_BLK * d, W), lambda j, n: (jnp.maximum(nq(n) - 1, 0), j))
    done = pl.BlockSpec((ATT_BLK * d, W), lambda j, n: (jnp.maximum(n - 1, 0), j))
    bspec = pl.BlockSpec((HB, 2 * ATT_BLK, ATT_BLK), lambda j, n: (j, 0, 0))
    rows = pl.BlockSpec((None, d, H, 1, LANES), lambda j, n: (nq(n), 0, 0, 0, 0))
    return pl.pallas_call(
        body, name=name, grid=(D // W, nb + 1),
        in_specs=[cur, prev, cur, prev, cur, bspec, cur, rows, rows],
        out_specs=[cur, done, done, bspec],
        out_shape=[_sds((S, D)), _sds((S, D)), _sds((S, D)), _sds((H, 2 * ATT_BLK, ATT_BLK))],
        scratch_shapes=[pltpu.VMEM((d, ATT_BLK, W), f32), pltpu.VMEM((d, ATT_BLK, W), f32)],
        compiler_params=_params(("arbitrary", "arbitrary")),
    )(q, k, k, v, v, bias_t, datt, lse_rows, dsum_rows)


def _attn_combine(os_, lses, name):
    S, D = os_[0].shape
    tr = _tile(S, 256, 16)
    head_cols = jnp.asarray((np.arange(LANES)[:, None] == np.arange(D)[None, :] // HEAD_DIM).astype(np.float32))

    def body(o0, o1, o2, l0, l1, l2, hc_ref, att_ref, attb_ref, lse_ref):
        a, b, c = l0[...], l1[...], l2[...]
        m = jnp.maximum(jnp.maximum(a, b), c)
        e0, e1, e2 = jnp.exp(a - m), jnp.exp(b - m), jnp.exp(c - m)
        tot = e0 + e1 + e2
        wide = lambda w: _dot_exact(w / tot, hc_ref[...])
        att = wide(e0) * o0[...] + wide(e1) * o1[...] + wide(e2) * o2[...]
        att_ref[...] = att
        attb_ref[...] = att.astype(bf16)
        lse_ref[...] = m + jnp.log(tot)

    wide_spec = pl.BlockSpec((tr, D), lambda i: (i, 0))
    lane_spec = pl.BlockSpec((tr, LANES), lambda i: (i, 0))
    return pl.pallas_call(
        body, name=name, grid=(S // tr,),
        in_specs=[wide_spec] * 3 + [lane_spec] * 3 + [pl.BlockSpec((LANES, D), lambda i: (0, 0))],
        out_specs=[wide_spec, wide_spec, lane_spec], out_shape=[_sds((S, D)), _sds((S, D), bf16), _sds((S, LANES))],
        compiler_params=_params(("parallel",)),
    )(*os_, *lses, head_cols)


ANY = pl.BlockSpec(memory_space=pl.ANY)


def _all_gather(vs, name):
    n = len(vs)

    def body(*refs):
        x_refs, out_refs = refs[:n], refs[n:2 * n]
        send_sems, recv_sems, local_sems = refs[2 * n:]
        x, y, c = lax.axis_index("x"), lax.axis_index("y"), lax.axis_index("c")
        me, sibling = (x, y, c), (x, y, 1 - c)
        chips = [(1 - x, y), (x, 1 - y), (1 - x, 1 - y)]

        def slot(i, px, py, pc):
            return out_refs[i].at[4 * px + 2 * py + pc]

        def copy(i, k, block, to, src=None):
            return pltpu.make_async_remote_copy(
                src_ref=slot(i, *block) if src is None else src, dst_ref=slot(i, *block),
                send_sem=send_sems.at[i, k], recv_sem=recv_sems.at[i, k], device_id=to, device_id_type=MESH)

        mine = [pltpu.make_async_copy(x_refs[i], slot(i, *me), local_sems.at[i]) for i in range(n)]
        for cp in mine:
            cp.start()
        first = []
        for i in range(n):
            first.append(copy(i, 0, me, sibling, src=x_refs[i]))
            first += [copy(i, 1 + j, me, (*chip, c), src=x_refs[i]) for j, chip in enumerate(chips)]
        for cp in first:
            cp.start()
        passed = []
        for i in range(n):
            for j, chip in enumerate(chips):
                copy(i, 1 + j, (*chip, c), me).wait_recv()
                cp = copy(i, 4 + j, (*chip, c), sibling)
                cp.start()
                passed.append(cp)
        for i in range(n):
            copy(i, 0, sibling, me).wait_recv()
            for j, chip in enumerate(chips):
                copy(i, 4 + j, (*chip, 1 - c), me).wait_recv()
        for cp in first + passed:
            cp.wait_send()
        for cp in mine:
            cp.wait()

    return pl.pallas_call(
        body, name=name, out_shape=[_sds((N_DEV,) + v.shape, v.dtype) for v in vs], in_specs=[ANY] * n,
        out_specs=[ANY] * n,
        scratch_shapes=[pltpu.SemaphoreType.DMA((n, 7)), pltpu.SemaphoreType.DMA((n, 7)), pltpu.SemaphoreType.DMA((n,))],
    )(*vs)


def _sum_slots(t, name):
    n, R, C = t.shape
    tr = _tile(R, PACK_ROW_TILE, 16)

    def body(t_ref, o_ref):
        acc = t_ref[0].astype(f32)
        for k in range(1, n):
            acc = acc + t_ref[k].astype(f32)
        o_ref[...] = acc

    return pl.pallas_call(
        body, name=name, grid=(R // tr,),
        in_specs=[pl.BlockSpec((n, tr, C), lambda i: (0, i, 0))],
        out_specs=pl.BlockSpec((tr, C), lambda i: (i, 0)), out_shape=_sds((R, C)),
        compiler_params=_params(("parallel",)),
    )(t)


HBM_SPEC = pl.BlockSpec(memory_space=pltpu.HBM)
SEM_SPEC = pl.BlockSpec(memory_space=pltpu.SEMAPHORE)
EFFECT = pltpu.SideEffectType.DATAFLOW_SIDE_EFFECTING


def _mesh_pos(p):
    return (p // 4, (p // 2) % 2, p % 2)


def _exchange_copy(src_refs, land_refs, send_sems, recv_sems, whole, dests, i, k):
    me = 4 * lax.axis_index("x") + 2 * lax.axis_index("y") + lax.axis_index("c")
    to = (me + k) % N_DEV
    frm = (me + N_DEV - k) % N_DEV
    lo, hi = dests
    src = src_refs[i] if whole else src_refs[i].at[jnp.minimum(jnp.maximum(to - lo, 0), hi - lo - 1)]
    s = i * (N_DEV - 1) + k - 1
    send = pltpu.make_async_remote_copy(src_ref=src, dst_ref=land_refs[i].at[me], send_sem=send_sems.at[s],
                                        recv_sem=recv_sems.at[s], device_id=_mesh_pos(to), device_id_type=MESH)
    recv = pltpu.make_async_remote_copy(src_ref=src, dst_ref=land_refs[i].at[frm], send_sem=send_sems.at[s],
                                        recv_sem=recv_sems.at[s], device_id=_mesh_pos(to), device_id_type=MESH)
    return send, recv, (to >= lo) & (to < hi), (me >= lo) & (me < hi)


def _exchange_start(srcs, whole, name, after=None, dests=(0, N_DEV)):
    n = len(srcs)
    lands = [lax.empty((N_DEV,) + s.shape[-2:], s.dtype) for s in srcs]
    after = list(after or [])
    n_in = 2 * n + len(after)
    everyone = dests == (0, N_DEV)

    def body(*refs):
        src_refs, land_refs = refs[:n], refs[n:2 * n]
        send_sems, recv_sems, token = refs[n_in], refs[n_in + 1], refs[-1]
        for i in range(n):
            for k in range(1, N_DEV):
                send, _, sends, _ = _exchange_copy(src_refs, land_refs, send_sems, recv_sems, whole, dests, i, k)
                if everyone:
                    send.start()
                else:
                    pl.when(sends)(send.start)
        token[...] = jnp.zeros_like(token)

    sems = pltpu.SemaphoreType.DMA((n * (N_DEV - 1),))
    outs = pl.pallas_call(
        body, name=name,
        out_shape=(sems, sems, *[pltpu.HBM(a.shape, a.dtype) for a in srcs + lands], _sds((8, LANES))),
        in_specs=[HBM_SPEC] * (2 * n) + [pl.BlockSpec(memory_space=pl.ANY)] * len(after),
        out_specs=(SEM_SPEC, SEM_SPEC, *[HBM_SPEC] * (2 * n), pl.BlockSpec(memory_space=pltpu.VMEM)),
        input_output_aliases={i: 2 + i for i in range(2 * n)},
        compiler_params=pltpu.CompilerParams(has_side_effects=EFFECT),
    )(*[pltpu.with_memory_space_constraint(a, pltpu.HBM) for a in srcs + lands], *after)
    return (outs[0], outs[1], list(outs[2:2 + n]), list(outs[2 + n:2 + 2 * n]), whole, dests), outs[-1]


def _exchange_wait(handle, after, name):
    send_sems, recv_sems, srcs, lands, whole, dests = handle
    n = len(srcs)
    everyone = dests == (0, N_DEV)

    def body(*refs):
        src_refs, land_refs = refs[:n], refs[n:2 * n]
        send_sems, recv_sems = refs[2 * n], refs[2 * n + 1]
        for i in range(n):
            for k in range(1, N_DEV):
                send, recv, sends, receives = _exchange_copy(src_refs, land_refs, send_sems, recv_sems, whole, dests, i, k)
                if everyone:
                    send.wait_send()
                    recv.wait_recv()
                else:
                    pl.when(sends)(send.wait_send)
                    pl.when(receives)(recv.wait_recv)

    outs = pl.pallas_call(
        body, name=name, out_shape=tuple(pltpu.HBM(a.shape, a.dtype) for a in srcs + lands),
        in_specs=[HBM_SPEC] * (2 * n) + [SEM_SPEC, SEM_SPEC, pl.BlockSpec(memory_space=pl.ANY)],
        out_specs=[HBM_SPEC] * (2 * n), input_output_aliases={i: i for i in range(2 * n)},
        compiler_params=pltpu.CompilerParams(has_side_effects=EFFECT),
    )(*srcs, *lands, send_sems, recv_sems, after)
    return list(outs[n:])


def _tie(v, token):
    return v + token[0:1, 0:1].astype(v.dtype).reshape((1,) * v.ndim)


def _with_own(land, own, me):
    return lax.dynamic_update_slice_in_dim(land, own[None].astype(land.dtype), me, 0)


class _Overlap:
    def __init__(self, shards, me, after):
        self.me = me
        self.names = list(shards)
        self.handle, self.token = _exchange_start([shards[nm] for nm in self.names], True, "weights_start", after)
        self.sent = {}

    def weights(self, after):
        lands = _exchange_wait(self.handle, after, "weights_wait")
        own = self.handle[2]
        return {nm: _full_from_blocks(nm, _with_own(land, o, self.me)) for nm, land, o in zip(self.names, lands, own)}

    def send(self, tag, grads):
        names = list(grads)
        handle, token = _exchange_start([_blocks_from_full(nm, grads[nm]) for nm in names], False, f"grads_start_{tag}")
        self.sent[tag] = (names, handle)
        return token

    def send_rows(self, tag, rows, dests):
        lo, hi = dests
        blocks = rows.reshape(hi - lo, rows.shape[0] // (hi - lo), rows.shape[1])
        handle, token = _exchange_start([blocks], False, f"grads_start_{tag}", None, dests)
        self.sent[tag] = ([tag], handle)
        return token

    def received(self, tag, after):
        names, handle = self.sent[tag]
        lands = _exchange_wait(handle, after, f"grads_wait_{tag}")
        lo = handle[5][0]
        own = [lax.dynamic_index_in_dim(b, self.me - lo, 0, keepdims=False) for b in handle[2]]
        return {nm: _with_own(land, o, self.me) for nm, land, o in zip(names, lands, own)}


ADAM_ROWS = 32


def _adamw(w, g, m, v, name):
    deep = w.ndim == 3
    R, C = w.shape[0], w.shape[-1]
    cb = LANES if C % LANES == 0 else C
    n_parts = g.shape[0] if g.ndim == 3 else 0

    def body(w_ref, g_ref, m_ref, v_ref, d_ref, m2_ref, v2_ref, *g_out):
        at = (lambda ref, sl: ref.at[sl, 0, :]) if deep else (lambda ref, sl: ref.at[sl, :])

        def update(sl):
            if n_parts:
                gv = g_ref[0, sl, :].astype(f32)
                for k in range(1, n_parts):
                    gv = gv + g_ref[k, sl, :].astype(f32)
                at(g_out[0], sl)[...] = gv
            else:
                gv = g_ref[sl, :]
            m2 = ADAM_B1 * at(m_ref, sl)[...] + (1.0 - ADAM_B1) * gv
            v2 = ADAM_B2 * at(v_ref, sl)[...] + (1.0 - ADAM_B2) * jnp.square(gv)
            m_hat = m2 / (1.0 - ADAM_B1 ** ADAM_STEP)
            v_hat = v2 / (1.0 - ADAM_B2 ** ADAM_STEP)
            at(d_ref, sl)[...] = -ADAM_LR * (m_hat / (jnp.sqrt(v_hat) + ADAM_EPS) + ADAM_WD * at(w_ref, sl)[...])
            at(m2_ref, sl)[...] = m2
            at(v2_ref, sl)[...] = v2

        main = R // ADAM_ROWS
        if main:
            lax.fori_loop(0, main, lambda i, c: (update(pl.ds(pl.multiple_of(i * ADAM_ROWS, ADAM_ROWS), ADAM_ROWS)), c)[1], 0)
        if R % ADAM_ROWS:
            update(pl.ds(main * ADAM_ROWS, R % ADAM_ROWS))

    spec = pl.BlockSpec((R, 1, cb), lambda j: (0, 0, j)) if deep else pl.BlockSpec((R, cb), lambda j: (0, j))
    g_spec = pl.BlockSpec((n_parts, R, cb), lambda j: (0, 0, j)) if n_parts else pl.BlockSpec((R, cb), lambda j: (0, j))
    n_out = 4 if n_parts else 3
    return pl.pallas_call(
        body, name=name, grid=(C // cb,), in_specs=[spec, g_spec, spec, spec], out_specs=[spec] * n_out,
        out_shape=[_sds(w.shape)] * n_out, compiler_params=_params(("parallel",)),
    )(w, g, m, v)


BIG_PARAMS = ("hy_w_in", "hy_w_out", "cv_w_pw1", "cv_w_pw2", "ffn_w_gate", "ffn_w_up", "ffn_w_down")


def _shards_2d(w):
    t = lambda a: jnp.transpose(a)
    return dict(in_t=t(w["hy_w_in"][0]), out=w["hy_w_out"][0], pw1=w["cv_w_pw1"][0], pw2=w["cv_w_pw2"][0],
                gate_t0=t(w["ffn_w_gate"][0]), gate_t1=t(w["ffn_w_gate"][1]), up_t0=t(w["ffn_w_up"][0]),
                up_t1=t(w["ffn_w_up"][1]), down0=w["ffn_w_down"][0], down1=w["ffn_w_down"][1])


def _unshard_2d(s):
    t = lambda a: jnp.transpose(a)
    out = dict(hy_w_out=s["out"][None], cv_w_pw1=s["pw1"][None], cv_w_pw2=s["pw2"][None],
               ffn_w_gate=jnp.stack([t(s["gate_t0"]), t(s["gate_t1"])]),
               ffn_w_up=jnp.stack([t(s["up_t0"]), t(s["up_t1"])]), ffn_w_down=jnp.stack([s["down0"], s["down1"]]))
    if "in_t" in s:
        out["hy_w_in"] = t(s["in_t"])[None]
    return out


def _full_from_blocks(nm, g):
    if nm == "pw1":
        return jnp.transpose(g, (1, 0, 2)).reshape(g.shape[1], N_DEV * g.shape[2])
    return g.reshape(N_DEV * g.shape[1], g.shape[2])


def _blocks_from_full(nm, g):
    if nm == "pw1":
        return jnp.transpose(g.reshape(g.shape[0], N_DEV, g.shape[1] // N_DEV), (1, 0, 2))
    return g.reshape(N_DEV, g.shape[0] // N_DEV, g.shape[1])


class _VecPack:
    def __init__(self, shapes):
        self.shapes = [tuple(s) for s in shapes]
        self.sizes = [int(np.prod(s)) for s in self.shapes]
        total = sum(self.sizes)
        self.rows = -(-(-(-total // LANES)) // 8) * 8
        self.total = total

    def pack(self, arrays):
        flat = jnp.concatenate([a.astype(f32).reshape(-1) for a in arrays])
        flat = jnp.pad(flat, (0, self.rows * LANES - self.total))
        return flat.reshape(self.rows, LANES)

    def unpack(self, packed):
        flat = packed.reshape(-1)
        out, off = [], 0
        for shp, n in zip(self.shapes, self.sizes):
            out.append(flat[off:off + n].reshape(shp))
            off += n
        return out

    def unpack_stacked(self, stacked, only=None):
        flat = stacked.reshape(stacked.shape[0], -1)
        offs = np.concatenate([[0], np.cumsum(self.sizes)])
        get = lambda i: flat[:, offs[i]:offs[i + 1]].reshape((stacked.shape[0],) + self.shapes[i])
        return get(only) if only is not None else [get(i) for i in range(len(self.shapes))]


def _row(v):
    return v.reshape(1, -1)


def _pad_lanes(v):
    v = v.reshape(1, -1)
    return jnp.pad(v, ((0, 0), (0, LANES - v.shape[1])))


def _ffn_fwd(h, w_gate_t, w_up_t, w_down, tag):
    F = w_down.shape[0]
    a = _mm(h, w_gate_t, tb=True, out_dtype=bf16, name=f"ffn_gate_{tag}")
    u = _mm(h, w_up_t, tb=True, out_dtype=bf16, name=f"ffn_up_{tag}")
    (f,), _ = _rowwise(f"swiglu_{tag}", lambda rv, vv: ([_silu(rv[0].astype(f32)) * rv[1].astype(f32)], []), [a, u], [],
                       [(F, bf16)], [], sub=16, col_chunk=_tile(F, 512, LANES))
    out = _mm(f, w_down, name=f"ffn_down_{tag}")
    return out, (a, u, f)


def _ffn_bwd(h, w_gate_t, w_up_t, w_down, saved, dout, tag):
    a, u, f = saved
    F = w_down.shape[0]
    df = _mm(dout, w_down, tb=True, out_dtype=bf16, name=f"ffn_down_dx_{tag}")
    dw_down = _mm(f, dout, ta=True, out_dtype=bf16, name=f"ffn_down_dw_{tag}")

    def fn(rv, vv):
        av, uv, dv = rv[0].astype(f32), rv[1].astype(f32), rv[2].astype(f32)
        sig = jax.nn.sigmoid(av)
        act = av * sig
        return [dv * uv * (sig + act * (1.0 - sig)), dv * act], []

    (da, du), _ = _rowwise(f"swiglu_bwd_{tag}", fn, [a, u, df], [], [(F, bf16), (F, bf16)], [], sub=16,
                           col_chunk=_tile(F, 512, LANES))
    dh = _mm(du, w_up_t, add=_mm(da, w_gate_t, name=f"ffn_gate_dx_{tag}"), name=f"ffn_up_dx_{tag}")
    dw_gate_t = _mm(da, h, ta=True, out_dtype=bf16, name=f"ffn_gate_dw_{tag}")
    dw_up_t = _mm(du, h, ta=True, out_dtype=bf16, name=f"ffn_up_dw_{tag}")
    return dh, dw_gate_t, dw_up_t, dw_down


def _local_step(x, target, mod, w_in_t, comm, small):
    S, D = x.shape
    di = small["hy_ssm_norm_g"].shape[-1]
    nh = small["hy_dt_bias"].shape[-1]
    cd = small["hy_conv_b"].shape[-1]
    m = [[_row(mod[i, j]) for j in range(6)] for i in range(2)]

    off_q = di + cd + nh
    w_qkv_t = w_in_t[off_q:]
    seg = dict(z=(w_in_t, 0, di), xbc=(w_in_t, di, cd), dt=(w_in_t, di + cd, LANES))
    for i, nm in enumerate(("q0", "q1", "q2", "k", "v")):
        seg[nm] = (w_qkv_t, i * D, D)

    g_mix = [_row(small["norm_mix_g"][i]) for i in range(2)]
    g_ffn = [_row(small["norm_ffn_g"][i]) for i in range(2)]
    conv_w, conv_b = small["hy_conv_w_full"], _row(small["hy_conv_b"][0])
    dt_bias, a_log, d_skip = (_pad_lanes(small[k][0]) for k in ("hy_dt_bias", "hy_a_log", "hy_d_skip"))
    g_ssm = _row(small["hy_ssm_norm_g"][0])
    onehot = jnp.asarray(_bucket_onehot())
    rel_t = small["rel_table"].T
    H = D // HEAD_DIM
    bias = [_exact_mm(rel_t[gi * H:(gi + 1) * H], onehot[gi], name=f"rel_bias_{gi}")
            .reshape(H, 2 * ATT_BLK, ATT_BLK) + _band_penalty() for gi in range(3)]

    h1 = _adaln_fwd(x, g_mix[0], m[0][1], m[0][0], "adaln_mix0")
    proj = {nm: _mm(h1, mat, tb=True, b_rows=(off, cnt), name=f"in_{nm}") for nm, (mat, off, cnt) in seg.items()}
    xbc_pre, xbc = _conv_fwd(proj["xbc"], conv_w, conv_b, silu=True, name="ssm_conv", tr=1024)
    y, hin = _ssd_fwd(xbc, proj["dt"], dt_bias, a_log, d_skip, di, "ssd_fwd")
    (yg,), _ = _rowwise("ssm_gate", lambda rv, vv: ([_gate_f(rv[0], rv[1], vv[0])], []),
                        [y, proj["z"]], [g_ssm], [(di, bf16)], [], sub=16)
    og = [_attn_fwd(proj[f"q{gi}"], proj["k"], proj["v"], bias[gi], d, f"attn_fwd_{gi}")
          for gi, d in enumerate(ATT_DILATIONS)]
    att, att_b, lse_tot = _attn_combine([a for a, _ in og], [_from_blocks(b, LANES) for _, b in og], "attn_combine")
    W = comm.weights(after=att_b)
    w_out_y, w_out_a = W["out"][:di], W["out"][di:]
    mix0 = _mm(att_b, w_out_a, add=_mm(yg, w_out_y, name="out_y"), name="out_a")
    x1, h2 = _resid_adaln_fwd(x, m[0][2], mix0, g_ffn[0], m[0][4], m[0][3], "resid_mix0_adaln_ffn0")
    f0, ffn0_saved = _ffn_fwd(h2, W["gate_t0"], W["up_t0"], W["down0"], "0")
    x2, h3 = _resid_adaln_fwd(x1, m[0][5], f0, g_mix[1], m[1][1], m[1][0], "resid_ffn0_adaln_mix1")
    pw1 = _mm(h3, W["pw1"], bias=_row(small["cv_b_pw1_full"]), name="cv_pw1")
    (u,), _ = _rowwise("cv_glu", lambda rv, vv: ([rv[0] * jax.nn.sigmoid(rv[1])], []),
                       [(pw1, 0, D), (pw1, 1, D)], [], [(D, f32)], [])
    (u2,) = _conv_fwd(u, small["cv_w_dw_full"], _row(small["cv_b_dw_full"]), silu=False, name="cv_dw")
    ln_g, ln_b = _row(small["cv_ln_g_full"]), _row(small["cv_ln_b_full"])
    (u3,), _ = _rowwise("cv_lnsilu", lambda rv, vv: ([_lnsilu_f(rv[0], vv[0], vv[1])], []),
                        [u2], [ln_g, ln_b], [(D, bf16)], [], sub=16)
    mix1 = _mm(u3, W["pw2"], bias=_row(small["cv_b_pw2_full"]), name="cv_pw2")
    x3, h4 = _resid_adaln_fwd(x2, m[1][2], mix1, g_ffn[1], m[1][4], m[1][3], "resid_mix1_adaln_ffn1")
    f1, ffn1_saved = _ffn_fwd(h4, W["gate_t1"], W["up_t1"], W["down1"], "1")

    g_fin = _row(small["final_norm_g"])
    dmod = [[None] * 6 for _ in range(2)]
    d_norm_mix, d_norm_ffn = [None, None], [None, None]
    big = {}

    def final_fn(rv, vv):
        xv, fv, tv = rv
        gate = vv[1]
        yv, vjp = jax.vjp(_rms, xv + gate * fv, vv[0])
        err = yv - tv
        dx, dg = vjp(err / D)
        part = 0.5 * jnp.sum(jnp.mean(err * err, -1, keepdims=True), 0, keepdims=True)
        return [dx, gate * dx], [dg, jnp.broadcast_to(part, (1, LANES)), jnp.sum(dx * fv, 0, keepdims=True)]

    (dx4, df1), (d_fin, loss, dmod[1][5]) = _rowwise("loss_head", final_fn, [x3, f1, target], [g_fin, m[1][5]],
                                                      [(D, f32), (D, bf16)], [D, LANES, D], sub=16)

    dh4, big["gate_t1"], big["up_t1"], big["down1"] = _ffn_bwd(h4, W["gate_t1"], W["up_t1"], W["down1"], ffn1_saved, df1, "1")
    dx3, dmix1, (d_norm_ffn[1], dmod[1][4], dmod[1][3], dmod[1][2], d_b_pw2) = _adaln_resid_bwd(
        x3, g_ffn[1], m[1][4], m[1][3], dh4, dx4, mix1, m[1][2], "adaln_ffn1_resid_mix1_bwd")
    du3 = _mm(dmix1, W["pw2"], tb=True, name="cv_pw2_dx")
    big["pw2"] = _mm(u3, dmix1, ta=True, out_dtype=bf16, name="cv_pw2_dw")

    def lnsilu_bwd(rv, vv):
        _, vjp = jax.vjp(_lnsilu_f, rv[0], vv[0], vv[1])
        du, dg, db = vjp(rv[1])
        return [du], [dg, db]

    (du2,), (d_ln_g, d_ln_b) = _rowwise("cv_lnsilu_bwd", lnsilu_bwd, [u2, du3], [ln_g, ln_b], [(D, f32)], [D, D])
    du, d_w_dw, d_b_dw = _conv_bwd(u, small["cv_w_dw_full"], du2, None, silu=False, name="cv_dw_bwd")

    def glu_bwd(rv, vv):
        a, gt, d = rv
        _, vjp = jax.vjp(lambda a_, g_: a_ * jax.nn.sigmoid(g_), a, gt)
        da, dg = vjp(d)
        return [da, dg], [jnp.sum(da, 0, keepdims=True), jnp.sum(dg, 0, keepdims=True)]

    (dpa, dpg), (d_b1a, d_b1g) = _rowwise("cv_glu_bwd", glu_bwd, [(pw1, 0, D), (pw1, 1, D), du], [],
                                           [(D, bf16), (D, bf16)], [D, D], sub=16)
    dpw1 = jnp.concatenate([dpa, dpg], axis=1)
    d_b_pw1 = jnp.concatenate([d_b1a, d_b1g], axis=1)
    dh3 = _mm(dpw1, W["pw1"], tb=True, name="cv_pw1_dx")
    big["pw1"] = _mm(h3, dpw1, ta=True, out_dtype=bf16, name="cv_pw1_dw")
    token = comm.send("layer1", {nm: big[nm] for nm in ("gate_t1", "up_t1", "down1", "pw2", "pw1")})
    dx2, df0, (d_norm_mix[1], dmod[1][1], dmod[1][0], dmod[0][5], _) = _adaln_resid_bwd(
        x2, g_mix[1], m[1][1], _tie(m[1][0], token), dh3, dx3, f0, m[0][5], "adaln_mix1_resid_ffn0_bwd")

    dh2, big["gate_t0"], big["up_t0"], big["down0"] = _ffn_bwd(h2, W["gate_t0"], W["up_t0"], W["down0"], ffn0_saved, df0, "0")
    dx1, dmix0, (d_norm_ffn[0], dmod[0][4], dmod[0][3], dmod[0][2], _) = _adaln_resid_bwd(
        x1, g_ffn[0], m[0][4], m[0][3], dh2, dx2, mix0, m[0][2], "adaln_ffn0_resid_mix0_bwd")
    dyg = _mm(dmix0, w_out_y, tb=True, name="out_y_dx")
    datt = _mm(dmix0, w_out_a, tb=True, name="out_a_dx")
    big["out"] = jnp.concatenate([_mm(yg, dmix0, ta=True, out_dtype=bf16, name="out_y_dw"),
                                  _mm(att_b, dmix0, ta=True, out_dtype=bf16, name="out_a_dw")], axis=0)
    token = comm.send("layer0", {nm: big[nm] for nm in ("gate_t0", "up_t0", "down0", "out")})
    g_ssm = _tie(g_ssm, token)

    def gate_bwd(rv, vv):
        _, vjp = jax.vjp(_gate_f, rv[0], rv[1], vv[0])
        dy_, dz_, dg_ = vjp(rv[2])
        return [dy_, dz_], [dg_]

    (dy, dz), (d_g_ssm,) = _rowwise("ssm_gate_bwd", gate_bwd, [y, proj["z"], dyg], [g_ssm], [(di, f32), (di, bf16)], [di],
                                    sub=16)
    dxbc, ddtraw, d_a_log, d_dskip, d_dt_bias = _ssd_bwd(xbc, proj["dt"], dt_bias, a_log, d_skip, hin, y, dy, di, "ssd_bwd")
    dxbc_pre, d_conv_w, d_conv_b = _conv_bwd(proj["xbc"], conv_w, dxbc, xbc_pre, silu=True, name="ssm_conv_bwd",
                                             dx_dtype=bf16, tr=1024)
    dh1 = None
    early = []
    for nm, dseg in (("z", dz), ("xbc", dxbc_pre), ("dt", ddtraw)):
        mat, off, cnt = seg[nm]
        dh1 = _mm(dseg, mat, b_rows=(off, cnt), add=dh1, name=f"in_{nm}_dx")
        dwp = _mm(dseg, h1, ta=True, out_dtype=bf16, name=f"in_{nm}_dw")
        early.append(dwp[:nh] if nm == "dt" else dwp)
    early = jnp.concatenate(early, axis=0)
    shard_rows = w_in_t.shape[0] // N_DEV
    n_early = off_q // shard_rows
    token = comm.send_rows("in_early", early[:n_early * shard_rows], (0, n_early))
    bias = [_tie(b, token) for b in bias]

    dq, dks, dvs, dbs = [], [], [], []
    lse_heads = lse_tot[:, :H]
    dsum_heads = _head_sums(att, datt, "attn_dsum")[:, :H]
    for gi, d in enumerate(ATT_DILATIONS):
        a, b, c_, e = _attn_bwd(proj[f"q{gi}"], proj["k"], proj["v"], bias[gi], datt,
                                _by_block(lse_heads, d), _by_block(dsum_heads, d), d, f"attn_bwd_{gi}")
        dq.append(a)
        dks.append(b)
        dvs.append(c_)
        dbs.append(e)
    dk = _add3(*dks, "attn_dk")
    dv = _add3(*dvs, "attn_dv")
    d_rel = jnp.concatenate(
        [_exact_mm(dbs[gi].reshape(H, -1), onehot[gi], tb=True, name=f"rel_grad_{gi}") for gi in range(3)], axis=0).T

    dsegs = (("q0", dq[0]), ("q1", dq[1]), ("q2", dq[2]), ("k", dk), ("v", dv))
    late = jnp.concatenate([early[n_early * shard_rows:]] +
                           [_mm(dseg, h1, ta=True, out_dtype=bf16, name=f"in_{nm}_dw") for nm, dseg in dsegs], axis=0)
    token = comm.send_rows("in_late", late, (n_early, N_DEV))
    w_qkv_after = _tie(w_qkv_t, token)
    for nm, dseg in dsegs:
        _, off, cnt = seg[nm]
        dh1 = _mm(dseg, w_qkv_after, b_rows=(off, cnt), add=dh1, name=f"in_{nm}_dx")
    dx0, (d_norm_mix[0], dmod[0][1], dmod[0][0]) = _adaln_bwd(x, g_mix[0], m[0][1], m[0][0], dh1, dx1, "adaln_mix0_bwd")

    smallg = dict(
        loss=loss, dmod=jnp.stack([jnp.concatenate(dmod[i], axis=1)[0] for i in range(2)]),
        norm_mix_g=jnp.concatenate(d_norm_mix, axis=0), norm_ffn_g=jnp.concatenate(d_norm_ffn, axis=0),
        hy_conv_w=d_conv_w, hy_conv_b=d_conv_b, hy_dt_bias=d_dt_bias[:, :nh], hy_a_log=d_a_log[:, :nh],
        hy_d_skip=d_dskip[:, :nh], hy_ssm_norm_g=d_g_ssm, rel_table=d_rel,
        cv_b_pw1=d_b_pw1, cv_w_dw=d_w_dw, cv_b_dw=d_b_dw, cv_ln_g=d_ln_g, cv_ln_b=d_ln_b, cv_b_pw2=d_b_pw2,
        final_norm_g=d_fin)
    return dx0, n_early, smallg


SMALL_GRAD_ORDER = ("loss", "dmod", "norm_mix_g", "norm_ffn_g", "hy_conv_w", "hy_conv_b", "hy_dt_bias", "hy_a_log",
                    "hy_d_skip", "hy_ssm_norm_g", "rel_table", "cv_b_pw1", "cv_w_dw", "cv_b_dw", "cv_ln_g", "cv_ln_b",
                    "cv_b_pw2", "final_norm_g")


def kernel(x, c, ada_w, ada_b, norm_mix_g, norm_ffn_g, hy_w_in, hy_conv_w, hy_conv_b, hy_dt_bias, hy_a_log, hy_d_skip, hy_ssm_norm_g, hy_w_out, rel_table, cv_w_pw1, cv_b_pw1, cv_w_dw, cv_b_dw, cv_ln_g, cv_ln_b, cv_w_pw2, cv_b_pw2, ffn_w_gate, ffn_w_up, ffn_w_down, final_norm_g, loss_target, m_ada_w, m_ada_b, m_norm_mix_g, m_norm_ffn_g, m_hy_w_in, m_hy_conv_w, m_hy_conv_b, m_hy_dt_bias, m_hy_a_log, m_hy_d_skip, m_hy_ssm_norm_g, m_hy_w_out, m_rel_table, m_cv_w_pw1, m_cv_b_pw1, m_cv_w_dw, m_cv_b_dw, m_cv_ln_g, m_cv_ln_b, m_cv_w_pw2, m_cv_b_pw2, m_ffn_w_gate, m_ffn_w_up, m_ffn_w_down, m_final_norm_g, v_ada_w, v_ada_b, v_norm_mix_g, v_norm_ffn_g, v_hy_w_in, v_hy_conv_w, v_hy_conv_b, v_hy_dt_bias, v_hy_a_log, v_hy_d_skip, v_hy_ssm_norm_g, v_hy_w_out, v_rel_table, v_cv_w_pw1, v_cv_b_pw1, v_cv_w_dw, v_cv_b_dw, v_cv_ln_g, v_cv_ln_b, v_cv_w_pw2, v_cv_b_pw2, v_ffn_w_gate, v_ffn_w_up, v_ffn_w_down, v_final_norm_g):
    names = ("ada_w", "ada_b", "norm_mix_g", "norm_ffn_g", "hy_w_in", "hy_conv_w", "hy_conv_b", "hy_dt_bias", "hy_a_log",
             "hy_d_skip", "hy_ssm_norm_g", "hy_w_out", "rel_table", "cv_w_pw1", "cv_b_pw1", "cv_w_dw", "cv_b_dw", "cv_ln_g",
             "cv_ln_b", "cv_w_pw2", "cv_b_pw2", "ffn_w_gate", "ffn_w_up", "ffn_w_down", "final_norm_g")
    w = dict(zip(names, (ada_w, ada_b, norm_mix_g, norm_ffn_g, hy_w_in, hy_conv_w, hy_conv_b, hy_dt_bias, hy_a_log, hy_d_skip,
                         hy_ssm_norm_g, hy_w_out, rel_table, cv_w_pw1, cv_b_pw1, cv_w_dw, cv_b_dw, cv_ln_g, cv_ln_b, cv_w_pw2,
                         cv_b_pw2, ffn_w_gate, ffn_w_up, ffn_w_down, final_norm_g)))
    mom = dict(zip(names, (m_ada_w, m_ada_b, m_norm_mix_g, m_norm_ffn_g, m_hy_w_in, m_hy_conv_w, m_hy_conv_b, m_hy_dt_bias,
                           m_hy_a_log, m_hy_d_skip, m_hy_ssm_norm_g, m_hy_w_out, m_rel_table, m_cv_w_pw1, m_cv_b_pw1, m_cv_w_dw,
                           m_cv_b_dw, m_cv_ln_g, m_cv_ln_b, m_cv_w_pw2, m_cv_b_pw2, m_ffn_w_gate, m_ffn_w_up, m_ffn_w_down,
                           m_final_norm_g)))
    vel = dict(zip(names, (v_ada_w, v_ada_b, v_norm_mix_g, v_norm_ffn_g, v_hy_w_in, v_hy_conv_w, v_hy_conv_b, v_hy_dt_bias,
                           v_hy_a_log, v_hy_d_skip, v_hy_ssm_norm_g, v_hy_w_out, v_rel_table, v_cv_w_pw1, v_cv_b_pw1, v_cv_w_dw,
                           v_cv_b_dw, v_cv_ln_g, v_cv_ln_b, v_cv_w_pw2, v_cv_b_pw2, v_ffn_w_gate, v_ffn_w_up, v_ffn_w_down,
                           v_final_norm_g)))
    S, D = x.shape[1], x.shape[2]
    ax, ay, ac = lax.axis_index("x"), lax.axis_index("y"), lax.axis_index("c")
    me = 4 * ax + 2 * ay + ac
    nmod = ada_w.shape[2]

    w2 = _shards_2d(w)
    big_names = list(w2)
    sharded_small = ("hy_conv_w", "cv_b_pw1", "cv_w_dw", "cv_b_dw", "cv_ln_g", "cv_ln_b", "cv_b_pw2")
    vp = _VecPack([c.shape] + [w[nm].shape for nm in sharded_small])
    g_in, sg = _all_gather([w2["in_t"].astype(bf16), vp.pack([c] + [w[nm] for nm in sharded_small])], "gather_w_in")
    w_in_t = _full_from_blocks("in_t", g_in)
    parts = vp.unpack_stacked(sg)
    c_all = parts[0][:, 0]
    small = {k: w[k] for k in ("norm_mix_g", "norm_ffn_g", "hy_conv_b", "hy_dt_bias", "hy_a_log", "hy_d_skip",
                               "hy_ssm_norm_g", "rel_table", "final_norm_g")}
    for p, nm in zip(parts[1:], sharded_small):
        p = p[:, 0]
        p = jnp.moveaxis(p, 0, -2)
        small[nm + "_full"] = p.reshape(p.shape[:-2] + (N_DEV * p.shape[-1],))

    (cs_all,), _ = _rowwise("ada_silu", lambda rv, vv: ([_silu(rv[0])], []), [c_all], [], [(D, f32)], [])
    b_mine = lax.dynamic_slice_in_dim(ada_b, me * nmod, nmod, axis=1)
    mod_part = jnp.stack([_mm(cs_all, ada_w[i], bias=b_mine[i:i + 1], name=f"ada_mod_{i}") for i in range(2)])
    (mod_all,) = _all_gather([mod_part.reshape(2 * N_DEV, nmod)], "gather_mod")
    mod_all = mod_all.reshape(N_DEV, 2, N_DEV, nmod)
    mod_mine = lax.dynamic_index_in_dim(mod_all, me, axis=2, keepdims=False)
    mod = jnp.transpose(mod_mine, (1, 0, 2)).reshape(2, 6, D)
    comm = _Overlap({nm: w2[nm].astype(bf16) for nm in big_names if nm != "in_t"}, me, after=[mod, w_in_t])
    mod = _tie(mod, comm.token)

    dx0, n_early, sgrad = _local_step(x[0], loss_target[0], mod, w_in_t, comm, small)

    gp = _VecPack([sgrad[k].shape for k in SMALL_GRAD_ORDER])
    small_handle, after = _exchange_start([gp.pack([sgrad[k] for k in SMALL_GRAD_ORDER])], True, "small_grads_start")
    m2, v2 = _shards_2d(mom), _shards_2d(vel)
    g2, d2, nm2, nv2 = {}, {}, {}, {}
    slots = {}
    for tag in ("layer1", "layer0"):
        slots.update(comm.received(tag, after))
        for nm in comm.sent[tag][0]:
            d2[nm], nm2[nm], nv2[nm], g2[nm] = _adamw(w2[nm], slots[nm], m2[nm], v2[nm], f"adamw_{nm}")
            after = g2[nm]
    (g_all,) = _exchange_wait(small_handle, after, "small_grads_wait")
    g_all = _with_own(g_all, small_handle[2][0], me)
    tot = dict(zip(SMALL_GRAD_ORDER, gp.unpack(_sum_slots(g_all, "sum_small_grads"))))
    dmod_all = gp.unpack_stacked(g_all, only=SMALL_GRAD_ORDER.index("dmod"))
    loss = tot["loss"][0, 0]

    grads = {}
    dmod_mine = lax.dynamic_slice_in_dim(dmod_all, me * nmod, nmod, axis=2)
    grads["ada_w"] = jnp.stack([_mm(cs_all, dmod_mine[:, i], ta=True, name=f"ada_w_grad_{i}") for i in range(2)])
    grads["ada_b"] = tot["dmod"]
    grads["norm_mix_g"], grads["norm_ffn_g"] = tot["norm_mix_g"], tot["norm_ffn_g"]
    grads["hy_conv_b"] = tot["hy_conv_b"]
    grads["hy_dt_bias"] = tot["hy_dt_bias"]
    grads["hy_a_log"] = tot["hy_a_log"]
    grads["hy_d_skip"] = tot["hy_d_skip"]
    grads["hy_ssm_norm_g"] = tot["hy_ssm_norm_g"]
    grads["rel_table"] = tot["rel_table"]
    grads["final_norm_g"] = tot["final_norm_g"][0]
    for nm in sharded_small:
        n = w[nm].shape[-1]
        grads[nm] = lax.dynamic_slice_in_dim(tot[nm], me * n, n, axis=1).reshape(w[nm].shape)

    delta, new_m, new_v = {}, {}, {}
    shp = ada_w.shape
    two = lambda t: t.reshape(-1, shp[-1])
    d_, m_, v_ = _adamw(two(ada_w), two(grads["ada_w"]), two(m_ada_w), two(v_ada_w), "adamw_ada_w")
    delta["ada_w"], new_m["ada_w"], new_v["ada_w"] = d_.reshape(shp), m_.reshape(shp), v_.reshape(shp)
    rest = [nm for nm in names if nm not in BIG_PARAMS and nm != "ada_w"]
    sp = _VecPack([w[nm].shape for nm in rest])
    packs = [sp.pack([t[nm] for nm in rest]) for t in (w, grads, mom, vel)]
    ds_, ms_, vs_ = _adamw(*packs, "adamw_small")
    for nm, a, b, e in zip(rest, sp.unpack(ds_), sp.unpack(ms_), sp.unpack(vs_)):
        delta[nm], new_m[nm], new_v[nm] = a, b, e

    slots.update(comm.received("in_early", ds_))
    slots.update(comm.received("in_late", slots["in_early"]))
    stored = lambda t: jnp.transpose(t, (2, 0, 1))
    in_slots = jnp.where(me < n_early, slots["in_early"], slots["in_late"])
    d_in, m_in, v_in, g_in = _adamw(stored(hy_w_in), in_slots, stored(m_hy_w_in), stored(v_hy_w_in), "adamw_in_t")
    for dst, part, t in ((grads, g2, g_in), (delta, d2, d_in), (new_m, nm2, m_in), (new_v, nv2, v_in)):
        dst.update(_unshard_2d(part))
        dst["hy_w_in"] = jnp.transpose(t, (1, 2, 0))

    return (loss, dx0[None], *[grads[n] for n in names], *[delta[n] for n in names],
            *[new_m[n] for n in names], *[new_v[n] for n in names])
```

```python
import functools
import math

import numpy as np
import jax
import jax.numpy as jnp
from jax import lax
from jax.experimental import pallas as pl
from jax.experimental.pallas import tpu as pltpu

f32 = jnp.float32
bf16 = jnp.bfloat16
EPS = 1e-6
N_DEV = 8
LANES = 128
SSM_STATE = 128
SSM_CHUNK = 128
SSM_GROUPS = 4
HEAD_DIM = 64
ATT_BLK = 128
ATT_DILATIONS = (1, 4, 16)
REL_BUCKETS = 32
REL_MAX_DIST = 2048
ADAM_LR, ADAM_B1, ADAM_B2, ADAM_EPS, ADAM_WD, ADAM_STEP = 0.001, 0.9, 0.999, 1e-08, 0.01, 10
PACK_ROW_TILE = 256
MESH = pl.DeviceIdType.MESH
VMEM_LIMIT = 48 * 1024 * 1024


def _sds(shape, dtype=f32):
    return jax.ShapeDtypeStruct(tuple(shape), dtype)


def _tile(n, cap, mult):
    best = None
    t = mult
    while t <= min(n, cap):
        if n % t == 0:
            best = t
        t += mult
    return best if best is not None else n


def _params(sem):
    return pltpu.CompilerParams(dimension_semantics=sem, vmem_limit_bytes=VMEM_LIMIT)


def _mm(a, b, *, name, ta=False, tb=False, b_rows=None, bias=None, add=None, out_dtype=f32,
        tm_cap=512, tn_cap=1536, tk_cap=8192):
    if ta:
        K, M = a.shape
    else:
        M, K = a.shape
    off, cnt = b_rows if b_rows is not None else (0, b.shape[0])
    if tb:
        N, K2 = cnt, b.shape[1]
    else:
        K2, N = cnt, b.shape[1]
    assert K == K2, (a.shape, b.shape, ta, tb, b_rows)
    if ta and a.dtype == f32:
        tm_cap = min(tm_cap, 256)
    if not ta:
        tm_cap = 2 * tm_cap
    tm = _tile(M, tm_cap, LANES)
    tn = _tile(math.gcd(off, N) if tb else N, tn_cap, LANES)
    tk = _tile(K if tb else math.gcd(off, K), tk_cap, LANES)
    assert N % tn == 0 and K % tk == 0 and off % (tn if tb else tk) == 0, (name, off, N, K, tn, tk)
    nk = K // tk
    jo, ko = (off // tn, 0) if tb else (0, off // tk)
    has_bias, has_add = bias is not None, add is not None
    dn = (((0 if ta else 1,), (1 if tb else 0,)), ((), ()))

    def body(*refs):
        a_ref, b_ref = refs[0], refs[1]
        pos = 2
        bias_ref = add_ref = None
        if has_bias:
            bias_ref = refs[pos]
            pos += 1
        if has_add:
            add_ref = refs[pos]
            pos += 1
        o_ref = refs[pos]
        k = pl.program_id(2)
        part = lax.dot_general(a_ref[...].astype(bf16), b_ref[...].astype(bf16), dn, preferred_element_type=f32)

        def finish(r):
            if has_bias:
                r = r + bias_ref[...]
            if has_add:
                r = r + add_ref[...]
            o_ref[...] = r.astype(o_ref.dtype)

        if nk == 1:
            finish(part)
        else:
            acc_ref = refs[pos + 1]

            @pl.when(k == 0)
            def _():
                acc_ref[...] = part

            @pl.when((k > 0) & (k < nk - 1))
            def _():
                acc_ref[...] += part

            @pl.when(k == nk - 1)
            def _():
                finish(acc_ref[...] + part)

    in_specs = [
        pl.BlockSpec((tk, tm), lambda i, j, k: (k, i)) if ta else pl.BlockSpec((tm, tk), lambda i, j, k: (i, k)),
        pl.BlockSpec((tn, tk), lambda i, j, k: (j + jo, k)) if tb else pl.BlockSpec((tk, tn), lambda i, j, k: (k + ko, j)),
    ]
    args = [a, b]
    if has_bias:
        in_specs.append(pl.BlockSpec((1, tn), lambda i, j, k: (0, j)))
        args.append(bias)
    if has_add:
        in_specs.append(pl.BlockSpec((tm, tn), lambda i, j, k: (i, j)))
        args.append(add)
    return pl.pallas_call(
        body, name=name, grid=(M // tm, N // tn, nk), in_specs=in_specs,
        out_specs=pl.BlockSpec((tm, tn), lambda i, j, k: (i, j)), out_shape=_sds((M, N), out_dtype),
        scratch_shapes=[pltpu.VMEM((tm, tn), f32)] if nk > 1 else [],
        compiler_params=_params(("parallel", "parallel", "arbitrary")),
    )(*args)


def _rowwise(name, fn, rows, vecs, out_rows, out_accs, *, tr_cap=512, sub=8, col_chunk=None):
    rows = [r if isinstance(r, tuple) else (r, 0, r.shape[1]) for r in rows]
    R = rows[0][0].shape[0]
    tr = _tile(R, tr_cap, 8)
    sub = sub if tr % sub == 0 else tr
    n_r, n_v, n_or, n_oa = len(rows), len(vecs), len(out_rows), len(out_accs)

    def body(*refs):
        row_refs = refs[:n_r]
        vec_refs = refs[n_r:n_r + n_v]
        orow_refs = refs[n_r + n_v:n_r + n_v + n_or]
        oacc_refs = refs[n_r + n_v + n_or:]
        vv = [r[...] for r in vec_refs]

        n_sub = tr // sub
        together = 4 if n_sub % 4 == 0 else 1

        def step(s, accs):
            for t in range(together):
                sl = pl.ds(pl.multiple_of((s * together + t) * sub, sub), sub)
                if col_chunk is None:
                    ro, ao = fn([r[sl, :] for r in row_refs], vv)
                    for o_ref, o in zip(orow_refs, ro):
                        o_ref[sl, :] = o.astype(o_ref.dtype)
                    accs = tuple(x + y for x, y in zip(accs, ao))
                else:
                    for c0 in range(0, rows[0][2], col_chunk):
                        cs_ = pl.ds(c0, col_chunk)
                        ro, _ = fn([r[sl, cs_] for r in row_refs], vv)
                        for o_ref, o in zip(orow_refs, ro):
                            o_ref[sl, cs_] = o.astype(o_ref.dtype)
            return accs

        accs = lax.fori_loop(0, n_sub // together, step, tuple(jnp.zeros((1, w), f32) for w in out_accs))
        if n_oa:
            @pl.when(pl.program_id(0) == 0)
            def _():
                for ref in oacc_refs:
                    ref[...] = jnp.zeros_like(ref)

            for ref, x in zip(oacc_refs, accs):
                ref[...] += x

    in_specs = [pl.BlockSpec((tr, w), functools.partial(lambda i, cb: (i, cb), cb=cb)) for (_, cb, w) in rows]
    in_specs += [pl.BlockSpec((1, v.shape[1]), lambda i: (0, 0)) for v in vecs]
    out_specs = [pl.BlockSpec((tr, w), lambda i: (i, 0)) for (w, _) in out_rows]
    out_specs += [pl.BlockSpec((1, w), lambda i: (0, 0)) for w in out_accs]
    out_shape = [_sds((R, w), dt) for (w, dt) in out_rows] + [_sds((1, w)) for w in out_accs]
    res = pl.pallas_call(
        body, name=name, grid=(R // tr,), in_specs=in_specs, out_specs=out_specs, out_shape=out_shape,
        compiler_params=_params(("arbitrary",)),
    )(*[r[0] for r in rows], *vecs)
    return res[:n_or], res[n_or:]


def _silu(x):
    return x * jax.nn.sigmoid(x)


def _rms(x, g):
    return x * lax.rsqrt(jnp.mean(x * x, -1, keepdims=True) + EPS) * g


def _adaln_f(x, g, sc, sh):
    return _rms(x, g) * (1.0 + sc) + sh


def _gate_f(y, z, g):
    return _rms(y * _silu(z), g)


def _lnsilu_f(u, g, b):
    mu = jnp.mean(u, -1, keepdims=True)
    var = jnp.mean(jnp.square(u - mu), -1, keepdims=True)
    return _silu((u - mu) * lax.rsqrt(var + EPS) * g + b)


def _adaln_fwd(x, g, sc, sh, name):
    (h,), _ = _rowwise(name, lambda rv, vv: ([_adaln_f(rv[0], *vv)], []), [x], [g, sc, sh], [(x.shape[1], bf16)], [],
                       sub=16)
    return h


def _adaln_bwd(x, g, sc, sh, dh, dres, name):
    def fn(rv, vv):
        xv, dhv, drv = rv
        _, vjp = jax.vjp(_adaln_f, xv, *vv)
        dx, dg, dsc, dsh = vjp(dhv)
        return [dx + drv], [dg, dsc, dsh]
    w = x.shape[1]
    (dx,), accs = _rowwise(name, fn, [x, dh, dres], [g, sc, sh], [(w, f32)], [w, w, w])
    return dx, accs


def _resid_adaln_fwd(x, gate, mix, g, sc, sh, name):
    def fn(rv, vv):
        xn = rv[0] + vv[0] * rv[1]
        return [xn, _adaln_f(xn, vv[1], vv[2], vv[3])], []
    w = x.shape[1]
    (xn, h), _ = _rowwise(name, fn, [x, mix], [gate, g, sc, sh], [(w, f32), (w, bf16)], [], sub=16)
    return xn, h


def _adaln_resid_bwd(x, g, sc, sh, dh, dres, mix, gate, name):
    def fn(rv, vv):
        xv, dhv, drv, mv = rv
        _, vjp = jax.vjp(_adaln_f, xv, vv[0], vv[1], vv[2])
        dx, dg, dsc, dsh = vjp(dhv)
        dx = dx + drv
        dm = vv[3] * dx
        return [dx, dm], [dg, dsc, dsh, jnp.sum(dx * mv, 0, keepdims=True), jnp.sum(dm, 0, keepdims=True)]
    w = x.shape[1]
    (dx, dmix), accs = _rowwise(name, fn, [x, dh, dres, mix], [g, sc, sh, gate], [(w, f32), (w, bf16)], [w] * 5, sub=16)
    return dx, dmix, accs


def _add3(a, b, c, name):
    (y,), _ = _rowwise(name, lambda rv, vv: ([rv[0] + rv[1] + rv[2]], []), [a, b, c], [], [(a.shape[1], bf16)], [],
                       sub=16)
    return y


CONV_HALO = 32
CONV_ROWS = 64


def _conv_fwd(x, w, b, *, silu, name, tr=512):
    S, C = x.shape
    K = w.shape[0]
    H = CONV_HALO
    assert K - 1 <= H and S % tr == 0 and tr % H == 0 and C % LANES == 0
    nh = tr // H

    def body(xp_ref, xc_ref, w_ref, b_ref, *rest):
        outs, scr = rest[:-1], rest[-1]
        i = pl.program_id(1)
        scr[pl.ds(0, H), :] = jnp.where(i > 0, xp_ref[...], 0.0)
        scr[pl.ds(H, tr), :] = xc_ref[...]
        taps = [w_ref[pl.ds(k, 1), :] for k in range(K)]
        for c0 in range(0, tr, CONV_ROWS):
            acc = jnp.zeros((CONV_ROWS, LANES), f32) + b_ref[...]
            for k in range(K):
                acc = acc + scr[pl.ds(c0 + H - (K - 1) + k, CONV_ROWS), :] * taps[k]
            outs[0][pl.ds(c0, CONV_ROWS), :] = acc.astype(outs[0].dtype)
            if silu:
                outs[1][pl.ds(c0, CONV_ROWS), :] = _silu(acc)

    n_out = 2 if silu else 1
    return pl.pallas_call(
        body, name=name, grid=(C // LANES, S // tr),
        in_specs=[pl.BlockSpec((H, LANES), lambda j, i: (jnp.maximum(i * nh - 1, 0), j)),
                  pl.BlockSpec((tr, LANES), lambda j, i: (i, j)),
                  pl.BlockSpec((K, LANES), lambda j, i: (0, j)),
                  pl.BlockSpec((1, LANES), lambda j, i: (0, j))],
        out_specs=[pl.BlockSpec((tr, LANES), lambda j, i: (i, j))] * n_out,
        out_shape=[_sds((S, C), bf16), _sds((S, C))] if silu else [_sds((S, C))],
        scratch_shapes=[pltpu.VMEM((tr + H, LANES), f32)],
        compiler_params=_params(("parallel", "arbitrary")),
    )(x, x, w, b)


def _conv_bwd(x, w, dact, pre, *, silu, name, dx_dtype=f32, tr=512):
    S, C = x.shape
    K = w.shape[0]
    H = CONV_HALO
    nh = tr // H
    n_i = S // tr
    kp = -(-K // 8) * 8

    def dsilu(p):
        s = jax.nn.sigmoid(p)
        return s * (1.0 + p * (1.0 - s))

    def body(*refs):
        if silu:
            xp_ref, xc_ref, w_ref, dc_ref, dn_ref, pc_ref, pn_ref, dx_ref, dw_ref, db_ref, xs, ds = refs
        else:
            xp_ref, xc_ref, w_ref, dc_ref, dn_ref, dx_ref, dw_ref, db_ref, xs, ds = refs
        i = pl.program_id(1)
        xs[pl.ds(0, H), :] = jnp.where(i > 0, xp_ref[...], 0.0)
        xs[pl.ds(H, tr), :] = xc_ref[...]
        dcur = dc_ref[...]
        dnext = dn_ref[...]
        if silu:
            dcur = dcur * dsilu(pc_ref[...].astype(f32))
            dnext = dnext * dsilu(pn_ref[...].astype(f32))
        ds[pl.ds(0, tr), :] = dcur
        ds[pl.ds(tr, H), :] = jnp.where(i < n_i - 1, dnext, 0.0)
        taps = [w_ref[pl.ds(k, 1), :] for k in range(K)]
        fold = lambda t: jnp.sum(t.reshape(CONV_ROWS // 8, 8, LANES), axis=0)
        dw_parts = [jnp.zeros((8, LANES), f32) for _ in range(K)]
        db_part = jnp.zeros((8, LANES), f32)
        for c0 in range(0, tr, CONV_ROWS):
            acc = jnp.zeros((CONV_ROWS, LANES), f32)
            d_c = ds[pl.ds(c0, CONV_ROWS), :]
            for k in range(K):
                acc = acc + ds[pl.ds(c0 + K - 1 - k, CONV_ROWS), :] * taps[k]
                dw_parts[k] = dw_parts[k] + fold(d_c * xs[pl.ds(c0 + H - (K - 1) + k, CONV_ROWS), :])
            db_part = db_part + fold(d_c)
            dx_ref[pl.ds(c0, CONV_ROWS), :] = acc.astype(dx_ref.dtype)

        @pl.when(i == 0)
        def _():
            dw_ref[...] = jnp.zeros_like(dw_ref)
            db_ref[...] = jnp.zeros_like(db_ref)

        for k in range(K):
            dw_ref[pl.ds(k, 1), :] += jnp.sum(dw_parts[k], 0, keepdims=True)
        db_ref[...] += jnp.sum(db_part, 0, keepdims=True)

    prev = pl.BlockSpec((H, LANES), lambda j, i: (jnp.maximum(i * nh - 1, 0), j))
    cur = pl.BlockSpec((tr, LANES), lambda j, i: (i, j))
    nxt = pl.BlockSpec((H, LANES), lambda j, i: (jnp.minimum((i + 1) * nh, n_i * nh - 1), j))
    in_specs = [prev, cur, pl.BlockSpec((K, LANES), lambda j, i: (0, j)), cur, nxt]
    args = [x, x, w, dact, dact]
    if silu:
        in_specs += [cur, nxt]
        args += [pre, pre]
    dx, dw, db = pl.pallas_call(
        body, name=name, grid=(C // LANES, n_i), in_specs=in_specs,
        out_specs=[cur, pl.BlockSpec((kp, LANES), lambda j, i: (0, j)), pl.BlockSpec((1, LANES), lambda j, i: (0, j))],
        out_shape=[_sds((S, C), dx_dtype), _sds((kp, C)), _sds((1, C))],
        scratch_shapes=[pltpu.VMEM((tr + H, LANES), f32), pltpu.VMEM((tr + H, LANES), f32)],
        compiler_params=_params(("parallel", "arbitrary")),
    )(*args)
    return dx, dw[:K], db


def _dot(a, b):
    return jnp.dot(a.astype(bf16), b.astype(bf16), preferred_element_type=f32)


def _dot_nt(a, b):
    return lax.dot_general(a.astype(bf16), b.astype(bf16), (((1,), (1,)), ((), ())), preferred_element_type=f32)


def _dot_tn(a, b):
    return lax.dot_general(a.astype(bf16), b.astype(bf16), (((0,), (0,)), ((), ())), preferred_element_type=f32)


def _softplus(x):
    return jnp.maximum(x, 0.0) + jnp.log(1.0 + jnp.exp(-jnp.abs(x)))


def _tri(q):
    i = lax.broadcasted_iota(jnp.int32, (q, q), 0)
    j = lax.broadcasted_iota(jnp.int32, (q, q), 1)
    return i >= j


def _ssd_prep(dtraw, dt_bias, a_log):
    q = dtraw.shape[0]
    dt = _softplus(dtraw + dt_bias)
    A = -jnp.exp(a_log)
    tri = _tri(q)
    cs = jnp.dot(tri.astype(f32), dt * A, preferred_element_type=f32, precision=lax.Precision.HIGHEST)
    return dt, A, cs, cs.T, tri


def _expand(cols, h0, n, width):
    q = cols.shape[0]
    return jnp.concatenate([jnp.broadcast_to(cols[:, h0 + r:h0 + r + 1], (q, width)) for r in range(n)], axis=1)


def _ssd_fwd(xbc, dtraw, dt_bias, a_log, d_skip, di, name):
    S, CD = xbc.shape
    Q, N, G = SSM_CHUNK, SSM_STATE, SSM_GROUPS
    nc = S // Q
    nh = di // HEAD_DIM
    R = nh // G
    gw = R * HEAD_DIM
    col_of_head = jnp.asarray((np.arange(LANES)[:, None] == np.arange(di)[None, :] // HEAD_DIM).astype(np.float32))
    dsk_wide = jnp.repeat(d_skip[0, :nh], HEAD_DIM)[None]

    def body(xbc_ref, dt_ref, bias_ref, alog_ref, dskw_ref, coh_ref, y_ref, hin_ref, state):
        c = pl.program_id(0)

        @pl.when(c == 0)
        def _():
            state[...] = jnp.zeros_like(state)

        hin_ref[...] = state[...]
        dt, A, cs, csT, tri = _ssd_prep(dt_ref[...], bias_ref[...], alog_ref[...])
        elast = jnp.exp(cs[Q - 1:Q, :])
        coh = coh_ref[...]
        dt_w, ecs_w, dend_w = _dot_exact(dt, coh), _dot_exact(jnp.exp(cs), coh), _dot_exact(jnp.exp(cs[Q - 1:Q, :] - cs), coh)
        for g in range(G):
            h0 = g * R
            cols = pl.ds(g * gw, gw)
            lanes = slice(g * gw, (g + 1) * gw)
            Bg = xbc_ref[:, pl.ds(di + g * N, N)]
            Cg = xbc_ref[:, pl.ds(di + G * N + g * N, N)]
            xg = xbc_ref[:, cols]
            Hg = state[cols, :]
            Gm = _dot_nt(Cg, Bg)
            xdt = xg * dt_w[:, lanes]
            yoff = _dot_nt(Cg, Hg) * ecs_w[:, lanes]
            ys = []
            for r in range(R):
                h = h0 + r
                L = jnp.exp(jnp.where(tri, cs[:, h:h + 1] - csT[h:h + 1, :], -jnp.inf))
                ys.append(_dot(Gm * L, xdt[:, r * HEAD_DIM:(r + 1) * HEAD_DIM]))
            y_ref[:, cols] = jnp.concatenate(ys, axis=1) + yoff + xg * dskw_ref[:, cols]
            hnew = _dot_tn(xdt * dend_w[:, lanes], Bg)
            escale = jnp.concatenate([jnp.broadcast_to(elast[:, h0 + r:h0 + r + 1], (HEAD_DIM, N)) for r in range(R)], axis=0)
            state[cols, :] = escale * Hg + hnew

    vec = pl.BlockSpec((1, LANES), lambda c: (0, 0))
    return pl.pallas_call(
        body, name=name, grid=(nc,),
        in_specs=[pl.BlockSpec((Q, CD), lambda c: (c, 0)), pl.BlockSpec((Q, LANES), lambda c: (c, 0)), vec, vec,
                  pl.BlockSpec((1, di), lambda c: (0, 0)), pl.BlockSpec((LANES, di), lambda c: (0, 0))],
        out_specs=[pl.BlockSpec((Q, di), lambda c: (c, 0)), pl.BlockSpec((None, di, N), lambda c: (c, 0, 0))],
        out_shape=[_sds((S, di)), _sds((nc, di, N))],
        scratch_shapes=[pltpu.VMEM((di, N), f32)],
        compiler_params=_params(("arbitrary",)),
    )(xbc, dtraw, dt_bias, a_log, dsk_wide, col_of_head)


def _dot_exact(a, b):
    bb = b.astype(bf16)
    hi = a.astype(bf16)
    rest = a - hi.astype(f32)
    mid = rest.astype(bf16)
    low = (rest - mid.astype(f32)).astype(bf16)
    one_pass = lambda t: jnp.dot(t, bb, preferred_element_type=f32)
    return one_pass(hi) + one_pass(mid) + one_pass(low)


def _ssd_bwd(xbc, dtraw, dt_bias, a_log, d_skip, hin, y, dy, di, name):
    S, CD = xbc.shape
    Q, N, G = SSM_CHUNK, SSM_STATE, SSM_GROUPS
    nc = S // Q
    nh = di // HEAD_DIM
    R = nh // G
    gw = R * HEAD_DIM
    P = HEAD_DIM
    head_of_col = jnp.asarray((np.arange(di)[:, None] // P == np.arange(LANES)[None, :]).astype(np.float32))
    dsk_wide = jnp.repeat(d_skip[0, :nh], P)[None]

    def body(xbc_ref, dt_ref, bias_ref, alog_ref, dskw_ref, hoc_ref, hin_ref, y_ref, dy_ref,
             dxbc_ref, ddt_ref, dA_ref, ddsk_ref, dtb_ref, dstate, dxdt_all, tend_all, yoff_all, colterm_all):
        c = pl.program_id(0)

        @pl.when(c == 0)
        def _():
            dstate[...] = jnp.zeros_like(dstate)
            dA_ref[...] = jnp.zeros_like(dA_ref)
            ddsk_ref[...] = jnp.zeros_like(ddsk_ref)
            dtb_ref[...] = jnp.zeros_like(dtb_ref)

        dtraw_v = dt_ref[...]
        dt, A, cs, csT, tri = _ssd_prep(dtraw_v, bias_ref[...], alog_ref[...])
        tri_t = jnp.logical_not(tri) | (lax.broadcasted_iota(jnp.int32, (Q, Q), 0) == lax.broadcasted_iota(jnp.int32, (Q, Q), 1))
        ecs = jnp.exp(cs)
        dend = jnp.exp(cs[Q - 1:Q, :] - cs)
        elast = jnp.exp(cs[Q - 1:Q, :])
        hoc = hoc_ref[...]
        state_dot = jnp.sum(_dot_exact(dstate[...] * hin_ref[...], jnp.ones((N, LANES), f32)) * hoc, 0, keepdims=True) * elast
        for g in range(G):
            h0 = g * R
            Bg = xbc_ref[:, pl.ds(di + g * N, N)]
            Cg = xbc_ref[:, pl.ds(di + G * N + g * N, N)]
            xg = xbc_ref[:, pl.ds(g * gw, gw)]
            dyg = dy_ref[:, pl.ds(g * gw, gw)]
            Hg = hin_ref[pl.ds(g * gw, gw), :]
            dHg = dstate[pl.ds(g * gw, gw), :]
            dt_e = _expand(dt, h0, R, P)
            ecs_e = _expand(ecs, h0, R, P)
            dend_e = _expand(dend, h0, R, P)
            cols = pl.ds(g * gw, gw)
            Gm = _dot_nt(Cg, Bg)
            Gm_t = _dot_nt(Bg, Cg)
            xdt = xg * dt_e
            dye = dyg * ecs_e
            bdh = _dot_nt(Bg, dHg)
            dC = _dot(dye, Hg)
            dB = _dot(xdt * dend_e, dHg)
            dHin = _dot_tn(dye, Cg)
            dxdt_state = dend_e * bdh
            end_term = xdt * dxdt_state
            tend_all[:, cols] = end_term
            yoff_all[:, cols] = _dot_nt(Cg, Hg) * ecs_e
            dG = jnp.zeros((Q, Q), f32)
            dxd = []
            for r in range(R):
                h = h0 + r
                sl = slice(r * P, (r + 1) * P)
                seg = cs[:, h:h + 1] - csT[h:h + 1, :]
                L = jnp.exp(jnp.where(tri, seg, -jnp.inf))
                L_t = jnp.exp(jnp.where(tri_t, -seg, -jnp.inf))
                dyh = dyg[:, sl]
                dG = dG + _dot_nt(dyh, xdt[:, sl]) * L
                dxd.append(_dot(Gm_t * L_t, dyh))
            dxdt_diag = jnp.concatenate(dxd, axis=1)
            dxdt = dxdt_diag + dxdt_state
            dxdt_all[:, cols] = dxdt
            colterm_all[:, cols] = xdt.astype(bf16).astype(f32) * dxdt_diag + end_term
            dxbc_ref[:, cols] = dxdt * dt_e + dyg * dskw_ref[:, cols]
            dxbc_ref[:, pl.ds(di + g * N, N)] = dB + _dot_tn(dG, Cg)
            dxbc_ref[:, pl.ds(di + G * N + g * N, N)] = dC + _dot(dG, Bg)
            escale = jnp.concatenate([jnp.broadcast_to(elast[:, h0 + r:h0 + r + 1], (P, N)) for r in range(R)], axis=0)
            dstate[pl.ds(g * gw, gw), :] = escale * dHg + dHin
        xs = xbc_ref[:, pl.ds(0, di)]
        dyv = dy_ref[...]
        yoff = yoff_all[...]
        y_diag = y_ref[...] - dskw_ref[...] * xs - yoff
        rs_y = _dot_exact(dyv.astype(bf16).astype(f32) * y_diag + dyv * yoff, hoc)
        rs_c = _dot_exact(colterm_all[...], hoc)
        rs_x = _dot_exact(dxdt_all[...] * xs, hoc)
        end_dot = _dot_exact(jnp.broadcast_to(jnp.sum(tend_all[...], 0, keepdims=True), (8, di)), hoc)[0:1]
        last = lax.broadcasted_iota(jnp.int32, (Q, 1), 0) == Q - 1
        dcs = rs_y - rs_c + jnp.where(last, end_dot + state_dot, 0.0)
        da = lax.dot_general(tri.astype(f32), dcs, (((0,), (0,)), ((), ())), preferred_element_type=f32,
                             precision=lax.Precision.HIGHEST)
        ddt = da * A + rs_x
        ddtraw = ddt * jax.nn.sigmoid(dtraw_v + bias_ref[...])
        ddt_ref[...] = ddtraw.astype(ddt_ref.dtype)
        dA_ref[...] += jnp.sum(da * dt, 0, keepdims=True) * A
        ddsk_ref[...] += jnp.sum(_dot_exact(dyv * xs, hoc), 0, keepdims=True)
        dtb_ref[...] += jnp.sum(ddtraw, 0, keepdims=True)

    vec = pl.BlockSpec((1, LANES), lambda c: (0, 0))
    rev = lambda c: (nc - 1 - c, 0)
    return pl.pallas_call(
        body, name=name, grid=(nc,),
        in_specs=[pl.BlockSpec((Q, CD), rev), pl.BlockSpec((Q, LANES), rev), vec, vec,
                  pl.BlockSpec((1, di), lambda c: (0, 0)), pl.BlockSpec((di, LANES), lambda c: (0, 0)),
                  pl.BlockSpec((None, di, N), lambda c: (nc - 1 - c, 0, 0)), pl.BlockSpec((Q, di), rev),
                  pl.BlockSpec((Q, di), rev)],
        out_specs=[pl.BlockSpec((Q, CD), rev), pl.BlockSpec((Q, LANES), rev), vec, vec, vec],
        out_shape=[_sds((S, CD)), _sds((S, LANES), bf16), _sds((1, LANES)), _sds((1, LANES)), _sds((1, LANES))],
        scratch_shapes=[pltpu.VMEM((di, N), f32)] + [pltpu.VMEM((Q, di), f32)] * 4,
        compiler_params=_params(("arbitrary",)),
    )(xbc, dtraw, dt_bias, a_log, dsk_wide, head_of_col, hin, y, dy)


def _t5_bucket_np(dist):
    max_exact = REL_BUCKETS // 2
    n = np.maximum(dist, 1).astype(np.float32)
    large = np.float32(max_exact) + np.log(n / np.float32(max_exact)) / np.float32(math.log(REL_MAX_DIST / max_exact)) * np.float32(REL_BUCKETS - max_exact)
    large = np.minimum(large.astype(np.int32), REL_BUCKETS - 1)
    return np.where(dist < max_exact, dist, large)


def _bucket_onehot():
    i = np.arange(ATT_BLK)[None, :]
    j = np.arange(2 * ATT_BLK)[:, None]
    delta = np.maximum(ATT_BLK + i - j, 0)
    out = np.zeros((len(ATT_DILATIONS), REL_BUCKETS, ATT_BLK * 2 * ATT_BLK), np.float32)
    for gi, d in enumerate(ATT_DILATIONS):
        b = _t5_bucket_np(delta * d).reshape(-1)
        out[gi, b, np.arange(b.size)] = 1.0
    return out


def _exact_mm(a, b, *, name, tb=False):
    M, K = a.shape
    N = b.shape[0] if tb else b.shape[1]
    tn = _tile(N, 4096, LANES)
    dn = (((1,), (1 if tb else 0,)), ((), ()))

    def body(a_ref, b_ref, o_ref):
        o_ref[...] = lax.dot_general(a_ref[...], b_ref[...], dn, preferred_element_type=f32,
                                     precision=lax.Precision.HIGHEST)

    return pl.pallas_call(
        body, name=name, grid=(N // tn,),
        in_specs=[pl.BlockSpec((M, K), lambda j: (0, 0)),
                  pl.BlockSpec((tn, K), lambda j: (j, 0)) if tb else pl.BlockSpec((K, tn), lambda j: (0, j))],
        out_specs=pl.BlockSpec((M, tn), lambda j: (0, j)), out_shape=_sds((M, N)),
        compiler_params=_params(("parallel",)),
    )(a, b)


def _band_penalty():
    i = np.arange(ATT_BLK)[None, :]
    j = np.arange(2 * ATT_BLK)[:, None]
    delta = ATT_BLK + i - j
    return np.where((delta >= 0) & (delta <= ATT_BLK), 0.0, -np.inf).astype(np.float32)


def _first_block_keep(n):
    key = lax.broadcasted_iota(jnp.int32, (2 * ATT_BLK, ATT_BLK), 0)
    return (key >= ATT_BLK) | (n > 0)


ATT_SCALE = HEAD_DIM ** -0.5


def _rows(ref, r, d):
    return ref[...] if d == 1 else ref[pl.ds(r, ATT_BLK, stride=d), :]


def _set_rows(ref, r, d, val):
    if d == 1:
        ref[...] = val
    else:
        ref[pl.ds(r, ATT_BLK, stride=d), :] = val


def _attn_width(d, D):
    return D if d == 1 else LANES


def _over_residues(d, one, unroll=1):
    if d == 1:
        one(0)
    else:
        lax.fori_loop(0, d, lambda r, c: (one(r), c)[1], 0, unroll=unroll)


def _attn_fwd(q, k, v, bias_t, d, name):
    S, D = q.shape
    nb = S // (d * ATT_BLK)
    H = D // HEAD_DIM
    W = _attn_width(d, D)
    HB = W // HEAD_DIM

    def body(q_ref, kp_ref, kc_ref, vp_ref, vc_ref, b_ref, o_ref, lse_ref):
        keep = _first_block_keep(pl.program_id(1))
        first = lax.broadcasted_iota(jnp.int32, (1, LANES), 1) < HEAD_DIM

        def one(r):
            qs = (_rows(q_ref, r, d) * ATT_SCALE).astype(bf16)
            kcat = jnp.concatenate([_rows(kp_ref, r, d), _rows(kc_ref, r, d)], axis=0).astype(bf16)
            vcat = jnp.concatenate([_rows(vp_ref, r, d), _rows(vc_ref, r, d)], axis=0).astype(bf16)
            outs = []
            for pair in range(W // LANES):
                ps = slice(pair * LANES, (pair + 1) * LANES)
                q2, k2, v2 = qs[:, ps], kcat[:, ps], vcat[:, ps]
                o2 = jnp.zeros((ATT_BLK, LANES), f32)
                for e in range(2):
                    h = 2 * pair + e
                    mine = first if e == 0 else jnp.logical_not(first)
                    zero = jnp.zeros((), bf16)
                    st = jnp.where(keep, _dot_nt(k2, jnp.where(mine, q2, zero)) + b_ref[h], -jnp.inf)
                    m = jnp.max(st, 0, keepdims=True)
                    pt = jnp.exp(st - m)
                    l = jnp.sum(pt, 0, keepdims=True)
                    o2 = o2 + _dot_tn(pt * (1.0 / l), jnp.where(mine, v2, zero))
                    lse_ref[r, h] = m + jnp.log(l)
                outs.append(o2)
            _set_rows(o_ref, r, d, jnp.concatenate(outs, axis=1))

        _over_residues(d, one, unroll=4)

    cur = pl.BlockSpec((ATT_BLK * d, W), lambda j, n: (n, j))
    prev = pl.BlockSpec((ATT_BLK * d, W), lambda j, n: (jnp.maximum(n - 1, 0), j))
    return pl.pallas_call(
        body, name=name, grid=(D // W, nb),
        in_specs=[cur, prev, cur, prev, cur, pl.BlockSpec((HB, 2 * ATT_BLK, ATT_BLK), lambda j, n: (j, 0, 0))],
        out_specs=[cur, pl.BlockSpec((None, d, HB, 1, LANES), lambda j, n: (n, 0, j, 0, 0))],
        out_shape=[_sds((S, D)), _sds((nb, d, H, 1, LANES))],
        compiler_params=_params(("parallel", "arbitrary")),
    )(q, k, k, v, v, bias_t)


def _from_blocks(rows, lanes=None):
    nb, d, H = rows.shape[:3]
    a = jnp.transpose(rows[:, :, :, 0, :], (0, 3, 1, 2)).reshape(nb * ATT_BLK * d, H)
    return a if lanes is None else jnp.pad(a, ((0, 0), (0, lanes - H)))


def _by_block(a, d):
    S, H = a.shape
    t = jnp.transpose(a.reshape(S // (d * ATT_BLK), ATT_BLK, d, H), (0, 2, 3, 1))
    return t[:, :, :, None, :]


def _head_sums(a, b, name):
    S, D = a.shape
    tr = _tile(S, 512, 8)
    hoc = jnp.asarray((np.arange(D)[:, None] // HEAD_DIM == np.arange(LANES)[None, :]).astype(np.float32))

    def body(a_ref, b_ref, h_ref, o_ref):
        o_ref[...] = _dot_exact(a_ref[...] * b_ref[...], h_ref[...])

    return pl.pallas_call(
        body, name=name, grid=(S // tr,),
        in_specs=[pl.BlockSpec((tr, D), lambda i: (i, 0)), pl.BlockSpec((tr, D), lambda i: (i, 0)),
                  pl.BlockSpec((D, LANES), lambda i: (0, 0))],
        out_specs=pl.BlockSpec((tr, LANES), lambda i: (i, 0)), out_shape=_sds((S, LANES)),
        compiler_params=_params(("parallel",)),
    )(a, b, hoc)


def _attn_bwd(q, k, v, bias_t, datt, lse_rows, dsum_rows, d, name):
    S, D = q.shape
    nb = S // (d * ATT_BLK)
    H = D // HEAD_DIM
    W = _attn_width(d, D)
    HB = W // HEAD_DIM

    def body(q_ref, kp_ref, kc_ref, vp_ref, vc_ref, b_ref, do_ref, lse_ref, dsum_ref,
             dq_ref, dk_ref, dv_ref, db_ref, carry_k, carry_v):
        j = pl.program_id(0)
        n = pl.program_id(1)

        @pl.when(n == 0)
        def _():
            carry_k[...] = jnp.zeros_like(carry_k)
            carry_v[...] = jnp.zeros_like(carry_v)
            db_ref[...] = jnp.zeros_like(db_ref)

        @pl.when(n < nb)
        def _():
            key = lax.broadcasted_iota(jnp.int32, (2 * ATT_BLK, ATT_BLK), 0)
            keep = (key >= ATT_BLK) | (n > 0)
            first = lax.broadcasted_iota(jnp.int32, (1, LANES), 1) < HEAD_DIM

            def one(r):
                qs = (_rows(q_ref, r, d) * ATT_SCALE).astype(bf16)
                kcat = jnp.concatenate([_rows(kp_ref, r, d), _rows(kc_ref, r, d)], axis=0).astype(bf16)
                vcat = jnp.concatenate([_rows(vp_ref, r, d), _rows(vc_ref, r, d)], axis=0).astype(bf16)
                dob = _rows(do_ref, r, d).astype(bf16)
                dqs, dks, dvs = [], [], []
                for pair in range(W // LANES):
                    ps = slice(pair * LANES, (pair + 1) * LANES)
                    q2, k2, v2, do2 = qs[:, ps], kcat[:, ps], vcat[:, ps], dob[:, ps]
                    dq2 = jnp.zeros((ATT_BLK, LANES), f32)
                    dk2 = jnp.zeros((2 * ATT_BLK, LANES), f32)
                    dv2 = jnp.zeros((2 * ATT_BLK, LANES), f32)
                    for e in range(2):
                        h = 2 * pair + e
                        mine = first if e == 0 else jnp.logical_not(first)
                        zero = jnp.zeros((), bf16)
                        qm, dom, km = jnp.where(mine, q2, zero), jnp.where(mine, do2, zero), jnp.where(mine, k2, zero)
                        st = jnp.where(keep, _dot_nt(k2, qm) + b_ref[h], -jnp.inf)
                        pt = jnp.exp(st - lse_ref[r, j * HB + h])
                        dst = pt * (_dot_nt(v2, dom) - dsum_ref[r, j * HB + h])
                        db_ref[h] += dst
                        dv2 = dv2 + _dot(pt, dom)
                        dk2 = dk2 + _dot(dst, qm)
                        dq2 = dq2 + _dot_tn(dst, km)
                    dqs.append(dq2 * ATT_SCALE)
                    dks.append(dk2)
                    dvs.append(dv2)
                _set_rows(dq_ref, r, d, jnp.concatenate(dqs, axis=1))
                dk = jnp.concatenate(dks, axis=1)
                dv = jnp.concatenate(dvs, axis=1)
                _set_rows(dk_ref, r, d, carry_k[r] + dk[:ATT_BLK])
                _set_rows(dv_ref, r, d, carry_v[r] + dv[:ATT_BLK])
                carry_k[r] = dk[ATT_BLK:]
                carry_v[r] = dv[ATT_BLK:]

            _over_residues(d, one, unroll=2)

        @pl.when(n == nb)
        def _():
            def last(r):
                _set_rows(dk_ref, r, d, carry_k[r])
                _set_rows(dv_ref, r, d, carry_v[r])

            _over_residues(d, last)

    nq = lambda n: jnp.minimum(n, nb - 1)
    cur = pl.BlockSpec((ATT_BLK * d, W), lambda j, n: (nq(n), j))
    prev = pl.BlockSpec((ATT_BLK * d, W), lambda j, n: (jnp.maximum(nq(n) - 1, 0), j))
    done = pl.BlockSpec((ATT_BLK * d, W), lambda j, n: (jnp.maximum(n - 1, 0), j))
    bspec = pl.BlockSpec((HB, 2 * ATT_BLK, ATT_BLK), lambda j, n: (j, 0, 0))
    rows = pl.BlockSpec((None, d, H, 1, LANES), lambda j, n: (nq(n), 0, 0, 0, 0))
    return pl.pallas_call(
        body, name=name, grid=(D // W, nb + 1),
        in_specs=[cur, prev, cur, prev, cur, bspec, cur, rows, rows],
        out_specs=[cur, done, done, bspec],
        out_shape=[_sds((S, D)), _sds((S, D)), _sds((S, D)), _sds((H, 2 * ATT_BLK, ATT_BLK))],
        scratch_shapes=[pltpu.VMEM((d, ATT_BLK, W), f32), pltpu.VMEM((d, ATT_BLK, W), f32)],
        compiler_params=_params(("arbitrary", "arbitrary")),
    )(q, k, k, v, v, bias_t, datt, lse_rows, dsum_rows)


def _attn_combine(os_, lses, name):
    S, D = os_[0].shape
    tr = _tile(S, 256, 16)
    head_cols = jnp.asarray((np.arange(LANES)[:, None] == np.arange(D)[None, :] // HEAD_DIM).astype(np.float32))

    def body(o0, o1, o2, l0, l1, l2, hc_ref, att_ref, attb_ref, lse_ref):
        a, b, c = l0[...], l1[...], l2[...]
        m = jnp.maximum(jnp.maximum(a, b), c)
        e0, e1, e2 = jnp.exp(a - m), jnp.exp(b - m), jnp.exp(c - m)
        tot = e0 + e1 + e2
        wide = lambda w: _dot_exact(w / tot, hc_ref[...])
        att = wide(e0) * o0[...] + wide(e1) * o1[...] + wide(e2) * o2[...]
        att_ref[...] = att
        attb_ref[...] = att.astype(bf16)
        lse_ref[...] = m + jnp.log(tot)

    wide_spec = pl.BlockSpec((tr, D), lambda i: (i, 0))
    lane_spec = pl.BlockSpec((tr, LANES), lambda i: (i, 0))
    return pl.pallas_call(
        body, name=name, grid=(S // tr,),
        in_specs=[wide_spec] * 3 + [lane_spec] * 3 + [pl.BlockSpec((LANES, D), lambda i: (0, 0))],
        out_specs=[wide_spec, wide_spec, lane_spec], out_shape=[_sds((S, D)), _sds((S, D), bf16), _sds((S, LANES))],
        compiler_params=_params(("parallel",)),
    )(*os_, *lses, head_cols)


ANY = pl.BlockSpec(memory_space=pl.ANY)


def _all_gather(vs, name):
    n = len(vs)

    def body(*refs):
        x_refs, out_refs = refs[:n], refs[n:2 * n]
        send_sems, recv_sems, local_sems = refs[2 * n:]
        x, y, c = lax.axis_index("x"), lax.axis_index("y"), lax.axis_index("c")
        me, sibling = (x, y, c), (x, y, 1 - c)
        chips = [(1 - x, y), (x, 1 - y), (1 - x, 1 - y)]

        def slot(i, px, py, pc):
            return out_refs[i].at[4 * px + 2 * py + pc]

        def copy(i, k, block, to, src=None):
            return pltpu.make_async_remote_copy(
                src_ref=slot(i, *block) if src is None else src, dst_ref=slot(i, *block),
                send_sem=send_sems.at[i, k], recv_sem=recv_sems.at[i, k], device_id=to, device_id_type=MESH)

        mine = [pltpu.make_async_copy(x_refs[i], slot(i, *me), local_sems.at[i]) for i in range(n)]
        for cp in mine:
            cp.start()
        first = []
        for i in range(n):
            first.append(copy(i, 0, me, sibling, src=x_refs[i]))
            first += [copy(i, 1 + j, me, (*chip, c), src=x_refs[i]) for j, chip in enumerate(chips)]
        for cp in first:
            cp.start()
        passed = []
        for i in range(n):
            for j, chip in enumerate(chips):
                copy(i, 1 + j, (*chip, c), me).wait_recv()
                cp = copy(i, 4 + j, (*chip, c), sibling)
                cp.start()
                passed.append(cp)
        for i in range(n):
            copy(i, 0, sibling, me).wait_recv()
            for j, chip in enumerate(chips):
                copy(i, 4 + j, (*chip, 1 - c), me).wait_recv()
        for cp in first + passed:
            cp.wait_send()
        for cp in mine:
            cp.wait()

    return pl.pallas_call(
        body, name=name, out_shape=[_sds((N_DEV,) + v.shape, v.dtype) for v in vs], in_specs=[ANY] * n,
        out_specs=[ANY] * n,
        scratch_shapes=[pltpu.SemaphoreType.DMA((n, 7)), pltpu.SemaphoreType.DMA((n, 7)), pltpu.SemaphoreType.DMA((n,))],
    )(*vs)


def _sum_slots(t, name):
    n, R, C = t.shape
    tr = _tile(R, PACK_ROW_TILE, 16)

    def body(t_ref, o_ref):
        acc = t_ref[0].astype(f32)
        for k in range(1, n):
            acc = acc + t_ref[k].astype(f32)
        o_ref[...] = acc

    return pl.pallas_call(
        body, name=name, grid=(R // tr,),
        in_specs=[pl.BlockSpec((n, tr, C), lambda i: (0, i, 0))],
        out_specs=pl.BlockSpec((tr, C), lambda i: (i, 0)), out_shape=_sds((R, C)),
        compiler_params=_params(("parallel",)),
    )(t)


HBM_SPEC = pl.BlockSpec(memory_space=pltpu.HBM)
SEM_SPEC = pl.BlockSpec(memory_space=pltpu.SEMAPHORE)
EFFECT = pltpu.SideEffectType.DATAFLOW_SIDE_EFFECTING


def _mesh_pos(p):
    return (p // 4, (p // 2) % 2, p % 2)


def _exchange_copy(src_refs, land_refs, send_sems, recv_sems, whole, dests, i, k):
    me = 4 * lax.axis_index("x") + 2 * lax.axis_index("y") + lax.axis_index("c")
    to = (me + k) % N_DEV
    frm = (me + N_DEV - k) % N_DEV
    lo, hi = dests
    src = src_refs[i] if whole else src_refs[i].at[jnp.minimum(jnp.maximum(to - lo, 0), hi - lo - 1)]
    s = i * (N_DEV - 1) + k - 1
    send = pltpu.make_async_remote_copy(src_ref=src, dst_ref=land_refs[i].at[me], send_sem=send_sems.at[s],
                                        recv_sem=recv_sems.at[s], device_id=_mesh_pos(to), device_id_type=MESH)
    recv = pltpu.make_async_remote_copy(src_ref=src, dst_ref=land_refs[i].at[frm], send_sem=send_sems.at[s],
                                        recv_sem=recv_sems.at[s], device_id=_mesh_pos(to), device_id_type=MESH)
    return send, recv, (to >= lo) & (to < hi), (me >= lo) & (me < hi)


def _exchange_start(srcs, whole, name, after=None, dests=(0, N_DEV)):
    n = len(srcs)
    lands = [lax.empty((N_DEV,) + s.shape[-2:], s.dtype) for s in srcs]
    after = list(after or [])
    n_in = 2 * n + len(after)
    everyone = dests == (0, N_DEV)

    def body(*refs):
        src_refs, land_refs = refs[:n], refs[n:2 * n]
        send_sems, recv_sems, token = refs[n_in], refs[n_in + 1], refs[-1]
        for i in range(n):
            for k in range(1, N_DEV):
                send, _, sends, _ = _exchange_copy(src_refs, land_refs, send_sems, recv_sems, whole, dests, i, k)
                if everyone:
                    send.start()
                else:
                    pl.when(sends)(send.start)
        token[...] = jnp.zeros_like(token)

    sems = pltpu.SemaphoreType.DMA((n * (N_DEV - 1),))
    outs = pl.pallas_call(
        body, name=name,
        out_shape=(sems, sems, *[pltpu.HBM(a.shape, a.dtype) for a in srcs + lands], _sds((8, LANES))),
        in_specs=[HBM_SPEC] * (2 * n) + [pl.BlockSpec(memory_space=pl.ANY)] * len(after),
        out_specs=(SEM_SPEC, SEM_SPEC, *[HBM_SPEC] * (2 * n), pl.BlockSpec(memory_space=pltpu.VMEM)),
        input_output_aliases={i: 2 + i for i in range(2 * n)},
        compiler_params=pltpu.CompilerParams(has_side_effects=EFFECT),
    )(*[pltpu.with_memory_space_constraint(a, pltpu.HBM) for a in srcs + lands], *after)
    return (outs[0], outs[1], list(outs[2:2 + n]), list(outs[2 + n:2 + 2 * n]), whole, dests), outs[-1]


def _exchange_wait(handle, after, name):
    send_sems, recv_sems, srcs, lands, whole, dests = handle
    n = len(srcs)
    everyone = dests == (0, N_DEV)

    def body(*refs):
        src_refs, land_refs = refs[:n], refs[n:2 * n]
        send_sems, recv_sems = refs[2 * n], refs[2 * n + 1]
        for i in range(n):
            for k in range(1, N_DEV):
                send, recv, sends, receives = _exchange_copy(src_refs, land_refs, send_sems, recv_sems, whole, dests, i, k)
                if everyone:
                    send.wait_send()
                    recv.wait_recv()
                else:
                    pl.when(sends)(send.wait_send)
                    pl.when(receives)(recv.wait_recv)

    outs = pl.pallas_call(
        body, name=name, out_shape=tuple(pltpu.HBM(a.shape, a.dtype) for a in srcs + lands),
        in_specs=[HBM_SPEC] * (2 * n) + [SEM_SPEC, SEM_SPEC, pl.BlockSpec(memory_space=pl.ANY)],
        out_specs=[HBM_SPEC] * (2 * n), input_output_aliases={i: i for i in range(2 * n)},
        compiler_params=pltpu.CompilerParams(has_side_effects=EFFECT),
    )(*srcs, *lands, send_sems, recv_sems, after)
    return list(outs[n:])


def _tie(v, token):
    return v + token[0:1, 0:1].astype(v.dtype).reshape((1,) * v.ndim)


def _with_own(land, own, me):
    return lax.dynamic_update_slice_in_dim(land, own[None].astype(land.dtype), me, 0)


class _Overlap:
    def __init__(self, shards, me, after):
        self.me = me
        self.names = list(shards)
        self.handle, self.token = _exchange_start([shards[nm] for nm in self.names], True, "weights_start", after)
        self.sent = {}

    def weights(self, after):
        lands = _exchange_wait(self.handle, after, "weights_wait")
        own = self.handle[2]
        return {nm: _full_from_blocks(nm, _with_own(land, o, self.me)) for nm, land, o in zip(self.names, lands, own)}

    def send(self, tag, grads):
        names = list(grads)
        handle, token = _exchange_start([_blocks_from_full(nm, grads[nm]) for nm in names], False, f"grads_start_{tag}")
        self.sent[tag] = (names, handle)
        return token

    def send_rows(self, tag, rows, dests):
        lo, hi = dests
        blocks = rows.reshape(hi - lo, rows.shape[0] // (hi - lo), rows.shape[1])
        handle, token = _exchange_start([blocks], False, f"grads_start_{tag}", None, dests)
        self.sent[tag] = ([tag], handle)
        return token

    def received(self, tag, after):
        names, handle = self.sent[tag]
        lands = _exchange_wait(handle, after, f"grads_wait_{tag}")
        lo = handle[5][0]
        own = [lax.dynamic_index_in_dim(b, self.me - lo, 0, keepdims=False) for b in handle[2]]
        return {nm: _with_own(land, o, self.me) for nm, land, o in zip(names, lands, own)}


ADAM_ROWS = 32


def _adamw(w, g, m, v, name):
    deep = w.ndim == 3
    R, C = w.shape[0], w.shape[-1]
    cb = 2 * LANES if (C % (2 * LANES) == 0 and not deep) else (LANES if C % LANES == 0 else C)
    n_parts = g.shape[0] if g.ndim == 3 else 0

    def body(w_ref, g_ref, m_ref, v_ref, d_ref, m2_ref, v2_ref, *g_out):
        at = (lambda ref, sl: ref.at[sl, 0, :]) if deep else (lambda ref, sl: ref.at[sl, :])

        def update(sl):
            if n_parts:
                gv = g_ref[0, sl, :].astype(f32)
                for k in range(1, n_parts):
                    gv = gv + g_ref[k, sl, :].astype(f32)
                at(g_out[0], sl)[...] = gv
            else:
                gv = g_ref[sl, :]
            m2 = ADAM_B1 * at(m_ref, sl)[...] + (1.0 - ADAM_B1) * gv
            v2 = ADAM_B2 * at(v_ref, sl)[...] + (1.0 - ADAM_B2) * jnp.square(gv)
            m_hat = m2 / (1.0 - ADAM_B1 ** ADAM_STEP)
            v_hat = v2 / (1.0 - ADAM_B2 ** ADAM_STEP)
            at(d_ref, sl)[...] = -ADAM_LR * (m_hat / (jnp.sqrt(v_hat) + ADAM_EPS) + ADAM_WD * at(w_ref, sl)[...])
            at(m2_ref, sl)[...] = m2
            at(v2_ref, sl)[...] = v2

        main = R // ADAM_ROWS
        if main:
            lax.fori_loop(0, main, lambda i, c: (update(pl.ds(pl.multiple_of(i * ADAM_ROWS, ADAM_ROWS), ADAM_ROWS)), c)[1], 0)
        if R % ADAM_ROWS:
            update(pl.ds(main * ADAM_ROWS, R % ADAM_ROWS))

    spec = pl.BlockSpec((R, 1, cb), lambda j: (0, 0, j)) if deep else pl.BlockSpec((R, cb), lambda j: (0, j))
    g_spec = pl.BlockSpec((n_parts, R, cb), lambda j: (0, 0, j)) if n_parts else pl.BlockSpec((R, cb), lambda j: (0, j))
    n_out = 4 if n_parts else 3
    return pl.pallas_call(
        body, name=name, grid=(C // cb,), in_specs=[spec, g_spec, spec, spec], out_specs=[spec] * n_out,
        out_shape=[_sds(w.shape)] * n_out, compiler_params=_params(("parallel",)),
    )(w, g, m, v)


BIG_PARAMS = ("hy_w_in", "hy_w_out", "cv_w_pw1", "cv_w_pw2", "ffn_w_gate", "ffn_w_up", "ffn_w_down")


def _shards_2d(w):
    t = lambda a: jnp.transpose(a)
    return dict(in_t=t(w["hy_w_in"][0]), out=w["hy_w_out"][0], pw1=w["cv_w_pw1"][0], pw2=w["cv_w_pw2"][0],
                gate_t0=t(w["ffn_w_gate"][0]), gate_t1=t(w["ffn_w_gate"][1]), up_t0=t(w["ffn_w_up"][0]),
                up_t1=t(w["ffn_w_up"][1]), down0=w["ffn_w_down"][0], down1=w["ffn_w_down"][1])


def _unshard_2d(s):
    t = lambda a: jnp.transpose(a)
    out = dict(hy_w_out=s["out"][None], cv_w_pw1=s["pw1"][None], cv_w_pw2=s["pw2"][None],
               ffn_w_gate=jnp.stack([t(s["gate_t0"]), t(s["gate_t1"])]),
               ffn_w_up=jnp.stack([t(s["up_t0"]), t(s["up_t1"])]), ffn_w_down=jnp.stack([s["down0"], s["down1"]]))
    if "in_t" in s:
        out["hy_w_in"] = t(s["in_t"])[None]
    return out


def _full_from_blocks(nm, g):
    if nm == "pw1":
        return jnp.transpose(g, (1, 0, 2)).reshape(g.shape[1], N_DEV * g.shape[2])
    return g.reshape(N_DEV * g.shape[1], g.shape[2])


def _blocks_from_full(nm, g):
    if nm == "pw1":
        return jnp.transpose(g.reshape(g.shape[0], N_DEV, g.shape[1] // N_DEV), (1, 0, 2))
    return g.reshape(N_DEV, g.shape[0] // N_DEV, g.shape[1])


class _VecPack:
    def __init__(self, shapes):
        self.shapes = [tuple(s) for s in shapes]
        self.sizes = [int(np.prod(s)) for s in self.shapes]
        total = sum(self.sizes)
        self.rows = -(-(-(-total // LANES)) // 8) * 8
        self.total = total

    def pack(self, arrays):
        flat = jnp.concatenate([a.astype(f32).reshape(-1) for a in arrays])
        flat = jnp.pad(flat, (0, self.rows * LANES - self.total))
        return flat.reshape(self.rows, LANES)

    def unpack(self, packed):
        flat = packed.reshape(-1)
        out, off = [], 0
        for shp, n in zip(self.shapes, self.sizes):
            out.append(flat[off:off + n].reshape(shp))
            off += n
        return out

    def unpack_stacked(self, stacked, only=None):
        flat = stacked.reshape(stacked.shape[0], -1)
        offs = np.concatenate([[0], np.cumsum(self.sizes)])
        get = lambda i: flat[:, offs[i]:offs[i + 1]].reshape((stacked.shape[0],) + self.shapes[i])
        return get(only) if only is not None else [get(i) for i in range(len(self.shapes))]


def _row(v):
    return v.reshape(1, -1)


def _pad_lanes(v):
    v = v.reshape(1, -1)
    return jnp.pad(v, ((0, 0), (0, LANES - v.shape[1])))


def _ffn_fwd(h, w_gate_t, w_up_t, w_down, tag):
    F = w_down.shape[0]
    a = _mm(h, w_gate_t, tb=True, out_dtype=bf16, name=f"ffn_gate_{tag}")
    u = _mm(h, w_up_t, tb=True, out_dtype=bf16, name=f"ffn_up_{tag}")
    (f,), _ = _rowwise(f"swiglu_{tag}", lambda rv, vv: ([_silu(rv[0].astype(f32)) * rv[1].astype(f32)], []), [a, u], [],
                       [(F, bf16)], [], sub=16, col_chunk=_tile(F, 512, LANES))
    out = _mm(f, w_down, name=f"ffn_down_{tag}")
    return out, (a, u, f)


def _ffn_bwd(h, w_gate_t, w_up_t, w_down, saved, dout, tag):
    a, u, f = saved
    F = w_down.shape[0]
    df = _mm(dout, w_down, tb=True, out_dtype=bf16, name=f"ffn_down_dx_{tag}")
    dw_down = _mm(f, dout, ta=True, out_dtype=bf16, name=f"ffn_down_dw_{tag}")

    def fn(rv, vv):
        av, uv, dv = rv[0].astype(f32), rv[1].astype(f32), rv[2].astype(f32)
        sig = jax.nn.sigmoid(av)
        act = av * sig
        return [dv * uv * (sig + act * (1.0 - sig)), dv * act], []

    (da, du), _ = _rowwise(f"swiglu_bwd_{tag}", fn, [a, u, df], [], [(F, bf16), (F, bf16)], [], sub=16,
                           col_chunk=_tile(F, 512, LANES))
    dh = _mm(du, w_up_t, add=_mm(da, w_gate_t, name=f"ffn_gate_dx_{tag}"), name=f"ffn_up_dx_{tag}")
    dw_gate_t = _mm(da, h, ta=True, out_dtype=bf16, name=f"ffn_gate_dw_{tag}")
    dw_up_t = _mm(du, h, ta=True, out_dtype=bf16, name=f"ffn_up_dw_{tag}")
    return dh, dw_gate_t, dw_up_t, dw_down


def _local_step(x, target, mod, w_in_t, comm, small):
    S, D = x.shape
    di = small["hy_ssm_norm_g"].shape[-1]
    nh = small["hy_dt_bias"].shape[-1]
    cd = small["hy_conv_b"].shape[-1]
    m = [[_row(mod[i, j]) for j in range(6)] for i in range(2)]

    off_q = di + cd + nh
    w_qkv_t = w_in_t[off_q:]
    seg = dict(z=(w_in_t, 0, di), xbc=(w_in_t, di, cd), dt=(w_in_t, di + cd, LANES))
    for i, nm in enumerate(("q0", "q1", "q2", "k", "v")):
        seg[nm] = (w_qkv_t, i * D, D)

    g_mix = [_row(small["norm_mix_g"][i]) for i in range(2)]
    g_ffn = [_row(small["norm_ffn_g"][i]) for i in range(2)]
    conv_w, conv_b = small["hy_conv_w_full"], _row(small["hy_conv_b"][0])
    dt_bias, a_log, d_skip = (_pad_lanes(small[k][0]) for k in ("hy_dt_bias", "hy_a_log", "hy_d_skip"))
    g_ssm = _row(small["hy_ssm_norm_g"][0])
    onehot = jnp.asarray(_bucket_onehot())
    rel_t = small["rel_table"].T
    H = D // HEAD_DIM
    bias = [_exact_mm(rel_t[gi * H:(gi + 1) * H], onehot[gi], name=f"rel_bias_{gi}")
            .reshape(H, 2 * ATT_BLK, ATT_BLK) + _band_penalty() for gi in range(3)]

    h1 = _adaln_fwd(x, g_mix[0], m[0][1], m[0][0], "adaln_mix0")
    proj = {nm: _mm(h1, mat, tb=True, b_rows=(off, cnt), name=f"in_{nm}") for nm, (mat, off, cnt) in seg.items()}
    xbc_pre, xbc = _conv_fwd(proj["xbc"], conv_w, conv_b, silu=True, name="ssm_conv", tr=1024)
    y, hin = _ssd_fwd(xbc, proj["dt"], dt_bias, a_log, d_skip, di, "ssd_fwd")
    (yg,), _ = _rowwise("ssm_gate", lambda rv, vv: ([_gate_f(rv[0], rv[1], vv[0])], []),
                        [y, proj["z"]], [g_ssm], [(di, bf16)], [], sub=16)
    og = [_attn_fwd(proj[f"q{gi}"], proj["k"], proj["v"], bias[gi], d, f"attn_fwd_{gi}")
          for gi, d in enumerate(ATT_DILATIONS)]
    att, att_b, lse_tot = _attn_combine([a for a, _ in og], [_from_blocks(b, LANES) for _, b in og], "attn_combine")
    W = comm.weights(after=att_b)
    w_out_y, w_out_a = W["out"][:di], W["out"][di:]
    mix0 = _mm(att_b, w_out_a, add=_mm(yg, w_out_y, name="out_y"), name="out_a")
    x1, h2 = _resid_adaln_fwd(x, m[0][2], mix0, g_ffn[0], m[0][4], m[0][3], "resid_mix0_adaln_ffn0")
    f0, ffn0_saved = _ffn_fwd(h2, W["gate_t0"], W["up_t0"], W["down0"], "0")
    x2, h3 = _resid_adaln_fwd(x1, m[0][5], f0, g_mix[1], m[1][1], m[1][0], "resid_ffn0_adaln_mix1")
    pw1 = _mm(h3, W["pw1"], bias=_row(small["cv_b_pw1_full"]), name="cv_pw1")
    (u,), _ = _rowwise("cv_glu", lambda rv, vv: ([rv[0] * jax.nn.sigmoid(rv[1])], []),
                       [(pw1, 0, D), (pw1, 1, D)], [], [(D, f32)], [])
    (u2,) = _conv_fwd(u, small["cv_w_dw_full"], _row(small["cv_b_dw_full"]), silu=False, name="cv_dw")
    ln_g, ln_b = _row(small["cv_ln_g_full"]), _row(small["cv_ln_b_full"])
    (u3,), _ = _rowwise("cv_lnsilu", lambda rv, vv: ([_lnsilu_f(rv[0], vv[0], vv[1])], []),
                        [u2], [ln_g, ln_b], [(D, bf16)], [], sub=16)
    mix1 = _mm(u3, W["pw2"], bias=_row(small["cv_b_pw2_full"]), name="cv_pw2")
    x3, h4 = _resid_adaln_fwd(x2, m[1][2], mix1, g_ffn[1], m[1][4], m[1][3], "resid_mix1_adaln_ffn1")
    f1, ffn1_saved = _ffn_fwd(h4, W["gate_t1"], W["up_t1"], W["down1"], "1")

    g_fin = _row(small["final_norm_g"])
    dmod = [[None] * 6 for _ in range(2)]
    d_norm_mix, d_norm_ffn = [None, None], [None, None]
    big = {}

    def final_fn(rv, vv):
        xv, fv, tv = rv
        gate = vv[1]
        yv, vjp = jax.vjp(_rms, xv + gate * fv, vv[0])
        err = yv - tv
        dx, dg = vjp(err / D)
        part = 0.5 * jnp.sum(jnp.mean(err * err, -1, keepdims=True), 0, keepdims=True)
        return [dx, gate * dx], [dg, jnp.broadcast_to(part, (1, LANES)), jnp.sum(dx * fv, 0, keepdims=True)]

    (dx4, df1), (d_fin, loss, dmod[1][5]) = _rowwise("loss_head", final_fn, [x3, f1, target], [g_fin, m[1][5]],
                                                      [(D, f32), (D, bf16)], [D, LANES, D], sub=16)

    dh4, big["gate_t1"], big["up_t1"], big["down1"] = _ffn_bwd(h4, W["gate_t1"], W["up_t1"], W["down1"], ffn1_saved, df1, "1")
    dx3, dmix1, (d_norm_ffn[1], dmod[1][4], dmod[1][3], dmod[1][2], d_b_pw2) = _adaln_resid_bwd(
        x3, g_ffn[1], m[1][4], m[1][3], dh4, dx4, mix1, m[1][2], "adaln_ffn1_resid_mix1_bwd")
    du3 = _mm(dmix1, W["pw2"], tb=True, name="cv_pw2_dx")
    big["pw2"] = _mm(u3, dmix1, ta=True, out_dtype=bf16, name="cv_pw2_dw")

    def lnsilu_bwd(rv, vv):
        _, vjp = jax.vjp(_lnsilu_f, rv[0], vv[0], vv[1])
        du, dg, db = vjp(rv[1])
        return [du], [dg, db]

    (du2,), (d_ln_g, d_ln_b) = _rowwise("cv_lnsilu_bwd", lnsilu_bwd, [u2, du3], [ln_g, ln_b], [(D, f32)], [D, D])
    du, d_w_dw, d_b_dw = _conv_bwd(u, small["cv_w_dw_full"], du2, None, silu=False, name="cv_dw_bwd")

    def glu_bwd(rv, vv):
        a, gt, d = rv
        _, vjp = jax.vjp(lambda a_, g_: a_ * jax.nn.sigmoid(g_), a, gt)
        da, dg = vjp(d)
        return [da, dg], [jnp.sum(da, 0, keepdims=True), jnp.sum(dg, 0, keepdims=True)]

    (dpa, dpg), (d_b1a, d_b1g) = _rowwise("cv_glu_bwd", glu_bwd, [(pw1, 0, D), (pw1, 1, D), du], [],
                                           [(D, bf16), (D, bf16)], [D, D], sub=16)
    dpw1 = jnp.concatenate([dpa, dpg], axis=1)
    d_b_pw1 = jnp.concatenate([d_b1a, d_b1g], axis=1)
    dh3 = _mm(dpw1, W["pw1"], tb=True, name="cv_pw1_dx")
    big["pw1"] = _mm(h3, dpw1, ta=True, out_dtype=bf16, name="cv_pw1_dw")
    token = comm.send("layer1", {nm: big[nm] for nm in ("gate_t1", "up_t1", "down1", "pw2", "pw1")})
    dx2, df0, (d_norm_mix[1], dmod[1][1], dmod[1][0], dmod[0][5], _) = _adaln_resid_bwd(
        x2, g_mix[1], m[1][1], _tie(m[1][0], token), dh3, dx3, f0, m[0][5], "adaln_mix1_resid_ffn0_bwd")

    dh2, big["gate_t0"], big["up_t0"], big["down0"] = _ffn_bwd(h2, W["gate_t0"], W["up_t0"], W["down0"], ffn0_saved, df0, "0")
    dx1, dmix0, (d_norm_ffn[0], dmod[0][4], dmod[0][3], dmod[0][2], _) = _adaln_resid_bwd(
        x1, g_ffn[0], m[0][4], m[0][3], dh2, dx2, mix0, m[0][2], "adaln_ffn0_resid_mix0_bwd")
    dyg = _mm(dmix0, w_out_y, tb=True, name="out_y_dx")
    datt = _mm(dmix0, w_out_a, tb=True, name="out_a_dx")
    big["out"] = jnp.concatenate([_mm(yg, dmix0, ta=True, out_dtype=bf16, name="out_y_dw"),
                                  _mm(att_b, dmix0, ta=True, out_dtype=bf16, name="out_a_dw")], axis=0)
    token = comm.send("layer0", {nm: big[nm] for nm in ("gate_t0", "up_t0", "down0", "out")})
    g_ssm = _tie(g_ssm, token)

    def gate_bwd(rv, vv):
        _, vjp = jax.vjp(_gate_f, rv[0], rv[1], vv[0])
        dy_, dz_, dg_ = vjp(rv[2])
        return [dy_, dz_], [dg_]

    (dy, dz), (d_g_ssm,) = _rowwise("ssm_gate_bwd", gate_bwd, [y, proj["z"], dyg], [g_ssm], [(di, f32), (di, bf16)], [di],
                                    sub=16)
    dxbc, ddtraw, d_a_log, d_dskip, d_dt_bias = _ssd_bwd(xbc, proj["dt"], dt_bias, a_log, d_skip, hin, y, dy, di, "ssd_bwd")
    dxbc_pre, d_conv_w, d_conv_b = _conv_bwd(proj["xbc"], conv_w, dxbc, xbc_pre, silu=True, name="ssm_conv_bwd",
                                             dx_dtype=bf16, tr=1024)
    dh1 = None
    early = []
    for nm, dseg in (("z", dz), ("xbc", dxbc_pre), ("dt", ddtraw)):
        mat, off, cnt = seg[nm]
        dh1 = _mm(dseg, mat, b_rows=(off, cnt), add=dh1, name=f"in_{nm}_dx")
        dwp = _mm(dseg, h1, ta=True, out_dtype=bf16, name=f"in_{nm}_dw")
        early.append(dwp[:nh] if nm == "dt" else dwp)
    early = jnp.concatenate(early, axis=0)
    shard_rows = w_in_t.shape[0] // N_DEV
    n_early = off_q // shard_rows
    token = comm.send_rows("in_early", early[:n_early * shard_rows], (0, n_early))
    bias = [_tie(b, token) for b in bias]

    dq, dks, dvs, dbs = [], [], [], []
    lse_heads = lse_tot[:, :H]
    dsum_heads = _head_sums(att, datt, "attn_dsum")[:, :H]
    for gi, d in enumerate(ATT_DILATIONS):
        a, b, c_, e = _attn_bwd(proj[f"q{gi}"], proj["k"], proj["v"], bias[gi], datt,
                                _by_block(lse_heads, d), _by_block(dsum_heads, d), d, f"attn_bwd_{gi}")
        dq.append(a)
        dks.append(b)
        dvs.append(c_)
        dbs.append(e)
    dk = _add3(*dks, "attn_dk")
    dv = _add3(*dvs, "attn_dv")
    d_rel = jnp.concatenate(
        [_exact_mm(dbs[gi].reshape(H, -1), onehot[gi], tb=True, name=f"rel_grad_{gi}") for gi in range(3)], axis=0).T

    dsegs = (("q0", dq[0]), ("q1", dq[1]), ("q2", dq[2]), ("k", dk), ("v", dv))
    late = jnp.concatenate([early[n_early * shard_rows:]] +
                           [_mm(dseg, h1, ta=True, out_dtype=bf16, name=f"in_{nm}_dw") for nm, dseg in dsegs], axis=0)
    token = comm.send_rows("in_late", late, (n_early, N_DEV))
    w_qkv_after = _tie(w_qkv_t, token)
    for nm, dseg in dsegs:
        _, off, cnt = seg[nm]
        dh1 = _mm(dseg, w_qkv_after, b_rows=(off, cnt), add=dh1, name=f"in_{nm}_dx")
    dx0, (d_norm_mix[0], dmod[0][1], dmod[0][0]) = _adaln_bwd(x, g_mix[0], m[0][1], m[0][0], dh1, dx1, "adaln_mix0_bwd")

    smallg = dict(
        loss=loss, dmod=jnp.stack([jnp.concatenate(dmod[i], axis=1)[0] for i in range(2)]),
        norm_mix_g=jnp.concatenate(d_norm_mix, axis=0), norm_ffn_g=jnp.concatenate(d_norm_ffn, axis=0),
        hy_conv_w=d_conv_w, hy_conv_b=d_conv_b, hy_dt_bias=d_dt_bias[:, :nh], hy_a_log=d_a_log[:, :nh],
        hy_d_skip=d_dskip[:, :nh], hy_ssm_norm_g=d_g_ssm, rel_table=d_rel,
        cv_b_pw1=d_b_pw1, cv_w_dw=d_w_dw, cv_b_dw=d_b_dw, cv_ln_g=d_ln_g, cv_ln_b=d_ln_b, cv_b_pw2=d_b_pw2,
        final_norm_g=d_fin)
    return dx0, n_early, smallg


SMALL_GRAD_ORDER = ("loss", "dmod", "norm_mix_g", "norm_ffn_g", "hy_conv_w", "hy_conv_b", "hy_dt_bias", "hy_a_log",
                    "hy_d_skip", "hy_ssm_norm_g", "rel_table", "cv_b_pw1", "cv_w_dw", "cv_b_dw", "cv_ln_g", "cv_ln_b",
                    "cv_b_pw2", "final_norm_g")


def kernel(x, c, ada_w, ada_b, norm_mix_g, norm_ffn_g, hy_w_in, hy_conv_w, hy_conv_b, hy_dt_bias, hy_a_log, hy_d_skip, hy_ssm_norm_g, hy_w_out, rel_table, cv_w_pw1, cv_b_pw1, cv_w_dw, cv_b_dw, cv_ln_g, cv_ln_b, cv_w_pw2, cv_b_pw2, ffn_w_gate, ffn_w_up, ffn_w_down, final_norm_g, loss_target, m_ada_w, m_ada_b, m_norm_mix_g, m_norm_ffn_g, m_hy_w_in, m_hy_conv_w, m_hy_conv_b, m_hy_dt_bias, m_hy_a_log, m_hy_d_skip, m_hy_ssm_norm_g, m_hy_w_out, m_rel_table, m_cv_w_pw1, m_cv_b_pw1, m_cv_w_dw, m_cv_b_dw, m_cv_ln_g, m_cv_ln_b, m_cv_w_pw2, m_cv_b_pw2, m_ffn_w_gate, m_ffn_w_up, m_ffn_w_down, m_final_norm_g, v_ada_w, v_ada_b, v_norm_mix_g, v_norm_ffn_g, v_hy_w_in, v_hy_conv_w, v_hy_conv_b, v_hy_dt_bias, v_hy_a_log, v_hy_d_skip, v_hy_ssm_norm_g, v_hy_w_out, v_rel_table, v_cv_w_pw1, v_cv_b_pw1, v_cv_w_dw, v_cv_b_dw, v_cv_ln_g, v_cv_ln_b, v_cv_w_pw2, v_cv_b_pw2, v_ffn_w_gate, v_ffn_w_up, v_ffn_w_down, v_final_norm_g):
    names = ("ada_w", "ada_b", "norm_mix_g", "norm_ffn_g", "hy_w_in", "hy_conv_w", "hy_conv_b", "hy_dt_bias", "hy_a_log",
             "hy_d_skip", "hy_ssm_norm_g", "hy_w_out", "rel_table", "cv_w_pw1", "cv_b_pw1", "cv_w_dw", "cv_b_dw", "cv_ln_g",
             "cv_ln_b", "cv_w_pw2", "cv_b_pw2", "ffn_w_gate", "ffn_w_up", "ffn_w_down", "final_norm_g")
    w = dict(zip(names, (ada_w, ada_b, norm_mix_g, norm_ffn_g, hy_w_in, hy_conv_w, hy_conv_b, hy_dt_bias, hy_a_log, hy_d_skip,
                         hy_ssm_norm_g, hy_w_out, rel_table, cv_w_pw1, cv_b_pw1, cv_w_dw, cv_b_dw, cv_ln_g, cv_ln_b, cv_w_pw2,
                         cv_b_pw2, ffn_w_gate, ffn_w_up, ffn_w_down, final_norm_g)))
    mom = dict(zip(names, (m_ada_w, m_ada_b, m_norm_mix_g, m_norm_ffn_g, m_hy_w_in, m_hy_conv_w, m_hy_conv_b, m_hy_dt_bias,
                           m_hy_a_log, m_hy_d_skip, m_hy_ssm_norm_g, m_hy_w_out, m_rel_table, m_cv_w_pw1, m_cv_b_pw1, m_cv_w_dw,
                           m_cv_b_dw, m_cv_ln_g, m_cv_ln_b, m_cv_w_pw2, m_cv_b_pw2, m_ffn_w_gate, m_ffn_w_up, m_ffn_w_down,
                           m_final_norm_g)))
    vel = dict(zip(names, (v_ada_w, v_ada_b, v_norm_mix_g, v_norm_ffn_g, v_hy_w_in, v_hy_conv_w, v_hy_conv_b, v_hy_dt_bias,
                           v_hy_a_log, v_hy_d_skip, v_hy_ssm_norm_g, v_hy_w_out, v_rel_table, v_cv_w_pw1, v_cv_b_pw1, v_cv_w_dw,
                           v_cv_b_dw, v_cv_ln_g, v_cv_ln_b, v_cv_w_pw2, v_cv_b_pw2, v_ffn_w_gate, v_ffn_w_up, v_ffn_w_down,
                           v_final_norm_g)))
    S, D = x.shape[1], x.shape[2]
    ax, ay, ac = lax.axis_index("x"), lax.axis_index("y"), lax.axis_index("c")
    me = 4 * ax + 2 * ay + ac
    nmod = ada_w.shape[2]

    w2 = _shards_2d(w)
    big_names = list(w2)
    sharded_small = ("hy_conv_w", "cv_b_pw1", "cv_w_dw", "cv_b_dw", "cv_ln_g", "cv_ln_b", "cv_b_pw2")
    vp = _VecPack([c.shape] + [w[nm].shape for nm in sharded_small])
    g_in, sg = _all_gather([w2["in_t"].astype(bf16), vp.pack([c] + [w[nm] for nm in sharded_small])], "gather_w_in")
    w_in_t = _full_from_blocks("in_t", g_in)
    parts = vp.unpack_stacked(sg)
    c_all = parts[0][:, 0]
    small = {k: w[k] for k in ("norm_mix_g", "norm_ffn_g", "hy_conv_b", "hy_dt_bias", "hy_a_log", "hy_d_skip",
                               "hy_ssm_norm_g", "rel_table", "final_norm_g")}
    for p, nm in zip(parts[1:], sharded_small):
        p = p[:, 0]
        p = jnp.moveaxis(p, 0, -2)
        small[nm + "_full"] = p.reshape(p.shape[:-2] + (N_DEV * p.shape[-1],))

    (cs_all,), _ = _rowwise("ada_silu", lambda rv, vv: ([_silu(rv[0])], []), [c_all], [], [(D, f32)], [])
    b_mine = lax.dynamic_slice_in_dim(ada_b, me * nmod, nmod, axis=1)
    mod_part = jnp.stack([_mm(cs_all, ada_w[i], bias=b_mine[i:i + 1], name=f"ada_mod_{i}") for i in range(2)])
    (mod_all,) = _all_gather([mod_part.reshape(2 * N_DEV, nmod)], "gather_mod")
    mod_all = mod_all.reshape(N_DEV, 2, N_DEV, nmod)
    mod_mine = lax.dynamic_index_in_dim(mod_all, me, axis=2, keepdims=False)
    mod = jnp.transpose(mod_mine, (1, 0, 2)).reshape(2, 6, D)
    comm = _Overlap({nm: w2[nm].astype(bf16) for nm in big_names if nm != "in_t"}, me, after=[mod, w_in_t])
    mod = _tie(mod, comm.token)

    dx0, n_early, sgrad = _local_step(x[0], loss_target[0], mod, w_in_t, comm, small)

    gp = _VecPack([sgrad[k].shape for k in SMALL_GRAD_ORDER])
    small_handle, after = _exchange_start([gp.pack([sgrad[k] for k in SMALL_GRAD_ORDER])], True, "small_grads_start")
    m2, v2 = _shards_2d(mom), _shards_2d(vel)
    g2, d2, nm2, nv2 = {}, {}, {}, {}
    slots = {}
    for tag in ("layer1", "layer0"):
        slots.update(comm.received(tag, after))
        for nm in comm.sent[tag][0]:
            d2[nm], nm2[nm], nv2[nm], g2[nm] = _adamw(w2[nm], slots[nm], m2[nm], v2[nm], f"adamw_{nm}")
            after = g2[nm]
    (g_all,) = _exchange_wait(small_handle, after, "small_grads_wait")
    g_all = _with_own(g_all, small_handle[2][0], me)
    tot = dict(zip(SMALL_GRAD_ORDER, gp.unpack(_sum_slots(g_all, "sum_small_grads"))))
    dmod_all = gp.unpack_stacked(g_all, only=SMALL_GRAD_ORDER.index("dmod"))
    loss = tot["loss"][0, 0]

    grads = {}
    dmod_mine = lax.dynamic_slice_in_dim(dmod_all, me * nmod, nmod, axis=2)
    grads["ada_w"] = jnp.stack([_mm(cs_all, dmod_mine[:, i], ta=True, name=f"ada_w_grad_{i}") for i in range(2)])
    grads["ada_b"] = tot["dmod"]
    grads["norm_mix_g"], grads["norm_ffn_g"] = tot["norm_mix_g"], tot["norm_ffn_g"]
    grads["hy_conv_b"] = tot["hy_conv_b"]
    grads["hy_dt_bias"] = tot["hy_dt_bias"]
    grads["hy_a_log"] = tot["hy_a_log"]
    grads["hy_d_skip"] = tot["hy_d_skip"]
    grads["hy_ssm_norm_g"] = tot["hy_ssm_norm_g"]
    grads["rel_table"] = tot["rel_table"]
    grads["final_norm_g"] = tot["final_norm_g"][0]
    for nm in sharded_small:
        n = w[nm].shape[-1]
        grads[nm] = lax.dynamic_slice_in_dim(tot[nm], me * n, n, axis=1).reshape(w[nm].shape)

    delta, new_m, new_v = {}, {}, {}
    shp = ada_w.shape
    two = lambda t: t.reshape(-1, shp[-1])
    d_, m_, v_ = _adamw(two(ada_w), two(grads["ada_w"]), two(m_ada_w), two(v_ada_w), "adamw_ada_w")
    delta["ada_w"], new_m["ada_w"], new_v["ada_w"] = d_.reshape(shp), m_.reshape(shp), v_.reshape(shp)
    rest = [nm for nm in names if nm not in BIG_PARAMS and nm != "ada_w"]
    sp = _VecPack([w[nm].shape for nm in rest])
    packs = [sp.pack([t[nm] for nm in rest]) for t in (w, grads, mom, vel)]
    ds_, ms_, vs_ = _adamw(*packs, "adamw_small")
    for nm, a, b, e in zip(rest, sp.unpack(ds_), sp.unpack(ms_), sp.unpack(vs_)):
        delta[nm], new_m[nm], new_v[nm] = a, b, e

    slots.update(comm.received("in_early", ds_))
    slots.update(comm.received("in_late", slots["in_early"]))
    stored = lambda t: jnp.transpose(t, (2, 0, 1))
    in_slots = jnp.where(me < n_early, slots["in_early"], slots["in_late"])
    d_in, m_in, v_in, g_in = _adamw(stored(hy_w_in), in_slots, stored(m_hy_w_in), stored(v_hy_w_in), "adamw_in_t")
    for dst, part, t in ((grads, g2, g_in), (delta, d2, d_in), (new_m, nm2, m_in), (new_v, nv2, v_in)):
        dst.update(_unshard_2d(part))
        dst["hy_w_in"] = jnp.transpose(t, (1, 2, 0))

    return (loss, dx0[None], *[grads[n] for n in names], *[delta[n] for n in names],
            *[new_m[n] for n in names], *[new_v[n] for n in names])
```
